```python
import jax, jax.numpy as jnp
from jax import lax
import numpy as np

D_MODEL = 1024
BATCH = 8
SEQ = 8192
DEPTH = 2

CONV_CHANNELS = 512
CONV_GROUPS = 8
CONV_WIDTH = 3
N_Q_HEADS = 8
N_KV_HEADS = 2
HEAD_DIM = 64
ATTN_WIDTH = N_Q_HEADS * HEAD_DIM
WINDOW = 128
BLOCK = 128
MIX_WIDTH = CONV_CHANNELS + ATTN_WIDTH
IN_COLS = 3 * CONV_CHANNELS + (N_Q_HEADS + 2 * N_KV_HEADS) * HEAD_DIM
D_FF = -((-8 * D_MODEL) // (3 * 256)) * 256
EPS = 1e-6
NEG_INF = -1e30

kernel_name = "hymba_style_conv_swa_sink_hybrid"


def rms_norm(x, g):
    xf = x.astype(jnp.float32)
    y = xf * lax.rsqrt(jnp.mean(xf * xf, axis=-1, keepdims=True) + EPS)
    return (y * g.astype(jnp.float32)).astype(x.dtype)


def short_gated_conv(b_gate, c_gate, h, conv_w):
    seq = h.shape[1]
    u = c_gate * h
    up = jnp.pad(u, ((0, 0), (CONV_WIDTH - 1, 0), (0, 0)))
    y = conv_w[0] * up[:, 0:seq]
    for tap in range(1, CONV_WIDTH):
        y = y + conv_w[tap] * up[:, tap:tap + seq]
    return b_gate * y


def band_keys(t, nb):
    b = t.shape[0]
    tb = t.reshape(b, nb, BLOCK, t.shape[2], t.shape[3])
    prev = jnp.pad(tb[:, :-1], ((0, 0), (1, 0), (0, 0), (0, 0), (0, 0)))
    return jnp.concatenate([prev, tb], axis=2)


def sliding_window_attention_with_sinks(q, k, v, sinks):
    b, seq = q.shape[0], q.shape[1]
    nb = seq // BLOCK
    grp = N_Q_HEADS // N_KV_HEADS
    qb = q.reshape(b, nb, BLOCK, N_KV_HEADS, grp, HEAD_DIM)
    kb = band_keys(k, nb)
    vb = band_keys(v, nb)
    scale = HEAD_DIM ** -0.5
    s = jnp.einsum('bnqhgd,bnkhd->bnhgqk', qb, kb).astype(jnp.float32) * scale
    qpos = jnp.arange(nb)[:, None] * BLOCK + jnp.arange(BLOCK)[None, :]
    kpos = (jnp.arange(nb)[:, None] - 1) * BLOCK + jnp.arange(2 * BLOCK)[None, :]
    diff = qpos[:, :, None] - kpos[:, None, :]
    valid = (diff >= 0) & (diff < WINDOW) & (kpos[:, None, :] >= 0)
    s = jnp.where(valid[None, :, None, None], s, NEG_INF)
    sink = sinks.astype(jnp.float32).reshape(N_KV_HEADS, grp)[None, None, :, :, None, None]
    m = jnp.maximum(jnp.max(s, axis=-1, keepdims=True), sink)
    p = jnp.exp(s - m)
    denom = jnp.sum(p, axis=-1, keepdims=True) + jnp.exp(sink - m)
    probs = (p / denom).astype(v.dtype)
    o = jnp.einsum('bnhgqk,bnkhd->bnqhgd', probs, vb)
    return o.reshape(b, seq, ATTN_WIDTH)


def _fwd_setup_inputs(seed: int = 0) -> dict:
    key = jax.random.key(seed)
    ks = jax.random.split(key, 16)
    f32 = jnp.float32

    def gain(k, shape):
        return 1.0 + 0.02 * jax.random.normal(k, shape, f32)

    return {
        "x": jax.random.normal(ks[0], (BATCH, SEQ, D_MODEL), f32),
        "norm1_g": gain(ks[1], (DEPTH, D_MODEL)),
        "w_in": jax.random.normal(ks[2], (DEPTH, D_MODEL, IN_COLS), f32) * D_MODEL ** -0.5,
        "conv_w": jax.random.normal(ks[3], (DEPTH, CONV_WIDTH, CONV_CHANNELS), f32) * CONV_WIDTH ** -0.5,
        "q_norm_g": gain(ks[4], (DEPTH, HEAD_DIM)),
        "k_norm_g": gain(ks[5], (DEPTH, HEAD_DIM)),
        "sinks": 0.5 * jax.random.normal(ks[6], (DEPTH, N_Q_HEADS), f32),
        "conv_out_g": gain(ks[7], (DEPTH, CONV_CHANNELS)),
        "attn_out_g": gain(ks[8], (DEPTH, ATTN_WIDTH)),
        "w_o": jax.random.normal(ks[9], (DEPTH, MIX_WIDTH, D_MODEL), f32) * MIX_WIDTH ** -0.5,
        "norm2_g": gain(ks[10], (DEPTH, D_MODEL)),
        "w_gate": jax.random.normal(ks[11], (DEPTH, D_MODEL, D_FF), f32) * D_MODEL ** -0.5,
        "w_up": jax.random.normal(ks[12], (DEPTH, D_MODEL, D_FF), f32) * D_MODEL ** -0.5,
        "w_down": jax.random.normal(ks[13], (DEPTH, D_FF, D_MODEL), f32) * D_FF ** -0.5,
    }


def _fwd_reference(x, norm1_g, w_in, conv_w, q_norm_g, k_norm_g, sinks, conv_out_g,
              attn_out_g, w_o, norm2_g, w_gate, w_up, w_down):
    b, seq = x.shape[0], x.shape[1]
    c = CONV_CHANNELS
    o_q = 3 * c
    o_k = o_q + ATTN_WIDTH
    o_v = o_k + N_KV_HEADS * HEAD_DIM
    for l in range(DEPTH):
        h = rms_norm(x, norm1_g[l])
        proj = h @ w_in[l]
        b_gate = proj[..., 0:c]
        c_gate = proj[..., c:2 * c]
        hc = proj[..., 2 * c:3 * c]
        q = proj[..., o_q:o_k].reshape(b, seq, N_Q_HEADS, HEAD_DIM)
        k = proj[..., o_k:o_v].reshape(b, seq, N_KV_HEADS, HEAD_DIM)
        v = proj[..., o_v:].reshape(b, seq, N_KV_HEADS, HEAD_DIM)

        conv_out = short_gated_conv(b_gate, c_gate, hc, conv_w[l])

        q = rms_norm(q, q_norm_g[l])
        k = rms_norm(k, k_norm_g[l])
        attn_out = sliding_window_attention_with_sinks(q, k, v, sinks[l])

        mix = jnp.concatenate([rms_norm(conv_out, conv_out_g[l]),
                               rms_norm(attn_out, attn_out_g[l])], axis=-1)
        x = x + mix @ w_o[l]

        h2 = rms_norm(x, norm2_g[l])
        x = x + (jax.nn.silu(h2 @ w_gate[l]) * (h2 @ w_up[l])) @ w_down[l]
    return x


import jax as _jax
import jax.numpy as _jnp

TWIN_FORMAT = 'train_step'
FWD_PARAMS = ['x', 'norm1_g', 'w_in', 'conv_w', 'q_norm_g', 'k_norm_g', 'sinks', 'conv_out_g', 'attn_out_g', 'w_o', 'norm2_g', 'w_gate', 'w_up', 'w_down']
TWIN_WEIGHTS = ['norm1_g', 'w_in', 'conv_w', 'q_norm_g', 'k_norm_g', 'sinks', 'conv_out_g', 'attn_out_g', 'w_o', 'norm2_g', 'w_gate', 'w_up', 'w_down']
TWIN_DIFF_INPUT = 'x'
TWIN_INPUTS = ['x', 'norm1_g', 'w_in', 'conv_w', 'q_norm_g', 'k_norm_g', 'sinks', 'conv_out_g', 'attn_out_g', 'w_o', 'norm2_g', 'w_gate', 'w_up', 'w_down', 'loss_target', 'm_norm1_g', 'm_w_in', 'm_conv_w', 'm_q_norm_g', 'm_k_norm_g', 'm_sinks', 'm_conv_out_g', 'm_attn_out_g', 'm_w_o', 'm_norm2_g', 'm_w_gate', 'm_w_up', 'm_w_down', 'v_norm1_g', 'v_w_in', 'v_conv_w', 'v_q_norm_g', 'v_k_norm_g', 'v_sinks', 'v_conv_out_g', 'v_attn_out_g', 'v_w_o', 'v_norm2_g', 'v_w_gate', 'v_w_up', 'v_w_down']
TWIN_OUTPUTS = ['loss', 'grad_x', 'grad_norm1_g', 'grad_w_in', 'grad_conv_w', 'grad_q_norm_g', 'grad_k_norm_g', 'grad_sinks', 'grad_conv_out_g', 'grad_attn_out_g', 'grad_w_o', 'grad_norm2_g', 'grad_w_gate', 'grad_w_up', 'grad_w_down', 'delta_norm1_g', 'delta_w_in', 'delta_conv_w', 'delta_q_norm_g', 'delta_k_norm_g', 'delta_sinks', 'delta_conv_out_g', 'delta_attn_out_g', 'delta_w_o', 'delta_norm2_g', 'delta_w_gate', 'delta_w_up', 'delta_w_down', 'new_m_norm1_g', 'new_m_w_in', 'new_m_conv_w', 'new_m_q_norm_g', 'new_m_k_norm_g', 'new_m_sinks', 'new_m_conv_out_g', 'new_m_attn_out_g', 'new_m_w_o', 'new_m_norm2_g', 'new_m_w_gate', 'new_m_w_up', 'new_m_w_down', 'new_v_norm1_g', 'new_v_w_in', 'new_v_conv_w', 'new_v_q_norm_g', 'new_v_k_norm_g', 'new_v_sinks', 'new_v_conv_out_g', 'new_v_attn_out_g', 'new_v_w_o', 'new_v_norm2_g', 'new_v_w_gate', 'new_v_w_up', 'new_v_w_down']
TWIN_LEAF_KINDS = {'loss': 'loss', 'grad_x': 'grad_x', 'grad_norm1_g': 'grad_w', 'grad_w_in': 'grad_w', 'grad_conv_w': 'grad_w', 'grad_q_norm_g': 'grad_w', 'grad_k_norm_g': 'grad_w', 'grad_sinks': 'grad_w', 'grad_conv_out_g': 'grad_w', 'grad_attn_out_g': 'grad_w', 'grad_w_o': 'grad_w', 'grad_norm2_g': 'grad_w', 'grad_w_gate': 'grad_w', 'grad_w_up': 'grad_w', 'grad_w_down': 'grad_w', 'delta_norm1_g': 'delta_w', 'delta_w_in': 'delta_w', 'delta_conv_w': 'delta_w', 'delta_q_norm_g': 'delta_w', 'delta_k_norm_g': 'delta_w', 'delta_sinks': 'delta_w', 'delta_conv_out_g': 'delta_w', 'delta_attn_out_g': 'delta_w', 'delta_w_o': 'delta_w', 'delta_norm2_g': 'delta_w', 'delta_w_gate': 'delta_w', 'delta_w_up': 'delta_w', 'delta_w_down': 'delta_w', 'new_m_norm1_g': 'new_m', 'new_m_w_in': 'new_m', 'new_m_conv_w': 'new_m', 'new_m_q_norm_g': 'new_m', 'new_m_k_norm_g': 'new_m', 'new_m_sinks': 'new_m', 'new_m_conv_out_g': 'new_m', 'new_m_attn_out_g': 'new_m', 'new_m_w_o': 'new_m', 'new_m_norm2_g': 'new_m', 'new_m_w_gate': 'new_m', 'new_m_w_up': 'new_m', 'new_m_w_down': 'new_m', 'new_v_norm1_g': 'new_v', 'new_v_w_in': 'new_v', 'new_v_conv_w': 'new_v', 'new_v_q_norm_g': 'new_v', 'new_v_k_norm_g': 'new_v', 'new_v_sinks': 'new_v', 'new_v_conv_out_g': 'new_v', 'new_v_attn_out_g': 'new_v', 'new_v_w_o': 'new_v', 'new_v_norm2_g': 'new_v', 'new_v_w_gate': 'new_v', 'new_v_w_up': 'new_v', 'new_v_w_down': 'new_v'}


def _forward(args):
    return _fwd_reference(*[args[k] for k in FWD_PARAMS])


def _output_shape():
    def fwd():
        inp = _fwd_setup_inputs(0)
        return _fwd_reference(*[inp[k] for k in FWD_PARAMS])
    out = _jax.eval_shape(fwd)
    return out.shape, out.dtype

N_MICROBATCH = 1
ADAM_LR = 0.001
ADAM_B1 = 0.9
ADAM_B2 = 0.999
ADAM_EPS = 1e-08
ADAM_WD = 0.01
ADAM_STEP = 10
PER_EXAMPLE_BATCH_AXIS = {'x': 0, 'loss_target': 0}
SHARED_INPUTS = []
_WEIGHT_DTYPES = {'norm1_g': _jnp.float32, 'w_in': _jnp.float32, 'conv_w': _jnp.float32, 'q_norm_g': _jnp.float32, 'k_norm_g': _jnp.float32, 'sinks': _jnp.float32, 'conv_out_g': _jnp.float32, 'attn_out_g': _jnp.float32, 'w_o': _jnp.float32, 'norm2_g': _jnp.float32, 'w_gate': _jnp.float32, 'w_up': _jnp.float32, 'w_down': _jnp.float32}
MOMENT_SCALE = {'norm1_g': 2.452862e+00, 'w_in': 1.667798e+00, 'conv_w': 1.486588e+00, 'q_norm_g': 2.120614e+00, 'k_norm_g': 2.106821e+00, 'sinks': 4.934952e-01, 'conv_out_g': 8.090827e+01, 'attn_out_g': 6.311039e+01, 'w_o': 2.965492e+00, 'norm2_g': 4.931356e+01, 'w_gate': 4.252473e-01, 'w_up': 4.997630e-01, 'w_down': 7.890675e-01}


def _to_microbatches(a, axis):
    t = _jnp.moveaxis(a, axis, 0)
    t = t.reshape((N_MICROBATCH, t.shape[0] // N_MICROBATCH) + t.shape[1:])
    return _jnp.moveaxis(t, 1, axis + 1)


def setup_inputs(seed: int = 0) -> dict:
    inp = _fwd_setup_inputs(seed)
    key = _jax.random.fold_in(_jax.random.key(seed), 7919)
    shape, _ = _output_shape()
    out = dict(inp)
    out["loss_target"] = _jax.random.normal(_jax.random.fold_in(key, 0), shape, _jnp.float32)
    for i, name in enumerate(TWIN_WEIGHTS):
        w = inp[name].astype(_jnp.float32)
        if MOMENT_SCALE is None:
            s = _jnp.sqrt(_jnp.mean(_jnp.square(w)) + 1e-30)
        else:
            s = MOMENT_SCALE[name]
        km, kv = _jax.random.split(_jax.random.fold_in(key, i + 1))
        out[name] = w
        out["m_" + name] = s * _jax.random.normal(km, w.shape, _jnp.float32)
        out["v_" + name] = (s * s) * _jax.random.uniform(kv, w.shape, _jnp.float32, 0.5, 1.5)
    if N_MICROBATCH > 1:
        for name, axis in PER_EXAMPLE_BATCH_AXIS.items():
            out[name] = _to_microbatches(out[name], axis)
    return {'x': out['x'], 'norm1_g': out['norm1_g'], 'w_in': out['w_in'], 'conv_w': out['conv_w'], 'q_norm_g': out['q_norm_g'], 'k_norm_g': out['k_norm_g'], 'sinks': out['sinks'], 'conv_out_g': out['conv_out_g'], 'attn_out_g': out['attn_out_g'], 'w_o': out['w_o'], 'norm2_g': out['norm2_g'], 'w_gate': out['w_gate'], 'w_up': out['w_up'], 'w_down': out['w_down'], 'loss_target': out['loss_target'], 'm_norm1_g': out['m_norm1_g'], 'm_w_in': out['m_w_in'], 'm_conv_w': out['m_conv_w'], 'm_q_norm_g': out['m_q_norm_g'], 'm_k_norm_g': out['m_k_norm_g'], 'm_sinks': out['m_sinks'], 'm_conv_out_g': out['m_conv_out_g'], 'm_attn_out_g': out['m_attn_out_g'], 'm_w_o': out['m_w_o'], 'm_norm2_g': out['m_norm2_g'], 'm_w_gate': out['m_w_gate'], 'm_w_up': out['m_w_up'], 'm_w_down': out['m_w_down'], 'v_norm1_g': out['v_norm1_g'], 'v_w_in': out['v_w_in'], 'v_conv_w': out['v_conv_w'], 'v_q_norm_g': out['v_q_norm_g'], 'v_k_norm_g': out['v_k_norm_g'], 'v_sinks': out['v_sinks'], 'v_conv_out_g': out['v_conv_out_g'], 'v_attn_out_g': out['v_attn_out_g'], 'v_w_o': out['v_w_o'], 'v_norm2_g': out['v_norm2_g'], 'v_w_gate': out['v_w_gate'], 'v_w_up': out['v_w_up'], 'v_w_down': out['v_w_down']}


def _loss(weights, diff, rest, loss_target):
    with _jax.named_scope("forward"):
        args = {**rest, TWIN_DIFF_INPUT: diff, **{k: w.astype(_WEIGHT_DTYPES[k]) for k, w in weights.items()}}
        y = _forward(args)
    with _jax.named_scope("loss_head"):
        err = _jnp.square(y.astype(_jnp.float32) - loss_target)
        return 0.5 * _jnp.sum(_jnp.mean(err, axis=-1)) if err.ndim else 0.5 * err


def _adamw(w, g, m, v):
    m = ADAM_B1 * m + (1.0 - ADAM_B1) * g
    v = ADAM_B2 * v + (1.0 - ADAM_B2) * _jnp.square(g)
    m_hat = m / (1.0 - ADAM_B1 ** ADAM_STEP)
    v_hat = v / (1.0 - ADAM_B2 ** ADAM_STEP)
    delta = -ADAM_LR * (m_hat / (_jnp.sqrt(v_hat) + ADAM_EPS) + ADAM_WD * w)
    return delta, m, v


def reference(x, norm1_g, w_in, conv_w, q_norm_g, k_norm_g, sinks, conv_out_g, attn_out_g, w_o, norm2_g, w_gate, w_up, w_down, loss_target, m_norm1_g, m_w_in, m_conv_w, m_q_norm_g, m_k_norm_g, m_sinks, m_conv_out_g, m_attn_out_g, m_w_o, m_norm2_g, m_w_gate, m_w_up, m_w_down, v_norm1_g, v_w_in, v_conv_w, v_q_norm_g, v_k_norm_g, v_sinks, v_conv_out_g, v_attn_out_g, v_w_o, v_norm2_g, v_w_gate, v_w_up, v_w_down):
    given = dict(x=x, norm1_g=norm1_g, w_in=w_in, conv_w=conv_w, q_norm_g=q_norm_g, k_norm_g=k_norm_g, sinks=sinks, conv_out_g=conv_out_g, attn_out_g=attn_out_g, w_o=w_o, norm2_g=norm2_g, w_gate=w_gate, w_up=w_up, w_down=w_down, loss_target=loss_target, m_norm1_g=m_norm1_g, m_w_in=m_w_in, m_conv_w=m_conv_w, m_q_norm_g=m_q_norm_g, m_k_norm_g=m_k_norm_g, m_sinks=m_sinks, m_conv_out_g=m_conv_out_g, m_attn_out_g=m_attn_out_g, m_w_o=m_w_o, m_norm2_g=m_norm2_g, m_w_gate=m_w_gate, m_w_up=m_w_up, m_w_down=m_w_down, v_norm1_g=v_norm1_g, v_w_in=v_w_in, v_conv_w=v_conv_w, v_q_norm_g=v_q_norm_g, v_k_norm_g=v_k_norm_g, v_sinks=v_sinks, v_conv_out_g=v_conv_out_g, v_attn_out_g=v_attn_out_g, v_w_o=v_w_o, v_norm2_g=v_norm2_g, v_w_gate=v_w_gate, v_w_up=v_w_up, v_w_down=v_w_down)
    weights = {n: given[n] for n in TWIN_WEIGHTS}
    shared = {n: given[n] for n in SHARED_INPUTS}
    per_example = {n: given[n] for n in ['x']}
    grad_fn = _jax.value_and_grad(_loss, argnums=(0, 1))

    def one_microbatch(ex, loss_target):
        ex = dict(ex)
        diff = ex.pop(TWIN_DIFF_INPUT)
        return grad_fn(weights, diff, {**shared, **ex}, loss_target)

    if N_MICROBATCH == 1:
        loss, (grad_w, grad_x) = one_microbatch(per_example, given["loss_target"])
    else:
        def body(carry, xs):
            loss_sum, grad_sum = carry
            l_k, (gw_k, gx_k) = one_microbatch(xs[0], xs[1])
            with _jax.named_scope("update"):
                return (loss_sum + l_k, _jax.tree.map(_jnp.add, grad_sum, gw_k)), gx_k

        init = (_jnp.zeros((), _jnp.float32), _jax.tree.map(_jnp.zeros_like, weights))
        (loss, grad_w), grad_x = _jax.lax.scan(body, init, (per_example, given["loss_target"]))
    with _jax.named_scope("update"):
        delta_w, new_m, new_v = {}, {}, {}
        for n in TWIN_WEIGHTS:
            delta_w[n], new_m[n], new_v[n] = _adamw(weights[n], grad_w[n], given["m_" + n], given["v_" + n])
    return (loss, grad_x, *[grad_w[n] for n in TWIN_WEIGHTS], *[delta_w[n] for n in TWIN_WEIGHTS],
            *[new_m[n] for n in TWIN_WEIGHTS], *[new_v[n] for n in TWIN_WEIGHTS])
```

```python
import functools

import jax
import jax.numpy as jnp
from jax import lax
from jax.experimental import pallas as pl
from jax.experimental.pallas import tpu as pltpu

F32 = jnp.float32
BF16 = jnp.bfloat16

D = 1024
CC = 512
NQ = 8
NKV = 2
HD = 64
HP = 128
GRP = NQ // NKV
FF = 2816
FF_CHUNK = FF // 2
BLK = 128
EPS = 1e-6
NEG = -1e30
SCALE = HD ** -0.5
O_BG, O_CG, O_HC, O_Q = 0, CC, 2 * CC, 3 * CC
O_K = O_Q + NQ * HP
O_V = O_K + NKV * HP
NP = O_V + NKV * HP
NMAIN = O_K
MIXW = CC + NQ * HP
N_CHIPS = 4
VMEM_LIMIT = 56 * 1024 * 1024
MESH = pl.DeviceIdType.MESH

ADAM_LR, ADAM_B1, ADAM_B2, ADAM_EPS, ADAM_WD, ADAM_STEP = 0.001, 0.9, 0.999, 1e-08, 0.01, 10


def _cparams(sem=None, **kw):
    if sem is not None:
        kw["dimension_semantics"] = sem
    return pltpu.CompilerParams(vmem_limit_bytes=VMEM_LIMIT, **kw)


def _const_spec(shape):
    nd = len(shape)
    return pl.BlockSpec(shape, lambda *_: (0,) * nd, pipeline_mode=pl.Buffered(1))


def _nt(a, b):
    return lax.dot_general(a, b, (((1,), (1,)), ((), ())), preferred_element_type=F32)


def _tn(a, b):
    return lax.dot_general(a, b, (((0,), (0,)), ((), ())), preferred_element_type=F32)


def _rms_fwd(x, inv_n):
    r = lax.rsqrt(jnp.sum(x * x, axis=-1, keepdims=True) * inv_n + EPS)
    return r, x * r


def _rms_bwd(dy, g, xh, r, inv_n):
    dxh = dy * g
    return r * (dxh - xh * (jnp.sum(dxh * xh, axis=-1, keepdims=True) * inv_n))


def _inproj_fwd(x, g1, wp, tm):
    t = x.shape[0]

    def body(x_ref, g_ref, w_ref, p_ref, h_ref):
        _, xh = _rms_fwd(x_ref[...], 1.0 / D)
        h = (xh * g_ref[...]).astype(BF16)
        h_ref[...] = h
        p_ref[...] = jnp.dot(h, w_ref[...], preferred_element_type=F32)

    return pl.pallas_call(
        body, name="inproj_fwd", grid=(t // tm,),
        in_specs=[pl.BlockSpec((tm, D), lambda i: (i, 0)), _const_spec((1, D)), _const_spec((D, NP))],
        out_specs=[pl.BlockSpec((tm, NP), lambda i: (i, 0)), pl.BlockSpec((tm, D), lambda i: (i, 0))],
        out_shape=[jax.ShapeDtypeStruct((t, NP), F32), jax.ShapeDtypeStruct((t, D), BF16)],
        compiler_params=_cparams(("parallel",)),
    )(x, g1, wp)


def _band_mask():
    r_io = lax.broadcasted_iota(jnp.int32, (BLK, 2 * BLK), 0)
    c_io = lax.broadcasted_iota(jnp.int32, (BLK, 2 * BLK), 1)
    return (c_io > r_io) & (c_io <= r_io + BLK), c_io


def _conv_taps(uf, n):
    u1 = pltpu.roll(uf, 1, 0)[8:8 + n]
    u2 = pltpu.roll(uf, 2, 0)[8:8 + n]
    return u1, u2


def _attn_probs(qn, kband, sink, valid):
    s = _nt(qn, kband) * SCALE
    s = jnp.where(valid, s, NEG)
    m = jnp.maximum(jnp.max(s, axis=-1, keepdims=True), sink)
    p = jnp.exp(s - m)
    es = jnp.exp(sink - m)
    inv = 1.0 / (jnp.sum(p, axis=-1, keepdims=True) + es)
    return p * inv, es * inv


def _norm_keys(kraw, gk):
    out = []
    for h in range(NKV):
        kh = kraw[:, h * HP:(h + 1) * HP]
        rk, khat = _rms_fwd(kh, 1.0 / HD)
        out.append((khat, rk, (khat * gk).astype(BF16)))
    return out


def _mixer_fwd(proj, x, cw, gq, gk, sinks, gco, gao, wo, tq):
    t = proj.shape[0]
    nb = tq // BLK
    r8 = tq // 8

    def body(p_ref, cgp_ref, hcp_ref, kvp_ref, x_ref, cw_ref, gq_ref, gk_ref, sk_ref, gco_ref, gao_ref,
             wo_ref, xm_ref, mix_ref):
        i = pl.program_id(0)
        cg = p_ref[:, O_CG:O_CG + CC]
        hc = p_ref[:, O_HC:O_HC + CC]
        u = cg * hc
        up = jnp.where(i > 0, cgp_ref[...] * hcp_ref[...], 0.0)
        u1, u2 = _conv_taps(jnp.concatenate([up, u], axis=0), tq)
        y = cw_ref[0:1, :] * u2 + cw_ref[1:2, :] * u1 + cw_ref[2:3, :] * u
        co = p_ref[:, O_BG:O_BG + CC] * y
        _, coh = _rms_fwd(co, 1.0 / CC)
        cn = coh * gco_ref[...]
        kraw = jnp.concatenate([kvp_ref[:, 0:NKV * HP], p_ref[:, O_K:O_K + NKV * HP]], axis=0)
        vraw = jnp.concatenate([kvp_ref[:, NKV * HP:], p_ref[:, O_V:O_V + NKV * HP]], axis=0)
        keys = _norm_keys(kraw, gk_ref[...])
        vb = [vraw[:, h * HP:(h + 1) * HP].astype(BF16) for h in range(NKV)]
        base_valid, c_io = _band_mask()
        rows = []
        for b in range(nb):
            lo = jnp.where(i * nb + b == 0, BLK, 0)
            valid = base_valid & (c_io >= lo)
            outs = []
            for g in range(NQ):
                h = g // GRP
                qg = p_ref[b * BLK:(b + 1) * BLK, O_Q + g * HP:O_Q + (g + 1) * HP]
                _, qh = _rms_fwd(qg, 1.0 / HD)
                qn = (qh * gq_ref[...]).astype(BF16)
                pr, _ = _attn_probs(qn, keys[h][2][b * BLK:b * BLK + 2 * BLK], sk_ref[0, g], valid)
                outs.append(jnp.dot(pr.astype(BF16), vb[h][b * BLK:b * BLK + 2 * BLK],
                                    preferred_element_type=F32))
            rows.append(jnp.concatenate(outs, axis=1))
        ao = jnp.concatenate(rows, axis=0)
        _, aoh = _rms_fwd(ao, 1.0 / (NQ * HD))
        an = aoh * gao_ref[...]
        mix = jnp.concatenate([cn, an], axis=1).astype(BF16)
        mix_ref[...] = mix
        xm_ref[...] = x_ref[...] + jnp.dot(mix, wo_ref[...], preferred_element_type=F32)

    prev8 = lambda col: pl.BlockSpec((8, CC), lambda i: (jnp.maximum(i * r8 - 1, 0), col))
    return pl.pallas_call(
        body, name="mixer_fwd", grid=(t // tq,),
        in_specs=[
            pl.BlockSpec((tq, NP), lambda i: (i, 0)),
            prev8(O_CG // CC), prev8(O_HC // CC),
            pl.BlockSpec((BLK, 2 * NKV * HP), lambda i: (jnp.maximum(i * nb - 1, 0), O_K // (2 * NKV * HP))),
            pl.BlockSpec((tq, D), lambda i: (i, 0)),
            _const_spec((8, CC)), _const_spec((1, HP)), _const_spec((1, HP)),
            pl.BlockSpec(memory_space=pltpu.SMEM),
            _const_spec((1, CC)), _const_spec((1, NQ * HP)), _const_spec((MIXW, D)),
        ],
        out_specs=[pl.BlockSpec((tq, D), lambda i: (i, 0)), pl.BlockSpec((tq, MIXW), lambda i: (i, 0))],
        out_shape=[jax.ShapeDtypeStruct((t, D), F32), jax.ShapeDtypeStruct((t, MIXW), BF16)],
        compiler_params=_cparams(("parallel",)),
    )(proj, proj, proj, proj, x, cw, gq, gk, sinks, gco, gao, wo)


def _ffn_fwd(xm, g2, wg, wu, wd, tm):
    t = xm.shape[0]

    def body(x_ref, g_ref, wg_ref, wu_ref, wd_ref, xo_ref, a_ref, b_ref, h2_ref):
        xv = x_ref[...]
        _, xh = _rms_fwd(xv, 1.0 / D)
        h2 = (xh * g_ref[...]).astype(BF16)
        h2_ref[...] = h2
        acc = xv
        for c0 in range(0, FF, FF_CHUNK):
            cols = slice(c0, c0 + FF_CHUNK)
            a = jnp.dot(h2, wg_ref[:, cols], preferred_element_type=F32)
            b = jnp.dot(h2, wu_ref[:, cols], preferred_element_type=F32)
            a_ref[:, cols] = a.astype(BF16)
            b_ref[:, cols] = b.astype(BF16)
            hm = (a * jax.nn.sigmoid(a) * b).astype(BF16)
            acc = acc + jnp.dot(hm, wd_ref[cols, :], preferred_element_type=F32)
        xo_ref[...] = acc

    row = lambda w: pl.BlockSpec((tm, w), lambda i: (i, 0))
    return pl.pallas_call(
        body, name="ffn_fwd", grid=(t // tm,),
        in_specs=[row(D), _const_spec((1, D)), _const_spec((D, FF)), _const_spec((D, FF)), _const_spec((FF, D))],
        out_specs=[row(D), row(FF), row(FF), row(D)],
        out_shape=[jax.ShapeDtypeStruct((t, D), F32), jax.ShapeDtypeStruct((t, FF), BF16),
                   jax.ShapeDtypeStruct((t, FF), BF16), jax.ShapeDtypeStruct((t, D), BF16)],
        compiler_params=_cparams(("parallel",)),
    )(xm, g2, wg, wu, wd)


def _loss_and_grad(y, tgt, tm):
    t = y.shape[0]

    def body(y_ref, t_ref, l_ref, dy_ref):
        @pl.when(pl.program_id(0) == 0)
        def _():
            l_ref[...] = jnp.zeros_like(l_ref)

        e = y_ref[...] - t_ref[...]
        dy_ref[...] = e * (1.0 / D)
        s = jnp.sum(jnp.sum(e * e, axis=-1, keepdims=True), axis=0, keepdims=True)
        l_ref[...] += s * (0.5 / D)

    row = pl.BlockSpec((tm, D), lambda i: (i, 0))
    return pl.pallas_call(
        body, name="loss", grid=(t // tm,), in_specs=[row, row],
        out_specs=[pl.BlockSpec((8, 128), lambda i: (0, 0)), row],
        out_shape=[jax.ShapeDtypeStruct((8, 128), F32), jax.ShapeDtypeStruct((t, D), F32)],
        compiler_params=_cparams(("arbitrary",)),
    )(y, tgt)


def _ffn_bwd(dy, xm, g2, a, b, wg, wu, wd, tm):
    t = dy.shape[0]

    def body(dy_ref, x_ref, g_ref, a_ref, b_ref, wg_ref, wu_ref, wd_ref, dx_ref, da_ref, db_ref, hm_ref, dg_ref):
        @pl.when(pl.program_id(0) == 0)
        def _():
            dg_ref[...] = jnp.zeros_like(dg_ref)

        dyv = dy_ref[...]
        dyb = dyv.astype(BF16)
        dh2 = jnp.zeros_like(dyv)
        for c0 in range(0, FF, FF_CHUNK):
            cols = slice(c0, c0 + FF_CHUNK)
            dhm = _nt(dyb, wd_ref[cols, :])
            av = a_ref[:, cols].astype(F32)
            bv = b_ref[:, cols].astype(F32)
            sig = jax.nn.sigmoid(av)
            sil = av * sig
            hm_ref[:, cols] = (sil * bv).astype(BF16)
            da = (dhm * bv * (sig * (1.0 + av * (1.0 - sig)))).astype(BF16)
            db = (dhm * sil).astype(BF16)
            da_ref[:, cols] = da
            db_ref[:, cols] = db
            dh2 = dh2 + _nt(da, wg_ref[:, cols]) + _nt(db, wu_ref[:, cols])
        r, xh = _rms_fwd(x_ref[...], 1.0 / D)
        dg_ref[...] += jnp.sum(dh2 * xh, axis=0, keepdims=True)
        dx_ref[...] = dyv + _rms_bwd(dh2, g_ref[...], xh, r, 1.0 / D)

    row = lambda w: pl.BlockSpec((tm, w), lambda i: (i, 0))
    return pl.pallas_call(
        body, name="ffn_bwd", grid=(t // tm,),
        in_specs=[row(D), row(D), _const_spec((1, D)), row(FF), row(FF),
                  _const_spec((D, FF)), _const_spec((D, FF)), _const_spec((FF, D))],
        out_specs=[row(D), row(FF), row(FF), row(FF), pl.BlockSpec((1, D), lambda i: (0, 0))],
        out_shape=[jax.ShapeDtypeStruct((t, D), F32), jax.ShapeDtypeStruct((t, FF), BF16),
                   jax.ShapeDtypeStruct((t, FF), BF16), jax.ShapeDtypeStruct((t, FF), BF16),
                   jax.ShapeDtypeStruct((1, D), F32)],
        compiler_params=_cparams(("arbitrary",)),
    )(dy, xm, g2, a, b, wg, wu, wd)


def _wgrad(a, b, tn, tt, name):
    t, k = a.shape
    n = b.shape[1]
    nsteps = t // tt

    def body(a_ref, b_ref, o_ref):
        @pl.when(pl.program_id(1) == 0)
        def _():
            o_ref[...] = jnp.zeros_like(o_ref)

        o_ref[...] += _tn(a_ref[...].astype(BF16), b_ref[...].astype(BF16))

    return pl.pallas_call(
        body, name=name, grid=(n // tn, nsteps),
        in_specs=[pl.BlockSpec((tt, k), lambda j, s: (s, 0)), pl.BlockSpec((tt, tn), lambda j, s: (s, j))],
        out_specs=pl.BlockSpec((k, tn), lambda j, s: (0, j)),
        out_shape=jax.ShapeDtypeStruct((k, n), F32),
        compiler_params=_cparams(("parallel", "arbitrary")),
    )(a, b)


def _mixer_bwd(dxm, proj, cw, gq, gk, sinks, gco, gao, wo, tq):
    t = proj.shape[0]
    nb = tq // BLK
    r8 = tq // 8
    nt = t // tq
    te = tq + 8
    kvw = 2 * NKV * HP

    def body(dx_ref, dxn_ref, p_ref, cgp_ref, hcp_ref, bgn_ref, cgn_ref, hcn_ref, kvp_ref, cw_ref, gq_ref,
             gk_ref, sk_ref, gco_ref, gao_ref, wo_ref,
             dpm_ref, dkvm_ref, dkvh_ref, dcw_ref, dgq_ref, dgk_ref, dsk_ref, dgco_ref, dgao_ref, acc_ref):
        i = pl.program_id(0)

        @pl.when(i == 0)
        def _():
            for r in (dcw_ref, dgq_ref, dgk_ref, dsk_ref, dgco_ref, dgao_ref):
                r[...] = jnp.zeros_like(r)

        acc_ref[...] = jnp.zeros_like(acc_ref)
        live_rows = jnp.where(i < nt - 1, te, tq)
        dxb = dx_ref[...].astype(BF16)
        dxe = jnp.concatenate([dxb, dxn_ref[...].astype(BF16)], axis=0)
        dcn = _nt(dxe, wo_ref[0:CC, :])
        bg = jnp.concatenate([p_ref[:, O_BG:O_BG + CC], bgn_ref[...]], axis=0)
        cg = jnp.concatenate([p_ref[:, O_CG:O_CG + CC], cgn_ref[...]], axis=0)
        hc = jnp.concatenate([p_ref[:, O_HC:O_HC + CC], hcn_ref[...]], axis=0)
        u = cg * hc
        up = jnp.where(i > 0, cgp_ref[...] * hcp_ref[...], 0.0)
        u1, u2 = _conv_taps(jnp.concatenate([up, u], axis=0), te)
        w0, w1, w2 = cw_ref[0:1, :], cw_ref[1:2, :], cw_ref[2:3, :]
        y = w0 * u2 + w1 * u1 + w2 * u
        co = bg * y
        rc, coh = _rms_fwd(co, 1.0 / CC)
        dco = _rms_bwd(dcn, gco_ref[...], coh, rc, 1.0 / CC)
        row_io = lax.broadcasted_iota(jnp.int32, (te, 1), 0)
        own = row_io < tq
        dgco_ref[...] += jnp.sum(jnp.where(own, dcn * coh, 0.0), axis=0, keepdims=True)
        dyc = jnp.where(row_io < live_rows, dco * bg, 0.0)
        dyo = jnp.where(own, dyc, 0.0)
        dcw_ref[0:1, :] += jnp.sum(dyo * u2, axis=0, keepdims=True)
        dcw_ref[1:2, :] += jnp.sum(dyo * u1, axis=0, keepdims=True)
        dcw_ref[2:3, :] += jnp.sum(dyo * u, axis=0, keepdims=True)
        dy1 = pltpu.roll(dyc, te - 1, 0)[0:tq]
        dy2 = pltpu.roll(dyc, te - 2, 0)[0:tq]
        du = w2 * dyc[0:tq] + w1 * dy1 + w0 * dy2
        dpm_ref[:, O_BG:O_BG + CC] = (dco[0:tq] * y[0:tq]).astype(BF16)
        dpm_ref[:, O_CG:O_CG + CC] = (du * hc[0:tq]).astype(BF16)
        dpm_ref[:, O_HC:O_HC + CC] = (du * cg[0:tq]).astype(BF16)
        dan = _nt(dxb, wo_ref[CC:MIXW, :])
        kraw = jnp.concatenate([kvp_ref[:, 0:NKV * HP], p_ref[:, O_K:O_K + NKV * HP]], axis=0)
        vraw = jnp.concatenate([kvp_ref[:, NKV * HP:], p_ref[:, O_V:O_V + NKV * HP]], axis=0)
        gqv, gkv = gq_ref[...], gk_ref[...]
        keys = _norm_keys(kraw, gkv)
        vb = [vraw[:, h * HP:(h + 1) * HP].astype(BF16) for h in range(NKV)]
        base_valid, c_io = _band_mask()
        lane = lax.broadcasted_iota(jnp.int32, (1, HP), 1)
        for b in range(nb):
            lo = jnp.where(i * nb + b == 0, BLK, 0)
            valid = base_valid & (c_io >= lo)
            band = slice(b * BLK, b * BLK + 2 * BLK)
            qs, prs, pss, outs = [], [], [], []
            for g in range(NQ):
                h = g // GRP
                qg = p_ref[b * BLK:(b + 1) * BLK, O_Q + g * HP:O_Q + (g + 1) * HP]
                rq, qh = _rms_fwd(qg, 1.0 / HD)
                qn = (qh * gqv).astype(BF16)
                pr, ps = _attn_probs(qn, keys[h][2][band], sk_ref[0, g], valid)
                qs.append((rq, qh, qn))
                prs.append(pr)
                pss.append(ps)
                outs.append(jnp.dot(pr.astype(BF16), vb[h][band], preferred_element_type=F32))
            ao = jnp.concatenate(outs, axis=1)
            ra, aoh = _rms_fwd(ao, 1.0 / (NQ * HD))
            danb = dan[b * BLK:(b + 1) * BLK]
            dgao_ref[...] += jnp.sum(danb * aoh, axis=0, keepdims=True)
            dao = _rms_bwd(danb, gao_ref[...], aoh, ra, 1.0 / (NQ * HD))
            dqs = []
            for h in range(NKV):
                dss, dobs = [], []
                for g in range(h * GRP, (h + 1) * GRP):
                    rq, qh, qn = qs[g]
                    dob = dao[:, g * HP:(g + 1) * HP].astype(BF16)
                    dp = _nt(dob, vb[h][band])
                    delta = jnp.sum(prs[g] * dp, axis=-1, keepdims=True)
                    dsb = (prs[g] * (dp - delta) * SCALE).astype(BF16)
                    dsk = -jnp.sum(pss[g] * delta, axis=0, keepdims=True)
                    dsk_ref[...] += jnp.where(lane == g, dsk, 0.0)
                    dqn = jnp.dot(dsb, keys[h][2][band], preferred_element_type=F32)
                    dgq_ref[...] += jnp.sum(dqn * qh, axis=0, keepdims=True)
                    dqs.append(_rms_bwd(dqn, gqv, qh, rq, 1.0 / HD).astype(BF16))
                    dss.append(dsb)
                    dobs.append(dob)
                grp = slice(h * GRP, (h + 1) * GRP)
                dkn = _tn(jnp.concatenate(dss, axis=0), jnp.concatenate([q[2] for q in qs[grp]], axis=0))
                dv = _tn(jnp.concatenate([p.astype(BF16) for p in prs[grp]], axis=0),
                         jnp.concatenate(dobs, axis=0))
                khat, rk = keys[h][0][band], keys[h][1][band]
                dgk_ref[...] += jnp.sum(dkn * khat, axis=0, keepdims=True)
                acc_ref[band, h * HP:(h + 1) * HP] += _rms_bwd(dkn, gkv, khat, rk, 1.0 / HD)
                acc_ref[band, (NKV + h) * HP:(NKV + h + 1) * HP] += dv
            dpm_ref[b * BLK:(b + 1) * BLK, O_Q:O_K] = jnp.concatenate(dqs, axis=1)
        dkvh_ref[...] = acc_ref[0:BLK, :]
        dkvm_ref[...] = acc_ref[BLK:, :]

    prev8 = lambda col: pl.BlockSpec((8, CC), lambda i: (jnp.maximum(i * r8 - 1, 0), col))
    next8 = lambda col: pl.BlockSpec((8, CC), lambda i: (jnp.minimum((i + 1) * r8, t // 8 - 1), col))
    small = lambda n: pl.BlockSpec((1, n), lambda i: (0, 0))
    return pl.pallas_call(
        body, name="mixer_bwd", grid=(nt,),
        in_specs=[
            pl.BlockSpec((tq, D), lambda i: (i, 0)),
            pl.BlockSpec((8, D), lambda i: (jnp.minimum((i + 1) * r8, t // 8 - 1), 0)),
            pl.BlockSpec((tq, NP), lambda i: (i, 0)),
            prev8(O_CG // CC), prev8(O_HC // CC),
            next8(O_BG // CC), next8(O_CG // CC), next8(O_HC // CC),
            pl.BlockSpec((BLK, kvw), lambda i: (jnp.maximum(i * nb - 1, 0), O_K // kvw)),
            _const_spec((8, CC)), _const_spec((1, HP)), _const_spec((1, HP)),
            pl.BlockSpec(memory_space=pltpu.SMEM),
            _const_spec((1, CC)), _const_spec((1, NQ * HP)), _const_spec((MIXW, D)),
        ],
        out_specs=[
            pl.BlockSpec((tq, NMAIN), lambda i: (i, 0)),
            pl.BlockSpec((tq, kvw), lambda i: (i, 0)),
            pl.BlockSpec((BLK, kvw), lambda i: (i, 0)),
            pl.BlockSpec((8, CC), lambda i: (0, 0)), small(HP), small(HP), small(HP), small(CC), small(NQ * HP),
        ],
        out_shape=[
            jax.ShapeDtypeStruct((t, NMAIN), BF16), jax.ShapeDtypeStruct((t, kvw), F32),
            jax.ShapeDtypeStruct((nt * BLK, kvw), F32),
            jax.ShapeDtypeStruct((8, CC), F32), jax.ShapeDtypeStruct((1, HP), F32), jax.ShapeDtypeStruct((1, HP), F32),
            jax.ShapeDtypeStruct((1, HP), F32), jax.ShapeDtypeStruct((1, CC), F32),
            jax.ShapeDtypeStruct((1, NQ * HP), F32),
        ],
        scratch_shapes=[pltpu.VMEM((tq + BLK, kvw), F32)],
        compiler_params=_cparams(("arbitrary",)),
    )(dxm, dxm, proj, proj, proj, proj, proj, proj, proj, cw, gq, gk, sinks, gco, gao, wo)


def _inproj_bwd(dpm, dkv, wp, x, g1, dxm, tm):
    t = x.shape[0]
    kvw = 2 * NKV * HP

    def body(dp_ref, dk_ref, w_ref, x_ref, g_ref, dxm_ref, dx_ref, dg_ref):
        @pl.when(pl.program_id(0) == 0)
        def _():
            dg_ref[...] = jnp.zeros_like(dg_ref)

        dh = _nt(dp_ref[...], w_ref[:, 0:NMAIN]) + _nt(dk_ref[...], w_ref[:, NMAIN:NP])
        r, xh = _rms_fwd(x_ref[...], 1.0 / D)
        dg_ref[...] += jnp.sum(dh * xh, axis=0, keepdims=True)
        dx_ref[...] = dxm_ref[...] + _rms_bwd(dh, g_ref[...], xh, r, 1.0 / D)

    row = lambda w: pl.BlockSpec((tm, w), lambda i: (i, 0))
    return pl.pallas_call(
        body, name="inproj_bwd", grid=(t // tm,),
        in_specs=[row(NMAIN), row(kvw), _const_spec((D, NP)), row(D), _const_spec((1, D)), row(D)],
        out_specs=[row(D), pl.BlockSpec((1, D), lambda i: (0, 0))],
        out_shape=[jax.ShapeDtypeStruct((t, D), F32), jax.ShapeDtypeStruct((1, D), F32)],
        compiler_params=_cparams(("arbitrary",)),
    )(dpm, dkv, wp, x, g1, dxm)


def _rows_tile(rows):
    for cand in (512, 256, 128, 64, 32, 16, 8):
        if rows % cand == 0:
            return cand
    return rows


def _add_pairs(xs, ys, name):
    outs = []
    for n, (xa, ya) in enumerate(zip(xs, ys)):
        rows, cols = xa.shape
        tr = _rows_tile(rows)

        def body(x_ref, y_ref, o_ref):
            o_ref[...] = (x_ref[...].astype(F32) + y_ref[...].astype(F32)).astype(BF16)

        spec = pl.BlockSpec((tr, cols), lambda i: (i, 0))
        outs.append(pl.pallas_call(
            body, name=f"{name}_{n}", grid=(rows // tr,), in_specs=[spec, spec], out_specs=spec,
            out_shape=jax.ShapeDtypeStruct((rows, cols), BF16), compiler_params=_cparams(("parallel",)),
        )(xa, ya))
    return outs


def _sum_chips(cs, name):
    outs = []
    for n, ca in enumerate(cs):
        _, rows, cols = ca.shape
        tr = _rows_tile(rows)

        def body(c_ref, o_ref):
            acc = c_ref[0].astype(F32)
            for j in range(1, N_CHIPS):
                acc = acc + c_ref[j].astype(F32)
            o_ref[...] = acc

        outs.append(pl.pallas_call(
            body, name=f"{name}_{n}", grid=(rows // tr,),
            in_specs=[pl.BlockSpec((N_CHIPS, tr, cols), lambda i: (0, i, 0))],
            out_specs=pl.BlockSpec((tr, cols), lambda i: (i, 0)),
            out_shape=jax.ShapeDtypeStruct((rows, cols), F32), compiler_params=_cparams(("parallel",)),
        )(ca))
    return outs


def _adamw(w, g, m, v, name):
    rows, cols = w.shape
    tr = _rows_tile(rows)
    c1 = 1.0 - ADAM_B1 ** ADAM_STEP
    c2 = 1.0 - ADAM_B2 ** ADAM_STEP

    def body(w_ref, g_ref, m_ref, v_ref, d_ref, mo_ref, vo_ref):
        gv = g_ref[...]
        mn = ADAM_B1 * m_ref[...] + (1.0 - ADAM_B1) * gv
        vn = ADAM_B2 * v_ref[...] + (1.0 - ADAM_B2) * (gv * gv)
        mo_ref[...] = mn
        vo_ref[...] = vn
        d_ref[...] = -ADAM_LR * ((mn / c1) / (jnp.sqrt(vn / c2) + ADAM_EPS) + ADAM_WD * w_ref[...])

    spec = pl.BlockSpec((tr, cols), lambda i: (i, 0))
    sds = jax.ShapeDtypeStruct((rows, cols), F32)
    return pl.pallas_call(
        body, name=name, grid=(rows // tr,), in_specs=[spec] * 4, out_specs=[spec] * 3, out_shape=[sds] * 3,
        compiler_params=_cparams(("parallel",)),
    )(w, g, m, v)


def _place():
    x, y, c = lax.axis_index("x"), lax.axis_index("y"), lax.axis_index("c")
    chips = [(1 - x, y), (x, 1 - y), (1 - x, 1 - y)]
    return x, y, c, chips


ANY = pl.BlockSpec(memory_space=pl.ANY)


def _gather_weights(shards, cwp):
    nw = len(shards)

    def body(*refs):
        srcs, cw_ref = refs[:nw], refs[nw]
        outs, ocw_ref = refs[nw + 1:2 * nw + 1], refs[2 * nw + 1]
        ssem, rsem, fssem, frsem, lsem = refs[2 * nw + 2:]
        x, y, c, chips = _place()
        kme = 2 * x + y
        sib = (x, y, 1 - c)
        local = [pltpu.make_async_copy(srcs[w], outs[w].at[kme], lsem.at[w]) for w in range(nw)]
        local.append(pltpu.make_async_copy(cw_ref, ocw_ref.at[kme], lsem.at[nw]))
        for cp in local:
            cp.start()

        def landing(w, block_chip, layer):
            return ocw_ref.at[block_chip] if w == nw else outs[w].at[block_chip, layer]

        def plane(j, w, block_chip, src, to):
            return pltpu.make_async_remote_copy(
                src_ref=src, dst_ref=landing(w, block_chip, c), send_sem=ssem.at[j, w], recv_sem=rsem.at[j, w],
                device_id=to, device_id_type=MESH)

        def passed(j, w, block_chip, layer):
            ref = landing(w, block_chip, layer)
            return pltpu.make_async_remote_copy(
                src_ref=ref, dst_ref=ref, send_sem=fssem.at[j, w], recv_sem=frsem.at[j, w],
                device_id=sib, device_id_type=MESH)

        sends = []
        for j, (px, py) in enumerate(chips):
            for w in range(nw + 1):
                cp = plane(j, w, kme, cw_ref if w == nw else srcs[w].at[c], (px, py, c))
                cp.start()
                sends.append(cp)
        for j, (px, py) in enumerate(chips):
            for w in range(nw + 1):
                got = landing(w, 2 * px + py, c)
                plane(j, w, 2 * px + py, got, (px, py, c)).wait_recv()
                if w < nw:
                    cp = passed(j, w, 2 * px + py, c)
                    cp.start()
                    sends.append(cp)
        for j, (px, py) in enumerate(chips):
            for w in range(nw):
                passed(j, w, 2 * px + py, 1 - c).wait_recv()
        for cp in sends:
            cp.wait_send()
        for cp in local:
            cp.wait()

    out_shape = [jax.ShapeDtypeStruct((N_CHIPS,) + s.shape, s.dtype) for s in shards]
    out_shape.append(jax.ShapeDtypeStruct((N_CHIPS,) + cwp.shape, cwp.dtype))
    return pl.pallas_call(
        body, name="gather_weights", in_specs=[ANY] * (nw + 1), out_specs=[ANY] * (nw + 1), out_shape=out_shape,
        scratch_shapes=[pltpu.SemaphoreType.DMA((3, nw + 1)), pltpu.SemaphoreType.DMA((3, nw + 1)),
                        pltpu.SemaphoreType.DMA((3, nw)), pltpu.SemaphoreType.DMA((3, nw)),
                        pltpu.SemaphoreType.DMA((nw + 1,))],
        compiler_params=_cparams(has_side_effects=True),
    )(*shards, cwp)


def _swap_layers(gs):
    nw = len(gs)

    def body(*refs):
        srcs = refs[:nw]
        mine, theirs = refs[nw:2 * nw], refs[2 * nw:3 * nw]
        ssem, rsem, lsem = refs[3 * nw:]
        x, y, c, _ = _place()
        cps = []
        for w in range(nw):
            lc = pltpu.make_async_copy(srcs[w].at[c], mine[w], lsem.at[w])
            lc.start()
            cp = pltpu.make_async_remote_copy(
                src_ref=srcs[w].at[1 - c], dst_ref=theirs[w], send_sem=ssem.at[w], recv_sem=rsem.at[w],
                device_id=(x, y, 1 - c), device_id_type=MESH)
            cp.start()
            cps.append((lc, cp))
        for lc, cp in cps:
            cp.wait()
            lc.wait()

    half = [jax.ShapeDtypeStruct(g.shape[1:], g.dtype) for g in gs]
    res = pl.pallas_call(
        body, name="swap_layers", in_specs=[ANY] * nw, out_specs=[ANY] * (2 * nw), out_shape=half + half,
        scratch_shapes=[pltpu.SemaphoreType.DMA((nw,))] * 3,
        compiler_params=_cparams(has_side_effects=True),
    )(*gs)
    return res[:nw], res[nw:]


def _scatter_chips(ps):
    nw = len(ps)

    def body(*refs):
        srcs, outs = refs[:nw], refs[nw:2 * nw]
        ssem, rsem, lsem = refs[2 * nw:]
        x, y, c, chips = _place()
        kme = 2 * x + y
        cps = []
        for w in range(nw):
            lc = pltpu.make_async_copy(srcs[w].at[kme], outs[w].at[kme], lsem.at[w])
            lc.start()
            cps.append(lc)
        sends = []
        for j, (px, py) in enumerate(chips):
            for w in range(nw):
                cp = pltpu.make_async_remote_copy(
                    src_ref=srcs[w].at[2 * px + py], dst_ref=outs[w].at[kme], send_sem=ssem.at[j, w],
                    recv_sem=rsem.at[j, w], device_id=(px, py, c), device_id_type=MESH)
                cp.start()
                sends.append(cp)
        for j, (px, py) in enumerate(chips):
            for w in range(nw):
                ref = outs[w].at[2 * px + py]
                pltpu.make_async_remote_copy(
                    src_ref=ref, dst_ref=ref, send_sem=ssem.at[j, w], recv_sem=rsem.at[j, w],
                    device_id=(px, py, c), device_id_type=MESH).wait_recv()
        for cp in sends:
            cp.wait_send()
        for lc in cps:
            lc.wait()

    return pl.pallas_call(
        body, name="scatter_chips", in_specs=[ANY] * nw, out_specs=[ANY] * nw,
        out_shape=[jax.ShapeDtypeStruct(p.shape, p.dtype) for p in ps],
        scratch_shapes=[pltpu.SemaphoreType.DMA((3, nw)), pltpu.SemaphoreType.DMA((3, nw)),
                        pltpu.SemaphoreType.DMA((nw,))],
        compiler_params=_cparams(has_side_effects=True),
    )(*ps)


def _join_layers(rs):
    nw = len(rs)

    def body(*refs):
        srcs, outs = refs[:nw], refs[nw:2 * nw]
        ssem, rsem, lsem = refs[2 * nw:]
        x, y, c, _ = _place()
        cps = []
        for w in range(nw):
            lc = pltpu.make_async_copy(srcs[w], outs[w].at[c], lsem.at[w])
            lc.start()
            cp = pltpu.make_async_remote_copy(
                src_ref=srcs[w], dst_ref=outs[w].at[c], send_sem=ssem.at[w], recv_sem=rsem.at[w],
                device_id=(x, y, 1 - c), device_id_type=MESH)
            cp.start()
            cps.append((lc, cp))
        for lc, cp in cps:
            cp.wait()
            lc.wait()

    return pl.pallas_call(
        body, name="join_layers", in_specs=[ANY] * nw, out_specs=[ANY] * nw,
        out_shape=[jax.ShapeDtypeStruct((2,) + r.shape, r.dtype) for r in rs],
        scratch_shapes=[pltpu.SemaphoreType.DMA((nw,))] * 3,
        compiler_params=_cparams(has_side_effects=True),
    )(*rs)


def _allreduce_small(v):
    rows = v.shape[0]

    def body(v_ref, o_ref, buf, ssem, rsem):
        x, y, c, _ = _place()
        me = 4 * x + 2 * y + c
        buf[me] = v_ref[...]
        sends = []
        for r in range(1, 8):
            peer = (x ^ (r >> 2), y ^ ((r >> 1) & 1), c ^ (r & 1))
            cp = pltpu.make_async_remote_copy(
                src_ref=v_ref, dst_ref=buf.at[me], send_sem=ssem.at[r - 1], recv_sem=rsem.at[r - 1],
                device_id=peer, device_id_type=MESH)
            cp.start()
            sends.append(cp)
        for r in range(1, 8):
            src = me ^ r
            pltpu.make_async_remote_copy(
                src_ref=v_ref, dst_ref=buf.at[src], send_sem=ssem.at[r - 1], recv_sem=rsem.at[r - 1],
                device_id=(x, y, c), device_id_type=MESH).wait_recv()
        for cp in sends:
            cp.wait_send()
        acc = buf[0]
        for d in range(1, 8):
            acc = acc + buf[d]
        o_ref[...] = acc

    vm = pl.BlockSpec(memory_space=pltpu.VMEM)
    return pl.pallas_call(
        body, name="allreduce_small", in_specs=[vm], out_specs=vm,
        out_shape=jax.ShapeDtypeStruct(v.shape, F32),
        scratch_shapes=[pltpu.VMEM((8, rows, 128), F32), pltpu.SemaphoreType.DMA((7,)),
                        pltpu.SemaphoreType.DMA((7,))],
        compiler_params=_cparams(has_side_effects=True),
    )(v)


def _pad_heads(w, n_heads, axis):
    shp = w.shape
    w = w.reshape(shp[:axis] + (n_heads, HD) + shp[axis + 1:])
    pad = [(0, 0)] * w.ndim
    pad[axis + 1] = (0, HP - HD)
    w = jnp.pad(w, pad)
    return w.reshape(shp[:axis] + (n_heads * HP,) + shp[axis + 1:])


def _strip_heads(w, n_heads, axis):
    shp = w.shape
    w = w.reshape(shp[:axis] + (n_heads, HP) + shp[axis + 1:])
    w = lax.slice_in_dim(w, 0, HD, axis=axis + 1)
    return w.reshape(shp[:axis] + (n_heads * HD,) + shp[axis + 1:])


def _unshard_cols(g4):
    k, r, c = g4.shape
    return jnp.transpose(g4, (1, 0, 2)).reshape(r, k * c)


def _shard_cols(w):
    r, n = w.shape
    return jnp.transpose(w.reshape(r, N_CHIPS, n // N_CHIPS), (1, 0, 2))


def _pad_win(win):
    parts = [win[:, :3 * CC], _pad_heads(win[:, 3 * CC:3 * CC + NQ * HD], NQ, 1),
             _pad_heads(win[:, 3 * CC + NQ * HD:3 * CC + (NQ + NKV) * HD], NKV, 1),
             _pad_heads(win[:, 3 * CC + (NQ + NKV) * HD:], NKV, 1)]
    return jnp.concatenate(parts, axis=1)


def _strip_win(gp):
    parts = [gp[:, :3 * CC], _strip_heads(gp[:, O_Q:O_K], NQ, 1), _strip_heads(gp[:, O_K:O_V], NKV, 1),
             _strip_heads(gp[:, O_V:], NKV, 1)]
    return jnp.concatenate(parts, axis=1)


def _count(shape):
    n = 1
    for s in shape:
        n *= s
    return n


def _pack_rows(arrs):
    flat = [jnp.pad(a.reshape(-1), (0, (-_count(a.shape)) % 128)) for a in arrs]
    v = jnp.concatenate(flat)
    rows = -(-v.shape[0] // (8 * 128)) * 8
    return jnp.pad(v, (0, rows * 128 - v.shape[0])).reshape(rows, 128)


def kernel(x, norm1_g, w_in, conv_w, q_norm_g, k_norm_g, sinks, conv_out_g, attn_out_g, w_o, norm2_g, w_gate, w_up, w_down, loss_target, m_norm1_g, m_w_in, m_conv_w, m_q_norm_g, m_k_norm_g, m_sinks, m_conv_out_g, m_attn_out_g, m_w_o, m_norm2_g, m_w_gate, m_w_up, m_w_down, v_norm1_g, v_w_in, v_conv_w, v_q_norm_g, v_k_norm_g, v_sinks, v_conv_out_g, v_attn_out_g, v_w_o, v_norm2_g, v_w_gate, v_w_up, v_w_down):
    depth = w_in.shape[0]
    t = x.shape[1]
    xs = x.reshape(t, D)
    tgt = loss_target.reshape(t, D)
    xi, yi = lax.axis_index("x"), lax.axis_index("y")
    kme = 2 * xi + yi
    tm = min(512, t)
    tq = min(256, t)
    tf = min(256, t)

    cwp = jnp.pad(conv_w.reshape(depth * 3, CC // N_CHIPS), ((0, 8 - depth * 3), (0, 0)))
    big = [w_in, w_o, w_gate, w_up, w_down]
    gathered = _gather_weights([w.astype(BF16) for w in big], cwp)
    a_in, a_o, a_g, a_u, a_d, a_cw = gathered
    cw_full = _unshard_cols(a_cw)[:depth * 3].reshape(depth, 3, CC)
    layers = []
    for l in range(depth):
        wp = _pad_win(_unshard_cols(a_in[:, l]))
        wo = a_o[:, l].reshape(D, D)
        wo = jnp.concatenate([wo[:CC], _pad_heads(wo[CC:], NQ, 0)], axis=0)
        layers.append(dict(
            wp=wp, wo=wo, wg=_unshard_cols(a_g[:, l]), wu=_unshard_cols(a_u[:, l]), wd=a_d[:, l].reshape(FF, D),
            cw=jnp.pad(cw_full[l], ((0, 5), (0, 0))),
            g1=norm1_g[l].reshape(1, D), g2=norm2_g[l].reshape(1, D),
            gq=jnp.pad(q_norm_g[l], (0, HP - HD)).reshape(1, HP), gk=jnp.pad(k_norm_g[l], (0, HP - HD)).reshape(1, HP),
            sk=sinks[l].reshape(1, NQ), gco=conv_out_g[l].reshape(1, CC),
            gao=_pad_heads(attn_out_g[l], NQ, 0).reshape(1, NQ * HP)))

    saved = []
    cur = xs
    for l in range(depth):
        p = layers[l]
        proj, h = _inproj_fwd(cur, p["g1"], p["wp"], tm)
        xm, mix = _mixer_fwd(proj, cur, p["cw"], p["gq"], p["gk"], p["sk"], p["gco"], p["gao"], p["wo"], tq)
        xo, a, b, h2 = _ffn_fwd(xm, p["g2"], p["wg"], p["wu"], p["wd"], tf)
        saved.append(dict(x=cur, proj=proj, h=h, xm=xm, mix=mix, a=a, b=b, h2=h2))
        cur = xo
    lpart, dy = _loss_and_grad(cur, tgt, tm)
    loss = lax.psum(lpart[0, 0], ("x", "y", "c"))

    nt = t // tq
    gbig = [None] * depth
    gsmall = [None] * depth
    for l in reversed(range(depth)):
        p, s = layers[l], saved[l]
        dxm, da, db, hm, dg2 = _ffn_bwd(dy, s["xm"], p["g2"], s["a"], s["b"], p["wg"], p["wu"], p["wd"], tf)
        g_wg = _wgrad(s["h2"], da, FF // 2, tm, "wgrad_gate")
        g_wu = _wgrad(s["h2"], db, FF // 2, tm, "wgrad_up")
        g_wdt = _wgrad(dy, hm, FF // 2, tm, "wgrad_down")
        dpm, dkvm, dkvh, dcw, dgq, dgk, dsk, dgco, dgao = _mixer_bwd(
            dxm, s["proj"], p["cw"], p["gq"], p["gk"], p["sk"], p["gco"], p["gao"], p["wo"], tq)
        g_wot = _wgrad(dxm, s["mix"], MIXW, tm, "wgrad_o")
        kvw = dkvm.shape[1]
        halo = jnp.concatenate([dkvh.reshape(nt, BLK, kvw)[1:], jnp.zeros((1, BLK, kvw), F32)], axis=0)
        halo = jnp.pad(halo, ((0, 0), (tq - BLK, 0), (0, 0)))
        dkv = (dkvm.reshape(nt, tq, kvw) + halo).reshape(t, kvw).astype(BF16)
        dx, dg1 = _inproj_bwd(dpm, dkv, p["wp"], s["x"], p["g1"], dxm, tm)
        g_wpm = _wgrad(s["h"], dpm, NMAIN // 2, tm, "wgrad_in_main")
        g_wpk = _wgrad(s["h"], dkv, kvw, tm, "wgrad_in_kv")
        dy = dx
        g_in = _strip_win(jnp.concatenate([g_wpm, g_wpk], axis=1))
        g_ot = jnp.concatenate([g_wot[:, :CC], _strip_heads(g_wot[:, CC:], NQ, 1)], axis=1)
        gbig[l] = [_shard_cols(g.astype(BF16)) for g in (g_in, g_ot, g_wg, g_wu, g_wdt)]
        gsmall[l] = dict(g1=dg1, cw=dcw[:3], gq=dgq[0, :HD], gk=dgk[0, :HD], sk=dsk[0, :NQ], gco=dgco,
                         gao=_strip_heads(dgao.reshape(NQ * HP), NQ, 0), g2=dg2)
    grad_x = dy.reshape(x.shape)

    nw = len(big)
    gs = [jnp.stack([gbig[l][w] for l in range(depth)]) for w in range(nw)]
    mine, theirs = _swap_layers(gs)
    flat = lambda a4: a4.reshape(a4.shape[0] * a4.shape[1], a4.shape[2])
    ps = _add_pairs([flat(a4) for a4 in mine], [flat(a4) for a4 in theirs], "presum")
    cs = _scatter_chips([q.reshape(m4.shape) for q, m4 in zip(ps, mine)])
    rs = _join_layers(_sum_chips(cs, "chipsum"))
    r_in, r_ot, r_g, r_u, r_dt = rs
    g_big = [r_in, jnp.transpose(r_ot, (0, 2, 1)), r_g, r_u, jnp.transpose(r_dt, (0, 2, 1))]

    small_shapes = dict(g1=(D,), cw=(3, CC), gq=(HD,), gk=(HD,), sk=(NQ,), gco=(CC,), gao=(NQ * HD,), g2=(D,))
    red = _allreduce_small(_pack_rows([gsmall[l][n] for l in range(depth) for n in small_shapes])).reshape(-1)
    red_small, offs = {n: [] for n in small_shapes}, 0
    for l in range(depth):
        for n, shp in small_shapes.items():
            cnt = _count(shp)
            red_small[n].append(red[offs:offs + cnt].reshape(shp))
            offs += -(-cnt // 128) * 128
    g_small = {n: jnp.stack(v) for n, v in red_small.items()}
    g_cw = lax.dynamic_slice_in_dim(g_small["cw"], kme * (CC // N_CHIPS), CC // N_CHIPS, axis=2)

    weights = [norm1_g, w_in, conv_w, q_norm_g, k_norm_g, sinks, conv_out_g, attn_out_g, w_o, norm2_g, w_gate,
               w_up, w_down]
    moms = [m_norm1_g, m_w_in, m_conv_w, m_q_norm_g, m_k_norm_g, m_sinks, m_conv_out_g, m_attn_out_g, m_w_o,
            m_norm2_g, m_w_gate, m_w_up, m_w_down]
    vars_ = [v_norm1_g, v_w_in, v_conv_w, v_q_norm_g, v_k_norm_g, v_sinks, v_conv_out_g, v_attn_out_g, v_w_o,
             v_norm2_g, v_w_gate, v_w_up, v_w_down]
    grads = [g_small["g1"], g_big[0], g_cw, g_small["gq"], g_small["gk"], g_small["sk"], g_small["gco"],
             g_small["gao"], g_big[1], g_small["g2"], g_big[2], g_big[3], g_big[4]]
    n_w = len(weights)
    big_idx = [1, 8, 10, 11, 12]
    small_idx = [n for n in range(n_w) if n not in big_idx]
    deltas, new_m, new_v = [None] * n_w, [None] * n_w, [None] * n_w
    for n in big_idx:
        shp = weights[n].shape
        two = [a3.reshape(shp[0] * shp[1], shp[2]) for a3 in (weights[n], grads[n], moms[n], vars_[n])]
        res = _adamw(*two, f"adamw_{n}")
        deltas[n], new_m[n], new_v[n] = [r.reshape(shp) for r in res]
    res = _adamw(*[_pack_rows([arrs[n] for n in small_idx]) for arrs in (weights, grads, moms, vars_)],
                 "adamw_small")
    offs = 0
    for n in small_idx:
        shp = weights[n].shape
        cnt = _count(shp)
        deltas[n], new_m[n], new_v[n] = [r.reshape(-1)[offs:offs + cnt].reshape(shp) for r in res]
        offs += -(-cnt // 128) * 128
    return (loss, grad_x, *grads, *deltas, *new_m, *new_v)
```

```python
import functools

import jax
import jax.numpy as jnp
from jax import lax
from jax.experimental import pallas as pl
from jax.experimental.pallas import tpu as pltpu

F32 = jnp.float32
BF16 = jnp.bfloat16

D = 1024
CC = 512
NQ = 8
NKV = 2
HD = 64
HP = 128
GRP = NQ // NKV
FF = 2816
FF_CHUNK = FF // 2
BLK = 128
EPS = 1e-6
NEG = -1e30
SCALE = HD ** -0.5
O_BG, O_CG, O_HC, O_Q = 0, CC, 2 * CC, 3 * CC
O_K = O_Q + NQ * HP
O_V = O_K + NKV * HP
NP = O_V + NKV * HP
NMAIN = O_K
MIXW = CC + NQ * HP
N_CHIPS = 4
VMEM_LIMIT = 56 * 1024 * 1024
MESH = pl.DeviceIdType.MESH

ADAM_LR, ADAM_B1, ADAM_B2, ADAM_EPS, ADAM_WD, ADAM_STEP = 0.001, 0.9, 0.999, 1e-08, 0.01, 10


def _cparams(sem=None, **kw):
    if sem is not None:
        kw["dimension_semantics"] = sem
    return pltpu.CompilerParams(vmem_limit_bytes=VMEM_LIMIT, **kw)


def _const_spec(shape):
    nd = len(shape)
    return pl.BlockSpec(shape, lambda *_: (0,) * nd, pipeline_mode=pl.Buffered(1))


def _nt(a, b):
    return lax.dot_general(a, b, (((1,), (1,)), ((), ())), preferred_element_type=F32)


def _tn(a, b):
    return lax.dot_general(a, b, (((0,), (0,)), ((), ())), preferred_element_type=F32)


def _rms_fwd(x, inv_n):
    r = lax.rsqrt(jnp.sum(x * x, axis=-1, keepdims=True) * inv_n + EPS)
    return r, x * r


def _rms_bwd(dy, g, xh, r, inv_n):
    dxh = dy * g
    return r * (dxh - xh * (jnp.sum(dxh * xh, axis=-1, keepdims=True) * inv_n))


def _inproj_fwd(x, g1, wp, tm):
    t = x.shape[0]

    def body(x_ref, g_ref, w_ref, p_ref, h_ref):
        _, xh = _rms_fwd(x_ref[...], 1.0 / D)
        h = (xh * g_ref[...]).astype(BF16)
        h_ref[...] = h
        p_ref[...] = jnp.dot(h, w_ref[...], preferred_element_type=F32)

    return pl.pallas_call(
        body, name="inproj_fwd", grid=(t // tm,),
        in_specs=[pl.BlockSpec((tm, D), lambda i: (i, 0)), _const_spec((1, D)), _const_spec((D, NP))],
        out_specs=[pl.BlockSpec((tm, NP), lambda i: (i, 0)), pl.BlockSpec((tm, D), lambda i: (i, 0))],
        out_shape=[jax.ShapeDtypeStruct((t, NP), F32), jax.ShapeDtypeStruct((t, D), BF16)],
        compiler_params=_cparams(("parallel",)),
    )(x, g1, wp)


def _band_mask():
    r_io = lax.broadcasted_iota(jnp.int32, (BLK, 2 * BLK), 0)
    c_io = lax.broadcasted_iota(jnp.int32, (BLK, 2 * BLK), 1)
    return (c_io > r_io) & (c_io <= r_io + BLK), c_io


def _conv_taps(uf, n):
    u1 = pltpu.roll(uf, 1, 0)[8:8 + n]
    u2 = pltpu.roll(uf, 2, 0)[8:8 + n]
    return u1, u2


def _attn_probs(qn, kband, sink, valid):
    s = _nt(qn, kband) * SCALE
    s = jnp.where(valid, s, NEG)
    m = jnp.maximum(jnp.max(s, axis=-1, keepdims=True), sink)
    p = jnp.exp(s - m)
    es = jnp.exp(sink - m)
    inv = 1.0 / (jnp.sum(p, axis=-1, keepdims=True) + es)
    return p * inv, es * inv


def _norm_keys(kraw, gk):
    out = []
    for h in range(NKV):
        kh = kraw[:, h * HP:(h + 1) * HP]
        rk, khat = _rms_fwd(kh, 1.0 / HD)
        out.append((khat, rk, (khat * gk).astype(BF16)))
    return out


def _mixer_fwd(proj, x, cw, gq, gk, sinks, gco, gao, wo, tq):
    t = proj.shape[0]
    nb = tq // BLK
    r8 = tq // 8

    def body(p_ref, cgp_ref, hcp_ref, kvp_ref, x_ref, cw_ref, gq_ref, gk_ref, sk_ref, gco_ref, gao_ref,
             wo_ref, xm_ref, mix_ref):
        i = pl.program_id(0)
        cg = p_ref[:, O_CG:O_CG + CC]
        hc = p_ref[:, O_HC:O_HC + CC]
        u = cg * hc
        up = jnp.where(i > 0, cgp_ref[...] * hcp_ref[...], 0.0)
        u1, u2 = _conv_taps(jnp.concatenate([up, u], axis=0), tq)
        y = cw_ref[0:1, :] * u2 + cw_ref[1:2, :] * u1 + cw_ref[2:3, :] * u
        co = p_ref[:, O_BG:O_BG + CC] * y
        _, coh = _rms_fwd(co, 1.0 / CC)
        cn = coh * gco_ref[...]
        kraw = jnp.concatenate([kvp_ref[:, 0:NKV * HP], p_ref[:, O_K:O_K + NKV * HP]], axis=0)
        vraw = jnp.concatenate([kvp_ref[:, NKV * HP:], p_ref[:, O_V:O_V + NKV * HP]], axis=0)
        keys = _norm_keys(kraw, gk_ref[...])
        vb = [vraw[:, h * HP:(h + 1) * HP].astype(BF16) for h in range(NKV)]
        base_valid, c_io = _band_mask()
        rows = []
        for b in range(nb):
            lo = jnp.where(i * nb + b == 0, BLK, 0)
            valid = base_valid & (c_io >= lo)
            outs = []
            for g in range(NQ):
                h = g // GRP
                qg = p_ref[b * BLK:(b + 1) * BLK, O_Q + g * HP:O_Q + (g + 1) * HP]
                _, qh = _rms_fwd(qg, 1.0 / HD)
                qn = (qh * gq_ref[...]).astype(BF16)
                pr, _ = _attn_probs(qn, keys[h][2][b * BLK:b * BLK + 2 * BLK], sk_ref[0, g], valid)
                outs.append(jnp.dot(pr.astype(BF16), vb[h][b * BLK:b * BLK + 2 * BLK],
                                    preferred_element_type=F32))
            rows.append(jnp.concatenate(outs, axis=1))
        ao = jnp.concatenate(rows, axis=0)
        _, aoh = _rms_fwd(ao, 1.0 / (NQ * HD))
        an = aoh * gao_ref[...]
        mix = jnp.concatenate([cn, an], axis=1).astype(BF16)
        mix_ref[...] = mix
        xm_ref[...] = x_ref[...] + jnp.dot(mix, wo_ref[...], preferred_element_type=F32)

    prev8 = lambda col: pl.BlockSpec((8, CC), lambda i: (jnp.maximum(i * r8 - 1, 0), col))
    return pl.pallas_call(
        body, name="mixer_fwd", grid=(t // tq,),
        in_specs=[
            pl.BlockSpec((tq, NP), lambda i: (i, 0)),
            prev8(O_CG // CC), prev8(O_HC // CC),
            pl.BlockSpec((BLK, 2 * NKV * HP), lambda i: (jnp.maximum(i * nb - 1, 0), O_K // (2 * NKV * HP))),
            pl.BlockSpec((tq, D), lambda i: (i, 0)),
            _const_spec((8, CC)), _const_spec((1, HP)), _const_spec((1, HP)),
            pl.BlockSpec(memory_space=pltpu.SMEM),
            _const_spec((1, CC)), _const_spec((1, NQ * HP)), _const_spec((MIXW, D)),
        ],
        out_specs=[pl.BlockSpec((tq, D), lambda i: (i, 0)), pl.BlockSpec((tq, MIXW), lambda i: (i, 0))],
        out_shape=[jax.ShapeDtypeStruct((t, D), F32), jax.ShapeDtypeStruct((t, MIXW), BF16)],
        compiler_params=_cparams(("parallel",)),
    )(proj, proj, proj, proj, x, cw, gq, gk, sinks, gco, gao, wo)


def _ffn_fwd(xm, g2, wg, wu, wd, tm):
    t = xm.shape[0]

    def body(x_ref, g_ref, wg_ref, wu_ref, wd_ref, xo_ref, a_ref, b_ref, h2_ref):
        xv = x_ref[...]
        _, xh = _rms_fwd(xv, 1.0 / D)
        h2 = (xh * g_ref[...]).astype(BF16)
        h2_ref[...] = h2
        acc = xv
        for c0 in range(0, FF, FF_CHUNK):
            cols = slice(c0, c0 + FF_CHUNK)
            a = jnp.dot(h2, wg_ref[:, cols], preferred_element_type=F32)
            b = jnp.dot(h2, wu_ref[:, cols], preferred_element_type=F32)
            a_ref[:, cols] = a.astype(BF16)
            b_ref[:, cols] = b.astype(BF16)
            hm = (a * jax.nn.sigmoid(a) * b).astype(BF16)
            acc = acc + jnp.dot(hm, wd_ref[cols, :], preferred_element_type=F32)
        xo_ref[...] = acc

    row = lambda w: pl.BlockSpec((tm, w), lambda i: (i, 0))
    return pl.pallas_call(
        body, name="ffn_fwd", grid=(t // tm,),
        in_specs=[row(D), _const_spec((1, D)), _const_spec((D, FF)), _const_spec((D, FF)), _const_spec((FF, D))],
        out_specs=[row(D), row(FF), row(FF), row(D)],
        out_shape=[jax.ShapeDtypeStruct((t, D), F32), jax.ShapeDtypeStruct((t, FF), BF16),
                   jax.ShapeDtypeStruct((t, FF), BF16), jax.ShapeDtypeStruct((t, D), BF16)],
        compiler_params=_cparams(("parallel",)),
    )(xm, g2, wg, wu, wd)


def _loss_and_grad(y, tgt, tm):
    t = y.shape[0]

    def body(y_ref, t_ref, l_ref, dy_ref):
        @pl.when(pl.program_id(0) == 0)
        def _():
            l_ref[...] = jnp.zeros_like(l_ref)

        e = y_ref[...] - t_ref[...]
        dy_ref[...] = e * (1.0 / D)
        s = jnp.sum(jnp.sum(e * e, axis=-1, keepdims=True), axis=0, keepdims=True)
        l_ref[...] += s * (0.5 / D)

    row = pl.BlockSpec((tm, D), lambda i: (i, 0))
    return pl.pallas_call(
        body, name="loss", grid=(t // tm,), in_specs=[row, row],
        out_specs=[pl.BlockSpec((8, 128), lambda i: (0, 0)), row],
        out_shape=[jax.ShapeDtypeStruct((8, 128), F32), jax.ShapeDtypeStruct((t, D), F32)],
        compiler_params=_cparams(("arbitrary",)),
    )(y, tgt)


def _ffn_bwd(dy, xm, g2, a, b, wg, wu, wd, tm):
    t = dy.shape[0]

    def body(dy_ref, x_ref, g_ref, a_ref, b_ref, wg_ref, wu_ref, wd_ref, dx_ref, da_ref, db_ref, hm_ref, dg_ref):
        @pl.when(pl.program_id(0) == 0)
        def _():
            dg_ref[...] = jnp.zeros_like(dg_ref)

        dyv = dy_ref[...]
        dyb = dyv.astype(BF16)
        dh2 = jnp.zeros_like(dyv)
        for c0 in range(0, FF, FF_CHUNK):
            cols = slice(c0, c0 + FF_CHUNK)
            dhm = _nt(dyb, wd_ref[cols, :])
            av = a_ref[:, cols].astype(F32)
            bv = b_ref[:, cols].astype(F32)
            sig = jax.nn.sigmoid(av)
            sil = av * sig
            hm_ref[:, cols] = (sil * bv).astype(BF16)
            da = (dhm * bv * (sig * (1.0 + av * (1.0 - sig)))).astype(BF16)
            db = (dhm * sil).astype(BF16)
            da_ref[:, cols] = da
            db_ref[:, cols] = db
            dh2 = dh2 + _nt(da, wg_ref[:, cols]) + _nt(db, wu_ref[:, cols])
        r, xh = _rms_fwd(x_ref[...], 1.0 / D)
        dg_ref[...] += jnp.sum(dh2 * xh, axis=0, keepdims=True)
        dx_ref[...] = dyv + _rms_bwd(dh2, g_ref[...], xh, r, 1.0 / D)

    row = lambda w: pl.BlockSpec((tm, w), lambda i: (i, 0))
    return pl.pallas_call(
        body, name="ffn_bwd", grid=(t // tm,),
        in_specs=[row(D), row(D), _const_spec((1, D)), row(FF), row(FF),
                  _const_spec((D, FF)), _const_spec((D, FF)), _const_spec((FF, D))],
        out_specs=[row(D), row(FF), row(FF), row(FF), pl.BlockSpec((1, D), lambda i: (0, 0))],
        out_shape=[jax.ShapeDtypeStruct((t, D), F32), jax.ShapeDtypeStruct((t, FF), BF16),
                   jax.ShapeDtypeStruct((t, FF), BF16), jax.ShapeDtypeStruct((t, FF), BF16),
                   jax.ShapeDtypeStruct((1, D), F32)],
        compiler_params=_cparams(("arbitrary",)),
    )(dy, xm, g2, a, b, wg, wu, wd)


def _wgrad(a, b, tn, tt, name):
    t, k = a.shape
    n = b.shape[1]
    nsteps = t // tt

    def body(a_ref, b_ref, o_ref):
        @pl.when(pl.program_id(1) == 0)
        def _():
            o_ref[...] = jnp.zeros_like(o_ref)

        o_ref[...] += _tn(a_ref[...].astype(BF16), b_ref[...].astype(BF16))

    return pl.pallas_call(
        body, name=name, grid=(n // tn, nsteps),
        in_specs=[pl.BlockSpec((tt, k), lambda j, s: (s, 0)), pl.BlockSpec((tt, tn), lambda j, s: (s, j))],
        out_specs=pl.BlockSpec((k, tn), lambda j, s: (0, j)),
        out_shape=jax.ShapeDtypeStruct((k, n), F32),
        compiler_params=_cparams(("parallel", "arbitrary")),
    )(a, b)


def _mixer_bwd(dxm, proj, cw, gq, gk, sinks, gco, gao, wo, tq):
    t = proj.shape[0]
    nb = tq // BLK
    r8 = tq // 8
    nt = t // tq
    te = tq + 8
    kvw = 2 * NKV * HP

    def body(dx_ref, dxn_ref, p_ref, cgp_ref, hcp_ref, bgn_ref, cgn_ref, hcn_ref, kvp_ref, cw_ref, gq_ref,
             gk_ref, sk_ref, gco_ref, gao_ref, wo_ref,
             dpm_ref, dkvm_ref, dkvh_ref, dcw_ref, dgq_ref, dgk_ref, dsk_ref, dgco_ref, dgao_ref, acc_ref):
        i = pl.program_id(0)

        @pl.when(i == 0)
        def _():
            for r in (dcw_ref, dgq_ref, dgk_ref, dsk_ref, dgco_ref, dgao_ref):
                r[...] = jnp.zeros_like(r)

        acc_ref[...] = jnp.zeros_like(acc_ref)
        live_rows = jnp.where(i < nt - 1, te, tq)
        dxb = dx_ref[...].astype(BF16)
        dxe = jnp.concatenate([dxb, dxn_ref[...].astype(BF16)], axis=0)
        dcn = _nt(dxe, wo_ref[0:CC, :])
        bg = jnp.concatenate([p_ref[:, O_BG:O_BG + CC], bgn_ref[...]], axis=0)
        cg = jnp.concatenate([p_ref[:, O_CG:O_CG + CC], cgn_ref[...]], axis=0)
        hc = jnp.concatenate([p_ref[:, O_HC:O_HC + CC], hcn_ref[...]], axis=0)
        u = cg * hc
        up = jnp.where(i > 0, cgp_ref[...] * hcp_ref[...], 0.0)
        u1, u2 = _conv_taps(jnp.concatenate([up, u], axis=0), te)
        w0, w1, w2 = cw_ref[0:1, :], cw_ref[1:2, :], cw_ref[2:3, :]
        y = w0 * u2 + w1 * u1 + w2 * u
        co = bg * y
        rc, coh = _rms_fwd(co, 1.0 / CC)
        dco = _rms_bwd(dcn, gco_ref[...], coh, rc, 1.0 / CC)
        row_io = lax.broadcasted_iota(jnp.int32, (te, 1), 0)
        own = row_io < tq
        dgco_ref[...] += jnp.sum(jnp.where(own, dcn * coh, 0.0), axis=0, keepdims=True)
        dyc = jnp.where(row_io < live_rows, dco * bg, 0.0)
        dyo = jnp.where(own, dyc, 0.0)
        dcw_ref[0:1, :] += jnp.sum(dyo * u2, axis=0, keepdims=True)
        dcw_ref[1:2, :] += jnp.sum(dyo * u1, axis=0, keepdims=True)
        dcw_ref[2:3, :] += jnp.sum(dyo * u, axis=0, keepdims=True)
        dy1 = pltpu.roll(dyc, te - 1, 0)[0:tq]
        dy2 = pltpu.roll(dyc, te - 2, 0)[0:tq]
        du = w2 * dyc[0:tq] + w1 * dy1 + w0 * dy2
        dpm_ref[:, O_BG:O_BG + CC] = (dco[0:tq] * y[0:tq]).astype(BF16)
        dpm_ref[:, O_CG:O_CG + CC] = (du * hc[0:tq]).astype(BF16)
        dpm_ref[:, O_HC:O_HC + CC] = (du * cg[0:tq]).astype(BF16)
        dan = _nt(dxb, wo_ref[CC:MIXW, :])
        kraw = jnp.concatenate([kvp_ref[:, 0:NKV * HP], p_ref[:, O_K:O_K + NKV * HP]], axis=0)
        vraw = jnp.concatenate([kvp_ref[:, NKV * HP:], p_ref[:, O_V:O_V + NKV * HP]], axis=0)
        gqv, gkv = gq_ref[...], gk_ref[...]
        keys = _norm_keys(kraw, gkv)
        vb = [vraw[:, h * HP:(h + 1) * HP].astype(BF16) for h in range(NKV)]
        base_valid, c_io = _band_mask()
        lane = lax.broadcasted_iota(jnp.int32, (1, HP), 1)
        for b in range(nb):
            lo = jnp.where(i * nb + b == 0, BLK, 0)
            valid = base_valid & (c_io >= lo)
            band = slice(b * BLK, b * BLK + 2 * BLK)
            qs, prs, pss, outs = [], [], [], []
            for g in range(NQ):
                h = g // GRP
                qg = p_ref[b * BLK:(b + 1) * BLK, O_Q + g * HP:O_Q + (g + 1) * HP]
                rq, qh = _rms_fwd(qg, 1.0 / HD)
                qn = (qh * gqv).astype(BF16)
                pr, ps = _attn_probs(qn, keys[h][2][band], sk_ref[0, g], valid)
                qs.append((rq, qh, qn))
                prs.append(pr)
                pss.append(ps)
                outs.append(jnp.dot(pr.astype(BF16), vb[h][band], preferred_element_type=F32))
            ao = jnp.concatenate(outs, axis=1)
            ra, aoh = _rms_fwd(ao, 1.0 / (NQ * HD))
            danb = dan[b * BLK:(b + 1) * BLK]
            dgao_ref[...] += jnp.sum(danb * aoh, axis=0, keepdims=True)
            dao = _rms_bwd(danb, gao_ref[...], aoh, ra, 1.0 / (NQ * HD))
            dqs = []
            for h in range(NKV):
                dss, dobs = [], []
                for g in range(h * GRP, (h + 1) * GRP):
                    rq, qh, qn = qs[g]
                    dob = dao[:, g * HP:(g + 1) * HP].astype(BF16)
                    dp = _nt(dob, vb[h][band])
                    delta = jnp.sum(prs[g] * dp, axis=-1, keepdims=True)
                    dsb = (prs[g] * (dp - delta) * SCALE).astype(BF16)
                    dsk = -jnp.sum(pss[g] * delta, axis=0, keepdims=True)
                    dsk_ref[...] += jnp.where(lane == g, dsk, 0.0)
                    dqn = jnp.dot(dsb, keys[h][2][band], preferred_element_type=F32)
                    dgq_ref[...] += jnp.sum(dqn * qh, axis=0, keepdims=True)
                    dqs.append(_rms_bwd(dqn, gqv, qh, rq, 1.0 / HD).astype(BF16))
                    dss.append(dsb)
                    dobs.append(dob)
                grp = slice(h * GRP, (h + 1) * GRP)
                dkn = _tn(jnp.concatenate(dss, axis=0), jnp.concatenate([q[2] for q in qs[grp]], axis=0))
                dv = _tn(jnp.concatenate([p.astype(BF16) for p in prs[grp]], axis=0),
                         jnp.concatenate(dobs, axis=0))
                khat, rk = keys[h][0][band], keys[h][1][band]
                dgk_ref[...] += jnp.sum(dkn * khat, axis=0, keepdims=True)
                acc_ref[band, h * HP:(h + 1) * HP] += _rms_bwd(dkn, gkv, khat, rk, 1.0 / HD)
                acc_ref[band, (NKV + h) * HP:(NKV + h + 1) * HP] += dv
            dpm_ref[b * BLK:(b + 1) * BLK, O_Q:O_K] = jnp.concatenate(dqs, axis=1)
        dkvh_ref[...] = acc_ref[0:BLK, :]
        dkvm_ref[...] = acc_ref[BLK:, :]

    prev8 = lambda col: pl.BlockSpec((8, CC), lambda i: (jnp.maximum(i * r8 - 1, 0), col))
    next8 = lambda col: pl.BlockSpec((8, CC), lambda i: (jnp.minimum((i + 1) * r8, t // 8 - 1), col))
    small = lambda n: pl.BlockSpec((1, n), lambda i: (0, 0))
    return pl.pallas_call(
        body, name="mixer_bwd", grid=(nt,),
        in_specs=[
            pl.BlockSpec((tq, D), lambda i: (i, 0)),
            pl.BlockSpec((8, D), lambda i: (jnp.minimum((i + 1) * r8, t // 8 - 1), 0)),
            pl.BlockSpec((tq, NP), lambda i: (i, 0)),
            prev8(O_CG // CC), prev8(O_HC // CC),
            next8(O_BG // CC), next8(O_CG // CC), next8(O_HC // CC),
            pl.BlockSpec((BLK, kvw), lambda i: (jnp.maximum(i * nb - 1, 0), O_K // kvw)),
            _const_spec((8, CC)), _const_spec((1, HP)), _const_spec((1, HP)),
            pl.BlockSpec(memory_space=pltpu.SMEM),
            _const_spec((1, CC)), _const_spec((1, NQ * HP)), _const_spec((MIXW, D)),
        ],
        out_specs=[
            pl.BlockSpec((tq, NMAIN), lambda i: (i, 0)),
            pl.BlockSpec((tq, kvw), lambda i: (i, 0)),
            pl.BlockSpec((BLK, kvw), lambda i: (i, 0)),
            pl.BlockSpec((8, CC), lambda i: (0, 0)), small(HP), small(HP), small(HP), small(CC), small(NQ * HP),
        ],
        out_shape=[
            jax.ShapeDtypeStruct((t, NMAIN), BF16), jax.ShapeDtypeStruct((t, kvw), F32),
            jax.ShapeDtypeStruct((nt * BLK, kvw), F32),
            jax.ShapeDtypeStruct((8, CC), F32), jax.ShapeDtypeStruct((1, HP), F32), jax.ShapeDtypeStruct((1, HP), F32),
            jax.ShapeDtypeStruct((1, HP), F32), jax.ShapeDtypeStruct((1, CC), F32),
            jax.ShapeDtypeStruct((1, NQ * HP), F32),
        ],
        scratch_shapes=[pltpu.VMEM((tq + BLK, kvw), F32)],
        compiler_params=_cparams(("arbitrary",)),
    )(dxm, dxm, proj, proj, proj, proj, proj, proj, proj, cw, gq, gk, sinks, gco, gao, wo)


def _inproj_bwd(dpm, dkv, wp, x, g1, dxm, tm):
    t = x.shape[0]
    kvw = 2 * NKV * HP

    def body(dp_ref, dk_ref, w_ref, x_ref, g_ref, dxm_ref, dx_ref, dg_ref):
        @pl.when(pl.program_id(0) == 0)
        def _():
            dg_ref[...] = jnp.zeros_like(dg_ref)

        dh = _nt(dp_ref[...], w_ref[:, 0:NMAIN]) + _nt(dk_ref[...], w_ref[:, NMAIN:NP])
        r, xh = _rms_fwd(x_ref[...], 1.0 / D)
        dg_ref[...] += jnp.sum(dh * xh, axis=0, keepdims=True)
        dx_ref[...] = dxm_ref[...] + _rms_bwd(dh, g_ref[...], xh, r, 1.0 / D)

    row = lambda w: pl.BlockSpec((tm, w), lambda i: (i, 0))
    return pl.pallas_call(
        body, name="inproj_bwd", grid=(t // tm,),
        in_specs=[row(NMAIN), row(kvw), _const_spec((D, NP)), row(D), _const_spec((1, D)), row(D)],
        out_specs=[row(D), pl.BlockSpec((1, D), lambda i: (0, 0))],
        out_shape=[jax.ShapeDtypeStruct((t, D), F32), jax.ShapeDtypeStruct((1, D), F32)],
        compiler_params=_cparams(("arbitrary",)),
    )(dpm, dkv, wp, x, g1, dxm)


def _rows_tile(rows):
    for cand in (512, 256, 128, 64, 32, 16, 8):
        if rows % cand == 0:
            return cand
    return rows


def _add_pairs(xs, ys, name):
    outs = []
    for n, (xa, ya) in enumerate(zip(xs, ys)):
        rows, cols = xa.shape
        tr = _rows_tile(rows)

        def body(x_ref, y_ref, o_ref):
            o_ref[...] = (x_ref[...].astype(F32) + y_ref[...].astype(F32)).astype(BF16)

        spec = pl.BlockSpec((tr, cols), lambda i: (i, 0))
        outs.append(pl.pallas_call(
            body, name=f"{name}_{n}", grid=(rows // tr,), in_specs=[spec, spec], out_specs=spec,
            out_shape=jax.ShapeDtypeStruct((rows, cols), BF16), compiler_params=_cparams(("parallel",)),
        )(xa, ya))
    return outs


def _sum_chips(cs, name):
    outs = []
    for n, ca in enumerate(cs):
        _, rows, cols = ca.shape
        tr = _rows_tile(rows)

        def body(c_ref, o_ref):
            acc = c_ref[0].astype(F32)
            for j in range(1, N_CHIPS):
                acc = acc + c_ref[j].astype(F32)
            o_ref[...] = acc

        outs.append(pl.pallas_call(
            body, name=f"{name}_{n}", grid=(rows // tr,),
            in_specs=[pl.BlockSpec((N_CHIPS, tr, cols), lambda i: (0, i, 0))],
            out_specs=pl.BlockSpec((tr, cols), lambda i: (i, 0)),
            out_shape=jax.ShapeDtypeStruct((rows, cols), F32), compiler_params=_cparams(("parallel",)),
        )(ca))
    return outs


def _adamw(w, g, m, v, name):
    rows, cols = w.shape
    tr = _rows_tile(rows)
    c1 = 1.0 - ADAM_B1 ** ADAM_STEP
    c2 = 1.0 - ADAM_B2 ** ADAM_STEP

    def body(w_ref, g_ref, m_ref, v_ref, d_ref, mo_ref, vo_ref):
        gv = g_ref[...]
        mn = ADAM_B1 * m_ref[...] + (1.0 - ADAM_B1) * gv
        vn = ADAM_B2 * v_ref[...] + (1.0 - ADAM_B2) * (gv * gv)
        mo_ref[...] = mn
        vo_ref[...] = vn
        d_ref[...] = -ADAM_LR * ((mn / c1) / (jnp.sqrt(vn / c2) + ADAM_EPS) + ADAM_WD * w_ref[...])

    spec = pl.BlockSpec((tr, cols), lambda i: (i, 0))
    sds = jax.ShapeDtypeStruct((rows, cols), F32)
    return pl.pallas_call(
        body, name=name, grid=(rows // tr,), in_specs=[spec] * 4, out_specs=[spec] * 3, out_shape=[sds] * 3,
        compiler_params=_cparams(("parallel",)),
    )(w, g, m, v)


def _place():
    x, y, c = lax.axis_index("x"), lax.axis_index("y"), lax.axis_index("c")
    chips = [(1 - x, y), (x, 1 - y), (1 - x, 1 - y)]
    return x, y, c, chips


ANY = pl.BlockSpec(memory_space=pl.ANY)
DMA_ROWS = 64


def _pieces(shape):
    rows = shape[-2]
    step = DMA_ROWS if rows % DMA_ROWS == 0 else rows
    lead = [()]
    for n in shape[:-2]:
        lead = [i + (k,) for i in lead for k in range(n)]
    return [i + (pl.ds(r0, step),) for i in lead for r0 in range(0, rows, step)]


def _start_pieces(make, src, dst):
    for idx in _pieces(src.shape):
        make(src.at[idx], dst.at[idx]).start()


def _gather_weights(shards, cwp):
    nw = len(shards)

    def body(*refs):
        srcs, cw_ref = refs[:nw], refs[nw]
        outs, ocw_ref = refs[nw + 1:2 * nw + 1], refs[2 * nw + 1]
        ssem, rsem, fssem, frsem, lsem = refs[2 * nw + 2:]
        x, y, c, chips = _place()
        kme = 2 * x + y
        sib = (x, y, 1 - c)

        def mine(w):
            return (cw_ref, ocw_ref.at[kme]) if w == nw else (srcs[w], outs[w].at[kme])

        def keep(w):
            return lambda s, d: pltpu.make_async_copy(s, d, lsem.at[w])

        def landing(w, block_chip, layer):
            return ocw_ref.at[block_chip] if w == nw else outs[w].at[block_chip, layer]

        def plane(j, w, to):
            return lambda s, d: pltpu.make_async_remote_copy(
                src_ref=s, dst_ref=d, send_sem=ssem.at[j, w], recv_sem=rsem.at[j, w], device_id=to,
                device_id_type=MESH)

        def passed(j, w):
            return lambda s, d: pltpu.make_async_remote_copy(
                src_ref=s, dst_ref=d, send_sem=fssem.at[j, w], recv_sem=frsem.at[j, w], device_id=sib,
                device_id_type=MESH)

        for w in range(nw + 1):
            _start_pieces(keep(w), *mine(w))
        for j, (px, py) in enumerate(chips):
            for w in range(nw + 1):
                _start_pieces(plane(j, w, (px, py, c)), cw_ref if w == nw else srcs[w].at[c], landing(w, kme, c))
        for j, (px, py) in enumerate(chips):
            for w in range(nw + 1):
                got = landing(w, 2 * px + py, c)
                plane(j, w, (px, py, c))(got, got).wait_recv()
                if w < nw:
                    _start_pieces(passed(j, w), got, got)
        for j, (px, py) in enumerate(chips):
            for w in range(nw):
                got = landing(w, 2 * px + py, 1 - c)
                passed(j, w)(got, got).wait_recv()
        for j, (px, py) in enumerate(chips):
            for w in range(nw + 1):
                sent = landing(w, kme, c)
                plane(j, w, (px, py, c))(sent, sent).wait_send()
                if w < nw:
                    fwd = landing(w, 2 * px + py, c)
                    passed(j, w)(fwd, fwd).wait_send()
        for w in range(nw + 1):
            keep(w)(*mine(w)).wait()

    out_shape = [jax.ShapeDtypeStruct((N_CHIPS,) + s.shape, s.dtype) for s in shards]
    out_shape.append(jax.ShapeDtypeStruct((N_CHIPS,) + cwp.shape, cwp.dtype))
    return pl.pallas_call(
        body, name="gather_weights", in_specs=[ANY] * (nw + 1), out_specs=[ANY] * (nw + 1), out_shape=out_shape,
        scratch_shapes=[pltpu.SemaphoreType.DMA((3, nw + 1)), pltpu.SemaphoreType.DMA((3, nw + 1)),
                        pltpu.SemaphoreType.DMA((3, nw)), pltpu.SemaphoreType.DMA((3, nw)),
                        pltpu.SemaphoreType.DMA((nw + 1,))],
        compiler_params=_cparams(has_side_effects=True),
    )(*shards, cwp)


def _swap_layers(gs):
    nw = len(gs)

    def body(*refs):
        srcs = refs[:nw]
        mine, theirs = refs[nw:2 * nw], refs[2 * nw:3 * nw]
        ssem, rsem, lsem = refs[3 * nw:]
        x, y, c, _ = _place()

        def keep(w):
            return lambda s, d: pltpu.make_async_copy(s, d, lsem.at[w])

        def give(w):
            return lambda s, d: pltpu.make_async_remote_copy(
                src_ref=s, dst_ref=d, send_sem=ssem.at[w], recv_sem=rsem.at[w], device_id=(x, y, 1 - c),
                device_id_type=MESH)

        for w in range(nw):
            _start_pieces(keep(w), srcs[w].at[c], mine[w])
            _start_pieces(give(w), srcs[w].at[1 - c], theirs[w])
        for w in range(nw):
            give(w)(srcs[w].at[1 - c], theirs[w]).wait()
            keep(w)(srcs[w].at[c], mine[w]).wait()

    half = [jax.ShapeDtypeStruct(g.shape[1:], g.dtype) for g in gs]
    res = pl.pallas_call(
        body, name="swap_layers", in_specs=[ANY] * nw, out_specs=[ANY] * (2 * nw), out_shape=half + half,
        scratch_shapes=[pltpu.SemaphoreType.DMA((nw,))] * 3,
        compiler_params=_cparams(has_side_effects=True),
    )(*gs)
    return res[:nw], res[nw:]


def _scatter_chips(ps):
    nw = len(ps)

    def body(*refs):
        srcs, outs = refs[:nw], refs[nw:2 * nw]
        ssem, rsem, lsem = refs[2 * nw:]
        x, y, c, chips = _place()
        kme = 2 * x + y

        def keep(w):
            return lambda s, d: pltpu.make_async_copy(s, d, lsem.at[w])

        def give(j, w, to):
            return lambda s, d: pltpu.make_async_remote_copy(
                src_ref=s, dst_ref=d, send_sem=ssem.at[j, w], recv_sem=rsem.at[j, w], device_id=to,
                device_id_type=MESH)

        for w in range(nw):
            _start_pieces(keep(w), srcs[w].at[kme], outs[w].at[kme])
        for j, (px, py) in enumerate(chips):
            for w in range(nw):
                _start_pieces(give(j, w, (px, py, c)), srcs[w].at[2 * px + py], outs[w].at[kme])
        for j, (px, py) in enumerate(chips):
            for w in range(nw):
                got = outs[w].at[2 * px + py]
                give(j, w, (px, py, c))(got, got).wait_recv()
        for j, (px, py) in enumerate(chips):
            for w in range(nw):
                sent = srcs[w].at[2 * px + py]
                give(j, w, (px, py, c))(sent, sent).wait_send()
        for w in range(nw):
            keep(w)(srcs[w].at[kme], outs[w].at[kme]).wait()

    return pl.pallas_call(
        body, name="scatter_chips", in_specs=[ANY] * nw, out_specs=[ANY] * nw,
        out_shape=[jax.ShapeDtypeStruct(p.shape, p.dtype) for p in ps],
        scratch_shapes=[pltpu.SemaphoreType.DMA((3, nw)), pltpu.SemaphoreType.DMA((3, nw)),
                        pltpu.SemaphoreType.DMA((nw,))],
        compiler_params=_cparams(has_side_effects=True),
    )(*ps)


def _join_layers(rs):
    nw = len(rs)

    def body(*refs):
        srcs, outs = refs[:nw], refs[nw:2 * nw]
        ssem, rsem, lsem = refs[2 * nw:]
        x, y, c, _ = _place()

        def keep(w):
            return lambda s, d: pltpu.make_async_copy(s, d, lsem.at[w])

        def give(w):
            return lambda s, d: pltpu.make_async_remote_copy(
                src_ref=s, dst_ref=d, send_sem=ssem.at[w], recv_sem=rsem.at[w], device_id=(x, y, 1 - c),
                device_id_type=MESH)

        for w in range(nw):
            _start_pieces(keep(w), srcs[w], outs[w].at[c])
            _start_pieces(give(w), srcs[w], outs[w].at[c])
        for w in range(nw):
            give(w)(srcs[w], outs[w].at[1 - c]).wait()
            keep(w)(srcs[w], outs[w].at[c]).wait()

    return pl.pallas_call(
        body, name="join_layers", in_specs=[ANY] * nw, out_specs=[ANY] * nw,
        out_shape=[jax.ShapeDtypeStruct((2,) + r.shape, r.dtype) for r in rs],
        scratch_shapes=[pltpu.SemaphoreType.DMA((nw,))] * 3,
        compiler_params=_cparams(has_side_effects=True),
    )(*rs)


def _allreduce_small(v):
    rows = v.shape[0]

    def body(v_ref, o_ref, buf, ssem, rsem):
        x, y, c, _ = _place()
        me = 4 * x + 2 * y + c
        buf[me] = v_ref[...]
        sends = []
        for r in range(1, 8):
            peer = (x ^ (r >> 2), y ^ ((r >> 1) & 1), c ^ (r & 1))
            cp = pltpu.make_async_remote_copy(
                src_ref=v_ref, dst_ref=buf.at[me], send_sem=ssem.at[r - 1], recv_sem=rsem.at[r - 1],
                device_id=peer, device_id_type=MESH)
            cp.start()
            sends.append(cp)
        for r in range(1, 8):
            src = me ^ r
            pltpu.make_async_remote_copy(
                src_ref=v_ref, dst_ref=buf.at[src], send_sem=ssem.at[r - 1], recv_sem=rsem.at[r - 1],
                device_id=(x, y, c), device_id_type=MESH).wait_recv()
        for cp in sends:
            cp.wait_send()
        acc = buf[0]
        for d in range(1, 8):
            acc = acc + buf[d]
        o_ref[...] = acc

    vm = pl.BlockSpec(memory_space=pltpu.VMEM)
    return pl.pallas_call(
        body, name="allreduce_small", in_specs=[vm], out_specs=vm,
        out_shape=jax.ShapeDtypeStruct(v.shape, F32),
        scratch_shapes=[pltpu.VMEM((8, rows, 128), F32), pltpu.SemaphoreType.DMA((7,)),
                        pltpu.SemaphoreType.DMA((7,))],
        compiler_params=_cparams(has_side_effects=True),
    )(v)


def _pad_heads(w, n_heads, axis):
    shp = w.shape
    w = w.reshape(shp[:axis] + (n_heads, HD) + shp[axis + 1:])
    pad = [(0, 0)] * w.ndim
    pad[axis + 1] = (0, HP - HD)
    w = jnp.pad(w, pad)
    return w.reshape(shp[:axis] + (n_heads * HP,) + shp[axis + 1:])


def _strip_heads(w, n_heads, axis):
    shp = w.shape
    w = w.reshape(shp[:axis] + (n_heads, HP) + shp[axis + 1:])
    w = lax.slice_in_dim(w, 0, HD, axis=axis + 1)
    return w.reshape(shp[:axis] + (n_heads * HD,) + shp[axis + 1:])


def _unshard_cols(g4):
    k, r, c = g4.shape
    return jnp.transpose(g4, (1, 0, 2)).reshape(r, k * c)


def _shard_cols(w):
    r, n = w.shape
    return jnp.transpose(w.reshape(r, N_CHIPS, n // N_CHIPS), (1, 0, 2))


def _pad_win(win):
    parts = [win[:, :3 * CC], _pad_heads(win[:, 3 * CC:3 * CC + NQ * HD], NQ, 1),
             _pad_heads(win[:, 3 * CC + NQ * HD:3 * CC + (NQ + NKV) * HD], NKV, 1),
             _pad_heads(win[:, 3 * CC + (NQ + NKV) * HD:], NKV, 1)]
    return jnp.concatenate(parts, axis=1)


def _strip_win(gp):
    parts = [gp[:, :3 * CC], _strip_heads(gp[:, O_Q:O_K], NQ, 1), _strip_heads(gp[:, O_K:O_V], NKV, 1),
             _strip_heads(gp[:, O_V:], NKV, 1)]
    return jnp.concatenate(parts, axis=1)


def _count(shape):
    n = 1
    for s in shape:
        n *= s
    return n


def _pack_rows(arrs):
    flat = [jnp.pad(a.reshape(-1), (0, (-_count(a.shape)) % 128)) for a in arrs]
    v = jnp.concatenate(flat)
    rows = -(-v.shape[0] // (8 * 128)) * 8
    return jnp.pad(v, (0, rows * 128 - v.shape[0])).reshape(rows, 128)


def kernel(x, norm1_g, w_in, conv_w, q_norm_g, k_norm_g, sinks, conv_out_g, attn_out_g, w_o, norm2_g, w_gate, w_up, w_down, loss_target, m_norm1_g, m_w_in, m_conv_w, m_q_norm_g, m_k_norm_g, m_sinks, m_conv_out_g, m_attn_out_g, m_w_o, m_norm2_g, m_w_gate, m_w_up, m_w_down, v_norm1_g, v_w_in, v_conv_w, v_q_norm_g, v_k_norm_g, v_sinks, v_conv_out_g, v_attn_out_g, v_w_o, v_norm2_g, v_w_gate, v_w_up, v_w_down):
    depth = w_in.shape[0]
    t = x.shape[1]
    xs = x.reshape(t, D)
    tgt = loss_target.reshape(t, D)
    xi, yi = lax.axis_index("x"), lax.axis_index("y")
    kme = 2 * xi + yi
    tm = min(512, t)
    tq = min(256, t)
    tf = min(256, t)

    cwp = jnp.pad(conv_w.reshape(depth * 3, CC // N_CHIPS), ((0, 8 - depth * 3), (0, 0)))
    big = [w_in, w_o, w_gate, w_up, w_down]
    gathered = _gather_weights([w.astype(BF16) for w in big], cwp)
    a_in, a_o, a_g, a_u, a_d, a_cw = gathered
    cw_full = _unshard_cols(a_cw)[:depth * 3].reshape(depth, 3, CC)
    layers = []
    for l in range(depth):
        wp = _pad_win(_unshard_cols(a_in[:, l]))
        wo = a_o[:, l].reshape(D, D)
        wo = jnp.concatenate([wo[:CC], _pad_heads(wo[CC:], NQ, 0)], axis=0)
        layers.append(dict(
            wp=wp, wo=wo, wg=_unshard_cols(a_g[:, l]), wu=_unshard_cols(a_u[:, l]), wd=a_d[:, l].reshape(FF, D),
            cw=jnp.pad(cw_full[l], ((0, 5), (0, 0))),
            g1=norm1_g[l].reshape(1, D), g2=norm2_g[l].reshape(1, D),
            gq=jnp.pad(q_norm_g[l], (0, HP - HD)).reshape(1, HP), gk=jnp.pad(k_norm_g[l], (0, HP - HD)).reshape(1, HP),
            sk=sinks[l].reshape(1, NQ), gco=conv_out_g[l].reshape(1, CC),
            gao=_pad_heads(attn_out_g[l], NQ, 0).reshape(1, NQ * HP)))

    saved = []
    cur = xs
    for l in range(depth):
        p = layers[l]
        proj, h = _inproj_fwd(cur, p["g1"], p["wp"], tm)
        xm, mix = _mixer_fwd(proj, cur, p["cw"], p["gq"], p["gk"], p["sk"], p["gco"], p["gao"], p["wo"], tq)
        xo, a, b, h2 = _ffn_fwd(xm, p["g2"], p["wg"], p["wu"], p["wd"], tf)
        saved.append(dict(x=cur, proj=proj, h=h, xm=xm, mix=mix, a=a, b=b, h2=h2))
        cur = xo
    lpart, dy = _loss_and_grad(cur, tgt, tm)
    loss = lax.psum(lpart[0, 0], ("x", "y", "c"))

    nt = t // tq
    gbig = [None] * depth
    gsmall = [None] * depth
    for l in reversed(range(depth)):
        p, s = layers[l], saved[l]
        dxm, da, db, hm, dg2 = _ffn_bwd(dy, s["xm"], p["g2"], s["a"], s["b"], p["wg"], p["wu"], p["wd"], tf)
        g_wg = _wgrad(s["h2"], da, FF // 2, tm, "wgrad_gate")
        g_wu = _wgrad(s["h2"], db, FF // 2, tm, "wgrad_up")
        g_wdt = _wgrad(dy, hm, FF // 2, tm, "wgrad_down")
        dpm, dkvm, dkvh, dcw, dgq, dgk, dsk, dgco, dgao = _mixer_bwd(
            dxm, s["proj"], p["cw"], p["gq"], p["gk"], p["sk"], p["gco"], p["gao"], p["wo"], tq)
        g_wot = _wgrad(dxm, s["mix"], MIXW, tm, "wgrad_o")
        kvw = dkvm.shape[1]
        halo = jnp.concatenate([dkvh.reshape(nt, BLK, kvw)[1:], jnp.zeros((1, BLK, kvw), F32)], axis=0)
        halo = jnp.pad(halo, ((0, 0), (tq - BLK, 0), (0, 0)))
        dkv = (dkvm.reshape(nt, tq, kvw) + halo).reshape(t, kvw).astype(BF16)
        dx, dg1 = _inproj_bwd(dpm, dkv, p["wp"], s["x"], p["g1"], dxm, tm)
        g_wpm = _wgrad(s["h"], dpm, NMAIN // 2, tm, "wgrad_in_main")
        g_wpk = _wgrad(s["h"], dkv, kvw, tm, "wgrad_in_kv")
        dy = dx
        g_in = _strip_win(jnp.concatenate([g_wpm, g_wpk], axis=1))
        g_ot = jnp.concatenate([g_wot[:, :CC], _strip_heads(g_wot[:, CC:], NQ, 1)], axis=1)
        gbig[l] = [_shard_cols(g.astype(BF16)) for g in (g_in, g_ot, g_wg, g_wu, g_wdt)]
        gsmall[l] = dict(g1=dg1, cw=dcw[:3], gq=dgq[0, :HD], gk=dgk[0, :HD], sk=dsk[0, :NQ], gco=dgco,
                         gao=_strip_heads(dgao.reshape(NQ * HP), NQ, 0), g2=dg2)
    grad_x = dy.reshape(x.shape)

    nw = len(big)
    gs = [jnp.stack([gbig[l][w] for l in range(depth)]) for w in range(nw)]
    mine, theirs = _swap_layers(gs)
    flat = lambda a4: a4.reshape(a4.shape[0] * a4.shape[1], a4.shape[2])
    ps = _add_pairs([flat(a4) for a4 in mine], [flat(a4) for a4 in theirs], "presum")
    cs = _scatter_chips([q.reshape(m4.shape) for q, m4 in zip(ps, mine)])
    rs = _join_layers(_sum_chips(cs, "chipsum"))
    r_in, r_ot, r_g, r_u, r_dt = rs
    g_big = [r_in, jnp.transpose(r_ot, (0, 2, 1)), r_g, r_u, jnp.transpose(r_dt, (0, 2, 1))]

    small_shapes = dict(g1=(D,), cw=(3, CC), gq=(HD,), gk=(HD,), sk=(NQ,), gco=(CC,), gao=(NQ * HD,), g2=(D,))
    red = _allreduce_small(_pack_rows([gsmall[l][n] for l in range(depth) for n in small_shapes])).reshape(-1)
    red_small, offs = {n: [] for n in small_shapes}, 0
    for l in range(depth):
        for n, shp in small_shapes.items():
            cnt = _count(shp)
            red_small[n].append(red[offs:offs + cnt].reshape(shp))
            offs += -(-cnt // 128) * 128
    g_small = {n: jnp.stack(v) for n, v in red_small.items()}
    g_cw = lax.dynamic_slice_in_dim(g_small["cw"], kme * (CC // N_CHIPS), CC // N_CHIPS, axis=2)

    weights = [norm1_g, w_in, conv_w, q_norm_g, k_norm_g, sinks, conv_out_g, attn_out_g, w_o, norm2_g, w_gate,
               w_up, w_down]
    moms = [m_norm1_g, m_w_in, m_conv_w, m_q_norm_g, m_k_norm_g, m_sinks, m_conv_out_g, m_attn_out_g, m_w_o,
            m_norm2_g, m_w_gate, m_w_up, m_w_down]
    vars_ = [v_norm1_g, v_w_in, v_conv_w, v_q_norm_g, v_k_norm_g, v_sinks, v_conv_out_g, v_attn_out_g, v_w_o,
             v_norm2_g, v_w_gate, v_w_up, v_w_down]
    grads = [g_small["g1"], g_big[0], g_cw, g_small["gq"], g_small["gk"], g_small["sk"], g_small["gco"],
             g_small["gao"], g_big[1], g_small["g2"], g_big[2], g_big[3], g_big[4]]
    n_w = len(weights)
    big_idx = [1, 8, 10, 11, 12]
    small_idx = [n for n in range(n_w) if n not in big_idx]
    deltas, new_m, new_v = [None] * n_w, [None] * n_w, [None] * n_w
    for n in big_idx:
        shp = weights[n].shape
        two = [a3.reshape(shp[0] * shp[1], shp[2]) for a3 in (weights[n], grads[n], moms[n], vars_[n])]
        res = _adamw(*two, f"adamw_{n}")
        deltas[n], new_m[n], new_v[n] = [r.reshape(shp) for r in res]
    res = _adamw(*[_pack_rows([arrs[n] for n in small_idx]) for arrs in (weights, grads, moms, vars_)],
                 "adamw_small")
    offs = 0
    for n in small_idx:
        shp = weights[n].shape
        cnt = _count(shp)
        deltas[n], new_m[n], new_v[n] = [r.reshape(-1)[offs:offs + cnt].reshape(shp) for r in res]
        offs += -(-cnt // 128) * 128
    return (loss, grad_x, *grads, *deltas, *new_m, *new_v)
```

```python
import functools

import jax
import jax.numpy as jnp
from jax import lax
from jax.experimental import pallas as pl
from jax.experimental.pallas import tpu as pltpu

F32 = jnp.float32
BF16 = jnp.bfloat16

D = 1024
CC = 512
NQ = 8
NKV = 2
HD = 64
HP = 128
GRP = NQ // NKV
FF = 2816
FF_CHUNK = FF // 2
BLK = 128
EPS = 1e-6
NEG = -1e30
SCALE = HD ** -0.5
O_BG, O_CG, O_HC, O_Q = 0, CC, 2 * CC, 3 * CC
O_K = O_Q + NQ * HP
O_V = O_K + NKV * HP
NP = O_V + NKV * HP
NMAIN = O_K
MIXW = CC + NQ * HP
N_CHIPS = 4
VMEM_LIMIT = 56 * 1024 * 1024
MESH = pl.DeviceIdType.MESH

ADAM_LR, ADAM_B1, ADAM_B2, ADAM_EPS, ADAM_WD, ADAM_STEP = 0.001, 0.9, 0.999, 1e-08, 0.01, 10


def _cparams(sem=None, **kw):
    if sem is not None:
        kw["dimension_semantics"] = sem
    return pltpu.CompilerParams(vmem_limit_bytes=VMEM_LIMIT, **kw)


def _const_spec(shape):
    nd = len(shape)
    return pl.BlockSpec(shape, lambda *_: (0,) * nd, pipeline_mode=pl.Buffered(1))


def _nt(a, b):
    return lax.dot_general(a, b, (((1,), (1,)), ((), ())), preferred_element_type=F32)


def _tn(a, b):
    return lax.dot_general(a, b, (((0,), (0,)), ((), ())), preferred_element_type=F32)


def _rms_fwd(x, inv_n):
    r = lax.rsqrt(jnp.sum(x * x, axis=-1, keepdims=True) * inv_n + EPS)
    return r, x * r


def _rms_bwd(dy, g, xh, r, inv_n):
    dxh = dy * g
    return r * (dxh - xh * (jnp.sum(dxh * xh, axis=-1, keepdims=True) * inv_n))


def _inproj_fwd(x, g1, wp, tm):
    t = x.shape[0]

    def body(x_ref, g_ref, w_ref, p_ref, h_ref):
        _, xh = _rms_fwd(x_ref[...], 1.0 / D)
        h = (xh * g_ref[...]).astype(BF16)
        h_ref[...] = h
        p_ref[...] = jnp.dot(h, w_ref[...], preferred_element_type=F32)

    return pl.pallas_call(
        body, name="inproj_fwd", grid=(t // tm,),
        in_specs=[pl.BlockSpec((tm, D), lambda i: (i, 0)), _const_spec((1, D)), _const_spec((D, NP))],
        out_specs=[pl.BlockSpec((tm, NP), lambda i: (i, 0)), pl.BlockSpec((tm, D), lambda i: (i, 0))],
        out_shape=[jax.ShapeDtypeStruct((t, NP), F32), jax.ShapeDtypeStruct((t, D), BF16)],
        compiler_params=_cparams(("parallel",)),
    )(x, g1, wp)


def _band_mask():
    r_io = lax.broadcasted_iota(jnp.int32, (BLK, 2 * BLK), 0)
    c_io = lax.broadcasted_iota(jnp.int32, (BLK, 2 * BLK), 1)
    return (c_io > r_io) & (c_io <= r_io + BLK), c_io


def _conv_taps(uf, n):
    u1 = pltpu.roll(uf, 1, 0)[8:8 + n]
    u2 = pltpu.roll(uf, 2, 0)[8:8 + n]
    return u1, u2


def _attn_probs(qn, kband, sink, valid):
    s = _nt(qn, kband) * SCALE
    s = jnp.where(valid, s, NEG)
    m = jnp.maximum(jnp.max(s, axis=-1, keepdims=True), sink)
    p = jnp.exp(s - m)
    es = jnp.exp(sink - m)
    inv = 1.0 / (jnp.sum(p, axis=-1, keepdims=True) + es)
    return p * inv, es * inv


def _norm_keys(kraw, gk):
    out = []
    for h in range(NKV):
        kh = kraw[:, h * HP:(h + 1) * HP]
        rk, khat = _rms_fwd(kh, 1.0 / HD)
        out.append((khat, rk, (khat * gk).astype(BF16)))
    return out


def _mixer_fwd(proj, x, cw, gq, gk, sinks, gco, gao, wo, tq):
    t = proj.shape[0]
    nb = tq // BLK
    r8 = tq // 8

    def body(p_ref, cgp_ref, hcp_ref, kvp_ref, x_ref, cw_ref, gq_ref, gk_ref, sk_ref, gco_ref, gao_ref,
             wo_ref, xm_ref, mix_ref):
        i = pl.program_id(0)
        cg = p_ref[:, O_CG:O_CG + CC]
        hc = p_ref[:, O_HC:O_HC + CC]
        u = cg * hc
        up = jnp.where(i > 0, cgp_ref[...] * hcp_ref[...], 0.0)
        u1, u2 = _conv_taps(jnp.concatenate([up, u], axis=0), tq)
        y = cw_ref[0:1, :] * u2 + cw_ref[1:2, :] * u1 + cw_ref[2:3, :] * u
        co = p_ref[:, O_BG:O_BG + CC] * y
        _, coh = _rms_fwd(co, 1.0 / CC)
        cn = coh * gco_ref[...]
        kraw = jnp.concatenate([kvp_ref[:, 0:NKV * HP], p_ref[:, O_K:O_K + NKV * HP]], axis=0)
        vraw = jnp.concatenate([kvp_ref[:, NKV * HP:], p_ref[:, O_V:O_V + NKV * HP]], axis=0)
        keys = _norm_keys(kraw, gk_ref[...])
        vb = [vraw[:, h * HP:(h + 1) * HP].astype(BF16) for h in range(NKV)]
        base_valid, c_io = _band_mask()
        rows = []
        for b in range(nb):
            lo = jnp.where(i * nb + b == 0, BLK, 0)
            valid = base_valid & (c_io >= lo)
            outs = []
            for g in range(NQ):
                h = g // GRP
                qg = p_ref[b * BLK:(b + 1) * BLK, O_Q + g * HP:O_Q + (g + 1) * HP]
                _, qh = _rms_fwd(qg, 1.0 / HD)
                qn = (qh * gq_ref[...]).astype(BF16)
                pr, _ = _attn_probs(qn, keys[h][2][b * BLK:b * BLK + 2 * BLK], sk_ref[0, g], valid)
                outs.append(jnp.dot(pr.astype(BF16), vb[h][b * BLK:b * BLK + 2 * BLK],
                                    preferred_element_type=F32))
            rows.append(jnp.concatenate(outs, axis=1))
        ao = jnp.concatenate(rows, axis=0)
        _, aoh = _rms_fwd(ao, 1.0 / (NQ * HD))
        an = aoh * gao_ref[...]
        mix = jnp.concatenate([cn, an], axis=1).astype(BF16)
        mix_ref[...] = mix
        xm_ref[...] = x_ref[...] + jnp.dot(mix, wo_ref[...], preferred_element_type=F32)

    prev8 = lambda col: pl.BlockSpec((8, CC), lambda i: (jnp.maximum(i * r8 - 1, 0), col))
    return pl.pallas_call(
        body, name="mixer_fwd", grid=(t // tq,),
        in_specs=[
            pl.BlockSpec((tq, NP), lambda i: (i, 0)),
            prev8(O_CG // CC), prev8(O_HC // CC),
            pl.BlockSpec((BLK, 2 * NKV * HP), lambda i: (jnp.maximum(i * nb - 1, 0), O_K // (2 * NKV * HP))),
            pl.BlockSpec((tq, D), lambda i: (i, 0)),
            _const_spec((8, CC)), _const_spec((1, HP)), _const_spec((1, HP)),
            pl.BlockSpec(memory_space=pltpu.SMEM),
            _const_spec((1, CC)), _const_spec((1, NQ * HP)), _const_spec((MIXW, D)),
        ],
        out_specs=[pl.BlockSpec((tq, D), lambda i: (i, 0)), pl.BlockSpec((tq, MIXW), lambda i: (i, 0))],
        out_shape=[jax.ShapeDtypeStruct((t, D), F32), jax.ShapeDtypeStruct((t, MIXW), BF16)],
        compiler_params=_cparams(("parallel",)),
    )(proj, proj, proj, proj, x, cw, gq, gk, sinks, gco, gao, wo)


def _ffn_fwd(xm, g2, wg, wu, wd, tm):
    t = xm.shape[0]

    def body(x_ref, g_ref, wg_ref, wu_ref, wd_ref, xo_ref, a_ref, b_ref, h2_ref):
        xv = x_ref[...]
        _, xh = _rms_fwd(xv, 1.0 / D)
        h2 = (xh * g_ref[...]).astype(BF16)
        h2_ref[...] = h2
        acc = xv
        for c0 in range(0, FF, FF_CHUNK):
            cols = slice(c0, c0 + FF_CHUNK)
            a = jnp.dot(h2, wg_ref[:, cols], preferred_element_type=F32)
            b = jnp.dot(h2, wu_ref[:, cols], preferred_element_type=F32)
            a_ref[:, cols] = a.astype(BF16)
            b_ref[:, cols] = b.astype(BF16)
            hm = (a * jax.nn.sigmoid(a) * b).astype(BF16)
            acc = acc + jnp.dot(hm, wd_ref[cols, :], preferred_element_type=F32)
        xo_ref[...] = acc

    row = lambda w: pl.BlockSpec((tm, w), lambda i: (i, 0))
    return pl.pallas_call(
        body, name="ffn_fwd", grid=(t // tm,),
        in_specs=[row(D), _const_spec((1, D)), _const_spec((D, FF)), _const_spec((D, FF)), _const_spec((FF, D))],
        out_specs=[row(D), row(FF), row(FF), row(D)],
        out_shape=[jax.ShapeDtypeStruct((t, D), F32), jax.ShapeDtypeStruct((t, FF), BF16),
                   jax.ShapeDtypeStruct((t, FF), BF16), jax.ShapeDtypeStruct((t, D), BF16)],
        compiler_params=_cparams(("parallel",)),
    )(xm, g2, wg, wu, wd)


def _loss_and_grad(y, tgt, tm):
    t = y.shape[0]

    def body(y_ref, t_ref, l_ref, dy_ref):
        @pl.when(pl.program_id(0) == 0)
        def _():
            l_ref[...] = jnp.zeros_like(l_ref)

        e = y_ref[...] - t_ref[...]
        dy_ref[...] = e * (1.0 / D)
        s = jnp.sum(jnp.sum(e * e, axis=-1, keepdims=True), axis=0, keepdims=True)
        l_ref[...] += s * (0.5 / D)

    row = pl.BlockSpec((tm, D), lambda i: (i, 0))
    return pl.pallas_call(
        body, name="loss", grid=(t // tm,), in_specs=[row, row],
        out_specs=[pl.BlockSpec((8, 128), lambda i: (0, 0)), row],
        out_shape=[jax.ShapeDtypeStruct((8, 128), F32), jax.ShapeDtypeStruct((t, D), F32)],
        compiler_params=_cparams(("arbitrary",)),
    )(y, tgt)


def _ffn_bwd(dy, xm, g2, a, b, wg, wu, wd, tm):
    t = dy.shape[0]

    def body(dy_ref, x_ref, g_ref, a_ref, b_ref, wg_ref, wu_ref, wd_ref, dx_ref, da_ref, db_ref, hm_ref, dg_ref):
        @pl.when(pl.program_id(0) == 0)
        def _():
            dg_ref[...] = jnp.zeros_like(dg_ref)

        dyv = dy_ref[...]
        dyb = dyv.astype(BF16)
        dh2 = jnp.zeros_like(dyv)
        for c0 in range(0, FF, FF_CHUNK):
            cols = slice(c0, c0 + FF_CHUNK)
            dhm = _nt(dyb, wd_ref[cols, :])
            av = a_ref[:, cols].astype(F32)
            bv = b_ref[:, cols].astype(F32)
            sig = jax.nn.sigmoid(av)
            sil = av * sig
            hm_ref[:, cols] = (sil * bv).astype(BF16)
            da = (dhm * bv * (sig * (1.0 + av * (1.0 - sig)))).astype(BF16)
            db = (dhm * sil).astype(BF16)
            da_ref[:, cols] = da
            db_ref[:, cols] = db
            dh2 = dh2 + _nt(da, wg_ref[:, cols]) + _nt(db, wu_ref[:, cols])
        r, xh = _rms_fwd(x_ref[...], 1.0 / D)
        dg_ref[...] += jnp.sum(dh2 * xh, axis=0, keepdims=True)
        dx_ref[...] = dyv + _rms_bwd(dh2, g_ref[...], xh, r, 1.0 / D)

    row = lambda w: pl.BlockSpec((tm, w), lambda i: (i, 0))
    return pl.pallas_call(
        body, name="ffn_bwd", grid=(t // tm,),
        in_specs=[row(D), row(D), _const_spec((1, D)), row(FF), row(FF),
                  _const_spec((D, FF)), _const_spec((D, FF)), _const_spec((FF, D))],
        out_specs=[row(D), row(FF), row(FF), row(FF), pl.BlockSpec((1, D), lambda i: (0, 0))],
        out_shape=[jax.ShapeDtypeStruct((t, D), F32), jax.ShapeDtypeStruct((t, FF), BF16),
                   jax.ShapeDtypeStruct((t, FF), BF16), jax.ShapeDtypeStruct((t, FF), BF16),
                   jax.ShapeDtypeStruct((1, D), F32)],
        compiler_params=_cparams(("arbitrary",)),
    )(dy, xm, g2, a, b, wg, wu, wd)


def _wgrad(a, b, tn, tt, name):
    t, k = a.shape
    n = b.shape[1]
    nsteps = t // tt

    def body(a_ref, b_ref, o_ref):
        @pl.when(pl.program_id(1) == 0)
        def _():
            o_ref[...] = jnp.zeros_like(o_ref)

        o_ref[...] += _tn(a_ref[...].astype(BF16), b_ref[...].astype(BF16))

    return pl.pallas_call(
        body, name=name, grid=(n // tn, nsteps),
        in_specs=[pl.BlockSpec((tt, k), lambda j, s: (s, 0)), pl.BlockSpec((tt, tn), lambda j, s: (s, j))],
        out_specs=pl.BlockSpec((k, tn), lambda j, s: (0, j)),
        out_shape=jax.ShapeDtypeStruct((k, n), F32),
        compiler_params=_cparams(("parallel", "arbitrary")),
    )(a, b)


def _mixer_bwd(dxm, proj, cw, gq, gk, sinks, gco, gao, wo, tq):
    t = proj.shape[0]
    nb = tq // BLK
    r8 = tq // 8
    nt = t // tq
    te = tq + 8
    kvw = 2 * NKV * HP

    def body(dx_ref, dxn_ref, p_ref, cgp_ref, hcp_ref, bgn_ref, cgn_ref, hcn_ref, kvp_ref, cw_ref, gq_ref,
             gk_ref, sk_ref, gco_ref, gao_ref, wo_ref,
             dpm_ref, dkvm_ref, dkvh_ref, dcw_ref, dgq_ref, dgk_ref, dsk_ref, dgco_ref, dgao_ref, acc_ref):
        i = pl.program_id(0)

        @pl.when(i == 0)
        def _():
            for r in (dcw_ref, dgq_ref, dgk_ref, dsk_ref, dgco_ref, dgao_ref):
                r[...] = jnp.zeros_like(r)

        acc_ref[...] = jnp.zeros_like(acc_ref)
        live_rows = jnp.where(i < nt - 1, te, tq)
        dxb = dx_ref[...].astype(BF16)
        dxe = jnp.concatenate([dxb, dxn_ref[...].astype(BF16)], axis=0)
        dcn = _nt(dxe, wo_ref[0:CC, :])
        bg = jnp.concatenate([p_ref[:, O_BG:O_BG + CC], bgn_ref[...]], axis=0)
        cg = jnp.concatenate([p_ref[:, O_CG:O_CG + CC], cgn_ref[...]], axis=0)
        hc = jnp.concatenate([p_ref[:, O_HC:O_HC + CC], hcn_ref[...]], axis=0)
        u = cg * hc
        up = jnp.where(i > 0, cgp_ref[...] * hcp_ref[...], 0.0)
        u1, u2 = _conv_taps(jnp.concatenate([up, u], axis=0), te)
        w0, w1, w2 = cw_ref[0:1, :], cw_ref[1:2, :], cw_ref[2:3, :]
        y = w0 * u2 + w1 * u1 + w2 * u
        co = bg * y
        rc, coh = _rms_fwd(co, 1.0 / CC)
        dco = _rms_bwd(dcn, gco_ref[...], coh, rc, 1.0 / CC)
        row_io = lax.broadcasted_iota(jnp.int32, (te, 1), 0)
        own = row_io < tq
        dgco_ref[...] += jnp.sum(jnp.where(own, dcn * coh, 0.0), axis=0, keepdims=True)
        dyc = jnp.where(row_io < live_rows, dco * bg, 0.0)
        dyo = jnp.where(own, dyc, 0.0)
        dcw_ref[0:1, :] += jnp.sum(dyo * u2, axis=0, keepdims=True)
        dcw_ref[1:2, :] += jnp.sum(dyo * u1, axis=0, keepdims=True)
        dcw_ref[2:3, :] += jnp.sum(dyo * u, axis=0, keepdims=True)
        dy1 = pltpu.roll(dyc, te - 1, 0)[0:tq]
        dy2 = pltpu.roll(dyc, te - 2, 0)[0:tq]
        du = w2 * dyc[0:tq] + w1 * dy1 + w0 * dy2
        dpm_ref[:, O_BG:O_BG + CC] = (dco[0:tq] * y[0:tq]).astype(BF16)
        dpm_ref[:, O_CG:O_CG + CC] = (du * hc[0:tq]).astype(BF16)
        dpm_ref[:, O_HC:O_HC + CC] = (du * cg[0:tq]).astype(BF16)
        dan = _nt(dxb, wo_ref[CC:MIXW, :])
        kraw = jnp.concatenate([kvp_ref[:, 0:NKV * HP], p_ref[:, O_K:O_K + NKV * HP]], axis=0)
        vraw = jnp.concatenate([kvp_ref[:, NKV * HP:], p_ref[:, O_V:O_V + NKV * HP]], axis=0)
        gqv, gkv = gq_ref[...], gk_ref[...]
        keys = _norm_keys(kraw, gkv)
        vb = [vraw[:, h * HP:(h + 1) * HP].astype(BF16) for h in range(NKV)]
        base_valid, c_io = _band_mask()
        lane = lax.broadcasted_iota(jnp.int32, (1, HP), 1)
        for b in range(nb):
            lo = jnp.where(i * nb + b == 0, BLK, 0)
            valid = base_valid & (c_io >= lo)
            band = slice(b * BLK, b * BLK + 2 * BLK)
            qs, prs, pss, outs = [], [], [], []
            for g in range(NQ):
                h = g // GRP
                qg = p_ref[b * BLK:(b + 1) * BLK, O_Q + g * HP:O_Q + (g + 1) * HP]
                rq, qh = _rms_fwd(qg, 1.0 / HD)
                qn = (qh * gqv).astype(BF16)
                pr, ps = _attn_probs(qn, keys[h][2][band], sk_ref[0, g], valid)
                qs.append((rq, qh, qn))
                prs.append(pr)
                pss.append(ps)
                outs.append(jnp.dot(pr.astype(BF16), vb[h][band], preferred_element_type=F32))
            ao = jnp.concatenate(outs, axis=1)
            ra, aoh = _rms_fwd(ao, 1.0 / (NQ * HD))
            danb = dan[b * BLK:(b + 1) * BLK]
            dgao_ref[...] += jnp.sum(danb * aoh, axis=0, keepdims=True)
            dao = _rms_bwd(danb, gao_ref[...], aoh, ra, 1.0 / (NQ * HD))
            dqs = []
            for h in range(NKV):
                dss, dobs = [], []
                for g in range(h * GRP, (h + 1) * GRP):
                    rq, qh, qn = qs[g]
                    dob = dao[:, g * HP:(g + 1) * HP].astype(BF16)
                    dp = _nt(dob, vb[h][band])
                    delta = jnp.sum(prs[g] * dp, axis=-1, keepdims=True)
                    dsb = (prs[g] * (dp - delta) * SCALE).astype(BF16)
                    dsk = -jnp.sum(pss[g] * delta, axis=0, keepdims=True)
                    dsk_ref[...] += jnp.where(lane == g, dsk, 0.0)
                    dqn = jnp.dot(dsb, keys[h][2][band], preferred_element_type=F32)
                    dgq_ref[...] += jnp.sum(dqn * qh, axis=0, keepdims=True)
                    dqs.append(_rms_bwd(dqn, gqv, qh, rq, 1.0 / HD).astype(BF16))
                    dss.append(dsb)
                    dobs.append(dob)
                grp = slice(h * GRP, (h + 1) * GRP)
                dkn = _tn(jnp.concatenate(dss, axis=0), jnp.concatenate([q[2] for q in qs[grp]], axis=0))
                dv = _tn(jnp.concatenate([p.astype(BF16) for p in prs[grp]], axis=0),
                         jnp.concatenate(dobs, axis=0))
                khat, rk = keys[h][0][band], keys[h][1][band]
                dgk_ref[...] += jnp.sum(dkn * khat, axis=0, keepdims=True)
                acc_ref[band, h * HP:(h + 1) * HP] += _rms_bwd(dkn, gkv, khat, rk, 1.0 / HD)
                acc_ref[band, (NKV + h) * HP:(NKV + h + 1) * HP] += dv
            dpm_ref[b * BLK:(b + 1) * BLK, O_Q:O_K] = jnp.concatenate(dqs, axis=1)
        dkvh_ref[...] = acc_ref[0:BLK, :]
        dkvm_ref[...] = acc_ref[BLK:, :]

    prev8 = lambda col: pl.BlockSpec((8, CC), lambda i: (jnp.maximum(i * r8 - 1, 0), col))
    next8 = lambda col: pl.BlockSpec((8, CC), lambda i: (jnp.minimum((i + 1) * r8, t // 8 - 1), col))
    small = lambda n: pl.BlockSpec((1, n), lambda i: (0, 0))
    return pl.pallas_call(
        body, name="mixer_bwd", grid=(nt,),
        in_specs=[
            pl.BlockSpec((tq, D), lambda i: (i, 0)),
            pl.BlockSpec((8, D), lambda i: (jnp.minimum((i + 1) * r8, t // 8 - 1), 0)),
            pl.BlockSpec((tq, NP), lambda i: (i, 0)),
            prev8(O_CG // CC), prev8(O_HC // CC),
            next8(O_BG // CC), next8(O_CG // CC), next8(O_HC // CC),
            pl.BlockSpec((BLK, kvw), lambda i: (jnp.maximum(i * nb - 1, 0), O_K // kvw)),
            _const_spec((8, CC)), _const_spec((1, HP)), _const_spec((1, HP)),
            pl.BlockSpec(memory_space=pltpu.SMEM),
            _const_spec((1, CC)), _const_spec((1, NQ * HP)), _const_spec((MIXW, D)),
        ],
        out_specs=[
            pl.BlockSpec((tq, NMAIN), lambda i: (i, 0)),
            pl.BlockSpec((tq, kvw), lambda i: (i, 0)),
            pl.BlockSpec((BLK, kvw), lambda i: (i, 0)),
            pl.BlockSpec((8, CC), lambda i: (0, 0)), small(HP), small(HP), small(HP), small(CC), small(NQ * HP),
        ],
        out_shape=[
            jax.ShapeDtypeStruct((t, NMAIN), BF16), jax.ShapeDtypeStruct((t, kvw), F32),
            jax.ShapeDtypeStruct((nt * BLK, kvw), F32),
            jax.ShapeDtypeStruct((8, CC), F32), jax.ShapeDtypeStruct((1, HP), F32), jax.ShapeDtypeStruct((1, HP), F32),
            jax.ShapeDtypeStruct((1, HP), F32), jax.ShapeDtypeStruct((1, CC), F32),
            jax.ShapeDtypeStruct((1, NQ * HP), F32),
        ],
        scratch_shapes=[pltpu.VMEM((tq + BLK, kvw), F32)],
        compiler_params=_cparams(("arbitrary",)),
    )(dxm, dxm, proj, proj, proj, proj, proj, proj, proj, cw, gq, gk, sinks, gco, gao, wo)


def _inproj_bwd(dpm, dkv, wp, x, g1, dxm, tm):
    t = x.shape[0]
    kvw = 2 * NKV * HP

    def body(dp_ref, dk_ref, w_ref, x_ref, g_ref, dxm_ref, dx_ref, dg_ref):
        @pl.when(pl.program_id(0) == 0)
        def _():
            dg_ref[...] = jnp.zeros_like(dg_ref)

        dh = _nt(dp_ref[...], w_ref[:, 0:NMAIN]) + _nt(dk_ref[...], w_ref[:, NMAIN:NP])
        r, xh = _rms_fwd(x_ref[...], 1.0 / D)
        dg_ref[...] += jnp.sum(dh * xh, axis=0, keepdims=True)
        dx_ref[...] = dxm_ref[...] + _rms_bwd(dh, g_ref[...], xh, r, 1.0 / D)

    row = lambda w: pl.BlockSpec((tm, w), lambda i: (i, 0))
    return pl.pallas_call(
        body, name="inproj_bwd", grid=(t // tm,),
        in_specs=[row(NMAIN), row(kvw), _const_spec((D, NP)), row(D), _const_spec((1, D)), row(D)],
        out_specs=[row(D), pl.BlockSpec((1, D), lambda i: (0, 0))],
        out_shape=[jax.ShapeDtypeStruct((t, D), F32), jax.ShapeDtypeStruct((1, D), F32)],
        compiler_params=_cparams(("arbitrary",)),
    )(dpm, dkv, wp, x, g1, dxm)


def _rows_tile(rows):
    for cand in (512, 256, 128, 64, 32, 16, 8):
        if rows % cand == 0:
            return cand
    return rows


def _add_pairs(xs, ys, name):
    outs = []
    for n, (xa, ya) in enumerate(zip(xs, ys)):
        rows, cols = xa.shape
        tr = _rows_tile(rows)

        def body(x_ref, y_ref, o_ref):
            o_ref[...] = (x_ref[...].astype(F32) + y_ref[...].astype(F32)).astype(BF16)

        spec = pl.BlockSpec((tr, cols), lambda i: (i, 0))
        outs.append(pl.pallas_call(
            body, name=f"{name}_{n}", grid=(rows // tr,), in_specs=[spec, spec], out_specs=spec,
            out_shape=jax.ShapeDtypeStruct((rows, cols), BF16), compiler_params=_cparams(("parallel",)),
        )(xa, ya))
    return outs


def _sum_chips(cs, name):
    outs = []
    for n, ca in enumerate(cs):
        _, rows, cols = ca.shape
        tr = _rows_tile(rows)

        def body(c_ref, o_ref):
            acc = c_ref[0].astype(F32)
            for j in range(1, N_CHIPS):
                acc = acc + c_ref[j].astype(F32)
            o_ref[...] = acc

        outs.append(pl.pallas_call(
            body, name=f"{name}_{n}", grid=(rows // tr,),
            in_specs=[pl.BlockSpec((N_CHIPS, tr, cols), lambda i: (0, i, 0))],
            out_specs=pl.BlockSpec((tr, cols), lambda i: (i, 0)),
            out_shape=jax.ShapeDtypeStruct((rows, cols), F32), compiler_params=_cparams(("parallel",)),
        )(ca))
    return outs


def _adamw(w, g, m, v, name):
    rows, cols = w.shape
    tr = _rows_tile(rows)
    c1 = 1.0 - ADAM_B1 ** ADAM_STEP
    c2 = 1.0 - ADAM_B2 ** ADAM_STEP

    def body(w_ref, g_ref, m_ref, v_ref, d_ref, mo_ref, vo_ref):
        gv = g_ref[...]
        mn = ADAM_B1 * m_ref[...] + (1.0 - ADAM_B1) * gv
        vn = ADAM_B2 * v_ref[...] + (1.0 - ADAM_B2) * (gv * gv)
        mo_ref[...] = mn
        vo_ref[...] = vn
        d_ref[...] = -ADAM_LR * ((mn / c1) / (jnp.sqrt(vn / c2) + ADAM_EPS) + ADAM_WD * w_ref[...])

    spec = pl.BlockSpec((tr, cols), lambda i: (i, 0))
    sds = jax.ShapeDtypeStruct((rows, cols), F32)
    return pl.pallas_call(
        body, name=name, grid=(rows // tr,), in_specs=[spec] * 4, out_specs=[spec] * 3, out_shape=[sds] * 3,
        compiler_params=_cparams(("parallel",)),
    )(w, g, m, v)


def _place():
    x, y, c = lax.axis_index("x"), lax.axis_index("y"), lax.axis_index("c")
    chips = [(1 - x, y), (x, 1 - y), (1 - x, 1 - y)]
    return x, y, c, chips


ANY = pl.BlockSpec(memory_space=pl.ANY)
DMA_ROWS = 64


def _pieces(shape):
    rows = shape[-2]
    step = DMA_ROWS if rows % DMA_ROWS == 0 else rows
    lead = [()]
    for n in shape[:-2]:
        lead = [i + (k,) for i in lead for k in range(n)]
    return [i + (pl.ds(r0, step),) for i in lead for r0 in range(0, rows, step)]


def _start_pieces(make, src, dst):
    for idx in _pieces(src.shape):
        make(src.at[idx], dst.at[idx]).start()


def _gather_weights(shards, cwp):
    nw = len(shards)

    def body(*refs):
        srcs, cw_ref = refs[:nw], refs[nw]
        outs, ocw_ref = refs[nw + 1:2 * nw + 1], refs[2 * nw + 1]
        ssem, rsem, fssem, frsem = refs[2 * nw + 2:]
        x, y, c, chips = _place()
        kme = 2 * x + y
        sib = (x, y, 1 - c)

        def landing(w, block_chip, layer):
            return ocw_ref.at[block_chip] if w == nw else outs[w].at[block_chip, layer]

        def plane(j, w, to):
            return lambda s, d: pltpu.make_async_remote_copy(
                src_ref=s, dst_ref=d, send_sem=ssem.at[j, w], recv_sem=rsem.at[j, w], device_id=to,
                device_id_type=MESH)

        def passed(j, w):
            return lambda s, d: pltpu.make_async_remote_copy(
                src_ref=s, dst_ref=d, send_sem=fssem.at[j, w], recv_sem=frsem.at[j, w], device_id=sib,
                device_id_type=MESH)

        for j, (px, py) in enumerate(chips):
            for w in range(nw + 1):
                _start_pieces(plane(j, w, (px, py, c)), cw_ref if w == nw else srcs[w].at[c], landing(w, kme, c))
        for j, (px, py) in enumerate(chips):
            for w in range(nw + 1):
                got = landing(w, 2 * px + py, c)
                plane(j, w, (px, py, c))(got, got).wait_recv()
                if w < nw:
                    _start_pieces(passed(j, w), got, got)
        for j, (px, py) in enumerate(chips):
            for w in range(nw):
                got = landing(w, 2 * px + py, 1 - c)
                passed(j, w)(got, got).wait_recv()
        for j, (px, py) in enumerate(chips):
            for w in range(nw + 1):
                sent = landing(w, kme, c)
                plane(j, w, (px, py, c))(sent, sent).wait_send()
                if w < nw:
                    fwd = landing(w, 2 * px + py, c)
                    passed(j, w)(fwd, fwd).wait_send()

    out_shape = [jax.ShapeDtypeStruct((N_CHIPS,) + s.shape, s.dtype) for s in shards]
    out_shape.append(jax.ShapeDtypeStruct((N_CHIPS,) + cwp.shape, cwp.dtype))
    return pl.pallas_call(
        body, name="gather_weights", in_specs=[ANY] * (nw + 1), out_specs=[ANY] * (nw + 1), out_shape=out_shape,
        scratch_shapes=[pltpu.SemaphoreType.DMA((3, nw + 1)), pltpu.SemaphoreType.DMA((3, nw + 1)),
                        pltpu.SemaphoreType.DMA((3, nw)), pltpu.SemaphoreType.DMA((3, nw))],
        compiler_params=_cparams(has_side_effects=True),
    )(*shards, cwp)


def _swap_layers(gs):
    nw = len(gs)

    def body(*refs):
        srcs, theirs = refs[:nw], refs[nw:2 * nw]
        ssem, rsem = refs[2 * nw:]
        x, y, c, _ = _place()

        def give(w):
            return lambda s, d: pltpu.make_async_remote_copy(
                src_ref=s, dst_ref=d, send_sem=ssem.at[w], recv_sem=rsem.at[w], device_id=(x, y, 1 - c),
                device_id_type=MESH)

        for w in range(nw):
            _start_pieces(give(w), srcs[w].at[1 - c], theirs[w])
        for w in range(nw):
            give(w)(srcs[w].at[1 - c], theirs[w]).wait()

    return pl.pallas_call(
        body, name="swap_layers", in_specs=[ANY] * nw, out_specs=[ANY] * nw,
        out_shape=[jax.ShapeDtypeStruct(g.shape[1:], g.dtype) for g in gs],
        scratch_shapes=[pltpu.SemaphoreType.DMA((nw,))] * 2,
        compiler_params=_cparams(has_side_effects=True),
    )(*gs)


def _scatter_chips(ps):
    nw = len(ps)

    def body(*refs):
        srcs, outs = refs[:nw], refs[nw:2 * nw]
        ssem, rsem = refs[2 * nw:]
        x, y, c, chips = _place()
        kme = 2 * x + y

        def give(j, w, to):
            return lambda s, d: pltpu.make_async_remote_copy(
                src_ref=s, dst_ref=d, send_sem=ssem.at[j, w], recv_sem=rsem.at[j, w], device_id=to,
                device_id_type=MESH)

        for j, (px, py) in enumerate(chips):
            for w in range(nw):
                _start_pieces(give(j, w, (px, py, c)), srcs[w].at[2 * px + py], outs[w].at[kme])
        for j, (px, py) in enumerate(chips):
            for w in range(nw):
                got = outs[w].at[2 * px + py]
                give(j, w, (px, py, c))(got, got).wait_recv()
        for j, (px, py) in enumerate(chips):
            for w in range(nw):
                sent = srcs[w].at[2 * px + py]
                give(j, w, (px, py, c))(sent, sent).wait_send()

    return pl.pallas_call(
        body, name="scatter_chips", in_specs=[ANY] * nw, out_specs=[ANY] * nw,
        out_shape=[jax.ShapeDtypeStruct(p.shape, p.dtype) for p in ps],
        scratch_shapes=[pltpu.SemaphoreType.DMA((3, nw)), pltpu.SemaphoreType.DMA((3, nw))],
        compiler_params=_cparams(has_side_effects=True),
    )(*ps)


def _swap_siblings(rs):
    nw = len(rs)

    def body(*refs):
        srcs, outs = refs[:nw], refs[nw:2 * nw]
        ssem, rsem = refs[2 * nw:]
        x, y, c, _ = _place()

        def give(w):
            return lambda s, d: pltpu.make_async_remote_copy(
                src_ref=s, dst_ref=d, send_sem=ssem.at[w], recv_sem=rsem.at[w], device_id=(x, y, 1 - c),
                device_id_type=MESH)

        for w in range(nw):
            _start_pieces(give(w), srcs[w], outs[w])
        for w in range(nw):
            give(w)(srcs[w], outs[w]).wait()

    return pl.pallas_call(
        body, name="swap_siblings", in_specs=[ANY] * nw, out_specs=[ANY] * nw,
        out_shape=[jax.ShapeDtypeStruct(r.shape, r.dtype) for r in rs],
        scratch_shapes=[pltpu.SemaphoreType.DMA((nw,))] * 2,
        compiler_params=_cparams(has_side_effects=True),
    )(*rs)


def _allreduce_small(v):
    rows = v.shape[0]

    def body(v_ref, o_ref, buf, ssem, rsem):
        x, y, c, _ = _place()
        me = 4 * x + 2 * y + c
        buf[me] = v_ref[...]
        sends = []
        for r in range(1, 8):
            peer = (x ^ (r >> 2), y ^ ((r >> 1) & 1), c ^ (r & 1))
            cp = pltpu.make_async_remote_copy(
                src_ref=v_ref, dst_ref=buf.at[me], send_sem=ssem.at[r - 1], recv_sem=rsem.at[r - 1],
                device_id=peer, device_id_type=MESH)
            cp.start()
            sends.append(cp)
        for r in range(1, 8):
            src = me ^ r
            pltpu.make_async_remote_copy(
                src_ref=v_ref, dst_ref=buf.at[src], send_sem=ssem.at[r - 1], recv_sem=rsem.at[r - 1],
                device_id=(x, y, c), device_id_type=MESH).wait_recv()
        for cp in sends:
            cp.wait_send()
        acc = buf[0]
        for d in range(1, 8):
            acc = acc + buf[d]
        o_ref[...] = acc

    vm = pl.BlockSpec(memory_space=pltpu.VMEM)
    return pl.pallas_call(
        body, name="allreduce_small", in_specs=[vm], out_specs=vm,
        out_shape=jax.ShapeDtypeStruct(v.shape, F32),
        scratch_shapes=[pltpu.VMEM((8, rows, 128), F32), pltpu.SemaphoreType.DMA((7,)),
                        pltpu.SemaphoreType.DMA((7,))],
        compiler_params=_cparams(has_side_effects=True),
    )(v)


def _pad_heads(w, n_heads, axis):
    shp = w.shape
    w = w.reshape(shp[:axis] + (n_heads, HD) + shp[axis + 1:])
    pad = [(0, 0)] * w.ndim
    pad[axis + 1] = (0, HP - HD)
    w = jnp.pad(w, pad)
    return w.reshape(shp[:axis] + (n_heads * HP,) + shp[axis + 1:])


def _strip_heads(w, n_heads, axis):
    shp = w.shape
    w = w.reshape(shp[:axis] + (n_heads, HP) + shp[axis + 1:])
    w = lax.slice_in_dim(w, 0, HD, axis=axis + 1)
    return w.reshape(shp[:axis] + (n_heads * HD,) + shp[axis + 1:])


def _unshard_cols(g4):
    k, r, c = g4.shape
    return jnp.transpose(g4, (1, 0, 2)).reshape(r, k * c)


def _shard_cols(w):
    r, n = w.shape
    return jnp.transpose(w.reshape(r, N_CHIPS, n // N_CHIPS), (1, 0, 2))


def _pad_win(win):
    parts = [win[:, :3 * CC], _pad_heads(win[:, 3 * CC:3 * CC + NQ * HD], NQ, 1),
             _pad_heads(win[:, 3 * CC + NQ * HD:3 * CC + (NQ + NKV) * HD], NKV, 1),
             _pad_heads(win[:, 3 * CC + (NQ + NKV) * HD:], NKV, 1)]
    return jnp.concatenate(parts, axis=1)


def _strip_win(gp):
    parts = [gp[:, :3 * CC], _strip_heads(gp[:, O_Q:O_K], NQ, 1), _strip_heads(gp[:, O_K:O_V], NKV, 1),
             _strip_heads(gp[:, O_V:], NKV, 1)]
    return jnp.concatenate(parts, axis=1)


def _count(shape):
    n = 1
    for s in shape:
        n *= s
    return n


def _pack_rows(arrs):
    flat = [jnp.pad(a.reshape(-1), (0, (-_count(a.shape)) % 128)) for a in arrs]
    v = jnp.concatenate(flat)
    rows = -(-v.shape[0] // (8 * 128)) * 8
    return jnp.pad(v, (0, rows * 128 - v.shape[0])).reshape(rows, 128)


def kernel(x, norm1_g, w_in, conv_w, q_norm_g, k_norm_g, sinks, conv_out_g, attn_out_g, w_o, norm2_g, w_gate, w_up, w_down, loss_target, m_norm1_g, m_w_in, m_conv_w, m_q_norm_g, m_k_norm_g, m_sinks, m_conv_out_g, m_attn_out_g, m_w_o, m_norm2_g, m_w_gate, m_w_up, m_w_down, v_norm1_g, v_w_in, v_conv_w, v_q_norm_g, v_k_norm_g, v_sinks, v_conv_out_g, v_attn_out_g, v_w_o, v_norm2_g, v_w_gate, v_w_up, v_w_down):
    depth = w_in.shape[0]
    t = x.shape[1]
    xs = x.reshape(t, D)
    tgt = loss_target.reshape(t, D)
    xi, yi = lax.axis_index("x"), lax.axis_index("y")
    kme = 2 * xi + yi
    tm = min(512, t)
    tq = min(256, t)
    tf = min(256, t)

    cwp = jnp.pad(conv_w.reshape(depth * 3, CC // N_CHIPS), ((0, 8 - depth * 3), (0, 0)))
    big = [w_in, w_o, w_gate, w_up, w_down]
    own = [w.astype(BF16) for w in big] + [cwp]
    gathered = _gather_weights(own[:-1], cwp)
    a_in, a_o, a_g, a_u, a_d, a_cw = [lax.dynamic_update_index_in_dim(g, o, kme, 0) for g, o in zip(gathered, own)]
    cw_full = _unshard_cols(a_cw)[:depth * 3].reshape(depth, 3, CC)
    layers = []
    for l in range(depth):
        wp = _pad_win(_unshard_cols(a_in[:, l]))
        wo = a_o[:, l].reshape(D, D)
        wo = jnp.concatenate([wo[:CC], _pad_heads(wo[CC:], NQ, 0)], axis=0)
        layers.append(dict(
            wp=wp, wo=wo, wg=_unshard_cols(a_g[:, l]), wu=_unshard_cols(a_u[:, l]), wd=a_d[:, l].reshape(FF, D),
            cw=jnp.pad(cw_full[l], ((0, 5), (0, 0))),
            g1=norm1_g[l].reshape(1, D), g2=norm2_g[l].reshape(1, D),
            gq=jnp.pad(q_norm_g[l], (0, HP - HD)).reshape(1, HP), gk=jnp.pad(k_norm_g[l], (0, HP - HD)).reshape(1, HP),
            sk=sinks[l].reshape(1, NQ), gco=conv_out_g[l].reshape(1, CC),
            gao=_pad_heads(attn_out_g[l], NQ, 0).reshape(1, NQ * HP)))

    saved = []
    cur = xs
    for l in range(depth):
        p = layers[l]
        proj, h = _inproj_fwd(cur, p["g1"], p["wp"], tm)
        xm, mix = _mixer_fwd(proj, cur, p["cw"], p["gq"], p["gk"], p["sk"], p["gco"], p["gao"], p["wo"], tq)
        xo, a, b, h2 = _ffn_fwd(xm, p["g2"], p["wg"], p["wu"], p["wd"], tf)
        saved.append(dict(x=cur, proj=proj, h=h, xm=xm, mix=mix, a=a, b=b, h2=h2))
        cur = xo
    lpart, dy = _loss_and_grad(cur, tgt, tm)
    loss = lax.psum(lpart[0, 0], ("x", "y", "c"))

    nt = t // tq
    gbig = [None] * depth
    gsmall = [None] * depth
    for l in reversed(range(depth)):
        p, s = layers[l], saved[l]
        dxm, da, db, hm, dg2 = _ffn_bwd(dy, s["xm"], p["g2"], s["a"], s["b"], p["wg"], p["wu"], p["wd"], tf)
        g_wg = _wgrad(s["h2"], da, FF // 2, tm, "wgrad_gate")
        g_wu = _wgrad(s["h2"], db, FF // 2, tm, "wgrad_up")
        g_wdt = _wgrad(dy, hm, FF // 2, tm, "wgrad_down")
        dpm, dkvm, dkvh, dcw, dgq, dgk, dsk, dgco, dgao = _mixer_bwd(
            dxm, s["proj"], p["cw"], p["gq"], p["gk"], p["sk"], p["gco"], p["gao"], p["wo"], tq)
        g_wot = _wgrad(dxm, s["mix"], MIXW, tm, "wgrad_o")
        kvw = dkvm.shape[1]
        halo = jnp.concatenate([dkvh.reshape(nt, BLK, kvw)[1:], jnp.zeros((1, BLK, kvw), F32)], axis=0)
        halo = jnp.pad(halo, ((0, 0), (tq - BLK, 0), (0, 0)))
        dkv = (dkvm.reshape(nt, tq, kvw) + halo).reshape(t, kvw).astype(BF16)
        dx, dg1 = _inproj_bwd(dpm, dkv, p["wp"], s["x"], p["g1"], dxm, tm)
        g_wpm = _wgrad(s["h"], dpm, NMAIN // 2, tm, "wgrad_in_main")
        g_wpk = _wgrad(s["h"], dkv, kvw, tm, "wgrad_in_kv")
        dy = dx
        g_in = _strip_win(jnp.concatenate([g_wpm, g_wpk], axis=1))
        g_ot = jnp.concatenate([g_wot[:, :CC], _strip_heads(g_wot[:, CC:], NQ, 1)], axis=1)
        gbig[l] = [_shard_cols(g.astype(BF16)) for g in (g_in, g_ot, g_wg, g_wu, g_wdt)]
        gsmall[l] = dict(g1=dg1, cw=dcw[:3], gq=dgq[0, :HD], gk=dgk[0, :HD], sk=dsk[0, :NQ], gco=dgco,
                         gao=_strip_heads(dgao.reshape(NQ * HP), NQ, 0), g2=dg2)
    grad_x = dy.reshape(x.shape)

    nw = len(big)
    gs = [jnp.stack([gbig[l][w] for l in range(depth)]) for w in range(nw)]
    ci = lax.axis_index("c")
    theirs = _swap_layers(gs)
    mine = [lax.dynamic_index_in_dim(g, ci, 0, keepdims=False) for g in gs]
    flat = lambda a4: a4.reshape(a4.shape[0] * a4.shape[1], a4.shape[2])
    ps = _add_pairs([flat(a4) for a4 in mine], [flat(a4) for a4 in theirs], "presum")
    ps = [q.reshape(m4.shape) for q, m4 in zip(ps, mine)]
    cs = [lax.dynamic_update_index_in_dim(got, lax.dynamic_index_in_dim(q, kme, 0, keepdims=False), kme, 0)
          for got, q in zip(_scatter_chips(ps), ps)]
    r_mine = _sum_chips(cs, "chipsum")
    r_theirs = _swap_siblings(r_mine)
    rs = [jnp.where(ci == 0, jnp.stack([a, b]), jnp.stack([b, a])) for a, b in zip(r_mine, r_theirs)]
    r_in, r_ot, r_g, r_u, r_dt = rs
    g_big = [r_in, jnp.transpose(r_ot, (0, 2, 1)), r_g, r_u, jnp.transpose(r_dt, (0, 2, 1))]

    small_shapes = dict(g1=(D,), cw=(3, CC), gq=(HD,), gk=(HD,), sk=(NQ,), gco=(CC,), gao=(NQ * HD,), g2=(D,))
    red = _allreduce_small(_pack_rows([gsmall[l][n] for l in range(depth) for n in small_shapes])).reshape(-1)
    red_small, offs = {n: [] for n in small_shapes}, 0
    for l in range(depth):
        for n, shp in small_shapes.items():
            cnt = _count(shp)
            red_small[n].append(red[offs:offs + cnt].reshape(shp))
            offs += -(-cnt // 128) * 128
    g_small = {n: jnp.stack(v) for n, v in red_small.items()}
    g_cw = lax.dynamic_slice_in_dim(g_small["cw"], kme * (CC // N_CHIPS), CC // N_CHIPS, axis=2)

    weights = [norm1_g, w_in, conv_w, q_norm_g, k_norm_g, sinks, conv_out_g, attn_out_g, w_o, norm2_g, w_gate,
               w_up, w_down]
    moms = [m_norm1_g, m_w_in, m_conv_w, m_q_norm_g, m_k_norm_g, m_sinks, m_conv_out_g, m_attn_out_g, m_w_o,
            m_norm2_g, m_w_gate, m_w_up, m_w_down]
    vars_ = [v_norm1_g, v_w_in, v_conv_w, v_q_norm_g, v_k_norm_g, v_sinks, v_conv_out_g, v_attn_out_g, v_w_o,
             v_norm2_g, v_w_gate, v_w_up, v_w_down]
    grads = [g_small["g1"], g_big[0], g_cw, g_small["gq"], g_small["gk"], g_small["sk"], g_small["gco"],
             g_small["gao"], g_big[1], g_small["g2"], g_big[2], g_big[3], g_big[4]]
    n_w = len(weights)
    big_idx = [1, 8, 10, 11, 12]
    small_idx = [n for n in range(n_w) if n not in big_idx]
    deltas, new_m, new_v = [None] * n_w, [None] * n_w, [None] * n_w
    for n in big_idx:
        shp = weights[n].shape
        two = [a3.reshape(shp[0] * shp[1], shp[2]) for a3 in (weights[n], grads[n], moms[n], vars_[n])]
        res = _adamw(*two, f"adamw_{n}")
        deltas[n], new_m[n], new_v[n] = [r.reshape(shp) for r in res]
    res = _adamw(*[_pack_rows([arrs[n] for n in small_idx]) for arrs in (weights, grads, moms, vars_)],
                 "adamw_small")
    offs = 0
    for n in small_idx:
        shp = weights[n].shape
        cnt = _count(shp)
        deltas[n], new_m[n], new_v[n] = [r.reshape(-1)[offs:offs + cnt].reshape(shp) for r in res]
        offs += -(-cnt // 128) * 128
    return (loss, grad_x, *grads, *deltas, *new_m, *new_v)
```

```python
import functools

import jax
import jax.numpy as jnp
from jax import lax
from jax.experimental import pallas as pl
from jax.experimental.pallas import tpu as pltpu

F32 = jnp.float32
BF16 = jnp.bfloat16

D = 1024
CC = 512
NQ = 8
NKV = 2
HD = 64
HP = 128
GRP = NQ // NKV
FF = 2816
FFB = FF // 4
BLK = 128
EPS = 1e-6
NEG = -1e30
SCALE = HD ** -0.5
O_BG, O_CG, O_HC, O_Q = 0, CC, 2 * CC, 3 * CC
O_K = O_Q + NQ * HP
O_V = O_K + NKV * HP
NP = O_V + NKV * HP
NMAIN = O_K
MIXW = CC + NQ * HP
N_CHIPS = 4
VMEM_LIMIT = 56 * 1024 * 1024
MESH = pl.DeviceIdType.MESH

ADAM_LR, ADAM_B1, ADAM_B2, ADAM_EPS, ADAM_WD, ADAM_STEP = 0.001, 0.9, 0.999, 1e-08, 0.01, 10


def _cparams(sem=None, **kw):
    if sem is not None:
        kw["dimension_semantics"] = sem
    return pltpu.CompilerParams(vmem_limit_bytes=VMEM_LIMIT, **kw)


def _const_spec(shape):
    nd = len(shape)
    return pl.BlockSpec(shape, lambda *_: (0,) * nd, pipeline_mode=pl.Buffered(1))


def _nt(a, b):
    return lax.dot_general(a, b, (((1,), (1,)), ((), ())), preferred_element_type=F32)


def _tn(a, b):
    return lax.dot_general(a, b, (((0,), (0,)), ((), ())), preferred_element_type=F32)


def _rms_fwd(x, inv_n):
    r = lax.rsqrt(jnp.sum(x * x, axis=-1, keepdims=True) * inv_n + EPS)
    return r, x * r


def _rms_bwd(dy, g, xh, r, inv_n):
    dxh = dy * g
    return r * (dxh - xh * (jnp.sum(dxh * xh, axis=-1, keepdims=True) * inv_n))


def _inproj_fwd(x, g1, wp, tm):
    t = x.shape[0]

    def body(x_ref, g_ref, w_ref, p_ref, h_ref):
        _, xh = _rms_fwd(x_ref[...], 1.0 / D)
        h = (xh * g_ref[...]).astype(BF16)
        h_ref[...] = h
        p_ref[...] = jnp.dot(h, w_ref[...], preferred_element_type=F32)

    return pl.pallas_call(
        body, name="inproj_fwd", grid=(t // tm,),
        in_specs=[pl.BlockSpec((tm, D), lambda i: (i, 0)), _const_spec((1, D)), _const_spec((D, NP))],
        out_specs=[pl.BlockSpec((tm, NP), lambda i: (i, 0)), pl.BlockSpec((tm, D), lambda i: (i, 0))],
        out_shape=[jax.ShapeDtypeStruct((t, NP), F32), jax.ShapeDtypeStruct((t, D), BF16)],
        compiler_params=_cparams(("parallel",)),
    )(x, g1, wp)


def _band_mask():
    r_io = lax.broadcasted_iota(jnp.int32, (BLK, 2 * BLK), 0)
    c_io = lax.broadcasted_iota(jnp.int32, (BLK, 2 * BLK), 1)
    return (c_io > r_io) & (c_io <= r_io + BLK), c_io


def _conv_taps(uf, n):
    u1 = pltpu.roll(uf, 1, 0)[8:8 + n]
    u2 = pltpu.roll(uf, 2, 0)[8:8 + n]
    return u1, u2


def _attn_probs(qn, kband, sink, valid):
    s = _nt(qn, kband) * SCALE
    s = jnp.where(valid, s, NEG)
    m = jnp.maximum(jnp.max(s, axis=-1, keepdims=True), sink)
    p = jnp.exp(s - m)
    es = jnp.exp(sink - m)
    inv = 1.0 / (jnp.sum(p, axis=-1, keepdims=True) + es)
    return p * inv, es * inv


def _norm_keys(kraw, gk):
    out = []
    for h in range(NKV):
        kh = kraw[:, h * HP:(h + 1) * HP]
        rk, khat = _rms_fwd(kh, 1.0 / HD)
        out.append((khat, rk, (khat * gk).astype(BF16)))
    return out


def _mixer_fwd(proj, x, cw, gq, gk, sinks, gco, gao, wo, tq):
    t = proj.shape[0]
    nb = tq // BLK
    r8 = tq // 8

    def body(p_ref, cgp_ref, hcp_ref, kvp_ref, x_ref, cw_ref, gq_ref, gk_ref, sk_ref, gco_ref, gao_ref,
             wo_ref, xm_ref, mix_ref):
        i = pl.program_id(0)
        cg = p_ref[:, O_CG:O_CG + CC]
        hc = p_ref[:, O_HC:O_HC + CC]
        u = cg * hc
        up = jnp.where(i > 0, cgp_ref[...] * hcp_ref[...], 0.0)
        u1, u2 = _conv_taps(jnp.concatenate([up, u], axis=0), tq)
        y = cw_ref[0:1, :] * u2 + cw_ref[1:2, :] * u1 + cw_ref[2:3, :] * u
        co = p_ref[:, O_BG:O_BG + CC] * y
        _, coh = _rms_fwd(co, 1.0 / CC)
        cn = coh * gco_ref[...]
        kraw = jnp.concatenate([kvp_ref[:, 0:NKV * HP], p_ref[:, O_K:O_K + NKV * HP]], axis=0)
        vraw = jnp.concatenate([kvp_ref[:, NKV * HP:], p_ref[:, O_V:O_V + NKV * HP]], axis=0)
        keys = _norm_keys(kraw, gk_ref[...])
        vb = [vraw[:, h * HP:(h + 1) * HP].astype(BF16) for h in range(NKV)]
        base_valid, c_io = _band_mask()
        rows = []
        for b in range(nb):
            lo = jnp.where(i * nb + b == 0, BLK, 0)
            valid = base_valid & (c_io >= lo)
            outs = []
            for g in range(NQ):
                h = g // GRP
                qg = p_ref[b * BLK:(b + 1) * BLK, O_Q + g * HP:O_Q + (g + 1) * HP]
                _, qh = _rms_fwd(qg, 1.0 / HD)
                qn = (qh * gq_ref[...]).astype(BF16)
                pr, _ = _attn_probs(qn, keys[h][2][b * BLK:b * BLK + 2 * BLK], sk_ref[0, g], valid)
                outs.append(jnp.dot(pr.astype(BF16), vb[h][b * BLK:b * BLK + 2 * BLK],
                                    preferred_element_type=F32))
            rows.append(jnp.concatenate(outs, axis=1))
        ao = jnp.concatenate(rows, axis=0)
        _, aoh = _rms_fwd(ao, 1.0 / (NQ * HD))
        an = aoh * gao_ref[...]
        mix = jnp.concatenate([cn, an], axis=1).astype(BF16)
        mix_ref[...] = mix
        xm_ref[...] = x_ref[...] + jnp.dot(mix, wo_ref[...], preferred_element_type=F32)

    prev8 = lambda col: pl.BlockSpec((8, CC), lambda i: (jnp.maximum(i * r8 - 1, 0), col))
    return pl.pallas_call(
        body, name="mixer_fwd", grid=(t // tq,),
        in_specs=[
            pl.BlockSpec((tq, NP), lambda i: (i, 0)),
            prev8(O_CG // CC), prev8(O_HC // CC),
            pl.BlockSpec((BLK, 2 * NKV * HP), lambda i: (jnp.maximum(i * nb - 1, 0), O_K // (2 * NKV * HP))),
            pl.BlockSpec((tq, D), lambda i: (i, 0)),
            _const_spec((8, CC)), _const_spec((1, HP)), _const_spec((1, HP)),
            pl.BlockSpec(memory_space=pltpu.SMEM),
            _const_spec((1, CC)), _const_spec((1, NQ * HP)), _const_spec((MIXW, D)),
        ],
        out_specs=[pl.BlockSpec((tq, D), lambda i: (i, 0)), pl.BlockSpec((tq, MIXW), lambda i: (i, 0))],
        out_shape=[jax.ShapeDtypeStruct((t, D), F32), jax.ShapeDtypeStruct((t, MIXW), BF16)],
        compiler_params=_cparams(("parallel",)),
    )(proj, proj, proj, proj, x, cw, gq, gk, sinks, gco, gao, wo)


def _ffn_weight_specs():
    return [pl.BlockSpec((N_CHIPS, D, FFB), lambda i: (0, 0, 0), pipeline_mode=pl.Buffered(1)),
            pl.BlockSpec((N_CHIPS, D, FFB), lambda i: (0, 1, 0), pipeline_mode=pl.Buffered(1)),
            pl.BlockSpec((N_CHIPS, FFB, D), lambda i: (0, 0, 0), pipeline_mode=pl.Buffered(1))]


def _ffn_fwd(xm, g2, ga, gb, tm):
    t = xm.shape[0]

    def body(x_ref, g_ref, wg_ref, wu_ref, wd_ref, xo_ref, a_ref, b_ref, h2_ref):
        xv = x_ref[...]
        _, xh = _rms_fwd(xv, 1.0 / D)
        h2 = (xh * g_ref[...]).astype(BF16)
        h2_ref[...] = h2
        acc = xv
        for k in range(N_CHIPS):
            a = jnp.dot(h2, wg_ref[k], preferred_element_type=F32)
            b = jnp.dot(h2, wu_ref[k], preferred_element_type=F32)
            a_ref[k] = a.astype(BF16)
            b_ref[k] = b.astype(BF16)
            hm = (a * jax.nn.sigmoid(a) * b).astype(BF16)
            acc = acc + jnp.dot(hm, wd_ref[k], preferred_element_type=F32)
        xo_ref[...] = acc

    row = lambda w: pl.BlockSpec((tm, w), lambda i: (i, 0))
    blk = pl.BlockSpec((N_CHIPS, tm, FFB), lambda i: (0, i, 0))
    return pl.pallas_call(
        body, name="ffn_fwd", grid=(t // tm,),
        in_specs=[row(D), _const_spec((1, D))] + _ffn_weight_specs(),
        out_specs=[row(D), blk, blk, row(D)],
        out_shape=[jax.ShapeDtypeStruct((t, D), F32), jax.ShapeDtypeStruct((N_CHIPS, t, FFB), BF16),
                   jax.ShapeDtypeStruct((N_CHIPS, t, FFB), BF16), jax.ShapeDtypeStruct((t, D), BF16)],
        compiler_params=_cparams(("parallel",)),
    )(xm, g2, ga, ga, gb)


def _loss_and_grad(y, tgt, tm):
    t = y.shape[0]

    def body(y_ref, t_ref, l_ref, dy_ref):
        @pl.when(pl.program_id(0) == 0)
        def _():
            l_ref[...] = jnp.zeros_like(l_ref)

        e = y_ref[...] - t_ref[...]
        dy_ref[...] = e * (1.0 / D)
        s = jnp.sum(jnp.sum(e * e, axis=-1, keepdims=True), axis=0, keepdims=True)
        l_ref[...] += s * (0.5 / D)

    row = pl.BlockSpec((tm, D), lambda i: (i, 0))
    return pl.pallas_call(
        body, name="loss", grid=(t // tm,), in_specs=[row, row],
        out_specs=[pl.BlockSpec((8, 128), lambda i: (0, 0)), row],
        out_shape=[jax.ShapeDtypeStruct((8, 128), F32), jax.ShapeDtypeStruct((t, D), F32)],
        compiler_params=_cparams(("arbitrary",)),
    )(y, tgt)


def _ffn_bwd(dy, xm, g2, a, b, ga, gb, tm):
    t = dy.shape[0]

    def body(dy_ref, x_ref, g_ref, a_ref, b_ref, wg_ref, wu_ref, wd_ref, dx_ref, da_ref, db_ref, hm_ref, dg_ref):
        @pl.when(pl.program_id(0) == 0)
        def _():
            dg_ref[...] = jnp.zeros_like(dg_ref)

        dyv = dy_ref[...]
        dyb = dyv.astype(BF16)
        dh2 = jnp.zeros_like(dyv)
        for k in range(N_CHIPS):
            dhm = _nt(dyb, wd_ref[k])
            av = a_ref[k].astype(F32)
            bv = b_ref[k].astype(F32)
            sig = jax.nn.sigmoid(av)
            sil = av * sig
            hm_ref[k] = (sil * bv).astype(BF16)
            da = (dhm * bv * (sig * (1.0 + av * (1.0 - sig)))).astype(BF16)
            db = (dhm * sil).astype(BF16)
            da_ref[k] = da
            db_ref[k] = db
            dh2 = dh2 + _nt(da, wg_ref[k]) + _nt(db, wu_ref[k])
        r, xh = _rms_fwd(x_ref[...], 1.0 / D)
        dg_ref[...] += jnp.sum(dh2 * xh, axis=0, keepdims=True)
        dx_ref[...] = dyv + _rms_bwd(dh2, g_ref[...], xh, r, 1.0 / D)

    row = lambda w: pl.BlockSpec((tm, w), lambda i: (i, 0))
    blk = pl.BlockSpec((N_CHIPS, tm, FFB), lambda i: (0, i, 0))
    bsd = jax.ShapeDtypeStruct((N_CHIPS, t, FFB), BF16)
    return pl.pallas_call(
        body, name="ffn_bwd", grid=(t // tm,),
        in_specs=[row(D), row(D), _const_spec((1, D)), blk, blk] + _ffn_weight_specs(),
        out_specs=[row(D), blk, blk, blk, pl.BlockSpec((1, D), lambda i: (0, 0))],
        out_shape=[jax.ShapeDtypeStruct((t, D), F32), bsd, bsd, bsd, jax.ShapeDtypeStruct((1, D), F32)],
        compiler_params=_cparams(("arbitrary",)),
    )(dy, xm, g2, a, b, ga, ga, gb)


def _wgrad_blocks(a, b, tt, name):
    a_blocks = a.ndim == 3
    t = a.shape[1] if a_blocks else a.shape[0]
    rows = a.shape[2] if a_blocks else a.shape[1]
    cols = b.shape[1] if a_blocks else b.shape[2]

    def body(a_ref, b_ref, o_ref, acc_ref):
        s = pl.program_id(0)

        @pl.when(s == 0)
        def _():
            acc_ref[...] = jnp.zeros_like(acc_ref)

        if a_blocks:
            bv = b_ref[...].astype(BF16)
            for k in range(N_CHIPS):
                acc_ref[k] += _tn(a_ref[k], bv)
        else:
            av = a_ref[...].astype(BF16)
            for k in range(N_CHIPS):
                acc_ref[k] += _tn(av, b_ref[k])

        @pl.when(s == pl.num_programs(0) - 1)
        def _():
            o_ref[...] = acc_ref[...].astype(BF16)

    tok = lambda w: pl.BlockSpec((tt, w), lambda s: (s, 0))
    blk = lambda w: pl.BlockSpec((N_CHIPS, tt, w), lambda s: (0, s, 0))
    return pl.pallas_call(
        body, name=name, grid=(t // tt,),
        in_specs=[blk(rows), tok(cols)] if a_blocks else [tok(rows), blk(cols)],
        out_specs=pl.BlockSpec((N_CHIPS, rows, cols), lambda s: (0, 0, 0)),
        out_shape=jax.ShapeDtypeStruct((N_CHIPS, rows, cols), BF16),
        scratch_shapes=[pltpu.VMEM((N_CHIPS, rows, cols), F32)],
        compiler_params=_cparams(("arbitrary",)),
    )(a, b)


def _wgrad(a, b, tn, tt, name):
    t, k = a.shape
    n = b.shape[1]
    nsteps = t // tt

    def body(a_ref, b_ref, o_ref):
        @pl.when(pl.program_id(1) == 0)
        def _():
            o_ref[...] = jnp.zeros_like(o_ref)

        o_ref[...] += _tn(a_ref[...].astype(BF16), b_ref[...].astype(BF16))

    return pl.pallas_call(
        body, name=name, grid=(n // tn, nsteps),
        in_specs=[pl.BlockSpec((tt, k), lambda j, s: (s, 0)), pl.BlockSpec((tt, tn), lambda j, s: (s, j))],
        out_specs=pl.BlockSpec((k, tn), lambda j, s: (0, j)),
        out_shape=jax.ShapeDtypeStruct((k, n), F32),
        compiler_params=_cparams(("parallel", "arbitrary")),
    )(a, b)


def _mixer_bwd(dxm, proj, cw, gq, gk, sinks, gco, gao, wo, tq):
    t = proj.shape[0]
    nb = tq // BLK
    r8 = tq // 8
    nt = t // tq
    te = tq + 8
    kvw = 2 * NKV * HP

    def body(dx_ref, dxn_ref, p_ref, cgp_ref, hcp_ref, bgn_ref, cgn_ref, hcn_ref, kvp_ref, cw_ref, gq_ref,
             gk_ref, sk_ref, gco_ref, gao_ref, wo_ref,
             dpm_ref, dkvm_ref, dkvh_ref, dcw_ref, dgq_ref, dgk_ref, dsk_ref, dgco_ref, dgao_ref, acc_ref):
        i = pl.program_id(0)

        @pl.when(i == 0)
        def _():
            for r in (dcw_ref, dgq_ref, dgk_ref, dsk_ref, dgco_ref, dgao_ref):
                r[...] = jnp.zeros_like(r)

        acc_ref[...] = jnp.zeros_like(acc_ref)
        live_rows = jnp.where(i < nt - 1, te, tq)
        dxb = dx_ref[...].astype(BF16)
        dxe = jnp.concatenate([dxb, dxn_ref[...].astype(BF16)], axis=0)
        dcn = _nt(dxe, wo_ref[0:CC, :])
        bg = jnp.concatenate([p_ref[:, O_BG:O_BG + CC], bgn_ref[...]], axis=0)
        cg = jnp.concatenate([p_ref[:, O_CG:O_CG + CC], cgn_ref[...]], axis=0)
        hc = jnp.concatenate([p_ref[:, O_HC:O_HC + CC], hcn_ref[...]], axis=0)
        u = cg * hc
        up = jnp.where(i > 0, cgp_ref[...] * hcp_ref[...], 0.0)
        u1, u2 = _conv_taps(jnp.concatenate([up, u], axis=0), te)
        w0, w1, w2 = cw_ref[0:1, :], cw_ref[1:2, :], cw_ref[2:3, :]
        y = w0 * u2 + w1 * u1 + w2 * u
        co = bg * y
        rc, coh = _rms_fwd(co, 1.0 / CC)
        dco = _rms_bwd(dcn, gco_ref[...], coh, rc, 1.0 / CC)
        row_io = lax.broadcasted_iota(jnp.int32, (te, 1), 0)
        own = row_io < tq
        dgco_ref[...] += jnp.sum(jnp.where(own, dcn * coh, 0.0), axis=0, keepdims=True)
        dyc = jnp.where(row_io < live_rows, dco * bg, 0.0)
        dyo = jnp.where(own, dyc, 0.0)
        dcw_ref[0:1, :] += jnp.sum(dyo * u2, axis=0, keepdims=True)
        dcw_ref[1:2, :] += jnp.sum(dyo * u1, axis=0, keepdims=True)
        dcw_ref[2:3, :] += jnp.sum(dyo * u, axis=0, keepdims=True)
        dy1 = pltpu.roll(dyc, te - 1, 0)[0:tq]
        dy2 = pltpu.roll(dyc, te - 2, 0)[0:tq]
        du = w2 * dyc[0:tq] + w1 * dy1 + w0 * dy2
        dpm_ref[:, O_BG:O_BG + CC] = (dco[0:tq] * y[0:tq]).astype(BF16)
        dpm_ref[:, O_CG:O_CG + CC] = (du * hc[0:tq]).astype(BF16)
        dpm_ref[:, O_HC:O_HC + CC] = (du * cg[0:tq]).astype(BF16)
        dan = _nt(dxb, wo_ref[CC:MIXW, :])
        kraw = jnp.concatenate([kvp_ref[:, 0:NKV * HP], p_ref[:, O_K:O_K + NKV * HP]], axis=0)
        vraw = jnp.concatenate([kvp_ref[:, NKV * HP:], p_ref[:, O_V:O_V + NKV * HP]], axis=0)
        gqv, gkv = gq_ref[...], gk_ref[...]
        keys = _norm_keys(kraw, gkv)
        vb = [vraw[:, h * HP:(h + 1) * HP].astype(BF16) for h in range(NKV)]
        base_valid, c_io = _band_mask()
        lane = lax.broadcasted_iota(jnp.int32, (1, HP), 1)
        for b in range(nb):
            lo = jnp.where(i * nb + b == 0, BLK, 0)
            valid = base_valid & (c_io >= lo)
            band = slice(b * BLK, b * BLK + 2 * BLK)
            qs, prs, pss, outs = [], [], [], []
            for g in range(NQ):
                h = g // GRP
                qg = p_ref[b * BLK:(b + 1) * BLK, O_Q + g * HP:O_Q + (g + 1) * HP]
                rq, qh = _rms_fwd(qg, 1.0 / HD)
                qn = (qh * gqv).astype(BF16)
                pr, ps = _attn_probs(qn, keys[h][2][band], sk_ref[0, g], valid)
                qs.append((rq, qh, qn))
                prs.append(pr)
                pss.append(ps)
                outs.append(jnp.dot(pr.astype(BF16), vb[h][band], preferred_element_type=F32))
            ao = jnp.concatenate(outs, axis=1)
            ra, aoh = _rms_fwd(ao, 1.0 / (NQ * HD))
            danb = dan[b * BLK:(b + 1) * BLK]
            dgao_ref[...] += jnp.sum(danb * aoh, axis=0, keepdims=True)
            dao = _rms_bwd(danb, gao_ref[...], aoh, ra, 1.0 / (NQ * HD))
            dqs = []
            for h in range(NKV):
                dss, dobs = [], []
                for g in range(h * GRP, (h + 1) * GRP):
                    rq, qh, qn = qs[g]
                    dob = dao[:, g * HP:(g + 1) * HP].astype(BF16)
                    dp = _nt(dob, vb[h][band])
                    delta = jnp.sum(prs[g] * dp, axis=-1, keepdims=True)
                    dsb = (prs[g] * (dp - delta) * SCALE).astype(BF16)
                    dsk = -jnp.sum(pss[g] * delta, axis=0, keepdims=True)
                    dsk_ref[...] += jnp.where(lane == g, dsk, 0.0)
                    dqn = jnp.dot(dsb, keys[h][2][band], preferred_element_type=F32)
                    dgq_ref[...] += jnp.sum(dqn * qh, axis=0, keepdims=True)
                    dqs.append(_rms_bwd(dqn, gqv, qh, rq, 1.0 / HD).astype(BF16))
                    dss.append(dsb)
                    dobs.append(dob)
                grp = slice(h * GRP, (h + 1) * GRP)
                dkn = _tn(jnp.concatenate(dss, axis=0), jnp.concatenate([q[2] for q in qs[grp]], axis=0))
                dv = _tn(jnp.concatenate([p.astype(BF16) for p in prs[grp]], axis=0),
                         jnp.concatenate(dobs, axis=0))
                khat, rk = keys[h][0][band], keys[h][1][band]
                dgk_ref[...] += jnp.sum(dkn * khat, axis=0, keepdims=True)
                acc_ref[band, h * HP:(h + 1) * HP] += _rms_bwd(dkn, gkv, khat, rk, 1.0 / HD)
                acc_ref[band, (NKV + h) * HP:(NKV + h + 1) * HP] += dv
            dpm_ref[b * BLK:(b + 1) * BLK, O_Q:O_K] = jnp.concatenate(dqs, axis=1)
        dkvh_ref[...] = acc_ref[0:BLK, :]
        dkvm_ref[...] = acc_ref[BLK:, :]

    prev8 = lambda col: pl.BlockSpec((8, CC), lambda i: (jnp.maximum(i * r8 - 1, 0), col))
    next8 = lambda col: pl.BlockSpec((8, CC), lambda i: (jnp.minimum((i + 1) * r8, t // 8 - 1), col))
    small = lambda n: pl.BlockSpec((1, n), lambda i: (0, 0))
    return pl.pallas_call(
        body, name="mixer_bwd", grid=(nt,),
        in_specs=[
            pl.BlockSpec((tq, D), lambda i: (i, 0)),
            pl.BlockSpec((8, D), lambda i: (jnp.minimum((i + 1) * r8, t // 8 - 1), 0)),
            pl.BlockSpec((tq, NP), lambda i: (i, 0)),
            prev8(O_CG // CC), prev8(O_HC // CC),
            next8(O_BG // CC), next8(O_CG // CC), next8(O_HC // CC),
            pl.BlockSpec((BLK, kvw), lambda i: (jnp.maximum(i * nb - 1, 0), O_K // kvw)),
            _const_spec((8, CC)), _const_spec((1, HP)), _const_spec((1, HP)),
            pl.BlockSpec(memory_space=pltpu.SMEM),
            _const_spec((1, CC)), _const_spec((1, NQ * HP)), _const_spec((MIXW, D)),
        ],
        out_specs=[
            pl.BlockSpec((tq, NMAIN), lambda i: (i, 0)),
            pl.BlockSpec((tq, kvw), lambda i: (i, 0)),
            pl.BlockSpec((BLK, kvw), lambda i: (i, 0)),
            pl.BlockSpec((8, CC), lambda i: (0, 0)), small(HP), small(HP), small(HP), small(CC), small(NQ * HP),
        ],
        out_shape=[
            jax.ShapeDtypeStruct((t, NMAIN), BF16), jax.ShapeDtypeStruct((t, kvw), F32),
            jax.ShapeDtypeStruct((nt * BLK, kvw), F32),
            jax.ShapeDtypeStruct((8, CC), F32), jax.ShapeDtypeStruct((1, HP), F32), jax.ShapeDtypeStruct((1, HP), F32),
            jax.ShapeDtypeStruct((1, HP), F32), jax.ShapeDtypeStruct((1, CC), F32),
            jax.ShapeDtypeStruct((1, NQ * HP), F32),
        ],
        scratch_shapes=[pltpu.VMEM((tq + BLK, kvw), F32)],
        compiler_params=_cparams(("arbitrary",)),
    )(dxm, dxm, proj, proj, proj, proj, proj, proj, proj, cw, gq, gk, sinks, gco, gao, wo)


def _inproj_bwd(dpm, dkv, wp, x, g1, dxm, tm):
    t = x.shape[0]
    kvw = 2 * NKV * HP

    def body(dp_ref, dk_ref, w_ref, x_ref, g_ref, dxm_ref, dx_ref, dg_ref):
        @pl.when(pl.program_id(0) == 0)
        def _():
            dg_ref[...] = jnp.zeros_like(dg_ref)

        dh = _nt(dp_ref[...], w_ref[:, 0:NMAIN]) + _nt(dk_ref[...], w_ref[:, NMAIN:NP])
        r, xh = _rms_fwd(x_ref[...], 1.0 / D)
        dg_ref[...] += jnp.sum(dh * xh, axis=0, keepdims=True)
        dx_ref[...] = dxm_ref[...] + _rms_bwd(dh, g_ref[...], xh, r, 1.0 / D)

    row = lambda w: pl.BlockSpec((tm, w), lambda i: (i, 0))
    return pl.pallas_call(
        body, name="inproj_bwd", grid=(t // tm,),
        in_specs=[row(NMAIN), row(kvw), _const_spec((D, NP)), row(D), _const_spec((1, D)), row(D)],
        out_specs=[row(D), pl.BlockSpec((1, D), lambda i: (0, 0))],
        out_shape=[jax.ShapeDtypeStruct((t, D), F32), jax.ShapeDtypeStruct((1, D), F32)],
        compiler_params=_cparams(("arbitrary",)),
    )(dpm, dkv, wp, x, g1, dxm)


def _rows_tile(rows):
    for cand in (512, 256, 128, 64, 32, 16, 8):
        if rows % cand == 0:
            return cand
    return rows


def _presum_halves(gs, theirs, core):
    outs = []
    for n, (ga, ta) in enumerate(zip(gs, theirs)):
        _, hr, cols = ta.shape

        def body(c_ref, g_ref, t_ref, o_ref):
            o_ref[...] = (g_ref[...].astype(F32) + t_ref[...].astype(F32)).astype(BF16)

        half = pl.BlockSpec((None, hr, cols), lambda k, c_ref: (k, 0, 0))
        outs.append(pl.pallas_call(
            body, name=f"presum_{n}",
            grid_spec=pltpu.PrefetchScalarGridSpec(
                num_scalar_prefetch=1, grid=(N_CHIPS,),
                in_specs=[pl.BlockSpec((None, hr, cols), lambda k, c_ref: (k, c_ref[0], 0)), half],
                out_specs=half),
            out_shape=jax.ShapeDtypeStruct(ta.shape, BF16), compiler_params=_cparams(("parallel",)),
        )(core, ga, ta))
    return outs


def _sum_chips(cs, name):
    outs = []
    for n, ca in enumerate(cs):
        _, rows, cols = ca.shape
        tr = _rows_tile(rows)

        def body(c_ref, o_ref):
            acc = c_ref[0].astype(F32)
            for j in range(1, N_CHIPS):
                acc = acc + c_ref[j].astype(F32)
            o_ref[...] = acc

        outs.append(pl.pallas_call(
            body, name=f"{name}_{n}", grid=(rows // tr,),
            in_specs=[pl.BlockSpec((N_CHIPS, tr, cols), lambda i: (0, i, 0))],
            out_specs=pl.BlockSpec((tr, cols), lambda i: (i, 0)),
            out_shape=jax.ShapeDtypeStruct((rows, cols), F32), compiler_params=_cparams(("parallel",)),
        )(ca))
    return outs


def _adamw(w, g, m, v, name):
    rows, cols = w.shape
    tr = _rows_tile(rows)
    c1 = 1.0 - ADAM_B1 ** ADAM_STEP
    c2 = 1.0 - ADAM_B2 ** ADAM_STEP

    def body(w_ref, g_ref, m_ref, v_ref, d_ref, mo_ref, vo_ref):
        gv = g_ref[...]
        mn = ADAM_B1 * m_ref[...] + (1.0 - ADAM_B1) * gv
        vn = ADAM_B2 * v_ref[...] + (1.0 - ADAM_B2) * (gv * gv)
        mo_ref[...] = mn
        vo_ref[...] = vn
        d_ref[...] = -ADAM_LR * ((mn / c1) / (jnp.sqrt(vn / c2) + ADAM_EPS) + ADAM_WD * w_ref[...])

    spec = pl.BlockSpec((tr, cols), lambda i: (i, 0))
    sds = jax.ShapeDtypeStruct((rows, cols), F32)
    return pl.pallas_call(
        body, name=name, grid=(rows // tr,), in_specs=[spec] * 4, out_specs=[spec] * 3, out_shape=[sds] * 3,
        compiler_params=_cparams(("parallel",)),
    )(w, g, m, v)


def _place():
    x, y, c = lax.axis_index("x"), lax.axis_index("y"), lax.axis_index("c")
    chips = [(1 - x, y), (x, 1 - y), (1 - x, 1 - y)]
    return x, y, c, chips


ANY = pl.BlockSpec(memory_space=pl.ANY)
DMA_ROWS = 64


def _pieces(shape):
    rows = shape[-2]
    step = DMA_ROWS if rows % DMA_ROWS == 0 else rows
    lead = [()]
    for n in shape[:-2]:
        lead = [i + (k,) for i in lead for k in range(n)]
    return [i + (pl.ds(r0, step),) for i in lead for r0 in range(0, rows, step)]


def _start_pieces(make, src, dst):
    for idx in _pieces(src.shape):
        make(src.at[idx], dst.at[idx]).start()


def _gather_layer(blocks, layer):
    nw = len(blocks)

    def body(*refs):
        srcs, outs = refs[:nw], refs[nw:2 * nw]
        ssem, rsem, fssem, frsem = refs[2 * nw:]
        x, y, c, chips = _place()
        kme = 2 * x + y

        def plane(j, w, to):
            return lambda s, d: pltpu.make_async_remote_copy(
                src_ref=s, dst_ref=d, send_sem=ssem.at[j, w], recv_sem=rsem.at[j, w], device_id=to,
                device_id_type=MESH)

        def passed(j, w):
            return lambda s, d: pltpu.make_async_remote_copy(
                src_ref=s, dst_ref=d, send_sem=fssem.at[j, w], recv_sem=frsem.at[j, w],
                device_id=(x, y, 1 - c), device_id_type=MESH)

        @pl.when(c == layer)
        def _():
            for j, (px, py) in enumerate(chips):
                for w in range(nw):
                    _start_pieces(plane(j, w, (px, py, c)), srcs[w], outs[w].at[kme])
            for j, (px, py) in enumerate(chips):
                for w in range(nw):
                    got = outs[w].at[2 * px + py]
                    plane(j, w, (px, py, c))(got, got).wait_recv()
                    _start_pieces(passed(j, w), got, got)
            for j, (px, py) in enumerate(chips):
                for w in range(nw):
                    got = outs[w].at[2 * px + py]
                    plane(j, w, (px, py, c))(got, got).wait_send()
                    passed(j, w)(got, got).wait_send()

        @pl.when(c != layer)
        def _():
            for j, (px, py) in enumerate(chips):
                for w in range(nw):
                    got = outs[w].at[2 * px + py]
                    passed(j, w)(got, got).wait_recv()

    return pl.pallas_call(
        body, name=f"gather_layer{layer}", in_specs=[ANY] * nw, out_specs=[ANY] * nw,
        out_shape=[jax.ShapeDtypeStruct((N_CHIPS,) + b.shape, b.dtype) for b in blocks],
        scratch_shapes=[pltpu.SemaphoreType.DMA((3, nw))] * 4,
        compiler_params=_cparams(has_side_effects=True),
    )(*blocks)


def _swap_halves(gs):
    nw = len(gs)

    def body(*refs):
        srcs, theirs = refs[:nw], refs[nw:2 * nw]
        ssem, rsem = refs[2 * nw:]
        x, y, c, _ = _place()

        def give(w):
            return lambda s, d: pltpu.make_async_remote_copy(
                src_ref=s, dst_ref=d, send_sem=ssem.at[w], recv_sem=rsem.at[w], device_id=(x, y, 1 - c),
                device_id_type=MESH)

        for w in range(nw):
            hr = theirs[w].shape[1]
            _start_pieces(give(w), srcs[w].at[:, pl.ds((1 - c) * hr, hr)], theirs[w])
        for w in range(nw):
            give(w)(theirs[w], theirs[w]).wait()

    return pl.pallas_call(
        body, name="swap_halves", in_specs=[ANY] * nw, out_specs=[ANY] * nw,
        out_shape=[jax.ShapeDtypeStruct((g.shape[0], g.shape[1] // 2, g.shape[2]), g.dtype) for g in gs],
        scratch_shapes=[pltpu.SemaphoreType.DMA((nw,))] * 2,
        compiler_params=_cparams(has_side_effects=True),
    )(*gs)


def _scatter_chips(ps):
    nw = len(ps)

    def body(*refs):
        srcs, outs = refs[:nw], refs[nw:2 * nw]
        ssem, rsem = refs[2 * nw:]
        x, y, c, chips = _place()
        kme = 2 * x + y

        def give(j, w, to):
            return lambda s, d: pltpu.make_async_remote_copy(
                src_ref=s, dst_ref=d, send_sem=ssem.at[j, w], recv_sem=rsem.at[j, w], device_id=to,
                device_id_type=MESH)

        for j, (px, py) in enumerate(chips):
            for w in range(nw):
                _start_pieces(give(j, w, (px, py, c)), srcs[w].at[2 * px + py], outs[w].at[kme])
        for j, (px, py) in enumerate(chips):
            for w in range(nw):
                got = outs[w].at[2 * px + py]
                give(j, w, (px, py, c))(got, got).wait_recv()
        for j, (px, py) in enumerate(chips):
            for w in range(nw):
                sent = srcs[w].at[2 * px + py]
                give(j, w, (px, py, c))(sent, sent).wait_send()

    return pl.pallas_call(
        body, name="scatter_chips", in_specs=[ANY] * nw, out_specs=[ANY] * nw,
        out_shape=[jax.ShapeDtypeStruct(p.shape, p.dtype) for p in ps],
        scratch_shapes=[pltpu.SemaphoreType.DMA((3, nw)), pltpu.SemaphoreType.DMA((3, nw))],
        compiler_params=_cparams(has_side_effects=True),
    )(*ps)


def _swap_siblings(rs):
    nw = len(rs)

    def body(*refs):
        srcs, outs = refs[:nw], refs[nw:2 * nw]
        ssem, rsem = refs[2 * nw:]
        x, y, c, _ = _place()

        def give(w):
            return lambda s, d: pltpu.make_async_remote_copy(
                src_ref=s, dst_ref=d, send_sem=ssem.at[w], recv_sem=rsem.at[w], device_id=(x, y, 1 - c),
                device_id_type=MESH)

        for w in range(nw):
            _start_pieces(give(w), srcs[w], outs[w])
        for w in range(nw):
            give(w)(srcs[w], outs[w]).wait()

    return pl.pallas_call(
        body, name="swap_siblings", in_specs=[ANY] * nw, out_specs=[ANY] * nw,
        out_shape=[jax.ShapeDtypeStruct(r.shape, r.dtype) for r in rs],
        scratch_shapes=[pltpu.SemaphoreType.DMA((nw,))] * 2,
        compiler_params=_cparams(has_side_effects=True),
    )(*rs)


def _allreduce_small(v):
    rows = v.shape[0]

    def body(v_ref, o_ref, buf, ssem, rsem):
        x, y, c, _ = _place()
        me = 4 * x + 2 * y + c
        buf[me] = v_ref[...]
        sends = []
        for r in range(1, 8):
            peer = (x ^ (r >> 2), y ^ ((r >> 1) & 1), c ^ (r & 1))
            cp = pltpu.make_async_remote_copy(
                src_ref=v_ref, dst_ref=buf.at[me], send_sem=ssem.at[r - 1], recv_sem=rsem.at[r - 1],
                device_id=peer, device_id_type=MESH)
            cp.start()
            sends.append(cp)
        for r in range(1, 8):
            src = me ^ r
            pltpu.make_async_remote_copy(
                src_ref=v_ref, dst_ref=buf.at[src], send_sem=ssem.at[r - 1], recv_sem=rsem.at[r - 1],
                device_id=(x, y, c), device_id_type=MESH).wait_recv()
        for cp in sends:
            cp.wait_send()
        acc = buf[0]
        for d in range(1, 8):
            acc = acc + buf[d]
        o_ref[...] = acc

    vm = pl.BlockSpec(memory_space=pltpu.VMEM)
    return pl.pallas_call(
        body, name="allreduce_small", in_specs=[vm], out_specs=vm,
        out_shape=jax.ShapeDtypeStruct(v.shape, F32),
        scratch_shapes=[pltpu.VMEM((8, rows, 128), F32), pltpu.SemaphoreType.DMA((7,)),
                        pltpu.SemaphoreType.DMA((7,))],
        compiler_params=_cparams(has_side_effects=True),
    )(v)


def _pad_heads(w, n_heads, axis):
    shp = w.shape
    w = w.reshape(shp[:axis] + (n_heads, HD) + shp[axis + 1:])
    pad = [(0, 0)] * w.ndim
    pad[axis + 1] = (0, HP - HD)
    w = jnp.pad(w, pad)
    return w.reshape(shp[:axis] + (n_heads * HP,) + shp[axis + 1:])


def _strip_heads(w, n_heads, axis):
    shp = w.shape
    w = w.reshape(shp[:axis] + (n_heads, HP) + shp[axis + 1:])
    w = lax.slice_in_dim(w, 0, HD, axis=axis + 1)
    return w.reshape(shp[:axis] + (n_heads * HD,) + shp[axis + 1:])


def _unshard_cols(g4):
    k, r, c = g4.shape
    return jnp.transpose(g4, (1, 0, 2)).reshape(r, k * c)


def _shard_cols(w):
    r, n = w.shape
    return jnp.transpose(w.reshape(r, N_CHIPS, n // N_CHIPS), (1, 0, 2))


def _pad_win(win):
    parts = [win[:, :3 * CC], _pad_heads(win[:, 3 * CC:3 * CC + NQ * HD], NQ, 1),
             _pad_heads(win[:, 3 * CC + NQ * HD:3 * CC + (NQ + NKV) * HD], NKV, 1),
             _pad_heads(win[:, 3 * CC + (NQ + NKV) * HD:], NKV, 1)]
    return jnp.concatenate(parts, axis=1)


def _strip_win(gp):
    parts = [gp[:, :3 * CC], _strip_heads(gp[:, O_Q:O_K], NQ, 1), _strip_heads(gp[:, O_K:O_V], NKV, 1),
             _strip_heads(gp[:, O_V:], NKV, 1)]
    return jnp.concatenate(parts, axis=1)


def _count(shape):
    n = 1
    for s in shape:
        n *= s
    return n


def _pack_rows(arrs):
    flat = [jnp.pad(a.reshape(-1), (0, (-_count(a.shape)) % 128)) for a in arrs]
    v = jnp.concatenate(flat)
    rows = -(-v.shape[0] // (8 * 128)) * 8
    return jnp.pad(v, (0, rows * 128 - v.shape[0])).reshape(rows, 128)


def kernel(x, norm1_g, w_in, conv_w, q_norm_g, k_norm_g, sinks, conv_out_g, attn_out_g, w_o, norm2_g, w_gate, w_up, w_down, loss_target, m_norm1_g, m_w_in, m_conv_w, m_q_norm_g, m_k_norm_g, m_sinks, m_conv_out_g, m_attn_out_g, m_w_o, m_norm2_g, m_w_gate, m_w_up, m_w_down, v_norm1_g, v_w_in, v_conv_w, v_q_norm_g, v_k_norm_g, v_sinks, v_conv_out_g, v_attn_out_g, v_w_o, v_norm2_g, v_w_gate, v_w_up, v_w_down):
    depth = w_in.shape[0]
    t = x.shape[1]
    xs = x.reshape(t, D)
    tgt = loss_target.reshape(t, D)
    xi, yi = lax.axis_index("x"), lax.axis_index("y")
    kme = 2 * xi + yi
    tm = min(512, t)
    tq = min(256, t)
    tf = min(256, t)

    cwp = jnp.pad(conv_w.reshape(depth * 3, CC // N_CHIPS), ((0, 8 - depth * 3), (0, 0)))
    gathered = []
    for l in range(depth):
        own = [jnp.concatenate([w_gate[l], w_up[l]], axis=0).astype(BF16),
               jnp.concatenate([w_down[l], w_o[l]], axis=0).astype(BF16), w_in[l].astype(BF16)]
        own += [cwp] if l == 0 else []
        gathered.append([lax.dynamic_update_index_in_dim(g, o, kme, 0)
                         for g, o in zip(_gather_layer(own, l), own)])
    cw_full = _unshard_cols(gathered[0][3])[:depth * 3].reshape(depth, 3, CC)
    layers = []
    for l in range(depth):
        ga, gb, gc = gathered[l][:3]
        wp = _pad_win(_unshard_cols(gc))
        wo = gb[:, FFB:].reshape(D, D)
        wo = jnp.concatenate([wo[:CC], _pad_heads(wo[CC:], NQ, 0)], axis=0)
        layers.append(dict(
            wp=wp, wo=wo, ga=ga, gb=gb, cw=jnp.pad(cw_full[l], ((0, 5), (0, 0))),
            g1=norm1_g[l].reshape(1, D), g2=norm2_g[l].reshape(1, D),
            gq=jnp.pad(q_norm_g[l], (0, HP - HD)).reshape(1, HP), gk=jnp.pad(k_norm_g[l], (0, HP - HD)).reshape(1, HP),
            sk=sinks[l].reshape(1, NQ), gco=conv_out_g[l].reshape(1, CC),
            gao=_pad_heads(attn_out_g[l], NQ, 0).reshape(1, NQ * HP)))

    saved = []
    cur = xs
    for l in range(depth):
        p = layers[l]
        proj, h = _inproj_fwd(cur, p["g1"], p["wp"], tm)
        xm, mix = _mixer_fwd(proj, cur, p["cw"], p["gq"], p["gk"], p["sk"], p["gco"], p["gao"], p["wo"], tq)
        xo, a, b, h2 = _ffn_fwd(xm, p["g2"], p["ga"], p["gb"], tf)
        saved.append(dict(x=cur, proj=proj, h=h, xm=xm, mix=mix, a=a, b=b, h2=h2))
        cur = xo
    lpart, dy = _loss_and_grad(cur, tgt, tm)
    loss = lax.psum(lpart[0, 0], ("x", "y", "c"))

    nt = t // tq
    ci = lax.axis_index("c")
    core = ci.reshape(1).astype(jnp.int32)
    rbig = [None] * depth
    gsmall = [None] * depth
    for l in reversed(range(depth)):
        p, s = layers[l], saved[l]
        dxm, da, db, hm, dg2 = _ffn_bwd(dy, s["xm"], p["g2"], s["a"], s["b"], p["ga"], p["gb"], tf)
        g_wg = _wgrad_blocks(s["h2"], da, tm, "wgrad_gate")
        g_wu = _wgrad_blocks(s["h2"], db, tm, "wgrad_up")
        g_wd = _wgrad_blocks(hm, dy, tm, "wgrad_down")
        dpm, dkvm, dkvh, dcw, dgq, dgk, dsk, dgco, dgao = _mixer_bwd(
            dxm, s["proj"], p["cw"], p["gq"], p["gk"], p["sk"], p["gco"], p["gao"], p["wo"], tq)
        g_wot = _wgrad(dxm, s["mix"], MIXW, tm, "wgrad_o")
        kvw = dkvm.shape[1]
        halo = jnp.concatenate([dkvh.reshape(nt, BLK, kvw)[1:], jnp.zeros((1, BLK, kvw), F32)], axis=0)
        halo = jnp.pad(halo, ((0, 0), (tq - BLK, 0), (0, 0)))
        dkv = (dkvm.reshape(nt, tq, kvw) + halo).reshape(t, kvw).astype(BF16)
        dx, dg1 = _inproj_bwd(dpm, dkv, p["wp"], s["x"], p["g1"], dxm, tm)
        g_wpm = _wgrad(s["h"], dpm, NMAIN // 2, tm, "wgrad_in_main")
        g_wpk = _wgrad(s["h"], dkv, kvw, tm, "wgrad_in_kv")
        dy = dx
        g_in = _strip_win(jnp.concatenate([g_wpm, g_wpk], axis=1))
        g_ot = jnp.concatenate([g_wot[:, :CC], _strip_heads(g_wot[:, CC:], NQ, 1)], axis=1)
        gsmall[l] = dict(g1=dg1, cw=dcw[:3], gq=dgq[0, :HD], gk=dgk[0, :HD], sk=dsk[0, :NQ], gco=dgco,
                         gao=_strip_heads(dgao.reshape(NQ * HP), NQ, 0), g2=dg2)
        gs = [_shard_cols(g_in.astype(BF16)), _shard_cols(g_ot.astype(BF16)), g_wg, g_wu, g_wd]
        ps = _presum_halves(gs, _swap_halves(gs), core)
        cs = [lax.dynamic_update_index_in_dim(got, lax.dynamic_index_in_dim(q, kme, 0, keepdims=False), kme, 0)
              for got, q in zip(_scatter_chips(ps), ps)]
        r_mine = _sum_chips(cs, "chipsum")
        r_theirs = _swap_siblings(r_mine)
        r_in, r_ot, r_g, r_u, r_d = [
            jnp.where(ci == 0, jnp.concatenate([a, b], axis=0), jnp.concatenate([b, a], axis=0))
            for a, b in zip(r_mine, r_theirs)]
        rbig[l] = [r_in, r_ot.T, r_g, r_u, r_d]
    grad_x = dy.reshape(x.shape)
    g_big = [jnp.stack([rbig[l][w] for l in range(depth)]) for w in range(5)]

    small_shapes = dict(g1=(D,), cw=(3, CC), gq=(HD,), gk=(HD,), sk=(NQ,), gco=(CC,), gao=(NQ * HD,), g2=(D,))
    red = _allreduce_small(_pack_rows([gsmall[l][n] for l in range(depth) for n in small_shapes])).reshape(-1)
    red_small, offs = {n: [] for n in small_shapes}, 0
    for l in range(depth):
        for n, shp in small_shapes.items():
            cnt = _count(shp)
            red_small[n].append(red[offs:offs + cnt].reshape(shp))
            offs += -(-cnt // 128) * 128
    g_small = {n: jnp.stack(v) for n, v in red_small.items()}
    g_cw = lax.dynamic_slice_in_dim(g_small["cw"], kme * (CC // N_CHIPS), CC // N_CHIPS, axis=2)

    weights = [norm1_g, w_in, conv_w, q_norm_g, k_norm_g, sinks, conv_out_g, attn_out_g, w_o, norm2_g, w_gate,
               w_up, w_down]
    moms = [m_norm1_g, m_w_in, m_conv_w, m_q_norm_g, m_k_norm_g, m_sinks, m_conv_out_g, m_attn_out_g, m_w_o,
            m_norm2_g, m_w_gate, m_w_up, m_w_down]
    vars_ = [v_norm1_g, v_w_in, v_conv_w, v_q_norm_g, v_k_norm_g, v_sinks, v_conv_out_g, v_attn_out_g, v_w_o,
             v_norm2_g, v_w_gate, v_w_up, v_w_down]
    grads = [g_small["g1"], g_big[0], g_cw, g_small["gq"], g_small["gk"], g_small["sk"], g_small["gco"],
             g_small["gao"], g_big[1], g_small["g2"], g_big[2], g_big[3], g_big[4]]
    n_w = len(weights)
    big_idx = [1, 8, 10, 11, 12]
    small_idx = [n for n in range(n_w) if n not in big_idx]
    deltas, new_m, new_v = [None] * n_w, [None] * n_w, [None] * n_w
    for n in big_idx:
        shp = weights[n].shape
        two = [a3.reshape(shp[0] * shp[1], shp[2]) for a3 in (weights[n], grads[n], moms[n], vars_[n])]
        res = _adamw(*two, f"adamw_{n}")
        deltas[n], new_m[n], new_v[n] = [r.reshape(shp) for r in res]
    res = _adamw(*[_pack_rows([arrs[n] for n in small_idx]) for arrs in (weights, grads, moms, vars_)],
                 "adamw_small")
    offs = 0
    for n in small_idx:
        shp = weights[n].shape
        cnt = _count(shp)
        deltas[n], new_m[n], new_v[n] = [r.reshape(-1)[offs:offs + cnt].reshape(shp) for r in res]
        offs += -(-cnt // 128) * 128
    return (loss, grad_x, *grads, *deltas, *new_m, *new_v)
```

```python
import functools

import jax
import jax.numpy as jnp
from jax import lax
from jax.experimental import pallas as pl
from jax.experimental.pallas import tpu as pltpu
from jax.experimental.pallas import tpu_sc as plsc

F32 = jnp.float32
BF16 = jnp.bfloat16

D = 1024
CC = 512
NQ = 8
NKV = 2
HD = 64
HP = 128
GRP = NQ // NKV
FF = 2816
FFB = FF // 4
BLK = 128
EPS = 1e-6
NEG = -1e30
SCALE = HD ** -0.5
O_BG, O_CG, O_HC, O_Q = 0, CC, 2 * CC, 3 * CC
O_K = O_Q + NQ * HP
O_V = O_K + NKV * HP
NP = O_V + NKV * HP
NMAIN = O_K
MIXW = CC + NQ * HP
N_CHIPS = 4
VMEM_LIMIT = 56 * 1024 * 1024
MESH = pl.DeviceIdType.MESH

ADAM_LR, ADAM_B1, ADAM_B2, ADAM_EPS, ADAM_WD, ADAM_STEP = 0.001, 0.9, 0.999, 1e-08, 0.01, 10


def _cparams(sem=None, **kw):
    if sem is not None:
        kw["dimension_semantics"] = sem
    return pltpu.CompilerParams(vmem_limit_bytes=VMEM_LIMIT, **kw)


def _const_spec(shape):
    nd = len(shape)
    return pl.BlockSpec(shape, lambda *_: (0,) * nd, pipeline_mode=pl.Buffered(1))


def _nt(a, b):
    return lax.dot_general(a, b, (((1,), (1,)), ((), ())), preferred_element_type=F32)


def _tn(a, b):
    return lax.dot_general(a, b, (((0,), (0,)), ((), ())), preferred_element_type=F32)


def _rms_fwd(x, inv_n):
    r = lax.rsqrt(jnp.sum(x * x, axis=-1, keepdims=True) * inv_n + EPS)
    return r, x * r


def _rms_bwd(dy, g, xh, r, inv_n):
    dxh = dy * g
    return r * (dxh - xh * (jnp.sum(dxh * xh, axis=-1, keepdims=True) * inv_n))


def _inproj_fwd(x, g1, wp, tm):
    t = x.shape[0]

    def body(x_ref, g_ref, w_ref, p_ref, h_ref):
        _, xh = _rms_fwd(x_ref[...], 1.0 / D)
        h = (xh * g_ref[...]).astype(BF16)
        h_ref[...] = h
        p_ref[...] = jnp.dot(h, w_ref[...], preferred_element_type=F32)

    return pl.pallas_call(
        body, name="inproj_fwd", grid=(t // tm,),
        in_specs=[pl.BlockSpec((tm, D), lambda i: (i, 0)), _const_spec((1, D)), _const_spec((D, NP))],
        out_specs=[pl.BlockSpec((tm, NP), lambda i: (i, 0)), pl.BlockSpec((tm, D), lambda i: (i, 0))],
        out_shape=[jax.ShapeDtypeStruct((t, NP), F32), jax.ShapeDtypeStruct((t, D), BF16)],
        compiler_params=_cparams(("parallel",)),
    )(x, g1, wp)


def _band_mask():
    r_io = lax.broadcasted_iota(jnp.int32, (BLK, 2 * BLK), 0)
    c_io = lax.broadcasted_iota(jnp.int32, (BLK, 2 * BLK), 1)
    return (c_io > r_io) & (c_io <= r_io + BLK), c_io


def _conv_taps(uf, n):
    u1 = pltpu.roll(uf, 1, 0)[8:8 + n]
    u2 = pltpu.roll(uf, 2, 0)[8:8 + n]
    return u1, u2


def _attn_probs(qn, kband, sink, valid):
    s = _nt(qn, kband) * SCALE
    s = jnp.where(valid, s, NEG)
    m = jnp.maximum(jnp.max(s, axis=-1, keepdims=True), sink)
    p = jnp.exp(s - m)
    es = jnp.exp(sink - m)
    inv = 1.0 / (jnp.sum(p, axis=-1, keepdims=True) + es)
    return p * inv, es * inv


def _norm_keys(kraw, gk):
    out = []
    for h in range(NKV):
        kh = kraw[:, h * HP:(h + 1) * HP]
        rk, khat = _rms_fwd(kh, 1.0 / HD)
        out.append((khat, rk, (khat * gk).astype(BF16)))
    return out


def _mixer_fwd(proj, x, cw, gq, gk, sinks, gco, gao, wo, tq):
    t = proj.shape[0]
    nb = tq // BLK
    r8 = tq // 8

    def body(p_ref, cgp_ref, hcp_ref, kvp_ref, x_ref, cw_ref, gq_ref, gk_ref, sk_ref, gco_ref, gao_ref,
             wo_ref, xm_ref, mix_ref):
        i = pl.program_id(0)
        cg = p_ref[:, O_CG:O_CG + CC]
        hc = p_ref[:, O_HC:O_HC + CC]
        u = cg * hc
        up = jnp.where(i > 0, cgp_ref[...] * hcp_ref[...], 0.0)
        u1, u2 = _conv_taps(jnp.concatenate([up, u], axis=0), tq)
        y = cw_ref[0:1, :] * u2 + cw_ref[1:2, :] * u1 + cw_ref[2:3, :] * u
        co = p_ref[:, O_BG:O_BG + CC] * y
        _, coh = _rms_fwd(co, 1.0 / CC)
        cn = coh * gco_ref[...]
        kraw = jnp.concatenate([kvp_ref[:, 0:NKV * HP], p_ref[:, O_K:O_K + NKV * HP]], axis=0)
        vraw = jnp.concatenate([kvp_ref[:, NKV * HP:], p_ref[:, O_V:O_V + NKV * HP]], axis=0)
        keys = _norm_keys(kraw, gk_ref[...])
        vb = [vraw[:, h * HP:(h + 1) * HP].astype(BF16) for h in range(NKV)]
        base_valid, c_io = _band_mask()
        rows = []
        for b in range(nb):
            lo = jnp.where(i * nb + b == 0, BLK, 0)
            valid = base_valid & (c_io >= lo)
            outs = []
            for g in range(NQ):
                h = g // GRP
                qg = p_ref[b * BLK:(b + 1) * BLK, O_Q + g * HP:O_Q + (g + 1) * HP]
                _, qh = _rms_fwd(qg, 1.0 / HD)
                qn = (qh * gq_ref[...]).astype(BF16)
                pr, _ = _attn_probs(qn, keys[h][2][b * BLK:b * BLK + 2 * BLK], sk_ref[0, g], valid)
                outs.append(jnp.dot(pr.astype(BF16), vb[h][b * BLK:b * BLK + 2 * BLK],
                                    preferred_element_type=F32))
            rows.append(jnp.concatenate(outs, axis=1))
        ao = jnp.concatenate(rows, axis=0)
        _, aoh = _rms_fwd(ao, 1.0 / (NQ * HD))
        an = aoh * gao_ref[...]
        mix = jnp.concatenate([cn, an], axis=1).astype(BF16)
        mix_ref[...] = mix
        xm_ref[...] = x_ref[...] + jnp.dot(mix, wo_ref[...], preferred_element_type=F32)

    prev8 = lambda col: pl.BlockSpec((8, CC), lambda i: (jnp.maximum(i * r8 - 1, 0), col))
    return pl.pallas_call(
        body, name="mixer_fwd", grid=(t // tq,),
        in_specs=[
            pl.BlockSpec((tq, NP), lambda i: (i, 0)),
            prev8(O_CG // CC), prev8(O_HC // CC),
            pl.BlockSpec((BLK, 2 * NKV * HP), lambda i: (jnp.maximum(i * nb - 1, 0), O_K // (2 * NKV * HP))),
            pl.BlockSpec((tq, D), lambda i: (i, 0)),
            _const_spec((8, CC)), _const_spec((1, HP)), _const_spec((1, HP)),
            pl.BlockSpec(memory_space=pltpu.SMEM),
            _const_spec((1, CC)), _const_spec((1, NQ * HP)), _const_spec((MIXW, D)),
        ],
        out_specs=[pl.BlockSpec((tq, D), lambda i: (i, 0)), pl.BlockSpec((tq, MIXW), lambda i: (i, 0))],
        out_shape=[jax.ShapeDtypeStruct((t, D), F32), jax.ShapeDtypeStruct((t, MIXW), BF16)],
        compiler_params=_cparams(("parallel",)),
    )(proj, proj, proj, proj, x, cw, gq, gk, sinks, gco, gao, wo)


def _ffn_weight_specs():
    return [pl.BlockSpec((N_CHIPS, D, FFB), lambda i: (0, 0, 0), pipeline_mode=pl.Buffered(1)),
            pl.BlockSpec((N_CHIPS, D, FFB), lambda i: (0, 1, 0), pipeline_mode=pl.Buffered(1)),
            pl.BlockSpec((N_CHIPS, FFB, D), lambda i: (0, 0, 0), pipeline_mode=pl.Buffered(1))]


def _ffn_fwd(xm, g2, ga, gb, tm):
    t = xm.shape[0]

    def body(x_ref, g_ref, wg_ref, wu_ref, wd_ref, xo_ref, a_ref, b_ref, h2_ref):
        xv = x_ref[...]
        _, xh = _rms_fwd(xv, 1.0 / D)
        h2 = (xh * g_ref[...]).astype(BF16)
        h2_ref[...] = h2
        acc = xv
        for k in range(N_CHIPS):
            a = jnp.dot(h2, wg_ref[k], preferred_element_type=F32)
            b = jnp.dot(h2, wu_ref[k], preferred_element_type=F32)
            a_ref[k] = a.astype(BF16)
            b_ref[k] = b.astype(BF16)
            hm = (a * jax.nn.sigmoid(a) * b).astype(BF16)
            acc = acc + jnp.dot(hm, wd_ref[k], preferred_element_type=F32)
        xo_ref[...] = acc

    row = lambda w: pl.BlockSpec((tm, w), lambda i: (i, 0))
    blk = pl.BlockSpec((N_CHIPS, tm, FFB), lambda i: (0, i, 0))
    return pl.pallas_call(
        body, name="ffn_fwd", grid=(t // tm,),
        in_specs=[row(D), _const_spec((1, D))] + _ffn_weight_specs(),
        out_specs=[row(D), blk, blk, row(D)],
        out_shape=[jax.ShapeDtypeStruct((t, D), F32), jax.ShapeDtypeStruct((N_CHIPS, t, FFB), BF16),
                   jax.ShapeDtypeStruct((N_CHIPS, t, FFB), BF16), jax.ShapeDtypeStruct((t, D), BF16)],
        compiler_params=_cparams(("parallel",)),
    )(xm, g2, ga, ga, gb)


def _loss_and_grad(y, tgt, tm):
    t = y.shape[0]

    def body(y_ref, t_ref, l_ref, dy_ref):
        @pl.when(pl.program_id(0) == 0)
        def _():
            l_ref[...] = jnp.zeros_like(l_ref)

        e = y_ref[...] - t_ref[...]
        dy_ref[...] = e * (1.0 / D)
        s = jnp.sum(jnp.sum(e * e, axis=-1, keepdims=True), axis=0, keepdims=True)
        l_ref[...] += s * (0.5 / D)

    row = pl.BlockSpec((tm, D), lambda i: (i, 0))
    return pl.pallas_call(
        body, name="loss", grid=(t // tm,), in_specs=[row, row],
        out_specs=[pl.BlockSpec((8, 128), lambda i: (0, 0)), row],
        out_shape=[jax.ShapeDtypeStruct((8, 128), F32), jax.ShapeDtypeStruct((t, D), F32)],
        compiler_params=_cparams(("arbitrary",)),
    )(y, tgt)


def _ffn_bwd(dy, xm, g2, a, b, ga, gb, tm):
    t = dy.shape[0]

    def body(dy_ref, x_ref, g_ref, a_ref, b_ref, wg_ref, wu_ref, wd_ref, dx_ref, da_ref, db_ref, hm_ref, dg_ref):
        @pl.when(pl.program_id(0) == 0)
        def _():
            dg_ref[...] = jnp.zeros_like(dg_ref)

        dyv = dy_ref[...]
        dyb = dyv.astype(BF16)
        dh2 = jnp.zeros_like(dyv)
        for k in range(N_CHIPS):
            dhm = _nt(dyb, wd_ref[k])
            av = a_ref[k].astype(F32)
            bv = b_ref[k].astype(F32)
            sig = jax.nn.sigmoid(av)
            sil = av * sig
            hm_ref[k] = (sil * bv).astype(BF16)
            da = (dhm * bv * (sig * (1.0 + av * (1.0 - sig)))).astype(BF16)
            db = (dhm * sil).astype(BF16)
            da_ref[k] = da
            db_ref[k] = db
            dh2 = dh2 + _nt(da, wg_ref[k]) + _nt(db, wu_ref[k])
        r, xh = _rms_fwd(x_ref[...], 1.0 / D)
        dg_ref[...] += jnp.sum(dh2 * xh, axis=0, keepdims=True)
        dx_ref[...] = dyv + _rms_bwd(dh2, g_ref[...], xh, r, 1.0 / D)

    row = lambda w: pl.BlockSpec((tm, w), lambda i: (i, 0))
    blk = pl.BlockSpec((N_CHIPS, tm, FFB), lambda i: (0, i, 0))
    bsd = jax.ShapeDtypeStruct((N_CHIPS, t, FFB), BF16)
    return pl.pallas_call(
        body, name="ffn_bwd", grid=(t // tm,),
        in_specs=[row(D), row(D), _const_spec((1, D)), blk, blk] + _ffn_weight_specs(),
        out_specs=[row(D), blk, blk, blk, pl.BlockSpec((1, D), lambda i: (0, 0))],
        out_shape=[jax.ShapeDtypeStruct((t, D), F32), bsd, bsd, bsd, jax.ShapeDtypeStruct((1, D), F32)],
        compiler_params=_cparams(("arbitrary",)),
    )(dy, xm, g2, a, b, ga, ga, gb)


def _wgrad_blocks(a, b, tt, name):
    a_blocks = a.ndim == 3
    t = a.shape[1] if a_blocks else a.shape[0]
    rows = a.shape[2] if a_blocks else a.shape[1]
    cols = b.shape[1] if a_blocks else b.shape[2]

    def body(a_ref, b_ref, o_ref, acc_ref):
        s = pl.program_id(0)

        @pl.when(s == 0)
        def _():
            acc_ref[...] = jnp.zeros_like(acc_ref)

        if a_blocks:
            bv = b_ref[...].astype(BF16)
            for k in range(N_CHIPS):
                acc_ref[k] += _tn(a_ref[k], bv)
        else:
            av = a_ref[...].astype(BF16)
            for k in range(N_CHIPS):
                acc_ref[k] += _tn(av, b_ref[k])

        @pl.when(s == pl.num_programs(0) - 1)
        def _():
            o_ref[...] = acc_ref[...].astype(BF16)

    tok = lambda w: pl.BlockSpec((tt, w), lambda s: (s, 0))
    blk = lambda w: pl.BlockSpec((N_CHIPS, tt, w), lambda s: (0, s, 0))
    return pl.pallas_call(
        body, name=name, grid=(t // tt,),
        in_specs=[blk(rows), tok(cols)] if a_blocks else [tok(rows), blk(cols)],
        out_specs=pl.BlockSpec((N_CHIPS, rows, cols), lambda s: (0, 0, 0)),
        out_shape=jax.ShapeDtypeStruct((N_CHIPS, rows, cols), BF16),
        scratch_shapes=[pltpu.VMEM((N_CHIPS, rows, cols), F32)],
        compiler_params=_cparams(("arbitrary",)),
    )(a, b)


def _wgrad(a, b, tn, tt, name):
    t, k = a.shape
    n = b.shape[1]
    nsteps = t // tt

    def body(a_ref, b_ref, o_ref):
        @pl.when(pl.program_id(1) == 0)
        def _():
            o_ref[...] = jnp.zeros_like(o_ref)

        o_ref[...] += _tn(a_ref[...].astype(BF16), b_ref[...].astype(BF16))

    return pl.pallas_call(
        body, name=name, grid=(n // tn, nsteps),
        in_specs=[pl.BlockSpec((tt, k), lambda j, s: (s, 0)), pl.BlockSpec((tt, tn), lambda j, s: (s, j))],
        out_specs=pl.BlockSpec((k, tn), lambda j, s: (0, j)),
        out_shape=jax.ShapeDtypeStruct((k, n), F32),
        compiler_params=_cparams(("parallel", "arbitrary")),
    )(a, b)


def _mixer_bwd(dxm, proj, cw, gq, gk, sinks, gco, gao, wo, tq):
    t = proj.shape[0]
    nb = tq // BLK
    r8 = tq // 8
    nt = t // tq
    te = tq + 8
    kvw = 2 * NKV * HP

    def body(dx_ref, dxn_ref, p_ref, cgp_ref, hcp_ref, bgn_ref, cgn_ref, hcn_ref, kvp_ref, cw_ref, gq_ref,
             gk_ref, sk_ref, gco_ref, gao_ref, wo_ref,
             dpm_ref, dkvm_ref, dkvh_ref, dcw_ref, dgq_ref, dgk_ref, dsk_ref, dgco_ref, dgao_ref, acc_ref):
        i = pl.program_id(0)

        @pl.when(i == 0)
        def _():
            for r in (dcw_ref, dgq_ref, dgk_ref, dsk_ref, dgco_ref, dgao_ref):
                r[...] = jnp.zeros_like(r)

        acc_ref[...] = jnp.zeros_like(acc_ref)
        live_rows = jnp.where(i < nt - 1, te, tq)
        dxb = dx_ref[...].astype(BF16)
        dxe = jnp.concatenate([dxb, dxn_ref[...].astype(BF16)], axis=0)
        dcn = _nt(dxe, wo_ref[0:CC, :])
        bg = jnp.concatenate([p_ref[:, O_BG:O_BG + CC], bgn_ref[...]], axis=0)
        cg = jnp.concatenate([p_ref[:, O_CG:O_CG + CC], cgn_ref[...]], axis=0)
        hc = jnp.concatenate([p_ref[:, O_HC:O_HC + CC], hcn_ref[...]], axis=0)
        u = cg * hc
        up = jnp.where(i > 0, cgp_ref[...] * hcp_ref[...], 0.0)
        u1, u2 = _conv_taps(jnp.concatenate([up, u], axis=0), te)
        w0, w1, w2 = cw_ref[0:1, :], cw_ref[1:2, :], cw_ref[2:3, :]
        y = w0 * u2 + w1 * u1 + w2 * u
        co = bg * y
        rc, coh = _rms_fwd(co, 1.0 / CC)
        dco = _rms_bwd(dcn, gco_ref[...], coh, rc, 1.0 / CC)
        row_io = lax.broadcasted_iota(jnp.int32, (te, 1), 0)
        own = row_io < tq
        dgco_ref[...] += jnp.sum(jnp.where(own, dcn * coh, 0.0), axis=0, keepdims=True)
        dyc = jnp.where(row_io < live_rows, dco * bg, 0.0)
        dyo = jnp.where(own, dyc, 0.0)
        dcw_ref[0:1, :] += jnp.sum(dyo * u2, axis=0, keepdims=True)
        dcw_ref[1:2, :] += jnp.sum(dyo * u1, axis=0, keepdims=True)
        dcw_ref[2:3, :] += jnp.sum(dyo * u, axis=0, keepdims=True)
        dy1 = pltpu.roll(dyc, te - 1, 0)[0:tq]
        dy2 = pltpu.roll(dyc, te - 2, 0)[0:tq]
        du = w2 * dyc[0:tq] + w1 * dy1 + w0 * dy2
        dpm_ref[:, O_BG:O_BG + CC] = (dco[0:tq] * y[0:tq]).astype(BF16)
        dpm_ref[:, O_CG:O_CG + CC] = (du * hc[0:tq]).astype(BF16)
        dpm_ref[:, O_HC:O_HC + CC] = (du * cg[0:tq]).astype(BF16)
        dan = _nt(dxb, wo_ref[CC:MIXW, :])
        kraw = jnp.concatenate([kvp_ref[:, 0:NKV * HP], p_ref[:, O_K:O_K + NKV * HP]], axis=0)
        vraw = jnp.concatenate([kvp_ref[:, NKV * HP:], p_ref[:, O_V:O_V + NKV * HP]], axis=0)
        gqv, gkv = gq_ref[...], gk_ref[...]
        keys = _norm_keys(kraw, gkv)
        vb = [vraw[:, h * HP:(h + 1) * HP].astype(BF16) for h in range(NKV)]
        base_valid, c_io = _band_mask()
        lane = lax.broadcasted_iota(jnp.int32, (1, HP), 1)
        for b in range(nb):
            lo = jnp.where(i * nb + b == 0, BLK, 0)
            valid = base_valid & (c_io >= lo)
            band = slice(b * BLK, b * BLK + 2 * BLK)
            qs, prs, pss, outs = [], [], [], []
            for g in range(NQ):
                h = g // GRP
                qg = p_ref[b * BLK:(b + 1) * BLK, O_Q + g * HP:O_Q + (g + 1) * HP]
                rq, qh = _rms_fwd(qg, 1.0 / HD)
                qn = (qh * gqv).astype(BF16)
                pr, ps = _attn_probs(qn, keys[h][2][band], sk_ref[0, g], valid)
                qs.append((rq, qh, qn))
                prs.append(pr)
                pss.append(ps)
                outs.append(jnp.dot(pr.astype(BF16), vb[h][band], preferred_element_type=F32))
            ao = jnp.concatenate(outs, axis=1)
            ra, aoh = _rms_fwd(ao, 1.0 / (NQ * HD))
            danb = dan[b * BLK:(b + 1) * BLK]
            dgao_ref[...] += jnp.sum(danb * aoh, axis=0, keepdims=True)
            dao = _rms_bwd(danb, gao_ref[...], aoh, ra, 1.0 / (NQ * HD))
            dqs = []
            for h in range(NKV):
                dss, dobs = [], []
                for g in range(h * GRP, (h + 1) * GRP):
                    rq, qh, qn = qs[g]
                    dob = dao[:, g * HP:(g + 1) * HP].astype(BF16)
                    dp = _nt(dob, vb[h][band])
                    delta = jnp.sum(prs[g] * dp, axis=-1, keepdims=True)
                    dsb = (prs[g] * (dp - delta) * SCALE).astype(BF16)
                    dsk = -jnp.sum(pss[g] * delta, axis=0, keepdims=True)
                    dsk_ref[...] += jnp.where(lane == g, dsk, 0.0)
                    dqn = jnp.dot(dsb, keys[h][2][band], preferred_element_type=F32)
                    dgq_ref[...] += jnp.sum(dqn * qh, axis=0, keepdims=True)
                    dqs.append(_rms_bwd(dqn, gqv, qh, rq, 1.0 / HD).astype(BF16))
                    dss.append(dsb)
                    dobs.append(dob)
                grp = slice(h * GRP, (h + 1) * GRP)
                dkn = _tn(jnp.concatenate(dss, axis=0), jnp.concatenate([q[2] for q in qs[grp]], axis=0))
                dv = _tn(jnp.concatenate([p.astype(BF16) for p in prs[grp]], axis=0),
                         jnp.concatenate(dobs, axis=0))
                khat, rk = keys[h][0][band], keys[h][1][band]
                dgk_ref[...] += jnp.sum(dkn * khat, axis=0, keepdims=True)
                acc_ref[band, h * HP:(h + 1) * HP] += _rms_bwd(dkn, gkv, khat, rk, 1.0 / HD)
                acc_ref[band, (NKV + h) * HP:(NKV + h + 1) * HP] += dv
            dpm_ref[b * BLK:(b + 1) * BLK, O_Q:O_K] = jnp.concatenate(dqs, axis=1)
        dkvh_ref[...] = acc_ref[0:BLK, :]
        dkvm_ref[...] = acc_ref[BLK:, :]

    prev8 = lambda col: pl.BlockSpec((8, CC), lambda i: (jnp.maximum(i * r8 - 1, 0), col))
    next8 = lambda col: pl.BlockSpec((8, CC), lambda i: (jnp.minimum((i + 1) * r8, t // 8 - 1), col))
    small = lambda n: pl.BlockSpec((1, n), lambda i: (0, 0))
    return pl.pallas_call(
        body, name="mixer_bwd", grid=(nt,),
        in_specs=[
            pl.BlockSpec((tq, D), lambda i: (i, 0)),
            pl.BlockSpec((8, D), lambda i: (jnp.minimum((i + 1) * r8, t // 8 - 1), 0)),
            pl.BlockSpec((tq, NP), lambda i: (i, 0)),
            prev8(O_CG // CC), prev8(O_HC // CC),
            next8(O_BG // CC), next8(O_CG // CC), next8(O_HC // CC),
            pl.BlockSpec((BLK, kvw), lambda i: (jnp.maximum(i * nb - 1, 0), O_K // kvw)),
            _const_spec((8, CC)), _const_spec((1, HP)), _const_spec((1, HP)),
            pl.BlockSpec(memory_space=pltpu.SMEM),
            _const_spec((1, CC)), _const_spec((1, NQ * HP)), _const_spec((MIXW, D)),
        ],
        out_specs=[
            pl.BlockSpec((tq, NMAIN), lambda i: (i, 0)),
            pl.BlockSpec((tq, kvw), lambda i: (i, 0)),
            pl.BlockSpec((BLK, kvw), lambda i: (i, 0)),
            pl.BlockSpec((8, CC), lambda i: (0, 0)), small(HP), small(HP), small(HP), small(CC), small(NQ * HP),
        ],
        out_shape=[
            jax.ShapeDtypeStruct((t, NMAIN), BF16), jax.ShapeDtypeStruct((t, kvw), F32),
            jax.ShapeDtypeStruct((nt * BLK, kvw), F32),
            jax.ShapeDtypeStruct((8, CC), F32), jax.ShapeDtypeStruct((1, HP), F32), jax.ShapeDtypeStruct((1, HP), F32),
            jax.ShapeDtypeStruct((1, HP), F32), jax.ShapeDtypeStruct((1, CC), F32),
            jax.ShapeDtypeStruct((1, NQ * HP), F32),
        ],
        scratch_shapes=[pltpu.VMEM((tq + BLK, kvw), F32)],
        compiler_params=_cparams(("arbitrary",)),
    )(dxm, dxm, proj, proj, proj, proj, proj, proj, proj, cw, gq, gk, sinks, gco, gao, wo)


def _inproj_bwd(dpm, dkv, wp, x, g1, dxm, tm):
    t = x.shape[0]
    kvw = 2 * NKV * HP

    def body(dp_ref, dk_ref, w_ref, x_ref, g_ref, dxm_ref, dx_ref, dg_ref):
        @pl.when(pl.program_id(0) == 0)
        def _():
            dg_ref[...] = jnp.zeros_like(dg_ref)

        dh = _nt(dp_ref[...], w_ref[:, 0:NMAIN]) + _nt(dk_ref[...], w_ref[:, NMAIN:NP])
        r, xh = _rms_fwd(x_ref[...], 1.0 / D)
        dg_ref[...] += jnp.sum(dh * xh, axis=0, keepdims=True)
        dx_ref[...] = dxm_ref[...] + _rms_bwd(dh, g_ref[...], xh, r, 1.0 / D)

    row = lambda w: pl.BlockSpec((tm, w), lambda i: (i, 0))
    return pl.pallas_call(
        body, name="inproj_bwd", grid=(t // tm,),
        in_specs=[row(NMAIN), row(kvw), _const_spec((D, NP)), row(D), _const_spec((1, D)), row(D)],
        out_specs=[row(D), pl.BlockSpec((1, D), lambda i: (0, 0))],
        out_shape=[jax.ShapeDtypeStruct((t, D), F32), jax.ShapeDtypeStruct((1, D), F32)],
        compiler_params=_cparams(("arbitrary",)),
    )(dpm, dkv, wp, x, g1, dxm)


def _rows_tile(rows):
    for cand in (512, 256, 128, 64, 32, 16, 8):
        if rows % cand == 0:
            return cand
    return rows


def _presum_halves(gs, theirs, core):
    outs = []
    for n, (ga, ta) in enumerate(zip(gs, theirs)):
        _, hr, cols = ta.shape

        def body(c_ref, g_ref, t_ref, o_ref):
            o_ref[...] = (g_ref[...].astype(F32) + t_ref[...].astype(F32)).astype(BF16)

        half = pl.BlockSpec((None, hr, cols), lambda k, c_ref: (k, 0, 0))
        outs.append(pl.pallas_call(
            body, name=f"presum_{n}",
            grid_spec=pltpu.PrefetchScalarGridSpec(
                num_scalar_prefetch=1, grid=(N_CHIPS,),
                in_specs=[pl.BlockSpec((None, hr, cols), lambda k, c_ref: (k, c_ref[0], 0)), half],
                out_specs=half),
            out_shape=jax.ShapeDtypeStruct(ta.shape, BF16), compiler_params=_cparams(("parallel",)),
        )(core, ga, ta))
    return outs


def _sum_chips(cs, name):
    outs = []
    for n, ca in enumerate(cs):
        _, rows, cols = ca.shape
        tr = _rows_tile(rows)

        def body(c_ref, o_ref):
            acc = c_ref[0].astype(F32)
            for j in range(1, N_CHIPS):
                acc = acc + c_ref[j].astype(F32)
            o_ref[...] = acc

        outs.append(pl.pallas_call(
            body, name=f"{name}_{n}", grid=(rows // tr,),
            in_specs=[pl.BlockSpec((N_CHIPS, tr, cols), lambda i: (0, i, 0))],
            out_specs=pl.BlockSpec((tr, cols), lambda i: (i, 0)),
            out_shape=jax.ShapeDtypeStruct((rows, cols), F32), compiler_params=_cparams(("parallel",)),
        )(ca))
    return outs


def _adamw(w, g, m, v, name):
    rows, cols = w.shape
    tr = _rows_tile(rows)
    c1 = 1.0 - ADAM_B1 ** ADAM_STEP
    c2 = 1.0 - ADAM_B2 ** ADAM_STEP

    def body(w_ref, g_ref, m_ref, v_ref, d_ref, mo_ref, vo_ref):
        gv = g_ref[...]
        mn = ADAM_B1 * m_ref[...] + (1.0 - ADAM_B1) * gv
        vn = ADAM_B2 * v_ref[...] + (1.0 - ADAM_B2) * (gv * gv)
        mo_ref[...] = mn
        vo_ref[...] = vn
        d_ref[...] = -ADAM_LR * ((mn / c1) / (jnp.sqrt(vn / c2) + ADAM_EPS) + ADAM_WD * w_ref[...])

    spec = pl.BlockSpec((tr, cols), lambda i: (i, 0))
    sds = jax.ShapeDtypeStruct((rows, cols), F32)
    return pl.pallas_call(
        body, name=name, grid=(rows // tr,), in_specs=[spec] * 4, out_specs=[spec] * 3, out_shape=[sds] * 3,
        compiler_params=_cparams(("parallel",)),
    )(w, g, m, v)


def _place():
    x, y, c = lax.axis_index("x"), lax.axis_index("y"), lax.axis_index("c")
    chips = [(1 - x, y), (x, 1 - y), (1 - x, 1 - y)]
    return x, y, c, chips


ANY = pl.BlockSpec(memory_space=pl.ANY)
DMA_ROWS = 64


def _pieces(shape):
    rows = shape[-2]
    step = DMA_ROWS if rows % DMA_ROWS == 0 else rows
    lead = [()]
    for n in shape[:-2]:
        lead = [i + (k,) for i in lead for k in range(n)]
    return [i + (pl.ds(r0, step),) for i in lead for r0 in range(0, rows, step)]


def _start_pieces(make, src, dst):
    for idx in _pieces(src.shape):
        make(src.at[idx], dst.at[idx]).start()


def _gather_layer(blocks, layer):
    nw = len(blocks)

    def body(*refs):
        _gather_body(refs[:nw], refs[nw:2 * nw], refs[2 * nw:], layer, _start_pieces)

    return pl.pallas_call(
        body, name=f"gather_layer{layer}", in_specs=[ANY] * nw, out_specs=[ANY] * nw,
        out_shape=[jax.ShapeDtypeStruct((N_CHIPS,) + b.shape, b.dtype) for b in blocks],
        scratch_shapes=[pltpu.SemaphoreType.DMA((3, nw))] * 4,
        compiler_params=_cparams(has_side_effects=True),
    )(*blocks)


def _gather_body(srcs, outs, sems, layer, start):
    nw = len(srcs)
    ssem, rsem, fssem, frsem = sems
    x, y, c, chips = _place()
    kme = 2 * x + y

    def plane(j, w, to):
        return lambda s, d: pltpu.make_async_remote_copy(
            src_ref=s, dst_ref=d, send_sem=ssem.at[j, w], recv_sem=rsem.at[j, w], device_id=to,
            device_id_type=MESH)

    def passed(j, w):
        return lambda s, d: pltpu.make_async_remote_copy(
            src_ref=s, dst_ref=d, send_sem=fssem.at[j, w], recv_sem=frsem.at[j, w],
            device_id=(x, y, 1 - c), device_id_type=MESH)

    @pl.when(c == layer)
    def _():
        for j, (px, py) in enumerate(chips):
            for w in range(nw):
                start(plane(j, w, (px, py, c)), srcs[w], outs[w].at[kme])
        for j, (px, py) in enumerate(chips):
            for w in range(nw):
                got = outs[w].at[2 * px + py]
                plane(j, w, (px, py, c))(got, got).wait_recv()
                start(passed(j, w), got, got)
        for j, (px, py) in enumerate(chips):
            for w in range(nw):
                got = outs[w].at[2 * px + py]
                plane(j, w, (px, py, c))(got, got).wait_send()
                passed(j, w)(got, got).wait_send()

    @pl.when(c != layer)
    def _():
        for j, (px, py) in enumerate(chips):
            for w in range(nw):
                got = outs[w].at[2 * px + py]
                passed(j, w)(got, got).wait_recv()


def _handshake_all():
    x, y, c, _ = _place()
    barrier = pltpu.get_barrier_semaphore()
    for r in range(1, 8):
        peer = (x ^ (r >> 2), y ^ ((r >> 1) & 1), c ^ (r & 1))
        pl.semaphore_signal(barrier, inc=1, device_id=peer, device_id_type=MESH)
    pl.semaphore_wait(barrier, 7)


def _gather_layer_async(blocks, layer, collective_id):
    hbm = pltpu.MemorySpace.HBM
    srcs = [jax.new_ref(b, memory_space=hbm) for b in blocks]
    outs = [jax.empty_ref(jax.ShapeDtypeStruct((N_CHIPS,) + b.shape, b.dtype), memory_space=hbm) for b in blocks]

    @pl.kernel(mesh=plsc.ScalarSubcoreMesh(axis_name="seq", num_cores=1), name=f"gather_layer{layer}_seq",
               scratch_types=[pltpu.SemaphoreType.DMA((3, len(blocks)))] * 4,
               compiler_params=pltpu.CompilerParams(collective_id=collective_id))
    def launch(*sems):
        _handshake_all()
        _gather_body(srcs, outs, sems, layer, lambda make, s, d: make(s, d).start())

    launch()
    return [o[...] for o in outs]


def _swap_halves(gs):
    nw = len(gs)

    def body(*refs):
        srcs, theirs = refs[:nw], refs[nw:2 * nw]
        ssem, rsem = refs[2 * nw:]
        x, y, c, _ = _place()

        def give(w):
            return lambda s, d: pltpu.make_async_remote_copy(
                src_ref=s, dst_ref=d, send_sem=ssem.at[w], recv_sem=rsem.at[w], device_id=(x, y, 1 - c),
                device_id_type=MESH)

        for w in range(nw):
            hr = theirs[w].shape[1]
            _start_pieces(give(w), srcs[w].at[:, pl.ds((1 - c) * hr, hr)], theirs[w])
        for w in range(nw):
            give(w)(theirs[w], theirs[w]).wait()

    return pl.pallas_call(
        body, name="swap_halves", in_specs=[ANY] * nw, out_specs=[ANY] * nw,
        out_shape=[jax.ShapeDtypeStruct((g.shape[0], g.shape[1] // 2, g.shape[2]), g.dtype) for g in gs],
        scratch_shapes=[pltpu.SemaphoreType.DMA((nw,))] * 2,
        compiler_params=_cparams(has_side_effects=True),
    )(*gs)


def _scatter_chips(ps):
    nw = len(ps)

    def body(*refs):
        srcs, outs = refs[:nw], refs[nw:2 * nw]
        ssem, rsem = refs[2 * nw:]
        x, y, c, chips = _place()
        kme = 2 * x + y

        def give(j, w, to):
            return lambda s, d: pltpu.make_async_remote_copy(
                src_ref=s, dst_ref=d, send_sem=ssem.at[j, w], recv_sem=rsem.at[j, w], device_id=to,
                device_id_type=MESH)

        for j, (px, py) in enumerate(chips):
            for w in range(nw):
                _start_pieces(give(j, w, (px, py, c)), srcs[w].at[2 * px + py], outs[w].at[kme])
        for j, (px, py) in enumerate(chips):
            for w in range(nw):
                got = outs[w].at[2 * px + py]
                give(j, w, (px, py, c))(got, got).wait_recv()
        for j, (px, py) in enumerate(chips):
            for w in range(nw):
                sent = srcs[w].at[2 * px + py]
                give(j, w, (px, py, c))(sent, sent).wait_send()

    return pl.pallas_call(
        body, name="scatter_chips", in_specs=[ANY] * nw, out_specs=[ANY] * nw,
        out_shape=[jax.ShapeDtypeStruct(p.shape, p.dtype) for p in ps],
        scratch_shapes=[pltpu.SemaphoreType.DMA((3, nw)), pltpu.SemaphoreType.DMA((3, nw))],
        compiler_params=_cparams(has_side_effects=True),
    )(*ps)


def _swap_siblings(rs):
    nw = len(rs)

    def body(*refs):
        srcs, outs = refs[:nw], refs[nw:2 * nw]
        ssem, rsem = refs[2 * nw:]
        x, y, c, _ = _place()

        def give(w):
            return lambda s, d: pltpu.make_async_remote_copy(
                src_ref=s, dst_ref=d, send_sem=ssem.at[w], recv_sem=rsem.at[w], device_id=(x, y, 1 - c),
                device_id_type=MESH)

        for w in range(nw):
            _start_pieces(give(w), srcs[w], outs[w])
        for w in range(nw):
            give(w)(srcs[w], outs[w]).wait()

    return pl.pallas_call(
        body, name="swap_siblings", in_specs=[ANY] * nw, out_specs=[ANY] * nw,
        out_shape=[jax.ShapeDtypeStruct(r.shape, r.dtype) for r in rs],
        scratch_shapes=[pltpu.SemaphoreType.DMA((nw,))] * 2,
        compiler_params=_cparams(has_side_effects=True),
    )(*rs)


def _allreduce_small(v):
    rows = v.shape[0]

    def body(v_ref, o_ref, buf, ssem, rsem):
        x, y, c, _ = _place()
        me = 4 * x + 2 * y + c
        buf[me] = v_ref[...]
        sends = []
        for r in range(1, 8):
            peer = (x ^ (r >> 2), y ^ ((r >> 1) & 1), c ^ (r & 1))
            cp = pltpu.make_async_remote_copy(
                src_ref=v_ref, dst_ref=buf.at[me], send_sem=ssem.at[r - 1], recv_sem=rsem.at[r - 1],
                device_id=peer, device_id_type=MESH)
            cp.start()
            sends.append(cp)
        for r in range(1, 8):
            src = me ^ r
            pltpu.make_async_remote_copy(
                src_ref=v_ref, dst_ref=buf.at[src], send_sem=ssem.at[r - 1], recv_sem=rsem.at[r - 1],
                device_id=(x, y, c), device_id_type=MESH).wait_recv()
        for cp in sends:
            cp.wait_send()
        acc = buf[0]
        for d in range(1, 8):
            acc = acc + buf[d]
        o_ref[...] = acc

    vm = pl.BlockSpec(memory_space=pltpu.VMEM)
    return pl.pallas_call(
        body, name="allreduce_small", in_specs=[vm], out_specs=vm,
        out_shape=jax.ShapeDtypeStruct(v.shape, F32),
        scratch_shapes=[pltpu.VMEM((8, rows, 128), F32), pltpu.SemaphoreType.DMA((7,)),
                        pltpu.SemaphoreType.DMA((7,))],
        compiler_params=_cparams(has_side_effects=True),
    )(v)


def _pad_heads(w, n_heads, axis):
    shp = w.shape
    w = w.reshape(shp[:axis] + (n_heads, HD) + shp[axis + 1:])
    pad = [(0, 0)] * w.ndim
    pad[axis + 1] = (0, HP - HD)
    w = jnp.pad(w, pad)
    return w.reshape(shp[:axis] + (n_heads * HP,) + shp[axis + 1:])


def _strip_heads(w, n_heads, axis):
    shp = w.shape
    w = w.reshape(shp[:axis] + (n_heads, HP) + shp[axis + 1:])
    w = lax.slice_in_dim(w, 0, HD, axis=axis + 1)
    return w.reshape(shp[:axis] + (n_heads * HD,) + shp[axis + 1:])


def _unshard_cols(g4):
    k, r, c = g4.shape
    return jnp.transpose(g4, (1, 0, 2)).reshape(r, k * c)


def _shard_cols(w):
    r, n = w.shape
    return jnp.transpose(w.reshape(r, N_CHIPS, n // N_CHIPS), (1, 0, 2))


def _pad_win(win):
    parts = [win[:, :3 * CC], _pad_heads(win[:, 3 * CC:3 * CC + NQ * HD], NQ, 1),
             _pad_heads(win[:, 3 * CC + NQ * HD:3 * CC + (NQ + NKV) * HD], NKV, 1),
             _pad_heads(win[:, 3 * CC + (NQ + NKV) * HD:], NKV, 1)]
    return jnp.concatenate(parts, axis=1)


def _strip_win(gp):
    parts = [gp[:, :3 * CC], _strip_heads(gp[:, O_Q:O_K], NQ, 1), _strip_heads(gp[:, O_K:O_V], NKV, 1),
             _strip_heads(gp[:, O_V:], NKV, 1)]
    return jnp.concatenate(parts, axis=1)


def _count(shape):
    n = 1
    for s in shape:
        n *= s
    return n


def _pack_rows(arrs):
    flat = [jnp.pad(a.reshape(-1), (0, (-_count(a.shape)) % 128)) for a in arrs]
    v = jnp.concatenate(flat)
    rows = -(-v.shape[0] // (8 * 128)) * 8
    return jnp.pad(v, (0, rows * 128 - v.shape[0])).reshape(rows, 128)


def kernel(x, norm1_g, w_in, conv_w, q_norm_g, k_norm_g, sinks, conv_out_g, attn_out_g, w_o, norm2_g, w_gate, w_up, w_down, loss_target, m_norm1_g, m_w_in, m_conv_w, m_q_norm_g, m_k_norm_g, m_sinks, m_conv_out_g, m_attn_out_g, m_w_o, m_norm2_g, m_w_gate, m_w_up, m_w_down, v_norm1_g, v_w_in, v_conv_w, v_q_norm_g, v_k_norm_g, v_sinks, v_conv_out_g, v_attn_out_g, v_w_o, v_norm2_g, v_w_gate, v_w_up, v_w_down):
    depth = w_in.shape[0]
    t = x.shape[1]
    xs = x.reshape(t, D)
    tgt = loss_target.reshape(t, D)
    xi, yi = lax.axis_index("x"), lax.axis_index("y")
    kme = 2 * xi + yi
    tm = min(512, t)
    tq = min(256, t)
    tf = min(256, t)

    cwp = jnp.pad(conv_w.reshape(depth * 3, CC // N_CHIPS), ((0, 8 - depth * 3), (0, 0)))
    gathered = []
    for l in range(depth):
        own = [jnp.concatenate([w_gate[l], w_up[l]], axis=0).astype(BF16),
               jnp.concatenate([w_down[l], w_o[l]], axis=0).astype(BF16), w_in[l].astype(BF16)]
        own += [cwp] if l == 0 else []
        got = _gather_layer(own, l) if l == 0 else _gather_layer_async(own, l, collective_id=l)
        gathered.append([lax.dynamic_update_index_in_dim(g, o, kme, 0) for g, o in zip(got, own)])
    cw_full = _unshard_cols(gathered[0][3])[:depth * 3].reshape(depth, 3, CC)
    layers = []
    for l in range(depth):
        ga, gb, gc = gathered[l][:3]
        wp = _pad_win(_unshard_cols(gc))
        wo = gb[:, FFB:].reshape(D, D)
        wo = jnp.concatenate([wo[:CC], _pad_heads(wo[CC:], NQ, 0)], axis=0)
        layers.append(dict(
            wp=wp, wo=wo, ga=ga, gb=gb, cw=jnp.pad(cw_full[l], ((0, 5), (0, 0))),
            g1=norm1_g[l].reshape(1, D), g2=norm2_g[l].reshape(1, D),
            gq=jnp.pad(q_norm_g[l], (0, HP - HD)).reshape(1, HP), gk=jnp.pad(k_norm_g[l], (0, HP - HD)).reshape(1, HP),
            sk=sinks[l].reshape(1, NQ), gco=conv_out_g[l].reshape(1, CC),
            gao=_pad_heads(attn_out_g[l], NQ, 0).reshape(1, NQ * HP)))

    saved = []
    cur = xs
    for l in range(depth):
        p = layers[l]
        proj, h = _inproj_fwd(cur, p["g1"], p["wp"], tm)
        xm, mix = _mixer_fwd(proj, cur, p["cw"], p["gq"], p["gk"], p["sk"], p["gco"], p["gao"], p["wo"], tq)
        xo, a, b, h2 = _ffn_fwd(xm, p["g2"], p["ga"], p["gb"], tf)
        saved.append(dict(x=cur, proj=proj, h=h, xm=xm, mix=mix, a=a, b=b, h2=h2))
        cur = xo
    lpart, dy = _loss_and_grad(cur, tgt, tm)
    loss = lax.psum(lpart[0, 0], ("x", "y", "c"))

    nt = t // tq
    ci = lax.axis_index("c")
    core = ci.reshape(1).astype(jnp.int32)
    rbig = [None] * depth
    gsmall = [None] * depth
    for l in reversed(range(depth)):
        p, s = layers[l], saved[l]
        dxm, da, db, hm, dg2 = _ffn_bwd(dy, s["xm"], p["g2"], s["a"], s["b"], p["ga"], p["gb"], tf)
        g_wg = _wgrad_blocks(s["h2"], da, tm, "wgrad_gate")
        g_wu = _wgrad_blocks(s["h2"], db, tm, "wgrad_up")
        g_wd = _wgrad_blocks(hm, dy, tm, "wgrad_down")
        dpm, dkvm, dkvh, dcw, dgq, dgk, dsk, dgco, dgao = _mixer_bwd(
            dxm, s["proj"], p["cw"], p["gq"], p["gk"], p["sk"], p["gco"], p["gao"], p["wo"], tq)
        g_wot = _wgrad(dxm, s["mix"], MIXW, tm, "wgrad_o")
        kvw = dkvm.shape[1]
        halo = jnp.concatenate([dkvh.reshape(nt, BLK, kvw)[1:], jnp.zeros((1, BLK, kvw), F32)], axis=0)
        halo = jnp.pad(halo, ((0, 0), (tq - BLK, 0), (0, 0)))
        dkv = (dkvm.reshape(nt, tq, kvw) + halo).reshape(t, kvw).astype(BF16)
        dx, dg1 = _inproj_bwd(dpm, dkv, p["wp"], s["x"], p["g1"], dxm, tm)
        g_wpm = _wgrad(s["h"], dpm, NMAIN // 2, tm, "wgrad_in_main")
        g_wpk = _wgrad(s["h"], dkv, kvw, tm, "wgrad_in_kv")
        dy = dx
        g_in = _strip_win(jnp.concatenate([g_wpm, g_wpk], axis=1))
        g_ot = jnp.concatenate([g_wot[:, :CC], _strip_heads(g_wot[:, CC:], NQ, 1)], axis=1)
        gsmall[l] = dict(g1=dg1, cw=dcw[:3], gq=dgq[0, :HD], gk=dgk[0, :HD], sk=dsk[0, :NQ], gco=dgco,
                         gao=_strip_heads(dgao.reshape(NQ * HP), NQ, 0), g2=dg2)
        gs = [_shard_cols(g_in.astype(BF16)), _shard_cols(g_ot.astype(BF16)), g_wg, g_wu, g_wd]
        ps = _presum_halves(gs, _swap_halves(gs), core)
        cs = [lax.dynamic_update_index_in_dim(got, lax.dynamic_index_in_dim(q, kme, 0, keepdims=False), kme, 0)
              for got, q in zip(_scatter_chips(ps), ps)]
        r_mine = _sum_chips(cs, "chipsum")
        r_theirs = _swap_siblings(r_mine)
        r_in, r_ot, r_g, r_u, r_d = [
            jnp.where(ci == 0, jnp.concatenate([a, b], axis=0), jnp.concatenate([b, a], axis=0))
            for a, b in zip(r_mine, r_theirs)]
        rbig[l] = [r_in, r_ot.T, r_g, r_u, r_d]
    grad_x = dy.reshape(x.shape)
    g_big = [jnp.stack([rbig[l][w] for l in range(depth)]) for w in range(5)]

    small_shapes = dict(g1=(D,), cw=(3, CC), gq=(HD,), gk=(HD,), sk=(NQ,), gco=(CC,), gao=(NQ * HD,), g2=(D,))
    red = _allreduce_small(_pack_rows([gsmall[l][n] for l in range(depth) for n in small_shapes])).reshape(-1)
    red_small, offs = {n: [] for n in small_shapes}, 0
    for l in range(depth):
        for n, shp in small_shapes.items():
            cnt = _count(shp)
            red_small[n].append(red[offs:offs + cnt].reshape(shp))
            offs += -(-cnt // 128) * 128
    g_small = {n: jnp.stack(v) for n, v in red_small.items()}
    g_cw = lax.dynamic_slice_in_dim(g_small["cw"], kme * (CC // N_CHIPS), CC // N_CHIPS, axis=2)

    weights = [norm1_g, w_in, conv_w, q_norm_g, k_norm_g, sinks, conv_out_g, attn_out_g, w_o, norm2_g, w_gate,
               w_up, w_down]
    moms = [m_norm1_g, m_w_in, m_conv_w, m_q_norm_g, m_k_norm_g, m_sinks, m_conv_out_g, m_attn_out_g, m_w_o,
            m_norm2_g, m_w_gate, m_w_up, m_w_down]
    vars_ = [v_norm1_g, v_w_in, v_conv_w, v_q_norm_g, v_k_norm_g, v_sinks, v_conv_out_g, v_attn_out_g, v_w_o,
             v_norm2_g, v_w_gate, v_w_up, v_w_down]
    grads = [g_small["g1"], g_big[0], g_cw, g_small["gq"], g_small["gk"], g_small["sk"], g_small["gco"],
             g_small["gao"], g_big[1], g_small["g2"], g_big[2], g_big[3], g_big[4]]
    n_w = len(weights)
    big_idx = [1, 8, 10, 11, 12]
    small_idx = [n for n in range(n_w) if n not in big_idx]
    deltas, new_m, new_v = [None] * n_w, [None] * n_w, [None] * n_w
    for n in big_idx:
        shp = weights[n].shape
        two = [a3.reshape(shp[0] * shp[1], shp[2]) for a3 in (weights[n], grads[n], moms[n], vars_[n])]
        res = _adamw(*two, f"adamw_{n}")
        deltas[n], new_m[n], new_v[n] = [r.reshape(shp) for r in res]
    res = _adamw(*[_pack_rows([arrs[n] for n in small_idx]) for arrs in (weights, grads, moms, vars_)],
                 "adamw_small")
    offs = 0
    for n in small_idx:
        shp = weights[n].shape
        cnt = _count(shp)
        deltas[n], new_m[n], new_v[n] = [r.reshape(-1)[offs:offs + cnt].reshape(shp) for r in res]
        offs += -(-cnt // 128) * 128
    return (loss, grad_x, *grads, *deltas, *new_m, *new_v)
```

```python
import functools

import jax
import jax.numpy as jnp
from jax import lax
from jax.experimental import pallas as pl
from jax.experimental.pallas import tpu as pltpu
from jax.experimental.pallas import tpu_sc as plsc

F32 = jnp.float32
BF16 = jnp.bfloat16

D = 1024
CC = 512
NQ = 8
NKV = 2
HD = 64
HP = 128
GRP = NQ // NKV
FF = 2816
FFB = FF // 4
BLK = 128
EPS = 1e-6
NEG = -1e30
SCALE = HD ** -0.5
O_BG, O_CG, O_HC, O_Q = 0, CC, 2 * CC, 3 * CC
O_K = O_Q + NQ * HP
O_V = O_K + NKV * HP
NP = O_V + NKV * HP
NMAIN = O_K
MIXW = CC + NQ * HP
N_CHIPS = 4
VMEM_LIMIT = 56 * 1024 * 1024
MESH = pl.DeviceIdType.MESH

ADAM_LR, ADAM_B1, ADAM_B2, ADAM_EPS, ADAM_WD, ADAM_STEP = 0.001, 0.9, 0.999, 1e-08, 0.01, 10


def _cparams(sem=None, **kw):
    if sem is not None:
        kw["dimension_semantics"] = sem
    return pltpu.CompilerParams(vmem_limit_bytes=VMEM_LIMIT, **kw)


def _const_spec(shape):
    nd = len(shape)
    return pl.BlockSpec(shape, lambda *_: (0,) * nd, pipeline_mode=pl.Buffered(1))


def _nt(a, b):
    return lax.dot_general(a, b, (((1,), (1,)), ((), ())), preferred_element_type=F32)


def _tn(a, b):
    return lax.dot_general(a, b, (((0,), (0,)), ((), ())), preferred_element_type=F32)


def _rms_fwd(x, inv_n):
    r = lax.rsqrt(jnp.sum(x * x, axis=-1, keepdims=True) * inv_n + EPS)
    return r, x * r


def _rms_bwd(dy, g, xh, r, inv_n):
    dxh = dy * g
    return r * (dxh - xh * (jnp.sum(dxh * xh, axis=-1, keepdims=True) * inv_n))


def _inproj_fwd(x, g1, wp, tm):
    t = x.shape[0]

    def body(x_ref, g_ref, w_ref, p_ref, h_ref):
        _, xh = _rms_fwd(x_ref[...], 1.0 / D)
        h = (xh * g_ref[...]).astype(BF16)
        h_ref[...] = h
        p_ref[...] = jnp.dot(h, w_ref[...], preferred_element_type=F32)

    return pl.pallas_call(
        body, name="inproj_fwd", grid=(t // tm,),
        in_specs=[pl.BlockSpec((tm, D), lambda i: (i, 0)), _const_spec((1, D)), _const_spec((D, NP))],
        out_specs=[pl.BlockSpec((tm, NP), lambda i: (i, 0)), pl.BlockSpec((tm, D), lambda i: (i, 0))],
        out_shape=[jax.ShapeDtypeStruct((t, NP), F32), jax.ShapeDtypeStruct((t, D), BF16)],
        compiler_params=_cparams(("parallel",)),
    )(x, g1, wp)


def _band_mask():
    r_io = lax.broadcasted_iota(jnp.int32, (BLK, 2 * BLK), 0)
    c_io = lax.broadcasted_iota(jnp.int32, (BLK, 2 * BLK), 1)
    return (c_io > r_io) & (c_io <= r_io + BLK), c_io


def _conv_taps(uf, n):
    u1 = pltpu.roll(uf, 1, 0)[8:8 + n]
    u2 = pltpu.roll(uf, 2, 0)[8:8 + n]
    return u1, u2


def _attn_probs(qn, kband, sink, valid):
    s = _nt(qn, kband) * SCALE
    s = jnp.where(valid, s, NEG)
    m = jnp.maximum(jnp.max(s, axis=-1, keepdims=True), sink)
    p = jnp.exp(s - m)
    es = jnp.exp(sink - m)
    inv = 1.0 / (jnp.sum(p, axis=-1, keepdims=True) + es)
    return p * inv, es * inv


def _norm_keys(kraw, gk):
    out = []
    for h in range(NKV):
        kh = kraw[:, h * HP:(h + 1) * HP]
        rk, khat = _rms_fwd(kh, 1.0 / HD)
        out.append((khat, rk, (khat * gk).astype(BF16)))
    return out


def _mixer_fwd(proj, x, cw, gq, gk, sinks, gco, gao, wo, tq):
    t = proj.shape[0]
    nb = tq // BLK
    r8 = tq // 8

    def body(p_ref, cgp_ref, hcp_ref, kvp_ref, x_ref, cw_ref, gq_ref, gk_ref, sk_ref, gco_ref, gao_ref,
             wo_ref, xm_ref, mix_ref):
        i = pl.program_id(0)
        cg = p_ref[:, O_CG:O_CG + CC]
        hc = p_ref[:, O_HC:O_HC + CC]
        u = cg * hc
        up = jnp.where(i > 0, cgp_ref[...] * hcp_ref[...], 0.0)
        u1, u2 = _conv_taps(jnp.concatenate([up, u], axis=0), tq)
        y = cw_ref[0:1, :] * u2 + cw_ref[1:2, :] * u1 + cw_ref[2:3, :] * u
        co = p_ref[:, O_BG:O_BG + CC] * y
        _, coh = _rms_fwd(co, 1.0 / CC)
        cn = coh * gco_ref[...]
        kraw = jnp.concatenate([kvp_ref[:, 0:NKV * HP], p_ref[:, O_K:O_K + NKV * HP]], axis=0)
        vraw = jnp.concatenate([kvp_ref[:, NKV * HP:], p_ref[:, O_V:O_V + NKV * HP]], axis=0)
        keys = _norm_keys(kraw, gk_ref[...])
        vb = [vraw[:, h * HP:(h + 1) * HP].astype(BF16) for h in range(NKV)]
        base_valid, c_io = _band_mask()
        rows = []
        for b in range(nb):
            lo = jnp.where(i * nb + b == 0, BLK, 0)
            valid = base_valid & (c_io >= lo)
            outs = []
            for g in range(NQ):
                h = g // GRP
                qg = p_ref[b * BLK:(b + 1) * BLK, O_Q + g * HP:O_Q + (g + 1) * HP]
                _, qh = _rms_fwd(qg, 1.0 / HD)
                qn = (qh * gq_ref[...]).astype(BF16)
                pr, _ = _attn_probs(qn, keys[h][2][b * BLK:b * BLK + 2 * BLK], sk_ref[0, g], valid)
                outs.append(jnp.dot(pr.astype(BF16), vb[h][b * BLK:b * BLK + 2 * BLK],
                                    preferred_element_type=F32))
            rows.append(jnp.concatenate(outs, axis=1))
        ao = jnp.concatenate(rows, axis=0)
        _, aoh = _rms_fwd(ao, 1.0 / (NQ * HD))
        an = aoh * gao_ref[...]
        mix = jnp.concatenate([cn, an], axis=1).astype(BF16)
        mix_ref[...] = mix
        xm_ref[...] = x_ref[...] + jnp.dot(mix, wo_ref[...], preferred_element_type=F32)

    prev8 = lambda col: pl.BlockSpec((8, CC), lambda i: (jnp.maximum(i * r8 - 1, 0), col))
    return pl.pallas_call(
        body, name="mixer_fwd", grid=(t // tq,),
        in_specs=[
            pl.BlockSpec((tq, NP), lambda i: (i, 0)),
            prev8(O_CG // CC), prev8(O_HC // CC),
            pl.BlockSpec((BLK, 2 * NKV * HP), lambda i: (jnp.maximum(i * nb - 1, 0), O_K // (2 * NKV * HP))),
            pl.BlockSpec((tq, D), lambda i: (i, 0)),
            _const_spec((8, CC)), _const_spec((1, HP)), _const_spec((1, HP)),
            pl.BlockSpec(memory_space=pltpu.SMEM),
            _const_spec((1, CC)), _const_spec((1, NQ * HP)), _const_spec((MIXW, D)),
        ],
        out_specs=[pl.BlockSpec((tq, D), lambda i: (i, 0)), pl.BlockSpec((tq, MIXW), lambda i: (i, 0))],
        out_shape=[jax.ShapeDtypeStruct((t, D), F32), jax.ShapeDtypeStruct((t, MIXW), BF16)],
        compiler_params=_cparams(("parallel",)),
    )(proj, proj, proj, proj, x, cw, gq, gk, sinks, gco, gao, wo)


def _ffn_weight_specs():
    return [pl.BlockSpec((N_CHIPS, D, FFB), lambda i: (0, 0, 0), pipeline_mode=pl.Buffered(1)),
            pl.BlockSpec((N_CHIPS, D, FFB), lambda i: (0, 1, 0), pipeline_mode=pl.Buffered(1)),
            pl.BlockSpec((N_CHIPS, FFB, D), lambda i: (0, 0, 0), pipeline_mode=pl.Buffered(1))]


def _ffn_fwd(xm, g2, ga, gb, tm):
    t = xm.shape[0]

    def body(x_ref, g_ref, wg_ref, wu_ref, wd_ref, xo_ref, a_ref, b_ref, h2_ref):
        xv = x_ref[...]
        _, xh = _rms_fwd(xv, 1.0 / D)
        h2 = (xh * g_ref[...]).astype(BF16)
        h2_ref[...] = h2
        acc = xv
        for k in range(N_CHIPS):
            a = jnp.dot(h2, wg_ref[k], preferred_element_type=F32)
            b = jnp.dot(h2, wu_ref[k], preferred_element_type=F32)
            a_ref[k] = a.astype(BF16)
            b_ref[k] = b.astype(BF16)
            hm = (a * jax.nn.sigmoid(a) * b).astype(BF16)
            acc = acc + jnp.dot(hm, wd_ref[k], preferred_element_type=F32)
        xo_ref[...] = acc

    row = lambda w: pl.BlockSpec((tm, w), lambda i: (i, 0))
    blk = pl.BlockSpec((N_CHIPS, tm, FFB), lambda i: (0, i, 0))
    return pl.pallas_call(
        body, name="ffn_fwd", grid=(t // tm,),
        in_specs=[row(D), _const_spec((1, D))] + _ffn_weight_specs(),
        out_specs=[row(D), blk, blk, row(D)],
        out_shape=[jax.ShapeDtypeStruct((t, D), F32), jax.ShapeDtypeStruct((N_CHIPS, t, FFB), BF16),
                   jax.ShapeDtypeStruct((N_CHIPS, t, FFB), BF16), jax.ShapeDtypeStruct((t, D), BF16)],
        compiler_params=_cparams(("parallel",)),
    )(xm, g2, ga, ga, gb)


def _loss_and_grad(y, tgt, tm):
    t = y.shape[0]

    def body(y_ref, t_ref, l_ref, dy_ref):
        @pl.when(pl.program_id(0) == 0)
        def _():
            l_ref[...] = jnp.zeros_like(l_ref)

        e = y_ref[...] - t_ref[...]
        dy_ref[...] = e * (1.0 / D)
        s = jnp.sum(jnp.sum(e * e, axis=-1, keepdims=True), axis=0, keepdims=True)
        l_ref[...] += s * (0.5 / D)

    row = pl.BlockSpec((tm, D), lambda i: (i, 0))
    return pl.pallas_call(
        body, name="loss", grid=(t // tm,), in_specs=[row, row],
        out_specs=[pl.BlockSpec((8, 128), lambda i: (0, 0)), row],
        out_shape=[jax.ShapeDtypeStruct((8, 128), F32), jax.ShapeDtypeStruct((t, D), F32)],
        compiler_params=_cparams(("arbitrary",)),
    )(y, tgt)


def _ffn_bwd(dy, xm, g2, a, b, ga, gb, tm):
    t = dy.shape[0]

    def body(dy_ref, x_ref, g_ref, a_ref, b_ref, wg_ref, wu_ref, wd_ref, dx_ref, da_ref, db_ref, hm_ref, dg_ref):
        @pl.when(pl.program_id(0) == 0)
        def _():
            dg_ref[...] = jnp.zeros_like(dg_ref)

        dyv = dy_ref[...]
        dyb = dyv.astype(BF16)
        dh2 = jnp.zeros_like(dyv)
        for k in range(N_CHIPS):
            dhm = _nt(dyb, wd_ref[k])
            av = a_ref[k].astype(F32)
            bv = b_ref[k].astype(F32)
            sig = jax.nn.sigmoid(av)
            sil = av * sig
            hm_ref[k] = (sil * bv).astype(BF16)
            da = (dhm * bv * (sig * (1.0 + av * (1.0 - sig)))).astype(BF16)
            db = (dhm * sil).astype(BF16)
            da_ref[k] = da
            db_ref[k] = db
            dh2 = dh2 + _nt(da, wg_ref[k]) + _nt(db, wu_ref[k])
        r, xh = _rms_fwd(x_ref[...], 1.0 / D)
        dg_ref[...] += jnp.sum(dh2 * xh, axis=0, keepdims=True)
        dx_ref[...] = dyv + _rms_bwd(dh2, g_ref[...], xh, r, 1.0 / D)

    row = lambda w: pl.BlockSpec((tm, w), lambda i: (i, 0))
    blk = pl.BlockSpec((N_CHIPS, tm, FFB), lambda i: (0, i, 0))
    bsd = jax.ShapeDtypeStruct((N_CHIPS, t, FFB), BF16)
    return pl.pallas_call(
        body, name="ffn_bwd", grid=(t // tm,),
        in_specs=[row(D), row(D), _const_spec((1, D)), blk, blk] + _ffn_weight_specs(),
        out_specs=[row(D), blk, blk, blk, pl.BlockSpec((1, D), lambda i: (0, 0))],
        out_shape=[jax.ShapeDtypeStruct((t, D), F32), bsd, bsd, bsd, jax.ShapeDtypeStruct((1, D), F32)],
        compiler_params=_cparams(("arbitrary",)),
    )(dy, xm, g2, a, b, ga, ga, gb)


def _wgrad_blocks(a, b, tt, name):
    a_blocks = a.ndim == 3
    t = a.shape[1] if a_blocks else a.shape[0]
    rows = a.shape[2] if a_blocks else a.shape[1]
    cols = b.shape[1] if a_blocks else b.shape[2]

    def body(a_ref, b_ref, o_ref, acc_ref):
        s = pl.program_id(0)

        @pl.when(s == 0)
        def _():
            acc_ref[...] = jnp.zeros_like(acc_ref)

        if a_blocks:
            bv = b_ref[...].astype(BF16)
            for k in range(N_CHIPS):
                acc_ref[k] += _tn(a_ref[k], bv)
        else:
            av = a_ref[...].astype(BF16)
            for k in range(N_CHIPS):
                acc_ref[k] += _tn(av, b_ref[k])

        @pl.when(s == pl.num_programs(0) - 1)
        def _():
            o_ref[...] = acc_ref[...].astype(BF16)

    tok = lambda w: pl.BlockSpec((tt, w), lambda s: (s, 0))
    blk = lambda w: pl.BlockSpec((N_CHIPS, tt, w), lambda s: (0, s, 0))
    return pl.pallas_call(
        body, name=name, grid=(t // tt,),
        in_specs=[blk(rows), tok(cols)] if a_blocks else [tok(rows), blk(cols)],
        out_specs=pl.BlockSpec((N_CHIPS, rows, cols), lambda s: (0, 0, 0)),
        out_shape=jax.ShapeDtypeStruct((N_CHIPS, rows, cols), BF16),
        scratch_shapes=[pltpu.VMEM((N_CHIPS, rows, cols), F32)],
        compiler_params=_cparams(("arbitrary",)),
    )(a, b)


def _wgrad(a, b, tn, tt, name):
    t, k = a.shape
    n = b.shape[1]
    nsteps = t // tt

    def body(a_ref, b_ref, o_ref):
        @pl.when(pl.program_id(1) == 0)
        def _():
            o_ref[...] = jnp.zeros_like(o_ref)

        o_ref[...] += _tn(a_ref[...].astype(BF16), b_ref[...].astype(BF16))

    return pl.pallas_call(
        body, name=name, grid=(n // tn, nsteps),
        in_specs=[pl.BlockSpec((tt, k), lambda j, s: (s, 0)), pl.BlockSpec((tt, tn), lambda j, s: (s, j))],
        out_specs=pl.BlockSpec((k, tn), lambda j, s: (0, j)),
        out_shape=jax.ShapeDtypeStruct((k, n), F32),
        compiler_params=_cparams(("parallel", "arbitrary")),
    )(a, b)


def _mixer_bwd(dxm, proj, cw, gq, gk, sinks, gco, gao, wo, tq):
    t = proj.shape[0]
    nb = tq // BLK
    r8 = tq // 8
    nt = t // tq
    te = tq + 8
    kvw = 2 * NKV * HP

    def body(dx_ref, dxn_ref, p_ref, cgp_ref, hcp_ref, bgn_ref, cgn_ref, hcn_ref, kvp_ref, cw_ref, gq_ref,
             gk_ref, sk_ref, gco_ref, gao_ref, wo_ref,
             dpm_ref, dkvm_ref, dkvh_ref, dcw_ref, dgq_ref, dgk_ref, dsk_ref, dgco_ref, dgao_ref, acc_ref):
        i = pl.program_id(0)

        @pl.when(i == 0)
        def _():
            for r in (dcw_ref, dgq_ref, dgk_ref, dsk_ref, dgco_ref, dgao_ref):
                r[...] = jnp.zeros_like(r)

        acc_ref[...] = jnp.zeros_like(acc_ref)
        live_rows = jnp.where(i < nt - 1, te, tq)
        dxb = dx_ref[...].astype(BF16)
        dxe = jnp.concatenate([dxb, dxn_ref[...].astype(BF16)], axis=0)
        dcn = _nt(dxe, wo_ref[0:CC, :])
        bg = jnp.concatenate([p_ref[:, O_BG:O_BG + CC], bgn_ref[...]], axis=0)
        cg = jnp.concatenate([p_ref[:, O_CG:O_CG + CC], cgn_ref[...]], axis=0)
        hc = jnp.concatenate([p_ref[:, O_HC:O_HC + CC], hcn_ref[...]], axis=0)
        u = cg * hc
        up = jnp.where(i > 0, cgp_ref[...] * hcp_ref[...], 0.0)
        u1, u2 = _conv_taps(jnp.concatenate([up, u], axis=0), te)
        w0, w1, w2 = cw_ref[0:1, :], cw_ref[1:2, :], cw_ref[2:3, :]
        y = w0 * u2 + w1 * u1 + w2 * u
        co = bg * y
        rc, coh = _rms_fwd(co, 1.0 / CC)
        dco = _rms_bwd(dcn, gco_ref[...], coh, rc, 1.0 / CC)
        row_io = lax.broadcasted_iota(jnp.int32, (te, 1), 0)
        own = row_io < tq
        dgco_ref[...] += jnp.sum(jnp.where(own, dcn * coh, 0.0), axis=0, keepdims=True)
        dyc = jnp.where(row_io < live_rows, dco * bg, 0.0)
        dyo = jnp.where(own, dyc, 0.0)
        dcw_ref[0:1, :] += jnp.sum(dyo * u2, axis=0, keepdims=True)
        dcw_ref[1:2, :] += jnp.sum(dyo * u1, axis=0, keepdims=True)
        dcw_ref[2:3, :] += jnp.sum(dyo * u, axis=0, keepdims=True)
        dy1 = pltpu.roll(dyc, te - 1, 0)[0:tq]
        dy2 = pltpu.roll(dyc, te - 2, 0)[0:tq]
        du = w2 * dyc[0:tq] + w1 * dy1 + w0 * dy2
        dpm_ref[:, O_BG:O_BG + CC] = (dco[0:tq] * y[0:tq]).astype(BF16)
        dpm_ref[:, O_CG:O_CG + CC] = (du * hc[0:tq]).astype(BF16)
        dpm_ref[:, O_HC:O_HC + CC] = (du * cg[0:tq]).astype(BF16)
        dan = _nt(dxb, wo_ref[CC:MIXW, :])
        kraw = jnp.concatenate([kvp_ref[:, 0:NKV * HP], p_ref[:, O_K:O_K + NKV * HP]], axis=0)
        vraw = jnp.concatenate([kvp_ref[:, NKV * HP:], p_ref[:, O_V:O_V + NKV * HP]], axis=0)
        gqv, gkv = gq_ref[...], gk_ref[...]
        keys = _norm_keys(kraw, gkv)
        vb = [vraw[:, h * HP:(h + 1) * HP].astype(BF16) for h in range(NKV)]
        base_valid, c_io = _band_mask()
        lane = lax.broadcasted_iota(jnp.int32, (1, HP), 1)
        for b in range(nb):
            lo = jnp.where(i * nb + b == 0, BLK, 0)
            valid = base_valid & (c_io >= lo)
            band = slice(b * BLK, b * BLK + 2 * BLK)
            qs, prs, pss, outs = [], [], [], []
            for g in range(NQ):
                h = g // GRP
                qg = p_ref[b * BLK:(b + 1) * BLK, O_Q + g * HP:O_Q + (g + 1) * HP]
                rq, qh = _rms_fwd(qg, 1.0 / HD)
                qn = (qh * gqv).astype(BF16)
                pr, ps = _attn_probs(qn, keys[h][2][band], sk_ref[0, g], valid)
                qs.append((rq, qh, qn))
                prs.append(pr)
                pss.append(ps)
                outs.append(jnp.dot(pr.astype(BF16), vb[h][band], preferred_element_type=F32))
            ao = jnp.concatenate(outs, axis=1)
            ra, aoh = _rms_fwd(ao, 1.0 / (NQ * HD))
            danb = dan[b * BLK:(b + 1) * BLK]
            dgao_ref[...] += jnp.sum(danb * aoh, axis=0, keepdims=True)
            dao = _rms_bwd(danb, gao_ref[...], aoh, ra, 1.0 / (NQ * HD))
            dqs = []
            for h in range(NKV):
                dss, dobs = [], []
                for g in range(h * GRP, (h + 1) * GRP):
                    rq, qh, qn = qs[g]
                    dob = dao[:, g * HP:(g + 1) * HP].astype(BF16)
                    dp = _nt(dob, vb[h][band])
                    delta = jnp.sum(prs[g] * dp, axis=-1, keepdims=True)
                    dsb = (prs[g] * (dp - delta) * SCALE).astype(BF16)
                    dsk = -jnp.sum(pss[g] * delta, axis=0, keepdims=True)
                    dsk_ref[...] += jnp.where(lane == g, dsk, 0.0)
                    dqn = jnp.dot(dsb, keys[h][2][band], preferred_element_type=F32)
                    dgq_ref[...] += jnp.sum(dqn * qh, axis=0, keepdims=True)
                    dqs.append(_rms_bwd(dqn, gqv, qh, rq, 1.0 / HD).astype(BF16))
                    dss.append(dsb)
                    dobs.append(dob)
                grp = slice(h * GRP, (h + 1) * GRP)
                dkn = _tn(jnp.concatenate(dss, axis=0), jnp.concatenate([q[2] for q in qs[grp]], axis=0))
                dv = _tn(jnp.concatenate([p.astype(BF16) for p in prs[grp]], axis=0),
                         jnp.concatenate(dobs, axis=0))
                khat, rk = keys[h][0][band], keys[h][1][band]
                dgk_ref[...] += jnp.sum(dkn * khat, axis=0, keepdims=True)
                acc_ref[band, h * HP:(h + 1) * HP] += _rms_bwd(dkn, gkv, khat, rk, 1.0 / HD)
                acc_ref[band, (NKV + h) * HP:(NKV + h + 1) * HP] += dv
            dpm_ref[b * BLK:(b + 1) * BLK, O_Q:O_K] = jnp.concatenate(dqs, axis=1)
        dkvh_ref[...] = acc_ref[0:BLK, :]
        dkvm_ref[...] = acc_ref[BLK:, :]

    prev8 = lambda col: pl.BlockSpec((8, CC), lambda i: (jnp.maximum(i * r8 - 1, 0), col))
    next8 = lambda col: pl.BlockSpec((8, CC), lambda i: (jnp.minimum((i + 1) * r8, t // 8 - 1), col))
    small = lambda n: pl.BlockSpec((1, n), lambda i: (0, 0))
    return pl.pallas_call(
        body, name="mixer_bwd", grid=(nt,),
        in_specs=[
            pl.BlockSpec((tq, D), lambda i: (i, 0)),
            pl.BlockSpec((8, D), lambda i: (jnp.minimum((i + 1) * r8, t // 8 - 1), 0)),
            pl.BlockSpec((tq, NP), lambda i: (i, 0)),
            prev8(O_CG // CC), prev8(O_HC // CC),
            next8(O_BG // CC), next8(O_CG // CC), next8(O_HC // CC),
            pl.BlockSpec((BLK, kvw), lambda i: (jnp.maximum(i * nb - 1, 0), O_K // kvw)),
            _const_spec((8, CC)), _const_spec((1, HP)), _const_spec((1, HP)),
            pl.BlockSpec(memory_space=pltpu.SMEM),
            _const_spec((1, CC)), _const_spec((1, NQ * HP)), _const_spec((MIXW, D)),
        ],
        out_specs=[
            pl.BlockSpec((tq, NMAIN), lambda i: (i, 0)),
            pl.BlockSpec((tq, kvw), lambda i: (i, 0)),
            pl.BlockSpec((BLK, kvw), lambda i: (i, 0)),
            pl.BlockSpec((8, CC), lambda i: (0, 0)), small(HP), small(HP), small(HP), small(CC), small(NQ * HP),
        ],
        out_shape=[
            jax.ShapeDtypeStruct((t, NMAIN), BF16), jax.ShapeDtypeStruct((t, kvw), F32),
            jax.ShapeDtypeStruct((nt * BLK, kvw), F32),
            jax.ShapeDtypeStruct((8, CC), F32), jax.ShapeDtypeStruct((1, HP), F32), jax.ShapeDtypeStruct((1, HP), F32),
            jax.ShapeDtypeStruct((1, HP), F32), jax.ShapeDtypeStruct((1, CC), F32),
            jax.ShapeDtypeStruct((1, NQ * HP), F32),
        ],
        scratch_shapes=[pltpu.VMEM((tq + BLK, kvw), F32)],
        compiler_params=_cparams(("arbitrary",)),
    )(dxm, dxm, proj, proj, proj, proj, proj, proj, proj, cw, gq, gk, sinks, gco, gao, wo)


def _inproj_bwd(dpm, dkv, wp, x, g1, dxm, tm):
    t = x.shape[0]
    kvw = 2 * NKV * HP

    def body(dp_ref, dk_ref, w_ref, x_ref, g_ref, dxm_ref, dx_ref, dg_ref):
        @pl.when(pl.program_id(0) == 0)
        def _():
            dg_ref[...] = jnp.zeros_like(dg_ref)

        dh = _nt(dp_ref[...], w_ref[:, 0:NMAIN]) + _nt(dk_ref[...], w_ref[:, NMAIN:NP])
        r, xh = _rms_fwd(x_ref[...], 1.0 / D)
        dg_ref[...] += jnp.sum(dh * xh, axis=0, keepdims=True)
        dx_ref[...] = dxm_ref[...] + _rms_bwd(dh, g_ref[...], xh, r, 1.0 / D)

    row = lambda w: pl.BlockSpec((tm, w), lambda i: (i, 0))
    return pl.pallas_call(
        body, name="inproj_bwd", grid=(t // tm,),
        in_specs=[row(NMAIN), row(kvw), _const_spec((D, NP)), row(D), _const_spec((1, D)), row(D)],
        out_specs=[row(D), pl.BlockSpec((1, D), lambda i: (0, 0))],
        out_shape=[jax.ShapeDtypeStruct((t, D), F32), jax.ShapeDtypeStruct((1, D), F32)],
        compiler_params=_cparams(("arbitrary",)),
    )(dpm, dkv, wp, x, g1, dxm)


def _rows_tile(rows):
    for cand in (512, 256, 128, 64, 32, 16, 8):
        if rows % cand == 0:
            return cand
    return rows


def _presum_halves(gs, theirs, core):
    outs = []
    for n, (ga, ta) in enumerate(zip(gs, theirs)):
        _, hr, cols = ta.shape

        def body(c_ref, g_ref, t_ref, o_ref):
            o_ref[...] = (g_ref[...].astype(F32) + t_ref[...].astype(F32)).astype(BF16)

        half = pl.BlockSpec((None, hr, cols), lambda k, c_ref: (k, 0, 0))
        outs.append(pl.pallas_call(
            body, name=f"presum_{n}",
            grid_spec=pltpu.PrefetchScalarGridSpec(
                num_scalar_prefetch=1, grid=(N_CHIPS,),
                in_specs=[pl.BlockSpec((None, hr, cols), lambda k, c_ref: (k, c_ref[0], 0)), half],
                out_specs=half),
            out_shape=jax.ShapeDtypeStruct(ta.shape, BF16), compiler_params=_cparams(("parallel",)),
        )(core, ga, ta))
    return outs


def _sum_chips(cs, name):
    outs = []
    for n, ca in enumerate(cs):
        _, rows, cols = ca.shape
        tr = _rows_tile(rows)

        def body(c_ref, o_ref):
            acc = c_ref[0].astype(F32)
            for j in range(1, N_CHIPS):
                acc = acc + c_ref[j].astype(F32)
            o_ref[...] = acc

        outs.append(pl.pallas_call(
            body, name=f"{name}_{n}", grid=(rows // tr,),
            in_specs=[pl.BlockSpec((N_CHIPS, tr, cols), lambda i: (0, i, 0))],
            out_specs=pl.BlockSpec((tr, cols), lambda i: (i, 0)),
            out_shape=jax.ShapeDtypeStruct((rows, cols), F32), compiler_params=_cparams(("parallel",)),
        )(ca))
    return outs


def _adamw(w, g, m, v, name):
    rows, cols = w.shape
    tr = _rows_tile(rows)
    c1 = 1.0 - ADAM_B1 ** ADAM_STEP
    c2 = 1.0 - ADAM_B2 ** ADAM_STEP

    def body(w_ref, g_ref, m_ref, v_ref, d_ref, mo_ref, vo_ref):
        gv = g_ref[...]
        mn = ADAM_B1 * m_ref[...] + (1.0 - ADAM_B1) * gv
        vn = ADAM_B2 * v_ref[...] + (1.0 - ADAM_B2) * (gv * gv)
        mo_ref[...] = mn
        vo_ref[...] = vn
        d_ref[...] = -ADAM_LR * ((mn / c1) / (jnp.sqrt(vn / c2) + ADAM_EPS) + ADAM_WD * w_ref[...])

    spec = pl.BlockSpec((tr, cols), lambda i: (i, 0))
    sds = jax.ShapeDtypeStruct((rows, cols), F32)
    return pl.pallas_call(
        body, name=name, grid=(rows // tr,), in_specs=[spec] * 4, out_specs=[spec] * 3, out_shape=[sds] * 3,
        compiler_params=_cparams(("parallel",)),
    )(w, g, m, v)


def _place():
    x, y, c = lax.axis_index("x"), lax.axis_index("y"), lax.axis_index("c")
    chips = [(1 - x, y), (x, 1 - y), (1 - x, 1 - y)]
    return x, y, c, chips


ANY = pl.BlockSpec(memory_space=pl.ANY)
DMA_ROWS = 64


def _pieces(shape):
    rows = shape[-2]
    step = DMA_ROWS if rows % DMA_ROWS == 0 else rows
    lead = [()]
    for n in shape[:-2]:
        lead = [i + (k,) for i in lead for k in range(n)]
    return [i + (pl.ds(r0, step),) for i in lead for r0 in range(0, rows, step)]


def _start_pieces(make, src, dst):
    for idx in _pieces(src.shape):
        make(src.at[idx], dst.at[idx]).start()


def _gather_layer(blocks, layer):
    nw = len(blocks)

    def body(*refs):
        _gather_body(refs[:nw], refs[nw:2 * nw], refs[2 * nw:], layer, _start_pieces)

    return pl.pallas_call(
        body, name=f"gather_layer{layer}", in_specs=[ANY] * nw, out_specs=[ANY] * nw,
        out_shape=[jax.ShapeDtypeStruct((N_CHIPS,) + b.shape, b.dtype) for b in blocks],
        scratch_shapes=[pltpu.SemaphoreType.DMA((3, nw))] * 4,
        compiler_params=_cparams(has_side_effects=True),
    )(*blocks)


def _gather_body(srcs, outs, sems, layer, start):
    nw = len(srcs)
    ssem, rsem, fssem, frsem = sems
    x, y, c, chips = _place()
    kme = 2 * x + y

    def plane(j, w, to):
        return lambda s, d: pltpu.make_async_remote_copy(
            src_ref=s, dst_ref=d, send_sem=ssem.at[j, w], recv_sem=rsem.at[j, w], device_id=to,
            device_id_type=MESH)

    def passed(j, w):
        return lambda s, d: pltpu.make_async_remote_copy(
            src_ref=s, dst_ref=d, send_sem=fssem.at[j, w], recv_sem=frsem.at[j, w],
            device_id=(x, y, 1 - c), device_id_type=MESH)

    @pl.when(c == layer)
    def _():
        for j, (px, py) in enumerate(chips):
            for w in range(nw):
                start(plane(j, w, (px, py, c)), srcs[w], outs[w].at[kme])
        for j, (px, py) in enumerate(chips):
            for w in range(nw):
                got = outs[w].at[2 * px + py]
                plane(j, w, (px, py, c))(got, got).wait_recv()
                start(passed(j, w), got, got)
        for j, (px, py) in enumerate(chips):
            for w in range(nw):
                got = outs[w].at[2 * px + py]
                plane(j, w, (px, py, c))(got, got).wait_send()
                passed(j, w)(got, got).wait_send()

    @pl.when(c != layer)
    def _():
        for j, (px, py) in enumerate(chips):
            for w in range(nw):
                got = outs[w].at[2 * px + py]
                passed(j, w)(got, got).wait_recv()


def _handshake_all():
    x, y, c, _ = _place()
    barrier = pltpu.get_barrier_semaphore()
    for r in range(1, 8):
        peer = (x ^ (r >> 2), y ^ ((r >> 1) & 1), c ^ (r & 1))
        pl.semaphore_signal(barrier, inc=1, device_id=peer, device_id_type=MESH)
    pl.semaphore_wait(barrier, 7)


def _gather_layer_async(blocks, layer, collective_id):
    hbm = pltpu.MemorySpace.HBM
    srcs = [jax.new_ref(b, memory_space=hbm) for b in blocks]
    outs = [jax.empty_ref(jax.ShapeDtypeStruct((N_CHIPS,) + b.shape, b.dtype), memory_space=hbm) for b in blocks]

    @pl.kernel(mesh=plsc.ScalarSubcoreMesh(axis_name="seq", num_cores=1), name=f"gather_layer{layer}_seq",
               scratch_types=[pltpu.SemaphoreType.DMA((3, len(blocks)))] * 4,
               compiler_params=pltpu.CompilerParams(collective_id=collective_id))
    def launch(*sems):
        _handshake_all()
        _gather_body(srcs, outs, sems, layer, lambda make, s, d: make(s, d).start())

    launch()
    return [o[...] for o in outs]


def _swap_halves(gs):
    nw = len(gs)

    def body(*refs):
        srcs, theirs = refs[:nw], refs[nw:2 * nw]
        ssem, rsem = refs[2 * nw:]
        x, y, c, _ = _place()

        def give(w):
            return lambda s, d: pltpu.make_async_remote_copy(
                src_ref=s, dst_ref=d, send_sem=ssem.at[w], recv_sem=rsem.at[w], device_id=(x, y, 1 - c),
                device_id_type=MESH)

        for w in range(nw):
            hr = theirs[w].shape[1]
            _start_pieces(give(w), srcs[w].at[:, pl.ds((1 - c) * hr, hr)], theirs[w])
        for w in range(nw):
            give(w)(theirs[w], theirs[w]).wait()

    return pl.pallas_call(
        body, name="swap_halves", in_specs=[ANY] * nw, out_specs=[ANY] * nw,
        out_shape=[jax.ShapeDtypeStruct((g.shape[0], g.shape[1] // 2, g.shape[2]), g.dtype) for g in gs],
        scratch_shapes=[pltpu.SemaphoreType.DMA((nw,))] * 2,
        compiler_params=_cparams(has_side_effects=True),
    )(*gs)


def _scatter_chips(ps):
    nw = len(ps)

    def body(*refs):
        srcs, outs = refs[:nw], refs[nw:2 * nw]
        ssem, rsem = refs[2 * nw:]
        x, y, c, chips = _place()
        kme = 2 * x + y

        def give(j, w, to):
            return lambda s, d: pltpu.make_async_remote_copy(
                src_ref=s, dst_ref=d, send_sem=ssem.at[j, w], recv_sem=rsem.at[j, w], device_id=to,
                device_id_type=MESH)

        for j, (px, py) in enumerate(chips):
            for w in range(nw):
                _start_pieces(give(j, w, (px, py, c)), srcs[w].at[2 * px + py], outs[w].at[kme])
        for j, (px, py) in enumerate(chips):
            for w in range(nw):
                got = outs[w].at[2 * px + py]
                give(j, w, (px, py, c))(got, got).wait_recv()
        for j, (px, py) in enumerate(chips):
            for w in range(nw):
                sent = srcs[w].at[2 * px + py]
                give(j, w, (px, py, c))(sent, sent).wait_send()

    return pl.pallas_call(
        body, name="scatter_chips", in_specs=[ANY] * nw, out_specs=[ANY] * nw,
        out_shape=[jax.ShapeDtypeStruct(p.shape, p.dtype) for p in ps],
        scratch_shapes=[pltpu.SemaphoreType.DMA((3, nw)), pltpu.SemaphoreType.DMA((3, nw))],
        compiler_params=_cparams(has_side_effects=True),
    )(*ps)


def _swap_siblings(rs):
    nw = len(rs)

    def body(*refs):
        srcs, outs = refs[:nw], refs[nw:2 * nw]
        ssem, rsem = refs[2 * nw:]
        x, y, c, _ = _place()

        def give(w):
            return lambda s, d: pltpu.make_async_remote_copy(
                src_ref=s, dst_ref=d, send_sem=ssem.at[w], recv_sem=rsem.at[w], device_id=(x, y, 1 - c),
                device_id_type=MESH)

        for w in range(nw):
            _start_pieces(give(w), srcs[w], outs[w])
        for w in range(nw):
            give(w)(srcs[w], outs[w]).wait()

    return pl.pallas_call(
        body, name="swap_siblings", in_specs=[ANY] * nw, out_specs=[ANY] * nw,
        out_shape=[jax.ShapeDtypeStruct(r.shape, r.dtype) for r in rs],
        scratch_shapes=[pltpu.SemaphoreType.DMA((nw,))] * 2,
        compiler_params=_cparams(has_side_effects=True),
    )(*rs)


def _allreduce_small(v):
    rows = v.shape[0]

    def body(v_ref, o_ref, buf, ssem, rsem):
        x, y, c, _ = _place()
        me = 4 * x + 2 * y + c
        buf[me] = v_ref[...]
        sends = []
        for r in range(1, 8):
            peer = (x ^ (r >> 2), y ^ ((r >> 1) & 1), c ^ (r & 1))
            cp = pltpu.make_async_remote_copy(
                src_ref=v_ref, dst_ref=buf.at[me], send_sem=ssem.at[r - 1], recv_sem=rsem.at[r - 1],
                device_id=peer, device_id_type=MESH)
            cp.start()
            sends.append(cp)
        for r in range(1, 8):
            src = me ^ r
            pltpu.make_async_remote_copy(
                src_ref=v_ref, dst_ref=buf.at[src], send_sem=ssem.at[r - 1], recv_sem=rsem.at[r - 1],
                device_id=(x, y, c), device_id_type=MESH).wait_recv()
        for cp in sends:
            cp.wait_send()
        acc = buf[0]
        for d in range(1, 8):
            acc = acc + buf[d]
        o_ref[...] = acc

    vm = pl.BlockSpec(memory_space=pltpu.VMEM)
    return pl.pallas_call(
        body, name="allreduce_small", in_specs=[vm], out_specs=vm,
        out_shape=jax.ShapeDtypeStruct(v.shape, F32),
        scratch_shapes=[pltpu.VMEM((8, rows, 128), F32), pltpu.SemaphoreType.DMA((7,)),
                        pltpu.SemaphoreType.DMA((7,))],
        compiler_params=_cparams(has_side_effects=True),
    )(v)


def _pad_heads(w, n_heads, axis):
    shp = w.shape
    w = w.reshape(shp[:axis] + (n_heads, HD) + shp[axis + 1:])
    pad = [(0, 0)] * w.ndim
    pad[axis + 1] = (0, HP - HD)
    w = jnp.pad(w, pad)
    return w.reshape(shp[:axis] + (n_heads * HP,) + shp[axis + 1:])


def _strip_heads(w, n_heads, axis):
    shp = w.shape
    w = w.reshape(shp[:axis] + (n_heads, HP) + shp[axis + 1:])
    w = lax.slice_in_dim(w, 0, HD, axis=axis + 1)
    return w.reshape(shp[:axis] + (n_heads * HD,) + shp[axis + 1:])


def _unshard_cols(g4):
    k, r, c = g4.shape
    return jnp.transpose(g4, (1, 0, 2)).reshape(r, k * c)


def _shard_cols(w):
    r, n = w.shape
    return jnp.transpose(w.reshape(r, N_CHIPS, n // N_CHIPS), (1, 0, 2))


def _pad_win(win):
    parts = [win[:, :3 * CC], _pad_heads(win[:, 3 * CC:3 * CC + NQ * HD], NQ, 1),
             _pad_heads(win[:, 3 * CC + NQ * HD:3 * CC + (NQ + NKV) * HD], NKV, 1),
             _pad_heads(win[:, 3 * CC + (NQ + NKV) * HD:], NKV, 1)]
    return jnp.concatenate(parts, axis=1)


def _strip_win(gp):
    parts = [gp[:, :3 * CC], _strip_heads(gp[:, O_Q:O_K], NQ, 1), _strip_heads(gp[:, O_K:O_V], NKV, 1),
             _strip_heads(gp[:, O_V:], NKV, 1)]
    return jnp.concatenate(parts, axis=1)


def _count(shape):
    n = 1
    for s in shape:
        n *= s
    return n


def _pack_rows(arrs):
    flat = [jnp.pad(a.reshape(-1), (0, (-_count(a.shape)) % 128)) for a in arrs]
    v = jnp.concatenate(flat)
    rows = -(-v.shape[0] // (8 * 128)) * 8
    return jnp.pad(v, (0, rows * 128 - v.shape[0])).reshape(rows, 128)


def kernel(x, norm1_g, w_in, conv_w, q_norm_g, k_norm_g, sinks, conv_out_g, attn_out_g, w_o, norm2_g, w_gate, w_up, w_down, loss_target, m_norm1_g, m_w_in, m_conv_w, m_q_norm_g, m_k_norm_g, m_sinks, m_conv_out_g, m_attn_out_g, m_w_o, m_norm2_g, m_w_gate, m_w_up, m_w_down, v_norm1_g, v_w_in, v_conv_w, v_q_norm_g, v_k_norm_g, v_sinks, v_conv_out_g, v_attn_out_g, v_w_o, v_norm2_g, v_w_gate, v_w_up, v_w_down):
    depth = w_in.shape[0]
    t = x.shape[1]
    xs = x.reshape(t, D)
    tgt = loss_target.reshape(t, D)
    xi, yi = lax.axis_index("x"), lax.axis_index("y")
    kme = 2 * xi + yi
    tm = min(512, t)
    tq = min(256, t)
    tf = min(256, t)

    cwp = jnp.pad(conv_w.reshape(depth * 3, CC // N_CHIPS), ((0, 8 - depth * 3), (0, 0)))
    own = [[jnp.concatenate([w_gate[l], w_up[l]], axis=0).astype(BF16),
            jnp.concatenate([w_down[l], w_o[l]], axis=0).astype(BF16), w_in[l].astype(BF16)]
           for l in range(depth)]
    own[0].append(cwp)
    got0 = _gather_layer(own[0], 0)
    own[1] = lax.optimization_barrier((own[1], got0))[0]
    arrived = [got0, _gather_layer_async(own[1], 1, collective_id=1)]

    def layer_params(l, got):
        ga, gb, gc = [lax.dynamic_update_index_in_dim(g, o, kme, 0) for g, o in zip(got[:3], own[l][:3])]
        wo = gb[:, FFB:].reshape(D, D)
        return dict(
            wp=_pad_win(_unshard_cols(gc)), ga=ga, gb=gb,
            wo=jnp.concatenate([wo[:CC], _pad_heads(wo[CC:], NQ, 0)], axis=0),
            cw=jnp.pad(cw_full[l], ((0, 5), (0, 0))),
            g1=norm1_g[l].reshape(1, D), g2=norm2_g[l].reshape(1, D),
            gq=jnp.pad(q_norm_g[l], (0, HP - HD)).reshape(1, HP), gk=jnp.pad(k_norm_g[l], (0, HP - HD)).reshape(1, HP),
            sk=sinks[l].reshape(1, NQ), gco=conv_out_g[l].reshape(1, CC),
            gao=_pad_heads(attn_out_g[l], NQ, 0).reshape(1, NQ * HP))

    cw_full = _unshard_cols(lax.dynamic_update_index_in_dim(got0[3], cwp, kme, 0))[:depth * 3].reshape(depth, 3, CC)

    saved, layers = [], []
    cur = xs
    for l in range(depth):
        p = layer_params(l, arrived[l] if l == 0 else lax.optimization_barrier((arrived[l], cur))[0])
        layers.append(p)
        proj, h = _inproj_fwd(cur, p["g1"], p["wp"], tm)
        xm, mix = _mixer_fwd(proj, cur, p["cw"], p["gq"], p["gk"], p["sk"], p["gco"], p["gao"], p["wo"], tq)
        xo, a, b, h2 = _ffn_fwd(xm, p["g2"], p["ga"], p["gb"], tf)
        saved.append(dict(x=cur, proj=proj, h=h, xm=xm, mix=mix, a=a, b=b, h2=h2))
        cur = xo
    lpart, dy = _loss_and_grad(cur, tgt, tm)
    loss = lax.psum(lpart[0, 0], ("x", "y", "c"))

    nt = t // tq
    ci = lax.axis_index("c")
    core = ci.reshape(1).astype(jnp.int32)
    rbig = [None] * depth
    gsmall = [None] * depth
    for l in reversed(range(depth)):
        p, s = layers[l], saved[l]
        dxm, da, db, hm, dg2 = _ffn_bwd(dy, s["xm"], p["g2"], s["a"], s["b"], p["ga"], p["gb"], tf)
        g_wg = _wgrad_blocks(s["h2"], da, tm, "wgrad_gate")
        g_wu = _wgrad_blocks(s["h2"], db, tm, "wgrad_up")
        g_wd = _wgrad_blocks(hm, dy, tm, "wgrad_down")
        dpm, dkvm, dkvh, dcw, dgq, dgk, dsk, dgco, dgao = _mixer_bwd(
            dxm, s["proj"], p["cw"], p["gq"], p["gk"], p["sk"], p["gco"], p["gao"], p["wo"], tq)
        g_wot = _wgrad(dxm, s["mix"], MIXW, tm, "wgrad_o")
        kvw = dkvm.shape[1]
        halo = jnp.concatenate([dkvh.reshape(nt, BLK, kvw)[1:], jnp.zeros((1, BLK, kvw), F32)], axis=0)
        halo = jnp.pad(halo, ((0, 0), (tq - BLK, 0), (0, 0)))
        dkv = (dkvm.reshape(nt, tq, kvw) + halo).reshape(t, kvw).astype(BF16)
        dx, dg1 = _inproj_bwd(dpm, dkv, p["wp"], s["x"], p["g1"], dxm, tm)
        g_wpm = _wgrad(s["h"], dpm, NMAIN // 2, tm, "wgrad_in_main")
        g_wpk = _wgrad(s["h"], dkv, kvw, tm, "wgrad_in_kv")
        dy = dx
        g_in = _strip_win(jnp.concatenate([g_wpm, g_wpk], axis=1))
        g_ot = jnp.concatenate([g_wot[:, :CC], _strip_heads(g_wot[:, CC:], NQ, 1)], axis=1)
        gsmall[l] = dict(g1=dg1, cw=dcw[:3], gq=dgq[0, :HD], gk=dgk[0, :HD], sk=dsk[0, :NQ], gco=dgco,
                         gao=_strip_heads(dgao.reshape(NQ * HP), NQ, 0), g2=dg2)
        gs = [_shard_cols(g_in.astype(BF16)), _shard_cols(g_ot.astype(BF16)), g_wg, g_wu, g_wd]
        ps = _presum_halves(gs, _swap_halves(gs), core)
        cs = [lax.dynamic_update_index_in_dim(got, lax.dynamic_index_in_dim(q, kme, 0, keepdims=False), kme, 0)
              for got, q in zip(_scatter_chips(ps), ps)]
        r_mine = _sum_chips(cs, "chipsum")
        r_theirs = _swap_siblings(r_mine)
        r_in, r_ot, r_g, r_u, r_d = [
            jnp.where(ci == 0, jnp.concatenate([a, b], axis=0), jnp.concatenate([b, a], axis=0))
            for a, b in zip(r_mine, r_theirs)]
        rbig[l] = [r_in, r_ot.T, r_g, r_u, r_d]
    grad_x = dy.reshape(x.shape)
    g_big = [jnp.stack([rbig[l][w] for l in range(depth)]) for w in range(5)]

    small_shapes = dict(g1=(D,), cw=(3, CC), gq=(HD,), gk=(HD,), sk=(NQ,), gco=(CC,), gao=(NQ * HD,), g2=(D,))
    red = _allreduce_small(_pack_rows([gsmall[l][n] for l in range(depth) for n in small_shapes])).reshape(-1)
    red_small, offs = {n: [] for n in small_shapes}, 0
    for l in range(depth):
        for n, shp in small_shapes.items():
            cnt = _count(shp)
            red_small[n].append(red[offs:offs + cnt].reshape(shp))
            offs += -(-cnt // 128) * 128
    g_small = {n: jnp.stack(v) for n, v in red_small.items()}
    g_cw = lax.dynamic_slice_in_dim(g_small["cw"], kme * (CC // N_CHIPS), CC // N_CHIPS, axis=2)

    weights = [norm1_g, w_in, conv_w, q_norm_g, k_norm_g, sinks, conv_out_g, attn_out_g, w_o, norm2_g, w_gate,
               w_up, w_down]
    moms = [m_norm1_g, m_w_in, m_conv_w, m_q_norm_g, m_k_norm_g, m_sinks, m_conv_out_g, m_attn_out_g, m_w_o,
            m_norm2_g, m_w_gate, m_w_up, m_w_down]
    vars_ = [v_norm1_g, v_w_in, v_conv_w, v_q_norm_g, v_k_norm_g, v_sinks, v_conv_out_g, v_attn_out_g, v_w_o,
             v_norm2_g, v_w_gate, v_w_up, v_w_down]
    grads = [g_small["g1"], g_big[0], g_cw, g_small["gq"], g_small["gk"], g_small["sk"], g_small["gco"],
             g_small["gao"], g_big[1], g_small["g2"], g_big[2], g_big[3], g_big[4]]
    n_w = len(weights)
    big_idx = [1, 8, 10, 11, 12]
    small_idx = [n for n in range(n_w) if n not in big_idx]
    deltas, new_m, new_v = [None] * n_w, [None] * n_w, [None] * n_w
    for n in big_idx:
        shp = weights[n].shape
        two = [a3.reshape(shp[0] * shp[1], shp[2]) for a3 in (weights[n], grads[n], moms[n], vars_[n])]
        res = _adamw(*two, f"adamw_{n}")
        deltas[n], new_m[n], new_v[n] = [r.reshape(shp) for r in res]
    res = _adamw(*[_pack_rows([arrs[n] for n in small_idx]) for arrs in (weights, grads, moms, vars_)],
                 "adamw_small")
    offs = 0
    for n in small_idx:
        shp = weights[n].shape
        cnt = _count(shp)
        deltas[n], new_m[n], new_v[n] = [r.reshape(-1)[offs:offs + cnt].reshape(shp) for r in res]
        offs += -(-cnt // 128) * 128
    return (loss, grad_x, *grads, *deltas, *new_m, *new_v)
```

```python
import functools

import jax
import jax.numpy as jnp
from jax import lax
from jax.experimental import pallas as pl
from jax.experimental.pallas import tpu as pltpu
from jax.experimental.pallas import tpu_sc as plsc

F32 = jnp.float32
BF16 = jnp.bfloat16

D = 1024
CC = 512
NQ = 8
NKV = 2
HD = 64
HP = 128
GRP = NQ // NKV
FF = 2816
FFB = FF // 4
BLK = 128
EPS = 1e-6
NEG = -1e30
SCALE = HD ** -0.5
O_BG, O_CG, O_HC, O_Q = 0, CC, 2 * CC, 3 * CC
O_K = O_Q + NQ * HP
O_V = O_K + NKV * HP
NP = O_V + NKV * HP
NMAIN = O_K
MIXW = CC + NQ * HP
N_CHIPS = 4
VMEM_LIMIT = 56 * 1024 * 1024
MESH = pl.DeviceIdType.MESH

ADAM_LR, ADAM_B1, ADAM_B2, ADAM_EPS, ADAM_WD, ADAM_STEP = 0.001, 0.9, 0.999, 1e-08, 0.01, 10


def _cparams(sem=None, **kw):
    if sem is not None:
        kw["dimension_semantics"] = sem
    return pltpu.CompilerParams(vmem_limit_bytes=VMEM_LIMIT, **kw)


def _const_spec(shape):
    nd = len(shape)
    return pl.BlockSpec(shape, lambda *_: (0,) * nd, pipeline_mode=pl.Buffered(1))


def _nt(a, b):
    return lax.dot_general(a, b, (((1,), (1,)), ((), ())), preferred_element_type=F32)


def _tn(a, b):
    return lax.dot_general(a, b, (((0,), (0,)), ((), ())), preferred_element_type=F32)


def _rms_fwd(x, inv_n):
    r = lax.rsqrt(jnp.sum(x * x, axis=-1, keepdims=True) * inv_n + EPS)
    return r, x * r


def _rms_bwd(dy, g, xh, r, inv_n):
    dxh = dy * g
    return r * (dxh - xh * (jnp.sum(dxh * xh, axis=-1, keepdims=True) * inv_n))


def _inproj_fwd(x, g1, wp, tm):
    t = x.shape[0]

    def body(x_ref, g_ref, w_ref, p_ref, h_ref):
        _, xh = _rms_fwd(x_ref[...], 1.0 / D)
        h = (xh * g_ref[...]).astype(BF16)
        h_ref[...] = h
        p_ref[...] = jnp.dot(h, w_ref[...], preferred_element_type=F32)

    return pl.pallas_call(
        body, name="inproj_fwd", grid=(t // tm,),
        in_specs=[pl.BlockSpec((tm, D), lambda i: (i, 0)), _const_spec((1, D)), _const_spec((D, NP))],
        out_specs=[pl.BlockSpec((tm, NP), lambda i: (i, 0)), pl.BlockSpec((tm, D), lambda i: (i, 0))],
        out_shape=[jax.ShapeDtypeStruct((t, NP), F32), jax.ShapeDtypeStruct((t, D), BF16)],
        compiler_params=_cparams(("parallel",)),
    )(x, g1, wp)


def _band_mask():
    r_io = lax.broadcasted_iota(jnp.int32, (BLK, 2 * BLK), 0)
    c_io = lax.broadcasted_iota(jnp.int32, (BLK, 2 * BLK), 1)
    return (c_io > r_io) & (c_io <= r_io + BLK), c_io


def _conv_taps(uf, n):
    u1 = pltpu.roll(uf, 1, 0)[8:8 + n]
    u2 = pltpu.roll(uf, 2, 0)[8:8 + n]
    return u1, u2


def _attn_probs(qn, kband, sink, valid):
    s = _nt(qn, kband) * SCALE
    s = jnp.where(valid, s, NEG)
    m = jnp.maximum(jnp.max(s, axis=-1, keepdims=True), sink)
    p = jnp.exp(s - m)
    es = jnp.exp(sink - m)
    inv = 1.0 / (jnp.sum(p, axis=-1, keepdims=True) + es)
    return p * inv, es * inv


def _norm_keys(kraw, gk):
    out = []
    for h in range(NKV):
        kh = kraw[:, h * HP:(h + 1) * HP]
        rk, khat = _rms_fwd(kh, 1.0 / HD)
        out.append((khat, rk, (khat * gk).astype(BF16)))
    return out


def _mixer_fwd(proj, x, cw, gq, gk, sinks, gco, gao, wo, tq):
    t = proj.shape[0]
    nb = tq // BLK
    r8 = tq // 8

    def body(p_ref, cgp_ref, hcp_ref, kvp_ref, x_ref, cw_ref, gq_ref, gk_ref, sk_ref, gco_ref, gao_ref,
             wo_ref, xm_ref, mix_ref):
        i = pl.program_id(0)
        cg = p_ref[:, O_CG:O_CG + CC]
        hc = p_ref[:, O_HC:O_HC + CC]
        u = cg * hc
        up = jnp.where(i > 0, cgp_ref[...] * hcp_ref[...], 0.0)
        u1, u2 = _conv_taps(jnp.concatenate([up, u], axis=0), tq)
        y = cw_ref[0:1, :] * u2 + cw_ref[1:2, :] * u1 + cw_ref[2:3, :] * u
        co = p_ref[:, O_BG:O_BG + CC] * y
        _, coh = _rms_fwd(co, 1.0 / CC)
        cn = coh * gco_ref[...]
        kraw = jnp.concatenate([kvp_ref[:, 0:NKV * HP], p_ref[:, O_K:O_K + NKV * HP]], axis=0)
        vraw = jnp.concatenate([kvp_ref[:, NKV * HP:], p_ref[:, O_V:O_V + NKV * HP]], axis=0)
        keys = _norm_keys(kraw, gk_ref[...])
        vb = [vraw[:, h * HP:(h + 1) * HP].astype(BF16) for h in range(NKV)]
        base_valid, c_io = _band_mask()
        rows = []
        for b in range(nb):
            lo = jnp.where(i * nb + b == 0, BLK, 0)
            valid = base_valid & (c_io >= lo)
            outs = []
            for g in range(NQ):
                h = g // GRP
                qg = p_ref[b * BLK:(b + 1) * BLK, O_Q + g * HP:O_Q + (g + 1) * HP]
                _, qh = _rms_fwd(qg, 1.0 / HD)
                qn = (qh * gq_ref[...]).astype(BF16)
                pr, _ = _attn_probs(qn, keys[h][2][b * BLK:b * BLK + 2 * BLK], sk_ref[0, g], valid)
                outs.append(jnp.dot(pr.astype(BF16), vb[h][b * BLK:b * BLK + 2 * BLK],
                                    preferred_element_type=F32))
            rows.append(jnp.concatenate(outs, axis=1))
        ao = jnp.concatenate(rows, axis=0)
        _, aoh = _rms_fwd(ao, 1.0 / (NQ * HD))
        an = aoh * gao_ref[...]
        mix = jnp.concatenate([cn, an], axis=1).astype(BF16)
        mix_ref[...] = mix
        xm_ref[...] = x_ref[...] + jnp.dot(mix, wo_ref[...], preferred_element_type=F32)

    prev8 = lambda col: pl.BlockSpec((8, CC), lambda i: (jnp.maximum(i * r8 - 1, 0), col))
    return pl.pallas_call(
        body, name="mixer_fwd", grid=(t // tq,),
        in_specs=[
            pl.BlockSpec((tq, NP), lambda i: (i, 0)),
            prev8(O_CG // CC), prev8(O_HC // CC),
            pl.BlockSpec((BLK, 2 * NKV * HP), lambda i: (jnp.maximum(i * nb - 1, 0), O_K // (2 * NKV * HP))),
            pl.BlockSpec((tq, D), lambda i: (i, 0)),
            _const_spec((8, CC)), _const_spec((1, HP)), _const_spec((1, HP)),
            pl.BlockSpec(memory_space=pltpu.SMEM),
            _const_spec((1, CC)), _const_spec((1, NQ * HP)), _const_spec((MIXW, D)),
        ],
        out_specs=[pl.BlockSpec((tq, D), lambda i: (i, 0)), pl.BlockSpec((tq, MIXW), lambda i: (i, 0))],
        out_shape=[jax.ShapeDtypeStruct((t, D), F32), jax.ShapeDtypeStruct((t, MIXW), BF16)],
        compiler_params=_cparams(("parallel",)),
    )(proj, proj, proj, proj, x, cw, gq, gk, sinks, gco, gao, wo)


def _ffn_weight_specs():
    return [pl.BlockSpec((N_CHIPS, D, FFB), lambda i: (0, 0, 0), pipeline_mode=pl.Buffered(1)),
            pl.BlockSpec((N_CHIPS, D, FFB), lambda i: (0, 1, 0), pipeline_mode=pl.Buffered(1)),
            pl.BlockSpec((N_CHIPS, FFB, D), lambda i: (0, 0, 0), pipeline_mode=pl.Buffered(1))]


def _ffn_fwd(xm, g2, ga, gb, tm):
    t = xm.shape[0]

    def body(x_ref, g_ref, wg_ref, wu_ref, wd_ref, xo_ref, a_ref, b_ref, h2_ref):
        xv = x_ref[...]
        _, xh = _rms_fwd(xv, 1.0 / D)
        h2 = (xh * g_ref[...]).astype(BF16)
        h2_ref[...] = h2
        acc = xv
        for k in range(N_CHIPS):
            a = jnp.dot(h2, wg_ref[k], preferred_element_type=F32)
            b = jnp.dot(h2, wu_ref[k], preferred_element_type=F32)
            a_ref[k] = a.astype(BF16)
            b_ref[k] = b.astype(BF16)
            hm = (a * jax.nn.sigmoid(a) * b).astype(BF16)
            acc = acc + jnp.dot(hm, wd_ref[k], preferred_element_type=F32)
        xo_ref[...] = acc

    row = lambda w: pl.BlockSpec((tm, w), lambda i: (i, 0))
    blk = pl.BlockSpec((N_CHIPS, tm, FFB), lambda i: (0, i, 0))
    return pl.pallas_call(
        body, name="ffn_fwd", grid=(t // tm,),
        in_specs=[row(D), _const_spec((1, D))] + _ffn_weight_specs(),
        out_specs=[row(D), blk, blk, row(D)],
        out_shape=[jax.ShapeDtypeStruct((t, D), F32), jax.ShapeDtypeStruct((N_CHIPS, t, FFB), BF16),
                   jax.ShapeDtypeStruct((N_CHIPS, t, FFB), BF16), jax.ShapeDtypeStruct((t, D), BF16)],
        compiler_params=_cparams(("parallel",)),
    )(xm, g2, ga, ga, gb)


def _loss_and_grad(y, tgt, tm):
    t = y.shape[0]

    def body(y_ref, t_ref, l_ref, dy_ref):
        @pl.when(pl.program_id(0) == 0)
        def _():
            l_ref[...] = jnp.zeros_like(l_ref)

        e = y_ref[...] - t_ref[...]
        dy_ref[...] = e * (1.0 / D)
        s = jnp.sum(jnp.sum(e * e, axis=-1, keepdims=True), axis=0, keepdims=True)
        l_ref[...] += s * (0.5 / D)

    row = pl.BlockSpec((tm, D), lambda i: (i, 0))
    return pl.pallas_call(
        body, name="loss", grid=(t // tm,), in_specs=[row, row],
        out_specs=[pl.BlockSpec((8, 128), lambda i: (0, 0)), row],
        out_shape=[jax.ShapeDtypeStruct((8, 128), F32), jax.ShapeDtypeStruct((t, D), F32)],
        compiler_params=_cparams(("arbitrary",)),
    )(y, tgt)


def _ffn_bwd(dy, xm, g2, a, b, ga, gb, tm):
    t = dy.shape[0]

    def body(dy_ref, x_ref, g_ref, a_ref, b_ref, wg_ref, wu_ref, wd_ref, dx_ref, da_ref, db_ref, hm_ref, dg_ref):
        @pl.when(pl.program_id(0) == 0)
        def _():
            dg_ref[...] = jnp.zeros_like(dg_ref)

        dyv = dy_ref[...]
        dyb = dyv.astype(BF16)
        dh2 = jnp.zeros_like(dyv)
        for k in range(N_CHIPS):
            dhm = _nt(dyb, wd_ref[k])
            av = a_ref[k].astype(F32)
            bv = b_ref[k].astype(F32)
            sig = jax.nn.sigmoid(av)
            sil = av * sig
            hm_ref[k] = (sil * bv).astype(BF16)
            da = (dhm * bv * (sig * (1.0 + av * (1.0 - sig)))).astype(BF16)
            db = (dhm * sil).astype(BF16)
            da_ref[k] = da
            db_ref[k] = db
            dh2 = dh2 + _nt(da, wg_ref[k]) + _nt(db, wu_ref[k])
        r, xh = _rms_fwd(x_ref[...], 1.0 / D)
        dg_ref[...] += jnp.sum(dh2 * xh, axis=0, keepdims=True)
        dx_ref[...] = dyv + _rms_bwd(dh2, g_ref[...], xh, r, 1.0 / D)

    row = lambda w: pl.BlockSpec((tm, w), lambda i: (i, 0))
    blk = pl.BlockSpec((N_CHIPS, tm, FFB), lambda i: (0, i, 0))
    bsd = jax.ShapeDtypeStruct((N_CHIPS, t, FFB), BF16)
    return pl.pallas_call(
        body, name="ffn_bwd", grid=(t // tm,),
        in_specs=[row(D), row(D), _const_spec((1, D)), blk, blk] + _ffn_weight_specs(),
        out_specs=[row(D), blk, blk, blk, pl.BlockSpec((1, D), lambda i: (0, 0))],
        out_shape=[jax.ShapeDtypeStruct((t, D), F32), bsd, bsd, bsd, jax.ShapeDtypeStruct((1, D), F32)],
        compiler_params=_cparams(("arbitrary",)),
    )(dy, xm, g2, a, b, ga, ga, gb)


def _wgrad_blocks(a, b, tt, name):
    a_blocks = a.ndim == 3
    t = a.shape[1] if a_blocks else a.shape[0]
    rows = a.shape[2] if a_blocks else a.shape[1]
    cols = b.shape[1] if a_blocks else b.shape[2]

    def body(a_ref, b_ref, o_ref, acc_ref):
        s = pl.program_id(0)

        @pl.when(s == 0)
        def _():
            acc_ref[...] = jnp.zeros_like(acc_ref)

        if a_blocks:
            bv = b_ref[...].astype(BF16)
            for k in range(N_CHIPS):
                acc_ref[k] += _tn(a_ref[k], bv)
        else:
            av = a_ref[...].astype(BF16)
            for k in range(N_CHIPS):
                acc_ref[k] += _tn(av, b_ref[k])

        @pl.when(s == pl.num_programs(0) - 1)
        def _():
            o_ref[...] = acc_ref[...].astype(BF16)

    tok = lambda w: pl.BlockSpec((tt, w), lambda s: (s, 0))
    blk = lambda w: pl.BlockSpec((N_CHIPS, tt, w), lambda s: (0, s, 0))
    return pl.pallas_call(
        body, name=name, grid=(t // tt,),
        in_specs=[blk(rows), tok(cols)] if a_blocks else [tok(rows), blk(cols)],
        out_specs=pl.BlockSpec((N_CHIPS, rows, cols), lambda s: (0, 0, 0)),
        out_shape=jax.ShapeDtypeStruct((N_CHIPS, rows, cols), BF16),
        scratch_shapes=[pltpu.VMEM((N_CHIPS, rows, cols), F32)],
        compiler_params=_cparams(("arbitrary",)),
    )(a, b)


def _wgrad(a, b, tn, tt, name):
    t, k = a.shape
    n = b.shape[1]
    nsteps = t // tt

    def body(a_ref, b_ref, o_ref):
        @pl.when(pl.program_id(1) == 0)
        def _():
            o_ref[...] = jnp.zeros_like(o_ref)

        o_ref[...] += _tn(a_ref[...].astype(BF16), b_ref[...].astype(BF16))

    return pl.pallas_call(
        body, name=name, grid=(n // tn, nsteps),
        in_specs=[pl.BlockSpec((tt, k), lambda j, s: (s, 0)), pl.BlockSpec((tt, tn), lambda j, s: (s, j))],
        out_specs=pl.BlockSpec((k, tn), lambda j, s: (0, j)),
        out_shape=jax.ShapeDtypeStruct((k, n), F32),
        compiler_params=_cparams(("parallel", "arbitrary")),
    )(a, b)


def _mixer_bwd(dxm, proj, cw, gq, gk, sinks, gco, gao, wo, tq):
    t = proj.shape[0]
    nb = tq // BLK
    r8 = tq // 8
    nt = t // tq
    te = tq + 8
    kvw = 2 * NKV * HP

    def body(dx_ref, dxn_ref, p_ref, cgp_ref, hcp_ref, bgn_ref, cgn_ref, hcn_ref, kvp_ref, cw_ref, gq_ref,
             gk_ref, sk_ref, gco_ref, gao_ref, wo_ref,
             dpm_ref, dkvm_ref, dkvh_ref, dcw_ref, dgq_ref, dgk_ref, dsk_ref, dgco_ref, dgao_ref, acc_ref):
        i = pl.program_id(0)

        @pl.when(i == 0)
        def _():
            for r in (dcw_ref, dgq_ref, dgk_ref, dsk_ref, dgco_ref, dgao_ref):
                r[...] = jnp.zeros_like(r)

        acc_ref[...] = jnp.zeros_like(acc_ref)
        live_rows = jnp.where(i < nt - 1, te, tq)
        dxb = dx_ref[...].astype(BF16)
        dxe = jnp.concatenate([dxb, dxn_ref[...].astype(BF16)], axis=0)
        dcn = _nt(dxe, wo_ref[0:CC, :])
        bg = jnp.concatenate([p_ref[:, O_BG:O_BG + CC], bgn_ref[...]], axis=0)
        cg = jnp.concatenate([p_ref[:, O_CG:O_CG + CC], cgn_ref[...]], axis=0)
        hc = jnp.concatenate([p_ref[:, O_HC:O_HC + CC], hcn_ref[...]], axis=0)
        u = cg * hc
        up = jnp.where(i > 0, cgp_ref[...] * hcp_ref[...], 0.0)
        u1, u2 = _conv_taps(jnp.concatenate([up, u], axis=0), te)
        w0, w1, w2 = cw_ref[0:1, :], cw_ref[1:2, :], cw_ref[2:3, :]
        y = w0 * u2 + w1 * u1 + w2 * u
        co = bg * y
        rc, coh = _rms_fwd(co, 1.0 / CC)
        dco = _rms_bwd(dcn, gco_ref[...], coh, rc, 1.0 / CC)
        row_io = lax.broadcasted_iota(jnp.int32, (te, 1), 0)
        own = row_io < tq
        dgco_ref[...] += jnp.sum(jnp.where(own, dcn * coh, 0.0), axis=0, keepdims=True)
        dyc = jnp.where(row_io < live_rows, dco * bg, 0.0)
        dyo = jnp.where(own, dyc, 0.0)
        dcw_ref[0:1, :] += jnp.sum(dyo * u2, axis=0, keepdims=True)
        dcw_ref[1:2, :] += jnp.sum(dyo * u1, axis=0, keepdims=True)
        dcw_ref[2:3, :] += jnp.sum(dyo * u, axis=0, keepdims=True)
        dy1 = pltpu.roll(dyc, te - 1, 0)[0:tq]
        dy2 = pltpu.roll(dyc, te - 2, 0)[0:tq]
        du = w2 * dyc[0:tq] + w1 * dy1 + w0 * dy2
        dpm_ref[:, O_BG:O_BG + CC] = (dco[0:tq] * y[0:tq]).astype(BF16)
        dpm_ref[:, O_CG:O_CG + CC] = (du * hc[0:tq]).astype(BF16)
        dpm_ref[:, O_HC:O_HC + CC] = (du * cg[0:tq]).astype(BF16)
        dan = _nt(dxb, wo_ref[CC:MIXW, :])
        kraw = jnp.concatenate([kvp_ref[:, 0:NKV * HP], p_ref[:, O_K:O_K + NKV * HP]], axis=0)
        vraw = jnp.concatenate([kvp_ref[:, NKV * HP:], p_ref[:, O_V:O_V + NKV * HP]], axis=0)
        gqv, gkv = gq_ref[...], gk_ref[...]
        keys = _norm_keys(kraw, gkv)
        vb = [vraw[:, h * HP:(h + 1) * HP].astype(BF16) for h in range(NKV)]
        base_valid, c_io = _band_mask()
        lane = lax.broadcasted_iota(jnp.int32, (1, HP), 1)
        for b in range(nb):
            lo = jnp.where(i * nb + b == 0, BLK, 0)
            valid = base_valid & (c_io >= lo)
            band = slice(b * BLK, b * BLK + 2 * BLK)
            qs, prs, pss, outs = [], [], [], []
            for g in range(NQ):
                h = g // GRP
                qg = p_ref[b * BLK:(b + 1) * BLK, O_Q + g * HP:O_Q + (g + 1) * HP]
                rq, qh = _rms_fwd(qg, 1.0 / HD)
                qn = (qh * gqv).astype(BF16)
                pr, ps = _attn_probs(qn, keys[h][2][band], sk_ref[0, g], valid)
                qs.append((rq, qh, qn))
                prs.append(pr)
                pss.append(ps)
                outs.append(jnp.dot(pr.astype(BF16), vb[h][band], preferred_element_type=F32))
            ao = jnp.concatenate(outs, axis=1)
            ra, aoh = _rms_fwd(ao, 1.0 / (NQ * HD))
            danb = dan[b * BLK:(b + 1) * BLK]
            dgao_ref[...] += jnp.sum(danb * aoh, axis=0, keepdims=True)
            dao = _rms_bwd(danb, gao_ref[...], aoh, ra, 1.0 / (NQ * HD))
            dqs = []
            for h in range(NKV):
                dss, dobs = [], []
                for g in range(h * GRP, (h + 1) * GRP):
                    rq, qh, qn = qs[g]
                    dob = dao[:, g * HP:(g + 1) * HP].astype(BF16)
                    dp = _nt(dob, vb[h][band])
                    delta = jnp.sum(prs[g] * dp, axis=-1, keepdims=True)
                    dsb = (prs[g] * (dp - delta) * SCALE).astype(BF16)
                    dsk = -jnp.sum(pss[g] * delta, axis=0, keepdims=True)
                    dsk_ref[...] += jnp.where(lane == g, dsk, 0.0)
                    dqn = jnp.dot(dsb, keys[h][2][band], preferred_element_type=F32)
                    dgq_ref[...] += jnp.sum(dqn * qh, axis=0, keepdims=True)
                    dqs.append(_rms_bwd(dqn, gqv, qh, rq, 1.0 / HD).astype(BF16))
                    dss.append(dsb)
                    dobs.append(dob)
                grp = slice(h * GRP, (h + 1) * GRP)
                dkn = _tn(jnp.concatenate(dss, axis=0), jnp.concatenate([q[2] for q in qs[grp]], axis=0))
                dv = _tn(jnp.concatenate([p.astype(BF16) for p in prs[grp]], axis=0),
                         jnp.concatenate(dobs, axis=0))
                khat, rk = keys[h][0][band], keys[h][1][band]
                dgk_ref[...] += jnp.sum(dkn * khat, axis=0, keepdims=True)
                acc_ref[band, h * HP:(h + 1) * HP] += _rms_bwd(dkn, gkv, khat, rk, 1.0 / HD)
                acc_ref[band, (NKV + h) * HP:(NKV + h + 1) * HP] += dv
            dpm_ref[b * BLK:(b + 1) * BLK, O_Q:O_K] = jnp.concatenate(dqs, axis=1)
        dkvh_ref[...] = acc_ref[0:BLK, :]
        dkvm_ref[...] = acc_ref[BLK:, :]

    prev8 = lambda col: pl.BlockSpec((8, CC), lambda i: (jnp.maximum(i * r8 - 1, 0), col))
    next8 = lambda col: pl.BlockSpec((8, CC), lambda i: (jnp.minimum((i + 1) * r8, t // 8 - 1), col))
    small = lambda n: pl.BlockSpec((1, n), lambda i: (0, 0))
    return pl.pallas_call(
        body, name="mixer_bwd", grid=(nt,),
        in_specs=[
            pl.BlockSpec((tq, D), lambda i: (i, 0)),
            pl.BlockSpec((8, D), lambda i: (jnp.minimum((i + 1) * r8, t // 8 - 1), 0)),
            pl.BlockSpec((tq, NP), lambda i: (i, 0)),
            prev8(O_CG // CC), prev8(O_HC // CC),
            next8(O_BG // CC), next8(O_CG // CC), next8(O_HC // CC),
            pl.BlockSpec((BLK, kvw), lambda i: (jnp.maximum(i * nb - 1, 0), O_K // kvw)),
            _const_spec((8, CC)), _const_spec((1, HP)), _const_spec((1, HP)),
            pl.BlockSpec(memory_space=pltpu.SMEM),
            _const_spec((1, CC)), _const_spec((1, NQ * HP)), _const_spec((MIXW, D)),
        ],
        out_specs=[
            pl.BlockSpec((tq, NMAIN), lambda i: (i, 0)),
            pl.BlockSpec((tq, kvw), lambda i: (i, 0)),
            pl.BlockSpec((BLK, kvw), lambda i: (i, 0)),
            pl.BlockSpec((8, CC), lambda i: (0, 0)), small(HP), small(HP), small(HP), small(CC), small(NQ * HP),
        ],
        out_shape=[
            jax.ShapeDtypeStruct((t, NMAIN), BF16), jax.ShapeDtypeStruct((t, kvw), F32),
            jax.ShapeDtypeStruct((nt * BLK, kvw), F32),
            jax.ShapeDtypeStruct((8, CC), F32), jax.ShapeDtypeStruct((1, HP), F32), jax.ShapeDtypeStruct((1, HP), F32),
            jax.ShapeDtypeStruct((1, HP), F32), jax.ShapeDtypeStruct((1, CC), F32),
            jax.ShapeDtypeStruct((1, NQ * HP), F32),
        ],
        scratch_shapes=[pltpu.VMEM((tq + BLK, kvw), F32)],
        compiler_params=_cparams(("arbitrary",)),
    )(dxm, dxm, proj, proj, proj, proj, proj, proj, proj, cw, gq, gk, sinks, gco, gao, wo)


def _inproj_bwd(dpm, dkv, wp, x, g1, dxm, tm):
    t = x.shape[0]
    kvw = 2 * NKV * HP

    def body(dp_ref, dk_ref, w_ref, x_ref, g_ref, dxm_ref, dx_ref, dg_ref):
        @pl.when(pl.program_id(0) == 0)
        def _():
            dg_ref[...] = jnp.zeros_like(dg_ref)

        dh = _nt(dp_ref[...], w_ref[:, 0:NMAIN]) + _nt(dk_ref[...], w_ref[:, NMAIN:NP])
        r, xh = _rms_fwd(x_ref[...], 1.0 / D)
        dg_ref[...] += jnp.sum(dh * xh, axis=0, keepdims=True)
        dx_ref[...] = dxm_ref[...] + _rms_bwd(dh, g_ref[...], xh, r, 1.0 / D)

    row = lambda w: pl.BlockSpec((tm, w), lambda i: (i, 0))
    return pl.pallas_call(
        body, name="inproj_bwd", grid=(t // tm,),
        in_specs=[row(NMAIN), row(kvw), _const_spec((D, NP)), row(D), _const_spec((1, D)), row(D)],
        out_specs=[row(D), pl.BlockSpec((1, D), lambda i: (0, 0))],
        out_shape=[jax.ShapeDtypeStruct((t, D), F32), jax.ShapeDtypeStruct((1, D), F32)],
        compiler_params=_cparams(("arbitrary",)),
    )(dpm, dkv, wp, x, g1, dxm)


def _rows_tile(rows):
    for cand in (512, 256, 128, 64, 32, 16, 8):
        if rows % cand == 0:
            return cand
    return rows


def _presum_halves(gs, theirs, core):
    outs = []
    for n, (ga, ta) in enumerate(zip(gs, theirs)):
        _, hr, cols = ta.shape

        def body(c_ref, g_ref, t_ref, o_ref):
            o_ref[...] = (g_ref[...].astype(F32) + t_ref[...].astype(F32)).astype(BF16)

        half = pl.BlockSpec((None, hr, cols), lambda k, c_ref: (k, 0, 0))
        outs.append(pl.pallas_call(
            body, name=f"presum_{n}",
            grid_spec=pltpu.PrefetchScalarGridSpec(
                num_scalar_prefetch=1, grid=(N_CHIPS,),
                in_specs=[pl.BlockSpec((None, hr, cols), lambda k, c_ref: (k, c_ref[0], 0)), half],
                out_specs=half),
            out_shape=jax.ShapeDtypeStruct(ta.shape, BF16), compiler_params=_cparams(("parallel",)),
        )(core, ga, ta))
    return outs


def _sum_chips(cs, name):
    outs = []
    for n, ca in enumerate(cs):
        _, rows, cols = ca.shape
        tr = _rows_tile(rows)

        def body(c_ref, o_ref):
            acc = c_ref[0].astype(F32)
            for j in range(1, N_CHIPS):
                acc = acc + c_ref[j].astype(F32)
            o_ref[...] = acc

        outs.append(pl.pallas_call(
            body, name=f"{name}_{n}", grid=(rows // tr,),
            in_specs=[pl.BlockSpec((N_CHIPS, tr, cols), lambda i: (0, i, 0))],
            out_specs=pl.BlockSpec((tr, cols), lambda i: (i, 0)),
            out_shape=jax.ShapeDtypeStruct((rows, cols), F32), compiler_params=_cparams(("parallel",)),
        )(ca))
    return outs


def _adamw(w, g, m, v, name):
    rows, cols = w.shape
    tr = _rows_tile(rows)
    c1 = 1.0 - ADAM_B1 ** ADAM_STEP
    c2 = 1.0 - ADAM_B2 ** ADAM_STEP

    def body(w_ref, g_ref, m_ref, v_ref, d_ref, mo_ref, vo_ref):
        gv = g_ref[...]
        mn = ADAM_B1 * m_ref[...] + (1.0 - ADAM_B1) * gv
        vn = ADAM_B2 * v_ref[...] + (1.0 - ADAM_B2) * (gv * gv)
        mo_ref[...] = mn
        vo_ref[...] = vn
        d_ref[...] = -ADAM_LR * ((mn / c1) / (jnp.sqrt(vn / c2) + ADAM_EPS) + ADAM_WD * w_ref[...])

    spec = pl.BlockSpec((tr, cols), lambda i: (i, 0))
    sds = jax.ShapeDtypeStruct((rows, cols), F32)
    return pl.pallas_call(
        body, name=name, grid=(rows // tr,), in_specs=[spec] * 4, out_specs=[spec] * 3, out_shape=[sds] * 3,
        compiler_params=_cparams(("parallel",)),
    )(w, g, m, v)


def _place():
    x, y, c = lax.axis_index("x"), lax.axis_index("y"), lax.axis_index("c")
    chips = [(1 - x, y), (x, 1 - y), (1 - x, 1 - y)]
    return x, y, c, chips


ANY = pl.BlockSpec(memory_space=pl.ANY)
DMA_ROWS = 64


def _pieces(shape):
    rows = shape[-2]
    step = DMA_ROWS if rows % DMA_ROWS == 0 else rows
    lead = [()]
    for n in shape[:-2]:
        lead = [i + (k,) for i in lead for k in range(n)]
    return [i + (pl.ds(r0, step),) for i in lead for r0 in range(0, rows, step)]


def _start_pieces(make, src, dst):
    for idx in _pieces(src.shape):
        make(src.at[idx], dst.at[idx]).start()


def _gather_layer(blocks, layer):
    nw = len(blocks)

    def body(*refs):
        _gather_body(refs[:nw], refs[nw:2 * nw], refs[2 * nw:], layer, _start_pieces)

    return pl.pallas_call(
        body, name=f"gather_layer{layer}", in_specs=[ANY] * nw, out_specs=[ANY] * nw,
        out_shape=[jax.ShapeDtypeStruct((N_CHIPS,) + b.shape, b.dtype) for b in blocks],
        scratch_shapes=[pltpu.SemaphoreType.DMA((3, nw))] * 4,
        compiler_params=_cparams(has_side_effects=True),
    )(*blocks)


def _gather_body(srcs, outs, sems, layer, start):
    nw = len(srcs)
    ssem, rsem, fssem, frsem = sems
    x, y, c, chips = _place()
    kme = 2 * x + y

    def plane(j, w, to):
        return lambda s, d: pltpu.make_async_remote_copy(
            src_ref=s, dst_ref=d, send_sem=ssem.at[j, w], recv_sem=rsem.at[j, w], device_id=to,
            device_id_type=MESH)

    def passed(j, w):
        return lambda s, d: pltpu.make_async_remote_copy(
            src_ref=s, dst_ref=d, send_sem=fssem.at[j, w], recv_sem=frsem.at[j, w],
            device_id=(x, y, 1 - c), device_id_type=MESH)

    @pl.when(c == layer)
    def _():
        for j, (px, py) in enumerate(chips):
            for w in range(nw):
                start(plane(j, w, (px, py, c)), srcs[w], outs[w].at[kme])
        for j, (px, py) in enumerate(chips):
            for w in range(nw):
                got = outs[w].at[2 * px + py]
                plane(j, w, (px, py, c))(got, got).wait_recv()
                start(passed(j, w), got, got)
        for j, (px, py) in enumerate(chips):
            for w in range(nw):
                got = outs[w].at[2 * px + py]
                plane(j, w, (px, py, c))(got, got).wait_send()
                passed(j, w)(got, got).wait_send()

    @pl.when(c != layer)
    def _():
        for j, (px, py) in enumerate(chips):
            for w in range(nw):
                got = outs[w].at[2 * px + py]
                passed(j, w)(got, got).wait_recv()


def _handshake_all():
    x, y, c, _ = _place()
    barrier = pltpu.get_barrier_semaphore()
    for r in range(1, 8):
        peer = (x ^ (r >> 2), y ^ ((r >> 1) & 1), c ^ (r & 1))
        pl.semaphore_signal(barrier, inc=1, device_id=peer, device_id_type=MESH)
    pl.semaphore_wait(barrier, 7)


def _gather_layer_async(blocks, layer, collective_id):
    hbm = pltpu.MemorySpace.HBM
    srcs = [jax.new_ref(b, memory_space=hbm) for b in blocks]
    outs = [jax.empty_ref(jax.ShapeDtypeStruct((N_CHIPS,) + b.shape, b.dtype), memory_space=hbm) for b in blocks]

    @pl.kernel(mesh=plsc.ScalarSubcoreMesh(axis_name="seq", num_cores=1), name=f"gather_layer{layer}_seq",
               scratch_types=[pltpu.SemaphoreType.DMA((3, len(blocks)))] * 4,
               compiler_params=pltpu.CompilerParams(collective_id=collective_id))
    def launch(*sems):
        _handshake_all()
        _gather_body(srcs, outs, sems, layer, lambda make, s, d: make(s, d).start())

    launch()
    return [o[...] for o in outs]


def _swap_halves(gs):
    nw = len(gs)

    def body(*refs):
        srcs, theirs = refs[:nw], refs[nw:2 * nw]
        ssem, rsem = refs[2 * nw:]
        x, y, c, _ = _place()

        def give(w):
            return lambda s, d: pltpu.make_async_remote_copy(
                src_ref=s, dst_ref=d, send_sem=ssem.at[w], recv_sem=rsem.at[w], device_id=(x, y, 1 - c),
                device_id_type=MESH)

        for w in range(nw):
            hr = theirs[w].shape[1]
            _start_pieces(give(w), srcs[w].at[:, pl.ds((1 - c) * hr, hr)], theirs[w])
        for w in range(nw):
            give(w)(theirs[w], theirs[w]).wait()

    return pl.pallas_call(
        body, name="swap_halves", in_specs=[ANY] * nw, out_specs=[ANY] * nw,
        out_shape=[jax.ShapeDtypeStruct((g.shape[0], g.shape[1] // 2, g.shape[2]), g.dtype) for g in gs],
        scratch_shapes=[pltpu.SemaphoreType.DMA((nw,))] * 2,
        compiler_params=_cparams(has_side_effects=True),
    )(*gs)


def _scatter_chips(ps):
    nw = len(ps)

    def body(*refs):
        _scatter_body(refs[:nw], refs[nw:2 * nw], refs[2 * nw:], _start_pieces)

    return pl.pallas_call(
        body, name="scatter_chips", in_specs=[ANY] * nw, out_specs=[ANY] * nw,
        out_shape=[jax.ShapeDtypeStruct(p.shape, p.dtype) for p in ps],
        scratch_shapes=[pltpu.SemaphoreType.DMA((3, nw)), pltpu.SemaphoreType.DMA((3, nw))],
        compiler_params=_cparams(has_side_effects=True),
    )(*ps)


def _scatter_body(srcs, outs, sems, start):
    nw = len(srcs)
    ssem, rsem = sems
    x, y, c, chips = _place()
    kme = 2 * x + y

    def give(j, w, to):
        return lambda s, d: pltpu.make_async_remote_copy(
            src_ref=s, dst_ref=d, send_sem=ssem.at[j, w], recv_sem=rsem.at[j, w], device_id=to,
            device_id_type=MESH)

    for j, (px, py) in enumerate(chips):
        for w in range(nw):
            start(give(j, w, (px, py, c)), srcs[w].at[2 * px + py], outs[w].at[kme])
    for j, (px, py) in enumerate(chips):
        for w in range(nw):
            got = outs[w].at[2 * px + py]
            give(j, w, (px, py, c))(got, got).wait_recv()
    for j, (px, py) in enumerate(chips):
        for w in range(nw):
            sent = srcs[w].at[2 * px + py]
            give(j, w, (px, py, c))(sent, sent).wait_send()


def _scatter_chips_async(ps, name, collective_id):
    hbm = pltpu.MemorySpace.HBM
    srcs = [jax.new_ref(p, memory_space=hbm) for p in ps]
    outs = [jax.empty_ref(jax.ShapeDtypeStruct(p.shape, p.dtype), memory_space=hbm) for p in ps]

    @pl.kernel(mesh=plsc.ScalarSubcoreMesh(axis_name="seq", num_cores=1), name=name,
               scratch_types=[pltpu.SemaphoreType.DMA((3, len(ps)))] * 2,
               compiler_params=pltpu.CompilerParams(collective_id=collective_id))
    def launch(*sems):
        _handshake_all()
        _scatter_body(srcs, outs, sems, lambda make, s, d: make(s, d).start())

    launch()
    return [o[...] for o in outs]


def _swap_siblings(rs):
    nw = len(rs)

    def body(*refs):
        srcs, outs = refs[:nw], refs[nw:2 * nw]
        ssem, rsem = refs[2 * nw:]
        x, y, c, _ = _place()

        def give(w):
            return lambda s, d: pltpu.make_async_remote_copy(
                src_ref=s, dst_ref=d, send_sem=ssem.at[w], recv_sem=rsem.at[w], device_id=(x, y, 1 - c),
                device_id_type=MESH)

        for w in range(nw):
            _start_pieces(give(w), srcs[w], outs[w])
        for w in range(nw):
            give(w)(srcs[w], outs[w]).wait()

    return pl.pallas_call(
        body, name="swap_siblings", in_specs=[ANY] * nw, out_specs=[ANY] * nw,
        out_shape=[jax.ShapeDtypeStruct(r.shape, r.dtype) for r in rs],
        scratch_shapes=[pltpu.SemaphoreType.DMA((nw,))] * 2,
        compiler_params=_cparams(has_side_effects=True),
    )(*rs)


def _allreduce_small(v):
    rows = v.shape[0]

    def body(v_ref, o_ref, buf, ssem, rsem):
        x, y, c, _ = _place()
        me = 4 * x + 2 * y + c
        buf[me] = v_ref[...]
        sends = []
        for r in range(1, 8):
            peer = (x ^ (r >> 2), y ^ ((r >> 1) & 1), c ^ (r & 1))
            cp = pltpu.make_async_remote_copy(
                src_ref=v_ref, dst_ref=buf.at[me], send_sem=ssem.at[r - 1], recv_sem=rsem.at[r - 1],
                device_id=peer, device_id_type=MESH)
            cp.start()
            sends.append(cp)
        for r in range(1, 8):
            src = me ^ r
            pltpu.make_async_remote_copy(
                src_ref=v_ref, dst_ref=buf.at[src], send_sem=ssem.at[r - 1], recv_sem=rsem.at[r - 1],
                device_id=(x, y, c), device_id_type=MESH).wait_recv()
        for cp in sends:
            cp.wait_send()
        acc = buf[0]
        for d in range(1, 8):
            acc = acc + buf[d]
        o_ref[...] = acc

    vm = pl.BlockSpec(memory_space=pltpu.VMEM)
    return pl.pallas_call(
        body, name="allreduce_small", in_specs=[vm], out_specs=vm,
        out_shape=jax.ShapeDtypeStruct(v.shape, F32),
        scratch_shapes=[pltpu.VMEM((8, rows, 128), F32), pltpu.SemaphoreType.DMA((7,)),
                        pltpu.SemaphoreType.DMA((7,))],
        compiler_params=_cparams(has_side_effects=True),
    )(v)


def _pad_heads(w, n_heads, axis):
    shp = w.shape
    w = w.reshape(shp[:axis] + (n_heads, HD) + shp[axis + 1:])
    pad = [(0, 0)] * w.ndim
    pad[axis + 1] = (0, HP - HD)
    w = jnp.pad(w, pad)
    return w.reshape(shp[:axis] + (n_heads * HP,) + shp[axis + 1:])


def _strip_heads(w, n_heads, axis):
    shp = w.shape
    w = w.reshape(shp[:axis] + (n_heads, HP) + shp[axis + 1:])
    w = lax.slice_in_dim(w, 0, HD, axis=axis + 1)
    return w.reshape(shp[:axis] + (n_heads * HD,) + shp[axis + 1:])


def _unshard_cols(g4):
    k, r, c = g4.shape
    return jnp.transpose(g4, (1, 0, 2)).reshape(r, k * c)


def _shard_cols(w):
    r, n = w.shape
    return jnp.transpose(w.reshape(r, N_CHIPS, n // N_CHIPS), (1, 0, 2))


def _pad_win(win):
    parts = [win[:, :3 * CC], _pad_heads(win[:, 3 * CC:3 * CC + NQ * HD], NQ, 1),
             _pad_heads(win[:, 3 * CC + NQ * HD:3 * CC + (NQ + NKV) * HD], NKV, 1),
             _pad_heads(win[:, 3 * CC + (NQ + NKV) * HD:], NKV, 1)]
    return jnp.concatenate(parts, axis=1)


def _strip_win(gp):
    parts = [gp[:, :3 * CC], _strip_heads(gp[:, O_Q:O_K], NQ, 1), _strip_heads(gp[:, O_K:O_V], NKV, 1),
             _strip_heads(gp[:, O_V:], NKV, 1)]
    return jnp.concatenate(parts, axis=1)


def _count(shape):
    n = 1
    for s in shape:
        n *= s
    return n


def _pack_rows(arrs):
    flat = [jnp.pad(a.reshape(-1), (0, (-_count(a.shape)) % 128)) for a in arrs]
    v = jnp.concatenate(flat)
    rows = -(-v.shape[0] // (8 * 128)) * 8
    return jnp.pad(v, (0, rows * 128 - v.shape[0])).reshape(rows, 128)


def kernel(x, norm1_g, w_in, conv_w, q_norm_g, k_norm_g, sinks, conv_out_g, attn_out_g, w_o, norm2_g, w_gate, w_up, w_down, loss_target, m_norm1_g, m_w_in, m_conv_w, m_q_norm_g, m_k_norm_g, m_sinks, m_conv_out_g, m_attn_out_g, m_w_o, m_norm2_g, m_w_gate, m_w_up, m_w_down, v_norm1_g, v_w_in, v_conv_w, v_q_norm_g, v_k_norm_g, v_sinks, v_conv_out_g, v_attn_out_g, v_w_o, v_norm2_g, v_w_gate, v_w_up, v_w_down):
    depth = w_in.shape[0]
    t = x.shape[1]
    xs = x.reshape(t, D)
    tgt = loss_target.reshape(t, D)
    xi, yi = lax.axis_index("x"), lax.axis_index("y")
    kme = 2 * xi + yi
    tm = min(512, t)
    tq = min(256, t)
    tf = min(256, t)

    cwp = jnp.pad(conv_w.reshape(depth * 3, CC // N_CHIPS), ((0, 8 - depth * 3), (0, 0)))
    own = [[jnp.concatenate([w_gate[l], w_up[l]], axis=0).astype(BF16),
            jnp.concatenate([w_down[l], w_o[l]], axis=0).astype(BF16), w_in[l].astype(BF16)]
           for l in range(depth)]
    own[0].append(cwp)
    got0 = _gather_layer(own[0], 0)
    own[1] = lax.optimization_barrier((own[1], got0))[0]
    arrived = [got0, _gather_layer_async(own[1], 1, collective_id=1)]

    def layer_params(l, got):
        ga, gb, gc = [lax.dynamic_update_index_in_dim(g, o, kme, 0) for g, o in zip(got[:3], own[l][:3])]
        wo = gb[:, FFB:].reshape(D, D)
        return dict(
            wp=_pad_win(_unshard_cols(gc)), ga=ga, gb=gb,
            wo=jnp.concatenate([wo[:CC], _pad_heads(wo[CC:], NQ, 0)], axis=0),
            cw=jnp.pad(cw_full[l], ((0, 5), (0, 0))),
            g1=norm1_g[l].reshape(1, D), g2=norm2_g[l].reshape(1, D),
            gq=jnp.pad(q_norm_g[l], (0, HP - HD)).reshape(1, HP), gk=jnp.pad(k_norm_g[l], (0, HP - HD)).reshape(1, HP),
            sk=sinks[l].reshape(1, NQ), gco=conv_out_g[l].reshape(1, CC),
            gao=_pad_heads(attn_out_g[l], NQ, 0).reshape(1, NQ * HP))

    cw_full = _unshard_cols(lax.dynamic_update_index_in_dim(got0[3], cwp, kme, 0))[:depth * 3].reshape(depth, 3, CC)

    saved, layers = [], []
    cur = xs
    for l in range(depth):
        p = layer_params(l, arrived[l] if l == 0 else lax.optimization_barrier((arrived[l], cur))[0])
        layers.append(p)
        proj, h = _inproj_fwd(cur, p["g1"], p["wp"], tm)
        xm, mix = _mixer_fwd(proj, cur, p["cw"], p["gq"], p["gk"], p["sk"], p["gco"], p["gao"], p["wo"], tq)
        xo, a, b, h2 = _ffn_fwd(xm, p["g2"], p["ga"], p["gb"], tf)
        saved.append(dict(x=cur, proj=proj, h=h, xm=xm, mix=mix, a=a, b=b, h2=h2))
        cur = xo
    lpart, dy = _loss_and_grad(cur, tgt, tm)
    loss = lax.psum(lpart[0, 0], ("x", "y", "c"))

    nt = t // tq
    ci = lax.axis_index("c")
    core = ci.reshape(1).astype(jnp.int32)
    rbig = [dict() for _ in range(depth)]
    gsmall = [None] * depth

    def reduce_start(gs, name, collective_id):
        ps = _presum_halves(gs, _swap_halves(gs), core)
        got = _scatter_chips(ps) if collective_id is None else _scatter_chips_async(ps, name, collective_id)
        return ps, got

    def reduce_finish(started, after):
        ps, got = started
        if after is not None:
            got = lax.optimization_barrier((got, after))[0]
        cs = [lax.dynamic_update_index_in_dim(g, lax.dynamic_index_in_dim(q, kme, 0, keepdims=False), kme, 0)
              for g, q in zip(got, ps)]
        r_mine = _sum_chips(cs, "chipsum")
        return [jnp.where(ci == 0, jnp.concatenate([a, b], axis=0), jnp.concatenate([b, a], axis=0))
                for a, b in zip(r_mine, _swap_siblings(r_mine))]

    in_flight = None
    for l in reversed(range(depth)):
        p, s = layers[l], saved[l]
        dxm, da, db, hm, dg2 = _ffn_bwd(dy, s["xm"], p["g2"], s["a"], s["b"], p["ga"], p["gb"], tf)
        g_wg = _wgrad_blocks(s["h2"], da, tm, "wgrad_gate")
        g_wu = _wgrad_blocks(s["h2"], db, tm, "wgrad_up")
        g_wd = _wgrad_blocks(hm, dy, tm, "wgrad_down")
        if in_flight is not None:
            rbig[l + 1]["in"], r_ot = reduce_finish(in_flight, g_wd)
            rbig[l + 1]["o"] = r_ot.T
        ffn_flight = reduce_start([g_wg, g_wu, g_wd], f"scatter_ffn{l}_seq", 2 + 2 * l)
        dpm, dkvm, dkvh, dcw, dgq, dgk, dsk, dgco, dgao = _mixer_bwd(
            dxm, s["proj"], p["cw"], p["gq"], p["gk"], p["sk"], p["gco"], p["gao"], p["wo"], tq)
        g_wot = _wgrad(dxm, s["mix"], MIXW, tm, "wgrad_o")
        kvw = dkvm.shape[1]
        halo = jnp.concatenate([dkvh.reshape(nt, BLK, kvw)[1:], jnp.zeros((1, BLK, kvw), F32)], axis=0)
        halo = jnp.pad(halo, ((0, 0), (tq - BLK, 0), (0, 0)))
        dkv = (dkvm.reshape(nt, tq, kvw) + halo).reshape(t, kvw).astype(BF16)
        dx, dg1 = _inproj_bwd(dpm, dkv, p["wp"], s["x"], p["g1"], dxm, tm)
        g_wpm = _wgrad(s["h"], dpm, NMAIN // 2, tm, "wgrad_in_main")
        g_wpk = _wgrad(s["h"], dkv, kvw, tm, "wgrad_in_kv")
        dy = dx
        g_in = _strip_win(jnp.concatenate([g_wpm, g_wpk], axis=1))
        g_ot = jnp.concatenate([g_wot[:, :CC], _strip_heads(g_wot[:, CC:], NQ, 1)], axis=1)
        gsmall[l] = dict(g1=dg1, cw=dcw[:3], gq=dgq[0, :HD], gk=dgk[0, :HD], sk=dsk[0, :NQ], gco=dgco,
                         gao=_strip_heads(dgao.reshape(NQ * HP), NQ, 0), g2=dg2)
        rbig[l]["g"], rbig[l]["u"], rbig[l]["d"] = reduce_finish(ffn_flight, dx)
        in_flight = reduce_start([_shard_cols(g_in.astype(BF16)), _shard_cols(g_ot.astype(BF16))],
                                 f"scatter_in{l}_seq", 3 + 2 * l if l > 0 else None)
    rbig[0]["in"], r_ot = reduce_finish(in_flight, None)
    rbig[0]["o"] = r_ot.T
    grad_x = dy.reshape(x.shape)
    g_big = [jnp.stack([rbig[l][w] for l in range(depth)]) for w in ("in", "o", "g", "u", "d")]

    small_shapes = dict(g1=(D,), cw=(3, CC), gq=(HD,), gk=(HD,), sk=(NQ,), gco=(CC,), gao=(NQ * HD,), g2=(D,))
    red = _allreduce_small(_pack_rows([gsmall[l][n] for l in range(depth) for n in small_shapes])).reshape(-1)
    red_small, offs = {n: [] for n in small_shapes}, 0
    for l in range(depth):
        for n, shp in small_shapes.items():
            cnt = _count(shp)
            red_small[n].append(red[offs:offs + cnt].reshape(shp))
            offs += -(-cnt // 128) * 128
    g_small = {n: jnp.stack(v) for n, v in red_small.items()}
    g_cw = lax.dynamic_slice_in_dim(g_small["cw"], kme * (CC // N_CHIPS), CC // N_CHIPS, axis=2)

    weights = [norm1_g, w_in, conv_w, q_norm_g, k_norm_g, sinks, conv_out_g, attn_out_g, w_o, norm2_g, w_gate,
               w_up, w_down]
    moms = [m_norm1_g, m_w_in, m_conv_w, m_q_norm_g, m_k_norm_g, m_sinks, m_conv_out_g, m_attn_out_g, m_w_o,
            m_norm2_g, m_w_gate, m_w_up, m_w_down]
    vars_ = [v_norm1_g, v_w_in, v_conv_w, v_q_norm_g, v_k_norm_g, v_sinks, v_conv_out_g, v_attn_out_g, v_w_o,
             v_norm2_g, v_w_gate, v_w_up, v_w_down]
    grads = [g_small["g1"], g_big[0], g_cw, g_small["gq"], g_small["gk"], g_small["sk"], g_small["gco"],
             g_small["gao"], g_big[1], g_small["g2"], g_big[2], g_big[3], g_big[4]]
    n_w = len(weights)
    big_idx = [1, 8, 10, 11, 12]
    small_idx = [n for n in range(n_w) if n not in big_idx]
    deltas, new_m, new_v = [None] * n_w, [None] * n_w, [None] * n_w
    for n in big_idx:
        shp = weights[n].shape
        two = [a3.reshape(shp[0] * shp[1], shp[2]) for a3 in (weights[n], grads[n], moms[n], vars_[n])]
        res = _adamw(*two, f"adamw_{n}")
        deltas[n], new_m[n], new_v[n] = [r.reshape(shp) for r in res]
    res = _adamw(*[_pack_rows([arrs[n] for n in small_idx]) for arrs in (weights, grads, moms, vars_)],
                 "adamw_small")
    offs = 0
    for n in small_idx:
        shp = weights[n].shape
        cnt = _count(shp)
        deltas[n], new_m[n], new_v[n] = [r.reshape(-1)[offs:offs + cnt].reshape(shp) for r in res]
        offs += -(-cnt // 128) * 128
    return (loss, grad_x, *grads, *deltas, *new_m, *new_v)
```

```python
import functools

import jax
import jax.numpy as jnp
from jax import lax
from jax.experimental import pallas as pl
from jax.experimental.pallas import tpu as pltpu
from jax.experimental.pallas import tpu_sc as plsc

F32 = jnp.float32
BF16 = jnp.bfloat16

D = 1024
CC = 512
NQ = 8
NKV = 2
HD = 64
HP = 128
GRP = NQ // NKV
FF = 2816
FFB = FF // 4
BLK = 128
EPS = 1e-6
NEG = -1e30
SCALE = HD ** -0.5
O_BG, O_CG, O_HC, O_Q = 0, CC, 2 * CC, 3 * CC
O_K = O_Q + NQ * HP
O_V = O_K + NKV * HP
NP = O_V + NKV * HP
NMAIN = O_K
MIXW = CC + NQ * HP
N_CHIPS = 4
VMEM_LIMIT = 56 * 1024 * 1024
MESH = pl.DeviceIdType.MESH

ADAM_LR, ADAM_B1, ADAM_B2, ADAM_EPS, ADAM_WD, ADAM_STEP = 0.001, 0.9, 0.999, 1e-08, 0.01, 10


def _cparams(sem=None, **kw):
    if sem is not None:
        kw["dimension_semantics"] = sem
    return pltpu.CompilerParams(vmem_limit_bytes=VMEM_LIMIT, **kw)


def _const_spec(shape):
    nd = len(shape)
    return pl.BlockSpec(shape, lambda *_: (0,) * nd, pipeline_mode=pl.Buffered(1))


def _nt(a, b):
    return lax.dot_general(a, b, (((1,), (1,)), ((), ())), preferred_element_type=F32)


def _tn(a, b):
    return lax.dot_general(a, b, (((0,), (0,)), ((), ())), preferred_element_type=F32)


def _rms_fwd(x, inv_n):
    r = lax.rsqrt(jnp.sum(x * x, axis=-1, keepdims=True) * inv_n + EPS)
    return r, x * r


def _rms_bwd(dy, g, xh, r, inv_n):
    dxh = dy * g
    return r * (dxh - xh * (jnp.sum(dxh * xh, axis=-1, keepdims=True) * inv_n))


def _inproj_fwd(x, g1, wpt, tm):
    t = x.shape[0]

    def body(x_ref, g_ref, w_ref, p_ref, h_ref):
        _, xh = _rms_fwd(x_ref[...], 1.0 / D)
        h = (xh * g_ref[...]).astype(BF16)
        h_ref[...] = h
        p_ref[...] = _nt(h, w_ref[...])

    return pl.pallas_call(
        body, name="inproj_fwd", grid=(t // tm,),
        in_specs=[pl.BlockSpec((tm, D), lambda i: (i, 0)), _const_spec((1, D)), _const_spec((NP, D))],
        out_specs=[pl.BlockSpec((tm, NP), lambda i: (i, 0)), pl.BlockSpec((tm, D), lambda i: (i, 0))],
        out_shape=[jax.ShapeDtypeStruct((t, NP), F32), jax.ShapeDtypeStruct((t, D), BF16)],
        compiler_params=_cparams(("parallel",)),
    )(x, g1, wpt)


def _band_mask():
    r_io = lax.broadcasted_iota(jnp.int32, (BLK, 2 * BLK), 0)
    c_io = lax.broadcasted_iota(jnp.int32, (BLK, 2 * BLK), 1)
    return (c_io > r_io) & (c_io <= r_io + BLK), c_io


def _conv_taps(uf, n):
    u1 = pltpu.roll(uf, 1, 0)[8:8 + n]
    u2 = pltpu.roll(uf, 2, 0)[8:8 + n]
    return u1, u2


def _attn_probs(qn, kband, sink, valid):
    s = _nt(qn, kband) * SCALE
    s = jnp.where(valid, s, NEG)
    m = jnp.maximum(jnp.max(s, axis=-1, keepdims=True), sink)
    p = jnp.exp(s - m)
    es = jnp.exp(sink - m)
    inv = 1.0 / (jnp.sum(p, axis=-1, keepdims=True) + es)
    return p * inv, es * inv


def _norm_keys(kraw, gk):
    out = []
    for h in range(NKV):
        kh = kraw[:, h * HP:(h + 1) * HP]
        rk, khat = _rms_fwd(kh, 1.0 / HD)
        out.append((khat, rk, (khat * gk).astype(BF16)))
    return out


def _mixer_fwd(proj, x, cw, gq, gk, sinks, gco, gao, wo, tq):
    t = proj.shape[0]
    nb = tq // BLK
    r8 = tq // 8

    def body(p_ref, cgp_ref, hcp_ref, kvp_ref, x_ref, cw_ref, gq_ref, gk_ref, sk_ref, gco_ref, gao_ref,
             wo_ref, xm_ref, mix_ref):
        i = pl.program_id(0)
        cg = p_ref[:, O_CG:O_CG + CC]
        hc = p_ref[:, O_HC:O_HC + CC]
        u = cg * hc
        up = jnp.where(i > 0, cgp_ref[...] * hcp_ref[...], 0.0)
        u1, u2 = _conv_taps(jnp.concatenate([up, u], axis=0), tq)
        y = cw_ref[0:1, :] * u2 + cw_ref[1:2, :] * u1 + cw_ref[2:3, :] * u
        co = p_ref[:, O_BG:O_BG + CC] * y
        _, coh = _rms_fwd(co, 1.0 / CC)
        cn = coh * gco_ref[...]
        kraw = jnp.concatenate([kvp_ref[:, 0:NKV * HP], p_ref[:, O_K:O_K + NKV * HP]], axis=0)
        vraw = jnp.concatenate([kvp_ref[:, NKV * HP:], p_ref[:, O_V:O_V + NKV * HP]], axis=0)
        keys = _norm_keys(kraw, gk_ref[...])
        vb = [vraw[:, h * HP:(h + 1) * HP].astype(BF16) for h in range(NKV)]
        base_valid, c_io = _band_mask()
        rows = []
        for b in range(nb):
            lo = jnp.where(i * nb + b == 0, BLK, 0)
            valid = base_valid & (c_io >= lo)
            outs = []
            for g in range(NQ):
                h = g // GRP
                qg = p_ref[b * BLK:(b + 1) * BLK, O_Q + g * HP:O_Q + (g + 1) * HP]
                _, qh = _rms_fwd(qg, 1.0 / HD)
                qn = (qh * gq_ref[...]).astype(BF16)
                pr, _ = _attn_probs(qn, keys[h][2][b * BLK:b * BLK + 2 * BLK], sk_ref[0, g], valid)
                outs.append(jnp.dot(pr.astype(BF16), vb[h][b * BLK:b * BLK + 2 * BLK],
                                    preferred_element_type=F32))
            rows.append(jnp.concatenate(outs, axis=1))
        ao = jnp.concatenate(rows, axis=0)
        _, aoh = _rms_fwd(ao, 1.0 / (NQ * HD))
        an = aoh * gao_ref[...]
        mix = jnp.concatenate([cn, an], axis=1).astype(BF16)
        mix_ref[...] = mix
        xm_ref[...] = x_ref[...] + jnp.dot(mix, wo_ref[...], preferred_element_type=F32)

    prev8 = lambda col: pl.BlockSpec((8, CC), lambda i: (jnp.maximum(i * r8 - 1, 0), col))
    return pl.pallas_call(
        body, name="mixer_fwd", grid=(t // tq,),
        in_specs=[
            pl.BlockSpec((tq, NP), lambda i: (i, 0)),
            prev8(O_CG // CC), prev8(O_HC // CC),
            pl.BlockSpec((BLK, 2 * NKV * HP), lambda i: (jnp.maximum(i * nb - 1, 0), O_K // (2 * NKV * HP))),
            pl.BlockSpec((tq, D), lambda i: (i, 0)),
            _const_spec((8, CC)), _const_spec((1, HP)), _const_spec((1, HP)),
            pl.BlockSpec(memory_space=pltpu.SMEM),
            _const_spec((1, CC)), _const_spec((1, NQ * HP)), _const_spec((MIXW, D)),
        ],
        out_specs=[pl.BlockSpec((tq, D), lambda i: (i, 0)), pl.BlockSpec((tq, MIXW), lambda i: (i, 0))],
        out_shape=[jax.ShapeDtypeStruct((t, D), F32), jax.ShapeDtypeStruct((t, MIXW), BF16)],
        compiler_params=_cparams(("parallel",)),
    )(proj, proj, proj, proj, x, cw, gq, gk, sinks, gco, gao, wo)


def _ffn_weight_specs():
    return [pl.BlockSpec((N_CHIPS, FFB, D), lambda i, j=j: (0, j, 0), pipeline_mode=pl.Buffered(1))
            for j in range(3)]


def _ffn_fwd(xm, g2, gf, tm):
    t = xm.shape[0]

    def body(x_ref, g_ref, wg_ref, wu_ref, wd_ref, xo_ref, a_ref, b_ref, h2_ref):
        xv = x_ref[...]
        _, xh = _rms_fwd(xv, 1.0 / D)
        h2 = (xh * g_ref[...]).astype(BF16)
        h2_ref[...] = h2
        acc = xv
        for k in range(N_CHIPS):
            a = _nt(h2, wg_ref[k])
            b = _nt(h2, wu_ref[k])
            a_ref[k] = a.astype(BF16)
            b_ref[k] = b.astype(BF16)
            hm = (a * jax.nn.sigmoid(a) * b).astype(BF16)
            acc = acc + jnp.dot(hm, wd_ref[k], preferred_element_type=F32)
        xo_ref[...] = acc

    row = lambda w: pl.BlockSpec((tm, w), lambda i: (i, 0))
    blk = pl.BlockSpec((N_CHIPS, tm, FFB), lambda i: (0, i, 0))
    return pl.pallas_call(
        body, name="ffn_fwd", grid=(t // tm,),
        in_specs=[row(D), _const_spec((1, D))] + _ffn_weight_specs(),
        out_specs=[row(D), blk, blk, row(D)],
        out_shape=[jax.ShapeDtypeStruct((t, D), F32), jax.ShapeDtypeStruct((N_CHIPS, t, FFB), BF16),
                   jax.ShapeDtypeStruct((N_CHIPS, t, FFB), BF16), jax.ShapeDtypeStruct((t, D), BF16)],
        compiler_params=_cparams(("parallel",)),
    )(xm, g2, gf, gf, gf)


def _loss_and_grad(y, tgt, tm):
    t = y.shape[0]

    def body(y_ref, t_ref, l_ref, dy_ref):
        @pl.when(pl.program_id(0) == 0)
        def _():
            l_ref[...] = jnp.zeros_like(l_ref)

        e = y_ref[...] - t_ref[...]
        dy_ref[...] = e * (1.0 / D)
        s = jnp.sum(jnp.sum(e * e, axis=-1, keepdims=True), axis=0, keepdims=True)
        l_ref[...] += s * (0.5 / D)

    row = pl.BlockSpec((tm, D), lambda i: (i, 0))
    return pl.pallas_call(
        body, name="loss", grid=(t // tm,), in_specs=[row, row],
        out_specs=[pl.BlockSpec((8, 128), lambda i: (0, 0)), row],
        out_shape=[jax.ShapeDtypeStruct((8, 128), F32), jax.ShapeDtypeStruct((t, D), F32)],
        compiler_params=_cparams(("arbitrary",)),
    )(y, tgt)


def _ffn_bwd(dy, xm, g2, a, b, gf, tm):
    t = dy.shape[0]

    def body(dy_ref, x_ref, g_ref, a_ref, b_ref, wg_ref, wu_ref, wd_ref, dx_ref, da_ref, db_ref, hm_ref, dg_ref):
        @pl.when(pl.program_id(0) == 0)
        def _():
            dg_ref[...] = jnp.zeros_like(dg_ref)

        dyv = dy_ref[...]
        dyb = dyv.astype(BF16)
        dh2 = jnp.zeros_like(dyv)
        for k in range(N_CHIPS):
            dhm = _nt(dyb, wd_ref[k])
            av = a_ref[k].astype(F32)
            bv = b_ref[k].astype(F32)
            sig = jax.nn.sigmoid(av)
            sil = av * sig
            hm_ref[k] = (sil * bv).astype(BF16)
            da = (dhm * bv * (sig * (1.0 + av * (1.0 - sig)))).astype(BF16)
            db = (dhm * sil).astype(BF16)
            da_ref[k] = da
            db_ref[k] = db
            dh2 = (dh2 + jnp.dot(da, wg_ref[k], preferred_element_type=F32)
                   + jnp.dot(db, wu_ref[k], preferred_element_type=F32))
        r, xh = _rms_fwd(x_ref[...], 1.0 / D)
        dg_ref[...] += jnp.sum(dh2 * xh, axis=0, keepdims=True)
        dx_ref[...] = dyv + _rms_bwd(dh2, g_ref[...], xh, r, 1.0 / D)

    row = lambda w: pl.BlockSpec((tm, w), lambda i: (i, 0))
    blk = pl.BlockSpec((N_CHIPS, tm, FFB), lambda i: (0, i, 0))
    bsd = jax.ShapeDtypeStruct((N_CHIPS, t, FFB), BF16)
    return pl.pallas_call(
        body, name="ffn_bwd", grid=(t // tm,),
        in_specs=[row(D), row(D), _const_spec((1, D)), blk, blk] + _ffn_weight_specs(),
        out_specs=[row(D), blk, blk, blk, pl.BlockSpec((1, D), lambda i: (0, 0))],
        out_shape=[jax.ShapeDtypeStruct((t, D), F32), bsd, bsd, bsd, jax.ShapeDtypeStruct((1, D), F32)],
        compiler_params=_cparams(("arbitrary",)),
    )(dy, xm, g2, a, b, gf, gf, gf)


def _wgrad_blocks(a, b, tt, name):
    _, t, rows = a.shape
    cols = b.shape[1]
    nsteps = t // tt

    def body(a_ref, b_ref, o_ref, acc_ref):
        s = pl.program_id(0)

        @pl.when(s == 0)
        def _():
            acc_ref[...] = jnp.zeros_like(acc_ref)

        bv = b_ref[...].astype(BF16)
        for k in range(N_CHIPS):
            acc_ref[k] += _tn(a_ref[k], bv)

        @pl.when(s == nsteps - 1)
        def _():
            o_ref[...] = acc_ref[...].astype(BF16)

    return pl.pallas_call(
        body, name=name, grid=(nsteps,),
        in_specs=[pl.BlockSpec((N_CHIPS, tt, rows), lambda s: (0, s, 0)), pl.BlockSpec((tt, cols), lambda s: (s, 0))],
        out_specs=pl.BlockSpec((N_CHIPS, rows, cols), lambda s: (0, 0, 0)),
        out_shape=jax.ShapeDtypeStruct((N_CHIPS, rows, cols), BF16),
        scratch_shapes=[pltpu.VMEM((N_CHIPS, rows, cols), F32)],
        compiler_params=_cparams(("arbitrary",)),
    )(a, b)


def _wgrad(a, b, tn, tt, name):
    t, k = a.shape
    n = b.shape[1]
    nsteps = t // tt

    def body(a_ref, b_ref, o_ref):
        @pl.when(pl.program_id(1) == 0)
        def _():
            o_ref[...] = jnp.zeros_like(o_ref)

        o_ref[...] += _tn(a_ref[...].astype(BF16), b_ref[...].astype(BF16))

    return pl.pallas_call(
        body, name=name, grid=(n // tn, nsteps),
        in_specs=[pl.BlockSpec((tt, k), lambda j, s: (s, 0)), pl.BlockSpec((tt, tn), lambda j, s: (s, j))],
        out_specs=pl.BlockSpec((k, tn), lambda j, s: (0, j)),
        out_shape=jax.ShapeDtypeStruct((k, n), F32),
        compiler_params=_cparams(("parallel", "arbitrary")),
    )(a, b)


def _mixer_bwd(dxm, proj, cw, gq, gk, sinks, gco, gao, wo, tq):
    t = proj.shape[0]
    nb = tq // BLK
    r8 = tq // 8
    nt = t // tq
    te = tq + 8
    kvw = 2 * NKV * HP

    def body(dx_ref, dxn_ref, p_ref, cgp_ref, hcp_ref, bgn_ref, cgn_ref, hcn_ref, kvp_ref, cw_ref, gq_ref,
             gk_ref, sk_ref, gco_ref, gao_ref, wo_ref,
             dpm_ref, dkvm_ref, dkvh_ref, dcw_ref, dgq_ref, dgk_ref, dsk_ref, dgco_ref, dgao_ref, acc_ref):
        i = pl.program_id(0)

        @pl.when(i == 0)
        def _():
            for r in (dcw_ref, dgq_ref, dgk_ref, dsk_ref, dgco_ref, dgao_ref):
                r[...] = jnp.zeros_like(r)

        acc_ref[...] = jnp.zeros_like(acc_ref)
        live_rows = jnp.where(i < nt - 1, te, tq)
        dxb = dx_ref[...].astype(BF16)
        dxe = jnp.concatenate([dxb, dxn_ref[...].astype(BF16)], axis=0)
        dcn = _nt(dxe, wo_ref[0:CC, :])
        bg = jnp.concatenate([p_ref[:, O_BG:O_BG + CC], bgn_ref[...]], axis=0)
        cg = jnp.concatenate([p_ref[:, O_CG:O_CG + CC], cgn_ref[...]], axis=0)
        hc = jnp.concatenate([p_ref[:, O_HC:O_HC + CC], hcn_ref[...]], axis=0)
        u = cg * hc
        up = jnp.where(i > 0, cgp_ref[...] * hcp_ref[...], 0.0)
        u1, u2 = _conv_taps(jnp.concatenate([up, u], axis=0), te)
        w0, w1, w2 = cw_ref[0:1, :], cw_ref[1:2, :], cw_ref[2:3, :]
        y = w0 * u2 + w1 * u1 + w2 * u
        co = bg * y
        rc, coh = _rms_fwd(co, 1.0 / CC)
        dco = _rms_bwd(dcn, gco_ref[...], coh, rc, 1.0 / CC)
        row_io = lax.broadcasted_iota(jnp.int32, (te, 1), 0)
        own = row_io < tq
        dgco_ref[...] += jnp.sum(jnp.where(own, dcn * coh, 0.0), axis=0, keepdims=True)
        dyc = jnp.where(row_io < live_rows, dco * bg, 0.0)
        dyo = jnp.where(own, dyc, 0.0)
        dcw_ref[0:1, :] += jnp.sum(dyo * u2, axis=0, keepdims=True)
        dcw_ref[1:2, :] += jnp.sum(dyo * u1, axis=0, keepdims=True)
        dcw_ref[2:3, :] += jnp.sum(dyo * u, axis=0, keepdims=True)
        dy1 = pltpu.roll(dyc, te - 1, 0)[0:tq]
        dy2 = pltpu.roll(dyc, te - 2, 0)[0:tq]
        du = w2 * dyc[0:tq] + w1 * dy1 + w0 * dy2
        dpm_ref[:, O_BG:O_BG + CC] = (dco[0:tq] * y[0:tq]).astype(BF16)
        dpm_ref[:, O_CG:O_CG + CC] = (du * hc[0:tq]).astype(BF16)
        dpm_ref[:, O_HC:O_HC + CC] = (du * cg[0:tq]).astype(BF16)
        dan = _nt(dxb, wo_ref[CC:MIXW, :])
        kraw = jnp.concatenate([kvp_ref[:, 0:NKV * HP], p_ref[:, O_K:O_K + NKV * HP]], axis=0)
        vraw = jnp.concatenate([kvp_ref[:, NKV * HP:], p_ref[:, O_V:O_V + NKV * HP]], axis=0)
        gqv, gkv = gq_ref[...], gk_ref[...]
        keys = _norm_keys(kraw, gkv)
        vb = [vraw[:, h * HP:(h + 1) * HP].astype(BF16) for h in range(NKV)]
        base_valid, c_io = _band_mask()
        lane = lax.broadcasted_iota(jnp.int32, (1, HP), 1)
        for b in range(nb):
            lo = jnp.where(i * nb + b == 0, BLK, 0)
            valid = base_valid & (c_io >= lo)
            band = slice(b * BLK, b * BLK + 2 * BLK)
            qs, prs, pss, outs = [], [], [], []
            for g in range(NQ):
                h = g // GRP
                qg = p_ref[b * BLK:(b + 1) * BLK, O_Q + g * HP:O_Q + (g + 1) * HP]
                rq, qh = _rms_fwd(qg, 1.0 / HD)
                qn = (qh * gqv).astype(BF16)
                pr, ps = _attn_probs(qn, keys[h][2][band], sk_ref[0, g], valid)
                qs.append((rq, qh, qn))
                prs.append(pr)
                pss.append(ps)
                outs.append(jnp.dot(pr.astype(BF16), vb[h][band], preferred_element_type=F32))
            ao = jnp.concatenate(outs, axis=1)
            ra, aoh = _rms_fwd(ao, 1.0 / (NQ * HD))
            danb = dan[b * BLK:(b + 1) * BLK]
            dgao_ref[...] += jnp.sum(danb * aoh, axis=0, keepdims=True)
            dao = _rms_bwd(danb, gao_ref[...], aoh, ra, 1.0 / (NQ * HD))
            dqs = []
            for h in range(NKV):
                dss, dobs = [], []
                for g in range(h * GRP, (h + 1) * GRP):
                    rq, qh, qn = qs[g]
                    dob = dao[:, g * HP:(g + 1) * HP].astype(BF16)
                    dp = _nt(dob, vb[h][band])
                    delta = jnp.sum(prs[g] * dp, axis=-1, keepdims=True)
                    dsb = (prs[g] * (dp - delta) * SCALE).astype(BF16)
                    dsk = -jnp.sum(pss[g] * delta, axis=0, keepdims=True)
                    dsk_ref[...] += jnp.where(lane == g, dsk, 0.0)
                    dqn = jnp.dot(dsb, keys[h][2][band], preferred_element_type=F32)
                    dgq_ref[...] += jnp.sum(dqn * qh, axis=0, keepdims=True)
                    dqs.append(_rms_bwd(dqn, gqv, qh, rq, 1.0 / HD).astype(BF16))
                    dss.append(dsb)
                    dobs.append(dob)
                grp = slice(h * GRP, (h + 1) * GRP)
                dkn = _tn(jnp.concatenate(dss, axis=0), jnp.concatenate([q[2] for q in qs[grp]], axis=0))
                dv = _tn(jnp.concatenate([p.astype(BF16) for p in prs[grp]], axis=0),
                         jnp.concatenate(dobs, axis=0))
                khat, rk = keys[h][0][band], keys[h][1][band]
                dgk_ref[...] += jnp.sum(dkn * khat, axis=0, keepdims=True)
                acc_ref[band, h * HP:(h + 1) * HP] += _rms_bwd(dkn, gkv, khat, rk, 1.0 / HD)
                acc_ref[band, (NKV + h) * HP:(NKV + h + 1) * HP] += dv
            dpm_ref[b * BLK:(b + 1) * BLK, O_Q:O_K] = jnp.concatenate(dqs, axis=1)
        dkvh_ref[...] = acc_ref[0:BLK, :]
        dkvm_ref[...] = acc_ref[BLK:, :]

    prev8 = lambda col: pl.BlockSpec((8, CC), lambda i: (jnp.maximum(i * r8 - 1, 0), col))
    next8 = lambda col: pl.BlockSpec((8, CC), lambda i: (jnp.minimum((i + 1) * r8, t // 8 - 1), col))
    small = lambda n: pl.BlockSpec((1, n), lambda i: (0, 0))
    return pl.pallas_call(
        body, name="mixer_bwd", grid=(nt,),
        in_specs=[
            pl.BlockSpec((tq, D), lambda i: (i, 0)),
            pl.BlockSpec((8, D), lambda i: (jnp.minimum((i + 1) * r8, t // 8 - 1), 0)),
            pl.BlockSpec((tq, NP), lambda i: (i, 0)),
            prev8(O_CG // CC), prev8(O_HC // CC),
            next8(O_BG // CC), next8(O_CG // CC), next8(O_HC // CC),
            pl.BlockSpec((BLK, kvw), lambda i: (jnp.maximum(i * nb - 1, 0), O_K // kvw)),
            _const_spec((8, CC)), _const_spec((1, HP)), _const_spec((1, HP)),
            pl.BlockSpec(memory_space=pltpu.SMEM),
            _const_spec((1, CC)), _const_spec((1, NQ * HP)), _const_spec((MIXW, D)),
        ],
        out_specs=[
            pl.BlockSpec((tq, NMAIN), lambda i: (i, 0)),
            pl.BlockSpec((tq, kvw), lambda i: (i, 0)),
            pl.BlockSpec((BLK, kvw), lambda i: (i, 0)),
            pl.BlockSpec((8, CC), lambda i: (0, 0)), small(HP), small(HP), small(HP), small(CC), small(NQ * HP),
        ],
        out_shape=[
            jax.ShapeDtypeStruct((t, NMAIN), BF16), jax.ShapeDtypeStruct((t, kvw), F32),
            jax.ShapeDtypeStruct((nt * BLK, kvw), F32),
            jax.ShapeDtypeStruct((8, CC), F32), jax.ShapeDtypeStruct((1, HP), F32), jax.ShapeDtypeStruct((1, HP), F32),
            jax.ShapeDtypeStruct((1, HP), F32), jax.ShapeDtypeStruct((1, CC), F32),
            jax.ShapeDtypeStruct((1, NQ * HP), F32),
        ],
        scratch_shapes=[pltpu.VMEM((tq + BLK, kvw), F32)],
        compiler_params=_cparams(("arbitrary",)),
    )(dxm, dxm, proj, proj, proj, proj, proj, proj, proj, cw, gq, gk, sinks, gco, gao, wo)


def _inproj_bwd(dpm, dkv, wpt, x, g1, dxm, tm):
    t = x.shape[0]
    kvw = 2 * NKV * HP

    def body(dp_ref, dk_ref, w_ref, x_ref, g_ref, dxm_ref, dx_ref, dg_ref):
        @pl.when(pl.program_id(0) == 0)
        def _():
            dg_ref[...] = jnp.zeros_like(dg_ref)

        dh = (jnp.dot(dp_ref[...], w_ref[0:NMAIN, :], preferred_element_type=F32)
              + jnp.dot(dk_ref[...], w_ref[NMAIN:NP, :], preferred_element_type=F32))
        r, xh = _rms_fwd(x_ref[...], 1.0 / D)
        dg_ref[...] += jnp.sum(dh * xh, axis=0, keepdims=True)
        dx_ref[...] = dxm_ref[...] + _rms_bwd(dh, g_ref[...], xh, r, 1.0 / D)

    row = lambda w: pl.BlockSpec((tm, w), lambda i: (i, 0))
    return pl.pallas_call(
        body, name="inproj_bwd", grid=(t // tm,),
        in_specs=[row(NMAIN), row(kvw), _const_spec((NP, D)), row(D), _const_spec((1, D)), row(D)],
        out_specs=[row(D), pl.BlockSpec((1, D), lambda i: (0, 0))],
        out_shape=[jax.ShapeDtypeStruct((t, D), F32), jax.ShapeDtypeStruct((1, D), F32)],
        compiler_params=_cparams(("arbitrary",)),
    )(dpm, dkv, wpt, x, g1, dxm)


def _rows_tile(rows, cap=512):
    for cand in range(min(rows, cap) // 16 * 16, 0, -16):
        if rows % cand == 0:
            return cand
    return rows


def _presum_halves(gs, theirs, core):
    outs = []
    for n, (ga, ta) in enumerate(zip(gs, theirs)):
        _, hr, cols = ta.shape

        def body(c_ref, g_ref, t_ref, o_ref):
            o_ref[...] = (g_ref[...].astype(F32) + t_ref[...].astype(F32)).astype(BF16)

        half = pl.BlockSpec((None, hr, cols), lambda k, c_ref: (k, 0, 0))
        outs.append(pl.pallas_call(
            body, name=f"presum_{n}",
            grid_spec=pltpu.PrefetchScalarGridSpec(
                num_scalar_prefetch=1, grid=(N_CHIPS,),
                in_specs=[pl.BlockSpec((None, hr, cols), lambda k, c_ref: (k, c_ref[0], 0)), half],
                out_specs=half),
            out_shape=jax.ShapeDtypeStruct(ta.shape, BF16), compiler_params=_cparams(("parallel",)),
        )(core, ga, ta))
    return outs


def _sum_chips(cs, name):
    outs = []
    for n, ca in enumerate(cs):
        _, rows, cols = ca.shape
        tr = _rows_tile(rows)

        def body(c_ref, o_ref):
            acc = c_ref[0].astype(F32)
            for j in range(1, N_CHIPS):
                acc = acc + c_ref[j].astype(F32)
            o_ref[...] = acc

        outs.append(pl.pallas_call(
            body, name=f"{name}_{n}", grid=(rows // tr,),
            in_specs=[pl.BlockSpec((N_CHIPS, tr, cols), lambda i: (0, i, 0))],
            out_specs=pl.BlockSpec((tr, cols), lambda i: (i, 0)),
            out_shape=jax.ShapeDtypeStruct((rows, cols), F32), compiler_params=_cparams(("parallel",)),
        )(ca))
    return outs


def _adamw(w, g, m, v, name):
    rows, cols = w.shape
    tr = _rows_tile(rows, 256)
    c1 = 1.0 - ADAM_B1 ** ADAM_STEP
    c2 = 1.0 - ADAM_B2 ** ADAM_STEP

    def body(w_ref, g_ref, m_ref, v_ref, d_ref, mo_ref, vo_ref):
        gv = g_ref[...]
        mn = ADAM_B1 * m_ref[...] + (1.0 - ADAM_B1) * gv
        vn = ADAM_B2 * v_ref[...] + (1.0 - ADAM_B2) * (gv * gv)
        mo_ref[...] = mn
        vo_ref[...] = vn
        d_ref[...] = -ADAM_LR * ((mn / c1) / (jnp.sqrt(vn / c2) + ADAM_EPS) + ADAM_WD * w_ref[...])

    spec = pl.BlockSpec((tr, cols), lambda i: (i, 0))
    sds = jax.ShapeDtypeStruct((rows, cols), F32)
    return pl.pallas_call(
        body, name=name, grid=(rows // tr,), in_specs=[spec] * 4, out_specs=[spec] * 3, out_shape=[sds] * 3,
        compiler_params=_cparams(("parallel",)),
    )(w, g, m, v)


def _place():
    x, y, c = lax.axis_index("x"), lax.axis_index("y"), lax.axis_index("c")
    chips = [(1 - x, y), (x, 1 - y), (1 - x, 1 - y)]
    return x, y, c, chips


ANY = pl.BlockSpec(memory_space=pl.ANY)
DMA_ROWS = 64


def _pieces(shape):
    rows = shape[-2]
    step = DMA_ROWS if rows % DMA_ROWS == 0 else rows
    lead = [()]
    for n in shape[:-2]:
        lead = [i + (k,) for i in lead for k in range(n)]
    return [i + (pl.ds(r0, step),) for i in lead for r0 in range(0, rows, step)]


def _start_pieces(make, src, dst):
    for idx in _pieces(src.shape):
        make(src.at[idx], dst.at[idx]).start()


def _gather_layer(blocks, layer):
    nw = len(blocks)

    def body(*refs):
        _gather_body(refs[:nw], refs[nw:2 * nw], refs[2 * nw:], layer, _start_pieces)

    return pl.pallas_call(
        body, name=f"gather_layer{layer}", in_specs=[ANY] * nw, out_specs=[ANY] * nw,
        out_shape=[jax.ShapeDtypeStruct((N_CHIPS,) + b.shape, b.dtype) for b in blocks],
        scratch_shapes=[pltpu.SemaphoreType.DMA((3, nw))] * 4,
        compiler_params=_cparams(has_side_effects=True),
    )(*blocks)


def _gather_body(srcs, outs, sems, layer, start):
    nw = len(srcs)
    ssem, rsem, fssem, frsem = sems
    x, y, c, chips = _place()
    kme = 2 * x + y

    def plane(j, w, to):
        return lambda s, d: pltpu.make_async_remote_copy(
            src_ref=s, dst_ref=d, send_sem=ssem.at[j, w], recv_sem=rsem.at[j, w], device_id=to,
            device_id_type=MESH)

    def passed(j, w):
        return lambda s, d: pltpu.make_async_remote_copy(
            src_ref=s, dst_ref=d, send_sem=fssem.at[j, w], recv_sem=frsem.at[j, w],
            device_id=(x, y, 1 - c), device_id_type=MESH)

    @pl.when(c == layer)
    def _():
        for j, (px, py) in enumerate(chips):
            for w in range(nw):
                start(plane(j, w, (px, py, c)), srcs[w], outs[w].at[kme])
        for j, (px, py) in enumerate(chips):
            for w in range(nw):
                got = outs[w].at[2 * px + py]
                plane(j, w, (px, py, c))(got, got).wait_recv()
                start(passed(j, w), got, got)
        for j, (px, py) in enumerate(chips):
            for w in range(nw):
                got = outs[w].at[2 * px + py]
                plane(j, w, (px, py, c))(got, got).wait_send()
                passed(j, w)(got, got).wait_send()

    @pl.when(c != layer)
    def _():
        for j, (px, py) in enumerate(chips):
            for w in range(nw):
                got = outs[w].at[2 * px + py]
                passed(j, w)(got, got).wait_recv()


def _handshake_all():
    x, y, c, _ = _place()
    barrier = pltpu.get_barrier_semaphore()
    for r in range(1, 8):
        peer = (x ^ (r >> 2), y ^ ((r >> 1) & 1), c ^ (r & 1))
        pl.semaphore_signal(barrier, inc=1, device_id=peer, device_id_type=MESH)
    pl.semaphore_wait(barrier, 7)


def _gather_layer_async(blocks, layer, name, collective_id):
    hbm = pltpu.MemorySpace.HBM
    srcs = [jax.new_ref(b, memory_space=hbm) for b in blocks]
    outs = [jax.empty_ref(jax.ShapeDtypeStruct((N_CHIPS,) + b.shape, b.dtype), memory_space=hbm) for b in blocks]

    @pl.kernel(mesh=plsc.ScalarSubcoreMesh(axis_name="seq", num_cores=1), name=name,
               scratch_types=[pltpu.SemaphoreType.DMA((3, len(blocks)))] * 4,
               compiler_params=pltpu.CompilerParams(collective_id=collective_id))
    def launch(*sems):
        _handshake_all()
        _gather_body(srcs, outs, sems, layer, lambda make, s, d: make(s, d).start())

    launch()
    return [o[...] for o in outs]


def _swap_halves(gs):
    nw = len(gs)

    def body(*refs):
        srcs, theirs = refs[:nw], refs[nw:2 * nw]
        ssem, rsem = refs[2 * nw:]
        x, y, c, _ = _place()

        def give(w):
            return lambda s, d: pltpu.make_async_remote_copy(
                src_ref=s, dst_ref=d, send_sem=ssem.at[w], recv_sem=rsem.at[w], device_id=(x, y, 1 - c),
                device_id_type=MESH)

        for w in range(nw):
            hr = theirs[w].shape[1]
            _start_pieces(give(w), srcs[w].at[:, pl.ds((1 - c) * hr, hr)], theirs[w])
        for w in range(nw):
            give(w)(theirs[w], theirs[w]).wait()

    return pl.pallas_call(
        body, name="swap_halves", in_specs=[ANY] * nw, out_specs=[ANY] * nw,
        out_shape=[jax.ShapeDtypeStruct((g.shape[0], g.shape[1] // 2, g.shape[2]), g.dtype) for g in gs],
        scratch_shapes=[pltpu.SemaphoreType.DMA((nw,))] * 2,
        compiler_params=_cparams(has_side_effects=True),
    )(*gs)


def _scatter_chips(ps):
    nw = len(ps)

    def body(*refs):
        _scatter_body(refs[:nw], refs[nw:2 * nw], refs[2 * nw:], _start_pieces)

    return pl.pallas_call(
        body, name="scatter_chips", in_specs=[ANY] * nw, out_specs=[ANY] * nw,
        out_shape=[jax.ShapeDtypeStruct(p.shape, p.dtype) for p in ps],
        scratch_shapes=[pltpu.SemaphoreType.DMA((3, nw)), pltpu.SemaphoreType.DMA((3, nw))],
        compiler_params=_cparams(has_side_effects=True),
    )(*ps)


def _scatter_body(srcs, outs, sems, start):
    nw = len(srcs)
    ssem, rsem = sems
    x, y, c, chips = _place()
    kme = 2 * x + y

    def give(j, w, to):
        return lambda s, d: pltpu.make_async_remote_copy(
            src_ref=s, dst_ref=d, send_sem=ssem.at[j, w], recv_sem=rsem.at[j, w], device_id=to,
            device_id_type=MESH)

    for j, (px, py) in enumerate(chips):
        for w in range(nw):
            start(give(j, w, (px, py, c)), srcs[w].at[2 * px + py], outs[w].at[kme])
    for j, (px, py) in enumerate(chips):
        for w in range(nw):
            got = outs[w].at[2 * px + py]
            give(j, w, (px, py, c))(got, got).wait_recv()
    for j, (px, py) in enumerate(chips):
        for w in range(nw):
            sent = srcs[w].at[2 * px + py]
            give(j, w, (px, py, c))(sent, sent).wait_send()


def _scatter_chips_async(ps, name, collective_id):
    hbm = pltpu.MemorySpace.HBM
    srcs = [jax.new_ref(p, memory_space=hbm) for p in ps]
    outs = [jax.empty_ref(jax.ShapeDtypeStruct(p.shape, p.dtype), memory_space=hbm) for p in ps]

    @pl.kernel(mesh=plsc.ScalarSubcoreMesh(axis_name="seq", num_cores=1), name=name,
               scratch_types=[pltpu.SemaphoreType.DMA((3, len(ps)))] * 2,
               compiler_params=pltpu.CompilerParams(collective_id=collective_id))
    def launch(*sems):
        _handshake_all()
        _scatter_body(srcs, outs, sems, lambda make, s, d: make(s, d).start())

    launch()
    return [o[...] for o in outs]


def _swap_siblings(rs):
    nw = len(rs)

    def body(*refs):
        srcs, outs = refs[:nw], refs[nw:2 * nw]
        ssem, rsem = refs[2 * nw:]
        x, y, c, _ = _place()

        def give(w):
            return lambda s, d: pltpu.make_async_remote_copy(
                src_ref=s, dst_ref=d, send_sem=ssem.at[w], recv_sem=rsem.at[w], device_id=(x, y, 1 - c),
                device_id_type=MESH)

        for w in range(nw):
            _start_pieces(give(w), srcs[w], outs[w])
        for w in range(nw):
            give(w)(srcs[w], outs[w]).wait()

    return pl.pallas_call(
        body, name="swap_siblings", in_specs=[ANY] * nw, out_specs=[ANY] * nw,
        out_shape=[jax.ShapeDtypeStruct(r.shape, r.dtype) for r in rs],
        scratch_shapes=[pltpu.SemaphoreType.DMA((nw,))] * 2,
        compiler_params=_cparams(has_side_effects=True),
    )(*rs)


def _allreduce_small(v):
    rows = v.shape[0]

    def body(v_ref, o_ref, buf, ssem, rsem):
        x, y, c, _ = _place()
        me = 4 * x + 2 * y + c
        buf[me] = v_ref[...]
        sends = []
        for r in range(1, 8):
            peer = (x ^ (r >> 2), y ^ ((r >> 1) & 1), c ^ (r & 1))
            cp = pltpu.make_async_remote_copy(
                src_ref=v_ref, dst_ref=buf.at[me], send_sem=ssem.at[r - 1], recv_sem=rsem.at[r - 1],
                device_id=peer, device_id_type=MESH)
            cp.start()
            sends.append(cp)
        for r in range(1, 8):
            src = me ^ r
            pltpu.make_async_remote_copy(
                src_ref=v_ref, dst_ref=buf.at[src], send_sem=ssem.at[r - 1], recv_sem=rsem.at[r - 1],
                device_id=(x, y, c), device_id_type=MESH).wait_recv()
        for cp in sends:
            cp.wait_send()
        acc = buf[0]
        for d in range(1, 8):
            acc = acc + buf[d]
        o_ref[...] = acc

    vm = pl.BlockSpec(memory_space=pltpu.VMEM)
    return pl.pallas_call(
        body, name="allreduce_small", in_specs=[vm], out_specs=vm,
        out_shape=jax.ShapeDtypeStruct(v.shape, F32),
        scratch_shapes=[pltpu.VMEM((8, rows, 128), F32), pltpu.SemaphoreType.DMA((7,)),
                        pltpu.SemaphoreType.DMA((7,))],
        compiler_params=_cparams(has_side_effects=True),
    )(v)


def _pad_heads(w, n_heads, axis):
    shp = w.shape
    w = w.reshape(shp[:axis] + (n_heads, HD) + shp[axis + 1:])
    pad = [(0, 0)] * w.ndim
    pad[axis + 1] = (0, HP - HD)
    w = jnp.pad(w, pad)
    return w.reshape(shp[:axis] + (n_heads * HP,) + shp[axis + 1:])


def _strip_heads(w, n_heads, axis):
    shp = w.shape
    w = w.reshape(shp[:axis] + (n_heads, HP) + shp[axis + 1:])
    w = lax.slice_in_dim(w, 0, HD, axis=axis + 1)
    return w.reshape(shp[:axis] + (n_heads * HD,) + shp[axis + 1:])


def _pad_win_t(wint):
    parts = [wint[:3 * CC], _pad_heads(wint[3 * CC:3 * CC + NQ * HD], NQ, 0),
             _pad_heads(wint[3 * CC + NQ * HD:3 * CC + (NQ + NKV) * HD], NKV, 0),
             _pad_heads(wint[3 * CC + (NQ + NKV) * HD:], NKV, 0)]
    return jnp.concatenate(parts, axis=0)


def _strip_win_t(gpt):
    parts = [gpt[:3 * CC], _strip_heads(gpt[O_Q:O_K], NQ, 0), _strip_heads(gpt[O_K:O_V], NKV, 0),
             _strip_heads(gpt[O_V:], NKV, 0)]
    return jnp.concatenate(parts, axis=0)


def _t(w):
    return jnp.swapaxes(w, -1, -2)


def _count(shape):
    n = 1
    for s in shape:
        n *= s
    return n


def _pack_rows(arrs):
    flat = [jnp.pad(a.reshape(-1), (0, (-_count(a.shape)) % 128)) for a in arrs]
    v = jnp.concatenate(flat)
    rows = -(-v.shape[0] // (8 * 128)) * 8
    return jnp.pad(v, (0, rows * 128 - v.shape[0])).reshape(rows, 128)


def kernel(x, norm1_g, w_in, conv_w, q_norm_g, k_norm_g, sinks, conv_out_g, attn_out_g, w_o, norm2_g, w_gate, w_up, w_down, loss_target, m_norm1_g, m_w_in, m_conv_w, m_q_norm_g, m_k_norm_g, m_sinks, m_conv_out_g, m_attn_out_g, m_w_o, m_norm2_g, m_w_gate, m_w_up, m_w_down, v_norm1_g, v_w_in, v_conv_w, v_q_norm_g, v_k_norm_g, v_sinks, v_conv_out_g, v_attn_out_g, v_w_o, v_norm2_g, v_w_gate, v_w_up, v_w_down):
    depth = w_in.shape[0]
    t = x.shape[1]
    xs = x.reshape(t, D)
    tgt = loss_target.reshape(t, D)
    xi, yi = lax.axis_index("x"), lax.axis_index("y")
    kme = 2 * xi + yi
    tm = min(512, t)
    tq = min(256, t)
    tf = min(256, t)

    cwp = jnp.pad(conv_w.reshape(depth * 3, CC // N_CHIPS), ((0, 8 - depth * 3), (0, 0)))
    own_f = [jnp.concatenate([_t(w_gate[l]), _t(w_up[l]), w_down[l]], axis=0).astype(BF16) for l in range(depth)]
    own_o = [w_o[l].astype(BF16) for l in range(depth)]
    own_i = [_t(w_in[l]).astype(BF16) for l in range(depth)]
    mine = lambda got, own: lax.dynamic_update_index_in_dim(got, own, kme, 0)
    got_i0, got_o0, got_cw = _gather_layer([own_i[0], own_o[0], cwp], 0)
    gf0_in = lax.optimization_barrier((own_f[0], got_i0))[0]
    (got_f0,) = _gather_layer_async([gf0_in], 0, "gather_ffn0_seq", collective_id=6)
    l1_in = lax.optimization_barrier(([own_f[1], own_o[1], own_i[1]], got_f0))[0]
    got_l1 = _gather_layer_async(l1_in, 1, "gather_layer1_seq", collective_id=1)
    cw_full = mine(got_cw, cwp).transpose(1, 0, 2).reshape(8, CC)[:depth * 3].reshape(depth, 3, CC)

    def layer_params(l, got_i, got_o):
        wo = mine(got_o, own_o[l]).reshape(D, D)
        return dict(
            wpt=_pad_win_t(mine(got_i, own_i[l]).reshape(N_CHIPS * 576, D)),
            wo=jnp.concatenate([wo[:CC], _pad_heads(wo[CC:], NQ, 0)], axis=0),
            cw=jnp.pad(cw_full[l], ((0, 5), (0, 0))),
            g1=norm1_g[l].reshape(1, D), g2=norm2_g[l].reshape(1, D),
            gq=jnp.pad(q_norm_g[l], (0, HP - HD)).reshape(1, HP), gk=jnp.pad(k_norm_g[l], (0, HP - HD)).reshape(1, HP),
            sk=sinks[l].reshape(1, NQ), gco=conv_out_g[l].reshape(1, CC),
            gao=_pad_heads(attn_out_g[l], NQ, 0).reshape(1, NQ * HP))

    saved, layers = [], []
    cur = xs
    for l in range(depth):
        if l == 0:
            p = layer_params(0, got_i0, got_o0)
        else:
            got_f1, got_o1, got_i1 = lax.optimization_barrier((got_l1, cur))[0]
            p = layer_params(1, got_i1, got_o1)
        proj, h = _inproj_fwd(cur, p["g1"], p["wpt"], tm)
        xm, mix = _mixer_fwd(proj, cur, p["cw"], p["gq"], p["gk"], p["sk"], p["gco"], p["gao"], p["wo"], tq)
        p["gf"] = mine(lax.optimization_barrier((got_f0, xm))[0] if l == 0 else got_f1, own_f[l])
        layers.append(p)
        xo, a, b, h2 = _ffn_fwd(xm, p["g2"], p["gf"], tf)
        saved.append(dict(x=cur, proj=proj, h=h, xm=xm, mix=mix, a=a, b=b, h2=h2))
        cur = xo
    lpart, dy = _loss_and_grad(cur, tgt, tm)
    loss = lax.psum(lpart[0, 0], ("x", "y", "c"))

    nt = t // tq
    ci = lax.axis_index("c")
    core = ci.reshape(1).astype(jnp.int32)
    rbig = [dict() for _ in range(depth)]
    gsmall = [None] * depth

    def reduce_start(gs, name, collective_id):
        ps = _presum_halves(gs, _swap_halves(gs), core)
        got = _scatter_chips(ps) if collective_id is None else _scatter_chips_async(ps, name, collective_id)
        return ps, got

    def reduce_finish(started, after):
        ps, got = started
        if after is not None:
            got = lax.optimization_barrier((got, after))[0]
        cs = [lax.dynamic_update_index_in_dim(g, lax.dynamic_index_in_dim(q, kme, 0, keepdims=False), kme, 0)
              for g, q in zip(got, ps)]
        r_mine = _sum_chips(cs, "chipsum")
        return [jnp.where(ci == 0, jnp.concatenate([a, b], axis=0), jnp.concatenate([b, a], axis=0))
                for a, b in zip(r_mine, _swap_siblings(r_mine))]

    in_flight = None
    for l in reversed(range(depth)):
        p, s = layers[l], saved[l]
        dxm, da, db, hm, dg2 = _ffn_bwd(dy, s["xm"], p["g2"], s["a"], s["b"], p["gf"], tf)
        g_wg = _wgrad_blocks(da, s["h2"], tm, "wgrad_gate")
        g_wu = _wgrad_blocks(db, s["h2"], tm, "wgrad_up")
        g_wd = _wgrad_blocks(hm, dy, tm, "wgrad_down")
        if in_flight is not None:
            rbig[l + 1]["in"], rbig[l + 1]["o"] = reduce_finish(in_flight, g_wd)
        ffn_flight = reduce_start([g_wg, g_wu, g_wd], f"scatter_ffn{l}_seq", 2 + 2 * l)
        dpm, dkvm, dkvh, dcw, dgq, dgk, dsk, dgco, dgao = _mixer_bwd(
            dxm, s["proj"], p["cw"], p["gq"], p["gk"], p["sk"], p["gco"], p["gao"], p["wo"], tq)
        g_wo = _wgrad(s["mix"], dxm, D, tm, "wgrad_o")
        kvw = dkvm.shape[1]
        halo = jnp.concatenate([dkvh.reshape(nt, BLK, kvw)[1:], jnp.zeros((1, BLK, kvw), F32)], axis=0)
        halo = jnp.pad(halo, ((0, 0), (tq - BLK, 0), (0, 0)))
        dkv = (dkvm.reshape(nt, tq, kvw) + halo).reshape(t, kvw).astype(BF16)
        dx, dg1 = _inproj_bwd(dpm, dkv, p["wpt"], s["x"], p["g1"], dxm, tm)
        g_wpm = _wgrad(dpm, s["h"], D, tm, "wgrad_in_main")
        g_wpk = _wgrad(dkv, s["h"], D, tm, "wgrad_in_kv")
        dy = dx
        g_in = _strip_win_t(jnp.concatenate([g_wpm, g_wpk], axis=0)).astype(BF16)
        g_o = jnp.concatenate([g_wo[:CC], _strip_heads(g_wo[CC:], NQ, 0)], axis=0).astype(BF16)
        gsmall[l] = dict(g1=dg1, cw=dcw[:3], gq=dgq[0, :HD], gk=dgk[0, :HD], sk=dsk[0, :NQ], gco=dgco,
                         gao=_strip_heads(dgao.reshape(NQ * HP), NQ, 0), g2=dg2)
        rbig[l]["g"], rbig[l]["u"], rbig[l]["d"] = reduce_finish(ffn_flight, dx)
        in_flight = reduce_start([g_in.reshape(N_CHIPS, -1, D), g_o.reshape(N_CHIPS, -1, D)],
                                 f"scatter_in{l}_seq", 3 + 2 * l if l > 0 else None)
    rbig[0]["in"], rbig[0]["o"] = reduce_finish(in_flight, None)
    grad_x = dy.reshape(x.shape)
    g_big = [jnp.stack([rbig[l][w] for l in range(depth)]) for w in ("in", "o", "g", "u", "d")]

    small_shapes = dict(g1=(D,), cw=(3, CC), gq=(HD,), gk=(HD,), sk=(NQ,), gco=(CC,), gao=(NQ * HD,), g2=(D,))
    red = _allreduce_small(_pack_rows([gsmall[l][n] for l in range(depth) for n in small_shapes])).reshape(-1)
    red_small, offs = {n: [] for n in small_shapes}, 0
    for l in range(depth):
        for n, shp in small_shapes.items():
            cnt = _count(shp)
            red_small[n].append(red[offs:offs + cnt].reshape(shp))
            offs += -(-cnt // 128) * 128
    g_small = {n: jnp.stack(v) for n, v in red_small.items()}
    g_cw = lax.dynamic_slice_in_dim(g_small["cw"], kme * (CC // N_CHIPS), CC // N_CHIPS, axis=2)

    weights = [norm1_g, w_in, conv_w, q_norm_g, k_norm_g, sinks, conv_out_g, attn_out_g, w_o, norm2_g, w_gate,
               w_up, w_down]
    moms = [m_norm1_g, m_w_in, m_conv_w, m_q_norm_g, m_k_norm_g, m_sinks, m_conv_out_g, m_attn_out_g, m_w_o,
            m_norm2_g, m_w_gate, m_w_up, m_w_down]
    vars_ = [v_norm1_g, v_w_in, v_conv_w, v_q_norm_g, v_k_norm_g, v_sinks, v_conv_out_g, v_attn_out_g, v_w_o,
             v_norm2_g, v_w_gate, v_w_up, v_w_down]
    grads = [g_small["g1"], _t(g_big[0]), g_cw, g_small["gq"], g_small["gk"], g_small["sk"], g_small["gco"],
             g_small["gao"], g_big[1], g_small["g2"], _t(g_big[2]), _t(g_big[3]), g_big[4]]
    n_w = len(weights)
    big_idx = [1, 8, 10, 11, 12]
    small_idx = [n for n in range(n_w) if n not in big_idx]
    deltas, new_m, new_v = [None] * n_w, [None] * n_w, [None] * n_w
    for n, g in zip(big_idx, g_big):
        flip = g.shape != weights[n].shape
        rows2d = lambda a3: (_t(a3) if flip else a3).reshape(-1, D)
        res = _adamw(rows2d(weights[n]), g.reshape(-1, D), rows2d(moms[n]), rows2d(vars_[n]), f"adamw_{n}")
        res = [r.reshape(g.shape) for r in res]
        deltas[n], new_m[n], new_v[n] = [_t(r) for r in res] if flip else res
    res = _adamw(*[_pack_rows([arrs[n] for n in small_idx]) for arrs in (weights, grads, moms, vars_)],
                 "adamw_small")
    offs = 0
    for n in small_idx:
        shp = weights[n].shape
        cnt = _count(shp)
        deltas[n], new_m[n], new_v[n] = [r.reshape(-1)[offs:offs + cnt].reshape(shp) for r in res]
        offs += -(-cnt // 128) * 128
    return (loss, grad_x, *grads, *deltas, *new_m, *new_v)
```

```python
import functools

import jax
import jax.numpy as jnp
from jax import lax
from jax.experimental import pallas as pl
from jax.experimental.pallas import tpu as pltpu
from jax.experimental.pallas import tpu_sc as plsc

F32 = jnp.float32
BF16 = jnp.bfloat16

D = 1024
CC = 512
NQ = 8
NKV = 2
HD = 64
HP = 128
GRP = NQ // NKV
FF = 2816
FFB = FF // 4
BLK = 128
EPS = 1e-6
NEG = -1e30
SCALE = HD ** -0.5
O_BG, O_CG, O_HC, O_Q = 0, CC, 2 * CC, 3 * CC
O_K = O_Q + NQ * HP
O_V = O_K + NKV * HP
NP = O_V + NKV * HP
NMAIN = O_K
MIXW = CC + NQ * HP
N_CHIPS = 4
VMEM_LIMIT = 56 * 1024 * 1024
MESH = pl.DeviceIdType.MESH

ADAM_LR, ADAM_B1, ADAM_B2, ADAM_EPS, ADAM_WD, ADAM_STEP = 0.001, 0.9, 0.999, 1e-08, 0.01, 10


def _cparams(sem=None, **kw):
    if sem is not None:
        kw["dimension_semantics"] = sem
    return pltpu.CompilerParams(vmem_limit_bytes=VMEM_LIMIT, **kw)


def _const_spec(shape):
    nd = len(shape)
    return pl.BlockSpec(shape, lambda *_: (0,) * nd, pipeline_mode=pl.Buffered(1))


def _nt(a, b):
    return lax.dot_general(a, b, (((1,), (1,)), ((), ())), preferred_element_type=F32)


def _tn(a, b):
    return lax.dot_general(a, b, (((0,), (0,)), ((), ())), preferred_element_type=F32)


def _rms_fwd(x, inv_n):
    r = lax.rsqrt(jnp.sum(x * x, axis=-1, keepdims=True) * inv_n + EPS)
    return r, x * r


def _rms_bwd(dy, g, xh, r, inv_n):
    dxh = dy * g
    return r * (dxh - xh * (jnp.sum(dxh * xh, axis=-1, keepdims=True) * inv_n))


def _inproj_fwd(x, g1, wpt, tm):
    t = x.shape[0]

    def body(x_ref, g_ref, w_ref, p_ref, h_ref):
        _, xh = _rms_fwd(x_ref[...], 1.0 / D)
        h = (xh * g_ref[...]).astype(BF16)
        h_ref[...] = h
        p_ref[...] = _nt(h, w_ref[...])

    return pl.pallas_call(
        body, name="inproj_fwd", grid=(t // tm,),
        in_specs=[pl.BlockSpec((tm, D), lambda i: (i, 0)), _const_spec((1, D)), _const_spec((NP, D))],
        out_specs=[pl.BlockSpec((tm, NP), lambda i: (i, 0)), pl.BlockSpec((tm, D), lambda i: (i, 0))],
        out_shape=[jax.ShapeDtypeStruct((t, NP), F32), jax.ShapeDtypeStruct((t, D), BF16)],
        compiler_params=_cparams(("parallel",)),
    )(x, g1, wpt)


def _band_mask():
    r_io = lax.broadcasted_iota(jnp.int32, (BLK, 2 * BLK), 0)
    c_io = lax.broadcasted_iota(jnp.int32, (BLK, 2 * BLK), 1)
    return (c_io > r_io) & (c_io <= r_io + BLK), c_io


def _conv_taps(uf, n):
    u1 = pltpu.roll(uf, 1, 0)[8:8 + n]
    u2 = pltpu.roll(uf, 2, 0)[8:8 + n]
    return u1, u2


def _attn_probs(qn, kband, sink, valid):
    s = _nt(qn, kband) * SCALE
    s = jnp.where(valid, s, NEG)
    m = jnp.maximum(jnp.max(s, axis=-1, keepdims=True), sink)
    p = jnp.exp(s - m)
    es = jnp.exp(sink - m)
    inv = 1.0 / (jnp.sum(p, axis=-1, keepdims=True) + es)
    return p * inv, es * inv


def _norm_keys(kraw, gk):
    out = []
    for h in range(NKV):
        kh = kraw[:, h * HP:(h + 1) * HP]
        rk, khat = _rms_fwd(kh, 1.0 / HD)
        out.append((khat, rk, (khat * gk).astype(BF16)))
    return out


def _mixer_fwd(proj, x, cw, gq, gk, sinks, gco, gao, wo, tq):
    t = proj.shape[0]
    nb = tq // BLK
    r8 = tq // 8

    def body(p_ref, cgp_ref, hcp_ref, kvp_ref, x_ref, cw_ref, gq_ref, gk_ref, sk_ref, gco_ref, gao_ref,
             wo_ref, xm_ref, mix_ref):
        i = pl.program_id(0)
        cg = p_ref[:, O_CG:O_CG + CC]
        hc = p_ref[:, O_HC:O_HC + CC]
        u = cg * hc
        up = jnp.where(i > 0, cgp_ref[...] * hcp_ref[...], 0.0)
        u1, u2 = _conv_taps(jnp.concatenate([up, u], axis=0), tq)
        y = cw_ref[0:1, :] * u2 + cw_ref[1:2, :] * u1 + cw_ref[2:3, :] * u
        co = p_ref[:, O_BG:O_BG + CC] * y
        _, coh = _rms_fwd(co, 1.0 / CC)
        cn = coh * gco_ref[...]
        kraw = jnp.concatenate([kvp_ref[:, 0:NKV * HP], p_ref[:, O_K:O_K + NKV * HP]], axis=0)
        vraw = jnp.concatenate([kvp_ref[:, NKV * HP:], p_ref[:, O_V:O_V + NKV * HP]], axis=0)
        keys = _norm_keys(kraw, gk_ref[...])
        vb = [vraw[:, h * HP:(h + 1) * HP].astype(BF16) for h in range(NKV)]
        base_valid, c_io = _band_mask()
        rows = []
        for b in range(nb):
            lo = jnp.where(i * nb + b == 0, BLK, 0)
            valid = base_valid & (c_io >= lo)
            outs = []
            for g in range(NQ):
                h = g // GRP
                qg = p_ref[b * BLK:(b + 1) * BLK, O_Q + g * HP:O_Q + (g + 1) * HP]
                _, qh = _rms_fwd(qg, 1.0 / HD)
                qn = (qh * gq_ref[...]).astype(BF16)
                pr, _ = _attn_probs(qn, keys[h][2][b * BLK:b * BLK + 2 * BLK], sk_ref[0, g], valid)
                outs.append(jnp.dot(pr.astype(BF16), vb[h][b * BLK:b * BLK + 2 * BLK],
                                    preferred_element_type=F32))
            rows.append(jnp.concatenate(outs, axis=1))
        ao = jnp.concatenate(rows, axis=0)
        _, aoh = _rms_fwd(ao, 1.0 / (NQ * HD))
        an = aoh * gao_ref[...]
        mix = jnp.concatenate([cn, an], axis=1).astype(BF16)
        mix_ref[...] = mix
        xm_ref[...] = x_ref[...] + jnp.dot(mix, wo_ref[...], preferred_element_type=F32)

    prev8 = lambda col: pl.BlockSpec((8, CC), lambda i: (jnp.maximum(i * r8 - 1, 0), col))
    return pl.pallas_call(
        body, name="mixer_fwd", grid=(t // tq,),
        in_specs=[
            pl.BlockSpec((tq, NP), lambda i: (i, 0)),
            prev8(O_CG // CC), prev8(O_HC // CC),
            pl.BlockSpec((BLK, 2 * NKV * HP), lambda i: (jnp.maximum(i * nb - 1, 0), O_K // (2 * NKV * HP))),
            pl.BlockSpec((tq, D), lambda i: (i, 0)),
            _const_spec((8, CC)), _const_spec((1, HP)), _const_spec((1, HP)),
            pl.BlockSpec(memory_space=pltpu.SMEM),
            _const_spec((1, CC)), _const_spec((1, NQ * HP)), _const_spec((MIXW, D)),
        ],
        out_specs=[pl.BlockSpec((tq, D), lambda i: (i, 0)), pl.BlockSpec((tq, MIXW), lambda i: (i, 0))],
        out_shape=[jax.ShapeDtypeStruct((t, D), F32), jax.ShapeDtypeStruct((t, MIXW), BF16)],
        compiler_params=_cparams(("parallel",)),
    )(proj, proj, proj, proj, x, cw, gq, gk, sinks, gco, gao, wo)


def _ffn_weight_specs():
    return [pl.BlockSpec((N_CHIPS, FFB, D), lambda i, j=j: (0, j, 0), pipeline_mode=pl.Buffered(1))
            for j in range(3)]


def _ffn_fwd(xm, g2, gf, tm):
    t = xm.shape[0]

    def body(x_ref, g_ref, wg_ref, wu_ref, wd_ref, xo_ref, a_ref, b_ref, h2_ref):
        xv = x_ref[...]
        _, xh = _rms_fwd(xv, 1.0 / D)
        h2 = (xh * g_ref[...]).astype(BF16)
        h2_ref[...] = h2
        acc = xv
        for k in range(N_CHIPS):
            a = _nt(h2, wg_ref[k])
            b = _nt(h2, wu_ref[k])
            a_ref[k] = a.astype(BF16)
            b_ref[k] = b.astype(BF16)
            hm = (a * jax.nn.sigmoid(a) * b).astype(BF16)
            acc = acc + jnp.dot(hm, wd_ref[k], preferred_element_type=F32)
        xo_ref[...] = acc

    row = lambda w: pl.BlockSpec((tm, w), lambda i: (i, 0))
    blk = pl.BlockSpec((N_CHIPS, tm, FFB), lambda i: (0, i, 0))
    return pl.pallas_call(
        body, name="ffn_fwd", grid=(t // tm,),
        in_specs=[row(D), _const_spec((1, D))] + _ffn_weight_specs(),
        out_specs=[row(D), blk, blk, row(D)],
        out_shape=[jax.ShapeDtypeStruct((t, D), F32), jax.ShapeDtypeStruct((N_CHIPS, t, FFB), BF16),
                   jax.ShapeDtypeStruct((N_CHIPS, t, FFB), BF16), jax.ShapeDtypeStruct((t, D), BF16)],
        compiler_params=_cparams(("parallel",)),
    )(xm, g2, gf, gf, gf)


def _loss_and_grad(y, tgt, tm):
    t = y.shape[0]

    def body(y_ref, t_ref, l_ref, dy_ref):
        @pl.when(pl.program_id(0) == 0)
        def _():
            l_ref[...] = jnp.zeros_like(l_ref)

        e = y_ref[...] - t_ref[...]
        dy_ref[...] = e * (1.0 / D)
        s = jnp.sum(jnp.sum(e * e, axis=-1, keepdims=True), axis=0, keepdims=True)
        l_ref[...] += s * (0.5 / D)

    row = pl.BlockSpec((tm, D), lambda i: (i, 0))
    return pl.pallas_call(
        body, name="loss", grid=(t // tm,), in_specs=[row, row],
        out_specs=[pl.BlockSpec((8, 128), lambda i: (0, 0)), row],
        out_shape=[jax.ShapeDtypeStruct((8, 128), F32), jax.ShapeDtypeStruct((t, D), F32)],
        compiler_params=_cparams(("arbitrary",)),
    )(y, tgt)


def _ffn_bwd(dy, xm, g2, a, b, gf, tm):
    t = dy.shape[0]

    def body(dy_ref, x_ref, g_ref, a_ref, b_ref, wg_ref, wu_ref, wd_ref, dx_ref, da_ref, db_ref, hm_ref, dg_ref):
        @pl.when(pl.program_id(0) == 0)
        def _():
            dg_ref[...] = jnp.zeros_like(dg_ref)

        dyv = dy_ref[...]
        dyb = dyv.astype(BF16)
        dh2 = jnp.zeros_like(dyv)
        for k in range(N_CHIPS):
            dhm = _nt(dyb, wd_ref[k])
            av = a_ref[k].astype(F32)
            bv = b_ref[k].astype(F32)
            sig = jax.nn.sigmoid(av)
            sil = av * sig
            hm_ref[k] = (sil * bv).astype(BF16)
            da = (dhm * bv * (sig * (1.0 + av * (1.0 - sig)))).astype(BF16)
            db = (dhm * sil).astype(BF16)
            da_ref[k] = da
            db_ref[k] = db
            dh2 = (dh2 + jnp.dot(da, wg_ref[k], preferred_element_type=F32)
                   + jnp.dot(db, wu_ref[k], preferred_element_type=F32))
        r, xh = _rms_fwd(x_ref[...], 1.0 / D)
        dg_ref[...] += jnp.sum(dh2 * xh, axis=0, keepdims=True)
        dx_ref[...] = dyv + _rms_bwd(dh2, g_ref[...], xh, r, 1.0 / D)

    row = lambda w: pl.BlockSpec((tm, w), lambda i: (i, 0))
    blk = pl.BlockSpec((N_CHIPS, tm, FFB), lambda i: (0, i, 0))
    bsd = jax.ShapeDtypeStruct((N_CHIPS, t, FFB), BF16)
    return pl.pallas_call(
        body, name="ffn_bwd", grid=(t // tm,),
        in_specs=[row(D), row(D), _const_spec((1, D)), blk, blk] + _ffn_weight_specs(),
        out_specs=[row(D), blk, blk, blk, pl.BlockSpec((1, D), lambda i: (0, 0))],
        out_shape=[jax.ShapeDtypeStruct((t, D), F32), bsd, bsd, bsd, jax.ShapeDtypeStruct((1, D), F32)],
        compiler_params=_cparams(("arbitrary",)),
    )(dy, xm, g2, a, b, gf, gf, gf)


def _wgrad_blocks(a, b, tt, name):
    _, t, rows = a.shape
    cols = b.shape[1]
    nsteps = t // tt

    def body(a_ref, b_ref, o_ref, acc_ref):
        s = pl.program_id(0)

        @pl.when(s == 0)
        def _():
            acc_ref[...] = jnp.zeros_like(acc_ref)

        bv = b_ref[...].astype(BF16)
        for k in range(N_CHIPS):
            acc_ref[k] += _tn(a_ref[k], bv)

        @pl.when(s == nsteps - 1)
        def _():
            o_ref[...] = acc_ref[...].astype(BF16)

    return pl.pallas_call(
        body, name=name, grid=(nsteps,),
        in_specs=[pl.BlockSpec((N_CHIPS, tt, rows), lambda s: (0, s, 0)), pl.BlockSpec((tt, cols), lambda s: (s, 0))],
        out_specs=pl.BlockSpec((N_CHIPS, rows, cols), lambda s: (0, 0, 0)),
        out_shape=jax.ShapeDtypeStruct((N_CHIPS, rows, cols), BF16),
        scratch_shapes=[pltpu.VMEM((N_CHIPS, rows, cols), F32)],
        compiler_params=_cparams(("arbitrary",)),
    )(a, b)


def _wgrad(a, b, tn, tt, name):
    t, k = a.shape
    n = b.shape[1]
    nsteps = t // tt

    def body(a_ref, b_ref, o_ref):
        @pl.when(pl.program_id(1) == 0)
        def _():
            o_ref[...] = jnp.zeros_like(o_ref)

        o_ref[...] += _tn(a_ref[...].astype(BF16), b_ref[...].astype(BF16))

    return pl.pallas_call(
        body, name=name, grid=(n // tn, nsteps),
        in_specs=[pl.BlockSpec((tt, k), lambda j, s: (s, 0)), pl.BlockSpec((tt, tn), lambda j, s: (s, j))],
        out_specs=pl.BlockSpec((k, tn), lambda j, s: (0, j)),
        out_shape=jax.ShapeDtypeStruct((k, n), F32),
        compiler_params=_cparams(("parallel", "arbitrary")),
    )(a, b)


def _mixer_bwd(dxm, proj, cw, gq, gk, sinks, gco, gao, wo, tq):
    t = proj.shape[0]
    nb = tq // BLK
    r8 = tq // 8
    nt = t // tq
    te = tq + 8
    kvw = 2 * NKV * HP

    def body(dx_ref, dxn_ref, p_ref, cgp_ref, hcp_ref, bgn_ref, cgn_ref, hcn_ref, kvp_ref, cw_ref, gq_ref,
             gk_ref, sk_ref, gco_ref, gao_ref, wo_ref,
             dpm_ref, dkvm_ref, dkvh_ref, dcw_ref, dgq_ref, dgk_ref, dsk_ref, dgco_ref, dgao_ref, acc_ref):
        i = pl.program_id(0)

        @pl.when(i == 0)
        def _():
            for r in (dcw_ref, dgq_ref, dgk_ref, dsk_ref, dgco_ref, dgao_ref):
                r[...] = jnp.zeros_like(r)

        acc_ref[...] = jnp.zeros_like(acc_ref)
        live_rows = jnp.where(i < nt - 1, te, tq)
        dxb = dx_ref[...].astype(BF16)
        dxe = jnp.concatenate([dxb, dxn_ref[...].astype(BF16)], axis=0)
        dcn = _nt(dxe, wo_ref[0:CC, :])
        bg = jnp.concatenate([p_ref[:, O_BG:O_BG + CC], bgn_ref[...]], axis=0)
        cg = jnp.concatenate([p_ref[:, O_CG:O_CG + CC], cgn_ref[...]], axis=0)
        hc = jnp.concatenate([p_ref[:, O_HC:O_HC + CC], hcn_ref[...]], axis=0)
        u = cg * hc
        up = jnp.where(i > 0, cgp_ref[...] * hcp_ref[...], 0.0)
        u1, u2 = _conv_taps(jnp.concatenate([up, u], axis=0), te)
        w0, w1, w2 = cw_ref[0:1, :], cw_ref[1:2, :], cw_ref[2:3, :]
        y = w0 * u2 + w1 * u1 + w2 * u
        co = bg * y
        rc, coh = _rms_fwd(co, 1.0 / CC)
        dco = _rms_bwd(dcn, gco_ref[...], coh, rc, 1.0 / CC)
        row_io = lax.broadcasted_iota(jnp.int32, (te, 1), 0)
        own = row_io < tq
        dgco_ref[...] += jnp.sum(jnp.where(own, dcn * coh, 0.0), axis=0, keepdims=True)
        dyc = jnp.where(row_io < live_rows, dco * bg, 0.0)
        dyo = jnp.where(own, dyc, 0.0)
        dcw_ref[0:1, :] += jnp.sum(dyo * u2, axis=0, keepdims=True)
        dcw_ref[1:2, :] += jnp.sum(dyo * u1, axis=0, keepdims=True)
        dcw_ref[2:3, :] += jnp.sum(dyo * u, axis=0, keepdims=True)
        dy1 = pltpu.roll(dyc, te - 1, 0)[0:tq]
        dy2 = pltpu.roll(dyc, te - 2, 0)[0:tq]
        du = w2 * dyc[0:tq] + w1 * dy1 + w0 * dy2
        dpm_ref[:, O_BG:O_BG + CC] = (dco[0:tq] * y[0:tq]).astype(BF16)
        dpm_ref[:, O_CG:O_CG + CC] = (du * hc[0:tq]).astype(BF16)
        dpm_ref[:, O_HC:O_HC + CC] = (du * cg[0:tq]).astype(BF16)
        dan = _nt(dxb, wo_ref[CC:MIXW, :])
        kraw = jnp.concatenate([kvp_ref[:, 0:NKV * HP], p_ref[:, O_K:O_K + NKV * HP]], axis=0)
        vraw = jnp.concatenate([kvp_ref[:, NKV * HP:], p_ref[:, O_V:O_V + NKV * HP]], axis=0)
        gqv, gkv = gq_ref[...], gk_ref[...]
        keys = _norm_keys(kraw, gkv)
        vb = [vraw[:, h * HP:(h + 1) * HP].astype(BF16) for h in range(NKV)]
        base_valid, c_io = _band_mask()
        lane = lax.broadcasted_iota(jnp.int32, (1, HP), 1)
        for b in range(nb):
            lo = jnp.where(i * nb + b == 0, BLK, 0)
            valid = base_valid & (c_io >= lo)
            band = slice(b * BLK, b * BLK + 2 * BLK)
            qs, prs, pss, outs = [], [], [], []
            for g in range(NQ):
                h = g // GRP
                qg = p_ref[b * BLK:(b + 1) * BLK, O_Q + g * HP:O_Q + (g + 1) * HP]
                rq, qh = _rms_fwd(qg, 1.0 / HD)
                qn = (qh * gqv).astype(BF16)
                pr, ps = _attn_probs(qn, keys[h][2][band], sk_ref[0, g], valid)
                qs.append((rq, qh, qn))
                prs.append(pr)
                pss.append(ps)
                outs.append(jnp.dot(pr.astype(BF16), vb[h][band], preferred_element_type=F32))
            ao = jnp.concatenate(outs, axis=1)
            ra, aoh = _rms_fwd(ao, 1.0 / (NQ * HD))
            danb = dan[b * BLK:(b + 1) * BLK]
            dgao_ref[...] += jnp.sum(danb * aoh, axis=0, keepdims=True)
            dao = _rms_bwd(danb, gao_ref[...], aoh, ra, 1.0 / (NQ * HD))
            dqs = []
            for h in range(NKV):
                dss, dobs = [], []
                for g in range(h * GRP, (h + 1) * GRP):
                    rq, qh, qn = qs[g]
                    dob = dao[:, g * HP:(g + 1) * HP].astype(BF16)
                    dp = _nt(dob, vb[h][band])
                    delta = jnp.sum(prs[g] * dp, axis=-1, keepdims=True)
                    dsb = (prs[g] * (dp - delta) * SCALE).astype(BF16)
                    dsk = -jnp.sum(pss[g] * delta, axis=0, keepdims=True)
                    dsk_ref[...] += jnp.where(lane == g, dsk, 0.0)
                    dqn = jnp.dot(dsb, keys[h][2][band], preferred_element_type=F32)
                    dgq_ref[...] += jnp.sum(dqn * qh, axis=0, keepdims=True)
                    dqs.append(_rms_bwd(dqn, gqv, qh, rq, 1.0 / HD).astype(BF16))
                    dss.append(dsb)
                    dobs.append(dob)
                grp = slice(h * GRP, (h + 1) * GRP)
                dkn = _tn(jnp.concatenate(dss, axis=0), jnp.concatenate([q[2] for q in qs[grp]], axis=0))
                dv = _tn(jnp.concatenate([p.astype(BF16) for p in prs[grp]], axis=0),
                         jnp.concatenate(dobs, axis=0))
                khat, rk = keys[h][0][band], keys[h][1][band]
                dgk_ref[...] += jnp.sum(dkn * khat, axis=0, keepdims=True)
                acc_ref[band, h * HP:(h + 1) * HP] += _rms_bwd(dkn, gkv, khat, rk, 1.0 / HD)
                acc_ref[band, (NKV + h) * HP:(NKV + h + 1) * HP] += dv
            dpm_ref[b * BLK:(b + 1) * BLK, O_Q:O_K] = jnp.concatenate(dqs, axis=1)
        dkvh_ref[...] = acc_ref[0:BLK, :]
        dkvm_ref[...] = acc_ref[BLK:, :]

    prev8 = lambda col: pl.BlockSpec((8, CC), lambda i: (jnp.maximum(i * r8 - 1, 0), col))
    next8 = lambda col: pl.BlockSpec((8, CC), lambda i: (jnp.minimum((i + 1) * r8, t // 8 - 1), col))
    small = lambda n: pl.BlockSpec((1, n), lambda i: (0, 0))
    return pl.pallas_call(
        body, name="mixer_bwd", grid=(nt,),
        in_specs=[
            pl.BlockSpec((tq, D), lambda i: (i, 0)),
            pl.BlockSpec((8, D), lambda i: (jnp.minimum((i + 1) * r8, t // 8 - 1), 0)),
            pl.BlockSpec((tq, NP), lambda i: (i, 0)),
            prev8(O_CG // CC), prev8(O_HC // CC),
            next8(O_BG // CC), next8(O_CG // CC), next8(O_HC // CC),
            pl.BlockSpec((BLK, kvw), lambda i: (jnp.maximum(i * nb - 1, 0), O_K // kvw)),
            _const_spec((8, CC)), _const_spec((1, HP)), _const_spec((1, HP)),
            pl.BlockSpec(memory_space=pltpu.SMEM),
            _const_spec((1, CC)), _const_spec((1, NQ * HP)), _const_spec((MIXW, D)),
        ],
        out_specs=[
            pl.BlockSpec((tq, NMAIN), lambda i: (i, 0)),
            pl.BlockSpec((tq, kvw), lambda i: (i, 0)),
            pl.BlockSpec((BLK, kvw), lambda i: (i, 0)),
            pl.BlockSpec((8, CC), lambda i: (0, 0)), small(HP), small(HP), small(HP), small(CC), small(NQ * HP),
        ],
        out_shape=[
            jax.ShapeDtypeStruct((t, NMAIN), BF16), jax.ShapeDtypeStruct((t, kvw), F32),
            jax.ShapeDtypeStruct((nt * BLK, kvw), F32),
            jax.ShapeDtypeStruct((8, CC), F32), jax.ShapeDtypeStruct((1, HP), F32), jax.ShapeDtypeStruct((1, HP), F32),
            jax.ShapeDtypeStruct((1, HP), F32), jax.ShapeDtypeStruct((1, CC), F32),
            jax.ShapeDtypeStruct((1, NQ * HP), F32),
        ],
        scratch_shapes=[pltpu.VMEM((tq + BLK, kvw), F32)],
        compiler_params=_cparams(("arbitrary",)),
    )(dxm, dxm, proj, proj, proj, proj, proj, proj, proj, cw, gq, gk, sinks, gco, gao, wo)


def _inproj_bwd(dpm, dkv, wpt, x, g1, dxm, tm):
    t = x.shape[0]
    kvw = 2 * NKV * HP

    def body(dp_ref, dk_ref, w_ref, x_ref, g_ref, dxm_ref, dx_ref, dg_ref):
        @pl.when(pl.program_id(0) == 0)
        def _():
            dg_ref[...] = jnp.zeros_like(dg_ref)

        dh = (jnp.dot(dp_ref[...], w_ref[0:NMAIN, :], preferred_element_type=F32)
              + jnp.dot(dk_ref[...], w_ref[NMAIN:NP, :], preferred_element_type=F32))
        r, xh = _rms_fwd(x_ref[...], 1.0 / D)
        dg_ref[...] += jnp.sum(dh * xh, axis=0, keepdims=True)
        dx_ref[...] = dxm_ref[...] + _rms_bwd(dh, g_ref[...], xh, r, 1.0 / D)

    row = lambda w: pl.BlockSpec((tm, w), lambda i: (i, 0))
    return pl.pallas_call(
        body, name="inproj_bwd", grid=(t // tm,),
        in_specs=[row(NMAIN), row(kvw), _const_spec((NP, D)), row(D), _const_spec((1, D)), row(D)],
        out_specs=[row(D), pl.BlockSpec((1, D), lambda i: (0, 0))],
        out_shape=[jax.ShapeDtypeStruct((t, D), F32), jax.ShapeDtypeStruct((1, D), F32)],
        compiler_params=_cparams(("arbitrary",)),
    )(dpm, dkv, wpt, x, g1, dxm)


def _rows_tile(rows, cap=512):
    for cand in range(min(rows, cap) // 16 * 16, 0, -16):
        if rows % cand == 0:
            return cand
    return rows


def _presum_halves(gs, theirs, core):
    outs = []
    for n, (ga, ta) in enumerate(zip(gs, theirs)):
        _, hr, cols = ta.shape

        def body(c_ref, g_ref, t_ref, o_ref):
            o_ref[...] = (g_ref[...].astype(F32) + t_ref[...].astype(F32)).astype(BF16)

        half = pl.BlockSpec((None, hr, cols), lambda k, c_ref: (k, 0, 0))
        outs.append(pl.pallas_call(
            body, name=f"presum_{n}",
            grid_spec=pltpu.PrefetchScalarGridSpec(
                num_scalar_prefetch=1, grid=(N_CHIPS,),
                in_specs=[pl.BlockSpec((None, hr, cols), lambda k, c_ref: (k, c_ref[0], 0)), half],
                out_specs=half),
            out_shape=jax.ShapeDtypeStruct(ta.shape, BF16), compiler_params=_cparams(("parallel",)),
        )(core, ga, ta))
    return outs


def _sum_chips(cs, name):
    outs = []
    for n, ca in enumerate(cs):
        _, rows, cols = ca.shape
        tr = _rows_tile(rows)

        def body(c_ref, o_ref):
            acc = c_ref[0].astype(F32)
            for j in range(1, N_CHIPS):
                acc = acc + c_ref[j].astype(F32)
            o_ref[...] = acc

        outs.append(pl.pallas_call(
            body, name=f"{name}_{n}", grid=(rows // tr,),
            in_specs=[pl.BlockSpec((N_CHIPS, tr, cols), lambda i: (0, i, 0))],
            out_specs=pl.BlockSpec((tr, cols), lambda i: (i, 0)),
            out_shape=jax.ShapeDtypeStruct((rows, cols), F32), compiler_params=_cparams(("parallel",)),
        )(ca))
    return outs


def _adamw(w, g, m, v, name):
    rows, cols = w.shape
    tr = _rows_tile(rows, 256)
    c1 = 1.0 - ADAM_B1 ** ADAM_STEP
    c2 = 1.0 - ADAM_B2 ** ADAM_STEP

    def body(w_ref, g_ref, m_ref, v_ref, d_ref, mo_ref, vo_ref):
        gv = g_ref[...]
        mn = ADAM_B1 * m_ref[...] + (1.0 - ADAM_B1) * gv
        vn = ADAM_B2 * v_ref[...] + (1.0 - ADAM_B2) * (gv * gv)
        mo_ref[...] = mn
        vo_ref[...] = vn
        d_ref[...] = -ADAM_LR * ((mn / c1) / (jnp.sqrt(vn / c2) + ADAM_EPS) + ADAM_WD * w_ref[...])

    spec = pl.BlockSpec((tr, cols), lambda i: (i, 0))
    sds = jax.ShapeDtypeStruct((rows, cols), F32)
    return pl.pallas_call(
        body, name=name, grid=(rows // tr,), in_specs=[spec] * 4, out_specs=[spec] * 3, out_shape=[sds] * 3,
        compiler_params=_cparams(("parallel",)),
    )(w, g, m, v)


def _place():
    x, y, c = lax.axis_index("x"), lax.axis_index("y"), lax.axis_index("c")
    chips = [(1 - x, y), (x, 1 - y), (1 - x, 1 - y)]
    return x, y, c, chips


ANY = pl.BlockSpec(memory_space=pl.ANY)
DMA_ROWS = 64


def _pieces(shape):
    rows = shape[-2]
    step = DMA_ROWS if rows % DMA_ROWS == 0 else rows
    lead = [()]
    for n in shape[:-2]:
        lead = [i + (k,) for i in lead for k in range(n)]
    return [i + (pl.ds(r0, step),) for i in lead for r0 in range(0, rows, step)]


def _start_pieces(make, src, dst):
    for idx in _pieces(src.shape):
        make(src.at[idx], dst.at[idx]).start()


def _gather_layer(blocks, layer):
    nw = len(blocks)

    def body(*refs):
        _gather_body(refs[:nw], refs[nw:2 * nw], refs[2 * nw:], layer, _start_pieces)

    return pl.pallas_call(
        body, name=f"gather_layer{layer}", in_specs=[ANY] * nw, out_specs=[ANY] * nw,
        out_shape=[jax.ShapeDtypeStruct((N_CHIPS,) + b.shape, b.dtype) for b in blocks],
        scratch_shapes=[pltpu.SemaphoreType.DMA((3, nw))] * 4,
        compiler_params=_cparams(has_side_effects=True),
    )(*blocks)


def _gather_body(srcs, outs, sems, layer, start):
    nw = len(srcs)
    ssem, rsem, fssem, frsem = sems
    x, y, c, chips = _place()
    kme = 2 * x + y

    def plane(j, w, to):
        return lambda s, d: pltpu.make_async_remote_copy(
            src_ref=s, dst_ref=d, send_sem=ssem.at[j, w], recv_sem=rsem.at[j, w], device_id=to,
            device_id_type=MESH)

    def passed(j, w):
        return lambda s, d: pltpu.make_async_remote_copy(
            src_ref=s, dst_ref=d, send_sem=fssem.at[j, w], recv_sem=frsem.at[j, w],
            device_id=(x, y, 1 - c), device_id_type=MESH)

    @pl.when(c == layer)
    def _():
        for j, (px, py) in enumerate(chips):
            for w in range(nw):
                start(plane(j, w, (px, py, c)), srcs[w], outs[w].at[kme])
        for j, (px, py) in enumerate(chips):
            for w in range(nw):
                got = outs[w].at[2 * px + py]
                plane(j, w, (px, py, c))(got, got).wait_recv()
                start(passed(j, w), got, got)
        for j, (px, py) in enumerate(chips):
            for w in range(nw):
                got = outs[w].at[2 * px + py]
                plane(j, w, (px, py, c))(got, got).wait_send()
                passed(j, w)(got, got).wait_send()

    @pl.when(c != layer)
    def _():
        for j, (px, py) in enumerate(chips):
            for w in range(nw):
                got = outs[w].at[2 * px + py]
                passed(j, w)(got, got).wait_recv()


def _handshake_all():
    x, y, c, _ = _place()
    barrier = pltpu.get_barrier_semaphore()
    for r in range(1, 8):
        peer = (x ^ (r >> 2), y ^ ((r >> 1) & 1), c ^ (r & 1))
        pl.semaphore_signal(barrier, inc=1, device_id=peer, device_id_type=MESH)
    pl.semaphore_wait(barrier, 7)


def _gather_layer_async(blocks, layer, name, collective_id):
    hbm = pltpu.MemorySpace.HBM
    srcs = [jax.new_ref(b, memory_space=hbm) for b in blocks]
    outs = [jax.empty_ref(jax.ShapeDtypeStruct((N_CHIPS,) + b.shape, b.dtype), memory_space=hbm) for b in blocks]

    @pl.kernel(mesh=plsc.ScalarSubcoreMesh(axis_name="seq", num_cores=1), name=name,
               scratch_types=[pltpu.SemaphoreType.DMA((3, len(blocks)))] * 4,
               compiler_params=pltpu.CompilerParams(collective_id=collective_id))
    def launch(*sems):
        _handshake_all()
        _gather_body(srcs, outs, sems, layer, lambda make, s, d: make(s, d).start())

    launch()
    return [o[...] for o in outs]


def _swap_halves(gs):
    nw = len(gs)

    def body(*refs):
        srcs, theirs = refs[:nw], refs[nw:2 * nw]
        ssem, rsem = refs[2 * nw:]
        x, y, c, _ = _place()

        def give(w):
            return lambda s, d: pltpu.make_async_remote_copy(
                src_ref=s, dst_ref=d, send_sem=ssem.at[w], recv_sem=rsem.at[w], device_id=(x, y, 1 - c),
                device_id_type=MESH)

        for w in range(nw):
            hr = theirs[w].shape[1]
            _start_pieces(give(w), srcs[w].at[:, pl.ds((1 - c) * hr, hr)], theirs[w])
        for w in range(nw):
            give(w)(theirs[w], theirs[w]).wait()

    return pl.pallas_call(
        body, name="swap_halves", in_specs=[ANY] * nw, out_specs=[ANY] * nw,
        out_shape=[jax.ShapeDtypeStruct((g.shape[0], g.shape[1] // 2, g.shape[2]), g.dtype) for g in gs],
        scratch_shapes=[pltpu.SemaphoreType.DMA((nw,))] * 2,
        compiler_params=_cparams(has_side_effects=True),
    )(*gs)


def _scatter_chips(ps):
    nw = len(ps)

    def body(*refs):
        _scatter_body(refs[:nw], refs[nw:2 * nw], refs[2 * nw:], _start_pieces)

    return pl.pallas_call(
        body, name="scatter_chips", in_specs=[ANY] * nw, out_specs=[ANY] * nw,
        out_shape=[jax.ShapeDtypeStruct(p.shape, p.dtype) for p in ps],
        scratch_shapes=[pltpu.SemaphoreType.DMA((3, nw)), pltpu.SemaphoreType.DMA((3, nw))],
        compiler_params=_cparams(has_side_effects=True),
    )(*ps)


def _scatter_body(srcs, outs, sems, start):
    nw = len(srcs)
    ssem, rsem = sems
    x, y, c, chips = _place()
    kme = 2 * x + y

    def give(j, w, to):
        return lambda s, d: pltpu.make_async_remote_copy(
            src_ref=s, dst_ref=d, send_sem=ssem.at[j, w], recv_sem=rsem.at[j, w], device_id=to,
            device_id_type=MESH)

    for j, (px, py) in enumerate(chips):
        for w in range(nw):
            start(give(j, w, (px, py, c)), srcs[w].at[2 * px + py], outs[w].at[kme])
    for j, (px, py) in enumerate(chips):
        for w in range(nw):
            got = outs[w].at[2 * px + py]
            give(j, w, (px, py, c))(got, got).wait_recv()
    for j, (px, py) in enumerate(chips):
        for w in range(nw):
            sent = srcs[w].at[2 * px + py]
            give(j, w, (px, py, c))(sent, sent).wait_send()


def _scatter_chips_async(ps, name, collective_id):
    hbm = pltpu.MemorySpace.HBM
    srcs = [jax.new_ref(p, memory_space=hbm) for p in ps]
    outs = [jax.empty_ref(jax.ShapeDtypeStruct(p.shape, p.dtype), memory_space=hbm) for p in ps]

    @pl.kernel(mesh=plsc.ScalarSubcoreMesh(axis_name="seq", num_cores=1), name=name,
               scratch_types=[pltpu.SemaphoreType.DMA((3, len(ps)))] * 2,
               compiler_params=pltpu.CompilerParams(collective_id=collective_id))
    def launch(*sems):
        _handshake_all()
        _scatter_body(srcs, outs, sems, lambda make, s, d: make(s, d).start())

    launch()
    return [o[...] for o in outs]


def _swap_siblings(rs):
    nw = len(rs)

    def body(*refs):
        srcs, outs = refs[:nw], refs[nw:2 * nw]
        ssem, rsem = refs[2 * nw:]
        x, y, c, _ = _place()

        def give(w):
            return lambda s, d: pltpu.make_async_remote_copy(
                src_ref=s, dst_ref=d, send_sem=ssem.at[w], recv_sem=rsem.at[w], device_id=(x, y, 1 - c),
                device_id_type=MESH)

        for w in range(nw):
            _start_pieces(give(w), srcs[w], outs[w])
        for w in range(nw):
            give(w)(srcs[w], outs[w]).wait()

    return pl.pallas_call(
        body, name="swap_siblings", in_specs=[ANY] * nw, out_specs=[ANY] * nw,
        out_shape=[jax.ShapeDtypeStruct(r.shape, r.dtype) for r in rs],
        scratch_shapes=[pltpu.SemaphoreType.DMA((nw,))] * 2,
        compiler_params=_cparams(has_side_effects=True),
    )(*rs)


def _allreduce_small(v):
    rows = v.shape[0]

    def body(v_ref, o_ref, buf, ssem, rsem):
        x, y, c, _ = _place()
        me = 4 * x + 2 * y + c
        buf[me] = v_ref[...]
        sends = []
        for r in range(1, 8):
            peer = (x ^ (r >> 2), y ^ ((r >> 1) & 1), c ^ (r & 1))
            cp = pltpu.make_async_remote_copy(
                src_ref=v_ref, dst_ref=buf.at[me], send_sem=ssem.at[r - 1], recv_sem=rsem.at[r - 1],
                device_id=peer, device_id_type=MESH)
            cp.start()
            sends.append(cp)
        for r in range(1, 8):
            src = me ^ r
            pltpu.make_async_remote_copy(
                src_ref=v_ref, dst_ref=buf.at[src], send_sem=ssem.at[r - 1], recv_sem=rsem.at[r - 1],
                device_id=(x, y, c), device_id_type=MESH).wait_recv()
        for cp in sends:
            cp.wait_send()
        acc = buf[0]
        for d in range(1, 8):
            acc = acc + buf[d]
        o_ref[...] = acc

    vm = pl.BlockSpec(memory_space=pltpu.VMEM)
    return pl.pallas_call(
        body, name="allreduce_small", in_specs=[vm], out_specs=vm,
        out_shape=jax.ShapeDtypeStruct(v.shape, F32),
        scratch_shapes=[pltpu.VMEM((8, rows, 128), F32), pltpu.SemaphoreType.DMA((7,)),
                        pltpu.SemaphoreType.DMA((7,))],
        compiler_params=_cparams(has_side_effects=True),
    )(v)


def _pad_heads(w, n_heads, axis):
    shp = w.shape
    w = w.reshape(shp[:axis] + (n_heads, HD) + shp[axis + 1:])
    pad = [(0, 0)] * w.ndim
    pad[axis + 1] = (0, HP - HD)
    w = jnp.pad(w, pad)
    return w.reshape(shp[:axis] + (n_heads * HP,) + shp[axis + 1:])


def _strip_heads(w, n_heads, axis):
    shp = w.shape
    w = w.reshape(shp[:axis] + (n_heads, HP) + shp[axis + 1:])
    w = lax.slice_in_dim(w, 0, HD, axis=axis + 1)
    return w.reshape(shp[:axis] + (n_heads * HD,) + shp[axis + 1:])


def _pad_win_t(wint):
    parts = [wint[:3 * CC], _pad_heads(wint[3 * CC:3 * CC + NQ * HD], NQ, 0),
             _pad_heads(wint[3 * CC + NQ * HD:3 * CC + (NQ + NKV) * HD], NKV, 0),
             _pad_heads(wint[3 * CC + (NQ + NKV) * HD:], NKV, 0)]
    return jnp.concatenate(parts, axis=0)


def _strip_win_t(gpt):
    parts = [gpt[:3 * CC], _strip_heads(gpt[O_Q:O_K], NQ, 0), _strip_heads(gpt[O_K:O_V], NKV, 0),
             _strip_heads(gpt[O_V:], NKV, 0)]
    return jnp.concatenate(parts, axis=0)


def _t(w):
    return jnp.swapaxes(w, -1, -2)


def _count(shape):
    n = 1
    for s in shape:
        n *= s
    return n


def _pack_rows(arrs):
    flat = [jnp.pad(a.reshape(-1), (0, (-_count(a.shape)) % 128)) for a in arrs]
    v = jnp.concatenate(flat)
    rows = -(-v.shape[0] // (8 * 128)) * 8
    return jnp.pad(v, (0, rows * 128 - v.shape[0])).reshape(rows, 128)


def kernel(x, norm1_g, w_in, conv_w, q_norm_g, k_norm_g, sinks, conv_out_g, attn_out_g, w_o, norm2_g, w_gate, w_up, w_down, loss_target, m_norm1_g, m_w_in, m_conv_w, m_q_norm_g, m_k_norm_g, m_sinks, m_conv_out_g, m_attn_out_g, m_w_o, m_norm2_g, m_w_gate, m_w_up, m_w_down, v_norm1_g, v_w_in, v_conv_w, v_q_norm_g, v_k_norm_g, v_sinks, v_conv_out_g, v_attn_out_g, v_w_o, v_norm2_g, v_w_gate, v_w_up, v_w_down):
    depth = w_in.shape[0]
    t = x.shape[1]
    xs = x.reshape(t, D)
    tgt = loss_target.reshape(t, D)
    xi, yi = lax.axis_index("x"), lax.axis_index("y")
    kme = 2 * xi + yi
    tm = min(512, t)
    tq = min(256, t)
    tf = min(256, t)

    cwp = jnp.pad(conv_w.reshape(depth * 3, CC // N_CHIPS), ((0, 8 - depth * 3), (0, 0)))
    own_f = [jnp.concatenate([_t(w_gate[l]), _t(w_up[l]), w_down[l]], axis=0).astype(BF16) for l in range(depth)]
    own_o = [w_o[l].astype(BF16) for l in range(depth)]
    own_i = [_t(w_in[l]).astype(BF16) for l in range(depth)]
    mine = lambda got, own: lax.dynamic_update_index_in_dim(got, own, kme, 0)
    got_i0, got_o0, got_cw = _gather_layer([own_i[0], own_o[0], cwp], 0)
    gf0_in = lax.optimization_barrier((own_f[0], got_i0))[0]
    (got_f0,) = _gather_layer_async([gf0_in], 0, "gather_ffn0_seq", collective_id=6)
    cw_full = mine(got_cw, cwp).transpose(1, 0, 2).reshape(8, CC)[:depth * 3].reshape(depth, 3, CC)

    def layer_params(l, got_i, got_o):
        wo = mine(got_o, own_o[l]).reshape(D, D)
        return dict(
            wpt=_pad_win_t(mine(got_i, own_i[l]).reshape(N_CHIPS * 576, D)),
            wo=jnp.concatenate([wo[:CC], _pad_heads(wo[CC:], NQ, 0)], axis=0),
            cw=jnp.pad(cw_full[l], ((0, 5), (0, 0))),
            g1=norm1_g[l].reshape(1, D), g2=norm2_g[l].reshape(1, D),
            gq=jnp.pad(q_norm_g[l], (0, HP - HD)).reshape(1, HP), gk=jnp.pad(k_norm_g[l], (0, HP - HD)).reshape(1, HP),
            sk=sinks[l].reshape(1, NQ), gco=conv_out_g[l].reshape(1, CC),
            gao=_pad_heads(attn_out_g[l], NQ, 0).reshape(1, NQ * HP))

    saved, layers = [], []
    cur = xs
    for l in range(depth):
        if l == 0:
            p = layer_params(0, got_i0, got_o0)
        else:
            got_f1, got_o1, got_i1 = lax.optimization_barrier((got_l1, cur))[0]
            p = layer_params(1, got_i1, got_o1)
        proj, h = _inproj_fwd(cur, p["g1"], p["wpt"], tm)
        xm, mix = _mixer_fwd(proj, cur, p["cw"], p["gq"], p["gk"], p["sk"], p["gco"], p["gao"], p["wo"], tq)
        if l == 0:
            got_f0 = lax.optimization_barrier((got_f0, xm))[0]
            l1_in = lax.optimization_barrier(([own_f[1], own_o[1], own_i[1]], got_f0))[0]
            got_l1 = _gather_layer_async(l1_in, 1, "gather_layer1_seq", collective_id=1)
        p["gf"] = mine(got_f0 if l == 0 else got_f1, own_f[l])
        layers.append(p)
        xo, a, b, h2 = _ffn_fwd(xm, p["g2"], p["gf"], tf)
        saved.append(dict(x=cur, proj=proj, h=h, xm=xm, mix=mix, a=a, b=b, h2=h2))
        cur = xo
    lpart, dy = _loss_and_grad(cur, tgt, tm)
    loss = lax.psum(lpart[0, 0], ("x", "y", "c"))

    nt = t // tq
    ci = lax.axis_index("c")
    core = ci.reshape(1).astype(jnp.int32)
    rbig = [dict() for _ in range(depth)]
    gsmall = [None] * depth

    def reduce_start(gs, name, collective_id):
        ps = _presum_halves(gs, _swap_halves(gs), core)
        got = _scatter_chips(ps) if collective_id is None else _scatter_chips_async(ps, name, collective_id)
        return ps, got

    def reduce_finish(started, after):
        ps, got = started
        if after is not None:
            got = lax.optimization_barrier((got, after))[0]
        cs = [lax.dynamic_update_index_in_dim(g, lax.dynamic_index_in_dim(q, kme, 0, keepdims=False), kme, 0)
              for g, q in zip(got, ps)]
        r_mine = _sum_chips(cs, "chipsum")
        return [jnp.where(ci == 0, jnp.concatenate([a, b], axis=0), jnp.concatenate([b, a], axis=0))
                for a, b in zip(r_mine, _swap_siblings(r_mine))]

    in_flight = None
    for l in reversed(range(depth)):
        p, s = layers[l], saved[l]
        dxm, da, db, hm, dg2 = _ffn_bwd(dy, s["xm"], p["g2"], s["a"], s["b"], p["gf"], tf)
        g_wg = _wgrad_blocks(da, s["h2"], tm, "wgrad_gate")
        g_wu = _wgrad_blocks(db, s["h2"], tm, "wgrad_up")
        g_wd = _wgrad_blocks(hm, dy, tm, "wgrad_down")
        if in_flight is not None:
            rbig[l + 1]["in"], rbig[l + 1]["o"] = reduce_finish(in_flight, g_wd)
        ffn_flight = reduce_start([g_wg, g_wu, g_wd], f"scatter_ffn{l}_seq", 2 + 2 * l)
        dpm, dkvm, dkvh, dcw, dgq, dgk, dsk, dgco, dgao = _mixer_bwd(
            dxm, s["proj"], p["cw"], p["gq"], p["gk"], p["sk"], p["gco"], p["gao"], p["wo"], tq)
        g_wo = _wgrad(s["mix"], dxm, D, tm, "wgrad_o")
        kvw = dkvm.shape[1]
        halo = jnp.concatenate([dkvh.reshape(nt, BLK, kvw)[1:], jnp.zeros((1, BLK, kvw), F32)], axis=0)
        halo = jnp.pad(halo, ((0, 0), (tq - BLK, 0), (0, 0)))
        dkv = (dkvm.reshape(nt, tq, kvw) + halo).reshape(t, kvw).astype(BF16)
        dx, dg1 = _inproj_bwd(dpm, dkv, p["wpt"], s["x"], p["g1"], dxm, tm)
        g_wpm = _wgrad(dpm, s["h"], D, tm, "wgrad_in_main")
        g_wpk = _wgrad(dkv, s["h"], D, tm, "wgrad_in_kv")
        dy = dx
        g_in = _strip_win_t(jnp.concatenate([g_wpm, g_wpk], axis=0)).astype(BF16)
        g_o = jnp.concatenate([g_wo[:CC], _strip_heads(g_wo[CC:], NQ, 0)], axis=0).astype(BF16)
        gsmall[l] = dict(g1=dg1, cw=dcw[:3], gq=dgq[0, :HD], gk=dgk[0, :HD], sk=dsk[0, :NQ], gco=dgco,
                         gao=_strip_heads(dgao.reshape(NQ * HP), NQ, 0), g2=dg2)
        rbig[l]["g"], rbig[l]["u"], rbig[l]["d"] = reduce_finish(ffn_flight, dx)
        in_flight = reduce_start([g_in.reshape(N_CHIPS, -1, D), g_o.reshape(N_CHIPS, -1, D)],
                                 f"scatter_in{l}_seq", 3 + 2 * l if l > 0 else None)
    rbig[0]["in"], rbig[0]["o"] = reduce_finish(in_flight, None)
    grad_x = dy.reshape(x.shape)
    g_big = [jnp.stack([rbig[l][w] for l in range(depth)]) for w in ("in", "o", "g", "u", "d")]

    small_shapes = dict(g1=(D,), cw=(3, CC), gq=(HD,), gk=(HD,), sk=(NQ,), gco=(CC,), gao=(NQ * HD,), g2=(D,))
    red = _allreduce_small(_pack_rows([gsmall[l][n] for l in range(depth) for n in small_shapes])).reshape(-1)
    red_small, offs = {n: [] for n in small_shapes}, 0
    for l in range(depth):
        for n, shp in small_shapes.items():
            cnt = _count(shp)
            red_small[n].append(red[offs:offs + cnt].reshape(shp))
            offs += -(-cnt // 128) * 128
    g_small = {n: jnp.stack(v) for n, v in red_small.items()}
    g_cw = lax.dynamic_slice_in_dim(g_small["cw"], kme * (CC // N_CHIPS), CC // N_CHIPS, axis=2)

    weights = [norm1_g, w_in, conv_w, q_norm_g, k_norm_g, sinks, conv_out_g, attn_out_g, w_o, norm2_g, w_gate,
               w_up, w_down]
    moms = [m_norm1_g, m_w_in, m_conv_w, m_q_norm_g, m_k_norm_g, m_sinks, m_conv_out_g, m_attn_out_g, m_w_o,
            m_norm2_g, m_w_gate, m_w_up, m_w_down]
    vars_ = [v_norm1_g, v_w_in, v_conv_w, v_q_norm_g, v_k_norm_g, v_sinks, v_conv_out_g, v_attn_out_g, v_w_o,
             v_norm2_g, v_w_gate, v_w_up, v_w_down]
    grads = [g_small["g1"], _t(g_big[0]), g_cw, g_small["gq"], g_small["gk"], g_small["sk"], g_small["gco"],
             g_small["gao"], g_big[1], g_small["g2"], _t(g_big[2]), _t(g_big[3]), g_big[4]]
    n_w = len(weights)
    big_idx = [1, 8, 10, 11, 12]
    small_idx = [n for n in range(n_w) if n not in big_idx]
    deltas, new_m, new_v = [None] * n_w, [None] * n_w, [None] * n_w
    for n, g in zip(big_idx, g_big):
        flip = g.shape != weights[n].shape
        rows2d = lambda a3: (_t(a3) if flip else a3).reshape(-1, D)
        res = _adamw(rows2d(weights[n]), g.reshape(-1, D), rows2d(moms[n]), rows2d(vars_[n]), f"adamw_{n}")
        res = [r.reshape(g.shape) for r in res]
        deltas[n], new_m[n], new_v[n] = [_t(r) for r in res] if flip else res
    res = _adamw(*[_pack_rows([arrs[n] for n in small_idx]) for arrs in (weights, grads, moms, vars_)],
                 "adamw_small")
    offs = 0
    for n in small_idx:
        shp = weights[n].shape
        cnt = _count(shp)
        deltas[n], new_m[n], new_v[n] = [r.reshape(-1)[offs:offs + cnt].reshape(shp) for r in res]
        offs += -(-cnt // 128) * 128
    return (loss, grad_x, *grads, *deltas, *new_m, *new_v)
```

```python
import functools

import jax
import jax.numpy as jnp
from jax import lax
from jax.experimental import pallas as pl
from jax.experimental.pallas import tpu as pltpu
from jax.experimental.pallas import tpu_sc as plsc

F32 = jnp.float32
BF16 = jnp.bfloat16

D = 1024
CC = 512
NQ = 8
NKV = 2
HD = 64
HP = 128
GRP = NQ // NKV
FF = 2816
FFB = FF // 4
BLK = 128
EPS = 1e-6
NEG = -1e30
SCALE = HD ** -0.5
O_BG, O_CG, O_HC, O_Q = 0, CC, 2 * CC, 3 * CC
O_K = O_Q + NQ * HP
O_V = O_K + NKV * HP
NP = O_V + NKV * HP
NMAIN = O_K
MIXW = CC + NQ * HP
N_CHIPS = 4
VMEM_LIMIT = 56 * 1024 * 1024
MESH = pl.DeviceIdType.MESH

ADAM_LR, ADAM_B1, ADAM_B2, ADAM_EPS, ADAM_WD, ADAM_STEP = 0.001, 0.9, 0.999, 1e-08, 0.01, 10


def _cparams(sem=None, **kw):
    if sem is not None:
        kw["dimension_semantics"] = sem
    return pltpu.CompilerParams(vmem_limit_bytes=VMEM_LIMIT, **kw)


def _const_spec(shape):
    nd = len(shape)
    return pl.BlockSpec(shape, lambda *_: (0,) * nd, pipeline_mode=pl.Buffered(1))


def _nt(a, b):
    return lax.dot_general(a, b, (((1,), (1,)), ((), ())), preferred_element_type=F32)


def _tn(a, b):
    return lax.dot_general(a, b, (((0,), (0,)), ((), ())), preferred_element_type=F32)


def _rms_fwd(x, inv_n):
    r = lax.rsqrt(jnp.sum(x * x, axis=-1, keepdims=True) * inv_n + EPS)
    return r, x * r


def _rms_bwd(dy, g, xh, r, inv_n):
    dxh = dy * g
    return r * (dxh - xh * (jnp.sum(dxh * xh, axis=-1, keepdims=True) * inv_n))


def _inproj_fwd(x, g1, wpt, tm):
    t = x.shape[0]

    def body(x_ref, g_ref, w_ref, p_ref, h_ref):
        _, xh = _rms_fwd(x_ref[...], 1.0 / D)
        h = (xh * g_ref[...]).astype(BF16)
        h_ref[...] = h
        p_ref[...] = _nt(h, w_ref[...])

    return pl.pallas_call(
        body, name="inproj_fwd", grid=(t // tm,),
        in_specs=[pl.BlockSpec((tm, D), lambda i: (i, 0)), _const_spec((1, D)), _const_spec((NP, D))],
        out_specs=[pl.BlockSpec((tm, NP), lambda i: (i, 0)), pl.BlockSpec((tm, D), lambda i: (i, 0))],
        out_shape=[jax.ShapeDtypeStruct((t, NP), F32), jax.ShapeDtypeStruct((t, D), BF16)],
        compiler_params=_cparams(("parallel",)),
    )(x, g1, wpt)


def _band_mask():
    r_io = lax.broadcasted_iota(jnp.int32, (BLK, 2 * BLK), 0)
    c_io = lax.broadcasted_iota(jnp.int32, (BLK, 2 * BLK), 1)
    return (c_io > r_io) & (c_io <= r_io + BLK), c_io


def _conv_taps(uf, n):
    u1 = pltpu.roll(uf, 1, 0)[8:8 + n]
    u2 = pltpu.roll(uf, 2, 0)[8:8 + n]
    return u1, u2


def _attn_probs(qn, kband, sink, valid):
    s = _nt(qn, kband) * SCALE
    s = jnp.where(valid, s, NEG)
    m = jnp.maximum(jnp.max(s, axis=-1, keepdims=True), sink)
    p = jnp.exp(s - m)
    es = jnp.exp(sink - m)
    inv = 1.0 / (jnp.sum(p, axis=-1, keepdims=True) + es)
    return p * inv, es * inv


def _norm_keys(kraw, gk):
    out = []
    for h in range(NKV):
        kh = kraw[:, h * HP:(h + 1) * HP]
        rk, khat = _rms_fwd(kh, 1.0 / HD)
        out.append((khat, rk, (khat * gk).astype(BF16)))
    return out


def _mixer_fwd(proj, x, cw, gq, gk, sinks, gco, gao, wo, tq):
    t = proj.shape[0]
    nb = tq // BLK
    r8 = tq // 8

    def body(p_ref, cgp_ref, hcp_ref, kvp_ref, x_ref, cw_ref, gq_ref, gk_ref, sk_ref, gco_ref, gao_ref,
             wo_ref, xm_ref, mix_ref, ao_ref):
        i = pl.program_id(0)
        cg = p_ref[:, O_CG:O_CG + CC]
        hc = p_ref[:, O_HC:O_HC + CC]
        u = cg * hc
        up = jnp.where(i > 0, cgp_ref[...] * hcp_ref[...], 0.0)
        u1, u2 = _conv_taps(jnp.concatenate([up, u], axis=0), tq)
        y = cw_ref[0:1, :] * u2 + cw_ref[1:2, :] * u1 + cw_ref[2:3, :] * u
        co = p_ref[:, O_BG:O_BG + CC] * y
        _, coh = _rms_fwd(co, 1.0 / CC)
        cn = coh * gco_ref[...]
        kraw = jnp.concatenate([kvp_ref[:, 0:NKV * HP], p_ref[:, O_K:O_K + NKV * HP]], axis=0)
        vraw = jnp.concatenate([kvp_ref[:, NKV * HP:], p_ref[:, O_V:O_V + NKV * HP]], axis=0)
        keys = _norm_keys(kraw, gk_ref[...])
        vb = [vraw[:, h * HP:(h + 1) * HP].astype(BF16) for h in range(NKV)]
        base_valid, c_io = _band_mask()
        for b in range(nb):
            lo = jnp.where(i * nb + b == 0, BLK, 0)
            valid = base_valid & (c_io >= lo)
            for g in range(NQ):
                h = g // GRP
                qg = p_ref[b * BLK:(b + 1) * BLK, O_Q + g * HP:O_Q + (g + 1) * HP]
                _, qh = _rms_fwd(qg, 1.0 / HD)
                qn = (qh * gq_ref[...]).astype(BF16)
                pr, _ = _attn_probs(qn, keys[h][2][b * BLK:b * BLK + 2 * BLK], sk_ref[0, g], valid)
                ao_ref[b * BLK:(b + 1) * BLK, g * HP:(g + 1) * HP] = jnp.dot(
                    pr.astype(BF16), vb[h][b * BLK:b * BLK + 2 * BLK], preferred_element_type=F32)
        _, aoh = _rms_fwd(ao_ref[...], 1.0 / (NQ * HD))
        an = aoh * gao_ref[...]
        mix = jnp.concatenate([cn, an], axis=1).astype(BF16)
        mix_ref[...] = mix
        xm_ref[...] = x_ref[...] + jnp.dot(mix, wo_ref[...], preferred_element_type=F32)

    prev8 = lambda col: pl.BlockSpec((8, CC), lambda i: (jnp.maximum(i * r8 - 1, 0), col))
    return pl.pallas_call(
        body, name="mixer_fwd", grid=(t // tq,),
        in_specs=[
            pl.BlockSpec((tq, NP), lambda i: (i, 0)),
            prev8(O_CG // CC), prev8(O_HC // CC),
            pl.BlockSpec((BLK, 2 * NKV * HP), lambda i: (jnp.maximum(i * nb - 1, 0), O_K // (2 * NKV * HP))),
            pl.BlockSpec((tq, D), lambda i: (i, 0)),
            _const_spec((8, CC)), _const_spec((1, HP)), _const_spec((1, HP)),
            pl.BlockSpec(memory_space=pltpu.SMEM),
            _const_spec((1, CC)), _const_spec((1, NQ * HP)), _const_spec((MIXW, D)),
        ],
        out_specs=[pl.BlockSpec((tq, D), lambda i: (i, 0)), pl.BlockSpec((tq, MIXW), lambda i: (i, 0)),
                   pl.BlockSpec((tq, NQ * HP), lambda i: (i, 0))],
        out_shape=[jax.ShapeDtypeStruct((t, D), F32), jax.ShapeDtypeStruct((t, MIXW), BF16),
                   jax.ShapeDtypeStruct((t, NQ * HP), F32)],
        compiler_params=_cparams(("parallel",)),
    )(proj, proj, proj, proj, x, cw, gq, gk, sinks, gco, gao, wo)


def _ffn_weight_specs():
    return [pl.BlockSpec((N_CHIPS, FFB, D), lambda i, j=j: (0, j, 0), pipeline_mode=pl.Buffered(1))
            for j in range(3)]


def _ffn_fwd(xm, g2, gf, tm):
    t = xm.shape[0]

    def body(x_ref, g_ref, wg_ref, wu_ref, wd_ref, xo_ref, a_ref, b_ref, h2_ref):
        xv = x_ref[...]
        _, xh = _rms_fwd(xv, 1.0 / D)
        h2 = (xh * g_ref[...]).astype(BF16)
        h2_ref[...] = h2
        acc = xv
        for k in range(N_CHIPS):
            a = _nt(h2, wg_ref[k])
            b = _nt(h2, wu_ref[k])
            a_ref[k] = a.astype(BF16)
            b_ref[k] = b.astype(BF16)
            hm = (a * jax.nn.sigmoid(a) * b).astype(BF16)
            acc = acc + jnp.dot(hm, wd_ref[k], preferred_element_type=F32)
        xo_ref[...] = acc

    row = lambda w: pl.BlockSpec((tm, w), lambda i: (i, 0))
    blk = pl.BlockSpec((N_CHIPS, tm, FFB), lambda i: (0, i, 0))
    return pl.pallas_call(
        body, name="ffn_fwd", grid=(t // tm,),
        in_specs=[row(D), _const_spec((1, D))] + _ffn_weight_specs(),
        out_specs=[row(D), blk, blk, row(D)],
        out_shape=[jax.ShapeDtypeStruct((t, D), F32), jax.ShapeDtypeStruct((N_CHIPS, t, FFB), BF16),
                   jax.ShapeDtypeStruct((N_CHIPS, t, FFB), BF16), jax.ShapeDtypeStruct((t, D), BF16)],
        compiler_params=_cparams(("parallel",)),
    )(xm, g2, gf, gf, gf)


def _loss_and_grad(y, tgt, tm):
    t = y.shape[0]

    def body(y_ref, t_ref, l_ref, dy_ref):
        @pl.when(pl.program_id(0) == 0)
        def _():
            l_ref[...] = jnp.zeros_like(l_ref)

        e = y_ref[...] - t_ref[...]
        dy_ref[...] = e * (1.0 / D)
        s = jnp.sum(jnp.sum(e * e, axis=-1, keepdims=True), axis=0, keepdims=True)
        l_ref[...] += s * (0.5 / D)

    row = pl.BlockSpec((tm, D), lambda i: (i, 0))
    return pl.pallas_call(
        body, name="loss", grid=(t // tm,), in_specs=[row, row],
        out_specs=[pl.BlockSpec((8, 128), lambda i: (0, 0)), row],
        out_shape=[jax.ShapeDtypeStruct((8, 128), F32), jax.ShapeDtypeStruct((t, D), F32)],
        compiler_params=_cparams(("arbitrary",)),
    )(y, tgt)


def _ffn_bwd(dy, xm, g2, a, b, gf, tm):
    t = dy.shape[0]

    def body(dy_ref, x_ref, g_ref, a_ref, b_ref, wg_ref, wu_ref, wd_ref, dx_ref, da_ref, db_ref, hm_ref, dg_ref):
        @pl.when(pl.program_id(0) == 0)
        def _():
            dg_ref[...] = jnp.zeros_like(dg_ref)

        dyv = dy_ref[...]
        dyb = dyv.astype(BF16)
        dh2 = jnp.zeros_like(dyv)
        for k in range(N_CHIPS):
            dhm = _nt(dyb, wd_ref[k])
            av = a_ref[k].astype(F32)
            bv = b_ref[k].astype(F32)
            sig = jax.nn.sigmoid(av)
            sil = av * sig
            hm_ref[k] = (sil * bv).astype(BF16)
            da = (dhm * bv * (sig * (1.0 + av * (1.0 - sig)))).astype(BF16)
            db = (dhm * sil).astype(BF16)
            da_ref[k] = da
            db_ref[k] = db
            dh2 = (dh2 + jnp.dot(da, wg_ref[k], preferred_element_type=F32)
                   + jnp.dot(db, wu_ref[k], preferred_element_type=F32))
        r, xh = _rms_fwd(x_ref[...], 1.0 / D)
        dg_ref[...] += jnp.sum(dh2 * xh, axis=0, keepdims=True)
        dx_ref[...] = dyv + _rms_bwd(dh2, g_ref[...], xh, r, 1.0 / D)

    row = lambda w: pl.BlockSpec((tm, w), lambda i: (i, 0))
    blk = pl.BlockSpec((N_CHIPS, tm, FFB), lambda i: (0, i, 0))
    bsd = jax.ShapeDtypeStruct((N_CHIPS, t, FFB), BF16)
    return pl.pallas_call(
        body, name="ffn_bwd", grid=(t // tm,),
        in_specs=[row(D), row(D), _const_spec((1, D)), blk, blk] + _ffn_weight_specs(),
        out_specs=[row(D), blk, blk, blk, pl.BlockSpec((1, D), lambda i: (0, 0))],
        out_shape=[jax.ShapeDtypeStruct((t, D), F32), bsd, bsd, bsd, jax.ShapeDtypeStruct((1, D), F32)],
        compiler_params=_cparams(("arbitrary",)),
    )(dy, xm, g2, a, b, gf, gf, gf)


def _wgrad_blocks(a, b, tt, name):
    _, t, rows = a.shape
    cols = b.shape[1]
    nsteps = t // tt

    def body(a_ref, b_ref, o_ref, acc_ref):
        s = pl.program_id(0)

        @pl.when(s == 0)
        def _():
            acc_ref[...] = jnp.zeros_like(acc_ref)

        bv = b_ref[...].astype(BF16)
        for k in range(N_CHIPS):
            acc_ref[k] += _tn(a_ref[k], bv)

        @pl.when(s == nsteps - 1)
        def _():
            o_ref[...] = acc_ref[...].astype(BF16)

    return pl.pallas_call(
        body, name=name, grid=(nsteps,),
        in_specs=[pl.BlockSpec((N_CHIPS, tt, rows), lambda s: (0, s, 0)), pl.BlockSpec((tt, cols), lambda s: (s, 0))],
        out_specs=pl.BlockSpec((N_CHIPS, rows, cols), lambda s: (0, 0, 0)),
        out_shape=jax.ShapeDtypeStruct((N_CHIPS, rows, cols), BF16),
        scratch_shapes=[pltpu.VMEM((N_CHIPS, rows, cols), F32)],
        compiler_params=_cparams(("arbitrary",)),
    )(a, b)


def _wgrad(a, b, tn, tt, name):
    t, k = a.shape
    n = b.shape[1]
    nsteps = t // tt

    def body(a_ref, b_ref, o_ref):
        @pl.when(pl.program_id(1) == 0)
        def _():
            o_ref[...] = jnp.zeros_like(o_ref)

        o_ref[...] += _tn(a_ref[...].astype(BF16), b_ref[...].astype(BF16))

    return pl.pallas_call(
        body, name=name, grid=(n // tn, nsteps),
        in_specs=[pl.BlockSpec((tt, k), lambda j, s: (s, 0)), pl.BlockSpec((tt, tn), lambda j, s: (s, j))],
        out_specs=pl.BlockSpec((k, tn), lambda j, s: (0, j)),
        out_shape=jax.ShapeDtypeStruct((k, n), F32),
        compiler_params=_cparams(("parallel", "arbitrary")),
    )(a, b)


def _mixer_bwd(dxm, proj, ao, cw, gq, gk, sinks, gco, gao, wo, tq):
    t = proj.shape[0]
    nb = tq // BLK
    r8 = tq // 8
    nt = t // tq
    te = tq + 8
    kvw = 2 * NKV * HP

    def body(dx_ref, dxn_ref, p_ref, cgp_ref, hcp_ref, bgn_ref, cgn_ref, hcn_ref, kvp_ref, ao_ref, cw_ref, gq_ref,
             gk_ref, sk_ref, gco_ref, gao_ref, wo_ref,
             dpm_ref, dkvm_ref, dkvh_ref, dcw_ref, dgq_ref, dgk_ref, dsk_ref, dgco_ref, dgao_ref, acc_ref):
        i = pl.program_id(0)

        @pl.when(i == 0)
        def _():
            for r in (dcw_ref, dgq_ref, dgk_ref, dsk_ref, dgco_ref, dgao_ref):
                r[...] = jnp.zeros_like(r)

        acc_ref[...] = jnp.zeros_like(acc_ref)
        live_rows = jnp.where(i < nt - 1, te, tq)
        dxb = dx_ref[...].astype(BF16)
        dxe = jnp.concatenate([dxb, dxn_ref[...].astype(BF16)], axis=0)
        dcn = _nt(dxe, wo_ref[0:CC, :])
        bg = jnp.concatenate([p_ref[:, O_BG:O_BG + CC], bgn_ref[...]], axis=0)
        cg = jnp.concatenate([p_ref[:, O_CG:O_CG + CC], cgn_ref[...]], axis=0)
        hc = jnp.concatenate([p_ref[:, O_HC:O_HC + CC], hcn_ref[...]], axis=0)
        u = cg * hc
        up = jnp.where(i > 0, cgp_ref[...] * hcp_ref[...], 0.0)
        u1, u2 = _conv_taps(jnp.concatenate([up, u], axis=0), te)
        w0, w1, w2 = cw_ref[0:1, :], cw_ref[1:2, :], cw_ref[2:3, :]
        y = w0 * u2 + w1 * u1 + w2 * u
        co = bg * y
        rc, coh = _rms_fwd(co, 1.0 / CC)
        dco = _rms_bwd(dcn, gco_ref[...], coh, rc, 1.0 / CC)
        row_io = lax.broadcasted_iota(jnp.int32, (te, 1), 0)
        own = row_io < tq
        dgco_ref[...] += jnp.sum(jnp.where(own, dcn * coh, 0.0), axis=0, keepdims=True)
        dyc = jnp.where(row_io < live_rows, dco * bg, 0.0)
        dyo = jnp.where(own, dyc, 0.0)
        dcw_ref[0:1, :] += jnp.sum(dyo * u2, axis=0, keepdims=True)
        dcw_ref[1:2, :] += jnp.sum(dyo * u1, axis=0, keepdims=True)
        dcw_ref[2:3, :] += jnp.sum(dyo * u, axis=0, keepdims=True)
        dy1 = pltpu.roll(dyc, te - 1, 0)[0:tq]
        dy2 = pltpu.roll(dyc, te - 2, 0)[0:tq]
        du = w2 * dyc[0:tq] + w1 * dy1 + w0 * dy2
        dpm_ref[:, O_BG:O_BG + CC] = (dco[0:tq] * y[0:tq]).astype(BF16)
        dpm_ref[:, O_CG:O_CG + CC] = (du * hc[0:tq]).astype(BF16)
        dpm_ref[:, O_HC:O_HC + CC] = (du * cg[0:tq]).astype(BF16)
        kraw = jnp.concatenate([kvp_ref[:, 0:NKV * HP], p_ref[:, O_K:O_K + NKV * HP]], axis=0)
        vraw = jnp.concatenate([kvp_ref[:, NKV * HP:], p_ref[:, O_V:O_V + NKV * HP]], axis=0)
        gqv, gkv = gq_ref[...], gk_ref[...]
        keys = _norm_keys(kraw, gkv)
        vb = [vraw[:, h * HP:(h + 1) * HP].astype(BF16) for h in range(NKV)]
        base_valid, c_io = _band_mask()
        lane = lax.broadcasted_iota(jnp.int32, (1, HP), 1)
        for b in range(nb):
            lo = jnp.where(i * nb + b == 0, BLK, 0)
            valid = base_valid & (c_io >= lo)
            band = slice(b * BLK, b * BLK + 2 * BLK)
            blk = slice(b * BLK, (b + 1) * BLK)
            ra, aoh = _rms_fwd(ao_ref[blk, :], 1.0 / (NQ * HD))
            danb = _nt(dxb[blk], wo_ref[CC:MIXW, :])
            dgao_ref[...] += jnp.sum(danb * aoh, axis=0, keepdims=True)
            dao = _rms_bwd(danb, gao_ref[...], aoh, ra, 1.0 / (NQ * HD))
            for g in range(NQ):
                h = g // GRP
                khat, rk, kn = [a[band] for a in keys[h]]
                rq, qh = _rms_fwd(p_ref[blk, O_Q + g * HP:O_Q + (g + 1) * HP], 1.0 / HD)
                qn = (qh * gqv).astype(BF16)
                pr, ps = _attn_probs(qn, kn, sk_ref[0, g], valid)
                dob = dao[:, g * HP:(g + 1) * HP].astype(BF16)
                dp = _nt(dob, vb[h][band])
                delta = jnp.sum(pr * dp, axis=-1, keepdims=True)
                dsb = (pr * (dp - delta) * SCALE).astype(BF16)
                dsk = -jnp.sum(ps * delta, axis=0, keepdims=True)
                dsk_ref[...] += jnp.where(lane == g, dsk, 0.0)
                dqn = jnp.dot(dsb, kn, preferred_element_type=F32)
                dgq_ref[...] += jnp.sum(dqn * qh, axis=0, keepdims=True)
                dpm_ref[blk, O_Q + g * HP:O_Q + (g + 1) * HP] = _rms_bwd(dqn, gqv, qh, rq, 1.0 / HD).astype(BF16)
                dkn = _tn(dsb, qn)
                dgk_ref[...] += jnp.sum(dkn * khat, axis=0, keepdims=True)
                acc_ref[band, h * HP:(h + 1) * HP] += _rms_bwd(dkn, gkv, khat, rk, 1.0 / HD)
                acc_ref[band, (NKV + h) * HP:(NKV + h + 1) * HP] += _tn(pr.astype(BF16), dob)
        dkvh_ref[...] = acc_ref[0:BLK, :]
        dkvm_ref[...] = acc_ref[BLK:, :]

    prev8 = lambda col: pl.BlockSpec((8, CC), lambda i: (jnp.maximum(i * r8 - 1, 0), col))
    next8 = lambda col: pl.BlockSpec((8, CC), lambda i: (jnp.minimum((i + 1) * r8, t // 8 - 1), col))
    small = lambda n: pl.BlockSpec((1, n), lambda i: (0, 0))
    return pl.pallas_call(
        body, name="mixer_bwd", grid=(nt,),
        in_specs=[
            pl.BlockSpec((tq, D), lambda i: (i, 0)),
            pl.BlockSpec((8, D), lambda i: (jnp.minimum((i + 1) * r8, t // 8 - 1), 0)),
            pl.BlockSpec((tq, NP), lambda i: (i, 0)),
            prev8(O_CG // CC), prev8(O_HC // CC),
            next8(O_BG // CC), next8(O_CG // CC), next8(O_HC // CC),
            pl.BlockSpec((BLK, kvw), lambda i: (jnp.maximum(i * nb - 1, 0), O_K // kvw)),
            pl.BlockSpec((tq, NQ * HP), lambda i: (i, 0)),
            _const_spec((8, CC)), _const_spec((1, HP)), _const_spec((1, HP)),
            pl.BlockSpec(memory_space=pltpu.SMEM),
            _const_spec((1, CC)), _const_spec((1, NQ * HP)), _const_spec((MIXW, D)),
        ],
        out_specs=[
            pl.BlockSpec((tq, NMAIN), lambda i: (i, 0)),
            pl.BlockSpec((tq, kvw), lambda i: (i, 0)),
            pl.BlockSpec((BLK, kvw), lambda i: (i, 0)),
            pl.BlockSpec((8, CC), lambda i: (0, 0)), small(HP), small(HP), small(HP), small(CC), small(NQ * HP),
        ],
        out_shape=[
            jax.ShapeDtypeStruct((t, NMAIN), BF16), jax.ShapeDtypeStruct((t, kvw), F32),
            jax.ShapeDtypeStruct((nt * BLK, kvw), F32),
            jax.ShapeDtypeStruct((8, CC), F32), jax.ShapeDtypeStruct((1, HP), F32), jax.ShapeDtypeStruct((1, HP), F32),
            jax.ShapeDtypeStruct((1, HP), F32), jax.ShapeDtypeStruct((1, CC), F32),
            jax.ShapeDtypeStruct((1, NQ * HP), F32),
        ],
        scratch_shapes=[pltpu.VMEM((tq + BLK, kvw), F32)],
        compiler_params=_cparams(("arbitrary",)),
    )(dxm, dxm, proj, proj, proj, proj, proj, proj, proj, ao, cw, gq, gk, sinks, gco, gao, wo)


def _inproj_bwd(dpm, dkv, wpt, x, g1, dxm, tm):
    t = x.shape[0]
    kvw = 2 * NKV * HP

    def body(dp_ref, dk_ref, w_ref, x_ref, g_ref, dxm_ref, dx_ref, dg_ref):
        @pl.when(pl.program_id(0) == 0)
        def _():
            dg_ref[...] = jnp.zeros_like(dg_ref)

        dh = (jnp.dot(dp_ref[...], w_ref[0:NMAIN, :], preferred_element_type=F32)
              + jnp.dot(dk_ref[...], w_ref[NMAIN:NP, :], preferred_element_type=F32))
        r, xh = _rms_fwd(x_ref[...], 1.0 / D)
        dg_ref[...] += jnp.sum(dh * xh, axis=0, keepdims=True)
        dx_ref[...] = dxm_ref[...] + _rms_bwd(dh, g_ref[...], xh, r, 1.0 / D)

    row = lambda w: pl.BlockSpec((tm, w), lambda i: (i, 0))
    return pl.pallas_call(
        body, name="inproj_bwd", grid=(t // tm,),
        in_specs=[row(NMAIN), row(kvw), _const_spec((NP, D)), row(D), _const_spec((1, D)), row(D)],
        out_specs=[row(D), pl.BlockSpec((1, D), lambda i: (0, 0))],
        out_shape=[jax.ShapeDtypeStruct((t, D), F32), jax.ShapeDtypeStruct((1, D), F32)],
        compiler_params=_cparams(("arbitrary",)),
    )(dpm, dkv, wpt, x, g1, dxm)


def _rows_tile(rows, cap=512):
    for cand in range(min(rows, cap) // 16 * 16, 0, -16):
        if rows % cand == 0:
            return cand
    return rows


def _presum_halves(gs, theirs, core):
    outs = []
    for n, (ga, ta) in enumerate(zip(gs, theirs)):
        _, hr, cols = ta.shape

        def body(c_ref, g_ref, t_ref, o_ref):
            o_ref[...] = (g_ref[...].astype(F32) + t_ref[...].astype(F32)).astype(BF16)

        half = pl.BlockSpec((None, hr, cols), lambda k, c_ref: (k, 0, 0))
        outs.append(pl.pallas_call(
            body, name=f"presum_{n}",
            grid_spec=pltpu.PrefetchScalarGridSpec(
                num_scalar_prefetch=1, grid=(N_CHIPS,),
                in_specs=[pl.BlockSpec((None, hr, cols), lambda k, c_ref: (k, c_ref[0], 0)), half],
                out_specs=half),
            out_shape=jax.ShapeDtypeStruct(ta.shape, BF16), compiler_params=_cparams(("parallel",)),
        )(core, ga, ta))
    return outs


def _sum_chips(cs, name):
    outs = []
    for n, ca in enumerate(cs):
        _, rows, cols = ca.shape
        tr = _rows_tile(rows)

        def body(c_ref, o_ref):
            acc = c_ref[0].astype(F32)
            for j in range(1, N_CHIPS):
                acc = acc + c_ref[j].astype(F32)
            o_ref[...] = acc

        outs.append(pl.pallas_call(
            body, name=f"{name}_{n}", grid=(rows // tr,),
            in_specs=[pl.BlockSpec((N_CHIPS, tr, cols), lambda i: (0, i, 0))],
            out_specs=pl.BlockSpec((tr, cols), lambda i: (i, 0)),
            out_shape=jax.ShapeDtypeStruct((rows, cols), F32), compiler_params=_cparams(("parallel",)),
        )(ca))
    return outs


def _adamw(w, g, m, v, name):
    rows, cols = w.shape
    tr = _rows_tile(rows, 256)
    c1 = 1.0 - ADAM_B1 ** ADAM_STEP
    c2 = 1.0 - ADAM_B2 ** ADAM_STEP

    def body(w_ref, g_ref, m_ref, v_ref, d_ref, mo_ref, vo_ref):
        gv = g_ref[...]
        mn = ADAM_B1 * m_ref[...] + (1.0 - ADAM_B1) * gv
        vn = ADAM_B2 * v_ref[...] + (1.0 - ADAM_B2) * (gv * gv)
        mo_ref[...] = mn
        vo_ref[...] = vn
        d_ref[...] = -ADAM_LR * ((mn / c1) / (jnp.sqrt(vn / c2) + ADAM_EPS) + ADAM_WD * w_ref[...])

    spec = pl.BlockSpec((tr, cols), lambda i: (i, 0))
    sds = jax.ShapeDtypeStruct((rows, cols), F32)
    return pl.pallas_call(
        body, name=name, grid=(rows // tr,), in_specs=[spec] * 4, out_specs=[spec] * 3, out_shape=[sds] * 3,
        compiler_params=_cparams(("parallel",)),
    )(w, g, m, v)


def _place():
    x, y, c = lax.axis_index("x"), lax.axis_index("y"), lax.axis_index("c")
    chips = [(1 - x, y), (x, 1 - y), (1 - x, 1 - y)]
    return x, y, c, chips


ANY = pl.BlockSpec(memory_space=pl.ANY)
DMA_ROWS = 64


def _pieces(shape):
    rows = shape[-2]
    step = DMA_ROWS if rows % DMA_ROWS == 0 else rows
    lead = [()]
    for n in shape[:-2]:
        lead = [i + (k,) for i in lead for k in range(n)]
    return [i + (pl.ds(r0, step),) for i in lead for r0 in range(0, rows, step)]


def _start_pieces(make, src, dst):
    for idx in _pieces(src.shape):
        make(src.at[idx], dst.at[idx]).start()


def _gather_layer(blocks, layer):
    nw = len(blocks)

    def body(*refs):
        _gather_body(refs[:nw], refs[nw:2 * nw], refs[2 * nw:], layer, _start_pieces)

    return pl.pallas_call(
        body, name=f"gather_layer{layer}", in_specs=[ANY] * nw, out_specs=[ANY] * nw,
        out_shape=[jax.ShapeDtypeStruct((N_CHIPS,) + b.shape, b.dtype) for b in blocks],
        scratch_shapes=[pltpu.SemaphoreType.DMA((3, nw))] * 4,
        compiler_params=_cparams(has_side_effects=True),
    )(*blocks)


def _gather_body(srcs, outs, sems, layer, start):
    nw = len(srcs)
    ssem, rsem, fssem, frsem = sems
    x, y, c, chips = _place()
    kme = 2 * x + y

    def plane(j, w, to):
        return lambda s, d: pltpu.make_async_remote_copy(
            src_ref=s, dst_ref=d, send_sem=ssem.at[j, w], recv_sem=rsem.at[j, w], device_id=to,
            device_id_type=MESH)

    def passed(j, w):
        return lambda s, d: pltpu.make_async_remote_copy(
            src_ref=s, dst_ref=d, send_sem=fssem.at[j, w], recv_sem=frsem.at[j, w],
            device_id=(x, y, 1 - c), device_id_type=MESH)

    @pl.when(c == layer)
    def _():
        for j, (px, py) in enumerate(chips):
            for w in range(nw):
                start(plane(j, w, (px, py, c)), srcs[w], outs[w].at[kme])
        for j, (px, py) in enumerate(chips):
            for w in range(nw):
                got = outs[w].at[2 * px + py]
                plane(j, w, (px, py, c))(got, got).wait_recv()
                start(passed(j, w), got, got)
        for j, (px, py) in enumerate(chips):
            for w in range(nw):
                got = outs[w].at[2 * px + py]
                plane(j, w, (px, py, c))(got, got).wait_send()
                passed(j, w)(got, got).wait_send()

    @pl.when(c != layer)
    def _():
        for j, (px, py) in enumerate(chips):
            for w in range(nw):
                got = outs[w].at[2 * px + py]
                passed(j, w)(got, got).wait_recv()


def _handshake_all():
    x, y, c, _ = _place()
    barrier = pltpu.get_barrier_semaphore()
    for r in range(1, 8):
        peer = (x ^ (r >> 2), y ^ ((r >> 1) & 1), c ^ (r & 1))
        pl.semaphore_signal(barrier, inc=1, device_id=peer, device_id_type=MESH)
    pl.semaphore_wait(barrier, 7)


def _gather_layer_async(blocks, layer, name, collective_id):
    hbm = pltpu.MemorySpace.HBM
    srcs = [jax.new_ref(b, memory_space=hbm) for b in blocks]
    outs = [jax.empty_ref(jax.ShapeDtypeStruct((N_CHIPS,) + b.shape, b.dtype), memory_space=hbm) for b in blocks]

    @pl.kernel(mesh=plsc.ScalarSubcoreMesh(axis_name="seq", num_cores=1), name=name,
               scratch_types=[pltpu.SemaphoreType.DMA((3, len(blocks)))] * 4,
               compiler_params=pltpu.CompilerParams(collective_id=collective_id))
    def launch(*sems):
        _handshake_all()
        _gather_body(srcs, outs, sems, layer, lambda make, s, d: make(s, d).start())

    launch()
    return [o[...] for o in outs]


def _swap_halves(gs):
    nw = len(gs)

    def body(*refs):
        srcs, theirs = refs[:nw], refs[nw:2 * nw]
        ssem, rsem = refs[2 * nw:]
        x, y, c, _ = _place()

        def give(w):
            return lambda s, d: pltpu.make_async_remote_copy(
                src_ref=s, dst_ref=d, send_sem=ssem.at[w], recv_sem=rsem.at[w], device_id=(x, y, 1 - c),
                device_id_type=MESH)

        for w in range(nw):
            hr = theirs[w].shape[1]
            _start_pieces(give(w), srcs[w].at[:, pl.ds((1 - c) * hr, hr)], theirs[w])
        for w in range(nw):
            give(w)(theirs[w], theirs[w]).wait()

    return pl.pallas_call(
        body, name="swap_halves", in_specs=[ANY] * nw, out_specs=[ANY] * nw,
        out_shape=[jax.ShapeDtypeStruct((g.shape[0], g.shape[1] // 2, g.shape[2]), g.dtype) for g in gs],
        scratch_shapes=[pltpu.SemaphoreType.DMA((nw,))] * 2,
        compiler_params=_cparams(has_side_effects=True),
    )(*gs)


def _scatter_chips(ps):
    nw = len(ps)

    def body(*refs):
        _scatter_body(refs[:nw], refs[nw:2 * nw], refs[2 * nw:], _start_pieces)

    return pl.pallas_call(
        body, name="scatter_chips", in_specs=[ANY] * nw, out_specs=[ANY] * nw,
        out_shape=[jax.ShapeDtypeStruct(p.shape, p.dtype) for p in ps],
        scratch_shapes=[pltpu.SemaphoreType.DMA((3, nw)), pltpu.SemaphoreType.DMA((3, nw))],
        compiler_params=_cparams(has_side_effects=True),
    )(*ps)


def _scatter_body(srcs, outs, sems, start):
    nw = len(srcs)
    ssem, rsem = sems
    x, y, c, chips = _place()
    kme = 2 * x + y

    def give(j, w, to):
        return lambda s, d: pltpu.make_async_remote_copy(
            src_ref=s, dst_ref=d, send_sem=ssem.at[j, w], recv_sem=rsem.at[j, w], device_id=to,
            device_id_type=MESH)

    for j, (px, py) in enumerate(chips):
        for w in range(nw):
            start(give(j, w, (px, py, c)), srcs[w].at[2 * px + py], outs[w].at[kme])
    for j, (px, py) in enumerate(chips):
        for w in range(nw):
            got = outs[w].at[2 * px + py]
            give(j, w, (px, py, c))(got, got).wait_recv()
    for j, (px, py) in enumerate(chips):
        for w in range(nw):
            sent = srcs[w].at[2 * px + py]
            give(j, w, (px, py, c))(sent, sent).wait_send()


def _scatter_chips_async(ps, name, collective_id):
    hbm = pltpu.MemorySpace.HBM
    srcs = [jax.new_ref(p, memory_space=hbm) for p in ps]
    outs = [jax.empty_ref(jax.ShapeDtypeStruct(p.shape, p.dtype), memory_space=hbm) for p in ps]

    @pl.kernel(mesh=plsc.ScalarSubcoreMesh(axis_name="seq", num_cores=1), name=name,
               scratch_types=[pltpu.SemaphoreType.DMA((3, len(ps)))] * 2,
               compiler_params=pltpu.CompilerParams(collective_id=collective_id))
    def launch(*sems):
        _handshake_all()
        _scatter_body(srcs, outs, sems, lambda make, s, d: make(s, d).start())

    launch()
    return [o[...] for o in outs]


def _swap_siblings(rs):
    nw = len(rs)

    def body(*refs):
        srcs, outs = refs[:nw], refs[nw:2 * nw]
        ssem, rsem = refs[2 * nw:]
        x, y, c, _ = _place()

        def give(w):
            return lambda s, d: pltpu.make_async_remote_copy(
                src_ref=s, dst_ref=d, send_sem=ssem.at[w], recv_sem=rsem.at[w], device_id=(x, y, 1 - c),
                device_id_type=MESH)

        for w in range(nw):
            _start_pieces(give(w), srcs[w], outs[w])
        for w in range(nw):
            give(w)(srcs[w], outs[w]).wait()

    return pl.pallas_call(
        body, name="swap_siblings", in_specs=[ANY] * nw, out_specs=[ANY] * nw,
        out_shape=[jax.ShapeDtypeStruct(r.shape, r.dtype) for r in rs],
        scratch_shapes=[pltpu.SemaphoreType.DMA((nw,))] * 2,
        compiler_params=_cparams(has_side_effects=True),
    )(*rs)


def _allreduce_small(v):
    rows = v.shape[0]

    def body(v_ref, o_ref, buf, ssem, rsem):
        x, y, c, _ = _place()
        me = 4 * x + 2 * y + c
        buf[me] = v_ref[...]
        sends = []
        for r in range(1, 8):
            peer = (x ^ (r >> 2), y ^ ((r >> 1) & 1), c ^ (r & 1))
            cp = pltpu.make_async_remote_copy(
                src_ref=v_ref, dst_ref=buf.at[me], send_sem=ssem.at[r - 1], recv_sem=rsem.at[r - 1],
                device_id=peer, device_id_type=MESH)
            cp.start()
            sends.append(cp)
        for r in range(1, 8):
            src = me ^ r
            pltpu.make_async_remote_copy(
                src_ref=v_ref, dst_ref=buf.at[src], send_sem=ssem.at[r - 1], recv_sem=rsem.at[r - 1],
                device_id=(x, y, c), device_id_type=MESH).wait_recv()
        for cp in sends:
            cp.wait_send()
        acc = buf[0]
        for d in range(1, 8):
            acc = acc + buf[d]
        o_ref[...] = acc

    vm = pl.BlockSpec(memory_space=pltpu.VMEM)
    return pl.pallas_call(
        body, name="allreduce_small", in_specs=[vm], out_specs=vm,
        out_shape=jax.ShapeDtypeStruct(v.shape, F32),
        scratch_shapes=[pltpu.VMEM((8, rows, 128), F32), pltpu.SemaphoreType.DMA((7,)),
                        pltpu.SemaphoreType.DMA((7,))],
        compiler_params=_cparams(has_side_effects=True),
    )(v)


def _pad_heads(w, n_heads, axis):
    shp = w.shape
    w = w.reshape(shp[:axis] + (n_heads, HD) + shp[axis + 1:])
    pad = [(0, 0)] * w.ndim
    pad[axis + 1] = (0, HP - HD)
    w = jnp.pad(w, pad)
    return w.reshape(shp[:axis] + (n_heads * HP,) + shp[axis + 1:])


def _strip_heads(w, n_heads, axis):
    shp = w.shape
    w = w.reshape(shp[:axis] + (n_heads, HP) + shp[axis + 1:])
    w = lax.slice_in_dim(w, 0, HD, axis=axis + 1)
    return w.reshape(shp[:axis] + (n_heads * HD,) + shp[axis + 1:])


def _pad_win_t(wint):
    parts = [wint[:3 * CC], _pad_heads(wint[3 * CC:3 * CC + NQ * HD], NQ, 0),
             _pad_heads(wint[3 * CC + NQ * HD:3 * CC + (NQ + NKV) * HD], NKV, 0),
             _pad_heads(wint[3 * CC + (NQ + NKV) * HD:], NKV, 0)]
    return jnp.concatenate(parts, axis=0)


def _strip_win_t(gpt):
    parts = [gpt[:3 * CC], _strip_heads(gpt[O_Q:O_K], NQ, 0), _strip_heads(gpt[O_K:O_V], NKV, 0),
             _strip_heads(gpt[O_V:], NKV, 0)]
    return jnp.concatenate(parts, axis=0)


def _t(w):
    return jnp.swapaxes(w, -1, -2)


def _count(shape):
    n = 1
    for s in shape:
        n *= s
    return n


def _pack_rows(arrs):
    flat = [jnp.pad(a.reshape(-1), (0, (-_count(a.shape)) % 128)) for a in arrs]
    v = jnp.concatenate(flat)
    rows = -(-v.shape[0] // (8 * 128)) * 8
    return jnp.pad(v, (0, rows * 128 - v.shape[0])).reshape(rows, 128)


def kernel(x, norm1_g, w_in, conv_w, q_norm_g, k_norm_g, sinks, conv_out_g, attn_out_g, w_o, norm2_g, w_gate, w_up, w_down, loss_target, m_norm1_g, m_w_in, m_conv_w, m_q_norm_g, m_k_norm_g, m_sinks, m_conv_out_g, m_attn_out_g, m_w_o, m_norm2_g, m_w_gate, m_w_up, m_w_down, v_norm1_g, v_w_in, v_conv_w, v_q_norm_g, v_k_norm_g, v_sinks, v_conv_out_g, v_attn_out_g, v_w_o, v_norm2_g, v_w_gate, v_w_up, v_w_down):
    depth = w_in.shape[0]
    t = x.shape[1]
    xs = x.reshape(t, D)
    tgt = loss_target.reshape(t, D)
    xi, yi = lax.axis_index("x"), lax.axis_index("y")
    kme = 2 * xi + yi
    tm = min(512, t)
    tq = min(256, t)
    tf = min(256, t)

    cwp = jnp.pad(conv_w.reshape(depth * 3, CC // N_CHIPS), ((0, 8 - depth * 3), (0, 0)))
    own_f = [jnp.concatenate([_t(w_gate[l]), _t(w_up[l]), w_down[l]], axis=0).astype(BF16) for l in range(depth)]
    own_o = [w_o[l].astype(BF16) for l in range(depth)]
    own_i = [_t(w_in[l]).astype(BF16) for l in range(depth)]
    mine = lambda got, own: lax.dynamic_update_index_in_dim(got, own, kme, 0)
    got_i0, got_o0, got_cw = _gather_layer([own_i[0], own_o[0], cwp], 0)
    gf0_in = lax.optimization_barrier((own_f[0], got_i0))[0]
    (got_f0,) = _gather_layer_async([gf0_in], 0, "gather_ffn0_seq", collective_id=6)
    cw_full = mine(got_cw, cwp).transpose(1, 0, 2).reshape(8, CC)[:depth * 3].reshape(depth, 3, CC)

    def layer_params(l, got_i, got_o):
        wo = mine(got_o, own_o[l]).reshape(D, D)
        return dict(
            wpt=_pad_win_t(mine(got_i, own_i[l]).reshape(N_CHIPS * 576, D)),
            wo=jnp.concatenate([wo[:CC], _pad_heads(wo[CC:], NQ, 0)], axis=0),
            cw=jnp.pad(cw_full[l], ((0, 5), (0, 0))),
            g1=norm1_g[l].reshape(1, D), g2=norm2_g[l].reshape(1, D),
            gq=jnp.pad(q_norm_g[l], (0, HP - HD)).reshape(1, HP), gk=jnp.pad(k_norm_g[l], (0, HP - HD)).reshape(1, HP),
            sk=sinks[l].reshape(1, NQ), gco=conv_out_g[l].reshape(1, CC),
            gao=_pad_heads(attn_out_g[l], NQ, 0).reshape(1, NQ * HP))

    saved, layers = [], []
    cur = xs
    for l in range(depth):
        if l == 0:
            p = layer_params(0, got_i0, got_o0)
        else:
            got_f1, got_o1, got_i1 = lax.optimization_barrier((got_l1, cur))[0]
            p = layer_params(1, got_i1, got_o1)
        proj, h = _inproj_fwd(cur, p["g1"], p["wpt"], tm)
        xm, mix, ao = _mixer_fwd(proj, cur, p["cw"], p["gq"], p["gk"], p["sk"], p["gco"], p["gao"], p["wo"], tq)
        if l == 0:
            got_f0 = lax.optimization_barrier((got_f0, xm))[0]
            l1_in = lax.optimization_barrier(([own_f[1], own_o[1], own_i[1]], got_f0))[0]
            got_l1 = _gather_layer_async(l1_in, 1, "gather_layer1_seq", collective_id=1)
        p["gf"] = mine(got_f0 if l == 0 else got_f1, own_f[l])
        layers.append(p)
        xo, a, b, h2 = _ffn_fwd(xm, p["g2"], p["gf"], tm)
        saved.append(dict(x=cur, proj=proj, h=h, xm=xm, mix=mix, ao=ao, a=a, b=b, h2=h2))
        cur = xo
    lpart, dy = _loss_and_grad(cur, tgt, tm)
    loss = lax.psum(lpart[0, 0], ("x", "y", "c"))

    nt = t // tq
    ci = lax.axis_index("c")
    core = ci.reshape(1).astype(jnp.int32)
    rbig = [dict() for _ in range(depth)]
    gsmall = [None] * depth

    def reduce_start(gs, name, collective_id):
        ps = _presum_halves(gs, _swap_halves(gs), core)
        got = _scatter_chips(ps) if collective_id is None else _scatter_chips_async(ps, name, collective_id)
        return ps, got

    def reduce_finish(started, after):
        ps, got = started
        if after is not None:
            got = lax.optimization_barrier((got, after))[0]
        cs = [lax.dynamic_update_index_in_dim(g, lax.dynamic_index_in_dim(q, kme, 0, keepdims=False), kme, 0)
              for g, q in zip(got, ps)]
        r_mine = _sum_chips(cs, "chipsum")
        return [jnp.where(ci == 0, jnp.concatenate([a, b], axis=0), jnp.concatenate([b, a], axis=0))
                for a, b in zip(r_mine, _swap_siblings(r_mine))]

    in_flight = None
    for l in reversed(range(depth)):
        p, s = layers[l], saved[l]
        dxm, da, db, hm, dg2 = _ffn_bwd(dy, s["xm"], p["g2"], s["a"], s["b"], p["gf"], tf)
        g_wg = _wgrad_blocks(da, s["h2"], tm, "wgrad_gate")
        g_wu = _wgrad_blocks(db, s["h2"], tm, "wgrad_up")
        g_wd = _wgrad_blocks(hm, dy, tm, "wgrad_down")
        if in_flight is not None:
            rbig[l + 1]["in"], rbig[l + 1]["o"] = reduce_finish(in_flight, g_wd)
        ffn_flight = reduce_start([g_wg, g_wu, g_wd], f"scatter_ffn{l}_seq", 2 + 2 * l)
        dpm, dkvm, dkvh, dcw, dgq, dgk, dsk, dgco, dgao = _mixer_bwd(
            dxm, s["proj"], s["ao"], p["cw"], p["gq"], p["gk"], p["sk"], p["gco"], p["gao"], p["wo"], tq)
        g_wo = _wgrad(s["mix"], dxm, D, tm, "wgrad_o")
        kvw = dkvm.shape[1]
        halo = jnp.concatenate([dkvh.reshape(nt, BLK, kvw)[1:], jnp.zeros((1, BLK, kvw), F32)], axis=0)
        halo = jnp.pad(halo, ((0, 0), (tq - BLK, 0), (0, 0)))
        dkv = (dkvm.reshape(nt, tq, kvw) + halo).reshape(t, kvw).astype(BF16)
        dx, dg1 = _inproj_bwd(dpm, dkv, p["wpt"], s["x"], p["g1"], dxm, tm)
        g_wpm = _wgrad(dpm, s["h"], D, tm, "wgrad_in_main")
        g_wpk = _wgrad(dkv, s["h"], D, tm, "wgrad_in_kv")
        dy = dx
        g_in = _strip_win_t(jnp.concatenate([g_wpm, g_wpk], axis=0)).astype(BF16)
        g_o = jnp.concatenate([g_wo[:CC], _strip_heads(g_wo[CC:], NQ, 0)], axis=0).astype(BF16)
        gsmall[l] = dict(g1=dg1, cw=dcw[:3], gq=dgq[0, :HD], gk=dgk[0, :HD], sk=dsk[0, :NQ], gco=dgco,
                         gao=_strip_heads(dgao.reshape(NQ * HP), NQ, 0), g2=dg2)
        rbig[l]["g"], rbig[l]["u"], rbig[l]["d"] = reduce_finish(ffn_flight, dx)
        in_flight = reduce_start([g_in.reshape(N_CHIPS, -1, D), g_o.reshape(N_CHIPS, -1, D)],
                                 f"scatter_in{l}_seq", 3 + 2 * l)
    grad_x = dy.reshape(x.shape)

    small_shapes = dict(g1=(D,), cw=(3, CC), gq=(HD,), gk=(HD,), sk=(NQ,), gco=(CC,), gao=(NQ * HD,), g2=(D,))
    red = _allreduce_small(_pack_rows([gsmall[l][n] for l in range(depth) for n in small_shapes])).reshape(-1)
    red_small, offs = {n: [] for n in small_shapes}, 0
    for l in range(depth):
        for n, shp in small_shapes.items():
            cnt = _count(shp)
            red_small[n].append(red[offs:offs + cnt].reshape(shp))
            offs += -(-cnt // 128) * 128
    g_small = {n: jnp.stack(v) for n, v in red_small.items()}
    g_cw = lax.dynamic_slice_in_dim(g_small["cw"], kme * (CC // N_CHIPS), CC // N_CHIPS, axis=2)

    weights = [norm1_g, w_in, conv_w, q_norm_g, k_norm_g, sinks, conv_out_g, attn_out_g, w_o, norm2_g, w_gate,
               w_up, w_down]
    moms = [m_norm1_g, m_w_in, m_conv_w, m_q_norm_g, m_k_norm_g, m_sinks, m_conv_out_g, m_attn_out_g, m_w_o,
            m_norm2_g, m_w_gate, m_w_up, m_w_down]
    vars_ = [v_norm1_g, v_w_in, v_conv_w, v_q_norm_g, v_k_norm_g, v_sinks, v_conv_out_g, v_attn_out_g, v_w_o,
             v_norm2_g, v_w_gate, v_w_up, v_w_down]
    n_w = len(weights)
    big_idx = dict(zip(("in", "o", "g", "u", "d"), (1, 8, 10, 11, 12)))
    small_idx = [n for n in range(n_w) if n not in big_idx.values()]
    grads, deltas, new_m, new_v = [None] * n_w, [None] * n_w, [None] * n_w, [None] * n_w
    for n, g in zip(small_idx, (g_small["g1"], g_cw, g_small["gq"], g_small["gk"], g_small["sk"], g_small["gco"],
                                g_small["gao"], g_small["g2"])):
        grads[n] = g

    def update_big(name):
        n = big_idx[name]
        g = jnp.stack([rbig[l][name] for l in range(depth)])
        flip = g.shape != weights[n].shape
        rows2d = lambda a3: (_t(a3) if flip else a3).reshape(-1, D)
        res = _adamw(rows2d(weights[n]), g.reshape(-1, D), rows2d(moms[n]), rows2d(vars_[n]), f"adamw_{n}")
        res = [g] + [r.reshape(g.shape) for r in res]
        grads[n], deltas[n], new_m[n], new_v[n] = [_t(r) for r in res] if flip else res

    for name in ("g", "u", "d"):
        update_big(name)
    rbig[0]["in"], rbig[0]["o"] = reduce_finish(in_flight, new_v[big_idx["d"]])
    for name in ("in", "o"):
        update_big(name)
    res = _adamw(*[_pack_rows([arrs[n] for n in small_idx]) for arrs in (weights, grads, moms, vars_)],
                 "adamw_small")
    offs = 0
    for n in small_idx:
        shp = weights[n].shape
        cnt = _count(shp)
        deltas[n], new_m[n], new_v[n] = [r.reshape(-1)[offs:offs + cnt].reshape(shp) for r in res]
        offs += -(-cnt // 128) * 128
    return (loss, grad_x, *grads, *deltas, *new_m, *new_v)
```

```python
import functools

import jax
import jax.numpy as jnp
from jax import lax
from jax.experimental import pallas as pl
from jax.experimental.pallas import tpu as pltpu
from jax.experimental.pallas import tpu_sc as plsc

F32 = jnp.float32
BF16 = jnp.bfloat16

D = 1024
CC = 512
NQ = 8
NKV = 2
HD = 64
HP = 128
GRP = NQ // NKV
FF = 2816
FFB = FF // 4
BLK = 128
EPS = 1e-6
NEG = -1e30
SCALE = HD ** -0.5
O_BG, O_CG, O_HC, O_Q = 0, CC, 2 * CC, 3 * CC
O_K = O_Q + NQ * HP
O_V = O_K + NKV * HP
NP = O_V + NKV * HP
NMAIN = O_K
MIXW = CC + NQ * HP
N_CHIPS = 4
VMEM_LIMIT = 56 * 1024 * 1024
MESH = pl.DeviceIdType.MESH

ADAM_LR, ADAM_B1, ADAM_B2, ADAM_EPS, ADAM_WD, ADAM_STEP = 0.001, 0.9, 0.999, 1e-08, 0.01, 10


def _cparams(sem=None, **kw):
    if sem is not None:
        kw["dimension_semantics"] = sem
    return pltpu.CompilerParams(vmem_limit_bytes=VMEM_LIMIT, **kw)


def _const_spec(shape):
    nd = len(shape)
    return pl.BlockSpec(shape, lambda *_: (0,) * nd, pipeline_mode=pl.Buffered(1))


def _nt(a, b):
    return lax.dot_general(a, b, (((1,), (1,)), ((), ())), preferred_element_type=F32)


def _tn(a, b):
    return lax.dot_general(a, b, (((0,), (0,)), ((), ())), preferred_element_type=F32)


def _rms_fwd(x, inv_n):
    r = lax.rsqrt(jnp.sum(x * x, axis=-1, keepdims=True) * inv_n + EPS)
    return r, x * r


def _rms_bwd(dy, g, xh, r, inv_n):
    dxh = dy * g
    return r * (dxh - xh * (jnp.sum(dxh * xh, axis=-1, keepdims=True) * inv_n))


def _inproj_fwd(x, g1, wpt, tm):
    t = x.shape[0]

    def body(x_ref, g_ref, w_ref, p_ref, h_ref):
        _, xh = _rms_fwd(x_ref[...], 1.0 / D)
        h = (xh * g_ref[...]).astype(BF16)
        h_ref[...] = h
        p_ref[...] = _nt(h, w_ref[...])

    return pl.pallas_call(
        body, name="inproj_fwd", grid=(t // tm,),
        in_specs=[pl.BlockSpec((tm, D), lambda i: (i, 0)), _const_spec((1, D)), _const_spec((NP, D))],
        out_specs=[pl.BlockSpec((tm, NP), lambda i: (i, 0)), pl.BlockSpec((tm, D), lambda i: (i, 0))],
        out_shape=[jax.ShapeDtypeStruct((t, NP), F32), jax.ShapeDtypeStruct((t, D), BF16)],
        compiler_params=_cparams(("parallel",)),
    )(x, g1, wpt)


def _band_mask():
    r_io = lax.broadcasted_iota(jnp.int32, (BLK, 2 * BLK), 0)
    c_io = lax.broadcasted_iota(jnp.int32, (BLK, 2 * BLK), 1)
    return (c_io > r_io) & (c_io <= r_io + BLK), c_io


def _conv_taps(uf, n):
    u1 = pltpu.roll(uf, 1, 0)[8:8 + n]
    u2 = pltpu.roll(uf, 2, 0)[8:8 + n]
    return u1, u2


def _attn_probs(qn, kband, sink, valid):
    s = _nt(qn, kband) * SCALE
    s = jnp.where(valid, s, NEG)
    m = jnp.maximum(jnp.max(s, axis=-1, keepdims=True), sink)
    p = jnp.exp(s - m)
    es = jnp.exp(sink - m)
    inv = 1.0 / (jnp.sum(p, axis=-1, keepdims=True) + es)
    return p * inv, es * inv


def _norm_keys(kraw, gk):
    out = []
    for h in range(NKV):
        kh = kraw[:, h * HP:(h + 1) * HP]
        rk, khat = _rms_fwd(kh, 1.0 / HD)
        out.append((khat, rk, (khat * gk).astype(BF16)))
    return out


def _mixer_fwd(proj, x, cw, gq, gk, sinks, gco, gao, wo, tq):
    t = proj.shape[0]
    nb = tq // BLK
    r8 = tq // 8

    def body(p_ref, cgp_ref, hcp_ref, kvp_ref, x_ref, cw_ref, gq_ref, gk_ref, sk_ref, gco_ref, gao_ref,
             wo_ref, xm_ref, mix_ref, ao_ref):
        i = pl.program_id(0)
        cg = p_ref[:, O_CG:O_CG + CC]
        hc = p_ref[:, O_HC:O_HC + CC]
        u = cg * hc
        up = jnp.where(i > 0, cgp_ref[...] * hcp_ref[...], 0.0)
        u1, u2 = _conv_taps(jnp.concatenate([up, u], axis=0), tq)
        y = cw_ref[0:1, :] * u2 + cw_ref[1:2, :] * u1 + cw_ref[2:3, :] * u
        co = p_ref[:, O_BG:O_BG + CC] * y
        _, coh = _rms_fwd(co, 1.0 / CC)
        cn = coh * gco_ref[...]
        kraw = jnp.concatenate([kvp_ref[:, 0:NKV * HP], p_ref[:, O_K:O_K + NKV * HP]], axis=0)
        vraw = jnp.concatenate([kvp_ref[:, NKV * HP:], p_ref[:, O_V:O_V + NKV * HP]], axis=0)
        keys = _norm_keys(kraw, gk_ref[...])
        vb = [vraw[:, h * HP:(h + 1) * HP].astype(BF16) for h in range(NKV)]
        base_valid, c_io = _band_mask()
        for b in range(nb):
            lo = jnp.where(i * nb + b == 0, BLK, 0)
            valid = base_valid & (c_io >= lo)
            for g in range(NQ):
                h = g // GRP
                qg = p_ref[b * BLK:(b + 1) * BLK, O_Q + g * HP:O_Q + (g + 1) * HP]
                _, qh = _rms_fwd(qg, 1.0 / HD)
                qn = (qh * gq_ref[...]).astype(BF16)
                pr, _ = _attn_probs(qn, keys[h][2][b * BLK:b * BLK + 2 * BLK], sk_ref[0, g], valid)
                ao_ref[b * BLK:(b + 1) * BLK, g * HP:(g + 1) * HP] = jnp.dot(
                    pr.astype(BF16), vb[h][b * BLK:b * BLK + 2 * BLK], preferred_element_type=F32)
        _, aoh = _rms_fwd(ao_ref[...], 1.0 / (NQ * HD))
        an = aoh * gao_ref[...]
        mix = jnp.concatenate([cn, an], axis=1).astype(BF16)
        mix_ref[...] = mix
        xm_ref[...] = x_ref[...] + jnp.dot(mix, wo_ref[...], preferred_element_type=F32)

    prev8 = lambda col: pl.BlockSpec((8, CC), lambda i: (jnp.maximum(i * r8 - 1, 0), col))
    return pl.pallas_call(
        body, name="mixer_fwd", grid=(t // tq,),
        in_specs=[
            pl.BlockSpec((tq, NP), lambda i: (i, 0)),
            prev8(O_CG // CC), prev8(O_HC // CC),
            pl.BlockSpec((BLK, 2 * NKV * HP), lambda i: (jnp.maximum(i * nb - 1, 0), O_K // (2 * NKV * HP))),
            pl.BlockSpec((tq, D), lambda i: (i, 0)),
            _const_spec((8, CC)), _const_spec((1, HP)), _const_spec((1, HP)),
            pl.BlockSpec(memory_space=pltpu.SMEM),
            _const_spec((1, CC)), _const_spec((1, NQ * HP)), _const_spec((MIXW, D)),
        ],
        out_specs=[pl.BlockSpec((tq, D), lambda i: (i, 0)), pl.BlockSpec((tq, MIXW), lambda i: (i, 0)),
                   pl.BlockSpec((tq, NQ * HP), lambda i: (i, 0))],
        out_shape=[jax.ShapeDtypeStruct((t, D), F32), jax.ShapeDtypeStruct((t, MIXW), BF16),
                   jax.ShapeDtypeStruct((t, NQ * HP), F32)],
        compiler_params=_cparams(("parallel",)),
    )(proj, proj, proj, proj, x, cw, gq, gk, sinks, gco, gao, wo)


def _ffn_weight_specs():
    return [pl.BlockSpec((N_CHIPS, FFB, D), lambda i, j=j: (0, j, 0), pipeline_mode=pl.Buffered(1))
            for j in range(3)]


def _ffn_fwd(xm, g2, gf, tm):
    t = xm.shape[0]

    def body(x_ref, g_ref, wg_ref, wu_ref, wd_ref, xo_ref, a_ref, b_ref, h2_ref):
        xv = x_ref[...]
        _, xh = _rms_fwd(xv, 1.0 / D)
        h2 = (xh * g_ref[...]).astype(BF16)
        h2_ref[...] = h2
        acc = xv
        for k in range(N_CHIPS):
            a = _nt(h2, wg_ref[k])
            b = _nt(h2, wu_ref[k])
            a_ref[k] = a.astype(BF16)
            b_ref[k] = b.astype(BF16)
            hm = (a * jax.nn.sigmoid(a) * b).astype(BF16)
            acc = acc + jnp.dot(hm, wd_ref[k], preferred_element_type=F32)
        xo_ref[...] = acc

    row = lambda w: pl.BlockSpec((tm, w), lambda i: (i, 0))
    blk = pl.BlockSpec((N_CHIPS, tm, FFB), lambda i: (0, i, 0))
    return pl.pallas_call(
        body, name="ffn_fwd", grid=(t // tm,),
        in_specs=[row(D), _const_spec((1, D))] + _ffn_weight_specs(),
        out_specs=[row(D), blk, blk, row(D)],
        out_shape=[jax.ShapeDtypeStruct((t, D), F32), jax.ShapeDtypeStruct((N_CHIPS, t, FFB), BF16),
                   jax.ShapeDtypeStruct((N_CHIPS, t, FFB), BF16), jax.ShapeDtypeStruct((t, D), BF16)],
        compiler_params=_cparams(("parallel",)),
    )(xm, g2, gf, gf, gf)


def _loss_and_grad(y, tgt, tm):
    t = y.shape[0]

    def body(y_ref, t_ref, l_ref, dy_ref):
        @pl.when(pl.program_id(0) == 0)
        def _():
            l_ref[...] = jnp.zeros_like(l_ref)

        e = y_ref[...] - t_ref[...]
        dy_ref[...] = e * (1.0 / D)
        s = jnp.sum(jnp.sum(e * e, axis=-1, keepdims=True), axis=0, keepdims=True)
        l_ref[...] += s * (0.5 / D)

    row = pl.BlockSpec((tm, D), lambda i: (i, 0))
    return pl.pallas_call(
        body, name="loss", grid=(t // tm,), in_specs=[row, row],
        out_specs=[pl.BlockSpec((8, 128), lambda i: (0, 0)), row],
        out_shape=[jax.ShapeDtypeStruct((8, 128), F32), jax.ShapeDtypeStruct((t, D), F32)],
        compiler_params=_cparams(("arbitrary",)),
    )(y, tgt)


def _ffn_bwd(dy, xm, g2, a, b, gf, tm):
    t = dy.shape[0]

    def body(dy_ref, x_ref, g_ref, a_ref, b_ref, wg_ref, wu_ref, wd_ref, dx_ref, da_ref, db_ref, hm_ref, dg_ref):
        @pl.when(pl.program_id(0) == 0)
        def _():
            dg_ref[...] = jnp.zeros_like(dg_ref)

        dyv = dy_ref[...]
        dyb = dyv.astype(BF16)
        dh2 = jnp.zeros_like(dyv)
        for k in range(N_CHIPS):
            dhm = _nt(dyb, wd_ref[k])
            av = a_ref[k].astype(F32)
            bv = b_ref[k].astype(F32)
            sig = jax.nn.sigmoid(av)
            sil = av * sig
            hm_ref[k] = (sil * bv).astype(BF16)
            da = (dhm * bv * (sig * (1.0 + av * (1.0 - sig)))).astype(BF16)
            db = (dhm * sil).astype(BF16)
            da_ref[k] = da
            db_ref[k] = db
            dh2 = (dh2 + jnp.dot(da, wg_ref[k], preferred_element_type=F32)
                   + jnp.dot(db, wu_ref[k], preferred_element_type=F32))
        r, xh = _rms_fwd(x_ref[...], 1.0 / D)
        dg_ref[...] += jnp.sum(dh2 * xh, axis=0, keepdims=True)
        dx_ref[...] = dyv + _rms_bwd(dh2, g_ref[...], xh, r, 1.0 / D)

    row = lambda w: pl.BlockSpec((tm, w), lambda i: (i, 0))
    blk = pl.BlockSpec((N_CHIPS, tm, FFB), lambda i: (0, i, 0))
    bsd = jax.ShapeDtypeStruct((N_CHIPS, t, FFB), BF16)
    return pl.pallas_call(
        body, name="ffn_bwd", grid=(t // tm,),
        in_specs=[row(D), row(D), _const_spec((1, D)), blk, blk] + _ffn_weight_specs(),
        out_specs=[row(D), blk, blk, blk, pl.BlockSpec((1, D), lambda i: (0, 0))],
        out_shape=[jax.ShapeDtypeStruct((t, D), F32), bsd, bsd, bsd, jax.ShapeDtypeStruct((1, D), F32)],
        compiler_params=_cparams(("arbitrary",)),
    )(dy, xm, g2, a, b, gf, gf, gf)


def _wgrad_blocks(a, b, tt, name):
    _, t, rows = a.shape
    cols = b.shape[1]
    nsteps = t // tt

    def body(a_ref, b_ref, o_ref, acc_ref):
        s = pl.program_id(0)

        @pl.when(s == 0)
        def _():
            acc_ref[...] = jnp.zeros_like(acc_ref)

        bv = b_ref[...].astype(BF16)
        for k in range(N_CHIPS):
            acc_ref[k] += _tn(a_ref[k], bv)

        @pl.when(s == nsteps - 1)
        def _():
            o_ref[...] = acc_ref[...].astype(BF16)

    return pl.pallas_call(
        body, name=name, grid=(nsteps,),
        in_specs=[pl.BlockSpec((N_CHIPS, tt, rows), lambda s: (0, s, 0)), pl.BlockSpec((tt, cols), lambda s: (s, 0))],
        out_specs=pl.BlockSpec((N_CHIPS, rows, cols), lambda s: (0, 0, 0)),
        out_shape=jax.ShapeDtypeStruct((N_CHIPS, rows, cols), BF16),
        scratch_shapes=[pltpu.VMEM((N_CHIPS, rows, cols), F32)],
        compiler_params=_cparams(("arbitrary",)),
    )(a, b)


def _wgrad(a, b, tn, tt, name):
    t, k = a.shape
    n = b.shape[1]
    nsteps = t // tt

    def body(a_ref, b_ref, o_ref):
        @pl.when(pl.program_id(1) == 0)
        def _():
            o_ref[...] = jnp.zeros_like(o_ref)

        o_ref[...] += _tn(a_ref[...].astype(BF16), b_ref[...].astype(BF16))

    return pl.pallas_call(
        body, name=name, grid=(n // tn, nsteps),
        in_specs=[pl.BlockSpec((tt, k), lambda j, s: (s, 0)), pl.BlockSpec((tt, tn), lambda j, s: (s, j))],
        out_specs=pl.BlockSpec((k, tn), lambda j, s: (0, j)),
        out_shape=jax.ShapeDtypeStruct((k, n), F32),
        compiler_params=_cparams(("parallel", "arbitrary")),
    )(a, b)


def _mixer_bwd(dxm, proj, ao, cw, gq, gk, sinks, gco, gao, wo, tq):
    t = proj.shape[0]
    nb = tq // BLK
    r8 = tq // 8
    nt = t // tq
    te = tq + 8
    kvw = 2 * NKV * HP

    def body(dx_ref, dxn_ref, p_ref, cgp_ref, hcp_ref, bgn_ref, cgn_ref, hcn_ref, kvp_ref, ao_ref, cw_ref, gq_ref,
             gk_ref, sk_ref, gco_ref, gao_ref, wo_ref,
             dpm_ref, dkvm_ref, dkvh_ref, dcw_ref, dgq_ref, dgk_ref, dsk_ref, dgco_ref, dgao_ref, acc_ref):
        i = pl.program_id(0)

        @pl.when(i == 0)
        def _():
            for r in (dcw_ref, dgq_ref, dgk_ref, dsk_ref, dgco_ref, dgao_ref):
                r[...] = jnp.zeros_like(r)

        acc_ref[...] = jnp.zeros_like(acc_ref)
        live_rows = jnp.where(i < nt - 1, te, tq)
        dxb = dx_ref[...].astype(BF16)
        dxe = jnp.concatenate([dxb, dxn_ref[...].astype(BF16)], axis=0)
        dcn = _nt(dxe, wo_ref[0:CC, :])
        bg = jnp.concatenate([p_ref[:, O_BG:O_BG + CC], bgn_ref[...]], axis=0)
        cg = jnp.concatenate([p_ref[:, O_CG:O_CG + CC], cgn_ref[...]], axis=0)
        hc = jnp.concatenate([p_ref[:, O_HC:O_HC + CC], hcn_ref[...]], axis=0)
        u = cg * hc
        up = jnp.where(i > 0, cgp_ref[...] * hcp_ref[...], 0.0)
        u1, u2 = _conv_taps(jnp.concatenate([up, u], axis=0), te)
        w0, w1, w2 = cw_ref[0:1, :], cw_ref[1:2, :], cw_ref[2:3, :]
        y = w0 * u2 + w1 * u1 + w2 * u
        co = bg * y
        rc, coh = _rms_fwd(co, 1.0 / CC)
        dco = _rms_bwd(dcn, gco_ref[...], coh, rc, 1.0 / CC)
        row_io = lax.broadcasted_iota(jnp.int32, (te, 1), 0)
        own = row_io < tq
        dgco_ref[...] += jnp.sum(jnp.where(own, dcn * coh, 0.0), axis=0, keepdims=True)
        dyc = jnp.where(row_io < live_rows, dco * bg, 0.0)
        dyo = jnp.where(own, dyc, 0.0)
        dcw_ref[0:1, :] += jnp.sum(dyo * u2, axis=0, keepdims=True)
        dcw_ref[1:2, :] += jnp.sum(dyo * u1, axis=0, keepdims=True)
        dcw_ref[2:3, :] += jnp.sum(dyo * u, axis=0, keepdims=True)
        dy1 = pltpu.roll(dyc, te - 1, 0)[0:tq]
        dy2 = pltpu.roll(dyc, te - 2, 0)[0:tq]
        du = w2 * dyc[0:tq] + w1 * dy1 + w0 * dy2
        dpm_ref[:, O_BG:O_BG + CC] = (dco[0:tq] * y[0:tq]).astype(BF16)
        dpm_ref[:, O_CG:O_CG + CC] = (du * hc[0:tq]).astype(BF16)
        dpm_ref[:, O_HC:O_HC + CC] = (du * cg[0:tq]).astype(BF16)
        kraw = jnp.concatenate([kvp_ref[:, 0:NKV * HP], p_ref[:, O_K:O_K + NKV * HP]], axis=0)
        vraw = jnp.concatenate([kvp_ref[:, NKV * HP:], p_ref[:, O_V:O_V + NKV * HP]], axis=0)
        gqv, gkv = gq_ref[...], gk_ref[...]
        keys = _norm_keys(kraw, gkv)
        vb = [vraw[:, h * HP:(h + 1) * HP].astype(BF16) for h in range(NKV)]
        base_valid, c_io = _band_mask()
        lane = lax.broadcasted_iota(jnp.int32, (1, HP), 1)
        dgq, dgk, dsk = (jnp.zeros((1, HP), F32) for _ in range(3))
        dgao = jnp.zeros((1, NQ * HP), F32)
        for b in range(nb):
            lo = jnp.where(i * nb + b == 0, BLK, 0)
            valid = base_valid & (c_io >= lo)
            band = slice(b * BLK, b * BLK + 2 * BLK)
            blk = slice(b * BLK, (b + 1) * BLK)
            ra, aoh = _rms_fwd(ao_ref[blk, :], 1.0 / (NQ * HD))
            danb = _nt(dxb[blk], wo_ref[CC:MIXW, :])
            dgao = dgao + jnp.sum(danb * aoh, axis=0, keepdims=True)
            dao = _rms_bwd(danb, gao_ref[...], aoh, ra, 1.0 / (NQ * HD))
            fwd = []
            for g in range(NQ):
                rq, qh = _rms_fwd(p_ref[blk, O_Q + g * HP:O_Q + (g + 1) * HP], 1.0 / HD)
                qn = (qh * gqv).astype(BF16)
                fwd.append((rq, qh, qn) + _attn_probs(qn, keys[g // GRP][2][band], sk_ref[0, g], valid))
            dqs = []
            for h in range(NKV):
                khat, rk, kn = [a[band] for a in keys[h]]
                dss, prbs, qns, dobs = [], [], [], []
                for g in range(h * GRP, (h + 1) * GRP):
                    rq, qh, qn, pr, ps = fwd[g]
                    dob = dao[:, g * HP:(g + 1) * HP].astype(BF16)
                    dp = _nt(dob, vb[h][band])
                    delta = jnp.sum(pr * dp, axis=-1, keepdims=True)
                    dsb = (pr * (dp - delta) * SCALE).astype(BF16)
                    dsk = dsk + jnp.where(lane == g, -jnp.sum(ps * delta, axis=0, keepdims=True), 0.0)
                    dqn = jnp.dot(dsb, kn, preferred_element_type=F32)
                    dgq = dgq + jnp.sum(dqn * qh, axis=0, keepdims=True)
                    dqs.append(_rms_bwd(dqn, gqv, qh, rq, 1.0 / HD).astype(BF16))
                    dss.append(dsb)
                    prbs.append(pr.astype(BF16))
                    qns.append(qn)
                    dobs.append(dob)
                dkn = _tn(jnp.concatenate(dss, axis=0), jnp.concatenate(qns, axis=0))
                dv = _tn(jnp.concatenate(prbs, axis=0), jnp.concatenate(dobs, axis=0))
                dgk = dgk + jnp.sum(dkn * khat, axis=0, keepdims=True)
                acc_ref[band, h * HP:(h + 1) * HP] += _rms_bwd(dkn, gkv, khat, rk, 1.0 / HD)
                acc_ref[band, (NKV + h) * HP:(NKV + h + 1) * HP] += dv
            dpm_ref[blk, O_Q:O_K] = jnp.concatenate(dqs, axis=1)
        dgq_ref[...] += dgq
        dgk_ref[...] += dgk
        dsk_ref[...] += dsk
        dgao_ref[...] += dgao
        dkvh_ref[...] = acc_ref[0:BLK, :]
        dkvm_ref[...] = acc_ref[BLK:, :]

    prev8 = lambda col: pl.BlockSpec((8, CC), lambda i: (jnp.maximum(i * r8 - 1, 0), col))
    next8 = lambda col: pl.BlockSpec((8, CC), lambda i: (jnp.minimum((i + 1) * r8, t // 8 - 1), col))
    small = lambda n: pl.BlockSpec((1, n), lambda i: (0, 0))
    return pl.pallas_call(
        body, name="mixer_bwd", grid=(nt,),
        in_specs=[
            pl.BlockSpec((tq, D), lambda i: (i, 0)),
            pl.BlockSpec((8, D), lambda i: (jnp.minimum((i + 1) * r8, t // 8 - 1), 0)),
            pl.BlockSpec((tq, NP), lambda i: (i, 0)),
            prev8(O_CG // CC), prev8(O_HC // CC),
            next8(O_BG // CC), next8(O_CG // CC), next8(O_HC // CC),
            pl.BlockSpec((BLK, kvw), lambda i: (jnp.maximum(i * nb - 1, 0), O_K // kvw)),
            pl.BlockSpec((tq, NQ * HP), lambda i: (i, 0)),
            _const_spec((8, CC)), _const_spec((1, HP)), _const_spec((1, HP)),
            pl.BlockSpec(memory_space=pltpu.SMEM),
            _const_spec((1, CC)), _const_spec((1, NQ * HP)), _const_spec((MIXW, D)),
        ],
        out_specs=[
            pl.BlockSpec((tq, NMAIN), lambda i: (i, 0)),
            pl.BlockSpec((tq, kvw), lambda i: (i, 0)),
            pl.BlockSpec((BLK, kvw), lambda i: (i, 0)),
            pl.BlockSpec((8, CC), lambda i: (0, 0)), small(HP), small(HP), small(HP), small(CC), small(NQ * HP),
        ],
        out_shape=[
            jax.ShapeDtypeStruct((t, NMAIN), BF16), jax.ShapeDtypeStruct((t, kvw), F32),
            jax.ShapeDtypeStruct((nt * BLK, kvw), F32),
            jax.ShapeDtypeStruct((8, CC), F32), jax.ShapeDtypeStruct((1, HP), F32), jax.ShapeDtypeStruct((1, HP), F32),
            jax.ShapeDtypeStruct((1, HP), F32), jax.ShapeDtypeStruct((1, CC), F32),
            jax.ShapeDtypeStruct((1, NQ * HP), F32),
        ],
        scratch_shapes=[pltpu.VMEM((tq + BLK, kvw), F32)],
        compiler_params=_cparams(("arbitrary",)),
    )(dxm, dxm, proj, proj, proj, proj, proj, proj, proj, ao, cw, gq, gk, sinks, gco, gao, wo)


def _inproj_bwd(dpm, dkv, wpt, x, g1, dxm, tm):
    t = x.shape[0]
    kvw = 2 * NKV * HP

    def body(dp_ref, dk_ref, w_ref, x_ref, g_ref, dxm_ref, dx_ref, dg_ref):
        @pl.when(pl.program_id(0) == 0)
        def _():
            dg_ref[...] = jnp.zeros_like(dg_ref)

        dh = (jnp.dot(dp_ref[...], w_ref[0:NMAIN, :], preferred_element_type=F32)
              + jnp.dot(dk_ref[...], w_ref[NMAIN:NP, :], preferred_element_type=F32))
        r, xh = _rms_fwd(x_ref[...], 1.0 / D)
        dg_ref[...] += jnp.sum(dh * xh, axis=0, keepdims=True)
        dx_ref[...] = dxm_ref[...] + _rms_bwd(dh, g_ref[...], xh, r, 1.0 / D)

    row = lambda w: pl.BlockSpec((tm, w), lambda i: (i, 0))
    return pl.pallas_call(
        body, name="inproj_bwd", grid=(t // tm,),
        in_specs=[row(NMAIN), row(kvw), _const_spec((NP, D)), row(D), _const_spec((1, D)), row(D)],
        out_specs=[row(D), pl.BlockSpec((1, D), lambda i: (0, 0))],
        out_shape=[jax.ShapeDtypeStruct((t, D), F32), jax.ShapeDtypeStruct((1, D), F32)],
        compiler_params=_cparams(("arbitrary",)),
    )(dpm, dkv, wpt, x, g1, dxm)


def _rows_tile(rows, cap=512):
    for cand in range(min(rows, cap) // 16 * 16, 0, -16):
        if rows % cand == 0:
            return cand
    return rows


def _presum_halves(gs, theirs, core):
    outs = []
    for n, (ga, ta) in enumerate(zip(gs, theirs)):
        _, hr, cols = ta.shape

        def body(c_ref, g_ref, t_ref, o_ref):
            o_ref[...] = (g_ref[...].astype(F32) + t_ref[...].astype(F32)).astype(BF16)

        half = pl.BlockSpec((None, hr, cols), lambda k, c_ref: (k, 0, 0))
        outs.append(pl.pallas_call(
            body, name=f"presum_{n}",
            grid_spec=pltpu.PrefetchScalarGridSpec(
                num_scalar_prefetch=1, grid=(N_CHIPS,),
                in_specs=[pl.BlockSpec((None, hr, cols), lambda k, c_ref: (k, c_ref[0], 0)), half],
                out_specs=half),
            out_shape=jax.ShapeDtypeStruct(ta.shape, BF16), compiler_params=_cparams(("parallel",)),
        )(core, ga, ta))
    return outs


def _sum_chips(cs, name):
    outs = []
    for n, ca in enumerate(cs):
        _, rows, cols = ca.shape
        tr = _rows_tile(rows)

        def body(c_ref, o_ref):
            acc = c_ref[0].astype(F32)
            for j in range(1, N_CHIPS):
                acc = acc + c_ref[j].astype(F32)
            o_ref[...] = acc

        outs.append(pl.pallas_call(
            body, name=f"{name}_{n}", grid=(rows // tr,),
            in_specs=[pl.BlockSpec((N_CHIPS, tr, cols), lambda i: (0, i, 0))],
            out_specs=pl.BlockSpec((tr, cols), lambda i: (i, 0)),
            out_shape=jax.ShapeDtypeStruct((rows, cols), F32), compiler_params=_cparams(("parallel",)),
        )(ca))
    return outs


def _adamw(w, g, m, v, name):
    rows, cols = w.shape
    tr = _rows_tile(rows, 256)
    c1 = 1.0 - ADAM_B1 ** ADAM_STEP
    c2 = 1.0 - ADAM_B2 ** ADAM_STEP

    def body(w_ref, g_ref, m_ref, v_ref, d_ref, mo_ref, vo_ref):
        gv = g_ref[...]
        mn = ADAM_B1 * m_ref[...] + (1.0 - ADAM_B1) * gv
        vn = ADAM_B2 * v_ref[...] + (1.0 - ADAM_B2) * (gv * gv)
        mo_ref[...] = mn
        vo_ref[...] = vn
        d_ref[...] = -ADAM_LR * ((mn / c1) / (jnp.sqrt(vn / c2) + ADAM_EPS) + ADAM_WD * w_ref[...])

    spec = pl.BlockSpec((tr, cols), lambda i: (i, 0))
    sds = jax.ShapeDtypeStruct((rows, cols), F32)
    return pl.pallas_call(
        body, name=name, grid=(rows // tr,), in_specs=[spec] * 4, out_specs=[spec] * 3, out_shape=[sds] * 3,
        compiler_params=_cparams(("parallel",)),
    )(w, g, m, v)


def _place():
    x, y, c = lax.axis_index("x"), lax.axis_index("y"), lax.axis_index("c")
    chips = [(1 - x, y), (x, 1 - y), (1 - x, 1 - y)]
    return x, y, c, chips


ANY = pl.BlockSpec(memory_space=pl.ANY)
DMA_ROWS = 64


def _pieces(shape):
    rows = shape[-2]
    step = DMA_ROWS if rows % DMA_ROWS == 0 else rows
    lead = [()]
    for n in shape[:-2]:
        lead = [i + (k,) for i in lead for k in range(n)]
    return [i + (pl.ds(r0, step),) for i in lead for r0 in range(0, rows, step)]


def _start_pieces(make, src, dst):
    for idx in _pieces(src.shape):
        make(src.at[idx], dst.at[idx]).start()


def _gather_layer(blocks, layer):
    nw = len(blocks)

    def body(*refs):
        _gather_body(refs[:nw], refs[nw:2 * nw], refs[2 * nw:], layer, _start_pieces)

    return pl.pallas_call(
        body, name=f"gather_layer{layer}", in_specs=[ANY] * nw, out_specs=[ANY] * nw,
        out_shape=[jax.ShapeDtypeStruct((N_CHIPS,) + b.shape, b.dtype) for b in blocks],
        scratch_shapes=[pltpu.SemaphoreType.DMA((3, nw))] * 4,
        compiler_params=_cparams(has_side_effects=True),
    )(*blocks)


def _gather_body(srcs, outs, sems, layer, start):
    nw = len(srcs)
    ssem, rsem, fssem, frsem = sems
    x, y, c, chips = _place()
    kme = 2 * x + y

    def plane(j, w, to):
        return lambda s, d: pltpu.make_async_remote_copy(
            src_ref=s, dst_ref=d, send_sem=ssem.at[j, w], recv_sem=rsem.at[j, w], device_id=to,
            device_id_type=MESH)

    def passed(j, w):
        return lambda s, d: pltpu.make_async_remote_copy(
            src_ref=s, dst_ref=d, send_sem=fssem.at[j, w], recv_sem=frsem.at[j, w],
            device_id=(x, y, 1 - c), device_id_type=MESH)

    @pl.when(c == layer)
    def _():
        for j, (px, py) in enumerate(chips):
            for w in range(nw):
                start(plane(j, w, (px, py, c)), srcs[w], outs[w].at[kme])
        for j, (px, py) in enumerate(chips):
            for w in range(nw):
                got = outs[w].at[2 * px + py]
                plane(j, w, (px, py, c))(got, got).wait_recv()
                start(passed(j, w), got, got)
        for j, (px, py) in enumerate(chips):
            for w in range(nw):
                got = outs[w].at[2 * px + py]
                plane(j, w, (px, py, c))(got, got).wait_send()
                passed(j, w)(got, got).wait_send()

    @pl.when(c != layer)
    def _():
        for j, (px, py) in enumerate(chips):
            for w in range(nw):
                got = outs[w].at[2 * px + py]
                passed(j, w)(got, got).wait_recv()


def _handshake_all():
    x, y, c, _ = _place()
    barrier = pltpu.get_barrier_semaphore()
    for r in range(1, 8):
        peer = (x ^ (r >> 2), y ^ ((r >> 1) & 1), c ^ (r & 1))
        pl.semaphore_signal(barrier, inc=1, device_id=peer, device_id_type=MESH)
    pl.semaphore_wait(barrier, 7)


def _gather_layer_async(blocks, layer, name, collective_id):
    hbm = pltpu.MemorySpace.HBM
    srcs = [jax.new_ref(b, memory_space=hbm) for b in blocks]
    outs = [jax.empty_ref(jax.ShapeDtypeStruct((N_CHIPS,) + b.shape, b.dtype), memory_space=hbm) for b in blocks]

    @pl.kernel(mesh=plsc.ScalarSubcoreMesh(axis_name="seq", num_cores=1), name=name,
               scratch_types=[pltpu.SemaphoreType.DMA((3, len(blocks)))] * 4,
               compiler_params=pltpu.CompilerParams(collective_id=collective_id))
    def launch(*sems):
        _handshake_all()
        _gather_body(srcs, outs, sems, layer, lambda make, s, d: make(s, d).start())

    launch()
    return [o[...] for o in outs]


def _swap_halves(gs):
    nw = len(gs)

    def body(*refs):
        srcs, theirs = refs[:nw], refs[nw:2 * nw]
        ssem, rsem = refs[2 * nw:]
        x, y, c, _ = _place()

        def give(w):
            return lambda s, d: pltpu.make_async_remote_copy(
                src_ref=s, dst_ref=d, send_sem=ssem.at[w], recv_sem=rsem.at[w], device_id=(x, y, 1 - c),
                device_id_type=MESH)

        for w in range(nw):
            hr = theirs[w].shape[1]
            _start_pieces(give(w), srcs[w].at[:, pl.ds((1 - c) * hr, hr)], theirs[w])
        for w in range(nw):
            give(w)(theirs[w], theirs[w]).wait()

    return pl.pallas_call(
        body, name="swap_halves", in_specs=[ANY] * nw, out_specs=[ANY] * nw,
        out_shape=[jax.ShapeDtypeStruct((g.shape[0], g.shape[1] // 2, g.shape[2]), g.dtype) for g in gs],
        scratch_shapes=[pltpu.SemaphoreType.DMA((nw,))] * 2,
        compiler_params=_cparams(has_side_effects=True),
    )(*gs)


def _scatter_chips(ps):
    nw = len(ps)

    def body(*refs):
        _scatter_body(refs[:nw], refs[nw:2 * nw], refs[2 * nw:], _start_pieces)

    return pl.pallas_call(
        body, name="scatter_chips", in_specs=[ANY] * nw, out_specs=[ANY] * nw,
        out_shape=[jax.ShapeDtypeStruct(p.shape, p.dtype) for p in ps],
        scratch_shapes=[pltpu.SemaphoreType.DMA((3, nw)), pltpu.SemaphoreType.DMA((3, nw))],
        compiler_params=_cparams(has_side_effects=True),
    )(*ps)


def _scatter_body(srcs, outs, sems, start):
    nw = len(srcs)
    ssem, rsem = sems
    x, y, c, chips = _place()
    kme = 2 * x + y

    def give(j, w, to):
        return lambda s, d: pltpu.make_async_remote_copy(
            src_ref=s, dst_ref=d, send_sem=ssem.at[j, w], recv_sem=rsem.at[j, w], device_id=to,
            device_id_type=MESH)

    for j, (px, py) in enumerate(chips):
        for w in range(nw):
            start(give(j, w, (px, py, c)), srcs[w].at[2 * px + py], outs[w].at[kme])
    for j, (px, py) in enumerate(chips):
        for w in range(nw):
            got = outs[w].at[2 * px + py]
            give(j, w, (px, py, c))(got, got).wait_recv()
    for j, (px, py) in enumerate(chips):
        for w in range(nw):
            sent = srcs[w].at[2 * px + py]
            give(j, w, (px, py, c))(sent, sent).wait_send()


def _scatter_chips_async(ps, name, collective_id):
    hbm = pltpu.MemorySpace.HBM
    srcs = [jax.new_ref(p, memory_space=hbm) for p in ps]
    outs = [jax.empty_ref(jax.ShapeDtypeStruct(p.shape, p.dtype), memory_space=hbm) for p in ps]

    @pl.kernel(mesh=plsc.ScalarSubcoreMesh(axis_name="seq", num_cores=1), name=name,
               scratch_types=[pltpu.SemaphoreType.DMA((3, len(ps)))] * 2,
               compiler_params=pltpu.CompilerParams(collective_id=collective_id))
    def launch(*sems):
        _handshake_all()
        _scatter_body(srcs, outs, sems, lambda make, s, d: make(s, d).start())

    launch()
    return [o[...] for o in outs]


def _swap_siblings(rs):
    nw = len(rs)

    def body(*refs):
        srcs, outs = refs[:nw], refs[nw:2 * nw]
        ssem, rsem = refs[2 * nw:]
        x, y, c, _ = _place()

        def give(w):
            return lambda s, d: pltpu.make_async_remote_copy(
                src_ref=s, dst_ref=d, send_sem=ssem.at[w], recv_sem=rsem.at[w], device_id=(x, y, 1 - c),
                device_id_type=MESH)

        for w in range(nw):
            _start_pieces(give(w), srcs[w], outs[w])
        for w in range(nw):
            give(w)(srcs[w], outs[w]).wait()

    return pl.pallas_call(
        body, name="swap_siblings", in_specs=[ANY] * nw, out_specs=[ANY] * nw,
        out_shape=[jax.ShapeDtypeStruct(r.shape, r.dtype) for r in rs],
        scratch_shapes=[pltpu.SemaphoreType.DMA((nw,))] * 2,
        compiler_params=_cparams(has_side_effects=True),
    )(*rs)


def _allreduce_small(v):
    rows = v.shape[0]

    def body(v_ref, o_ref, buf, ssem, rsem):
        x, y, c, _ = _place()
        me = 4 * x + 2 * y + c
        buf[me] = v_ref[...]
        sends = []
        for r in range(1, 8):
            peer = (x ^ (r >> 2), y ^ ((r >> 1) & 1), c ^ (r & 1))
            cp = pltpu.make_async_remote_copy(
                src_ref=v_ref, dst_ref=buf.at[me], send_sem=ssem.at[r - 1], recv_sem=rsem.at[r - 1],
                device_id=peer, device_id_type=MESH)
            cp.start()
            sends.append(cp)
        for r in range(1, 8):
            src = me ^ r
            pltpu.make_async_remote_copy(
                src_ref=v_ref, dst_ref=buf.at[src], send_sem=ssem.at[r - 1], recv_sem=rsem.at[r - 1],
                device_id=(x, y, c), device_id_type=MESH).wait_recv()
        for cp in sends:
            cp.wait_send()
        acc = buf[0]
        for d in range(1, 8):
            acc = acc + buf[d]
        o_ref[...] = acc

    vm = pl.BlockSpec(memory_space=pltpu.VMEM)
    return pl.pallas_call(
        body, name="allreduce_small", in_specs=[vm], out_specs=vm,
        out_shape=jax.ShapeDtypeStruct(v.shape, F32),
        scratch_shapes=[pltpu.VMEM((8, rows, 128), F32), pltpu.SemaphoreType.DMA((7,)),
                        pltpu.SemaphoreType.DMA((7,))],
        compiler_params=_cparams(has_side_effects=True),
    )(v)


def _pad_heads(w, n_heads, axis):
    shp = w.shape
    w = w.reshape(shp[:axis] + (n_heads, HD) + shp[axis + 1:])
    pad = [(0, 0)] * w.ndim
    pad[axis + 1] = (0, HP - HD)
    w = jnp.pad(w, pad)
    return w.reshape(shp[:axis] + (n_heads * HP,) + shp[axis + 1:])


def _strip_heads(w, n_heads, axis):
    shp = w.shape
    w = w.reshape(shp[:axis] + (n_heads, HP) + shp[axis + 1:])
    w = lax.slice_in_dim(w, 0, HD, axis=axis + 1)
    return w.reshape(shp[:axis] + (n_heads * HD,) + shp[axis + 1:])


def _pad_win_t(wint):
    parts = [wint[:3 * CC], _pad_heads(wint[3 * CC:3 * CC + NQ * HD], NQ, 0),
             _pad_heads(wint[3 * CC + NQ * HD:3 * CC + (NQ + NKV) * HD], NKV, 0),
             _pad_heads(wint[3 * CC + (NQ + NKV) * HD:], NKV, 0)]
    return jnp.concatenate(parts, axis=0)


def _strip_win_t(gpt):
    parts = [gpt[:3 * CC], _strip_heads(gpt[O_Q:O_K], NQ, 0), _strip_heads(gpt[O_K:O_V], NKV, 0),
             _strip_heads(gpt[O_V:], NKV, 0)]
    return jnp.concatenate(parts, axis=0)


def _t(w):
    return jnp.swapaxes(w, -1, -2)


def _count(shape):
    n = 1
    for s in shape:
        n *= s
    return n


def _pack_rows(arrs):
    flat = [jnp.pad(a.reshape(-1), (0, (-_count(a.shape)) % 128)) for a in arrs]
    v = jnp.concatenate(flat)
    rows = -(-v.shape[0] // (8 * 128)) * 8
    return jnp.pad(v, (0, rows * 128 - v.shape[0])).reshape(rows, 128)


def kernel(x, norm1_g, w_in, conv_w, q_norm_g, k_norm_g, sinks, conv_out_g, attn_out_g, w_o, norm2_g, w_gate, w_up, w_down, loss_target, m_norm1_g, m_w_in, m_conv_w, m_q_norm_g, m_k_norm_g, m_sinks, m_conv_out_g, m_attn_out_g, m_w_o, m_norm2_g, m_w_gate, m_w_up, m_w_down, v_norm1_g, v_w_in, v_conv_w, v_q_norm_g, v_k_norm_g, v_sinks, v_conv_out_g, v_attn_out_g, v_w_o, v_norm2_g, v_w_gate, v_w_up, v_w_down):
    depth = w_in.shape[0]
    t = x.shape[1]
    xs = x.reshape(t, D)
    tgt = loss_target.reshape(t, D)
    xi, yi = lax.axis_index("x"), lax.axis_index("y")
    kme = 2 * xi + yi
    tm = min(512, t)
    tq = min(256, t)
    tf = min(256, t)

    cwp = jnp.pad(conv_w.reshape(depth * 3, CC // N_CHIPS), ((0, 8 - depth * 3), (0, 0)))
    own_f = [jnp.concatenate([_t(w_gate[l]), _t(w_up[l]), w_down[l]], axis=0).astype(BF16) for l in range(depth)]
    own_o = [w_o[l].astype(BF16) for l in range(depth)]
    own_i = [_t(w_in[l]).astype(BF16) for l in range(depth)]
    mine = lambda got, own: lax.dynamic_update_index_in_dim(got, own, kme, 0)
    got_i0, got_o0, got_cw = _gather_layer([own_i[0], own_o[0], cwp], 0)
    gf0_in = lax.optimization_barrier((own_f[0], got_i0))[0]
    (got_f0,) = _gather_layer_async([gf0_in], 0, "gather_ffn0_seq", collective_id=6)
    cw_full = mine(got_cw, cwp).transpose(1, 0, 2).reshape(8, CC)[:depth * 3].reshape(depth, 3, CC)

    def layer_params(l, got_i, got_o):
        wo = mine(got_o, own_o[l]).reshape(D, D)
        return dict(
            wpt=_pad_win_t(mine(got_i, own_i[l]).reshape(N_CHIPS * 576, D)),
            wo=jnp.concatenate([wo[:CC], _pad_heads(wo[CC:], NQ, 0)], axis=0),
            cw=jnp.pad(cw_full[l], ((0, 5), (0, 0))),
            g1=norm1_g[l].reshape(1, D), g2=norm2_g[l].reshape(1, D),
            gq=jnp.pad(q_norm_g[l], (0, HP - HD)).reshape(1, HP), gk=jnp.pad(k_norm_g[l], (0, HP - HD)).reshape(1, HP),
            sk=sinks[l].reshape(1, NQ), gco=conv_out_g[l].reshape(1, CC),
            gao=_pad_heads(attn_out_g[l], NQ, 0).reshape(1, NQ * HP))

    saved, layers = [], []
    cur = xs
    for l in range(depth):
        if l == 0:
            p = layer_params(0, got_i0, got_o0)
        else:
            got_f1, got_o1, got_i1 = lax.optimization_barrier((got_l1, cur))[0]
            p = layer_params(1, got_i1, got_o1)
        proj, h = _inproj_fwd(cur, p["g1"], p["wpt"], tm)
        xm, mix, ao = _mixer_fwd(proj, cur, p["cw"], p["gq"], p["gk"], p["sk"], p["gco"], p["gao"], p["wo"], tq)
        if l == 0:
            got_f0 = lax.optimization_barrier((got_f0, xm))[0]
            l1_in = lax.optimization_barrier(([own_f[1], own_o[1], own_i[1]], got_f0))[0]
            got_l1 = _gather_layer_async(l1_in, 1, "gather_layer1_seq", collective_id=1)
        p["gf"] = mine(got_f0 if l == 0 else got_f1, own_f[l])
        layers.append(p)
        xo, a, b, h2 = _ffn_fwd(xm, p["g2"], p["gf"], tm)
        saved.append(dict(x=cur, proj=proj, h=h, xm=xm, mix=mix, ao=ao, a=a, b=b, h2=h2))
        cur = xo
    lpart, dy = _loss_and_grad(cur, tgt, tm)
    loss = lax.psum(lpart[0, 0], ("x", "y", "c"))

    nt = t // tq
    ci = lax.axis_index("c")
    core = ci.reshape(1).astype(jnp.int32)
    rbig = [dict() for _ in range(depth)]
    gsmall = [None] * depth

    def reduce_start(gs, name, collective_id):
        ps = _presum_halves(gs, _swap_halves(gs), core)
        got = _scatter_chips(ps) if collective_id is None else _scatter_chips_async(ps, name, collective_id)
        return ps, got

    def reduce_finish(started, after):
        ps, got = started
        if after is not None:
            got = lax.optimization_barrier((got, after))[0]
        cs = [lax.dynamic_update_index_in_dim(g, lax.dynamic_index_in_dim(q, kme, 0, keepdims=False), kme, 0)
              for g, q in zip(got, ps)]
        r_mine = _sum_chips(cs, "chipsum")
        return [jnp.where(ci == 0, jnp.concatenate([a, b], axis=0), jnp.concatenate([b, a], axis=0))
                for a, b in zip(r_mine, _swap_siblings(r_mine))]

    in_flight = None
    for l in reversed(range(depth)):
        p, s = layers[l], saved[l]
        dxm, da, db, hm, dg2 = _ffn_bwd(dy, s["xm"], p["g2"], s["a"], s["b"], p["gf"], tf)
        g_wg = _wgrad_blocks(da, s["h2"], tm, "wgrad_gate")
        g_wu = _wgrad_blocks(db, s["h2"], tm, "wgrad_up")
        g_wd = _wgrad_blocks(hm, dy, tm, "wgrad_down")
        if in_flight is not None:
            rbig[l + 1]["in"], rbig[l + 1]["o"] = reduce_finish(in_flight, g_wd)
        ffn_flight = reduce_start([g_wg, g_wu, g_wd], f"scatter_ffn{l}_seq", 2 + 2 * l)
        dpm, dkvm, dkvh, dcw, dgq, dgk, dsk, dgco, dgao = _mixer_bwd(
            dxm, s["proj"], s["ao"], p["cw"], p["gq"], p["gk"], p["sk"], p["gco"], p["gao"], p["wo"], tq)
        g_wo = _wgrad(s["mix"], dxm, D, tm, "wgrad_o")
        kvw = dkvm.shape[1]
        halo = jnp.concatenate([dkvh.reshape(nt, BLK, kvw)[1:], jnp.zeros((1, BLK, kvw), F32)], axis=0)
        halo = jnp.pad(halo, ((0, 0), (tq - BLK, 0), (0, 0)))
        dkv = (dkvm.reshape(nt, tq, kvw) + halo).reshape(t, kvw).astype(BF16)
        dx, dg1 = _inproj_bwd(dpm, dkv, p["wpt"], s["x"], p["g1"], dxm, tm)
        g_wpm = _wgrad(dpm, s["h"], D, tm, "wgrad_in_main")
        g_wpk = _wgrad(dkv, s["h"], D, tm, "wgrad_in_kv")
        dy = dx
        g_in = _strip_win_t(jnp.concatenate([g_wpm, g_wpk], axis=0)).astype(BF16)
        g_o = jnp.concatenate([g_wo[:CC], _strip_heads(g_wo[CC:], NQ, 0)], axis=0).astype(BF16)
        gsmall[l] = dict(g1=dg1, cw=dcw[:3], gq=dgq[0, :HD], gk=dgk[0, :HD], sk=dsk[0, :NQ], gco=dgco,
                         gao=_strip_heads(dgao.reshape(NQ * HP), NQ, 0), g2=dg2)
        rbig[l]["g"], rbig[l]["u"], rbig[l]["d"] = reduce_finish(ffn_flight, dx)
        in_flight = reduce_start([g_in.reshape(N_CHIPS, -1, D), g_o.reshape(N_CHIPS, -1, D)],
                                 f"scatter_in{l}_seq", 3 + 2 * l)
    grad_x = dy.reshape(x.shape)

    small_shapes = dict(g1=(D,), cw=(3, CC), gq=(HD,), gk=(HD,), sk=(NQ,), gco=(CC,), gao=(NQ * HD,), g2=(D,))
    red = _allreduce_small(_pack_rows([gsmall[l][n] for l in range(depth) for n in small_shapes])).reshape(-1)
    red_small, offs = {n: [] for n in small_shapes}, 0
    for l in range(depth):
        for n, shp in small_shapes.items():
            cnt = _count(shp)
            red_small[n].append(red[offs:offs + cnt].reshape(shp))
            offs += -(-cnt // 128) * 128
    g_small = {n: jnp.stack(v) for n, v in red_small.items()}
    g_cw = lax.dynamic_slice_in_dim(g_small["cw"], kme * (CC // N_CHIPS), CC // N_CHIPS, axis=2)

    weights = [norm1_g, w_in, conv_w, q_norm_g, k_norm_g, sinks, conv_out_g, attn_out_g, w_o, norm2_g, w_gate,
               w_up, w_down]
    moms = [m_norm1_g, m_w_in, m_conv_w, m_q_norm_g, m_k_norm_g, m_sinks, m_conv_out_g, m_attn_out_g, m_w_o,
            m_norm2_g, m_w_gate, m_w_up, m_w_down]
    vars_ = [v_norm1_g, v_w_in, v_conv_w, v_q_norm_g, v_k_norm_g, v_sinks, v_conv_out_g, v_attn_out_g, v_w_o,
             v_norm2_g, v_w_gate, v_w_up, v_w_down]
    n_w = len(weights)
    big_idx = dict(zip(("in", "o", "g", "u", "d"), (1, 8, 10, 11, 12)))
    small_idx = [n for n in range(n_w) if n not in big_idx.values()]
    grads, deltas, new_m, new_v = [None] * n_w, [None] * n_w, [None] * n_w, [None] * n_w
    for n, g in zip(small_idx, (g_small["g1"], g_cw, g_small["gq"], g_small["gk"], g_small["sk"], g_small["gco"],
                                g_small["gao"], g_small["g2"])):
        grads[n] = g

    def update_big(name):
        n = big_idx[name]
        g = jnp.stack([rbig[l][name] for l in range(depth)])
        flip = g.shape != weights[n].shape
        rows2d = lambda a3: (_t(a3) if flip else a3).reshape(-1, D)
        res = _adamw(rows2d(weights[n]), g.reshape(-1, D), rows2d(moms[n]), rows2d(vars_[n]), f"adamw_{n}")
        res = [g] + [r.reshape(g.shape) for r in res]
        grads[n], deltas[n], new_m[n], new_v[n] = [_t(r) for r in res] if flip else res

    for name in ("g", "u", "d"):
        update_big(name)
    rbig[0]["in"], rbig[0]["o"] = reduce_finish(in_flight, new_v[big_idx["d"]])
    for name in ("in", "o"):
        update_big(name)
    res = _adamw(*[_pack_rows([arrs[n] for n in small_idx]) for arrs in (weights, grads, moms, vars_)],
                 "adamw_small")
    offs = 0
    for n in small_idx:
        shp = weights[n].shape
        cnt = _count(shp)
        deltas[n], new_m[n], new_v[n] = [r.reshape(-1)[offs:offs + cnt].reshape(shp) for r in res]
        offs += -(-cnt // 128) * 128
    return (loss, grad_x, *grads, *deltas, *new_m, *new_v)
```

```python
import functools

import jax
import jax.numpy as jnp
from jax import lax
from jax.experimental import pallas as pl
from jax.experimental.pallas import tpu as pltpu
from jax.experimental.pallas import tpu_sc as plsc

F32 = jnp.float32
BF16 = jnp.bfloat16

D = 1024
CC = 512
NQ = 8
NKV = 2
HD = 64
HP = 128
GRP = NQ // NKV
FF = 2816
FFB = FF // 4
BLK = 128
EPS = 1e-6
NEG = -1e30
SCALE = HD ** -0.5
O_BG, O_CG, O_HC, O_Q = 0, CC, 2 * CC, 3 * CC
O_K = O_Q + NQ * HP
O_V = O_K + NKV * HP
NP = O_V + NKV * HP
NMAIN = O_K
MIXW = CC + NQ * HP
N_CHIPS = 4
VMEM_LIMIT = 56 * 1024 * 1024
MESH = pl.DeviceIdType.MESH

ADAM_LR, ADAM_B1, ADAM_B2, ADAM_EPS, ADAM_WD, ADAM_STEP = 0.001, 0.9, 0.999, 1e-08, 0.01, 10


def _cparams(sem=None, **kw):
    if sem is not None:
        kw["dimension_semantics"] = sem
    return pltpu.CompilerParams(vmem_limit_bytes=VMEM_LIMIT, **kw)


def _const_spec(shape):
    nd = len(shape)
    return pl.BlockSpec(shape, lambda *_: (0,) * nd, pipeline_mode=pl.Buffered(1))


def _nt(a, b):
    return lax.dot_general(a, b, (((1,), (1,)), ((), ())), preferred_element_type=F32)


def _tn(a, b):
    return lax.dot_general(a, b, (((0,), (0,)), ((), ())), preferred_element_type=F32)


def _rms_fwd(x, inv_n):
    r = lax.rsqrt(jnp.sum(x * x, axis=-1, keepdims=True) * inv_n + EPS)
    return r, x * r


def _rms_bwd(dy, g, xh, r, inv_n):
    dxh = dy * g
    return r * (dxh - xh * (jnp.sum(dxh * xh, axis=-1, keepdims=True) * inv_n))


def _inproj_fwd(x, g1, wpt, tm):
    t = x.shape[0]

    def body(x_ref, g_ref, w_ref, p_ref, h_ref):
        _, xh = _rms_fwd(x_ref[...], 1.0 / D)
        h = (xh * g_ref[...]).astype(BF16)
        h_ref[...] = h
        p_ref[...] = _nt(h, w_ref[...])

    return pl.pallas_call(
        body, name="inproj_fwd", grid=(t // tm,),
        in_specs=[pl.BlockSpec((tm, D), lambda i: (i, 0)), _const_spec((1, D)), _const_spec((NP, D))],
        out_specs=[pl.BlockSpec((tm, NP), lambda i: (i, 0)), pl.BlockSpec((tm, D), lambda i: (i, 0))],
        out_shape=[jax.ShapeDtypeStruct((t, NP), F32), jax.ShapeDtypeStruct((t, D), BF16)],
        compiler_params=_cparams(("parallel",)),
    )(x, g1, wpt)


def _band_mask():
    r_io = lax.broadcasted_iota(jnp.int32, (BLK, 2 * BLK), 0)
    c_io = lax.broadcasted_iota(jnp.int32, (BLK, 2 * BLK), 1)
    return (c_io > r_io) & (c_io <= r_io + BLK), c_io


def _conv_taps(uf, n):
    u1 = pltpu.roll(uf, 1, 0)[8:8 + n]
    u2 = pltpu.roll(uf, 2, 0)[8:8 + n]
    return u1, u2


def _attn_probs(qs, kband, sink, valid):
    s = jnp.where(valid, _nt(qs, kband), NEG)
    m = jnp.maximum(jnp.max(s, axis=-1, keepdims=True), sink)
    p = jnp.exp(s - m)
    es = jnp.exp(sink - m)
    inv = 1.0 / (jnp.sum(p, axis=-1, keepdims=True) + es)
    return p * inv, es * inv


def _norm_keys(kraw, gk):
    out = []
    for h in range(NKV):
        kh = kraw[:, h * HP:(h + 1) * HP]
        rk, khat = _rms_fwd(kh, 1.0 / HD)
        out.append((khat, rk, (khat * gk).astype(BF16)))
    return out


def _mixer_fwd(proj, x, cw, gq, gk, sinks, gco, gao, wo, tq):
    t = proj.shape[0]
    nb = tq // BLK
    r8 = tq // 8

    def body(p_ref, cgp_ref, hcp_ref, kvp_ref, x_ref, cw_ref, gq_ref, gk_ref, sk_ref, gco_ref, gao_ref,
             wo_ref, xm_ref, mix_ref, ao_ref):
        i = pl.program_id(0)
        cg = p_ref[:, O_CG:O_CG + CC]
        hc = p_ref[:, O_HC:O_HC + CC]
        u = cg * hc
        up = jnp.where(i > 0, cgp_ref[...] * hcp_ref[...], 0.0)
        u1, u2 = _conv_taps(jnp.concatenate([up, u], axis=0), tq)
        y = cw_ref[0:1, :] * u2 + cw_ref[1:2, :] * u1 + cw_ref[2:3, :] * u
        co = p_ref[:, O_BG:O_BG + CC] * y
        _, coh = _rms_fwd(co, 1.0 / CC)
        cn = coh * gco_ref[...]
        kraw = jnp.concatenate([kvp_ref[:, 0:NKV * HP], p_ref[:, O_K:O_K + NKV * HP]], axis=0)
        vraw = jnp.concatenate([kvp_ref[:, NKV * HP:], p_ref[:, O_V:O_V + NKV * HP]], axis=0)
        keys = _norm_keys(kraw, gk_ref[...])
        vb = [vraw[:, h * HP:(h + 1) * HP].astype(BF16) for h in range(NKV)]
        base_valid, c_io = _band_mask()
        gqs = gq_ref[...] * SCALE
        for b in range(nb):
            lo = jnp.where(i * nb + b == 0, BLK, 0)
            valid = base_valid & (c_io >= lo)
            for g in range(NQ):
                h = g // GRP
                qg = p_ref[b * BLK:(b + 1) * BLK, O_Q + g * HP:O_Q + (g + 1) * HP]
                _, qh = _rms_fwd(qg, 1.0 / HD)
                qs = (qh * gqs).astype(BF16)
                pr, _ = _attn_probs(qs, keys[h][2][b * BLK:b * BLK + 2 * BLK], sk_ref[0, g], valid)
                ao_ref[b * BLK:(b + 1) * BLK, g * HP:(g + 1) * HP] = jnp.dot(
                    pr.astype(BF16), vb[h][b * BLK:b * BLK + 2 * BLK], preferred_element_type=F32)
        _, aoh = _rms_fwd(ao_ref[...], 1.0 / (NQ * HD))
        an = aoh * gao_ref[...]
        mix = jnp.concatenate([cn, an], axis=1).astype(BF16)
        mix_ref[...] = mix
        xm_ref[...] = x_ref[...] + jnp.dot(mix, wo_ref[...], preferred_element_type=F32)

    prev8 = lambda col: pl.BlockSpec((8, CC), lambda i: (jnp.maximum(i * r8 - 1, 0), col))
    return pl.pallas_call(
        body, name="mixer_fwd", grid=(t // tq,),
        in_specs=[
            pl.BlockSpec((tq, NP), lambda i: (i, 0)),
            prev8(O_CG // CC), prev8(O_HC // CC),
            pl.BlockSpec((BLK, 2 * NKV * HP), lambda i: (jnp.maximum(i * nb - 1, 0), O_K // (2 * NKV * HP))),
            pl.BlockSpec((tq, D), lambda i: (i, 0)),
            _const_spec((8, CC)), _const_spec((1, HP)), _const_spec((1, HP)),
            pl.BlockSpec(memory_space=pltpu.SMEM),
            _const_spec((1, CC)), _const_spec((1, NQ * HP)), _const_spec((MIXW, D)),
        ],
        out_specs=[pl.BlockSpec((tq, D), lambda i: (i, 0)), pl.BlockSpec((tq, MIXW), lambda i: (i, 0)),
                   pl.BlockSpec((tq, NQ * HP), lambda i: (i, 0))],
        out_shape=[jax.ShapeDtypeStruct((t, D), F32), jax.ShapeDtypeStruct((t, MIXW), BF16),
                   jax.ShapeDtypeStruct((t, NQ * HP), F32)],
        compiler_params=_cparams(("parallel",)),
    )(proj, proj, proj, proj, x, cw, gq, gk, sinks, gco, gao, wo)


def _ffn_weight_specs():
    return [pl.BlockSpec((N_CHIPS, FFB, D), lambda i, j=j: (0, j, 0), pipeline_mode=pl.Buffered(1))
            for j in range(3)]


def _ffn_fwd(xm, g2, gf, tm, tgt=None):
    t = xm.shape[0]
    last = tgt is not None

    def body(x_ref, g_ref, wg_ref, wu_ref, wd_ref, *rest):
        t_ref, rest = (rest[0], rest[1:]) if last else (None, rest)
        l_ref, rest = (rest[0], rest[1:]) if last else (None, rest)
        xo_ref, a_ref, b_ref, h2_ref = rest
        xv = x_ref[...]
        _, xh = _rms_fwd(xv, 1.0 / D)
        h2 = (xh * g_ref[...]).astype(BF16)
        h2_ref[...] = h2
        acc = xv
        for k in range(N_CHIPS):
            a = _nt(h2, wg_ref[k])
            b = _nt(h2, wu_ref[k])
            a_ref[k] = a.astype(BF16)
            b_ref[k] = b.astype(BF16)
            hm = (a * jax.nn.sigmoid(a) * b).astype(BF16)
            acc = acc + jnp.dot(hm, wd_ref[k], preferred_element_type=F32)
        if last:
            @pl.when(pl.program_id(0) == 0)
            def _():
                l_ref[...] = jnp.zeros_like(l_ref)

            e = acc - t_ref[...]
            xo_ref[...] = e * (1.0 / D)
            l_ref[...] += jnp.sum(jnp.sum(e * e, axis=-1, keepdims=True), axis=0, keepdims=True) * (0.5 / D)
        else:
            xo_ref[...] = acc

    row = lambda w: pl.BlockSpec((tm, w), lambda i: (i, 0))
    blk = pl.BlockSpec((N_CHIPS, tm, FFB), lambda i: (0, i, 0))
    bsd = jax.ShapeDtypeStruct((N_CHIPS, t, FFB), BF16)
    return pl.pallas_call(
        body, name="ffn_fwd_loss" if last else "ffn_fwd", grid=(t // tm,),
        in_specs=[row(D), _const_spec((1, D))] + _ffn_weight_specs() + ([row(D)] if last else []),
        out_specs=([pl.BlockSpec((8, 128), lambda i: (0, 0))] if last else []) + [row(D), blk, blk, row(D)],
        out_shape=([jax.ShapeDtypeStruct((8, 128), F32)] if last else [])
        + [jax.ShapeDtypeStruct((t, D), F32), bsd, bsd, jax.ShapeDtypeStruct((t, D), BF16)],
        compiler_params=_cparams(("arbitrary" if last else "parallel",)),
    )(*((xm, g2, gf, gf, gf) + ((tgt,) if last else ())))


def _ffn_bwd(dy, xm, g2, a, b, gf, tm):
    t = dy.shape[0]

    def body(dy_ref, x_ref, g_ref, a_ref, b_ref, wg_ref, wu_ref, wd_ref, dx_ref, da_ref, db_ref, hm_ref, dg_ref):
        @pl.when(pl.program_id(0) == 0)
        def _():
            dg_ref[...] = jnp.zeros_like(dg_ref)

        dyv = dy_ref[...]
        dyb = dyv.astype(BF16)
        dh2 = jnp.zeros_like(dyv)
        for k in range(N_CHIPS):
            dhm = _nt(dyb, wd_ref[k])
            av = a_ref[k].astype(F32)
            bv = b_ref[k].astype(F32)
            sig = jax.nn.sigmoid(av)
            sil = av * sig
            hm_ref[k] = (sil * bv).astype(BF16)
            da = (dhm * bv * (sig * (1.0 + av * (1.0 - sig)))).astype(BF16)
            db = (dhm * sil).astype(BF16)
            da_ref[k] = da
            db_ref[k] = db
            dh2 = (dh2 + jnp.dot(da, wg_ref[k], preferred_element_type=F32)
                   + jnp.dot(db, wu_ref[k], preferred_element_type=F32))
        r, xh = _rms_fwd(x_ref[...], 1.0 / D)
        dg_ref[...] += jnp.sum(dh2 * xh, axis=0, keepdims=True)
        dx_ref[...] = dyv + _rms_bwd(dh2, g_ref[...], xh, r, 1.0 / D)

    row = lambda w: pl.BlockSpec((tm, w), lambda i: (i, 0))
    blk = pl.BlockSpec((N_CHIPS, tm, FFB), lambda i: (0, i, 0))
    bsd = jax.ShapeDtypeStruct((N_CHIPS, t, FFB), BF16)
    return pl.pallas_call(
        body, name="ffn_bwd", grid=(t // tm,),
        in_specs=[row(D), row(D), _const_spec((1, D)), blk, blk] + _ffn_weight_specs(),
        out_specs=[row(D), blk, blk, blk, pl.BlockSpec((1, D), lambda i: (0, 0))],
        out_shape=[jax.ShapeDtypeStruct((t, D), F32), bsd, bsd, bsd, jax.ShapeDtypeStruct((1, D), F32)],
        compiler_params=_cparams(("arbitrary",)),
    )(dy, xm, g2, a, b, gf, gf, gf)


def _wgrad_blocks(a, b, tt, name):
    _, t, rows = a.shape
    cols = b.shape[1]
    nsteps = t // tt

    def body(a_ref, b_ref, o_ref, acc_ref):
        s = pl.program_id(0)

        @pl.when(s == 0)
        def _():
            acc_ref[...] = jnp.zeros_like(acc_ref)

        bv = b_ref[...].astype(BF16)
        for k in range(N_CHIPS):
            acc_ref[k] += _tn(a_ref[k], bv)

        @pl.when(s == nsteps - 1)
        def _():
            o_ref[...] = acc_ref[...].astype(BF16)

    return pl.pallas_call(
        body, name=name, grid=(nsteps,),
        in_specs=[pl.BlockSpec((N_CHIPS, tt, rows), lambda s: (0, s, 0)), pl.BlockSpec((tt, cols), lambda s: (s, 0))],
        out_specs=pl.BlockSpec((N_CHIPS, rows, cols), lambda s: (0, 0, 0)),
        out_shape=jax.ShapeDtypeStruct((N_CHIPS, rows, cols), BF16),
        scratch_shapes=[pltpu.VMEM((N_CHIPS, rows, cols), F32)],
        compiler_params=_cparams(("arbitrary",)),
    )(a, b)


def _wgrad(a, b, tn, tt, name):
    t, k = a.shape
    n = b.shape[1]
    nsteps = t // tt

    def body(a_ref, b_ref, o_ref):
        @pl.when(pl.program_id(1) == 0)
        def _():
            o_ref[...] = jnp.zeros_like(o_ref)

        o_ref[...] += _tn(a_ref[...].astype(BF16), b_ref[...].astype(BF16))

    return pl.pallas_call(
        body, name=name, grid=(n // tn, nsteps),
        in_specs=[pl.BlockSpec((tt, k), lambda j, s: (s, 0)), pl.BlockSpec((tt, tn), lambda j, s: (s, j))],
        out_specs=pl.BlockSpec((k, tn), lambda j, s: (0, j)),
        out_shape=jax.ShapeDtypeStruct((k, n), F32),
        compiler_params=_cparams(("parallel", "arbitrary")),
    )(a, b)


def _mixer_bwd(dxm, proj, ao, cw, gq, gk, sinks, gco, gao, wo, tq):
    t = proj.shape[0]
    nb = tq // BLK
    r8 = tq // 8
    nt = t // tq
    te = tq + 8
    kvw = 2 * NKV * HP

    def body(dx_ref, dxn_ref, p_ref, cgp_ref, hcp_ref, bgn_ref, cgn_ref, hcn_ref, kvp_ref, ao_ref, cw_ref, gq_ref,
             gk_ref, sk_ref, gco_ref, gao_ref, wo_ref,
             dpm_ref, dkvm_ref, dkvh_ref, dcw_ref, dgq_ref, dgk_ref, dsk_ref, dgco_ref, dgao_ref, acc_ref):
        i = pl.program_id(0)

        @pl.when(i == 0)
        def _():
            for r in (dcw_ref, dgq_ref, dgk_ref, dsk_ref, dgco_ref, dgao_ref):
                r[...] = jnp.zeros_like(r)

        acc_ref[...] = jnp.zeros_like(acc_ref)
        live_rows = jnp.where(i < nt - 1, te, tq)
        dxb = dx_ref[...].astype(BF16)
        dxe = jnp.concatenate([dxb, dxn_ref[...].astype(BF16)], axis=0)
        dcn = _nt(dxe, wo_ref[0:CC, :])
        bg = jnp.concatenate([p_ref[:, O_BG:O_BG + CC], bgn_ref[...]], axis=0)
        cg = jnp.concatenate([p_ref[:, O_CG:O_CG + CC], cgn_ref[...]], axis=0)
        hc = jnp.concatenate([p_ref[:, O_HC:O_HC + CC], hcn_ref[...]], axis=0)
        u = cg * hc
        up = jnp.where(i > 0, cgp_ref[...] * hcp_ref[...], 0.0)
        u1, u2 = _conv_taps(jnp.concatenate([up, u], axis=0), te)
        w0, w1, w2 = cw_ref[0:1, :], cw_ref[1:2, :], cw_ref[2:3, :]
        y = w0 * u2 + w1 * u1 + w2 * u
        co = bg * y
        rc, coh = _rms_fwd(co, 1.0 / CC)
        dco = _rms_bwd(dcn, gco_ref[...], coh, rc, 1.0 / CC)
        row_io = lax.broadcasted_iota(jnp.int32, (te, 1), 0)
        own = row_io < tq
        dgco_ref[...] += jnp.sum(jnp.where(own, dcn * coh, 0.0), axis=0, keepdims=True)
        dyc = jnp.where(row_io < live_rows, dco * bg, 0.0)
        dyo = jnp.where(own, dyc, 0.0)
        dcw_ref[0:1, :] += jnp.sum(dyo * u2, axis=0, keepdims=True)
        dcw_ref[1:2, :] += jnp.sum(dyo * u1, axis=0, keepdims=True)
        dcw_ref[2:3, :] += jnp.sum(dyo * u, axis=0, keepdims=True)
        dy1 = pltpu.roll(dyc, te - 1, 0)[0:tq]
        dy2 = pltpu.roll(dyc, te - 2, 0)[0:tq]
        du = w2 * dyc[0:tq] + w1 * dy1 + w0 * dy2
        dpm_ref[:, O_BG:O_BG + CC] = (dco[0:tq] * y[0:tq]).astype(BF16)
        dpm_ref[:, O_CG:O_CG + CC] = (du * hc[0:tq]).astype(BF16)
        dpm_ref[:, O_HC:O_HC + CC] = (du * cg[0:tq]).astype(BF16)
        kraw = jnp.concatenate([kvp_ref[:, 0:NKV * HP], p_ref[:, O_K:O_K + NKV * HP]], axis=0)
        vraw = jnp.concatenate([kvp_ref[:, NKV * HP:], p_ref[:, O_V:O_V + NKV * HP]], axis=0)
        gqv, gkv = gq_ref[...], gk_ref[...]
        keys = _norm_keys(kraw, gkv)
        vb = [vraw[:, h * HP:(h + 1) * HP].astype(BF16) for h in range(NKV)]
        base_valid, c_io = _band_mask()
        lane = lax.broadcasted_iota(jnp.int32, (1, HP), 1)
        dgq, dgk, dsk = (jnp.zeros((1, HP), F32) for _ in range(3))
        dgao = jnp.zeros((1, NQ * HP), F32)
        for b in range(nb):
            lo = jnp.where(i * nb + b == 0, BLK, 0)
            valid = base_valid & (c_io >= lo)
            band = slice(b * BLK, b * BLK + 2 * BLK)
            blk = slice(b * BLK, (b + 1) * BLK)
            ra, aoh = _rms_fwd(ao_ref[blk, :], 1.0 / (NQ * HD))
            danb = _nt(dxb[blk], wo_ref[CC:MIXW, :])
            dgao = dgao + jnp.sum(danb * aoh, axis=0, keepdims=True)
            dao = _rms_bwd(danb, gao_ref[...], aoh, ra, 1.0 / (NQ * HD))
            fwd = []
            for g in range(NQ):
                rq, qh = _rms_fwd(p_ref[blk, O_Q + g * HP:O_Q + (g + 1) * HP], 1.0 / HD)
                qs = (qh * (gqv * SCALE)).astype(BF16)
                fwd.append((rq, qh, qs) + _attn_probs(qs, keys[g // GRP][2][band], sk_ref[0, g], valid))
            dqs = []
            for h in range(NKV):
                khat, rk, kn = [a[band] for a in keys[h]]
                dss, prbs, qns, dobs = [], [], [], []
                for g in range(h * GRP, (h + 1) * GRP):
                    rq, qh, qs, pr, ps = fwd[g]
                    dob = dao[:, g * HP:(g + 1) * HP].astype(BF16)
                    dp = _nt(dob, vb[h][band])
                    delta = jnp.sum(pr * dp, axis=-1, keepdims=True)
                    dsb = (pr * (dp - delta)).astype(BF16)
                    dsk = dsk + jnp.where(lane == g, -jnp.sum(ps * delta, axis=0, keepdims=True), 0.0)
                    dqn = jnp.dot(dsb, kn, preferred_element_type=F32) * SCALE
                    dgq = dgq + jnp.sum(dqn * qh, axis=0, keepdims=True)
                    dqs.append(_rms_bwd(dqn, gqv, qh, rq, 1.0 / HD).astype(BF16))
                    dss.append(dsb)
                    prbs.append(pr.astype(BF16))
                    qns.append(qs)
                    dobs.append(dob)
                dkn = _tn(jnp.concatenate(dss, axis=0), jnp.concatenate(qns, axis=0))
                dv = _tn(jnp.concatenate(prbs, axis=0), jnp.concatenate(dobs, axis=0))
                dgk = dgk + jnp.sum(dkn * khat, axis=0, keepdims=True)
                acc_ref[band, h * HP:(h + 1) * HP] += _rms_bwd(dkn, gkv, khat, rk, 1.0 / HD)
                acc_ref[band, (NKV + h) * HP:(NKV + h + 1) * HP] += dv
            dpm_ref[blk, O_Q:O_K] = jnp.concatenate(dqs, axis=1)
        dgq_ref[...] += dgq
        dgk_ref[...] += dgk
        dsk_ref[...] += dsk
        dgao_ref[...] += dgao
        dkvh_ref[...] = acc_ref[0:BLK, :]
        dkvm_ref[...] = acc_ref[BLK:, :]

    prev8 = lambda col: pl.BlockSpec((8, CC), lambda i: (jnp.maximum(i * r8 - 1, 0), col))
    next8 = lambda col: pl.BlockSpec((8, CC), lambda i: (jnp.minimum((i + 1) * r8, t // 8 - 1), col))
    small = lambda n: pl.BlockSpec((1, n), lambda i: (0, 0))
    return pl.pallas_call(
        body, name="mixer_bwd", grid=(nt,),
        in_specs=[
            pl.BlockSpec((tq, D), lambda i: (i, 0)),
            pl.BlockSpec((8, D), lambda i: (jnp.minimum((i + 1) * r8, t // 8 - 1), 0)),
            pl.BlockSpec((tq, NP), lambda i: (i, 0)),
            prev8(O_CG // CC), prev8(O_HC // CC),
            next8(O_BG // CC), next8(O_CG // CC), next8(O_HC // CC),
            pl.BlockSpec((BLK, kvw), lambda i: (jnp.maximum(i * nb - 1, 0), O_K // kvw)),
            pl.BlockSpec((tq, NQ * HP), lambda i: (i, 0)),
            _const_spec((8, CC)), _const_spec((1, HP)), _const_spec((1, HP)),
            pl.BlockSpec(memory_space=pltpu.SMEM),
            _const_spec((1, CC)), _const_spec((1, NQ * HP)), _const_spec((MIXW, D)),
        ],
        out_specs=[
            pl.BlockSpec((tq, NMAIN), lambda i: (i, 0)),
            pl.BlockSpec((tq, kvw), lambda i: (i, 0)),
            pl.BlockSpec((BLK, kvw), lambda i: (i, 0)),
            pl.BlockSpec((8, CC), lambda i: (0, 0)), small(HP), small(HP), small(HP), small(CC), small(NQ * HP),
        ],
        out_shape=[
            jax.ShapeDtypeStruct((t, NMAIN), BF16), jax.ShapeDtypeStruct((t, kvw), F32),
            jax.ShapeDtypeStruct((nt * BLK, kvw), F32),
            jax.ShapeDtypeStruct((8, CC), F32), jax.ShapeDtypeStruct((1, HP), F32), jax.ShapeDtypeStruct((1, HP), F32),
            jax.ShapeDtypeStruct((1, HP), F32), jax.ShapeDtypeStruct((1, CC), F32),
            jax.ShapeDtypeStruct((1, NQ * HP), F32),
        ],
        scratch_shapes=[pltpu.VMEM((tq + BLK, kvw), F32)],
        compiler_params=_cparams(("arbitrary",)),
    )(dxm, dxm, proj, proj, proj, proj, proj, proj, proj, ao, cw, gq, gk, sinks, gco, gao, wo)


def _inproj_bwd(dpm, dkv, wpt, x, g1, dxm, tm):
    t = x.shape[0]
    kvw = 2 * NKV * HP

    def body(dp_ref, dk_ref, w_ref, x_ref, g_ref, dxm_ref, dx_ref, dg_ref):
        @pl.when(pl.program_id(0) == 0)
        def _():
            dg_ref[...] = jnp.zeros_like(dg_ref)

        dh = (jnp.dot(dp_ref[...], w_ref[0:NMAIN, :], preferred_element_type=F32)
              + jnp.dot(dk_ref[...], w_ref[NMAIN:NP, :], preferred_element_type=F32))
        r, xh = _rms_fwd(x_ref[...], 1.0 / D)
        dg_ref[...] += jnp.sum(dh * xh, axis=0, keepdims=True)
        dx_ref[...] = dxm_ref[...] + _rms_bwd(dh, g_ref[...], xh, r, 1.0 / D)

    row = lambda w: pl.BlockSpec((tm, w), lambda i: (i, 0))
    return pl.pallas_call(
        body, name="inproj_bwd", grid=(t // tm,),
        in_specs=[row(NMAIN), row(kvw), _const_spec((NP, D)), row(D), _const_spec((1, D)), row(D)],
        out_specs=[row(D), pl.BlockSpec((1, D), lambda i: (0, 0))],
        out_shape=[jax.ShapeDtypeStruct((t, D), F32), jax.ShapeDtypeStruct((1, D), F32)],
        compiler_params=_cparams(("arbitrary",)),
    )(dpm, dkv, wpt, x, g1, dxm)


def _rows_tile(rows, cap=512):
    for cand in range(min(rows, cap) // 16 * 16, 0, -16):
        if rows % cand == 0:
            return cand
    return rows


def _presum_halves(gs, theirs, core):
    n = len(gs)

    def body(c_ref, *refs):
        for g_ref, t_ref, o_ref in zip(refs[:n], refs[n:2 * n], refs[2 * n:]):
            o_ref[...] = (g_ref[...].astype(F32) + t_ref[...].astype(F32)).astype(BF16)

    half = lambda ta: pl.BlockSpec((None,) + ta.shape[1:], lambda k, c_ref: (k, 0, 0))
    own = lambda ta: pl.BlockSpec((None,) + ta.shape[1:], lambda k, c_ref: (k, c_ref[0], 0))
    return pl.pallas_call(
        body, name="presum",
        grid_spec=pltpu.PrefetchScalarGridSpec(
            num_scalar_prefetch=1, grid=(N_CHIPS,),
            in_specs=[own(ta) for ta in theirs] + [half(ta) for ta in theirs],
            out_specs=[half(ta) for ta in theirs]),
        out_shape=[jax.ShapeDtypeStruct(ta.shape, BF16) for ta in theirs],
        compiler_params=_cparams(("parallel",)),
    )(core, *gs, *theirs)


def _sum_chips(cs):
    n = len(cs)
    steps = 2

    def body(*refs):
        for c_ref, o_ref in zip(refs[:n], refs[n:]):
            acc = c_ref[0].astype(F32)
            for j in range(1, N_CHIPS):
                acc = acc + c_ref[j].astype(F32)
            o_ref[...] = acc

    return pl.pallas_call(
        body, name="chipsum", grid=(steps,),
        in_specs=[pl.BlockSpec((N_CHIPS, c.shape[1] // steps, c.shape[2]), lambda i: (0, i, 0)) for c in cs],
        out_specs=[pl.BlockSpec((c.shape[1] // steps, c.shape[2]), lambda i: (i, 0)) for c in cs],
        out_shape=[jax.ShapeDtypeStruct(c.shape[1:], F32) for c in cs],
        compiler_params=_cparams(("parallel",)),
    )(*cs)


def _adamw(w, g, m, v, name):
    rows, cols = w.shape
    tr = _rows_tile(rows, 256)
    c1 = 1.0 - ADAM_B1 ** ADAM_STEP
    c2 = 1.0 - ADAM_B2 ** ADAM_STEP

    def body(w_ref, g_ref, m_ref, v_ref, d_ref, mo_ref, vo_ref):
        gv = g_ref[...]
        mn = ADAM_B1 * m_ref[...] + (1.0 - ADAM_B1) * gv
        vn = ADAM_B2 * v_ref[...] + (1.0 - ADAM_B2) * (gv * gv)
        mo_ref[...] = mn
        vo_ref[...] = vn
        d_ref[...] = -ADAM_LR * ((mn / c1) / (jnp.sqrt(vn / c2) + ADAM_EPS) + ADAM_WD * w_ref[...])

    spec = pl.BlockSpec((tr, cols), lambda i: (i, 0))
    sds = jax.ShapeDtypeStruct((rows, cols), F32)
    return pl.pallas_call(
        body, name=name, grid=(rows // tr,), in_specs=[spec] * 4, out_specs=[spec] * 3, out_shape=[sds] * 3,
        compiler_params=_cparams(("parallel",)),
    )(w, g, m, v)


def _place():
    x, y, c = lax.axis_index("x"), lax.axis_index("y"), lax.axis_index("c")
    chips = [(1 - x, y), (x, 1 - y), (1 - x, 1 - y)]
    return x, y, c, chips


ANY = pl.BlockSpec(memory_space=pl.ANY)
DMA_ROWS = 64


def _pieces(shape):
    rows = shape[-2]
    step = DMA_ROWS if rows % DMA_ROWS == 0 else rows
    lead = [()]
    for n in shape[:-2]:
        lead = [i + (k,) for i in lead for k in range(n)]
    return [i + (pl.ds(r0, step),) for i in lead for r0 in range(0, rows, step)]


def _start_pieces(make, src, dst):
    for idx in _pieces(src.shape):
        make(src.at[idx], dst.at[idx]).start()


def _gather_layer(blocks, layer):
    nw = len(blocks)

    def body(*refs):
        _gather_body(refs[:nw], refs[nw:2 * nw], refs[2 * nw:], layer, _start_pieces)

    return pl.pallas_call(
        body, name=f"gather_layer{layer}", in_specs=[ANY] * nw, out_specs=[ANY] * nw,
        out_shape=[jax.ShapeDtypeStruct((N_CHIPS,) + b.shape, b.dtype) for b in blocks],
        scratch_shapes=[pltpu.SemaphoreType.DMA((3, nw))] * 4,
        compiler_params=_cparams(has_side_effects=True),
    )(*blocks)


def _gather_body(srcs, outs, sems, layer, start):
    nw = len(srcs)
    ssem, rsem, fssem, frsem = sems
    x, y, c, chips = _place()
    kme = 2 * x + y

    def plane(j, w, to):
        return lambda s, d: pltpu.make_async_remote_copy(
            src_ref=s, dst_ref=d, send_sem=ssem.at[j, w], recv_sem=rsem.at[j, w], device_id=to,
            device_id_type=MESH)

    def passed(j, w):
        return lambda s, d: pltpu.make_async_remote_copy(
            src_ref=s, dst_ref=d, send_sem=fssem.at[j, w], recv_sem=frsem.at[j, w],
            device_id=(x, y, 1 - c), device_id_type=MESH)

    @pl.when(c == layer)
    def _():
        for j, (px, py) in enumerate(chips):
            for w in range(nw):
                start(plane(j, w, (px, py, c)), srcs[w], outs[w].at[kme])
        for j, (px, py) in enumerate(chips):
            for w in range(nw):
                got = outs[w].at[2 * px + py]
                plane(j, w, (px, py, c))(got, got).wait_recv()
                start(passed(j, w), got, got)
        for j, (px, py) in enumerate(chips):
            for w in range(nw):
                got = outs[w].at[2 * px + py]
                plane(j, w, (px, py, c))(got, got).wait_send()
                passed(j, w)(got, got).wait_send()

    @pl.when(c != layer)
    def _():
        for j, (px, py) in enumerate(chips):
            for w in range(nw):
                got = outs[w].at[2 * px + py]
                passed(j, w)(got, got).wait_recv()


def _handshake_all():
    x, y, c, _ = _place()
    barrier = pltpu.get_barrier_semaphore()
    for r in range(1, 8):
        peer = (x ^ (r >> 2), y ^ ((r >> 1) & 1), c ^ (r & 1))
        pl.semaphore_signal(barrier, inc=1, device_id=peer, device_id_type=MESH)
    pl.semaphore_wait(barrier, 7)


def _gather_layer_async(blocks, layer, name, collective_id):
    hbm = pltpu.MemorySpace.HBM
    srcs = [jax.new_ref(b, memory_space=hbm) for b in blocks]
    outs = [jax.empty_ref(jax.ShapeDtypeStruct((N_CHIPS,) + b.shape, b.dtype), memory_space=hbm) for b in blocks]

    @pl.kernel(mesh=plsc.ScalarSubcoreMesh(axis_name="seq", num_cores=1), name=name,
               scratch_types=[pltpu.SemaphoreType.DMA((3, len(blocks)))] * 4,
               compiler_params=pltpu.CompilerParams(collective_id=collective_id))
    def launch(*sems):
        _handshake_all()
        _gather_body(srcs, outs, sems, layer, lambda make, s, d: make(s, d).start())

    launch()
    return [o[...] for o in outs]


def _swap_halves(gs):
    nw = len(gs)

    def body(*refs):
        srcs, theirs = refs[:nw], refs[nw:2 * nw]
        ssem, rsem = refs[2 * nw:]
        x, y, c, _ = _place()

        def give(w):
            return lambda s, d: pltpu.make_async_remote_copy(
                src_ref=s, dst_ref=d, send_sem=ssem.at[w], recv_sem=rsem.at[w], device_id=(x, y, 1 - c),
                device_id_type=MESH)

        for w in range(nw):
            hr = theirs[w].shape[1]
            _start_pieces(give(w), srcs[w].at[:, pl.ds((1 - c) * hr, hr)], theirs[w])
        for w in range(nw):
            give(w)(theirs[w], theirs[w]).wait()

    return pl.pallas_call(
        body, name="swap_halves", in_specs=[ANY] * nw, out_specs=[ANY] * nw,
        out_shape=[jax.ShapeDtypeStruct((g.shape[0], g.shape[1] // 2, g.shape[2]), g.dtype) for g in gs],
        scratch_shapes=[pltpu.SemaphoreType.DMA((nw,))] * 2,
        compiler_params=_cparams(has_side_effects=True),
    )(*gs)


def _scatter_chips(ps):
    nw = len(ps)

    def body(*refs):
        _scatter_body(refs[:nw], refs[nw:2 * nw], refs[2 * nw:], _start_pieces)

    return pl.pallas_call(
        body, name="scatter_chips", in_specs=[ANY] * nw, out_specs=[ANY] * nw,
        out_shape=[jax.ShapeDtypeStruct(p.shape, p.dtype) for p in ps],
        scratch_shapes=[pltpu.SemaphoreType.DMA((3, nw)), pltpu.SemaphoreType.DMA((3, nw))],
        compiler_params=_cparams(has_side_effects=True),
    )(*ps)


def _scatter_body(srcs, outs, sems, start):
    nw = len(srcs)
    ssem, rsem = sems
    x, y, c, chips = _place()
    kme = 2 * x + y

    def give(j, w, to):
        return lambda s, d: pltpu.make_async_remote_copy(
            src_ref=s, dst_ref=d, send_sem=ssem.at[j, w], recv_sem=rsem.at[j, w], device_id=to,
            device_id_type=MESH)

    for j, (px, py) in enumerate(chips):
        for w in range(nw):
            start(give(j, w, (px, py, c)), srcs[w].at[2 * px + py], outs[w].at[kme])
    for j, (px, py) in enumerate(chips):
        for w in range(nw):
            got = outs[w].at[2 * px + py]
            give(j, w, (px, py, c))(got, got).wait_recv()
    for j, (px, py) in enumerate(chips):
        for w in range(nw):
            sent = srcs[w].at[2 * px + py]
            give(j, w, (px, py, c))(sent, sent).wait_send()


def _scatter_chips_async(ps, name, collective_id):
    hbm = pltpu.MemorySpace.HBM
    srcs = [jax.new_ref(p, memory_space=hbm) for p in ps]
    outs = [jax.empty_ref(jax.ShapeDtypeStruct(p.shape, p.dtype), memory_space=hbm) for p in ps]

    @pl.kernel(mesh=plsc.ScalarSubcoreMesh(axis_name="seq", num_cores=1), name=name,
               scratch_types=[pltpu.SemaphoreType.DMA((3, len(ps)))] * 2,
               compiler_params=pltpu.CompilerParams(collective_id=collective_id))
    def launch(*sems):
        _handshake_all()
        _scatter_body(srcs, outs, sems, lambda make, s, d: make(s, d).start())

    launch()
    return [o[...] for o in outs]


def _swap_siblings(rs):
    nw = len(rs)

    def body(*refs):
        srcs, outs = refs[:nw], refs[nw:2 * nw]
        ssem, rsem = refs[2 * nw:]
        x, y, c, _ = _place()

        def give(w):
            return lambda s, d: pltpu.make_async_remote_copy(
                src_ref=s, dst_ref=d, send_sem=ssem.at[w], recv_sem=rsem.at[w], device_id=(x, y, 1 - c),
                device_id_type=MESH)

        for w in range(nw):
            _start_pieces(give(w), srcs[w], outs[w])
        for w in range(nw):
            give(w)(srcs[w], outs[w]).wait()

    return pl.pallas_call(
        body, name="swap_siblings", in_specs=[ANY] * nw, out_specs=[ANY] * nw,
        out_shape=[jax.ShapeDtypeStruct(r.shape, r.dtype) for r in rs],
        scratch_shapes=[pltpu.SemaphoreType.DMA((nw,))] * 2,
        compiler_params=_cparams(has_side_effects=True),
    )(*rs)


def _allreduce_small(v):
    rows = v.shape[0]

    def body(v_ref, o_ref, buf, ssem, rsem):
        x, y, c, _ = _place()
        me = 4 * x + 2 * y + c
        buf[me] = v_ref[...]
        sends = []
        for r in range(1, 8):
            peer = (x ^ (r >> 2), y ^ ((r >> 1) & 1), c ^ (r & 1))
            cp = pltpu.make_async_remote_copy(
                src_ref=v_ref, dst_ref=buf.at[me], send_sem=ssem.at[r - 1], recv_sem=rsem.at[r - 1],
                device_id=peer, device_id_type=MESH)
            cp.start()
            sends.append(cp)
        for r in range(1, 8):
            src = me ^ r
            pltpu.make_async_remote_copy(
                src_ref=v_ref, dst_ref=buf.at[src], send_sem=ssem.at[r - 1], recv_sem=rsem.at[r - 1],
                device_id=(x, y, c), device_id_type=MESH).wait_recv()
        for cp in sends:
            cp.wait_send()
        acc = buf[0]
        for d in range(1, 8):
            acc = acc + buf[d]
        o_ref[...] = acc

    vm = pl.BlockSpec(memory_space=pltpu.VMEM)
    return pl.pallas_call(
        body, name="allreduce_small", in_specs=[vm], out_specs=vm,
        out_shape=jax.ShapeDtypeStruct(v.shape, F32),
        scratch_shapes=[pltpu.VMEM((8, rows, 128), F32), pltpu.SemaphoreType.DMA((7,)),
                        pltpu.SemaphoreType.DMA((7,))],
        compiler_params=_cparams(has_side_effects=True),
    )(v)


def _pad_heads(w, n_heads, axis):
    shp = w.shape
    w = w.reshape(shp[:axis] + (n_heads, HD) + shp[axis + 1:])
    pad = [(0, 0)] * w.ndim
    pad[axis + 1] = (0, HP - HD)
    w = jnp.pad(w, pad)
    return w.reshape(shp[:axis] + (n_heads * HP,) + shp[axis + 1:])


def _strip_heads(w, n_heads, axis):
    shp = w.shape
    w = w.reshape(shp[:axis] + (n_heads, HP) + shp[axis + 1:])
    w = lax.slice_in_dim(w, 0, HD, axis=axis + 1)
    return w.reshape(shp[:axis] + (n_heads * HD,) + shp[axis + 1:])


def _pad_win_t(wint):
    parts = [wint[:3 * CC], _pad_heads(wint[3 * CC:3 * CC + NQ * HD], NQ, 0),
             _pad_heads(wint[3 * CC + NQ * HD:3 * CC + (NQ + NKV) * HD], NKV, 0),
             _pad_heads(wint[3 * CC + (NQ + NKV) * HD:], NKV, 0)]
    return jnp.concatenate(parts, axis=0)


def _strip_win_t(gpt):
    parts = [gpt[:3 * CC], _strip_heads(gpt[O_Q:O_K], NQ, 0), _strip_heads(gpt[O_K:O_V], NKV, 0),
             _strip_heads(gpt[O_V:], NKV, 0)]
    return jnp.concatenate(parts, axis=0)


def _t(w):
    return jnp.swapaxes(w, -1, -2)


def _count(shape):
    n = 1
    for s in shape:
        n *= s
    return n


def _pack_rows(arrs):
    flat = [jnp.pad(a.reshape(-1), (0, (-_count(a.shape)) % 128)) for a in arrs]
    v = jnp.concatenate(flat)
    rows = -(-v.shape[0] // (8 * 128)) * 8
    return jnp.pad(v, (0, rows * 128 - v.shape[0])).reshape(rows, 128)


def kernel(x, norm1_g, w_in, conv_w, q_norm_g, k_norm_g, sinks, conv_out_g, attn_out_g, w_o, norm2_g, w_gate, w_up, w_down, loss_target, m_norm1_g, m_w_in, m_conv_w, m_q_norm_g, m_k_norm_g, m_sinks, m_conv_out_g, m_attn_out_g, m_w_o, m_norm2_g, m_w_gate, m_w_up, m_w_down, v_norm1_g, v_w_in, v_conv_w, v_q_norm_g, v_k_norm_g, v_sinks, v_conv_out_g, v_attn_out_g, v_w_o, v_norm2_g, v_w_gate, v_w_up, v_w_down):
    depth = w_in.shape[0]
    t = x.shape[1]
    xs = x.reshape(t, D)
    tgt = loss_target.reshape(t, D)
    xi, yi = lax.axis_index("x"), lax.axis_index("y")
    kme = 2 * xi + yi
    tm = min(512, t)
    tq = min(256, t)
    tf = min(256, t)

    cwp = jnp.pad(conv_w.reshape(depth * 3, CC // N_CHIPS), ((0, 8 - depth * 3), (0, 0)))
    own_f = [jnp.concatenate([_t(w_gate[l]), _t(w_up[l]), w_down[l]], axis=0).astype(BF16) for l in range(depth)]
    own_o = [w_o[l].astype(BF16) for l in range(depth)]
    own_i = [_t(w_in[l]).astype(BF16) for l in range(depth)]
    mine = lambda got, own: lax.dynamic_update_index_in_dim(got, own, kme, 0)
    got_i0, got_o0, got_cw = _gather_layer([own_i[0], own_o[0], cwp], 0)
    gf0_in = lax.optimization_barrier((own_f[0], got_i0))[0]
    (got_f0,) = _gather_layer_async([gf0_in], 0, "gather_ffn0_seq", collective_id=6)
    cw_full = mine(got_cw, cwp).transpose(1, 0, 2).reshape(8, CC)[:depth * 3].reshape(depth, 3, CC)

    def layer_params(l, got_i, got_o):
        wo = mine(got_o, own_o[l]).reshape(D, D)
        return dict(
            wpt=_pad_win_t(mine(got_i, own_i[l]).reshape(N_CHIPS * 576, D)),
            wo=jnp.concatenate([wo[:CC], _pad_heads(wo[CC:], NQ, 0)], axis=0),
            cw=jnp.pad(cw_full[l], ((0, 5), (0, 0))),
            g1=norm1_g[l].reshape(1, D), g2=norm2_g[l].reshape(1, D),
            gq=jnp.pad(q_norm_g[l], (0, HP - HD)).reshape(1, HP), gk=jnp.pad(k_norm_g[l], (0, HP - HD)).reshape(1, HP),
            sk=sinks[l].reshape(1, NQ), gco=conv_out_g[l].reshape(1, CC),
            gao=_pad_heads(attn_out_g[l], NQ, 0).reshape(1, NQ * HP))

    saved, layers = [], []
    cur = xs
    for l in range(depth):
        x_in = cur
        if l == 0:
            p = layer_params(0, got_i0, got_o0)
        else:
            got_f1, got_o1, got_i1 = lax.optimization_barrier((got_l1, cur))[0]
            p = layer_params(1, got_i1, got_o1)
        proj, h = _inproj_fwd(cur, p["g1"], p["wpt"], tm)
        xm, mix, ao = _mixer_fwd(proj, cur, p["cw"], p["gq"], p["gk"], p["sk"], p["gco"], p["gao"], p["wo"], tq)
        if l == 0:
            got_f0 = lax.optimization_barrier((got_f0, xm))[0]
            l1_in = lax.optimization_barrier(([own_f[1], own_o[1], own_i[1]], got_f0))[0]
            got_l1 = _gather_layer_async(l1_in, 1, "gather_layer1_seq", collective_id=1)
        p["gf"] = mine(got_f0 if l == 0 else got_f1, own_f[l])
        layers.append(p)
        if l < depth - 1:
            cur, a, b, h2 = _ffn_fwd(xm, p["g2"], p["gf"], tm)
        else:
            lpart, dy, a, b, h2 = _ffn_fwd(xm, p["g2"], p["gf"], tm, tgt)
        saved.append(dict(x=x_in, proj=proj, h=h, xm=xm, mix=mix, ao=ao, a=a, b=b, h2=h2))
    loss = lax.psum(lpart[0, 0], ("x", "y", "c"))

    nt = t // tq
    ci = lax.axis_index("c")
    core = ci.reshape(1).astype(jnp.int32)
    rbig = [dict() for _ in range(depth)]
    gsmall = [None] * depth

    def reduce_start(gs, name, collective_id):
        ps = _presum_halves(gs, _swap_halves(gs), core)
        got = _scatter_chips(ps) if collective_id is None else _scatter_chips_async(ps, name, collective_id)
        return ps, got

    def reduce_finish(started, after):
        ps, got = started
        if after is not None:
            got = lax.optimization_barrier((got, after))[0]
        cs = [lax.dynamic_update_index_in_dim(g, lax.dynamic_index_in_dim(q, kme, 0, keepdims=False), kme, 0)
              for g, q in zip(got, ps)]
        r_mine = _sum_chips(cs)
        return [jnp.where(ci == 0, jnp.concatenate([a, b], axis=0), jnp.concatenate([b, a], axis=0))
                for a, b in zip(r_mine, _swap_siblings(r_mine))]

    in_flight = None
    for l in reversed(range(depth)):
        p, s = layers[l], saved[l]
        dxm, da, db, hm, dg2 = _ffn_bwd(dy, s["xm"], p["g2"], s["a"], s["b"], p["gf"], tf)
        g_wg = _wgrad_blocks(da, s["h2"], tm, "wgrad_gate")
        g_wu = _wgrad_blocks(db, s["h2"], tm, "wgrad_up")
        g_wd = _wgrad_blocks(hm, dy, tm, "wgrad_down")
        if in_flight is not None:
            rbig[l + 1]["in"], rbig[l + 1]["o"] = reduce_finish(in_flight, g_wd)
        ffn_flight = reduce_start([g_wg, g_wu, g_wd], f"scatter_ffn{l}_seq", 2 + 2 * l)
        dpm, dkvm, dkvh, dcw, dgq, dgk, dsk, dgco, dgao = _mixer_bwd(
            dxm, s["proj"], s["ao"], p["cw"], p["gq"], p["gk"], p["sk"], p["gco"], p["gao"], p["wo"], tq)
        g_wo = _wgrad(s["mix"], dxm, D, tm, "wgrad_o")
        kvw = dkvm.shape[1]
        halo = jnp.concatenate([dkvh.reshape(nt, BLK, kvw)[1:], jnp.zeros((1, BLK, kvw), F32)], axis=0)
        halo = jnp.pad(halo, ((0, 0), (tq - BLK, 0), (0, 0)))
        dkv = (dkvm.reshape(nt, tq, kvw) + halo).reshape(t, kvw).astype(BF16)
        dx, dg1 = _inproj_bwd(dpm, dkv, p["wpt"], s["x"], p["g1"], dxm, tm)
        g_wpm = _wgrad(dpm, s["h"], D, tm, "wgrad_in_main")
        g_wpk = _wgrad(dkv, s["h"], D, tm, "wgrad_in_kv")
        dy = dx
        g_in = _strip_win_t(jnp.concatenate([g_wpm, g_wpk], axis=0)).astype(BF16)
        g_o = jnp.concatenate([g_wo[:CC], _strip_heads(g_wo[CC:], NQ, 0)], axis=0).astype(BF16)
        gsmall[l] = dict(g1=dg1, cw=dcw[:3], gq=dgq[0, :HD], gk=dgk[0, :HD], sk=dsk[0, :NQ], gco=dgco,
                         gao=_strip_heads(dgao.reshape(NQ * HP), NQ, 0), g2=dg2)
        rbig[l]["g"], rbig[l]["u"], rbig[l]["d"] = reduce_finish(ffn_flight, dx)
        in_flight = reduce_start([g_in.reshape(N_CHIPS, -1, D), g_o.reshape(N_CHIPS, -1, D)],
                                 f"scatter_in{l}_seq", 3 + 2 * l)
    grad_x = dy.reshape(x.shape)

    small_shapes = dict(g1=(D,), cw=(3, CC), gq=(HD,), gk=(HD,), sk=(NQ,), gco=(CC,), gao=(NQ * HD,), g2=(D,))
    red = _allreduce_small(_pack_rows([gsmall[l][n] for l in range(depth) for n in small_shapes])).reshape(-1)
    red_small, offs = {n: [] for n in small_shapes}, 0
    for l in range(depth):
        for n, shp in small_shapes.items():
            cnt = _count(shp)
            red_small[n].append(red[offs:offs + cnt].reshape(shp))
            offs += -(-cnt // 128) * 128
    g_small = {n: jnp.stack(v) for n, v in red_small.items()}
    g_cw = lax.dynamic_slice_in_dim(g_small["cw"], kme * (CC // N_CHIPS), CC // N_CHIPS, axis=2)

    weights = [norm1_g, w_in, conv_w, q_norm_g, k_norm_g, sinks, conv_out_g, attn_out_g, w_o, norm2_g, w_gate,
               w_up, w_down]
    moms = [m_norm1_g, m_w_in, m_conv_w, m_q_norm_g, m_k_norm_g, m_sinks, m_conv_out_g, m_attn_out_g, m_w_o,
            m_norm2_g, m_w_gate, m_w_up, m_w_down]
    vars_ = [v_norm1_g, v_w_in, v_conv_w, v_q_norm_g, v_k_norm_g, v_sinks, v_conv_out_g, v_attn_out_g, v_w_o,
             v_norm2_g, v_w_gate, v_w_up, v_w_down]
    n_w = len(weights)
    big_idx = dict(zip(("in", "o", "g", "u", "d"), (1, 8, 10, 11, 12)))
    small_idx = [n for n in range(n_w) if n not in big_idx.values()]
    grads, deltas, new_m, new_v = [None] * n_w, [None] * n_w, [None] * n_w, [None] * n_w
    for n, g in zip(small_idx, (g_small["g1"], g_cw, g_small["gq"], g_small["gk"], g_small["sk"], g_small["gco"],
                                g_small["gao"], g_small["g2"])):
        grads[n] = g

    def update_big(name):
        n = big_idx[name]
        g = jnp.stack([rbig[l][name] for l in range(depth)])
        flip = g.shape != weights[n].shape
        rows2d = lambda a3: (_t(a3) if flip else a3).reshape(-1, D)
        res = _adamw(rows2d(weights[n]), g.reshape(-1, D), rows2d(moms[n]), rows2d(vars_[n]), f"adamw_{n}")
        res = [g] + [r.reshape(g.shape) for r in res]
        grads[n], deltas[n], new_m[n], new_v[n] = [_t(r) for r in res] if flip else res

    for name in ("g", "u", "d"):
        update_big(name)
    rbig[0]["in"], rbig[0]["o"] = reduce_finish(in_flight, new_v[big_idx["d"]])
    for name in ("in", "o"):
        update_big(name)
    res = _adamw(*[_pack_rows([arrs[n] for n in small_idx]) for arrs in (weights, grads, moms, vars_)],
                 "adamw_small")
    offs = 0
    for n in small_idx:
        shp = weights[n].shape
        cnt = _count(shp)
        deltas[n], new_m[n], new_v[n] = [r.reshape(-1)[offs:offs + cnt].reshape(shp) for r in res]
        offs += -(-cnt // 128) * 128
    return (loss, grad_x, *grads, *deltas, *new_m, *new_v)
```

```python
import functools

import jax
import jax.numpy as jnp
from jax import lax
from jax.experimental import pallas as pl
from jax.experimental.pallas import tpu as pltpu
from jax.experimental.pallas import tpu_sc as plsc

F32 = jnp.float32
BF16 = jnp.bfloat16

D = 1024
CC = 512
NQ = 8
NKV = 2
HD = 64
HP = 128
GRP = NQ // NKV
FF = 2816
FFB = FF // 4
BLK = 128
EPS = 1e-6
NEG = -1e30
SCALE = HD ** -0.5
O_BG, O_CG, O_HC, O_Q = 0, CC, 2 * CC, 3 * CC
O_K = O_Q + NQ * HP
O_V = O_K + NKV * HP
NP = O_V + NKV * HP
NMAIN = O_K
MIXW = CC + NQ * HP
N_CHIPS = 4
VMEM_LIMIT = 56 * 1024 * 1024
MESH = pl.DeviceIdType.MESH

ADAM_LR, ADAM_B1, ADAM_B2, ADAM_EPS, ADAM_WD, ADAM_STEP = 0.001, 0.9, 0.999, 1e-08, 0.01, 10


def _cparams(sem=None, **kw):
    if sem is not None:
        kw["dimension_semantics"] = sem
    return pltpu.CompilerParams(vmem_limit_bytes=VMEM_LIMIT, **kw)


def _const_spec(shape):
    nd = len(shape)
    return pl.BlockSpec(shape, lambda *_: (0,) * nd, pipeline_mode=pl.Buffered(1))


def _nt(a, b):
    return lax.dot_general(a, b, (((1,), (1,)), ((), ())), preferred_element_type=F32)


def _tn(a, b):
    return lax.dot_general(a, b, (((0,), (0,)), ((), ())), preferred_element_type=F32)


def _rms_fwd(x, inv_n):
    r = lax.rsqrt(jnp.sum(x * x, axis=-1, keepdims=True) * inv_n + EPS)
    return r, x * r


def _rms_bwd(dy, g, xh, r, inv_n):
    dxh = dy * g
    return r * (dxh - xh * (jnp.sum(dxh * xh, axis=-1, keepdims=True) * inv_n))


def _inproj_fwd(x, g1, wpt, tm):
    t = x.shape[0]

    def body(x_ref, g_ref, w_ref, p_ref, h_ref):
        _, xh = _rms_fwd(x_ref[...], 1.0 / D)
        h = (xh * g_ref[...]).astype(BF16)
        h_ref[...] = h
        p_ref[...] = _nt(h, w_ref[...])

    return pl.pallas_call(
        body, name="inproj_fwd", grid=(t // tm,),
        in_specs=[pl.BlockSpec((tm, D), lambda i: (i, 0)), _const_spec((1, D)), _const_spec((NP, D))],
        out_specs=[pl.BlockSpec((tm, NP), lambda i: (i, 0)), pl.BlockSpec((tm, D), lambda i: (i, 0))],
        out_shape=[jax.ShapeDtypeStruct((t, NP), F32), jax.ShapeDtypeStruct((t, D), BF16)],
        compiler_params=_cparams(("parallel",)),
    )(x, g1, wpt)


def _band_mask():
    r_io = lax.broadcasted_iota(jnp.int32, (BLK, 2 * BLK), 0)
    c_io = lax.broadcasted_iota(jnp.int32, (BLK, 2 * BLK), 1)
    return (c_io > r_io) & (c_io <= r_io + BLK), c_io


def _conv_taps(uf, n):
    u1 = pltpu.roll(uf, 1, 0)[8:8 + n]
    u2 = pltpu.roll(uf, 2, 0)[8:8 + n]
    return u1, u2


def _attn_probs(qs, kband, sink, valid):
    s = jnp.where(valid, _nt(qs, kband), NEG)
    m = jnp.maximum(jnp.max(s, axis=-1, keepdims=True), sink)
    p = jnp.exp(s - m)
    es = jnp.exp(sink - m)
    inv = 1.0 / (jnp.sum(p, axis=-1, keepdims=True) + es)
    return p * inv, es * inv


def _norm_keys(kraw, gk):
    out = []
    for h in range(NKV):
        kh = kraw[:, h * HP:(h + 1) * HP]
        rk, khat = _rms_fwd(kh, 1.0 / HD)
        out.append((khat, rk, (khat * gk).astype(BF16)))
    return out


def _mixer_fwd(proj, x, cw, gq, gk, sinks, gco, gao, wo, tq):
    t = proj.shape[0]
    nb = tq // BLK
    r8 = tq // 8

    def body(p_ref, cgp_ref, hcp_ref, kvp_ref, x_ref, cw_ref, gq_ref, gk_ref, sk_ref, gco_ref, gao_ref,
             wo_ref, xm_ref, mix_ref, ao_ref):
        i = pl.program_id(0)
        cg = p_ref[:, O_CG:O_CG + CC]
        hc = p_ref[:, O_HC:O_HC + CC]
        u = cg * hc
        up = jnp.where(i > 0, cgp_ref[...] * hcp_ref[...], 0.0)
        u1, u2 = _conv_taps(jnp.concatenate([up, u], axis=0), tq)
        y = cw_ref[0:1, :] * u2 + cw_ref[1:2, :] * u1 + cw_ref[2:3, :] * u
        co = p_ref[:, O_BG:O_BG + CC] * y
        _, coh = _rms_fwd(co, 1.0 / CC)
        cn = coh * gco_ref[...]
        kraw = jnp.concatenate([kvp_ref[:, 0:NKV * HP], p_ref[:, O_K:O_K + NKV * HP]], axis=0)
        vraw = jnp.concatenate([kvp_ref[:, NKV * HP:], p_ref[:, O_V:O_V + NKV * HP]], axis=0)
        keys = _norm_keys(kraw, gk_ref[...])
        vb = [vraw[:, h * HP:(h + 1) * HP].astype(BF16) for h in range(NKV)]
        base_valid, c_io = _band_mask()
        gqs = gq_ref[...] * SCALE
        for b in range(nb):
            lo = jnp.where(i * nb + b == 0, BLK, 0)
            valid = base_valid & (c_io >= lo)
            for g in range(NQ):
                h = g // GRP
                qg = p_ref[b * BLK:(b + 1) * BLK, O_Q + g * HP:O_Q + (g + 1) * HP]
                _, qh = _rms_fwd(qg, 1.0 / HD)
                qs = (qh * gqs).astype(BF16)
                pr, _ = _attn_probs(qs, keys[h][2][b * BLK:b * BLK + 2 * BLK], sk_ref[0, g], valid)
                ao_ref[b * BLK:(b + 1) * BLK, g * HP:(g + 1) * HP] = jnp.dot(
                    pr.astype(BF16), vb[h][b * BLK:b * BLK + 2 * BLK], preferred_element_type=F32)
        _, aoh = _rms_fwd(ao_ref[...], 1.0 / (NQ * HD))
        an = aoh * gao_ref[...]
        mix = jnp.concatenate([cn, an], axis=1).astype(BF16)
        mix_ref[...] = mix
        xm_ref[...] = x_ref[...] + jnp.dot(mix, wo_ref[...], preferred_element_type=F32)

    prev8 = lambda col: pl.BlockSpec((8, CC), lambda i: (jnp.maximum(i * r8 - 1, 0), col))
    return pl.pallas_call(
        body, name="mixer_fwd", grid=(t // tq,),
        in_specs=[
            pl.BlockSpec((tq, NP), lambda i: (i, 0)),
            prev8(O_CG // CC), prev8(O_HC // CC),
            pl.BlockSpec((BLK, 2 * NKV * HP), lambda i: (jnp.maximum(i * nb - 1, 0), O_K // (2 * NKV * HP))),
            pl.BlockSpec((tq, D), lambda i: (i, 0)),
            _const_spec((8, CC)), _const_spec((1, HP)), _const_spec((1, HP)),
            pl.BlockSpec(memory_space=pltpu.SMEM),
            _const_spec((1, CC)), _const_spec((1, NQ * HP)), _const_spec((MIXW, D)),
        ],
        out_specs=[pl.BlockSpec((tq, D), lambda i: (i, 0)), pl.BlockSpec((tq, MIXW), lambda i: (i, 0)),
                   pl.BlockSpec((tq, NQ * HP), lambda i: (i, 0))],
        out_shape=[jax.ShapeDtypeStruct((t, D), F32), jax.ShapeDtypeStruct((t, MIXW), BF16),
                   jax.ShapeDtypeStruct((t, NQ * HP), F32)],
        compiler_params=_cparams(("parallel",)),
    )(proj, proj, proj, proj, x, cw, gq, gk, sinks, gco, gao, wo)


def _ffn_weight_specs():
    return [pl.BlockSpec((N_CHIPS, FFB, D), lambda i, j=j: (0, j, 0), pipeline_mode=pl.Buffered(1))
            for j in range(3)]


def _ffn_fwd(xm, g2, gf, tm, tgt=None):
    t = xm.shape[0]
    last = tgt is not None

    def body(x_ref, g_ref, wg_ref, wu_ref, wd_ref, *rest):
        t_ref, rest = (rest[0], rest[1:]) if last else (None, rest)
        l_ref, rest = (rest[0], rest[1:]) if last else (None, rest)
        xo_ref, a_ref, b_ref, h2_ref = rest
        xv = x_ref[...]
        _, xh = _rms_fwd(xv, 1.0 / D)
        h2 = (xh * g_ref[...]).astype(BF16)
        h2_ref[...] = h2
        acc = xv
        for k in range(N_CHIPS):
            a = _nt(h2, wg_ref[k])
            b = _nt(h2, wu_ref[k])
            a_ref[k] = a.astype(BF16)
            b_ref[k] = b.astype(BF16)
            hm = (a * jax.nn.sigmoid(a) * b).astype(BF16)
            acc = acc + jnp.dot(hm, wd_ref[k], preferred_element_type=F32)
        if last:
            @pl.when(pl.program_id(0) == 0)
            def _():
                l_ref[...] = jnp.zeros_like(l_ref)

            e = acc - t_ref[...]
            xo_ref[...] = e * (1.0 / D)
            l_ref[...] += jnp.sum(jnp.sum(e * e, axis=-1, keepdims=True), axis=0, keepdims=True) * (0.5 / D)
        else:
            xo_ref[...] = acc

    row = lambda w: pl.BlockSpec((tm, w), lambda i: (i, 0))
    blk = pl.BlockSpec((N_CHIPS, tm, FFB), lambda i: (0, i, 0))
    bsd = jax.ShapeDtypeStruct((N_CHIPS, t, FFB), BF16)
    return pl.pallas_call(
        body, name="ffn_fwd_loss" if last else "ffn_fwd", grid=(t // tm,),
        in_specs=[row(D), _const_spec((1, D))] + _ffn_weight_specs() + ([row(D)] if last else []),
        out_specs=([pl.BlockSpec((8, 128), lambda i: (0, 0))] if last else []) + [row(D), blk, blk, row(D)],
        out_shape=([jax.ShapeDtypeStruct((8, 128), F32)] if last else [])
        + [jax.ShapeDtypeStruct((t, D), F32), bsd, bsd, jax.ShapeDtypeStruct((t, D), BF16)],
        compiler_params=_cparams(("arbitrary" if last else "parallel",)),
    )(*((xm, g2, gf, gf, gf) + ((tgt,) if last else ())))


def _ffn_bwd(dy, xm, g2, a, b, gf, tm):
    t = dy.shape[0]

    def body(dy_ref, x_ref, g_ref, a_ref, b_ref, wg_ref, wu_ref, wd_ref, dx_ref, da_ref, db_ref, hm_ref, dg_ref):
        @pl.when(pl.program_id(0) == 0)
        def _():
            dg_ref[...] = jnp.zeros_like(dg_ref)

        dyv = dy_ref[...]
        dyb = dyv.astype(BF16)
        dh2 = jnp.zeros_like(dyv)
        for k in range(N_CHIPS):
            dhm = _nt(dyb, wd_ref[k])
            av = a_ref[k].astype(F32)
            bv = b_ref[k].astype(F32)
            sig = jax.nn.sigmoid(av)
            sil = av * sig
            hm_ref[k] = (sil * bv).astype(BF16)
            da = (dhm * bv * (sig * (1.0 + av * (1.0 - sig)))).astype(BF16)
            db = (dhm * sil).astype(BF16)
            da_ref[k] = da
            db_ref[k] = db
            dh2 = (dh2 + jnp.dot(da, wg_ref[k], preferred_element_type=F32)
                   + jnp.dot(db, wu_ref[k], preferred_element_type=F32))
        r, xh = _rms_fwd(x_ref[...], 1.0 / D)
        dg_ref[...] += jnp.sum(dh2 * xh, axis=0, keepdims=True)
        dx_ref[...] = dyv + _rms_bwd(dh2, g_ref[...], xh, r, 1.0 / D)

    row = lambda w: pl.BlockSpec((tm, w), lambda i: (i, 0))
    blk = pl.BlockSpec((N_CHIPS, tm, FFB), lambda i: (0, i, 0))
    bsd = jax.ShapeDtypeStruct((N_CHIPS, t, FFB), BF16)
    return pl.pallas_call(
        body, name="ffn_bwd", grid=(t // tm,),
        in_specs=[row(D), row(D), _const_spec((1, D)), blk, blk] + _ffn_weight_specs(),
        out_specs=[row(D), blk, blk, blk, pl.BlockSpec((1, D), lambda i: (0, 0))],
        out_shape=[jax.ShapeDtypeStruct((t, D), F32), bsd, bsd, bsd, jax.ShapeDtypeStruct((1, D), F32)],
        compiler_params=_cparams(("arbitrary",)),
    )(dy, xm, g2, a, b, gf, gf, gf)


def _wgrad_blocks(a, b, tt, name):
    _, t, rows = a.shape
    cols = b.shape[1]
    nsteps = t // tt

    def body(a_ref, b_ref, o_ref, acc_ref):
        s = pl.program_id(0)

        @pl.when(s == 0)
        def _():
            acc_ref[...] = jnp.zeros_like(acc_ref)

        bv = b_ref[...].astype(BF16)
        for k in range(N_CHIPS):
            acc_ref[k] += _tn(a_ref[k], bv)

        @pl.when(s == nsteps - 1)
        def _():
            o_ref[...] = acc_ref[...].astype(BF16)

    return pl.pallas_call(
        body, name=name, grid=(nsteps,),
        in_specs=[pl.BlockSpec((N_CHIPS, tt, rows), lambda s: (0, s, 0)), pl.BlockSpec((tt, cols), lambda s: (s, 0))],
        out_specs=pl.BlockSpec((N_CHIPS, rows, cols), lambda s: (0, 0, 0)),
        out_shape=jax.ShapeDtypeStruct((N_CHIPS, rows, cols), BF16),
        scratch_shapes=[pltpu.VMEM((N_CHIPS, rows, cols), F32)],
        compiler_params=_cparams(("arbitrary",)),
    )(a, b)


def _wgrad(a, b, tn, tt, name):
    t, k = a.shape
    n = b.shape[1]
    nsteps = t // tt

    def body(a_ref, b_ref, o_ref):
        @pl.when(pl.program_id(1) == 0)
        def _():
            o_ref[...] = jnp.zeros_like(o_ref)

        o_ref[...] += _tn(a_ref[...].astype(BF16), b_ref[...].astype(BF16))

    return pl.pallas_call(
        body, name=name, grid=(n // tn, nsteps),
        in_specs=[pl.BlockSpec((tt, k), lambda j, s: (s, 0)), pl.BlockSpec((tt, tn), lambda j, s: (s, j))],
        out_specs=pl.BlockSpec((k, tn), lambda j, s: (0, j)),
        out_shape=jax.ShapeDtypeStruct((k, n), F32),
        compiler_params=_cparams(("parallel", "arbitrary")),
    )(a, b)


def _mixer_bwd(dxm, proj, ao, cw, gq, gk, sinks, gco, gao, wo, tq):
    t = proj.shape[0]
    nb = tq // BLK
    r8 = tq // 8
    nt = t // tq
    te = tq + 8
    kvw = 2 * NKV * HP

    def body(dx_ref, dxn_ref, p_ref, cgp_ref, hcp_ref, bgn_ref, cgn_ref, hcn_ref, kvp_ref, ao_ref, cw_ref, gq_ref,
             gk_ref, sk_ref, gco_ref, gao_ref, wo_ref,
             dpm_ref, dkvm_ref, dkvh_ref, dcw_ref, dgq_ref, dgk_ref, dsk_ref, dgco_ref, dgao_ref, acc_ref):
        i = pl.program_id(0)

        @pl.when(i == 0)
        def _():
            for r in (dcw_ref, dgq_ref, dgk_ref, dsk_ref, dgco_ref, dgao_ref):
                r[...] = jnp.zeros_like(r)

        acc_ref[...] = jnp.zeros_like(acc_ref)
        live_rows = jnp.where(i < nt - 1, te, tq)
        dxb = dx_ref[...].astype(BF16)
        dxe = jnp.concatenate([dxb, dxn_ref[...].astype(BF16)], axis=0)
        dcn = _nt(dxe, wo_ref[0:CC, :])
        bg = jnp.concatenate([p_ref[:, O_BG:O_BG + CC], bgn_ref[...]], axis=0)
        cg = jnp.concatenate([p_ref[:, O_CG:O_CG + CC], cgn_ref[...]], axis=0)
        hc = jnp.concatenate([p_ref[:, O_HC:O_HC + CC], hcn_ref[...]], axis=0)
        u = cg * hc
        up = jnp.where(i > 0, cgp_ref[...] * hcp_ref[...], 0.0)
        u1, u2 = _conv_taps(jnp.concatenate([up, u], axis=0), te)
        w0, w1, w2 = cw_ref[0:1, :], cw_ref[1:2, :], cw_ref[2:3, :]
        y = w0 * u2 + w1 * u1 + w2 * u
        co = bg * y
        rc, coh = _rms_fwd(co, 1.0 / CC)
        dco = _rms_bwd(dcn, gco_ref[...], coh, rc, 1.0 / CC)
        row_io = lax.broadcasted_iota(jnp.int32, (te, 1), 0)
        own = row_io < tq
        dgco_ref[...] += jnp.sum(jnp.where(own, dcn * coh, 0.0), axis=0, keepdims=True)
        dyc = jnp.where(row_io < live_rows, dco * bg, 0.0)
        dyo = jnp.where(own, dyc, 0.0)
        dcw_ref[0:1, :] += jnp.sum(dyo * u2, axis=0, keepdims=True)
        dcw_ref[1:2, :] += jnp.sum(dyo * u1, axis=0, keepdims=True)
        dcw_ref[2:3, :] += jnp.sum(dyo * u, axis=0, keepdims=True)
        dy1 = pltpu.roll(dyc, te - 1, 0)[0:tq]
        dy2 = pltpu.roll(dyc, te - 2, 0)[0:tq]
        du = w2 * dyc[0:tq] + w1 * dy1 + w0 * dy2
        dpm_ref[:, O_BG:O_BG + CC] = (dco[0:tq] * y[0:tq]).astype(BF16)
        dpm_ref[:, O_CG:O_CG + CC] = (du * hc[0:tq]).astype(BF16)
        dpm_ref[:, O_HC:O_HC + CC] = (du * cg[0:tq]).astype(BF16)
        kraw = jnp.concatenate([kvp_ref[:, 0:NKV * HP], p_ref[:, O_K:O_K + NKV * HP]], axis=0)
        vraw = jnp.concatenate([kvp_ref[:, NKV * HP:], p_ref[:, O_V:O_V + NKV * HP]], axis=0)
        gqv, gkv = gq_ref[...], gk_ref[...]
        keys = _norm_keys(kraw, gkv)
        vb = [vraw[:, h * HP:(h + 1) * HP].astype(BF16) for h in range(NKV)]
        base_valid, c_io = _band_mask()
        lane = lax.broadcasted_iota(jnp.int32, (1, HP), 1)
        dgq, dgk, dsk = (jnp.zeros((1, HP), F32) for _ in range(3))
        dgao = jnp.zeros((1, NQ * HP), F32)
        for b in range(nb):
            lo = jnp.where(i * nb + b == 0, BLK, 0)
            valid = base_valid & (c_io >= lo)
            band = slice(b * BLK, b * BLK + 2 * BLK)
            blk = slice(b * BLK, (b + 1) * BLK)
            ra, aoh = _rms_fwd(ao_ref[blk, :], 1.0 / (NQ * HD))
            danb = _nt(dxb[blk], wo_ref[CC:MIXW, :])
            dgao = dgao + jnp.sum(danb * aoh, axis=0, keepdims=True)
            dao = _rms_bwd(danb, gao_ref[...], aoh, ra, 1.0 / (NQ * HD))
            fwd = []
            for g in range(NQ):
                rq, qh = _rms_fwd(p_ref[blk, O_Q + g * HP:O_Q + (g + 1) * HP], 1.0 / HD)
                qs = (qh * (gqv * SCALE)).astype(BF16)
                fwd.append((rq, qh, qs) + _attn_probs(qs, keys[g // GRP][2][band], sk_ref[0, g], valid))
            dqs = []
            for h in range(NKV):
                khat, rk, kn = [a[band] for a in keys[h]]
                dss, prbs, qns, dobs = [], [], [], []
                for g in range(h * GRP, (h + 1) * GRP):
                    rq, qh, qs, pr, ps = fwd[g]
                    dob = dao[:, g * HP:(g + 1) * HP].astype(BF16)
                    dp = _nt(dob, vb[h][band])
                    delta = jnp.sum(pr * dp, axis=-1, keepdims=True)
                    dsb = (pr * (dp - delta)).astype(BF16)
                    dsk = dsk + jnp.where(lane == g, -jnp.sum(ps * delta, axis=0, keepdims=True), 0.0)
                    dqn = jnp.dot(dsb, kn, preferred_element_type=F32) * SCALE
                    dgq = dgq + jnp.sum(dqn * qh, axis=0, keepdims=True)
                    dqs.append(_rms_bwd(dqn, gqv, qh, rq, 1.0 / HD).astype(BF16))
                    dss.append(dsb)
                    prbs.append(pr.astype(BF16))
                    qns.append(qs)
                    dobs.append(dob)
                dkn = _tn(jnp.concatenate(dss, axis=0), jnp.concatenate(qns, axis=0))
                dv = _tn(jnp.concatenate(prbs, axis=0), jnp.concatenate(dobs, axis=0))
                dgk = dgk + jnp.sum(dkn * khat, axis=0, keepdims=True)
                acc_ref[band, h * HP:(h + 1) * HP] += _rms_bwd(dkn, gkv, khat, rk, 1.0 / HD)
                acc_ref[band, (NKV + h) * HP:(NKV + h + 1) * HP] += dv
            dpm_ref[blk, O_Q:O_K] = jnp.concatenate(dqs, axis=1)
        dgq_ref[...] += dgq
        dgk_ref[...] += dgk
        dsk_ref[...] += dsk
        dgao_ref[...] += dgao
        dkvh_ref[...] = acc_ref[0:BLK, :]
        dkvm_ref[...] = acc_ref[BLK:, :]

    prev8 = lambda col: pl.BlockSpec((8, CC), lambda i: (jnp.maximum(i * r8 - 1, 0), col))
    next8 = lambda col: pl.BlockSpec((8, CC), lambda i: (jnp.minimum((i + 1) * r8, t // 8 - 1), col))
    small = lambda n: pl.BlockSpec((1, n), lambda i: (0, 0))
    return pl.pallas_call(
        body, name="mixer_bwd", grid=(nt,),
        in_specs=[
            pl.BlockSpec((tq, D), lambda i: (i, 0)),
            pl.BlockSpec((8, D), lambda i: (jnp.minimum((i + 1) * r8, t // 8 - 1), 0)),
            pl.BlockSpec((tq, NP), lambda i: (i, 0)),
            prev8(O_CG // CC), prev8(O_HC // CC),
            next8(O_BG // CC), next8(O_CG // CC), next8(O_HC // CC),
            pl.BlockSpec((BLK, kvw), lambda i: (jnp.maximum(i * nb - 1, 0), O_K // kvw)),
            pl.BlockSpec((tq, NQ * HP), lambda i: (i, 0)),
            _const_spec((8, CC)), _const_spec((1, HP)), _const_spec((1, HP)),
            pl.BlockSpec(memory_space=pltpu.SMEM),
            _const_spec((1, CC)), _const_spec((1, NQ * HP)), _const_spec((MIXW, D)),
        ],
        out_specs=[
            pl.BlockSpec((tq, NMAIN), lambda i: (i, 0)),
            pl.BlockSpec((tq, kvw), lambda i: (i, 0)),
            pl.BlockSpec((BLK, kvw), lambda i: (i, 0)),
            pl.BlockSpec((8, CC), lambda i: (0, 0)), small(HP), small(HP), small(HP), small(CC), small(NQ * HP),
        ],
        out_shape=[
            jax.ShapeDtypeStruct((t, NMAIN), BF16), jax.ShapeDtypeStruct((t, kvw), F32),
            jax.ShapeDtypeStruct((nt * BLK, kvw), F32),
            jax.ShapeDtypeStruct((8, CC), F32), jax.ShapeDtypeStruct((1, HP), F32), jax.ShapeDtypeStruct((1, HP), F32),
            jax.ShapeDtypeStruct((1, HP), F32), jax.ShapeDtypeStruct((1, CC), F32),
            jax.ShapeDtypeStruct((1, NQ * HP), F32),
        ],
        scratch_shapes=[pltpu.VMEM((tq + BLK, kvw), F32)],
        compiler_params=_cparams(("arbitrary",)),
    )(dxm, dxm, proj, proj, proj, proj, proj, proj, proj, ao, cw, gq, gk, sinks, gco, gao, wo)


def _inproj_bwd(dpm, dkv, wpt, x, g1, dxm, tm):
    t = x.shape[0]
    kvw = 2 * NKV * HP

    def body(dp_ref, dk_ref, w_ref, x_ref, g_ref, dxm_ref, dx_ref, dg_ref):
        @pl.when(pl.program_id(0) == 0)
        def _():
            dg_ref[...] = jnp.zeros_like(dg_ref)

        dh = (jnp.dot(dp_ref[...], w_ref[0:NMAIN, :], preferred_element_type=F32)
              + jnp.dot(dk_ref[...], w_ref[NMAIN:NP, :], preferred_element_type=F32))
        r, xh = _rms_fwd(x_ref[...], 1.0 / D)
        dg_ref[...] += jnp.sum(dh * xh, axis=0, keepdims=True)
        dx_ref[...] = dxm_ref[...] + _rms_bwd(dh, g_ref[...], xh, r, 1.0 / D)

    row = lambda w: pl.BlockSpec((tm, w), lambda i: (i, 0))
    return pl.pallas_call(
        body, name="inproj_bwd", grid=(t // tm,),
        in_specs=[row(NMAIN), row(kvw), _const_spec((NP, D)), row(D), _const_spec((1, D)), row(D)],
        out_specs=[row(D), pl.BlockSpec((1, D), lambda i: (0, 0))],
        out_shape=[jax.ShapeDtypeStruct((t, D), F32), jax.ShapeDtypeStruct((1, D), F32)],
        compiler_params=_cparams(("arbitrary",)),
    )(dpm, dkv, wpt, x, g1, dxm)


def _rows_tile(rows, cap=512):
    for cand in range(min(rows, cap) // 16 * 16, 0, -16):
        if rows % cand == 0:
            return cand
    return rows


def _presum_halves(gs, theirs, core):
    n = len(gs)

    def body(c_ref, *refs):
        for g_ref, t_ref, o_ref in zip(refs[:n], refs[n:2 * n], refs[2 * n:]):
            o_ref[...] = (g_ref[...].astype(F32) + t_ref[...].astype(F32)).astype(BF16)

    half = lambda ta: pl.BlockSpec((None,) + ta.shape[1:], lambda k, c_ref: (k, 0, 0))
    own = lambda ta: pl.BlockSpec((None,) + ta.shape[1:], lambda k, c_ref: (k, c_ref[0], 0))
    return pl.pallas_call(
        body, name="presum",
        grid_spec=pltpu.PrefetchScalarGridSpec(
            num_scalar_prefetch=1, grid=(N_CHIPS,),
            in_specs=[own(ta) for ta in theirs] + [half(ta) for ta in theirs],
            out_specs=[half(ta) for ta in theirs]),
        out_shape=[jax.ShapeDtypeStruct(ta.shape, BF16) for ta in theirs],
        compiler_params=_cparams(("parallel",)),
    )(core, *gs, *theirs)


def _sum_chips(cs):
    n = len(cs)
    steps = 2

    def body(*refs):
        for c_ref, o_ref in zip(refs[:n], refs[n:]):
            acc = c_ref[0].astype(F32)
            for j in range(1, N_CHIPS):
                acc = acc + c_ref[j].astype(F32)
            o_ref[...] = acc

    return pl.pallas_call(
        body, name="chipsum", grid=(steps,),
        in_specs=[pl.BlockSpec((N_CHIPS, c.shape[1] // steps, c.shape[2]), lambda i: (0, i, 0)) for c in cs],
        out_specs=[pl.BlockSpec((c.shape[1] // steps, c.shape[2]), lambda i: (i, 0)) for c in cs],
        out_shape=[jax.ShapeDtypeStruct(c.shape[1:], F32) for c in cs],
        compiler_params=_cparams(("parallel",)),
    )(*cs)


def _adamw(w, g, m, v, name):
    rows, cols = w.shape
    tr = _rows_tile(rows, 256)
    c1 = 1.0 - ADAM_B1 ** ADAM_STEP
    c2 = 1.0 - ADAM_B2 ** ADAM_STEP

    def body(w_ref, g_ref, m_ref, v_ref, d_ref, mo_ref, vo_ref):
        gv = g_ref[...]
        mn = ADAM_B1 * m_ref[...] + (1.0 - ADAM_B1) * gv
        vn = ADAM_B2 * v_ref[...] + (1.0 - ADAM_B2) * (gv * gv)
        mo_ref[...] = mn
        vo_ref[...] = vn
        d_ref[...] = -ADAM_LR * ((mn / c1) / (jnp.sqrt(vn / c2) + ADAM_EPS) + ADAM_WD * w_ref[...])

    spec = pl.BlockSpec((tr, cols), lambda i: (i, 0))
    sds = jax.ShapeDtypeStruct((rows, cols), F32)
    return pl.pallas_call(
        body, name=name, grid=(rows // tr,), in_specs=[spec] * 4, out_specs=[spec] * 3, out_shape=[sds] * 3,
        compiler_params=_cparams(("parallel",)),
    )(w, g, m, v)


def _place():
    x, y, c = lax.axis_index("x"), lax.axis_index("y"), lax.axis_index("c")
    chips = [(1 - x, y), (x, 1 - y), (1 - x, 1 - y)]
    return x, y, c, chips


ANY = pl.BlockSpec(memory_space=pl.ANY)
DMA_ROWS = 64


def _pieces(shape):
    rows = shape[-2]
    step = DMA_ROWS if rows % DMA_ROWS == 0 else rows
    lead = [()]
    for n in shape[:-2]:
        lead = [i + (k,) for i in lead for k in range(n)]
    return [i + (pl.ds(r0, step),) for i in lead for r0 in range(0, rows, step)]


def _start_pieces(make, src, dst):
    for idx in _pieces(src.shape):
        make(src.at[idx], dst.at[idx]).start()


def _gather_layer(blocks, layer):
    nw = len(blocks)

    def body(*refs):
        _gather_body(refs[:nw], refs[nw:2 * nw], refs[2 * nw:], layer, _start_pieces)

    return pl.pallas_call(
        body, name=f"gather_layer{layer}", in_specs=[ANY] * nw, out_specs=[ANY] * nw,
        out_shape=[jax.ShapeDtypeStruct((N_CHIPS,) + b.shape, b.dtype) for b in blocks],
        scratch_shapes=[pltpu.SemaphoreType.DMA((3, nw))] * 4,
        compiler_params=_cparams(has_side_effects=True),
    )(*blocks)


def _gather_body(srcs, outs, sems, layer, start):
    nw = len(srcs)
    ssem, rsem, fssem, frsem = sems
    x, y, c, chips = _place()
    kme = 2 * x + y

    def plane(j, w, to):
        return lambda s, d: pltpu.make_async_remote_copy(
            src_ref=s, dst_ref=d, send_sem=ssem.at[j, w], recv_sem=rsem.at[j, w], device_id=to,
            device_id_type=MESH)

    def passed(j, w):
        return lambda s, d: pltpu.make_async_remote_copy(
            src_ref=s, dst_ref=d, send_sem=fssem.at[j, w], recv_sem=frsem.at[j, w],
            device_id=(x, y, 1 - c), device_id_type=MESH)

    @pl.when(c == layer)
    def _():
        for j, (px, py) in enumerate(chips):
            for w in range(nw):
                start(plane(j, w, (px, py, c)), srcs[w], outs[w].at[kme])
        for j, (px, py) in enumerate(chips):
            for w in range(nw):
                got = outs[w].at[2 * px + py]
                plane(j, w, (px, py, c))(got, got).wait_recv()
                start(passed(j, w), got, got)
        for j, (px, py) in enumerate(chips):
            for w in range(nw):
                got = outs[w].at[2 * px + py]
                plane(j, w, (px, py, c))(got, got).wait_send()
                passed(j, w)(got, got).wait_send()

    @pl.when(c != layer)
    def _():
        for j, (px, py) in enumerate(chips):
            for w in range(nw):
                got = outs[w].at[2 * px + py]
                passed(j, w)(got, got).wait_recv()


def _handshake_all():
    x, y, c, _ = _place()
    barrier = pltpu.get_barrier_semaphore()
    for r in range(1, 8):
        peer = (x ^ (r >> 2), y ^ ((r >> 1) & 1), c ^ (r & 1))
        pl.semaphore_signal(barrier, inc=1, device_id=peer, device_id_type=MESH)
    pl.semaphore_wait(barrier, 7)


def _gather_layer_async(blocks, layer, name, collective_id):
    hbm = pltpu.MemorySpace.HBM
    srcs = [jax.new_ref(b, memory_space=hbm) for b in blocks]
    outs = [jax.empty_ref(jax.ShapeDtypeStruct((N_CHIPS,) + b.shape, b.dtype), memory_space=hbm) for b in blocks]

    @pl.kernel(mesh=plsc.ScalarSubcoreMesh(axis_name="seq", num_cores=1), name=name,
               scratch_types=[pltpu.SemaphoreType.DMA((3, len(blocks)))] * 4,
               compiler_params=pltpu.CompilerParams(collective_id=collective_id))
    def launch(*sems):
        _handshake_all()
        _gather_body(srcs, outs, sems, layer, lambda make, s, d: make(s, d).start())

    launch()
    return [o[...] for o in outs]


def _swap_halves(gs):
    nw = len(gs)

    def body(*refs):
        srcs, theirs = refs[:nw], refs[nw:2 * nw]
        ssem, rsem = refs[2 * nw:]
        x, y, c, _ = _place()

        def give(w):
            return lambda s, d: pltpu.make_async_remote_copy(
                src_ref=s, dst_ref=d, send_sem=ssem.at[w], recv_sem=rsem.at[w], device_id=(x, y, 1 - c),
                device_id_type=MESH)

        for w in range(nw):
            hr = theirs[w].shape[1]
            _start_pieces(give(w), srcs[w].at[:, pl.ds((1 - c) * hr, hr)], theirs[w])
        for w in range(nw):
            give(w)(theirs[w], theirs[w]).wait()

    return pl.pallas_call(
        body, name="swap_halves", in_specs=[ANY] * nw, out_specs=[ANY] * nw,
        out_shape=[jax.ShapeDtypeStruct((g.shape[0], g.shape[1] // 2, g.shape[2]), g.dtype) for g in gs],
        scratch_shapes=[pltpu.SemaphoreType.DMA((nw,))] * 2,
        compiler_params=_cparams(has_side_effects=True),
    )(*gs)


def _scatter_chips(ps):
    nw = len(ps)

    def body(*refs):
        _scatter_body(refs[:nw], refs[nw:2 * nw], refs[2 * nw:], _start_pieces)

    return pl.pallas_call(
        body, name="scatter_chips", in_specs=[ANY] * nw, out_specs=[ANY] * nw,
        out_shape=[jax.ShapeDtypeStruct(p.shape, p.dtype) for p in ps],
        scratch_shapes=[pltpu.SemaphoreType.DMA((3, nw)), pltpu.SemaphoreType.DMA((3, nw))],
        compiler_params=_cparams(has_side_effects=True),
    )(*ps)


def _scatter_body(srcs, outs, sems, start):
    nw = len(srcs)
    ssem, rsem = sems
    x, y, c, chips = _place()
    kme = 2 * x + y

    def give(j, w, to):
        return lambda s, d: pltpu.make_async_remote_copy(
            src_ref=s, dst_ref=d, send_sem=ssem.at[j, w], recv_sem=rsem.at[j, w], device_id=to,
            device_id_type=MESH)

    for j, (px, py) in enumerate(chips):
        for w in range(nw):
            start(give(j, w, (px, py, c)), srcs[w].at[2 * px + py], outs[w].at[kme])
    for j, (px, py) in enumerate(chips):
        for w in range(nw):
            got = outs[w].at[2 * px + py]
            give(j, w, (px, py, c))(got, got).wait_recv()
    for j, (px, py) in enumerate(chips):
        for w in range(nw):
            sent = srcs[w].at[2 * px + py]
            give(j, w, (px, py, c))(sent, sent).wait_send()


def _scatter_chips_async(ps, name, collective_id):
    hbm = pltpu.MemorySpace.HBM
    srcs = [jax.new_ref(p, memory_space=hbm) for p in ps]
    outs = [jax.empty_ref(jax.ShapeDtypeStruct(p.shape, p.dtype), memory_space=hbm) for p in ps]

    @pl.kernel(mesh=plsc.ScalarSubcoreMesh(axis_name="seq", num_cores=1), name=name,
               scratch_types=[pltpu.SemaphoreType.DMA((3, len(ps)))] * 2,
               compiler_params=pltpu.CompilerParams(collective_id=collective_id))
    def launch(*sems):
        _handshake_all()
        _scatter_body(srcs, outs, sems, lambda make, s, d: make(s, d).start())

    launch()
    return [o[...] for o in outs]


def _swap_siblings(rs):
    nw = len(rs)

    def body(*refs):
        srcs, outs = refs[:nw], refs[nw:2 * nw]
        ssem, rsem = refs[2 * nw:]
        x, y, c, _ = _place()

        def give(w):
            return lambda s, d: pltpu.make_async_remote_copy(
                src_ref=s, dst_ref=d, send_sem=ssem.at[w], recv_sem=rsem.at[w], device_id=(x, y, 1 - c),
                device_id_type=MESH)

        for w in range(nw):
            _start_pieces(give(w), srcs[w], outs[w])
        for w in range(nw):
            give(w)(srcs[w], outs[w]).wait()

    return pl.pallas_call(
        body, name="swap_siblings", in_specs=[ANY] * nw, out_specs=[ANY] * nw,
        out_shape=[jax.ShapeDtypeStruct(r.shape, r.dtype) for r in rs],
        scratch_shapes=[pltpu.SemaphoreType.DMA((nw,))] * 2,
        compiler_params=_cparams(has_side_effects=True),
    )(*rs)


def _allreduce_small(v):
    rows = v.shape[0]

    def body(v_ref, o_ref, buf, ssem, rsem):
        x, y, c, _ = _place()
        me = 4 * x + 2 * y + c
        buf[me] = v_ref[...]
        sends = []
        for r in range(1, 8):
            peer = (x ^ (r >> 2), y ^ ((r >> 1) & 1), c ^ (r & 1))
            cp = pltpu.make_async_remote_copy(
                src_ref=v_ref, dst_ref=buf.at[me], send_sem=ssem.at[r - 1], recv_sem=rsem.at[r - 1],
                device_id=peer, device_id_type=MESH)
            cp.start()
            sends.append(cp)
        for r in range(1, 8):
            src = me ^ r
            pltpu.make_async_remote_copy(
                src_ref=v_ref, dst_ref=buf.at[src], send_sem=ssem.at[r - 1], recv_sem=rsem.at[r - 1],
                device_id=(x, y, c), device_id_type=MESH).wait_recv()
        for cp in sends:
            cp.wait_send()
        acc = buf[0]
        for d in range(1, 8):
            acc = acc + buf[d]
        o_ref[...] = acc

    vm = pl.BlockSpec(memory_space=pltpu.VMEM)
    return pl.pallas_call(
        body, name="allreduce_small", in_specs=[vm], out_specs=vm,
        out_shape=jax.ShapeDtypeStruct(v.shape, F32),
        scratch_shapes=[pltpu.VMEM((8, rows, 128), F32), pltpu.SemaphoreType.DMA((7,)),
                        pltpu.SemaphoreType.DMA((7,))],
        compiler_params=_cparams(has_side_effects=True),
    )(v)


def _pad_heads(w, n_heads, axis):
    shp = w.shape
    w = w.reshape(shp[:axis] + (n_heads, HD) + shp[axis + 1:])
    pad = [(0, 0)] * w.ndim
    pad[axis + 1] = (0, HP - HD)
    w = jnp.pad(w, pad)
    return w.reshape(shp[:axis] + (n_heads * HP,) + shp[axis + 1:])


def _strip_heads(w, n_heads, axis):
    shp = w.shape
    w = w.reshape(shp[:axis] + (n_heads, HP) + shp[axis + 1:])
    w = lax.slice_in_dim(w, 0, HD, axis=axis + 1)
    return w.reshape(shp[:axis] + (n_heads * HD,) + shp[axis + 1:])


def _pad_win_t(wint):
    parts = [wint[:3 * CC], _pad_heads(wint[3 * CC:3 * CC + NQ * HD], NQ, 0),
             _pad_heads(wint[3 * CC + NQ * HD:3 * CC + (NQ + NKV) * HD], NKV, 0),
             _pad_heads(wint[3 * CC + (NQ + NKV) * HD:], NKV, 0)]
    return jnp.concatenate(parts, axis=0)


def _strip_win_t(gpt):
    parts = [gpt[:3 * CC], _strip_heads(gpt[O_Q:O_K], NQ, 0), _strip_heads(gpt[O_K:O_V], NKV, 0),
             _strip_heads(gpt[O_V:], NKV, 0)]
    return jnp.concatenate(parts, axis=0)


def _t(w):
    return jnp.swapaxes(w, -1, -2)


def _count(shape):
    n = 1
    for s in shape:
        n *= s
    return n


def _pack_rows(arrs):
    flat = [jnp.pad(a.reshape(-1), (0, (-_count(a.shape)) % 128)) for a in arrs]
    v = jnp.concatenate(flat)
    rows = -(-v.shape[0] // (8 * 128)) * 8
    return jnp.pad(v, (0, rows * 128 - v.shape[0])).reshape(rows, 128)


def kernel(x, norm1_g, w_in, conv_w, q_norm_g, k_norm_g, sinks, conv_out_g, attn_out_g, w_o, norm2_g, w_gate, w_up, w_down, loss_target, m_norm1_g, m_w_in, m_conv_w, m_q_norm_g, m_k_norm_g, m_sinks, m_conv_out_g, m_attn_out_g, m_w_o, m_norm2_g, m_w_gate, m_w_up, m_w_down, v_norm1_g, v_w_in, v_conv_w, v_q_norm_g, v_k_norm_g, v_sinks, v_conv_out_g, v_attn_out_g, v_w_o, v_norm2_g, v_w_gate, v_w_up, v_w_down):
    depth = w_in.shape[0]
    t = x.shape[1]
    xs = x.reshape(t, D)
    tgt = loss_target.reshape(t, D)
    xi, yi = lax.axis_index("x"), lax.axis_index("y")
    kme = 2 * xi + yi
    tm = min(512, t)
    tq = min(512, t)
    tf = min(256, t)
    tw = min(1024, t)

    cwp = jnp.pad(conv_w.reshape(depth * 3, CC // N_CHIPS), ((0, 8 - depth * 3), (0, 0)))
    own_f = [jnp.concatenate([_t(w_gate[l]), _t(w_up[l]), w_down[l]], axis=0).astype(BF16) for l in range(depth)]
    own_o = [w_o[l].astype(BF16) for l in range(depth)]
    own_i = [_t(w_in[l]).astype(BF16) for l in range(depth)]
    mine = lambda got, own: lax.dynamic_update_index_in_dim(got, own, kme, 0)
    got_i0, got_o0, got_cw = _gather_layer([own_i[0], own_o[0], cwp], 0)
    gf0_in = lax.optimization_barrier((own_f[0], got_i0))[0]
    (got_f0,) = _gather_layer_async([gf0_in], 0, "gather_ffn0_seq", collective_id=6)
    cw_full = mine(got_cw, cwp).transpose(1, 0, 2).reshape(8, CC)[:depth * 3].reshape(depth, 3, CC)

    def layer_params(l, got_i, got_o):
        wo = mine(got_o, own_o[l]).reshape(D, D)
        return dict(
            wpt=_pad_win_t(mine(got_i, own_i[l]).reshape(N_CHIPS * 576, D)),
            wo=jnp.concatenate([wo[:CC], _pad_heads(wo[CC:], NQ, 0)], axis=0),
            cw=jnp.pad(cw_full[l], ((0, 5), (0, 0))),
            g1=norm1_g[l].reshape(1, D), g2=norm2_g[l].reshape(1, D),
            gq=jnp.pad(q_norm_g[l], (0, HP - HD)).reshape(1, HP), gk=jnp.pad(k_norm_g[l], (0, HP - HD)).reshape(1, HP),
            sk=sinks[l].reshape(1, NQ), gco=conv_out_g[l].reshape(1, CC),
            gao=_pad_heads(attn_out_g[l], NQ, 0).reshape(1, NQ * HP))

    saved, layers = [], []
    cur = xs
    for l in range(depth):
        x_in = cur
        if l == 0:
            p = layer_params(0, got_i0, got_o0)
        else:
            got_f1, got_o1, got_i1 = lax.optimization_barrier((got_l1, cur))[0]
            p = layer_params(1, got_i1, got_o1)
        proj, h = _inproj_fwd(cur, p["g1"], p["wpt"], tm)
        xm, mix, ao = _mixer_fwd(proj, cur, p["cw"], p["gq"], p["gk"], p["sk"], p["gco"], p["gao"], p["wo"], tq)
        if l == 0:
            got_f0 = lax.optimization_barrier((got_f0, xm))[0]
            l1_in = lax.optimization_barrier(([own_f[1], own_o[1], own_i[1]], got_f0))[0]
            got_l1 = _gather_layer_async(l1_in, 1, "gather_layer1_seq", collective_id=1)
        p["gf"] = mine(got_f0 if l == 0 else got_f1, own_f[l])
        layers.append(p)
        if l < depth - 1:
            cur, a, b, h2 = _ffn_fwd(xm, p["g2"], p["gf"], tm)
        else:
            lpart, dy, a, b, h2 = _ffn_fwd(xm, p["g2"], p["gf"], tm, tgt)
        saved.append(dict(x=x_in, proj=proj, h=h, xm=xm, mix=mix, ao=ao, a=a, b=b, h2=h2))
    loss = lax.psum(lpart[0, 0], ("x", "y", "c"))

    nt = t // tq
    ci = lax.axis_index("c")
    core = ci.reshape(1).astype(jnp.int32)
    rbig = [dict() for _ in range(depth)]
    gsmall = [None] * depth

    def reduce_start(gs, name, collective_id):
        ps = _presum_halves(gs, _swap_halves(gs), core)
        got = _scatter_chips(ps) if collective_id is None else _scatter_chips_async(ps, name, collective_id)
        return ps, got

    def reduce_finish(started, after):
        ps, got = started
        if after is not None:
            got = lax.optimization_barrier((got, after))[0]
        cs = [lax.dynamic_update_index_in_dim(g, lax.dynamic_index_in_dim(q, kme, 0, keepdims=False), kme, 0)
              for g, q in zip(got, ps)]
        r_mine = _sum_chips(cs)
        return [jnp.where(ci == 0, jnp.concatenate([a, b], axis=0), jnp.concatenate([b, a], axis=0))
                for a, b in zip(r_mine, _swap_siblings(r_mine))]

    in_flight = None
    for l in reversed(range(depth)):
        p, s = layers[l], saved[l]
        dxm, da, db, hm, dg2 = _ffn_bwd(dy, s["xm"], p["g2"], s["a"], s["b"], p["gf"], tf)
        g_wg = _wgrad_blocks(da, s["h2"], tw, "wgrad_gate")
        g_wu = _wgrad_blocks(db, s["h2"], tw, "wgrad_up")
        g_wd = _wgrad_blocks(hm, dy, tw, "wgrad_down")
        if in_flight is not None:
            rbig[l + 1]["in"], rbig[l + 1]["o"] = reduce_finish(in_flight, g_wd)
        ffn_flight = reduce_start([g_wg, g_wu, g_wd], f"scatter_ffn{l}_seq", 2 + 2 * l)
        dpm, dkvm, dkvh, dcw, dgq, dgk, dsk, dgco, dgao = _mixer_bwd(
            dxm, s["proj"], s["ao"], p["cw"], p["gq"], p["gk"], p["sk"], p["gco"], p["gao"], p["wo"], tq)
        g_wo = _wgrad(s["mix"], dxm, D, tw, "wgrad_o")
        kvw = dkvm.shape[1]
        halo = jnp.concatenate([dkvh.reshape(nt, BLK, kvw)[1:], jnp.zeros((1, BLK, kvw), F32)], axis=0)
        halo = jnp.pad(halo, ((0, 0), (tq - BLK, 0), (0, 0)))
        dkv = (dkvm.reshape(nt, tq, kvw) + halo).reshape(t, kvw).astype(BF16)
        dx, dg1 = _inproj_bwd(dpm, dkv, p["wpt"], s["x"], p["g1"], dxm, tm)
        g_wpm = _wgrad(dpm, s["h"], D, tw, "wgrad_in_main")
        g_wpk = _wgrad(dkv, s["h"], D, tw, "wgrad_in_kv")
        dy = dx
        g_in = _strip_win_t(jnp.concatenate([g_wpm, g_wpk], axis=0)).astype(BF16)
        g_o = jnp.concatenate([g_wo[:CC], _strip_heads(g_wo[CC:], NQ, 0)], axis=0).astype(BF16)
        gsmall[l] = dict(g1=dg1, cw=dcw[:3], gq=dgq[0, :HD], gk=dgk[0, :HD], sk=dsk[0, :NQ], gco=dgco,
                         gao=_strip_heads(dgao.reshape(NQ * HP), NQ, 0), g2=dg2)
        rbig[l]["g"], rbig[l]["u"], rbig[l]["d"] = reduce_finish(ffn_flight, dx)
        in_flight = reduce_start([g_in.reshape(N_CHIPS, -1, D), g_o.reshape(N_CHIPS, -1, D)],
                                 f"scatter_in{l}_seq", 3 + 2 * l)
    grad_x = dy.reshape(x.shape)

    small_shapes = dict(g1=(D,), cw=(3, CC), gq=(HD,), gk=(HD,), sk=(NQ,), gco=(CC,), gao=(NQ * HD,), g2=(D,))
    red = _allreduce_small(_pack_rows([gsmall[l][n] for l in range(depth) for n in small_shapes])).reshape(-1)
    red_small, offs = {n: [] for n in small_shapes}, 0
    for l in range(depth):
        for n, shp in small_shapes.items():
            cnt = _count(shp)
            red_small[n].append(red[offs:offs + cnt].reshape(shp))
            offs += -(-cnt // 128) * 128
    g_small = {n: jnp.stack(v) for n, v in red_small.items()}
    g_cw = lax.dynamic_slice_in_dim(g_small["cw"], kme * (CC // N_CHIPS), CC // N_CHIPS, axis=2)

    weights = [norm1_g, w_in, conv_w, q_norm_g, k_norm_g, sinks, conv_out_g, attn_out_g, w_o, norm2_g, w_gate,
               w_up, w_down]
    moms = [m_norm1_g, m_w_in, m_conv_w, m_q_norm_g, m_k_norm_g, m_sinks, m_conv_out_g, m_attn_out_g, m_w_o,
            m_norm2_g, m_w_gate, m_w_up, m_w_down]
    vars_ = [v_norm1_g, v_w_in, v_conv_w, v_q_norm_g, v_k_norm_g, v_sinks, v_conv_out_g, v_attn_out_g, v_w_o,
             v_norm2_g, v_w_gate, v_w_up, v_w_down]
    n_w = len(weights)
    big_idx = dict(zip(("in", "o", "g", "u", "d"), (1, 8, 10, 11, 12)))
    small_idx = [n for n in range(n_w) if n not in big_idx.values()]
    grads, deltas, new_m, new_v = [None] * n_w, [None] * n_w, [None] * n_w, [None] * n_w
    for n, g in zip(small_idx, (g_small["g1"], g_cw, g_small["gq"], g_small["gk"], g_small["sk"], g_small["gco"],
                                g_small["gao"], g_small["g2"])):
        grads[n] = g

    def update_big(name):
        n = big_idx[name]
        g = jnp.stack([rbig[l][name] for l in range(depth)])
        flip = g.shape != weights[n].shape
        rows2d = lambda a3: (_t(a3) if flip else a3).reshape(-1, D)
        res = _adamw(rows2d(weights[n]), g.reshape(-1, D), rows2d(moms[n]), rows2d(vars_[n]), f"adamw_{n}")
        res = [g] + [r.reshape(g.shape) for r in res]
        grads[n], deltas[n], new_m[n], new_v[n] = [_t(r) for r in res] if flip else res

    for name in ("g", "u", "d"):
        update_big(name)
    rbig[0]["in"], rbig[0]["o"] = reduce_finish(in_flight, new_v[big_idx["d"]])
    for name in ("in", "o"):
        update_big(name)
    res = _adamw(*[_pack_rows([arrs[n] for n in small_idx]) for arrs in (weights, grads, moms, vars_)],
                 "adamw_small")
    offs = 0
    for n in small_idx:
        shp = weights[n].shape
        cnt = _count(shp)
        deltas[n], new_m[n], new_v[n] = [r.reshape(-1)[offs:offs + cnt].reshape(shp) for r in res]
        offs += -(-cnt // 128) * 128
    return (loss, grad_x, *grads, *deltas, *new_m, *new_v)
```

```python
import functools

import jax
import jax.numpy as jnp
from jax import lax
from jax.experimental import pallas as pl
from jax.experimental.pallas import tpu as pltpu
from jax.experimental.pallas import tpu_sc as plsc

F32 = jnp.float32
BF16 = jnp.bfloat16

D = 1024
CC = 512
NQ = 8
NKV = 2
HD = 64
HP = 128
GRP = NQ // NKV
FF = 2816
FFB = FF // 4
BLK = 128
EPS = 1e-6
NEG = -1e30
SCALE = HD ** -0.5
O_BG, O_CG, O_HC, O_Q = 0, CC, 2 * CC, 3 * CC
O_K = O_Q + NQ * HP
O_V = O_K + NKV * HP
NP = O_V + NKV * HP
NMAIN = O_K
MIXW = CC + NQ * HP
N_CHIPS = 4
VMEM_LIMIT = 56 * 1024 * 1024
MESH = pl.DeviceIdType.MESH

ADAM_LR, ADAM_B1, ADAM_B2, ADAM_EPS, ADAM_WD, ADAM_STEP = 0.001, 0.9, 0.999, 1e-08, 0.01, 10


def _cparams(sem=None, **kw):
    if sem is not None:
        kw["dimension_semantics"] = sem
    return pltpu.CompilerParams(vmem_limit_bytes=VMEM_LIMIT, **kw)


def _const_spec(shape):
    nd = len(shape)
    return pl.BlockSpec(shape, lambda *_: (0,) * nd, pipeline_mode=pl.Buffered(1))


def _nt(a, b):
    return lax.dot_general(a, b, (((1,), (1,)), ((), ())), preferred_element_type=F32)


def _tn(a, b):
    return lax.dot_general(a, b, (((0,), (0,)), ((), ())), preferred_element_type=F32)


def _rms_fwd(x, inv_n):
    r = lax.rsqrt(jnp.sum(x * x, axis=-1, keepdims=True) * inv_n + EPS)
    return r, x * r


def _rms_bwd(dy, g, xh, r, inv_n):
    dxh = dy * g
    return r * (dxh - xh * (jnp.sum(dxh * xh, axis=-1, keepdims=True) * inv_n))


def _inproj_fwd(x, g1, wpt, tm):
    t = x.shape[0]

    def body(x_ref, g_ref, w_ref, p_ref, h_ref):
        _, xh = _rms_fwd(x_ref[...], 1.0 / D)
        h = (xh * g_ref[...]).astype(BF16)
        h_ref[...] = h
        p_ref[...] = _nt(h, w_ref[...])

    return pl.pallas_call(
        body, name="inproj_fwd", grid=(t // tm,),
        in_specs=[pl.BlockSpec((tm, D), lambda i: (i, 0)), _const_spec((1, D)), _const_spec((NP, D))],
        out_specs=[pl.BlockSpec((tm, NP), lambda i: (i, 0)), pl.BlockSpec((tm, D), lambda i: (i, 0))],
        out_shape=[jax.ShapeDtypeStruct((t, NP), F32), jax.ShapeDtypeStruct((t, D), BF16)],
        compiler_params=_cparams(("parallel",)),
    )(x, g1, wpt)


def _band_mask():
    r_io = lax.broadcasted_iota(jnp.int32, (BLK, 2 * BLK), 0)
    c_io = lax.broadcasted_iota(jnp.int32, (BLK, 2 * BLK), 1)
    return (c_io > r_io) & (c_io <= r_io + BLK), c_io


def _conv_taps(uf, n):
    u1 = pltpu.roll(uf, 1, 0)[8:8 + n]
    u2 = pltpu.roll(uf, 2, 0)[8:8 + n]
    return u1, u2


def _attn_probs(qs, kband, sink, valid):
    s = jnp.where(valid, _nt(qs, kband), NEG)
    m = jnp.maximum(jnp.max(s, axis=-1, keepdims=True), sink)
    p = jnp.exp(s - m)
    es = jnp.exp(sink - m)
    inv = 1.0 / (jnp.sum(p, axis=-1, keepdims=True) + es)
    return p * inv, es * inv


def _norm_keys(kraw, gk):
    out = []
    for h in range(NKV):
        kh = kraw[:, h * HP:(h + 1) * HP]
        rk, khat = _rms_fwd(kh, 1.0 / HD)
        out.append((khat, rk, (khat * gk).astype(BF16)))
    return out


def _mixer_fwd(proj, x, cw, gq, gk, sinks, gco, gao, wo, tq):
    t = proj.shape[0]
    nb = tq // BLK
    r8 = tq // 8

    def body(p_ref, cgp_ref, hcp_ref, kvp_ref, x_ref, cw_ref, gq_ref, gk_ref, sk_ref, gco_ref, gao_ref,
             wo_ref, xm_ref, mix_ref, ao_ref):
        i = pl.program_id(0)
        cg = p_ref[:, O_CG:O_CG + CC]
        hc = p_ref[:, O_HC:O_HC + CC]
        u = cg * hc
        up = jnp.where(i > 0, cgp_ref[...] * hcp_ref[...], 0.0)
        u1, u2 = _conv_taps(jnp.concatenate([up, u], axis=0), tq)
        y = cw_ref[0:1, :] * u2 + cw_ref[1:2, :] * u1 + cw_ref[2:3, :] * u
        co = p_ref[:, O_BG:O_BG + CC] * y
        _, coh = _rms_fwd(co, 1.0 / CC)
        cn = coh * gco_ref[...]
        kraw = jnp.concatenate([kvp_ref[:, 0:NKV * HP], p_ref[:, O_K:O_K + NKV * HP]], axis=0)
        vraw = jnp.concatenate([kvp_ref[:, NKV * HP:], p_ref[:, O_V:O_V + NKV * HP]], axis=0)
        keys = _norm_keys(kraw, gk_ref[...])
        vb = [vraw[:, h * HP:(h + 1) * HP].astype(BF16) for h in range(NKV)]
        base_valid, c_io = _band_mask()
        gqs = gq_ref[...] * SCALE
        for b in range(nb):
            lo = jnp.where(i * nb + b == 0, BLK, 0)
            valid = base_valid & (c_io >= lo)
            for g in range(NQ):
                h = g // GRP
                qg = p_ref[b * BLK:(b + 1) * BLK, O_Q + g * HP:O_Q + (g + 1) * HP]
                _, qh = _rms_fwd(qg, 1.0 / HD)
                qs = (qh * gqs).astype(BF16)
                pr, _ = _attn_probs(qs, keys[h][2][b * BLK:b * BLK + 2 * BLK], sk_ref[0, g], valid)
                ao_ref[b * BLK:(b + 1) * BLK, g * HP:(g + 1) * HP] = jnp.dot(
                    pr.astype(BF16), vb[h][b * BLK:b * BLK + 2 * BLK], preferred_element_type=F32)
        _, aoh = _rms_fwd(ao_ref[...], 1.0 / (NQ * HD))
        an = aoh * gao_ref[...]
        mix = jnp.concatenate([cn, an], axis=1).astype(BF16)
        mix_ref[...] = mix
        xm_ref[...] = x_ref[...] + jnp.dot(mix, wo_ref[...], preferred_element_type=F32)

    prev8 = lambda col: pl.BlockSpec((8, CC), lambda i: (jnp.maximum(i * r8 - 1, 0), col))
    return pl.pallas_call(
        body, name="mixer_fwd", grid=(t // tq,),
        in_specs=[
            pl.BlockSpec((tq, NP), lambda i: (i, 0)),
            prev8(O_CG // CC), prev8(O_HC // CC),
            pl.BlockSpec((BLK, 2 * NKV * HP), lambda i: (jnp.maximum(i * nb - 1, 0), O_K // (2 * NKV * HP))),
            pl.BlockSpec((tq, D), lambda i: (i, 0)),
            _const_spec((8, CC)), _const_spec((1, HP)), _const_spec((1, HP)),
            pl.BlockSpec(memory_space=pltpu.SMEM),
            _const_spec((1, CC)), _const_spec((1, NQ * HP)), _const_spec((MIXW, D)),
        ],
        out_specs=[pl.BlockSpec((tq, D), lambda i: (i, 0)), pl.BlockSpec((tq, MIXW), lambda i: (i, 0)),
                   pl.BlockSpec((tq, NQ * HP), lambda i: (i, 0))],
        out_shape=[jax.ShapeDtypeStruct((t, D), F32), jax.ShapeDtypeStruct((t, MIXW), BF16),
                   jax.ShapeDtypeStruct((t, NQ * HP), F32)],
        compiler_params=_cparams(("parallel",)),
    )(proj, proj, proj, proj, x, cw, gq, gk, sinks, gco, gao, wo)


def _ffn_weight_specs():
    return [pl.BlockSpec((N_CHIPS, FFB, D), lambda i, j=j: (0, j, 0), pipeline_mode=pl.Buffered(1))
            for j in range(3)]


def _ffn_fwd(xm, g2, gf, tm, tgt=None):
    t = xm.shape[0]
    last = tgt is not None

    def body(x_ref, g_ref, wg_ref, wu_ref, wd_ref, *rest):
        t_ref, rest = (rest[0], rest[1:]) if last else (None, rest)
        l_ref, rest = (rest[0], rest[1:]) if last else (None, rest)
        xo_ref, a_ref, b_ref, h2_ref = rest
        xv = x_ref[...]
        _, xh = _rms_fwd(xv, 1.0 / D)
        h2 = (xh * g_ref[...]).astype(BF16)
        h2_ref[...] = h2
        acc = xv
        for k in range(N_CHIPS):
            a = _nt(h2, wg_ref[k])
            b = _nt(h2, wu_ref[k])
            a_ref[k] = a.astype(BF16)
            b_ref[k] = b.astype(BF16)
            hm = (a * jax.nn.sigmoid(a) * b).astype(BF16)
            acc = acc + jnp.dot(hm, wd_ref[k], preferred_element_type=F32)
        if last:
            @pl.when(pl.program_id(0) == 0)
            def _():
                l_ref[...] = jnp.zeros_like(l_ref)

            e = acc - t_ref[...]
            xo_ref[...] = e * (1.0 / D)
            l_ref[...] += jnp.sum(jnp.sum(e * e, axis=-1, keepdims=True), axis=0, keepdims=True) * (0.5 / D)
        else:
            xo_ref[...] = acc

    row = lambda w: pl.BlockSpec((tm, w), lambda i: (i, 0))
    blk = pl.BlockSpec((N_CHIPS, tm, FFB), lambda i: (0, i, 0))
    bsd = jax.ShapeDtypeStruct((N_CHIPS, t, FFB), BF16)
    return pl.pallas_call(
        body, name="ffn_fwd_loss" if last else "ffn_fwd", grid=(t // tm,),
        in_specs=[row(D), _const_spec((1, D))] + _ffn_weight_specs() + ([row(D)] if last else []),
        out_specs=([pl.BlockSpec((8, 128), lambda i: (0, 0))] if last else []) + [row(D), blk, blk, row(D)],
        out_shape=([jax.ShapeDtypeStruct((8, 128), F32)] if last else [])
        + [jax.ShapeDtypeStruct((t, D), F32), bsd, bsd, jax.ShapeDtypeStruct((t, D), BF16)],
        compiler_params=_cparams(("arbitrary" if last else "parallel",)),
    )(*((xm, g2, gf, gf, gf) + ((tgt,) if last else ())))


def _ffn_bwd(dy, xm, g2, a, b, gf, tm):
    t = dy.shape[0]

    def body(dy_ref, x_ref, g_ref, a_ref, b_ref, wg_ref, wu_ref, wd_ref, dx_ref, da_ref, db_ref, hm_ref, dg_ref):
        @pl.when(pl.program_id(0) == 0)
        def _():
            dg_ref[...] = jnp.zeros_like(dg_ref)

        dyv = dy_ref[...]
        dyb = dyv.astype(BF16)
        for k in range(N_CHIPS):
            dhm = _nt(dyb, wd_ref[k])
            av = a_ref[k].astype(F32)
            bv = b_ref[k].astype(F32)
            sig = jax.nn.sigmoid(av)
            sil = av * sig
            hm_ref[k] = (sil * bv).astype(BF16)
            da_ref[k] = (dhm * bv * (sig * (1.0 + av * (1.0 - sig)))).astype(BF16)
            db_ref[k] = (dhm * sil).astype(BF16)
        dh2 = jnp.zeros_like(dyv)
        for k in range(N_CHIPS):
            dh2 = (dh2 + jnp.dot(da_ref[k], wg_ref[k], preferred_element_type=F32)
                   + jnp.dot(db_ref[k], wu_ref[k], preferred_element_type=F32))
        r, xh = _rms_fwd(x_ref[...], 1.0 / D)
        dg_ref[...] += jnp.sum(dh2 * xh, axis=0, keepdims=True)
        dx_ref[...] = dyv + _rms_bwd(dh2, g_ref[...], xh, r, 1.0 / D)

    row = lambda w: pl.BlockSpec((tm, w), lambda i: (i, 0))
    blk = pl.BlockSpec((N_CHIPS, tm, FFB), lambda i: (0, i, 0))
    bsd = jax.ShapeDtypeStruct((N_CHIPS, t, FFB), BF16)
    return pl.pallas_call(
        body, name="ffn_bwd", grid=(t // tm,),
        in_specs=[row(D), row(D), _const_spec((1, D)), blk, blk] + _ffn_weight_specs(),
        out_specs=[row(D), blk, blk, blk, pl.BlockSpec((1, D), lambda i: (0, 0))],
        out_shape=[jax.ShapeDtypeStruct((t, D), F32), bsd, bsd, bsd, jax.ShapeDtypeStruct((1, D), F32)],
        compiler_params=_cparams(("arbitrary",)),
    )(dy, xm, g2, a, b, gf, gf, gf)


def _wgrad_blocks(a, b, tt, name):
    _, t, rows = a.shape
    cols = b.shape[1]
    nsteps = t // tt

    def body(a_ref, b_ref, o_ref, acc_ref):
        s = pl.program_id(0)

        @pl.when(s == 0)
        def _():
            acc_ref[...] = jnp.zeros_like(acc_ref)

        bv = b_ref[...].astype(BF16)
        for k in range(N_CHIPS):
            acc_ref[k] += _tn(a_ref[k], bv)

        @pl.when(s == nsteps - 1)
        def _():
            o_ref[...] = acc_ref[...].astype(BF16)

    return pl.pallas_call(
        body, name=name, grid=(nsteps,),
        in_specs=[pl.BlockSpec((N_CHIPS, tt, rows), lambda s: (0, s, 0)), pl.BlockSpec((tt, cols), lambda s: (s, 0))],
        out_specs=pl.BlockSpec((N_CHIPS, rows, cols), lambda s: (0, 0, 0)),
        out_shape=jax.ShapeDtypeStruct((N_CHIPS, rows, cols), BF16),
        scratch_shapes=[pltpu.VMEM((N_CHIPS, rows, cols), F32)],
        compiler_params=_cparams(("arbitrary",)),
    )(a, b)


def _wgrad(a, b, tt, name):
    t, k = a.shape
    n = b.shape[1]
    nsteps = t // tt

    def body(a_ref, b_ref, o_ref, acc_ref):
        s = pl.program_id(0)

        @pl.when(s == 0)
        def _():
            acc_ref[...] = jnp.zeros_like(acc_ref)

        acc_ref[...] += _tn(a_ref[...].astype(BF16), b_ref[...].astype(BF16))

        @pl.when(s == nsteps - 1)
        def _():
            o_ref[...] = acc_ref[...].astype(BF16)

    return pl.pallas_call(
        body, name=name, grid=(nsteps,),
        in_specs=[pl.BlockSpec((tt, k), lambda s: (s, 0)), pl.BlockSpec((tt, n), lambda s: (s, 0))],
        out_specs=pl.BlockSpec((k, n), lambda s: (0, 0)),
        out_shape=jax.ShapeDtypeStruct((k, n), BF16),
        scratch_shapes=[pltpu.VMEM((k, n), F32)],
        compiler_params=_cparams(("arbitrary",)),
    )(a, b)


def _mixer_bwd(dxm, proj, ao, cw, gq, gk, sinks, gco, gao, wo, tq):
    t = proj.shape[0]
    nb = tq // BLK
    r8 = tq // 8
    nt = t // tq
    te = tq + 8
    kvw = 2 * NKV * HP

    def body(dx_ref, dxn_ref, p_ref, cgp_ref, hcp_ref, bgn_ref, cgn_ref, hcn_ref, kvp_ref, ao_ref, cw_ref, gq_ref,
             gk_ref, sk_ref, gco_ref, gao_ref, wo_ref,
             dpm_ref, dkvm_ref, dkvh_ref, dcw_ref, dgq_ref, dgk_ref, dsk_ref, dgco_ref, dgao_ref, acc_ref):
        i = pl.program_id(0)

        @pl.when(i == 0)
        def _():
            for r in (dcw_ref, dgq_ref, dgk_ref, dsk_ref, dgco_ref, dgao_ref):
                r[...] = jnp.zeros_like(r)

        acc_ref[...] = jnp.zeros_like(acc_ref)
        live_rows = jnp.where(i < nt - 1, te, tq)
        dxb = dx_ref[...].astype(BF16)
        dxe = jnp.concatenate([dxb, dxn_ref[...].astype(BF16)], axis=0)
        dcn = _nt(dxe, wo_ref[0:CC, :])
        bg = jnp.concatenate([p_ref[:, O_BG:O_BG + CC], bgn_ref[...]], axis=0)
        cg = jnp.concatenate([p_ref[:, O_CG:O_CG + CC], cgn_ref[...]], axis=0)
        hc = jnp.concatenate([p_ref[:, O_HC:O_HC + CC], hcn_ref[...]], axis=0)
        u = cg * hc
        up = jnp.where(i > 0, cgp_ref[...] * hcp_ref[...], 0.0)
        u1, u2 = _conv_taps(jnp.concatenate([up, u], axis=0), te)
        w0, w1, w2 = cw_ref[0:1, :], cw_ref[1:2, :], cw_ref[2:3, :]
        y = w0 * u2 + w1 * u1 + w2 * u
        co = bg * y
        rc, coh = _rms_fwd(co, 1.0 / CC)
        dco = _rms_bwd(dcn, gco_ref[...], coh, rc, 1.0 / CC)
        row_io = lax.broadcasted_iota(jnp.int32, (te, 1), 0)
        own = row_io < tq
        dgco_ref[...] += jnp.sum(jnp.where(own, dcn * coh, 0.0), axis=0, keepdims=True)
        dyc = jnp.where(row_io < live_rows, dco * bg, 0.0)
        dyo = jnp.where(own, dyc, 0.0)
        dcw_ref[0:1, :] += jnp.sum(dyo * u2, axis=0, keepdims=True)
        dcw_ref[1:2, :] += jnp.sum(dyo * u1, axis=0, keepdims=True)
        dcw_ref[2:3, :] += jnp.sum(dyo * u, axis=0, keepdims=True)
        dy1 = pltpu.roll(dyc, te - 1, 0)[0:tq]
        dy2 = pltpu.roll(dyc, te - 2, 0)[0:tq]
        du = w2 * dyc[0:tq] + w1 * dy1 + w0 * dy2
        dpm_ref[:, O_BG:O_BG + CC] = (dco[0:tq] * y[0:tq]).astype(BF16)
        dpm_ref[:, O_CG:O_CG + CC] = (du * hc[0:tq]).astype(BF16)
        dpm_ref[:, O_HC:O_HC + CC] = (du * cg[0:tq]).astype(BF16)
        kraw = jnp.concatenate([kvp_ref[:, 0:NKV * HP], p_ref[:, O_K:O_K + NKV * HP]], axis=0)
        vraw = jnp.concatenate([kvp_ref[:, NKV * HP:], p_ref[:, O_V:O_V + NKV * HP]], axis=0)
        gqv, gkv = gq_ref[...], gk_ref[...]
        keys = _norm_keys(kraw, gkv)
        vb = [vraw[:, h * HP:(h + 1) * HP].astype(BF16) for h in range(NKV)]
        base_valid, c_io = _band_mask()
        lane = lax.broadcasted_iota(jnp.int32, (1, HP), 1)
        dgq, dgk, dsk = (jnp.zeros((1, HP), F32) for _ in range(3))
        dgao = jnp.zeros((1, NQ * HP), F32)
        for b in range(nb):
            lo = jnp.where(i * nb + b == 0, BLK, 0)
            valid = base_valid & (c_io >= lo)
            band = slice(b * BLK, b * BLK + 2 * BLK)
            blk = slice(b * BLK, (b + 1) * BLK)
            ra, aoh = _rms_fwd(ao_ref[blk, :], 1.0 / (NQ * HD))
            danb = _nt(dxb[blk], wo_ref[CC:MIXW, :])
            dgao = dgao + jnp.sum(danb * aoh, axis=0, keepdims=True)
            dao = _rms_bwd(danb, gao_ref[...], aoh, ra, 1.0 / (NQ * HD))
            fwd = []
            for g in range(NQ):
                rq, qh = _rms_fwd(p_ref[blk, O_Q + g * HP:O_Q + (g + 1) * HP], 1.0 / HD)
                qs = (qh * (gqv * SCALE)).astype(BF16)
                fwd.append((rq, qh, qs) + _attn_probs(qs, keys[g // GRP][2][band], sk_ref[0, g], valid))
            dqs = []
            for h in range(NKV):
                khat, rk, kn = [a[band] for a in keys[h]]
                dss, prbs, qns, dobs = [], [], [], []
                for g in range(h * GRP, (h + 1) * GRP):
                    rq, qh, qs, pr, ps = fwd[g]
                    dob = dao[:, g * HP:(g + 1) * HP].astype(BF16)
                    dp = _nt(dob, vb[h][band])
                    delta = jnp.sum(pr * dp, axis=-1, keepdims=True)
                    dsb = (pr * (dp - delta)).astype(BF16)
                    dsk = dsk + jnp.where(lane == g, -jnp.sum(ps * delta, axis=0, keepdims=True), 0.0)
                    dqn = jnp.dot(dsb, kn, preferred_element_type=F32) * SCALE
                    dgq = dgq + jnp.sum(dqn * qh, axis=0, keepdims=True)
                    dqs.append(_rms_bwd(dqn, gqv, qh, rq, 1.0 / HD).astype(BF16))
                    dss.append(dsb)
                    prbs.append(pr.astype(BF16))
                    qns.append(qs)
                    dobs.append(dob)
                dkn = _tn(jnp.concatenate(dss, axis=0), jnp.concatenate(qns, axis=0))
                dv = _tn(jnp.concatenate(prbs, axis=0), jnp.concatenate(dobs, axis=0))
                dgk = dgk + jnp.sum(dkn * khat, axis=0, keepdims=True)
                acc_ref[band, h * HP:(h + 1) * HP] += _rms_bwd(dkn, gkv, khat, rk, 1.0 / HD)
                acc_ref[band, (NKV + h) * HP:(NKV + h + 1) * HP] += dv
            dpm_ref[blk, O_Q:O_K] = jnp.concatenate(dqs, axis=1)
        dgq_ref[...] += dgq
        dgk_ref[...] += dgk
        dsk_ref[...] += dsk
        dgao_ref[...] += dgao
        dkvh_ref[...] = acc_ref[0:BLK, :]
        dkvm_ref[...] = acc_ref[BLK:, :]

    prev8 = lambda col: pl.BlockSpec((8, CC), lambda i: (jnp.maximum(i * r8 - 1, 0), col))
    next8 = lambda col: pl.BlockSpec((8, CC), lambda i: (jnp.minimum((i + 1) * r8, t // 8 - 1), col))
    small = lambda n: pl.BlockSpec((1, n), lambda i: (0, 0))
    return pl.pallas_call(
        body, name="mixer_bwd", grid=(nt,),
        in_specs=[
            pl.BlockSpec((tq, D), lambda i: (i, 0)),
            pl.BlockSpec((8, D), lambda i: (jnp.minimum((i + 1) * r8, t // 8 - 1), 0)),
            pl.BlockSpec((tq, NP), lambda i: (i, 0)),
            prev8(O_CG // CC), prev8(O_HC // CC),
            next8(O_BG // CC), next8(O_CG // CC), next8(O_HC // CC),
            pl.BlockSpec((BLK, kvw), lambda i: (jnp.maximum(i * nb - 1, 0), O_K // kvw)),
            pl.BlockSpec((tq, NQ * HP), lambda i: (i, 0)),
            _const_spec((8, CC)), _const_spec((1, HP)), _const_spec((1, HP)),
            pl.BlockSpec(memory_space=pltpu.SMEM),
            _const_spec((1, CC)), _const_spec((1, NQ * HP)), _const_spec((MIXW, D)),
        ],
        out_specs=[
            pl.BlockSpec((tq, NMAIN), lambda i: (i, 0)),
            pl.BlockSpec((tq, kvw), lambda i: (i, 0)),
            pl.BlockSpec((BLK, kvw), lambda i: (i, 0)),
            pl.BlockSpec((8, CC), lambda i: (0, 0)), small(HP), small(HP), small(HP), small(CC), small(NQ * HP),
        ],
        out_shape=[
            jax.ShapeDtypeStruct((t, NMAIN), BF16), jax.ShapeDtypeStruct((t, kvw), F32),
            jax.ShapeDtypeStruct((nt * BLK, kvw), F32),
            jax.ShapeDtypeStruct((8, CC), F32), jax.ShapeDtypeStruct((1, HP), F32), jax.ShapeDtypeStruct((1, HP), F32),
            jax.ShapeDtypeStruct((1, HP), F32), jax.ShapeDtypeStruct((1, CC), F32),
            jax.ShapeDtypeStruct((1, NQ * HP), F32),
        ],
        scratch_shapes=[pltpu.VMEM((tq + BLK, kvw), F32)],
        compiler_params=_cparams(("arbitrary",)),
    )(dxm, dxm, proj, proj, proj, proj, proj, proj, proj, ao, cw, gq, gk, sinks, gco, gao, wo)


def _inproj_bwd(dpm, dkvm, dkvh, wpt, x, g1, dxm, tm):
    t = x.shape[0]
    kvw = 2 * NKV * HP
    nt = t // tm

    def body(dp_ref, dk_ref, dh_ref, w_ref, x_ref, g_ref, dxm_ref, dx_ref, dg_ref, dkv_ref):
        i = pl.program_id(0)

        @pl.when(i == 0)
        def _():
            dg_ref[...] = jnp.zeros_like(dg_ref)

        halo = jnp.where(i < nt - 1, dh_ref[...], 0.0)
        dkv_ref[0:tm - BLK, :] = dk_ref[0:tm - BLK, :].astype(BF16)
        dkv_ref[tm - BLK:tm, :] = (dk_ref[tm - BLK:tm, :] + halo).astype(BF16)
        dh = (jnp.dot(dp_ref[...], w_ref[0:NMAIN, :], preferred_element_type=F32)
              + jnp.dot(dkv_ref[...], w_ref[NMAIN:NP, :], preferred_element_type=F32))
        r, xh = _rms_fwd(x_ref[...], 1.0 / D)
        dg_ref[...] += jnp.sum(dh * xh, axis=0, keepdims=True)
        dx_ref[...] = dxm_ref[...] + _rms_bwd(dh, g_ref[...], xh, r, 1.0 / D)

    row = lambda w: pl.BlockSpec((tm, w), lambda i: (i, 0))
    return pl.pallas_call(
        body, name="inproj_bwd", grid=(nt,),
        in_specs=[row(NMAIN), row(kvw), pl.BlockSpec((BLK, kvw), lambda i: (jnp.minimum(i + 1, nt - 1), 0)),
                  _const_spec((NP, D)), row(D), _const_spec((1, D)), row(D)],
        out_specs=[row(D), pl.BlockSpec((1, D), lambda i: (0, 0)), row(kvw)],
        out_shape=[jax.ShapeDtypeStruct((t, D), F32), jax.ShapeDtypeStruct((1, D), F32),
                   jax.ShapeDtypeStruct((t, kvw), BF16)],
        compiler_params=_cparams(("arbitrary",)),
    )(dpm, dkvm, dkvh, wpt, x, g1, dxm)


def _rows_tile(rows, cap=512):
    for cand in range(min(rows, cap) // 16 * 16, 0, -16):
        if rows % cand == 0:
            return cand
    return rows


def _presum_halves(gs, theirs, core):
    n = len(gs)

    def body(c_ref, *refs):
        for g_ref, t_ref, o_ref in zip(refs[:n], refs[n:2 * n], refs[2 * n:]):
            o_ref[...] = (g_ref[...].astype(F32) + t_ref[...].astype(F32)).astype(BF16)

    half = lambda ta: pl.BlockSpec((None,) + ta.shape[1:], lambda k, c_ref: (k, 0, 0))
    own = lambda ta: pl.BlockSpec((None,) + ta.shape[1:], lambda k, c_ref: (k, c_ref[0], 0))
    return pl.pallas_call(
        body, name="presum",
        grid_spec=pltpu.PrefetchScalarGridSpec(
            num_scalar_prefetch=1, grid=(N_CHIPS,),
            in_specs=[own(ta) for ta in theirs] + [half(ta) for ta in theirs],
            out_specs=[half(ta) for ta in theirs]),
        out_shape=[jax.ShapeDtypeStruct(ta.shape, BF16) for ta in theirs],
        compiler_params=_cparams(("parallel",)),
    )(core, *gs, *theirs)


def _sum_chips(cs):
    n = len(cs)
    steps = 2

    def body(*refs):
        for c_ref, o_ref in zip(refs[:n], refs[n:]):
            acc = c_ref[0].astype(F32)
            for j in range(1, N_CHIPS):
                acc = acc + c_ref[j].astype(F32)
            o_ref[...] = acc

    return pl.pallas_call(
        body, name="chipsum", grid=(steps,),
        in_specs=[pl.BlockSpec((N_CHIPS, c.shape[1] // steps, c.shape[2]), lambda i: (0, i, 0)) for c in cs],
        out_specs=[pl.BlockSpec((c.shape[1] // steps, c.shape[2]), lambda i: (i, 0)) for c in cs],
        out_shape=[jax.ShapeDtypeStruct(c.shape[1:], F32) for c in cs],
        compiler_params=_cparams(("parallel",)),
    )(*cs)


def _adamw(w, g, m, v, name):
    rows, cols = w.shape
    tr = _rows_tile(rows, 256)
    c1 = 1.0 - ADAM_B1 ** ADAM_STEP
    c2 = 1.0 - ADAM_B2 ** ADAM_STEP

    def body(w_ref, g_ref, m_ref, v_ref, d_ref, mo_ref, vo_ref):
        gv = g_ref[...]
        mn = ADAM_B1 * m_ref[...] + (1.0 - ADAM_B1) * gv
        vn = ADAM_B2 * v_ref[...] + (1.0 - ADAM_B2) * (gv * gv)
        mo_ref[...] = mn
        vo_ref[...] = vn
        d_ref[...] = -ADAM_LR * ((mn / c1) / (jnp.sqrt(vn / c2) + ADAM_EPS) + ADAM_WD * w_ref[...])

    spec = pl.BlockSpec((tr, cols), lambda i: (i, 0))
    sds = jax.ShapeDtypeStruct((rows, cols), F32)
    return pl.pallas_call(
        body, name=name, grid=(rows // tr,), in_specs=[spec] * 4, out_specs=[spec] * 3, out_shape=[sds] * 3,
        compiler_params=_cparams(("parallel",)),
    )(w, g, m, v)


def _place():
    x, y, c = lax.axis_index("x"), lax.axis_index("y"), lax.axis_index("c")
    chips = [(1 - x, y), (x, 1 - y), (1 - x, 1 - y)]
    return x, y, c, chips


ANY = pl.BlockSpec(memory_space=pl.ANY)
DMA_ROWS = 64


def _pieces(shape):
    rows = shape[-2]
    step = DMA_ROWS if rows % DMA_ROWS == 0 else rows
    lead = [()]
    for n in shape[:-2]:
        lead = [i + (k,) for i in lead for k in range(n)]
    return [i + (pl.ds(r0, step),) for i in lead for r0 in range(0, rows, step)]


def _start_pieces(make, src, dst):
    for idx in _pieces(src.shape):
        make(src.at[idx], dst.at[idx]).start()


def _gather_layer(blocks, layer):
    nw = len(blocks)

    def body(*refs):
        _gather_body(refs[:nw], refs[nw:2 * nw], refs[2 * nw:], layer, _start_pieces)

    return pl.pallas_call(
        body, name=f"gather_layer{layer}", in_specs=[ANY] * nw, out_specs=[ANY] * nw,
        out_shape=[jax.ShapeDtypeStruct((N_CHIPS,) + b.shape, b.dtype) for b in blocks],
        scratch_shapes=[pltpu.SemaphoreType.DMA((3, nw))] * 4,
        compiler_params=_cparams(has_side_effects=True),
    )(*blocks)


def _gather_body(srcs, outs, sems, layer, start):
    nw = len(srcs)
    ssem, rsem, fssem, frsem = sems
    x, y, c, chips = _place()
    kme = 2 * x + y

    def plane(j, w, to):
        return lambda s, d: pltpu.make_async_remote_copy(
            src_ref=s, dst_ref=d, send_sem=ssem.at[j, w], recv_sem=rsem.at[j, w], device_id=to,
            device_id_type=MESH)

    def passed(j, w):
        return lambda s, d: pltpu.make_async_remote_copy(
            src_ref=s, dst_ref=d, send_sem=fssem.at[j, w], recv_sem=frsem.at[j, w],
            device_id=(x, y, 1 - c), device_id_type=MESH)

    @pl.when(c == layer)
    def _():
        for j, (px, py) in enumerate(chips):
            for w in range(nw):
                start(plane(j, w, (px, py, c)), srcs[w], outs[w].at[kme])
        for j, (px, py) in enumerate(chips):
            for w in range(nw):
                got = outs[w].at[2 * px + py]
                plane(j, w, (px, py, c))(got, got).wait_recv()
                start(passed(j, w), got, got)
        for j, (px, py) in enumerate(chips):
            for w in range(nw):
                got = outs[w].at[2 * px + py]
                plane(j, w, (px, py, c))(got, got).wait_send()
                passed(j, w)(got, got).wait_send()

    @pl.when(c != layer)
    def _():
        for j, (px, py) in enumerate(chips):
            for w in range(nw):
                got = outs[w].at[2 * px + py]
                passed(j, w)(got, got).wait_recv()


def _handshake_all():
    x, y, c, _ = _place()
    barrier = pltpu.get_barrier_semaphore()
    for r in range(1, 8):
        peer = (x ^ (r >> 2), y ^ ((r >> 1) & 1), c ^ (r & 1))
        pl.semaphore_signal(barrier, inc=1, device_id=peer, device_id_type=MESH)
    pl.semaphore_wait(barrier, 7)


def _gather_layer_async(blocks, layer, name, collective_id):
    hbm = pltpu.MemorySpace.HBM
    srcs = [jax.new_ref(b, memory_space=hbm) for b in blocks]
    outs = [jax.empty_ref(jax.ShapeDtypeStruct((N_CHIPS,) + b.shape, b.dtype), memory_space=hbm) for b in blocks]

    @pl.kernel(mesh=plsc.ScalarSubcoreMesh(axis_name="seq", num_cores=1), name=name,
               scratch_types=[pltpu.SemaphoreType.DMA((3, len(blocks)))] * 4,
               compiler_params=pltpu.CompilerParams(collective_id=collective_id))
    def launch(*sems):
        _handshake_all()
        _gather_body(srcs, outs, sems, layer, lambda make, s, d: make(s, d).start())

    launch()
    return [o[...] for o in outs]


def _swap_halves(gs):
    nw = len(gs)

    def body(*refs):
        srcs, theirs = refs[:nw], refs[nw:2 * nw]
        ssem, rsem = refs[2 * nw:]
        x, y, c, _ = _place()

        def give(w):
            return lambda s, d: pltpu.make_async_remote_copy(
                src_ref=s, dst_ref=d, send_sem=ssem.at[w], recv_sem=rsem.at[w], device_id=(x, y, 1 - c),
                device_id_type=MESH)

        for w in range(nw):
            hr = theirs[w].shape[1]
            _start_pieces(give(w), srcs[w].at[:, pl.ds((1 - c) * hr, hr)], theirs[w])
        for w in range(nw):
            give(w)(theirs[w], theirs[w]).wait()

    return pl.pallas_call(
        body, name="swap_halves", in_specs=[ANY] * nw, out_specs=[ANY] * nw,
        out_shape=[jax.ShapeDtypeStruct((g.shape[0], g.shape[1] // 2, g.shape[2]), g.dtype) for g in gs],
        scratch_shapes=[pltpu.SemaphoreType.DMA((nw,))] * 2,
        compiler_params=_cparams(has_side_effects=True),
    )(*gs)


def _scatter_chips(ps):
    nw = len(ps)

    def body(*refs):
        _scatter_body(refs[:nw], refs[nw:2 * nw], refs[2 * nw:], _start_pieces)

    return pl.pallas_call(
        body, name="scatter_chips", in_specs=[ANY] * nw, out_specs=[ANY] * nw,
        out_shape=[jax.ShapeDtypeStruct(p.shape, p.dtype) for p in ps],
        scratch_shapes=[pltpu.SemaphoreType.DMA((3, nw)), pltpu.SemaphoreType.DMA((3, nw))],
        compiler_params=_cparams(has_side_effects=True),
    )(*ps)


def _scatter_body(srcs, outs, sems, start):
    nw = len(srcs)
    ssem, rsem = sems
    x, y, c, chips = _place()
    kme = 2 * x + y

    def give(j, w, to):
        return lambda s, d: pltpu.make_async_remote_copy(
            src_ref=s, dst_ref=d, send_sem=ssem.at[j, w], recv_sem=rsem.at[j, w], device_id=to,
            device_id_type=MESH)

    for j, (px, py) in enumerate(chips):
        for w in range(nw):
            start(give(j, w, (px, py, c)), srcs[w].at[2 * px + py], outs[w].at[kme])
    for j, (px, py) in enumerate(chips):
        for w in range(nw):
            got = outs[w].at[2 * px + py]
            give(j, w, (px, py, c))(got, got).wait_recv()
    for j, (px, py) in enumerate(chips):
        for w in range(nw):
            sent = srcs[w].at[2 * px + py]
            give(j, w, (px, py, c))(sent, sent).wait_send()


def _scatter_chips_async(ps, name, collective_id):
    hbm = pltpu.MemorySpace.HBM
    srcs = [jax.new_ref(p, memory_space=hbm) for p in ps]
    outs = [jax.empty_ref(jax.ShapeDtypeStruct(p.shape, p.dtype), memory_space=hbm) for p in ps]

    @pl.kernel(mesh=plsc.ScalarSubcoreMesh(axis_name="seq", num_cores=1), name=name,
               scratch_types=[pltpu.SemaphoreType.DMA((3, len(ps)))] * 2,
               compiler_params=pltpu.CompilerParams(collective_id=collective_id))
    def launch(*sems):
        _handshake_all()
        _scatter_body(srcs, outs, sems, lambda make, s, d: make(s, d).start())

    launch()
    return [o[...] for o in outs]


def _swap_siblings(rs):
    nw = len(rs)

    def body(*refs):
        srcs, outs = refs[:nw], refs[nw:2 * nw]
        ssem, rsem = refs[2 * nw:]
        x, y, c, _ = _place()

        def give(w):
            return lambda s, d: pltpu.make_async_remote_copy(
                src_ref=s, dst_ref=d, send_sem=ssem.at[w], recv_sem=rsem.at[w], device_id=(x, y, 1 - c),
                device_id_type=MESH)

        for w in range(nw):
            _start_pieces(give(w), srcs[w], outs[w])
        for w in range(nw):
            give(w)(srcs[w], outs[w]).wait()

    return pl.pallas_call(
        body, name="swap_siblings", in_specs=[ANY] * nw, out_specs=[ANY] * nw,
        out_shape=[jax.ShapeDtypeStruct(r.shape, r.dtype) for r in rs],
        scratch_shapes=[pltpu.SemaphoreType.DMA((nw,))] * 2,
        compiler_params=_cparams(has_side_effects=True),
    )(*rs)


def _allreduce_small(v):
    rows = v.shape[0]

    def body(v_ref, o_ref, buf, ssem, rsem):
        x, y, c, _ = _place()
        me = 4 * x + 2 * y + c
        buf[me] = v_ref[...]
        sends = []
        for r in range(1, 8):
            peer = (x ^ (r >> 2), y ^ ((r >> 1) & 1), c ^ (r & 1))
            cp = pltpu.make_async_remote_copy(
                src_ref=v_ref, dst_ref=buf.at[me], send_sem=ssem.at[r - 1], recv_sem=rsem.at[r - 1],
                device_id=peer, device_id_type=MESH)
            cp.start()
            sends.append(cp)
        for r in range(1, 8):
            src = me ^ r
            pltpu.make_async_remote_copy(
                src_ref=v_ref, dst_ref=buf.at[src], send_sem=ssem.at[r - 1], recv_sem=rsem.at[r - 1],
                device_id=(x, y, c), device_id_type=MESH).wait_recv()
        for cp in sends:
            cp.wait_send()
        acc = buf[0]
        for d in range(1, 8):
            acc = acc + buf[d]
        o_ref[...] = acc

    vm = pl.BlockSpec(memory_space=pltpu.VMEM)
    return pl.pallas_call(
        body, name="allreduce_small", in_specs=[vm], out_specs=vm,
        out_shape=jax.ShapeDtypeStruct(v.shape, F32),
        scratch_shapes=[pltpu.VMEM((8, rows, 128), F32), pltpu.SemaphoreType.DMA((7,)),
                        pltpu.SemaphoreType.DMA((7,))],
        compiler_params=_cparams(has_side_effects=True),
    )(v)


def _pad_heads(w, n_heads, axis):
    shp = w.shape
    w = w.reshape(shp[:axis] + (n_heads, HD) + shp[axis + 1:])
    pad = [(0, 0)] * w.ndim
    pad[axis + 1] = (0, HP - HD)
    w = jnp.pad(w, pad)
    return w.reshape(shp[:axis] + (n_heads * HP,) + shp[axis + 1:])


def _strip_heads(w, n_heads, axis):
    shp = w.shape
    w = w.reshape(shp[:axis] + (n_heads, HP) + shp[axis + 1:])
    w = lax.slice_in_dim(w, 0, HD, axis=axis + 1)
    return w.reshape(shp[:axis] + (n_heads * HD,) + shp[axis + 1:])


def _pad_win_t(wint):
    parts = [wint[:3 * CC], _pad_heads(wint[3 * CC:3 * CC + NQ * HD], NQ, 0),
             _pad_heads(wint[3 * CC + NQ * HD:3 * CC + (NQ + NKV) * HD], NKV, 0),
             _pad_heads(wint[3 * CC + (NQ + NKV) * HD:], NKV, 0)]
    return jnp.concatenate(parts, axis=0)


def _strip_win_t(gpt):
    parts = [gpt[:3 * CC], _strip_heads(gpt[O_Q:O_K], NQ, 0), _strip_heads(gpt[O_K:O_V], NKV, 0),
             _strip_heads(gpt[O_V:], NKV, 0)]
    return jnp.concatenate(parts, axis=0)


def _t(w):
    return jnp.swapaxes(w, -1, -2)


def _count(shape):
    n = 1
    for s in shape:
        n *= s
    return n


def _pack_rows(arrs):
    flat = [jnp.pad(a.reshape(-1), (0, (-_count(a.shape)) % 128)) for a in arrs]
    v = jnp.concatenate(flat)
    rows = -(-v.shape[0] // (8 * 128)) * 8
    return jnp.pad(v, (0, rows * 128 - v.shape[0])).reshape(rows, 128)


def kernel(x, norm1_g, w_in, conv_w, q_norm_g, k_norm_g, sinks, conv_out_g, attn_out_g, w_o, norm2_g, w_gate, w_up, w_down, loss_target, m_norm1_g, m_w_in, m_conv_w, m_q_norm_g, m_k_norm_g, m_sinks, m_conv_out_g, m_attn_out_g, m_w_o, m_norm2_g, m_w_gate, m_w_up, m_w_down, v_norm1_g, v_w_in, v_conv_w, v_q_norm_g, v_k_norm_g, v_sinks, v_conv_out_g, v_attn_out_g, v_w_o, v_norm2_g, v_w_gate, v_w_up, v_w_down):
    depth = w_in.shape[0]
    t = x.shape[1]
    xs = x.reshape(t, D)
    tgt = loss_target.reshape(t, D)
    xi, yi = lax.axis_index("x"), lax.axis_index("y")
    kme = 2 * xi + yi
    tm = min(512, t)
    tq = min(512, t)
    tf = min(256, t)
    tw = min(1024, t)

    cwp = jnp.pad(conv_w.reshape(depth * 3, CC // N_CHIPS), ((0, 8 - depth * 3), (0, 0)))
    own_f = [jnp.concatenate([_t(w_gate[l]), _t(w_up[l]), w_down[l]], axis=0).astype(BF16) for l in range(depth)]
    own_o = [w_o[l].astype(BF16) for l in range(depth)]
    own_i = [_t(w_in[l]).astype(BF16) for l in range(depth)]
    mine = lambda got, own: lax.dynamic_update_index_in_dim(got, own, kme, 0)
    got_i0, got_o0, got_cw = _gather_layer([own_i[0], own_o[0], cwp], 0)
    gf0_in = lax.optimization_barrier((own_f[0], got_i0))[0]
    (got_f0,) = _gather_layer_async([gf0_in], 0, "gather_ffn0_seq", collective_id=6)
    cw_full = mine(got_cw, cwp).transpose(1, 0, 2).reshape(8, CC)[:depth * 3].reshape(depth, 3, CC)

    def layer_params(l, got_i, got_o):
        wo = mine(got_o, own_o[l]).reshape(D, D)
        return dict(
            wpt=_pad_win_t(mine(got_i, own_i[l]).reshape(N_CHIPS * 576, D)),
            wo=jnp.concatenate([wo[:CC], _pad_heads(wo[CC:], NQ, 0)], axis=0),
            cw=jnp.pad(cw_full[l], ((0, 5), (0, 0))),
            g1=norm1_g[l].reshape(1, D), g2=norm2_g[l].reshape(1, D),
            gq=jnp.pad(q_norm_g[l], (0, HP - HD)).reshape(1, HP), gk=jnp.pad(k_norm_g[l], (0, HP - HD)).reshape(1, HP),
            sk=sinks[l].reshape(1, NQ), gco=conv_out_g[l].reshape(1, CC),
            gao=_pad_heads(attn_out_g[l], NQ, 0).reshape(1, NQ * HP))

    saved, layers = [], []
    cur = xs
    for l in range(depth):
        x_in = cur
        if l == 0:
            p = layer_params(0, got_i0, got_o0)
        else:
            got_f1, got_o1, got_i1 = lax.optimization_barrier((got_l1, cur))[0]
            p = layer_params(1, got_i1, got_o1)
        proj, h = _inproj_fwd(cur, p["g1"], p["wpt"], tm)
        xm, mix, ao = _mixer_fwd(proj, cur, p["cw"], p["gq"], p["gk"], p["sk"], p["gco"], p["gao"], p["wo"], tq)
        if l == 0:
            got_f0 = lax.optimization_barrier((got_f0, xm))[0]
            l1_in = lax.optimization_barrier(([own_f[1], own_o[1], own_i[1]], got_f0))[0]
            got_l1 = _gather_layer_async(l1_in, 1, "gather_layer1_seq", collective_id=1)
        p["gf"] = mine(got_f0 if l == 0 else got_f1, own_f[l])
        layers.append(p)
        if l < depth - 1:
            cur, a, b, h2 = _ffn_fwd(xm, p["g2"], p["gf"], tm)
        else:
            lpart, dy, a, b, h2 = _ffn_fwd(xm, p["g2"], p["gf"], tm, tgt)
        saved.append(dict(x=x_in, proj=proj, h=h, xm=xm, mix=mix, ao=ao, a=a, b=b, h2=h2))
    loss = lax.psum(lpart[0, 0], ("x", "y", "c"))

    nt = t // tq
    ci = lax.axis_index("c")
    core = ci.reshape(1).astype(jnp.int32)
    rbig = [dict() for _ in range(depth)]
    gsmall = [None] * depth

    def reduce_start(gs, name, collective_id):
        ps = _presum_halves(gs, _swap_halves(gs), core)
        got = _scatter_chips(ps) if collective_id is None else _scatter_chips_async(ps, name, collective_id)
        return ps, got

    def reduce_finish(started, after):
        ps, got = started
        if after is not None:
            got = lax.optimization_barrier((got, after))[0]
        cs = [lax.dynamic_update_index_in_dim(g, lax.dynamic_index_in_dim(q, kme, 0, keepdims=False), kme, 0)
              for g, q in zip(got, ps)]
        r_mine = _sum_chips(cs)
        return [jnp.where(ci == 0, jnp.concatenate([a, b], axis=0), jnp.concatenate([b, a], axis=0))
                for a, b in zip(r_mine, _swap_siblings(r_mine))]

    in_flight = None
    for l in reversed(range(depth)):
        p, s = layers[l], saved[l]
        dxm, da, db, hm, dg2 = _ffn_bwd(dy, s["xm"], p["g2"], s["a"], s["b"], p["gf"], tf)
        g_wg = _wgrad_blocks(da, s["h2"], tw, "wgrad_gate")
        g_wu = _wgrad_blocks(db, s["h2"], tw, "wgrad_up")
        g_wd = _wgrad_blocks(hm, dy, tw, "wgrad_down")
        if in_flight is not None:
            rbig[l + 1]["in"], rbig[l + 1]["o"] = reduce_finish(in_flight, g_wd)
        ffn_flight = reduce_start([g_wg, g_wu, g_wd], f"scatter_ffn{l}_seq", 2 + 2 * l)
        dpm, dkvm, dkvh, dcw, dgq, dgk, dsk, dgco, dgao = _mixer_bwd(
            dxm, s["proj"], s["ao"], p["cw"], p["gq"], p["gk"], p["sk"], p["gco"], p["gao"], p["wo"], tq)
        g_wo = _wgrad(s["mix"], dxm, tw, "wgrad_o")
        dx, dg1, dkv = _inproj_bwd(dpm, dkvm, dkvh, p["wpt"], s["x"], p["g1"], dxm, tq)
        g_wpm = _wgrad(dpm, s["h"], tw, "wgrad_in_main")
        g_wpk = _wgrad(dkv, s["h"], tw, "wgrad_in_kv")
        dy = dx
        g_in = _strip_win_t(jnp.concatenate([g_wpm, g_wpk], axis=0))
        g_o = jnp.concatenate([g_wo[:CC], _strip_heads(g_wo[CC:], NQ, 0)], axis=0)
        gsmall[l] = dict(g1=dg1, cw=dcw[:3], gq=dgq[0, :HD], gk=dgk[0, :HD], sk=dsk[0, :NQ], gco=dgco,
                         gao=_strip_heads(dgao.reshape(NQ * HP), NQ, 0), g2=dg2)
        rbig[l]["g"], rbig[l]["u"], rbig[l]["d"] = reduce_finish(ffn_flight, dx)
        in_flight = reduce_start([g_in.reshape(N_CHIPS, -1, D), g_o.reshape(N_CHIPS, -1, D)],
                                 f"scatter_in{l}_seq", 3 + 2 * l)
    grad_x = dy.reshape(x.shape)

    small_shapes = dict(g1=(D,), cw=(3, CC), gq=(HD,), gk=(HD,), sk=(NQ,), gco=(CC,), gao=(NQ * HD,), g2=(D,))
    red = _allreduce_small(_pack_rows([gsmall[l][n] for l in range(depth) for n in small_shapes])).reshape(-1)
    red_small, offs = {n: [] for n in small_shapes}, 0
    for l in range(depth):
        for n, shp in small_shapes.items():
            cnt = _count(shp)
            red_small[n].append(red[offs:offs + cnt].reshape(shp))
            offs += -(-cnt // 128) * 128
    g_small = {n: jnp.stack(v) for n, v in red_small.items()}
    g_cw = lax.dynamic_slice_in_dim(g_small["cw"], kme * (CC // N_CHIPS), CC // N_CHIPS, axis=2)

    weights = [norm1_g, w_in, conv_w, q_norm_g, k_norm_g, sinks, conv_out_g, attn_out_g, w_o, norm2_g, w_gate,
               w_up, w_down]
    moms = [m_norm1_g, m_w_in, m_conv_w, m_q_norm_g, m_k_norm_g, m_sinks, m_conv_out_g, m_attn_out_g, m_w_o,
            m_norm2_g, m_w_gate, m_w_up, m_w_down]
    vars_ = [v_norm1_g, v_w_in, v_conv_w, v_q_norm_g, v_k_norm_g, v_sinks, v_conv_out_g, v_attn_out_g, v_w_o,
             v_norm2_g, v_w_gate, v_w_up, v_w_down]
    n_w = len(weights)
    big_idx = dict(zip(("in", "o", "g", "u", "d"), (1, 8, 10, 11, 12)))
    small_idx = [n for n in range(n_w) if n not in big_idx.values()]
    grads, deltas, new_m, new_v = [None] * n_w, [None] * n_w, [None] * n_w, [None] * n_w
    for n, g in zip(small_idx, (g_small["g1"], g_cw, g_small["gq"], g_small["gk"], g_small["sk"], g_small["gco"],
                                g_small["gao"], g_small["g2"])):
        grads[n] = g

    def update_big(name):
        n = big_idx[name]
        g = jnp.stack([rbig[l][name] for l in range(depth)])
        flip = g.shape != weights[n].shape
        rows2d = lambda a3: (_t(a3) if flip else a3).reshape(-1, D)
        res = _adamw(rows2d(weights[n]), g.reshape(-1, D), rows2d(moms[n]), rows2d(vars_[n]), f"adamw_{n}")
        res = [g] + [r.reshape(g.shape) for r in res]
        grads[n], deltas[n], new_m[n], new_v[n] = [_t(r) for r in res] if flip else res

    for name in ("g", "u", "d"):
        update_big(name)
    rbig[0]["in"], rbig[0]["o"] = reduce_finish(in_flight, new_v[big_idx["d"]])
    for name in ("in", "o"):
        update_big(name)
    res = _adamw(*[_pack_rows([arrs[n] for n in small_idx]) for arrs in (weights, grads, moms, vars_)],
                 "adamw_small")
    offs = 0
    for n in small_idx:
        shp = weights[n].shape
        cnt = _count(shp)
        deltas[n], new_m[n], new_v[n] = [r.reshape(-1)[offs:offs + cnt].reshape(shp) for r in res]
        offs += -(-cnt // 128) * 128
    return (loss, grad_x, *grads, *deltas, *new_m, *new_v)
```

```python
import functools

import jax
import jax.numpy as jnp
from jax import lax
from jax.experimental import pallas as pl
from jax.experimental.pallas import tpu as pltpu
from jax.experimental.pallas import tpu_sc as plsc

F32 = jnp.float32
BF16 = jnp.bfloat16

D = 1024
CC = 512
NQ = 8
NKV = 2
HD = 64
HP = 128
GRP = NQ // NKV
FF = 2816
FFB = FF // 4
BLK = 128
EPS = 1e-6
NEG = -1e30
SCALE = HD ** -0.5
O_BG, O_CG, O_HC, O_Q = 0, CC, 2 * CC, 3 * CC
O_K = O_Q + NQ * HP
O_V = O_K + NKV * HP
NP = O_V + NKV * HP
NMAIN = O_K
MIXW = CC + NQ * HP
N_CHIPS = 4
VMEM_LIMIT = 56 * 1024 * 1024
MESH = pl.DeviceIdType.MESH

ADAM_LR, ADAM_B1, ADAM_B2, ADAM_EPS, ADAM_WD, ADAM_STEP = 0.001, 0.9, 0.999, 1e-08, 0.01, 10


def _cparams(sem=None, **kw):
    if sem is not None:
        kw["dimension_semantics"] = sem
    return pltpu.CompilerParams(vmem_limit_bytes=VMEM_LIMIT, **kw)


def _const_spec(shape):
    nd = len(shape)
    return pl.BlockSpec(shape, lambda *_: (0,) * nd, pipeline_mode=pl.Buffered(1))


def _nt(a, b):
    return lax.dot_general(a, b, (((1,), (1,)), ((), ())), preferred_element_type=F32)


def _tn(a, b):
    return lax.dot_general(a, b, (((0,), (0,)), ((), ())), preferred_element_type=F32)


def _rms_fwd(x, inv_n):
    r = lax.rsqrt(jnp.sum(x * x, axis=-1, keepdims=True) * inv_n + EPS)
    return r, x * r


def _rms_bwd(dy, g, xh, r, inv_n):
    dxh = dy * g
    return r * (dxh - xh * (jnp.sum(dxh * xh, axis=-1, keepdims=True) * inv_n))


W_IN_ROWS = 3 * CC + (NQ + 2 * NKV) * HD
W_IN_BLOCK = W_IN_ROWS // N_CHIPS


def _padded_row(row):
    return row + max(row - O_Q, 0) // HD * (HP - HD)


def _w_in_pieces(k):
    first = k * W_IN_BLOCK
    plain = min(max(O_Q - first, 0), W_IN_BLOCK)
    pieces = [(0, first, plain)] if plain else []
    return pieces + [(r, _padded_row(first + r), HD) for r in range(plain, W_IN_BLOCK, HD)]


def _inproj_fwd(x, g1, gi, own_i, chip, tm):
    t = x.shape[0]

    def body(chip_ref, x_ref, g_ref, gi_ref, own_ref, p_ref, h_ref, w_ref, sem):
        @pl.when(pl.program_id(0) == 0)
        def _():
            for k in range(N_CHIPS):
                for src, dst, rows in _w_in_pieces(k):
                    @pl.when(chip_ref[0] == k)
                    def _():
                        pltpu.make_async_copy(own_ref.at[pl.ds(src, rows)], w_ref.at[pl.ds(dst, rows)], sem).start()

                    @pl.when(chip_ref[0] != k)
                    def _():
                        pltpu.make_async_copy(gi_ref.at[k, pl.ds(src, rows)], w_ref.at[pl.ds(dst, rows)], sem).start()
            for slot in range(NQ + 2 * NKV):
                w_ref[O_Q + slot * HP + HD:O_Q + (slot + 1) * HP, :] = jnp.zeros((HP - HD, D), BF16)
            landed = w_ref.at[pl.ds(0, W_IN_ROWS)]
            pltpu.make_async_copy(landed, landed, sem).wait()

        _, xh = _rms_fwd(x_ref[...], 1.0 / D)
        h = (xh * g_ref[...]).astype(BF16)
        h_ref[...] = h
        p_ref[...] = _nt(h, w_ref[...])

    const = lambda shape: pl.BlockSpec(shape, lambda i, c: (0,) * len(shape))
    return pl.pallas_call(
        body, name="inproj_fwd",
        grid_spec=pltpu.PrefetchScalarGridSpec(
            num_scalar_prefetch=1, grid=(t // tm,),
            in_specs=[pl.BlockSpec((tm, D), lambda i, c: (i, 0)), const((1, D)), ANY, ANY],
            out_specs=[pl.BlockSpec((tm, NP), lambda i, c: (i, 0)), pl.BlockSpec((tm, D), lambda i, c: (i, 0)),
                       const((NP, D))],
            scratch_shapes=[pltpu.SemaphoreType.DMA(())]),
        out_shape=[jax.ShapeDtypeStruct((t, NP), F32), jax.ShapeDtypeStruct((t, D), BF16),
                   jax.ShapeDtypeStruct((NP, D), BF16)],
        compiler_params=_cparams(("arbitrary",)),
    )(chip, x, g1, gi, own_i)


def _band_mask():
    r_io = lax.broadcasted_iota(jnp.int32, (BLK, 2 * BLK), 0)
    c_io = lax.broadcasted_iota(jnp.int32, (BLK, 2 * BLK), 1)
    return (c_io > r_io) & (c_io <= r_io + BLK), c_io


def _conv_taps(uf, n):
    u1 = pltpu.roll(uf, 1, 0)[8:8 + n]
    u2 = pltpu.roll(uf, 2, 0)[8:8 + n]
    return u1, u2


def _attn_probs(qs, kband, sink, valid):
    s = jnp.where(valid, _nt(qs, kband), NEG)
    m = jnp.maximum(jnp.max(s, axis=-1, keepdims=True), sink)
    p = jnp.exp(s - m)
    es = jnp.exp(sink - m)
    inv = 1.0 / (jnp.sum(p, axis=-1, keepdims=True) + es)
    return p * inv, es * inv


def _norm_keys(kraw, gk):
    out = []
    for h in range(NKV):
        kh = kraw[:, h * HP:(h + 1) * HP]
        rk, khat = _rms_fwd(kh, 1.0 / HD)
        out.append((khat, rk, (khat * gk).astype(BF16)))
    return out


def _mixer_fwd(proj, x, cw, gq, gk, sinks, gco, gao, wo, tq):
    t = proj.shape[0]
    nb = tq // BLK
    r8 = tq // 8

    def body(p_ref, cgp_ref, hcp_ref, kvp_ref, x_ref, cw_ref, gq_ref, gk_ref, sk_ref, gco_ref, gao_ref,
             wo_ref, xm_ref, mix_ref, ao_ref):
        i = pl.program_id(0)
        cg = p_ref[:, O_CG:O_CG + CC]
        hc = p_ref[:, O_HC:O_HC + CC]
        u = cg * hc
        up = jnp.where(i > 0, cgp_ref[...] * hcp_ref[...], 0.0)
        u1, u2 = _conv_taps(jnp.concatenate([up, u], axis=0), tq)
        y = cw_ref[0:1, :] * u2 + cw_ref[1:2, :] * u1 + cw_ref[2:3, :] * u
        co = p_ref[:, O_BG:O_BG + CC] * y
        _, coh = _rms_fwd(co, 1.0 / CC)
        cn = coh * gco_ref[...]
        kraw = jnp.concatenate([kvp_ref[:, 0:NKV * HP], p_ref[:, O_K:O_K + NKV * HP]], axis=0)
        vraw = jnp.concatenate([kvp_ref[:, NKV * HP:], p_ref[:, O_V:O_V + NKV * HP]], axis=0)
        keys = _norm_keys(kraw, gk_ref[...])
        vb = [vraw[:, h * HP:(h + 1) * HP].astype(BF16) for h in range(NKV)]
        base_valid, c_io = _band_mask()
        gqs = gq_ref[...] * SCALE
        for b in range(nb):
            lo = jnp.where(i * nb + b == 0, BLK, 0)
            valid = base_valid & (c_io >= lo)
            for g in range(NQ):
                h = g // GRP
                qg = p_ref[b * BLK:(b + 1) * BLK, O_Q + g * HP:O_Q + (g + 1) * HP]
                _, qh = _rms_fwd(qg, 1.0 / HD)
                qs = (qh * gqs).astype(BF16)
                pr, _ = _attn_probs(qs, keys[h][2][b * BLK:b * BLK + 2 * BLK], sk_ref[0, g], valid)
                ao_ref[b * BLK:(b + 1) * BLK, g * HP:(g + 1) * HP] = jnp.dot(
                    pr.astype(BF16), vb[h][b * BLK:b * BLK + 2 * BLK], preferred_element_type=F32)
        _, aoh = _rms_fwd(ao_ref[...], 1.0 / (NQ * HD))
        an = aoh * gao_ref[...]
        mix = jnp.concatenate([cn, an], axis=1).astype(BF16)
        mix_ref[...] = mix
        xm_ref[...] = x_ref[...] + jnp.dot(mix, wo_ref[...], preferred_element_type=F32)

    prev8 = lambda col: pl.BlockSpec((8, CC), lambda i: (jnp.maximum(i * r8 - 1, 0), col))
    return pl.pallas_call(
        body, name="mixer_fwd", grid=(t // tq,),
        in_specs=[
            pl.BlockSpec((tq, NP), lambda i: (i, 0)),
            prev8(O_CG // CC), prev8(O_HC // CC),
            pl.BlockSpec((BLK, 2 * NKV * HP), lambda i: (jnp.maximum(i * nb - 1, 0), O_K // (2 * NKV * HP))),
            pl.BlockSpec((tq, D), lambda i: (i, 0)),
            _const_spec((8, CC)), _const_spec((1, HP)), _const_spec((1, HP)),
            pl.BlockSpec(memory_space=pltpu.SMEM),
            _const_spec((1, CC)), _const_spec((1, NQ * HP)), _const_spec((MIXW, D)),
        ],
        out_specs=[pl.BlockSpec((tq, D), lambda i: (i, 0)), pl.BlockSpec((tq, MIXW), lambda i: (i, 0)),
                   pl.BlockSpec((tq, NQ * HP), lambda i: (i, 0))],
        out_shape=[jax.ShapeDtypeStruct((t, D), F32), jax.ShapeDtypeStruct((t, MIXW), BF16),
                   jax.ShapeDtypeStruct((t, NQ * HP), F32)],
        compiler_params=_cparams(("parallel",)),
    )(proj, proj, proj, proj, x, cw, gq, gk, sinks, gco, gao, wo)


def _ffn_weight_specs():
    return [pl.BlockSpec((N_CHIPS, FFB, D), lambda i, j=j: (0, j, 0), pipeline_mode=pl.Buffered(1))
            for j in range(3)]


def _ffn_fwd(xm, g2, gf, tm, tgt=None):
    t = xm.shape[0]
    last = tgt is not None

    def body(x_ref, g_ref, wg_ref, wu_ref, wd_ref, *rest):
        t_ref, rest = (rest[0], rest[1:]) if last else (None, rest)
        l_ref, rest = (rest[0], rest[1:]) if last else (None, rest)
        xo_ref, a_ref, b_ref, h2_ref = rest
        xv = x_ref[...]
        _, xh = _rms_fwd(xv, 1.0 / D)
        h2 = (xh * g_ref[...]).astype(BF16)
        h2_ref[...] = h2
        acc = xv
        for k in range(N_CHIPS):
            a = _nt(h2, wg_ref[k])
            b = _nt(h2, wu_ref[k])
            a_ref[k] = a.astype(BF16)
            b_ref[k] = b.astype(BF16)
            hm = (a * jax.nn.sigmoid(a) * b).astype(BF16)
            acc = acc + jnp.dot(hm, wd_ref[k], preferred_element_type=F32)
        if last:
            @pl.when(pl.program_id(0) == 0)
            def _():
                l_ref[...] = jnp.zeros_like(l_ref)

            e = acc - t_ref[...]
            xo_ref[...] = e * (1.0 / D)
            l_ref[...] += jnp.sum(jnp.sum(e * e, axis=-1, keepdims=True), axis=0, keepdims=True) * (0.5 / D)
        else:
            xo_ref[...] = acc

    row = lambda w: pl.BlockSpec((tm, w), lambda i: (i, 0))
    blk = pl.BlockSpec((N_CHIPS, tm, FFB), lambda i: (0, i, 0))
    bsd = jax.ShapeDtypeStruct((N_CHIPS, t, FFB), BF16)
    return pl.pallas_call(
        body, name="ffn_fwd_loss" if last else "ffn_fwd", grid=(t // tm,),
        in_specs=[row(D), _const_spec((1, D))] + _ffn_weight_specs() + ([row(D)] if last else []),
        out_specs=([pl.BlockSpec((8, 128), lambda i: (0, 0))] if last else []) + [row(D), blk, blk, row(D)],
        out_shape=([jax.ShapeDtypeStruct((8, 128), F32)] if last else [])
        + [jax.ShapeDtypeStruct((t, D), F32), bsd, bsd, jax.ShapeDtypeStruct((t, D), BF16)],
        compiler_params=_cparams(("arbitrary" if last else "parallel",)),
    )(*((xm, g2, gf, gf, gf) + ((tgt,) if last else ())))


def _ffn_bwd(dy, xm, g2, a, b, gf, tm):
    t = dy.shape[0]

    def body(dy_ref, x_ref, g_ref, a_ref, b_ref, wg_ref, wu_ref, wd_ref, dx_ref, da_ref, db_ref, hm_ref, dg_ref):
        @pl.when(pl.program_id(0) == 0)
        def _():
            dg_ref[...] = jnp.zeros_like(dg_ref)

        dyv = dy_ref[...]
        dyb = dyv.astype(BF16)
        dh2 = jnp.zeros_like(dyv)
        for k in range(N_CHIPS):
            dhm = _nt(dyb, wd_ref[k])
            av = a_ref[k].astype(F32)
            bv = b_ref[k].astype(F32)
            sig = jax.nn.sigmoid(av)
            sil = av * sig
            hm_ref[k] = (sil * bv).astype(BF16)
            da = (dhm * bv * (sig * (1.0 + av * (1.0 - sig)))).astype(BF16)
            db = (dhm * sil).astype(BF16)
            da_ref[k] = da
            db_ref[k] = db
            dh2 = (dh2 + jnp.dot(da, wg_ref[k], preferred_element_type=F32)
                   + jnp.dot(db, wu_ref[k], preferred_element_type=F32))
        r, xh = _rms_fwd(x_ref[...], 1.0 / D)
        dg_ref[...] += jnp.sum(dh2 * xh, axis=0, keepdims=True)
        dx_ref[...] = dyv + _rms_bwd(dh2, g_ref[...], xh, r, 1.0 / D)

    row = lambda w: pl.BlockSpec((tm, w), lambda i: (i, 0))
    blk = pl.BlockSpec((N_CHIPS, tm, FFB), lambda i: (0, i, 0))
    bsd = jax.ShapeDtypeStruct((N_CHIPS, t, FFB), BF16)
    return pl.pallas_call(
        body, name="ffn_bwd", grid=(t // tm,),
        in_specs=[row(D), row(D), _const_spec((1, D)), blk, blk] + _ffn_weight_specs(),
        out_specs=[row(D), blk, blk, blk, pl.BlockSpec((1, D), lambda i: (0, 0))],
        out_shape=[jax.ShapeDtypeStruct((t, D), F32), bsd, bsd, bsd, jax.ShapeDtypeStruct((1, D), F32)],
        compiler_params=_cparams(("arbitrary",)),
    )(dy, xm, g2, a, b, gf, gf, gf)


def _wgrad_blocks(a, b, tt, name):
    _, t, rows = a.shape
    cols = b.shape[1]
    nsteps = t // tt

    def body(a_ref, b_ref, o_ref, acc_ref):
        s = pl.program_id(0)

        @pl.when(s == 0)
        def _():
            acc_ref[...] = jnp.zeros_like(acc_ref)

        bv = b_ref[...].astype(BF16)
        for k in range(N_CHIPS):
            acc_ref[k] += _tn(a_ref[k], bv)

        @pl.when(s == nsteps - 1)
        def _():
            o_ref[...] = acc_ref[...].astype(BF16)

    return pl.pallas_call(
        body, name=name, grid=(nsteps,),
        in_specs=[pl.BlockSpec((N_CHIPS, tt, rows), lambda s: (0, s, 0)), pl.BlockSpec((tt, cols), lambda s: (s, 0))],
        out_specs=pl.BlockSpec((N_CHIPS, rows, cols), lambda s: (0, 0, 0)),
        out_shape=jax.ShapeDtypeStruct((N_CHIPS, rows, cols), BF16),
        scratch_shapes=[pltpu.VMEM((N_CHIPS, rows, cols), F32)],
        compiler_params=_cparams(("arbitrary",)),
    )(a, b)


def _head_rows(first, n_heads):
    return [(first + g * HD, first + g * HP, HD) for g in range(n_heads)]


def _wgrad(a, b, tt, name, pieces=None):
    t, k = a.shape
    n = b.shape[1]
    nsteps = t // tt
    pieces = pieces or [(0, 0, k)]
    rows = sum(p[2] for p in pieces)

    def body(a_ref, b_ref, o_ref, acc_ref):
        s = pl.program_id(0)

        @pl.when(s == 0)
        def _():
            acc_ref[...] = jnp.zeros_like(acc_ref)

        acc_ref[...] += _tn(a_ref[...].astype(BF16), b_ref[...].astype(BF16))

        @pl.when(s == nsteps - 1)
        def _():
            for dst, src, size in pieces:
                o_ref[dst:dst + size, :] = acc_ref[src:src + size, :].astype(BF16)

    return pl.pallas_call(
        body, name=name, grid=(nsteps,),
        in_specs=[pl.BlockSpec((tt, k), lambda s: (s, 0)), pl.BlockSpec((tt, n), lambda s: (s, 0))],
        out_specs=pl.BlockSpec((rows, n), lambda s: (0, 0)),
        out_shape=jax.ShapeDtypeStruct((rows, n), BF16),
        scratch_shapes=[pltpu.VMEM((k, n), F32)],
        compiler_params=_cparams(("arbitrary",)),
    )(a, b)


def _mixer_bwd(dxm, proj, ao, cw, gq, gk, sinks, gco, gao, wo, tq):
    t = proj.shape[0]
    nb = tq // BLK
    r8 = tq // 8
    nt = t // tq
    te = tq + 8
    kvw = 2 * NKV * HP

    def body(dx_ref, dxn_ref, p_ref, cgp_ref, hcp_ref, bgn_ref, cgn_ref, hcn_ref, kvp_ref, ao_ref, cw_ref, gq_ref,
             gk_ref, sk_ref, gco_ref, gao_ref, wo_ref,
             dpm_ref, dkvm_ref, dkvh_ref, dcw_ref, dgq_ref, dgk_ref, dsk_ref, dgco_ref, dgao_ref, acc_ref):
        i = pl.program_id(0)

        @pl.when(i == 0)
        def _():
            for r in (dcw_ref, dgq_ref, dgk_ref, dsk_ref, dgco_ref, dgao_ref):
                r[...] = jnp.zeros_like(r)

        acc_ref[...] = jnp.zeros_like(acc_ref)
        live_rows = jnp.where(i < nt - 1, te, tq)
        dxb = dx_ref[...].astype(BF16)
        dxe = jnp.concatenate([dxb, dxn_ref[...].astype(BF16)], axis=0)
        dcn = _nt(dxe, wo_ref[0:CC, :])
        bg = jnp.concatenate([p_ref[:, O_BG:O_BG + CC], bgn_ref[...]], axis=0)
        cg = jnp.concatenate([p_ref[:, O_CG:O_CG + CC], cgn_ref[...]], axis=0)
        hc = jnp.concatenate([p_ref[:, O_HC:O_HC + CC], hcn_ref[...]], axis=0)
        u = cg * hc
        up = jnp.where(i > 0, cgp_ref[...] * hcp_ref[...], 0.0)
        u1, u2 = _conv_taps(jnp.concatenate([up, u], axis=0), te)
        w0, w1, w2 = cw_ref[0:1, :], cw_ref[1:2, :], cw_ref[2:3, :]
        y = w0 * u2 + w1 * u1 + w2 * u
        co = bg * y
        rc, coh = _rms_fwd(co, 1.0 / CC)
        dco = _rms_bwd(dcn, gco_ref[...], coh, rc, 1.0 / CC)
        row_io = lax.broadcasted_iota(jnp.int32, (te, 1), 0)
        own = row_io < tq
        dgco_ref[...] += jnp.sum(jnp.where(own, dcn * coh, 0.0), axis=0, keepdims=True)
        dyc = jnp.where(row_io < live_rows, dco * bg, 0.0)
        dyo = jnp.where(own, dyc, 0.0)
        dcw_ref[0:1, :] += jnp.sum(dyo * u2, axis=0, keepdims=True)
        dcw_ref[1:2, :] += jnp.sum(dyo * u1, axis=0, keepdims=True)
        dcw_ref[2:3, :] += jnp.sum(dyo * u, axis=0, keepdims=True)
        dy1 = pltpu.roll(dyc, te - 1, 0)[0:tq]
        dy2 = pltpu.roll(dyc, te - 2, 0)[0:tq]
        du = w2 * dyc[0:tq] + w1 * dy1 + w0 * dy2
        dpm_ref[:, O_BG:O_BG + CC] = (dco[0:tq] * y[0:tq]).astype(BF16)
        dpm_ref[:, O_CG:O_CG + CC] = (du * hc[0:tq]).astype(BF16)
        dpm_ref[:, O_HC:O_HC + CC] = (du * cg[0:tq]).astype(BF16)
        kraw = jnp.concatenate([kvp_ref[:, 0:NKV * HP], p_ref[:, O_K:O_K + NKV * HP]], axis=0)
        vraw = jnp.concatenate([kvp_ref[:, NKV * HP:], p_ref[:, O_V:O_V + NKV * HP]], axis=0)
        gqv, gkv = gq_ref[...], gk_ref[...]
        keys = _norm_keys(kraw, gkv)
        vb = [vraw[:, h * HP:(h + 1) * HP].astype(BF16) for h in range(NKV)]
        base_valid, c_io = _band_mask()
        lane = lax.broadcasted_iota(jnp.int32, (1, HP), 1)
        dgq, dgk, dsk = (jnp.zeros((1, HP), F32) for _ in range(3))
        dgao = jnp.zeros((1, NQ * HP), F32)
        for b in range(nb):
            lo = jnp.where(i * nb + b == 0, BLK, 0)
            valid = base_valid & (c_io >= lo)
            band = slice(b * BLK, b * BLK + 2 * BLK)
            blk = slice(b * BLK, (b + 1) * BLK)
            ra, aoh = _rms_fwd(ao_ref[blk, :], 1.0 / (NQ * HD))
            danb = _nt(dxb[blk], wo_ref[CC:MIXW, :])
            dgao = dgao + jnp.sum(danb * aoh, axis=0, keepdims=True)
            dao = _rms_bwd(danb, gao_ref[...], aoh, ra, 1.0 / (NQ * HD))
            fwd = []
            for g in range(NQ):
                rq, qh = _rms_fwd(p_ref[blk, O_Q + g * HP:O_Q + (g + 1) * HP], 1.0 / HD)
                qs = (qh * (gqv * SCALE)).astype(BF16)
                fwd.append((rq, qh, qs) + _attn_probs(qs, keys[g // GRP][2][band], sk_ref[0, g], valid))
            dqs = []
            for h in range(NKV):
                khat, rk, kn = [a[band] for a in keys[h]]
                dss, prbs, qns, dobs = [], [], [], []
                for g in range(h * GRP, (h + 1) * GRP):
                    rq, qh, qs, pr, ps = fwd[g]
                    dob = dao[:, g * HP:(g + 1) * HP].astype(BF16)
                    dp = _nt(dob, vb[h][band])
                    delta = jnp.sum(pr * dp, axis=-1, keepdims=True)
                    dsb = (pr * (dp - delta)).astype(BF16)
                    dsk = dsk + jnp.where(lane == g, -jnp.sum(ps * delta, axis=0, keepdims=True), 0.0)
                    dqn = jnp.dot(dsb, kn, preferred_element_type=F32) * SCALE
                    dgq = dgq + jnp.sum(dqn * qh, axis=0, keepdims=True)
                    dqs.append(_rms_bwd(dqn, gqv, qh, rq, 1.0 / HD).astype(BF16))
                    dss.append(dsb)
                    prbs.append(pr.astype(BF16))
                    qns.append(qs)
                    dobs.append(dob)
                dkn = _tn(jnp.concatenate(dss, axis=0), jnp.concatenate(qns, axis=0))
                dv = _tn(jnp.concatenate(prbs, axis=0), jnp.concatenate(dobs, axis=0))
                dgk = dgk + jnp.sum(dkn * khat, axis=0, keepdims=True)
                acc_ref[band, h * HP:(h + 1) * HP] += _rms_bwd(dkn, gkv, khat, rk, 1.0 / HD)
                acc_ref[band, (NKV + h) * HP:(NKV + h + 1) * HP] += dv
            dpm_ref[blk, O_Q:O_K] = jnp.concatenate(dqs, axis=1)
        dgq_ref[...] += dgq
        dgk_ref[...] += dgk
        dsk_ref[...] += dsk
        dgao_ref[...] += dgao
        dkvh_ref[...] = acc_ref[0:BLK, :]
        dkvm_ref[...] = acc_ref[BLK:, :]

    prev8 = lambda col: pl.BlockSpec((8, CC), lambda i: (jnp.maximum(i * r8 - 1, 0), col))
    next8 = lambda col: pl.BlockSpec((8, CC), lambda i: (jnp.minimum((i + 1) * r8, t // 8 - 1), col))
    small = lambda n: pl.BlockSpec((1, n), lambda i: (0, 0))
    return pl.pallas_call(
        body, name="mixer_bwd", grid=(nt,),
        in_specs=[
            pl.BlockSpec((tq, D), lambda i: (i, 0)),
            pl.BlockSpec((8, D), lambda i: (jnp.minimum((i + 1) * r8, t // 8 - 1), 0)),
            pl.BlockSpec((tq, NP), lambda i: (i, 0)),
            prev8(O_CG // CC), prev8(O_HC // CC),
            next8(O_BG // CC), next8(O_CG // CC), next8(O_HC // CC),
            pl.BlockSpec((BLK, kvw), lambda i: (jnp.maximum(i * nb - 1, 0), O_K // kvw)),
            pl.BlockSpec((tq, NQ * HP), lambda i: (i, 0)),
            _const_spec((8, CC)), _const_spec((1, HP)), _const_spec((1, HP)),
            pl.BlockSpec(memory_space=pltpu.SMEM),
            _const_spec((1, CC)), _const_spec((1, NQ * HP)), _const_spec((MIXW, D)),
        ],
        out_specs=[
            pl.BlockSpec((tq, NMAIN), lambda i: (i, 0)),
            pl.BlockSpec((tq, kvw), lambda i: (i, 0)),
            pl.BlockSpec((BLK, kvw), lambda i: (i, 0)),
            pl.BlockSpec((8, CC), lambda i: (0, 0)), small(HP), small(HP), small(HP), small(CC), small(NQ * HP),
        ],
        out_shape=[
            jax.ShapeDtypeStruct((t, NMAIN), BF16), jax.ShapeDtypeStruct((t, kvw), F32),
            jax.ShapeDtypeStruct((nt * BLK, kvw), F32),
            jax.ShapeDtypeStruct((8, CC), F32), jax.ShapeDtypeStruct((1, HP), F32), jax.ShapeDtypeStruct((1, HP), F32),
            jax.ShapeDtypeStruct((1, HP), F32), jax.ShapeDtypeStruct((1, CC), F32),
            jax.ShapeDtypeStruct((1, NQ * HP), F32),
        ],
        scratch_shapes=[pltpu.VMEM((tq + BLK, kvw), F32)],
        compiler_params=_cparams(("arbitrary",)),
    )(dxm, dxm, proj, proj, proj, proj, proj, proj, proj, ao, cw, gq, gk, sinks, gco, gao, wo)


def _inproj_bwd(dpm, dkvm, dkvh, wpt, x, g1, dxm, tm):
    t = x.shape[0]
    kvw = 2 * NKV * HP
    nt = t // tm

    def body(dp_ref, dk_ref, dh_ref, w_ref, x_ref, g_ref, dxm_ref, dx_ref, dg_ref, dkv_ref):
        i = pl.program_id(0)

        @pl.when(i == 0)
        def _():
            dg_ref[...] = jnp.zeros_like(dg_ref)

        halo = jnp.where(i < nt - 1, dh_ref[...], 0.0)
        dkv_ref[0:tm - BLK, :] = dk_ref[0:tm - BLK, :].astype(BF16)
        dkv_ref[tm - BLK:tm, :] = (dk_ref[tm - BLK:tm, :] + halo).astype(BF16)
        dh = (jnp.dot(dp_ref[...], w_ref[0:NMAIN, :], preferred_element_type=F32)
              + jnp.dot(dkv_ref[...], w_ref[NMAIN:NP, :], preferred_element_type=F32))
        r, xh = _rms_fwd(x_ref[...], 1.0 / D)
        dg_ref[...] += jnp.sum(dh * xh, axis=0, keepdims=True)
        dx_ref[...] = dxm_ref[...] + _rms_bwd(dh, g_ref[...], xh, r, 1.0 / D)

    row = lambda w: pl.BlockSpec((tm, w), lambda i: (i, 0))
    return pl.pallas_call(
        body, name="inproj_bwd", grid=(nt,),
        in_specs=[row(NMAIN), row(kvw), pl.BlockSpec((BLK, kvw), lambda i: (jnp.minimum(i + 1, nt - 1), 0)),
                  _const_spec((NP, D)), row(D), _const_spec((1, D)), row(D)],
        out_specs=[row(D), pl.BlockSpec((1, D), lambda i: (0, 0)), row(kvw)],
        out_shape=[jax.ShapeDtypeStruct((t, D), F32), jax.ShapeDtypeStruct((1, D), F32),
                   jax.ShapeDtypeStruct((t, kvw), BF16)],
        compiler_params=_cparams(("arbitrary",)),
    )(dpm, dkvm, dkvh, wpt, x, g1, dxm)


def _rows_tile(rows, cap=512):
    for cand in range(min(rows, cap) // 16 * 16, 0, -16):
        if rows % cand == 0:
            return cand
    return rows


def _presum_halves(gs, theirs, core):
    n = len(gs)

    def body(c_ref, *refs):
        for g_ref, t_ref, o_ref in zip(refs[:n], refs[n:2 * n], refs[2 * n:]):
            o_ref[...] = (g_ref[...].astype(F32) + t_ref[...].astype(F32)).astype(BF16)

    half = lambda ta: pl.BlockSpec((None,) + ta.shape[1:], lambda k, c_ref: (k, 0, 0))
    own = lambda ta: pl.BlockSpec((None,) + ta.shape[1:], lambda k, c_ref: (k, c_ref[0], 0))
    return pl.pallas_call(
        body, name="presum",
        grid_spec=pltpu.PrefetchScalarGridSpec(
            num_scalar_prefetch=1, grid=(N_CHIPS,),
            in_specs=[own(ta) for ta in theirs] + [half(ta) for ta in theirs],
            out_specs=[half(ta) for ta in theirs]),
        out_shape=[jax.ShapeDtypeStruct(ta.shape, BF16) for ta in theirs],
        compiler_params=_cparams(("parallel",)),
    )(core, *gs, *theirs)


def _sum_chips(cs):
    n = len(cs)
    steps = 2

    def body(*refs):
        for c_ref, o_ref in zip(refs[:n], refs[n:]):
            acc = c_ref[0].astype(F32)
            for j in range(1, N_CHIPS):
                acc = acc + c_ref[j].astype(F32)
            o_ref[...] = acc

    return pl.pallas_call(
        body, name="chipsum", grid=(steps,),
        in_specs=[pl.BlockSpec((N_CHIPS, c.shape[1] // steps, c.shape[2]), lambda i: (0, i, 0)) for c in cs],
        out_specs=[pl.BlockSpec((c.shape[1] // steps, c.shape[2]), lambda i: (i, 0)) for c in cs],
        out_shape=[jax.ShapeDtypeStruct(c.shape[1:], F32) for c in cs],
        compiler_params=_cparams(("parallel",)),
    )(*cs)


def _adamw(w, g, m, v, name):
    rows, cols = w.shape
    tr = _rows_tile(rows, 256)
    c1 = 1.0 - ADAM_B1 ** ADAM_STEP
    c2 = 1.0 - ADAM_B2 ** ADAM_STEP

    def body(w_ref, g_ref, m_ref, v_ref, d_ref, mo_ref, vo_ref):
        gv = g_ref[...]
        mn = ADAM_B1 * m_ref[...] + (1.0 - ADAM_B1) * gv
        vn = ADAM_B2 * v_ref[...] + (1.0 - ADAM_B2) * (gv * gv)
        mo_ref[...] = mn
        vo_ref[...] = vn
        d_ref[...] = -ADAM_LR * ((mn / c1) / (jnp.sqrt(vn / c2) + ADAM_EPS) + ADAM_WD * w_ref[...])

    spec = pl.BlockSpec((tr, cols), lambda i: (i, 0))
    sds = jax.ShapeDtypeStruct((rows, cols), F32)
    return pl.pallas_call(
        body, name=name, grid=(rows // tr,), in_specs=[spec] * 4, out_specs=[spec] * 3, out_shape=[sds] * 3,
        compiler_params=_cparams(("parallel",)),
    )(w, g, m, v)


def _place():
    x, y, c = lax.axis_index("x"), lax.axis_index("y"), lax.axis_index("c")
    chips = [(1 - x, y), (x, 1 - y), (1 - x, 1 - y)]
    return x, y, c, chips


ANY = pl.BlockSpec(memory_space=pl.ANY)
DMA_ROWS = 64


def _pieces(shape):
    rows = shape[-2]
    step = DMA_ROWS if rows % DMA_ROWS == 0 else rows
    lead = [()]
    for n in shape[:-2]:
        lead = [i + (k,) for i in lead for k in range(n)]
    return [i + (pl.ds(r0, step),) for i in lead for r0 in range(0, rows, step)]


def _start_pieces(make, src, dst):
    for idx in _pieces(src.shape):
        make(src.at[idx], dst.at[idx]).start()


def _gather_layer(blocks, layer):
    nw = len(blocks)

    def body(*refs):
        _gather_body(refs[:nw], refs[nw:2 * nw], refs[2 * nw:], layer, _start_pieces)

    return pl.pallas_call(
        body, name=f"gather_layer{layer}", in_specs=[ANY] * nw, out_specs=[ANY] * nw,
        out_shape=[jax.ShapeDtypeStruct((N_CHIPS,) + b.shape, b.dtype) for b in blocks],
        scratch_shapes=[pltpu.SemaphoreType.DMA((3, nw))] * 4,
        compiler_params=_cparams(has_side_effects=True),
    )(*blocks)


def _gather_body(srcs, outs, sems, layer, start):
    nw = len(srcs)
    ssem, rsem, fssem, frsem = sems
    x, y, c, chips = _place()
    kme = 2 * x + y

    def plane(j, w, to):
        return lambda s, d: pltpu.make_async_remote_copy(
            src_ref=s, dst_ref=d, send_sem=ssem.at[j, w], recv_sem=rsem.at[j, w], device_id=to,
            device_id_type=MESH)

    def passed(j, w):
        return lambda s, d: pltpu.make_async_remote_copy(
            src_ref=s, dst_ref=d, send_sem=fssem.at[j, w], recv_sem=frsem.at[j, w],
            device_id=(x, y, 1 - c), device_id_type=MESH)

    @pl.when(c == layer)
    def _():
        for j, (px, py) in enumerate(chips):
            for w in range(nw):
                start(plane(j, w, (px, py, c)), srcs[w], outs[w].at[kme])
        for j, (px, py) in enumerate(chips):
            for w in range(nw):
                got = outs[w].at[2 * px + py]
                plane(j, w, (px, py, c))(got, got).wait_recv()
                start(passed(j, w), got, got)
        for j, (px, py) in enumerate(chips):
            for w in range(nw):
                got = outs[w].at[2 * px + py]
                plane(j, w, (px, py, c))(got, got).wait_send()
                passed(j, w)(got, got).wait_send()

    @pl.when(c != layer)
    def _():
        for j, (px, py) in enumerate(chips):
            for w in range(nw):
                got = outs[w].at[2 * px + py]
                passed(j, w)(got, got).wait_recv()


def _handshake_all():
    x, y, c, _ = _place()
    barrier = pltpu.get_barrier_semaphore()
    for r in range(1, 8):
        peer = (x ^ (r >> 2), y ^ ((r >> 1) & 1), c ^ (r & 1))
        pl.semaphore_signal(barrier, inc=1, device_id=peer, device_id_type=MESH)
    pl.semaphore_wait(barrier, 7)


def _gather_layer_async(blocks, layer, name, collective_id):
    hbm = pltpu.MemorySpace.HBM
    srcs = [jax.new_ref(b, memory_space=hbm) for b in blocks]
    outs = [jax.empty_ref(jax.ShapeDtypeStruct((N_CHIPS,) + b.shape, b.dtype), memory_space=hbm) for b in blocks]

    @pl.kernel(mesh=plsc.ScalarSubcoreMesh(axis_name="seq", num_cores=1), name=name,
               scratch_types=[pltpu.SemaphoreType.DMA((3, len(blocks)))] * 4,
               compiler_params=pltpu.CompilerParams(collective_id=collective_id))
    def launch(*sems):
        _handshake_all()
        _gather_body(srcs, outs, sems, layer, lambda make, s, d: make(s, d).start())

    launch()
    return [o[...] for o in outs]


def _swap_halves(gs):
    nw = len(gs)

    def body(*refs):
        srcs, theirs = refs[:nw], refs[nw:2 * nw]
        ssem, rsem = refs[2 * nw:]
        x, y, c, _ = _place()

        def give(w):
            return lambda s, d: pltpu.make_async_remote_copy(
                src_ref=s, dst_ref=d, send_sem=ssem.at[w], recv_sem=rsem.at[w], device_id=(x, y, 1 - c),
                device_id_type=MESH)

        for w in range(nw):
            hr = theirs[w].shape[1]
            _start_pieces(give(w), srcs[w].at[:, pl.ds((1 - c) * hr, hr)], theirs[w])
        for w in range(nw):
            give(w)(theirs[w], theirs[w]).wait()

    return pl.pallas_call(
        body, name="swap_halves", in_specs=[ANY] * nw, out_specs=[ANY] * nw,
        out_shape=[jax.ShapeDtypeStruct((g.shape[0], g.shape[1] // 2, g.shape[2]), g.dtype) for g in gs],
        scratch_shapes=[pltpu.SemaphoreType.DMA((nw,))] * 2,
        compiler_params=_cparams(has_side_effects=True),
    )(*gs)


def _scatter_chips(ps):
    nw = len(ps)

    def body(*refs):
        _scatter_body(refs[:nw], refs[nw:2 * nw], refs[2 * nw:], _start_pieces)

    return pl.pallas_call(
        body, name="scatter_chips", in_specs=[ANY] * nw, out_specs=[ANY] * nw,
        out_shape=[jax.ShapeDtypeStruct(p.shape, p.dtype) for p in ps],
        scratch_shapes=[pltpu.SemaphoreType.DMA((3, nw)), pltpu.SemaphoreType.DMA((3, nw))],
        compiler_params=_cparams(has_side_effects=True),
    )(*ps)


def _scatter_body(srcs, outs, sems, start):
    nw = len(srcs)
    ssem, rsem = sems
    x, y, c, chips = _place()
    kme = 2 * x + y

    def give(j, w, to):
        return lambda s, d: pltpu.make_async_remote_copy(
            src_ref=s, dst_ref=d, send_sem=ssem.at[j, w], recv_sem=rsem.at[j, w], device_id=to,
            device_id_type=MESH)

    for j, (px, py) in enumerate(chips):
        for w in range(nw):
            start(give(j, w, (px, py, c)), srcs[w].at[2 * px + py], outs[w].at[kme])
    for j, (px, py) in enumerate(chips):
        for w in range(nw):
            got = outs[w].at[2 * px + py]
            give(j, w, (px, py, c))(got, got).wait_recv()
    for j, (px, py) in enumerate(chips):
        for w in range(nw):
            sent = srcs[w].at[2 * px + py]
            give(j, w, (px, py, c))(sent, sent).wait_send()


def _scatter_chips_async(ps, name, collective_id):
    hbm = pltpu.MemorySpace.HBM
    srcs = [jax.new_ref(p, memory_space=hbm) for p in ps]
    outs = [jax.empty_ref(jax.ShapeDtypeStruct(p.shape, p.dtype), memory_space=hbm) for p in ps]

    @pl.kernel(mesh=plsc.ScalarSubcoreMesh(axis_name="seq", num_cores=1), name=name,
               scratch_types=[pltpu.SemaphoreType.DMA((3, len(ps)))] * 2,
               compiler_params=pltpu.CompilerParams(collective_id=collective_id))
    def launch(*sems):
        _handshake_all()
        _scatter_body(srcs, outs, sems, lambda make, s, d: make(s, d).start())

    launch()
    return [o[...] for o in outs]


def _swap_siblings(rs):
    nw = len(rs)

    def body(*refs):
        srcs, outs = refs[:nw], refs[nw:2 * nw]
        ssem, rsem = refs[2 * nw:]
        x, y, c, _ = _place()

        def give(w):
            return lambda s, d: pltpu.make_async_remote_copy(
                src_ref=s, dst_ref=d, send_sem=ssem.at[w], recv_sem=rsem.at[w], device_id=(x, y, 1 - c),
                device_id_type=MESH)

        for w in range(nw):
            _start_pieces(give(w), srcs[w], outs[w])
        for w in range(nw):
            give(w)(srcs[w], outs[w]).wait()

    return pl.pallas_call(
        body, name="swap_siblings", in_specs=[ANY] * nw, out_specs=[ANY] * nw,
        out_shape=[jax.ShapeDtypeStruct(r.shape, r.dtype) for r in rs],
        scratch_shapes=[pltpu.SemaphoreType.DMA((nw,))] * 2,
        compiler_params=_cparams(has_side_effects=True),
    )(*rs)


def _allreduce_small(v):
    rows = v.shape[0]

    def body(v_ref, o_ref, buf, ssem, rsem):
        x, y, c, _ = _place()
        me = 4 * x + 2 * y + c
        buf[me] = v_ref[...]
        sends = []
        for r in range(1, 8):
            peer = (x ^ (r >> 2), y ^ ((r >> 1) & 1), c ^ (r & 1))
            cp = pltpu.make_async_remote_copy(
                src_ref=v_ref, dst_ref=buf.at[me], send_sem=ssem.at[r - 1], recv_sem=rsem.at[r - 1],
                device_id=peer, device_id_type=MESH)
            cp.start()
            sends.append(cp)
        for r in range(1, 8):
            src = me ^ r
            pltpu.make_async_remote_copy(
                src_ref=v_ref, dst_ref=buf.at[src], send_sem=ssem.at[r - 1], recv_sem=rsem.at[r - 1],
                device_id=(x, y, c), device_id_type=MESH).wait_recv()
        for cp in sends:
            cp.wait_send()
        acc = buf[0]
        for d in range(1, 8):
            acc = acc + buf[d]
        o_ref[...] = acc

    vm = pl.BlockSpec(memory_space=pltpu.VMEM)
    return pl.pallas_call(
        body, name="allreduce_small", in_specs=[vm], out_specs=vm,
        out_shape=jax.ShapeDtypeStruct(v.shape, F32),
        scratch_shapes=[pltpu.VMEM((8, rows, 128), F32), pltpu.SemaphoreType.DMA((7,)),
                        pltpu.SemaphoreType.DMA((7,))],
        compiler_params=_cparams(has_side_effects=True),
    )(v)


def _pad_heads(w, n_heads, axis):
    shp = w.shape
    w = w.reshape(shp[:axis] + (n_heads, HD) + shp[axis + 1:])
    pad = [(0, 0)] * w.ndim
    pad[axis + 1] = (0, HP - HD)
    w = jnp.pad(w, pad)
    return w.reshape(shp[:axis] + (n_heads * HP,) + shp[axis + 1:])


def _strip_heads(w, n_heads, axis):
    shp = w.shape
    w = w.reshape(shp[:axis] + (n_heads, HP) + shp[axis + 1:])
    w = lax.slice_in_dim(w, 0, HD, axis=axis + 1)
    return w.reshape(shp[:axis] + (n_heads * HD,) + shp[axis + 1:])


def _t(w):
    return jnp.swapaxes(w, -1, -2)


def _count(shape):
    n = 1
    for s in shape:
        n *= s
    return n


def _pack_rows(arrs):
    flat = [jnp.pad(a.reshape(-1), (0, (-_count(a.shape)) % 128)) for a in arrs]
    v = jnp.concatenate(flat)
    rows = -(-v.shape[0] // (8 * 128)) * 8
    return jnp.pad(v, (0, rows * 128 - v.shape[0])).reshape(rows, 128)


def kernel(x, norm1_g, w_in, conv_w, q_norm_g, k_norm_g, sinks, conv_out_g, attn_out_g, w_o, norm2_g, w_gate, w_up, w_down, loss_target, m_norm1_g, m_w_in, m_conv_w, m_q_norm_g, m_k_norm_g, m_sinks, m_conv_out_g, m_attn_out_g, m_w_o, m_norm2_g, m_w_gate, m_w_up, m_w_down, v_norm1_g, v_w_in, v_conv_w, v_q_norm_g, v_k_norm_g, v_sinks, v_conv_out_g, v_attn_out_g, v_w_o, v_norm2_g, v_w_gate, v_w_up, v_w_down):
    depth = w_in.shape[0]
    t = x.shape[1]
    xs = x.reshape(t, D)
    tgt = loss_target.reshape(t, D)
    xi, yi = lax.axis_index("x"), lax.axis_index("y")
    kme = 2 * xi + yi
    tm = min(512, t)
    tq = min(512, t)
    tf = min(256, t)
    tw = min(1024, t)

    cwp = jnp.pad(conv_w.reshape(depth * 3, CC // N_CHIPS), ((0, 8 - depth * 3), (0, 0)))
    own_f = [jnp.concatenate([_t(w_gate[l]), _t(w_up[l]), w_down[l]], axis=0).astype(BF16) for l in range(depth)]
    own_o = [w_o[l].astype(BF16) for l in range(depth)]
    own_i = [_t(w_in[l]).astype(BF16) for l in range(depth)]
    mine = lambda got, own: lax.dynamic_update_index_in_dim(got, own, kme, 0)
    got_i0, got_o0, got_cw = _gather_layer([own_i[0], own_o[0], cwp], 0)
    gf0_in = lax.optimization_barrier((own_f[0], got_i0))[0]
    (got_f0,) = _gather_layer_async([gf0_in], 0, "gather_ffn0_seq", collective_id=6)
    cw_full = mine(got_cw, cwp).transpose(1, 0, 2).reshape(8, CC)[:depth * 3].reshape(depth, 3, CC)

    chip = kme.reshape(1).astype(jnp.int32)

    def layer_params(l, got_o):
        wo = mine(got_o, own_o[l]).reshape(D, D)
        return dict(
            wo=jnp.concatenate([wo[:CC], _pad_heads(wo[CC:], NQ, 0)], axis=0),
            cw=jnp.pad(cw_full[l], ((0, 5), (0, 0))),
            g1=norm1_g[l].reshape(1, D), g2=norm2_g[l].reshape(1, D),
            gq=jnp.pad(q_norm_g[l], (0, HP - HD)).reshape(1, HP), gk=jnp.pad(k_norm_g[l], (0, HP - HD)).reshape(1, HP),
            sk=sinks[l].reshape(1, NQ), gco=conv_out_g[l].reshape(1, CC),
            gao=_pad_heads(attn_out_g[l], NQ, 0).reshape(1, NQ * HP))

    saved, layers = [], []
    cur = xs
    for l in range(depth):
        x_in = cur
        if l == 0:
            got_i, p = got_i0, layer_params(0, got_o0)
        else:
            got_f1, got_o1, got_i = lax.optimization_barrier((got_l1, cur))[0]
            p = layer_params(1, got_o1)
        proj, h, p["wpt"] = _inproj_fwd(cur, p["g1"], got_i, own_i[l], chip, tm)
        xm, mix, ao = _mixer_fwd(proj, cur, p["cw"], p["gq"], p["gk"], p["sk"], p["gco"], p["gao"], p["wo"], tq)
        if l == 0:
            got_f0 = lax.optimization_barrier((got_f0, xm))[0]
            l1_in = lax.optimization_barrier(([own_f[1], own_o[1], own_i[1]], got_f0))[0]
            got_l1 = _gather_layer_async(l1_in, 1, "gather_layer1_seq", collective_id=1)
        p["gf"] = mine(got_f0 if l == 0 else got_f1, own_f[l])
        layers.append(p)
        if l < depth - 1:
            cur, a, b, h2 = _ffn_fwd(xm, p["g2"], p["gf"], tm)
        else:
            lpart, dy, a, b, h2 = _ffn_fwd(xm, p["g2"], p["gf"], tm, tgt)
        saved.append(dict(x=x_in, proj=proj, h=h, xm=xm, mix=mix, ao=ao, a=a, b=b, h2=h2))
    loss = lax.psum(lpart[0, 0], ("x", "y", "c"))

    nt = t // tq
    ci = lax.axis_index("c")
    core = ci.reshape(1).astype(jnp.int32)
    rbig = [dict() for _ in range(depth)]
    gsmall = [None] * depth

    def reduce_start(gs, name, collective_id):
        ps = _presum_halves(gs, _swap_halves(gs), core)
        got = _scatter_chips(ps) if collective_id is None else _scatter_chips_async(ps, name, collective_id)
        return ps, got

    def reduce_finish(started, after):
        ps, got = started
        if after is not None:
            got = lax.optimization_barrier((got, after))[0]
        cs = [lax.dynamic_update_index_in_dim(g, lax.dynamic_index_in_dim(q, kme, 0, keepdims=False), kme, 0)
              for g, q in zip(got, ps)]
        r_mine = _sum_chips(cs)
        return [jnp.where(ci == 0, jnp.concatenate([a, b], axis=0), jnp.concatenate([b, a], axis=0))
                for a, b in zip(r_mine, _swap_siblings(r_mine))]

    in_flight = None
    for l in reversed(range(depth)):
        p, s = layers[l], saved[l]
        dxm, da, db, hm, dg2 = _ffn_bwd(dy, s["xm"], p["g2"], s["a"], s["b"], p["gf"], tf)
        g_wg = _wgrad_blocks(da, s["h2"], tw, "wgrad_gate")
        g_wu = _wgrad_blocks(db, s["h2"], tw, "wgrad_up")
        g_wd = _wgrad_blocks(hm, dy, tw, "wgrad_down")
        if in_flight is not None:
            rbig[l + 1]["in"], rbig[l + 1]["o"] = reduce_finish(in_flight, g_wd)
        ffn_flight = reduce_start([g_wg, g_wu, g_wd], f"scatter_ffn{l}_seq", 2 + 2 * l)
        dpm, dkvm, dkvh, dcw, dgq, dgk, dsk, dgco, dgao = _mixer_bwd(
            dxm, s["proj"], s["ao"], p["cw"], p["gq"], p["gk"], p["sk"], p["gco"], p["gao"], p["wo"], tq)
        g_o = _wgrad(s["mix"], dxm, tw, "wgrad_o", [(0, 0, CC)] + _head_rows(CC, NQ))
        dx, dg1, dkv = _inproj_bwd(dpm, dkvm, dkvh, p["wpt"], s["x"], p["g1"], dxm, tq)
        g_in = jnp.concatenate(
            [_wgrad(dpm, s["h"], tw, "wgrad_in_main", [(0, 0, O_Q)] + _head_rows(O_Q, NQ)),
             _wgrad(dkv, s["h"], tw, "wgrad_in_kv", _head_rows(0, 2 * NKV))], axis=0)
        dy = dx
        gsmall[l] = dict(g1=dg1, cw=dcw[:3], gq=dgq[0, :HD], gk=dgk[0, :HD], sk=dsk[0, :NQ], gco=dgco,
                         gao=_strip_heads(dgao.reshape(NQ * HP), NQ, 0), g2=dg2)
        rbig[l]["g"], rbig[l]["u"], rbig[l]["d"] = reduce_finish(ffn_flight, dx)
        in_flight = reduce_start([g_in.reshape(N_CHIPS, -1, D), g_o.reshape(N_CHIPS, -1, D)],
                                 f"scatter_in{l}_seq", 3 + 2 * l)
    grad_x = dy.reshape(x.shape)

    small_shapes = dict(g1=(D,), cw=(3, CC), gq=(HD,), gk=(HD,), sk=(NQ,), gco=(CC,), gao=(NQ * HD,), g2=(D,))
    red = _allreduce_small(_pack_rows([gsmall[l][n] for l in range(depth) for n in small_shapes])).reshape(-1)
    red_small, offs = {n: [] for n in small_shapes}, 0
    for l in range(depth):
        for n, shp in small_shapes.items():
            cnt = _count(shp)
            red_small[n].append(red[offs:offs + cnt].reshape(shp))
            offs += -(-cnt // 128) * 128
    g_small = {n: jnp.stack(v) for n, v in red_small.items()}
    g_cw = lax.dynamic_slice_in_dim(g_small["cw"], kme * (CC // N_CHIPS), CC // N_CHIPS, axis=2)

    weights = [norm1_g, w_in, conv_w, q_norm_g, k_norm_g, sinks, conv_out_g, attn_out_g, w_o, norm2_g, w_gate,
               w_up, w_down]
    moms = [m_norm1_g, m_w_in, m_conv_w, m_q_norm_g, m_k_norm_g, m_sinks, m_conv_out_g, m_attn_out_g, m_w_o,
            m_norm2_g, m_w_gate, m_w_up, m_w_down]
    vars_ = [v_norm1_g, v_w_in, v_conv_w, v_q_norm_g, v_k_norm_g, v_sinks, v_conv_out_g, v_attn_out_g, v_w_o,
             v_norm2_g, v_w_gate, v_w_up, v_w_down]
    n_w = len(weights)
    big_idx = dict(zip(("in", "o", "g", "u", "d"), (1, 8, 10, 11, 12)))
    small_idx = [n for n in range(n_w) if n not in big_idx.values()]
    grads, deltas, new_m, new_v = [None] * n_w, [None] * n_w, [None] * n_w, [None] * n_w
    for n, g in zip(small_idx, (g_small["g1"], g_cw, g_small["gq"], g_small["gk"], g_small["sk"], g_small["gco"],
                                g_small["gao"], g_small["g2"])):
        grads[n] = g

    def update_big(name):
        n = big_idx[name]
        g = jnp.stack([rbig[l][name] for l in range(depth)])
        flip = g.shape != weights[n].shape
        rows2d = lambda a3: (_t(a3) if flip else a3).reshape(-1, D)
        res = _adamw(rows2d(weights[n]), g.reshape(-1, D), rows2d(moms[n]), rows2d(vars_[n]), f"adamw_{n}")
        res = [g] + [r.reshape(g.shape) for r in res]
        grads[n], deltas[n], new_m[n], new_v[n] = [_t(r) for r in res] if flip else res

    for name in ("g", "u", "d"):
        update_big(name)
    rbig[0]["in"], rbig[0]["o"] = reduce_finish(in_flight, new_v[big_idx["d"]])
    for name in ("in", "o"):
        update_big(name)
    res = _adamw(*[_pack_rows([arrs[n] for n in small_idx]) for arrs in (weights, grads, moms, vars_)],
                 "adamw_small")
    offs = 0
    for n in small_idx:
        shp = weights[n].shape
        cnt = _count(shp)
        deltas[n], new_m[n], new_v[n] = [r.reshape(-1)[offs:offs + cnt].reshape(shp) for r in res]
        offs += -(-cnt // 128) * 128
    return (loss, grad_x, *grads, *deltas, *new_m, *new_v)
```

```python
import functools

import jax
import jax.numpy as jnp
from jax import lax
from jax.experimental import pallas as pl
from jax.experimental.pallas import tpu as pltpu
from jax.experimental.pallas import tpu_sc as plsc

F32 = jnp.float32
BF16 = jnp.bfloat16

D = 1024
CC = 512
NQ = 8
NKV = 2
HD = 64
HP = 128
GRP = NQ // NKV
FF = 2816
FFB = FF // 4
BLK = 128
EPS = 1e-6
NEG = -1e30
SCALE = HD ** -0.5
O_BG, O_CG, O_HC, O_Q = 0, CC, 2 * CC, 3 * CC
O_K = O_Q + NQ * HP
O_V = O_K + NKV * HP
NP = O_V + NKV * HP
NMAIN = O_K
MIXW = CC + NQ * HP
N_CHIPS = 4
VMEM_LIMIT = 56 * 1024 * 1024
MESH = pl.DeviceIdType.MESH

ADAM_LR, ADAM_B1, ADAM_B2, ADAM_EPS, ADAM_WD, ADAM_STEP = 0.001, 0.9, 0.999, 1e-08, 0.01, 10


def _cparams(sem=None, **kw):
    if sem is not None:
        kw["dimension_semantics"] = sem
    return pltpu.CompilerParams(vmem_limit_bytes=VMEM_LIMIT, **kw)


def _const_spec(shape):
    nd = len(shape)
    return pl.BlockSpec(shape, lambda *_: (0,) * nd, pipeline_mode=pl.Buffered(1))


def _nt(a, b):
    return lax.dot_general(a, b, (((1,), (1,)), ((), ())), preferred_element_type=F32)


def _tn(a, b):
    return lax.dot_general(a, b, (((0,), (0,)), ((), ())), preferred_element_type=F32)


def _rms_fwd(x, inv_n):
    r = lax.rsqrt(jnp.sum(x * x, axis=-1, keepdims=True) * inv_n + EPS)
    return r, x * r


def _rms_bwd(dy, g, xh, r, inv_n):
    dxh = dy * g
    return r * (dxh - xh * (jnp.sum(dxh * xh, axis=-1, keepdims=True) * inv_n))


W_IN_ROWS = 3 * CC + (NQ + 2 * NKV) * HD
W_IN_BLOCK = W_IN_ROWS // N_CHIPS


def _padded_row(row):
    return row + max(row - O_Q, 0) // HD * (HP - HD)


def _w_in_pieces(k):
    first = k * W_IN_BLOCK
    plain = min(max(O_Q - first, 0), W_IN_BLOCK)
    pieces = [(0, first, plain)] if plain else []
    return pieces + [(r, _padded_row(first + r), HD) for r in range(plain, W_IN_BLOCK, HD)]


def _inproj_fwd(x, g1, gi, own_i, chip, tm):
    t = x.shape[0]

    def body(chip_ref, x_ref, g_ref, gi_ref, own_ref, p_ref, h_ref, w_ref, sem):
        @pl.when(pl.program_id(0) == 0)
        def _():
            for k in range(N_CHIPS):
                for src, dst, rows in _w_in_pieces(k):
                    @pl.when(chip_ref[0] == k)
                    def _():
                        pltpu.make_async_copy(own_ref.at[pl.ds(src, rows)], w_ref.at[pl.ds(dst, rows)], sem).start()

                    @pl.when(chip_ref[0] != k)
                    def _():
                        pltpu.make_async_copy(gi_ref.at[k, pl.ds(src, rows)], w_ref.at[pl.ds(dst, rows)], sem).start()
            for slot in range(NQ + 2 * NKV):
                w_ref[O_Q + slot * HP + HD:O_Q + (slot + 1) * HP, :] = jnp.zeros((HP - HD, D), BF16)
            landed = w_ref.at[pl.ds(0, W_IN_ROWS)]
            pltpu.make_async_copy(landed, landed, sem).wait()

        _, xh = _rms_fwd(x_ref[...], 1.0 / D)
        h = (xh * g_ref[...]).astype(BF16)
        h_ref[...] = h
        p_ref[...] = _nt(h, w_ref[...])

    const = lambda shape: pl.BlockSpec(shape, lambda i, c: (0,) * len(shape))
    return pl.pallas_call(
        body, name="inproj_fwd",
        grid_spec=pltpu.PrefetchScalarGridSpec(
            num_scalar_prefetch=1, grid=(t // tm,),
            in_specs=[pl.BlockSpec((tm, D), lambda i, c: (i, 0)), const((1, D)), ANY, ANY],
            out_specs=[pl.BlockSpec((tm, NP), lambda i, c: (i, 0)), pl.BlockSpec((tm, D), lambda i, c: (i, 0)),
                       const((NP, D))],
            scratch_shapes=[pltpu.SemaphoreType.DMA(())]),
        out_shape=[jax.ShapeDtypeStruct((t, NP), F32), jax.ShapeDtypeStruct((t, D), BF16),
                   jax.ShapeDtypeStruct((NP, D), BF16)],
        compiler_params=_cparams(("arbitrary",)),
    )(chip, x, g1, gi, own_i)


def _band_mask():
    r_io = lax.broadcasted_iota(jnp.int32, (BLK, 2 * BLK), 0)
    c_io = lax.broadcasted_iota(jnp.int32, (BLK, 2 * BLK), 1)
    return (c_io > r_io) & (c_io <= r_io + BLK), c_io


def _conv_taps(uf, n):
    u1 = pltpu.roll(uf, 1, 0)[8:8 + n]
    u2 = pltpu.roll(uf, 2, 0)[8:8 + n]
    return u1, u2


def _attn_probs(qs, kband, sink, valid):
    s = jnp.where(valid, _nt(qs, kband), NEG)
    m = jnp.maximum(jnp.max(s, axis=-1, keepdims=True), sink)
    p = jnp.exp(s - m)
    es = jnp.exp(sink - m)
    inv = 1.0 / (jnp.sum(p, axis=-1, keepdims=True) + es)
    return p * inv, es * inv


def _norm_keys(kraw, gk):
    out = []
    for h in range(NKV):
        kh = kraw[:, h * HP:(h + 1) * HP]
        rk, khat = _rms_fwd(kh, 1.0 / HD)
        out.append((khat, rk, (khat * gk).astype(BF16)))
    return out


def _mixer_fwd(proj, x, cw, gq, gk, sinks, gco, gao, wo, tq):
    t = proj.shape[0]
    nb = tq // BLK
    r8 = tq // 8

    def body(p_ref, cgp_ref, hcp_ref, kvp_ref, x_ref, cw_ref, gq_ref, gk_ref, sk_ref, gco_ref, gao_ref,
             wo_ref, xm_ref, mix_ref, ao_ref):
        i = pl.program_id(0)
        cg = p_ref[:, O_CG:O_CG + CC]
        hc = p_ref[:, O_HC:O_HC + CC]
        u = cg * hc
        up = jnp.where(i > 0, cgp_ref[...] * hcp_ref[...], 0.0)
        u1, u2 = _conv_taps(jnp.concatenate([up, u], axis=0), tq)
        y = cw_ref[0:1, :] * u2 + cw_ref[1:2, :] * u1 + cw_ref[2:3, :] * u
        co = p_ref[:, O_BG:O_BG + CC] * y
        _, coh = _rms_fwd(co, 1.0 / CC)
        cn = coh * gco_ref[...]
        kraw = jnp.concatenate([kvp_ref[:, 0:NKV * HP], p_ref[:, O_K:O_K + NKV * HP]], axis=0)
        vraw = jnp.concatenate([kvp_ref[:, NKV * HP:], p_ref[:, O_V:O_V + NKV * HP]], axis=0)
        keys = _norm_keys(kraw, gk_ref[...])
        vb = [vraw[:, h * HP:(h + 1) * HP].astype(BF16) for h in range(NKV)]
        base_valid, c_io = _band_mask()
        gqs = gq_ref[...] * SCALE
        for b in range(nb):
            lo = jnp.where(i * nb + b == 0, BLK, 0)
            valid = base_valid & (c_io >= lo)
            for g in range(NQ):
                h = g // GRP
                qg = p_ref[b * BLK:(b + 1) * BLK, O_Q + g * HP:O_Q + (g + 1) * HP]
                _, qh = _rms_fwd(qg, 1.0 / HD)
                qs = (qh * gqs).astype(BF16)
                pr, _ = _attn_probs(qs, keys[h][2][b * BLK:b * BLK + 2 * BLK], sk_ref[0, g], valid)
                ao_ref[b * BLK:(b + 1) * BLK, g * HP:(g + 1) * HP] = jnp.dot(
                    pr.astype(BF16), vb[h][b * BLK:b * BLK + 2 * BLK], preferred_element_type=F32)
        _, aoh = _rms_fwd(ao_ref[...], 1.0 / (NQ * HD))
        an = aoh * gao_ref[...]
        mix = jnp.concatenate([cn, an], axis=1).astype(BF16)
        mix_ref[...] = mix
        xm_ref[...] = x_ref[...] + jnp.dot(mix, wo_ref[...], preferred_element_type=F32)

    prev8 = lambda col: pl.BlockSpec((8, CC), lambda i: (jnp.maximum(i * r8 - 1, 0), col))
    return pl.pallas_call(
        body, name="mixer_fwd", grid=(t // tq,),
        in_specs=[
            pl.BlockSpec((tq, NP), lambda i: (i, 0)),
            prev8(O_CG // CC), prev8(O_HC // CC),
            pl.BlockSpec((BLK, 2 * NKV * HP), lambda i: (jnp.maximum(i * nb - 1, 0), O_K // (2 * NKV * HP))),
            pl.BlockSpec((tq, D), lambda i: (i, 0)),
            _const_spec((8, CC)), _const_spec((1, HP)), _const_spec((1, HP)),
            pl.BlockSpec(memory_space=pltpu.SMEM),
            _const_spec((1, CC)), _const_spec((1, NQ * HP)), _const_spec((MIXW, D)),
        ],
        out_specs=[pl.BlockSpec((tq, D), lambda i: (i, 0)), pl.BlockSpec((tq, MIXW), lambda i: (i, 0)),
                   pl.BlockSpec((tq, NQ * HP), lambda i: (i, 0))],
        out_shape=[jax.ShapeDtypeStruct((t, D), F32), jax.ShapeDtypeStruct((t, MIXW), BF16),
                   jax.ShapeDtypeStruct((t, NQ * HP), F32)],
        compiler_params=_cparams(("parallel",)),
    )(proj, proj, proj, proj, x, cw, gq, gk, sinks, gco, gao, wo)


def _ffn_weight_specs():
    return [pl.BlockSpec((N_CHIPS, FFB, D), lambda i, j=j: (0, j, 0), pipeline_mode=pl.Buffered(1))
            for j in range(3)]


def _ffn_fwd(xm, g2, gf, tm, tgt=None):
    t = xm.shape[0]
    last = tgt is not None

    def body(x_ref, g_ref, wg_ref, wu_ref, wd_ref, *rest):
        t_ref, rest = (rest[0], rest[1:]) if last else (None, rest)
        l_ref, rest = (rest[0], rest[1:]) if last else (None, rest)
        xo_ref, a_ref, b_ref, h2_ref = rest
        xv = x_ref[...]
        _, xh = _rms_fwd(xv, 1.0 / D)
        h2 = (xh * g_ref[...]).astype(BF16)
        h2_ref[...] = h2
        acc = xv
        for k in range(N_CHIPS):
            a = _nt(h2, wg_ref[k])
            b = _nt(h2, wu_ref[k])
            a_ref[k] = a.astype(BF16)
            b_ref[k] = b.astype(BF16)
            hm = (a * jax.nn.sigmoid(a) * b).astype(BF16)
            acc = acc + jnp.dot(hm, wd_ref[k], preferred_element_type=F32)
        if last:
            @pl.when(pl.program_id(0) == 0)
            def _():
                l_ref[...] = jnp.zeros_like(l_ref)

            e = acc - t_ref[...]
            xo_ref[...] = e * (1.0 / D)
            l_ref[...] += jnp.sum(jnp.sum(e * e, axis=-1, keepdims=True), axis=0, keepdims=True) * (0.5 / D)
        else:
            xo_ref[...] = acc

    row = lambda w: pl.BlockSpec((tm, w), lambda i: (i, 0))
    blk = pl.BlockSpec((N_CHIPS, tm, FFB), lambda i: (0, i, 0))
    bsd = jax.ShapeDtypeStruct((N_CHIPS, t, FFB), BF16)
    return pl.pallas_call(
        body, name="ffn_fwd_loss" if last else "ffn_fwd", grid=(t // tm,),
        in_specs=[row(D), _const_spec((1, D))] + _ffn_weight_specs() + ([row(D)] if last else []),
        out_specs=([pl.BlockSpec((8, 128), lambda i: (0, 0))] if last else []) + [row(D), blk, blk, row(D)],
        out_shape=([jax.ShapeDtypeStruct((8, 128), F32)] if last else [])
        + [jax.ShapeDtypeStruct((t, D), F32), bsd, bsd, jax.ShapeDtypeStruct((t, D), BF16)],
        compiler_params=_cparams(("arbitrary" if last else "parallel",)),
    )(*((xm, g2, gf, gf, gf) + ((tgt,) if last else ())))


def _ffn_bwd(dy, xm, g2, a, b, gf, tm):
    t = dy.shape[0]

    def body(dy_ref, x_ref, g_ref, a_ref, b_ref, wg_ref, wu_ref, wd_ref, dx_ref, da_ref, db_ref, hm_ref, dg_ref):
        @pl.when(pl.program_id(0) == 0)
        def _():
            dg_ref[...] = jnp.zeros_like(dg_ref)

        dyv = dy_ref[...]
        dyb = dyv.astype(BF16)
        dh2 = jnp.zeros_like(dyv)
        for k in range(N_CHIPS):
            dhm = _nt(dyb, wd_ref[k])
            av = a_ref[k].astype(F32)
            bv = b_ref[k].astype(F32)
            sig = jax.nn.sigmoid(av)
            sil = av * sig
            hm_ref[k] = (sil * bv).astype(BF16)
            da = (dhm * bv * (sig * (1.0 + av * (1.0 - sig)))).astype(BF16)
            db = (dhm * sil).astype(BF16)
            da_ref[k] = da
            db_ref[k] = db
            dh2 = (dh2 + jnp.dot(da, wg_ref[k], preferred_element_type=F32)
                   + jnp.dot(db, wu_ref[k], preferred_element_type=F32))
        r, xh = _rms_fwd(x_ref[...], 1.0 / D)
        dg_ref[...] += jnp.sum(dh2 * xh, axis=0, keepdims=True)
        dx_ref[...] = dyv + _rms_bwd(dh2, g_ref[...], xh, r, 1.0 / D)

    row = lambda w: pl.BlockSpec((tm, w), lambda i: (i, 0))
    blk = pl.BlockSpec((N_CHIPS, tm, FFB), lambda i: (0, i, 0))
    bsd = jax.ShapeDtypeStruct((N_CHIPS, t, FFB), BF16)
    return pl.pallas_call(
        body, name="ffn_bwd", grid=(t // tm,),
        in_specs=[row(D), row(D), _const_spec((1, D)), blk, blk] + _ffn_weight_specs(),
        out_specs=[row(D), blk, blk, blk, pl.BlockSpec((1, D), lambda i: (0, 0))],
        out_shape=[jax.ShapeDtypeStruct((t, D), F32), bsd, bsd, bsd, jax.ShapeDtypeStruct((1, D), F32)],
        compiler_params=_cparams(("arbitrary",)),
    )(dy, xm, g2, a, b, gf, gf, gf)


def _wgrad_blocks(a, b, tt, name):
    _, t, rows = a.shape
    cols = b.shape[1]
    nsteps = t // tt

    def body(a_ref, b_ref, o_ref, acc_ref):
        s = pl.program_id(0)

        @pl.when(s == 0)
        def _():
            acc_ref[...] = jnp.zeros_like(acc_ref)

        bv = b_ref[...].astype(BF16)
        for k in range(N_CHIPS):
            acc_ref[k] += _tn(a_ref[k], bv)

        @pl.when(s == nsteps - 1)
        def _():
            o_ref[...] = acc_ref[...].astype(BF16)

    return pl.pallas_call(
        body, name=name, grid=(nsteps,),
        in_specs=[pl.BlockSpec((N_CHIPS, tt, rows), lambda s: (0, s, 0)), pl.BlockSpec((tt, cols), lambda s: (s, 0))],
        out_specs=pl.BlockSpec((N_CHIPS, rows, cols), lambda s: (0, 0, 0)),
        out_shape=jax.ShapeDtypeStruct((N_CHIPS, rows, cols), BF16),
        scratch_shapes=[pltpu.VMEM((N_CHIPS, rows, cols), F32)],
        compiler_params=_cparams(("arbitrary",)),
    )(a, b)


def _head_rows(first, n_heads):
    return [(first + g * HD, first + g * HP, HD) for g in range(n_heads)]


def _wgrad(a, b, tt, name, pieces=None):
    t, k = a.shape
    n = b.shape[1]
    nsteps = t // tt
    pieces = pieces or [(0, 0, k)]
    rows = sum(p[2] for p in pieces)

    def body(a_ref, b_ref, o_ref, acc_ref):
        s = pl.program_id(0)

        @pl.when(s == 0)
        def _():
            acc_ref[...] = jnp.zeros_like(acc_ref)

        acc_ref[...] += _tn(a_ref[...].astype(BF16), b_ref[...].astype(BF16))

        @pl.when(s == nsteps - 1)
        def _():
            for dst, src, size in pieces:
                o_ref[dst:dst + size, :] = acc_ref[src:src + size, :].astype(BF16)

    return pl.pallas_call(
        body, name=name, grid=(nsteps,),
        in_specs=[pl.BlockSpec((tt, k), lambda s: (s, 0)), pl.BlockSpec((tt, n), lambda s: (s, 0))],
        out_specs=pl.BlockSpec((rows, n), lambda s: (0, 0)),
        out_shape=jax.ShapeDtypeStruct((rows, n), BF16),
        scratch_shapes=[pltpu.VMEM((k, n), F32)],
        compiler_params=_cparams(("arbitrary",)),
    )(a, b)


def _mixer_bwd(dxm, proj, ao, cw, gq, gk, sinks, gco, gao, wo, tq):
    t = proj.shape[0]
    nb = tq // BLK
    r8 = tq // 8
    nt = t // tq
    te = tq + 8
    kvw = 2 * NKV * HP

    def body(dx_ref, dxn_ref, p_ref, cgp_ref, hcp_ref, bgn_ref, cgn_ref, hcn_ref, kvp_ref, ao_ref, cw_ref, gq_ref,
             gk_ref, sk_ref, gco_ref, gao_ref, wo_ref,
             dpm_ref, dkvm_ref, dkvh_ref, dcw_ref, dgq_ref, dgk_ref, dsk_ref, dgco_ref, dgao_ref, acc_ref):
        i = pl.program_id(0)

        @pl.when(i == 0)
        def _():
            for r in (dcw_ref, dgq_ref, dgk_ref, dsk_ref, dgco_ref, dgao_ref):
                r[...] = jnp.zeros_like(r)

        acc_ref[...] = jnp.zeros_like(acc_ref)
        live_rows = jnp.where(i < nt - 1, te, tq)
        dxb = dx_ref[...].astype(BF16)
        dxe = jnp.concatenate([dxb, dxn_ref[...].astype(BF16)], axis=0)
        dcn = _nt(dxe, wo_ref[0:CC, :])
        bg = jnp.concatenate([p_ref[:, O_BG:O_BG + CC], bgn_ref[...]], axis=0)
        cg = jnp.concatenate([p_ref[:, O_CG:O_CG + CC], cgn_ref[...]], axis=0)
        hc = jnp.concatenate([p_ref[:, O_HC:O_HC + CC], hcn_ref[...]], axis=0)
        u = cg * hc
        up = jnp.where(i > 0, cgp_ref[...] * hcp_ref[...], 0.0)
        u1, u2 = _conv_taps(jnp.concatenate([up, u], axis=0), te)
        w0, w1, w2 = cw_ref[0:1, :], cw_ref[1:2, :], cw_ref[2:3, :]
        y = w0 * u2 + w1 * u1 + w2 * u
        co = bg * y
        rc, coh = _rms_fwd(co, 1.0 / CC)
        dco = _rms_bwd(dcn, gco_ref[...], coh, rc, 1.0 / CC)
        row_io = lax.broadcasted_iota(jnp.int32, (te, 1), 0)
        own = row_io < tq
        dgco_ref[...] += jnp.sum(jnp.where(own, dcn * coh, 0.0), axis=0, keepdims=True)
        dyc = jnp.where(row_io < live_rows, dco * bg, 0.0)
        dyo = jnp.where(own, dyc, 0.0)
        dcw_ref[0:1, :] += jnp.sum(dyo * u2, axis=0, keepdims=True)
        dcw_ref[1:2, :] += jnp.sum(dyo * u1, axis=0, keepdims=True)
        dcw_ref[2:3, :] += jnp.sum(dyo * u, axis=0, keepdims=True)
        dy1 = pltpu.roll(dyc, te - 1, 0)[0:tq]
        dy2 = pltpu.roll(dyc, te - 2, 0)[0:tq]
        du = w2 * dyc[0:tq] + w1 * dy1 + w0 * dy2
        dpm_ref[:, O_BG:O_BG + CC] = (dco[0:tq] * y[0:tq]).astype(BF16)
        dpm_ref[:, O_CG:O_CG + CC] = (du * hc[0:tq]).astype(BF16)
        dpm_ref[:, O_HC:O_HC + CC] = (du * cg[0:tq]).astype(BF16)
        kraw = jnp.concatenate([kvp_ref[:, 0:NKV * HP], p_ref[:, O_K:O_K + NKV * HP]], axis=0)
        vraw = jnp.concatenate([kvp_ref[:, NKV * HP:], p_ref[:, O_V:O_V + NKV * HP]], axis=0)
        gqv, gkv = gq_ref[...], gk_ref[...]
        keys = _norm_keys(kraw, gkv)
        vb = [vraw[:, h * HP:(h + 1) * HP].astype(BF16) for h in range(NKV)]
        base_valid, c_io = _band_mask()
        lane = lax.broadcasted_iota(jnp.int32, (1, HP), 1)
        dgq, dgk, dsk = (jnp.zeros((1, HP), F32) for _ in range(3))
        dgao = jnp.zeros((1, NQ * HP), F32)
        for b in range(nb):
            lo = jnp.where(i * nb + b == 0, BLK, 0)
            valid = base_valid & (c_io >= lo)
            band = slice(b * BLK, b * BLK + 2 * BLK)
            blk = slice(b * BLK, (b + 1) * BLK)
            ra, aoh = _rms_fwd(ao_ref[blk, :], 1.0 / (NQ * HD))
            danb = _nt(dxb[blk], wo_ref[CC:MIXW, :])
            dgao = dgao + jnp.sum(danb * aoh, axis=0, keepdims=True)
            dao = _rms_bwd(danb, gao_ref[...], aoh, ra, 1.0 / (NQ * HD))
            fwd = []
            for g in range(NQ):
                rq, qh = _rms_fwd(p_ref[blk, O_Q + g * HP:O_Q + (g + 1) * HP], 1.0 / HD)
                qs = (qh * (gqv * SCALE)).astype(BF16)
                fwd.append((rq, qh, qs) + _attn_probs(qs, keys[g // GRP][2][band], sk_ref[0, g], valid))
            dqs = []
            for h in range(NKV):
                khat, rk, kn = [a[band] for a in keys[h]]
                dss, prbs, qns, dobs = [], [], [], []
                for g in range(h * GRP, (h + 1) * GRP):
                    rq, qh, qs, pr, ps = fwd[g]
                    dob = dao[:, g * HP:(g + 1) * HP].astype(BF16)
                    dp = _nt(dob, vb[h][band])
                    delta = jnp.sum(pr * dp, axis=-1, keepdims=True)
                    dsb = (pr * (dp - delta)).astype(BF16)
                    dsk = dsk + jnp.where(lane == g, -jnp.sum(ps * delta, axis=0, keepdims=True), 0.0)
                    dqn = jnp.dot(dsb, kn, preferred_element_type=F32) * SCALE
                    dgq = dgq + jnp.sum(dqn * qh, axis=0, keepdims=True)
                    dqs.append(_rms_bwd(dqn, gqv, qh, rq, 1.0 / HD).astype(BF16))
                    dss.append(dsb)
                    prbs.append(pr.astype(BF16))
                    qns.append(qs)
                    dobs.append(dob)
                dkn = _tn(jnp.concatenate(dss, axis=0), jnp.concatenate(qns, axis=0))
                dv = _tn(jnp.concatenate(prbs, axis=0), jnp.concatenate(dobs, axis=0))
                dgk = dgk + jnp.sum(dkn * khat, axis=0, keepdims=True)
                acc_ref[band, h * HP:(h + 1) * HP] += _rms_bwd(dkn, gkv, khat, rk, 1.0 / HD)
                acc_ref[band, (NKV + h) * HP:(NKV + h + 1) * HP] += dv
            dpm_ref[blk, O_Q:O_K] = jnp.concatenate(dqs, axis=1)
        dgq_ref[...] += dgq
        dgk_ref[...] += dgk
        dsk_ref[...] += dsk
        dgao_ref[...] += dgao
        dkvh_ref[...] = acc_ref[0:BLK, :]
        dkvm_ref[...] = acc_ref[BLK:, :]

    prev8 = lambda col: pl.BlockSpec((8, CC), lambda i: (jnp.maximum(i * r8 - 1, 0), col))
    next8 = lambda col: pl.BlockSpec((8, CC), lambda i: (jnp.minimum((i + 1) * r8, t // 8 - 1), col))
    small = lambda n: pl.BlockSpec((1, n), lambda i: (0, 0))
    return pl.pallas_call(
        body, name="mixer_bwd", grid=(nt,),
        in_specs=[
            pl.BlockSpec((tq, D), lambda i: (i, 0)),
            pl.BlockSpec((8, D), lambda i: (jnp.minimum((i + 1) * r8, t // 8 - 1), 0)),
            pl.BlockSpec((tq, NP), lambda i: (i, 0)),
            prev8(O_CG // CC), prev8(O_HC // CC),
            next8(O_BG // CC), next8(O_CG // CC), next8(O_HC // CC),
            pl.BlockSpec((BLK, kvw), lambda i: (jnp.maximum(i * nb - 1, 0), O_K // kvw)),
            pl.BlockSpec((tq, NQ * HP), lambda i: (i, 0)),
            _const_spec((8, CC)), _const_spec((1, HP)), _const_spec((1, HP)),
            pl.BlockSpec(memory_space=pltpu.SMEM),
            _const_spec((1, CC)), _const_spec((1, NQ * HP)), _const_spec((MIXW, D)),
        ],
        out_specs=[
            pl.BlockSpec((tq, NMAIN), lambda i: (i, 0)),
            pl.BlockSpec((tq, kvw), lambda i: (i, 0)),
            pl.BlockSpec((BLK, kvw), lambda i: (i, 0)),
            pl.BlockSpec((8, CC), lambda i: (0, 0)), small(HP), small(HP), small(HP), small(CC), small(NQ * HP),
        ],
        out_shape=[
            jax.ShapeDtypeStruct((t, NMAIN), BF16), jax.ShapeDtypeStruct((t, kvw), F32),
            jax.ShapeDtypeStruct((nt * BLK, kvw), F32),
            jax.ShapeDtypeStruct((8, CC), F32), jax.ShapeDtypeStruct((1, HP), F32), jax.ShapeDtypeStruct((1, HP), F32),
            jax.ShapeDtypeStruct((1, HP), F32), jax.ShapeDtypeStruct((1, CC), F32),
            jax.ShapeDtypeStruct((1, NQ * HP), F32),
        ],
        scratch_shapes=[pltpu.VMEM((tq + BLK, kvw), F32)],
        compiler_params=_cparams(("arbitrary",)),
    )(dxm, dxm, proj, proj, proj, proj, proj, proj, proj, ao, cw, gq, gk, sinks, gco, gao, wo)


def _inproj_bwd(dpm, dkvm, dkvh, wpt, x, g1, dxm, tm):
    t = x.shape[0]
    kvw = 2 * NKV * HP
    nt = t // tm

    def body(dp_ref, dk_ref, dh_ref, w_ref, x_ref, g_ref, dxm_ref, dx_ref, dg_ref, dkv_ref):
        i = pl.program_id(0)

        @pl.when(i == 0)
        def _():
            dg_ref[...] = jnp.zeros_like(dg_ref)

        halo = jnp.where(i < nt - 1, dh_ref[...], 0.0)
        dkv_ref[0:tm - BLK, :] = dk_ref[0:tm - BLK, :].astype(BF16)
        dkv_ref[tm - BLK:tm, :] = (dk_ref[tm - BLK:tm, :] + halo).astype(BF16)
        dh = (jnp.dot(dp_ref[...], w_ref[0:NMAIN, :], preferred_element_type=F32)
              + jnp.dot(dkv_ref[...], w_ref[NMAIN:NP, :], preferred_element_type=F32))
        r, xh = _rms_fwd(x_ref[...], 1.0 / D)
        dg_ref[...] += jnp.sum(dh * xh, axis=0, keepdims=True)
        dx_ref[...] = dxm_ref[...] + _rms_bwd(dh, g_ref[...], xh, r, 1.0 / D)

    row = lambda w: pl.BlockSpec((tm, w), lambda i: (i, 0))
    return pl.pallas_call(
        body, name="inproj_bwd", grid=(nt,),
        in_specs=[row(NMAIN), row(kvw), pl.BlockSpec((BLK, kvw), lambda i: (jnp.minimum(i + 1, nt - 1), 0)),
                  _const_spec((NP, D)), row(D), _const_spec((1, D)), row(D)],
        out_specs=[row(D), pl.BlockSpec((1, D), lambda i: (0, 0)), row(kvw)],
        out_shape=[jax.ShapeDtypeStruct((t, D), F32), jax.ShapeDtypeStruct((1, D), F32),
                   jax.ShapeDtypeStruct((t, kvw), BF16)],
        compiler_params=_cparams(("arbitrary",)),
    )(dpm, dkvm, dkvh, wpt, x, g1, dxm)


def _rows_tile(rows, cap=512):
    for cand in range(min(rows, cap) // 16 * 16, 0, -16):
        if rows % cand == 0:
            return cand
    return rows


def _presum_halves(gs, theirs, core):
    n = len(gs)

    def body(c_ref, *refs):
        for g_ref, t_ref, o_ref in zip(refs[:n], refs[n:2 * n], refs[2 * n:]):
            o_ref[...] = (g_ref[...].astype(F32) + t_ref[...].astype(F32)).astype(BF16)

    half = lambda ta: pl.BlockSpec((None,) + ta.shape[1:], lambda k, c_ref: (k, 0, 0))
    own = lambda ta: pl.BlockSpec((None,) + ta.shape[1:], lambda k, c_ref: (k, c_ref[0], 0))
    return pl.pallas_call(
        body, name="presum",
        grid_spec=pltpu.PrefetchScalarGridSpec(
            num_scalar_prefetch=1, grid=(N_CHIPS,),
            in_specs=[own(ta) for ta in theirs] + [half(ta) for ta in theirs],
            out_specs=[half(ta) for ta in theirs]),
        out_shape=[jax.ShapeDtypeStruct(ta.shape, BF16) for ta in theirs],
        compiler_params=_cparams(("parallel",)),
    )(core, *gs, *theirs)


def _sum_chips(cs):
    n = len(cs)
    steps = 2

    def body(*refs):
        for c_ref, o_ref in zip(refs[:n], refs[n:]):
            acc = c_ref[0].astype(F32)
            for j in range(1, N_CHIPS):
                acc = acc + c_ref[j].astype(F32)
            o_ref[...] = acc

    return pl.pallas_call(
        body, name="chipsum", grid=(steps,),
        in_specs=[pl.BlockSpec((N_CHIPS, c.shape[1] // steps, c.shape[2]), lambda i: (0, i, 0)) for c in cs],
        out_specs=[pl.BlockSpec((c.shape[1] // steps, c.shape[2]), lambda i: (i, 0)) for c in cs],
        out_shape=[jax.ShapeDtypeStruct(c.shape[1:], F32) for c in cs],
        compiler_params=_cparams(("parallel",)),
    )(*cs)


def _adamw(w, g, m, v, name):
    rows, cols = w.shape
    tr = _rows_tile(rows, 256)
    c1 = 1.0 - ADAM_B1 ** ADAM_STEP
    c2 = 1.0 - ADAM_B2 ** ADAM_STEP

    def body(w_ref, g_ref, m_ref, v_ref, d_ref, mo_ref, vo_ref):
        gv = g_ref[...]
        mn = ADAM_B1 * m_ref[...] + (1.0 - ADAM_B1) * gv
        vn = ADAM_B2 * v_ref[...] + (1.0 - ADAM_B2) * (gv * gv)
        mo_ref[...] = mn
        vo_ref[...] = vn
        d_ref[...] = -ADAM_LR * ((mn / c1) / (jnp.sqrt(vn / c2) + ADAM_EPS) + ADAM_WD * w_ref[...])

    spec = pl.BlockSpec((tr, cols), lambda i: (i, 0))
    sds = jax.ShapeDtypeStruct((rows, cols), F32)
    return pl.pallas_call(
        body, name=name, grid=(rows // tr,), in_specs=[spec] * 4, out_specs=[spec] * 3, out_shape=[sds] * 3,
        compiler_params=_cparams(("parallel",)),
    )(w, g, m, v)


def _place():
    x, y, c = lax.axis_index("x"), lax.axis_index("y"), lax.axis_index("c")
    chips = [(1 - x, y), (x, 1 - y), (1 - x, 1 - y)]
    return x, y, c, chips


ANY = pl.BlockSpec(memory_space=pl.ANY)
DMA_ROWS = 64


def _pieces(shape):
    rows = shape[-2]
    step = DMA_ROWS if rows % DMA_ROWS == 0 else rows
    lead = [()]
    for n in shape[:-2]:
        lead = [i + (k,) for i in lead for k in range(n)]
    return [i + (pl.ds(r0, step),) for i in lead for r0 in range(0, rows, step)]


def _start_pieces(make, src, dst):
    for idx in _pieces(src.shape):
        make(src.at[idx], dst.at[idx]).start()


def _gather_layer(blocks, layer):
    nw = len(blocks)

    def body(*refs):
        _gather_body(refs[:nw], refs[nw:2 * nw], refs[2 * nw:], layer, _start_pieces)

    return pl.pallas_call(
        body, name=f"gather_layer{layer}", in_specs=[ANY] * nw, out_specs=[ANY] * nw,
        out_shape=[jax.ShapeDtypeStruct((N_CHIPS,) + b.shape, b.dtype) for b in blocks],
        scratch_shapes=[pltpu.SemaphoreType.DMA((3, nw))] * 4,
        compiler_params=_cparams(has_side_effects=True),
    )(*blocks)


def _gather_body(srcs, outs, sems, layer, start):
    nw = len(srcs)
    ssem, rsem, fssem, frsem = sems
    x, y, c, chips = _place()
    kme = 2 * x + y

    def plane(j, w, to):
        return lambda s, d: pltpu.make_async_remote_copy(
            src_ref=s, dst_ref=d, send_sem=ssem.at[j, w], recv_sem=rsem.at[j, w], device_id=to,
            device_id_type=MESH)

    def passed(j, w):
        return lambda s, d: pltpu.make_async_remote_copy(
            src_ref=s, dst_ref=d, send_sem=fssem.at[j, w], recv_sem=frsem.at[j, w],
            device_id=(x, y, 1 - c), device_id_type=MESH)

    @pl.when(c == layer)
    def _():
        for j, (px, py) in enumerate(chips):
            for w in range(nw):
                start(plane(j, w, (px, py, c)), srcs[w], outs[w].at[kme])
        for j, (px, py) in enumerate(chips):
            for w in range(nw):
                got = outs[w].at[2 * px + py]
                plane(j, w, (px, py, c))(got, got).wait_recv()
                start(passed(j, w), got, got)
        for j, (px, py) in enumerate(chips):
            for w in range(nw):
                got = outs[w].at[2 * px + py]
                plane(j, w, (px, py, c))(got, got).wait_send()
                passed(j, w)(got, got).wait_send()

    @pl.when(c != layer)
    def _():
        for j, (px, py) in enumerate(chips):
            for w in range(nw):
                got = outs[w].at[2 * px + py]
                passed(j, w)(got, got).wait_recv()


def _handshake(peers):
    barrier = pltpu.get_barrier_semaphore()
    for peer in peers:
        pl.semaphore_signal(barrier, inc=1, device_id=peer, device_id_type=MESH)
    pl.semaphore_wait(barrier, len(peers))


def _handshake_all():
    x, y, c, _ = _place()
    _handshake([(x ^ (r >> 2), y ^ ((r >> 1) & 1), c ^ (r & 1)) for r in range(1, 8)])


def _gather_layer_async(blocks, layer, name, collective_id):
    hbm = pltpu.MemorySpace.HBM
    srcs = [jax.new_ref(b, memory_space=hbm) for b in blocks]
    outs = [jax.empty_ref(jax.ShapeDtypeStruct((N_CHIPS,) + b.shape, b.dtype), memory_space=hbm) for b in blocks]

    @pl.kernel(mesh=plsc.ScalarSubcoreMesh(axis_name="seq", num_cores=1), name=name,
               scratch_types=[pltpu.SemaphoreType.DMA((3, len(blocks)))] * 4,
               compiler_params=pltpu.CompilerParams(collective_id=collective_id))
    def launch(*sems):
        _handshake_all()
        _gather_body(srcs, outs, sems, layer, lambda make, s, d: make(s, d).start())

    launch()
    return [o[...] for o in outs]


def _swap_siblings(arrs, halves, name, collective_id=None):
    nw = len(arrs)
    out_sds = [jax.ShapeDtypeStruct((a.shape[0], a.shape[1] // 2, a.shape[2]) if halves else a.shape, a.dtype)
               for a in arrs]

    def exchange(srcs, outs, ssem, rsem, start):
        x, y, c, _ = _place()

        def give(w):
            return lambda s, d: pltpu.make_async_remote_copy(
                src_ref=s, dst_ref=d, send_sem=ssem.at[w], recv_sem=rsem.at[w], device_id=(x, y, 1 - c),
                device_id_type=MESH)

        for w in range(nw):
            hr = outs[w].shape[1]
            start(give(w), srcs[w].at[:, pl.ds((1 - c) * hr, hr)] if halves else srcs[w], outs[w])
        for w in range(nw):
            give(w)(outs[w], outs[w]).wait()

    if collective_id is None:
        def body(*refs):
            exchange(refs[:nw], refs[nw:2 * nw], *refs[2 * nw:], _start_pieces)

        return pl.pallas_call(
            body, name=name, in_specs=[ANY] * nw, out_specs=[ANY] * nw, out_shape=out_sds,
            scratch_shapes=[pltpu.SemaphoreType.DMA((nw,))] * 2,
            compiler_params=_cparams(has_side_effects=True),
        )(*arrs)

    hbm = pltpu.MemorySpace.HBM
    srcs = [jax.new_ref(a, memory_space=hbm) for a in arrs]
    outs = [jax.empty_ref(sds, memory_space=hbm) for sds in out_sds]

    @pl.kernel(mesh=plsc.ScalarSubcoreMesh(axis_name="seq", num_cores=1), name=name,
               scratch_types=[pltpu.SemaphoreType.DMA((nw,))] * 2,
               compiler_params=pltpu.CompilerParams(collective_id=collective_id))
    def launch(ssem, rsem):
        x, y, c, _ = _place()
        _handshake([(x, y, 1 - c)])
        exchange(srcs, outs, ssem, rsem, lambda make, s, d: make(s, d).start())

    launch()
    return [o[...] for o in outs]


def _scatter_chips(ps):
    nw = len(ps)

    def body(*refs):
        _scatter_body(refs[:nw], refs[nw:2 * nw], refs[2 * nw:], _start_pieces)

    return pl.pallas_call(
        body, name="scatter_chips", in_specs=[ANY] * nw, out_specs=[ANY] * nw,
        out_shape=[jax.ShapeDtypeStruct(p.shape, p.dtype) for p in ps],
        scratch_shapes=[pltpu.SemaphoreType.DMA((3, nw)), pltpu.SemaphoreType.DMA((3, nw))],
        compiler_params=_cparams(has_side_effects=True),
    )(*ps)


def _scatter_body(srcs, outs, sems, start):
    nw = len(srcs)
    ssem, rsem = sems
    x, y, c, chips = _place()
    kme = 2 * x + y

    def give(j, w, to):
        return lambda s, d: pltpu.make_async_remote_copy(
            src_ref=s, dst_ref=d, send_sem=ssem.at[j, w], recv_sem=rsem.at[j, w], device_id=to,
            device_id_type=MESH)

    for j, (px, py) in enumerate(chips):
        for w in range(nw):
            start(give(j, w, (px, py, c)), srcs[w].at[2 * px + py], outs[w].at[kme])
    for j, (px, py) in enumerate(chips):
        for w in range(nw):
            got = outs[w].at[2 * px + py]
            give(j, w, (px, py, c))(got, got).wait_recv()
    for j, (px, py) in enumerate(chips):
        for w in range(nw):
            sent = srcs[w].at[2 * px + py]
            give(j, w, (px, py, c))(sent, sent).wait_send()


def _scatter_chips_async(ps, name, collective_id):
    hbm = pltpu.MemorySpace.HBM
    srcs = [jax.new_ref(p, memory_space=hbm) for p in ps]
    outs = [jax.empty_ref(jax.ShapeDtypeStruct(p.shape, p.dtype), memory_space=hbm) for p in ps]

    @pl.kernel(mesh=plsc.ScalarSubcoreMesh(axis_name="seq", num_cores=1), name=name,
               scratch_types=[pltpu.SemaphoreType.DMA((3, len(ps)))] * 2,
               compiler_params=pltpu.CompilerParams(collective_id=collective_id))
    def launch(*sems):
        _handshake_all()
        _scatter_body(srcs, outs, sems, lambda make, s, d: make(s, d).start())

    launch()
    return [o[...] for o in outs]


def _allreduce_small(v):
    rows = v.shape[0]

    def body(v_ref, o_ref, buf, ssem, rsem):
        x, y, c, _ = _place()
        me = 4 * x + 2 * y + c
        buf[me] = v_ref[...]
        sends = []
        for r in range(1, 8):
            peer = (x ^ (r >> 2), y ^ ((r >> 1) & 1), c ^ (r & 1))
            cp = pltpu.make_async_remote_copy(
                src_ref=v_ref, dst_ref=buf.at[me], send_sem=ssem.at[r - 1], recv_sem=rsem.at[r - 1],
                device_id=peer, device_id_type=MESH)
            cp.start()
            sends.append(cp)
        for r in range(1, 8):
            src = me ^ r
            pltpu.make_async_remote_copy(
                src_ref=v_ref, dst_ref=buf.at[src], send_sem=ssem.at[r - 1], recv_sem=rsem.at[r - 1],
                device_id=(x, y, c), device_id_type=MESH).wait_recv()
        for cp in sends:
            cp.wait_send()
        acc = buf[0]
        for d in range(1, 8):
            acc = acc + buf[d]
        o_ref[...] = acc

    vm = pl.BlockSpec(memory_space=pltpu.VMEM)
    return pl.pallas_call(
        body, name="allreduce_small", in_specs=[vm], out_specs=vm,
        out_shape=jax.ShapeDtypeStruct(v.shape, F32),
        scratch_shapes=[pltpu.VMEM((8, rows, 128), F32), pltpu.SemaphoreType.DMA((7,)),
                        pltpu.SemaphoreType.DMA((7,))],
        compiler_params=_cparams(has_side_effects=True),
    )(v)


def _pad_heads(w, n_heads, axis):
    shp = w.shape
    w = w.reshape(shp[:axis] + (n_heads, HD) + shp[axis + 1:])
    pad = [(0, 0)] * w.ndim
    pad[axis + 1] = (0, HP - HD)
    w = jnp.pad(w, pad)
    return w.reshape(shp[:axis] + (n_heads * HP,) + shp[axis + 1:])


def _strip_heads(w, n_heads, axis):
    shp = w.shape
    w = w.reshape(shp[:axis] + (n_heads, HP) + shp[axis + 1:])
    w = lax.slice_in_dim(w, 0, HD, axis=axis + 1)
    return w.reshape(shp[:axis] + (n_heads * HD,) + shp[axis + 1:])


def _t(w):
    return jnp.swapaxes(w, -1, -2)


def _count(shape):
    n = 1
    for s in shape:
        n *= s
    return n


def _pack_rows(arrs):
    flat = [jnp.pad(a.reshape(-1), (0, (-_count(a.shape)) % 128)) for a in arrs]
    v = jnp.concatenate(flat)
    rows = -(-v.shape[0] // (8 * 128)) * 8
    return jnp.pad(v, (0, rows * 128 - v.shape[0])).reshape(rows, 128)


def kernel(x, norm1_g, w_in, conv_w, q_norm_g, k_norm_g, sinks, conv_out_g, attn_out_g, w_o, norm2_g, w_gate, w_up, w_down, loss_target, m_norm1_g, m_w_in, m_conv_w, m_q_norm_g, m_k_norm_g, m_sinks, m_conv_out_g, m_attn_out_g, m_w_o, m_norm2_g, m_w_gate, m_w_up, m_w_down, v_norm1_g, v_w_in, v_conv_w, v_q_norm_g, v_k_norm_g, v_sinks, v_conv_out_g, v_attn_out_g, v_w_o, v_norm2_g, v_w_gate, v_w_up, v_w_down):
    depth = w_in.shape[0]
    t = x.shape[1]
    xs = x.reshape(t, D)
    tgt = loss_target.reshape(t, D)
    xi, yi = lax.axis_index("x"), lax.axis_index("y")
    kme = 2 * xi + yi
    tm = min(512, t)
    tq = min(512, t)
    tf = min(256, t)
    tw = min(1024, t)

    cwp = jnp.pad(conv_w.reshape(depth * 3, CC // N_CHIPS), ((0, 8 - depth * 3), (0, 0)))
    own_f = [jnp.concatenate([_t(w_gate[l]), _t(w_up[l]), w_down[l]], axis=0).astype(BF16) for l in range(depth)]
    own_o = [w_o[l].astype(BF16) for l in range(depth)]
    own_i = [_t(w_in[l]).astype(BF16) for l in range(depth)]
    mine = lambda got, own: lax.dynamic_update_index_in_dim(got, own, kme, 0)
    got_i0, got_o0, got_cw = _gather_layer([own_i[0], own_o[0], cwp], 0)
    gf0_in = lax.optimization_barrier((own_f[0], got_i0))[0]
    (got_f0,) = _gather_layer_async([gf0_in], 0, "gather_ffn0_seq", collective_id=6)
    cw_full = mine(got_cw, cwp).transpose(1, 0, 2).reshape(8, CC)[:depth * 3].reshape(depth, 3, CC)

    chip = kme.reshape(1).astype(jnp.int32)

    def layer_params(l, got_o):
        wo = mine(got_o, own_o[l]).reshape(D, D)
        return dict(
            wo=jnp.concatenate([wo[:CC], _pad_heads(wo[CC:], NQ, 0)], axis=0),
            cw=jnp.pad(cw_full[l], ((0, 5), (0, 0))),
            g1=norm1_g[l].reshape(1, D), g2=norm2_g[l].reshape(1, D),
            gq=jnp.pad(q_norm_g[l], (0, HP - HD)).reshape(1, HP), gk=jnp.pad(k_norm_g[l], (0, HP - HD)).reshape(1, HP),
            sk=sinks[l].reshape(1, NQ), gco=conv_out_g[l].reshape(1, CC),
            gao=_pad_heads(attn_out_g[l], NQ, 0).reshape(1, NQ * HP))

    saved, layers = [], []
    cur = xs
    for l in range(depth):
        x_in = cur
        if l == 0:
            got_i, p = got_i0, layer_params(0, got_o0)
        else:
            got_f1, got_o1, got_i = lax.optimization_barrier((got_l1, cur))[0]
            p = layer_params(1, got_o1)
        proj, h, p["wpt"] = _inproj_fwd(cur, p["g1"], got_i, own_i[l], chip, tm)
        xm, mix, ao = _mixer_fwd(proj, cur, p["cw"], p["gq"], p["gk"], p["sk"], p["gco"], p["gao"], p["wo"], tq)
        if l == 0:
            got_f0 = lax.optimization_barrier((got_f0, xm))[0]
            l1_in = lax.optimization_barrier(([own_f[1], own_o[1], own_i[1]], got_f0))[0]
            got_l1 = _gather_layer_async(l1_in, 1, "gather_layer1_seq", collective_id=1)
        p["gf"] = mine(got_f0 if l == 0 else got_f1, own_f[l])
        layers.append(p)
        if l < depth - 1:
            cur, a, b, h2 = _ffn_fwd(xm, p["g2"], p["gf"], tm)
        else:
            lpart, dy, a, b, h2 = _ffn_fwd(xm, p["g2"], p["gf"], tm, tgt)
        saved.append(dict(x=x_in, proj=proj, h=h, xm=xm, mix=mix, ao=ao, a=a, b=b, h2=h2))
    loss = lax.psum(lpart[0, 0], ("x", "y", "c"))

    nt = t // tq
    ci = lax.axis_index("c")
    core = ci.reshape(1).astype(jnp.int32)
    rbig = [dict() for _ in range(depth)]
    gsmall = [None] * depth

    def after_(vals, after):
        return vals if after is None else lax.optimization_barrier((vals, after))[0]

    def reduce_1(gs, tag, ids):
        return gs, _swap_siblings(gs, True, f"swap_halves_{tag}_seq", ids[0]), tag, ids

    def reduce_2(state, after):
        gs, theirs, tag, ids = state
        ps = _presum_halves(gs, after_(theirs, after), core)
        return ps, _scatter_chips_async(ps, f"scatter_{tag}_seq", ids[1]), tag, ids

    def reduce_3(state, after):
        ps, got, tag, ids = state
        cs = [lax.dynamic_update_index_in_dim(g, lax.dynamic_index_in_dim(q, kme, 0, keepdims=False), kme, 0)
              for g, q in zip(after_(got, after), ps)]
        r_mine = _sum_chips(cs)
        return r_mine, _swap_siblings(r_mine, False, f"swap_reduced_{tag}" + ("_seq" if ids[2] else ""), ids[2])

    def reduce_4(state, after):
        r_mine, r_theirs = state
        return [jnp.where(ci == 0, jnp.concatenate([a, b], axis=0), jnp.concatenate([b, a], axis=0))
                for a, b in zip(r_mine, after_(r_theirs, after))]

    ids = {"ffn1": (7, 4, 8), "in1": (9, 5, 10), "ffn0": (11, 2, 12), "in0": (13, 3, None)}
    in_2 = None
    handed = {}
    for l in reversed(range(depth)):
        p, s = layers[l], saved[l]
        dxm, da, db, hm, dg2 = _ffn_bwd(dy, s["xm"], p["g2"], s["a"], s["b"], p["gf"], tf)
        if in_2 is not None:
            in_2 = reduce_2(in_2, dxm)
        g_wg = _wgrad_blocks(da, s["h2"], tw, "wgrad_gate")
        g_wu = _wgrad_blocks(db, s["h2"], tw, "wgrad_up")
        g_wd = _wgrad_blocks(hm, dy, tw, "wgrad_down")
        if in_2 is not None:
            handed[f"in{l + 1}"] = reduce_3(in_2, g_wd)
        ffn_1 = reduce_1([g_wg, g_wu, g_wd], f"ffn{l}", ids[f"ffn{l}"])
        dpm, dkvm, dkvh, dcw, dgq, dgk, dsk, dgco, dgao = _mixer_bwd(
            dxm, s["proj"], s["ao"], p["cw"], p["gq"], p["gk"], p["sk"], p["gco"], p["gao"], p["wo"], tq)
        ffn_2 = reduce_2(ffn_1, dpm)
        g_o = _wgrad(s["mix"], dxm, tw, "wgrad_o", [(0, 0, CC)] + _head_rows(CC, NQ))
        dx, dg1, dkv = _inproj_bwd(dpm, dkvm, dkvh, p["wpt"], s["x"], p["g1"], dxm, tq)
        g_in = jnp.concatenate(
            [_wgrad(dpm, s["h"], tw, "wgrad_in_main", [(0, 0, O_Q)] + _head_rows(O_Q, NQ)),
             _wgrad(dkv, s["h"], tw, "wgrad_in_kv", _head_rows(0, 2 * NKV))], axis=0)
        dy = dx
        gsmall[l] = dict(g1=dg1, cw=dcw[:3], gq=dgq[0, :HD], gk=dgk[0, :HD], sk=dsk[0, :NQ], gco=dgco,
                         gao=_strip_heads(dgao.reshape(NQ * HP), NQ, 0), g2=dg2)
        handed[f"ffn{l}"] = reduce_3(ffn_2, g_in)
        in_2 = reduce_1([g_in.reshape(N_CHIPS, -1, D), g_o.reshape(N_CHIPS, -1, D)], f"in{l}", ids[f"in{l}"])
    grad_x = dy.reshape(x.shape)

    small_shapes = dict(g1=(D,), cw=(3, CC), gq=(HD,), gk=(HD,), sk=(NQ,), gco=(CC,), gao=(NQ * HD,), g2=(D,))
    red = _allreduce_small(_pack_rows([gsmall[l][n] for l in range(depth) for n in small_shapes])).reshape(-1)
    red_small, offs = {n: [] for n in small_shapes}, 0
    for l in range(depth):
        for n, shp in small_shapes.items():
            cnt = _count(shp)
            red_small[n].append(red[offs:offs + cnt].reshape(shp))
            offs += -(-cnt // 128) * 128
    g_small = {n: jnp.stack(v) for n, v in red_small.items()}
    g_cw = lax.dynamic_slice_in_dim(g_small["cw"], kme * (CC // N_CHIPS), CC // N_CHIPS, axis=2)

    weights = [norm1_g, w_in, conv_w, q_norm_g, k_norm_g, sinks, conv_out_g, attn_out_g, w_o, norm2_g, w_gate,
               w_up, w_down]
    moms = [m_norm1_g, m_w_in, m_conv_w, m_q_norm_g, m_k_norm_g, m_sinks, m_conv_out_g, m_attn_out_g, m_w_o,
            m_norm2_g, m_w_gate, m_w_up, m_w_down]
    vars_ = [v_norm1_g, v_w_in, v_conv_w, v_q_norm_g, v_k_norm_g, v_sinks, v_conv_out_g, v_attn_out_g, v_w_o,
             v_norm2_g, v_w_gate, v_w_up, v_w_down]
    n_w = len(weights)
    big_idx = dict(zip(("in", "o", "g", "u", "d"), (1, 8, 10, 11, 12)))
    small_idx = [n for n in range(n_w) if n not in big_idx.values()]
    grads, deltas, new_m, new_v = [None] * n_w, [None] * n_w, [None] * n_w, [None] * n_w
    for n, g in zip(small_idx, (g_small["g1"], g_cw, g_small["gq"], g_small["gk"], g_small["sk"], g_small["gco"],
                                g_small["gao"], g_small["g2"])):
        grads[n] = g

    def update_big(name):
        n = big_idx[name]
        g = jnp.stack([rbig[l][name] for l in range(depth)])
        flip = g.shape != weights[n].shape
        rows2d = lambda a3: (_t(a3) if flip else a3).reshape(-1, D)
        res = _adamw(rows2d(weights[n]), g.reshape(-1, D), rows2d(moms[n]), rows2d(vars_[n]), f"adamw_{n}")
        res = [g] + [r.reshape(g.shape) for r in res]
        grads[n], deltas[n], new_m[n], new_v[n] = [_t(r) for r in res] if flip else res

    for l in range(depth):
        rbig[l]["g"], rbig[l]["u"], rbig[l]["d"] = reduce_4(handed[f"ffn{l}"], red)
    rbig[1]["in"], rbig[1]["o"] = reduce_4(handed["in1"], red)
    update_big("g")
    in_2 = reduce_2(in_2, new_v[big_idx["g"]])
    update_big("u")
    update_big("d")
    rbig[0]["in"], rbig[0]["o"] = reduce_4(reduce_3(in_2, new_v[big_idx["d"]]), None)
    for name in ("in", "o"):
        update_big(name)
    res = _adamw(*[_pack_rows([arrs[n] for n in small_idx]) for arrs in (weights, grads, moms, vars_)],
                 "adamw_small")
    offs = 0
    for n in small_idx:
        shp = weights[n].shape
        cnt = _count(shp)
        deltas[n], new_m[n], new_v[n] = [r.reshape(-1)[offs:offs + cnt].reshape(shp) for r in res]
        offs += -(-cnt // 128) * 128
    return (loss, grad_x, *grads, *deltas, *new_m, *new_v)
```

```python
import functools

import jax
import jax.numpy as jnp
from jax import lax
from jax.experimental import pallas as pl
from jax.experimental.pallas import tpu as pltpu
from jax.experimental.pallas import tpu_sc as plsc

F32 = jnp.float32
BF16 = jnp.bfloat16

D = 1024
CC = 512
NQ = 8
NKV = 2
HD = 64
HP = 128
GRP = NQ // NKV
FF = 2816
FFB = FF // 4
BLK = 128
EPS = 1e-6
NEG = -1e30
SCALE = HD ** -0.5
O_BG, O_CG, O_HC, O_Q = 0, CC, 2 * CC, 3 * CC
O_K = O_Q + NQ * HP
O_V = O_K + NKV * HP
NP = O_V + NKV * HP
NMAIN = O_K
MIXW = CC + NQ * HD
N_CHIPS = 4
VMEM_LIMIT = 56 * 1024 * 1024
MESH = pl.DeviceIdType.MESH

ADAM_LR, ADAM_B1, ADAM_B2, ADAM_EPS, ADAM_WD, ADAM_STEP = 0.001, 0.9, 0.999, 1e-08, 0.01, 10


def _cparams(sem=None, **kw):
    if sem is not None:
        kw["dimension_semantics"] = sem
    return pltpu.CompilerParams(vmem_limit_bytes=VMEM_LIMIT, **kw)


def _const_spec(shape):
    nd = len(shape)
    return pl.BlockSpec(shape, lambda *_: (0,) * nd, pipeline_mode=pl.Buffered(1))


def _nt(a, b):
    return lax.dot_general(a, b, (((1,), (1,)), ((), ())), preferred_element_type=F32)


def _tn(a, b):
    return lax.dot_general(a, b, (((0,), (0,)), ((), ())), preferred_element_type=F32)


def _rms_fwd(x, inv_n):
    r = lax.rsqrt(jnp.sum(x * x, axis=-1, keepdims=True) * inv_n + EPS)
    return r, x * r


def _rms_bwd(dy, g, xh, r, inv_n):
    dxh = dy * g
    return r * (dxh - xh * (jnp.sum(dxh * xh, axis=-1, keepdims=True) * inv_n))


W_IN_ROWS = 3 * CC + (NQ + 2 * NKV) * HD
W_IN_BLOCK = W_IN_ROWS // N_CHIPS


def _padded_row(row):
    return row + max(row - O_Q, 0) // HD * (HP - HD)


def _w_in_pieces(k):
    first = k * W_IN_BLOCK
    plain = min(max(O_Q - first, 0), W_IN_BLOCK)
    pieces = [(0, first, plain)] if plain else []
    return pieces + [(r, _padded_row(first + r), HD) for r in range(plain, W_IN_BLOCK, HD)]


def _inproj_fwd(x, g1, gi, own_i, chip, tm):
    t = x.shape[0]

    def body(chip_ref, x_ref, g_ref, gi_ref, own_ref, p_ref, h_ref, w_ref, sem):
        @pl.when(pl.program_id(0) == 0)
        def _():
            for k in range(N_CHIPS):
                for src, dst, rows in _w_in_pieces(k):
                    @pl.when(chip_ref[0] == k)
                    def _():
                        pltpu.make_async_copy(own_ref.at[pl.ds(src, rows)], w_ref.at[pl.ds(dst, rows)], sem).start()

                    @pl.when(chip_ref[0] != k)
                    def _():
                        pltpu.make_async_copy(gi_ref.at[k, pl.ds(src, rows)], w_ref.at[pl.ds(dst, rows)], sem).start()
            for slot in range(NQ + 2 * NKV):
                w_ref[O_Q + slot * HP + HD:O_Q + (slot + 1) * HP, :] = jnp.zeros((HP - HD, D), BF16)
            landed = w_ref.at[pl.ds(0, W_IN_ROWS)]
            pltpu.make_async_copy(landed, landed, sem).wait()

        _, xh = _rms_fwd(x_ref[...], 1.0 / D)
        h = (xh * g_ref[...]).astype(BF16)
        h_ref[...] = h
        p_ref[...] = _nt(h, w_ref[...])

    const = lambda shape: pl.BlockSpec(shape, lambda i, c: (0,) * len(shape))
    return pl.pallas_call(
        body, name="inproj_fwd",
        grid_spec=pltpu.PrefetchScalarGridSpec(
            num_scalar_prefetch=1, grid=(t // tm,),
            in_specs=[pl.BlockSpec((tm, D), lambda i, c: (i, 0)), const((1, D)), ANY, ANY],
            out_specs=[pl.BlockSpec((tm, NP), lambda i, c: (i, 0)), pl.BlockSpec((tm, D), lambda i, c: (i, 0)),
                       const((NP, D))],
            scratch_shapes=[pltpu.SemaphoreType.DMA(())]),
        out_shape=[jax.ShapeDtypeStruct((t, NP), F32), jax.ShapeDtypeStruct((t, D), BF16),
                   jax.ShapeDtypeStruct((NP, D), BF16)],
        compiler_params=_cparams(("arbitrary",)),
    )(chip, x, g1, gi, own_i)


def _band_mask():
    r_io = lax.broadcasted_iota(jnp.int32, (BLK, 2 * BLK), 0)
    c_io = lax.broadcasted_iota(jnp.int32, (BLK, 2 * BLK), 1)
    return (c_io > r_io) & (c_io <= r_io + BLK), c_io


def _conv_taps(uf, n):
    u1 = pltpu.roll(uf, 1, 0)[8:8 + n]
    u2 = pltpu.roll(uf, 2, 0)[8:8 + n]
    return u1, u2


def _attn_probs(qs, kband, sink, valid):
    s = jnp.where(valid, _nt(qs, kband), NEG)
    m = jnp.maximum(jnp.max(s, axis=-1, keepdims=True), sink)
    p = jnp.exp(s - m)
    es = jnp.exp(sink - m)
    inv = 1.0 / (jnp.sum(p, axis=-1, keepdims=True) + es)
    return p * inv, es * inv


def _norm_keys(kraw, gk):
    out = []
    for h in range(NKV):
        kh = kraw[:, h * HP:(h + 1) * HP]
        rk, khat = _rms_fwd(kh, 1.0 / HD)
        out.append((khat, rk, (khat * gk).astype(BF16)))
    return out


def _mixer_fwd(proj, x, cw, gq, gk, sinks, gco, gao, wo, tq):
    t = proj.shape[0]
    nb = tq // BLK
    r8 = tq // 8

    def body(p_ref, cgp_ref, hcp_ref, kvp_ref, x_ref, cw_ref, gq_ref, gk_ref, sk_ref, gco_ref, gao_ref,
             wo_ref, xm_ref, mix_ref, ao_ref):
        i = pl.program_id(0)
        cg = p_ref[:, O_CG:O_CG + CC]
        hc = p_ref[:, O_HC:O_HC + CC]
        u = cg * hc
        up = jnp.where(i > 0, cgp_ref[...] * hcp_ref[...], 0.0)
        u1, u2 = _conv_taps(jnp.concatenate([up, u], axis=0), tq)
        y = cw_ref[0:1, :] * u2 + cw_ref[1:2, :] * u1 + cw_ref[2:3, :] * u
        co = p_ref[:, O_BG:O_BG + CC] * y
        _, coh = _rms_fwd(co, 1.0 / CC)
        cn = coh * gco_ref[...]
        kraw = jnp.concatenate([kvp_ref[:, 0:NKV * HP], p_ref[:, O_K:O_K + NKV * HP]], axis=0)
        vraw = jnp.concatenate([kvp_ref[:, NKV * HP:], p_ref[:, O_V:O_V + NKV * HP]], axis=0)
        keys = _norm_keys(kraw, gk_ref[...])
        vb = [vraw[:, h * HP:(h + 1) * HP].astype(BF16) for h in range(NKV)]
        base_valid, c_io = _band_mask()
        gqs = gq_ref[...] * SCALE
        for b in range(nb):
            lo = jnp.where(i * nb + b == 0, BLK, 0)
            valid = base_valid & (c_io >= lo)
            outs = []
            for g in range(NQ):
                h = g // GRP
                qg = p_ref[b * BLK:(b + 1) * BLK, O_Q + g * HP:O_Q + (g + 1) * HP]
                _, qh = _rms_fwd(qg, 1.0 / HD)
                qs = (qh * gqs).astype(BF16)
                pr, _ = _attn_probs(qs, keys[h][2][b * BLK:b * BLK + 2 * BLK], sk_ref[0, g], valid)
                outs.append(jnp.dot(pr.astype(BF16), vb[h][b * BLK:b * BLK + 2 * BLK], preferred_element_type=F32))
            for j in range(NQ // 2):
                ao_ref[b * BLK:(b + 1) * BLK, j * HP:(j + 1) * HP] = outs[2 * j] + pltpu.roll(outs[2 * j + 1], HD, 1)
        _, aoh = _rms_fwd(ao_ref[...], 1.0 / (NQ * HD))
        an = aoh * gao_ref[...]
        mix = jnp.concatenate([cn, an], axis=1).astype(BF16)
        mix_ref[...] = mix
        xm_ref[...] = x_ref[...] + jnp.dot(mix, wo_ref[...], preferred_element_type=F32)

    prev8 = lambda col: pl.BlockSpec((8, CC), lambda i: (jnp.maximum(i * r8 - 1, 0), col))
    return pl.pallas_call(
        body, name="mixer_fwd", grid=(t // tq,),
        in_specs=[
            pl.BlockSpec((tq, NP), lambda i: (i, 0)),
            prev8(O_CG // CC), prev8(O_HC // CC),
            pl.BlockSpec((BLK, 2 * NKV * HP), lambda i: (jnp.maximum(i * nb - 1, 0), O_K // (2 * NKV * HP))),
            pl.BlockSpec((tq, D), lambda i: (i, 0)),
            _const_spec((8, CC)), _const_spec((1, HP)), _const_spec((1, HP)),
            pl.BlockSpec(memory_space=pltpu.SMEM),
            _const_spec((1, CC)), _const_spec((1, NQ * HD)), _const_spec((MIXW, D)),
        ],
        out_specs=[pl.BlockSpec((tq, D), lambda i: (i, 0)), pl.BlockSpec((tq, MIXW), lambda i: (i, 0)),
                   pl.BlockSpec((tq, NQ * HD), lambda i: (i, 0))],
        out_shape=[jax.ShapeDtypeStruct((t, D), F32), jax.ShapeDtypeStruct((t, MIXW), BF16),
                   jax.ShapeDtypeStruct((t, NQ * HD), F32)],
        compiler_params=_cparams(("parallel",)),
    )(proj, proj, proj, proj, x, cw, gq, gk, sinks, gco, gao, wo)


def _ffn_weight_specs():
    return [pl.BlockSpec((N_CHIPS, FFB, D), lambda i, j=j: (0, j, 0), pipeline_mode=pl.Buffered(1))
            for j in range(3)]


def _ffn_fwd(xm, g2, gf, tm, tgt=None):
    t = xm.shape[0]
    last = tgt is not None

    def body(x_ref, g_ref, wg_ref, wu_ref, wd_ref, *rest):
        t_ref, rest = (rest[0], rest[1:]) if last else (None, rest)
        l_ref, rest = (rest[0], rest[1:]) if last else (None, rest)
        xo_ref, a_ref, b_ref, h2_ref = rest
        xv = x_ref[...]
        _, xh = _rms_fwd(xv, 1.0 / D)
        h2 = (xh * g_ref[...]).astype(BF16)
        h2_ref[...] = h2
        acc = xv
        for k in range(N_CHIPS):
            a = _nt(h2, wg_ref[k])
            b = _nt(h2, wu_ref[k])
            a_ref[k] = a.astype(BF16)
            b_ref[k] = b.astype(BF16)
            hm = (a * jax.nn.sigmoid(a) * b).astype(BF16)
            acc = acc + jnp.dot(hm, wd_ref[k], preferred_element_type=F32)
        if last:
            @pl.when(pl.program_id(0) == 0)
            def _():
                l_ref[...] = jnp.zeros_like(l_ref)

            e = acc - t_ref[...]
            xo_ref[...] = e * (1.0 / D)
            l_ref[...] += jnp.sum(jnp.sum(e * e, axis=-1, keepdims=True), axis=0, keepdims=True) * (0.5 / D)
        else:
            xo_ref[...] = acc

    row = lambda w: pl.BlockSpec((tm, w), lambda i: (i, 0))
    blk = pl.BlockSpec((N_CHIPS, tm, FFB), lambda i: (0, i, 0))
    bsd = jax.ShapeDtypeStruct((N_CHIPS, t, FFB), BF16)
    return pl.pallas_call(
        body, name="ffn_fwd_loss" if last else "ffn_fwd", grid=(t // tm,),
        in_specs=[row(D), _const_spec((1, D))] + _ffn_weight_specs() + ([row(D)] if last else []),
        out_specs=([pl.BlockSpec((8, 128), lambda i: (0, 0))] if last else []) + [row(D), blk, blk, row(D)],
        out_shape=([jax.ShapeDtypeStruct((8, 128), F32)] if last else [])
        + [jax.ShapeDtypeStruct((t, D), F32), bsd, bsd, jax.ShapeDtypeStruct((t, D), BF16)],
        compiler_params=_cparams(("arbitrary" if last else "parallel",)),
    )(*((xm, g2, gf, gf, gf) + ((tgt,) if last else ())))


def _ffn_bwd(dy, xm, g2, a, b, gf, tm):
    t = dy.shape[0]

    def body(dy_ref, x_ref, g_ref, a_ref, b_ref, wg_ref, wu_ref, wd_ref, dx_ref, da_ref, db_ref, hm_ref, dg_ref):
        @pl.when(pl.program_id(0) == 0)
        def _():
            dg_ref[...] = jnp.zeros_like(dg_ref)

        dyv = dy_ref[...]
        dyb = dyv.astype(BF16)
        dh2 = jnp.zeros_like(dyv)
        for k in range(N_CHIPS):
            dhm = _nt(dyb, wd_ref[k])
            av = a_ref[k].astype(F32)
            bv = b_ref[k].astype(F32)
            sig = jax.nn.sigmoid(av)
            sil = av * sig
            hm_ref[k] = (sil * bv).astype(BF16)
            da = (dhm * bv * (sig * (1.0 + av * (1.0 - sig)))).astype(BF16)
            db = (dhm * sil).astype(BF16)
            da_ref[k] = da
            db_ref[k] = db
            dh2 = (dh2 + jnp.dot(da, wg_ref[k], preferred_element_type=F32)
                   + jnp.dot(db, wu_ref[k], preferred_element_type=F32))
        r, xh = _rms_fwd(x_ref[...], 1.0 / D)
        dg_ref[...] += jnp.sum(dh2 * xh, axis=0, keepdims=True)
        dx_ref[...] = dyv + _rms_bwd(dh2, g_ref[...], xh, r, 1.0 / D)

    row = lambda w: pl.BlockSpec((tm, w), lambda i: (i, 0))
    blk = pl.BlockSpec((N_CHIPS, tm, FFB), lambda i: (0, i, 0))
    bsd = jax.ShapeDtypeStruct((N_CHIPS, t, FFB), BF16)
    return pl.pallas_call(
        body, name="ffn_bwd", grid=(t // tm,),
        in_specs=[row(D), row(D), _const_spec((1, D)), blk, blk] + _ffn_weight_specs(),
        out_specs=[row(D), blk, blk, blk, pl.BlockSpec((1, D), lambda i: (0, 0))],
        out_shape=[jax.ShapeDtypeStruct((t, D), F32), bsd, bsd, bsd, jax.ShapeDtypeStruct((1, D), F32)],
        compiler_params=_cparams(("arbitrary",)),
    )(dy, xm, g2, a, b, gf, gf, gf)


def _wgrad_blocks(a, b, tt, name):
    _, t, rows = a.shape
    cols = b.shape[1]
    nsteps = t // tt

    def body(a_ref, b_ref, o_ref, acc_ref):
        s = pl.program_id(0)

        @pl.when(s == 0)
        def _():
            acc_ref[...] = jnp.zeros_like(acc_ref)

        bv = b_ref[...].astype(BF16)
        for k in range(N_CHIPS):
            acc_ref[k] += _tn(a_ref[k], bv)

        @pl.when(s == nsteps - 1)
        def _():
            o_ref[...] = acc_ref[...].astype(BF16)

    return pl.pallas_call(
        body, name=name, grid=(nsteps,),
        in_specs=[pl.BlockSpec((N_CHIPS, tt, rows), lambda s: (0, s, 0)), pl.BlockSpec((tt, cols), lambda s: (s, 0))],
        out_specs=pl.BlockSpec((N_CHIPS, rows, cols), lambda s: (0, 0, 0)),
        out_shape=jax.ShapeDtypeStruct((N_CHIPS, rows, cols), BF16),
        scratch_shapes=[pltpu.VMEM((N_CHIPS, rows, cols), F32)],
        compiler_params=_cparams(("arbitrary",)),
    )(a, b)


def _head_rows(first, n_heads):
    return [(first + g * HD, first + g * HP, HD) for g in range(n_heads)]


def _wgrad(a, b, tt, name, pieces=None):
    t, k = a.shape
    n = b.shape[1]
    nsteps = t // tt
    pieces = pieces or [(0, 0, k)]
    rows = sum(p[2] for p in pieces)

    def body(a_ref, b_ref, o_ref, acc_ref):
        s = pl.program_id(0)

        @pl.when(s == 0)
        def _():
            acc_ref[...] = jnp.zeros_like(acc_ref)

        acc_ref[...] += _tn(a_ref[...].astype(BF16), b_ref[...].astype(BF16))

        @pl.when(s == nsteps - 1)
        def _():
            for dst, src, size in pieces:
                o_ref[dst:dst + size, :] = acc_ref[src:src + size, :].astype(BF16)

    return pl.pallas_call(
        body, name=name, grid=(nsteps,),
        in_specs=[pl.BlockSpec((tt, k), lambda s: (s, 0)), pl.BlockSpec((tt, n), lambda s: (s, 0))],
        out_specs=pl.BlockSpec((rows, n), lambda s: (0, 0)),
        out_shape=jax.ShapeDtypeStruct((rows, n), BF16),
        scratch_shapes=[pltpu.VMEM((k, n), F32)],
        compiler_params=_cparams(("arbitrary",)),
    )(a, b)


def _mixer_bwd(dxm, proj, ao, cw, gq, gk, sinks, gco, gao, wo, tq):
    t = proj.shape[0]
    nb = tq // BLK
    r8 = tq // 8
    nt = t // tq
    te = tq + 8
    kvw = 2 * NKV * HP

    def body(dx_ref, dxn_ref, p_ref, cgp_ref, hcp_ref, bgn_ref, cgn_ref, hcn_ref, kvp_ref, ao_ref, cw_ref, gq_ref,
             gk_ref, sk_ref, gco_ref, gao_ref, wo_ref,
             dpm_ref, dkvm_ref, dkvh_ref, dcw_ref, dgq_ref, dgk_ref, dsk_ref, dgco_ref, dgao_ref, acc_ref):
        i = pl.program_id(0)

        @pl.when(i == 0)
        def _():
            for r in (dcw_ref, dgq_ref, dgk_ref, dsk_ref, dgco_ref, dgao_ref):
                r[...] = jnp.zeros_like(r)

        acc_ref[...] = jnp.zeros_like(acc_ref)
        live_rows = jnp.where(i < nt - 1, te, tq)
        dxb = dx_ref[...].astype(BF16)
        dxe = jnp.concatenate([dxb, dxn_ref[...].astype(BF16)], axis=0)
        dcn = _nt(dxe, wo_ref[0:CC, :])
        bg = jnp.concatenate([p_ref[:, O_BG:O_BG + CC], bgn_ref[...]], axis=0)
        cg = jnp.concatenate([p_ref[:, O_CG:O_CG + CC], cgn_ref[...]], axis=0)
        hc = jnp.concatenate([p_ref[:, O_HC:O_HC + CC], hcn_ref[...]], axis=0)
        u = cg * hc
        up = jnp.where(i > 0, cgp_ref[...] * hcp_ref[...], 0.0)
        u1, u2 = _conv_taps(jnp.concatenate([up, u], axis=0), te)
        w0, w1, w2 = cw_ref[0:1, :], cw_ref[1:2, :], cw_ref[2:3, :]
        y = w0 * u2 + w1 * u1 + w2 * u
        co = bg * y
        rc, coh = _rms_fwd(co, 1.0 / CC)
        dco = _rms_bwd(dcn, gco_ref[...], coh, rc, 1.0 / CC)
        row_io = lax.broadcasted_iota(jnp.int32, (te, 1), 0)
        own = row_io < tq
        dgco_ref[...] += jnp.sum(jnp.where(own, dcn * coh, 0.0), axis=0, keepdims=True)
        dyc = jnp.where(row_io < live_rows, dco * bg, 0.0)
        dyo = jnp.where(own, dyc, 0.0)
        dcw_ref[0:1, :] += jnp.sum(dyo * u2, axis=0, keepdims=True)
        dcw_ref[1:2, :] += jnp.sum(dyo * u1, axis=0, keepdims=True)
        dcw_ref[2:3, :] += jnp.sum(dyo * u, axis=0, keepdims=True)
        dy1 = pltpu.roll(dyc, te - 1, 0)[0:tq]
        dy2 = pltpu.roll(dyc, te - 2, 0)[0:tq]
        du = w2 * dyc[0:tq] + w1 * dy1 + w0 * dy2
        dpm_ref[:, O_BG:O_BG + CC] = (dco[0:tq] * y[0:tq]).astype(BF16)
        dpm_ref[:, O_CG:O_CG + CC] = (du * hc[0:tq]).astype(BF16)
        dpm_ref[:, O_HC:O_HC + CC] = (du * cg[0:tq]).astype(BF16)
        kraw = jnp.concatenate([kvp_ref[:, 0:NKV * HP], p_ref[:, O_K:O_K + NKV * HP]], axis=0)
        vraw = jnp.concatenate([kvp_ref[:, NKV * HP:], p_ref[:, O_V:O_V + NKV * HP]], axis=0)
        gqv, gkv = gq_ref[...], gk_ref[...]
        keys = _norm_keys(kraw, gkv)
        vb = [vraw[:, h * HP:(h + 1) * HP].astype(BF16) for h in range(NKV)]
        base_valid, c_io = _band_mask()
        lane = lax.broadcasted_iota(jnp.int32, (1, HP), 1)
        dgq, dgk, dsk = (jnp.zeros((1, HP), F32) for _ in range(3))
        dgao = jnp.zeros((1, NQ * HD), F32)
        for b in range(nb):
            lo = jnp.where(i * nb + b == 0, BLK, 0)
            valid = base_valid & (c_io >= lo)
            band = slice(b * BLK, b * BLK + 2 * BLK)
            blk = slice(b * BLK, (b + 1) * BLK)
            ra, aoh = _rms_fwd(ao_ref[blk, :], 1.0 / (NQ * HD))
            danb = _nt(dxb[blk], wo_ref[CC:MIXW, :])
            dgao = dgao + jnp.sum(danb * aoh, axis=0, keepdims=True)
            dao = _rms_bwd(danb, gao_ref[...], aoh, ra, 1.0 / (NQ * HD))
            dos = [dao[:, g // 2 * HP:(g // 2 + 1) * HP] for g in range(NQ)]
            dos = [(d if g % 2 == 0 else pltpu.roll(d, HD, 1)).astype(BF16) for g, d in enumerate(dos)]
            fwd = []
            for g in range(NQ):
                rq, qh = _rms_fwd(p_ref[blk, O_Q + g * HP:O_Q + (g + 1) * HP], 1.0 / HD)
                qs = (qh * (gqv * SCALE)).astype(BF16)
                fwd.append((rq, qh, qs) + _attn_probs(qs, keys[g // GRP][2][band], sk_ref[0, g], valid))
            dqs = []
            for h in range(NKV):
                khat, rk, kn = [a[band] for a in keys[h]]
                dss, prbs, qns, dobs = [], [], [], []
                for g in range(h * GRP, (h + 1) * GRP):
                    rq, qh, qs, pr, ps = fwd[g]
                    dob = dos[g]
                    dp = _nt(dob, vb[h][band])
                    delta = jnp.sum(pr * dp, axis=-1, keepdims=True)
                    dsb = (pr * (dp - delta)).astype(BF16)
                    dsk = dsk + jnp.where(lane == g, -jnp.sum(ps * delta, axis=0, keepdims=True), 0.0)
                    dqn = jnp.dot(dsb, kn, preferred_element_type=F32) * SCALE
                    dgq = dgq + jnp.sum(dqn * qh, axis=0, keepdims=True)
                    dqs.append(_rms_bwd(dqn, gqv, qh, rq, 1.0 / HD).astype(BF16))
                    dss.append(dsb)
                    prbs.append(pr.astype(BF16))
                    qns.append(qs)
                    dobs.append(dob)
                dkn = _tn(jnp.concatenate(dss, axis=0), jnp.concatenate(qns, axis=0))
                dv = _tn(jnp.concatenate(prbs, axis=0), jnp.concatenate(dobs, axis=0))
                dgk = dgk + jnp.sum(dkn * khat, axis=0, keepdims=True)
                acc_ref[band, h * HP:(h + 1) * HP] += _rms_bwd(dkn, gkv, khat, rk, 1.0 / HD)
                acc_ref[band, (NKV + h) * HP:(NKV + h + 1) * HP] += dv
            dpm_ref[blk, O_Q:O_K] = jnp.concatenate(dqs, axis=1)
        dgq_ref[...] += dgq
        dgk_ref[...] += dgk
        dsk_ref[...] += dsk
        dgao_ref[...] += dgao
        dkvh_ref[...] = acc_ref[0:BLK, :]
        dkvm_ref[...] = acc_ref[BLK:, :]

    prev8 = lambda col: pl.BlockSpec((8, CC), lambda i: (jnp.maximum(i * r8 - 1, 0), col))
    next8 = lambda col: pl.BlockSpec((8, CC), lambda i: (jnp.minimum((i + 1) * r8, t // 8 - 1), col))
    small = lambda n: pl.BlockSpec((1, n), lambda i: (0, 0))
    return pl.pallas_call(
        body, name="mixer_bwd", grid=(nt,),
        in_specs=[
            pl.BlockSpec((tq, D), lambda i: (i, 0)),
            pl.BlockSpec((8, D), lambda i: (jnp.minimum((i + 1) * r8, t // 8 - 1), 0)),
            pl.BlockSpec((tq, NP), lambda i: (i, 0)),
            prev8(O_CG // CC), prev8(O_HC // CC),
            next8(O_BG // CC), next8(O_CG // CC), next8(O_HC // CC),
            pl.BlockSpec((BLK, kvw), lambda i: (jnp.maximum(i * nb - 1, 0), O_K // kvw)),
            pl.BlockSpec((tq, NQ * HD), lambda i: (i, 0)),
            _const_spec((8, CC)), _const_spec((1, HP)), _const_spec((1, HP)),
            pl.BlockSpec(memory_space=pltpu.SMEM),
            _const_spec((1, CC)), _const_spec((1, NQ * HD)), _const_spec((MIXW, D)),
        ],
        out_specs=[
            pl.BlockSpec((tq, NMAIN), lambda i: (i, 0)),
            pl.BlockSpec((tq, kvw), lambda i: (i, 0)),
            pl.BlockSpec((BLK, kvw), lambda i: (i, 0)),
            pl.BlockSpec((8, CC), lambda i: (0, 0)), small(HP), small(HP), small(HP), small(CC), small(NQ * HD),
        ],
        out_shape=[
            jax.ShapeDtypeStruct((t, NMAIN), BF16), jax.ShapeDtypeStruct((t, kvw), F32),
            jax.ShapeDtypeStruct((nt * BLK, kvw), F32),
            jax.ShapeDtypeStruct((8, CC), F32), jax.ShapeDtypeStruct((1, HP), F32), jax.ShapeDtypeStruct((1, HP), F32),
            jax.ShapeDtypeStruct((1, HP), F32), jax.ShapeDtypeStruct((1, CC), F32),
            jax.ShapeDtypeStruct((1, NQ * HD), F32),
        ],
        scratch_shapes=[pltpu.VMEM((tq + BLK, kvw), F32)],
        compiler_params=_cparams(("arbitrary",)),
    )(dxm, dxm, proj, proj, proj, proj, proj, proj, proj, ao, cw, gq, gk, sinks, gco, gao, wo)


def _inproj_bwd(dpm, dkvm, dkvh, wpt, x, g1, dxm, tm):
    t = x.shape[0]
    kvw = 2 * NKV * HP
    nt = t // tm

    def body(dp_ref, dk_ref, dh_ref, w_ref, x_ref, g_ref, dxm_ref, dx_ref, dg_ref, dkv_ref):
        i = pl.program_id(0)

        @pl.when(i == 0)
        def _():
            dg_ref[...] = jnp.zeros_like(dg_ref)

        halo = jnp.where(i < nt - 1, dh_ref[...], 0.0)
        dkv_ref[0:tm - BLK, :] = dk_ref[0:tm - BLK, :].astype(BF16)
        dkv_ref[tm - BLK:tm, :] = (dk_ref[tm - BLK:tm, :] + halo).astype(BF16)
        dh = (jnp.dot(dp_ref[...], w_ref[0:NMAIN, :], preferred_element_type=F32)
              + jnp.dot(dkv_ref[...], w_ref[NMAIN:NP, :], preferred_element_type=F32))
        r, xh = _rms_fwd(x_ref[...], 1.0 / D)
        dg_ref[...] += jnp.sum(dh * xh, axis=0, keepdims=True)
        dx_ref[...] = dxm_ref[...] + _rms_bwd(dh, g_ref[...], xh, r, 1.0 / D)

    row = lambda w: pl.BlockSpec((tm, w), lambda i: (i, 0))
    return pl.pallas_call(
        body, name="inproj_bwd", grid=(nt,),
        in_specs=[row(NMAIN), row(kvw), pl.BlockSpec((BLK, kvw), lambda i: (jnp.minimum(i + 1, nt - 1), 0)),
                  _const_spec((NP, D)), row(D), _const_spec((1, D)), row(D)],
        out_specs=[row(D), pl.BlockSpec((1, D), lambda i: (0, 0)), row(kvw)],
        out_shape=[jax.ShapeDtypeStruct((t, D), F32), jax.ShapeDtypeStruct((1, D), F32),
                   jax.ShapeDtypeStruct((t, kvw), BF16)],
        compiler_params=_cparams(("arbitrary",)),
    )(dpm, dkvm, dkvh, wpt, x, g1, dxm)


def _rows_tile(rows, cap=512):
    for cand in range(min(rows, cap) // 16 * 16, 0, -16):
        if rows % cand == 0:
            return cand
    return rows


def _presum_halves(gs, theirs, core):
    n = len(gs)

    def body(c_ref, *refs):
        for g_ref, t_ref, o_ref in zip(refs[:n], refs[n:2 * n], refs[2 * n:]):
            o_ref[...] = (g_ref[...].astype(F32) + t_ref[...].astype(F32)).astype(BF16)

    half = lambda ta: pl.BlockSpec((None,) + ta.shape[1:], lambda k, c_ref: (k, 0, 0))
    own = lambda ta: pl.BlockSpec((None,) + ta.shape[1:], lambda k, c_ref: (k, c_ref[0], 0))
    return pl.pallas_call(
        body, name="presum",
        grid_spec=pltpu.PrefetchScalarGridSpec(
            num_scalar_prefetch=1, grid=(N_CHIPS,),
            in_specs=[own(ta) for ta in theirs] + [half(ta) for ta in theirs],
            out_specs=[half(ta) for ta in theirs]),
        out_shape=[jax.ShapeDtypeStruct(ta.shape, BF16) for ta in theirs],
        compiler_params=_cparams(("parallel",)),
    )(core, *gs, *theirs)


def _sum_chips(cs):
    n = len(cs)
    steps = 2

    def body(*refs):
        for c_ref, o_ref in zip(refs[:n], refs[n:]):
            acc = c_ref[0].astype(F32)
            for j in range(1, N_CHIPS):
                acc = acc + c_ref[j].astype(F32)
            o_ref[...] = acc

    return pl.pallas_call(
        body, name="chipsum", grid=(steps,),
        in_specs=[pl.BlockSpec((N_CHIPS, c.shape[1] // steps, c.shape[2]), lambda i: (0, i, 0)) for c in cs],
        out_specs=[pl.BlockSpec((c.shape[1] // steps, c.shape[2]), lambda i: (i, 0)) for c in cs],
        out_shape=[jax.ShapeDtypeStruct(c.shape[1:], F32) for c in cs],
        compiler_params=_cparams(("parallel",)),
    )(*cs)


def _adamw(w, g, m, v, name):
    rows, cols = w.shape
    tr = _rows_tile(rows, 256)
    c1 = 1.0 - ADAM_B1 ** ADAM_STEP
    c2 = 1.0 - ADAM_B2 ** ADAM_STEP

    def body(w_ref, g_ref, m_ref, v_ref, d_ref, mo_ref, vo_ref):
        gv = g_ref[...]
        mn = ADAM_B1 * m_ref[...] + (1.0 - ADAM_B1) * gv
        vn = ADAM_B2 * v_ref[...] + (1.0 - ADAM_B2) * (gv * gv)
        mo_ref[...] = mn
        vo_ref[...] = vn
        d_ref[...] = -ADAM_LR * ((mn / c1) / (jnp.sqrt(vn / c2) + ADAM_EPS) + ADAM_WD * w_ref[...])

    spec = pl.BlockSpec((tr, cols), lambda i: (i, 0))
    sds = jax.ShapeDtypeStruct((rows, cols), F32)
    return pl.pallas_call(
        body, name=name, grid=(rows // tr,), in_specs=[spec] * 4, out_specs=[spec] * 3, out_shape=[sds] * 3,
        compiler_params=_cparams(("parallel",)),
    )(w, g, m, v)


def _place():
    x, y, c = lax.axis_index("x"), lax.axis_index("y"), lax.axis_index("c")
    chips = [(1 - x, y), (x, 1 - y), (1 - x, 1 - y)]
    return x, y, c, chips


ANY = pl.BlockSpec(memory_space=pl.ANY)
DMA_ROWS = 64


def _pieces(shape):
    rows = shape[-2]
    step = DMA_ROWS if rows % DMA_ROWS == 0 else rows
    lead = [()]
    for n in shape[:-2]:
        lead = [i + (k,) for i in lead for k in range(n)]
    return [i + (pl.ds(r0, step),) for i in lead for r0 in range(0, rows, step)]


def _start_pieces(make, src, dst):
    for idx in _pieces(src.shape):
        make(src.at[idx], dst.at[idx]).start()


def _gather_layer(blocks, layer):
    nw = len(blocks)

    def body(*refs):
        _gather_body(refs[:nw], refs[nw:2 * nw], refs[2 * nw:], layer, _start_pieces)

    return pl.pallas_call(
        body, name=f"gather_layer{layer}", in_specs=[ANY] * nw, out_specs=[ANY] * nw,
        out_shape=[jax.ShapeDtypeStruct((N_CHIPS,) + b.shape, b.dtype) for b in blocks],
        scratch_shapes=[pltpu.SemaphoreType.DMA((3, nw))] * 4,
        compiler_params=_cparams(has_side_effects=True),
    )(*blocks)


def _gather_body(srcs, outs, sems, layer, start):
    nw = len(srcs)
    ssem, rsem, fssem, frsem = sems
    x, y, c, chips = _place()
    kme = 2 * x + y

    def plane(j, w, to):
        return lambda s, d: pltpu.make_async_remote_copy(
            src_ref=s, dst_ref=d, send_sem=ssem.at[j, w], recv_sem=rsem.at[j, w], device_id=to,
            device_id_type=MESH)

    def passed(j, w):
        return lambda s, d: pltpu.make_async_remote_copy(
            src_ref=s, dst_ref=d, send_sem=fssem.at[j, w], recv_sem=frsem.at[j, w],
            device_id=(x, y, 1 - c), device_id_type=MESH)

    @pl.when(c == layer)
    def _():
        for j, (px, py) in enumerate(chips):
            for w in range(nw):
                start(plane(j, w, (px, py, c)), srcs[w], outs[w].at[kme])
        for j, (px, py) in enumerate(chips):
            for w in range(nw):
                got = outs[w].at[2 * px + py]
                plane(j, w, (px, py, c))(got, got).wait_recv()
                start(passed(j, w), got, got)
        for j, (px, py) in enumerate(chips):
            for w in range(nw):
                got = outs[w].at[2 * px + py]
                plane(j, w, (px, py, c))(got, got).wait_send()
                passed(j, w)(got, got).wait_send()

    @pl.when(c != layer)
    def _():
        for j, (px, py) in enumerate(chips):
            for w in range(nw):
                got = outs[w].at[2 * px + py]
                passed(j, w)(got, got).wait_recv()


def _handshake(peers):
    barrier = pltpu.get_barrier_semaphore()
    for peer in peers:
        pl.semaphore_signal(barrier, inc=1, device_id=peer, device_id_type=MESH)
    pl.semaphore_wait(barrier, len(peers))


def _handshake_all():
    x, y, c, _ = _place()
    _handshake([(x ^ (r >> 2), y ^ ((r >> 1) & 1), c ^ (r & 1)) for r in range(1, 8)])


def _gather_layer_async(blocks, layer, name, collective_id):
    hbm = pltpu.MemorySpace.HBM
    srcs = [jax.new_ref(b, memory_space=hbm) for b in blocks]
    outs = [jax.empty_ref(jax.ShapeDtypeStruct((N_CHIPS,) + b.shape, b.dtype), memory_space=hbm) for b in blocks]

    @pl.kernel(mesh=plsc.ScalarSubcoreMesh(axis_name="seq", num_cores=1), name=name,
               scratch_types=[pltpu.SemaphoreType.DMA((3, len(blocks)))] * 4,
               compiler_params=pltpu.CompilerParams(collective_id=collective_id))
    def launch(*sems):
        _handshake_all()
        _gather_body(srcs, outs, sems, layer, lambda make, s, d: make(s, d).start())

    launch()
    return [o[...] for o in outs]


def _swap_siblings(arrs, halves, name, collective_id=None):
    nw = len(arrs)
    out_sds = [jax.ShapeDtypeStruct((a.shape[0], a.shape[1] // 2, a.shape[2]) if halves else a.shape, a.dtype)
               for a in arrs]

    def exchange(srcs, outs, ssem, rsem, start):
        x, y, c, _ = _place()

        def give(w):
            return lambda s, d: pltpu.make_async_remote_copy(
                src_ref=s, dst_ref=d, send_sem=ssem.at[w], recv_sem=rsem.at[w], device_id=(x, y, 1 - c),
                device_id_type=MESH)

        for w in range(nw):
            hr = outs[w].shape[1]
            start(give(w), srcs[w].at[:, pl.ds((1 - c) * hr, hr)] if halves else srcs[w], outs[w])
        for w in range(nw):
            give(w)(outs[w], outs[w]).wait()

    if collective_id is None:
        def body(*refs):
            exchange(refs[:nw], refs[nw:2 * nw], *refs[2 * nw:], _start_pieces)

        return pl.pallas_call(
            body, name=name, in_specs=[ANY] * nw, out_specs=[ANY] * nw, out_shape=out_sds,
            scratch_shapes=[pltpu.SemaphoreType.DMA((nw,))] * 2,
            compiler_params=_cparams(has_side_effects=True),
        )(*arrs)

    hbm = pltpu.MemorySpace.HBM
    srcs = [jax.new_ref(a, memory_space=hbm) for a in arrs]
    outs = [jax.empty_ref(sds, memory_space=hbm) for sds in out_sds]

    @pl.kernel(mesh=plsc.ScalarSubcoreMesh(axis_name="seq", num_cores=1), name=name,
               scratch_types=[pltpu.SemaphoreType.DMA((nw,))] * 2,
               compiler_params=pltpu.CompilerParams(collective_id=collective_id))
    def launch(ssem, rsem):
        x, y, c, _ = _place()
        _handshake([(x, y, 1 - c)])
        exchange(srcs, outs, ssem, rsem, lambda make, s, d: make(s, d).start())

    launch()
    return [o[...] for o in outs]


def _scatter_chips(ps):
    nw = len(ps)

    def body(*refs):
        _scatter_body(refs[:nw], refs[nw:2 * nw], refs[2 * nw:], _start_pieces)

    return pl.pallas_call(
        body, name="scatter_chips", in_specs=[ANY] * nw, out_specs=[ANY] * nw,
        out_shape=[jax.ShapeDtypeStruct(p.shape, p.dtype) for p in ps],
        scratch_shapes=[pltpu.SemaphoreType.DMA((3, nw)), pltpu.SemaphoreType.DMA((3, nw))],
        compiler_params=_cparams(has_side_effects=True),
    )(*ps)


def _scatter_body(srcs, outs, sems, start):
    nw = len(srcs)
    ssem, rsem = sems
    x, y, c, chips = _place()
    kme = 2 * x + y

    def give(j, w, to):
        return lambda s, d: pltpu.make_async_remote_copy(
            src_ref=s, dst_ref=d, send_sem=ssem.at[j, w], recv_sem=rsem.at[j, w], device_id=to,
            device_id_type=MESH)

    for j, (px, py) in enumerate(chips):
        for w in range(nw):
            start(give(j, w, (px, py, c)), srcs[w].at[2 * px + py], outs[w].at[kme])
    for j, (px, py) in enumerate(chips):
        for w in range(nw):
            got = outs[w].at[2 * px + py]
            give(j, w, (px, py, c))(got, got).wait_recv()
    for j, (px, py) in enumerate(chips):
        for w in range(nw):
            sent = srcs[w].at[2 * px + py]
            give(j, w, (px, py, c))(sent, sent).wait_send()


def _scatter_chips_async(ps, name, collective_id):
    hbm = pltpu.MemorySpace.HBM
    srcs = [jax.new_ref(p, memory_space=hbm) for p in ps]
    outs = [jax.empty_ref(jax.ShapeDtypeStruct(p.shape, p.dtype), memory_space=hbm) for p in ps]

    @pl.kernel(mesh=plsc.ScalarSubcoreMesh(axis_name="seq", num_cores=1), name=name,
               scratch_types=[pltpu.SemaphoreType.DMA((3, len(ps)))] * 2,
               compiler_params=pltpu.CompilerParams(collective_id=collective_id))
    def launch(*sems):
        _handshake_all()
        _scatter_body(srcs, outs, sems, lambda make, s, d: make(s, d).start())

    launch()
    return [o[...] for o in outs]


def _allreduce_small(v):
    rows = v.shape[0]

    def body(v_ref, o_ref, buf, ssem, rsem):
        x, y, c, _ = _place()
        me = 4 * x + 2 * y + c
        buf[me] = v_ref[...]
        sends = []
        for r in range(1, 8):
            peer = (x ^ (r >> 2), y ^ ((r >> 1) & 1), c ^ (r & 1))
            cp = pltpu.make_async_remote_copy(
                src_ref=v_ref, dst_ref=buf.at[me], send_sem=ssem.at[r - 1], recv_sem=rsem.at[r - 1],
                device_id=peer, device_id_type=MESH)
            cp.start()
            sends.append(cp)
        for r in range(1, 8):
            src = me ^ r
            pltpu.make_async_remote_copy(
                src_ref=v_ref, dst_ref=buf.at[src], send_sem=ssem.at[r - 1], recv_sem=rsem.at[r - 1],
                device_id=(x, y, c), device_id_type=MESH).wait_recv()
        for cp in sends:
            cp.wait_send()
        acc = buf[0]
        for d in range(1, 8):
            acc = acc + buf[d]
        o_ref[...] = acc

    vm = pl.BlockSpec(memory_space=pltpu.VMEM)
    return pl.pallas_call(
        body, name="allreduce_small", in_specs=[vm], out_specs=vm,
        out_shape=jax.ShapeDtypeStruct(v.shape, F32),
        scratch_shapes=[pltpu.VMEM((8, rows, 128), F32), pltpu.SemaphoreType.DMA((7,)),
                        pltpu.SemaphoreType.DMA((7,))],
        compiler_params=_cparams(has_side_effects=True),
    )(v)


def _t(w):
    return jnp.swapaxes(w, -1, -2)


def _count(shape):
    n = 1
    for s in shape:
        n *= s
    return n


def _pack_rows(arrs):
    flat = [jnp.pad(a.reshape(-1), (0, (-_count(a.shape)) % 128)) for a in arrs]
    v = jnp.concatenate(flat)
    rows = -(-v.shape[0] // (8 * 128)) * 8
    return jnp.pad(v, (0, rows * 128 - v.shape[0])).reshape(rows, 128)


def kernel(x, norm1_g, w_in, conv_w, q_norm_g, k_norm_g, sinks, conv_out_g, attn_out_g, w_o, norm2_g, w_gate, w_up, w_down, loss_target, m_norm1_g, m_w_in, m_conv_w, m_q_norm_g, m_k_norm_g, m_sinks, m_conv_out_g, m_attn_out_g, m_w_o, m_norm2_g, m_w_gate, m_w_up, m_w_down, v_norm1_g, v_w_in, v_conv_w, v_q_norm_g, v_k_norm_g, v_sinks, v_conv_out_g, v_attn_out_g, v_w_o, v_norm2_g, v_w_gate, v_w_up, v_w_down):
    depth = w_in.shape[0]
    t = x.shape[1]
    xs = x.reshape(t, D)
    tgt = loss_target.reshape(t, D)
    xi, yi = lax.axis_index("x"), lax.axis_index("y")
    kme = 2 * xi + yi
    tm = min(512, t)
    tq = min(512, t)
    tf = min(256, t)
    tw = min(1024, t)

    cwp = jnp.pad(conv_w.reshape(depth * 3, CC // N_CHIPS), ((0, 8 - depth * 3), (0, 0)))
    own_f = [jnp.concatenate([_t(w_gate[l]), _t(w_up[l]), w_down[l]], axis=0).astype(BF16) for l in range(depth)]
    own_o = [w_o[l].astype(BF16) for l in range(depth)]
    own_i = [_t(w_in[l]).astype(BF16) for l in range(depth)]
    mine = lambda got, own: lax.dynamic_update_index_in_dim(got, own, kme, 0)
    got_i0, got_o0, got_cw = _gather_layer([own_i[0], own_o[0], cwp], 0)
    gf0_in = lax.optimization_barrier((own_f[0], got_i0))[0]
    (got_f0,) = _gather_layer_async([gf0_in], 0, "gather_ffn0_seq", collective_id=6)
    cw_full = mine(got_cw, cwp).transpose(1, 0, 2).reshape(8, CC)[:depth * 3].reshape(depth, 3, CC)

    chip = kme.reshape(1).astype(jnp.int32)

    def layer_params(l, got_o):
        return dict(
            wo=mine(got_o, own_o[l]).reshape(MIXW, D),
            cw=jnp.pad(cw_full[l], ((0, 5), (0, 0))),
            g1=norm1_g[l].reshape(1, D), g2=norm2_g[l].reshape(1, D),
            gq=jnp.pad(q_norm_g[l], (0, HP - HD)).reshape(1, HP), gk=jnp.pad(k_norm_g[l], (0, HP - HD)).reshape(1, HP),
            sk=sinks[l].reshape(1, NQ), gco=conv_out_g[l].reshape(1, CC),
            gao=attn_out_g[l].reshape(1, NQ * HD))

    saved, layers = [], []
    cur = xs
    for l in range(depth):
        x_in = cur
        if l == 0:
            got_i, p = got_i0, layer_params(0, got_o0)
        else:
            got_f1, got_o1, got_i = lax.optimization_barrier((got_l1, cur))[0]
            p = layer_params(1, got_o1)
        proj, h, p["wpt"] = _inproj_fwd(cur, p["g1"], got_i, own_i[l], chip, tm)
        xm, mix, ao = _mixer_fwd(proj, cur, p["cw"], p["gq"], p["gk"], p["sk"], p["gco"], p["gao"], p["wo"], tq)
        if l == 0:
            got_f0 = lax.optimization_barrier((got_f0, xm))[0]
            l1_in = lax.optimization_barrier(([own_f[1], own_o[1], own_i[1]], got_f0))[0]
            got_l1 = _gather_layer_async(l1_in, 1, "gather_layer1_seq", collective_id=1)
        p["gf"] = mine(got_f0 if l == 0 else got_f1, own_f[l])
        layers.append(p)
        if l < depth - 1:
            cur, a, b, h2 = _ffn_fwd(xm, p["g2"], p["gf"], tm)
        else:
            lpart, dy, a, b, h2 = _ffn_fwd(xm, p["g2"], p["gf"], tm, tgt)
        saved.append(dict(x=x_in, proj=proj, h=h, xm=xm, mix=mix, ao=ao, a=a, b=b, h2=h2))

    nt = t // tq
    ci = lax.axis_index("c")
    core = ci.reshape(1).astype(jnp.int32)
    rbig = [dict() for _ in range(depth)]
    gsmall = [None] * depth

    def after_(vals, after):
        return vals if after is None else lax.optimization_barrier((vals, after))[0]

    def reduce_1(gs, tag, ids):
        return gs, _swap_siblings(gs, True, f"swap_halves_{tag}_seq", ids[0]), tag, ids

    def reduce_2(state, after):
        gs, theirs, tag, ids = state
        ps = _presum_halves(gs, after_(theirs, after), core)
        return ps, _scatter_chips_async(ps, f"scatter_{tag}_seq", ids[1]), tag, ids

    def reduce_3(state, after):
        ps, got, tag, ids = state
        cs = [lax.dynamic_update_index_in_dim(g, lax.dynamic_index_in_dim(q, kme, 0, keepdims=False), kme, 0)
              for g, q in zip(after_(got, after), ps)]
        r_mine = _sum_chips(cs)
        return r_mine, _swap_siblings(r_mine, False, f"swap_reduced_{tag}" + ("_seq" if ids[2] else ""), ids[2])

    def reduce_4(state, after):
        r_mine, r_theirs = state
        return [jnp.where(ci == 0, jnp.concatenate([a, b], axis=0), jnp.concatenate([b, a], axis=0))
                for a, b in zip(r_mine, after_(r_theirs, after))]

    ids = {"ffn1": (7, 4, 8), "in1": (9, 5, 10), "ffn0": (11, 2, 12), "in0": (13, 3, None)}
    in_2 = None
    handed = {}
    for l in reversed(range(depth)):
        p, s = layers[l], saved[l]
        dxm, da, db, hm, dg2 = _ffn_bwd(dy, s["xm"], p["g2"], s["a"], s["b"], p["gf"], tf)
        if in_2 is not None:
            in_2 = reduce_2(in_2, dxm)
        g_wg = _wgrad_blocks(da, s["h2"], tw, "wgrad_gate")
        g_wu = _wgrad_blocks(db, s["h2"], tw, "wgrad_up")
        g_wd = _wgrad_blocks(hm, dy, tw, "wgrad_down")
        if in_2 is not None:
            handed[f"in{l + 1}"] = reduce_3(in_2, g_wd)
        ffn_1 = reduce_1([g_wg, g_wu, g_wd], f"ffn{l}", ids[f"ffn{l}"])
        dpm, dkvm, dkvh, dcw, dgq, dgk, dsk, dgco, dgao = _mixer_bwd(
            dxm, s["proj"], s["ao"], p["cw"], p["gq"], p["gk"], p["sk"], p["gco"], p["gao"], p["wo"], tq)
        ffn_2 = reduce_2(ffn_1, dpm)
        g_o = _wgrad(s["mix"], dxm, tw, "wgrad_o")
        dx, dg1, dkv = _inproj_bwd(dpm, dkvm, dkvh, p["wpt"], s["x"], p["g1"], dxm, tq)
        g_in = jnp.concatenate(
            [_wgrad(dpm, s["h"], tw, "wgrad_in_main", [(0, 0, O_Q)] + _head_rows(O_Q, NQ)),
             _wgrad(dkv, s["h"], tw, "wgrad_in_kv", _head_rows(0, 2 * NKV))], axis=0)
        dy = dx
        gsmall[l] = dict(g1=dg1, cw=dcw[:3], gq=dgq[0, :HD], gk=dgk[0, :HD], sk=dsk[0, :NQ], gco=dgco,
                         gao=dgao, g2=dg2)
        handed[f"ffn{l}"] = reduce_3(ffn_2, g_in)
        in_2 = reduce_1([g_in.reshape(N_CHIPS, -1, D), g_o.reshape(N_CHIPS, -1, D)], f"in{l}", ids[f"in{l}"])
    grad_x = dy.reshape(x.shape)

    small_shapes = dict(g1=(D,), cw=(3, CC), gq=(HD,), gk=(HD,), sk=(NQ,), gco=(CC,), gao=(NQ * HD,), g2=(D,))
    red = _allreduce_small(_pack_rows([gsmall[l][n] for l in range(depth) for n in small_shapes]
                                      + [lpart[0:1, 0:1]])).reshape(-1)
    red_small, offs = {n: [] for n in small_shapes}, 0
    for l in range(depth):
        for n, shp in small_shapes.items():
            cnt = _count(shp)
            red_small[n].append(red[offs:offs + cnt].reshape(shp))
            offs += -(-cnt // 128) * 128
    loss = red[offs]
    g_small = {n: jnp.stack(v) for n, v in red_small.items()}
    g_cw = lax.dynamic_slice_in_dim(g_small["cw"], kme * (CC // N_CHIPS), CC // N_CHIPS, axis=2)

    weights = [norm1_g, w_in, conv_w, q_norm_g, k_norm_g, sinks, conv_out_g, attn_out_g, w_o, norm2_g, w_gate,
               w_up, w_down]
    moms = [m_norm1_g, m_w_in, m_conv_w, m_q_norm_g, m_k_norm_g, m_sinks, m_conv_out_g, m_attn_out_g, m_w_o,
            m_norm2_g, m_w_gate, m_w_up, m_w_down]
    vars_ = [v_norm1_g, v_w_in, v_conv_w, v_q_norm_g, v_k_norm_g, v_sinks, v_conv_out_g, v_attn_out_g, v_w_o,
             v_norm2_g, v_w_gate, v_w_up, v_w_down]
    n_w = len(weights)
    big_idx = dict(zip(("in", "o", "g", "u", "d"), (1, 8, 10, 11, 12)))
    small_idx = [n for n in range(n_w) if n not in big_idx.values()]
    grads, deltas, new_m, new_v = [None] * n_w, [None] * n_w, [None] * n_w, [None] * n_w
    for n, g in zip(small_idx, (g_small["g1"], g_cw, g_small["gq"], g_small["gk"], g_small["sk"], g_small["gco"],
                                g_small["gao"], g_small["g2"])):
        grads[n] = g

    def update_big(name):
        n = big_idx[name]
        g = jnp.stack([rbig[l][name] for l in range(depth)])
        flip = g.shape != weights[n].shape
        rows2d = lambda a3: (_t(a3) if flip else a3).reshape(-1, D)
        res = _adamw(rows2d(weights[n]), g.reshape(-1, D), rows2d(moms[n]), rows2d(vars_[n]), f"adamw_{n}")
        res = [g] + [r.reshape(g.shape) for r in res]
        grads[n], deltas[n], new_m[n], new_v[n] = [_t(r) for r in res] if flip else res

    for l in range(depth):
        rbig[l]["g"], rbig[l]["u"], rbig[l]["d"] = reduce_4(handed[f"ffn{l}"], red)
    rbig[1]["in"], rbig[1]["o"] = reduce_4(handed["in1"], red)
    update_big("g")
    in_2 = reduce_2(in_2, new_v[big_idx["g"]])
    update_big("u")
    update_big("d")
    rbig[0]["in"], rbig[0]["o"] = reduce_4(reduce_3(in_2, new_v[big_idx["d"]]), None)
    for name in ("in", "o"):
        update_big(name)
    res = _adamw(*[_pack_rows([arrs[n] for n in small_idx]) for arrs in (weights, grads, moms, vars_)],
                 "adamw_small")
    offs = 0
    for n in small_idx:
        shp = weights[n].shape
        cnt = _count(shp)
        deltas[n], new_m[n], new_v[n] = [r.reshape(-1)[offs:offs + cnt].reshape(shp) for r in res]
        offs += -(-cnt // 128) * 128
    return (loss, grad_x, *grads, *deltas, *new_m, *new_v)
```

```python
import functools

import jax
import jax.numpy as jnp
from jax import lax
from jax.experimental import pallas as pl
from jax.experimental.pallas import tpu as pltpu
from jax.experimental.pallas import tpu_sc as plsc

F32 = jnp.float32
BF16 = jnp.bfloat16

D = 1024
CC = 512
NQ = 8
NKV = 2
HD = 64
HP = 128
GRP = NQ // NKV
FF = 2816
FFB = FF // 4
BLK = 128
EPS = 1e-6
NEG = -1e30
SCALE = HD ** -0.5
O_BG, O_CG, O_HC, O_Q = 0, CC, 2 * CC, 3 * CC
O_K = O_Q + NQ * HP
O_V = O_K + NKV * HP
NP = O_V + NKV * HP
NMAIN = O_K
MIXW = CC + NQ * HD
N_CHIPS = 4
VMEM_LIMIT = 56 * 1024 * 1024
MESH = pl.DeviceIdType.MESH

ADAM_LR, ADAM_B1, ADAM_B2, ADAM_EPS, ADAM_WD, ADAM_STEP = 0.001, 0.9, 0.999, 1e-08, 0.01, 10


def _cparams(sem=None, **kw):
    if sem is not None:
        kw["dimension_semantics"] = sem
    return pltpu.CompilerParams(vmem_limit_bytes=VMEM_LIMIT, **kw)


def _const_spec(shape):
    nd = len(shape)
    return pl.BlockSpec(shape, lambda *_: (0,) * nd, pipeline_mode=pl.Buffered(1))


def _nt(a, b):
    return lax.dot_general(a, b, (((1,), (1,)), ((), ())), preferred_element_type=F32)


def _tn(a, b):
    return lax.dot_general(a, b, (((0,), (0,)), ((), ())), preferred_element_type=F32)


def _rms_fwd(x, inv_n):
    r = lax.rsqrt(jnp.sum(x * x, axis=-1, keepdims=True) * inv_n + EPS)
    return r, x * r


def _rms_bwd(dy, g, xh, r, inv_n):
    dxh = dy * g
    return r * (dxh - xh * (jnp.sum(dxh * xh, axis=-1, keepdims=True) * inv_n))


W_IN_ROWS = 3 * CC + (NQ + 2 * NKV) * HD
W_IN_BLOCK = W_IN_ROWS // N_CHIPS


def _padded_row(row):
    return row + max(row - O_Q, 0) // HD * (HP - HD)


def _w_in_pieces(k):
    first = k * W_IN_BLOCK
    plain = min(max(O_Q - first, 0), W_IN_BLOCK)
    pieces = [(0, first, plain)] if plain else []
    return pieces + [(r, _padded_row(first + r), HD) for r in range(plain, W_IN_BLOCK, HD)]


def _inproj_fwd(x, g1, gi, own_i, chip, tm):
    t = x.shape[0]

    def body(chip_ref, x_ref, g_ref, gi_ref, own_ref, p_ref, h_ref, w_ref, sem):
        @pl.when(pl.program_id(0) == 0)
        def _():
            for k in range(N_CHIPS):
                for src, dst, rows in _w_in_pieces(k):
                    @pl.when(chip_ref[0] == k)
                    def _():
                        pltpu.make_async_copy(own_ref.at[pl.ds(src, rows)], w_ref.at[pl.ds(dst, rows)], sem).start()

                    @pl.when(chip_ref[0] != k)
                    def _():
                        pltpu.make_async_copy(gi_ref.at[k, pl.ds(src, rows)], w_ref.at[pl.ds(dst, rows)], sem).start()
            for slot in range(NQ + 2 * NKV):
                w_ref[O_Q + slot * HP + HD:O_Q + (slot + 1) * HP, :] = jnp.zeros((HP - HD, D), BF16)
            landed = w_ref.at[pl.ds(0, W_IN_ROWS)]
            pltpu.make_async_copy(landed, landed, sem).wait()

        _, xh = _rms_fwd(x_ref[...], 1.0 / D)
        h = (xh * g_ref[...]).astype(BF16)
        h_ref[...] = h
        p_ref[...] = _nt(h, w_ref[...])

    const = lambda shape: pl.BlockSpec(shape, lambda i, c: (0,) * len(shape))
    return pl.pallas_call(
        body, name="inproj_fwd",
        grid_spec=pltpu.PrefetchScalarGridSpec(
            num_scalar_prefetch=1, grid=(t // tm,),
            in_specs=[pl.BlockSpec((tm, D), lambda i, c: (i, 0)), const((1, D)), ANY, ANY],
            out_specs=[pl.BlockSpec((tm, NP), lambda i, c: (i, 0)), pl.BlockSpec((tm, D), lambda i, c: (i, 0)),
                       const((NP, D))],
            scratch_shapes=[pltpu.SemaphoreType.DMA(())]),
        out_shape=[jax.ShapeDtypeStruct((t, NP), F32), jax.ShapeDtypeStruct((t, D), BF16),
                   jax.ShapeDtypeStruct((NP, D), BF16)],
        compiler_params=_cparams(("arbitrary",)),
    )(chip, x, g1, gi, own_i)


def _band_mask():
    r_io = lax.broadcasted_iota(jnp.int32, (BLK, 2 * BLK), 0)
    c_io = lax.broadcasted_iota(jnp.int32, (BLK, 2 * BLK), 1)
    return (c_io > r_io) & (c_io <= r_io + BLK), c_io


def _conv_taps(uf, n):
    u1 = pltpu.roll(uf, 1, 0)[8:8 + n]
    u2 = pltpu.roll(uf, 2, 0)[8:8 + n]
    return u1, u2


def _attn_probs(qs, kband, sink, valid):
    s = jnp.where(valid, _nt(qs, kband), NEG)
    m = jnp.maximum(jnp.max(s, axis=-1, keepdims=True), sink)
    p = jnp.exp(s - m)
    es = jnp.exp(sink - m)
    inv = 1.0 / (jnp.sum(p, axis=-1, keepdims=True) + es)
    return p * inv, es * inv


def _norm_keys(kraw, gk):
    out = []
    for h in range(NKV):
        kh = kraw[:, h * HP:(h + 1) * HP]
        rk, khat = _rms_fwd(kh, 1.0 / HD)
        out.append((khat, rk, (khat * gk).astype(BF16)))
    return out


def _mixer_fwd(proj, x, cw, gq, gk, sinks, gco, gao, wo, tq):
    t = proj.shape[0]
    nb = tq // BLK
    r8 = tq // 8

    def body(p_ref, cgp_ref, hcp_ref, kvp_ref, x_ref, cw_ref, gq_ref, gk_ref, sk_ref, gco_ref, gao_ref,
             wo_ref, xm_ref, mix_ref, ao_ref, aop_ref):
        i = pl.program_id(0)
        cg = p_ref[:, O_CG:O_CG + CC]
        hc = p_ref[:, O_HC:O_HC + CC]
        u = cg * hc
        up = jnp.where(i > 0, cgp_ref[...] * hcp_ref[...], 0.0)
        u1, u2 = _conv_taps(jnp.concatenate([up, u], axis=0), tq)
        y = cw_ref[0:1, :] * u2 + cw_ref[1:2, :] * u1 + cw_ref[2:3, :] * u
        co = p_ref[:, O_BG:O_BG + CC] * y
        _, coh = _rms_fwd(co, 1.0 / CC)
        cn = coh * gco_ref[...]
        kraw = jnp.concatenate([kvp_ref[:, 0:NKV * HP], p_ref[:, O_K:O_K + NKV * HP]], axis=0)
        vraw = jnp.concatenate([kvp_ref[:, NKV * HP:], p_ref[:, O_V:O_V + NKV * HP]], axis=0)
        keys = _norm_keys(kraw, gk_ref[...])
        vb = [vraw[:, h * HP:(h + 1) * HP].astype(BF16) for h in range(NKV)]
        base_valid, c_io = _band_mask()
        gqs = gq_ref[...] * SCALE
        for b in range(nb):
            lo = jnp.where(i * nb + b == 0, BLK, 0)
            valid = base_valid & (c_io >= lo)
            for g in range(NQ):
                h = g // GRP
                qg = p_ref[b * BLK:(b + 1) * BLK, O_Q + g * HP:O_Q + (g + 1) * HP]
                _, qh = _rms_fwd(qg, 1.0 / HD)
                qs = (qh * gqs).astype(BF16)
                pr, _ = _attn_probs(qs, keys[h][2][b * BLK:b * BLK + 2 * BLK], sk_ref[0, g], valid)
                aop_ref[b * BLK:(b + 1) * BLK, g * HP:(g + 1) * HP] = jnp.dot(
                    pr.astype(BF16), vb[h][b * BLK:b * BLK + 2 * BLK], preferred_element_type=F32)
        for j in range(NQ // 2):
            ao_ref[:, j * HP:(j + 1) * HP] = (aop_ref[:, 2 * j * HP:(2 * j + 1) * HP]
                                              + pltpu.roll(aop_ref[:, (2 * j + 1) * HP:(2 * j + 2) * HP], HD, 1))
        _, aoh = _rms_fwd(ao_ref[...], 1.0 / (NQ * HD))
        an = aoh * gao_ref[...]
        mix = jnp.concatenate([cn, an], axis=1).astype(BF16)
        mix_ref[...] = mix
        xm_ref[...] = x_ref[...] + jnp.dot(mix, wo_ref[...], preferred_element_type=F32)

    prev8 = lambda col: pl.BlockSpec((8, CC), lambda i: (jnp.maximum(i * r8 - 1, 0), col))
    return pl.pallas_call(
        body, name="mixer_fwd", grid=(t // tq,),
        in_specs=[
            pl.BlockSpec((tq, NP), lambda i: (i, 0)),
            prev8(O_CG // CC), prev8(O_HC // CC),
            pl.BlockSpec((BLK, 2 * NKV * HP), lambda i: (jnp.maximum(i * nb - 1, 0), O_K // (2 * NKV * HP))),
            pl.BlockSpec((tq, D), lambda i: (i, 0)),
            _const_spec((8, CC)), _const_spec((1, HP)), _const_spec((1, HP)),
            pl.BlockSpec(memory_space=pltpu.SMEM),
            _const_spec((1, CC)), _const_spec((1, NQ * HD)), _const_spec((MIXW, D)),
        ],
        out_specs=[pl.BlockSpec((tq, D), lambda i: (i, 0)), pl.BlockSpec((tq, MIXW), lambda i: (i, 0)),
                   pl.BlockSpec((tq, NQ * HD), lambda i: (i, 0))],
        out_shape=[jax.ShapeDtypeStruct((t, D), F32), jax.ShapeDtypeStruct((t, MIXW), BF16),
                   jax.ShapeDtypeStruct((t, NQ * HD), F32)],
        scratch_shapes=[pltpu.VMEM((tq, NQ * HP), F32)],
        compiler_params=_cparams(("parallel",)),
    )(proj, proj, proj, proj, x, cw, gq, gk, sinks, gco, gao, wo)


def _ffn_weight_specs():
    return [pl.BlockSpec((N_CHIPS, FFB, D), lambda i, j=j: (0, j, 0), pipeline_mode=pl.Buffered(1))
            for j in range(3)]


def _ffn_fwd(xm, g2, gf, tm, tgt=None):
    t = xm.shape[0]
    last = tgt is not None

    def body(x_ref, g_ref, wg_ref, wu_ref, wd_ref, *rest):
        t_ref, rest = (rest[0], rest[1:]) if last else (None, rest)
        l_ref, rest = (rest[0], rest[1:]) if last else (None, rest)
        xo_ref, a_ref, b_ref, h2_ref = rest
        xv = x_ref[...]
        _, xh = _rms_fwd(xv, 1.0 / D)
        h2 = (xh * g_ref[...]).astype(BF16)
        h2_ref[...] = h2
        acc = xv
        for k in range(N_CHIPS):
            a = _nt(h2, wg_ref[k])
            b = _nt(h2, wu_ref[k])
            a_ref[k] = a.astype(BF16)
            b_ref[k] = b.astype(BF16)
            hm = (a * jax.nn.sigmoid(a) * b).astype(BF16)
            acc = acc + jnp.dot(hm, wd_ref[k], preferred_element_type=F32)
        if last:
            @pl.when(pl.program_id(0) == 0)
            def _():
                l_ref[...] = jnp.zeros_like(l_ref)

            e = acc - t_ref[...]
            xo_ref[...] = e * (1.0 / D)
            l_ref[...] += jnp.sum(jnp.sum(e * e, axis=-1, keepdims=True), axis=0, keepdims=True) * (0.5 / D)
        else:
            xo_ref[...] = acc

    row = lambda w: pl.BlockSpec((tm, w), lambda i: (i, 0))
    blk = pl.BlockSpec((N_CHIPS, tm, FFB), lambda i: (0, i, 0))
    bsd = jax.ShapeDtypeStruct((N_CHIPS, t, FFB), BF16)
    return pl.pallas_call(
        body, name="ffn_fwd_loss" if last else "ffn_fwd", grid=(t // tm,),
        in_specs=[row(D), _const_spec((1, D))] + _ffn_weight_specs() + ([row(D)] if last else []),
        out_specs=([pl.BlockSpec((8, 128), lambda i: (0, 0))] if last else []) + [row(D), blk, blk, row(D)],
        out_shape=([jax.ShapeDtypeStruct((8, 128), F32)] if last else [])
        + [jax.ShapeDtypeStruct((t, D), F32), bsd, bsd, jax.ShapeDtypeStruct((t, D), BF16)],
        compiler_params=_cparams(("arbitrary" if last else "parallel",)),
    )(*((xm, g2, gf, gf, gf) + ((tgt,) if last else ())))


def _ffn_bwd(dy, xm, g2, a, b, gf, tm):
    t = dy.shape[0]

    def body(dy_ref, x_ref, g_ref, a_ref, b_ref, wg_ref, wu_ref, wd_ref, dx_ref, da_ref, db_ref, hm_ref, dg_ref):
        @pl.when(pl.program_id(0) == 0)
        def _():
            dg_ref[...] = jnp.zeros_like(dg_ref)

        dyv = dy_ref[...]
        dyb = dyv.astype(BF16)
        dh2 = jnp.zeros_like(dyv)
        for k in range(N_CHIPS):
            dhm = _nt(dyb, wd_ref[k])
            av = a_ref[k].astype(F32)
            bv = b_ref[k].astype(F32)
            sig = jax.nn.sigmoid(av)
            sil = av * sig
            hm_ref[k] = (sil * bv).astype(BF16)
            da = (dhm * bv * (sig * (1.0 + av * (1.0 - sig)))).astype(BF16)
            db = (dhm * sil).astype(BF16)
            da_ref[k] = da
            db_ref[k] = db
            dh2 = (dh2 + jnp.dot(da, wg_ref[k], preferred_element_type=F32)
                   + jnp.dot(db, wu_ref[k], preferred_element_type=F32))
        r, xh = _rms_fwd(x_ref[...], 1.0 / D)
        dg_ref[...] += jnp.sum(dh2 * xh, axis=0, keepdims=True)
        dx_ref[...] = dyv + _rms_bwd(dh2, g_ref[...], xh, r, 1.0 / D)

    row = lambda w: pl.BlockSpec((tm, w), lambda i: (i, 0))
    blk = pl.BlockSpec((N_CHIPS, tm, FFB), lambda i: (0, i, 0))
    bsd = jax.ShapeDtypeStruct((N_CHIPS, t, FFB), BF16)
    return pl.pallas_call(
        body, name="ffn_bwd", grid=(t // tm,),
        in_specs=[row(D), row(D), _const_spec((1, D)), blk, blk] + _ffn_weight_specs(),
        out_specs=[row(D), blk, blk, blk, pl.BlockSpec((1, D), lambda i: (0, 0))],
        out_shape=[jax.ShapeDtypeStruct((t, D), F32), bsd, bsd, bsd, jax.ShapeDtypeStruct((1, D), F32)],
        compiler_params=_cparams(("arbitrary",)),
    )(dy, xm, g2, a, b, gf, gf, gf)


def _wgrad_blocks(a, b, tt, name):
    _, t, rows = a.shape
    cols = b.shape[1]
    nsteps = t // tt

    def body(a_ref, b_ref, o_ref, acc_ref):
        s = pl.program_id(0)

        @pl.when(s == 0)
        def _():
            acc_ref[...] = jnp.zeros_like(acc_ref)

        bv = b_ref[...].astype(BF16)
        for k in range(N_CHIPS):
            acc_ref[k] += _tn(a_ref[k], bv)

        @pl.when(s == nsteps - 1)
        def _():
            o_ref[...] = acc_ref[...].astype(BF16)

    return pl.pallas_call(
        body, name=name, grid=(nsteps,),
        in_specs=[pl.BlockSpec((N_CHIPS, tt, rows), lambda s: (0, s, 0)), pl.BlockSpec((tt, cols), lambda s: (s, 0))],
        out_specs=pl.BlockSpec((N_CHIPS, rows, cols), lambda s: (0, 0, 0)),
        out_shape=jax.ShapeDtypeStruct((N_CHIPS, rows, cols), BF16),
        scratch_shapes=[pltpu.VMEM((N_CHIPS, rows, cols), F32)],
        compiler_params=_cparams(("arbitrary",)),
    )(a, b)


def _head_rows(first, n_heads):
    return [(first + g * HD, first + g * HP, HD) for g in range(n_heads)]


def _wgrad(a, b, tt, name, pieces=None):
    t, k = a.shape
    n = b.shape[1]
    nsteps = t // tt
    pieces = pieces or [(0, 0, k)]
    rows = sum(p[2] for p in pieces)

    def body(a_ref, b_ref, o_ref, acc_ref):
        s = pl.program_id(0)

        @pl.when(s == 0)
        def _():
            acc_ref[...] = jnp.zeros_like(acc_ref)

        acc_ref[...] += _tn(a_ref[...].astype(BF16), b_ref[...].astype(BF16))

        @pl.when(s == nsteps - 1)
        def _():
            for dst, src, size in pieces:
                o_ref[dst:dst + size, :] = acc_ref[src:src + size, :].astype(BF16)

    return pl.pallas_call(
        body, name=name, grid=(nsteps,),
        in_specs=[pl.BlockSpec((tt, k), lambda s: (s, 0)), pl.BlockSpec((tt, n), lambda s: (s, 0))],
        out_specs=pl.BlockSpec((rows, n), lambda s: (0, 0)),
        out_shape=jax.ShapeDtypeStruct((rows, n), BF16),
        scratch_shapes=[pltpu.VMEM((k, n), F32)],
        compiler_params=_cparams(("arbitrary",)),
    )(a, b)


def _mixer_bwd(dxm, proj, ao, cw, gq, gk, sinks, gco, gao, wo, tq):
    t = proj.shape[0]
    nb = tq // BLK
    r8 = tq // 8
    nt = t // tq
    te = tq + 8
    kvw = 2 * NKV * HP

    def body(dx_ref, dxn_ref, p_ref, cgp_ref, hcp_ref, bgn_ref, cgn_ref, hcn_ref, kvp_ref, ao_ref, cw_ref, gq_ref,
             gk_ref, sk_ref, gco_ref, gao_ref, wo_ref,
             dpm_ref, dkvm_ref, dkvh_ref, dcw_ref, dgq_ref, dgk_ref, dsk_ref, dgco_ref, dgao_ref, acc_ref):
        i = pl.program_id(0)

        @pl.when(i == 0)
        def _():
            for r in (dcw_ref, dgq_ref, dgk_ref, dsk_ref, dgco_ref, dgao_ref):
                r[...] = jnp.zeros_like(r)

        acc_ref[...] = jnp.zeros_like(acc_ref)
        live_rows = jnp.where(i < nt - 1, te, tq)
        dxb = dx_ref[...].astype(BF16)
        dxe = jnp.concatenate([dxb, dxn_ref[...].astype(BF16)], axis=0)
        dcn = _nt(dxe, wo_ref[0:CC, :])
        bg = jnp.concatenate([p_ref[:, O_BG:O_BG + CC], bgn_ref[...]], axis=0)
        cg = jnp.concatenate([p_ref[:, O_CG:O_CG + CC], cgn_ref[...]], axis=0)
        hc = jnp.concatenate([p_ref[:, O_HC:O_HC + CC], hcn_ref[...]], axis=0)
        u = cg * hc
        up = jnp.where(i > 0, cgp_ref[...] * hcp_ref[...], 0.0)
        u1, u2 = _conv_taps(jnp.concatenate([up, u], axis=0), te)
        w0, w1, w2 = cw_ref[0:1, :], cw_ref[1:2, :], cw_ref[2:3, :]
        y = w0 * u2 + w1 * u1 + w2 * u
        co = bg * y
        rc, coh = _rms_fwd(co, 1.0 / CC)
        dco = _rms_bwd(dcn, gco_ref[...], coh, rc, 1.0 / CC)
        row_io = lax.broadcasted_iota(jnp.int32, (te, 1), 0)
        own = row_io < tq
        dgco_ref[...] += jnp.sum(jnp.where(own, dcn * coh, 0.0), axis=0, keepdims=True)
        dyc = jnp.where(row_io < live_rows, dco * bg, 0.0)
        dyo = jnp.where(own, dyc, 0.0)
        dcw_ref[0:1, :] += jnp.sum(dyo * u2, axis=0, keepdims=True)
        dcw_ref[1:2, :] += jnp.sum(dyo * u1, axis=0, keepdims=True)
        dcw_ref[2:3, :] += jnp.sum(dyo * u, axis=0, keepdims=True)
        dy1 = pltpu.roll(dyc, te - 1, 0)[0:tq]
        dy2 = pltpu.roll(dyc, te - 2, 0)[0:tq]
        du = w2 * dyc[0:tq] + w1 * dy1 + w0 * dy2
        dpm_ref[:, O_BG:O_BG + CC] = (dco[0:tq] * y[0:tq]).astype(BF16)
        dpm_ref[:, O_CG:O_CG + CC] = (du * hc[0:tq]).astype(BF16)
        dpm_ref[:, O_HC:O_HC + CC] = (du * cg[0:tq]).astype(BF16)
        kraw = jnp.concatenate([kvp_ref[:, 0:NKV * HP], p_ref[:, O_K:O_K + NKV * HP]], axis=0)
        vraw = jnp.concatenate([kvp_ref[:, NKV * HP:], p_ref[:, O_V:O_V + NKV * HP]], axis=0)
        gqv, gkv = gq_ref[...], gk_ref[...]
        keys = _norm_keys(kraw, gkv)
        vb = [vraw[:, h * HP:(h + 1) * HP].astype(BF16) for h in range(NKV)]
        base_valid, c_io = _band_mask()
        lane = lax.broadcasted_iota(jnp.int32, (1, HP), 1)
        dgq, dgk, dsk = (jnp.zeros((1, HP), F32) for _ in range(3))
        dgao = jnp.zeros((1, NQ * HD), F32)
        for b in range(nb):
            lo = jnp.where(i * nb + b == 0, BLK, 0)
            valid = base_valid & (c_io >= lo)
            band = slice(b * BLK, b * BLK + 2 * BLK)
            blk = slice(b * BLK, (b + 1) * BLK)
            ra, aoh = _rms_fwd(ao_ref[blk, :], 1.0 / (NQ * HD))
            danb = _nt(dxb[blk], wo_ref[CC:MIXW, :])
            dgao = dgao + jnp.sum(danb * aoh, axis=0, keepdims=True)
            dao = _rms_bwd(danb, gao_ref[...], aoh, ra, 1.0 / (NQ * HD))
            dos = [dao[:, g // 2 * HP:(g // 2 + 1) * HP] for g in range(NQ)]
            dos = [(d if g % 2 == 0 else pltpu.roll(d, HD, 1)).astype(BF16) for g, d in enumerate(dos)]
            fwd = []
            for g in range(NQ):
                rq, qh = _rms_fwd(p_ref[blk, O_Q + g * HP:O_Q + (g + 1) * HP], 1.0 / HD)
                qs = (qh * (gqv * SCALE)).astype(BF16)
                fwd.append((rq, qh, qs) + _attn_probs(qs, keys[g // GRP][2][band], sk_ref[0, g], valid))
            dqs = []
            for h in range(NKV):
                khat, rk, kn = [a[band] for a in keys[h]]
                dss, prbs, qns, dobs = [], [], [], []
                for g in range(h * GRP, (h + 1) * GRP):
                    rq, qh, qs, pr, ps = fwd[g]
                    dob = dos[g]
                    dp = _nt(dob, vb[h][band])
                    delta = jnp.sum(pr * dp, axis=-1, keepdims=True)
                    dsb = (pr * (dp - delta)).astype(BF16)
                    dsk = dsk + jnp.where(lane == g, -jnp.sum(ps * delta, axis=0, keepdims=True), 0.0)
                    dqn = jnp.dot(dsb, kn, preferred_element_type=F32) * SCALE
                    dgq = dgq + jnp.sum(dqn * qh, axis=0, keepdims=True)
                    dqs.append(_rms_bwd(dqn, gqv, qh, rq, 1.0 / HD).astype(BF16))
                    dss.append(dsb)
                    prbs.append(pr.astype(BF16))
                    qns.append(qs)
                    dobs.append(dob)
                dkn = _tn(jnp.concatenate(dss, axis=0), jnp.concatenate(qns, axis=0))
                dv = _tn(jnp.concatenate(prbs, axis=0), jnp.concatenate(dobs, axis=0))
                dgk = dgk + jnp.sum(dkn * khat, axis=0, keepdims=True)
                acc_ref[band, h * HP:(h + 1) * HP] += _rms_bwd(dkn, gkv, khat, rk, 1.0 / HD)
                acc_ref[band, (NKV + h) * HP:(NKV + h + 1) * HP] += dv
            dpm_ref[blk, O_Q:O_K] = jnp.concatenate(dqs, axis=1)
        dgq_ref[...] += dgq
        dgk_ref[...] += dgk
        dsk_ref[...] += dsk
        dgao_ref[...] += dgao
        dkvh_ref[...] = acc_ref[0:BLK, :]
        dkvm_ref[...] = acc_ref[BLK:, :]

    prev8 = lambda col: pl.BlockSpec((8, CC), lambda i: (jnp.maximum(i * r8 - 1, 0), col))
    next8 = lambda col: pl.BlockSpec((8, CC), lambda i: (jnp.minimum((i + 1) * r8, t // 8 - 1), col))
    small = lambda n: pl.BlockSpec((1, n), lambda i: (0, 0))
    return pl.pallas_call(
        body, name="mixer_bwd", grid=(nt,),
        in_specs=[
            pl.BlockSpec((tq, D), lambda i: (i, 0)),
            pl.BlockSpec((8, D), lambda i: (jnp.minimum((i + 1) * r8, t // 8 - 1), 0)),
            pl.BlockSpec((tq, NP), lambda i: (i, 0)),
            prev8(O_CG // CC), prev8(O_HC // CC),
            next8(O_BG // CC), next8(O_CG // CC), next8(O_HC // CC),
            pl.BlockSpec((BLK, kvw), lambda i: (jnp.maximum(i * nb - 1, 0), O_K // kvw)),
            pl.BlockSpec((tq, NQ * HD), lambda i: (i, 0)),
            _const_spec((8, CC)), _const_spec((1, HP)), _const_spec((1, HP)),
            pl.BlockSpec(memory_space=pltpu.SMEM),
            _const_spec((1, CC)), _const_spec((1, NQ * HD)), _const_spec((MIXW, D)),
        ],
        out_specs=[
            pl.BlockSpec((tq, NMAIN), lambda i: (i, 0)),
            pl.BlockSpec((tq, kvw), lambda i: (i, 0)),
            pl.BlockSpec((BLK, kvw), lambda i: (i, 0)),
            pl.BlockSpec((8, CC), lambda i: (0, 0)), small(HP), small(HP), small(HP), small(CC), small(NQ * HD),
        ],
        out_shape=[
            jax.ShapeDtypeStruct((t, NMAIN), BF16), jax.ShapeDtypeStruct((t, kvw), F32),
            jax.ShapeDtypeStruct((nt * BLK, kvw), F32),
            jax.ShapeDtypeStruct((8, CC), F32), jax.ShapeDtypeStruct((1, HP), F32), jax.ShapeDtypeStruct((1, HP), F32),
            jax.ShapeDtypeStruct((1, HP), F32), jax.ShapeDtypeStruct((1, CC), F32),
            jax.ShapeDtypeStruct((1, NQ * HD), F32),
        ],
        scratch_shapes=[pltpu.VMEM((tq + BLK, kvw), F32)],
        compiler_params=_cparams(("arbitrary",)),
    )(dxm, dxm, proj, proj, proj, proj, proj, proj, proj, ao, cw, gq, gk, sinks, gco, gao, wo)


def _inproj_bwd(dpm, dkvm, dkvh, wpt, x, g1, dxm, tm):
    t = x.shape[0]
    kvw = 2 * NKV * HP
    nt = t // tm

    def body(dp_ref, dk_ref, dh_ref, w_ref, x_ref, g_ref, dxm_ref, dx_ref, dg_ref, dkv_ref):
        i = pl.program_id(0)

        @pl.when(i == 0)
        def _():
            dg_ref[...] = jnp.zeros_like(dg_ref)

        halo = jnp.where(i < nt - 1, dh_ref[...], 0.0)
        dkv_ref[0:tm - BLK, :] = dk_ref[0:tm - BLK, :].astype(BF16)
        dkv_ref[tm - BLK:tm, :] = (dk_ref[tm - BLK:tm, :] + halo).astype(BF16)
        dh = (jnp.dot(dp_ref[...], w_ref[0:NMAIN, :], preferred_element_type=F32)
              + jnp.dot(dkv_ref[...], w_ref[NMAIN:NP, :], preferred_element_type=F32))
        r, xh = _rms_fwd(x_ref[...], 1.0 / D)
        dg_ref[...] += jnp.sum(dh * xh, axis=0, keepdims=True)
        dx_ref[...] = dxm_ref[...] + _rms_bwd(dh, g_ref[...], xh, r, 1.0 / D)

    row = lambda w: pl.BlockSpec((tm, w), lambda i: (i, 0))
    return pl.pallas_call(
        body, name="inproj_bwd", grid=(nt,),
        in_specs=[row(NMAIN), row(kvw), pl.BlockSpec((BLK, kvw), lambda i: (jnp.minimum(i + 1, nt - 1), 0)),
                  _const_spec((NP, D)), row(D), _const_spec((1, D)), row(D)],
        out_specs=[row(D), pl.BlockSpec((1, D), lambda i: (0, 0)), row(kvw)],
        out_shape=[jax.ShapeDtypeStruct((t, D), F32), jax.ShapeDtypeStruct((1, D), F32),
                   jax.ShapeDtypeStruct((t, kvw), BF16)],
        compiler_params=_cparams(("arbitrary",)),
    )(dpm, dkvm, dkvh, wpt, x, g1, dxm)


def _rows_tile(rows, cap=512):
    for cand in range(min(rows, cap) // 16 * 16, 0, -16):
        if rows % cand == 0:
            return cand
    return rows


def _presum_halves(gs, theirs, core):
    n = len(gs)

    def body(c_ref, *refs):
        for g_ref, t_ref, o_ref in zip(refs[:n], refs[n:2 * n], refs[2 * n:]):
            o_ref[...] = (g_ref[...].astype(F32) + t_ref[...].astype(F32)).astype(BF16)

    half = lambda ta: pl.BlockSpec((None,) + ta.shape[1:], lambda k, c_ref: (k, 0, 0))
    own = lambda ta: pl.BlockSpec((None,) + ta.shape[1:], lambda k, c_ref: (k, c_ref[0], 0))
    return pl.pallas_call(
        body, name="presum",
        grid_spec=pltpu.PrefetchScalarGridSpec(
            num_scalar_prefetch=1, grid=(N_CHIPS,),
            in_specs=[own(ta) for ta in theirs] + [half(ta) for ta in theirs],
            out_specs=[half(ta) for ta in theirs]),
        out_shape=[jax.ShapeDtypeStruct(ta.shape, BF16) for ta in theirs],
        compiler_params=_cparams(("parallel",)),
    )(core, *gs, *theirs)


def _sum_chips(cs):
    n = len(cs)
    steps = 2

    def body(*refs):
        for c_ref, o_ref in zip(refs[:n], refs[n:]):
            acc = c_ref[0].astype(F32)
            for j in range(1, N_CHIPS):
                acc = acc + c_ref[j].astype(F32)
            o_ref[...] = acc

    return pl.pallas_call(
        body, name="chipsum", grid=(steps,),
        in_specs=[pl.BlockSpec((N_CHIPS, c.shape[1] // steps, c.shape[2]), lambda i: (0, i, 0)) for c in cs],
        out_specs=[pl.BlockSpec((c.shape[1] // steps, c.shape[2]), lambda i: (i, 0)) for c in cs],
        out_shape=[jax.ShapeDtypeStruct(c.shape[1:], F32) for c in cs],
        compiler_params=_cparams(("parallel",)),
    )(*cs)


def _adamw(w, g, m, v, name):
    rows, cols = w.shape
    tr = _rows_tile(rows, 256)
    c1 = 1.0 - ADAM_B1 ** ADAM_STEP
    c2 = 1.0 - ADAM_B2 ** ADAM_STEP

    def body(w_ref, g_ref, m_ref, v_ref, d_ref, mo_ref, vo_ref):
        gv = g_ref[...]
        mn = ADAM_B1 * m_ref[...] + (1.0 - ADAM_B1) * gv
        vn = ADAM_B2 * v_ref[...] + (1.0 - ADAM_B2) * (gv * gv)
        mo_ref[...] = mn
        vo_ref[...] = vn
        d_ref[...] = -ADAM_LR * ((mn / c1) / (jnp.sqrt(vn / c2) + ADAM_EPS) + ADAM_WD * w_ref[...])

    spec = pl.BlockSpec((tr, cols), lambda i: (i, 0))
    sds = jax.ShapeDtypeStruct((rows, cols), F32)
    return pl.pallas_call(
        body, name=name, grid=(rows // tr,), in_specs=[spec] * 4, out_specs=[spec] * 3, out_shape=[sds] * 3,
        compiler_params=_cparams(("parallel",)),
    )(w, g, m, v)


def _place():
    x, y, c = lax.axis_index("x"), lax.axis_index("y"), lax.axis_index("c")
    chips = [(1 - x, y), (x, 1 - y), (1 - x, 1 - y)]
    return x, y, c, chips


ANY = pl.BlockSpec(memory_space=pl.ANY)
DMA_ROWS = 64


def _pieces(shape):
    rows = shape[-2]
    step = DMA_ROWS if rows % DMA_ROWS == 0 else rows
    lead = [()]
    for n in shape[:-2]:
        lead = [i + (k,) for i in lead for k in range(n)]
    return [i + (pl.ds(r0, step),) for i in lead for r0 in range(0, rows, step)]


def _start_pieces(make, src, dst):
    for idx in _pieces(src.shape):
        make(src.at[idx], dst.at[idx]).start()


def _gather_layer(blocks, layer):
    nw = len(blocks)

    def body(*refs):
        _gather_body(refs[:nw], refs[nw:2 * nw], refs[2 * nw:], layer, _start_pieces)

    return pl.pallas_call(
        body, name=f"gather_layer{layer}", in_specs=[ANY] * nw, out_specs=[ANY] * nw,
        out_shape=[jax.ShapeDtypeStruct((N_CHIPS,) + b.shape, b.dtype) for b in blocks],
        scratch_shapes=[pltpu.SemaphoreType.DMA((3, nw))] * 4,
        compiler_params=_cparams(has_side_effects=True),
    )(*blocks)


def _gather_body(srcs, outs, sems, layer, start):
    nw = len(srcs)
    ssem, rsem, fssem, frsem = sems
    x, y, c, chips = _place()
    kme = 2 * x + y

    def plane(j, w, to):
        return lambda s, d: pltpu.make_async_remote_copy(
            src_ref=s, dst_ref=d, send_sem=ssem.at[j, w], recv_sem=rsem.at[j, w], device_id=to,
            device_id_type=MESH)

    def passed(j, w):
        return lambda s, d: pltpu.make_async_remote_copy(
            src_ref=s, dst_ref=d, send_sem=fssem.at[j, w], recv_sem=frsem.at[j, w],
            device_id=(x, y, 1 - c), device_id_type=MESH)

    @pl.when(c == layer)
    def _():
        for j, (px, py) in enumerate(chips):
            for w in range(nw):
                start(plane(j, w, (px, py, c)), srcs[w], outs[w].at[kme])
        for j, (px, py) in enumerate(chips):
            for w in range(nw):
                got = outs[w].at[2 * px + py]
                plane(j, w, (px, py, c))(got, got).wait_recv()
                start(passed(j, w), got, got)
        for j, (px, py) in enumerate(chips):
            for w in range(nw):
                got = outs[w].at[2 * px + py]
                plane(j, w, (px, py, c))(got, got).wait_send()
                passed(j, w)(got, got).wait_send()

    @pl.when(c != layer)
    def _():
        for j, (px, py) in enumerate(chips):
            for w in range(nw):
                got = outs[w].at[2 * px + py]
                passed(j, w)(got, got).wait_recv()


def _handshake(peers):
    barrier = pltpu.get_barrier_semaphore()
    for peer in peers:
        pl.semaphore_signal(barrier, inc=1, device_id=peer, device_id_type=MESH)
    pl.semaphore_wait(barrier, len(peers))


def _handshake_all():
    x, y, c, _ = _place()
    _handshake([(x ^ (r >> 2), y ^ ((r >> 1) & 1), c ^ (r & 1)) for r in range(1, 8)])


def _gather_layer_async(blocks, layer, name, collective_id):
    hbm = pltpu.MemorySpace.HBM
    srcs = [jax.new_ref(b, memory_space=hbm) for b in blocks]
    outs = [jax.empty_ref(jax.ShapeDtypeStruct((N_CHIPS,) + b.shape, b.dtype), memory_space=hbm) for b in blocks]

    @pl.kernel(mesh=plsc.ScalarSubcoreMesh(axis_name="seq", num_cores=1), name=name,
               scratch_types=[pltpu.SemaphoreType.DMA((3, len(blocks)))] * 4,
               compiler_params=pltpu.CompilerParams(collective_id=collective_id))
    def launch(*sems):
        _handshake_all()
        _gather_body(srcs, outs, sems, layer, lambda make, s, d: make(s, d).start())

    launch()
    return [o[...] for o in outs]


def _swap_siblings(arrs, halves, name, collective_id=None):
    nw = len(arrs)
    out_sds = [jax.ShapeDtypeStruct((a.shape[0], a.shape[1] // 2, a.shape[2]) if halves else a.shape, a.dtype)
               for a in arrs]

    def exchange(srcs, outs, ssem, rsem, start):
        x, y, c, _ = _place()

        def give(w):
            return lambda s, d: pltpu.make_async_remote_copy(
                src_ref=s, dst_ref=d, send_sem=ssem.at[w], recv_sem=rsem.at[w], device_id=(x, y, 1 - c),
                device_id_type=MESH)

        for w in range(nw):
            hr = outs[w].shape[1]
            start(give(w), srcs[w].at[:, pl.ds((1 - c) * hr, hr)] if halves else srcs[w], outs[w])
        for w in range(nw):
            give(w)(outs[w], outs[w]).wait()

    if collective_id is None:
        def body(*refs):
            exchange(refs[:nw], refs[nw:2 * nw], *refs[2 * nw:], _start_pieces)

        return pl.pallas_call(
            body, name=name, in_specs=[ANY] * nw, out_specs=[ANY] * nw, out_shape=out_sds,
            scratch_shapes=[pltpu.SemaphoreType.DMA((nw,))] * 2,
            compiler_params=_cparams(has_side_effects=True),
        )(*arrs)

    hbm = pltpu.MemorySpace.HBM
    srcs = [jax.new_ref(a, memory_space=hbm) for a in arrs]
    outs = [jax.empty_ref(sds, memory_space=hbm) for sds in out_sds]

    @pl.kernel(mesh=plsc.ScalarSubcoreMesh(axis_name="seq", num_cores=1), name=name,
               scratch_types=[pltpu.SemaphoreType.DMA((nw,))] * 2,
               compiler_params=pltpu.CompilerParams(collective_id=collective_id))
    def launch(ssem, rsem):
        x, y, c, _ = _place()
        _handshake([(x, y, 1 - c)])
        exchange(srcs, outs, ssem, rsem, lambda make, s, d: make(s, d).start())

    launch()
    return [o[...] for o in outs]


def _scatter_chips(ps):
    nw = len(ps)

    def body(*refs):
        _scatter_body(refs[:nw], refs[nw:2 * nw], refs[2 * nw:], _start_pieces)

    return pl.pallas_call(
        body, name="scatter_chips", in_specs=[ANY] * nw, out_specs=[ANY] * nw,
        out_shape=[jax.ShapeDtypeStruct(p.shape, p.dtype) for p in ps],
        scratch_shapes=[pltpu.SemaphoreType.DMA((3, nw)), pltpu.SemaphoreType.DMA((3, nw))],
        compiler_params=_cparams(has_side_effects=True),
    )(*ps)


def _scatter_body(srcs, outs, sems, start):
    nw = len(srcs)
    ssem, rsem = sems
    x, y, c, chips = _place()
    kme = 2 * x + y

    def give(j, w, to):
        return lambda s, d: pltpu.make_async_remote_copy(
            src_ref=s, dst_ref=d, send_sem=ssem.at[j, w], recv_sem=rsem.at[j, w], device_id=to,
            device_id_type=MESH)

    for j, (px, py) in enumerate(chips):
        for w in range(nw):
            start(give(j, w, (px, py, c)), srcs[w].at[2 * px + py], outs[w].at[kme])
    for j, (px, py) in enumerate(chips):
        for w in range(nw):
            got = outs[w].at[2 * px + py]
            give(j, w, (px, py, c))(got, got).wait_recv()
    for j, (px, py) in enumerate(chips):
        for w in range(nw):
            sent = srcs[w].at[2 * px + py]
            give(j, w, (px, py, c))(sent, sent).wait_send()


def _scatter_chips_async(ps, name, collective_id):
    hbm = pltpu.MemorySpace.HBM
    srcs = [jax.new_ref(p, memory_space=hbm) for p in ps]
    outs = [jax.empty_ref(jax.ShapeDtypeStruct(p.shape, p.dtype), memory_space=hbm) for p in ps]

    @pl.kernel(mesh=plsc.ScalarSubcoreMesh(axis_name="seq", num_cores=1), name=name,
               scratch_types=[pltpu.SemaphoreType.DMA((3, len(ps)))] * 2,
               compiler_params=pltpu.CompilerParams(collective_id=collective_id))
    def launch(*sems):
        _handshake_all()
        _scatter_body(srcs, outs, sems, lambda make, s, d: make(s, d).start())

    launch()
    return [o[...] for o in outs]


def _allreduce_small(v):
    rows = v.shape[0]

    def body(v_ref, o_ref, buf, ssem, rsem):
        x, y, c, _ = _place()
        me = 4 * x + 2 * y + c
        buf[me] = v_ref[...]
        sends = []
        for r in range(1, 8):
            peer = (x ^ (r >> 2), y ^ ((r >> 1) & 1), c ^ (r & 1))
            cp = pltpu.make_async_remote_copy(
                src_ref=v_ref, dst_ref=buf.at[me], send_sem=ssem.at[r - 1], recv_sem=rsem.at[r - 1],
                device_id=peer, device_id_type=MESH)
            cp.start()
            sends.append(cp)
        for r in range(1, 8):
            src = me ^ r
            pltpu.make_async_remote_copy(
                src_ref=v_ref, dst_ref=buf.at[src], send_sem=ssem.at[r - 1], recv_sem=rsem.at[r - 1],
                device_id=(x, y, c), device_id_type=MESH).wait_recv()
        for cp in sends:
            cp.wait_send()
        acc = buf[0]
        for d in range(1, 8):
            acc = acc + buf[d]
        o_ref[...] = acc

    vm = pl.BlockSpec(memory_space=pltpu.VMEM)
    return pl.pallas_call(
        body, name="allreduce_small", in_specs=[vm], out_specs=vm,
        out_shape=jax.ShapeDtypeStruct(v.shape, F32),
        scratch_shapes=[pltpu.VMEM((8, rows, 128), F32), pltpu.SemaphoreType.DMA((7,)),
                        pltpu.SemaphoreType.DMA((7,))],
        compiler_params=_cparams(has_side_effects=True),
    )(v)


def _t(w):
    return jnp.swapaxes(w, -1, -2)


def _count(shape):
    n = 1
    for s in shape:
        n *= s
    return n


def _pack_rows(arrs):
    flat = [jnp.pad(a.reshape(-1), (0, (-_count(a.shape)) % 128)) for a in arrs]
    v = jnp.concatenate(flat)
    rows = -(-v.shape[0] // (8 * 128)) * 8
    return jnp.pad(v, (0, rows * 128 - v.shape[0])).reshape(rows, 128)


def kernel(x, norm1_g, w_in, conv_w, q_norm_g, k_norm_g, sinks, conv_out_g, attn_out_g, w_o, norm2_g, w_gate, w_up, w_down, loss_target, m_norm1_g, m_w_in, m_conv_w, m_q_norm_g, m_k_norm_g, m_sinks, m_conv_out_g, m_attn_out_g, m_w_o, m_norm2_g, m_w_gate, m_w_up, m_w_down, v_norm1_g, v_w_in, v_conv_w, v_q_norm_g, v_k_norm_g, v_sinks, v_conv_out_g, v_attn_out_g, v_w_o, v_norm2_g, v_w_gate, v_w_up, v_w_down):
    depth = w_in.shape[0]
    t = x.shape[1]
    xs = x.reshape(t, D)
    tgt = loss_target.reshape(t, D)
    xi, yi = lax.axis_index("x"), lax.axis_index("y")
    kme = 2 * xi + yi
    tm = min(512, t)
    tq = min(512, t)
    tf = min(256, t)
    tw = min(1024, t)

    cwp = jnp.pad(conv_w.reshape(depth * 3, CC // N_CHIPS), ((0, 8 - depth * 3), (0, 0)))
    own_f = [jnp.concatenate([_t(w_gate[l]), _t(w_up[l]), w_down[l]], axis=0).astype(BF16) for l in range(depth)]
    own_o = [w_o[l].astype(BF16) for l in range(depth)]
    own_i = [_t(w_in[l]).astype(BF16) for l in range(depth)]
    mine = lambda got, own: lax.dynamic_update_index_in_dim(got, own, kme, 0)
    got_i0, got_o0, got_cw = _gather_layer([own_i[0], own_o[0], cwp], 0)
    gf0_in = lax.optimization_barrier((own_f[0], got_i0))[0]
    (got_f0,) = _gather_layer_async([gf0_in], 0, "gather_ffn0_seq", collective_id=6)
    cw_full = mine(got_cw, cwp).transpose(1, 0, 2).reshape(8, CC)[:depth * 3].reshape(depth, 3, CC)

    chip = kme.reshape(1).astype(jnp.int32)

    def layer_params(l, got_o):
        return dict(
            wo=mine(got_o, own_o[l]).reshape(MIXW, D),
            cw=jnp.pad(cw_full[l], ((0, 5), (0, 0))),
            g1=norm1_g[l].reshape(1, D), g2=norm2_g[l].reshape(1, D),
            gq=jnp.pad(q_norm_g[l], (0, HP - HD)).reshape(1, HP), gk=jnp.pad(k_norm_g[l], (0, HP - HD)).reshape(1, HP),
            sk=sinks[l].reshape(1, NQ), gco=conv_out_g[l].reshape(1, CC),
            gao=attn_out_g[l].reshape(1, NQ * HD))

    saved, layers = [], []
    cur = xs
    for l in range(depth):
        x_in = cur
        if l == 0:
            got_i, p = got_i0, layer_params(0, got_o0)
        else:
            got_f1, got_o1, got_i = lax.optimization_barrier((got_l1, cur))[0]
            p = layer_params(1, got_o1)
        proj, h, p["wpt"] = _inproj_fwd(cur, p["g1"], got_i, own_i[l], chip, tm)
        xm, mix, ao = _mixer_fwd(proj, cur, p["cw"], p["gq"], p["gk"], p["sk"], p["gco"], p["gao"], p["wo"], tq)
        if l == 0:
            got_f0 = lax.optimization_barrier((got_f0, xm))[0]
            l1_in = lax.optimization_barrier(([own_f[1], own_o[1], own_i[1]], got_f0))[0]
            got_l1 = _gather_layer_async(l1_in, 1, "gather_layer1_seq", collective_id=1)
        p["gf"] = mine(got_f0 if l == 0 else got_f1, own_f[l])
        layers.append(p)
        if l < depth - 1:
            cur, a, b, h2 = _ffn_fwd(xm, p["g2"], p["gf"], tm)
        else:
            lpart, dy, a, b, h2 = _ffn_fwd(xm, p["g2"], p["gf"], tm, tgt)
        saved.append(dict(x=x_in, proj=proj, h=h, xm=xm, mix=mix, ao=ao, a=a, b=b, h2=h2))

    nt = t // tq
    ci = lax.axis_index("c")
    core = ci.reshape(1).astype(jnp.int32)
    rbig = [dict() for _ in range(depth)]
    gsmall = [None] * depth

    def after_(vals, after):
        return vals if after is None else lax.optimization_barrier((vals, after))[0]

    def reduce_1(gs, tag, ids):
        return gs, _swap_siblings(gs, True, f"swap_halves_{tag}_seq", ids[0]), tag, ids

    def reduce_2(state, after):
        gs, theirs, tag, ids = state
        ps = _presum_halves(gs, after_(theirs, after), core)
        return ps, _scatter_chips_async(ps, f"scatter_{tag}_seq", ids[1]), tag, ids

    def reduce_3(state, after):
        ps, got, tag, ids = state
        cs = [lax.dynamic_update_index_in_dim(g, lax.dynamic_index_in_dim(q, kme, 0, keepdims=False), kme, 0)
              for g, q in zip(after_(got, after), ps)]
        r_mine = _sum_chips(cs)
        return r_mine, _swap_siblings(r_mine, False, f"swap_reduced_{tag}" + ("_seq" if ids[2] else ""), ids[2])

    def reduce_4(state, after):
        r_mine, r_theirs = state
        return [jnp.where(ci == 0, jnp.concatenate([a, b], axis=0), jnp.concatenate([b, a], axis=0))
                for a, b in zip(r_mine, after_(r_theirs, after))]

    ids = {"ffn1": (7, 4, 8), "in1": (9, 5, 10), "ffn0": (11, 2, 12), "in0": (13, 3, None)}
    in_2 = None
    handed = {}
    for l in reversed(range(depth)):
        p, s = layers[l], saved[l]
        dxm, da, db, hm, dg2 = _ffn_bwd(dy, s["xm"], p["g2"], s["a"], s["b"], p["gf"], tf)
        if in_2 is not None:
            in_2 = reduce_2(in_2, dxm)
        g_wg = _wgrad_blocks(da, s["h2"], tw, "wgrad_gate")
        g_wu = _wgrad_blocks(db, s["h2"], tw, "wgrad_up")
        g_wd = _wgrad_blocks(hm, dy, tw, "wgrad_down")
        if in_2 is not None:
            handed[f"in{l + 1}"] = reduce_3(in_2, g_wd)
        ffn_1 = reduce_1([g_wg, g_wu, g_wd], f"ffn{l}", ids[f"ffn{l}"])
        dpm, dkvm, dkvh, dcw, dgq, dgk, dsk, dgco, dgao = _mixer_bwd(
            dxm, s["proj"], s["ao"], p["cw"], p["gq"], p["gk"], p["sk"], p["gco"], p["gao"], p["wo"], tq)
        ffn_2 = reduce_2(ffn_1, dpm)
        g_o = _wgrad(s["mix"], dxm, tw, "wgrad_o")
        dx, dg1, dkv = _inproj_bwd(dpm, dkvm, dkvh, p["wpt"], s["x"], p["g1"], dxm, tq)
        g_in = jnp.concatenate(
            [_wgrad(dpm, s["h"], tw, "wgrad_in_main", [(0, 0, O_Q)] + _head_rows(O_Q, NQ)),
             _wgrad(dkv, s["h"], tw, "wgrad_in_kv", _head_rows(0, 2 * NKV))], axis=0)
        dy = dx
        gsmall[l] = dict(g1=dg1, cw=dcw[:3], gq=dgq[0, :HD], gk=dgk[0, :HD], sk=dsk[0, :NQ], gco=dgco,
                         gao=dgao, g2=dg2)
        handed[f"ffn{l}"] = reduce_3(ffn_2, g_in)
        in_2 = reduce_1([g_in.reshape(N_CHIPS, -1, D), g_o.reshape(N_CHIPS, -1, D)], f"in{l}", ids[f"in{l}"])
    grad_x = dy.reshape(x.shape)

    small_shapes = dict(g1=(D,), cw=(3, CC), gq=(HD,), gk=(HD,), sk=(NQ,), gco=(CC,), gao=(NQ * HD,), g2=(D,))
    red = _allreduce_small(_pack_rows([gsmall[l][n] for l in range(depth) for n in small_shapes]
                                      + [lpart[0:1, 0:1]])).reshape(-1)
    red_small, offs = {n: [] for n in small_shapes}, 0
    for l in range(depth):
        for n, shp in small_shapes.items():
            cnt = _count(shp)
            red_small[n].append(red[offs:offs + cnt].reshape(shp))
            offs += -(-cnt // 128) * 128
    loss = red[offs]
    g_small = {n: jnp.stack(v) for n, v in red_small.items()}
    g_cw = lax.dynamic_slice_in_dim(g_small["cw"], kme * (CC // N_CHIPS), CC // N_CHIPS, axis=2)

    weights = [norm1_g, w_in, conv_w, q_norm_g, k_norm_g, sinks, conv_out_g, attn_out_g, w_o, norm2_g, w_gate,
               w_up, w_down]
    moms = [m_norm1_g, m_w_in, m_conv_w, m_q_norm_g, m_k_norm_g, m_sinks, m_conv_out_g, m_attn_out_g, m_w_o,
            m_norm2_g, m_w_gate, m_w_up, m_w_down]
    vars_ = [v_norm1_g, v_w_in, v_conv_w, v_q_norm_g, v_k_norm_g, v_sinks, v_conv_out_g, v_attn_out_g, v_w_o,
             v_norm2_g, v_w_gate, v_w_up, v_w_down]
    n_w = len(weights)
    big_idx = dict(zip(("in", "o", "g", "u", "d"), (1, 8, 10, 11, 12)))
    small_idx = [n for n in range(n_w) if n not in big_idx.values()]
    grads, deltas, new_m, new_v = [None] * n_w, [None] * n_w, [None] * n_w, [None] * n_w
    for n, g in zip(small_idx, (g_small["g1"], g_cw, g_small["gq"], g_small["gk"], g_small["sk"], g_small["gco"],
                                g_small["gao"], g_small["g2"])):
        grads[n] = g

    def update_big(name):
        n = big_idx[name]
        g = jnp.stack([rbig[l][name] for l in range(depth)])
        flip = g.shape != weights[n].shape
        rows2d = lambda a3: (_t(a3) if flip else a3).reshape(-1, D)
        res = _adamw(rows2d(weights[n]), g.reshape(-1, D), rows2d(moms[n]), rows2d(vars_[n]), f"adamw_{n}")
        res = [g] + [r.reshape(g.shape) for r in res]
        grads[n], deltas[n], new_m[n], new_v[n] = [_t(r) for r in res] if flip else res

    for l in range(depth):
        rbig[l]["g"], rbig[l]["u"], rbig[l]["d"] = reduce_4(handed[f"ffn{l}"], red)
    rbig[1]["in"], rbig[1]["o"] = reduce_4(handed["in1"], red)
    update_big("g")
    in_2 = reduce_2(in_2, new_v[big_idx["g"]])
    update_big("u")
    update_big("d")
    rbig[0]["in"], rbig[0]["o"] = reduce_4(reduce_3(in_2, new_v[big_idx["d"]]), None)
    for name in ("in", "o"):
        update_big(name)
    res = _adamw(*[_pack_rows([arrs[n] for n in small_idx]) for arrs in (weights, grads, moms, vars_)],
                 "adamw_small")
    offs = 0
    for n in small_idx:
        shp = weights[n].shape
        cnt = _count(shp)
        deltas[n], new_m[n], new_v[n] = [r.reshape(-1)[offs:offs + cnt].reshape(shp) for r in res]
        offs += -(-cnt // 128) * 128
    return (loss, grad_x, *grads, *deltas, *new_m, *new_v)
```

```python
import functools

import jax
import jax.numpy as jnp
from jax import lax
from jax.experimental import pallas as pl
from jax.experimental.pallas import tpu as pltpu
from jax.experimental.pallas import tpu_sc as plsc

F32 = jnp.float32
BF16 = jnp.bfloat16

D = 1024
CC = 512
NQ = 8
NKV = 2
HD = 64
HP = 128
GRP = NQ // NKV
FF = 2816
FFB = FF // 4
BLK = 128
EPS = 1e-6
NEG = -1e30
SCALE = HD ** -0.5
O_BG, O_CG, O_HC, O_Q = 0, CC, 2 * CC, 3 * CC
O_K = O_Q + NQ * HP
O_V = O_K + NKV * HP
NP = O_V + NKV * HP
NMAIN = O_K
MIXW = CC + NQ * HD
N_CHIPS = 4
VMEM_LIMIT = 56 * 1024 * 1024
MESH = pl.DeviceIdType.MESH

ADAM_LR, ADAM_B1, ADAM_B2, ADAM_EPS, ADAM_WD, ADAM_STEP = 0.001, 0.9, 0.999, 1e-08, 0.01, 10


def _cparams(sem=None, **kw):
    if sem is not None:
        kw["dimension_semantics"] = sem
    return pltpu.CompilerParams(vmem_limit_bytes=VMEM_LIMIT, **kw)


def _const_spec(shape):
    nd = len(shape)
    return pl.BlockSpec(shape, lambda *_: (0,) * nd, pipeline_mode=pl.Buffered(1))


def _nt(a, b):
    return lax.dot_general(a, b, (((1,), (1,)), ((), ())), preferred_element_type=F32)


def _tn(a, b):
    return lax.dot_general(a, b, (((0,), (0,)), ((), ())), preferred_element_type=F32)


def _rms_fwd(x, inv_n):
    r = lax.rsqrt(jnp.sum(x * x, axis=-1, keepdims=True) * inv_n + EPS)
    return r, x * r


def _rms_bwd(dy, g, xh, r, inv_n):
    dxh = dy * g
    return r * (dxh - xh * (jnp.sum(dxh * xh, axis=-1, keepdims=True) * inv_n))


W_IN_ROWS = 3 * CC + (NQ + 2 * NKV) * HD
W_IN_BLOCK = W_IN_ROWS // N_CHIPS


def _padded_row(row):
    return row + max(row - O_Q, 0) // HD * (HP - HD)


def _w_in_pieces(k):
    first = k * W_IN_BLOCK
    plain = min(max(O_Q - first, 0), W_IN_BLOCK)
    pieces = [(0, first, plain)] if plain else []
    return pieces + [(r, _padded_row(first + r), HD) for r in range(plain, W_IN_BLOCK, HD)]


def _inproj_fwd(x, g1, gi, own_i, chip, tm):
    t = x.shape[0]

    def body(chip_ref, x_ref, g_ref, gi_ref, own_ref, p_ref, h_ref, w_ref, sem):
        @pl.when(pl.program_id(0) == 0)
        def _():
            for k in range(N_CHIPS):
                for src, dst, rows in _w_in_pieces(k):
                    @pl.when(chip_ref[0] == k)
                    def _():
                        pltpu.make_async_copy(own_ref.at[pl.ds(src, rows)], w_ref.at[pl.ds(dst, rows)], sem).start()

                    @pl.when(chip_ref[0] != k)
                    def _():
                        pltpu.make_async_copy(gi_ref.at[k, pl.ds(src, rows)], w_ref.at[pl.ds(dst, rows)], sem).start()
            for slot in range(NQ + 2 * NKV):
                w_ref[O_Q + slot * HP + HD:O_Q + (slot + 1) * HP, :] = jnp.zeros((HP - HD, D), BF16)
            landed = w_ref.at[pl.ds(0, W_IN_ROWS)]
            pltpu.make_async_copy(landed, landed, sem).wait()

        _, xh = _rms_fwd(x_ref[...], 1.0 / D)
        h = (xh * g_ref[...]).astype(BF16)
        h_ref[...] = h
        p_ref[...] = _nt(h, w_ref[...])

    const = lambda shape: pl.BlockSpec(shape, lambda i, c: (0,) * len(shape))
    return pl.pallas_call(
        body, name="inproj_fwd",
        grid_spec=pltpu.PrefetchScalarGridSpec(
            num_scalar_prefetch=1, grid=(t // tm,),
            in_specs=[pl.BlockSpec((tm, D), lambda i, c: (i, 0)), const((1, D)), ANY, ANY],
            out_specs=[pl.BlockSpec((tm, NP), lambda i, c: (i, 0)), pl.BlockSpec((tm, D), lambda i, c: (i, 0)),
                       const((NP, D))],
            scratch_shapes=[pltpu.SemaphoreType.DMA(())]),
        out_shape=[jax.ShapeDtypeStruct((t, NP), F32), jax.ShapeDtypeStruct((t, D), BF16),
                   jax.ShapeDtypeStruct((NP, D), BF16)],
        compiler_params=_cparams(("arbitrary",)),
    )(chip, x, g1, gi, own_i)


def _band_mask():
    r_io = lax.broadcasted_iota(jnp.int32, (BLK, 2 * BLK), 0)
    c_io = lax.broadcasted_iota(jnp.int32, (BLK, 2 * BLK), 1)
    return (c_io > r_io) & (c_io <= r_io + BLK), c_io


def _conv_taps(uf, n):
    u1 = pltpu.roll(uf, 1, 0)[8:8 + n]
    u2 = pltpu.roll(uf, 2, 0)[8:8 + n]
    return u1, u2


def _attn_probs(qs, kband, sink, valid):
    s = jnp.where(valid, _nt(qs, kband), NEG)
    m = jnp.maximum(jnp.max(s, axis=-1, keepdims=True), sink)
    p = jnp.exp(s - m)
    es = jnp.exp(sink - m)
    inv = 1.0 / (jnp.sum(p, axis=-1, keepdims=True) + es)
    return p * inv, es * inv


def _norm_keys(kraw, gk):
    out = []
    for h in range(NKV):
        kh = kraw[:, h * HP:(h + 1) * HP]
        rk, khat = _rms_fwd(kh, 1.0 / HD)
        out.append((khat, rk, (khat * gk).astype(BF16)))
    return out


def _mixer_fwd(proj, x, cw, gq, gk, sinks, gco, gao, wo, tq):
    t = proj.shape[0]
    nb = tq // BLK
    r8 = tq // 8

    def body(p_ref, cgp_ref, hcp_ref, kvp_ref, x_ref, cw_ref, gq_ref, gk_ref, sk_ref, gco_ref, gao_ref,
             wo_ref, xm_ref, mix_ref, ao_ref, aop_ref):
        i = pl.program_id(0)
        cg = p_ref[:, O_CG:O_CG + CC]
        hc = p_ref[:, O_HC:O_HC + CC]
        u = cg * hc
        up = jnp.where(i > 0, cgp_ref[...] * hcp_ref[...], 0.0)
        u1, u2 = _conv_taps(jnp.concatenate([up, u], axis=0), tq)
        y = cw_ref[0:1, :] * u2 + cw_ref[1:2, :] * u1 + cw_ref[2:3, :] * u
        co = p_ref[:, O_BG:O_BG + CC] * y
        _, coh = _rms_fwd(co, 1.0 / CC)
        cn = coh * gco_ref[...]
        kraw = jnp.concatenate([kvp_ref[:, 0:NKV * HP], p_ref[:, O_K:O_K + NKV * HP]], axis=0)
        vraw = jnp.concatenate([kvp_ref[:, NKV * HP:], p_ref[:, O_V:O_V + NKV * HP]], axis=0)
        keys = _norm_keys(kraw, gk_ref[...])
        vb = [vraw[:, h * HP:(h + 1) * HP].astype(BF16) for h in range(NKV)]
        base_valid, c_io = _band_mask()
        gqs = gq_ref[...] * SCALE
        for b in range(nb):
            lo = jnp.where(i * nb + b == 0, BLK, 0)
            valid = base_valid & (c_io >= lo)
            for g in range(NQ):
                h = g // GRP
                qg = p_ref[b * BLK:(b + 1) * BLK, O_Q + g * HP:O_Q + (g + 1) * HP]
                _, qh = _rms_fwd(qg, 1.0 / HD)
                qs = (qh * gqs).astype(BF16)
                pr, _ = _attn_probs(qs, keys[h][2][b * BLK:b * BLK + 2 * BLK], sk_ref[0, g], valid)
                aop_ref[b * BLK:(b + 1) * BLK, g * HP:(g + 1) * HP] = jnp.dot(
                    pr.astype(BF16), vb[h][b * BLK:b * BLK + 2 * BLK], preferred_element_type=F32)
        for j in range(NQ // 2):
            ao_ref[:, j * HP:(j + 1) * HP] = (aop_ref[:, 2 * j * HP:(2 * j + 1) * HP]
                                              + pltpu.roll(aop_ref[:, (2 * j + 1) * HP:(2 * j + 2) * HP], HD, 1))
        _, aoh = _rms_fwd(ao_ref[...], 1.0 / (NQ * HD))
        an = aoh * gao_ref[...]
        mix = jnp.concatenate([cn, an], axis=1).astype(BF16)
        mix_ref[...] = mix
        xm_ref[...] = x_ref[...] + jnp.dot(mix, wo_ref[...], preferred_element_type=F32)

    prev8 = lambda col: pl.BlockSpec((8, CC), lambda i: (jnp.maximum(i * r8 - 1, 0), col))
    return pl.pallas_call(
        body, name="mixer_fwd", grid=(t // tq,),
        in_specs=[
            pl.BlockSpec((tq, NP), lambda i: (i, 0)),
            prev8(O_CG // CC), prev8(O_HC // CC),
            pl.BlockSpec((BLK, 2 * NKV * HP), lambda i: (jnp.maximum(i * nb - 1, 0), O_K // (2 * NKV * HP))),
            pl.BlockSpec((tq, D), lambda i: (i, 0)),
            _const_spec((8, CC)), _const_spec((1, HP)), _const_spec((1, HP)),
            pl.BlockSpec(memory_space=pltpu.SMEM),
            _const_spec((1, CC)), _const_spec((1, NQ * HD)), _const_spec((MIXW, D)),
        ],
        out_specs=[pl.BlockSpec((tq, D), lambda i: (i, 0)), pl.BlockSpec((tq, MIXW), lambda i: (i, 0)),
                   pl.BlockSpec((tq, NQ * HD), lambda i: (i, 0))],
        out_shape=[jax.ShapeDtypeStruct((t, D), F32), jax.ShapeDtypeStruct((t, MIXW), BF16),
                   jax.ShapeDtypeStruct((t, NQ * HD), F32)],
        scratch_shapes=[pltpu.VMEM((tq, NQ * HP), F32)],
        compiler_params=_cparams(("parallel",)),
    )(proj, proj, proj, proj, x, cw, gq, gk, sinks, gco, gao, wo)


def _ffn_weight_specs():
    return [pl.BlockSpec((N_CHIPS, FFB, D), lambda i, j=j: (0, j, 0), pipeline_mode=pl.Buffered(1))
            for j in range(3)]


def _ffn_fwd(xm, g2, gf, tm, tgt=None):
    t = xm.shape[0]
    last = tgt is not None

    def body(x_ref, g_ref, wg_ref, wu_ref, wd_ref, *rest):
        t_ref, rest = (rest[0], rest[1:]) if last else (None, rest)
        l_ref, rest = (rest[0], rest[1:]) if last else (None, rest)
        xo_ref, a_ref, b_ref, h2_ref = rest
        xv = x_ref[...]
        _, xh = _rms_fwd(xv, 1.0 / D)
        h2 = (xh * g_ref[...]).astype(BF16)
        h2_ref[...] = h2
        acc = xv
        for k in range(N_CHIPS):
            a = _nt(h2, wg_ref[k])
            b = _nt(h2, wu_ref[k])
            a_ref[k] = a.astype(BF16)
            b_ref[k] = b.astype(BF16)
            hm = (a * jax.nn.sigmoid(a) * b).astype(BF16)
            acc = acc + jnp.dot(hm, wd_ref[k], preferred_element_type=F32)
        if last:
            @pl.when(pl.program_id(0) == 0)
            def _():
                l_ref[...] = jnp.zeros_like(l_ref)

            e = acc - t_ref[...]
            xo_ref[...] = e * (1.0 / D)
            l_ref[...] += jnp.sum(jnp.sum(e * e, axis=-1, keepdims=True), axis=0, keepdims=True) * (0.5 / D)
        else:
            xo_ref[...] = acc

    row = lambda w: pl.BlockSpec((tm, w), lambda i: (i, 0))
    blk = pl.BlockSpec((N_CHIPS, tm, FFB), lambda i: (0, i, 0))
    bsd = jax.ShapeDtypeStruct((N_CHIPS, t, FFB), BF16)
    return pl.pallas_call(
        body, name="ffn_fwd_loss" if last else "ffn_fwd", grid=(t // tm,),
        in_specs=[row(D), _const_spec((1, D))] + _ffn_weight_specs() + ([row(D)] if last else []),
        out_specs=([pl.BlockSpec((8, 128), lambda i: (0, 0))] if last else []) + [row(D), blk, blk, row(D)],
        out_shape=([jax.ShapeDtypeStruct((8, 128), F32)] if last else [])
        + [jax.ShapeDtypeStruct((t, D), F32), bsd, bsd, jax.ShapeDtypeStruct((t, D), BF16)],
        compiler_params=_cparams(("arbitrary" if last else "parallel",)),
    )(*((xm, g2, gf, gf, gf) + ((tgt,) if last else ())))


def _ffn_bwd(dy, xm, g2, a, b, gf, tm):
    t = dy.shape[0]

    def body(dy_ref, x_ref, g_ref, a_ref, b_ref, wg_ref, wu_ref, wd_ref, dx_ref, da_ref, db_ref, hm_ref, dg_ref):
        @pl.when(pl.program_id(0) == 0)
        def _():
            dg_ref[...] = jnp.zeros_like(dg_ref)

        dyv = dy_ref[...]
        dyb = dyv.astype(BF16)
        dh2 = jnp.zeros_like(dyv)
        for k in range(N_CHIPS):
            dhm = _nt(dyb, wd_ref[k])
            av = a_ref[k].astype(F32)
            bv = b_ref[k].astype(F32)
            sig = jax.nn.sigmoid(av)
            sil = av * sig
            hm_ref[k] = (sil * bv).astype(BF16)
            da = (dhm * bv * (sig * (1.0 + av * (1.0 - sig)))).astype(BF16)
            db = (dhm * sil).astype(BF16)
            da_ref[k] = da
            db_ref[k] = db
            dh2 = (dh2 + jnp.dot(da, wg_ref[k], preferred_element_type=F32)
                   + jnp.dot(db, wu_ref[k], preferred_element_type=F32))
        r, xh = _rms_fwd(x_ref[...], 1.0 / D)
        dg_ref[...] += jnp.sum(dh2 * xh, axis=0, keepdims=True)
        dx_ref[...] = dyv + _rms_bwd(dh2, g_ref[...], xh, r, 1.0 / D)

    row = lambda w: pl.BlockSpec((tm, w), lambda i: (i, 0))
    blk = pl.BlockSpec((N_CHIPS, tm, FFB), lambda i: (0, i, 0))
    bsd = jax.ShapeDtypeStruct((N_CHIPS, t, FFB), BF16)
    return pl.pallas_call(
        body, name="ffn_bwd", grid=(t // tm,),
        in_specs=[row(D), row(D), _const_spec((1, D)), blk, blk] + _ffn_weight_specs(),
        out_specs=[row(D), blk, blk, blk, pl.BlockSpec((1, D), lambda i: (0, 0))],
        out_shape=[jax.ShapeDtypeStruct((t, D), F32), bsd, bsd, bsd, jax.ShapeDtypeStruct((1, D), F32)],
        compiler_params=_cparams(("arbitrary",)),
    )(dy, xm, g2, a, b, gf, gf, gf)


def _wgrad_blocks(a, b, tt, name):
    _, t, rows = a.shape
    cols = b.shape[1]
    nsteps = t // tt

    def body(a_ref, b_ref, o_ref, acc_ref):
        s = pl.program_id(0)

        @pl.when(s == 0)
        def _():
            acc_ref[...] = jnp.zeros_like(acc_ref)

        bv = b_ref[...].astype(BF16)
        for k in range(N_CHIPS):
            acc_ref[k] += _tn(a_ref[k], bv)

        @pl.when(s == nsteps - 1)
        def _():
            o_ref[...] = acc_ref[...].astype(BF16)

    return pl.pallas_call(
        body, name=name, grid=(nsteps,),
        in_specs=[pl.BlockSpec((N_CHIPS, tt, rows), lambda s: (0, s, 0)), pl.BlockSpec((tt, cols), lambda s: (s, 0))],
        out_specs=pl.BlockSpec((N_CHIPS, rows, cols), lambda s: (0, 0, 0)),
        out_shape=jax.ShapeDtypeStruct((N_CHIPS, rows, cols), BF16),
        scratch_shapes=[pltpu.VMEM((N_CHIPS, rows, cols), F32)],
        compiler_params=_cparams(("arbitrary",)),
    )(a, b)


def _head_rows(first, n_heads):
    return [(first + g * HD, first + g * HP, HD) for g in range(n_heads)]


def _wgrad(a, b, tt, name, pieces=None):
    t, k = a.shape
    n = b.shape[1]
    nsteps = t // tt
    pieces = pieces or [(0, 0, k)]
    rows = sum(p[2] for p in pieces)

    def body(a_ref, b_ref, o_ref, acc_ref):
        s = pl.program_id(0)

        @pl.when(s == 0)
        def _():
            acc_ref[...] = jnp.zeros_like(acc_ref)

        acc_ref[...] += _tn(a_ref[...].astype(BF16), b_ref[...].astype(BF16))

        @pl.when(s == nsteps - 1)
        def _():
            for dst, src, size in pieces:
                o_ref[dst:dst + size, :] = acc_ref[src:src + size, :].astype(BF16)

    return pl.pallas_call(
        body, name=name, grid=(nsteps,),
        in_specs=[pl.BlockSpec((tt, k), lambda s: (s, 0)), pl.BlockSpec((tt, n), lambda s: (s, 0))],
        out_specs=pl.BlockSpec((rows, n), lambda s: (0, 0)),
        out_shape=jax.ShapeDtypeStruct((rows, n), BF16),
        scratch_shapes=[pltpu.VMEM((k, n), F32)],
        compiler_params=_cparams(("arbitrary",)),
    )(a, b)


def _mixer_bwd(dxm, proj, ao, cw, gq, gk, sinks, gco, gao, wo, tq):
    t = proj.shape[0]
    nb = tq // BLK
    r8 = tq // 8
    nt = t // tq
    te = tq + 8
    kvw = 2 * NKV * HP

    def body(dx_ref, dxn_ref, p_ref, cgp_ref, hcp_ref, bgn_ref, cgn_ref, hcn_ref, kvp_ref, ao_ref, cw_ref, gq_ref,
             gk_ref, sk_ref, gco_ref, gao_ref, wo_ref,
             dpm_ref, dkvm_ref, dkvh_ref, dcw_ref, dgq_ref, dgk_ref, dsk_ref, dgco_ref, dgao_ref, acc_ref):
        i = pl.program_id(0)

        @pl.when(i == 0)
        def _():
            for r in (dcw_ref, dgq_ref, dgk_ref, dsk_ref, dgco_ref, dgao_ref):
                r[...] = jnp.zeros_like(r)

        acc_ref[...] = jnp.zeros_like(acc_ref)
        live_rows = jnp.where(i < nt - 1, te, tq)
        dxb = dx_ref[...].astype(BF16)
        dxe = jnp.concatenate([dxb, dxn_ref[...].astype(BF16)], axis=0)
        dcn = _nt(dxe, wo_ref[0:CC, :])
        bg = jnp.concatenate([p_ref[:, O_BG:O_BG + CC], bgn_ref[...]], axis=0)
        cg = jnp.concatenate([p_ref[:, O_CG:O_CG + CC], cgn_ref[...]], axis=0)
        hc = jnp.concatenate([p_ref[:, O_HC:O_HC + CC], hcn_ref[...]], axis=0)
        u = cg * hc
        up = jnp.where(i > 0, cgp_ref[...] * hcp_ref[...], 0.0)
        u1, u2 = _conv_taps(jnp.concatenate([up, u], axis=0), te)
        w0, w1, w2 = cw_ref[0:1, :], cw_ref[1:2, :], cw_ref[2:3, :]
        y = w0 * u2 + w1 * u1 + w2 * u
        co = bg * y
        rc, coh = _rms_fwd(co, 1.0 / CC)
        dco = _rms_bwd(dcn, gco_ref[...], coh, rc, 1.0 / CC)
        row_io = lax.broadcasted_iota(jnp.int32, (te, 1), 0)
        own = row_io < tq
        dgco_ref[...] += jnp.sum(jnp.where(own, dcn * coh, 0.0), axis=0, keepdims=True)
        dyc = jnp.where(row_io < live_rows, dco * bg, 0.0)
        dyo = jnp.where(own, dyc, 0.0)
        dcw_ref[0:1, :] += jnp.sum(dyo * u2, axis=0, keepdims=True)
        dcw_ref[1:2, :] += jnp.sum(dyo * u1, axis=0, keepdims=True)
        dcw_ref[2:3, :] += jnp.sum(dyo * u, axis=0, keepdims=True)
        dy1 = pltpu.roll(dyc, te - 1, 0)[0:tq]
        dy2 = pltpu.roll(dyc, te - 2, 0)[0:tq]
        du = w2 * dyc[0:tq] + w1 * dy1 + w0 * dy2
        dpm_ref[:, O_BG:O_BG + CC] = (dco[0:tq] * y[0:tq]).astype(BF16)
        dpm_ref[:, O_CG:O_CG + CC] = (du * hc[0:tq]).astype(BF16)
        dpm_ref[:, O_HC:O_HC + CC] = (du * cg[0:tq]).astype(BF16)
        kraw = jnp.concatenate([kvp_ref[:, 0:NKV * HP], p_ref[:, O_K:O_K + NKV * HP]], axis=0)
        vraw = jnp.concatenate([kvp_ref[:, NKV * HP:], p_ref[:, O_V:O_V + NKV * HP]], axis=0)
        gqv, gkv = gq_ref[...], gk_ref[...]
        keys = _norm_keys(kraw, gkv)
        vb = [vraw[:, h * HP:(h + 1) * HP].astype(BF16) for h in range(NKV)]
        base_valid, c_io = _band_mask()
        lane = lax.broadcasted_iota(jnp.int32, (1, HP), 1)
        dgq, dgk, dsk = (jnp.zeros((1, HP), F32) for _ in range(3))
        dgao = jnp.zeros((1, NQ * HD), F32)
        for b in range(nb):
            lo = jnp.where(i * nb + b == 0, BLK, 0)
            valid = base_valid & (c_io >= lo)
            band = slice(b * BLK, b * BLK + 2 * BLK)
            blk = slice(b * BLK, (b + 1) * BLK)
            ra, aoh = _rms_fwd(ao_ref[blk, :], 1.0 / (NQ * HD))
            danb = _nt(dxb[blk], wo_ref[CC:MIXW, :])
            dgao = dgao + jnp.sum(danb * aoh, axis=0, keepdims=True)
            dao = _rms_bwd(danb, gao_ref[...], aoh, ra, 1.0 / (NQ * HD))
            dos = [dao[:, g // 2 * HP:(g // 2 + 1) * HP] for g in range(NQ)]
            dos = [(d if g % 2 == 0 else pltpu.roll(d, HD, 1)).astype(BF16) for g, d in enumerate(dos)]
            fwd = []
            for g in range(NQ):
                rq, qh = _rms_fwd(p_ref[blk, O_Q + g * HP:O_Q + (g + 1) * HP], 1.0 / HD)
                qs = (qh * (gqv * SCALE)).astype(BF16)
                fwd.append((rq, qh, qs) + _attn_probs(qs, keys[g // GRP][2][band], sk_ref[0, g], valid))
            dqs = []
            for h in range(NKV):
                khat, rk, kn = [a[band] for a in keys[h]]
                dss, prbs, qns, dobs = [], [], [], []
                for g in range(h * GRP, (h + 1) * GRP):
                    rq, qh, qs, pr, ps = fwd[g]
                    dob = dos[g]
                    dp = _nt(dob, vb[h][band])
                    delta = jnp.sum(pr * dp, axis=-1, keepdims=True)
                    dsb = (pr * (dp - delta)).astype(BF16)
                    dsk = dsk + jnp.where(lane == g, -jnp.sum(ps * delta, axis=0, keepdims=True), 0.0)
                    dqn = jnp.dot(dsb, kn, preferred_element_type=F32) * SCALE
                    dgq = dgq + jnp.sum(dqn * qh, axis=0, keepdims=True)
                    dqs.append(_rms_bwd(dqn, gqv, qh, rq, 1.0 / HD).astype(BF16))
                    dss.append(dsb)
                    prbs.append(pr.astype(BF16))
                    qns.append(qs)
                    dobs.append(dob)
                dkn = _tn(jnp.concatenate(dss, axis=0), jnp.concatenate(qns, axis=0))
                dv = _tn(jnp.concatenate(prbs, axis=0), jnp.concatenate(dobs, axis=0))
                dgk = dgk + jnp.sum(dkn * khat, axis=0, keepdims=True)
                acc_ref[band, h * HP:(h + 1) * HP] += _rms_bwd(dkn, gkv, khat, rk, 1.0 / HD)
                acc_ref[band, (NKV + h) * HP:(NKV + h + 1) * HP] += dv
            dpm_ref[blk, O_Q:O_K] = jnp.concatenate(dqs, axis=1)
        dgq_ref[...] += dgq
        dgk_ref[...] += dgk
        dsk_ref[...] += dsk
        dgao_ref[...] += dgao
        dkvh_ref[...] = acc_ref[0:BLK, :]
        dkvm_ref[...] = acc_ref[BLK:, :]

    prev8 = lambda col: pl.BlockSpec((8, CC), lambda i: (jnp.maximum(i * r8 - 1, 0), col))
    next8 = lambda col: pl.BlockSpec((8, CC), lambda i: (jnp.minimum((i + 1) * r8, t // 8 - 1), col))
    small = lambda n: pl.BlockSpec((1, n), lambda i: (0, 0))
    return pl.pallas_call(
        body, name="mixer_bwd", grid=(nt,),
        in_specs=[
            pl.BlockSpec((tq, D), lambda i: (i, 0)),
            pl.BlockSpec((8, D), lambda i: (jnp.minimum((i + 1) * r8, t // 8 - 1), 0)),
            pl.BlockSpec((tq, NP), lambda i: (i, 0)),
            prev8(O_CG // CC), prev8(O_HC // CC),
            next8(O_BG // CC), next8(O_CG // CC), next8(O_HC // CC),
            pl.BlockSpec((BLK, kvw), lambda i: (jnp.maximum(i * nb - 1, 0), O_K // kvw)),
            pl.BlockSpec((tq, NQ * HD), lambda i: (i, 0)),
            _const_spec((8, CC)), _const_spec((1, HP)), _const_spec((1, HP)),
            pl.BlockSpec(memory_space=pltpu.SMEM),
            _const_spec((1, CC)), _const_spec((1, NQ * HD)), _const_spec((MIXW, D)),
        ],
        out_specs=[
            pl.BlockSpec((tq, NMAIN), lambda i: (i, 0)),
            pl.BlockSpec((tq, kvw), lambda i: (i, 0)),
            pl.BlockSpec((BLK, kvw), lambda i: (i, 0)),
            pl.BlockSpec((8, CC), lambda i: (0, 0)), small(HP), small(HP), small(HP), small(CC), small(NQ * HD),
        ],
        out_shape=[
            jax.ShapeDtypeStruct((t, NMAIN), BF16), jax.ShapeDtypeStruct((t, kvw), F32),
            jax.ShapeDtypeStruct((nt * BLK, kvw), F32),
            jax.ShapeDtypeStruct((8, CC), F32), jax.ShapeDtypeStruct((1, HP), F32), jax.ShapeDtypeStruct((1, HP), F32),
            jax.ShapeDtypeStruct((1, HP), F32), jax.ShapeDtypeStruct((1, CC), F32),
            jax.ShapeDtypeStruct((1, NQ * HD), F32),
        ],
        scratch_shapes=[pltpu.VMEM((tq + BLK, kvw), F32)],
        compiler_params=_cparams(("arbitrary",)),
    )(dxm, dxm, proj, proj, proj, proj, proj, proj, proj, ao, cw, gq, gk, sinks, gco, gao, wo)


def _inproj_bwd(dpm, dkvm, dkvh, wpt, x, g1, dxm, tm):
    t = x.shape[0]
    kvw = 2 * NKV * HP
    nt = t // tm

    def body(dp_ref, dk_ref, dh_ref, w_ref, x_ref, g_ref, dxm_ref, dx_ref, dg_ref, dkv_ref):
        i = pl.program_id(0)

        @pl.when(i == 0)
        def _():
            dg_ref[...] = jnp.zeros_like(dg_ref)

        halo = jnp.where(i < nt - 1, dh_ref[...], 0.0)
        dkv_ref[0:tm - BLK, :] = dk_ref[0:tm - BLK, :].astype(BF16)
        dkv_ref[tm - BLK:tm, :] = (dk_ref[tm - BLK:tm, :] + halo).astype(BF16)
        dh = (jnp.dot(dp_ref[...], w_ref[0:NMAIN, :], preferred_element_type=F32)
              + jnp.dot(dkv_ref[...], w_ref[NMAIN:NP, :], preferred_element_type=F32))
        r, xh = _rms_fwd(x_ref[...], 1.0 / D)
        dg_ref[...] += jnp.sum(dh * xh, axis=0, keepdims=True)
        dx_ref[...] = dxm_ref[...] + _rms_bwd(dh, g_ref[...], xh, r, 1.0 / D)

    row = lambda w: pl.BlockSpec((tm, w), lambda i: (i, 0))
    return pl.pallas_call(
        body, name="inproj_bwd", grid=(nt,),
        in_specs=[row(NMAIN), row(kvw), pl.BlockSpec((BLK, kvw), lambda i: (jnp.minimum(i + 1, nt - 1), 0)),
                  _const_spec((NP, D)), row(D), _const_spec((1, D)), row(D)],
        out_specs=[row(D), pl.BlockSpec((1, D), lambda i: (0, 0)), row(kvw)],
        out_shape=[jax.ShapeDtypeStruct((t, D), F32), jax.ShapeDtypeStruct((1, D), F32),
                   jax.ShapeDtypeStruct((t, kvw), BF16)],
        compiler_params=_cparams(("arbitrary",)),
    )(dpm, dkvm, dkvh, wpt, x, g1, dxm)


def _rows_tile(rows, cap=512):
    for cand in range(min(rows, cap) // 16 * 16, 0, -16):
        if rows % cand == 0:
            return cand
    return rows


def _presum_halves(gs, theirs, core):
    n = len(gs)

    def body(c_ref, *refs):
        for g_ref, t_ref, o_ref in zip(refs[:n], refs[n:2 * n], refs[2 * n:]):
            o_ref[...] = (g_ref[...].astype(F32) + t_ref[...].astype(F32)).astype(BF16)

    half = lambda ta: pl.BlockSpec((None,) + ta.shape[1:], lambda k, c_ref: (k, 0, 0))
    own = lambda ta: pl.BlockSpec((None,) + ta.shape[1:], lambda k, c_ref: (k, c_ref[0], 0))
    return pl.pallas_call(
        body, name="presum",
        grid_spec=pltpu.PrefetchScalarGridSpec(
            num_scalar_prefetch=1, grid=(N_CHIPS,),
            in_specs=[own(ta) for ta in theirs] + [half(ta) for ta in theirs],
            out_specs=[half(ta) for ta in theirs]),
        out_shape=[jax.ShapeDtypeStruct(ta.shape, BF16) for ta in theirs],
        compiler_params=_cparams(("parallel",)),
    )(core, *gs, *theirs)


def _sum_chips(got, ps, chip):
    n = len(got)
    steps = 2

    def body(chip_ref, *refs):
        for c_ref, own_ref, o_ref in zip(refs[:n], refs[n:2 * n], refs[2 * n:]):
            acc = None
            for j in range(N_CHIPS):
                term = jnp.where(chip_ref[0] == j, own_ref[...], c_ref[j]).astype(F32)
                acc = term if acc is None else acc + term
            o_ref[...] = acc

    tile = lambda c: (c.shape[1] // steps, c.shape[2])
    return pl.pallas_call(
        body, name="chipsum",
        grid_spec=pltpu.PrefetchScalarGridSpec(
            num_scalar_prefetch=1, grid=(steps,),
            in_specs=[pl.BlockSpec((N_CHIPS,) + tile(c), lambda i, chip_ref: (0, i, 0)) for c in got]
            + [pl.BlockSpec((None,) + tile(c), lambda i, chip_ref: (chip_ref[0], i, 0)) for c in got],
            out_specs=[pl.BlockSpec(tile(c), lambda i, chip_ref: (i, 0)) for c in got]),
        out_shape=[jax.ShapeDtypeStruct(c.shape[1:], F32) for c in got],
        compiler_params=_cparams(("parallel",)),
    )(chip, *got, *ps)


def _adamw(w, g, m, v, name):
    rows, cols = w.shape
    tr = _rows_tile(rows, 256)
    c1 = 1.0 - ADAM_B1 ** ADAM_STEP
    c2 = 1.0 - ADAM_B2 ** ADAM_STEP

    def body(w_ref, g_ref, m_ref, v_ref, d_ref, mo_ref, vo_ref):
        gv = g_ref[...]
        mn = ADAM_B1 * m_ref[...] + (1.0 - ADAM_B1) * gv
        vn = ADAM_B2 * v_ref[...] + (1.0 - ADAM_B2) * (gv * gv)
        mo_ref[...] = mn
        vo_ref[...] = vn
        d_ref[...] = -ADAM_LR * ((mn / c1) / (jnp.sqrt(vn / c2) + ADAM_EPS) + ADAM_WD * w_ref[...])

    spec = pl.BlockSpec((tr, cols), lambda i: (i, 0))
    sds = jax.ShapeDtypeStruct((rows, cols), F32)
    return pl.pallas_call(
        body, name=name, grid=(rows // tr,), in_specs=[spec] * 4, out_specs=[spec] * 3, out_shape=[sds] * 3,
        compiler_params=_cparams(("parallel",)),
    )(w, g, m, v)


def _place():
    x, y, c = lax.axis_index("x"), lax.axis_index("y"), lax.axis_index("c")
    chips = [(1 - x, y), (x, 1 - y), (1 - x, 1 - y)]
    return x, y, c, chips


ANY = pl.BlockSpec(memory_space=pl.ANY)
DMA_ROWS = 64


def _pieces(shape):
    rows = shape[-2]
    step = DMA_ROWS if rows % DMA_ROWS == 0 else rows
    lead = [()]
    for n in shape[:-2]:
        lead = [i + (k,) for i in lead for k in range(n)]
    return [i + (pl.ds(r0, step),) for i in lead for r0 in range(0, rows, step)]


def _start_pieces(make, src, dst):
    for idx in _pieces(src.shape):
        make(src.at[idx], dst.at[idx]).start()


def _gather_layer(blocks, layer):
    nw = len(blocks)

    def body(*refs):
        _gather_body(refs[:nw], refs[nw:2 * nw], refs[2 * nw:], layer, _start_pieces)

    return pl.pallas_call(
        body, name=f"gather_layer{layer}", in_specs=[ANY] * nw, out_specs=[ANY] * nw,
        out_shape=[jax.ShapeDtypeStruct((N_CHIPS,) + b.shape, b.dtype) for b in blocks],
        scratch_shapes=[pltpu.SemaphoreType.DMA((3, nw))] * 4,
        compiler_params=_cparams(has_side_effects=True),
    )(*blocks)


def _gather_body(srcs, outs, sems, layer, start):
    nw = len(srcs)
    ssem, rsem, fssem, frsem = sems
    x, y, c, chips = _place()
    kme = 2 * x + y

    def plane(j, w, to):
        return lambda s, d: pltpu.make_async_remote_copy(
            src_ref=s, dst_ref=d, send_sem=ssem.at[j, w], recv_sem=rsem.at[j, w], device_id=to,
            device_id_type=MESH)

    def passed(j, w):
        return lambda s, d: pltpu.make_async_remote_copy(
            src_ref=s, dst_ref=d, send_sem=fssem.at[j, w], recv_sem=frsem.at[j, w],
            device_id=(x, y, 1 - c), device_id_type=MESH)

    @pl.when(c == layer)
    def _():
        for j, (px, py) in enumerate(chips):
            for w in range(nw):
                start(plane(j, w, (px, py, c)), srcs[w], outs[w].at[kme])
        for j, (px, py) in enumerate(chips):
            for w in range(nw):
                got = outs[w].at[2 * px + py]
                plane(j, w, (px, py, c))(got, got).wait_recv()
                start(passed(j, w), got, got)
        for j, (px, py) in enumerate(chips):
            for w in range(nw):
                got = outs[w].at[2 * px + py]
                plane(j, w, (px, py, c))(got, got).wait_send()
                passed(j, w)(got, got).wait_send()

    @pl.when(c != layer)
    def _():
        for j, (px, py) in enumerate(chips):
            for w in range(nw):
                got = outs[w].at[2 * px + py]
                passed(j, w)(got, got).wait_recv()


def _handshake(peers):
    barrier = pltpu.get_barrier_semaphore()
    for peer in peers:
        pl.semaphore_signal(barrier, inc=1, device_id=peer, device_id_type=MESH)
    pl.semaphore_wait(barrier, len(peers))


def _handshake_all():
    x, y, c, _ = _place()
    _handshake([(x ^ (r >> 2), y ^ ((r >> 1) & 1), c ^ (r & 1)) for r in range(1, 8)])


def _gather_layer_async(blocks, layer, name, collective_id):
    hbm = pltpu.MemorySpace.HBM
    srcs = [jax.new_ref(b, memory_space=hbm) for b in blocks]
    outs = [jax.empty_ref(jax.ShapeDtypeStruct((N_CHIPS,) + b.shape, b.dtype), memory_space=hbm) for b in blocks]

    @pl.kernel(mesh=plsc.ScalarSubcoreMesh(axis_name="seq", num_cores=1), name=name,
               scratch_types=[pltpu.SemaphoreType.DMA((3, len(blocks)))] * 4,
               compiler_params=pltpu.CompilerParams(collective_id=collective_id))
    def launch(*sems):
        _handshake_all()
        _gather_body(srcs, outs, sems, layer, lambda make, s, d: make(s, d).start())

    launch()
    return [o[...] for o in outs]


def _swap_siblings(arrs, halves, name, collective_id=None):
    nw = len(arrs)
    out_sds = [jax.ShapeDtypeStruct((a.shape[0], a.shape[1] // 2, a.shape[2]) if halves else a.shape, a.dtype)
               for a in arrs]

    def exchange(srcs, outs, ssem, rsem, start):
        x, y, c, _ = _place()

        def give(w):
            return lambda s, d: pltpu.make_async_remote_copy(
                src_ref=s, dst_ref=d, send_sem=ssem.at[w], recv_sem=rsem.at[w], device_id=(x, y, 1 - c),
                device_id_type=MESH)

        for w in range(nw):
            hr = outs[w].shape[1]
            start(give(w), srcs[w].at[:, pl.ds((1 - c) * hr, hr)] if halves else srcs[w], outs[w])
        for w in range(nw):
            give(w)(outs[w], outs[w]).wait()

    if collective_id is None:
        def body(*refs):
            exchange(refs[:nw], refs[nw:2 * nw], *refs[2 * nw:], _start_pieces)

        return pl.pallas_call(
            body, name=name, in_specs=[ANY] * nw, out_specs=[ANY] * nw, out_shape=out_sds,
            scratch_shapes=[pltpu.SemaphoreType.DMA((nw,))] * 2,
            compiler_params=_cparams(has_side_effects=True),
        )(*arrs)

    hbm = pltpu.MemorySpace.HBM
    srcs = [jax.new_ref(a, memory_space=hbm) for a in arrs]
    outs = [jax.empty_ref(sds, memory_space=hbm) for sds in out_sds]

    @pl.kernel(mesh=plsc.ScalarSubcoreMesh(axis_name="seq", num_cores=1), name=name,
               scratch_types=[pltpu.SemaphoreType.DMA((nw,))] * 2,
               compiler_params=pltpu.CompilerParams(collective_id=collective_id))
    def launch(ssem, rsem):
        x, y, c, _ = _place()
        _handshake([(x, y, 1 - c)])
        exchange(srcs, outs, ssem, rsem, lambda make, s, d: make(s, d).start())

    launch()
    return [o[...] for o in outs]


def _scatter_chips(ps):
    nw = len(ps)

    def body(*refs):
        _scatter_body(refs[:nw], refs[nw:2 * nw], refs[2 * nw:], _start_pieces)

    return pl.pallas_call(
        body, name="scatter_chips", in_specs=[ANY] * nw, out_specs=[ANY] * nw,
        out_shape=[jax.ShapeDtypeStruct(p.shape, p.dtype) for p in ps],
        scratch_shapes=[pltpu.SemaphoreType.DMA((3, nw)), pltpu.SemaphoreType.DMA((3, nw))],
        compiler_params=_cparams(has_side_effects=True),
    )(*ps)


def _scatter_body(srcs, outs, sems, start):
    nw = len(srcs)
    ssem, rsem = sems
    x, y, c, chips = _place()
    kme = 2 * x + y

    def give(j, w, to):
        return lambda s, d: pltpu.make_async_remote_copy(
            src_ref=s, dst_ref=d, send_sem=ssem.at[j, w], recv_sem=rsem.at[j, w], device_id=to,
            device_id_type=MESH)

    for j, (px, py) in enumerate(chips):
        for w in range(nw):
            start(give(j, w, (px, py, c)), srcs[w].at[2 * px + py], outs[w].at[kme])
    for j, (px, py) in enumerate(chips):
        for w in range(nw):
            got = outs[w].at[2 * px + py]
            give(j, w, (px, py, c))(got, got).wait_recv()
    for j, (px, py) in enumerate(chips):
        for w in range(nw):
            sent = srcs[w].at[2 * px + py]
            give(j, w, (px, py, c))(sent, sent).wait_send()


def _scatter_chips_async(ps, name, collective_id):
    hbm = pltpu.MemorySpace.HBM
    srcs = [jax.new_ref(p, memory_space=hbm) for p in ps]
    outs = [jax.empty_ref(jax.ShapeDtypeStruct(p.shape, p.dtype), memory_space=hbm) for p in ps]

    @pl.kernel(mesh=plsc.ScalarSubcoreMesh(axis_name="seq", num_cores=1), name=name,
               scratch_types=[pltpu.SemaphoreType.DMA((3, len(ps)))] * 2,
               compiler_params=pltpu.CompilerParams(collective_id=collective_id))
    def launch(*sems):
        _handshake_all()
        _scatter_body(srcs, outs, sems, lambda make, s, d: make(s, d).start())

    launch()
    return [o[...] for o in outs]


def _allreduce_small(v):
    rows = v.shape[0]

    def body(v_ref, o_ref, buf, ssem, rsem):
        x, y, c, _ = _place()
        me = 4 * x + 2 * y + c
        buf[me] = v_ref[...]
        sends = []
        for r in range(1, 8):
            peer = (x ^ (r >> 2), y ^ ((r >> 1) & 1), c ^ (r & 1))
            cp = pltpu.make_async_remote_copy(
                src_ref=v_ref, dst_ref=buf.at[me], send_sem=ssem.at[r - 1], recv_sem=rsem.at[r - 1],
                device_id=peer, device_id_type=MESH)
            cp.start()
            sends.append(cp)
        for r in range(1, 8):
            src = me ^ r
            pltpu.make_async_remote_copy(
                src_ref=v_ref, dst_ref=buf.at[src], send_sem=ssem.at[r - 1], recv_sem=rsem.at[r - 1],
                device_id=(x, y, c), device_id_type=MESH).wait_recv()
        for cp in sends:
            cp.wait_send()
        acc = buf[0]
        for d in range(1, 8):
            acc = acc + buf[d]
        o_ref[...] = acc

    vm = pl.BlockSpec(memory_space=pltpu.VMEM)
    return pl.pallas_call(
        body, name="allreduce_small", in_specs=[vm], out_specs=vm,
        out_shape=jax.ShapeDtypeStruct(v.shape, F32),
        scratch_shapes=[pltpu.VMEM((8, rows, 128), F32), pltpu.SemaphoreType.DMA((7,)),
                        pltpu.SemaphoreType.DMA((7,))],
        compiler_params=_cparams(has_side_effects=True),
    )(v)


def _t(w):
    return jnp.swapaxes(w, -1, -2)


def _count(shape):
    n = 1
    for s in shape:
        n *= s
    return n


def _pack_rows(arrs):
    flat = [jnp.pad(a.reshape(-1), (0, (-_count(a.shape)) % 128)) for a in arrs]
    v = jnp.concatenate(flat)
    rows = -(-v.shape[0] // (8 * 128)) * 8
    return jnp.pad(v, (0, rows * 128 - v.shape[0])).reshape(rows, 128)


def kernel(x, norm1_g, w_in, conv_w, q_norm_g, k_norm_g, sinks, conv_out_g, attn_out_g, w_o, norm2_g, w_gate, w_up, w_down, loss_target, m_norm1_g, m_w_in, m_conv_w, m_q_norm_g, m_k_norm_g, m_sinks, m_conv_out_g, m_attn_out_g, m_w_o, m_norm2_g, m_w_gate, m_w_up, m_w_down, v_norm1_g, v_w_in, v_conv_w, v_q_norm_g, v_k_norm_g, v_sinks, v_conv_out_g, v_attn_out_g, v_w_o, v_norm2_g, v_w_gate, v_w_up, v_w_down):
    depth = w_in.shape[0]
    t = x.shape[1]
    xs = x.reshape(t, D)
    tgt = loss_target.reshape(t, D)
    xi, yi = lax.axis_index("x"), lax.axis_index("y")
    kme = 2 * xi + yi
    tm = min(512, t)
    tq = min(512, t)
    tf = min(256, t)
    tw = min(1024, t)

    cwp = jnp.pad(conv_w.reshape(depth * 3, CC // N_CHIPS), ((0, 8 - depth * 3), (0, 0)))
    own_f = [jnp.concatenate([_t(w_gate[l]), _t(w_up[l]), w_down[l]], axis=0).astype(BF16) for l in range(depth)]
    own_o = [w_o[l].astype(BF16) for l in range(depth)]
    own_i = [_t(w_in[l]).astype(BF16) for l in range(depth)]
    mine = lambda got, own: lax.dynamic_update_index_in_dim(got, own, kme, 0)
    got0 = _gather_layer_async([own_i[0], own_o[0], cwp], 0, "gather_in0_seq", collective_id=14)
    (got_i0, got_o0, got_cw), own_f, own_o, own_i = lax.optimization_barrier((got0, own_f, own_o, own_i))
    gf0_in = lax.optimization_barrier((own_f[0], got_i0))[0]
    (got_f0,) = _gather_layer_async([gf0_in], 0, "gather_ffn0_seq", collective_id=6)
    cw_full = mine(got_cw, cwp).transpose(1, 0, 2).reshape(8, CC)[:depth * 3].reshape(depth, 3, CC)

    chip = kme.reshape(1).astype(jnp.int32)

    def layer_params(l, got_o):
        return dict(
            wo=mine(got_o, own_o[l]).reshape(MIXW, D),
            cw=jnp.pad(cw_full[l], ((0, 5), (0, 0))),
            g1=norm1_g[l].reshape(1, D), g2=norm2_g[l].reshape(1, D),
            gq=jnp.pad(q_norm_g[l], (0, HP - HD)).reshape(1, HP), gk=jnp.pad(k_norm_g[l], (0, HP - HD)).reshape(1, HP),
            sk=sinks[l].reshape(1, NQ), gco=conv_out_g[l].reshape(1, CC),
            gao=attn_out_g[l].reshape(1, NQ * HD))

    saved, layers = [], []
    cur = xs
    for l in range(depth):
        x_in = cur
        if l == 0:
            got_i, p = got_i0, layer_params(0, got_o0)
        else:
            got_f1, got_o1, got_i = lax.optimization_barrier((got_l1, cur))[0]
            p = layer_params(1, got_o1)
        proj, h, p["wpt"] = _inproj_fwd(cur, p["g1"], got_i, own_i[l], chip, tm)
        xm, mix, ao = _mixer_fwd(proj, cur, p["cw"], p["gq"], p["gk"], p["sk"], p["gco"], p["gao"], p["wo"], tq)
        if l == 0:
            got_f0 = lax.optimization_barrier((got_f0, xm))[0]
            l1_in = lax.optimization_barrier(([own_f[1], own_o[1], own_i[1]], got_f0))[0]
            got_l1 = _gather_layer_async(l1_in, 1, "gather_layer1_seq", collective_id=1)
        p["gf"] = mine(got_f0 if l == 0 else got_f1, own_f[l])
        layers.append(p)
        if l < depth - 1:
            cur, a, b, h2 = _ffn_fwd(xm, p["g2"], p["gf"], tm)
        else:
            lpart, dy, a, b, h2 = _ffn_fwd(xm, p["g2"], p["gf"], tm, tgt)
        saved.append(dict(x=x_in, proj=proj, h=h, xm=xm, mix=mix, ao=ao, a=a, b=b, h2=h2))

    nt = t // tq
    ci = lax.axis_index("c")
    core = ci.reshape(1).astype(jnp.int32)
    rbig = [dict() for _ in range(depth)]
    gsmall = [None] * depth

    def after_(vals, after):
        return vals if after is None else lax.optimization_barrier((vals, after))[0]

    def reduce_1(gs, tag, ids):
        return gs, _swap_siblings(gs, True, f"swap_halves_{tag}_seq", ids[0]), tag, ids

    def reduce_2(state, after):
        gs, theirs, tag, ids = state
        ps = _presum_halves(gs, after_(theirs, after), core)
        return ps, _scatter_chips_async(ps, f"scatter_{tag}_seq", ids[1]), tag, ids

    def reduce_3(state, after):
        ps, got, tag, ids = state
        r_mine = _sum_chips(after_(got, after), ps, chip)
        return r_mine, _swap_siblings(r_mine, False, f"swap_reduced_{tag}" + ("_seq" if ids[2] else ""), ids[2])

    def reduce_4(state, after):
        r_mine, r_theirs = state
        return [jnp.where(ci == 0, jnp.concatenate([a, b], axis=0), jnp.concatenate([b, a], axis=0))
                for a, b in zip(r_mine, after_(r_theirs, after))]

    ids = {"ffn1": (7, 4, 8), "in1": (9, 5, 10), "ffn0": (11, 2, 12), "in0": (13, 3, None)}
    in_2 = None
    handed = {}
    for l in reversed(range(depth)):
        p, s = layers[l], saved[l]
        dxm, da, db, hm, dg2 = _ffn_bwd(dy, s["xm"], p["g2"], s["a"], s["b"], p["gf"], tf)
        if in_2 is not None:
            in_2 = reduce_2(in_2, dxm)
        g_wg = _wgrad_blocks(da, s["h2"], tw, "wgrad_gate")
        g_wu = _wgrad_blocks(db, s["h2"], tw, "wgrad_up")
        g_wd = _wgrad_blocks(hm, dy, tw, "wgrad_down")
        if in_2 is not None:
            handed[f"in{l + 1}"] = reduce_3(in_2, g_wd)
        ffn_1 = reduce_1([g_wg, g_wu, g_wd], f"ffn{l}", ids[f"ffn{l}"])
        dpm, dkvm, dkvh, dcw, dgq, dgk, dsk, dgco, dgao = _mixer_bwd(
            dxm, s["proj"], s["ao"], p["cw"], p["gq"], p["gk"], p["sk"], p["gco"], p["gao"], p["wo"], tq)
        ffn_2 = reduce_2(ffn_1, dpm)
        g_o = _wgrad(s["mix"], dxm, tw, "wgrad_o")
        dx, dg1, dkv = _inproj_bwd(dpm, dkvm, dkvh, p["wpt"], s["x"], p["g1"], dxm, tq)
        g_in = jnp.concatenate(
            [_wgrad(dpm, s["h"], tw, "wgrad_in_main", [(0, 0, O_Q)] + _head_rows(O_Q, NQ)),
             _wgrad(dkv, s["h"], tw, "wgrad_in_kv", _head_rows(0, 2 * NKV))], axis=0)
        dy = dx
        gsmall[l] = dict(g1=dg1, cw=dcw[:3], gq=dgq[0, :HD], gk=dgk[0, :HD], sk=dsk[0, :NQ], gco=dgco,
                         gao=dgao, g2=dg2)
        handed[f"ffn{l}"] = reduce_3(ffn_2, g_in)
        in_2 = reduce_1([g_in.reshape(N_CHIPS, -1, D), g_o.reshape(N_CHIPS, -1, D)], f"in{l}", ids[f"in{l}"])
    grad_x = dy.reshape(x.shape)

    small_shapes = dict(g1=(D,), cw=(3, CC), gq=(HD,), gk=(HD,), sk=(NQ,), gco=(CC,), gao=(NQ * HD,), g2=(D,))
    red = _allreduce_small(_pack_rows([gsmall[l][n] for l in range(depth) for n in small_shapes]
                                      + [lpart[0:1, 0:1]])).reshape(-1)
    red_small, offs = {n: [] for n in small_shapes}, 0
    for l in range(depth):
        for n, shp in small_shapes.items():
            cnt = _count(shp)
            red_small[n].append(red[offs:offs + cnt].reshape(shp))
            offs += -(-cnt // 128) * 128
    loss = red[offs]
    g_small = {n: jnp.stack(v) for n, v in red_small.items()}
    g_cw = lax.dynamic_slice_in_dim(g_small["cw"], kme * (CC // N_CHIPS), CC // N_CHIPS, axis=2)

    weights = [norm1_g, w_in, conv_w, q_norm_g, k_norm_g, sinks, conv_out_g, attn_out_g, w_o, norm2_g, w_gate,
               w_up, w_down]
    moms = [m_norm1_g, m_w_in, m_conv_w, m_q_norm_g, m_k_norm_g, m_sinks, m_conv_out_g, m_attn_out_g, m_w_o,
            m_norm2_g, m_w_gate, m_w_up, m_w_down]
    vars_ = [v_norm1_g, v_w_in, v_conv_w, v_q_norm_g, v_k_norm_g, v_sinks, v_conv_out_g, v_attn_out_g, v_w_o,
             v_norm2_g, v_w_gate, v_w_up, v_w_down]
    n_w = len(weights)
    big_idx = dict(zip(("in", "o", "g", "u", "d"), (1, 8, 10, 11, 12)))
    small_idx = [n for n in range(n_w) if n not in big_idx.values()]
    grads, deltas, new_m, new_v = [None] * n_w, [None] * n_w, [None] * n_w, [None] * n_w
    for n, g in zip(small_idx, (g_small["g1"], g_cw, g_small["gq"], g_small["gk"], g_small["sk"], g_small["gco"],
                                g_small["gao"], g_small["g2"])):
        grads[n] = g

    def update_big(name):
        n = big_idx[name]
        g = jnp.stack([rbig[l][name] for l in range(depth)])
        flip = g.shape != weights[n].shape
        rows2d = lambda a3: (_t(a3) if flip else a3).reshape(-1, D)
        res = _adamw(rows2d(weights[n]), g.reshape(-1, D), rows2d(moms[n]), rows2d(vars_[n]), f"adamw_{n}")
        res = [g] + [r.reshape(g.shape) for r in res]
        grads[n], deltas[n], new_m[n], new_v[n] = [_t(r) for r in res] if flip else res

    for l in range(depth):
        rbig[l]["g"], rbig[l]["u"], rbig[l]["d"] = reduce_4(handed[f"ffn{l}"], red)
    rbig[1]["in"], rbig[1]["o"] = reduce_4(handed["in1"], red)
    update_big("g")
    in_2 = reduce_2(in_2, new_v[big_idx["g"]])
    update_big("u")
    update_big("d")
    rbig[0]["in"], rbig[0]["o"] = reduce_4(reduce_3(in_2, new_v[big_idx["d"]]), None)
    for name in ("in", "o"):
        update_big(name)
    res = _adamw(*[_pack_rows([arrs[n] for n in small_idx]) for arrs in (weights, grads, moms, vars_)],
                 "adamw_small")
    offs = 0
    for n in small_idx:
        shp = weights[n].shape
        cnt = _count(shp)
        deltas[n], new_m[n], new_v[n] = [r.reshape(-1)[offs:offs + cnt].reshape(shp) for r in res]
        offs += -(-cnt // 128) * 128
    return (loss, grad_x, *grads, *deltas, *new_m, *new_v)
```

```python
import functools

import jax
import jax.numpy as jnp
from jax import lax
from jax.experimental import pallas as pl
from jax.experimental.pallas import tpu as pltpu
from jax.experimental.pallas import tpu_sc as plsc

F32 = jnp.float32
BF16 = jnp.bfloat16

D = 1024
CC = 512
NQ = 8
NKV = 2
HD = 64
HP = 128
GRP = NQ // NKV
FF = 2816
FFB = FF // 4
BLK = 128
EPS = 1e-6
NEG = -1e30
SCALE = HD ** -0.5
O_BG, O_CG, O_HC, O_Q = 0, CC, 2 * CC, 3 * CC
O_K = O_Q + NQ * HP
O_V = O_K + NKV * HP
NP = O_V + NKV * HP
NMAIN = O_K
MIXW = CC + NQ * HD
N_CHIPS = 4
VMEM_LIMIT = 56 * 1024 * 1024
MESH = pl.DeviceIdType.MESH

ADAM_LR, ADAM_B1, ADAM_B2, ADAM_EPS, ADAM_WD, ADAM_STEP = 0.001, 0.9, 0.999, 1e-08, 0.01, 10


def _cparams(sem=None, **kw):
    if sem is not None:
        kw["dimension_semantics"] = sem
    return pltpu.CompilerParams(vmem_limit_bytes=VMEM_LIMIT, **kw)


def _const_spec(shape):
    nd = len(shape)
    return pl.BlockSpec(shape, lambda *_: (0,) * nd, pipeline_mode=pl.Buffered(1))


def _nt(a, b):
    return lax.dot_general(a, b, (((1,), (1,)), ((), ())), preferred_element_type=F32)


def _tn(a, b):
    return lax.dot_general(a, b, (((0,), (0,)), ((), ())), preferred_element_type=F32)


def _rms_fwd(x, inv_n):
    r = lax.rsqrt(jnp.sum(x * x, axis=-1, keepdims=True) * inv_n + EPS)
    return r, x * r


def _rms_bwd(dy, g, xh, r, inv_n):
    dxh = dy * g
    return r * (dxh - xh * (jnp.sum(dxh * xh, axis=-1, keepdims=True) * inv_n))


W_IN_ROWS = 3 * CC + (NQ + 2 * NKV) * HD
W_IN_BLOCK = W_IN_ROWS // N_CHIPS


def _padded_row(row):
    return row + max(row - O_Q, 0) // HD * (HP - HD)


def _w_in_pieces(k):
    first = k * W_IN_BLOCK
    plain = min(max(O_Q - first, 0), W_IN_BLOCK)
    pieces = [(0, first, plain)] if plain else []
    return pieces + [(r, _padded_row(first + r), HD) for r in range(plain, W_IN_BLOCK, HD)]


def _inproj_fwd(x, g1, gi, own_i, chip, tm):
    t = x.shape[0]

    def body(chip_ref, x_ref, g_ref, gi_ref, own_ref, p_ref, h_ref, w_ref, sem):
        @pl.when(pl.program_id(0) == 0)
        def _():
            for k in range(N_CHIPS):
                for src, dst, rows in _w_in_pieces(k):
                    @pl.when(chip_ref[0] == k)
                    def _():
                        pltpu.make_async_copy(own_ref.at[pl.ds(src, rows)], w_ref.at[pl.ds(dst, rows)], sem).start()

                    @pl.when(chip_ref[0] != k)
                    def _():
                        pltpu.make_async_copy(gi_ref.at[k, pl.ds(src, rows)], w_ref.at[pl.ds(dst, rows)], sem).start()
            for slot in range(NQ + 2 * NKV):
                w_ref[O_Q + slot * HP + HD:O_Q + (slot + 1) * HP, :] = jnp.zeros((HP - HD, D), BF16)
            landed = w_ref.at[pl.ds(0, W_IN_ROWS)]
            pltpu.make_async_copy(landed, landed, sem).wait()

        _, xh = _rms_fwd(x_ref[...], 1.0 / D)
        h = (xh * g_ref[...]).astype(BF16)
        h_ref[...] = h
        p_ref[...] = _nt(h, w_ref[...])

    const = lambda shape: pl.BlockSpec(shape, lambda i, c: (0,) * len(shape))
    return pl.pallas_call(
        body, name="inproj_fwd",
        grid_spec=pltpu.PrefetchScalarGridSpec(
            num_scalar_prefetch=1, grid=(t // tm,),
            in_specs=[pl.BlockSpec((tm, D), lambda i, c: (i, 0)), const((1, D)), ANY, ANY],
            out_specs=[pl.BlockSpec((tm, NP), lambda i, c: (i, 0)), pl.BlockSpec((tm, D), lambda i, c: (i, 0)),
                       const((NP, D))],
            scratch_shapes=[pltpu.SemaphoreType.DMA(())]),
        out_shape=[jax.ShapeDtypeStruct((t, NP), F32), jax.ShapeDtypeStruct((t, D), BF16),
                   jax.ShapeDtypeStruct((NP, D), BF16)],
        compiler_params=_cparams(("arbitrary",)),
    )(chip, x, g1, gi, own_i)


def _band_mask():
    r_io = lax.broadcasted_iota(jnp.int32, (BLK, 2 * BLK), 0)
    c_io = lax.broadcasted_iota(jnp.int32, (BLK, 2 * BLK), 1)
    return (c_io > r_io) & (c_io <= r_io + BLK), c_io


def _conv_taps(uf, n):
    u1 = pltpu.roll(uf, 1, 0)[8:8 + n]
    u2 = pltpu.roll(uf, 2, 0)[8:8 + n]
    return u1, u2


def _attn_probs(qs, kband, sink, valid):
    s = jnp.where(valid, _nt(qs, kband), NEG)
    m = jnp.maximum(jnp.max(s, axis=-1, keepdims=True), sink)
    p = jnp.exp(s - m)
    es = jnp.exp(sink - m)
    inv = 1.0 / (jnp.sum(p, axis=-1, keepdims=True) + es)
    return p * inv, es * inv


def _norm_keys(kraw, gk):
    out = []
    for h in range(NKV):
        kh = kraw[:, h * HP:(h + 1) * HP]
        rk, khat = _rms_fwd(kh, 1.0 / HD)
        out.append((khat, rk, (khat * gk).astype(BF16)))
    return out


def _mixer_fwd(proj, x, cw, gq, gk, sinks, gco, gao, wo, tq):
    t = proj.shape[0]
    nb = tq // BLK
    r8 = tq // 8

    def body(p_ref, cgp_ref, hcp_ref, kvp_ref, x_ref, cw_ref, gq_ref, gk_ref, sk_ref, gco_ref, gao_ref,
             wo_ref, xm_ref, mix_ref, ao_ref, aop_ref):
        i = pl.program_id(0)
        cg = p_ref[:, O_CG:O_CG + CC]
        hc = p_ref[:, O_HC:O_HC + CC]
        u = cg * hc
        up = jnp.where(i > 0, cgp_ref[...] * hcp_ref[...], 0.0)
        u1, u2 = _conv_taps(jnp.concatenate([up, u], axis=0), tq)
        y = cw_ref[0:1, :] * u2 + cw_ref[1:2, :] * u1 + cw_ref[2:3, :] * u
        co = p_ref[:, O_BG:O_BG + CC] * y
        _, coh = _rms_fwd(co, 1.0 / CC)
        cn = coh * gco_ref[...]
        kraw = jnp.concatenate([kvp_ref[:, 0:NKV * HP], p_ref[:, O_K:O_K + NKV * HP]], axis=0)
        vraw = jnp.concatenate([kvp_ref[:, NKV * HP:], p_ref[:, O_V:O_V + NKV * HP]], axis=0)
        keys = _norm_keys(kraw, gk_ref[...])
        vb = [vraw[:, h * HP:(h + 1) * HP].astype(BF16) for h in range(NKV)]
        base_valid, c_io = _band_mask()
        gqs = gq_ref[...] * SCALE
        for b in range(nb):
            lo = jnp.where(i * nb + b == 0, BLK, 0)
            valid = base_valid & (c_io >= lo)
            for g in range(NQ):
                h = g // GRP
                qg = p_ref[b * BLK:(b + 1) * BLK, O_Q + g * HP:O_Q + (g + 1) * HP]
                _, qh = _rms_fwd(qg, 1.0 / HD)
                qs = (qh * gqs).astype(BF16)
                pr, _ = _attn_probs(qs, keys[h][2][b * BLK:b * BLK + 2 * BLK], sk_ref[0, g], valid)
                aop_ref[b * BLK:(b + 1) * BLK, g * HP:(g + 1) * HP] = jnp.dot(
                    pr.astype(BF16), vb[h][b * BLK:b * BLK + 2 * BLK], preferred_element_type=F32)
        for j in range(NQ // 2):
            ao_ref[:, j * HP:(j + 1) * HP] = (aop_ref[:, 2 * j * HP:(2 * j + 1) * HP]
                                              + pltpu.roll(aop_ref[:, (2 * j + 1) * HP:(2 * j + 2) * HP], HD, 1))
        _, aoh = _rms_fwd(ao_ref[...], 1.0 / (NQ * HD))
        an = aoh * gao_ref[...]
        mix = jnp.concatenate([cn, an], axis=1).astype(BF16)
        mix_ref[...] = mix
        xm_ref[...] = x_ref[...] + jnp.dot(mix, wo_ref[...], preferred_element_type=F32)

    prev8 = lambda col: pl.BlockSpec((8, CC), lambda i: (jnp.maximum(i * r8 - 1, 0), col))
    return pl.pallas_call(
        body, name="mixer_fwd", grid=(t // tq,),
        in_specs=[
            pl.BlockSpec((tq, NP), lambda i: (i, 0)),
            prev8(O_CG // CC), prev8(O_HC // CC),
            pl.BlockSpec((BLK, 2 * NKV * HP), lambda i: (jnp.maximum(i * nb - 1, 0), O_K // (2 * NKV * HP))),
            pl.BlockSpec((tq, D), lambda i: (i, 0)),
            _const_spec((8, CC)), _const_spec((1, HP)), _const_spec((1, HP)),
            pl.BlockSpec(memory_space=pltpu.SMEM),
            _const_spec((1, CC)), _const_spec((1, NQ * HD)), _const_spec((MIXW, D)),
        ],
        out_specs=[pl.BlockSpec((tq, D), lambda i: (i, 0)), pl.BlockSpec((tq, MIXW), lambda i: (i, 0)),
                   pl.BlockSpec((tq, NQ * HD), lambda i: (i, 0))],
        out_shape=[jax.ShapeDtypeStruct((t, D), F32), jax.ShapeDtypeStruct((t, MIXW), BF16),
                   jax.ShapeDtypeStruct((t, NQ * HD), F32)],
        scratch_shapes=[pltpu.VMEM((tq, NQ * HP), F32)],
        compiler_params=_cparams(("parallel",)),
    )(proj, proj, proj, proj, x, cw, gq, gk, sinks, gco, gao, wo)


def _ffn_weight_specs():
    return [pl.BlockSpec((N_CHIPS, FFB, D), lambda i, j=j: (0, j, 0), pipeline_mode=pl.Buffered(1))
            for j in range(3)]


def _ffn_fwd(xm, g2, gf, tm, tgt=None):
    t = xm.shape[0]
    last = tgt is not None

    def body(x_ref, g_ref, wg_ref, wu_ref, wd_ref, *rest):
        t_ref, rest = (rest[0], rest[1:]) if last else (None, rest)
        l_ref, rest = (rest[0], rest[1:]) if last else (None, rest)
        xo_ref, a_ref, b_ref, h2_ref = rest
        xv = x_ref[...]
        _, xh = _rms_fwd(xv, 1.0 / D)
        h2 = (xh * g_ref[...]).astype(BF16)
        h2_ref[...] = h2
        acc = xv
        for k in range(N_CHIPS):
            a = _nt(h2, wg_ref[k])
            b = _nt(h2, wu_ref[k])
            a_ref[k] = a.astype(BF16)
            b_ref[k] = b.astype(BF16)
            hm = (a * jax.nn.sigmoid(a) * b).astype(BF16)
            acc = acc + jnp.dot(hm, wd_ref[k], preferred_element_type=F32)
        if last:
            @pl.when(pl.program_id(0) == 0)
            def _():
                l_ref[...] = jnp.zeros_like(l_ref)

            e = acc - t_ref[...]
            xo_ref[...] = e * (1.0 / D)
            l_ref[...] += jnp.sum(jnp.sum(e * e, axis=-1, keepdims=True), axis=0, keepdims=True) * (0.5 / D)
        else:
            xo_ref[...] = acc

    row = lambda w: pl.BlockSpec((tm, w), lambda i: (i, 0))
    blk = pl.BlockSpec((N_CHIPS, tm, FFB), lambda i: (0, i, 0))
    bsd = jax.ShapeDtypeStruct((N_CHIPS, t, FFB), BF16)
    return pl.pallas_call(
        body, name="ffn_fwd_loss" if last else "ffn_fwd", grid=(t // tm,),
        in_specs=[row(D), _const_spec((1, D))] + _ffn_weight_specs() + ([row(D)] if last else []),
        out_specs=([pl.BlockSpec((8, 128), lambda i: (0, 0))] if last else []) + [row(D), blk, blk, row(D)],
        out_shape=([jax.ShapeDtypeStruct((8, 128), F32)] if last else [])
        + [jax.ShapeDtypeStruct((t, D), F32), bsd, bsd, jax.ShapeDtypeStruct((t, D), BF16)],
        compiler_params=_cparams(("arbitrary" if last else "parallel",)),
    )(*((xm, g2, gf, gf, gf) + ((tgt,) if last else ())))


def _ffn_bwd(dy, xm, g2, a, b, gf, tm):
    t = dy.shape[0]

    def body(dy_ref, x_ref, g_ref, a_ref, b_ref, wg_ref, wu_ref, wd_ref, dx_ref, da_ref, db_ref, hm_ref, dg_ref):
        @pl.when(pl.program_id(0) == 0)
        def _():
            dg_ref[...] = jnp.zeros_like(dg_ref)

        dyv = dy_ref[...]
        dyb = dyv.astype(BF16)
        dh2 = jnp.zeros_like(dyv)
        for k in range(N_CHIPS):
            dhm = _nt(dyb, wd_ref[k])
            av = a_ref[k].astype(F32)
            bv = b_ref[k].astype(F32)
            sig = jax.nn.sigmoid(av)
            sil = av * sig
            hm_ref[k] = (sil * bv).astype(BF16)
            da = (dhm * bv * (sig * (1.0 + av * (1.0 - sig)))).astype(BF16)
            db = (dhm * sil).astype(BF16)
            da_ref[k] = da
            db_ref[k] = db
            dh2 = (dh2 + jnp.dot(da, wg_ref[k], preferred_element_type=F32)
                   + jnp.dot(db, wu_ref[k], preferred_element_type=F32))
        r, xh = _rms_fwd(x_ref[...], 1.0 / D)
        dg_ref[...] += jnp.sum(dh2 * xh, axis=0, keepdims=True)
        dx_ref[...] = dyv + _rms_bwd(dh2, g_ref[...], xh, r, 1.0 / D)

    row = lambda w: pl.BlockSpec((tm, w), lambda i: (i, 0))
    blk = pl.BlockSpec((N_CHIPS, tm, FFB), lambda i: (0, i, 0))
    bsd = jax.ShapeDtypeStruct((N_CHIPS, t, FFB), BF16)
    return pl.pallas_call(
        body, name="ffn_bwd", grid=(t // tm,),
        in_specs=[row(D), row(D), _const_spec((1, D)), blk, blk] + _ffn_weight_specs(),
        out_specs=[row(D), blk, blk, blk, pl.BlockSpec((1, D), lambda i: (0, 0))],
        out_shape=[jax.ShapeDtypeStruct((t, D), F32), bsd, bsd, bsd, jax.ShapeDtypeStruct((1, D), F32)],
        compiler_params=_cparams(("arbitrary",)),
    )(dy, xm, g2, a, b, gf, gf, gf)


def _wgrad_blocks(a, b, tt, name):
    _, t, rows = a.shape
    cols = b.shape[1]
    nsteps = t // tt

    def body(a_ref, b_ref, o_ref, acc_ref):
        s = pl.program_id(0)

        @pl.when(s == 0)
        def _():
            acc_ref[...] = jnp.zeros_like(acc_ref)

        bv = b_ref[...].astype(BF16)
        for k in range(N_CHIPS):
            acc_ref[k] += _tn(a_ref[k], bv)

        @pl.when(s == nsteps - 1)
        def _():
            o_ref[...] = acc_ref[...].astype(BF16)

    return pl.pallas_call(
        body, name=name, grid=(nsteps,),
        in_specs=[pl.BlockSpec((N_CHIPS, tt, rows), lambda s: (0, s, 0)), pl.BlockSpec((tt, cols), lambda s: (s, 0))],
        out_specs=pl.BlockSpec((N_CHIPS, rows, cols), lambda s: (0, 0, 0)),
        out_shape=jax.ShapeDtypeStruct((N_CHIPS, rows, cols), BF16),
        scratch_shapes=[pltpu.VMEM((N_CHIPS, rows, cols), F32)],
        compiler_params=_cparams(("arbitrary",)),
    )(a, b)


def _head_rows(first, n_heads):
    return [(first + g * HD, first + g * HP, HD) for g in range(n_heads)]


def _wgrad(a, b, tt, name, pieces=None):
    t, k = a.shape
    n = b.shape[1]
    nsteps = t // tt
    pieces = pieces or [(0, 0, k)]
    rows = sum(p[2] for p in pieces)

    def body(a_ref, b_ref, o_ref, acc_ref):
        s = pl.program_id(0)

        @pl.when(s == 0)
        def _():
            acc_ref[...] = jnp.zeros_like(acc_ref)

        acc_ref[...] += _tn(a_ref[...].astype(BF16), b_ref[...].astype(BF16))

        @pl.when(s == nsteps - 1)
        def _():
            for dst, src, size in pieces:
                o_ref[dst:dst + size, :] = acc_ref[src:src + size, :].astype(BF16)

    return pl.pallas_call(
        body, name=name, grid=(nsteps,),
        in_specs=[pl.BlockSpec((tt, k), lambda s: (s, 0)), pl.BlockSpec((tt, n), lambda s: (s, 0))],
        out_specs=pl.BlockSpec((rows, n), lambda s: (0, 0)),
        out_shape=jax.ShapeDtypeStruct((rows, n), BF16),
        scratch_shapes=[pltpu.VMEM((k, n), F32)],
        compiler_params=_cparams(("arbitrary",)),
    )(a, b)


def _mixer_bwd(dxm, proj, ao, cw, gq, gk, sinks, gco, gao, wo, tq):
    t = proj.shape[0]
    nb = tq // BLK
    r8 = tq // 8
    nt = t // tq
    te = tq + 8
    kvw = 2 * NKV * HP

    def body(dx_ref, dxn_ref, p_ref, cgp_ref, hcp_ref, bgn_ref, cgn_ref, hcn_ref, kvp_ref, ao_ref, cw_ref, gq_ref,
             gk_ref, sk_ref, gco_ref, gao_ref, wo_ref,
             dpm_ref, dkvm_ref, dkvh_ref, dcw_ref, dgq_ref, dgk_ref, dsk_ref, dgco_ref, dgao_ref, acc_ref):
        i = pl.program_id(0)

        @pl.when(i == 0)
        def _():
            for r in (dcw_ref, dgq_ref, dgk_ref, dsk_ref, dgco_ref, dgao_ref):
                r[...] = jnp.zeros_like(r)

        acc_ref[...] = jnp.zeros_like(acc_ref)
        live_rows = jnp.where(i < nt - 1, te, tq)
        dxb = dx_ref[...].astype(BF16)
        dxe = jnp.concatenate([dxb, dxn_ref[...].astype(BF16)], axis=0)
        dcn = _nt(dxe, wo_ref[0:CC, :])
        bg = jnp.concatenate([p_ref[:, O_BG:O_BG + CC], bgn_ref[...]], axis=0)
        cg = jnp.concatenate([p_ref[:, O_CG:O_CG + CC], cgn_ref[...]], axis=0)
        hc = jnp.concatenate([p_ref[:, O_HC:O_HC + CC], hcn_ref[...]], axis=0)
        u = cg * hc
        up = jnp.where(i > 0, cgp_ref[...] * hcp_ref[...], 0.0)
        u1, u2 = _conv_taps(jnp.concatenate([up, u], axis=0), te)
        w0, w1, w2 = cw_ref[0:1, :], cw_ref[1:2, :], cw_ref[2:3, :]
        y = w0 * u2 + w1 * u1 + w2 * u
        co = bg * y
        rc, coh = _rms_fwd(co, 1.0 / CC)
        dco = _rms_bwd(dcn, gco_ref[...], coh, rc, 1.0 / CC)
        row_io = lax.broadcasted_iota(jnp.int32, (te, 1), 0)
        own = row_io < tq
        dgco_ref[...] += jnp.sum(jnp.where(own, dcn * coh, 0.0), axis=0, keepdims=True)
        dyc = jnp.where(row_io < live_rows, dco * bg, 0.0)
        dyo = jnp.where(own, dyc, 0.0)
        dcw_ref[0:1, :] += jnp.sum(dyo * u2, axis=0, keepdims=True)
        dcw_ref[1:2, :] += jnp.sum(dyo * u1, axis=0, keepdims=True)
        dcw_ref[2:3, :] += jnp.sum(dyo * u, axis=0, keepdims=True)
        dy1 = pltpu.roll(dyc, te - 1, 0)[0:tq]
        dy2 = pltpu.roll(dyc, te - 2, 0)[0:tq]
        du = w2 * dyc[0:tq] + w1 * dy1 + w0 * dy2
        dpm_ref[:, O_BG:O_BG + CC] = (dco[0:tq] * y[0:tq]).astype(BF16)
        dpm_ref[:, O_CG:O_CG + CC] = (du * hc[0:tq]).astype(BF16)
        dpm_ref[:, O_HC:O_HC + CC] = (du * cg[0:tq]).astype(BF16)
        kraw = jnp.concatenate([kvp_ref[:, 0:NKV * HP], p_ref[:, O_K:O_K + NKV * HP]], axis=0)
        vraw = jnp.concatenate([kvp_ref[:, NKV * HP:], p_ref[:, O_V:O_V + NKV * HP]], axis=0)
        gqv, gkv = gq_ref[...], gk_ref[...]
        keys = _norm_keys(kraw, gkv)
        vb = [vraw[:, h * HP:(h + 1) * HP].astype(BF16) for h in range(NKV)]
        base_valid, c_io = _band_mask()
        lane = lax.broadcasted_iota(jnp.int32, (1, HP), 1)
        dgq, dgk, dsk = (jnp.zeros((1, HP), F32) for _ in range(3))
        dgao = jnp.zeros((1, NQ * HD), F32)
        for b in range(nb):
            lo = jnp.where(i * nb + b == 0, BLK, 0)
            valid = base_valid & (c_io >= lo)
            band = slice(b * BLK, b * BLK + 2 * BLK)
            blk = slice(b * BLK, (b + 1) * BLK)
            ra, aoh = _rms_fwd(ao_ref[blk, :], 1.0 / (NQ * HD))
            danb = _nt(dxb[blk], wo_ref[CC:MIXW, :])
            dgao = dgao + jnp.sum(danb * aoh, axis=0, keepdims=True)
            dao = _rms_bwd(danb, gao_ref[...], aoh, ra, 1.0 / (NQ * HD))
            dos = [dao[:, g // 2 * HP:(g // 2 + 1) * HP] for g in range(NQ)]
            dos = [(d if g % 2 == 0 else pltpu.roll(d, HD, 1)).astype(BF16) for g, d in enumerate(dos)]
            fwd = []
            for g in range(NQ):
                rq, qh = _rms_fwd(p_ref[blk, O_Q + g * HP:O_Q + (g + 1) * HP], 1.0 / HD)
                qs = (qh * (gqv * SCALE)).astype(BF16)
                fwd.append((rq, qh, qs) + _attn_probs(qs, keys[g // GRP][2][band], sk_ref[0, g], valid))
            dqs = []
            for h in range(NKV):
                khat, rk, kn = [a[band] for a in keys[h]]
                dss, prbs, qns, dobs = [], [], [], []
                for g in range(h * GRP, (h + 1) * GRP):
                    rq, qh, qs, pr, ps = fwd[g]
                    dob = dos[g]
                    dp = _nt(dob, vb[h][band])
                    delta = jnp.sum(pr * dp, axis=-1, keepdims=True)
                    dsb = (pr * (dp - delta)).astype(BF16)
                    dsk = dsk + jnp.where(lane == g, -jnp.sum(ps * delta, axis=0, keepdims=True), 0.0)
                    dqn = jnp.dot(dsb, kn, preferred_element_type=F32) * SCALE
                    dgq = dgq + jnp.sum(dqn * qh, axis=0, keepdims=True)
                    dqs.append(_rms_bwd(dqn, gqv, qh, rq, 1.0 / HD).astype(BF16))
                    dss.append(dsb)
                    prbs.append(pr.astype(BF16))
                    qns.append(qs)
                    dobs.append(dob)
                dkn = _tn(jnp.concatenate(dss, axis=0), jnp.concatenate(qns, axis=0))
                dv = _tn(jnp.concatenate(prbs, axis=0), jnp.concatenate(dobs, axis=0))
                dgk = dgk + jnp.sum(dkn * khat, axis=0, keepdims=True)
                acc_ref[band, h * HP:(h + 1) * HP] += _rms_bwd(dkn, gkv, khat, rk, 1.0 / HD)
                acc_ref[band, (NKV + h) * HP:(NKV + h + 1) * HP] += dv
            dpm_ref[blk, O_Q:O_K] = jnp.concatenate(dqs, axis=1)
        dgq_ref[...] += dgq
        dgk_ref[...] += dgk
        dsk_ref[...] += dsk
        dgao_ref[...] += dgao
        dkvh_ref[...] = acc_ref[0:BLK, :]
        dkvm_ref[...] = acc_ref[BLK:, :]

    prev8 = lambda col: pl.BlockSpec((8, CC), lambda i: (jnp.maximum(i * r8 - 1, 0), col))
    next8 = lambda col: pl.BlockSpec((8, CC), lambda i: (jnp.minimum((i + 1) * r8, t // 8 - 1), col))
    small = lambda n: pl.BlockSpec((1, n), lambda i: (0, 0))
    return pl.pallas_call(
        body, name="mixer_bwd", grid=(nt,),
        in_specs=[
            pl.BlockSpec((tq, D), lambda i: (i, 0)),
            pl.BlockSpec((8, D), lambda i: (jnp.minimum((i + 1) * r8, t // 8 - 1), 0)),
            pl.BlockSpec((tq, NP), lambda i: (i, 0)),
            prev8(O_CG // CC), prev8(O_HC // CC),
            next8(O_BG // CC), next8(O_CG // CC), next8(O_HC // CC),
            pl.BlockSpec((BLK, kvw), lambda i: (jnp.maximum(i * nb - 1, 0), O_K // kvw)),
            pl.BlockSpec((tq, NQ * HD), lambda i: (i, 0)),
            _const_spec((8, CC)), _const_spec((1, HP)), _const_spec((1, HP)),
            pl.BlockSpec(memory_space=pltpu.SMEM),
            _const_spec((1, CC)), _const_spec((1, NQ * HD)), _const_spec((MIXW, D)),
        ],
        out_specs=[
            pl.BlockSpec((tq, NMAIN), lambda i: (i, 0)),
            pl.BlockSpec((tq, kvw), lambda i: (i, 0)),
            pl.BlockSpec((BLK, kvw), lambda i: (i, 0)),
            pl.BlockSpec((8, CC), lambda i: (0, 0)), small(HP), small(HP), small(HP), small(CC), small(NQ * HD),
        ],
        out_shape=[
            jax.ShapeDtypeStruct((t, NMAIN), BF16), jax.ShapeDtypeStruct((t, kvw), F32),
            jax.ShapeDtypeStruct((nt * BLK, kvw), F32),
            jax.ShapeDtypeStruct((8, CC), F32), jax.ShapeDtypeStruct((1, HP), F32), jax.ShapeDtypeStruct((1, HP), F32),
            jax.ShapeDtypeStruct((1, HP), F32), jax.ShapeDtypeStruct((1, CC), F32),
            jax.ShapeDtypeStruct((1, NQ * HD), F32),
        ],
        scratch_shapes=[pltpu.VMEM((tq + BLK, kvw), F32)],
        compiler_params=_cparams(("arbitrary",)),
    )(dxm, dxm, proj, proj, proj, proj, proj, proj, proj, ao, cw, gq, gk, sinks, gco, gao, wo)


def _inproj_bwd(dpm, dkvm, dkvh, wpt, x, g1, dxm, tm):
    t = x.shape[0]
    kvw = 2 * NKV * HP
    nt = t // tm

    def body(dp_ref, dk_ref, dh_ref, w_ref, x_ref, g_ref, dxm_ref, dx_ref, dg_ref, dkv_ref):
        i = pl.program_id(0)

        @pl.when(i == 0)
        def _():
            dg_ref[...] = jnp.zeros_like(dg_ref)

        halo = jnp.where(i < nt - 1, dh_ref[...], 0.0)
        dkv_ref[0:tm - BLK, :] = dk_ref[0:tm - BLK, :].astype(BF16)
        dkv_ref[tm - BLK:tm, :] = (dk_ref[tm - BLK:tm, :] + halo).astype(BF16)
        dh = (jnp.dot(dp_ref[...], w_ref[0:NMAIN, :], preferred_element_type=F32)
              + jnp.dot(dkv_ref[...], w_ref[NMAIN:NP, :], preferred_element_type=F32))
        r, xh = _rms_fwd(x_ref[...], 1.0 / D)
        dg_ref[...] += jnp.sum(dh * xh, axis=0, keepdims=True)
        dx_ref[...] = dxm_ref[...] + _rms_bwd(dh, g_ref[...], xh, r, 1.0 / D)

    row = lambda w: pl.BlockSpec((tm, w), lambda i: (i, 0))
    return pl.pallas_call(
        body, name="inproj_bwd", grid=(nt,),
        in_specs=[row(NMAIN), row(kvw), pl.BlockSpec((BLK, kvw), lambda i: (jnp.minimum(i + 1, nt - 1), 0)),
                  _const_spec((NP, D)), row(D), _const_spec((1, D)), row(D)],
        out_specs=[row(D), pl.BlockSpec((1, D), lambda i: (0, 0)), row(kvw)],
        out_shape=[jax.ShapeDtypeStruct((t, D), F32), jax.ShapeDtypeStruct((1, D), F32),
                   jax.ShapeDtypeStruct((t, kvw), BF16)],
        compiler_params=_cparams(("arbitrary",)),
    )(dpm, dkvm, dkvh, wpt, x, g1, dxm)


def _rows_tile(rows, cap=512):
    for cand in range(min(rows, cap) // 16 * 16, 0, -16):
        if rows % cand == 0:
            return cand
    return rows


def _presum_halves(gs, theirs, core):
    n = len(gs)

    def body(c_ref, *refs):
        for g_ref, t_ref, o_ref in zip(refs[:n], refs[n:2 * n], refs[2 * n:]):
            o_ref[...] = (g_ref[...].astype(F32) + t_ref[...].astype(F32)).astype(BF16)

    half = lambda ta: pl.BlockSpec((None,) + ta.shape[1:], lambda k, c_ref: (k, 0, 0))
    own = lambda ta: pl.BlockSpec((None,) + ta.shape[1:], lambda k, c_ref: (k, c_ref[0], 0))
    return pl.pallas_call(
        body, name="presum",
        grid_spec=pltpu.PrefetchScalarGridSpec(
            num_scalar_prefetch=1, grid=(N_CHIPS,),
            in_specs=[own(ta) for ta in theirs] + [half(ta) for ta in theirs],
            out_specs=[half(ta) for ta in theirs]),
        out_shape=[jax.ShapeDtypeStruct(ta.shape, BF16) for ta in theirs],
        compiler_params=_cparams(("parallel",)),
    )(core, *gs, *theirs)


def _sum_chips(got, ps, chip):
    n = len(got)
    steps = 2

    def body(chip_ref, *refs):
        for c_ref, own_ref, o_ref in zip(refs[:n], refs[n:2 * n], refs[2 * n:]):
            acc = None
            for j in range(N_CHIPS):
                term = jnp.where(chip_ref[0] == j, own_ref[...], c_ref[j]).astype(F32)
                acc = term if acc is None else acc + term
            o_ref[...] = acc

    tile = lambda c: (c.shape[1] // steps, c.shape[2])
    return pl.pallas_call(
        body, name="chipsum",
        grid_spec=pltpu.PrefetchScalarGridSpec(
            num_scalar_prefetch=1, grid=(steps,),
            in_specs=[pl.BlockSpec((N_CHIPS,) + tile(c), lambda i, chip_ref: (0, i, 0)) for c in got]
            + [pl.BlockSpec((None,) + tile(c), lambda i, chip_ref: (chip_ref[0], i, 0)) for c in got],
            out_specs=[pl.BlockSpec(tile(c), lambda i, chip_ref: (i, 0)) for c in got]),
        out_shape=[jax.ShapeDtypeStruct(c.shape[1:], F32) for c in got],
        compiler_params=_cparams(("parallel",)),
    )(chip, *got, *ps)


def _adamw(w, g, m, v, name):
    rows, cols = w.shape
    tr = _rows_tile(rows, 256)
    c1 = 1.0 - ADAM_B1 ** ADAM_STEP
    c2 = 1.0 - ADAM_B2 ** ADAM_STEP

    def body(w_ref, g_ref, m_ref, v_ref, d_ref, mo_ref, vo_ref):
        gv = g_ref[...]
        mn = ADAM_B1 * m_ref[...] + (1.0 - ADAM_B1) * gv
        vn = ADAM_B2 * v_ref[...] + (1.0 - ADAM_B2) * (gv * gv)
        mo_ref[...] = mn
        vo_ref[...] = vn
        d_ref[...] = -ADAM_LR * ((mn / c1) / (jnp.sqrt(vn / c2) + ADAM_EPS) + ADAM_WD * w_ref[...])

    spec = pl.BlockSpec((tr, cols), lambda i: (i, 0))
    sds = jax.ShapeDtypeStruct((rows, cols), F32)
    return pl.pallas_call(
        body, name=name, grid=(rows // tr,), in_specs=[spec] * 4, out_specs=[spec] * 3, out_shape=[sds] * 3,
        compiler_params=_cparams(("parallel",)),
    )(w, g, m, v)


def _place():
    x, y, c = lax.axis_index("x"), lax.axis_index("y"), lax.axis_index("c")
    chips = [(1 - x, y), (x, 1 - y), (1 - x, 1 - y)]
    return x, y, c, chips


ANY = pl.BlockSpec(memory_space=pl.ANY)
DMA_ROWS = 64


def _pieces(shape):
    rows = shape[-2]
    step = DMA_ROWS if rows % DMA_ROWS == 0 else rows
    lead = [()]
    for n in shape[:-2]:
        lead = [i + (k,) for i in lead for k in range(n)]
    return [i + (pl.ds(r0, step),) for i in lead for r0 in range(0, rows, step)]


def _start_pieces(make, src, dst):
    for idx in _pieces(src.shape):
        make(src.at[idx], dst.at[idx]).start()


def _gather_body(srcs, outs, sems, layer, start):
    nw = len(srcs)
    ssem, rsem, fssem, frsem = sems
    x, y, c, chips = _place()
    kme = 2 * x + y

    def plane(j, w, to):
        return lambda s, d: pltpu.make_async_remote_copy(
            src_ref=s, dst_ref=d, send_sem=ssem.at[j, w], recv_sem=rsem.at[j, w], device_id=to,
            device_id_type=MESH)

    def passed(j, w):
        return lambda s, d: pltpu.make_async_remote_copy(
            src_ref=s, dst_ref=d, send_sem=fssem.at[j, w], recv_sem=frsem.at[j, w],
            device_id=(x, y, 1 - c), device_id_type=MESH)

    @pl.when(c == layer)
    def _():
        for j, (px, py) in enumerate(chips):
            for w in range(nw):
                start(plane(j, w, (px, py, c)), srcs[w], outs[w].at[kme])
        for j, (px, py) in enumerate(chips):
            for w in range(nw):
                got = outs[w].at[2 * px + py]
                plane(j, w, (px, py, c))(got, got).wait_recv()
                start(passed(j, w), got, got)
        for j, (px, py) in enumerate(chips):
            for w in range(nw):
                got = outs[w].at[2 * px + py]
                plane(j, w, (px, py, c))(got, got).wait_send()
                passed(j, w)(got, got).wait_send()

    @pl.when(c != layer)
    def _():
        for j, (px, py) in enumerate(chips):
            for w in range(nw):
                got = outs[w].at[2 * px + py]
                passed(j, w)(got, got).wait_recv()


def _handshake(peers):
    barrier = pltpu.get_barrier_semaphore()
    for peer in peers:
        pl.semaphore_signal(barrier, inc=1, device_id=peer, device_id_type=MESH)
    pl.semaphore_wait(barrier, len(peers))


def _handshake_all():
    x, y, c, _ = _place()
    _handshake([(x ^ (r >> 2), y ^ ((r >> 1) & 1), c ^ (r & 1)) for r in range(1, 8)])


def _gather_layer_async(blocks, layer, name, collective_id):
    hbm = pltpu.MemorySpace.HBM
    srcs = [jax.new_ref(b, memory_space=hbm) for b in blocks]
    outs = [jax.empty_ref(jax.ShapeDtypeStruct((N_CHIPS,) + b.shape, b.dtype), memory_space=hbm) for b in blocks]

    @pl.kernel(mesh=plsc.ScalarSubcoreMesh(axis_name="seq", num_cores=1), name=name,
               scratch_types=[pltpu.SemaphoreType.DMA((3, len(blocks)))] * 4,
               compiler_params=pltpu.CompilerParams(collective_id=collective_id))
    def launch(*sems):
        _handshake_all()
        _gather_body(srcs, outs, sems, layer, lambda make, s, d: make(s, d).start())

    launch()
    return [o[...] for o in outs]


def _swap_siblings(arrs, halves, name, collective_id=None):
    nw = len(arrs)
    out_sds = [jax.ShapeDtypeStruct((a.shape[0], a.shape[1] // 2, a.shape[2]) if halves else a.shape, a.dtype)
               for a in arrs]

    def exchange(srcs, outs, ssem, rsem, start):
        x, y, c, _ = _place()

        def give(w):
            return lambda s, d: pltpu.make_async_remote_copy(
                src_ref=s, dst_ref=d, send_sem=ssem.at[w], recv_sem=rsem.at[w], device_id=(x, y, 1 - c),
                device_id_type=MESH)

        for w in range(nw):
            hr = outs[w].shape[1]
            start(give(w), srcs[w].at[:, pl.ds((1 - c) * hr, hr)] if halves else srcs[w], outs[w])
        for w in range(nw):
            give(w)(outs[w], outs[w]).wait()

    if collective_id is None:
        def body(*refs):
            exchange(refs[:nw], refs[nw:2 * nw], *refs[2 * nw:], _start_pieces)

        return pl.pallas_call(
            body, name=name, in_specs=[ANY] * nw, out_specs=[ANY] * nw, out_shape=out_sds,
            scratch_shapes=[pltpu.SemaphoreType.DMA((nw,))] * 2,
            compiler_params=_cparams(has_side_effects=True),
        )(*arrs)

    hbm = pltpu.MemorySpace.HBM
    srcs = [jax.new_ref(a, memory_space=hbm) for a in arrs]
    outs = [jax.empty_ref(sds, memory_space=hbm) for sds in out_sds]

    @pl.kernel(mesh=plsc.ScalarSubcoreMesh(axis_name="seq", num_cores=1), name=name,
               scratch_types=[pltpu.SemaphoreType.DMA((nw,))] * 2,
               compiler_params=pltpu.CompilerParams(collective_id=collective_id))
    def launch(ssem, rsem):
        x, y, c, _ = _place()
        _handshake([(x, y, 1 - c)])
        exchange(srcs, outs, ssem, rsem, lambda make, s, d: make(s, d).start())

    launch()
    return [o[...] for o in outs]


def _scatter_body(srcs, outs, sems, start):
    nw = len(srcs)
    ssem, rsem = sems
    x, y, c, chips = _place()
    kme = 2 * x + y

    def give(j, w, to):
        return lambda s, d: pltpu.make_async_remote_copy(
            src_ref=s, dst_ref=d, send_sem=ssem.at[j, w], recv_sem=rsem.at[j, w], device_id=to,
            device_id_type=MESH)

    for j, (px, py) in enumerate(chips):
        for w in range(nw):
            start(give(j, w, (px, py, c)), srcs[w].at[2 * px + py], outs[w].at[kme])
    for j, (px, py) in enumerate(chips):
        for w in range(nw):
            got = outs[w].at[2 * px + py]
            give(j, w, (px, py, c))(got, got).wait_recv()
    for j, (px, py) in enumerate(chips):
        for w in range(nw):
            sent = srcs[w].at[2 * px + py]
            give(j, w, (px, py, c))(sent, sent).wait_send()


def _scatter_chips_async(ps, name, collective_id):
    hbm = pltpu.MemorySpace.HBM
    srcs = [jax.new_ref(p, memory_space=hbm) for p in ps]
    outs = [jax.empty_ref(jax.ShapeDtypeStruct(p.shape, p.dtype), memory_space=hbm) for p in ps]

    @pl.kernel(mesh=plsc.ScalarSubcoreMesh(axis_name="seq", num_cores=1), name=name,
               scratch_types=[pltpu.SemaphoreType.DMA((3, len(ps)))] * 2,
               compiler_params=pltpu.CompilerParams(collective_id=collective_id))
    def launch(*sems):
        _handshake_all()
        _scatter_body(srcs, outs, sems, lambda make, s, d: make(s, d).start())

    launch()
    return [o[...] for o in outs]


def _allreduce_small(v):
    rows = v.shape[0]

    def body(v_ref, o_ref, buf, ssem, rsem):
        x, y, c, _ = _place()
        me = 4 * x + 2 * y + c
        buf[me] = v_ref[...]
        sends = []
        for r in range(1, 8):
            peer = (x ^ (r >> 2), y ^ ((r >> 1) & 1), c ^ (r & 1))
            cp = pltpu.make_async_remote_copy(
                src_ref=v_ref, dst_ref=buf.at[me], send_sem=ssem.at[r - 1], recv_sem=rsem.at[r - 1],
                device_id=peer, device_id_type=MESH)
            cp.start()
            sends.append(cp)
        for r in range(1, 8):
            src = me ^ r
            pltpu.make_async_remote_copy(
                src_ref=v_ref, dst_ref=buf.at[src], send_sem=ssem.at[r - 1], recv_sem=rsem.at[r - 1],
                device_id=(x, y, c), device_id_type=MESH).wait_recv()
        for cp in sends:
            cp.wait_send()
        acc = buf[0]
        for d in range(1, 8):
            acc = acc + buf[d]
        o_ref[...] = acc

    vm = pl.BlockSpec(memory_space=pltpu.VMEM)
    return pl.pallas_call(
        body, name="allreduce_small", in_specs=[vm], out_specs=vm,
        out_shape=jax.ShapeDtypeStruct(v.shape, F32),
        scratch_shapes=[pltpu.VMEM((8, rows, 128), F32), pltpu.SemaphoreType.DMA((7,)),
                        pltpu.SemaphoreType.DMA((7,))],
        compiler_params=_cparams(has_side_effects=True),
    )(v)


def _t(w):
    return jnp.swapaxes(w, -1, -2)


def _count(shape):
    n = 1
    for s in shape:
        n *= s
    return n


def _pack_rows(arrs):
    flat = [jnp.pad(a.reshape(-1), (0, (-_count(a.shape)) % 128)) for a in arrs]
    v = jnp.concatenate(flat)
    rows = -(-v.shape[0] // (8 * 128)) * 8
    return jnp.pad(v, (0, rows * 128 - v.shape[0])).reshape(rows, 128)


def kernel(x, norm1_g, w_in, conv_w, q_norm_g, k_norm_g, sinks, conv_out_g, attn_out_g, w_o, norm2_g, w_gate, w_up, w_down, loss_target, m_norm1_g, m_w_in, m_conv_w, m_q_norm_g, m_k_norm_g, m_sinks, m_conv_out_g, m_attn_out_g, m_w_o, m_norm2_g, m_w_gate, m_w_up, m_w_down, v_norm1_g, v_w_in, v_conv_w, v_q_norm_g, v_k_norm_g, v_sinks, v_conv_out_g, v_attn_out_g, v_w_o, v_norm2_g, v_w_gate, v_w_up, v_w_down):
    depth = w_in.shape[0]
    t = x.shape[1]
    xs = x.reshape(t, D)
    tgt = loss_target.reshape(t, D)
    xi, yi = lax.axis_index("x"), lax.axis_index("y")
    kme = 2 * xi + yi
    tm = min(512, t)
    tq = min(512, t)
    tf = min(256, t)
    tw = min(1024, t)

    cwp = jnp.pad(conv_w.reshape(depth * 3, CC // N_CHIPS), ((0, 8 - depth * 3), (0, 0)))
    own_f = [jnp.concatenate([_t(w_gate[l]), _t(w_up[l]), w_down[l]], axis=0).astype(BF16) for l in range(depth)]
    own_o = [w_o[l].astype(BF16) for l in range(depth)]
    own_i = [_t(w_in[l]).astype(BF16) for l in range(depth)]
    mine = lambda got, own: lax.dynamic_update_index_in_dim(got, own, kme, 0)
    got0 = _gather_layer_async([own_i[0], own_o[0], cwp], 0, "gather_in0_seq", collective_id=14)
    (got_i0, got_o0, got_cw), own_f, own_o, own_i = lax.optimization_barrier((got0, own_f, own_o, own_i))
    gf0_in = lax.optimization_barrier((own_f[0], got_i0))[0]
    (got_f0,) = _gather_layer_async([gf0_in], 0, "gather_ffn0_seq", collective_id=6)
    cw_full = mine(got_cw, cwp).transpose(1, 0, 2).reshape(8, CC)[:depth * 3].reshape(depth, 3, CC)

    chip = kme.reshape(1).astype(jnp.int32)

    def layer_params(l, got_o):
        return dict(
            wo=mine(got_o, own_o[l]).reshape(MIXW, D),
            cw=jnp.pad(cw_full[l], ((0, 5), (0, 0))),
            g1=norm1_g[l].reshape(1, D), g2=norm2_g[l].reshape(1, D),
            gq=jnp.pad(q_norm_g[l], (0, HP - HD)).reshape(1, HP), gk=jnp.pad(k_norm_g[l], (0, HP - HD)).reshape(1, HP),
            sk=sinks[l].reshape(1, NQ), gco=conv_out_g[l].reshape(1, CC),
            gao=attn_out_g[l].reshape(1, NQ * HD))

    saved, layers = [], []
    cur = xs
    for l in range(depth):
        x_in = cur
        if l == 0:
            got_i, p = got_i0, layer_params(0, got_o0)
        else:
            got_f1, got_o1, got_i = lax.optimization_barrier((got_l1, cur))[0]
            p = layer_params(1, got_o1)
        proj, h, p["wpt"] = _inproj_fwd(cur, p["g1"], got_i, own_i[l], chip, tm)
        xm, mix, ao = _mixer_fwd(proj, cur, p["cw"], p["gq"], p["gk"], p["sk"], p["gco"], p["gao"], p["wo"], tq)
        if l == 0:
            got_f0 = lax.optimization_barrier((got_f0, xm))[0]
            l1_in = lax.optimization_barrier(([own_f[1], own_o[1], own_i[1]], got_f0))[0]
            got_l1 = _gather_layer_async(l1_in, 1, "gather_layer1_seq", collective_id=1)
        p["gf"] = mine(got_f0 if l == 0 else got_f1, own_f[l])
        layers.append(p)
        if l < depth - 1:
            cur, a, b, h2 = _ffn_fwd(xm, p["g2"], p["gf"], tm)
        else:
            lpart, dy, a, b, h2 = _ffn_fwd(xm, p["g2"], p["gf"], tm, tgt)
        saved.append(dict(x=x_in, proj=proj, h=h, xm=xm, mix=mix, ao=ao, a=a, b=b, h2=h2))

    nt = t // tq
    ci = lax.axis_index("c")
    core = ci.reshape(1).astype(jnp.int32)
    rbig = [dict() for _ in range(depth)]
    gsmall = [None] * depth

    def after_(vals, after):
        return vals if after is None else lax.optimization_barrier((vals, after))[0]

    def reduce_1(gs, tag, ids):
        return gs, _swap_siblings(gs, True, f"swap_halves_{tag}_seq", ids[0]), tag, ids

    def reduce_2(state, after):
        gs, theirs, tag, ids = state
        ps = _presum_halves(gs, after_(theirs, after), core)
        return ps, _scatter_chips_async(ps, f"scatter_{tag}_seq", ids[1]), tag, ids

    def reduce_3(state, after):
        ps, got, tag, ids = state
        r_mine = _sum_chips(after_(got, after), ps, chip)
        return r_mine, _swap_siblings(r_mine, False, f"swap_reduced_{tag}" + ("_seq" if ids[2] else ""), ids[2])

    def reduce_4(state, after):
        r_mine, r_theirs = state
        return [jnp.where(ci == 0, jnp.concatenate([a, b], axis=0), jnp.concatenate([b, a], axis=0))
                for a, b in zip(r_mine, after_(r_theirs, after))]

    ids = {"ffn1": (7, 4, 8), "in1": (9, 5, 10), "ffn0": (11, 2, 12), "in0": (13, 3, None)}
    in_2 = None
    handed = {}
    for l in reversed(range(depth)):
        p, s = layers[l], saved[l]
        dxm, da, db, hm, dg2 = _ffn_bwd(dy, s["xm"], p["g2"], s["a"], s["b"], p["gf"], tf)
        if in_2 is not None:
            in_2 = reduce_2(in_2, dxm)
        g_wg = _wgrad_blocks(da, s["h2"], tw, "wgrad_gate")
        g_wu = _wgrad_blocks(db, s["h2"], tw, "wgrad_up")
        g_wd = _wgrad_blocks(hm, dy, tw, "wgrad_down")
        if in_2 is not None:
            handed[f"in{l + 1}"] = reduce_3(in_2, g_wd)
        ffn_1 = reduce_1([g_wg, g_wu, g_wd], f"ffn{l}", ids[f"ffn{l}"])
        dxm = after_(dxm, g_wd)
        dpm, dkvm, dkvh, dcw, dgq, dgk, dsk, dgco, dgao = _mixer_bwd(
            dxm, s["proj"], s["ao"], p["cw"], p["gq"], p["gk"], p["sk"], p["gco"], p["gao"], p["wo"], tq)
        ffn_2 = reduce_2(ffn_1, dpm)
        g_o = _wgrad(s["mix"], dxm, tw, "wgrad_o")
        dx, dg1, dkv = _inproj_bwd(dpm, dkvm, dkvh, p["wpt"], s["x"], p["g1"], dxm, tq)
        g_in = jnp.concatenate(
            [_wgrad(dpm, s["h"], tw, "wgrad_in_main", [(0, 0, O_Q)] + _head_rows(O_Q, NQ)),
             _wgrad(dkv, s["h"], tw, "wgrad_in_kv", _head_rows(0, 2 * NKV))], axis=0)
        dy = after_(dx, (g_in, g_o))
        gsmall[l] = dict(g1=dg1, cw=dcw[:3], gq=dgq[0, :HD], gk=dgk[0, :HD], sk=dsk[0, :NQ], gco=dgco,
                         gao=dgao, g2=dg2)
        handed[f"ffn{l}"] = reduce_3(ffn_2, g_in)
        in_2 = reduce_1([g_in.reshape(N_CHIPS, -1, D), g_o.reshape(N_CHIPS, -1, D)], f"in{l}", ids[f"in{l}"])
    grad_x = dy.reshape(x.shape)

    small_shapes = dict(g1=(D,), cw=(3, CC), gq=(HD,), gk=(HD,), sk=(NQ,), gco=(CC,), gao=(NQ * HD,), g2=(D,))
    red = _allreduce_small(_pack_rows([gsmall[l][n] for l in range(depth) for n in small_shapes]
                                      + [lpart[0:1, 0:1]])).reshape(-1)
    red_small, offs = {n: [] for n in small_shapes}, 0
    for l in range(depth):
        for n, shp in small_shapes.items():
            cnt = _count(shp)
            red_small[n].append(red[offs:offs + cnt].reshape(shp))
            offs += -(-cnt // 128) * 128
    loss = red[offs]
    g_small = {n: jnp.stack(v) for n, v in red_small.items()}
    g_cw = lax.dynamic_slice_in_dim(g_small["cw"], kme * (CC // N_CHIPS), CC // N_CHIPS, axis=2)

    weights = [norm1_g, w_in, conv_w, q_norm_g, k_norm_g, sinks, conv_out_g, attn_out_g, w_o, norm2_g, w_gate,
               w_up, w_down]
    moms = [m_norm1_g, m_w_in, m_conv_w, m_q_norm_g, m_k_norm_g, m_sinks, m_conv_out_g, m_attn_out_g, m_w_o,
            m_norm2_g, m_w_gate, m_w_up, m_w_down]
    vars_ = [v_norm1_g, v_w_in, v_conv_w, v_q_norm_g, v_k_norm_g, v_sinks, v_conv_out_g, v_attn_out_g, v_w_o,
             v_norm2_g, v_w_gate, v_w_up, v_w_down]
    n_w = len(weights)
    big_idx = dict(zip(("in", "o", "g", "u", "d"), (1, 8, 10, 11, 12)))
    small_idx = [n for n in range(n_w) if n not in big_idx.values()]
    grads, deltas, new_m, new_v = [None] * n_w, [None] * n_w, [None] * n_w, [None] * n_w
    for n, g in zip(small_idx, (g_small["g1"], g_cw, g_small["gq"], g_small["gk"], g_small["sk"], g_small["gco"],
                                g_small["gao"], g_small["g2"])):
        grads[n] = g

    def update_big(name):
        n = big_idx[name]
        g = jnp.stack([rbig[l][name] for l in range(depth)])
        flip = g.shape != weights[n].shape
        rows2d = lambda a3: (_t(a3) if flip else a3).reshape(-1, D)
        res = _adamw(rows2d(weights[n]), g.reshape(-1, D), rows2d(moms[n]), rows2d(vars_[n]), f"adamw_{n}")
        res = [g] + [r.reshape(g.shape) for r in res]
        grads[n], deltas[n], new_m[n], new_v[n] = [_t(r) for r in res] if flip else res

    for l in range(depth):
        rbig[l]["g"], rbig[l]["u"], rbig[l]["d"] = reduce_4(handed[f"ffn{l}"], red)
    rbig[1]["in"], rbig[1]["o"] = reduce_4(handed["in1"], red)
    update_big("g")
    in_2 = reduce_2(in_2, new_v[big_idx["g"]])
    update_big("u")
    update_big("d")
    rbig[0]["in"], rbig[0]["o"] = reduce_4(reduce_3(in_2, new_v[big_idx["d"]]), None)
    for name in ("in", "o"):
        update_big(name)
    res = _adamw(*[_pack_rows([arrs[n] for n in small_idx]) for arrs in (weights, grads, moms, vars_)],
                 "adamw_small")
    offs = 0
    for n in small_idx:
        shp = weights[n].shape
        cnt = _count(shp)
        deltas[n], new_m[n], new_v[n] = [r.reshape(-1)[offs:offs + cnt].reshape(shp) for r in res]
        offs += -(-cnt // 128) * 128
    return (loss, grad_x, *grads, *deltas, *new_m, *new_v)
```

```python
import jax
import jax.numpy as jnp
from jax import lax
from jax.experimental import pallas as pl
from jax.experimental.pallas import tpu as pltpu
from jax.experimental.pallas import tpu_sc as plsc

F32 = jnp.float32
BF16 = jnp.bfloat16

D = 1024
CC = 512
NQ = 8
NKV = 2
HD = 64
HP = 128
GRP = NQ // NKV
FF = 2816
FFB = FF // 4
BLK = 128
EPS = 1e-6
NEG = -1e30
SCALE = HD ** -0.5
O_BG, O_CG, O_HC, O_Q = 0, CC, 2 * CC, 3 * CC
O_K = O_Q + NQ * HP
O_V = O_K + NKV * HP
NP = O_V + NKV * HP
NMAIN = O_K
MIXW = CC + NQ * HD
N_CHIPS = 4
VMEM_LIMIT = 56 * 1024 * 1024
MESH = pl.DeviceIdType.MESH

ADAM_LR, ADAM_B1, ADAM_B2, ADAM_EPS, ADAM_WD, ADAM_STEP = 0.001, 0.9, 0.999, 1e-08, 0.01, 10


def _cparams(sem=None, **kw):
    if sem is not None:
        kw["dimension_semantics"] = sem
    return pltpu.CompilerParams(vmem_limit_bytes=VMEM_LIMIT, **kw)


def _const_spec(shape):
    nd = len(shape)
    return pl.BlockSpec(shape, lambda *_: (0,) * nd, pipeline_mode=pl.Buffered(1))


def _nt(a, b):
    return lax.dot_general(a, b, (((1,), (1,)), ((), ())), preferred_element_type=F32)


def _tn(a, b):
    return lax.dot_general(a, b, (((0,), (0,)), ((), ())), preferred_element_type=F32)


def _rms_fwd(x, inv_n):
    r = lax.rsqrt(jnp.sum(x * x, axis=-1, keepdims=True) * inv_n + EPS)
    return r, x * r


def _rms_bwd(dy, g, xh, r, inv_n):
    dxh = dy * g
    return r * (dxh - xh * (jnp.sum(dxh * xh, axis=-1, keepdims=True) * inv_n))


W_IN_ROWS = 3 * CC + (NQ + 2 * NKV) * HD
W_IN_BLOCK = W_IN_ROWS // N_CHIPS


def _padded_row(row):
    return row + max(row - O_Q, 0) // HD * (HP - HD)


def _w_in_pieces(k):
    first = k * W_IN_BLOCK
    plain = min(max(O_Q - first, 0), W_IN_BLOCK)
    pieces = [(0, first, plain)] if plain else []
    return pieces + [(r, _padded_row(first + r), HD) for r in range(plain, W_IN_BLOCK, HD)]


def _inproj_fwd(x, g1, gi, own_i, chip, tm):
    t = x.shape[0]

    def body(chip_ref, x_ref, g_ref, gi_ref, own_ref, p_ref, h_ref, w_ref, sem):
        @pl.when(pl.program_id(0) == 0)
        def _():
            for k in range(N_CHIPS):
                for src, dst, rows in _w_in_pieces(k):
                    @pl.when(chip_ref[0] == k)
                    def _():
                        pltpu.make_async_copy(own_ref.at[pl.ds(src, rows)], w_ref.at[pl.ds(dst, rows)], sem).start()

                    @pl.when(chip_ref[0] != k)
                    def _():
                        pltpu.make_async_copy(gi_ref.at[k, pl.ds(src, rows)], w_ref.at[pl.ds(dst, rows)], sem).start()
            for slot in range(NQ + 2 * NKV):
                w_ref[O_Q + slot * HP + HD:O_Q + (slot + 1) * HP, :] = jnp.zeros((HP - HD, D), BF16)
            landed = w_ref.at[pl.ds(0, W_IN_ROWS)]
            pltpu.make_async_copy(landed, landed, sem).wait()

        _, xh = _rms_fwd(x_ref[...], 1.0 / D)
        h = (xh * g_ref[...]).astype(BF16)
        h_ref[...] = h
        p_ref[...] = _nt(h, w_ref[...])

    const = lambda shape: pl.BlockSpec(shape, lambda i, c: (0,) * len(shape))
    return pl.pallas_call(
        body, name="inproj_fwd",
        grid_spec=pltpu.PrefetchScalarGridSpec(
            num_scalar_prefetch=1, grid=(t // tm,),
            in_specs=[pl.BlockSpec((tm, D), lambda i, c: (i, 0)), const((1, D)), ANY, ANY],
            out_specs=[pl.BlockSpec((tm, NP), lambda i, c: (i, 0)), pl.BlockSpec((tm, D), lambda i, c: (i, 0)),
                       const((NP, D))],
            scratch_shapes=[pltpu.SemaphoreType.DMA(())]),
        out_shape=[jax.ShapeDtypeStruct((t, NP), F32), jax.ShapeDtypeStruct((t, D), BF16),
                   jax.ShapeDtypeStruct((NP, D), BF16)],
        compiler_params=_cparams(("arbitrary",)),
    )(chip, x, g1, gi, own_i)


def _band_mask():
    r_io = lax.broadcasted_iota(jnp.int32, (BLK, 2 * BLK), 0)
    c_io = lax.broadcasted_iota(jnp.int32, (BLK, 2 * BLK), 1)
    return (c_io > r_io) & (c_io <= r_io + BLK), c_io


def _conv_taps(uf, n):
    u1 = pltpu.roll(uf, 1, 0)[8:8 + n]
    u2 = pltpu.roll(uf, 2, 0)[8:8 + n]
    return u1, u2


def _attn_probs(qs, kband, sink, valid):
    s = jnp.where(valid, _nt(qs, kband), NEG)
    m = jnp.maximum(jnp.max(s, axis=-1, keepdims=True), sink)
    p = jnp.exp(s - m)
    es = jnp.exp(sink - m)
    inv = 1.0 / (jnp.sum(p, axis=-1, keepdims=True) + es)
    return p * inv, es * inv


def _norm_keys(kraw, gk):
    out = []
    for h in range(NKV):
        kh = kraw[:, h * HP:(h + 1) * HP]
        rk, khat = _rms_fwd(kh, 1.0 / HD)
        out.append((khat, rk, (khat * gk).astype(BF16)))
    return out


def _mixer_fwd(proj, x, cw, gq, gk, sinks, gco, gao, wo, tq):
    t = proj.shape[0]
    nb = tq // BLK
    r8 = tq // 8

    def body(p_ref, cgp_ref, hcp_ref, kvp_ref, x_ref, cw_ref, gq_ref, gk_ref, sk_ref, gco_ref, gao_ref,
             wo_ref, xm_ref, mix_ref, ao_ref, aop_ref):
        i = pl.program_id(0)
        cg = p_ref[:, O_CG:O_CG + CC]
        hc = p_ref[:, O_HC:O_HC + CC]
        u = cg * hc
        up = jnp.where(i > 0, cgp_ref[...] * hcp_ref[...], 0.0)
        u1, u2 = _conv_taps(jnp.concatenate([up, u], axis=0), tq)
        y = cw_ref[0:1, :] * u2 + cw_ref[1:2, :] * u1 + cw_ref[2:3, :] * u
        co = p_ref[:, O_BG:O_BG + CC] * y
        _, coh = _rms_fwd(co, 1.0 / CC)
        cn = coh * gco_ref[...]
        kraw = jnp.concatenate([kvp_ref[:, 0:NKV * HP], p_ref[:, O_K:O_K + NKV * HP]], axis=0)
        vraw = jnp.concatenate([kvp_ref[:, NKV * HP:], p_ref[:, O_V:O_V + NKV * HP]], axis=0)
        keys = _norm_keys(kraw, gk_ref[...])
        vb = [vraw[:, h * HP:(h + 1) * HP].astype(BF16) for h in range(NKV)]
        base_valid, c_io = _band_mask()
        gqs = gq_ref[...] * SCALE
        for b in range(nb):
            lo = jnp.where(i * nb + b == 0, BLK, 0)
            valid = base_valid & (c_io >= lo)
            for g in range(NQ):
                h = g // GRP
                qg = p_ref[b * BLK:(b + 1) * BLK, O_Q + g * HP:O_Q + (g + 1) * HP]
                _, qh = _rms_fwd(qg, 1.0 / HD)
                qs = (qh * gqs).astype(BF16)
                pr, _ = _attn_probs(qs, keys[h][2][b * BLK:b * BLK + 2 * BLK], sk_ref[0, g], valid)
                aop_ref[b * BLK:(b + 1) * BLK, g * HP:(g + 1) * HP] = jnp.dot(
                    pr.astype(BF16), vb[h][b * BLK:b * BLK + 2 * BLK], preferred_element_type=F32)
        for j in range(NQ // 2):
            ao_ref[:, j * HP:(j + 1) * HP] = (aop_ref[:, 2 * j * HP:(2 * j + 1) * HP]
                                              + pltpu.roll(aop_ref[:, (2 * j + 1) * HP:(2 * j + 2) * HP], HD, 1))
        _, aoh = _rms_fwd(ao_ref[...], 1.0 / (NQ * HD))
        an = aoh * gao_ref[...]
        mix = jnp.concatenate([cn, an], axis=1).astype(BF16)
        mix_ref[...] = mix
        xm_ref[...] = x_ref[...] + jnp.dot(mix, wo_ref[...], preferred_element_type=F32)

    prev8 = lambda col: pl.BlockSpec((8, CC), lambda i: (jnp.maximum(i * r8 - 1, 0), col))
    return pl.pallas_call(
        body, name="mixer_fwd", grid=(t // tq,),
        in_specs=[
            pl.BlockSpec((tq, NP), lambda i: (i, 0)),
            prev8(O_CG // CC), prev8(O_HC // CC),
            pl.BlockSpec((BLK, 2 * NKV * HP), lambda i: (jnp.maximum(i * nb - 1, 0), O_K // (2 * NKV * HP))),
            pl.BlockSpec((tq, D), lambda i: (i, 0)),
            _const_spec((8, CC)), _const_spec((1, HP)), _const_spec((1, HP)),
            pl.BlockSpec(memory_space=pltpu.SMEM),
            _const_spec((1, CC)), _const_spec((1, NQ * HD)), _const_spec((MIXW, D)),
        ],
        out_specs=[pl.BlockSpec((tq, D), lambda i: (i, 0)), pl.BlockSpec((tq, MIXW), lambda i: (i, 0)),
                   pl.BlockSpec((tq, NQ * HD), lambda i: (i, 0))],
        out_shape=[jax.ShapeDtypeStruct((t, D), F32), jax.ShapeDtypeStruct((t, MIXW), BF16),
                   jax.ShapeDtypeStruct((t, NQ * HD), F32)],
        scratch_shapes=[pltpu.VMEM((tq, NQ * HP), F32)],
        compiler_params=_cparams(("parallel",)),
    )(proj, proj, proj, proj, x, cw, gq, gk, sinks, gco, gao, wo)


def _ffn_weight_specs():
    return [pl.BlockSpec((N_CHIPS, FFB, D), lambda i, j=j: (0, j, 0), pipeline_mode=pl.Buffered(1))
            for j in range(3)]


def _ffn_fwd(xm, g2, gf, tm, tgt=None):
    t = xm.shape[0]
    last = tgt is not None

    def body(x_ref, g_ref, wg_ref, wu_ref, wd_ref, *rest):
        t_ref, rest = (rest[0], rest[1:]) if last else (None, rest)
        l_ref, rest = (rest[0], rest[1:]) if last else (None, rest)
        xo_ref, a_ref, b_ref, h2_ref = rest
        xv = x_ref[...]
        _, xh = _rms_fwd(xv, 1.0 / D)
        h2 = (xh * g_ref[...]).astype(BF16)
        h2_ref[...] = h2
        acc = xv
        for k in range(N_CHIPS):
            a = _nt(h2, wg_ref[k])
            b = _nt(h2, wu_ref[k])
            a_ref[k] = a.astype(BF16)
            b_ref[k] = b.astype(BF16)
            hm = (a * jax.nn.sigmoid(a) * b).astype(BF16)
            acc = acc + jnp.dot(hm, wd_ref[k], preferred_element_type=F32)
        if last:
            @pl.when(pl.program_id(0) == 0)
            def _():
                l_ref[...] = jnp.zeros_like(l_ref)

            e = acc - t_ref[...]
            xo_ref[...] = e * (1.0 / D)
            l_ref[...] += jnp.sum(jnp.sum(e * e, axis=-1, keepdims=True), axis=0, keepdims=True) * (0.5 / D)
        else:
            xo_ref[...] = acc

    row = lambda w: pl.BlockSpec((tm, w), lambda i: (i, 0))
    blk = pl.BlockSpec((N_CHIPS, tm, FFB), lambda i: (0, i, 0))
    bsd = jax.ShapeDtypeStruct((N_CHIPS, t, FFB), BF16)
    return pl.pallas_call(
        body, name="ffn_fwd_loss" if last else "ffn_fwd", grid=(t // tm,),
        in_specs=[row(D), _const_spec((1, D))] + _ffn_weight_specs() + ([row(D)] if last else []),
        out_specs=([pl.BlockSpec((8, 128), lambda i: (0, 0))] if last else []) + [row(D), blk, blk, row(D)],
        out_shape=([jax.ShapeDtypeStruct((8, 128), F32)] if last else [])
        + [jax.ShapeDtypeStruct((t, D), F32), bsd, bsd, jax.ShapeDtypeStruct((t, D), BF16)],
        compiler_params=_cparams(("arbitrary" if last else "parallel",)),
    )(*((xm, g2, gf, gf, gf) + ((tgt,) if last else ())))


def _ffn_bwd(dy, xm, g2, a, b, gf, tm):
    t = dy.shape[0]

    def body(dy_ref, x_ref, g_ref, a_ref, b_ref, wg_ref, wu_ref, wd_ref, dx_ref, da_ref, db_ref, hm_ref, dg_ref):
        @pl.when(pl.program_id(0) == 0)
        def _():
            dg_ref[...] = jnp.zeros_like(dg_ref)

        dyv = dy_ref[...]
        dyb = dyv.astype(BF16)
        dh2 = jnp.zeros_like(dyv)
        for k in range(N_CHIPS):
            dhm = _nt(dyb, wd_ref[k])
            av = a_ref[k].astype(F32)
            bv = b_ref[k].astype(F32)
            sig = jax.nn.sigmoid(av)
            sil = av * sig
            hm_ref[k] = (sil * bv).astype(BF16)
            da = (dhm * bv * (sig * (1.0 + av * (1.0 - sig)))).astype(BF16)
            db = (dhm * sil).astype(BF16)
            da_ref[k] = da
            db_ref[k] = db
            dh2 = (dh2 + jnp.dot(da, wg_ref[k], preferred_element_type=F32)
                   + jnp.dot(db, wu_ref[k], preferred_element_type=F32))
        r, xh = _rms_fwd(x_ref[...], 1.0 / D)
        dg_ref[...] += jnp.sum(dh2 * xh, axis=0, keepdims=True)
        dx_ref[...] = dyv + _rms_bwd(dh2, g_ref[...], xh, r, 1.0 / D)

    row = lambda w: pl.BlockSpec((tm, w), lambda i: (i, 0))
    blk = pl.BlockSpec((N_CHIPS, tm, FFB), lambda i: (0, i, 0))
    bsd = jax.ShapeDtypeStruct((N_CHIPS, t, FFB), BF16)
    return pl.pallas_call(
        body, name="ffn_bwd", grid=(t // tm,),
        in_specs=[row(D), row(D), _const_spec((1, D)), blk, blk] + _ffn_weight_specs(),
        out_specs=[row(D), blk, blk, blk, pl.BlockSpec((1, D), lambda i: (0, 0))],
        out_shape=[jax.ShapeDtypeStruct((t, D), F32), bsd, bsd, bsd, jax.ShapeDtypeStruct((1, D), F32)],
        compiler_params=_cparams(("arbitrary",)),
    )(dy, xm, g2, a, b, gf, gf, gf)


def _wgrad_blocks(a, b, tt, name):
    _, t, rows = a.shape
    cols = b.shape[1]
    nsteps = t // tt

    def body(a_ref, b_ref, o_ref, acc_ref):
        s = pl.program_id(0)

        @pl.when(s == 0)
        def _():
            acc_ref[...] = jnp.zeros_like(acc_ref)

        bv = b_ref[...].astype(BF16)
        for k in range(N_CHIPS):
            acc_ref[k] += _tn(a_ref[k], bv)

        @pl.when(s == nsteps - 1)
        def _():
            o_ref[...] = acc_ref[...].astype(BF16)

    return pl.pallas_call(
        body, name=name, grid=(nsteps,),
        in_specs=[pl.BlockSpec((N_CHIPS, tt, rows), lambda s: (0, s, 0)), pl.BlockSpec((tt, cols), lambda s: (s, 0))],
        out_specs=pl.BlockSpec((N_CHIPS, rows, cols), lambda s: (0, 0, 0)),
        out_shape=jax.ShapeDtypeStruct((N_CHIPS, rows, cols), BF16),
        scratch_shapes=[pltpu.VMEM((N_CHIPS, rows, cols), F32)],
        compiler_params=_cparams(("arbitrary",)),
    )(a, b)


def _head_rows(first, n_heads):
    return [(first + g * HD, first + g * HP, HD) for g in range(n_heads)]


def _wgrad(a, b, tt, name, pieces=None):
    t, k = a.shape
    n = b.shape[1]
    nsteps = t // tt
    pieces = pieces or [(0, 0, k)]
    rows = sum(p[2] for p in pieces)

    def body(a_ref, b_ref, o_ref, acc_ref):
        s = pl.program_id(0)

        @pl.when(s == 0)
        def _():
            acc_ref[...] = jnp.zeros_like(acc_ref)

        acc_ref[...] += _tn(a_ref[...].astype(BF16), b_ref[...].astype(BF16))

        @pl.when(s == nsteps - 1)
        def _():
            for dst, src, size in pieces:
                o_ref[dst:dst + size, :] = acc_ref[src:src + size, :].astype(BF16)

    return pl.pallas_call(
        body, name=name, grid=(nsteps,),
        in_specs=[pl.BlockSpec((tt, k), lambda s: (s, 0)), pl.BlockSpec((tt, n), lambda s: (s, 0))],
        out_specs=pl.BlockSpec((rows, n), lambda s: (0, 0)),
        out_shape=jax.ShapeDtypeStruct((rows, n), BF16),
        scratch_shapes=[pltpu.VMEM((k, n), F32)],
        compiler_params=_cparams(("arbitrary",)),
    )(a, b)


def _mixer_bwd(dxm, proj, ao, cw, gq, gk, sinks, gco, gao, wo, tq):
    t = proj.shape[0]
    nb = tq // BLK
    r8 = tq // 8
    nt = t // tq
    te = tq + 8
    kvw = 2 * NKV * HP

    def body(dx_ref, dxn_ref, p_ref, cgp_ref, hcp_ref, bgn_ref, cgn_ref, hcn_ref, kvp_ref, ao_ref, cw_ref, gq_ref,
             gk_ref, sk_ref, gco_ref, gao_ref, wo_ref,
             dpm_ref, dkvm_ref, dkvh_ref, dcw_ref, dgq_ref, dgk_ref, dsk_ref, dgco_ref, dgao_ref, acc_ref):
        i = pl.program_id(0)

        @pl.when(i == 0)
        def _():
            for r in (dcw_ref, dgq_ref, dgk_ref, dsk_ref, dgco_ref, dgao_ref):
                r[...] = jnp.zeros_like(r)

        acc_ref[...] = jnp.zeros_like(acc_ref)
        live_rows = jnp.where(i < nt - 1, te, tq)
        dxb = dx_ref[...].astype(BF16)
        dxe = jnp.concatenate([dxb, dxn_ref[...].astype(BF16)], axis=0)
        dcn = _nt(dxe, wo_ref[0:CC, :])
        bg = jnp.concatenate([p_ref[:, O_BG:O_BG + CC], bgn_ref[...]], axis=0)
        cg = jnp.concatenate([p_ref[:, O_CG:O_CG + CC], cgn_ref[...]], axis=0)
        hc = jnp.concatenate([p_ref[:, O_HC:O_HC + CC], hcn_ref[...]], axis=0)
        u = cg * hc
        up = jnp.where(i > 0, cgp_ref[...] * hcp_ref[...], 0.0)
        u1, u2 = _conv_taps(jnp.concatenate([up, u], axis=0), te)
        w0, w1, w2 = cw_ref[0:1, :], cw_ref[1:2, :], cw_ref[2:3, :]
        y = w0 * u2 + w1 * u1 + w2 * u
        co = bg * y
        rc, coh = _rms_fwd(co, 1.0 / CC)
        dco = _rms_bwd(dcn, gco_ref[...], coh, rc, 1.0 / CC)
        row_io = lax.broadcasted_iota(jnp.int32, (te, 1), 0)
        own = row_io < tq
        dgco_ref[...] += jnp.sum(jnp.where(own, dcn * coh, 0.0), axis=0, keepdims=True)
        dyc = jnp.where(row_io < live_rows, dco * bg, 0.0)
        dyo = jnp.where(own, dyc, 0.0)
        dcw_ref[0:1, :] += jnp.sum(dyo * u2, axis=0, keepdims=True)
        dcw_ref[1:2, :] += jnp.sum(dyo * u1, axis=0, keepdims=True)
        dcw_ref[2:3, :] += jnp.sum(dyo * u, axis=0, keepdims=True)
        dy1 = pltpu.roll(dyc, te - 1, 0)[0:tq]
        dy2 = pltpu.roll(dyc, te - 2, 0)[0:tq]
        du = w2 * dyc[0:tq] + w1 * dy1 + w0 * dy2
        dpm_ref[:, O_BG:O_BG + CC] = (dco[0:tq] * y[0:tq]).astype(BF16)
        dpm_ref[:, O_CG:O_CG + CC] = (du * hc[0:tq]).astype(BF16)
        dpm_ref[:, O_HC:O_HC + CC] = (du * cg[0:tq]).astype(BF16)
        kraw = jnp.concatenate([kvp_ref[:, 0:NKV * HP], p_ref[:, O_K:O_K + NKV * HP]], axis=0)
        vraw = jnp.concatenate([kvp_ref[:, NKV * HP:], p_ref[:, O_V:O_V + NKV * HP]], axis=0)
        gqv, gkv = gq_ref[...], gk_ref[...]
        keys = _norm_keys(kraw, gkv)
        vb = [vraw[:, h * HP:(h + 1) * HP].astype(BF16) for h in range(NKV)]
        base_valid, c_io = _band_mask()
        lane = lax.broadcasted_iota(jnp.int32, (1, HP), 1)
        dgq, dgk, dsk = (jnp.zeros((1, HP), F32) for _ in range(3))
        dgao = jnp.zeros((1, NQ * HD), F32)
        for b in range(nb):
            lo = jnp.where(i * nb + b == 0, BLK, 0)
            valid = base_valid & (c_io >= lo)
            band = slice(b * BLK, b * BLK + 2 * BLK)
            blk = slice(b * BLK, (b + 1) * BLK)
            ra, aoh = _rms_fwd(ao_ref[blk, :], 1.0 / (NQ * HD))
            danb = _nt(dxb[blk], wo_ref[CC:MIXW, :])
            dgao = dgao + jnp.sum(danb * aoh, axis=0, keepdims=True)
            dao = _rms_bwd(danb, gao_ref[...], aoh, ra, 1.0 / (NQ * HD))
            dos = [dao[:, g // 2 * HP:(g // 2 + 1) * HP] for g in range(NQ)]
            dos = [(d if g % 2 == 0 else pltpu.roll(d, HD, 1)).astype(BF16) for g, d in enumerate(dos)]
            fwd = []
            for g in range(NQ):
                rq, qh = _rms_fwd(p_ref[blk, O_Q + g * HP:O_Q + (g + 1) * HP], 1.0 / HD)
                qs = (qh * (gqv * SCALE)).astype(BF16)
                fwd.append((rq, qh, qs) + _attn_probs(qs, keys[g // GRP][2][band], sk_ref[0, g], valid))
            dqs = []
            for h in range(NKV):
                khat, rk, kn = [a[band] for a in keys[h]]
                dss, prbs, qns, dobs = [], [], [], []
                for g in range(h * GRP, (h + 1) * GRP):
                    rq, qh, qs, pr, ps = fwd[g]
                    dob = dos[g]
                    dp = _nt(dob, vb[h][band])
                    delta = jnp.sum(pr * dp, axis=-1, keepdims=True)
                    dsb = (pr * (dp - delta)).astype(BF16)
                    dsk = dsk + jnp.where(lane == g, -jnp.sum(ps * delta, axis=0, keepdims=True), 0.0)
                    dqn = jnp.dot(dsb, kn, preferred_element_type=F32) * SCALE
                    dgq = dgq + jnp.sum(dqn * qh, axis=0, keepdims=True)
                    dqs.append(_rms_bwd(dqn, gqv, qh, rq, 1.0 / HD).astype(BF16))
                    dss.append(dsb)
                    prbs.append(pr.astype(BF16))
                    qns.append(qs)
                    dobs.append(dob)
                dkn = _tn(jnp.concatenate(dss, axis=0), jnp.concatenate(qns, axis=0))
                dv = _tn(jnp.concatenate(prbs, axis=0), jnp.concatenate(dobs, axis=0))
                dgk = dgk + jnp.sum(dkn * khat, axis=0, keepdims=True)
                acc_ref[band, h * HP:(h + 1) * HP] += _rms_bwd(dkn, gkv, khat, rk, 1.0 / HD)
                acc_ref[band, (NKV + h) * HP:(NKV + h + 1) * HP] += dv
            dpm_ref[blk, O_Q:O_K] = jnp.concatenate(dqs, axis=1)
        dgq_ref[...] += dgq
        dgk_ref[...] += dgk
        dsk_ref[...] += dsk
        dgao_ref[...] += dgao
        dkvh_ref[...] = acc_ref[0:BLK, :]
        dkvm_ref[...] = acc_ref[BLK:, :]

    prev8 = lambda col: pl.BlockSpec((8, CC), lambda i: (jnp.maximum(i * r8 - 1, 0), col))
    next8 = lambda col: pl.BlockSpec((8, CC), lambda i: (jnp.minimum((i + 1) * r8, t // 8 - 1), col))
    small = lambda n: pl.BlockSpec((1, n), lambda i: (0, 0))
    return pl.pallas_call(
        body, name="mixer_bwd", grid=(nt,),
        in_specs=[
            pl.BlockSpec((tq, D), lambda i: (i, 0)),
            pl.BlockSpec((8, D), lambda i: (jnp.minimum((i + 1) * r8, t // 8 - 1), 0)),
            pl.BlockSpec((tq, NP), lambda i: (i, 0)),
            prev8(O_CG // CC), prev8(O_HC // CC),
            next8(O_BG // CC), next8(O_CG // CC), next8(O_HC // CC),
            pl.BlockSpec((BLK, kvw), lambda i: (jnp.maximum(i * nb - 1, 0), O_K // kvw)),
            pl.BlockSpec((tq, NQ * HD), lambda i: (i, 0)),
            _const_spec((8, CC)), _const_spec((1, HP)), _const_spec((1, HP)),
            pl.BlockSpec(memory_space=pltpu.SMEM),
            _const_spec((1, CC)), _const_spec((1, NQ * HD)), _const_spec((MIXW, D)),
        ],
        out_specs=[
            pl.BlockSpec((tq, NMAIN), lambda i: (i, 0)),
            pl.BlockSpec((tq, kvw), lambda i: (i, 0)),
            pl.BlockSpec((BLK, kvw), lambda i: (i, 0)),
            pl.BlockSpec((8, CC), lambda i: (0, 0)), small(HP), small(HP), small(HP), small(CC), small(NQ * HD),
        ],
        out_shape=[
            jax.ShapeDtypeStruct((t, NMAIN), BF16), jax.ShapeDtypeStruct((t, kvw), F32),
            jax.ShapeDtypeStruct((nt * BLK, kvw), F32),
            jax.ShapeDtypeStruct((8, CC), F32), jax.ShapeDtypeStruct((1, HP), F32), jax.ShapeDtypeStruct((1, HP), F32),
            jax.ShapeDtypeStruct((1, HP), F32), jax.ShapeDtypeStruct((1, CC), F32),
            jax.ShapeDtypeStruct((1, NQ * HD), F32),
        ],
        scratch_shapes=[pltpu.VMEM((tq + BLK, kvw), F32)],
        compiler_params=_cparams(("arbitrary",)),
    )(dxm, dxm, proj, proj, proj, proj, proj, proj, proj, ao, cw, gq, gk, sinks, gco, gao, wo)


def _inproj_bwd(dpm, dkvm, dkvh, wpt, x, g1, dxm, tm):
    t = x.shape[0]
    kvw = 2 * NKV * HP
    nt = t // tm

    def body(dp_ref, dk_ref, dh_ref, w_ref, x_ref, g_ref, dxm_ref, dx_ref, dg_ref, dkv_ref):
        i = pl.program_id(0)

        @pl.when(i == 0)
        def _():
            dg_ref[...] = jnp.zeros_like(dg_ref)

        halo = jnp.where(i < nt - 1, dh_ref[...], 0.0)
        dkv_ref[0:tm - BLK, :] = dk_ref[0:tm - BLK, :].astype(BF16)
        dkv_ref[tm - BLK:tm, :] = (dk_ref[tm - BLK:tm, :] + halo).astype(BF16)
        dh = (jnp.dot(dp_ref[...], w_ref[0:NMAIN, :], preferred_element_type=F32)
              + jnp.dot(dkv_ref[...], w_ref[NMAIN:NP, :], preferred_element_type=F32))
        r, xh = _rms_fwd(x_ref[...], 1.0 / D)
        dg_ref[...] += jnp.sum(dh * xh, axis=0, keepdims=True)
        dx_ref[...] = dxm_ref[...] + _rms_bwd(dh, g_ref[...], xh, r, 1.0 / D)

    row = lambda w: pl.BlockSpec((tm, w), lambda i: (i, 0))
    return pl.pallas_call(
        body, name="inproj_bwd", grid=(nt,),
        in_specs=[row(NMAIN), row(kvw), pl.BlockSpec((BLK, kvw), lambda i: (jnp.minimum(i + 1, nt - 1), 0)),
                  _const_spec((NP, D)), row(D), _const_spec((1, D)), row(D)],
        out_specs=[row(D), pl.BlockSpec((1, D), lambda i: (0, 0)), row(kvw)],
        out_shape=[jax.ShapeDtypeStruct((t, D), F32), jax.ShapeDtypeStruct((1, D), F32),
                   jax.ShapeDtypeStruct((t, kvw), BF16)],
        compiler_params=_cparams(("arbitrary",)),
    )(dpm, dkvm, dkvh, wpt, x, g1, dxm)


def _rows_tile(rows, cap=512):
    for cand in range(min(rows, cap) // 16 * 16, 0, -16):
        if rows % cand == 0:
            return cand
    return rows


def _presum_halves(gs, theirs, core):
    n = len(gs)

    def body(c_ref, *refs):
        for g_ref, t_ref, o_ref in zip(refs[:n], refs[n:2 * n], refs[2 * n:]):
            o_ref[...] = (g_ref[...].astype(F32) + t_ref[...].astype(F32)).astype(BF16)

    half = lambda ta: pl.BlockSpec((None,) + ta.shape[1:], lambda k, c_ref: (k, 0, 0))
    own = lambda ta: pl.BlockSpec((None,) + ta.shape[1:], lambda k, c_ref: (k, c_ref[0], 0))
    return pl.pallas_call(
        body, name="presum",
        grid_spec=pltpu.PrefetchScalarGridSpec(
            num_scalar_prefetch=1, grid=(N_CHIPS,),
            in_specs=[own(ta) for ta in theirs] + [half(ta) for ta in theirs],
            out_specs=[half(ta) for ta in theirs]),
        out_shape=[jax.ShapeDtypeStruct(ta.shape, BF16) for ta in theirs],
        compiler_params=_cparams(("parallel",)),
    )(core, *gs, *theirs)


def _sum_chips(got, ps, chip):
    n = len(got)
    steps = 2

    def body(chip_ref, *refs):
        for c_ref, own_ref, o_ref in zip(refs[:n], refs[n:2 * n], refs[2 * n:]):
            acc = None
            for j in range(N_CHIPS):
                term = jnp.where(chip_ref[0] == j, own_ref[...], c_ref[j]).astype(F32)
                acc = term if acc is None else acc + term
            o_ref[...] = acc

    tile = lambda c: (c.shape[1] // steps, c.shape[2])
    return pl.pallas_call(
        body, name="chipsum",
        grid_spec=pltpu.PrefetchScalarGridSpec(
            num_scalar_prefetch=1, grid=(steps,),
            in_specs=[pl.BlockSpec((N_CHIPS,) + tile(c), lambda i, chip_ref: (0, i, 0)) for c in got]
            + [pl.BlockSpec((None,) + tile(c), lambda i, chip_ref: (chip_ref[0], i, 0)) for c in got],
            out_specs=[pl.BlockSpec(tile(c), lambda i, chip_ref: (i, 0)) for c in got]),
        out_shape=[jax.ShapeDtypeStruct(c.shape[1:], F32) for c in got],
        compiler_params=_cparams(("parallel",)),
    )(chip, *got, *ps)


def _adamw(w, g, m, v, name):
    rows, cols = w.shape
    tr = _rows_tile(rows, 256)
    c1 = 1.0 - ADAM_B1 ** ADAM_STEP
    c2 = 1.0 - ADAM_B2 ** ADAM_STEP

    def body(w_ref, g_ref, m_ref, v_ref, d_ref, mo_ref, vo_ref):
        gv = g_ref[...]
        mn = ADAM_B1 * m_ref[...] + (1.0 - ADAM_B1) * gv
        vn = ADAM_B2 * v_ref[...] + (1.0 - ADAM_B2) * (gv * gv)
        mo_ref[...] = mn
        vo_ref[...] = vn
        d_ref[...] = -ADAM_LR * ((mn / c1) / (jnp.sqrt(vn / c2) + ADAM_EPS) + ADAM_WD * w_ref[...])

    spec = pl.BlockSpec((tr, cols), lambda i: (i, 0))
    sds = jax.ShapeDtypeStruct((rows, cols), F32)
    return pl.pallas_call(
        body, name=name, grid=(rows // tr,), in_specs=[spec] * 4, out_specs=[spec] * 3, out_shape=[sds] * 3,
        compiler_params=_cparams(("parallel",)),
    )(w, g, m, v)


def _place():
    x, y, c = lax.axis_index("x"), lax.axis_index("y"), lax.axis_index("c")
    chips = [(1 - x, y), (x, 1 - y), (1 - x, 1 - y)]
    return x, y, c, chips


ANY = pl.BlockSpec(memory_space=pl.ANY)
DMA_ROWS = 64


def _pieces(shape):
    rows = shape[-2]
    step = DMA_ROWS if rows % DMA_ROWS == 0 else rows
    lead = [()]
    for n in shape[:-2]:
        lead = [i + (k,) for i in lead for k in range(n)]
    return [i + (pl.ds(r0, step),) for i in lead for r0 in range(0, rows, step)]


def _start_pieces(make, src, dst):
    for idx in _pieces(src.shape):
        make(src.at[idx], dst.at[idx]).start()


def _gather_body(srcs, outs, sems, layer, start):
    nw = len(srcs)
    ssem, rsem, fssem, frsem = sems
    x, y, c, chips = _place()
    kme = 2 * x + y

    def plane(j, w, to):
        return lambda s, d: pltpu.make_async_remote_copy(
            src_ref=s, dst_ref=d, send_sem=ssem.at[j, w], recv_sem=rsem.at[j, w], device_id=to,
            device_id_type=MESH)

    def passed(j, w):
        return lambda s, d: pltpu.make_async_remote_copy(
            src_ref=s, dst_ref=d, send_sem=fssem.at[j, w], recv_sem=frsem.at[j, w],
            device_id=(x, y, 1 - c), device_id_type=MESH)

    @pl.when(c == layer)
    def _():
        for j, (px, py) in enumerate(chips):
            for w in range(nw):
                start(plane(j, w, (px, py, c)), srcs[w], outs[w].at[kme])
        for j, (px, py) in enumerate(chips):
            for w in range(nw):
                got = outs[w].at[2 * px + py]
                plane(j, w, (px, py, c))(got, got).wait_recv()
                start(passed(j, w), got, got)
        for j, (px, py) in enumerate(chips):
            for w in range(nw):
                got = outs[w].at[2 * px + py]
                plane(j, w, (px, py, c))(got, got).wait_send()
                passed(j, w)(got, got).wait_send()

    @pl.when(c != layer)
    def _():
        for j, (px, py) in enumerate(chips):
            for w in range(nw):
                got = outs[w].at[2 * px + py]
                passed(j, w)(got, got).wait_recv()


def _handshake(peers):
    barrier = pltpu.get_barrier_semaphore()
    for peer in peers:
        pl.semaphore_signal(barrier, inc=1, device_id=peer, device_id_type=MESH)
    pl.semaphore_wait(barrier, len(peers))


def _handshake_all():
    x, y, c, _ = _place()
    _handshake([(x ^ (r >> 2), y ^ ((r >> 1) & 1), c ^ (r & 1)) for r in range(1, 8)])


def _gather_layer_async(blocks, layer, name, collective_id):
    hbm = pltpu.MemorySpace.HBM
    srcs = [jax.new_ref(b, memory_space=hbm) for b in blocks]
    outs = [jax.empty_ref(jax.ShapeDtypeStruct((N_CHIPS,) + b.shape, b.dtype), memory_space=hbm) for b in blocks]

    @pl.kernel(mesh=plsc.ScalarSubcoreMesh(axis_name="seq", num_cores=1), name=name,
               scratch_types=[pltpu.SemaphoreType.DMA((3, len(blocks)))] * 4,
               compiler_params=pltpu.CompilerParams(collective_id=collective_id))
    def launch(*sems):
        _handshake_all()
        _gather_body(srcs, outs, sems, layer, lambda make, s, d: make(s, d).start())

    launch()
    return [o[...] for o in outs]


def _swap_siblings(arrs, halves, name, collective_id=None):
    nw = len(arrs)
    out_sds = [jax.ShapeDtypeStruct((a.shape[0], a.shape[1] // 2, a.shape[2]) if halves else a.shape, a.dtype)
               for a in arrs]

    def exchange(srcs, outs, ssem, rsem, start):
        x, y, c, _ = _place()

        def give(w):
            return lambda s, d: pltpu.make_async_remote_copy(
                src_ref=s, dst_ref=d, send_sem=ssem.at[w], recv_sem=rsem.at[w], device_id=(x, y, 1 - c),
                device_id_type=MESH)

        for w in range(nw):
            hr = outs[w].shape[1]
            start(give(w), srcs[w].at[:, pl.ds((1 - c) * hr, hr)] if halves else srcs[w], outs[w])
        for w in range(nw):
            give(w)(outs[w], outs[w]).wait()

    if collective_id is None:
        def body(*refs):
            exchange(refs[:nw], refs[nw:2 * nw], *refs[2 * nw:], _start_pieces)

        return pl.pallas_call(
            body, name=name, in_specs=[ANY] * nw, out_specs=[ANY] * nw, out_shape=out_sds,
            scratch_shapes=[pltpu.SemaphoreType.DMA((nw,))] * 2,
            compiler_params=_cparams(has_side_effects=True),
        )(*arrs)

    hbm = pltpu.MemorySpace.HBM
    srcs = [jax.new_ref(a, memory_space=hbm) for a in arrs]
    outs = [jax.empty_ref(sds, memory_space=hbm) for sds in out_sds]

    @pl.kernel(mesh=plsc.ScalarSubcoreMesh(axis_name="seq", num_cores=1), name=name,
               scratch_types=[pltpu.SemaphoreType.DMA((nw,))] * 2,
               compiler_params=pltpu.CompilerParams(collective_id=collective_id))
    def launch(ssem, rsem):
        x, y, c, _ = _place()
        _handshake([(x, y, 1 - c)])
        exchange(srcs, outs, ssem, rsem, lambda make, s, d: make(s, d).start())

    launch()
    return [o[...] for o in outs]


def _scatter_body(srcs, outs, sems, start):
    nw = len(srcs)
    ssem, rsem = sems
    x, y, c, chips = _place()
    kme = 2 * x + y

    def give(j, w, to):
        return lambda s, d: pltpu.make_async_remote_copy(
            src_ref=s, dst_ref=d, send_sem=ssem.at[j, w], recv_sem=rsem.at[j, w], device_id=to,
            device_id_type=MESH)

    for j, (px, py) in enumerate(chips):
        for w in range(nw):
            start(give(j, w, (px, py, c)), srcs[w].at[2 * px + py], outs[w].at[kme])
    for j, (px, py) in enumerate(chips):
        for w in range(nw):
            got = outs[w].at[2 * px + py]
            give(j, w, (px, py, c))(got, got).wait_recv()
    for j, (px, py) in enumerate(chips):
        for w in range(nw):
            sent = srcs[w].at[2 * px + py]
            give(j, w, (px, py, c))(sent, sent).wait_send()


def _scatter_chips_async(ps, name, collective_id):
    hbm = pltpu.MemorySpace.HBM
    srcs = [jax.new_ref(p, memory_space=hbm) for p in ps]
    outs = [jax.empty_ref(jax.ShapeDtypeStruct(p.shape, p.dtype), memory_space=hbm) for p in ps]

    @pl.kernel(mesh=plsc.ScalarSubcoreMesh(axis_name="seq", num_cores=1), name=name,
               scratch_types=[pltpu.SemaphoreType.DMA((3, len(ps)))] * 2,
               compiler_params=pltpu.CompilerParams(collective_id=collective_id))
    def launch(*sems):
        _handshake_all()
        _scatter_body(srcs, outs, sems, lambda make, s, d: make(s, d).start())

    launch()
    return [o[...] for o in outs]


def _allreduce_small(v):
    rows = v.shape[0]

    def body(v_ref, o_ref, buf, ssem, rsem):
        x, y, c, _ = _place()
        me = 4 * x + 2 * y + c
        buf[me] = v_ref[...]
        sends = []
        for r in range(1, 8):
            peer = (x ^ (r >> 2), y ^ ((r >> 1) & 1), c ^ (r & 1))
            cp = pltpu.make_async_remote_copy(
                src_ref=v_ref, dst_ref=buf.at[me], send_sem=ssem.at[r - 1], recv_sem=rsem.at[r - 1],
                device_id=peer, device_id_type=MESH)
            cp.start()
            sends.append(cp)
        for r in range(1, 8):
            src = me ^ r
            pltpu.make_async_remote_copy(
                src_ref=v_ref, dst_ref=buf.at[src], send_sem=ssem.at[r - 1], recv_sem=rsem.at[r - 1],
                device_id=(x, y, c), device_id_type=MESH).wait_recv()
        for cp in sends:
            cp.wait_send()
        acc = buf[0]
        for d in range(1, 8):
            acc = acc + buf[d]
        o_ref[...] = acc

    vm = pl.BlockSpec(memory_space=pltpu.VMEM)
    return pl.pallas_call(
        body, name="allreduce_small", in_specs=[vm], out_specs=vm,
        out_shape=jax.ShapeDtypeStruct(v.shape, F32),
        scratch_shapes=[pltpu.VMEM((8, rows, 128), F32), pltpu.SemaphoreType.DMA((7,)),
                        pltpu.SemaphoreType.DMA((7,))],
        compiler_params=_cparams(has_side_effects=True),
    )(v)


def _t(w):
    return jnp.swapaxes(w, -1, -2)


def _count(shape):
    n = 1
    for s in shape:
        n *= s
    return n


def _pack_rows(arrs):
    flat = [jnp.pad(a.reshape(-1), (0, (-_count(a.shape)) % 128)) for a in arrs]
    v = jnp.concatenate(flat)
    rows = -(-v.shape[0] // (8 * 128)) * 8
    return jnp.pad(v, (0, rows * 128 - v.shape[0])).reshape(rows, 128)


def kernel(x, norm1_g, w_in, conv_w, q_norm_g, k_norm_g, sinks, conv_out_g, attn_out_g, w_o, norm2_g, w_gate, w_up, w_down, loss_target, m_norm1_g, m_w_in, m_conv_w, m_q_norm_g, m_k_norm_g, m_sinks, m_conv_out_g, m_attn_out_g, m_w_o, m_norm2_g, m_w_gate, m_w_up, m_w_down, v_norm1_g, v_w_in, v_conv_w, v_q_norm_g, v_k_norm_g, v_sinks, v_conv_out_g, v_attn_out_g, v_w_o, v_norm2_g, v_w_gate, v_w_up, v_w_down):
    depth = w_in.shape[0]
    t = x.shape[1]
    xs = x.reshape(t, D)
    tgt = loss_target.reshape(t, D)
    xi, yi = lax.axis_index("x"), lax.axis_index("y")
    kme = 2 * xi + yi
    tm = min(512, t)
    tq = min(512, t)
    tf = min(256, t)
    tw = min(1024, t)

    cwp = jnp.pad(conv_w.reshape(depth * 3, CC // N_CHIPS), ((0, 8 - depth * 3), (0, 0)))
    own_f = [jnp.concatenate([_t(w_gate[l]), _t(w_up[l]), w_down[l]], axis=0).astype(BF16) for l in range(depth)]
    own_o = [w_o[l].astype(BF16) for l in range(depth)]
    own_i = [_t(w_in[l]).astype(BF16) for l in range(depth)]
    mine = lambda got, own: lax.dynamic_update_index_in_dim(got, own, kme, 0)
    got0 = _gather_layer_async([own_i[0], own_o[0], cwp], 0, "gather_in0_seq", collective_id=14)
    (got_i0, got_o0, got_cw), own_f, own_o, own_i = lax.optimization_barrier((got0, own_f, own_o, own_i))
    gf0_in = lax.optimization_barrier((own_f[0], got_i0))[0]
    (got_f0,) = _gather_layer_async([gf0_in], 0, "gather_ffn0_seq", collective_id=6)
    cw_full = mine(got_cw, cwp).transpose(1, 0, 2).reshape(8, CC)[:depth * 3].reshape(depth, 3, CC)

    chip = kme.reshape(1).astype(jnp.int32)

    def layer_params(l, got_o):
        return dict(
            wo=mine(got_o, own_o[l]).reshape(MIXW, D),
            cw=jnp.pad(cw_full[l], ((0, 5), (0, 0))),
            g1=norm1_g[l].reshape(1, D), g2=norm2_g[l].reshape(1, D),
            gq=jnp.pad(q_norm_g[l], (0, HP - HD)).reshape(1, HP), gk=jnp.pad(k_norm_g[l], (0, HP - HD)).reshape(1, HP),
            sk=sinks[l].reshape(1, NQ), gco=conv_out_g[l].reshape(1, CC),
            gao=attn_out_g[l].reshape(1, NQ * HD))

    saved, layers = [], []
    cur = xs
    for l in range(depth):
        x_in = cur
        if l == 0:
            got_i, p = got_i0, layer_params(0, got_o0)
        else:
            got_f1, got_o1, got_i = lax.optimization_barrier((got_l1, cur))[0]
            p = layer_params(1, got_o1)
        proj, h, p["wpt"] = _inproj_fwd(cur, p["g1"], got_i, own_i[l], chip, tm)
        xm, mix, ao = _mixer_fwd(proj, cur, p["cw"], p["gq"], p["gk"], p["sk"], p["gco"], p["gao"], p["wo"], tq)
        if l == 0:
            got_f0 = lax.optimization_barrier((got_f0, xm))[0]
            l1_in = lax.optimization_barrier(([own_f[1], own_o[1], own_i[1]], got_f0))[0]
            got_l1 = _gather_layer_async(l1_in, 1, "gather_layer1_seq", collective_id=1)
        p["gf"] = mine(got_f0 if l == 0 else got_f1, own_f[l])
        layers.append(p)
        if l < depth - 1:
            cur, a, b, h2 = _ffn_fwd(xm, p["g2"], p["gf"], tm)
        else:
            lpart, dy, a, b, h2 = _ffn_fwd(xm, p["g2"], p["gf"], tm, tgt)
        saved.append(dict(x=x_in, proj=proj, h=h, xm=xm, mix=mix, ao=ao, a=a, b=b, h2=h2))

    ci = lax.axis_index("c")
    core = ci.reshape(1).astype(jnp.int32)
    rbig = [dict() for _ in range(depth)]
    gsmall = [None] * depth

    def after_(vals, after):
        return vals if after is None else lax.optimization_barrier((vals, after))[0]

    def reduce_1(gs, tag, ids):
        return gs, _swap_siblings(gs, True, f"swap_halves_{tag}_seq", ids[0]), tag, ids

    def reduce_2(state, after):
        gs, theirs, tag, ids = state
        ps = _presum_halves(gs, after_(theirs, after), core)
        return ps, _scatter_chips_async(ps, f"scatter_{tag}_seq", ids[1]), tag, ids

    def reduce_3(state, after):
        ps, got, tag, ids = state
        r_mine = _sum_chips(after_(got, after), ps, chip)
        return r_mine, _swap_siblings(r_mine, False, f"swap_reduced_{tag}" + ("_seq" if ids[2] else ""), ids[2])

    def reduce_4(state, after):
        r_mine, r_theirs = state
        return [jnp.where(ci == 0, jnp.concatenate([a, b], axis=0), jnp.concatenate([b, a], axis=0))
                for a, b in zip(r_mine, after_(r_theirs, after))]

    ids = {"ffn1": (7, 4, 8), "in1": (9, 5, 10), "ffn0": (11, 2, 12), "in0": (13, 3, None)}
    in_2 = None
    handed = {}
    for l in reversed(range(depth)):
        p, s = layers[l], saved[l]
        dxm, da, db, hm, dg2 = _ffn_bwd(dy, s["xm"], p["g2"], s["a"], s["b"], p["gf"], tf)
        if in_2 is not None:
            in_2 = reduce_2(in_2, dxm)
        g_wg = _wgrad_blocks(da, s["h2"], tw, "wgrad_gate")
        g_wu = _wgrad_blocks(db, s["h2"], tw, "wgrad_up")
        g_wd = _wgrad_blocks(hm, dy, tw, "wgrad_down")
        if in_2 is not None:
            handed[f"in{l + 1}"] = reduce_3(in_2, g_wd)
        ffn_1 = reduce_1([g_wg, g_wu, g_wd], f"ffn{l}", ids[f"ffn{l}"])
        dxm = after_(dxm, (g_wg, g_wu, g_wd))
        dpm, dkvm, dkvh, dcw, dgq, dgk, dsk, dgco, dgao = _mixer_bwd(
            dxm, s["proj"], s["ao"], p["cw"], p["gq"], p["gk"], p["sk"], p["gco"], p["gao"], p["wo"], tq)
        ffn_2 = reduce_2(ffn_1, dpm)
        g_o = _wgrad(s["mix"], dxm, tw, "wgrad_o")
        dx, dg1, dkv = _inproj_bwd(dpm, dkvm, dkvh, p["wpt"], s["x"], p["g1"], dxm, tq)
        g_in = jnp.concatenate(
            [_wgrad(dpm, s["h"], tw, "wgrad_in_main", [(0, 0, O_Q)] + _head_rows(O_Q, NQ)),
             _wgrad(dkv, s["h"], tw, "wgrad_in_kv", _head_rows(0, 2 * NKV))], axis=0)
        dy = after_(dx, (g_in, g_o))
        gsmall[l] = dict(g1=dg1, cw=dcw[:3], gq=dgq[0, :HD], gk=dgk[0, :HD], sk=dsk[0, :NQ], gco=dgco,
                         gao=dgao, g2=dg2)
        handed[f"ffn{l}"] = reduce_3(ffn_2, g_in)
        in_2 = reduce_1([g_in.reshape(N_CHIPS, -1, D), g_o.reshape(N_CHIPS, -1, D)], f"in{l}", ids[f"in{l}"])
    grad_x = dy.reshape(x.shape)

    small_shapes = dict(g1=(D,), cw=(3, CC), gq=(HD,), gk=(HD,), sk=(NQ,), gco=(CC,), gao=(NQ * HD,), g2=(D,))
    red = _allreduce_small(_pack_rows([gsmall[l][n] for l in range(depth) for n in small_shapes]
                                      + [lpart[0:1, 0:1]])).reshape(-1)
    red_small, offs = {n: [] for n in small_shapes}, 0
    for l in range(depth):
        for n, shp in small_shapes.items():
            cnt = _count(shp)
            red_small[n].append(red[offs:offs + cnt].reshape(shp))
            offs += -(-cnt // 128) * 128
    loss = red[offs]
    g_small = {n: jnp.stack(v) for n, v in red_small.items()}
    g_cw = lax.dynamic_slice_in_dim(g_small["cw"], kme * (CC // N_CHIPS), CC // N_CHIPS, axis=2)

    weights = [norm1_g, w_in, conv_w, q_norm_g, k_norm_g, sinks, conv_out_g, attn_out_g, w_o, norm2_g, w_gate,
               w_up, w_down]
    moms = [m_norm1_g, m_w_in, m_conv_w, m_q_norm_g, m_k_norm_g, m_sinks, m_conv_out_g, m_attn_out_g, m_w_o,
            m_norm2_g, m_w_gate, m_w_up, m_w_down]
    vars_ = [v_norm1_g, v_w_in, v_conv_w, v_q_norm_g, v_k_norm_g, v_sinks, v_conv_out_g, v_attn_out_g, v_w_o,
             v_norm2_g, v_w_gate, v_w_up, v_w_down]
    n_w = len(weights)
    big_idx = dict(zip(("in", "o", "g", "u", "d"), (1, 8, 10, 11, 12)))
    small_idx = [n for n in range(n_w) if n not in big_idx.values()]
    grads, deltas, new_m, new_v = [None] * n_w, [None] * n_w, [None] * n_w, [None] * n_w
    for n, g in zip(small_idx, (g_small["g1"], g_cw, g_small["gq"], g_small["gk"], g_small["sk"], g_small["gco"],
                                g_small["gao"], g_small["g2"])):
        grads[n] = g

    def update_big(name):
        n = big_idx[name]
        g = jnp.stack([rbig[l][name] for l in range(depth)])
        flip = g.shape != weights[n].shape
        rows2d = lambda a3: (_t(a3) if flip else a3).reshape(-1, D)
        res = _adamw(rows2d(weights[n]), g.reshape(-1, D), rows2d(moms[n]), rows2d(vars_[n]), f"adamw_{n}")
        res = [g] + [r.reshape(g.shape) for r in res]
        grads[n], deltas[n], new_m[n], new_v[n] = [_t(r) for r in res] if flip else res

    for l in range(depth):
        rbig[l]["g"], rbig[l]["u"], rbig[l]["d"] = reduce_4(handed[f"ffn{l}"], red)
    rbig[1]["in"], rbig[1]["o"] = reduce_4(handed["in1"], red)
    update_big("g")
    in_2 = reduce_2(in_2, new_v[big_idx["g"]])
    update_big("u")
    update_big("d")
    rbig[0]["in"], rbig[0]["o"] = reduce_4(reduce_3(in_2, new_v[big_idx["d"]]), None)
    for name in ("in", "o"):
        update_big(name)
    res = _adamw(*[_pack_rows([arrs[n] for n in small_idx]) for arrs in (weights, grads, moms, vars_)],
                 "adamw_small")
    offs = 0
    for n in small_idx:
        shp = weights[n].shape
        cnt = _count(shp)
        deltas[n], new_m[n], new_v[n] = [r.reshape(-1)[offs:offs + cnt].reshape(shp) for r in res]
        offs += -(-cnt // 128) * 128
    return (loss, grad_x, *grads, *deltas, *new_m, *new_v)
```

```python
import jax
import jax.numpy as jnp
from jax import lax
from jax.experimental import pallas as pl
from jax.experimental.pallas import tpu as pltpu
from jax.experimental.pallas import tpu_sc as plsc

F32 = jnp.float32
BF16 = jnp.bfloat16

D = 1024
CC = 512
NQ = 8
NKV = 2
HD = 64
HP = 128
GRP = NQ // NKV
FF = 2816
FFB = FF // 4
BLK = 128
EPS = 1e-6
NEG = -1e30
SCALE = HD ** -0.5
O_BG, O_CG, O_HC, O_Q = 0, CC, 2 * CC, 3 * CC
O_K = O_Q + NQ * HP
O_V = O_K + NKV * HP
NP = O_V + NKV * HP
NMAIN = O_K
MIXW = CC + NQ * HD
N_CHIPS = 4
VMEM_LIMIT = 56 * 1024 * 1024
MESH = pl.DeviceIdType.MESH

ADAM_LR, ADAM_B1, ADAM_B2, ADAM_EPS, ADAM_WD, ADAM_STEP = 0.001, 0.9, 0.999, 1e-08, 0.01, 10


def _cparams(sem=None, **kw):
    if sem is not None:
        kw["dimension_semantics"] = sem
    return pltpu.CompilerParams(vmem_limit_bytes=VMEM_LIMIT, **kw)


def _const_spec(shape):
    nd = len(shape)
    return pl.BlockSpec(shape, lambda *_: (0,) * nd, pipeline_mode=pl.Buffered(1))


def _nt(a, b):
    return lax.dot_general(a, b, (((1,), (1,)), ((), ())), preferred_element_type=F32)


def _tn(a, b):
    return lax.dot_general(a, b, (((0,), (0,)), ((), ())), preferred_element_type=F32)


def _rms_fwd(x, inv_n):
    r = lax.rsqrt(jnp.sum(x * x, axis=-1, keepdims=True) * inv_n + EPS)
    return r, x * r


def _rms_bwd(dy, g, xh, r, inv_n):
    dxh = dy * g
    return r * (dxh - xh * (jnp.sum(dxh * xh, axis=-1, keepdims=True) * inv_n))


W_IN_ROWS = 3 * CC + (NQ + 2 * NKV) * HD
W_IN_BLOCK = W_IN_ROWS // N_CHIPS


def _padded_row(row):
    return row + max(row - O_Q, 0) // HD * (HP - HD)


def _w_in_pieces(k):
    first = k * W_IN_BLOCK
    plain = min(max(O_Q - first, 0), W_IN_BLOCK)
    pieces = [(0, first, plain)] if plain else []
    return pieces + [(r, _padded_row(first + r), HD) for r in range(plain, W_IN_BLOCK, HD)]


def _inproj_fwd(x, g1, gi, own_i, chip, tm):
    t = x.shape[0]

    def body(chip_ref, x_ref, g_ref, gi_ref, own_ref, p_ref, h_ref, w_ref, sem):
        @pl.when(pl.program_id(0) == 0)
        def _():
            for k in range(N_CHIPS):
                for src, dst, rows in _w_in_pieces(k):
                    @pl.when(chip_ref[0] == k)
                    def _():
                        pltpu.make_async_copy(own_ref.at[pl.ds(src, rows)], w_ref.at[pl.ds(dst, rows)], sem).start()

                    @pl.when(chip_ref[0] != k)
                    def _():
                        pltpu.make_async_copy(gi_ref.at[k, pl.ds(src, rows)], w_ref.at[pl.ds(dst, rows)], sem).start()
            for slot in range(NQ + 2 * NKV):
                w_ref[O_Q + slot * HP + HD:O_Q + (slot + 1) * HP, :] = jnp.zeros((HP - HD, D), BF16)
            landed = w_ref.at[pl.ds(0, W_IN_ROWS)]
            pltpu.make_async_copy(landed, landed, sem).wait()

        _, xh = _rms_fwd(x_ref[...], 1.0 / D)
        h = (xh * g_ref[...]).astype(BF16)
        h_ref[...] = h
        p_ref[...] = _nt(h, w_ref[...])

    const = lambda shape: pl.BlockSpec(shape, lambda i, c: (0,) * len(shape))
    return pl.pallas_call(
        body, name="inproj_fwd",
        grid_spec=pltpu.PrefetchScalarGridSpec(
            num_scalar_prefetch=1, grid=(t // tm,),
            in_specs=[pl.BlockSpec((tm, D), lambda i, c: (i, 0)), const((1, D)), ANY, ANY],
            out_specs=[pl.BlockSpec((tm, NP), lambda i, c: (i, 0)), pl.BlockSpec((tm, D), lambda i, c: (i, 0)),
                       const((NP, D))],
            scratch_shapes=[pltpu.SemaphoreType.DMA(())]),
        out_shape=[jax.ShapeDtypeStruct((t, NP), F32), jax.ShapeDtypeStruct((t, D), BF16),
                   jax.ShapeDtypeStruct((NP, D), BF16)],
        compiler_params=_cparams(("arbitrary",)),
    )(chip, x, g1, gi, own_i)


def _band_mask():
    r_io = lax.broadcasted_iota(jnp.int32, (BLK, 2 * BLK), 0)
    c_io = lax.broadcasted_iota(jnp.int32, (BLK, 2 * BLK), 1)
    return (c_io > r_io) & (c_io <= r_io + BLK), c_io


def _conv_taps(uf, n):
    u1 = pltpu.roll(uf, 1, 0)[8:8 + n]
    u2 = pltpu.roll(uf, 2, 0)[8:8 + n]
    return u1, u2


def _attn_probs(qs, kband, sink, valid):
    s = jnp.where(valid, _nt(qs, kband), NEG)
    m = jnp.maximum(jnp.max(s, axis=-1, keepdims=True), sink)
    p = jnp.exp(s - m)
    es = jnp.exp(sink - m)
    inv = 1.0 / (jnp.sum(p, axis=-1, keepdims=True) + es)
    return p * inv, es * inv


def _norm_keys(kraw, gk):
    out = []
    for h in range(NKV):
        kh = kraw[:, h * HP:(h + 1) * HP]
        rk, khat = _rms_fwd(kh, 1.0 / HD)
        out.append((khat, rk, (khat * gk).astype(BF16)))
    return out


def _mixer_fwd(proj, x, cw, gq, gk, sinks, gco, gao, wo, tq):
    t = proj.shape[0]
    nb = tq // BLK
    r8 = tq // 8

    def body(p_ref, cgp_ref, hcp_ref, kvp_ref, x_ref, cw_ref, gq_ref, gk_ref, sk_ref, gco_ref, gao_ref,
             wo_ref, xm_ref, mix_ref, ao_ref, aop_ref):
        i = pl.program_id(0)
        cg = p_ref[:, O_CG:O_CG + CC]
        hc = p_ref[:, O_HC:O_HC + CC]
        u = cg * hc
        up = jnp.where(i > 0, cgp_ref[...] * hcp_ref[...], 0.0)
        u1, u2 = _conv_taps(jnp.concatenate([up, u], axis=0), tq)
        y = cw_ref[0:1, :] * u2 + cw_ref[1:2, :] * u1 + cw_ref[2:3, :] * u
        co = p_ref[:, O_BG:O_BG + CC] * y
        _, coh = _rms_fwd(co, 1.0 / CC)
        cn = coh * gco_ref[...]
        kraw = jnp.concatenate([kvp_ref[:, 0:NKV * HP], p_ref[:, O_K:O_K + NKV * HP]], axis=0)
        vraw = jnp.concatenate([kvp_ref[:, NKV * HP:], p_ref[:, O_V:O_V + NKV * HP]], axis=0)
        keys = _norm_keys(kraw, gk_ref[...])
        vb = [vraw[:, h * HP:(h + 1) * HP].astype(BF16) for h in range(NKV)]
        base_valid, c_io = _band_mask()
        gqs = gq_ref[...] * SCALE
        for b in range(nb):
            lo = jnp.where(i * nb + b == 0, BLK, 0)
            valid = base_valid & (c_io >= lo)
            for g in range(NQ):
                h = g // GRP
                qg = p_ref[b * BLK:(b + 1) * BLK, O_Q + g * HP:O_Q + (g + 1) * HP]
                _, qh = _rms_fwd(qg, 1.0 / HD)
                qs = (qh * gqs).astype(BF16)
                pr, _ = _attn_probs(qs, keys[h][2][b * BLK:b * BLK + 2 * BLK], sk_ref[0, g], valid)
                aop_ref[b * BLK:(b + 1) * BLK, g * HP:(g + 1) * HP] = jnp.dot(
                    pr.astype(BF16), vb[h][b * BLK:b * BLK + 2 * BLK], preferred_element_type=F32)
        for j in range(NQ // 2):
            ao_ref[:, j * HP:(j + 1) * HP] = (aop_ref[:, 2 * j * HP:(2 * j + 1) * HP]
                                              + pltpu.roll(aop_ref[:, (2 * j + 1) * HP:(2 * j + 2) * HP], HD, 1))
        _, aoh = _rms_fwd(ao_ref[...], 1.0 / (NQ * HD))
        an = aoh * gao_ref[...]
        mix = jnp.concatenate([cn, an], axis=1).astype(BF16)
        mix_ref[...] = mix
        xm_ref[...] = x_ref[...] + jnp.dot(mix, wo_ref[...], preferred_element_type=F32)

    prev8 = lambda col: pl.BlockSpec((8, CC), lambda i: (jnp.maximum(i * r8 - 1, 0), col))
    return pl.pallas_call(
        body, name="mixer_fwd", grid=(t // tq,),
        in_specs=[
            pl.BlockSpec((tq, NP), lambda i: (i, 0)),
            prev8(O_CG // CC), prev8(O_HC // CC),
            pl.BlockSpec((BLK, 2 * NKV * HP), lambda i: (jnp.maximum(i * nb - 1, 0), O_K // (2 * NKV * HP))),
            pl.BlockSpec((tq, D), lambda i: (i, 0)),
            _const_spec((8, CC)), _const_spec((1, HP)), _const_spec((1, HP)),
            pl.BlockSpec(memory_space=pltpu.SMEM),
            _const_spec((1, CC)), _const_spec((1, NQ * HD)), _const_spec((MIXW, D)),
        ],
        out_specs=[pl.BlockSpec((tq, D), lambda i: (i, 0)), pl.BlockSpec((tq, MIXW), lambda i: (i, 0)),
                   pl.BlockSpec((tq, NQ * HD), lambda i: (i, 0))],
        out_shape=[jax.ShapeDtypeStruct((t, D), F32), jax.ShapeDtypeStruct((t, MIXW), BF16),
                   jax.ShapeDtypeStruct((t, NQ * HD), F32)],
        scratch_shapes=[pltpu.VMEM((tq, NQ * HP), F32)],
        compiler_params=_cparams(("parallel",)),
    )(proj, proj, proj, proj, x, cw, gq, gk, sinks, gco, gao, wo)


def _ffn_weight_specs():
    return [pl.BlockSpec((N_CHIPS, FFB, D), lambda i, j=j: (0, j, 0), pipeline_mode=pl.Buffered(1))
            for j in range(3)]


def _ffn_fwd(xm, g2, gf, tm, tgt=None):
    t = xm.shape[0]
    last = tgt is not None

    def body(x_ref, g_ref, wg_ref, wu_ref, wd_ref, *rest):
        t_ref, rest = (rest[0], rest[1:]) if last else (None, rest)
        l_ref, rest = (rest[0], rest[1:]) if last else (None, rest)
        xo_ref, a_ref, b_ref, h2_ref = rest
        xv = x_ref[...]
        _, xh = _rms_fwd(xv, 1.0 / D)
        h2 = (xh * g_ref[...]).astype(BF16)
        h2_ref[...] = h2
        acc = xv
        for k in range(N_CHIPS):
            a = _nt(h2, wg_ref[k])
            b = _nt(h2, wu_ref[k])
            a_ref[k] = a.astype(BF16)
            b_ref[k] = b.astype(BF16)
            hm = (a * jax.nn.sigmoid(a) * b).astype(BF16)
            acc = acc + jnp.dot(hm, wd_ref[k], preferred_element_type=F32)
        if last:
            @pl.when(pl.program_id(0) == 0)
            def _():
                l_ref[...] = jnp.zeros_like(l_ref)

            e = acc - t_ref[...]
            xo_ref[...] = e * (1.0 / D)
            l_ref[...] += jnp.sum(jnp.sum(e * e, axis=-1, keepdims=True), axis=0, keepdims=True) * (0.5 / D)
        else:
            xo_ref[...] = acc

    row = lambda w: pl.BlockSpec((tm, w), lambda i: (i, 0))
    blk = pl.BlockSpec((N_CHIPS, tm, FFB), lambda i: (0, i, 0))
    bsd = jax.ShapeDtypeStruct((N_CHIPS, t, FFB), BF16)
    return pl.pallas_call(
        body, name="ffn_fwd_loss" if last else "ffn_fwd", grid=(t // tm,),
        in_specs=[row(D), _const_spec((1, D))] + _ffn_weight_specs() + ([row(D)] if last else []),
        out_specs=([pl.BlockSpec((8, 128), lambda i: (0, 0))] if last else []) + [row(D), blk, blk, row(D)],
        out_shape=([jax.ShapeDtypeStruct((8, 128), F32)] if last else [])
        + [jax.ShapeDtypeStruct((t, D), F32), bsd, bsd, jax.ShapeDtypeStruct((t, D), BF16)],
        compiler_params=_cparams(("arbitrary" if last else "parallel",)),
    )(*((xm, g2, gf, gf, gf) + ((tgt,) if last else ())))


def _ffn_bwd(dy, xm, g2, a, b, gf, tm):
    t = dy.shape[0]

    def body(dy_ref, x_ref, g_ref, a_ref, b_ref, wg_ref, wu_ref, wd_ref, dx_ref, da_ref, db_ref, hm_ref, dg_ref):
        @pl.when(pl.program_id(0) == 0)
        def _():
            dg_ref[...] = jnp.zeros_like(dg_ref)

        dyv = dy_ref[...]
        dyb = dyv.astype(BF16)
        dh2 = jnp.zeros_like(dyv)
        for k in range(N_CHIPS):
            dhm = _nt(dyb, wd_ref[k])
            av = a_ref[k].astype(F32)
            bv = b_ref[k].astype(F32)
            sig = jax.nn.sigmoid(av)
            sil = av * sig
            hm_ref[k] = (sil * bv).astype(BF16)
            da = (dhm * bv * (sig * (1.0 + av * (1.0 - sig)))).astype(BF16)
            db = (dhm * sil).astype(BF16)
            da_ref[k] = da
            db_ref[k] = db
            dh2 = (dh2 + jnp.dot(da, wg_ref[k], preferred_element_type=F32)
                   + jnp.dot(db, wu_ref[k], preferred_element_type=F32))
        r, xh = _rms_fwd(x_ref[...], 1.0 / D)
        dg_ref[...] += jnp.sum(dh2 * xh, axis=0, keepdims=True)
        dx_ref[...] = dyv + _rms_bwd(dh2, g_ref[...], xh, r, 1.0 / D)

    row = lambda w: pl.BlockSpec((tm, w), lambda i: (i, 0))
    blk = pl.BlockSpec((N_CHIPS, tm, FFB), lambda i: (0, i, 0))
    bsd = jax.ShapeDtypeStruct((N_CHIPS, t, FFB), BF16)
    return pl.pallas_call(
        body, name="ffn_bwd", grid=(t // tm,),
        in_specs=[row(D), row(D), _const_spec((1, D)), blk, blk] + _ffn_weight_specs(),
        out_specs=[row(D), blk, blk, blk, pl.BlockSpec((1, D), lambda i: (0, 0))],
        out_shape=[jax.ShapeDtypeStruct((t, D), F32), bsd, bsd, bsd, jax.ShapeDtypeStruct((1, D), F32)],
        compiler_params=_cparams(("arbitrary",)),
    )(dy, xm, g2, a, b, gf, gf, gf)


def _wgrad_blocks(a, b, tt, name):
    _, t, rows = a.shape
    cols = b.shape[1]
    nsteps = t // tt

    def body(a_ref, b_ref, o_ref, acc_ref):
        s = pl.program_id(0)

        @pl.when(s == 0)
        def _():
            acc_ref[...] = jnp.zeros_like(acc_ref)

        bv = b_ref[...].astype(BF16)
        for k in range(N_CHIPS):
            acc_ref[k] += _tn(a_ref[k], bv)

        @pl.when(s == nsteps - 1)
        def _():
            o_ref[...] = acc_ref[...].astype(BF16)

    return pl.pallas_call(
        body, name=name, grid=(nsteps,),
        in_specs=[pl.BlockSpec((N_CHIPS, tt, rows), lambda s: (0, s, 0)), pl.BlockSpec((tt, cols), lambda s: (s, 0))],
        out_specs=pl.BlockSpec((N_CHIPS, rows, cols), lambda s: (0, 0, 0)),
        out_shape=jax.ShapeDtypeStruct((N_CHIPS, rows, cols), BF16),
        scratch_shapes=[pltpu.VMEM((N_CHIPS, rows, cols), F32)],
        compiler_params=_cparams(("arbitrary",)),
    )(a, b)


def _head_rows(first, n_heads):
    return [(first + g * HD, first + g * HP, HD) for g in range(n_heads)]


def _wgrad(a, b, tt, name, pieces=None):
    t, k = a.shape
    n = b.shape[1]
    nsteps = t // tt
    pieces = pieces or [(0, 0, k)]
    rows = sum(p[2] for p in pieces)

    def body(a_ref, b_ref, o_ref, acc_ref):
        s = pl.program_id(0)

        @pl.when(s == 0)
        def _():
            acc_ref[...] = jnp.zeros_like(acc_ref)

        acc_ref[...] += _tn(a_ref[...].astype(BF16), b_ref[...].astype(BF16))

        @pl.when(s == nsteps - 1)
        def _():
            for dst, src, size in pieces:
                o_ref[dst:dst + size, :] = acc_ref[src:src + size, :].astype(BF16)

    return pl.pallas_call(
        body, name=name, grid=(nsteps,),
        in_specs=[pl.BlockSpec((tt, k), lambda s: (s, 0)), pl.BlockSpec((tt, n), lambda s: (s, 0))],
        out_specs=pl.BlockSpec((rows, n), lambda s: (0, 0)),
        out_shape=jax.ShapeDtypeStruct((rows, n), BF16),
        scratch_shapes=[pltpu.VMEM((k, n), F32)],
        compiler_params=_cparams(("arbitrary",)),
    )(a, b)


def _mixer_bwd(dxm, proj, ao, cw, gq, gk, sinks, gco, gao, wo, tq):
    t = proj.shape[0]
    nb = tq // BLK
    r8 = tq // 8
    nt = t // tq
    te = tq + 8
    kvw = 2 * NKV * HP

    def body(dx_ref, dxn_ref, p_ref, cgp_ref, hcp_ref, bgn_ref, cgn_ref, hcn_ref, kvp_ref, ao_ref, cw_ref, gq_ref,
             gk_ref, sk_ref, gco_ref, gao_ref, wo_ref,
             dpm_ref, dkvm_ref, dkvh_ref, dcw_ref, dgq_ref, dgk_ref, dsk_ref, dgco_ref, dgao_ref, acc_ref):
        i = pl.program_id(0)

        @pl.when(i == 0)
        def _():
            for r in (dcw_ref, dgq_ref, dgk_ref, dsk_ref, dgco_ref, dgao_ref):
                r[...] = jnp.zeros_like(r)

        acc_ref[...] = jnp.zeros_like(acc_ref)
        live_rows = jnp.where(i < nt - 1, te, tq)
        dxb = dx_ref[...].astype(BF16)
        dxe = jnp.concatenate([dxb, dxn_ref[...].astype(BF16)], axis=0)
        dcn = _nt(dxe, wo_ref[0:CC, :])
        bg = jnp.concatenate([p_ref[:, O_BG:O_BG + CC], bgn_ref[...]], axis=0)
        cg = jnp.concatenate([p_ref[:, O_CG:O_CG + CC], cgn_ref[...]], axis=0)
        hc = jnp.concatenate([p_ref[:, O_HC:O_HC + CC], hcn_ref[...]], axis=0)
        u = cg * hc
        up = jnp.where(i > 0, cgp_ref[...] * hcp_ref[...], 0.0)
        u1, u2 = _conv_taps(jnp.concatenate([up, u], axis=0), te)
        w0, w1, w2 = cw_ref[0:1, :], cw_ref[1:2, :], cw_ref[2:3, :]
        y = w0 * u2 + w1 * u1 + w2 * u
        co = bg * y
        rc, coh = _rms_fwd(co, 1.0 / CC)
        dco = _rms_bwd(dcn, gco_ref[...], coh, rc, 1.0 / CC)
        row_io = lax.broadcasted_iota(jnp.int32, (te, 1), 0)
        own = row_io < tq
        dgco_ref[...] += jnp.sum(jnp.where(own, dcn * coh, 0.0), axis=0, keepdims=True)
        dyc = jnp.where(row_io < live_rows, dco * bg, 0.0)
        dyo = jnp.where(own, dyc, 0.0)
        dcw_ref[0:1, :] += jnp.sum(dyo * u2, axis=0, keepdims=True)
        dcw_ref[1:2, :] += jnp.sum(dyo * u1, axis=0, keepdims=True)
        dcw_ref[2:3, :] += jnp.sum(dyo * u, axis=0, keepdims=True)
        dy1 = pltpu.roll(dyc, te - 1, 0)[0:tq]
        dy2 = pltpu.roll(dyc, te - 2, 0)[0:tq]
        du = w2 * dyc[0:tq] + w1 * dy1 + w0 * dy2
        dpm_ref[:, O_BG:O_BG + CC] = (dco[0:tq] * y[0:tq]).astype(BF16)
        dpm_ref[:, O_CG:O_CG + CC] = (du * hc[0:tq]).astype(BF16)
        dpm_ref[:, O_HC:O_HC + CC] = (du * cg[0:tq]).astype(BF16)
        kraw = jnp.concatenate([kvp_ref[:, 0:NKV * HP], p_ref[:, O_K:O_K + NKV * HP]], axis=0)
        vraw = jnp.concatenate([kvp_ref[:, NKV * HP:], p_ref[:, O_V:O_V + NKV * HP]], axis=0)
        gqv, gkv = gq_ref[...], gk_ref[...]
        keys = _norm_keys(kraw, gkv)
        vb = [vraw[:, h * HP:(h + 1) * HP].astype(BF16) for h in range(NKV)]
        base_valid, c_io = _band_mask()
        lane = lax.broadcasted_iota(jnp.int32, (1, HP), 1)
        dgq, dgk, dsk = (jnp.zeros((1, HP), F32) for _ in range(3))
        dgao = jnp.zeros((1, NQ * HD), F32)
        for b in range(nb):
            lo = jnp.where(i * nb + b == 0, BLK, 0)
            valid = base_valid & (c_io >= lo)
            band = slice(b * BLK, b * BLK + 2 * BLK)
            blk = slice(b * BLK, (b + 1) * BLK)
            ra, aoh = _rms_fwd(ao_ref[blk, :], 1.0 / (NQ * HD))
            danb = _nt(dxb[blk], wo_ref[CC:MIXW, :])
            dgao = dgao + jnp.sum(danb * aoh, axis=0, keepdims=True)
            dao = _rms_bwd(danb, gao_ref[...], aoh, ra, 1.0 / (NQ * HD))
            dos = [dao[:, g // 2 * HP:(g // 2 + 1) * HP] for g in range(NQ)]
            dos = [(d if g % 2 == 0 else pltpu.roll(d, HD, 1)).astype(BF16) for g, d in enumerate(dos)]
            fwd = []
            for g in range(NQ):
                rq, qh = _rms_fwd(p_ref[blk, O_Q + g * HP:O_Q + (g + 1) * HP], 1.0 / HD)
                qs = (qh * (gqv * SCALE)).astype(BF16)
                fwd.append((rq, qh, qs) + _attn_probs(qs, keys[g // GRP][2][band], sk_ref[0, g], valid))
            dqs = []
            for h in range(NKV):
                khat, rk, kn = [a[band] for a in keys[h]]
                dss, prbs, qns, dobs = [], [], [], []
                for g in range(h * GRP, (h + 1) * GRP):
                    rq, qh, qs, pr, ps = fwd[g]
                    dob = dos[g]
                    dp = _nt(dob, vb[h][band])
                    delta = jnp.sum(pr * dp, axis=-1, keepdims=True)
                    dsb = (pr * (dp - delta)).astype(BF16)
                    dsk = dsk + jnp.where(lane == g, -jnp.sum(ps * delta, axis=0, keepdims=True), 0.0)
                    dqn = jnp.dot(dsb, kn, preferred_element_type=F32) * SCALE
                    dgq = dgq + jnp.sum(dqn * qh, axis=0, keepdims=True)
                    dqs.append(_rms_bwd(dqn, gqv, qh, rq, 1.0 / HD).astype(BF16))
                    dss.append(dsb)
                    prbs.append(pr.astype(BF16))
                    qns.append(qs)
                    dobs.append(dob)
                dkn = _tn(jnp.concatenate(dss, axis=0), jnp.concatenate(qns, axis=0))
                dv = _tn(jnp.concatenate(prbs, axis=0), jnp.concatenate(dobs, axis=0))
                dgk = dgk + jnp.sum(dkn * khat, axis=0, keepdims=True)
                acc_ref[band, h * HP:(h + 1) * HP] += _rms_bwd(dkn, gkv, khat, rk, 1.0 / HD)
                acc_ref[band, (NKV + h) * HP:(NKV + h + 1) * HP] += dv
            dpm_ref[blk, O_Q:O_K] = jnp.concatenate(dqs, axis=1)
        dgq_ref[...] += dgq
        dgk_ref[...] += dgk
        dsk_ref[...] += dsk
        dgao_ref[...] += dgao
        dkvh_ref[...] = acc_ref[0:BLK, :]
        dkvm_ref[...] = acc_ref[BLK:, :]

    prev8 = lambda col: pl.BlockSpec((8, CC), lambda i: (jnp.maximum(i * r8 - 1, 0), col))
    next8 = lambda col: pl.BlockSpec((8, CC), lambda i: (jnp.minimum((i + 1) * r8, t // 8 - 1), col))
    small = lambda n: pl.BlockSpec((1, n), lambda i: (0, 0))
    return pl.pallas_call(
        body, name="mixer_bwd", grid=(nt,),
        in_specs=[
            pl.BlockSpec((tq, D), lambda i: (i, 0)),
            pl.BlockSpec((8, D), lambda i: (jnp.minimum((i + 1) * r8, t // 8 - 1), 0)),
            pl.BlockSpec((tq, NP), lambda i: (i, 0)),
            prev8(O_CG // CC), prev8(O_HC // CC),
            next8(O_BG // CC), next8(O_CG // CC), next8(O_HC // CC),
            pl.BlockSpec((BLK, kvw), lambda i: (jnp.maximum(i * nb - 1, 0), O_K // kvw)),
            pl.BlockSpec((tq, NQ * HD), lambda i: (i, 0)),
            _const_spec((8, CC)), _const_spec((1, HP)), _const_spec((1, HP)),
            pl.BlockSpec(memory_space=pltpu.SMEM),
            _const_spec((1, CC)), _const_spec((1, NQ * HD)), _const_spec((MIXW, D)),
        ],
        out_specs=[
            pl.BlockSpec((tq, NMAIN), lambda i: (i, 0)),
            pl.BlockSpec((tq, kvw), lambda i: (i, 0)),
            pl.BlockSpec((BLK, kvw), lambda i: (i, 0)),
            pl.BlockSpec((8, CC), lambda i: (0, 0)), small(HP), small(HP), small(HP), small(CC), small(NQ * HD),
        ],
        out_shape=[
            jax.ShapeDtypeStruct((t, NMAIN), BF16), jax.ShapeDtypeStruct((t, kvw), F32),
            jax.ShapeDtypeStruct((nt * BLK, kvw), F32),
            jax.ShapeDtypeStruct((8, CC), F32), jax.ShapeDtypeStruct((1, HP), F32), jax.ShapeDtypeStruct((1, HP), F32),
            jax.ShapeDtypeStruct((1, HP), F32), jax.ShapeDtypeStruct((1, CC), F32),
            jax.ShapeDtypeStruct((1, NQ * HD), F32),
        ],
        scratch_shapes=[pltpu.VMEM((tq + BLK, kvw), F32)],
        compiler_params=_cparams(("arbitrary",)),
    )(dxm, dxm, proj, proj, proj, proj, proj, proj, proj, ao, cw, gq, gk, sinks, gco, gao, wo)


def _inproj_bwd(dpm, dkvm, dkvh, wpt, x, g1, dxm, tm):
    t = x.shape[0]
    kvw = 2 * NKV * HP
    nt = t // tm

    def body(dp_ref, dk_ref, dh_ref, w_ref, x_ref, g_ref, dxm_ref, dx_ref, dg_ref, dkv_ref):
        i = pl.program_id(0)

        @pl.when(i == 0)
        def _():
            dg_ref[...] = jnp.zeros_like(dg_ref)

        halo = jnp.where(i < nt - 1, dh_ref[...], 0.0)
        dkv_ref[0:tm - BLK, :] = dk_ref[0:tm - BLK, :].astype(BF16)
        dkv_ref[tm - BLK:tm, :] = (dk_ref[tm - BLK:tm, :] + halo).astype(BF16)
        dh = (jnp.dot(dp_ref[...], w_ref[0:NMAIN, :], preferred_element_type=F32)
              + jnp.dot(dkv_ref[...], w_ref[NMAIN:NP, :], preferred_element_type=F32))
        r, xh = _rms_fwd(x_ref[...], 1.0 / D)
        dg_ref[...] += jnp.sum(dh * xh, axis=0, keepdims=True)
        dx_ref[...] = dxm_ref[...] + _rms_bwd(dh, g_ref[...], xh, r, 1.0 / D)

    row = lambda w: pl.BlockSpec((tm, w), lambda i: (i, 0))
    return pl.pallas_call(
        body, name="inproj_bwd", grid=(nt,),
        in_specs=[row(NMAIN), row(kvw), pl.BlockSpec((BLK, kvw), lambda i: (jnp.minimum(i + 1, nt - 1), 0)),
                  _const_spec((NP, D)), row(D), _const_spec((1, D)), row(D)],
        out_specs=[row(D), pl.BlockSpec((1, D), lambda i: (0, 0)), row(kvw)],
        out_shape=[jax.ShapeDtypeStruct((t, D), F32), jax.ShapeDtypeStruct((1, D), F32),
                   jax.ShapeDtypeStruct((t, kvw), BF16)],
        compiler_params=_cparams(("arbitrary",)),
    )(dpm, dkvm, dkvh, wpt, x, g1, dxm)


def _rows_tile(rows, cap=512):
    for cand in range(min(rows, cap) // 16 * 16, 0, -16):
        if rows % cand == 0:
            return cand
    return rows


def _presum_halves(gs, theirs, core):
    n = len(gs)

    def body(c_ref, *refs):
        for g_ref, t_ref, o_ref in zip(refs[:n], refs[n:2 * n], refs[2 * n:]):
            o_ref[...] = (g_ref[...].astype(F32) + t_ref[...].astype(F32)).astype(BF16)

    half = lambda ta: pl.BlockSpec((None,) + ta.shape[1:], lambda k, c_ref: (k, 0, 0))
    own = lambda ta: pl.BlockSpec((None,) + ta.shape[1:], lambda k, c_ref: (k, c_ref[0], 0))
    return pl.pallas_call(
        body, name="presum",
        grid_spec=pltpu.PrefetchScalarGridSpec(
            num_scalar_prefetch=1, grid=(N_CHIPS,),
            in_specs=[own(ta) for ta in theirs] + [half(ta) for ta in theirs],
            out_specs=[half(ta) for ta in theirs]),
        out_shape=[jax.ShapeDtypeStruct(ta.shape, BF16) for ta in theirs],
        compiler_params=_cparams(("parallel",)),
    )(core, *gs, *theirs)


def _sum_chips(got, ps, chip):
    n = len(got)
    steps = 2

    def body(chip_ref, *refs):
        for c_ref, own_ref, o_ref in zip(refs[:n], refs[n:2 * n], refs[2 * n:]):
            acc = None
            for j in range(N_CHIPS):
                term = jnp.where(chip_ref[0] == j, own_ref[...], c_ref[j]).astype(F32)
                acc = term if acc is None else acc + term
            o_ref[...] = acc

    tile = lambda c: (c.shape[1] // steps, c.shape[2])
    return pl.pallas_call(
        body, name="chipsum",
        grid_spec=pltpu.PrefetchScalarGridSpec(
            num_scalar_prefetch=1, grid=(steps,),
            in_specs=[pl.BlockSpec((N_CHIPS,) + tile(c), lambda i, chip_ref: (0, i, 0)) for c in got]
            + [pl.BlockSpec((None,) + tile(c), lambda i, chip_ref: (chip_ref[0], i, 0)) for c in got],
            out_specs=[pl.BlockSpec(tile(c), lambda i, chip_ref: (i, 0)) for c in got]),
        out_shape=[jax.ShapeDtypeStruct(c.shape[1:], F32) for c in got],
        compiler_params=_cparams(("parallel",)),
    )(chip, *got, *ps)


def _adamw(w, g, m, v, name):
    rows, cols = w.shape
    tr = _rows_tile(rows, 256)
    c1 = 1.0 - ADAM_B1 ** ADAM_STEP
    c2 = 1.0 - ADAM_B2 ** ADAM_STEP

    def body(w_ref, g_ref, m_ref, v_ref, d_ref, mo_ref, vo_ref):
        gv = g_ref[...]
        mn = ADAM_B1 * m_ref[...] + (1.0 - ADAM_B1) * gv
        vn = ADAM_B2 * v_ref[...] + (1.0 - ADAM_B2) * (gv * gv)
        mo_ref[...] = mn
        vo_ref[...] = vn
        d_ref[...] = -ADAM_LR * ((mn / c1) / (jnp.sqrt(vn / c2) + ADAM_EPS) + ADAM_WD * w_ref[...])

    spec = pl.BlockSpec((tr, cols), lambda i: (i, 0))
    sds = jax.ShapeDtypeStruct((rows, cols), F32)
    return pl.pallas_call(
        body, name=name, grid=(rows // tr,), in_specs=[spec] * 4, out_specs=[spec] * 3, out_shape=[sds] * 3,
        compiler_params=_cparams(("parallel",)),
    )(w, g, m, v)


def _place():
    x, y, c = lax.axis_index("x"), lax.axis_index("y"), lax.axis_index("c")
    chips = [(1 - x, y), (x, 1 - y), (1 - x, 1 - y)]
    return x, y, c, chips


ANY = pl.BlockSpec(memory_space=pl.ANY)
DMA_ROWS = 64


def _pieces(shape):
    rows = shape[-2]
    step = DMA_ROWS if rows % DMA_ROWS == 0 else rows
    lead = [()]
    for n in shape[:-2]:
        lead = [i + (k,) for i in lead for k in range(n)]
    return [i + (pl.ds(r0, step),) for i in lead for r0 in range(0, rows, step)]


def _start_pieces(make, src, dst):
    for idx in _pieces(src.shape):
        make(src.at[idx], dst.at[idx]).start()


def _gather_body(srcs, outs, sems, layer, start):
    nw = len(srcs)
    ssem, rsem, fssem, frsem = sems
    x, y, c, chips = _place()
    kme = 2 * x + y

    def plane(j, w, to):
        return lambda s, d: pltpu.make_async_remote_copy(
            src_ref=s, dst_ref=d, send_sem=ssem.at[j, w], recv_sem=rsem.at[j, w], device_id=to,
            device_id_type=MESH)

    if layer is None:
        for j, (px, py) in enumerate(chips):
            for w in range(nw):
                start(plane(j, w, (px, py, c)), srcs[w], outs[w].at[kme])
        for j, (px, py) in enumerate(chips):
            for w in range(nw):
                got = outs[w].at[2 * px + py]
                plane(j, w, (px, py, c))(got, got).wait()
        return

    def passed(j, w):
        return lambda s, d: pltpu.make_async_remote_copy(
            src_ref=s, dst_ref=d, send_sem=fssem.at[j, w], recv_sem=frsem.at[j, w],
            device_id=(x, y, 1 - c), device_id_type=MESH)

    @pl.when(c == layer)
    def _():
        for j, (px, py) in enumerate(chips):
            for w in range(nw):
                start(plane(j, w, (px, py, c)), srcs[w], outs[w].at[kme])
        for j, (px, py) in enumerate(chips):
            for w in range(nw):
                got = outs[w].at[2 * px + py]
                plane(j, w, (px, py, c))(got, got).wait_recv()
                start(passed(j, w), got, got)
        for j, (px, py) in enumerate(chips):
            for w in range(nw):
                got = outs[w].at[2 * px + py]
                plane(j, w, (px, py, c))(got, got).wait_send()
                passed(j, w)(got, got).wait_send()

    @pl.when(c != layer)
    def _():
        for j, (px, py) in enumerate(chips):
            for w in range(nw):
                got = outs[w].at[2 * px + py]
                passed(j, w)(got, got).wait_recv()


def _handshake(peers):
    barrier = pltpu.get_barrier_semaphore()
    for peer in peers:
        pl.semaphore_signal(barrier, inc=1, device_id=peer, device_id_type=MESH)
    pl.semaphore_wait(barrier, len(peers))


def _handshake_all():
    x, y, c, _ = _place()
    _handshake([(x ^ (r >> 2), y ^ ((r >> 1) & 1), c ^ (r & 1)) for r in range(1, 8)])


def _gather_layer_async(blocks, layer, name, collective_id):
    hbm = pltpu.MemorySpace.HBM
    srcs = [jax.new_ref(b, memory_space=hbm) for b in blocks]
    outs = [jax.empty_ref(jax.ShapeDtypeStruct((N_CHIPS,) + b.shape, b.dtype), memory_space=hbm) for b in blocks]

    @pl.kernel(mesh=plsc.ScalarSubcoreMesh(axis_name="seq", num_cores=1), name=name,
               scratch_types=[pltpu.SemaphoreType.DMA((3, len(blocks)))] * 4,
               compiler_params=pltpu.CompilerParams(collective_id=collective_id))
    def launch(*sems):
        _handshake_all()
        _gather_body(srcs, outs, sems, layer, lambda make, s, d: make(s, d).start())

    launch()
    return [o[...] for o in outs]


def _swap_siblings(arrs, halves, name, collective_id=None):
    nw = len(arrs)
    out_sds = [jax.ShapeDtypeStruct((a.shape[0], a.shape[1] // 2, a.shape[2]) if halves else a.shape, a.dtype)
               for a in arrs]

    def exchange(srcs, outs, ssem, rsem, start):
        x, y, c, _ = _place()

        def give(w):
            return lambda s, d: pltpu.make_async_remote_copy(
                src_ref=s, dst_ref=d, send_sem=ssem.at[w], recv_sem=rsem.at[w], device_id=(x, y, 1 - c),
                device_id_type=MESH)

        for w in range(nw):
            hr = outs[w].shape[1]
            start(give(w), srcs[w].at[:, pl.ds((1 - c) * hr, hr)] if halves else srcs[w], outs[w])
        for w in range(nw):
            give(w)(outs[w], outs[w]).wait()

    if collective_id is None:
        def body(*refs):
            exchange(refs[:nw], refs[nw:2 * nw], *refs[2 * nw:], _start_pieces)

        return pl.pallas_call(
            body, name=name, in_specs=[ANY] * nw, out_specs=[ANY] * nw, out_shape=out_sds,
            scratch_shapes=[pltpu.SemaphoreType.DMA((nw,))] * 2,
            compiler_params=_cparams(has_side_effects=True),
        )(*arrs)

    hbm = pltpu.MemorySpace.HBM
    srcs = [jax.new_ref(a, memory_space=hbm) for a in arrs]
    outs = [jax.empty_ref(sds, memory_space=hbm) for sds in out_sds]

    @pl.kernel(mesh=plsc.ScalarSubcoreMesh(axis_name="seq", num_cores=1), name=name,
               scratch_types=[pltpu.SemaphoreType.DMA((nw,))] * 2,
               compiler_params=pltpu.CompilerParams(collective_id=collective_id))
    def launch(ssem, rsem):
        x, y, c, _ = _place()
        _handshake([(x, y, 1 - c)])
        exchange(srcs, outs, ssem, rsem, lambda make, s, d: make(s, d).start())

    launch()
    return [o[...] for o in outs]


def _scatter_body(srcs, outs, sems, start):
    nw = len(srcs)
    ssem, rsem = sems
    x, y, c, chips = _place()
    kme = 2 * x + y

    def give(j, w, to):
        return lambda s, d: pltpu.make_async_remote_copy(
            src_ref=s, dst_ref=d, send_sem=ssem.at[j, w], recv_sem=rsem.at[j, w], device_id=to,
            device_id_type=MESH)

    for j, (px, py) in enumerate(chips):
        for w in range(nw):
            start(give(j, w, (px, py, c)), srcs[w].at[2 * px + py], outs[w].at[kme])
    for j, (px, py) in enumerate(chips):
        for w in range(nw):
            got = outs[w].at[2 * px + py]
            give(j, w, (px, py, c))(got, got).wait_recv()
    for j, (px, py) in enumerate(chips):
        for w in range(nw):
            sent = srcs[w].at[2 * px + py]
            give(j, w, (px, py, c))(sent, sent).wait_send()


def _scatter_chips_async(ps, name, collective_id):
    hbm = pltpu.MemorySpace.HBM
    srcs = [jax.new_ref(p, memory_space=hbm) for p in ps]
    outs = [jax.empty_ref(jax.ShapeDtypeStruct(p.shape, p.dtype), memory_space=hbm) for p in ps]

    @pl.kernel(mesh=plsc.ScalarSubcoreMesh(axis_name="seq", num_cores=1), name=name,
               scratch_types=[pltpu.SemaphoreType.DMA((3, len(ps)))] * 2,
               compiler_params=pltpu.CompilerParams(collective_id=collective_id))
    def launch(*sems):
        _handshake_all()
        _scatter_body(srcs, outs, sems, lambda make, s, d: make(s, d).start())

    launch()
    return [o[...] for o in outs]


def _allreduce_small(v):
    rows = v.shape[0]

    def body(v_ref, o_ref, buf, ssem, rsem):
        x, y, c, _ = _place()
        me = 4 * x + 2 * y + c
        buf[me] = v_ref[...]
        sends = []
        for r in range(1, 8):
            peer = (x ^ (r >> 2), y ^ ((r >> 1) & 1), c ^ (r & 1))
            cp = pltpu.make_async_remote_copy(
                src_ref=v_ref, dst_ref=buf.at[me], send_sem=ssem.at[r - 1], recv_sem=rsem.at[r - 1],
                device_id=peer, device_id_type=MESH)
            cp.start()
            sends.append(cp)
        for r in range(1, 8):
            src = me ^ r
            pltpu.make_async_remote_copy(
                src_ref=v_ref, dst_ref=buf.at[src], send_sem=ssem.at[r - 1], recv_sem=rsem.at[r - 1],
                device_id=(x, y, c), device_id_type=MESH).wait_recv()
        for cp in sends:
            cp.wait_send()
        acc = buf[0]
        for d in range(1, 8):
            acc = acc + buf[d]
        o_ref[...] = acc

    vm = pl.BlockSpec(memory_space=pltpu.VMEM)
    return pl.pallas_call(
        body, name="allreduce_small", in_specs=[vm], out_specs=vm,
        out_shape=jax.ShapeDtypeStruct(v.shape, F32),
        scratch_shapes=[pltpu.VMEM((8, rows, 128), F32), pltpu.SemaphoreType.DMA((7,)),
                        pltpu.SemaphoreType.DMA((7,))],
        compiler_params=_cparams(has_side_effects=True),
    )(v)


def _t(w):
    return jnp.swapaxes(w, -1, -2)


def _count(shape):
    n = 1
    for s in shape:
        n *= s
    return n


def _pack_rows(arrs):
    flat = [jnp.pad(a.reshape(-1), (0, (-_count(a.shape)) % 128)) for a in arrs]
    v = jnp.concatenate(flat)
    rows = -(-v.shape[0] // (8 * 128)) * 8
    return jnp.pad(v, (0, rows * 128 - v.shape[0])).reshape(rows, 128)


def kernel(x, norm1_g, w_in, conv_w, q_norm_g, k_norm_g, sinks, conv_out_g, attn_out_g, w_o, norm2_g, w_gate, w_up, w_down, loss_target, m_norm1_g, m_w_in, m_conv_w, m_q_norm_g, m_k_norm_g, m_sinks, m_conv_out_g, m_attn_out_g, m_w_o, m_norm2_g, m_w_gate, m_w_up, m_w_down, v_norm1_g, v_w_in, v_conv_w, v_q_norm_g, v_k_norm_g, v_sinks, v_conv_out_g, v_attn_out_g, v_w_o, v_norm2_g, v_w_gate, v_w_up, v_w_down):
    depth = w_in.shape[0]
    t = x.shape[1]
    xs = x.reshape(t, D)
    tgt = loss_target.reshape(t, D)
    xi, yi = lax.axis_index("x"), lax.axis_index("y")
    kme = 2 * xi + yi
    tm = min(512, t)
    tq = min(512, t)
    tf = min(256, t)
    tw = min(1024, t)

    cwp = jnp.pad(conv_w.reshape(depth * 3, CC // N_CHIPS), ((0, 8 - depth * 3), (0, 0)))
    own_f = [jnp.concatenate([_t(w_gate[l]), _t(w_up[l]), w_down[l]], axis=0).astype(BF16) for l in range(depth)]
    own_o = [w_o[l].astype(BF16) for l in range(depth)]
    own_i = [_t(w_in[l]).astype(BF16) for l in range(depth)]
    mine = lambda got, own: lax.dynamic_update_index_in_dim(got, own, kme, 0)
    (got_i0,) = _gather_layer_async([own_i[0]], None, "gather_in0_seq", collective_id=14)
    got_ocw = _gather_layer_async([own_o[0], cwp], None, "gather_o0_seq", collective_id=15)
    got_i0, own_f, own_o, own_i = lax.optimization_barrier((got_i0, own_f, own_o, own_i))
    gf0_in = lax.optimization_barrier((own_f[0], got_i0))[0]
    (got_f0,) = _gather_layer_async([gf0_in], 0, "gather_ffn0_seq", collective_id=6)

    chip = kme.reshape(1).astype(jnp.int32)

    def layer_params(l, got_o, cw_full):
        return dict(
            wo=mine(got_o, own_o[l]).reshape(MIXW, D),
            cw=jnp.pad(cw_full[l], ((0, 5), (0, 0))),
            g1=norm1_g[l].reshape(1, D), g2=norm2_g[l].reshape(1, D),
            gq=jnp.pad(q_norm_g[l], (0, HP - HD)).reshape(1, HP), gk=jnp.pad(k_norm_g[l], (0, HP - HD)).reshape(1, HP),
            sk=sinks[l].reshape(1, NQ), gco=conv_out_g[l].reshape(1, CC),
            gao=attn_out_g[l].reshape(1, NQ * HD))

    saved, layers = [], []
    cur = xs
    for l in range(depth):
        x_in = cur
        if l == 0:
            got_i = got_i0
        else:
            got_f1, got_o, got_i = lax.optimization_barrier((got_l1, cur))[0]
        proj, h, wpt = _inproj_fwd(cur, norm1_g[l].reshape(1, D), got_i, own_i[l], chip, tm)
        if l == 0:
            got_o, got_cw = lax.optimization_barrier((got_ocw, proj))[0]
            cw_full = mine(got_cw, cwp).transpose(1, 0, 2).reshape(8, CC)[:depth * 3].reshape(depth, 3, CC)
        p = layer_params(l, got_o, cw_full)
        p["wpt"] = wpt
        xm, mix, ao = _mixer_fwd(proj, cur, p["cw"], p["gq"], p["gk"], p["sk"], p["gco"], p["gao"], p["wo"], tq)
        if l == 0:
            got_f0 = lax.optimization_barrier((got_f0, xm))[0]
            l1_in = lax.optimization_barrier(([own_f[1], own_o[1], own_i[1]], got_f0))[0]
            got_l1 = _gather_layer_async(l1_in, 1, "gather_layer1_seq", collective_id=1)
        p["gf"] = mine(got_f0 if l == 0 else got_f1, own_f[l])
        layers.append(p)
        if l < depth - 1:
            cur, a, b, h2 = _ffn_fwd(xm, p["g2"], p["gf"], tm)
        else:
            lpart, dy, a, b, h2 = _ffn_fwd(xm, p["g2"], p["gf"], tm, tgt)
        saved.append(dict(x=x_in, proj=proj, h=h, xm=xm, mix=mix, ao=ao, a=a, b=b, h2=h2))

    ci = lax.axis_index("c")
    core = ci.reshape(1).astype(jnp.int32)
    rbig = [dict() for _ in range(depth)]
    gsmall = [None] * depth

    def after_(vals, after):
        return vals if after is None else lax.optimization_barrier((vals, after))[0]

    def reduce_1(gs, tag, ids):
        return gs, _swap_siblings(gs, True, f"swap_halves_{tag}_seq", ids[0]), tag, ids

    def reduce_2(state, after):
        gs, theirs, tag, ids = state
        ps = _presum_halves(gs, after_(theirs, after), core)
        return ps, _scatter_chips_async(ps, f"scatter_{tag}_seq", ids[1]), tag, ids

    def reduce_3(state, after):
        ps, got, tag, ids = state
        r_mine = _sum_chips(after_(got, after), ps, chip)
        return r_mine, _swap_siblings(r_mine, False, f"swap_reduced_{tag}" + ("_seq" if ids[2] else ""), ids[2])

    def reduce_4(state, after):
        r_mine, r_theirs = state
        return [jnp.where(ci == 0, jnp.concatenate([a, b], axis=0), jnp.concatenate([b, a], axis=0))
                for a, b in zip(r_mine, after_(r_theirs, after))]

    ids = {"ffn1": (7, 4, 8), "in1": (9, 5, 10), "ffn0": (11, 2, 12), "in0": (13, 3, None)}
    in_2 = None
    handed = {}
    for l in reversed(range(depth)):
        p, s = layers[l], saved[l]
        dxm, da, db, hm, dg2 = _ffn_bwd(dy, s["xm"], p["g2"], s["a"], s["b"], p["gf"], tf)
        if in_2 is not None:
            in_2 = reduce_2(in_2, dxm)
        g_wg = _wgrad_blocks(da, s["h2"], tw, "wgrad_gate")
        g_wu = _wgrad_blocks(db, s["h2"], tw, "wgrad_up")
        g_wd = _wgrad_blocks(hm, dy, tw, "wgrad_down")
        if in_2 is not None:
            handed[f"in{l + 1}"] = reduce_3(in_2, g_wd)
        ffn_1 = reduce_1([g_wg, g_wu, g_wd], f"ffn{l}", ids[f"ffn{l}"])
        dxm = after_(dxm, (g_wg, g_wu, g_wd))
        dpm, dkvm, dkvh, dcw, dgq, dgk, dsk, dgco, dgao = _mixer_bwd(
            dxm, s["proj"], s["ao"], p["cw"], p["gq"], p["gk"], p["sk"], p["gco"], p["gao"], p["wo"], tq)
        ffn_2 = reduce_2(ffn_1, dpm)
        g_o = _wgrad(s["mix"], dxm, tw, "wgrad_o")
        dx, dg1, dkv = _inproj_bwd(dpm, dkvm, dkvh, p["wpt"], s["x"], p["g1"], dxm, tq)
        g_in = jnp.concatenate(
            [_wgrad(dpm, s["h"], tw, "wgrad_in_main", [(0, 0, O_Q)] + _head_rows(O_Q, NQ)),
             _wgrad(dkv, s["h"], tw, "wgrad_in_kv", _head_rows(0, 2 * NKV))], axis=0)
        dy = after_(dx, (g_in, g_o))
        gsmall[l] = dict(g1=dg1, cw=dcw[:3], gq=dgq[0, :HD], gk=dgk[0, :HD], sk=dsk[0, :NQ], gco=dgco,
                         gao=dgao, g2=dg2)
        handed[f"ffn{l}"] = reduce_3(ffn_2, g_in)
        in_2 = reduce_1([g_in.reshape(N_CHIPS, -1, D), g_o.reshape(N_CHIPS, -1, D)], f"in{l}", ids[f"in{l}"])
    grad_x = dy.reshape(x.shape)

    small_shapes = dict(g1=(D,), cw=(3, CC), gq=(HD,), gk=(HD,), sk=(NQ,), gco=(CC,), gao=(NQ * HD,), g2=(D,))
    red = _allreduce_small(_pack_rows([gsmall[l][n] for l in range(depth) for n in small_shapes]
                                      + [lpart[0:1, 0:1]])).reshape(-1)
    red_small, offs = {n: [] for n in small_shapes}, 0
    for l in range(depth):
        for n, shp in small_shapes.items():
            cnt = _count(shp)
            red_small[n].append(red[offs:offs + cnt].reshape(shp))
            offs += -(-cnt // 128) * 128
    loss = red[offs]
    g_small = {n: jnp.stack(v) for n, v in red_small.items()}
    g_cw = lax.dynamic_slice_in_dim(g_small["cw"], kme * (CC // N_CHIPS), CC // N_CHIPS, axis=2)

    weights = [norm1_g, w_in, conv_w, q_norm_g, k_norm_g, sinks, conv_out_g, attn_out_g, w_o, norm2_g, w_gate,
               w_up, w_down]
    moms = [m_norm1_g, m_w_in, m_conv_w, m_q_norm_g, m_k_norm_g, m_sinks, m_conv_out_g, m_attn_out_g, m_w_o,
            m_norm2_g, m_w_gate, m_w_up, m_w_down]
    vars_ = [v_norm1_g, v_w_in, v_conv_w, v_q_norm_g, v_k_norm_g, v_sinks, v_conv_out_g, v_attn_out_g, v_w_o,
             v_norm2_g, v_w_gate, v_w_up, v_w_down]
    n_w = len(weights)
    big_idx = dict(zip(("in", "o", "g", "u", "d"), (1, 8, 10, 11, 12)))
    small_idx = [n for n in range(n_w) if n not in big_idx.values()]
    grads, deltas, new_m, new_v = [None] * n_w, [None] * n_w, [None] * n_w, [None] * n_w
    for n, g in zip(small_idx, (g_small["g1"], g_cw, g_small["gq"], g_small["gk"], g_small["sk"], g_small["gco"],
                                g_small["gao"], g_small["g2"])):
        grads[n] = g

    def update_big(name):
        n = big_idx[name]
        g = jnp.stack([rbig[l][name] for l in range(depth)])
        flip = g.shape != weights[n].shape
        rows2d = lambda a3: (_t(a3) if flip else a3).reshape(-1, D)
        res = _adamw(rows2d(weights[n]), g.reshape(-1, D), rows2d(moms[n]), rows2d(vars_[n]), f"adamw_{n}")
        res = [g] + [r.reshape(g.shape) for r in res]
        grads[n], deltas[n], new_m[n], new_v[n] = [_t(r) for r in res] if flip else res

    for l in range(depth):
        rbig[l]["g"], rbig[l]["u"], rbig[l]["d"] = reduce_4(handed[f"ffn{l}"], red)
    rbig[1]["in"], rbig[1]["o"] = reduce_4(handed["in1"], red)
    update_big("g")
    in_2 = reduce_2(in_2, new_v[big_idx["g"]])
    update_big("u")
    update_big("d")
    rbig[0]["in"], rbig[0]["o"] = reduce_4(reduce_3(in_2, new_v[big_idx["d"]]), None)
    for name in ("in", "o"):
        update_big(name)
    res = _adamw(*[_pack_rows([arrs[n] for n in small_idx]) for arrs in (weights, grads, moms, vars_)],
                 "adamw_small")
    offs = 0
    for n in small_idx:
        shp = weights[n].shape
        cnt = _count(shp)
        deltas[n], new_m[n], new_v[n] = [r.reshape(-1)[offs:offs + cnt].reshape(shp) for r in res]
        offs += -(-cnt // 128) * 128
    return (loss, grad_x, *grads, *deltas, *new_m, *new_v)
```

```python
import jax
import jax.numpy as jnp
from jax import lax
from jax.experimental import pallas as pl
from jax.experimental.pallas import tpu as pltpu
from jax.experimental.pallas import tpu_sc as plsc

F32 = jnp.float32
BF16 = jnp.bfloat16

D = 1024
CC = 512
NQ = 8
NKV = 2
HD = 64
HP = 128
GRP = NQ // NKV
FF = 2816
FFB = FF // 4
BLK = 128
EPS = 1e-6
NEG = -1e30
SCALE = HD ** -0.5
O_BG, O_CG, O_HC, O_Q = 0, CC, 2 * CC, 3 * CC
O_K = O_Q + NQ * HP
O_V = O_K + NKV * HP
NP = O_V + NKV * HP
NMAIN = O_K
MIXW = CC + NQ * HD
N_CHIPS = 4
VMEM_LIMIT = 56 * 1024 * 1024
MESH = pl.DeviceIdType.MESH

ADAM_LR, ADAM_B1, ADAM_B2, ADAM_EPS, ADAM_WD, ADAM_STEP = 0.001, 0.9, 0.999, 1e-08, 0.01, 10


def _cparams(sem=None, **kw):
    if sem is not None:
        kw["dimension_semantics"] = sem
    return pltpu.CompilerParams(vmem_limit_bytes=VMEM_LIMIT, **kw)


def _const_spec(shape):
    nd = len(shape)
    return pl.BlockSpec(shape, lambda *_: (0,) * nd, pipeline_mode=pl.Buffered(1))


def _nt(a, b):
    return lax.dot_general(a, b, (((1,), (1,)), ((), ())), preferred_element_type=F32)


def _tn(a, b):
    return lax.dot_general(a, b, (((0,), (0,)), ((), ())), preferred_element_type=F32)


def _rms_fwd(x, inv_n):
    r = lax.rsqrt(jnp.sum(x * x, axis=-1, keepdims=True) * inv_n + EPS)
    return r, x * r


def _rms_bwd(dy, g, xh, r, inv_n):
    dxh = dy * g
    return r * (dxh - xh * (jnp.sum(dxh * xh, axis=-1, keepdims=True) * inv_n))


W_IN_ROWS = 3 * CC + (NQ + 2 * NKV) * HD
W_IN_BLOCK = W_IN_ROWS // N_CHIPS


def _padded_row(row):
    return row + max(row - O_Q, 0) // HD * (HP - HD)


def _w_in_pieces(k):
    first = k * W_IN_BLOCK
    plain = min(max(O_Q - first, 0), W_IN_BLOCK)
    pieces = [(0, first, plain)] if plain else []
    return pieces + [(r, _padded_row(first + r), HD) for r in range(plain, W_IN_BLOCK, HD)]


def _inproj_fwd(x, g1, gi, own_i, chip, tm):
    t = x.shape[0]

    def body(chip_ref, x_ref, g_ref, gi_ref, own_ref, p_ref, h_ref, w_ref, sem):
        @pl.when(pl.program_id(0) == 0)
        def _():
            for k in range(N_CHIPS):
                for src, dst, rows in _w_in_pieces(k):
                    @pl.when(chip_ref[0] == k)
                    def _():
                        pltpu.make_async_copy(own_ref.at[pl.ds(src, rows)], w_ref.at[pl.ds(dst, rows)], sem).start()

                    @pl.when(chip_ref[0] != k)
                    def _():
                        pltpu.make_async_copy(gi_ref.at[k, pl.ds(src, rows)], w_ref.at[pl.ds(dst, rows)], sem).start()
            for slot in range(NQ + 2 * NKV):
                w_ref[O_Q + slot * HP + HD:O_Q + (slot + 1) * HP, :] = jnp.zeros((HP - HD, D), BF16)
            landed = w_ref.at[pl.ds(0, W_IN_ROWS)]
            pltpu.make_async_copy(landed, landed, sem).wait()

        _, xh = _rms_fwd(x_ref[...], 1.0 / D)
        h = (xh * g_ref[...]).astype(BF16)
        h_ref[...] = h
        p_ref[...] = _nt(h, w_ref[...])

    const = lambda shape: pl.BlockSpec(shape, lambda i, c: (0,) * len(shape))
    return pl.pallas_call(
        body, name="inproj_fwd",
        grid_spec=pltpu.PrefetchScalarGridSpec(
            num_scalar_prefetch=1, grid=(t // tm,),
            in_specs=[pl.BlockSpec((tm, D), lambda i, c: (i, 0)), const((1, D)), ANY, ANY],
            out_specs=[pl.BlockSpec((tm, NP), lambda i, c: (i, 0)), pl.BlockSpec((tm, D), lambda i, c: (i, 0)),
                       const((NP, D))],
            scratch_shapes=[pltpu.SemaphoreType.DMA(())]),
        out_shape=[jax.ShapeDtypeStruct((t, NP), F32), jax.ShapeDtypeStruct((t, D), BF16),
                   jax.ShapeDtypeStruct((NP, D), BF16)],
        compiler_params=_cparams(("arbitrary",)),
    )(chip, x, g1, gi, own_i)


def _band_mask():
    r_io = lax.broadcasted_iota(jnp.int32, (BLK, 2 * BLK), 0)
    c_io = lax.broadcasted_iota(jnp.int32, (BLK, 2 * BLK), 1)
    return (c_io > r_io) & (c_io <= r_io + BLK), c_io


def _conv_taps(uf, n):
    u1 = pltpu.roll(uf, 1, 0)[8:8 + n]
    u2 = pltpu.roll(uf, 2, 0)[8:8 + n]
    return u1, u2


def _attn_probs(qs, kband, sink, valid):
    s = jnp.where(valid, _nt(qs, kband), NEG)
    m = jnp.maximum(jnp.max(s, axis=-1, keepdims=True), sink)
    p = jnp.exp(s - m)
    es = jnp.exp(sink - m)
    inv = 1.0 / (jnp.sum(p, axis=-1, keepdims=True) + es)
    return p * inv, es * inv


def _norm_keys(kraw, gk):
    out = []
    for h in range(NKV):
        kh = kraw[:, h * HP:(h + 1) * HP]
        rk, khat = _rms_fwd(kh, 1.0 / HD)
        out.append((khat, rk, (khat * gk).astype(BF16)))
    return out


def _mixer_fwd(proj, x, cw, gq, gk, sinks, gco, gao, wo, tq):
    t = proj.shape[0]
    nb = tq // BLK
    r8 = tq // 8

    def body(p_ref, cgp_ref, hcp_ref, kvp_ref, x_ref, cw_ref, gq_ref, gk_ref, sk_ref, gco_ref, gao_ref,
             wo_ref, xm_ref, mix_ref, ao_ref, aop_ref):
        i = pl.program_id(0)
        cg = p_ref[:, O_CG:O_CG + CC]
        hc = p_ref[:, O_HC:O_HC + CC]
        u = cg * hc
        up = jnp.where(i > 0, cgp_ref[...] * hcp_ref[...], 0.0)
        u1, u2 = _conv_taps(jnp.concatenate([up, u], axis=0), tq)
        y = cw_ref[0:1, :] * u2 + cw_ref[1:2, :] * u1 + cw_ref[2:3, :] * u
        co = p_ref[:, O_BG:O_BG + CC] * y
        _, coh = _rms_fwd(co, 1.0 / CC)
        cn = coh * gco_ref[...]
        kraw = jnp.concatenate([kvp_ref[:, 0:NKV * HP], p_ref[:, O_K:O_K + NKV * HP]], axis=0)
        vraw = jnp.concatenate([kvp_ref[:, NKV * HP:], p_ref[:, O_V:O_V + NKV * HP]], axis=0)
        keys = _norm_keys(kraw, gk_ref[...])
        vb = [vraw[:, h * HP:(h + 1) * HP].astype(BF16) for h in range(NKV)]
        base_valid, c_io = _band_mask()
        gqs = gq_ref[...] * SCALE
        for b in range(nb):
            lo = jnp.where(i * nb + b == 0, BLK, 0)
            valid = base_valid & (c_io >= lo)
            for g in range(NQ):
                h = g // GRP
                qg = p_ref[b * BLK:(b + 1) * BLK, O_Q + g * HP:O_Q + (g + 1) * HP]
                _, qh = _rms_fwd(qg, 1.0 / HD)
                qs = (qh * gqs).astype(BF16)
                pr, _ = _attn_probs(qs, keys[h][2][b * BLK:b * BLK + 2 * BLK], sk_ref[0, g], valid)
                aop_ref[b * BLK:(b + 1) * BLK, g * HP:(g + 1) * HP] = jnp.dot(
                    pr.astype(BF16), vb[h][b * BLK:b * BLK + 2 * BLK], preferred_element_type=F32)
        for j in range(NQ // 2):
            ao_ref[:, j * HP:(j + 1) * HP] = (aop_ref[:, 2 * j * HP:(2 * j + 1) * HP]
                                              + pltpu.roll(aop_ref[:, (2 * j + 1) * HP:(2 * j + 2) * HP], HD, 1))
        _, aoh = _rms_fwd(ao_ref[...], 1.0 / (NQ * HD))
        an = aoh * gao_ref[...]
        mix = jnp.concatenate([cn, an], axis=1).astype(BF16)
        mix_ref[...] = mix
        xm_ref[...] = x_ref[...] + jnp.dot(mix, wo_ref[...], preferred_element_type=F32)

    prev8 = lambda col: pl.BlockSpec((8, CC), lambda i: (jnp.maximum(i * r8 - 1, 0), col))
    return pl.pallas_call(
        body, name="mixer_fwd", grid=(t // tq,),
        in_specs=[
            pl.BlockSpec((tq, NP), lambda i: (i, 0)),
            prev8(O_CG // CC), prev8(O_HC // CC),
            pl.BlockSpec((BLK, 2 * NKV * HP), lambda i: (jnp.maximum(i * nb - 1, 0), O_K // (2 * NKV * HP))),
            pl.BlockSpec((tq, D), lambda i: (i, 0)),
            _const_spec((8, CC)), _const_spec((1, HP)), _const_spec((1, HP)),
            pl.BlockSpec(memory_space=pltpu.SMEM),
            _const_spec((1, CC)), _const_spec((1, NQ * HD)), _const_spec((MIXW, D)),
        ],
        out_specs=[pl.BlockSpec((tq, D), lambda i: (i, 0)), pl.BlockSpec((tq, MIXW), lambda i: (i, 0)),
                   pl.BlockSpec((tq, NQ * HD), lambda i: (i, 0))],
        out_shape=[jax.ShapeDtypeStruct((t, D), F32), jax.ShapeDtypeStruct((t, MIXW), BF16),
                   jax.ShapeDtypeStruct((t, NQ * HD), F32)],
        scratch_shapes=[pltpu.VMEM((tq, NQ * HP), F32)],
        compiler_params=_cparams(("parallel",)),
    )(proj, proj, proj, proj, x, cw, gq, gk, sinks, gco, gao, wo)


def _ffn_weight_specs():
    return [pl.BlockSpec((N_CHIPS, FFB, D), lambda i, j=j: (0, j, 0), pipeline_mode=pl.Buffered(1))
            for j in range(3)]


def _ffn_fwd(xm, g2, gf, tm, tgt=None):
    t = xm.shape[0]
    last = tgt is not None

    def body(x_ref, g_ref, wg_ref, wu_ref, wd_ref, *rest):
        t_ref, rest = (rest[0], rest[1:]) if last else (None, rest)
        l_ref, rest = (rest[0], rest[1:]) if last else (None, rest)
        xo_ref, a_ref, b_ref, h2_ref = rest
        xv = x_ref[...]
        _, xh = _rms_fwd(xv, 1.0 / D)
        h2 = (xh * g_ref[...]).astype(BF16)
        h2_ref[...] = h2
        acc = xv
        for k in range(N_CHIPS):
            a = _nt(h2, wg_ref[k])
            b = _nt(h2, wu_ref[k])
            a_ref[k] = a.astype(BF16)
            b_ref[k] = b.astype(BF16)
            hm = (a * jax.nn.sigmoid(a) * b).astype(BF16)
            acc = acc + jnp.dot(hm, wd_ref[k], preferred_element_type=F32)
        if last:
            @pl.when(pl.program_id(0) == 0)
            def _():
                l_ref[...] = jnp.zeros_like(l_ref)

            e = acc - t_ref[...]
            xo_ref[...] = e * (1.0 / D)
            l_ref[...] += jnp.sum(jnp.sum(e * e, axis=-1, keepdims=True), axis=0, keepdims=True) * (0.5 / D)
        else:
            xo_ref[...] = acc

    row = lambda w: pl.BlockSpec((tm, w), lambda i: (i, 0))
    blk = pl.BlockSpec((N_CHIPS, tm, FFB), lambda i: (0, i, 0))
    bsd = jax.ShapeDtypeStruct((N_CHIPS, t, FFB), BF16)
    return pl.pallas_call(
        body, name="ffn_fwd_loss" if last else "ffn_fwd", grid=(t // tm,),
        in_specs=[row(D), _const_spec((1, D))] + _ffn_weight_specs() + ([row(D)] if last else []),
        out_specs=([pl.BlockSpec((8, 128), lambda i: (0, 0))] if last else []) + [row(D), blk, blk, row(D)],
        out_shape=([jax.ShapeDtypeStruct((8, 128), F32)] if last else [])
        + [jax.ShapeDtypeStruct((t, D), F32), bsd, bsd, jax.ShapeDtypeStruct((t, D), BF16)],
        compiler_params=_cparams(("arbitrary" if last else "parallel",)),
    )(*((xm, g2, gf, gf, gf) + ((tgt,) if last else ())))


def _ffn_bwd(dy, xm, g2, a, b, gf, tm):
    t = dy.shape[0]

    def body(dy_ref, x_ref, g_ref, a_ref, b_ref, wg_ref, wu_ref, wd_ref, dx_ref, da_ref, db_ref, hm_ref, dg_ref):
        @pl.when(pl.program_id(0) == 0)
        def _():
            dg_ref[...] = jnp.zeros_like(dg_ref)

        dyv = dy_ref[...]
        dyb = dyv.astype(BF16)
        dh2 = jnp.zeros_like(dyv)
        for k in range(N_CHIPS):
            dhm = _nt(dyb, wd_ref[k])
            av = a_ref[k].astype(F32)
            bv = b_ref[k].astype(F32)
            sig = jax.nn.sigmoid(av)
            sil = av * sig
            hm_ref[k] = (sil * bv).astype(BF16)
            da = (dhm * bv * (sig * (1.0 + av * (1.0 - sig)))).astype(BF16)
            db = (dhm * sil).astype(BF16)
            da_ref[k] = da
            db_ref[k] = db
            dh2 = (dh2 + jnp.dot(da, wg_ref[k], preferred_element_type=F32)
                   + jnp.dot(db, wu_ref[k], preferred_element_type=F32))
        r, xh = _rms_fwd(x_ref[...], 1.0 / D)
        dg_ref[...] += jnp.sum(dh2 * xh, axis=0, keepdims=True)
        dx_ref[...] = dyv + _rms_bwd(dh2, g_ref[...], xh, r, 1.0 / D)

    row = lambda w: pl.BlockSpec((tm, w), lambda i: (i, 0))
    blk = pl.BlockSpec((N_CHIPS, tm, FFB), lambda i: (0, i, 0))
    bsd = jax.ShapeDtypeStruct((N_CHIPS, t, FFB), BF16)
    return pl.pallas_call(
        body, name="ffn_bwd", grid=(t // tm,),
        in_specs=[row(D), row(D), _const_spec((1, D)), blk, blk] + _ffn_weight_specs(),
        out_specs=[row(D), blk, blk, blk, pl.BlockSpec((1, D), lambda i: (0, 0))],
        out_shape=[jax.ShapeDtypeStruct((t, D), F32), bsd, bsd, bsd, jax.ShapeDtypeStruct((1, D), F32)],
        compiler_params=_cparams(("arbitrary",)),
    )(dy, xm, g2, a, b, gf, gf, gf)


def _wgrad_blocks(a, b, tt, name):
    _, t, rows = a.shape
    cols = b.shape[1]
    nsteps = t // tt

    def body(a_ref, b_ref, o_ref, acc_ref):
        s = pl.program_id(0)

        @pl.when(s == 0)
        def _():
            acc_ref[...] = jnp.zeros_like(acc_ref)

        bv = b_ref[...].astype(BF16)
        for k in range(N_CHIPS):
            acc_ref[k] += _tn(a_ref[k], bv)

        @pl.when(s == nsteps - 1)
        def _():
            o_ref[...] = acc_ref[...].astype(BF16)

    return pl.pallas_call(
        body, name=name, grid=(nsteps,),
        in_specs=[pl.BlockSpec((N_CHIPS, tt, rows), lambda s: (0, s, 0)), pl.BlockSpec((tt, cols), lambda s: (s, 0))],
        out_specs=pl.BlockSpec((N_CHIPS, rows, cols), lambda s: (0, 0, 0)),
        out_shape=jax.ShapeDtypeStruct((N_CHIPS, rows, cols), BF16),
        scratch_shapes=[pltpu.VMEM((N_CHIPS, rows, cols), F32)],
        compiler_params=_cparams(("arbitrary",)),
    )(a, b)


def _head_rows(first, n_heads):
    return [(first + g * HD, first + g * HP, HD) for g in range(n_heads)]


def _wgrad(a, b, tt, name, pieces=None):
    t, k = a.shape
    n = b.shape[1]
    nsteps = t // tt
    pieces = pieces or [(0, 0, k)]
    rows = sum(p[2] for p in pieces)

    def body(a_ref, b_ref, o_ref, acc_ref):
        s = pl.program_id(0)

        @pl.when(s == 0)
        def _():
            acc_ref[...] = jnp.zeros_like(acc_ref)

        acc_ref[...] += _tn(a_ref[...].astype(BF16), b_ref[...].astype(BF16))

        @pl.when(s == nsteps - 1)
        def _():
            for dst, src, size in pieces:
                o_ref[dst:dst + size, :] = acc_ref[src:src + size, :].astype(BF16)

    return pl.pallas_call(
        body, name=name, grid=(nsteps,),
        in_specs=[pl.BlockSpec((tt, k), lambda s: (s, 0)), pl.BlockSpec((tt, n), lambda s: (s, 0))],
        out_specs=pl.BlockSpec((rows, n), lambda s: (0, 0)),
        out_shape=jax.ShapeDtypeStruct((rows, n), BF16),
        scratch_shapes=[pltpu.VMEM((k, n), F32)],
        compiler_params=_cparams(("arbitrary",)),
    )(a, b)


def _mixer_bwd(dxm, proj, ao, cw, gq, gk, sinks, gco, gao, wo, tq):
    t = proj.shape[0]
    nb = tq // BLK
    r8 = tq // 8
    nt = t // tq
    te = tq + 8
    kvw = 2 * NKV * HP

    def body(dx_ref, dxn_ref, p_ref, cgp_ref, hcp_ref, bgn_ref, cgn_ref, hcn_ref, kvp_ref, ao_ref, cw_ref, gq_ref,
             gk_ref, sk_ref, gco_ref, gao_ref, wo_ref,
             dpm_ref, dkvm_ref, dkvh_ref, dcw_ref, dgq_ref, dgk_ref, dsk_ref, dgco_ref, dgao_ref, acc_ref):
        i = pl.program_id(0)

        @pl.when(i == 0)
        def _():
            for r in (dcw_ref, dgq_ref, dgk_ref, dsk_ref, dgco_ref, dgao_ref):
                r[...] = jnp.zeros_like(r)

        acc_ref[...] = jnp.zeros_like(acc_ref)
        live_rows = jnp.where(i < nt - 1, te, tq)
        dxb = dx_ref[...].astype(BF16)
        dxe = jnp.concatenate([dxb, dxn_ref[...].astype(BF16)], axis=0)
        dcn = _nt(dxe, wo_ref[0:CC, :])
        bg = jnp.concatenate([p_ref[:, O_BG:O_BG + CC], bgn_ref[...]], axis=0)
        cg = jnp.concatenate([p_ref[:, O_CG:O_CG + CC], cgn_ref[...]], axis=0)
        hc = jnp.concatenate([p_ref[:, O_HC:O_HC + CC], hcn_ref[...]], axis=0)
        u = cg * hc
        up = jnp.where(i > 0, cgp_ref[...] * hcp_ref[...], 0.0)
        u1, u2 = _conv_taps(jnp.concatenate([up, u], axis=0), te)
        w0, w1, w2 = cw_ref[0:1, :], cw_ref[1:2, :], cw_ref[2:3, :]
        y = w0 * u2 + w1 * u1 + w2 * u
        co = bg * y
        rc, coh = _rms_fwd(co, 1.0 / CC)
        dco = _rms_bwd(dcn, gco_ref[...], coh, rc, 1.0 / CC)
        row_io = lax.broadcasted_iota(jnp.int32, (te, 1), 0)
        own = row_io < tq
        dgco_ref[...] += jnp.sum(jnp.where(own, dcn * coh, 0.0), axis=0, keepdims=True)
        dyc = jnp.where(row_io < live_rows, dco * bg, 0.0)
        dyo = jnp.where(own, dyc, 0.0)
        dcw_ref[0:1, :] += jnp.sum(dyo * u2, axis=0, keepdims=True)
        dcw_ref[1:2, :] += jnp.sum(dyo * u1, axis=0, keepdims=True)
        dcw_ref[2:3, :] += jnp.sum(dyo * u, axis=0, keepdims=True)
        dy1 = pltpu.roll(dyc, te - 1, 0)[0:tq]
        dy2 = pltpu.roll(dyc, te - 2, 0)[0:tq]
        du = w2 * dyc[0:tq] + w1 * dy1 + w0 * dy2
        dpm_ref[:, O_BG:O_BG + CC] = (dco[0:tq] * y[0:tq]).astype(BF16)
        dpm_ref[:, O_CG:O_CG + CC] = (du * hc[0:tq]).astype(BF16)
        dpm_ref[:, O_HC:O_HC + CC] = (du * cg[0:tq]).astype(BF16)
        kraw = jnp.concatenate([kvp_ref[:, 0:NKV * HP], p_ref[:, O_K:O_K + NKV * HP]], axis=0)
        vraw = jnp.concatenate([kvp_ref[:, NKV * HP:], p_ref[:, O_V:O_V + NKV * HP]], axis=0)
        gqv, gkv = gq_ref[...], gk_ref[...]
        keys = _norm_keys(kraw, gkv)
        vb = [vraw[:, h * HP:(h + 1) * HP].astype(BF16) for h in range(NKV)]
        base_valid, c_io = _band_mask()
        lane = lax.broadcasted_iota(jnp.int32, (1, HP), 1)
        dgq, dgk, dsk = (jnp.zeros((1, HP), F32) for _ in range(3))
        dgao = jnp.zeros((1, NQ * HD), F32)
        for b in range(nb):
            lo = jnp.where(i * nb + b == 0, BLK, 0)
            valid = base_valid & (c_io >= lo)
            band = slice(b * BLK, b * BLK + 2 * BLK)
            blk = slice(b * BLK, (b + 1) * BLK)
            ra, aoh = _rms_fwd(ao_ref[blk, :], 1.0 / (NQ * HD))
            danb = _nt(dxb[blk], wo_ref[CC:MIXW, :])
            dgao = dgao + jnp.sum(danb * aoh, axis=0, keepdims=True)
            dao = _rms_bwd(danb, gao_ref[...], aoh, ra, 1.0 / (NQ * HD))
            dos = [dao[:, g // 2 * HP:(g // 2 + 1) * HP] for g in range(NQ)]
            dos = [(d if g % 2 == 0 else pltpu.roll(d, HD, 1)).astype(BF16) for g, d in enumerate(dos)]
            fwd = []
            for g in range(NQ):
                rq, qh = _rms_fwd(p_ref[blk, O_Q + g * HP:O_Q + (g + 1) * HP], 1.0 / HD)
                qs = (qh * (gqv * SCALE)).astype(BF16)
                fwd.append((rq, qh, qs) + _attn_probs(qs, keys[g // GRP][2][band], sk_ref[0, g], valid))
            dqs = []
            for h in range(NKV):
                khat, rk, kn = [a[band] for a in keys[h]]
                dss, prbs, qns, dobs = [], [], [], []
                for g in range(h * GRP, (h + 1) * GRP):
                    rq, qh, qs, pr, ps = fwd[g]
                    dob = dos[g]
                    dp = _nt(dob, vb[h][band])
                    delta = jnp.sum(pr * dp, axis=-1, keepdims=True)
                    dsb = (pr * (dp - delta)).astype(BF16)
                    dsk = dsk + jnp.where(lane == g, -jnp.sum(ps * delta, axis=0, keepdims=True), 0.0)
                    dqn = jnp.dot(dsb, kn, preferred_element_type=F32) * SCALE
                    dgq = dgq + jnp.sum(dqn * qh, axis=0, keepdims=True)
                    dqs.append(_rms_bwd(dqn, gqv, qh, rq, 1.0 / HD).astype(BF16))
                    dss.append(dsb)
                    prbs.append(pr.astype(BF16))
                    qns.append(qs)
                    dobs.append(dob)
                dkn = _tn(jnp.concatenate(dss, axis=0), jnp.concatenate(qns, axis=0))
                dv = _tn(jnp.concatenate(prbs, axis=0), jnp.concatenate(dobs, axis=0))
                dgk = dgk + jnp.sum(dkn * khat, axis=0, keepdims=True)
                acc_ref[band, h * HP:(h + 1) * HP] += _rms_bwd(dkn, gkv, khat, rk, 1.0 / HD)
                acc_ref[band, (NKV + h) * HP:(NKV + h + 1) * HP] += dv
            dpm_ref[blk, O_Q:O_K] = jnp.concatenate(dqs, axis=1)
        dgq_ref[...] += dgq
        dgk_ref[...] += dgk
        dsk_ref[...] += dsk
        dgao_ref[...] += dgao
        dkvh_ref[...] = acc_ref[0:BLK, :]
        dkvm_ref[...] = acc_ref[BLK:, :]

    prev8 = lambda col: pl.BlockSpec((8, CC), lambda i: (jnp.maximum(i * r8 - 1, 0), col))
    next8 = lambda col: pl.BlockSpec((8, CC), lambda i: (jnp.minimum((i + 1) * r8, t // 8 - 1), col))
    small = lambda n: pl.BlockSpec((1, n), lambda i: (0, 0))
    return pl.pallas_call(
        body, name="mixer_bwd", grid=(nt,),
        in_specs=[
            pl.BlockSpec((tq, D), lambda i: (i, 0)),
            pl.BlockSpec((8, D), lambda i: (jnp.minimum((i + 1) * r8, t // 8 - 1), 0)),
            pl.BlockSpec((tq, NP), lambda i: (i, 0)),
            prev8(O_CG // CC), prev8(O_HC // CC),
            next8(O_BG // CC), next8(O_CG // CC), next8(O_HC // CC),
            pl.BlockSpec((BLK, kvw), lambda i: (jnp.maximum(i * nb - 1, 0), O_K // kvw)),
            pl.BlockSpec((tq, NQ * HD), lambda i: (i, 0)),
            _const_spec((8, CC)), _const_spec((1, HP)), _const_spec((1, HP)),
            pl.BlockSpec(memory_space=pltpu.SMEM),
            _const_spec((1, CC)), _const_spec((1, NQ * HD)), _const_spec((MIXW, D)),
        ],
        out_specs=[
            pl.BlockSpec((tq, NMAIN), lambda i: (i, 0)),
            pl.BlockSpec((tq, kvw), lambda i: (i, 0)),
            pl.BlockSpec((BLK, kvw), lambda i: (i, 0)),
            pl.BlockSpec((8, CC), lambda i: (0, 0)), small(HP), small(HP), small(HP), small(CC), small(NQ * HD),
        ],
        out_shape=[
            jax.ShapeDtypeStruct((t, NMAIN), BF16), jax.ShapeDtypeStruct((t, kvw), F32),
            jax.ShapeDtypeStruct((nt * BLK, kvw), F32),
            jax.ShapeDtypeStruct((8, CC), F32), jax.ShapeDtypeStruct((1, HP), F32), jax.ShapeDtypeStruct((1, HP), F32),
            jax.ShapeDtypeStruct((1, HP), F32), jax.ShapeDtypeStruct((1, CC), F32),
            jax.ShapeDtypeStruct((1, NQ * HD), F32),
        ],
        scratch_shapes=[pltpu.VMEM((tq + BLK, kvw), F32)],
        compiler_params=_cparams(("arbitrary",)),
    )(dxm, dxm, proj, proj, proj, proj, proj, proj, proj, ao, cw, gq, gk, sinks, gco, gao, wo)


def _inproj_bwd(dpm, dkvm, dkvh, wpt, x, g1, dxm, tm):
    t = x.shape[0]
    kvw = 2 * NKV * HP
    nt = t // tm

    def body(dp_ref, dk_ref, dh_ref, w_ref, x_ref, g_ref, dxm_ref, dx_ref, dg_ref, dkv_ref):
        i = pl.program_id(0)

        @pl.when(i == 0)
        def _():
            dg_ref[...] = jnp.zeros_like(dg_ref)

        halo = jnp.where(i < nt - 1, dh_ref[...], 0.0)
        dkv_ref[0:tm - BLK, :] = dk_ref[0:tm - BLK, :].astype(BF16)
        dkv_ref[tm - BLK:tm, :] = (dk_ref[tm - BLK:tm, :] + halo).astype(BF16)
        dh = (jnp.dot(dp_ref[...], w_ref[0:NMAIN, :], preferred_element_type=F32)
              + jnp.dot(dkv_ref[...], w_ref[NMAIN:NP, :], preferred_element_type=F32))
        r, xh = _rms_fwd(x_ref[...], 1.0 / D)
        dg_ref[...] += jnp.sum(dh * xh, axis=0, keepdims=True)
        dx_ref[...] = dxm_ref[...] + _rms_bwd(dh, g_ref[...], xh, r, 1.0 / D)

    row = lambda w: pl.BlockSpec((tm, w), lambda i: (i, 0))
    return pl.pallas_call(
        body, name="inproj_bwd", grid=(nt,),
        in_specs=[row(NMAIN), row(kvw), pl.BlockSpec((BLK, kvw), lambda i: (jnp.minimum(i + 1, nt - 1), 0)),
                  _const_spec((NP, D)), row(D), _const_spec((1, D)), row(D)],
        out_specs=[row(D), pl.BlockSpec((1, D), lambda i: (0, 0)), row(kvw)],
        out_shape=[jax.ShapeDtypeStruct((t, D), F32), jax.ShapeDtypeStruct((1, D), F32),
                   jax.ShapeDtypeStruct((t, kvw), BF16)],
        compiler_params=_cparams(("arbitrary",)),
    )(dpm, dkvm, dkvh, wpt, x, g1, dxm)


def _rows_tile(rows, cap=512):
    for cand in range(min(rows, cap) // 16 * 16, 0, -16):
        if rows % cand == 0:
            return cand
    return rows


def _presum_halves(gs, theirs, core):
    n = len(gs)

    def body(c_ref, *refs):
        for g_ref, t_ref, o_ref in zip(refs[:n], refs[n:2 * n], refs[2 * n:]):
            o_ref[...] = (g_ref[...].astype(F32) + t_ref[...].astype(F32)).astype(BF16)

    half = lambda ta: pl.BlockSpec((None,) + ta.shape[1:], lambda k, c_ref: (k, 0, 0))
    own = lambda ta: pl.BlockSpec((None,) + ta.shape[1:], lambda k, c_ref: (k, c_ref[0], 0))
    return pl.pallas_call(
        body, name="presum",
        grid_spec=pltpu.PrefetchScalarGridSpec(
            num_scalar_prefetch=1, grid=(N_CHIPS,),
            in_specs=[own(ta) for ta in theirs] + [half(ta) for ta in theirs],
            out_specs=[half(ta) for ta in theirs]),
        out_shape=[jax.ShapeDtypeStruct(ta.shape, BF16) for ta in theirs],
        compiler_params=_cparams(("parallel",)),
    )(core, *gs, *theirs)


def _sum_chips(got, ps, chip):
    n = len(got)
    steps = 2

    def body(chip_ref, *refs):
        for c_ref, own_ref, o_ref in zip(refs[:n], refs[n:2 * n], refs[2 * n:]):
            acc = None
            for j in range(N_CHIPS):
                term = jnp.where(chip_ref[0] == j, own_ref[...], c_ref[j]).astype(F32)
                acc = term if acc is None else acc + term
            o_ref[...] = acc

    tile = lambda c: (c.shape[1] // steps, c.shape[2])
    return pl.pallas_call(
        body, name="chipsum",
        grid_spec=pltpu.PrefetchScalarGridSpec(
            num_scalar_prefetch=1, grid=(steps,),
            in_specs=[pl.BlockSpec((N_CHIPS,) + tile(c), lambda i, chip_ref: (0, i, 0)) for c in got]
            + [pl.BlockSpec((None,) + tile(c), lambda i, chip_ref: (chip_ref[0], i, 0)) for c in got],
            out_specs=[pl.BlockSpec(tile(c), lambda i, chip_ref: (i, 0)) for c in got]),
        out_shape=[jax.ShapeDtypeStruct(c.shape[1:], F32) for c in got],
        compiler_params=_cparams(("parallel",)),
    )(chip, *got, *ps)


def _adamw(w, g, m, v, name):
    rows, cols = w.shape
    tr = _rows_tile(rows)
    c1 = 1.0 - ADAM_B1 ** ADAM_STEP
    c2 = 1.0 - ADAM_B2 ** ADAM_STEP

    def body(w_ref, g_ref, m_ref, v_ref, d_ref, mo_ref, vo_ref):
        gv = g_ref[...]
        mn = ADAM_B1 * m_ref[...] + (1.0 - ADAM_B1) * gv
        vn = ADAM_B2 * v_ref[...] + (1.0 - ADAM_B2) * (gv * gv)
        mo_ref[...] = mn
        vo_ref[...] = vn
        d_ref[...] = -ADAM_LR * ((mn / c1) / (jnp.sqrt(vn / c2) + ADAM_EPS) + ADAM_WD * w_ref[...])

    spec = pl.BlockSpec((tr, cols), lambda i: (i, 0))
    sds = jax.ShapeDtypeStruct((rows, cols), F32)
    return pl.pallas_call(
        body, name=name, grid=(rows // tr,), in_specs=[spec] * 4, out_specs=[spec] * 3, out_shape=[sds] * 3,
        compiler_params=_cparams(("parallel",)),
    )(w, g, m, v)


def _place():
    x, y, c = lax.axis_index("x"), lax.axis_index("y"), lax.axis_index("c")
    chips = [(1 - x, y), (x, 1 - y), (1 - x, 1 - y)]
    return x, y, c, chips


ANY = pl.BlockSpec(memory_space=pl.ANY)
DMA_ROWS = 64


def _pieces(shape):
    rows = shape[-2]
    step = DMA_ROWS if rows % DMA_ROWS == 0 else rows
    lead = [()]
    for n in shape[:-2]:
        lead = [i + (k,) for i in lead for k in range(n)]
    return [i + (pl.ds(r0, step),) for i in lead for r0 in range(0, rows, step)]


def _start_pieces(make, src, dst):
    for idx in _pieces(src.shape):
        make(src.at[idx], dst.at[idx]).start()


def _gather_body(srcs, outs, sems, layer, start):
    nw = len(srcs)
    ssem, rsem, fssem, frsem = sems
    x, y, c, chips = _place()
    kme = 2 * x + y

    def plane(j, w, to):
        return lambda s, d: pltpu.make_async_remote_copy(
            src_ref=s, dst_ref=d, send_sem=ssem.at[j, w], recv_sem=rsem.at[j, w], device_id=to,
            device_id_type=MESH)

    def passed(j, w):
        return lambda s, d: pltpu.make_async_remote_copy(
            src_ref=s, dst_ref=d, send_sem=fssem.at[j, w], recv_sem=frsem.at[j, w],
            device_id=(x, y, 1 - c), device_id_type=MESH)

    @pl.when(c == layer)
    def _():
        for j, (px, py) in enumerate(chips):
            for w in range(nw):
                start(plane(j, w, (px, py, c)), srcs[w], outs[w].at[kme])
        for j, (px, py) in enumerate(chips):
            for w in range(nw):
                got = outs[w].at[2 * px + py]
                plane(j, w, (px, py, c))(got, got).wait_recv()
                start(passed(j, w), got, got)
        for j, (px, py) in enumerate(chips):
            for w in range(nw):
                got = outs[w].at[2 * px + py]
                plane(j, w, (px, py, c))(got, got).wait_send()
                passed(j, w)(got, got).wait_send()

    @pl.when(c != layer)
    def _():
        for j, (px, py) in enumerate(chips):
            for w in range(nw):
                got = outs[w].at[2 * px + py]
                passed(j, w)(got, got).wait_recv()


def _handshake(peers):
    barrier = pltpu.get_barrier_semaphore()
    for peer in peers:
        pl.semaphore_signal(barrier, inc=1, device_id=peer, device_id_type=MESH)
    pl.semaphore_wait(barrier, len(peers))


def _handshake_all():
    x, y, c, _ = _place()
    _handshake([(x ^ (r >> 2), y ^ ((r >> 1) & 1), c ^ (r & 1)) for r in range(1, 8)])


def _gather_layer_async(blocks, layer, name, collective_id):
    hbm = pltpu.MemorySpace.HBM
    srcs = [jax.new_ref(b, memory_space=hbm) for b in blocks]
    outs = [jax.empty_ref(jax.ShapeDtypeStruct((N_CHIPS,) + b.shape, b.dtype), memory_space=hbm) for b in blocks]

    @pl.kernel(mesh=plsc.ScalarSubcoreMesh(axis_name="seq", num_cores=1), name=name,
               scratch_types=[pltpu.SemaphoreType.DMA((3, len(blocks)))] * 4,
               compiler_params=pltpu.CompilerParams(collective_id=collective_id))
    def launch(*sems):
        _handshake_all()
        _gather_body(srcs, outs, sems, layer, lambda make, s, d: make(s, d).start())

    launch()
    return [o[...] for o in outs]


def _swap_siblings(arrs, halves, name, collective_id=None):
    nw = len(arrs)
    out_sds = [jax.ShapeDtypeStruct((a.shape[0], a.shape[1] // 2, a.shape[2]) if halves else a.shape, a.dtype)
               for a in arrs]

    def exchange(srcs, outs, ssem, rsem, start):
        x, y, c, _ = _place()

        def give(w):
            return lambda s, d: pltpu.make_async_remote_copy(
                src_ref=s, dst_ref=d, send_sem=ssem.at[w], recv_sem=rsem.at[w], device_id=(x, y, 1 - c),
                device_id_type=MESH)

        for w in range(nw):
            hr = outs[w].shape[1]
            start(give(w), srcs[w].at[:, pl.ds((1 - c) * hr, hr)] if halves else srcs[w], outs[w])
        for w in range(nw):
            give(w)(outs[w], outs[w]).wait()

    if collective_id is None:
        def body(*refs):
            exchange(refs[:nw], refs[nw:2 * nw], *refs[2 * nw:], _start_pieces)

        return pl.pallas_call(
            body, name=name, in_specs=[ANY] * nw, out_specs=[ANY] * nw, out_shape=out_sds,
            scratch_shapes=[pltpu.SemaphoreType.DMA((nw,))] * 2,
            compiler_params=_cparams(has_side_effects=True),
        )(*arrs)

    hbm = pltpu.MemorySpace.HBM
    srcs = [jax.new_ref(a, memory_space=hbm) for a in arrs]
    outs = [jax.empty_ref(sds, memory_space=hbm) for sds in out_sds]

    @pl.kernel(mesh=plsc.ScalarSubcoreMesh(axis_name="seq", num_cores=1), name=name,
               scratch_types=[pltpu.SemaphoreType.DMA((nw,))] * 2,
               compiler_params=pltpu.CompilerParams(collective_id=collective_id))
    def launch(ssem, rsem):
        x, y, c, _ = _place()
        _handshake([(x, y, 1 - c)])
        exchange(srcs, outs, ssem, rsem, lambda make, s, d: make(s, d).start())

    launch()
    return [o[...] for o in outs]


def _scatter_body(srcs, outs, sems, start):
    nw = len(srcs)
    ssem, rsem = sems
    x, y, c, chips = _place()
    kme = 2 * x + y

    def give(j, w, to):
        return lambda s, d: pltpu.make_async_remote_copy(
            src_ref=s, dst_ref=d, send_sem=ssem.at[j, w], recv_sem=rsem.at[j, w], device_id=to,
            device_id_type=MESH)

    for j, (px, py) in enumerate(chips):
        for w in range(nw):
            start(give(j, w, (px, py, c)), srcs[w].at[2 * px + py], outs[w].at[kme])
    for j, (px, py) in enumerate(chips):
        for w in range(nw):
            got = outs[w].at[2 * px + py]
            give(j, w, (px, py, c))(got, got).wait_recv()
    for j, (px, py) in enumerate(chips):
        for w in range(nw):
            sent = srcs[w].at[2 * px + py]
            give(j, w, (px, py, c))(sent, sent).wait_send()


def _scatter_chips_async(ps, name, collective_id):
    hbm = pltpu.MemorySpace.HBM
    srcs = [jax.new_ref(p, memory_space=hbm) for p in ps]
    outs = [jax.empty_ref(jax.ShapeDtypeStruct(p.shape, p.dtype), memory_space=hbm) for p in ps]

    @pl.kernel(mesh=plsc.ScalarSubcoreMesh(axis_name="seq", num_cores=1), name=name,
               scratch_types=[pltpu.SemaphoreType.DMA((3, len(ps)))] * 2,
               compiler_params=pltpu.CompilerParams(collective_id=collective_id))
    def launch(*sems):
        _handshake_all()
        _scatter_body(srcs, outs, sems, lambda make, s, d: make(s, d).start())

    launch()
    return [o[...] for o in outs]


def _allreduce_small(v):
    rows = v.shape[0]

    def body(v_ref, o_ref, buf, ssem, rsem):
        x, y, c, _ = _place()
        me = 4 * x + 2 * y + c
        buf[me] = v_ref[...]
        sends = []
        for r in range(1, 8):
            peer = (x ^ (r >> 2), y ^ ((r >> 1) & 1), c ^ (r & 1))
            cp = pltpu.make_async_remote_copy(
                src_ref=v_ref, dst_ref=buf.at[me], send_sem=ssem.at[r - 1], recv_sem=rsem.at[r - 1],
                device_id=peer, device_id_type=MESH)
            cp.start()
            sends.append(cp)
        for r in range(1, 8):
            src = me ^ r
            pltpu.make_async_remote_copy(
                src_ref=v_ref, dst_ref=buf.at[src], send_sem=ssem.at[r - 1], recv_sem=rsem.at[r - 1],
                device_id=(x, y, c), device_id_type=MESH).wait_recv()
        for cp in sends:
            cp.wait_send()
        acc = buf[0]
        for d in range(1, 8):
            acc = acc + buf[d]
        o_ref[...] = acc

    vm = pl.BlockSpec(memory_space=pltpu.VMEM)
    return pl.pallas_call(
        body, name="allreduce_small", in_specs=[vm], out_specs=vm,
        out_shape=jax.ShapeDtypeStruct(v.shape, F32),
        scratch_shapes=[pltpu.VMEM((8, rows, 128), F32), pltpu.SemaphoreType.DMA((7,)),
                        pltpu.SemaphoreType.DMA((7,))],
        compiler_params=_cparams(has_side_effects=True),
    )(v)


def _t(w):
    return jnp.swapaxes(w, -1, -2)


def _count(shape):
    n = 1
    for s in shape:
        n *= s
    return n


def _pack_rows(arrs):
    flat = [jnp.pad(a.reshape(-1), (0, (-_count(a.shape)) % 128)) for a in arrs]
    v = jnp.concatenate(flat)
    rows = -(-v.shape[0] // (8 * 128)) * 8
    return jnp.pad(v, (0, rows * 128 - v.shape[0])).reshape(rows, 128)


def kernel(x, norm1_g, w_in, conv_w, q_norm_g, k_norm_g, sinks, conv_out_g, attn_out_g, w_o, norm2_g, w_gate, w_up, w_down, loss_target, m_norm1_g, m_w_in, m_conv_w, m_q_norm_g, m_k_norm_g, m_sinks, m_conv_out_g, m_attn_out_g, m_w_o, m_norm2_g, m_w_gate, m_w_up, m_w_down, v_norm1_g, v_w_in, v_conv_w, v_q_norm_g, v_k_norm_g, v_sinks, v_conv_out_g, v_attn_out_g, v_w_o, v_norm2_g, v_w_gate, v_w_up, v_w_down):
    depth = w_in.shape[0]
    t = x.shape[1]
    xs = x.reshape(t, D)
    tgt = loss_target.reshape(t, D)
    xi, yi = lax.axis_index("x"), lax.axis_index("y")
    kme = 2 * xi + yi
    tm = min(512, t)
    tq = min(512, t)
    tf = min(256, t)
    tw = min(1024, t)

    cwp = jnp.pad(conv_w.reshape(depth * 3, CC // N_CHIPS), ((0, 8 - depth * 3), (0, 0)))
    own_f = [jnp.concatenate([_t(w_gate[l]), _t(w_up[l]), w_down[l]], axis=0).astype(BF16) for l in range(depth)]
    own_o = [w_o[l].astype(BF16) for l in range(depth)]
    own_i = [_t(w_in[l]).astype(BF16) for l in range(depth)]
    mine = lambda got, own: lax.dynamic_update_index_in_dim(got, own, kme, 0)
    got0 = _gather_layer_async([own_i[0], own_o[0], cwp], 0, "gather_in0_seq", collective_id=14)
    (got_i0, got_o0, got_cw), own_f, own_o, own_i = lax.optimization_barrier((got0, own_f, own_o, own_i))
    gf0_in = lax.optimization_barrier((own_f[0], got_i0))[0]
    (got_f0,) = _gather_layer_async([gf0_in], 0, "gather_ffn0_seq", collective_id=6)
    cw_full = mine(got_cw, cwp).transpose(1, 0, 2).reshape(8, CC)[:depth * 3].reshape(depth, 3, CC)

    chip = kme.reshape(1).astype(jnp.int32)

    def layer_params(l, got_o):
        return dict(
            wo=mine(got_o, own_o[l]).reshape(MIXW, D),
            cw=jnp.pad(cw_full[l], ((0, 5), (0, 0))),
            g1=norm1_g[l].reshape(1, D), g2=norm2_g[l].reshape(1, D),
            gq=jnp.pad(q_norm_g[l], (0, HP - HD)).reshape(1, HP), gk=jnp.pad(k_norm_g[l], (0, HP - HD)).reshape(1, HP),
            sk=sinks[l].reshape(1, NQ), gco=conv_out_g[l].reshape(1, CC),
            gao=attn_out_g[l].reshape(1, NQ * HD))

    saved, layers = [], []
    cur = xs
    for l in range(depth):
        x_in = cur
        if l == 0:
            got_i, p = got_i0, layer_params(0, got_o0)
        else:
            got_f1, got_o1, got_i = lax.optimization_barrier((got_l1, cur))[0]
            p = layer_params(1, got_o1)
        proj, h, p["wpt"] = _inproj_fwd(cur, p["g1"], got_i, own_i[l], chip, tm)
        xm, mix, ao = _mixer_fwd(proj, cur, p["cw"], p["gq"], p["gk"], p["sk"], p["gco"], p["gao"], p["wo"], tq)
        if l == 0:
            got_f0 = lax.optimization_barrier((got_f0, xm))[0]
            l1_in = lax.optimization_barrier(([own_f[1], own_o[1], own_i[1]], got_f0))[0]
            got_l1 = _gather_layer_async(l1_in, 1, "gather_layer1_seq", collective_id=1)
        p["gf"] = mine(got_f0 if l == 0 else got_f1, own_f[l])
        layers.append(p)
        if l < depth - 1:
            cur, a, b, h2 = _ffn_fwd(xm, p["g2"], p["gf"], tm)
        else:
            lpart, dy, a, b, h2 = _ffn_fwd(xm, p["g2"], p["gf"], tm, tgt)
        saved.append(dict(x=x_in, proj=proj, h=h, xm=xm, mix=mix, ao=ao, a=a, b=b, h2=h2))

    ci = lax.axis_index("c")
    core = ci.reshape(1).astype(jnp.int32)
    rbig = [dict() for _ in range(depth)]
    gsmall = [None] * depth

    def after_(vals, after):
        return vals if after is None else lax.optimization_barrier((vals, after))[0]

    def reduce_1(gs, tag, ids):
        return gs, _swap_siblings(gs, True, f"swap_halves_{tag}_seq", ids[0]), tag, ids

    def reduce_2(state, after):
        gs, theirs, tag, ids = state
        ps = _presum_halves(gs, after_(theirs, after), core)
        return ps, _scatter_chips_async(ps, f"scatter_{tag}_seq", ids[1]), tag, ids

    def reduce_3(state, after):
        ps, got, tag, ids = state
        r_mine = _sum_chips(after_(got, after), ps, chip)
        return r_mine, _swap_siblings(r_mine, False, f"swap_reduced_{tag}" + ("_seq" if ids[2] else ""), ids[2])

    def reduce_4(state, after):
        r_mine, r_theirs = state
        return [jnp.where(ci == 0, jnp.concatenate([a, b], axis=0), jnp.concatenate([b, a], axis=0))
                for a, b in zip(r_mine, after_(r_theirs, after))]

    ids = {"ffn1": (7, 4, 8), "in1": (9, 5, 10), "ffn0": (11, 2, 12), "in0": (13, 3, None)}
    in_2 = None
    handed = {}
    for l in reversed(range(depth)):
        p, s = layers[l], saved[l]
        dxm, da, db, hm, dg2 = _ffn_bwd(dy, s["xm"], p["g2"], s["a"], s["b"], p["gf"], tf)
        if in_2 is not None:
            in_2 = reduce_2(in_2, dxm)
        g_wg = _wgrad_blocks(da, s["h2"], tw, "wgrad_gate")
        g_wu = _wgrad_blocks(db, s["h2"], tw, "wgrad_up")
        g_wd = _wgrad_blocks(hm, dy, tw, "wgrad_down")
        if in_2 is not None:
            handed[f"in{l + 1}"] = reduce_3(in_2, g_wd)
        ffn_1 = reduce_1([g_wg, g_wu, g_wd], f"ffn{l}", ids[f"ffn{l}"])
        dpm, dkvm, dkvh, dcw, dgq, dgk, dsk, dgco, dgao = _mixer_bwd(
            dxm, s["proj"], s["ao"], p["cw"], p["gq"], p["gk"], p["sk"], p["gco"], p["gao"], p["wo"], tq)
        ffn_2 = reduce_2(ffn_1, dpm)
        g_o = _wgrad(s["mix"], dxm, tw, "wgrad_o")
        dx, dg1, dkv = _inproj_bwd(dpm, dkvm, dkvh, p["wpt"], s["x"], p["g1"], dxm, tq)
        g_in = jnp.concatenate(
            [_wgrad(dpm, s["h"], tw, "wgrad_in_main", [(0, 0, O_Q)] + _head_rows(O_Q, NQ)),
             _wgrad(dkv, s["h"], tw, "wgrad_in_kv", _head_rows(0, 2 * NKV))], axis=0)
        dy = dx
        gsmall[l] = dict(g1=dg1, cw=dcw[:3], gq=dgq[0, :HD], gk=dgk[0, :HD], sk=dsk[0, :NQ], gco=dgco,
                         gao=dgao, g2=dg2)
        handed[f"ffn{l}"] = reduce_3(ffn_2, g_in)
        in_2 = reduce_1([g_in.reshape(N_CHIPS, -1, D), g_o.reshape(N_CHIPS, -1, D)], f"in{l}", ids[f"in{l}"])
    grad_x = dy.reshape(x.shape)

    small_shapes = dict(g1=(D,), cw=(3, CC), gq=(HD,), gk=(HD,), sk=(NQ,), gco=(CC,), gao=(NQ * HD,), g2=(D,))
    red = _allreduce_small(_pack_rows([gsmall[l][n] for l in range(depth) for n in small_shapes]
                                      + [lpart[0:1, 0:1]])).reshape(-1)
    red_small, offs = {n: [] for n in small_shapes}, 0
    for l in range(depth):
        for n, shp in small_shapes.items():
            cnt = _count(shp)
            red_small[n].append(red[offs:offs + cnt].reshape(shp))
            offs += -(-cnt // 128) * 128
    loss = red[offs]
    g_small = {n: jnp.stack(v) for n, v in red_small.items()}
    g_cw = lax.dynamic_slice_in_dim(g_small["cw"], kme * (CC // N_CHIPS), CC // N_CHIPS, axis=2)

    weights = [norm1_g, w_in, conv_w, q_norm_g, k_norm_g, sinks, conv_out_g, attn_out_g, w_o, norm2_g, w_gate,
               w_up, w_down]
    moms = [m_norm1_g, m_w_in, m_conv_w, m_q_norm_g, m_k_norm_g, m_sinks, m_conv_out_g, m_attn_out_g, m_w_o,
            m_norm2_g, m_w_gate, m_w_up, m_w_down]
    vars_ = [v_norm1_g, v_w_in, v_conv_w, v_q_norm_g, v_k_norm_g, v_sinks, v_conv_out_g, v_attn_out_g, v_w_o,
             v_norm2_g, v_w_gate, v_w_up, v_w_down]
    n_w = len(weights)
    big_idx = dict(zip(("in", "o", "g", "u", "d"), (1, 8, 10, 11, 12)))
    small_idx = [n for n in range(n_w) if n not in big_idx.values()]
    grads, deltas, new_m, new_v = [None] * n_w, [None] * n_w, [None] * n_w, [None] * n_w
    for n, g in zip(small_idx, (g_small["g1"], g_cw, g_small["gq"], g_small["gk"], g_small["sk"], g_small["gco"],
                                g_small["gao"], g_small["g2"])):
        grads[n] = g

    def update_big(name):
        n = big_idx[name]
        g = jnp.stack([rbig[l][name] for l in range(depth)])
        flip = g.shape != weights[n].shape
        rows2d = lambda a3: (_t(a3) if flip else a3).reshape(-1, D)
        res = _adamw(rows2d(weights[n]), g.reshape(-1, D), rows2d(moms[n]), rows2d(vars_[n]), f"adamw_{n}")
        res = [g] + [r.reshape(g.shape) for r in res]
        grads[n], deltas[n], new_m[n], new_v[n] = [_t(r) for r in res] if flip else res

    for l in range(depth):
        rbig[l]["g"], rbig[l]["u"], rbig[l]["d"] = reduce_4(handed[f"ffn{l}"], red)
    rbig[1]["in"], rbig[1]["o"] = reduce_4(handed["in1"], red)
    update_big("g")
    in_2 = reduce_2(in_2, new_v[big_idx["g"]])
    update_big("u")
    update_big("d")
    rbig[0]["in"], rbig[0]["o"] = reduce_4(reduce_3(in_2, new_v[big_idx["d"]]), None)
    for name in ("in", "o"):
        update_big(name)
    res = _adamw(*[_pack_rows([arrs[n] for n in small_idx]) for arrs in (weights, grads, moms, vars_)],
                 "adamw_small")
    offs = 0
    for n in small_idx:
        shp = weights[n].shape
        cnt = _count(shp)
        deltas[n], new_m[n], new_v[n] = [r.reshape(-1)[offs:offs + cnt].reshape(shp) for r in res]
        offs += -(-cnt // 128) * 128
    return (loss, grad_x, *grads, *deltas, *new_m, *new_v)
```

```python
import jax
import jax.numpy as jnp
from jax import lax
from jax.experimental import pallas as pl
from jax.experimental.pallas import tpu as pltpu
from jax.experimental.pallas import tpu_sc as plsc

F32 = jnp.float32
BF16 = jnp.bfloat16

D = 1024
CC = 512
NQ = 8
NKV = 2
HD = 64
HP = 128
GRP = NQ // NKV
FF = 2816
FFB = FF // 4
BLK = 128
EPS = 1e-6
NEG = -1e30
SCALE = HD ** -0.5
O_BG, O_CG, O_HC, O_Q = 0, CC, 2 * CC, 3 * CC
O_K = O_Q + NQ * HP
O_V = O_K + NKV * HP
NP = O_V + NKV * HP
NMAIN = O_K
MIXW = CC + NQ * HD
N_CHIPS = 4
VMEM_LIMIT = 56 * 1024 * 1024
MESH = pl.DeviceIdType.MESH

ADAM_LR, ADAM_B1, ADAM_B2, ADAM_EPS, ADAM_WD, ADAM_STEP = 0.001, 0.9, 0.999, 1e-08, 0.01, 10


def _cparams(sem=None, **kw):
    if sem is not None:
        kw["dimension_semantics"] = sem
    return pltpu.CompilerParams(vmem_limit_bytes=VMEM_LIMIT, **kw)


def _const_spec(shape):
    nd = len(shape)
    return pl.BlockSpec(shape, lambda *_: (0,) * nd, pipeline_mode=pl.Buffered(1))


def _nt(a, b):
    return lax.dot_general(a, b, (((1,), (1,)), ((), ())), preferred_element_type=F32)


def _tn(a, b):
    return lax.dot_general(a, b, (((0,), (0,)), ((), ())), preferred_element_type=F32)


def _rms_fwd(x, inv_n):
    r = lax.rsqrt(jnp.sum(x * x, axis=-1, keepdims=True) * inv_n + EPS)
    return r, x * r


def _rms_bwd(dy, g, xh, r, inv_n):
    dxh = dy * g
    return r * (dxh - xh * (jnp.sum(dxh * xh, axis=-1, keepdims=True) * inv_n))


W_IN_ROWS = 3 * CC + (NQ + 2 * NKV) * HD
W_IN_BLOCK = W_IN_ROWS // N_CHIPS


def _padded_row(row):
    return row + max(row - O_Q, 0) // HD * (HP - HD)


def _w_in_pieces(k):
    first = k * W_IN_BLOCK
    plain = min(max(O_Q - first, 0), W_IN_BLOCK)
    pieces = [(0, first, plain)] if plain else []
    return pieces + [(r, _padded_row(first + r), HD) for r in range(plain, W_IN_BLOCK, HD)]


def _inproj_fwd(x, g1, gi, own_i, chip, tm):
    t = x.shape[0]

    def body(chip_ref, x_ref, g_ref, gi_ref, own_ref, p_ref, h_ref, w_ref, sem):
        @pl.when(pl.program_id(0) == 0)
        def _():
            for k in range(N_CHIPS):
                for src, dst, rows in _w_in_pieces(k):
                    @pl.when(chip_ref[0] == k)
                    def _():
                        pltpu.make_async_copy(own_ref.at[pl.ds(src, rows)], w_ref.at[pl.ds(dst, rows)], sem).start()

                    @pl.when(chip_ref[0] != k)
                    def _():
                        pltpu.make_async_copy(gi_ref.at[k, pl.ds(src, rows)], w_ref.at[pl.ds(dst, rows)], sem).start()
            for slot in range(NQ + 2 * NKV):
                w_ref[O_Q + slot * HP + HD:O_Q + (slot + 1) * HP, :] = jnp.zeros((HP - HD, D), BF16)
            landed = w_ref.at[pl.ds(0, W_IN_ROWS)]
            pltpu.make_async_copy(landed, landed, sem).wait()

        _, xh = _rms_fwd(x_ref[...], 1.0 / D)
        h = (xh * g_ref[...]).astype(BF16)
        h_ref[...] = h
        p_ref[...] = _nt(h, w_ref[...])

    const = lambda shape: pl.BlockSpec(shape, lambda i, c: (0,) * len(shape))
    return pl.pallas_call(
        body, name="inproj_fwd",
        grid_spec=pltpu.PrefetchScalarGridSpec(
            num_scalar_prefetch=1, grid=(t // tm,),
            in_specs=[pl.BlockSpec((tm, D), lambda i, c: (i, 0)), const((1, D)), ANY, ANY],
            out_specs=[pl.BlockSpec((tm, NP), lambda i, c: (i, 0)), pl.BlockSpec((tm, D), lambda i, c: (i, 0)),
                       const((NP, D))],
            scratch_shapes=[pltpu.SemaphoreType.DMA(())]),
        out_shape=[jax.ShapeDtypeStruct((t, NP), F32), jax.ShapeDtypeStruct((t, D), BF16),
                   jax.ShapeDtypeStruct((NP, D), BF16)],
        compiler_params=_cparams(("arbitrary",)),
    )(chip, x, g1, gi, own_i)


def _band_mask():
    r_io = lax.broadcasted_iota(jnp.int32, (BLK, 2 * BLK), 0)
    c_io = lax.broadcasted_iota(jnp.int32, (BLK, 2 * BLK), 1)
    return (c_io > r_io) & (c_io <= r_io + BLK), c_io


def _conv_taps(uf, n):
    u1 = pltpu.roll(uf, 1, 0)[8:8 + n]
    u2 = pltpu.roll(uf, 2, 0)[8:8 + n]
    return u1, u2


def _attn_probs(qs, kband, sink, valid):
    s = jnp.where(valid, _nt(qs, kband), NEG)
    m = jnp.maximum(jnp.max(s, axis=-1, keepdims=True), sink)
    p = jnp.exp(s - m)
    es = jnp.exp(sink - m)
    inv = 1.0 / (jnp.sum(p, axis=-1, keepdims=True) + es)
    return p * inv, es * inv


def _norm_keys(kraw, gk):
    out = []
    for h in range(NKV):
        kh = kraw[:, h * HP:(h + 1) * HP]
        rk, khat = _rms_fwd(kh, 1.0 / HD)
        out.append((khat, rk, (khat * gk).astype(BF16)))
    return out


def _mixer_fwd(proj, x, cw, gq, gk, sinks, gco, gao, wo, tq):
    t = proj.shape[0]
    nb = tq // BLK
    r8 = tq // 8

    def body(p_ref, cgp_ref, hcp_ref, kvp_ref, x_ref, cw_ref, gq_ref, gk_ref, sk_ref, gco_ref, gao_ref,
             wo_ref, xm_ref, mix_ref, ao_ref, aop_ref):
        i = pl.program_id(0)
        cg = p_ref[:, O_CG:O_CG + CC]
        hc = p_ref[:, O_HC:O_HC + CC]
        u = cg * hc
        up = jnp.where(i > 0, cgp_ref[...] * hcp_ref[...], 0.0)
        u1, u2 = _conv_taps(jnp.concatenate([up, u], axis=0), tq)
        y = cw_ref[0:1, :] * u2 + cw_ref[1:2, :] * u1 + cw_ref[2:3, :] * u
        co = p_ref[:, O_BG:O_BG + CC] * y
        _, coh = _rms_fwd(co, 1.0 / CC)
        cn = coh * gco_ref[...]
        kraw = jnp.concatenate([kvp_ref[:, 0:NKV * HP], p_ref[:, O_K:O_K + NKV * HP]], axis=0)
        vraw = jnp.concatenate([kvp_ref[:, NKV * HP:], p_ref[:, O_V:O_V + NKV * HP]], axis=0)
        keys = _norm_keys(kraw, gk_ref[...])
        vb = [vraw[:, h * HP:(h + 1) * HP].astype(BF16) for h in range(NKV)]
        base_valid, c_io = _band_mask()
        gqs = gq_ref[...] * SCALE
        for b in range(nb):
            lo = jnp.where(i * nb + b == 0, BLK, 0)
            valid = base_valid & (c_io >= lo)
            for g in range(NQ):
                h = g // GRP
                qg = p_ref[b * BLK:(b + 1) * BLK, O_Q + g * HP:O_Q + (g + 1) * HP]
                _, qh = _rms_fwd(qg, 1.0 / HD)
                qs = (qh * gqs).astype(BF16)
                pr, _ = _attn_probs(qs, keys[h][2][b * BLK:b * BLK + 2 * BLK], sk_ref[0, g], valid)
                aop_ref[b * BLK:(b + 1) * BLK, g * HP:(g + 1) * HP] = jnp.dot(
                    pr.astype(BF16), vb[h][b * BLK:b * BLK + 2 * BLK], preferred_element_type=F32)
        for j in range(NQ // 2):
            ao_ref[:, j * HP:(j + 1) * HP] = (aop_ref[:, 2 * j * HP:(2 * j + 1) * HP]
                                              + pltpu.roll(aop_ref[:, (2 * j + 1) * HP:(2 * j + 2) * HP], HD, 1))
        _, aoh = _rms_fwd(ao_ref[...], 1.0 / (NQ * HD))
        an = aoh * gao_ref[...]
        mix = jnp.concatenate([cn, an], axis=1).astype(BF16)
        mix_ref[...] = mix
        xm_ref[...] = x_ref[...] + jnp.dot(mix, wo_ref[...], preferred_element_type=F32)

    prev8 = lambda col: pl.BlockSpec((8, CC), lambda i: (jnp.maximum(i * r8 - 1, 0), col))
    return pl.pallas_call(
        body, name="mixer_fwd", grid=(t // tq,),
        in_specs=[
            pl.BlockSpec((tq, NP), lambda i: (i, 0)),
            prev8(O_CG // CC), prev8(O_HC // CC),
            pl.BlockSpec((BLK, 2 * NKV * HP), lambda i: (jnp.maximum(i * nb - 1, 0), O_K // (2 * NKV * HP))),
            pl.BlockSpec((tq, D), lambda i: (i, 0)),
            _const_spec((8, CC)), _const_spec((1, HP)), _const_spec((1, HP)),
            pl.BlockSpec(memory_space=pltpu.SMEM),
            _const_spec((1, CC)), _const_spec((1, NQ * HD)), _const_spec((MIXW, D)),
        ],
        out_specs=[pl.BlockSpec((tq, D), lambda i: (i, 0)), pl.BlockSpec((tq, MIXW), lambda i: (i, 0)),
                   pl.BlockSpec((tq, NQ * HD), lambda i: (i, 0))],
        out_shape=[jax.ShapeDtypeStruct((t, D), F32), jax.ShapeDtypeStruct((t, MIXW), BF16),
                   jax.ShapeDtypeStruct((t, NQ * HD), F32)],
        scratch_shapes=[pltpu.VMEM((tq, NQ * HP), F32)],
        compiler_params=_cparams(("parallel",)),
    )(proj, proj, proj, proj, x, cw, gq, gk, sinks, gco, gao, wo)


def _ffn_weight_specs():
    return [pl.BlockSpec((N_CHIPS, FFB, D), lambda i, j=j: (0, j, 0), pipeline_mode=pl.Buffered(1))
            for j in range(3)]


def _ffn_fwd(xm, g2, gf, tm, tgt=None):
    t = xm.shape[0]
    last = tgt is not None

    def body(x_ref, g_ref, wg_ref, wu_ref, wd_ref, *rest):
        t_ref, rest = (rest[0], rest[1:]) if last else (None, rest)
        l_ref, rest = (rest[0], rest[1:]) if last else (None, rest)
        xo_ref, a_ref, b_ref, h2_ref = rest
        xv = x_ref[...]
        _, xh = _rms_fwd(xv, 1.0 / D)
        h2 = (xh * g_ref[...]).astype(BF16)
        h2_ref[...] = h2
        acc = xv
        for k in range(N_CHIPS):
            a = _nt(h2, wg_ref[k])
            b = _nt(h2, wu_ref[k])
            a_ref[k] = a.astype(BF16)
            b_ref[k] = b.astype(BF16)
            hm = (a * jax.nn.sigmoid(a) * b).astype(BF16)
            acc = acc + jnp.dot(hm, wd_ref[k], preferred_element_type=F32)
        if last:
            @pl.when(pl.program_id(0) == 0)
            def _():
                l_ref[...] = jnp.zeros_like(l_ref)

            e = acc - t_ref[...]
            xo_ref[...] = e * (1.0 / D)
            l_ref[...] += jnp.sum(jnp.sum(e * e, axis=-1, keepdims=True), axis=0, keepdims=True) * (0.5 / D)
        else:
            xo_ref[...] = acc

    row = lambda w: pl.BlockSpec((tm, w), lambda i: (i, 0))
    blk = pl.BlockSpec((N_CHIPS, tm, FFB), lambda i: (0, i, 0))
    bsd = jax.ShapeDtypeStruct((N_CHIPS, t, FFB), BF16)
    return pl.pallas_call(
        body, name="ffn_fwd_loss" if last else "ffn_fwd", grid=(t // tm,),
        in_specs=[row(D), _const_spec((1, D))] + _ffn_weight_specs() + ([row(D)] if last else []),
        out_specs=([pl.BlockSpec((8, 128), lambda i: (0, 0))] if last else []) + [row(D), blk, blk, row(D)],
        out_shape=([jax.ShapeDtypeStruct((8, 128), F32)] if last else [])
        + [jax.ShapeDtypeStruct((t, D), F32), bsd, bsd, jax.ShapeDtypeStruct((t, D), BF16)],
        compiler_params=_cparams(("arbitrary" if last else "parallel",)),
    )(*((xm, g2, gf, gf, gf) + ((tgt,) if last else ())))


def _ffn_bwd(dy, xm, g2, a, b, gf, tm):
    t = dy.shape[0]

    def body(dy_ref, x_ref, g_ref, a_ref, b_ref, wg_ref, wu_ref, wd_ref, dx_ref, da_ref, db_ref, hm_ref, dg_ref):
        @pl.when(pl.program_id(0) == 0)
        def _():
            dg_ref[...] = jnp.zeros_like(dg_ref)

        dyv = dy_ref[...]
        dyb = dyv.astype(BF16)
        dh2 = jnp.zeros_like(dyv)
        for k in range(N_CHIPS):
            dhm = _nt(dyb, wd_ref[k])
            av = a_ref[k].astype(F32)
            bv = b_ref[k].astype(F32)
            sig = jax.nn.sigmoid(av)
            sil = av * sig
            hm_ref[k] = (sil * bv).astype(BF16)
            da = (dhm * bv * (sig * (1.0 + av * (1.0 - sig)))).astype(BF16)
            db = (dhm * sil).astype(BF16)
            da_ref[k] = da
            db_ref[k] = db
            dh2 = (dh2 + jnp.dot(da, wg_ref[k], preferred_element_type=F32)
                   + jnp.dot(db, wu_ref[k], preferred_element_type=F32))
        r, xh = _rms_fwd(x_ref[...], 1.0 / D)
        dg_ref[...] += jnp.sum(dh2 * xh, axis=0, keepdims=True)
        dx_ref[...] = dyv + _rms_bwd(dh2, g_ref[...], xh, r, 1.0 / D)

    row = lambda w: pl.BlockSpec((tm, w), lambda i: (i, 0))
    blk = pl.BlockSpec((N_CHIPS, tm, FFB), lambda i: (0, i, 0))
    bsd = jax.ShapeDtypeStruct((N_CHIPS, t, FFB), BF16)
    return pl.pallas_call(
        body, name="ffn_bwd", grid=(t // tm,),
        in_specs=[row(D), row(D), _const_spec((1, D)), blk, blk] + _ffn_weight_specs(),
        out_specs=[row(D), blk, blk, blk, pl.BlockSpec((1, D), lambda i: (0, 0))],
        out_shape=[jax.ShapeDtypeStruct((t, D), F32), bsd, bsd, bsd, jax.ShapeDtypeStruct((1, D), F32)],
        compiler_params=_cparams(("arbitrary",)),
    )(dy, xm, g2, a, b, gf, gf, gf)


def _wgrad_blocks(a, b, tt, name):
    _, t, rows = a.shape
    cols = b.shape[1]
    nsteps = t // tt

    def body(a_ref, b_ref, o_ref, acc_ref):
        s = pl.program_id(0)

        @pl.when(s == 0)
        def _():
            acc_ref[...] = jnp.zeros_like(acc_ref)

        bv = b_ref[...].astype(BF16)
        for k in range(N_CHIPS):
            acc_ref[k] += _tn(a_ref[k], bv)

        @pl.when(s == nsteps - 1)
        def _():
            o_ref[...] = acc_ref[...].astype(BF16)

    return pl.pallas_call(
        body, name=name, grid=(nsteps,),
        in_specs=[pl.BlockSpec((N_CHIPS, tt, rows), lambda s: (0, s, 0)), pl.BlockSpec((tt, cols), lambda s: (s, 0))],
        out_specs=pl.BlockSpec((N_CHIPS, rows, cols), lambda s: (0, 0, 0)),
        out_shape=jax.ShapeDtypeStruct((N_CHIPS, rows, cols), BF16),
        scratch_shapes=[pltpu.VMEM((N_CHIPS, rows, cols), F32)],
        compiler_params=_cparams(("arbitrary",)),
    )(a, b)


def _head_rows(first, n_heads):
    return [(first + g * HD, first + g * HP, HD) for g in range(n_heads)]


def _wgrad(a, b, tt, name, pieces=None):
    t, k = a.shape
    n = b.shape[1]
    nsteps = t // tt
    pieces = pieces or [(0, 0, k)]
    rows = sum(p[2] for p in pieces)

    def body(a_ref, b_ref, o_ref, acc_ref):
        s = pl.program_id(0)

        @pl.when(s == 0)
        def _():
            acc_ref[...] = jnp.zeros_like(acc_ref)

        acc_ref[...] += _tn(a_ref[...].astype(BF16), b_ref[...].astype(BF16))

        @pl.when(s == nsteps - 1)
        def _():
            for dst, src, size in pieces:
                o_ref[dst:dst + size, :] = acc_ref[src:src + size, :].astype(BF16)

    return pl.pallas_call(
        body, name=name, grid=(nsteps,),
        in_specs=[pl.BlockSpec((tt, k), lambda s: (s, 0)), pl.BlockSpec((tt, n), lambda s: (s, 0))],
        out_specs=pl.BlockSpec((rows, n), lambda s: (0, 0)),
        out_shape=jax.ShapeDtypeStruct((rows, n), BF16),
        scratch_shapes=[pltpu.VMEM((k, n), F32)],
        compiler_params=_cparams(("arbitrary",)),
    )(a, b)


def _mixer_bwd(dxm, proj, ao, cw, gq, gk, sinks, gco, gao, wo, tq):
    t = proj.shape[0]
    nb = tq // BLK
    r8 = tq // 8
    nt = t // tq
    te = tq + 8
    kvw = 2 * NKV * HP

    def body(dx_ref, dxn_ref, p_ref, cgp_ref, hcp_ref, bgn_ref, cgn_ref, hcn_ref, kvp_ref, ao_ref, cw_ref, gq_ref,
             gk_ref, sk_ref, gco_ref, gao_ref, wo_ref,
             dpm_ref, dkvm_ref, dkvh_ref, dcw_ref, dgq_ref, dgk_ref, dsk_ref, dgco_ref, dgao_ref, acc_ref):
        i = pl.program_id(0)

        @pl.when(i == 0)
        def _():
            for r in (dcw_ref, dgq_ref, dgk_ref, dsk_ref, dgco_ref, dgao_ref):
                r[...] = jnp.zeros_like(r)

        acc_ref[...] = jnp.zeros_like(acc_ref)
        live_rows = jnp.where(i < nt - 1, te, tq)
        dxb = dx_ref[...].astype(BF16)
        dxe = jnp.concatenate([dxb, dxn_ref[...].astype(BF16)], axis=0)
        dcn = _nt(dxe, wo_ref[0:CC, :])
        bg = jnp.concatenate([p_ref[:, O_BG:O_BG + CC], bgn_ref[...]], axis=0)
        cg = jnp.concatenate([p_ref[:, O_CG:O_CG + CC], cgn_ref[...]], axis=0)
        hc = jnp.concatenate([p_ref[:, O_HC:O_HC + CC], hcn_ref[...]], axis=0)
        u = cg * hc
        up = jnp.where(i > 0, cgp_ref[...] * hcp_ref[...], 0.0)
        u1, u2 = _conv_taps(jnp.concatenate([up, u], axis=0), te)
        w0, w1, w2 = cw_ref[0:1, :], cw_ref[1:2, :], cw_ref[2:3, :]
        y = w0 * u2 + w1 * u1 + w2 * u
        co = bg * y
        rc, coh = _rms_fwd(co, 1.0 / CC)
        dco = _rms_bwd(dcn, gco_ref[...], coh, rc, 1.0 / CC)
        row_io = lax.broadcasted_iota(jnp.int32, (te, 1), 0)
        own = row_io < tq
        dgco_ref[...] += jnp.sum(jnp.where(own, dcn * coh, 0.0), axis=0, keepdims=True)
        dyc = jnp.where(row_io < live_rows, dco * bg, 0.0)
        dyo = jnp.where(own, dyc, 0.0)
        dcw_ref[0:1, :] += jnp.sum(dyo * u2, axis=0, keepdims=True)
        dcw_ref[1:2, :] += jnp.sum(dyo * u1, axis=0, keepdims=True)
        dcw_ref[2:3, :] += jnp.sum(dyo * u, axis=0, keepdims=True)
        dy1 = pltpu.roll(dyc, te - 1, 0)[0:tq]
        dy2 = pltpu.roll(dyc, te - 2, 0)[0:tq]
        du = w2 * dyc[0:tq] + w1 * dy1 + w0 * dy2
        dpm_ref[:, O_BG:O_BG + CC] = (dco[0:tq] * y[0:tq]).astype(BF16)
        dpm_ref[:, O_CG:O_CG + CC] = (du * hc[0:tq]).astype(BF16)
        dpm_ref[:, O_HC:O_HC + CC] = (du * cg[0:tq]).astype(BF16)
        kraw = jnp.concatenate([kvp_ref[:, 0:NKV * HP], p_ref[:, O_K:O_K + NKV * HP]], axis=0)
        vraw = jnp.concatenate([kvp_ref[:, NKV * HP:], p_ref[:, O_V:O_V + NKV * HP]], axis=0)
        gqv, gkv = gq_ref[...], gk_ref[...]
        keys = _norm_keys(kraw, gkv)
        vb = [vraw[:, h * HP:(h + 1) * HP].astype(BF16) for h in range(NKV)]
        base_valid, c_io = _band_mask()
        lane = lax.broadcasted_iota(jnp.int32, (1, HP), 1)
        dgq, dgk, dsk = (jnp.zeros((1, HP), F32) for _ in range(3))
        dgao = jnp.zeros((1, NQ * HD), F32)
        for b in range(nb):
            lo = jnp.where(i * nb + b == 0, BLK, 0)
            valid = base_valid & (c_io >= lo)
            band = slice(b * BLK, b * BLK + 2 * BLK)
            blk = slice(b * BLK, (b + 1) * BLK)
            ra, aoh = _rms_fwd(ao_ref[blk, :], 1.0 / (NQ * HD))
            danb = _nt(dxb[blk], wo_ref[CC:MIXW, :])
            dgao = dgao + jnp.sum(danb * aoh, axis=0, keepdims=True)
            dao = _rms_bwd(danb, gao_ref[...], aoh, ra, 1.0 / (NQ * HD))
            dos = [dao[:, g // 2 * HP:(g // 2 + 1) * HP] for g in range(NQ)]
            dos = [(d if g % 2 == 0 else pltpu.roll(d, HD, 1)).astype(BF16) for g, d in enumerate(dos)]
            fwd = []
            for g in range(NQ):
                rq, qh = _rms_fwd(p_ref[blk, O_Q + g * HP:O_Q + (g + 1) * HP], 1.0 / HD)
                qs = (qh * (gqv * SCALE)).astype(BF16)
                fwd.append((rq, qh, qs) + _attn_probs(qs, keys[g // GRP][2][band], sk_ref[0, g], valid))
            dqs = []
            for h in range(NKV):
                khat, rk, kn = [a[band] for a in keys[h]]
                dss, prbs, qns, dobs = [], [], [], []
                for g in range(h * GRP, (h + 1) * GRP):
                    rq, qh, qs, pr, ps = fwd[g]
                    dob = dos[g]
                    dp = _nt(dob, vb[h][band])
                    delta = jnp.sum(pr * dp, axis=-1, keepdims=True)
                    dsb = (pr * (dp - delta)).astype(BF16)
                    dsk = dsk + jnp.where(lane == g, -jnp.sum(ps * delta, axis=0, keepdims=True), 0.0)
                    dqn = jnp.dot(dsb, kn, preferred_element_type=F32) * SCALE
                    dgq = dgq + jnp.sum(dqn * qh, axis=0, keepdims=True)
                    dqs.append(_rms_bwd(dqn, gqv, qh, rq, 1.0 / HD).astype(BF16))
                    dss.append(dsb)
                    prbs.append(pr.astype(BF16))
                    qns.append(qs)
                    dobs.append(dob)
                dkn = _tn(jnp.concatenate(dss, axis=0), jnp.concatenate(qns, axis=0))
                dv = _tn(jnp.concatenate(prbs, axis=0), jnp.concatenate(dobs, axis=0))
                dgk = dgk + jnp.sum(dkn * khat, axis=0, keepdims=True)
                acc_ref[band, h * HP:(h + 1) * HP] += _rms_bwd(dkn, gkv, khat, rk, 1.0 / HD)
                acc_ref[band, (NKV + h) * HP:(NKV + h + 1) * HP] += dv
            dpm_ref[blk, O_Q:O_K] = jnp.concatenate(dqs, axis=1)
        dgq_ref[...] += dgq
        dgk_ref[...] += dgk
        dsk_ref[...] += dsk
        dgao_ref[...] += dgao
        dkvh_ref[...] = acc_ref[0:BLK, :]
        dkvm_ref[...] = acc_ref[BLK:, :]

    prev8 = lambda col: pl.BlockSpec((8, CC), lambda i: (jnp.maximum(i * r8 - 1, 0), col))
    next8 = lambda col: pl.BlockSpec((8, CC), lambda i: (jnp.minimum((i + 1) * r8, t // 8 - 1), col))
    small = lambda n: pl.BlockSpec((1, n), lambda i: (0, 0))
    return pl.pallas_call(
        body, name="mixer_bwd", grid=(nt,),
        in_specs=[
            pl.BlockSpec((tq, D), lambda i: (i, 0)),
            pl.BlockSpec((8, D), lambda i: (jnp.minimum((i + 1) * r8, t // 8 - 1), 0)),
            pl.BlockSpec((tq, NP), lambda i: (i, 0)),
            prev8(O_CG // CC), prev8(O_HC // CC),
            next8(O_BG // CC), next8(O_CG // CC), next8(O_HC // CC),
            pl.BlockSpec((BLK, kvw), lambda i: (jnp.maximum(i * nb - 1, 0), O_K // kvw)),
            pl.BlockSpec((tq, NQ * HD), lambda i: (i, 0)),
            _const_spec((8, CC)), _const_spec((1, HP)), _const_spec((1, HP)),
            pl.BlockSpec(memory_space=pltpu.SMEM),
            _const_spec((1, CC)), _const_spec((1, NQ * HD)), _const_spec((MIXW, D)),
        ],
        out_specs=[
            pl.BlockSpec((tq, NMAIN), lambda i: (i, 0)),
            pl.BlockSpec((tq, kvw), lambda i: (i, 0)),
            pl.BlockSpec((BLK, kvw), lambda i: (i, 0)),
            pl.BlockSpec((8, CC), lambda i: (0, 0)), small(HP), small(HP), small(HP), small(CC), small(NQ * HD),
        ],
        out_shape=[
            jax.ShapeDtypeStruct((t, NMAIN), BF16), jax.ShapeDtypeStruct((t, kvw), F32),
            jax.ShapeDtypeStruct((nt * BLK, kvw), F32),
            jax.ShapeDtypeStruct((8, CC), F32), jax.ShapeDtypeStruct((1, HP), F32), jax.ShapeDtypeStruct((1, HP), F32),
            jax.ShapeDtypeStruct((1, HP), F32), jax.ShapeDtypeStruct((1, CC), F32),
            jax.ShapeDtypeStruct((1, NQ * HD), F32),
        ],
        scratch_shapes=[pltpu.VMEM((tq + BLK, kvw), F32)],
        compiler_params=_cparams(("arbitrary",)),
    )(dxm, dxm, proj, proj, proj, proj, proj, proj, proj, ao, cw, gq, gk, sinks, gco, gao, wo)


def _inproj_bwd(dpm, dkvm, dkvh, wpt, x, g1, dxm, tm):
    t = x.shape[0]
    kvw = 2 * NKV * HP
    nt = t // tm

    def body(dp_ref, dk_ref, dh_ref, w_ref, x_ref, g_ref, dxm_ref, dx_ref, dg_ref, dkv_ref):
        i = pl.program_id(0)

        @pl.when(i == 0)
        def _():
            dg_ref[...] = jnp.zeros_like(dg_ref)

        halo = jnp.where(i < nt - 1, dh_ref[...], 0.0)
        dkv_ref[0:tm - BLK, :] = dk_ref[0:tm - BLK, :].astype(BF16)
        dkv_ref[tm - BLK:tm, :] = (dk_ref[tm - BLK:tm, :] + halo).astype(BF16)
        dh = (jnp.dot(dp_ref[...], w_ref[0:NMAIN, :], preferred_element_type=F32)
              + jnp.dot(dkv_ref[...], w_ref[NMAIN:NP, :], preferred_element_type=F32))
        r, xh = _rms_fwd(x_ref[...], 1.0 / D)
        dg_ref[...] += jnp.sum(dh * xh, axis=0, keepdims=True)
        dx_ref[...] = dxm_ref[...] + _rms_bwd(dh, g_ref[...], xh, r, 1.0 / D)

    row = lambda w: pl.BlockSpec((tm, w), lambda i: (i, 0))
    return pl.pallas_call(
        body, name="inproj_bwd", grid=(nt,),
        in_specs=[row(NMAIN), row(kvw), pl.BlockSpec((BLK, kvw), lambda i: (jnp.minimum(i + 1, nt - 1), 0)),
                  _const_spec((NP, D)), row(D), _const_spec((1, D)), row(D)],
        out_specs=[row(D), pl.BlockSpec((1, D), lambda i: (0, 0)), row(kvw)],
        out_shape=[jax.ShapeDtypeStruct((t, D), F32), jax.ShapeDtypeStruct((1, D), F32),
                   jax.ShapeDtypeStruct((t, kvw), BF16)],
        compiler_params=_cparams(("arbitrary",)),
    )(dpm, dkvm, dkvh, wpt, x, g1, dxm)


def _rows_tile(rows, cap=512):
    for cand in range(min(rows, cap) // 16 * 16, 0, -16):
        if rows % cand == 0:
            return cand
    return rows


def _presum_halves(gs, theirs, core):
    n = len(gs)

    def body(c_ref, *refs):
        for g_ref, t_ref, o_ref in zip(refs[:n], refs[n:2 * n], refs[2 * n:]):
            o_ref[...] = (g_ref[...].astype(F32) + t_ref[...].astype(F32)).astype(BF16)

    half = lambda ta: pl.BlockSpec((None,) + ta.shape[1:], lambda k, c_ref: (k, 0, 0))
    own = lambda ta: pl.BlockSpec((None,) + ta.shape[1:], lambda k, c_ref: (k, c_ref[0], 0))
    return pl.pallas_call(
        body, name="presum",
        grid_spec=pltpu.PrefetchScalarGridSpec(
            num_scalar_prefetch=1, grid=(N_CHIPS,),
            in_specs=[own(ta) for ta in theirs] + [half(ta) for ta in theirs],
            out_specs=[half(ta) for ta in theirs]),
        out_shape=[jax.ShapeDtypeStruct(ta.shape, BF16) for ta in theirs],
        compiler_params=_cparams(("parallel",)),
    )(core, *gs, *theirs)


def _sum_chips(got, ps, chip):
    n = len(got)
    steps = 2

    def body(chip_ref, *refs):
        for c_ref, own_ref, o_ref in zip(refs[:n], refs[n:2 * n], refs[2 * n:]):
            acc = None
            for j in range(N_CHIPS):
                term = jnp.where(chip_ref[0] == j, own_ref[...], c_ref[j]).astype(F32)
                acc = term if acc is None else acc + term
            o_ref[...] = acc

    tile = lambda c: (c.shape[1] // steps, c.shape[2])
    return pl.pallas_call(
        body, name="chipsum",
        grid_spec=pltpu.PrefetchScalarGridSpec(
            num_scalar_prefetch=1, grid=(steps,),
            in_specs=[pl.BlockSpec((N_CHIPS,) + tile(c), lambda i, chip_ref: (0, i, 0)) for c in got]
            + [pl.BlockSpec((None,) + tile(c), lambda i, chip_ref: (chip_ref[0], i, 0)) for c in got],
            out_specs=[pl.BlockSpec(tile(c), lambda i, chip_ref: (i, 0)) for c in got]),
        out_shape=[jax.ShapeDtypeStruct(c.shape[1:], F32) for c in got],
        compiler_params=_cparams(("parallel",)),
    )(chip, *got, *ps)


def _adamw(w, g, m, v, name):
    rows, cols = w.shape
    tr = _rows_tile(rows)
    c1 = 1.0 - ADAM_B1 ** ADAM_STEP
    c2 = 1.0 - ADAM_B2 ** ADAM_STEP

    def body(w_ref, g_ref, m_ref, v_ref, d_ref, mo_ref, vo_ref):
        gv = g_ref[...]
        mn = ADAM_B1 * m_ref[...] + (1.0 - ADAM_B1) * gv
        vn = ADAM_B2 * v_ref[...] + (1.0 - ADAM_B2) * (gv * gv)
        mo_ref[...] = mn
        vo_ref[...] = vn
        d_ref[...] = -ADAM_LR * ((mn / c1) / (jnp.sqrt(vn / c2) + ADAM_EPS) + ADAM_WD * w_ref[...])

    spec = pl.BlockSpec((tr, cols), lambda i: (i, 0))
    sds = jax.ShapeDtypeStruct((rows, cols), F32)
    return pl.pallas_call(
        body, name=name, grid=(rows // tr,), in_specs=[spec] * 4, out_specs=[spec] * 3, out_shape=[sds] * 3,
        compiler_params=_cparams(("parallel",)),
    )(w, g, m, v)


def _place():
    x, y, c = lax.axis_index("x"), lax.axis_index("y"), lax.axis_index("c")
    chips = [(1 - x, y), (x, 1 - y), (1 - x, 1 - y)]
    return x, y, c, chips


ANY = pl.BlockSpec(memory_space=pl.ANY)
DMA_ROWS = 64


def _pieces(shape):
    rows = shape[-2]
    step = DMA_ROWS if rows % DMA_ROWS == 0 else rows
    lead = [()]
    for n in shape[:-2]:
        lead = [i + (k,) for i in lead for k in range(n)]
    return [i + (pl.ds(r0, step),) for i in lead for r0 in range(0, rows, step)]


def _start_pieces(make, src, dst):
    for idx in _pieces(src.shape):
        make(src.at[idx], dst.at[idx]).start()


def _gather_body(srcs, outs, sems, layer, start):
    nw = len(srcs)
    ssem, rsem, fssem, frsem = sems
    x, y, c, chips = _place()
    kme = 2 * x + y

    def plane(j, w, to):
        return lambda s, d: pltpu.make_async_remote_copy(
            src_ref=s, dst_ref=d, send_sem=ssem.at[j, w], recv_sem=rsem.at[j, w], device_id=to,
            device_id_type=MESH)

    def passed(j, w):
        return lambda s, d: pltpu.make_async_remote_copy(
            src_ref=s, dst_ref=d, send_sem=fssem.at[j, w], recv_sem=frsem.at[j, w],
            device_id=(x, y, 1 - c), device_id_type=MESH)

    @pl.when(c == layer)
    def _():
        for j, (px, py) in enumerate(chips):
            for w in range(nw):
                start(plane(j, w, (px, py, c)), srcs[w], outs[w].at[kme])
        for j, (px, py) in enumerate(chips):
            for w in range(nw):
                got = outs[w].at[2 * px + py]
                plane(j, w, (px, py, c))(got, got).wait_recv()
                start(passed(j, w), got, got)
        for j, (px, py) in enumerate(chips):
            for w in range(nw):
                got = outs[w].at[2 * px + py]
                plane(j, w, (px, py, c))(got, got).wait_send()
                passed(j, w)(got, got).wait_send()

    @pl.when(c != layer)
    def _():
        for j, (px, py) in enumerate(chips):
            for w in range(nw):
                got = outs[w].at[2 * px + py]
                passed(j, w)(got, got).wait_recv()


def _handshake(peers):
    barrier = pltpu.get_barrier_semaphore()
    for peer in peers:
        pl.semaphore_signal(barrier, inc=1, device_id=peer, device_id_type=MESH)
    pl.semaphore_wait(barrier, len(peers))


def _handshake_all():
    x, y, c, _ = _place()
    _handshake([(x ^ (r >> 2), y ^ ((r >> 1) & 1), c ^ (r & 1)) for r in range(1, 8)])


def _gather_layer_async(blocks, layer, name, collective_id):
    hbm = pltpu.MemorySpace.HBM
    srcs = [jax.new_ref(b, memory_space=hbm) for b in blocks]
    outs = [jax.empty_ref(jax.ShapeDtypeStruct((N_CHIPS,) + b.shape, b.dtype), memory_space=hbm) for b in blocks]

    @pl.kernel(mesh=plsc.ScalarSubcoreMesh(axis_name="seq", num_cores=1), name=name,
               scratch_types=[pltpu.SemaphoreType.DMA((3, len(blocks)))] * 4,
               compiler_params=pltpu.CompilerParams(collective_id=collective_id))
    def launch(*sems):
        _handshake_all()
        _gather_body(srcs, outs, sems, layer, lambda make, s, d: make(s, d).start())

    launch()
    return [o[...] for o in outs]


def _swap_siblings(arrs, halves, name, collective_id=None):
    nw = len(arrs)
    out_sds = [jax.ShapeDtypeStruct((a.shape[0], a.shape[1] // 2, a.shape[2]) if halves else a.shape, a.dtype)
               for a in arrs]

    def exchange(srcs, outs, ssem, rsem, start):
        x, y, c, _ = _place()

        def give(w):
            return lambda s, d: pltpu.make_async_remote_copy(
                src_ref=s, dst_ref=d, send_sem=ssem.at[w], recv_sem=rsem.at[w], device_id=(x, y, 1 - c),
                device_id_type=MESH)

        for w in range(nw):
            hr = outs[w].shape[1]
            start(give(w), srcs[w].at[:, pl.ds((1 - c) * hr, hr)] if halves else srcs[w], outs[w])
        for w in range(nw):
            give(w)(outs[w], outs[w]).wait()

    if collective_id is None:
        def body(*refs):
            exchange(refs[:nw], refs[nw:2 * nw], *refs[2 * nw:], _start_pieces)

        return pl.pallas_call(
            body, name=name, in_specs=[ANY] * nw, out_specs=[ANY] * nw, out_shape=out_sds,
            scratch_shapes=[pltpu.SemaphoreType.DMA((nw,))] * 2,
            compiler_params=_cparams(has_side_effects=True),
        )(*arrs)

    hbm = pltpu.MemorySpace.HBM
    srcs = [jax.new_ref(a, memory_space=hbm) for a in arrs]
    outs = [jax.empty_ref(sds, memory_space=hbm) for sds in out_sds]

    @pl.kernel(mesh=plsc.ScalarSubcoreMesh(axis_name="seq", num_cores=1), name=name,
               scratch_types=[pltpu.SemaphoreType.DMA((nw,))] * 2,
               compiler_params=pltpu.CompilerParams(collective_id=collective_id))
    def launch(ssem, rsem):
        x, y, c, _ = _place()
        _handshake([(x, y, 1 - c)])
        exchange(srcs, outs, ssem, rsem, lambda make, s, d: make(s, d).start())

    launch()
    return [o[...] for o in outs]


def _scatter_body(srcs, outs, sems, start):
    nw = len(srcs)
    ssem, rsem = sems
    x, y, c, chips = _place()
    kme = 2 * x + y

    def give(j, w, to):
        return lambda s, d: pltpu.make_async_remote_copy(
            src_ref=s, dst_ref=d, send_sem=ssem.at[j, w], recv_sem=rsem.at[j, w], device_id=to,
            device_id_type=MESH)

    for j, (px, py) in enumerate(chips):
        for w in range(nw):
            start(give(j, w, (px, py, c)), srcs[w].at[2 * px + py], outs[w].at[kme])
    for j, (px, py) in enumerate(chips):
        for w in range(nw):
            got = outs[w].at[2 * px + py]
            give(j, w, (px, py, c))(got, got).wait_recv()
    for j, (px, py) in enumerate(chips):
        for w in range(nw):
            sent = srcs[w].at[2 * px + py]
            give(j, w, (px, py, c))(sent, sent).wait_send()


def _scatter_chips_async(ps, name, collective_id):
    hbm = pltpu.MemorySpace.HBM
    srcs = [jax.new_ref(p, memory_space=hbm) for p in ps]
    outs = [jax.empty_ref(jax.ShapeDtypeStruct(p.shape, p.dtype), memory_space=hbm) for p in ps]

    @pl.kernel(mesh=plsc.ScalarSubcoreMesh(axis_name="seq", num_cores=1), name=name,
               scratch_types=[pltpu.SemaphoreType.DMA((3, len(ps)))] * 2,
               compiler_params=pltpu.CompilerParams(collective_id=collective_id))
    def launch(*sems):
        _handshake_all()
        _scatter_body(srcs, outs, sems, lambda make, s, d: make(s, d).start())

    launch()
    return [o[...] for o in outs]


def _allreduce_small(v):
    rows = v.shape[0]

    def body(v_ref, o_ref, buf, ssem, rsem):
        x, y, c, _ = _place()
        me = 4 * x + 2 * y + c
        buf[me] = v_ref[...]
        sends = []
        for r in range(1, 8):
            peer = (x ^ (r >> 2), y ^ ((r >> 1) & 1), c ^ (r & 1))
            cp = pltpu.make_async_remote_copy(
                src_ref=v_ref, dst_ref=buf.at[me], send_sem=ssem.at[r - 1], recv_sem=rsem.at[r - 1],
                device_id=peer, device_id_type=MESH)
            cp.start()
            sends.append(cp)
        for r in range(1, 8):
            src = me ^ r
            pltpu.make_async_remote_copy(
                src_ref=v_ref, dst_ref=buf.at[src], send_sem=ssem.at[r - 1], recv_sem=rsem.at[r - 1],
                device_id=(x, y, c), device_id_type=MESH).wait_recv()
        for cp in sends:
            cp.wait_send()
        acc = buf[0]
        for d in range(1, 8):
            acc = acc + buf[d]
        o_ref[...] = acc

    vm = pl.BlockSpec(memory_space=pltpu.VMEM)
    return pl.pallas_call(
        body, name="allreduce_small", in_specs=[vm], out_specs=vm,
        out_shape=jax.ShapeDtypeStruct(v.shape, F32),
        scratch_shapes=[pltpu.VMEM((8, rows, 128), F32), pltpu.SemaphoreType.DMA((7,)),
                        pltpu.SemaphoreType.DMA((7,))],
        compiler_params=_cparams(has_side_effects=True),
    )(v)


def _t(w):
    return jnp.swapaxes(w, -1, -2)


def _count(shape):
    n = 1
    for s in shape:
        n *= s
    return n


def _pack_rows(arrs):
    flat = [jnp.pad(a.reshape(-1), (0, (-_count(a.shape)) % 128)) for a in arrs]
    v = jnp.concatenate(flat)
    rows = -(-v.shape[0] // (8 * 128)) * 8
    return jnp.pad(v, (0, rows * 128 - v.shape[0])).reshape(rows, 128)


def kernel(x, norm1_g, w_in, conv_w, q_norm_g, k_norm_g, sinks, conv_out_g, attn_out_g, w_o, norm2_g, w_gate, w_up, w_down, loss_target, m_norm1_g, m_w_in, m_conv_w, m_q_norm_g, m_k_norm_g, m_sinks, m_conv_out_g, m_attn_out_g, m_w_o, m_norm2_g, m_w_gate, m_w_up, m_w_down, v_norm1_g, v_w_in, v_conv_w, v_q_norm_g, v_k_norm_g, v_sinks, v_conv_out_g, v_attn_out_g, v_w_o, v_norm2_g, v_w_gate, v_w_up, v_w_down):
    depth = w_in.shape[0]
    t = x.shape[1]
    xs = x.reshape(t, D)
    tgt = loss_target.reshape(t, D)
    xi, yi = lax.axis_index("x"), lax.axis_index("y")
    kme = 2 * xi + yi
    tm = min(512, t)
    tq = min(512, t)
    tf = min(256, t)
    tw = min(1024, t)

    cwp = jnp.pad(conv_w.reshape(depth * 3, CC // N_CHIPS), ((0, 8 - depth * 3), (0, 0)))
    own_f = [jnp.concatenate([_t(w_gate[l]), _t(w_up[l]), w_down[l]], axis=0).astype(BF16) for l in range(depth)]
    own_o = [w_o[l].astype(BF16) for l in range(depth)]
    own_i = [_t(w_in[l]).astype(BF16) for l in range(depth)]
    mine = lambda got, own: lax.dynamic_update_index_in_dim(got, own, kme, 0)
    (got_i0,) = _gather_layer_async([own_i[0]], 0, "gather_in0_seq", collective_id=14)
    got_ocw = _gather_layer_async([own_o[0], cwp], 0, "gather_o0_seq", collective_id=15)
    got_i0, own_f, own_o, own_i = lax.optimization_barrier((got_i0, own_f, own_o, own_i))
    gf0_in = lax.optimization_barrier((own_f[0], got_i0))[0]
    (got_f0,) = _gather_layer_async([gf0_in], 0, "gather_ffn0_seq", collective_id=6)

    chip = kme.reshape(1).astype(jnp.int32)

    def layer_params(l, got_o, cw_full):
        return dict(
            wo=mine(got_o, own_o[l]).reshape(MIXW, D),
            cw=jnp.pad(cw_full[l], ((0, 5), (0, 0))),
            g1=norm1_g[l].reshape(1, D), g2=norm2_g[l].reshape(1, D),
            gq=jnp.pad(q_norm_g[l], (0, HP - HD)).reshape(1, HP), gk=jnp.pad(k_norm_g[l], (0, HP - HD)).reshape(1, HP),
            sk=sinks[l].reshape(1, NQ), gco=conv_out_g[l].reshape(1, CC),
            gao=attn_out_g[l].reshape(1, NQ * HD))

    saved, layers = [], []
    cur = xs
    for l in range(depth):
        x_in = cur
        if l == 0:
            got_i = got_i0
        else:
            got_f1, got_o, got_i = lax.optimization_barrier((got_l1, cur))[0]
        proj, h, wpt = _inproj_fwd(cur, norm1_g[l].reshape(1, D), got_i, own_i[l], chip, tm)
        if l == 0:
            got_o, got_cw = lax.optimization_barrier((got_ocw, proj))[0]
            cw_full = mine(got_cw, cwp).transpose(1, 0, 2).reshape(8, CC)[:depth * 3].reshape(depth, 3, CC)
        p = layer_params(l, got_o, cw_full)
        p["wpt"] = wpt
        xm, mix, ao = _mixer_fwd(proj, cur, p["cw"], p["gq"], p["gk"], p["sk"], p["gco"], p["gao"], p["wo"], tq)
        if l == 0:
            got_f0 = lax.optimization_barrier((got_f0, xm))[0]
            l1_in = lax.optimization_barrier(([own_f[1], own_o[1], own_i[1]], got_f0))[0]
            got_l1 = _gather_layer_async(l1_in, 1, "gather_layer1_seq", collective_id=1)
        p["gf"] = mine(got_f0 if l == 0 else got_f1, own_f[l])
        layers.append(p)
        if l < depth - 1:
            cur, a, b, h2 = _ffn_fwd(xm, p["g2"], p["gf"], tm)
        else:
            lpart, dy, a, b, h2 = _ffn_fwd(xm, p["g2"], p["gf"], tm, tgt)
        saved.append(dict(x=x_in, proj=proj, h=h, xm=xm, mix=mix, ao=ao, a=a, b=b, h2=h2))

    ci = lax.axis_index("c")
    core = ci.reshape(1).astype(jnp.int32)
    rbig = [dict() for _ in range(depth)]
    gsmall = [None] * depth

    def after_(vals, after):
        return vals if after is None else lax.optimization_barrier((vals, after))[0]

    def reduce_1(gs, tag, ids):
        return gs, _swap_siblings(gs, True, f"swap_halves_{tag}_seq", ids[0]), tag, ids

    def reduce_2(state, after):
        gs, theirs, tag, ids = state
        ps = _presum_halves(gs, after_(theirs, after), core)
        return ps, _scatter_chips_async(ps, f"scatter_{tag}_seq", ids[1]), tag, ids

    def reduce_3(state, after):
        ps, got, tag, ids = state
        r_mine = _sum_chips(after_(got, after), ps, chip)
        return r_mine, _swap_siblings(r_mine, False, f"swap_reduced_{tag}" + ("_seq" if ids[2] else ""), ids[2])

    def reduce_4(state, after):
        r_mine, r_theirs = state
        return [jnp.where(ci == 0, jnp.concatenate([a, b], axis=0), jnp.concatenate([b, a], axis=0))
                for a, b in zip(r_mine, after_(r_theirs, after))]

    ids = {"ffn1": (7, 4, 8), "in1": (9, 5, 10), "ffn0": (11, 2, 12), "in0": (13, 3, None)}
    in_2 = None
    handed = {}
    for l in reversed(range(depth)):
        p, s = layers[l], saved[l]
        dxm, da, db, hm, dg2 = _ffn_bwd(dy, s["xm"], p["g2"], s["a"], s["b"], p["gf"], tf)
        if in_2 is not None:
            in_2 = reduce_2(in_2, dxm)
        g_wg = _wgrad_blocks(da, s["h2"], tw, "wgrad_gate")
        g_wu = _wgrad_blocks(db, s["h2"], tw, "wgrad_up")
        g_wd = _wgrad_blocks(hm, dy, tw, "wgrad_down")
        if in_2 is not None:
            handed[f"in{l + 1}"] = reduce_3(in_2, g_wd)
        ffn_1 = reduce_1([g_wg, g_wu, g_wd], f"ffn{l}", ids[f"ffn{l}"])
        dpm, dkvm, dkvh, dcw, dgq, dgk, dsk, dgco, dgao = _mixer_bwd(
            dxm, s["proj"], s["ao"], p["cw"], p["gq"], p["gk"], p["sk"], p["gco"], p["gao"], p["wo"], tq)
        ffn_2 = reduce_2(ffn_1, dpm)
        g_o = _wgrad(s["mix"], dxm, tw, "wgrad_o")
        dx, dg1, dkv = _inproj_bwd(dpm, dkvm, dkvh, p["wpt"], s["x"], p["g1"], dxm, tq)
        g_in = jnp.concatenate(
            [_wgrad(dpm, s["h"], tw, "wgrad_in_main", [(0, 0, O_Q)] + _head_rows(O_Q, NQ)),
             _wgrad(dkv, s["h"], tw, "wgrad_in_kv", _head_rows(0, 2 * NKV))], axis=0)
        dy = dx
        gsmall[l] = dict(g1=dg1, cw=dcw[:3], gq=dgq[0, :HD], gk=dgk[0, :HD], sk=dsk[0, :NQ], gco=dgco,
                         gao=dgao, g2=dg2)
        handed[f"ffn{l}"] = reduce_3(ffn_2, g_in)
        in_2 = reduce_1([g_in.reshape(N_CHIPS, -1, D), g_o.reshape(N_CHIPS, -1, D)], f"in{l}", ids[f"in{l}"])
    grad_x = dy.reshape(x.shape)

    small_shapes = dict(g1=(D,), cw=(3, CC), gq=(HD,), gk=(HD,), sk=(NQ,), gco=(CC,), gao=(NQ * HD,), g2=(D,))
    red = _allreduce_small(_pack_rows([gsmall[l][n] for l in range(depth) for n in small_shapes]
                                      + [lpart[0:1, 0:1]])).reshape(-1)
    red_small, offs = {n: [] for n in small_shapes}, 0
    for l in range(depth):
        for n, shp in small_shapes.items():
            cnt = _count(shp)
            red_small[n].append(red[offs:offs + cnt].reshape(shp))
            offs += -(-cnt // 128) * 128
    loss = red[offs]
    g_small = {n: jnp.stack(v) for n, v in red_small.items()}
    g_cw = lax.dynamic_slice_in_dim(g_small["cw"], kme * (CC // N_CHIPS), CC // N_CHIPS, axis=2)

    weights = [norm1_g, w_in, conv_w, q_norm_g, k_norm_g, sinks, conv_out_g, attn_out_g, w_o, norm2_g, w_gate,
               w_up, w_down]
    moms = [m_norm1_g, m_w_in, m_conv_w, m_q_norm_g, m_k_norm_g, m_sinks, m_conv_out_g, m_attn_out_g, m_w_o,
            m_norm2_g, m_w_gate, m_w_up, m_w_down]
    vars_ = [v_norm1_g, v_w_in, v_conv_w, v_q_norm_g, v_k_norm_g, v_sinks, v_conv_out_g, v_attn_out_g, v_w_o,
             v_norm2_g, v_w_gate, v_w_up, v_w_down]
    n_w = len(weights)
    big_idx = dict(zip(("in", "o", "g", "u", "d"), (1, 8, 10, 11, 12)))
    small_idx = [n for n in range(n_w) if n not in big_idx.values()]
    grads, deltas, new_m, new_v = [None] * n_w, [None] * n_w, [None] * n_w, [None] * n_w
    for n, g in zip(small_idx, (g_small["g1"], g_cw, g_small["gq"], g_small["gk"], g_small["sk"], g_small["gco"],
                                g_small["gao"], g_small["g2"])):
        grads[n] = g

    def update_big(name):
        n = big_idx[name]
        g = jnp.stack([rbig[l][name] for l in range(depth)])
        flip = g.shape != weights[n].shape
        rows2d = lambda a3: (_t(a3) if flip else a3).reshape(-1, D)
        res = _adamw(rows2d(weights[n]), g.reshape(-1, D), rows2d(moms[n]), rows2d(vars_[n]), f"adamw_{n}")
        res = [g] + [r.reshape(g.shape) for r in res]
        grads[n], deltas[n], new_m[n], new_v[n] = [_t(r) for r in res] if flip else res

    for l in range(depth):
        rbig[l]["g"], rbig[l]["u"], rbig[l]["d"] = reduce_4(handed[f"ffn{l}"], red)
    rbig[1]["in"], rbig[1]["o"] = reduce_4(handed["in1"], red)
    update_big("g")
    in_2 = reduce_2(in_2, new_v[big_idx["g"]])
    update_big("u")
    update_big("d")
    rbig[0]["in"], rbig[0]["o"] = reduce_4(reduce_3(in_2, new_v[big_idx["d"]]), None)
    for name in ("in", "o"):
        update_big(name)
    res = _adamw(*[_pack_rows([arrs[n] for n in small_idx]) for arrs in (weights, grads, moms, vars_)],
                 "adamw_small")
    offs = 0
    for n in small_idx:
        shp = weights[n].shape
        cnt = _count(shp)
        deltas[n], new_m[n], new_v[n] = [r.reshape(-1)[offs:offs + cnt].reshape(shp) for r in res]
        offs += -(-cnt // 128) * 128
    return (loss, grad_x, *grads, *deltas, *new_m, *new_v)
```

```python
import jax
import jax.numpy as jnp
from jax import lax
from jax.experimental import pallas as pl
from jax.experimental.pallas import tpu as pltpu
from jax.experimental.pallas import tpu_sc as plsc

F32 = jnp.float32
BF16 = jnp.bfloat16

D = 1024
CC = 512
NQ = 8
NKV = 2
HD = 64
HP = 128
GRP = NQ // NKV
FF = 2816
FFB = FF // 4
BLK = 128
EPS = 1e-6
NEG = -1e30
SCALE = HD ** -0.5
O_BG, O_CG, O_HC, O_Q = 0, CC, 2 * CC, 3 * CC
O_K = O_Q + NQ * HP
O_V = O_K + NKV * HP
NP = O_V + NKV * HP
NMAIN = O_K
MIXW = CC + NQ * HD
N_CHIPS = 4
VMEM_LIMIT = 56 * 1024 * 1024
MESH = pl.DeviceIdType.MESH

ADAM_LR, ADAM_B1, ADAM_B2, ADAM_EPS, ADAM_WD, ADAM_STEP = 0.001, 0.9, 0.999, 1e-08, 0.01, 10


def _cparams(sem=None, **kw):
    if sem is not None:
        kw["dimension_semantics"] = sem
    return pltpu.CompilerParams(vmem_limit_bytes=VMEM_LIMIT, **kw)


def _const_spec(shape):
    nd = len(shape)
    return pl.BlockSpec(shape, lambda *_: (0,) * nd, pipeline_mode=pl.Buffered(1))


def _nt(a, b):
    return lax.dot_general(a, b, (((1,), (1,)), ((), ())), preferred_element_type=F32)


def _tn(a, b):
    return lax.dot_general(a, b, (((0,), (0,)), ((), ())), preferred_element_type=F32)


def _rms_fwd(x, inv_n):
    r = lax.rsqrt(jnp.sum(x * x, axis=-1, keepdims=True) * inv_n + EPS)
    return r, x * r


def _rms_bwd(dy, g, xh, r, inv_n):
    dxh = dy * g
    return r * (dxh - xh * (jnp.sum(dxh * xh, axis=-1, keepdims=True) * inv_n))


W_IN_ROWS = 3 * CC + (NQ + 2 * NKV) * HD
W_IN_BLOCK = W_IN_ROWS // N_CHIPS


def _padded_row(row):
    return row + max(row - O_Q, 0) // HD * (HP - HD)


def _w_in_pieces(k):
    first = k * W_IN_BLOCK
    plain = min(max(O_Q - first, 0), W_IN_BLOCK)
    pieces = [(0, first, plain)] if plain else []
    return pieces + [(r, _padded_row(first + r), HD) for r in range(plain, W_IN_BLOCK, HD)]


def _inproj_fwd(x, g1, gi, own_i, chip, tm):
    t = x.shape[0]

    def body(chip_ref, x_ref, g_ref, gi_ref, own_ref, p_ref, h_ref, w_ref, sem):
        @pl.when(pl.program_id(0) == 0)
        def _():
            for k in range(N_CHIPS):
                for src, dst, rows in _w_in_pieces(k):
                    @pl.when(chip_ref[0] == k)
                    def _():
                        pltpu.make_async_copy(own_ref.at[pl.ds(src, rows)], w_ref.at[pl.ds(dst, rows)], sem).start()

                    @pl.when(chip_ref[0] != k)
                    def _():
                        pltpu.make_async_copy(gi_ref.at[k, pl.ds(src, rows)], w_ref.at[pl.ds(dst, rows)], sem).start()
            for slot in range(NQ + 2 * NKV):
                w_ref[O_Q + slot * HP + HD:O_Q + (slot + 1) * HP, :] = jnp.zeros((HP - HD, D), BF16)
            landed = w_ref.at[pl.ds(0, W_IN_ROWS)]
            pltpu.make_async_copy(landed, landed, sem).wait()

        _, xh = _rms_fwd(x_ref[...], 1.0 / D)
        h = (xh * g_ref[...]).astype(BF16)
        h_ref[...] = h
        p_ref[...] = _nt(h, w_ref[...])

    const = lambda shape: pl.BlockSpec(shape, lambda i, c: (0,) * len(shape))
    return pl.pallas_call(
        body, name="inproj_fwd",
        grid_spec=pltpu.PrefetchScalarGridSpec(
            num_scalar_prefetch=1, grid=(t // tm,),
            in_specs=[pl.BlockSpec((tm, D), lambda i, c: (i, 0)), const((1, D)), ANY, ANY],
            out_specs=[pl.BlockSpec((tm, NP), lambda i, c: (i, 0)), pl.BlockSpec((tm, D), lambda i, c: (i, 0)),
                       const((NP, D))],
            scratch_shapes=[pltpu.SemaphoreType.DMA(())]),
        out_shape=[jax.ShapeDtypeStruct((t, NP), F32), jax.ShapeDtypeStruct((t, D), BF16),
                   jax.ShapeDtypeStruct((NP, D), BF16)],
        compiler_params=_cparams(("arbitrary",)),
    )(chip, x, g1, gi, own_i)


def _band_mask():
    r_io = lax.broadcasted_iota(jnp.int32, (BLK, 2 * BLK), 0)
    c_io = lax.broadcasted_iota(jnp.int32, (BLK, 2 * BLK), 1)
    return (c_io > r_io) & (c_io <= r_io + BLK), c_io


def _conv_taps(uf, n):
    u1 = pltpu.roll(uf, 1, 0)[8:8 + n]
    u2 = pltpu.roll(uf, 2, 0)[8:8 + n]
    return u1, u2


def _attn_probs(qs, kband, sink, valid):
    s = jnp.where(valid, _nt(qs, kband), NEG)
    m = jnp.maximum(jnp.max(s, axis=-1, keepdims=True), sink)
    p = jnp.exp(s - m)
    es = jnp.exp(sink - m)
    inv = 1.0 / (jnp.sum(p, axis=-1, keepdims=True) + es)
    return p * inv, es * inv


def _norm_keys(kraw, gk):
    out = []
    for h in range(NKV):
        kh = kraw[:, h * HP:(h + 1) * HP]
        rk, khat = _rms_fwd(kh, 1.0 / HD)
        out.append((khat, rk, (khat * gk).astype(BF16)))
    return out


def _mixer_fwd(proj, x, cw, gq, gk, sinks, gco, gao, wo, tq):
    t = proj.shape[0]
    nb = tq // BLK
    r8 = tq // 8

    def body(p_ref, cgp_ref, hcp_ref, kvp_ref, x_ref, cw_ref, gq_ref, gk_ref, sk_ref, gco_ref, gao_ref,
             wo_ref, xm_ref, mix_ref, ao_ref, aop_ref):
        i = pl.program_id(0)
        cg = p_ref[:, O_CG:O_CG + CC]
        hc = p_ref[:, O_HC:O_HC + CC]
        u = cg * hc
        up = jnp.where(i > 0, cgp_ref[...] * hcp_ref[...], 0.0)
        u1, u2 = _conv_taps(jnp.concatenate([up, u], axis=0), tq)
        y = cw_ref[0:1, :] * u2 + cw_ref[1:2, :] * u1 + cw_ref[2:3, :] * u
        co = p_ref[:, O_BG:O_BG + CC] * y
        _, coh = _rms_fwd(co, 1.0 / CC)
        cn = coh * gco_ref[...]
        kraw = jnp.concatenate([kvp_ref[:, 0:NKV * HP], p_ref[:, O_K:O_K + NKV * HP]], axis=0)
        vraw = jnp.concatenate([kvp_ref[:, NKV * HP:], p_ref[:, O_V:O_V + NKV * HP]], axis=0)
        keys = _norm_keys(kraw, gk_ref[...])
        vb = [vraw[:, h * HP:(h + 1) * HP].astype(BF16) for h in range(NKV)]
        base_valid, c_io = _band_mask()
        gqs = gq_ref[...] * SCALE
        for b in range(nb):
            lo = jnp.where(i * nb + b == 0, BLK, 0)
            valid = base_valid & (c_io >= lo)
            for g in range(NQ):
                h = g // GRP
                qg = p_ref[b * BLK:(b + 1) * BLK, O_Q + g * HP:O_Q + (g + 1) * HP]
                _, qh = _rms_fwd(qg, 1.0 / HD)
                qs = (qh * gqs).astype(BF16)
                pr, _ = _attn_probs(qs, keys[h][2][b * BLK:b * BLK + 2 * BLK], sk_ref[0, g], valid)
                aop_ref[b * BLK:(b + 1) * BLK, g * HP:(g + 1) * HP] = jnp.dot(
                    pr.astype(BF16), vb[h][b * BLK:b * BLK + 2 * BLK], preferred_element_type=F32)
        for j in range(NQ // 2):
            ao_ref[:, j * HP:(j + 1) * HP] = (aop_ref[:, 2 * j * HP:(2 * j + 1) * HP]
                                              + pltpu.roll(aop_ref[:, (2 * j + 1) * HP:(2 * j + 2) * HP], HD, 1))
        _, aoh = _rms_fwd(ao_ref[...], 1.0 / (NQ * HD))
        an = aoh * gao_ref[...]
        mix = jnp.concatenate([cn, an], axis=1).astype(BF16)
        mix_ref[...] = mix
        xm_ref[...] = x_ref[...] + jnp.dot(mix, wo_ref[...], preferred_element_type=F32)

    prev8 = lambda col: pl.BlockSpec((8, CC), lambda i: (jnp.maximum(i * r8 - 1, 0), col))
    return pl.pallas_call(
        body, name="mixer_fwd", grid=(t // tq,),
        in_specs=[
            pl.BlockSpec((tq, NP), lambda i: (i, 0)),
            prev8(O_CG // CC), prev8(O_HC // CC),
            pl.BlockSpec((BLK, 2 * NKV * HP), lambda i: (jnp.maximum(i * nb - 1, 0), O_K // (2 * NKV * HP))),
            pl.BlockSpec((tq, D), lambda i: (i, 0)),
            _const_spec((8, CC)), _const_spec((1, HP)), _const_spec((1, HP)),
            pl.BlockSpec(memory_space=pltpu.SMEM),
            _const_spec((1, CC)), _const_spec((1, NQ * HD)), _const_spec((MIXW, D)),
        ],
        out_specs=[pl.BlockSpec((tq, D), lambda i: (i, 0)), pl.BlockSpec((tq, MIXW), lambda i: (i, 0)),
                   pl.BlockSpec((tq, NQ * HD), lambda i: (i, 0))],
        out_shape=[jax.ShapeDtypeStruct((t, D), F32), jax.ShapeDtypeStruct((t, MIXW), BF16),
                   jax.ShapeDtypeStruct((t, NQ * HD), F32)],
        scratch_shapes=[pltpu.VMEM((tq, NQ * HP), F32)],
        compiler_params=_cparams(("parallel",)),
    )(proj, proj, proj, proj, x, cw, gq, gk, sinks, gco, gao, wo)


def _ffn_weight_specs():
    return [pl.BlockSpec((N_CHIPS, FFB, D), lambda i, j=j: (0, j, 0), pipeline_mode=pl.Buffered(1))
            for j in range(3)]


def _ffn_fwd(xm, g2, gf, tm, tgt=None):
    t = xm.shape[0]
    last = tgt is not None

    def body(x_ref, g_ref, wg_ref, wu_ref, wd_ref, *rest):
        t_ref, rest = (rest[0], rest[1:]) if last else (None, rest)
        l_ref, rest = (rest[0], rest[1:]) if last else (None, rest)
        xo_ref, a_ref, b_ref, h2_ref = rest
        xv = x_ref[...]
        _, xh = _rms_fwd(xv, 1.0 / D)
        h2 = (xh * g_ref[...]).astype(BF16)
        h2_ref[...] = h2
        acc = xv
        for k in range(N_CHIPS):
            a = _nt(h2, wg_ref[k])
            b = _nt(h2, wu_ref[k])
            a_ref[k] = a.astype(BF16)
            b_ref[k] = b.astype(BF16)
            hm = (a * jax.nn.sigmoid(a) * b).astype(BF16)
            acc = acc + jnp.dot(hm, wd_ref[k], preferred_element_type=F32)
        if last:
            @pl.when(pl.program_id(0) == 0)
            def _():
                l_ref[...] = jnp.zeros_like(l_ref)

            e = acc - t_ref[...]
            xo_ref[...] = e * (1.0 / D)
            l_ref[...] += jnp.sum(jnp.sum(e * e, axis=-1, keepdims=True), axis=0, keepdims=True) * (0.5 / D)
        else:
            xo_ref[...] = acc

    row = lambda w: pl.BlockSpec((tm, w), lambda i: (i, 0))
    blk = pl.BlockSpec((N_CHIPS, tm, FFB), lambda i: (0, i, 0))
    bsd = jax.ShapeDtypeStruct((N_CHIPS, t, FFB), BF16)
    return pl.pallas_call(
        body, name="ffn_fwd_loss" if last else "ffn_fwd", grid=(t // tm,),
        in_specs=[row(D), _const_spec((1, D))] + _ffn_weight_specs() + ([row(D)] if last else []),
        out_specs=([pl.BlockSpec((8, 128), lambda i: (0, 0))] if last else []) + [row(D), blk, blk, row(D)],
        out_shape=([jax.ShapeDtypeStruct((8, 128), F32)] if last else [])
        + [jax.ShapeDtypeStruct((t, D), F32), bsd, bsd, jax.ShapeDtypeStruct((t, D), BF16)],
        compiler_params=_cparams(("arbitrary" if last else "parallel",)),
    )(*((xm, g2, gf, gf, gf) + ((tgt,) if last else ())))


def _ffn_bwd(dy, xm, g2, a, b, gf, tm):
    t = dy.shape[0]

    def body(dy_ref, x_ref, g_ref, a_ref, b_ref, wg_ref, wu_ref, wd_ref, dx_ref, da_ref, db_ref, hm_ref, dg_ref):
        @pl.when(pl.program_id(0) == 0)
        def _():
            dg_ref[...] = jnp.zeros_like(dg_ref)

        dyv = dy_ref[...]
        dyb = dyv.astype(BF16)
        dh2 = jnp.zeros_like(dyv)
        for k in range(N_CHIPS):
            dhm = _nt(dyb, wd_ref[k])
            av = a_ref[k].astype(F32)
            bv = b_ref[k].astype(F32)
            sig = jax.nn.sigmoid(av)
            sil = av * sig
            hm_ref[k] = (sil * bv).astype(BF16)
            da = (dhm * bv * (sig * (1.0 + av * (1.0 - sig)))).astype(BF16)
            db = (dhm * sil).astype(BF16)
            da_ref[k] = da
            db_ref[k] = db
            dh2 = (dh2 + jnp.dot(da, wg_ref[k], preferred_element_type=F32)
                   + jnp.dot(db, wu_ref[k], preferred_element_type=F32))
        r, xh = _rms_fwd(x_ref[...], 1.0 / D)
        dg_ref[...] += jnp.sum(dh2 * xh, axis=0, keepdims=True)
        dx_ref[...] = dyv + _rms_bwd(dh2, g_ref[...], xh, r, 1.0 / D)

    row = lambda w: pl.BlockSpec((tm, w), lambda i: (i, 0))
    blk = pl.BlockSpec((N_CHIPS, tm, FFB), lambda i: (0, i, 0))
    bsd = jax.ShapeDtypeStruct((N_CHIPS, t, FFB), BF16)
    return pl.pallas_call(
        body, name="ffn_bwd", grid=(t // tm,),
        in_specs=[row(D), row(D), _const_spec((1, D)), blk, blk] + _ffn_weight_specs(),
        out_specs=[row(D), blk, blk, blk, pl.BlockSpec((1, D), lambda i: (0, 0))],
        out_shape=[jax.ShapeDtypeStruct((t, D), F32), bsd, bsd, bsd, jax.ShapeDtypeStruct((1, D), F32)],
        compiler_params=_cparams(("arbitrary",)),
    )(dy, xm, g2, a, b, gf, gf, gf)


def _wgrad_blocks(a, b, tt, name):
    _, t, rows = a.shape
    cols = b.shape[1]
    nsteps = t // tt

    def body(a_ref, b_ref, o_ref, acc_ref):
        s = pl.program_id(0)

        @pl.when(s == 0)
        def _():
            acc_ref[...] = jnp.zeros_like(acc_ref)

        bv = b_ref[...].astype(BF16)
        for k in range(N_CHIPS):
            acc_ref[k] += _tn(a_ref[k], bv)

        @pl.when(s == nsteps - 1)
        def _():
            o_ref[...] = acc_ref[...].astype(BF16)

    return pl.pallas_call(
        body, name=name, grid=(nsteps,),
        in_specs=[pl.BlockSpec((N_CHIPS, tt, rows), lambda s: (0, s, 0)), pl.BlockSpec((tt, cols), lambda s: (s, 0))],
        out_specs=pl.BlockSpec((N_CHIPS, rows, cols), lambda s: (0, 0, 0)),
        out_shape=jax.ShapeDtypeStruct((N_CHIPS, rows, cols), BF16),
        scratch_shapes=[pltpu.VMEM((N_CHIPS, rows, cols), F32)],
        compiler_params=_cparams(("arbitrary",)),
    )(a, b)


def _head_rows(first, n_heads):
    return [(first + g * HD, first + g * HP, HD) for g in range(n_heads)]


def _wgrad(a, b, tt, name, pieces=None, halo=None):
    t, k = a.shape
    n = b.shape[1]
    nsteps = t // tt
    pieces = pieces or [(0, 0, k)]
    rows = sum(p[2] for p in pieces)
    per_step = tt // halo[1] if halo else 0
    n_tiles = t // halo[1] if halo else 0

    def body(a_ref, b_ref, *rest):
        o_ref, acc_ref = rest[per_step:]
        s = pl.program_id(0)

        @pl.when(s == 0)
        def _():
            acc_ref[...] = jnp.zeros_like(acc_ref)

        if halo:
            tq = halo[1]
            parts = []
            for j in range(per_step):
                given = jnp.where(s * per_step + j + 1 < n_tiles, rest[j][...], 0.0)
                parts += [a_ref[j * tq:(j + 1) * tq - BLK, :], a_ref[(j + 1) * tq - BLK:(j + 1) * tq, :] + given]
            av = jnp.concatenate(parts, axis=0)
        else:
            av = a_ref[...]
        acc_ref[...] += _tn(av.astype(BF16), b_ref[...].astype(BF16))

        @pl.when(s == nsteps - 1)
        def _():
            for dst, src, size in pieces:
                o_ref[dst:dst + size, :] = acc_ref[src:src + size, :].astype(BF16)

    halo_specs = [pl.BlockSpec((BLK, k), lambda s, j=j: (jnp.minimum(s * per_step + j + 1, n_tiles - 1), 0))
                  for j in range(per_step)]
    return pl.pallas_call(
        body, name=name, grid=(nsteps,),
        in_specs=[pl.BlockSpec((tt, k), lambda s: (s, 0)), pl.BlockSpec((tt, n), lambda s: (s, 0))] + halo_specs,
        out_specs=pl.BlockSpec((rows, n), lambda s: (0, 0)),
        out_shape=jax.ShapeDtypeStruct((rows, n), BF16),
        scratch_shapes=[pltpu.VMEM((k, n), F32)],
        compiler_params=_cparams(("arbitrary",)),
    )(a, b, *([halo[0]] * per_step if halo else []))


def _mixer_bwd(dxm, proj, ao, cw, gq, gk, sinks, gco, gao, wo, tq):
    t = proj.shape[0]
    nb = tq // BLK
    r8 = tq // 8
    nt = t // tq
    te = tq + 8
    kvw = 2 * NKV * HP

    def body(dx_ref, dxn_ref, p_ref, cgp_ref, hcp_ref, bgn_ref, cgn_ref, hcn_ref, kvp_ref, ao_ref, cw_ref, gq_ref,
             gk_ref, sk_ref, gco_ref, gao_ref, wo_ref,
             dpm_ref, dkvm_ref, dkvh_ref, dcw_ref, dgq_ref, dgk_ref, dsk_ref, dgco_ref, dgao_ref, acc_ref):
        i = pl.program_id(0)

        @pl.when(i == 0)
        def _():
            for r in (dcw_ref, dgq_ref, dgk_ref, dsk_ref, dgco_ref, dgao_ref):
                r[...] = jnp.zeros_like(r)

        acc_ref[...] = jnp.zeros_like(acc_ref)
        live_rows = jnp.where(i < nt - 1, te, tq)
        dxb = dx_ref[...].astype(BF16)
        dxe = jnp.concatenate([dxb, dxn_ref[...].astype(BF16)], axis=0)
        dcn = _nt(dxe, wo_ref[0:CC, :])
        bg = jnp.concatenate([p_ref[:, O_BG:O_BG + CC], bgn_ref[...]], axis=0)
        cg = jnp.concatenate([p_ref[:, O_CG:O_CG + CC], cgn_ref[...]], axis=0)
        hc = jnp.concatenate([p_ref[:, O_HC:O_HC + CC], hcn_ref[...]], axis=0)
        u = cg * hc
        up = jnp.where(i > 0, cgp_ref[...] * hcp_ref[...], 0.0)
        u1, u2 = _conv_taps(jnp.concatenate([up, u], axis=0), te)
        w0, w1, w2 = cw_ref[0:1, :], cw_ref[1:2, :], cw_ref[2:3, :]
        y = w0 * u2 + w1 * u1 + w2 * u
        co = bg * y
        rc, coh = _rms_fwd(co, 1.0 / CC)
        dco = _rms_bwd(dcn, gco_ref[...], coh, rc, 1.0 / CC)
        row_io = lax.broadcasted_iota(jnp.int32, (te, 1), 0)
        own = row_io < tq
        dgco_ref[...] += jnp.sum(jnp.where(own, dcn * coh, 0.0), axis=0, keepdims=True)
        dyc = jnp.where(row_io < live_rows, dco * bg, 0.0)
        dyo = jnp.where(own, dyc, 0.0)
        dcw_ref[0:1, :] += jnp.sum(dyo * u2, axis=0, keepdims=True)
        dcw_ref[1:2, :] += jnp.sum(dyo * u1, axis=0, keepdims=True)
        dcw_ref[2:3, :] += jnp.sum(dyo * u, axis=0, keepdims=True)
        dy1 = pltpu.roll(dyc, te - 1, 0)[0:tq]
        dy2 = pltpu.roll(dyc, te - 2, 0)[0:tq]
        du = w2 * dyc[0:tq] + w1 * dy1 + w0 * dy2
        dpm_ref[:, O_BG:O_BG + CC] = (dco[0:tq] * y[0:tq]).astype(BF16)
        dpm_ref[:, O_CG:O_CG + CC] = (du * hc[0:tq]).astype(BF16)
        dpm_ref[:, O_HC:O_HC + CC] = (du * cg[0:tq]).astype(BF16)
        kraw = jnp.concatenate([kvp_ref[:, 0:NKV * HP], p_ref[:, O_K:O_K + NKV * HP]], axis=0)
        vraw = jnp.concatenate([kvp_ref[:, NKV * HP:], p_ref[:, O_V:O_V + NKV * HP]], axis=0)
        gqv, gkv = gq_ref[...], gk_ref[...]
        keys = _norm_keys(kraw, gkv)
        vb = [vraw[:, h * HP:(h + 1) * HP].astype(BF16) for h in range(NKV)]
        base_valid, c_io = _band_mask()
        lane = lax.broadcasted_iota(jnp.int32, (1, HP), 1)
        dgq, dgk, dsk = (jnp.zeros((1, HP), F32) for _ in range(3))
        dgao = jnp.zeros((1, NQ * HD), F32)
        for b in range(nb):
            lo = jnp.where(i * nb + b == 0, BLK, 0)
            valid = base_valid & (c_io >= lo)
            band = slice(b * BLK, b * BLK + 2 * BLK)
            blk = slice(b * BLK, (b + 1) * BLK)
            ra, aoh = _rms_fwd(ao_ref[blk, :], 1.0 / (NQ * HD))
            danb = _nt(dxb[blk], wo_ref[CC:MIXW, :])
            dgao = dgao + jnp.sum(danb * aoh, axis=0, keepdims=True)
            dao = _rms_bwd(danb, gao_ref[...], aoh, ra, 1.0 / (NQ * HD))
            dos = [dao[:, g // 2 * HP:(g // 2 + 1) * HP] for g in range(NQ)]
            dos = [(d if g % 2 == 0 else pltpu.roll(d, HD, 1)).astype(BF16) for g, d in enumerate(dos)]
            fwd = []
            for g in range(NQ):
                rq, qh = _rms_fwd(p_ref[blk, O_Q + g * HP:O_Q + (g + 1) * HP], 1.0 / HD)
                qs = (qh * (gqv * SCALE)).astype(BF16)
                fwd.append((rq, qh, qs) + _attn_probs(qs, keys[g // GRP][2][band], sk_ref[0, g], valid))
            dqs = []
            for h in range(NKV):
                khat, rk, kn = [a[band] for a in keys[h]]
                dss, prbs, qns, dobs = [], [], [], []
                for g in range(h * GRP, (h + 1) * GRP):
                    rq, qh, qs, pr, ps = fwd[g]
                    dob = dos[g]
                    dp = _nt(dob, vb[h][band])
                    delta = jnp.sum(pr * dp, axis=-1, keepdims=True)
                    dsb = (pr * (dp - delta)).astype(BF16)
                    dsk = dsk + jnp.where(lane == g, -jnp.sum(ps * delta, axis=0, keepdims=True), 0.0)
                    dqn = jnp.dot(dsb, kn, preferred_element_type=F32) * SCALE
                    dgq = dgq + jnp.sum(dqn * qh, axis=0, keepdims=True)
                    dqs.append(_rms_bwd(dqn, gqv, qh, rq, 1.0 / HD).astype(BF16))
                    dss.append(dsb)
                    prbs.append(pr.astype(BF16))
                    qns.append(qs)
                    dobs.append(dob)
                dkn = _tn(jnp.concatenate(dss, axis=0), jnp.concatenate(qns, axis=0))
                dv = _tn(jnp.concatenate(prbs, axis=0), jnp.concatenate(dobs, axis=0))
                dgk = dgk + jnp.sum(dkn * khat, axis=0, keepdims=True)
                acc_ref[band, h * HP:(h + 1) * HP] += _rms_bwd(dkn, gkv, khat, rk, 1.0 / HD)
                acc_ref[band, (NKV + h) * HP:(NKV + h + 1) * HP] += dv
            dpm_ref[blk, O_Q:O_K] = jnp.concatenate(dqs, axis=1)
        dgq_ref[...] += dgq
        dgk_ref[...] += dgk
        dsk_ref[...] += dsk
        dgao_ref[...] += dgao
        dkvh_ref[...] = acc_ref[0:BLK, :]
        dkvm_ref[...] = acc_ref[BLK:, :]

    prev8 = lambda col: pl.BlockSpec((8, CC), lambda i: (jnp.maximum(i * r8 - 1, 0), col))
    next8 = lambda col: pl.BlockSpec((8, CC), lambda i: (jnp.minimum((i + 1) * r8, t // 8 - 1), col))
    small = lambda n: pl.BlockSpec((1, n), lambda i: (0, 0))
    return pl.pallas_call(
        body, name="mixer_bwd", grid=(nt,),
        in_specs=[
            pl.BlockSpec((tq, D), lambda i: (i, 0)),
            pl.BlockSpec((8, D), lambda i: (jnp.minimum((i + 1) * r8, t // 8 - 1), 0)),
            pl.BlockSpec((tq, NP), lambda i: (i, 0)),
            prev8(O_CG // CC), prev8(O_HC // CC),
            next8(O_BG // CC), next8(O_CG // CC), next8(O_HC // CC),
            pl.BlockSpec((BLK, kvw), lambda i: (jnp.maximum(i * nb - 1, 0), O_K // kvw)),
            pl.BlockSpec((tq, NQ * HD), lambda i: (i, 0)),
            _const_spec((8, CC)), _const_spec((1, HP)), _const_spec((1, HP)),
            pl.BlockSpec(memory_space=pltpu.SMEM),
            _const_spec((1, CC)), _const_spec((1, NQ * HD)), _const_spec((MIXW, D)),
        ],
        out_specs=[
            pl.BlockSpec((tq, NMAIN), lambda i: (i, 0)),
            pl.BlockSpec((tq, kvw), lambda i: (i, 0)),
            pl.BlockSpec((BLK, kvw), lambda i: (i, 0)),
            pl.BlockSpec((8, CC), lambda i: (0, 0)), small(HP), small(HP), small(HP), small(CC), small(NQ * HD),
        ],
        out_shape=[
            jax.ShapeDtypeStruct((t, NMAIN), BF16), jax.ShapeDtypeStruct((t, kvw), F32),
            jax.ShapeDtypeStruct((nt * BLK, kvw), F32),
            jax.ShapeDtypeStruct((8, CC), F32), jax.ShapeDtypeStruct((1, HP), F32), jax.ShapeDtypeStruct((1, HP), F32),
            jax.ShapeDtypeStruct((1, HP), F32), jax.ShapeDtypeStruct((1, CC), F32),
            jax.ShapeDtypeStruct((1, NQ * HD), F32),
        ],
        scratch_shapes=[pltpu.VMEM((tq + BLK, kvw), F32)],
        compiler_params=_cparams(("arbitrary",)),
    )(dxm, dxm, proj, proj, proj, proj, proj, proj, proj, ao, cw, gq, gk, sinks, gco, gao, wo)


def _inproj_bwd(dpm, dkvm, dkvh, wpt, x, g1, dxm, tm):
    t = x.shape[0]
    kvw = 2 * NKV * HP
    nt = t // tm

    def body(dp_ref, dk_ref, dh_ref, w_ref, x_ref, g_ref, dxm_ref, dx_ref, dg_ref, dkv_ref):
        i = pl.program_id(0)

        @pl.when(i == 0)
        def _():
            dg_ref[...] = jnp.zeros_like(dg_ref)

        halo = jnp.where(i < nt - 1, dh_ref[...], 0.0)
        dkv_ref[0:tm - BLK, :] = dk_ref[0:tm - BLK, :].astype(BF16)
        dkv_ref[tm - BLK:tm, :] = (dk_ref[tm - BLK:tm, :] + halo).astype(BF16)
        dh = (jnp.dot(dp_ref[...], w_ref[0:NMAIN, :], preferred_element_type=F32)
              + jnp.dot(dkv_ref[...], w_ref[NMAIN:NP, :], preferred_element_type=F32))
        r, xh = _rms_fwd(x_ref[...], 1.0 / D)
        dg_ref[...] += jnp.sum(dh * xh, axis=0, keepdims=True)
        dx_ref[...] = dxm_ref[...] + _rms_bwd(dh, g_ref[...], xh, r, 1.0 / D)

    row = lambda w: pl.BlockSpec((tm, w), lambda i: (i, 0))
    return pl.pallas_call(
        body, name="inproj_bwd", grid=(nt,),
        in_specs=[row(NMAIN), row(kvw), pl.BlockSpec((BLK, kvw), lambda i: (jnp.minimum(i + 1, nt - 1), 0)),
                  _const_spec((NP, D)), row(D), _const_spec((1, D)), row(D)],
        out_specs=[row(D), pl.BlockSpec((1, D), lambda i: (0, 0)), row(kvw)],
        out_shape=[jax.ShapeDtypeStruct((t, D), F32), jax.ShapeDtypeStruct((1, D), F32),
                   jax.ShapeDtypeStruct((t, kvw), BF16)],
        compiler_params=_cparams(("arbitrary",)),
    )(dpm, dkvm, dkvh, wpt, x, g1, dxm)


def _rows_tile(rows, cap=512):
    for cand in range(min(rows, cap) // 16 * 16, 0, -16):
        if rows % cand == 0:
            return cand
    return rows


def _presum_halves(gs, theirs, core):
    n = len(gs)

    def body(c_ref, *refs):
        for g_ref, t_ref, o_ref in zip(refs[:n], refs[n:2 * n], refs[2 * n:]):
            o_ref[...] = (g_ref[...].astype(F32) + t_ref[...].astype(F32)).astype(BF16)

    half = lambda ta: pl.BlockSpec((None,) + ta.shape[1:], lambda k, c_ref: (k, 0, 0))
    own = lambda ta: pl.BlockSpec((None,) + ta.shape[1:], lambda k, c_ref: (k, c_ref[0], 0))
    return pl.pallas_call(
        body, name="presum",
        grid_spec=pltpu.PrefetchScalarGridSpec(
            num_scalar_prefetch=1, grid=(N_CHIPS,),
            in_specs=[own(ta) for ta in theirs] + [half(ta) for ta in theirs],
            out_specs=[half(ta) for ta in theirs]),
        out_shape=[jax.ShapeDtypeStruct(ta.shape, BF16) for ta in theirs],
        compiler_params=_cparams(("parallel",)),
    )(core, *gs, *theirs)


def _sum_chips(got, ps, chip):
    n = len(got)
    steps = 2

    def body(chip_ref, *refs):
        for c_ref, own_ref, o_ref in zip(refs[:n], refs[n:2 * n], refs[2 * n:]):
            acc = None
            for j in range(N_CHIPS):
                term = jnp.where(chip_ref[0] == j, own_ref[...], c_ref[j]).astype(F32)
                acc = term if acc is None else acc + term
            o_ref[...] = acc

    tile = lambda c: (c.shape[1] // steps, c.shape[2])
    return pl.pallas_call(
        body, name="chipsum",
        grid_spec=pltpu.PrefetchScalarGridSpec(
            num_scalar_prefetch=1, grid=(steps,),
            in_specs=[pl.BlockSpec((N_CHIPS,) + tile(c), lambda i, chip_ref: (0, i, 0)) for c in got]
            + [pl.BlockSpec((None,) + tile(c), lambda i, chip_ref: (chip_ref[0], i, 0)) for c in got],
            out_specs=[pl.BlockSpec(tile(c), lambda i, chip_ref: (i, 0)) for c in got]),
        out_shape=[jax.ShapeDtypeStruct(c.shape[1:], F32) for c in got],
        compiler_params=_cparams(("parallel",)),
    )(chip, *got, *ps)


def _adamw(w, g, m, v, name):
    rows, cols = w.shape
    tr = _rows_tile(rows)
    c1 = 1.0 - ADAM_B1 ** ADAM_STEP
    c2 = 1.0 - ADAM_B2 ** ADAM_STEP

    def body(w_ref, g_ref, m_ref, v_ref, d_ref, mo_ref, vo_ref):
        gv = g_ref[...]
        mn = ADAM_B1 * m_ref[...] + (1.0 - ADAM_B1) * gv
        vn = ADAM_B2 * v_ref[...] + (1.0 - ADAM_B2) * (gv * gv)
        mo_ref[...] = mn
        vo_ref[...] = vn
        d_ref[...] = -ADAM_LR * ((mn / c1) / (jnp.sqrt(vn / c2) + ADAM_EPS) + ADAM_WD * w_ref[...])

    spec = pl.BlockSpec((tr, cols), lambda i: (i, 0))
    sds = jax.ShapeDtypeStruct((rows, cols), F32)
    return pl.pallas_call(
        body, name=name, grid=(rows // tr,), in_specs=[spec] * 4, out_specs=[spec] * 3, out_shape=[sds] * 3,
        compiler_params=_cparams(("parallel",)),
    )(w, g, m, v)


def _place():
    x, y, c = lax.axis_index("x"), lax.axis_index("y"), lax.axis_index("c")
    chips = [(1 - x, y), (x, 1 - y), (1 - x, 1 - y)]
    return x, y, c, chips


ANY = pl.BlockSpec(memory_space=pl.ANY)
DMA_ROWS = 64


def _pieces(shape):
    rows = shape[-2]
    step = DMA_ROWS if rows % DMA_ROWS == 0 else rows
    lead = [()]
    for n in shape[:-2]:
        lead = [i + (k,) for i in lead for k in range(n)]
    return [i + (pl.ds(r0, step),) for i in lead for r0 in range(0, rows, step)]


def _start_pieces(make, src, dst):
    for idx in _pieces(src.shape):
        make(src.at[idx], dst.at[idx]).start()


def _gather_body(srcs, outs, sems, layer, start):
    nw = len(srcs)
    ssem, rsem, fssem, frsem = sems
    x, y, c, chips = _place()
    kme = 2 * x + y

    def plane(j, w, to):
        return lambda s, d: pltpu.make_async_remote_copy(
            src_ref=s, dst_ref=d, send_sem=ssem.at[j, w], recv_sem=rsem.at[j, w], device_id=to,
            device_id_type=MESH)

    def passed(j, w):
        return lambda s, d: pltpu.make_async_remote_copy(
            src_ref=s, dst_ref=d, send_sem=fssem.at[j, w], recv_sem=frsem.at[j, w],
            device_id=(x, y, 1 - c), device_id_type=MESH)

    @pl.when(c == layer)
    def _():
        for j, (px, py) in enumerate(chips):
            for w in range(nw):
                start(plane(j, w, (px, py, c)), srcs[w], outs[w].at[kme])
        for j, (px, py) in enumerate(chips):
            for w in range(nw):
                got = outs[w].at[2 * px + py]
                plane(j, w, (px, py, c))(got, got).wait_recv()
                start(passed(j, w), got, got)
        for j, (px, py) in enumerate(chips):
            for w in range(nw):
                got = outs[w].at[2 * px + py]
                plane(j, w, (px, py, c))(got, got).wait_send()
                passed(j, w)(got, got).wait_send()

    @pl.when(c != layer)
    def _():
        for j, (px, py) in enumerate(chips):
            for w in range(nw):
                got = outs[w].at[2 * px + py]
                passed(j, w)(got, got).wait_recv()


def _handshake(peers):
    barrier = pltpu.get_barrier_semaphore()
    for peer in peers:
        pl.semaphore_signal(barrier, inc=1, device_id=peer, device_id_type=MESH)
    pl.semaphore_wait(barrier, len(peers))


def _handshake_all():
    x, y, c, _ = _place()
    _handshake([(x ^ (r >> 2), y ^ ((r >> 1) & 1), c ^ (r & 1)) for r in range(1, 8)])


def _gather_layer_async(blocks, layer, name, collective_id):
    hbm = pltpu.MemorySpace.HBM
    srcs = [jax.new_ref(b, memory_space=hbm) for b in blocks]
    outs = [jax.empty_ref(jax.ShapeDtypeStruct((N_CHIPS,) + b.shape, b.dtype), memory_space=hbm) for b in blocks]

    @pl.kernel(mesh=plsc.ScalarSubcoreMesh(axis_name="seq", num_cores=1), name=name,
               scratch_types=[pltpu.SemaphoreType.DMA((3, len(blocks)))] * 4,
               compiler_params=pltpu.CompilerParams(collective_id=collective_id))
    def launch(*sems):
        _handshake_all()
        _gather_body(srcs, outs, sems, layer, lambda make, s, d: make(s, d).start())

    launch()
    return [o[...] for o in outs]


def _swap_siblings(arrs, halves, name, collective_id=None):
    nw = len(arrs)
    out_sds = [jax.ShapeDtypeStruct((a.shape[0], a.shape[1] // 2, a.shape[2]) if halves else a.shape, a.dtype)
               for a in arrs]

    def exchange(srcs, outs, ssem, rsem, start):
        x, y, c, _ = _place()

        def give(w):
            return lambda s, d: pltpu.make_async_remote_copy(
                src_ref=s, dst_ref=d, send_sem=ssem.at[w], recv_sem=rsem.at[w], device_id=(x, y, 1 - c),
                device_id_type=MESH)

        for w in range(nw):
            hr = outs[w].shape[1]
            start(give(w), srcs[w].at[:, pl.ds((1 - c) * hr, hr)] if halves else srcs[w], outs[w])
        for w in range(nw):
            give(w)(outs[w], outs[w]).wait()

    if collective_id is None:
        def body(*refs):
            exchange(refs[:nw], refs[nw:2 * nw], *refs[2 * nw:], _start_pieces)

        return pl.pallas_call(
            body, name=name, in_specs=[ANY] * nw, out_specs=[ANY] * nw, out_shape=out_sds,
            scratch_shapes=[pltpu.SemaphoreType.DMA((nw,))] * 2,
            compiler_params=_cparams(has_side_effects=True),
        )(*arrs)

    hbm = pltpu.MemorySpace.HBM
    srcs = [jax.new_ref(a, memory_space=hbm) for a in arrs]
    outs = [jax.empty_ref(sds, memory_space=hbm) for sds in out_sds]

    @pl.kernel(mesh=plsc.ScalarSubcoreMesh(axis_name="seq", num_cores=1), name=name,
               scratch_types=[pltpu.SemaphoreType.DMA((nw,))] * 2,
               compiler_params=pltpu.CompilerParams(collective_id=collective_id))
    def launch(ssem, rsem):
        x, y, c, _ = _place()
        _handshake([(x, y, 1 - c)])
        exchange(srcs, outs, ssem, rsem, lambda make, s, d: make(s, d).start())

    launch()
    return [o[...] for o in outs]


def _scatter_body(srcs, outs, sems, start):
    nw = len(srcs)
    ssem, rsem = sems
    x, y, c, chips = _place()
    kme = 2 * x + y

    def give(j, w, to):
        return lambda s, d: pltpu.make_async_remote_copy(
            src_ref=s, dst_ref=d, send_sem=ssem.at[j, w], recv_sem=rsem.at[j, w], device_id=to,
            device_id_type=MESH)

    for j, (px, py) in enumerate(chips):
        for w in range(nw):
            start(give(j, w, (px, py, c)), srcs[w].at[2 * px + py], outs[w].at[kme])
    for j, (px, py) in enumerate(chips):
        for w in range(nw):
            got = outs[w].at[2 * px + py]
            give(j, w, (px, py, c))(got, got).wait_recv()
    for j, (px, py) in enumerate(chips):
        for w in range(nw):
            sent = srcs[w].at[2 * px + py]
            give(j, w, (px, py, c))(sent, sent).wait_send()


def _scatter_chips_async(ps, name, collective_id):
    hbm = pltpu.MemorySpace.HBM
    srcs = [jax.new_ref(p, memory_space=hbm) for p in ps]
    outs = [jax.empty_ref(jax.ShapeDtypeStruct(p.shape, p.dtype), memory_space=hbm) for p in ps]

    @pl.kernel(mesh=plsc.ScalarSubcoreMesh(axis_name="seq", num_cores=1), name=name,
               scratch_types=[pltpu.SemaphoreType.DMA((3, len(ps)))] * 2,
               compiler_params=pltpu.CompilerParams(collective_id=collective_id))
    def launch(*sems):
        _handshake_all()
        _scatter_body(srcs, outs, sems, lambda make, s, d: make(s, d).start())

    launch()
    return [o[...] for o in outs]


def _allreduce_small(v):
    rows = v.shape[0]

    def body(v_ref, o_ref, buf, ssem, rsem):
        x, y, c, _ = _place()
        me = 4 * x + 2 * y + c
        buf[me] = v_ref[...]
        sends = []
        for r in range(1, 8):
            peer = (x ^ (r >> 2), y ^ ((r >> 1) & 1), c ^ (r & 1))
            cp = pltpu.make_async_remote_copy(
                src_ref=v_ref, dst_ref=buf.at[me], send_sem=ssem.at[r - 1], recv_sem=rsem.at[r - 1],
                device_id=peer, device_id_type=MESH)
            cp.start()
            sends.append(cp)
        for r in range(1, 8):
            src = me ^ r
            pltpu.make_async_remote_copy(
                src_ref=v_ref, dst_ref=buf.at[src], send_sem=ssem.at[r - 1], recv_sem=rsem.at[r - 1],
                device_id=(x, y, c), device_id_type=MESH).wait_recv()
        for cp in sends:
            cp.wait_send()
        acc = buf[0]
        for d in range(1, 8):
            acc = acc + buf[d]
        o_ref[...] = acc

    vm = pl.BlockSpec(memory_space=pltpu.VMEM)
    return pl.pallas_call(
        body, name="allreduce_small", in_specs=[vm], out_specs=vm,
        out_shape=jax.ShapeDtypeStruct(v.shape, F32),
        scratch_shapes=[pltpu.VMEM((8, rows, 128), F32), pltpu.SemaphoreType.DMA((7,)),
                        pltpu.SemaphoreType.DMA((7,))],
        compiler_params=_cparams(has_side_effects=True),
    )(v)


def _t(w):
    return jnp.swapaxes(w, -1, -2)


def _count(shape):
    n = 1
    for s in shape:
        n *= s
    return n


def _pack_rows(arrs):
    flat = [jnp.pad(a.reshape(-1), (0, (-_count(a.shape)) % 128)) for a in arrs]
    v = jnp.concatenate(flat)
    rows = -(-v.shape[0] // (8 * 128)) * 8
    return jnp.pad(v, (0, rows * 128 - v.shape[0])).reshape(rows, 128)


def kernel(x, norm1_g, w_in, conv_w, q_norm_g, k_norm_g, sinks, conv_out_g, attn_out_g, w_o, norm2_g, w_gate, w_up, w_down, loss_target, m_norm1_g, m_w_in, m_conv_w, m_q_norm_g, m_k_norm_g, m_sinks, m_conv_out_g, m_attn_out_g, m_w_o, m_norm2_g, m_w_gate, m_w_up, m_w_down, v_norm1_g, v_w_in, v_conv_w, v_q_norm_g, v_k_norm_g, v_sinks, v_conv_out_g, v_attn_out_g, v_w_o, v_norm2_g, v_w_gate, v_w_up, v_w_down):
    depth = w_in.shape[0]
    t = x.shape[1]
    xs = x.reshape(t, D)
    tgt = loss_target.reshape(t, D)
    xi, yi = lax.axis_index("x"), lax.axis_index("y")
    kme = 2 * xi + yi
    tm = min(512, t)
    tq = min(512, t)
    tf = min(256, t)
    tw = min(1024, t)

    cwp = jnp.pad(conv_w.reshape(depth * 3, CC // N_CHIPS), ((0, 8 - depth * 3), (0, 0)))
    own_f = [jnp.concatenate([_t(w_gate[l]), _t(w_up[l]), w_down[l]], axis=0).astype(BF16) for l in range(depth)]
    own_o = [w_o[l].astype(BF16) for l in range(depth)]
    own_i = [_t(w_in[l]).astype(BF16) for l in range(depth)]
    mine = lambda got, own: lax.dynamic_update_index_in_dim(got, own, kme, 0)
    (got_i0,) = _gather_layer_async([own_i[0]], 0, "gather_in0_seq", collective_id=14)
    got_ocw = _gather_layer_async([own_o[0], cwp], 0, "gather_o0_seq", collective_id=15)
    got_i0, own_f, own_o, own_i = lax.optimization_barrier((got_i0, own_f, own_o, own_i))
    gf0_in = lax.optimization_barrier((own_f[0], got_i0))[0]
    (got_f0,) = _gather_layer_async([gf0_in], 0, "gather_ffn0_seq", collective_id=6)

    chip = kme.reshape(1).astype(jnp.int32)

    def layer_params(l, got_o, cw_full):
        return dict(
            wo=mine(got_o, own_o[l]).reshape(MIXW, D),
            cw=jnp.pad(cw_full[l], ((0, 5), (0, 0))),
            g1=norm1_g[l].reshape(1, D), g2=norm2_g[l].reshape(1, D),
            gq=jnp.pad(q_norm_g[l], (0, HP - HD)).reshape(1, HP), gk=jnp.pad(k_norm_g[l], (0, HP - HD)).reshape(1, HP),
            sk=sinks[l].reshape(1, NQ), gco=conv_out_g[l].reshape(1, CC),
            gao=attn_out_g[l].reshape(1, NQ * HD))

    saved, layers = [], []
    cur = xs
    for l in range(depth):
        x_in = cur
        if l == 0:
            got_i = got_i0
        else:
            got_f1, got_o, got_i = lax.optimization_barrier((got_l1, cur))[0]
        proj, h, wpt = _inproj_fwd(cur, norm1_g[l].reshape(1, D), got_i, own_i[l], chip, tm)
        if l == 0:
            got_o, got_cw = lax.optimization_barrier((got_ocw, proj))[0]
            cw_full = mine(got_cw, cwp).transpose(1, 0, 2).reshape(8, CC)[:depth * 3].reshape(depth, 3, CC)
        p = layer_params(l, got_o, cw_full)
        p["wpt"] = wpt
        xm, mix, ao = _mixer_fwd(proj, cur, p["cw"], p["gq"], p["gk"], p["sk"], p["gco"], p["gao"], p["wo"], tq)
        if l == 0:
            got_f0 = lax.optimization_barrier((got_f0, xm))[0]
            l1_in = lax.optimization_barrier(([own_f[1], own_o[1], own_i[1]], got_f0))[0]
            got_l1 = _gather_layer_async(l1_in, 1, "gather_layer1_seq", collective_id=1)
        p["gf"] = mine(got_f0 if l == 0 else got_f1, own_f[l])
        layers.append(p)
        if l < depth - 1:
            cur, a, b, h2 = _ffn_fwd(xm, p["g2"], p["gf"], tm)
        else:
            lpart, dy, a, b, h2 = _ffn_fwd(xm, p["g2"], p["gf"], tm, tgt)
        saved.append(dict(x=x_in, proj=proj, h=h, xm=xm, mix=mix, ao=ao, a=a, b=b, h2=h2))

    ci = lax.axis_index("c")
    core = ci.reshape(1).astype(jnp.int32)
    rbig = [dict() for _ in range(depth)]
    gsmall = [None] * depth

    def after_(vals, after):
        return vals if after is None else lax.optimization_barrier((vals, after))[0]

    def reduce_1(gs, tag, ids):
        return gs, _swap_siblings(gs, True, f"swap_halves_{tag}_seq", ids[0]), tag, ids

    def reduce_2(state, after):
        gs, theirs, tag, ids = state
        ps = _presum_halves(gs, after_(theirs, after), core)
        return ps, _scatter_chips_async(ps, f"scatter_{tag}_seq", ids[1]), tag, ids

    def reduce_3(state, after):
        ps, got, tag, ids = state
        r_mine = _sum_chips(after_(got, after), ps, chip)
        return r_mine, _swap_siblings(r_mine, False, f"swap_reduced_{tag}" + ("_seq" if ids[2] else ""), ids[2])

    def reduce_4(state, after):
        r_mine, r_theirs = state
        return [jnp.where(ci == 0, jnp.concatenate([a, b], axis=0), jnp.concatenate([b, a], axis=0))
                for a, b in zip(r_mine, after_(r_theirs, after))]

    ids = {"ffn1": (7, 4, 8), "in1": (9, 5, 10), "ffn0": (11, 2, 12), "in0": (13, 3, None)}
    in_2 = None
    handed = {}
    for l in reversed(range(depth)):
        p, s = layers[l], saved[l]
        dxm, da, db, hm, dg2 = _ffn_bwd(dy, s["xm"], p["g2"], s["a"], s["b"], p["gf"], tf)
        if in_2 is not None:
            in_2 = reduce_2(in_2, dxm)
        g_wg = _wgrad_blocks(da, s["h2"], tw, "wgrad_gate")
        g_wu = _wgrad_blocks(db, s["h2"], tw, "wgrad_up")
        g_wd = _wgrad_blocks(hm, dy, tw, "wgrad_down")
        if in_2 is not None:
            handed[f"in{l + 1}"] = reduce_3(in_2, g_wd)
        ffn_1 = reduce_1([g_wg, g_wu, g_wd], f"ffn{l}", ids[f"ffn{l}"])
        dpm, dkvm, dkvh, dcw, dgq, dgk, dsk, dgco, dgao = _mixer_bwd(
            dxm, s["proj"], s["ao"], p["cw"], p["gq"], p["gk"], p["sk"], p["gco"], p["gao"], p["wo"], tq)
        ffn_2 = reduce_2(ffn_1, dpm)
        g_o = _wgrad(s["mix"], dxm, tw, "wgrad_o")
        g_in = jnp.concatenate(
            [_wgrad(dpm, s["h"], tw, "wgrad_in_main", [(0, 0, O_Q)] + _head_rows(O_Q, NQ)),
             _wgrad(dkvm, s["h"], tw, "wgrad_in_kv", _head_rows(0, 2 * NKV), halo=(dkvh, tq))], axis=0)
        handed[f"ffn{l}"] = reduce_3(ffn_2, g_in)
        in_2 = reduce_1([g_in.reshape(N_CHIPS, -1, D), g_o.reshape(N_CHIPS, -1, D)], f"in{l}", ids[f"in{l}"])
        dx, dg1, _ = _inproj_bwd(dpm, dkvm, dkvh, p["wpt"], s["x"], p["g1"], after_(dxm, (g_in, g_o)), tq)
        dy = dx
        gsmall[l] = dict(g1=dg1, cw=dcw[:3], gq=dgq[0, :HD], gk=dgk[0, :HD], sk=dsk[0, :NQ], gco=dgco,
                         gao=dgao, g2=dg2)
    grad_x = dy.reshape(x.shape)
    in_2 = reduce_2(in_2, dy)

    small_shapes = dict(g1=(D,), cw=(3, CC), gq=(HD,), gk=(HD,), sk=(NQ,), gco=(CC,), gao=(NQ * HD,), g2=(D,))
    red = _allreduce_small(_pack_rows([gsmall[l][n] for l in range(depth) for n in small_shapes]
                                      + [lpart[0:1, 0:1]])).reshape(-1)
    red_small, offs = {n: [] for n in small_shapes}, 0
    for l in range(depth):
        for n, shp in small_shapes.items():
            cnt = _count(shp)
            red_small[n].append(red[offs:offs + cnt].reshape(shp))
            offs += -(-cnt // 128) * 128
    loss = red[offs]
    g_small = {n: jnp.stack(v) for n, v in red_small.items()}
    g_cw = lax.dynamic_slice_in_dim(g_small["cw"], kme * (CC // N_CHIPS), CC // N_CHIPS, axis=2)

    weights = [norm1_g, w_in, conv_w, q_norm_g, k_norm_g, sinks, conv_out_g, attn_out_g, w_o, norm2_g, w_gate,
               w_up, w_down]
    moms = [m_norm1_g, m_w_in, m_conv_w, m_q_norm_g, m_k_norm_g, m_sinks, m_conv_out_g, m_attn_out_g, m_w_o,
            m_norm2_g, m_w_gate, m_w_up, m_w_down]
    vars_ = [v_norm1_g, v_w_in, v_conv_w, v_q_norm_g, v_k_norm_g, v_sinks, v_conv_out_g, v_attn_out_g, v_w_o,
             v_norm2_g, v_w_gate, v_w_up, v_w_down]
    n_w = len(weights)
    big_idx = dict(zip(("in", "o", "g", "u", "d"), (1, 8, 10, 11, 12)))
    small_idx = [n for n in range(n_w) if n not in big_idx.values()]
    grads, deltas, new_m, new_v = [None] * n_w, [None] * n_w, [None] * n_w, [None] * n_w
    for n, g in zip(small_idx, (g_small["g1"], g_cw, g_small["gq"], g_small["gk"], g_small["sk"], g_small["gco"],
                                g_small["gao"], g_small["g2"])):
        grads[n] = g

    def update_big(name):
        n = big_idx[name]
        g = jnp.stack([rbig[l][name] for l in range(depth)])
        flip = g.shape != weights[n].shape
        rows2d = lambda a3: (_t(a3) if flip else a3).reshape(-1, D)
        res = _adamw(rows2d(weights[n]), g.reshape(-1, D), rows2d(moms[n]), rows2d(vars_[n]), f"adamw_{n}")
        res = [g] + [r.reshape(g.shape) for r in res]
        grads[n], deltas[n], new_m[n], new_v[n] = [_t(r) for r in res] if flip else res

    for l in range(depth):
        rbig[l]["g"], rbig[l]["u"], rbig[l]["d"] = reduce_4(handed[f"ffn{l}"], red)
    rbig[1]["in"], rbig[1]["o"] = reduce_4(handed["in1"], red)
    for name in ("g", "u", "d"):
        update_big(name)
    rbig[0]["in"], rbig[0]["o"] = reduce_4(reduce_3(in_2, new_v[big_idx["d"]]), None)
    for name in ("in", "o"):
        update_big(name)
    res = _adamw(*[_pack_rows([arrs[n] for n in small_idx]) for arrs in (weights, grads, moms, vars_)],
                 "adamw_small")
    offs = 0
    for n in small_idx:
        shp = weights[n].shape
        cnt = _count(shp)
        deltas[n], new_m[n], new_v[n] = [r.reshape(-1)[offs:offs + cnt].reshape(shp) for r in res]
        offs += -(-cnt // 128) * 128
    return (loss, grad_x, *grads, *deltas, *new_m, *new_v)
```

```python
import jax
import jax.numpy as jnp
from jax import lax
from jax.experimental import pallas as pl
from jax.experimental.pallas import tpu as pltpu
from jax.experimental.pallas import tpu_sc as plsc

F32 = jnp.float32
BF16 = jnp.bfloat16

D = 1024
CC = 512
NQ = 8
NKV = 2
HD = 64
HP = 128
GRP = NQ // NKV
FF = 2816
FFB = FF // 4
BLK = 128
EPS = 1e-6
NEG = -1e30
SCALE = HD ** -0.5
O_BG, O_CG, O_HC, O_Q = 0, CC, 2 * CC, 3 * CC
O_K = O_Q + NQ * HP
O_V = O_K + NKV * HP
NP = O_V + NKV * HP
NMAIN = O_K
MIXW = CC + NQ * HD
N_CHIPS = 4
VMEM_LIMIT = 56 * 1024 * 1024
MESH = pl.DeviceIdType.MESH

ADAM_LR, ADAM_B1, ADAM_B2, ADAM_EPS, ADAM_WD, ADAM_STEP = 0.001, 0.9, 0.999, 1e-08, 0.01, 10


def _cparams(sem=None, **kw):
    if sem is not None:
        kw["dimension_semantics"] = sem
    return pltpu.CompilerParams(vmem_limit_bytes=VMEM_LIMIT, **kw)


def _const_spec(shape):
    nd = len(shape)
    return pl.BlockSpec(shape, lambda *_: (0,) * nd, pipeline_mode=pl.Buffered(1))


def _nt(a, b):
    return lax.dot_general(a, b, (((1,), (1,)), ((), ())), preferred_element_type=F32)


def _tn(a, b):
    return lax.dot_general(a, b, (((0,), (0,)), ((), ())), preferred_element_type=F32)


def _rms_fwd(x, inv_n):
    r = lax.rsqrt(jnp.sum(x * x, axis=-1, keepdims=True) * inv_n + EPS)
    return r, x * r


def _rms_bwd(dy, g, xh, r, inv_n):
    dxh = dy * g
    return r * (dxh - xh * (jnp.sum(dxh * xh, axis=-1, keepdims=True) * inv_n))


W_IN_ROWS = 3 * CC + (NQ + 2 * NKV) * HD
W_IN_BLOCK = W_IN_ROWS // N_CHIPS


def _padded_row(row):
    return row + max(row - O_Q, 0) // HD * (HP - HD)


def _w_in_pieces(k):
    first = k * W_IN_BLOCK
    plain = min(max(O_Q - first, 0), W_IN_BLOCK)
    pieces = [(0, first, plain)] if plain else []
    return pieces + [(r, _padded_row(first + r), HD) for r in range(plain, W_IN_BLOCK, HD)]


def _inproj_fwd(x, g1, gi, own_i, chip, tm):
    t = x.shape[0]

    def body(chip_ref, x_ref, g_ref, gi_ref, own_ref, p_ref, h_ref, w_ref, sem):
        @pl.when(pl.program_id(0) == 0)
        def _():
            for k in range(N_CHIPS):
                for src, dst, rows in _w_in_pieces(k):
                    @pl.when(chip_ref[0] == k)
                    def _():
                        pltpu.make_async_copy(own_ref.at[pl.ds(src, rows)], w_ref.at[pl.ds(dst, rows)], sem).start()

                    @pl.when(chip_ref[0] != k)
                    def _():
                        pltpu.make_async_copy(gi_ref.at[k, pl.ds(src, rows)], w_ref.at[pl.ds(dst, rows)], sem).start()
            for slot in range(NQ + 2 * NKV):
                w_ref[O_Q + slot * HP + HD:O_Q + (slot + 1) * HP, :] = jnp.zeros((HP - HD, D), BF16)
            landed = w_ref.at[pl.ds(0, W_IN_ROWS)]
            pltpu.make_async_copy(landed, landed, sem).wait()

        _, xh = _rms_fwd(x_ref[...], 1.0 / D)
        h = (xh * g_ref[...]).astype(BF16)
        h_ref[...] = h
        p_ref[...] = _nt(h, w_ref[...])

    const = lambda shape: pl.BlockSpec(shape, lambda i, c: (0,) * len(shape))
    return pl.pallas_call(
        body, name="inproj_fwd",
        grid_spec=pltpu.PrefetchScalarGridSpec(
            num_scalar_prefetch=1, grid=(t // tm,),
            in_specs=[pl.BlockSpec((tm, D), lambda i, c: (i, 0)), const((1, D)), ANY, ANY],
            out_specs=[pl.BlockSpec((tm, NP), lambda i, c: (i, 0)), pl.BlockSpec((tm, D), lambda i, c: (i, 0)),
                       const((NP, D))],
            scratch_shapes=[pltpu.SemaphoreType.DMA(())]),
        out_shape=[jax.ShapeDtypeStruct((t, NP), F32), jax.ShapeDtypeStruct((t, D), BF16),
                   jax.ShapeDtypeStruct((NP, D), BF16)],
        compiler_params=_cparams(("arbitrary",)),
    )(chip, x, g1, gi, own_i)


def _band_mask():
    r_io = lax.broadcasted_iota(jnp.int32, (BLK, 2 * BLK), 0)
    c_io = lax.broadcasted_iota(jnp.int32, (BLK, 2 * BLK), 1)
    return (c_io > r_io) & (c_io <= r_io + BLK), c_io


def _conv_taps(uf, n):
    u1 = pltpu.roll(uf, 1, 0)[8:8 + n]
    u2 = pltpu.roll(uf, 2, 0)[8:8 + n]
    return u1, u2


def _attn_probs(qs, kband, sink, valid):
    s = jnp.where(valid, _nt(qs, kband), NEG)
    m = jnp.maximum(jnp.max(s, axis=-1, keepdims=True), sink)
    p = jnp.exp(s - m)
    es = jnp.exp(sink - m)
    inv = 1.0 / (jnp.sum(p, axis=-1, keepdims=True) + es)
    return p * inv, es * inv


def _norm_keys(kraw, gk):
    out = []
    for h in range(NKV):
        kh = kraw[:, h * HP:(h + 1) * HP]
        rk, khat = _rms_fwd(kh, 1.0 / HD)
        out.append((khat, rk, (khat * gk).astype(BF16)))
    return out


def _mixer_fwd(proj, x, cw, gq, gk, sinks, gco, gao, wo, tq):
    t = proj.shape[0]
    nb = tq // BLK
    r8 = tq // 8

    def body(p_ref, cgp_ref, hcp_ref, kvp_ref, x_ref, cw_ref, gq_ref, gk_ref, sk_ref, gco_ref, gao_ref,
             wo_ref, xm_ref, mix_ref, ao_ref, aop_ref):
        i = pl.program_id(0)
        cg = p_ref[:, O_CG:O_CG + CC]
        hc = p_ref[:, O_HC:O_HC + CC]
        u = cg * hc
        up = jnp.where(i > 0, cgp_ref[...] * hcp_ref[...], 0.0)
        u1, u2 = _conv_taps(jnp.concatenate([up, u], axis=0), tq)
        y = cw_ref[0:1, :] * u2 + cw_ref[1:2, :] * u1 + cw_ref[2:3, :] * u
        co = p_ref[:, O_BG:O_BG + CC] * y
        _, coh = _rms_fwd(co, 1.0 / CC)
        cn = coh * gco_ref[...]
        kraw = jnp.concatenate([kvp_ref[:, 0:NKV * HP], p_ref[:, O_K:O_K + NKV * HP]], axis=0)
        vraw = jnp.concatenate([kvp_ref[:, NKV * HP:], p_ref[:, O_V:O_V + NKV * HP]], axis=0)
        keys = _norm_keys(kraw, gk_ref[...])
        vb = [vraw[:, h * HP:(h + 1) * HP].astype(BF16) for h in range(NKV)]
        base_valid, c_io = _band_mask()
        gqs = gq_ref[...] * SCALE
        for b in range(nb):
            lo = jnp.where(i * nb + b == 0, BLK, 0)
            valid = base_valid & (c_io >= lo)
            for g in range(NQ):
                h = g // GRP
                qg = p_ref[b * BLK:(b + 1) * BLK, O_Q + g * HP:O_Q + (g + 1) * HP]
                _, qh = _rms_fwd(qg, 1.0 / HD)
                qs = (qh * gqs).astype(BF16)
                pr, _ = _attn_probs(qs, keys[h][2][b * BLK:b * BLK + 2 * BLK], sk_ref[0, g], valid)
                aop_ref[b * BLK:(b + 1) * BLK, g * HP:(g + 1) * HP] = jnp.dot(
                    pr.astype(BF16), vb[h][b * BLK:b * BLK + 2 * BLK], preferred_element_type=F32)
        for j in range(NQ // 2):
            ao_ref[:, j * HP:(j + 1) * HP] = (aop_ref[:, 2 * j * HP:(2 * j + 1) * HP]
                                              + pltpu.roll(aop_ref[:, (2 * j + 1) * HP:(2 * j + 2) * HP], HD, 1))
        _, aoh = _rms_fwd(ao_ref[...], 1.0 / (NQ * HD))
        an = aoh * gao_ref[...]
        mix = jnp.concatenate([cn, an], axis=1).astype(BF16)
        mix_ref[...] = mix
        xm_ref[...] = x_ref[...] + jnp.dot(mix, wo_ref[...], preferred_element_type=F32)

    prev8 = lambda col: pl.BlockSpec((8, CC), lambda i: (jnp.maximum(i * r8 - 1, 0), col))
    return pl.pallas_call(
        body, name="mixer_fwd", grid=(t // tq,),
        in_specs=[
            pl.BlockSpec((tq, NP), lambda i: (i, 0)),
            prev8(O_CG // CC), prev8(O_HC // CC),
            pl.BlockSpec((BLK, 2 * NKV * HP), lambda i: (jnp.maximum(i * nb - 1, 0), O_K // (2 * NKV * HP))),
            pl.BlockSpec((tq, D), lambda i: (i, 0)),
            _const_spec((8, CC)), _const_spec((1, HP)), _const_spec((1, HP)),
            pl.BlockSpec(memory_space=pltpu.SMEM),
            _const_spec((1, CC)), _const_spec((1, NQ * HD)), _const_spec((MIXW, D)),
        ],
        out_specs=[pl.BlockSpec((tq, D), lambda i: (i, 0)), pl.BlockSpec((tq, MIXW), lambda i: (i, 0)),
                   pl.BlockSpec((tq, NQ * HD), lambda i: (i, 0))],
        out_shape=[jax.ShapeDtypeStruct((t, D), F32), jax.ShapeDtypeStruct((t, MIXW), BF16),
                   jax.ShapeDtypeStruct((t, NQ * HD), F32)],
        scratch_shapes=[pltpu.VMEM((tq, NQ * HP), F32)],
        compiler_params=_cparams(("parallel",)),
    )(proj, proj, proj, proj, x, cw, gq, gk, sinks, gco, gao, wo)


def _ffn_weight_specs():
    return [pl.BlockSpec((N_CHIPS, FFB, D), lambda i, j=j: (0, j, 0), pipeline_mode=pl.Buffered(1))
            for j in range(3)]


def _ffn_fwd(xm, g2, gf, tm, tgt=None):
    t = xm.shape[0]
    last = tgt is not None

    def body(x_ref, g_ref, wg_ref, wu_ref, wd_ref, *rest):
        t_ref, rest = (rest[0], rest[1:]) if last else (None, rest)
        l_ref, rest = (rest[0], rest[1:]) if last else (None, rest)
        xo_ref, a_ref, b_ref, h2_ref = rest
        xv = x_ref[...]
        _, xh = _rms_fwd(xv, 1.0 / D)
        h2 = (xh * g_ref[...]).astype(BF16)
        h2_ref[...] = h2
        acc = xv
        for k in range(N_CHIPS):
            a = _nt(h2, wg_ref[k])
            b = _nt(h2, wu_ref[k])
            a_ref[k] = a.astype(BF16)
            b_ref[k] = b.astype(BF16)
            hm = (a * jax.nn.sigmoid(a) * b).astype(BF16)
            acc = acc + jnp.dot(hm, wd_ref[k], preferred_element_type=F32)
        if last:
            @pl.when(pl.program_id(0) == 0)
            def _():
                l_ref[...] = jnp.zeros_like(l_ref)

            e = acc - t_ref[...]
            xo_ref[...] = e * (1.0 / D)
            l_ref[...] += jnp.sum(jnp.sum(e * e, axis=-1, keepdims=True), axis=0, keepdims=True) * (0.5 / D)
        else:
            xo_ref[...] = acc

    row = lambda w: pl.BlockSpec((tm, w), lambda i: (i, 0))
    blk = pl.BlockSpec((N_CHIPS, tm, FFB), lambda i: (0, i, 0))
    bsd = jax.ShapeDtypeStruct((N_CHIPS, t, FFB), BF16)
    return pl.pallas_call(
        body, name="ffn_fwd_loss" if last else "ffn_fwd", grid=(t // tm,),
        in_specs=[row(D), _const_spec((1, D))] + _ffn_weight_specs() + ([row(D)] if last else []),
        out_specs=([pl.BlockSpec((8, 128), lambda i: (0, 0))] if last else []) + [row(D), blk, blk, row(D)],
        out_shape=([jax.ShapeDtypeStruct((8, 128), F32)] if last else [])
        + [jax.ShapeDtypeStruct((t, D), F32), bsd, bsd, jax.ShapeDtypeStruct((t, D), BF16)],
        compiler_params=_cparams(("arbitrary" if last else "parallel",)),
    )(*((xm, g2, gf, gf, gf) + ((tgt,) if last else ())))


def _ffn_bwd(dy, xm, g2, a, b, gf, tm):
    t = dy.shape[0]

    def body(dy_ref, x_ref, g_ref, a_ref, b_ref, wg_ref, wu_ref, wd_ref, dx_ref, da_ref, db_ref, hm_ref, dg_ref):
        @pl.when(pl.program_id(0) == 0)
        def _():
            dg_ref[...] = jnp.zeros_like(dg_ref)

        dyv = dy_ref[...]
        dyb = dyv.astype(BF16)
        dh2 = jnp.zeros_like(dyv)
        for k in range(N_CHIPS):
            dhm = _nt(dyb, wd_ref[k])
            av = a_ref[k].astype(F32)
            bv = b_ref[k].astype(F32)
            sig = jax.nn.sigmoid(av)
            sil = av * sig
            hm_ref[k] = (sil * bv).astype(BF16)
            da = (dhm * bv * (sig * (1.0 + av * (1.0 - sig)))).astype(BF16)
            db = (dhm * sil).astype(BF16)
            da_ref[k] = da
            db_ref[k] = db
            dh2 = (dh2 + jnp.dot(da, wg_ref[k], preferred_element_type=F32)
                   + jnp.dot(db, wu_ref[k], preferred_element_type=F32))
        r, xh = _rms_fwd(x_ref[...], 1.0 / D)
        dg_ref[...] += jnp.sum(dh2 * xh, axis=0, keepdims=True)
        dx_ref[...] = dyv + _rms_bwd(dh2, g_ref[...], xh, r, 1.0 / D)

    row = lambda w: pl.BlockSpec((tm, w), lambda i: (i, 0))
    blk = pl.BlockSpec((N_CHIPS, tm, FFB), lambda i: (0, i, 0))
    bsd = jax.ShapeDtypeStruct((N_CHIPS, t, FFB), BF16)
    return pl.pallas_call(
        body, name="ffn_bwd", grid=(t // tm,),
        in_specs=[row(D), row(D), _const_spec((1, D)), blk, blk] + _ffn_weight_specs(),
        out_specs=[row(D), blk, blk, blk, pl.BlockSpec((1, D), lambda i: (0, 0))],
        out_shape=[jax.ShapeDtypeStruct((t, D), F32), bsd, bsd, bsd, jax.ShapeDtypeStruct((1, D), F32)],
        compiler_params=_cparams(("arbitrary",)),
    )(dy, xm, g2, a, b, gf, gf, gf)


def _wgrad_blocks(a, b, tt, name):
    _, t, rows = a.shape
    cols = b.shape[1]
    nsteps = t // tt

    def body(a_ref, b_ref, o_ref, acc_ref):
        s = pl.program_id(0)

        @pl.when(s == 0)
        def _():
            acc_ref[...] = jnp.zeros_like(acc_ref)

        bv = b_ref[...].astype(BF16)
        for k in range(N_CHIPS):
            acc_ref[k] += _tn(a_ref[k], bv)

        @pl.when(s == nsteps - 1)
        def _():
            o_ref[...] = acc_ref[...].astype(BF16)

    return pl.pallas_call(
        body, name=name, grid=(nsteps,),
        in_specs=[pl.BlockSpec((N_CHIPS, tt, rows), lambda s: (0, s, 0)), pl.BlockSpec((tt, cols), lambda s: (s, 0))],
        out_specs=pl.BlockSpec((N_CHIPS, rows, cols), lambda s: (0, 0, 0)),
        out_shape=jax.ShapeDtypeStruct((N_CHIPS, rows, cols), BF16),
        scratch_shapes=[pltpu.VMEM((N_CHIPS, rows, cols), F32)],
        compiler_params=_cparams(("arbitrary",)),
    )(a, b)


def _head_rows(first, n_heads):
    return [(first + g * HD, first + g * HP, HD) for g in range(n_heads)]


def _wgrad(a, b, tt, name):
    t, k = a.shape
    n = b.shape[1]
    nsteps = t // tt

    def body(a_ref, b_ref, o_ref, acc_ref):
        s = pl.program_id(0)

        @pl.when(s == 0)
        def _():
            acc_ref[...] = jnp.zeros_like(acc_ref)

        acc_ref[...] += _tn(a_ref[...].astype(BF16), b_ref[...].astype(BF16))

        @pl.when(s == nsteps - 1)
        def _():
            o_ref[...] = acc_ref[...].astype(BF16)

    return pl.pallas_call(
        body, name=name, grid=(nsteps,),
        in_specs=[pl.BlockSpec((tt, k), lambda s: (s, 0)), pl.BlockSpec((tt, n), lambda s: (s, 0))],
        out_specs=pl.BlockSpec((k, n), lambda s: (0, 0)),
        out_shape=jax.ShapeDtypeStruct((k, n), BF16),
        scratch_shapes=[pltpu.VMEM((k, n), F32)],
        compiler_params=_cparams(("arbitrary",)),
    )(a, b)


def _wgrad_in(dpm, dkv, h, tt):
    t = h.shape[0]
    nsteps = t // tt
    kvw = dkv.shape[1]
    pieces = [(0, 0, O_Q)] + _head_rows(O_Q, NQ + 2 * NKV)

    def body(m_ref, kv_ref, h_ref, o_ref, acc_ref):
        s = pl.program_id(0)

        @pl.when(s == 0)
        def _():
            acc_ref[...] = jnp.zeros_like(acc_ref)

        hv = h_ref[...]
        acc_ref[:NMAIN, :] += _tn(m_ref[...], hv)
        acc_ref[NMAIN:, :] += _tn(kv_ref[...], hv)

        @pl.when(s == nsteps - 1)
        def _():
            for dst, src, size in pieces:
                o_ref[dst:dst + size, :] = acc_ref[src:src + size, :].astype(BF16)

    return pl.pallas_call(
        body, name="wgrad_in", grid=(nsteps,),
        in_specs=[pl.BlockSpec((tt, NMAIN), lambda s: (s, 0)), pl.BlockSpec((tt, kvw), lambda s: (s, 0)),
                  pl.BlockSpec((tt, D), lambda s: (s, 0))],
        out_specs=pl.BlockSpec((W_IN_ROWS, D), lambda s: (0, 0)),
        out_shape=jax.ShapeDtypeStruct((W_IN_ROWS, D), BF16),
        scratch_shapes=[pltpu.VMEM((NMAIN + kvw, D), F32)],
        compiler_params=_cparams(("arbitrary",)),
    )(dpm, dkv, h)


def _mixer_bwd(dxm, proj, ao, cw, gq, gk, sinks, gco, gao, wo, tq):
    t = proj.shape[0]
    nb = tq // BLK
    r8 = tq // 8
    nt = t // tq
    te = tq + 8
    kvw = 2 * NKV * HP

    def body(dx_ref, dxn_ref, p_ref, cgp_ref, hcp_ref, bgn_ref, cgn_ref, hcn_ref, kvp_ref, ao_ref, cw_ref, gq_ref,
             gk_ref, sk_ref, gco_ref, gao_ref, wo_ref,
             dpm_ref, dkvm_ref, dkvh_ref, dcw_ref, dgq_ref, dgk_ref, dsk_ref, dgco_ref, dgao_ref, acc_ref):
        i = pl.program_id(0)

        @pl.when(i == 0)
        def _():
            for r in (dcw_ref, dgq_ref, dgk_ref, dsk_ref, dgco_ref, dgao_ref):
                r[...] = jnp.zeros_like(r)

        acc_ref[...] = jnp.zeros_like(acc_ref)
        live_rows = jnp.where(i < nt - 1, te, tq)
        dxb = dx_ref[...].astype(BF16)
        dxe = jnp.concatenate([dxb, dxn_ref[...].astype(BF16)], axis=0)
        dcn = _nt(dxe, wo_ref[0:CC, :])
        bg = jnp.concatenate([p_ref[:, O_BG:O_BG + CC], bgn_ref[...]], axis=0)
        cg = jnp.concatenate([p_ref[:, O_CG:O_CG + CC], cgn_ref[...]], axis=0)
        hc = jnp.concatenate([p_ref[:, O_HC:O_HC + CC], hcn_ref[...]], axis=0)
        u = cg * hc
        up = jnp.where(i > 0, cgp_ref[...] * hcp_ref[...], 0.0)
        u1, u2 = _conv_taps(jnp.concatenate([up, u], axis=0), te)
        w0, w1, w2 = cw_ref[0:1, :], cw_ref[1:2, :], cw_ref[2:3, :]
        y = w0 * u2 + w1 * u1 + w2 * u
        co = bg * y
        rc, coh = _rms_fwd(co, 1.0 / CC)
        dco = _rms_bwd(dcn, gco_ref[...], coh, rc, 1.0 / CC)
        row_io = lax.broadcasted_iota(jnp.int32, (te, 1), 0)
        own = row_io < tq
        dgco_ref[...] += jnp.sum(jnp.where(own, dcn * coh, 0.0), axis=0, keepdims=True)
        dyc = jnp.where(row_io < live_rows, dco * bg, 0.0)
        dyo = jnp.where(own, dyc, 0.0)
        dcw_ref[0:1, :] += jnp.sum(dyo * u2, axis=0, keepdims=True)
        dcw_ref[1:2, :] += jnp.sum(dyo * u1, axis=0, keepdims=True)
        dcw_ref[2:3, :] += jnp.sum(dyo * u, axis=0, keepdims=True)
        dy1 = pltpu.roll(dyc, te - 1, 0)[0:tq]
        dy2 = pltpu.roll(dyc, te - 2, 0)[0:tq]
        du = w2 * dyc[0:tq] + w1 * dy1 + w0 * dy2
        dpm_ref[:, O_BG:O_BG + CC] = (dco[0:tq] * y[0:tq]).astype(BF16)
        dpm_ref[:, O_CG:O_CG + CC] = (du * hc[0:tq]).astype(BF16)
        dpm_ref[:, O_HC:O_HC + CC] = (du * cg[0:tq]).astype(BF16)
        kraw = jnp.concatenate([kvp_ref[:, 0:NKV * HP], p_ref[:, O_K:O_K + NKV * HP]], axis=0)
        vraw = jnp.concatenate([kvp_ref[:, NKV * HP:], p_ref[:, O_V:O_V + NKV * HP]], axis=0)
        gqv, gkv = gq_ref[...], gk_ref[...]
        keys = _norm_keys(kraw, gkv)
        vb = [vraw[:, h * HP:(h + 1) * HP].astype(BF16) for h in range(NKV)]
        base_valid, c_io = _band_mask()
        lane = lax.broadcasted_iota(jnp.int32, (1, HP), 1)
        dgq, dgk, dsk = (jnp.zeros((1, HP), F32) for _ in range(3))
        dgao = jnp.zeros((1, NQ * HD), F32)
        for b in range(nb):
            lo = jnp.where(i * nb + b == 0, BLK, 0)
            valid = base_valid & (c_io >= lo)
            band = slice(b * BLK, b * BLK + 2 * BLK)
            blk = slice(b * BLK, (b + 1) * BLK)
            ra, aoh = _rms_fwd(ao_ref[blk, :], 1.0 / (NQ * HD))
            danb = _nt(dxb[blk], wo_ref[CC:MIXW, :])
            dgao = dgao + jnp.sum(danb * aoh, axis=0, keepdims=True)
            dao = _rms_bwd(danb, gao_ref[...], aoh, ra, 1.0 / (NQ * HD))
            dos = [dao[:, g // 2 * HP:(g // 2 + 1) * HP] for g in range(NQ)]
            dos = [(d if g % 2 == 0 else pltpu.roll(d, HD, 1)).astype(BF16) for g, d in enumerate(dos)]
            fwd = []
            for g in range(NQ):
                rq, qh = _rms_fwd(p_ref[blk, O_Q + g * HP:O_Q + (g + 1) * HP], 1.0 / HD)
                qs = (qh * (gqv * SCALE)).astype(BF16)
                fwd.append((rq, qh, qs) + _attn_probs(qs, keys[g // GRP][2][band], sk_ref[0, g], valid))
            dqs = []
            for h in range(NKV):
                khat, rk, kn = [a[band] for a in keys[h]]
                dss, prbs, qns, dobs = [], [], [], []
                for g in range(h * GRP, (h + 1) * GRP):
                    rq, qh, qs, pr, ps = fwd[g]
                    dob = dos[g]
                    dp = _nt(dob, vb[h][band])
                    delta = jnp.sum(pr * dp, axis=-1, keepdims=True)
                    dsb = (pr * (dp - delta)).astype(BF16)
                    dsk = dsk + jnp.where(lane == g, -jnp.sum(ps * delta, axis=0, keepdims=True), 0.0)
                    dqn = jnp.dot(dsb, kn, preferred_element_type=F32) * SCALE
                    dgq = dgq + jnp.sum(dqn * qh, axis=0, keepdims=True)
                    dqs.append(_rms_bwd(dqn, gqv, qh, rq, 1.0 / HD).astype(BF16))
                    dss.append(dsb)
                    prbs.append(pr.astype(BF16))
                    qns.append(qs)
                    dobs.append(dob)
                dkn = _tn(jnp.concatenate(dss, axis=0), jnp.concatenate(qns, axis=0))
                dv = _tn(jnp.concatenate(prbs, axis=0), jnp.concatenate(dobs, axis=0))
                dgk = dgk + jnp.sum(dkn * khat, axis=0, keepdims=True)
                acc_ref[band, h * HP:(h + 1) * HP] += _rms_bwd(dkn, gkv, khat, rk, 1.0 / HD)
                acc_ref[band, (NKV + h) * HP:(NKV + h + 1) * HP] += dv
            dpm_ref[blk, O_Q:O_K] = jnp.concatenate(dqs, axis=1)
        dgq_ref[...] += dgq
        dgk_ref[...] += dgk
        dsk_ref[...] += dsk
        dgao_ref[...] += dgao
        dkvh_ref[...] = acc_ref[0:BLK, :]
        dkvm_ref[...] = acc_ref[BLK:, :]

    prev8 = lambda col: pl.BlockSpec((8, CC), lambda i: (jnp.maximum(i * r8 - 1, 0), col))
    next8 = lambda col: pl.BlockSpec((8, CC), lambda i: (jnp.minimum((i + 1) * r8, t // 8 - 1), col))
    small = lambda n: pl.BlockSpec((1, n), lambda i: (0, 0))
    return pl.pallas_call(
        body, name="mixer_bwd", grid=(nt,),
        in_specs=[
            pl.BlockSpec((tq, D), lambda i: (i, 0)),
            pl.BlockSpec((8, D), lambda i: (jnp.minimum((i + 1) * r8, t // 8 - 1), 0)),
            pl.BlockSpec((tq, NP), lambda i: (i, 0)),
            prev8(O_CG // CC), prev8(O_HC // CC),
            next8(O_BG // CC), next8(O_CG // CC), next8(O_HC // CC),
            pl.BlockSpec((BLK, kvw), lambda i: (jnp.maximum(i * nb - 1, 0), O_K // kvw)),
            pl.BlockSpec((tq, NQ * HD), lambda i: (i, 0)),
            _const_spec((8, CC)), _const_spec((1, HP)), _const_spec((1, HP)),
            pl.BlockSpec(memory_space=pltpu.SMEM),
            _const_spec((1, CC)), _const_spec((1, NQ * HD)), _const_spec((MIXW, D)),
        ],
        out_specs=[
            pl.BlockSpec((tq, NMAIN), lambda i: (i, 0)),
            pl.BlockSpec((tq, kvw), lambda i: (i, 0)),
            pl.BlockSpec((BLK, kvw), lambda i: (i, 0)),
            pl.BlockSpec((8, CC), lambda i: (0, 0)), small(HP), small(HP), small(HP), small(CC), small(NQ * HD),
        ],
        out_shape=[
            jax.ShapeDtypeStruct((t, NMAIN), BF16), jax.ShapeDtypeStruct((t, kvw), F32),
            jax.ShapeDtypeStruct((nt * BLK, kvw), F32),
            jax.ShapeDtypeStruct((8, CC), F32), jax.ShapeDtypeStruct((1, HP), F32), jax.ShapeDtypeStruct((1, HP), F32),
            jax.ShapeDtypeStruct((1, HP), F32), jax.ShapeDtypeStruct((1, CC), F32),
            jax.ShapeDtypeStruct((1, NQ * HD), F32),
        ],
        scratch_shapes=[pltpu.VMEM((tq + BLK, kvw), F32)],
        compiler_params=_cparams(("arbitrary",)),
    )(dxm, dxm, proj, proj, proj, proj, proj, proj, proj, ao, cw, gq, gk, sinks, gco, gao, wo)


def _inproj_bwd(dpm, dkvm, dkvh, wpt, x, g1, dxm, tm):
    t = x.shape[0]
    kvw = 2 * NKV * HP
    nt = t // tm

    def body(dp_ref, dk_ref, dh_ref, w_ref, x_ref, g_ref, dxm_ref, dx_ref, dg_ref, dkv_ref):
        i = pl.program_id(0)

        @pl.when(i == 0)
        def _():
            dg_ref[...] = jnp.zeros_like(dg_ref)

        halo = jnp.where(i < nt - 1, dh_ref[...], 0.0)
        dkv_ref[0:tm - BLK, :] = dk_ref[0:tm - BLK, :].astype(BF16)
        dkv_ref[tm - BLK:tm, :] = (dk_ref[tm - BLK:tm, :] + halo).astype(BF16)
        dh = (jnp.dot(dp_ref[...], w_ref[0:NMAIN, :], preferred_element_type=F32)
              + jnp.dot(dkv_ref[...], w_ref[NMAIN:NP, :], preferred_element_type=F32))
        r, xh = _rms_fwd(x_ref[...], 1.0 / D)
        dg_ref[...] += jnp.sum(dh * xh, axis=0, keepdims=True)
        dx_ref[...] = dxm_ref[...] + _rms_bwd(dh, g_ref[...], xh, r, 1.0 / D)

    row = lambda w: pl.BlockSpec((tm, w), lambda i: (i, 0))
    return pl.pallas_call(
        body, name="inproj_bwd", grid=(nt,),
        in_specs=[row(NMAIN), row(kvw), pl.BlockSpec((BLK, kvw), lambda i: (jnp.minimum(i + 1, nt - 1), 0)),
                  _const_spec((NP, D)), row(D), _const_spec((1, D)), row(D)],
        out_specs=[row(D), pl.BlockSpec((1, D), lambda i: (0, 0)), row(kvw)],
        out_shape=[jax.ShapeDtypeStruct((t, D), F32), jax.ShapeDtypeStruct((1, D), F32),
                   jax.ShapeDtypeStruct((t, kvw), BF16)],
        compiler_params=_cparams(("arbitrary",)),
    )(dpm, dkvm, dkvh, wpt, x, g1, dxm)


def _rows_tile(rows, cap=512):
    for cand in range(min(rows, cap) // 16 * 16, 0, -16):
        if rows % cand == 0:
            return cand
    return rows


def _presum_halves(gs, theirs, core):
    n = len(gs)

    def body(c_ref, *refs):
        for g_ref, t_ref, o_ref in zip(refs[:n], refs[n:2 * n], refs[2 * n:]):
            o_ref[...] = (g_ref[...].astype(F32) + t_ref[...].astype(F32)).astype(BF16)

    half = lambda ta: pl.BlockSpec((None,) + ta.shape[1:], lambda k, c_ref: (k, 0, 0))
    own = lambda ta: pl.BlockSpec((None,) + ta.shape[1:], lambda k, c_ref: (k, c_ref[0], 0))
    return pl.pallas_call(
        body, name="presum",
        grid_spec=pltpu.PrefetchScalarGridSpec(
            num_scalar_prefetch=1, grid=(N_CHIPS,),
            in_specs=[own(ta) for ta in theirs] + [half(ta) for ta in theirs],
            out_specs=[half(ta) for ta in theirs]),
        out_shape=[jax.ShapeDtypeStruct(ta.shape, BF16) for ta in theirs],
        compiler_params=_cparams(("parallel",)),
    )(core, *gs, *theirs)


def _sum_chips(got, ps, chip):
    n = len(got)
    steps = 2

    def body(chip_ref, *refs):
        for c_ref, own_ref, o_ref in zip(refs[:n], refs[n:2 * n], refs[2 * n:]):
            acc = None
            for j in range(N_CHIPS):
                term = jnp.where(chip_ref[0] == j, own_ref[...], c_ref[j]).astype(F32)
                acc = term if acc is None else acc + term
            o_ref[...] = acc

    tile = lambda c: (c.shape[1] // steps, c.shape[2])
    return pl.pallas_call(
        body, name="chipsum",
        grid_spec=pltpu.PrefetchScalarGridSpec(
            num_scalar_prefetch=1, grid=(steps,),
            in_specs=[pl.BlockSpec((N_CHIPS,) + tile(c), lambda i, chip_ref: (0, i, 0)) for c in got]
            + [pl.BlockSpec((None,) + tile(c), lambda i, chip_ref: (chip_ref[0], i, 0)) for c in got],
            out_specs=[pl.BlockSpec(tile(c), lambda i, chip_ref: (i, 0)) for c in got]),
        out_shape=[jax.ShapeDtypeStruct(c.shape[1:], F32) for c in got],
        compiler_params=_cparams(("parallel",)),
    )(chip, *got, *ps)


def _adamw(w, g, m, v, name):
    rows, cols = w.shape
    tr = _rows_tile(rows)
    c1 = 1.0 - ADAM_B1 ** ADAM_STEP
    c2 = 1.0 - ADAM_B2 ** ADAM_STEP

    def body(w_ref, g_ref, m_ref, v_ref, d_ref, mo_ref, vo_ref):
        gv = g_ref[...]
        mn = ADAM_B1 * m_ref[...] + (1.0 - ADAM_B1) * gv
        vn = ADAM_B2 * v_ref[...] + (1.0 - ADAM_B2) * (gv * gv)
        mo_ref[...] = mn
        vo_ref[...] = vn
        d_ref[...] = -ADAM_LR * ((mn / c1) / (jnp.sqrt(vn / c2) + ADAM_EPS) + ADAM_WD * w_ref[...])

    spec = pl.BlockSpec((tr, cols), lambda i: (i, 0))
    sds = jax.ShapeDtypeStruct((rows, cols), F32)
    return pl.pallas_call(
        body, name=name, grid=(rows // tr,), in_specs=[spec] * 4, out_specs=[spec] * 3, out_shape=[sds] * 3,
        compiler_params=_cparams(("parallel",)),
    )(w, g, m, v)


def _place():
    x, y, c = lax.axis_index("x"), lax.axis_index("y"), lax.axis_index("c")
    chips = [(1 - x, y), (x, 1 - y), (1 - x, 1 - y)]
    return x, y, c, chips


ANY = pl.BlockSpec(memory_space=pl.ANY)
DMA_ROWS = 64


def _pieces(shape):
    rows = shape[-2]
    step = DMA_ROWS if rows % DMA_ROWS == 0 else rows
    lead = [()]
    for n in shape[:-2]:
        lead = [i + (k,) for i in lead for k in range(n)]
    return [i + (pl.ds(r0, step),) for i in lead for r0 in range(0, rows, step)]


def _start_pieces(make, src, dst):
    for idx in _pieces(src.shape):
        make(src.at[idx], dst.at[idx]).start()


def _gather_body(srcs, outs, sems, layer, start):
    nw = len(srcs)
    ssem, rsem, fssem, frsem = sems
    x, y, c, chips = _place()
    kme = 2 * x + y

    def plane(j, w, to):
        return lambda s, d: pltpu.make_async_remote_copy(
            src_ref=s, dst_ref=d, send_sem=ssem.at[j, w], recv_sem=rsem.at[j, w], device_id=to,
            device_id_type=MESH)

    def passed(j, w):
        return lambda s, d: pltpu.make_async_remote_copy(
            src_ref=s, dst_ref=d, send_sem=fssem.at[j, w], recv_sem=frsem.at[j, w],
            device_id=(x, y, 1 - c), device_id_type=MESH)

    @pl.when(c == layer)
    def _():
        for j, (px, py) in enumerate(chips):
            for w in range(nw):
                start(plane(j, w, (px, py, c)), srcs[w], outs[w].at[kme])
        for j, (px, py) in enumerate(chips):
            for w in range(nw):
                got = outs[w].at[2 * px + py]
                plane(j, w, (px, py, c))(got, got).wait_recv()
                start(passed(j, w), got, got)
        for j, (px, py) in enumerate(chips):
            for w in range(nw):
                got = outs[w].at[2 * px + py]
                plane(j, w, (px, py, c))(got, got).wait_send()
                passed(j, w)(got, got).wait_send()

    @pl.when(c != layer)
    def _():
        for j, (px, py) in enumerate(chips):
            for w in range(nw):
                got = outs[w].at[2 * px + py]
                passed(j, w)(got, got).wait_recv()


def _handshake(peers):
    barrier = pltpu.get_barrier_semaphore()
    for peer in peers:
        pl.semaphore_signal(barrier, inc=1, device_id=peer, device_id_type=MESH)
    pl.semaphore_wait(barrier, len(peers))


def _handshake_all():
    x, y, c, _ = _place()
    _handshake([(x ^ (r >> 2), y ^ ((r >> 1) & 1), c ^ (r & 1)) for r in range(1, 8)])


def _gather_layer_async(blocks, layer, name, collective_id):
    hbm = pltpu.MemorySpace.HBM
    srcs = [jax.new_ref(b, memory_space=hbm) for b in blocks]
    outs = [jax.empty_ref(jax.ShapeDtypeStruct((N_CHIPS,) + b.shape, b.dtype), memory_space=hbm) for b in blocks]

    @pl.kernel(mesh=plsc.ScalarSubcoreMesh(axis_name="seq", num_cores=1), name=name,
               scratch_types=[pltpu.SemaphoreType.DMA((3, len(blocks)))] * 4,
               compiler_params=pltpu.CompilerParams(collective_id=collective_id))
    def launch(*sems):
        _handshake_all()
        _gather_body(srcs, outs, sems, layer, lambda make, s, d: make(s, d).start())

    launch()
    return [o[...] for o in outs]


def _swap_siblings(arrs, halves, name, collective_id=None):
    nw = len(arrs)
    out_sds = [jax.ShapeDtypeStruct((a.shape[0], a.shape[1] // 2, a.shape[2]) if halves else a.shape, a.dtype)
               for a in arrs]

    def exchange(srcs, outs, ssem, rsem, start):
        x, y, c, _ = _place()

        def give(w):
            return lambda s, d: pltpu.make_async_remote_copy(
                src_ref=s, dst_ref=d, send_sem=ssem.at[w], recv_sem=rsem.at[w], device_id=(x, y, 1 - c),
                device_id_type=MESH)

        for w in range(nw):
            hr = outs[w].shape[1]
            start(give(w), srcs[w].at[:, pl.ds((1 - c) * hr, hr)] if halves else srcs[w], outs[w])
        for w in range(nw):
            give(w)(outs[w], outs[w]).wait()

    if collective_id is None:
        def body(*refs):
            exchange(refs[:nw], refs[nw:2 * nw], *refs[2 * nw:], _start_pieces)

        return pl.pallas_call(
            body, name=name, in_specs=[ANY] * nw, out_specs=[ANY] * nw, out_shape=out_sds,
            scratch_shapes=[pltpu.SemaphoreType.DMA((nw,))] * 2,
            compiler_params=_cparams(has_side_effects=True),
        )(*arrs)

    hbm = pltpu.MemorySpace.HBM
    srcs = [jax.new_ref(a, memory_space=hbm) for a in arrs]
    outs = [jax.empty_ref(sds, memory_space=hbm) for sds in out_sds]

    @pl.kernel(mesh=plsc.ScalarSubcoreMesh(axis_name="seq", num_cores=1), name=name,
               scratch_types=[pltpu.SemaphoreType.DMA((nw,))] * 2,
               compiler_params=pltpu.CompilerParams(collective_id=collective_id))
    def launch(ssem, rsem):
        x, y, c, _ = _place()
        _handshake([(x, y, 1 - c)])
        exchange(srcs, outs, ssem, rsem, lambda make, s, d: make(s, d).start())

    launch()
    return [o[...] for o in outs]


def _scatter_body(srcs, outs, sems, start):
    nw = len(srcs)
    ssem, rsem = sems
    x, y, c, chips = _place()
    kme = 2 * x + y

    def give(j, w, to):
        return lambda s, d: pltpu.make_async_remote_copy(
            src_ref=s, dst_ref=d, send_sem=ssem.at[j, w], recv_sem=rsem.at[j, w], device_id=to,
            device_id_type=MESH)

    for j, (px, py) in enumerate(chips):
        for w in range(nw):
            start(give(j, w, (px, py, c)), srcs[w].at[2 * px + py], outs[w].at[kme])
    for j, (px, py) in enumerate(chips):
        for w in range(nw):
            got = outs[w].at[2 * px + py]
            give(j, w, (px, py, c))(got, got).wait_recv()
    for j, (px, py) in enumerate(chips):
        for w in range(nw):
            sent = srcs[w].at[2 * px + py]
            give(j, w, (px, py, c))(sent, sent).wait_send()


def _scatter_chips_async(ps, name, collective_id):
    hbm = pltpu.MemorySpace.HBM
    srcs = [jax.new_ref(p, memory_space=hbm) for p in ps]
    outs = [jax.empty_ref(jax.ShapeDtypeStruct(p.shape, p.dtype), memory_space=hbm) for p in ps]

    @pl.kernel(mesh=plsc.ScalarSubcoreMesh(axis_name="seq", num_cores=1), name=name,
               scratch_types=[pltpu.SemaphoreType.DMA((3, len(ps)))] * 2,
               compiler_params=pltpu.CompilerParams(collective_id=collective_id))
    def launch(*sems):
        _handshake_all()
        _scatter_body(srcs, outs, sems, lambda make, s, d: make(s, d).start())

    launch()
    return [o[...] for o in outs]


def _allreduce_small(v):
    rows = v.shape[0]

    def body(v_ref, o_ref, buf, ssem, rsem):
        x, y, c, _ = _place()
        me = 4 * x + 2 * y + c
        buf[me] = v_ref[...]
        sends = []
        for r in range(1, 8):
            peer = (x ^ (r >> 2), y ^ ((r >> 1) & 1), c ^ (r & 1))
            cp = pltpu.make_async_remote_copy(
                src_ref=v_ref, dst_ref=buf.at[me], send_sem=ssem.at[r - 1], recv_sem=rsem.at[r - 1],
                device_id=peer, device_id_type=MESH)
            cp.start()
            sends.append(cp)
        for r in range(1, 8):
            src = me ^ r
            pltpu.make_async_remote_copy(
                src_ref=v_ref, dst_ref=buf.at[src], send_sem=ssem.at[r - 1], recv_sem=rsem.at[r - 1],
                device_id=(x, y, c), device_id_type=MESH).wait_recv()
        for cp in sends:
            cp.wait_send()
        acc = buf[0]
        for d in range(1, 8):
            acc = acc + buf[d]
        o_ref[...] = acc

    vm = pl.BlockSpec(memory_space=pltpu.VMEM)
    return pl.pallas_call(
        body, name="allreduce_small", in_specs=[vm], out_specs=vm,
        out_shape=jax.ShapeDtypeStruct(v.shape, F32),
        scratch_shapes=[pltpu.VMEM((8, rows, 128), F32), pltpu.SemaphoreType.DMA((7,)),
                        pltpu.SemaphoreType.DMA((7,))],
        compiler_params=_cparams(has_side_effects=True),
    )(v)


def _t(w):
    return jnp.swapaxes(w, -1, -2)


def _count(shape):
    n = 1
    for s in shape:
        n *= s
    return n


def _pack_rows(arrs):
    flat = [jnp.pad(a.reshape(-1), (0, (-_count(a.shape)) % 128)) for a in arrs]
    v = jnp.concatenate(flat)
    rows = -(-v.shape[0] // (8 * 128)) * 8
    return jnp.pad(v, (0, rows * 128 - v.shape[0])).reshape(rows, 128)


def kernel(x, norm1_g, w_in, conv_w, q_norm_g, k_norm_g, sinks, conv_out_g, attn_out_g, w_o, norm2_g, w_gate, w_up, w_down, loss_target, m_norm1_g, m_w_in, m_conv_w, m_q_norm_g, m_k_norm_g, m_sinks, m_conv_out_g, m_attn_out_g, m_w_o, m_norm2_g, m_w_gate, m_w_up, m_w_down, v_norm1_g, v_w_in, v_conv_w, v_q_norm_g, v_k_norm_g, v_sinks, v_conv_out_g, v_attn_out_g, v_w_o, v_norm2_g, v_w_gate, v_w_up, v_w_down):
    depth = w_in.shape[0]
    t = x.shape[1]
    xs = x.reshape(t, D)
    tgt = loss_target.reshape(t, D)
    xi, yi = lax.axis_index("x"), lax.axis_index("y")
    kme = 2 * xi + yi
    tm = min(512, t)
    tq = min(512, t)
    tf = min(256, t)
    tw = min(1024, t)

    cwp = jnp.pad(conv_w.reshape(depth * 3, CC // N_CHIPS), ((0, 8 - depth * 3), (0, 0)))
    own_f = [jnp.concatenate([_t(w_gate[l]), _t(w_up[l]), w_down[l]], axis=0).astype(BF16) for l in range(depth)]
    own_o = [w_o[l].astype(BF16) for l in range(depth)]
    own_i = [_t(w_in[l]).astype(BF16) for l in range(depth)]
    mine = lambda got, own: lax.dynamic_update_index_in_dim(got, own, kme, 0)
    (got_i0,) = _gather_layer_async([own_i[0]], 0, "gather_in0_seq", collective_id=14)
    got_ocw = _gather_layer_async([own_o[0], cwp], 0, "gather_o0_seq", collective_id=15)
    got_i0, own_f, own_o, own_i = lax.optimization_barrier((got_i0, own_f, own_o, own_i))
    gf0_in = lax.optimization_barrier((own_f[0], got_i0))[0]
    (got_f0,) = _gather_layer_async([gf0_in], 0, "gather_ffn0_seq", collective_id=6)

    chip = kme.reshape(1).astype(jnp.int32)

    def layer_params(l, got_o, cw_full):
        return dict(
            wo=mine(got_o, own_o[l]).reshape(MIXW, D),
            cw=jnp.pad(cw_full[l], ((0, 5), (0, 0))),
            g1=norm1_g[l].reshape(1, D), g2=norm2_g[l].reshape(1, D),
            gq=jnp.pad(q_norm_g[l], (0, HP - HD)).reshape(1, HP), gk=jnp.pad(k_norm_g[l], (0, HP - HD)).reshape(1, HP),
            sk=sinks[l].reshape(1, NQ), gco=conv_out_g[l].reshape(1, CC),
            gao=attn_out_g[l].reshape(1, NQ * HD))

    saved, layers = [], []
    cur = xs
    for l in range(depth):
        x_in = cur
        if l == 0:
            got_i = got_i0
        else:
            got_f1, got_o, got_i = lax.optimization_barrier((got_l1, cur))[0]
        proj, h, wpt = _inproj_fwd(cur, norm1_g[l].reshape(1, D), got_i, own_i[l], chip, tm)
        if l == 0:
            got_o, got_cw = lax.optimization_barrier((got_ocw, proj))[0]
            cw_full = mine(got_cw, cwp).transpose(1, 0, 2).reshape(8, CC)[:depth * 3].reshape(depth, 3, CC)
        p = layer_params(l, got_o, cw_full)
        p["wpt"] = wpt
        xm, mix, ao = _mixer_fwd(proj, cur, p["cw"], p["gq"], p["gk"], p["sk"], p["gco"], p["gao"], p["wo"], tq)
        if l == 0:
            got_f0 = lax.optimization_barrier((got_f0, xm))[0]
            l1_in = lax.optimization_barrier(([own_f[1], own_o[1], own_i[1]], got_f0))[0]
            got_l1 = _gather_layer_async(l1_in, 1, "gather_layer1_seq", collective_id=1)
        p["gf"] = mine(got_f0 if l == 0 else got_f1, own_f[l])
        layers.append(p)
        if l < depth - 1:
            cur, a, b, h2 = _ffn_fwd(xm, p["g2"], p["gf"], tm)
        else:
            lpart, dy, a, b, h2 = _ffn_fwd(xm, p["g2"], p["gf"], tm, tgt)
        saved.append(dict(x=x_in, proj=proj, h=h, xm=xm, mix=mix, ao=ao, a=a, b=b, h2=h2))

    ci = lax.axis_index("c")
    core = ci.reshape(1).astype(jnp.int32)
    rbig = [dict() for _ in range(depth)]
    gsmall = [None] * depth

    def after_(vals, after):
        return vals if after is None else lax.optimization_barrier((vals, after))[0]

    def reduce_1(gs, tag, ids):
        return gs, _swap_siblings(gs, True, f"swap_halves_{tag}_seq", ids[0]), tag, ids

    def reduce_2(state, after):
        gs, theirs, tag, ids = state
        ps = _presum_halves(gs, after_(theirs, after), core)
        return ps, _scatter_chips_async(ps, f"scatter_{tag}_seq", ids[1]), tag, ids

    def reduce_3(state, after):
        ps, got, tag, ids = state
        r_mine = _sum_chips(after_(got, after), ps, chip)
        return r_mine, _swap_siblings(r_mine, False, f"swap_reduced_{tag}" + ("_seq" if ids[2] else ""), ids[2])

    def reduce_4(state, after):
        r_mine, r_theirs = state
        return [jnp.where(ci == 0, jnp.concatenate([a, b], axis=0), jnp.concatenate([b, a], axis=0))
                for a, b in zip(r_mine, after_(r_theirs, after))]

    ids = {"ffn1": (7, 4, 8), "in1": (9, 5, 10), "ffn0": (11, 2, 12), "in0": (13, 3, None)}
    in_2 = None
    handed = {}
    for l in reversed(range(depth)):
        p, s = layers[l], saved[l]
        dxm, da, db, hm, dg2 = _ffn_bwd(dy, s["xm"], p["g2"], s["a"], s["b"], p["gf"], tf)
        if in_2 is not None:
            in_2 = reduce_2(in_2, dxm)
        g_wg = _wgrad_blocks(da, s["h2"], tw, "wgrad_gate")
        g_wu = _wgrad_blocks(db, s["h2"], tw, "wgrad_up")
        g_wd = _wgrad_blocks(hm, dy, tw, "wgrad_down")
        if in_2 is not None:
            handed[f"in{l + 1}"] = reduce_3(in_2, g_wd)
        ffn_1 = reduce_1([g_wg, g_wu, g_wd], f"ffn{l}", ids[f"ffn{l}"])
        dpm, dkvm, dkvh, dcw, dgq, dgk, dsk, dgco, dgao = _mixer_bwd(
            dxm, s["proj"], s["ao"], p["cw"], p["gq"], p["gk"], p["sk"], p["gco"], p["gao"], p["wo"], tq)
        ffn_2 = reduce_2(ffn_1, dpm)
        g_o = _wgrad(s["mix"], dxm, tw, "wgrad_o")
        dx, dg1, dkv = _inproj_bwd(dpm, dkvm, dkvh, p["wpt"], s["x"], p["g1"], dxm, tq)
        g_in = _wgrad_in(dpm, dkv, s["h"], tw)
        dy = dx
        gsmall[l] = dict(g1=dg1, cw=dcw[:3], gq=dgq[0, :HD], gk=dgk[0, :HD], sk=dsk[0, :NQ], gco=dgco,
                         gao=dgao, g2=dg2)
        handed[f"ffn{l}"] = reduce_3(ffn_2, g_in)
        in_2 = reduce_1([g_in.reshape(N_CHIPS, -1, D), g_o.reshape(N_CHIPS, -1, D)], f"in{l}", ids[f"in{l}"])
    grad_x = dy.reshape(x.shape)

    small_shapes = dict(g1=(D,), cw=(3, CC), gq=(HD,), gk=(HD,), sk=(NQ,), gco=(CC,), gao=(NQ * HD,), g2=(D,))
    red = _allreduce_small(_pack_rows([gsmall[l][n] for l in range(depth) for n in small_shapes]
                                      + [lpart[0:1, 0:1]])).reshape(-1)
    red_small, offs = {n: [] for n in small_shapes}, 0
    for l in range(depth):
        for n, shp in small_shapes.items():
            cnt = _count(shp)
            red_small[n].append(red[offs:offs + cnt].reshape(shp))
            offs += -(-cnt // 128) * 128
    loss = red[offs]
    g_small = {n: jnp.stack(v) for n, v in red_small.items()}
    g_cw = lax.dynamic_slice_in_dim(g_small["cw"], kme * (CC // N_CHIPS), CC // N_CHIPS, axis=2)

    weights = [norm1_g, w_in, conv_w, q_norm_g, k_norm_g, sinks, conv_out_g, attn_out_g, w_o, norm2_g, w_gate,
               w_up, w_down]
    moms = [m_norm1_g, m_w_in, m_conv_w, m_q_norm_g, m_k_norm_g, m_sinks, m_conv_out_g, m_attn_out_g, m_w_o,
            m_norm2_g, m_w_gate, m_w_up, m_w_down]
    vars_ = [v_norm1_g, v_w_in, v_conv_w, v_q_norm_g, v_k_norm_g, v_sinks, v_conv_out_g, v_attn_out_g, v_w_o,
             v_norm2_g, v_w_gate, v_w_up, v_w_down]
    n_w = len(weights)
    big_idx = dict(zip(("in", "o", "g", "u", "d"), (1, 8, 10, 11, 12)))
    small_idx = [n for n in range(n_w) if n not in big_idx.values()]
    grads, deltas, new_m, new_v = [None] * n_w, [None] * n_w, [None] * n_w, [None] * n_w
    for n, g in zip(small_idx, (g_small["g1"], g_cw, g_small["gq"], g_small["gk"], g_small["sk"], g_small["gco"],
                                g_small["gao"], g_small["g2"])):
        grads[n] = g

    def update_big(name):
        n = big_idx[name]
        g = jnp.stack([rbig[l][name] for l in range(depth)])
        flip = g.shape != weights[n].shape
        rows2d = lambda a3: (_t(a3) if flip else a3).reshape(-1, D)
        res = _adamw(rows2d(weights[n]), g.reshape(-1, D), rows2d(moms[n]), rows2d(vars_[n]), f"adamw_{n}")
        res = [g] + [r.reshape(g.shape) for r in res]
        grads[n], deltas[n], new_m[n], new_v[n] = [_t(r) for r in res] if flip else res

    for l in range(depth):
        rbig[l]["g"], rbig[l]["u"], rbig[l]["d"] = reduce_4(handed[f"ffn{l}"], red)
    rbig[1]["in"], rbig[1]["o"] = reduce_4(handed["in1"], red)
    update_big("g")
    in_2 = reduce_2(in_2, new_v[big_idx["g"]])
    update_big("u")
    update_big("d")
    rbig[0]["in"], rbig[0]["o"] = reduce_4(reduce_3(in_2, new_v[big_idx["d"]]), None)
    for name in ("in", "o"):
        update_big(name)
    res = _adamw(*[_pack_rows([arrs[n] for n in small_idx]) for arrs in (weights, grads, moms, vars_)],
                 "adamw_small")
    offs = 0
    for n in small_idx:
        shp = weights[n].shape
        cnt = _count(shp)
        deltas[n], new_m[n], new_v[n] = [r.reshape(-1)[offs:offs + cnt].reshape(shp) for r in res]
        offs += -(-cnt // 128) * 128
    return (loss, grad_x, *grads, *deltas, *new_m, *new_v)
```

```python
import jax
import jax.numpy as jnp
from jax import lax
from jax.experimental import pallas as pl
from jax.experimental.pallas import tpu as pltpu
from jax.experimental.pallas import tpu_sc as plsc

F32 = jnp.float32
BF16 = jnp.bfloat16

D = 1024
CC = 512
NQ = 8
NKV = 2
HD = 64
HP = 128
GRP = NQ // NKV
FF = 2816
FFB = FF // 4
BLK = 128
EPS = 1e-6
NEG = -1e30
SCALE = HD ** -0.5
O_BG, O_CG, O_HC, O_Q = 0, CC, 2 * CC, 3 * CC
O_K = O_Q + NQ * HP
O_V = O_K + NKV * HP
NP = O_V + NKV * HP
NMAIN = O_K
MIXW = CC + NQ * HD
N_CHIPS = 4
VMEM_LIMIT = 56 * 1024 * 1024
MESH = pl.DeviceIdType.MESH

ADAM_LR, ADAM_B1, ADAM_B2, ADAM_EPS, ADAM_WD, ADAM_STEP = 0.001, 0.9, 0.999, 1e-08, 0.01, 10


def _cparams(sem=None, **kw):
    if sem is not None:
        kw["dimension_semantics"] = sem
    return pltpu.CompilerParams(vmem_limit_bytes=VMEM_LIMIT, **kw)


def _const_spec(shape):
    nd = len(shape)
    return pl.BlockSpec(shape, lambda *_: (0,) * nd, pipeline_mode=pl.Buffered(1))


def _nt(a, b):
    return lax.dot_general(a, b, (((1,), (1,)), ((), ())), preferred_element_type=F32)


def _tn(a, b):
    return lax.dot_general(a, b, (((0,), (0,)), ((), ())), preferred_element_type=F32)


def _rms_fwd(x, inv_n):
    r = lax.rsqrt(jnp.sum(x * x, axis=-1, keepdims=True) * inv_n + EPS)
    return r, x * r


def _rms_bwd(dy, g, xh, r, inv_n):
    dxh = dy * g
    return r * (dxh - xh * (jnp.sum(dxh * xh, axis=-1, keepdims=True) * inv_n))


W_IN_ROWS = 3 * CC + (NQ + 2 * NKV) * HD
W_IN_BLOCK = W_IN_ROWS // N_CHIPS


def _padded_row(row):
    return row + max(row - O_Q, 0) // HD * (HP - HD)


def _w_in_pieces(k):
    first = k * W_IN_BLOCK
    plain = min(max(O_Q - first, 0), W_IN_BLOCK)
    pieces = [(0, first, plain)] if plain else []
    return pieces + [(r, _padded_row(first + r), HD) for r in range(plain, W_IN_BLOCK, HD)]


def _inproj_fwd(x, g1, gi, own_i, chip, tm):
    t = x.shape[0]

    def body(chip_ref, x_ref, g_ref, gi_ref, own_ref, p_ref, h_ref, w_ref, sem):
        @pl.when(pl.program_id(0) == 0)
        def _():
            for k in range(N_CHIPS):
                for src, dst, rows in _w_in_pieces(k):
                    @pl.when(chip_ref[0] == k)
                    def _():
                        pltpu.make_async_copy(own_ref.at[pl.ds(src, rows)], w_ref.at[pl.ds(dst, rows)], sem).start()

                    @pl.when(chip_ref[0] != k)
                    def _():
                        pltpu.make_async_copy(gi_ref.at[k, pl.ds(src, rows)], w_ref.at[pl.ds(dst, rows)], sem).start()
            for slot in range(NQ + 2 * NKV):
                w_ref[O_Q + slot * HP + HD:O_Q + (slot + 1) * HP, :] = jnp.zeros((HP - HD, D), BF16)
            landed = w_ref.at[pl.ds(0, W_IN_ROWS)]
            pltpu.make_async_copy(landed, landed, sem).wait()

        _, xh = _rms_fwd(x_ref[...], 1.0 / D)
        h = (xh * g_ref[...]).astype(BF16)
        h_ref[...] = h
        p_ref[...] = _nt(h, w_ref[...])

    const = lambda shape: pl.BlockSpec(shape, lambda i, c: (0,) * len(shape))
    return pl.pallas_call(
        body, name="inproj_fwd",
        grid_spec=pltpu.PrefetchScalarGridSpec(
            num_scalar_prefetch=1, grid=(t // tm,),
            in_specs=[pl.BlockSpec((tm, D), lambda i, c: (i, 0)), const((1, D)), ANY, ANY],
            out_specs=[pl.BlockSpec((tm, NP), lambda i, c: (i, 0)), pl.BlockSpec((tm, D), lambda i, c: (i, 0)),
                       const((NP, D))],
            scratch_shapes=[pltpu.SemaphoreType.DMA(())]),
        out_shape=[jax.ShapeDtypeStruct((t, NP), F32), jax.ShapeDtypeStruct((t, D), BF16),
                   jax.ShapeDtypeStruct((NP, D), BF16)],
        compiler_params=_cparams(("arbitrary",)),
    )(chip, x, g1, gi, own_i)


def _band_mask():
    r_io = lax.broadcasted_iota(jnp.int32, (BLK, 2 * BLK), 0)
    c_io = lax.broadcasted_iota(jnp.int32, (BLK, 2 * BLK), 1)
    return (c_io > r_io) & (c_io <= r_io + BLK), c_io


def _conv_taps(uf, n):
    u1 = pltpu.roll(uf, 1, 0)[8:8 + n]
    u2 = pltpu.roll(uf, 2, 0)[8:8 + n]
    return u1, u2


def _attn_probs(qs, kband, sink, valid):
    s = jnp.where(valid, _nt(qs, kband), NEG)
    m = jnp.maximum(jnp.max(s, axis=-1, keepdims=True), sink)
    p = jnp.exp(s - m)
    es = jnp.exp(sink - m)
    inv = 1.0 / (jnp.sum(p, axis=-1, keepdims=True) + es)
    return p * inv, es * inv


def _norm_keys(kraw, gk):
    out = []
    for h in range(NKV):
        kh = kraw[:, h * HP:(h + 1) * HP]
        rk, khat = _rms_fwd(kh, 1.0 / HD)
        out.append((khat, rk, (khat * gk).astype(BF16)))
    return out


def _mixer_fwd(proj, x, cw, gq, gk, sinks, gco, gao, wo, tq):
    t = proj.shape[0]
    nb = tq // BLK
    r8 = tq // 8

    def body(p_ref, cgp_ref, hcp_ref, kvp_ref, x_ref, cw_ref, gq_ref, gk_ref, sk_ref, gco_ref, gao_ref,
             wo_ref, xm_ref, mix_ref, ao_ref, aop_ref):
        i = pl.program_id(0)
        cg = p_ref[:, O_CG:O_CG + CC]
        hc = p_ref[:, O_HC:O_HC + CC]
        u = cg * hc
        up = jnp.where(i > 0, cgp_ref[...] * hcp_ref[...], 0.0)
        u1, u2 = _conv_taps(jnp.concatenate([up, u], axis=0), tq)
        y = cw_ref[0:1, :] * u2 + cw_ref[1:2, :] * u1 + cw_ref[2:3, :] * u
        co = p_ref[:, O_BG:O_BG + CC] * y
        _, coh = _rms_fwd(co, 1.0 / CC)
        cn = coh * gco_ref[...]
        kraw = jnp.concatenate([kvp_ref[:, 0:NKV * HP], p_ref[:, O_K:O_K + NKV * HP]], axis=0)
        vraw = jnp.concatenate([kvp_ref[:, NKV * HP:], p_ref[:, O_V:O_V + NKV * HP]], axis=0)
        keys = _norm_keys(kraw, gk_ref[...])
        vb = [vraw[:, h * HP:(h + 1) * HP].astype(BF16) for h in range(NKV)]
        base_valid, c_io = _band_mask()
        gqs = gq_ref[...] * SCALE
        for b in range(nb):
            lo = jnp.where(i * nb + b == 0, BLK, 0)
            valid = base_valid & (c_io >= lo)
            for g in range(NQ):
                h = g // GRP
                qg = p_ref[b * BLK:(b + 1) * BLK, O_Q + g * HP:O_Q + (g + 1) * HP]
                _, qh = _rms_fwd(qg, 1.0 / HD)
                qs = (qh * gqs).astype(BF16)
                pr, _ = _attn_probs(qs, keys[h][2][b * BLK:b * BLK + 2 * BLK], sk_ref[0, g], valid)
                aop_ref[b * BLK:(b + 1) * BLK, g * HP:(g + 1) * HP] = jnp.dot(
                    pr.astype(BF16), vb[h][b * BLK:b * BLK + 2 * BLK], preferred_element_type=F32)
        for j in range(NQ // 2):
            ao_ref[:, j * HP:(j + 1) * HP] = (aop_ref[:, 2 * j * HP:(2 * j + 1) * HP]
                                              + pltpu.roll(aop_ref[:, (2 * j + 1) * HP:(2 * j + 2) * HP], HD, 1))
        _, aoh = _rms_fwd(ao_ref[...], 1.0 / (NQ * HD))
        an = aoh * gao_ref[...]
        mix = jnp.concatenate([cn, an], axis=1).astype(BF16)
        mix_ref[...] = mix
        xm_ref[...] = x_ref[...] + jnp.dot(mix, wo_ref[...], preferred_element_type=F32)

    prev8 = lambda col: pl.BlockSpec((8, CC), lambda i: (jnp.maximum(i * r8 - 1, 0), col))
    return pl.pallas_call(
        body, name="mixer_fwd", grid=(t // tq,),
        in_specs=[
            pl.BlockSpec((tq, NP), lambda i: (i, 0)),
            prev8(O_CG // CC), prev8(O_HC // CC),
            pl.BlockSpec((BLK, 2 * NKV * HP), lambda i: (jnp.maximum(i * nb - 1, 0), O_K // (2 * NKV * HP))),
            pl.BlockSpec((tq, D), lambda i: (i, 0)),
            _const_spec((8, CC)), _const_spec((1, HP)), _const_spec((1, HP)),
            pl.BlockSpec(memory_space=pltpu.SMEM),
            _const_spec((1, CC)), _const_spec((1, NQ * HD)), _const_spec((MIXW, D)),
        ],
        out_specs=[pl.BlockSpec((tq, D), lambda i: (i, 0)), pl.BlockSpec((tq, MIXW), lambda i: (i, 0)),
                   pl.BlockSpec((tq, NQ * HD), lambda i: (i, 0))],
        out_shape=[jax.ShapeDtypeStruct((t, D), F32), jax.ShapeDtypeStruct((t, MIXW), BF16),
                   jax.ShapeDtypeStruct((t, NQ * HD), F32)],
        scratch_shapes=[pltpu.VMEM((tq, NQ * HP), F32)],
        compiler_params=_cparams(("parallel",)),
    )(proj, proj, proj, proj, x, cw, gq, gk, sinks, gco, gao, wo)


def _ffn_weight_specs():
    return [pl.BlockSpec((N_CHIPS, FFB, D), lambda i, j=j: (0, j, 0), pipeline_mode=pl.Buffered(1))
            for j in range(3)]


def _ffn_fwd(xm, g2, gf, tm, tgt=None):
    t = xm.shape[0]
    last = tgt is not None

    def body(x_ref, g_ref, wg_ref, wu_ref, wd_ref, *rest):
        t_ref, rest = (rest[0], rest[1:]) if last else (None, rest)
        l_ref, rest = (rest[0], rest[1:]) if last else (None, rest)
        xo_ref, a_ref, b_ref, h2_ref = rest
        xv = x_ref[...]
        _, xh = _rms_fwd(xv, 1.0 / D)
        h2 = (xh * g_ref[...]).astype(BF16)
        h2_ref[...] = h2
        acc = xv
        for k in range(N_CHIPS):
            a = _nt(h2, wg_ref[k])
            b = _nt(h2, wu_ref[k])
            a_ref[k] = a.astype(BF16)
            b_ref[k] = b.astype(BF16)
            hm = (a * jax.nn.sigmoid(a) * b).astype(BF16)
            acc = acc + jnp.dot(hm, wd_ref[k], preferred_element_type=F32)
        if last:
            @pl.when(pl.program_id(0) == 0)
            def _():
                l_ref[...] = jnp.zeros_like(l_ref)

            e = acc - t_ref[...]
            xo_ref[...] = e * (1.0 / D)
            l_ref[...] += jnp.sum(jnp.sum(e * e, axis=-1, keepdims=True), axis=0, keepdims=True) * (0.5 / D)
        else:
            xo_ref[...] = acc

    row = lambda w: pl.BlockSpec((tm, w), lambda i: (i, 0))
    blk = pl.BlockSpec((N_CHIPS, tm, FFB), lambda i: (0, i, 0))
    bsd = jax.ShapeDtypeStruct((N_CHIPS, t, FFB), BF16)
    return pl.pallas_call(
        body, name="ffn_fwd_loss" if last else "ffn_fwd", grid=(t // tm,),
        in_specs=[row(D), _const_spec((1, D))] + _ffn_weight_specs() + ([row(D)] if last else []),
        out_specs=([pl.BlockSpec((8, 128), lambda i: (0, 0))] if last else []) + [row(D), blk, blk, row(D)],
        out_shape=([jax.ShapeDtypeStruct((8, 128), F32)] if last else [])
        + [jax.ShapeDtypeStruct((t, D), F32), bsd, bsd, jax.ShapeDtypeStruct((t, D), BF16)],
        compiler_params=_cparams(("arbitrary" if last else "parallel",)),
    )(*((xm, g2, gf, gf, gf) + ((tgt,) if last else ())))


def _ffn_bwd(dy, xm, g2, a, b, gf, tm):
    t = dy.shape[0]

    def body(dy_ref, x_ref, g_ref, a_ref, b_ref, wg_ref, wu_ref, wd_ref, dx_ref, da_ref, db_ref, hm_ref, dg_ref):
        @pl.when(pl.program_id(0) == 0)
        def _():
            dg_ref[...] = jnp.zeros_like(dg_ref)

        dyv = dy_ref[...]
        dyb = dyv.astype(BF16)
        dh2 = jnp.zeros_like(dyv)
        for k in range(N_CHIPS):
            dhm = _nt(dyb, wd_ref[k])
            av = a_ref[k].astype(F32)
            bv = b_ref[k].astype(F32)
            sig = jax.nn.sigmoid(av)
            sil = av * sig
            hm_ref[k] = (sil * bv).astype(BF16)
            da = (dhm * bv * (sig * (1.0 + av * (1.0 - sig)))).astype(BF16)
            db = (dhm * sil).astype(BF16)
            da_ref[k] = da
            db_ref[k] = db
            dh2 = (dh2 + jnp.dot(da, wg_ref[k], preferred_element_type=F32)
                   + jnp.dot(db, wu_ref[k], preferred_element_type=F32))
        r, xh = _rms_fwd(x_ref[...], 1.0 / D)
        dg_ref[...] += jnp.sum(dh2 * xh, axis=0, keepdims=True)
        dx_ref[...] = dyv + _rms_bwd(dh2, g_ref[...], xh, r, 1.0 / D)

    row = lambda w: pl.BlockSpec((tm, w), lambda i: (i, 0))
    blk = pl.BlockSpec((N_CHIPS, tm, FFB), lambda i: (0, i, 0))
    bsd = jax.ShapeDtypeStruct((N_CHIPS, t, FFB), BF16)
    return pl.pallas_call(
        body, name="ffn_bwd", grid=(t // tm,),
        in_specs=[row(D), row(D), _const_spec((1, D)), blk, blk] + _ffn_weight_specs(),
        out_specs=[row(D), blk, blk, blk, pl.BlockSpec((1, D), lambda i: (0, 0))],
        out_shape=[jax.ShapeDtypeStruct((t, D), F32), bsd, bsd, bsd, jax.ShapeDtypeStruct((1, D), F32)],
        compiler_params=_cparams(("arbitrary",)),
    )(dy, xm, g2, a, b, gf, gf, gf)


def _wgrad_blocks(a, b, tt, name):
    _, t, rows = a.shape
    cols = b.shape[1]
    nsteps = t // tt

    def body(a_ref, b_ref, o_ref, acc_ref):
        s = pl.program_id(0)

        @pl.when(s == 0)
        def _():
            acc_ref[...] = jnp.zeros_like(acc_ref)

        bv = b_ref[...].astype(BF16)
        for k in range(N_CHIPS):
            acc_ref[k] += _tn(a_ref[k], bv)

        @pl.when(s == nsteps - 1)
        def _():
            o_ref[...] = acc_ref[...].astype(BF16)

    return pl.pallas_call(
        body, name=name, grid=(nsteps,),
        in_specs=[pl.BlockSpec((N_CHIPS, tt, rows), lambda s: (0, s, 0)), pl.BlockSpec((tt, cols), lambda s: (s, 0))],
        out_specs=pl.BlockSpec((N_CHIPS, rows, cols), lambda s: (0, 0, 0)),
        out_shape=jax.ShapeDtypeStruct((N_CHIPS, rows, cols), BF16),
        scratch_shapes=[pltpu.VMEM((N_CHIPS, rows, cols), F32)],
        compiler_params=_cparams(("arbitrary",)),
    )(a, b)


def _head_rows(first, n_heads):
    return [(first + g * HD, first + g * HP, HD) for g in range(n_heads)]


def _wgrad(a, b, tt, name):
    t, k = a.shape
    n = b.shape[1]
    nsteps = t // tt

    def body(a_ref, b_ref, o_ref, acc_ref):
        s = pl.program_id(0)

        @pl.when(s == 0)
        def _():
            acc_ref[...] = jnp.zeros_like(acc_ref)

        acc_ref[...] += _tn(a_ref[...].astype(BF16), b_ref[...].astype(BF16))

        @pl.when(s == nsteps - 1)
        def _():
            o_ref[...] = acc_ref[...].astype(BF16)

    return pl.pallas_call(
        body, name=name, grid=(nsteps,),
        in_specs=[pl.BlockSpec((tt, k), lambda s: (s, 0)), pl.BlockSpec((tt, n), lambda s: (s, 0))],
        out_specs=pl.BlockSpec((k, n), lambda s: (0, 0)),
        out_shape=jax.ShapeDtypeStruct((k, n), BF16),
        scratch_shapes=[pltpu.VMEM((k, n), F32)],
        compiler_params=_cparams(("arbitrary",)),
    )(a, b)


def _wgrad_in(dpm, dkv, h, tt):
    t = h.shape[0]
    nsteps = t // tt
    kvw = dkv.shape[1]
    pieces = [(0, 0, O_Q)] + _head_rows(O_Q, NQ + 2 * NKV)

    def body(m_ref, kv_ref, h_ref, o_ref, acc_ref):
        s = pl.program_id(0)

        @pl.when(s == 0)
        def _():
            acc_ref[...] = jnp.zeros_like(acc_ref)

        hv = h_ref[...]
        acc_ref[:NMAIN, :] += _tn(m_ref[...], hv)
        acc_ref[NMAIN:, :] += _tn(kv_ref[...], hv)

        @pl.when(s == nsteps - 1)
        def _():
            for dst, src, size in pieces:
                o_ref[dst:dst + size, :] = acc_ref[src:src + size, :].astype(BF16)

    return pl.pallas_call(
        body, name="wgrad_in", grid=(nsteps,),
        in_specs=[pl.BlockSpec((tt, NMAIN), lambda s: (s, 0)), pl.BlockSpec((tt, kvw), lambda s: (s, 0)),
                  pl.BlockSpec((tt, D), lambda s: (s, 0))],
        out_specs=pl.BlockSpec((W_IN_ROWS, D), lambda s: (0, 0)),
        out_shape=jax.ShapeDtypeStruct((W_IN_ROWS, D), BF16),
        scratch_shapes=[pltpu.VMEM((NMAIN + kvw, D), F32)],
        compiler_params=_cparams(("arbitrary",)),
    )(dpm, dkv, h)


def _mixer_bwd(dxm, proj, ao, cw, gq, gk, sinks, gco, gao, wo, tq):
    t = proj.shape[0]
    nb = tq // BLK
    r8 = tq // 8
    nt = t // tq
    te = tq + 8
    kvw = 2 * NKV * HP

    def body(dx_ref, dxn_ref, p_ref, cgp_ref, hcp_ref, bgn_ref, cgn_ref, hcn_ref, kvp_ref, ao_ref, cw_ref, gq_ref,
             gk_ref, sk_ref, gco_ref, gao_ref, wo_ref,
             dpm_ref, dkvm_ref, dkvh_ref, dcw_ref, dgq_ref, dgk_ref, dsk_ref, dgco_ref, dgao_ref, acc_ref):
        i = pl.program_id(0)

        @pl.when(i == 0)
        def _():
            for r in (dcw_ref, dgq_ref, dgk_ref, dsk_ref, dgco_ref, dgao_ref):
                r[...] = jnp.zeros_like(r)

        acc_ref[...] = jnp.zeros_like(acc_ref)
        live_rows = jnp.where(i < nt - 1, te, tq)
        dxb = dx_ref[...].astype(BF16)
        dxe = jnp.concatenate([dxb, dxn_ref[...].astype(BF16)], axis=0)
        dcn = _nt(dxe, wo_ref[0:CC, :])
        bg = jnp.concatenate([p_ref[:, O_BG:O_BG + CC], bgn_ref[...]], axis=0)
        cg = jnp.concatenate([p_ref[:, O_CG:O_CG + CC], cgn_ref[...]], axis=0)
        hc = jnp.concatenate([p_ref[:, O_HC:O_HC + CC], hcn_ref[...]], axis=0)
        u = cg * hc
        up = jnp.where(i > 0, cgp_ref[...] * hcp_ref[...], 0.0)
        u1, u2 = _conv_taps(jnp.concatenate([up, u], axis=0), te)
        w0, w1, w2 = cw_ref[0:1, :], cw_ref[1:2, :], cw_ref[2:3, :]
        y = w0 * u2 + w1 * u1 + w2 * u
        co = bg * y
        rc, coh = _rms_fwd(co, 1.0 / CC)
        dco = _rms_bwd(dcn, gco_ref[...], coh, rc, 1.0 / CC)
        row_io = lax.broadcasted_iota(jnp.int32, (te, 1), 0)
        own = row_io < tq
        dgco_ref[...] += jnp.sum(jnp.where(own, dcn * coh, 0.0), axis=0, keepdims=True)
        dyc = jnp.where(row_io < live_rows, dco * bg, 0.0)
        dyo = jnp.where(own, dyc, 0.0)
        dcw_ref[0:1, :] += jnp.sum(dyo * u2, axis=0, keepdims=True)
        dcw_ref[1:2, :] += jnp.sum(dyo * u1, axis=0, keepdims=True)
        dcw_ref[2:3, :] += jnp.sum(dyo * u, axis=0, keepdims=True)
        dy1 = pltpu.roll(dyc, te - 1, 0)[0:tq]
        dy2 = pltpu.roll(dyc, te - 2, 0)[0:tq]
        du = w2 * dyc[0:tq] + w1 * dy1 + w0 * dy2
        dpm_ref[:, O_BG:O_BG + CC] = (dco[0:tq] * y[0:tq]).astype(BF16)
        dpm_ref[:, O_CG:O_CG + CC] = (du * hc[0:tq]).astype(BF16)
        dpm_ref[:, O_HC:O_HC + CC] = (du * cg[0:tq]).astype(BF16)
        kraw = jnp.concatenate([kvp_ref[:, 0:NKV * HP], p_ref[:, O_K:O_K + NKV * HP]], axis=0)
        vraw = jnp.concatenate([kvp_ref[:, NKV * HP:], p_ref[:, O_V:O_V + NKV * HP]], axis=0)
        gqv, gkv = gq_ref[...], gk_ref[...]
        keys = _norm_keys(kraw, gkv)
        vb = [vraw[:, h * HP:(h + 1) * HP].astype(BF16) for h in range(NKV)]
        base_valid, c_io = _band_mask()
        lane = lax.broadcasted_iota(jnp.int32, (1, HP), 1)
        dgq, dgk, dsk = (jnp.zeros((1, HP), F32) for _ in range(3))
        dgao = jnp.zeros((1, NQ * HD), F32)
        for b in range(nb):
            lo = jnp.where(i * nb + b == 0, BLK, 0)
            valid = base_valid & (c_io >= lo)
            band = slice(b * BLK, b * BLK + 2 * BLK)
            blk = slice(b * BLK, (b + 1) * BLK)
            ra, aoh = _rms_fwd(ao_ref[blk, :], 1.0 / (NQ * HD))
            danb = _nt(dxb[blk], wo_ref[CC:MIXW, :])
            dgao = dgao + jnp.sum(danb * aoh, axis=0, keepdims=True)
            dao = _rms_bwd(danb, gao_ref[...], aoh, ra, 1.0 / (NQ * HD))
            dos = [dao[:, g // 2 * HP:(g // 2 + 1) * HP] for g in range(NQ)]
            dos = [(d if g % 2 == 0 else pltpu.roll(d, HD, 1)).astype(BF16) for g, d in enumerate(dos)]
            fwd = []
            for g in range(NQ):
                rq, qh = _rms_fwd(p_ref[blk, O_Q + g * HP:O_Q + (g + 1) * HP], 1.0 / HD)
                qs = (qh * (gqv * SCALE)).astype(BF16)
                fwd.append((rq, qh, qs) + _attn_probs(qs, keys[g // GRP][2][band], sk_ref[0, g], valid))
            dqs = []
            for h in range(NKV):
                khat, rk, kn = [a[band] for a in keys[h]]
                dss, prbs, qns, dobs = [], [], [], []
                for g in range(h * GRP, (h + 1) * GRP):
                    rq, qh, qs, pr, ps = fwd[g]
                    dob = dos[g]
                    dp = _nt(dob, vb[h][band])
                    delta = jnp.sum(pr * dp, axis=-1, keepdims=True)
                    dsb = (pr * (dp - delta)).astype(BF16)
                    dsk = dsk + jnp.where(lane == g, -jnp.sum(ps * delta, axis=0, keepdims=True), 0.0)
                    dqn = jnp.dot(dsb, kn, preferred_element_type=F32) * SCALE
                    dgq = dgq + jnp.sum(dqn * qh, axis=0, keepdims=True)
                    dqs.append(_rms_bwd(dqn, gqv, qh, rq, 1.0 / HD).astype(BF16))
                    dss.append(dsb)
                    prbs.append(pr.astype(BF16))
                    qns.append(qs)
                    dobs.append(dob)
                dkn = _tn(jnp.concatenate(dss, axis=0), jnp.concatenate(qns, axis=0))
                dv = _tn(jnp.concatenate(prbs, axis=0), jnp.concatenate(dobs, axis=0))
                dgk = dgk + jnp.sum(dkn * khat, axis=0, keepdims=True)
                acc_ref[band, h * HP:(h + 1) * HP] += _rms_bwd(dkn, gkv, khat, rk, 1.0 / HD)
                acc_ref[band, (NKV + h) * HP:(NKV + h + 1) * HP] += dv
            dpm_ref[blk, O_Q:O_K] = jnp.concatenate(dqs, axis=1)
        dgq_ref[...] += dgq
        dgk_ref[...] += dgk
        dsk_ref[...] += dsk
        dgao_ref[...] += dgao
        dkvh_ref[...] = acc_ref[0:BLK, :]
        dkvm_ref[...] = acc_ref[BLK:, :]

    prev8 = lambda col: pl.BlockSpec((8, CC), lambda i: (jnp.maximum(i * r8 - 1, 0), col))
    next8 = lambda col: pl.BlockSpec((8, CC), lambda i: (jnp.minimum((i + 1) * r8, t // 8 - 1), col))
    small = lambda n: pl.BlockSpec((1, n), lambda i: (0, 0))
    return pl.pallas_call(
        body, name="mixer_bwd", grid=(nt,),
        in_specs=[
            pl.BlockSpec((tq, D), lambda i: (i, 0)),
            pl.BlockSpec((8, D), lambda i: (jnp.minimum((i + 1) * r8, t // 8 - 1), 0)),
            pl.BlockSpec((tq, NP), lambda i: (i, 0)),
            prev8(O_CG // CC), prev8(O_HC // CC),
            next8(O_BG // CC), next8(O_CG // CC), next8(O_HC // CC),
            pl.BlockSpec((BLK, kvw), lambda i: (jnp.maximum(i * nb - 1, 0), O_K // kvw)),
            pl.BlockSpec((tq, NQ * HD), lambda i: (i, 0)),
            _const_spec((8, CC)), _const_spec((1, HP)), _const_spec((1, HP)),
            pl.BlockSpec(memory_space=pltpu.SMEM),
            _const_spec((1, CC)), _const_spec((1, NQ * HD)), _const_spec((MIXW, D)),
        ],
        out_specs=[
            pl.BlockSpec((tq, NMAIN), lambda i: (i, 0)),
            pl.BlockSpec((tq, kvw), lambda i: (i, 0)),
            pl.BlockSpec((BLK, kvw), lambda i: (i, 0)),
            pl.BlockSpec((8, CC), lambda i: (0, 0)), small(HP), small(HP), small(HP), small(CC), small(NQ * HD),
        ],
        out_shape=[
            jax.ShapeDtypeStruct((t, NMAIN), BF16), jax.ShapeDtypeStruct((t, kvw), F32),
            jax.ShapeDtypeStruct((nt * BLK, kvw), F32),
            jax.ShapeDtypeStruct((8, CC), F32), jax.ShapeDtypeStruct((1, HP), F32), jax.ShapeDtypeStruct((1, HP), F32),
            jax.ShapeDtypeStruct((1, HP), F32), jax.ShapeDtypeStruct((1, CC), F32),
            jax.ShapeDtypeStruct((1, NQ * HD), F32),
        ],
        scratch_shapes=[pltpu.VMEM((tq + BLK, kvw), F32)],
        compiler_params=_cparams(("arbitrary",)),
    )(dxm, dxm, proj, proj, proj, proj, proj, proj, proj, ao, cw, gq, gk, sinks, gco, gao, wo)


def _inproj_bwd(dpm, dkvm, dkvh, wpt, x, g1, dxm, tm):
    t = x.shape[0]
    kvw = 2 * NKV * HP
    nt = t // tm

    def body(dp_ref, dk_ref, dh_ref, w_ref, x_ref, g_ref, dxm_ref, dx_ref, dg_ref, dkv_ref):
        i = pl.program_id(0)

        @pl.when(i == 0)
        def _():
            dg_ref[...] = jnp.zeros_like(dg_ref)

        halo = jnp.where(i < nt - 1, dh_ref[...], 0.0)
        dkv_ref[0:tm - BLK, :] = dk_ref[0:tm - BLK, :].astype(BF16)
        dkv_ref[tm - BLK:tm, :] = (dk_ref[tm - BLK:tm, :] + halo).astype(BF16)
        dh = (jnp.dot(dp_ref[...], w_ref[0:NMAIN, :], preferred_element_type=F32)
              + jnp.dot(dkv_ref[...], w_ref[NMAIN:NP, :], preferred_element_type=F32))
        r, xh = _rms_fwd(x_ref[...], 1.0 / D)
        dg_ref[...] += jnp.sum(dh * xh, axis=0, keepdims=True)
        dx_ref[...] = dxm_ref[...] + _rms_bwd(dh, g_ref[...], xh, r, 1.0 / D)

    row = lambda w: pl.BlockSpec((tm, w), lambda i: (i, 0))
    return pl.pallas_call(
        body, name="inproj_bwd", grid=(nt,),
        in_specs=[row(NMAIN), row(kvw), pl.BlockSpec((BLK, kvw), lambda i: (jnp.minimum(i + 1, nt - 1), 0)),
                  _const_spec((NP, D)), row(D), _const_spec((1, D)), row(D)],
        out_specs=[row(D), pl.BlockSpec((1, D), lambda i: (0, 0)), row(kvw)],
        out_shape=[jax.ShapeDtypeStruct((t, D), F32), jax.ShapeDtypeStruct((1, D), F32),
                   jax.ShapeDtypeStruct((t, kvw), BF16)],
        compiler_params=_cparams(("arbitrary",)),
    )(dpm, dkvm, dkvh, wpt, x, g1, dxm)


def _rows_tile(rows, cap=512):
    for cand in range(min(rows, cap) // 16 * 16, 0, -16):
        if rows % cand == 0:
            return cand
    return rows


def _presum_halves(gs, theirs, core):
    n = len(gs)

    def body(c_ref, *refs):
        for g_ref, t_ref, o_ref in zip(refs[:n], refs[n:2 * n], refs[2 * n:]):
            o_ref[...] = (g_ref[...].astype(F32) + t_ref[...].astype(F32)).astype(BF16)

    half = lambda ta: pl.BlockSpec((None,) + ta.shape[1:], lambda k, c_ref: (k, 0, 0))
    own = lambda ta: pl.BlockSpec((None,) + ta.shape[1:], lambda k, c_ref: (k, c_ref[0], 0))
    return pl.pallas_call(
        body, name="presum",
        grid_spec=pltpu.PrefetchScalarGridSpec(
            num_scalar_prefetch=1, grid=(N_CHIPS,),
            in_specs=[own(ta) for ta in theirs] + [half(ta) for ta in theirs],
            out_specs=[half(ta) for ta in theirs]),
        out_shape=[jax.ShapeDtypeStruct(ta.shape, BF16) for ta in theirs],
        compiler_params=_cparams(("parallel",)),
    )(core, *gs, *theirs)


def _sum_chips(got, ps, chip):
    n = len(got)
    steps = 2

    def body(chip_ref, *refs):
        for c_ref, own_ref, o_ref in zip(refs[:n], refs[n:2 * n], refs[2 * n:]):
            acc = None
            for j in range(N_CHIPS):
                term = jnp.where(chip_ref[0] == j, own_ref[...], c_ref[j]).astype(F32)
                acc = term if acc is None else acc + term
            o_ref[...] = acc

    tile = lambda c: (c.shape[1] // steps, c.shape[2])
    return pl.pallas_call(
        body, name="chipsum",
        grid_spec=pltpu.PrefetchScalarGridSpec(
            num_scalar_prefetch=1, grid=(steps,),
            in_specs=[pl.BlockSpec((N_CHIPS,) + tile(c), lambda i, chip_ref: (0, i, 0)) for c in got]
            + [pl.BlockSpec((None,) + tile(c), lambda i, chip_ref: (chip_ref[0], i, 0)) for c in got],
            out_specs=[pl.BlockSpec(tile(c), lambda i, chip_ref: (i, 0)) for c in got]),
        out_shape=[jax.ShapeDtypeStruct(c.shape[1:], F32) for c in got],
        compiler_params=_cparams(("parallel",)),
    )(chip, *got, *ps)


def _adamw(w, g, m, v, name):
    rows, cols = w.shape
    tr = _rows_tile(rows)
    c1 = 1.0 - ADAM_B1 ** ADAM_STEP
    c2 = 1.0 - ADAM_B2 ** ADAM_STEP

    def body(w_ref, g_ref, m_ref, v_ref, d_ref, mo_ref, vo_ref):
        gv = g_ref[...]
        mn = ADAM_B1 * m_ref[...] + (1.0 - ADAM_B1) * gv
        vn = ADAM_B2 * v_ref[...] + (1.0 - ADAM_B2) * (gv * gv)
        mo_ref[...] = mn
        vo_ref[...] = vn
        d_ref[...] = -ADAM_LR * ((mn / c1) / (jnp.sqrt(vn / c2) + ADAM_EPS) + ADAM_WD * w_ref[...])

    spec = pl.BlockSpec((tr, cols), lambda i: (i, 0))
    sds = jax.ShapeDtypeStruct((rows, cols), F32)
    return pl.pallas_call(
        body, name=name, grid=(rows // tr,), in_specs=[spec] * 4, out_specs=[spec] * 3, out_shape=[sds] * 3,
        compiler_params=_cparams(("parallel",)),
    )(w, g, m, v)


def _place():
    x, y, c = lax.axis_index("x"), lax.axis_index("y"), lax.axis_index("c")
    chips = [(1 - x, y), (x, 1 - y), (1 - x, 1 - y)]
    return x, y, c, chips


ANY = pl.BlockSpec(memory_space=pl.ANY)
DMA_ROWS = 64


def _pieces(shape):
    rows = shape[-2]
    step = DMA_ROWS if rows % DMA_ROWS == 0 else rows
    lead = [()]
    for n in shape[:-2]:
        lead = [i + (k,) for i in lead for k in range(n)]
    return [i + (pl.ds(r0, step),) for i in lead for r0 in range(0, rows, step)]


def _start_pieces(make, src, dst):
    for idx in _pieces(src.shape):
        make(src.at[idx], dst.at[idx]).start()


def _gather_body(srcs, outs, sems, layer, start):
    nw = len(srcs)
    ssem, rsem, fssem, frsem = sems
    x, y, c, chips = _place()
    kme = 2 * x + y

    def plane(j, w, to):
        return lambda s, d: pltpu.make_async_remote_copy(
            src_ref=s, dst_ref=d, send_sem=ssem.at[j, w], recv_sem=rsem.at[j, w], device_id=to,
            device_id_type=MESH)

    def passed(j, w):
        return lambda s, d: pltpu.make_async_remote_copy(
            src_ref=s, dst_ref=d, send_sem=fssem.at[j, w], recv_sem=frsem.at[j, w],
            device_id=(x, y, 1 - c), device_id_type=MESH)

    @pl.when(c == layer)
    def _():
        for j, (px, py) in enumerate(chips):
            for w in range(nw):
                start(plane(j, w, (px, py, c)), srcs[w], outs[w].at[kme])
        for j, (px, py) in enumerate(chips):
            for w in range(nw):
                got = outs[w].at[2 * px + py]
                plane(j, w, (px, py, c))(got, got).wait_recv()
                start(passed(j, w), got, got)
        for j, (px, py) in enumerate(chips):
            for w in range(nw):
                got = outs[w].at[2 * px + py]
                plane(j, w, (px, py, c))(got, got).wait_send()
                passed(j, w)(got, got).wait_send()

    @pl.when(c != layer)
    def _():
        for j, (px, py) in enumerate(chips):
            for w in range(nw):
                got = outs[w].at[2 * px + py]
                passed(j, w)(got, got).wait_recv()


def _handshake(peers):
    barrier = pltpu.get_barrier_semaphore()
    for peer in peers:
        pl.semaphore_signal(barrier, inc=1, device_id=peer, device_id_type=MESH)
    pl.semaphore_wait(barrier, len(peers))


def _handshake_all():
    x, y, c, _ = _place()
    _handshake([(x ^ (r >> 2), y ^ ((r >> 1) & 1), c ^ (r & 1)) for r in range(1, 8)])


def _gather_layer_async(blocks, layer, name, collective_id):
    hbm = pltpu.MemorySpace.HBM
    srcs = [jax.new_ref(b, memory_space=hbm) for b in blocks]
    outs = [jax.empty_ref(jax.ShapeDtypeStruct((N_CHIPS,) + b.shape, b.dtype), memory_space=hbm) for b in blocks]

    @pl.kernel(mesh=plsc.ScalarSubcoreMesh(axis_name="seq", num_cores=1), name=name,
               scratch_types=[pltpu.SemaphoreType.DMA((3, len(blocks)))] * 4,
               compiler_params=pltpu.CompilerParams(collective_id=collective_id))
    def launch(*sems):
        _handshake_all()
        _gather_body(srcs, outs, sems, layer, lambda make, s, d: make(s, d).start())

    launch()
    return [o[...] for o in outs]


def _swap_siblings(arrs, halves, name, collective_id=None):
    nw = len(arrs)
    out_sds = [jax.ShapeDtypeStruct((a.shape[0], a.shape[1] // 2, a.shape[2]) if halves else a.shape, a.dtype)
               for a in arrs]

    def exchange(srcs, outs, ssem, rsem, start):
        x, y, c, _ = _place()

        def give(w):
            return lambda s, d: pltpu.make_async_remote_copy(
                src_ref=s, dst_ref=d, send_sem=ssem.at[w], recv_sem=rsem.at[w], device_id=(x, y, 1 - c),
                device_id_type=MESH)

        for w in range(nw):
            hr = outs[w].shape[1]
            start(give(w), srcs[w].at[:, pl.ds((1 - c) * hr, hr)] if halves else srcs[w], outs[w])
        for w in range(nw):
            give(w)(outs[w], outs[w]).wait()

    if collective_id is None:
        def body(*refs):
            exchange(refs[:nw], refs[nw:2 * nw], *refs[2 * nw:], _start_pieces)

        return pl.pallas_call(
            body, name=name, in_specs=[ANY] * nw, out_specs=[ANY] * nw, out_shape=out_sds,
            scratch_shapes=[pltpu.SemaphoreType.DMA((nw,))] * 2,
            compiler_params=_cparams(has_side_effects=True),
        )(*arrs)

    hbm = pltpu.MemorySpace.HBM
    srcs = [jax.new_ref(a, memory_space=hbm) for a in arrs]
    outs = [jax.empty_ref(sds, memory_space=hbm) for sds in out_sds]

    @pl.kernel(mesh=plsc.ScalarSubcoreMesh(axis_name="seq", num_cores=1), name=name,
               scratch_types=[pltpu.SemaphoreType.DMA((nw,))] * 2,
               compiler_params=pltpu.CompilerParams(collective_id=collective_id))
    def launch(ssem, rsem):
        x, y, c, _ = _place()
        _handshake([(x, y, 1 - c)])
        exchange(srcs, outs, ssem, rsem, lambda make, s, d: make(s, d).start())

    launch()
    return [o[...] for o in outs]


def _scatter_body(srcs, outs, sems, start):
    nw = len(srcs)
    ssem, rsem = sems
    x, y, c, chips = _place()
    kme = 2 * x + y

    def give(j, w, to):
        return lambda s, d: pltpu.make_async_remote_copy(
            src_ref=s, dst_ref=d, send_sem=ssem.at[j, w], recv_sem=rsem.at[j, w], device_id=to,
            device_id_type=MESH)

    for j, (px, py) in enumerate(chips):
        for w in range(nw):
            start(give(j, w, (px, py, c)), srcs[w].at[2 * px + py], outs[w].at[kme])
    for j, (px, py) in enumerate(chips):
        for w in range(nw):
            got = outs[w].at[2 * px + py]
            give(j, w, (px, py, c))(got, got).wait_recv()
    for j, (px, py) in enumerate(chips):
        for w in range(nw):
            sent = srcs[w].at[2 * px + py]
            give(j, w, (px, py, c))(sent, sent).wait_send()


def _scatter_chips_async(ps, name, collective_id):
    hbm = pltpu.MemorySpace.HBM
    srcs = [jax.new_ref(p, memory_space=hbm) for p in ps]
    outs = [jax.empty_ref(jax.ShapeDtypeStruct(p.shape, p.dtype), memory_space=hbm) for p in ps]

    @pl.kernel(mesh=plsc.ScalarSubcoreMesh(axis_name="seq", num_cores=1), name=name,
               scratch_types=[pltpu.SemaphoreType.DMA((3, len(ps)))] * 2,
               compiler_params=pltpu.CompilerParams(collective_id=collective_id))
    def launch(*sems):
        _handshake_all()
        _scatter_body(srcs, outs, sems, lambda make, s, d: make(s, d).start())

    launch()
    return [o[...] for o in outs]


def _allreduce_small(v):
    rows = v.shape[0]

    def body(v_ref, o_ref, buf, ssem, rsem):
        x, y, c, _ = _place()
        me = 4 * x + 2 * y + c
        buf[me] = v_ref[...]
        sends = []
        for r in range(1, 8):
            peer = (x ^ (r >> 2), y ^ ((r >> 1) & 1), c ^ (r & 1))
            cp = pltpu.make_async_remote_copy(
                src_ref=v_ref, dst_ref=buf.at[me], send_sem=ssem.at[r - 1], recv_sem=rsem.at[r - 1],
                device_id=peer, device_id_type=MESH)
            cp.start()
            sends.append(cp)
        for r in range(1, 8):
            src = me ^ r
            pltpu.make_async_remote_copy(
                src_ref=v_ref, dst_ref=buf.at[src], send_sem=ssem.at[r - 1], recv_sem=rsem.at[r - 1],
                device_id=(x, y, c), device_id_type=MESH).wait_recv()
        for cp in sends:
            cp.wait_send()
        acc = buf[0]
        for d in range(1, 8):
            acc = acc + buf[d]
        o_ref[...] = acc

    vm = pl.BlockSpec(memory_space=pltpu.VMEM)
    return pl.pallas_call(
        body, name="allreduce_small", in_specs=[vm], out_specs=vm,
        out_shape=jax.ShapeDtypeStruct(v.shape, F32),
        scratch_shapes=[pltpu.VMEM((8, rows, 128), F32), pltpu.SemaphoreType.DMA((7,)),
                        pltpu.SemaphoreType.DMA((7,))],
        compiler_params=_cparams(has_side_effects=True),
    )(v)


def _t(w):
    return jnp.swapaxes(w, -1, -2)


def _count(shape):
    n = 1
    for s in shape:
        n *= s
    return n


def _pack_rows(arrs):
    flat = [jnp.pad(a.reshape(-1), (0, (-_count(a.shape)) % 128)) for a in arrs]
    v = jnp.concatenate(flat)
    rows = -(-v.shape[0] // (8 * 128)) * 8
    return jnp.pad(v, (0, rows * 128 - v.shape[0])).reshape(rows, 128)


def kernel(x, norm1_g, w_in, conv_w, q_norm_g, k_norm_g, sinks, conv_out_g, attn_out_g, w_o, norm2_g, w_gate, w_up, w_down, loss_target, m_norm1_g, m_w_in, m_conv_w, m_q_norm_g, m_k_norm_g, m_sinks, m_conv_out_g, m_attn_out_g, m_w_o, m_norm2_g, m_w_gate, m_w_up, m_w_down, v_norm1_g, v_w_in, v_conv_w, v_q_norm_g, v_k_norm_g, v_sinks, v_conv_out_g, v_attn_out_g, v_w_o, v_norm2_g, v_w_gate, v_w_up, v_w_down):
    depth = w_in.shape[0]
    t = x.shape[1]
    xs = x.reshape(t, D)
    tgt = loss_target.reshape(t, D)
    xi, yi = lax.axis_index("x"), lax.axis_index("y")
    kme = 2 * xi + yi
    tm = min(512, t)
    tq = min(512, t)
    tf = min(256, t)
    tw = min(1024, t)

    cwp = jnp.pad(conv_w.reshape(depth * 3, CC // N_CHIPS), ((0, 8 - depth * 3), (0, 0)))
    own_f = [jnp.concatenate([_t(w_gate[l]), _t(w_up[l]), w_down[l]], axis=0).astype(BF16) for l in range(depth)]
    own_o = [w_o[l].astype(BF16) for l in range(depth)]
    own_i = [_t(w_in[l]).astype(BF16) for l in range(depth)]
    mine = lambda got, own: lax.dynamic_update_index_in_dim(got, own, kme, 0)
    (got_i0,) = _gather_layer_async([own_i[0]], 0, "gather_in0_seq", collective_id=14)
    got_ocw = _gather_layer_async([own_o[0], cwp], 0, "gather_o0_seq", collective_id=15)
    got_i0, own_f, own_o, own_i = lax.optimization_barrier((got_i0, own_f, own_o, own_i))
    gf0_in = lax.optimization_barrier((own_f[0], got_i0))[0]
    (got_f0,) = _gather_layer_async([gf0_in], 0, "gather_ffn0_seq", collective_id=6)

    chip = kme.reshape(1).astype(jnp.int32)

    def layer_params(l, got_o, cw_full):
        return dict(
            wo=mine(got_o, own_o[l]).reshape(MIXW, D),
            cw=jnp.pad(cw_full[l], ((0, 5), (0, 0))),
            g1=norm1_g[l].reshape(1, D), g2=norm2_g[l].reshape(1, D),
            gq=jnp.pad(q_norm_g[l], (0, HP - HD)).reshape(1, HP), gk=jnp.pad(k_norm_g[l], (0, HP - HD)).reshape(1, HP),
            sk=sinks[l].reshape(1, NQ), gco=conv_out_g[l].reshape(1, CC),
            gao=attn_out_g[l].reshape(1, NQ * HD))

    saved, layers = [], []
    cur = xs
    for l in range(depth):
        x_in = cur
        if l == 0:
            got_i = got_i0
        else:
            got_f1, got_o, got_i = lax.optimization_barrier((got_l1, cur))[0]
        proj, h, wpt = _inproj_fwd(cur, norm1_g[l].reshape(1, D), got_i, own_i[l], chip, tm)
        if l == 0:
            got_o, got_cw = lax.optimization_barrier((got_ocw, proj))[0]
            cw_full = mine(got_cw, cwp).transpose(1, 0, 2).reshape(8, CC)[:depth * 3].reshape(depth, 3, CC)
        p = layer_params(l, got_o, cw_full)
        p["wpt"] = wpt
        xm, mix, ao = _mixer_fwd(proj, cur, p["cw"], p["gq"], p["gk"], p["sk"], p["gco"], p["gao"], p["wo"], tq)
        if l == 0:
            got_f0 = lax.optimization_barrier((got_f0, xm))[0]
            l1_in = lax.optimization_barrier(([own_f[1], own_o[1], own_i[1]], got_f0))[0]
            got_l1 = _gather_layer_async(l1_in, 1, "gather_layer1_seq", collective_id=1)
        p["gf"] = mine(got_f0 if l == 0 else got_f1, own_f[l])
        layers.append(p)
        if l < depth - 1:
            cur, a, b, h2 = _ffn_fwd(xm, p["g2"], p["gf"], tm)
        else:
            lpart, dy, a, b, h2 = _ffn_fwd(xm, p["g2"], p["gf"], tm, tgt)
        saved.append(dict(x=x_in, proj=proj, h=h, xm=xm, mix=mix, ao=ao, a=a, b=b, h2=h2))

    ci = lax.axis_index("c")
    core = ci.reshape(1).astype(jnp.int32)
    rbig = [dict() for _ in range(depth)]
    gsmall = [None] * depth

    def after_(vals, after):
        return vals if after is None else lax.optimization_barrier((vals, after))[0]

    def reduce_1(gs, tag, ids):
        return gs, _swap_siblings(gs, True, f"swap_halves_{tag}_seq", ids[0]), tag, ids

    def reduce_2(state, after):
        gs, theirs, tag, ids = state
        ps = _presum_halves(gs, after_(theirs, after), core)
        return ps, _scatter_chips_async(ps, f"scatter_{tag}_seq", ids[1]), tag, ids

    def reduce_3(state, after):
        ps, got, tag, ids = state
        r_mine = _sum_chips(after_(got, after), ps, chip)
        return r_mine, _swap_siblings(r_mine, False, f"swap_reduced_{tag}" + ("_seq" if ids[2] else ""), ids[2])

    def reduce_4(state, after):
        r_mine, r_theirs = state
        return [jnp.where(ci == 0, jnp.concatenate([a, b], axis=0), jnp.concatenate([b, a], axis=0))
                for a, b in zip(r_mine, after_(r_theirs, after))]

    ids = {"ffn1": (7, 4, 8), "in1": (9, 5, 10), "ffn0": (11, 2, 12), "in0": (13, 3, None)}
    in_2 = scattering = None
    handed = {}
    for l in reversed(range(depth)):
        p, s = layers[l], saved[l]
        dxm, da, db, hm, dg2 = _ffn_bwd(dy, s["xm"], p["g2"], s["a"], s["b"], p["gf"], tf)
        if in_2 is not None:
            in_2 = reduce_2(in_2, dxm)
        g_wg = _wgrad_blocks(da, s["h2"], tw, "wgrad_gate")
        g_wu = _wgrad_blocks(db, s["h2"], tw, "wgrad_up")
        g_wd = _wgrad_blocks(hm, dy, tw, "wgrad_down")
        if in_2 is not None:
            handed[f"in{l + 1}"] = reduce_3(in_2, g_wd)
        ffn_1 = reduce_1([g_wg, g_wu, g_wd], f"ffn{l}", ids[f"ffn{l}"])
        dpm, dkvm, dkvh, dcw, dgq, dgk, dsk, dgco, dgao = _mixer_bwd(
            after_(dxm, scattering), s["proj"], s["ao"], p["cw"], p["gq"], p["gk"], p["sk"], p["gco"], p["gao"],
            p["wo"], tq)
        ffn_2 = reduce_2(ffn_1, dpm)
        scattering = ffn_2[0]
        g_o = _wgrad(s["mix"], dxm, tw, "wgrad_o")
        dx, dg1, dkv = _inproj_bwd(dpm, dkvm, dkvh, p["wpt"], s["x"], p["g1"], dxm, tq)
        g_in = _wgrad_in(dpm, dkv, s["h"], tw)
        dy = dx
        gsmall[l] = dict(g1=dg1, cw=dcw[:3], gq=dgq[0, :HD], gk=dgk[0, :HD], sk=dsk[0, :NQ], gco=dgco,
                         gao=dgao, g2=dg2)
        handed[f"ffn{l}"] = reduce_3(ffn_2, g_in)
        in_2 = reduce_1([g_in.reshape(N_CHIPS, -1, D), g_o.reshape(N_CHIPS, -1, D)], f"in{l}", ids[f"in{l}"])
    grad_x = dy.reshape(x.shape)

    small_shapes = dict(g1=(D,), cw=(3, CC), gq=(HD,), gk=(HD,), sk=(NQ,), gco=(CC,), gao=(NQ * HD,), g2=(D,))
    red = _allreduce_small(_pack_rows([gsmall[l][n] for l in range(depth) for n in small_shapes]
                                      + [lpart[0:1, 0:1]])).reshape(-1)
    red_small, offs = {n: [] for n in small_shapes}, 0
    for l in range(depth):
        for n, shp in small_shapes.items():
            cnt = _count(shp)
            red_small[n].append(red[offs:offs + cnt].reshape(shp))
            offs += -(-cnt // 128) * 128
    loss = red[offs]
    g_small = {n: jnp.stack(v) for n, v in red_small.items()}
    g_cw = lax.dynamic_slice_in_dim(g_small["cw"], kme * (CC // N_CHIPS), CC // N_CHIPS, axis=2)

    weights = [norm1_g, w_in, conv_w, q_norm_g, k_norm_g, sinks, conv_out_g, attn_out_g, w_o, norm2_g, w_gate,
               w_up, w_down]
    moms = [m_norm1_g, m_w_in, m_conv_w, m_q_norm_g, m_k_norm_g, m_sinks, m_conv_out_g, m_attn_out_g, m_w_o,
            m_norm2_g, m_w_gate, m_w_up, m_w_down]
    vars_ = [v_norm1_g, v_w_in, v_conv_w, v_q_norm_g, v_k_norm_g, v_sinks, v_conv_out_g, v_attn_out_g, v_w_o,
             v_norm2_g, v_w_gate, v_w_up, v_w_down]
    n_w = len(weights)
    big_idx = dict(zip(("in", "o", "g", "u", "d"), (1, 8, 10, 11, 12)))
    small_idx = [n for n in range(n_w) if n not in big_idx.values()]
    grads, deltas, new_m, new_v = [None] * n_w, [None] * n_w, [None] * n_w, [None] * n_w
    for n, g in zip(small_idx, (g_small["g1"], g_cw, g_small["gq"], g_small["gk"], g_small["sk"], g_small["gco"],
                                g_small["gao"], g_small["g2"])):
        grads[n] = g

    def update_big(name):
        n = big_idx[name]
        g = jnp.stack([rbig[l][name] for l in range(depth)])
        flip = g.shape != weights[n].shape
        rows2d = lambda a3: (_t(a3) if flip else a3).reshape(-1, D)
        res = _adamw(rows2d(weights[n]), g.reshape(-1, D), rows2d(moms[n]), rows2d(vars_[n]), f"adamw_{n}")
        res = [g] + [r.reshape(g.shape) for r in res]
        grads[n], deltas[n], new_m[n], new_v[n] = [_t(r) for r in res] if flip else res

    for l in range(depth):
        rbig[l]["g"], rbig[l]["u"], rbig[l]["d"] = reduce_4(handed[f"ffn{l}"], red)
    rbig[1]["in"], rbig[1]["o"] = reduce_4(handed["in1"], red)
    update_big("g")
    in_2 = reduce_2(in_2, new_v[big_idx["g"]])
    update_big("u")
    update_big("d")
    rbig[0]["in"], rbig[0]["o"] = reduce_4(reduce_3(in_2, new_v[big_idx["d"]]), None)
    for name in ("in", "o"):
        update_big(name)
    res = _adamw(*[_pack_rows([arrs[n] for n in small_idx]) for arrs in (weights, grads, moms, vars_)],
                 "adamw_small")
    offs = 0
    for n in small_idx:
        shp = weights[n].shape
        cnt = _count(shp)
        deltas[n], new_m[n], new_v[n] = [r.reshape(-1)[offs:offs + cnt].reshape(shp) for r in res]
        offs += -(-cnt // 128) * 128
    return (loss, grad_x, *grads, *deltas, *new_m, *new_v)
```

```python
import jax
import jax.numpy as jnp
from jax import lax
from jax.experimental import pallas as pl
from jax.experimental.pallas import tpu as pltpu
from jax.experimental.pallas import tpu_sc as plsc

F32 = jnp.float32
BF16 = jnp.bfloat16

D = 1024
CC = 512
NQ = 8
NKV = 2
HD = 64
HP = 128
GRP = NQ // NKV
FF = 2816
FFB = FF // 4
BLK = 128
EPS = 1e-6
NEG = -1e30
SCALE = HD ** -0.5
O_BG, O_CG, O_HC, O_Q = 0, CC, 2 * CC, 3 * CC
O_K = O_Q + NQ * HP
O_V = O_K + NKV * HP
NP = O_V + NKV * HP
NMAIN = O_K
MIXW = CC + NQ * HD
N_CHIPS = 4
VMEM_LIMIT = 56 * 1024 * 1024
MESH = pl.DeviceIdType.MESH

ADAM_LR, ADAM_B1, ADAM_B2, ADAM_EPS, ADAM_WD, ADAM_STEP = 0.001, 0.9, 0.999, 1e-08, 0.01, 10


def _cparams(sem=None, **kw):
    if sem is not None:
        kw["dimension_semantics"] = sem
    return pltpu.CompilerParams(vmem_limit_bytes=VMEM_LIMIT, **kw)


def _const_spec(shape):
    nd = len(shape)
    return pl.BlockSpec(shape, lambda *_: (0,) * nd, pipeline_mode=pl.Buffered(1))


def _nt(a, b):
    return lax.dot_general(a, b, (((1,), (1,)), ((), ())), preferred_element_type=F32)


def _tn(a, b):
    return lax.dot_general(a, b, (((0,), (0,)), ((), ())), preferred_element_type=F32)


def _rms_fwd(x, inv_n):
    r = lax.rsqrt(jnp.sum(x * x, axis=-1, keepdims=True) * inv_n + EPS)
    return r, x * r


def _rms_bwd(dy, g, xh, r, inv_n):
    dxh = dy * g
    return r * (dxh - xh * (jnp.sum(dxh * xh, axis=-1, keepdims=True) * inv_n))


W_IN_ROWS = 3 * CC + (NQ + 2 * NKV) * HD
W_IN_BLOCK = W_IN_ROWS // N_CHIPS


def _padded_row(row):
    return row + max(row - O_Q, 0) // HD * (HP - HD)


def _w_in_pieces(k):
    first = k * W_IN_BLOCK
    plain = min(max(O_Q - first, 0), W_IN_BLOCK)
    pieces = [(0, first, plain)] if plain else []
    return pieces + [(r, _padded_row(first + r), HD) for r in range(plain, W_IN_BLOCK, HD)]


def _inproj_fwd(x, g1, gi, own_i, chip, tm):
    t = x.shape[0]

    def body(chip_ref, x_ref, g_ref, gi_ref, own_ref, p_ref, h_ref, w_ref, sem):
        @pl.when(pl.program_id(0) == 0)
        def _():
            for k in range(N_CHIPS):
                for src, dst, rows in _w_in_pieces(k):
                    @pl.when(chip_ref[0] == k)
                    def _():
                        pltpu.make_async_copy(own_ref.at[pl.ds(src, rows)], w_ref.at[pl.ds(dst, rows)], sem).start()

                    @pl.when(chip_ref[0] != k)
                    def _():
                        pltpu.make_async_copy(gi_ref.at[k, pl.ds(src, rows)], w_ref.at[pl.ds(dst, rows)], sem).start()
            for slot in range(NQ + 2 * NKV):
                w_ref[O_Q + slot * HP + HD:O_Q + (slot + 1) * HP, :] = jnp.zeros((HP - HD, D), BF16)
            landed = w_ref.at[pl.ds(0, W_IN_ROWS)]
            pltpu.make_async_copy(landed, landed, sem).wait()

        _, xh = _rms_fwd(x_ref[...], 1.0 / D)
        h = (xh * g_ref[...]).astype(BF16)
        h_ref[...] = h
        p_ref[...] = _nt(h, w_ref[...])

    const = lambda shape: pl.BlockSpec(shape, lambda i, c: (0,) * len(shape))
    return pl.pallas_call(
        body, name="inproj_fwd",
        grid_spec=pltpu.PrefetchScalarGridSpec(
            num_scalar_prefetch=1, grid=(t // tm,),
            in_specs=[pl.BlockSpec((tm, D), lambda i, c: (i, 0)), const((1, D)), ANY, ANY],
            out_specs=[pl.BlockSpec((tm, NP), lambda i, c: (i, 0)), pl.BlockSpec((tm, D), lambda i, c: (i, 0)),
                       const((NP, D))],
            scratch_shapes=[pltpu.SemaphoreType.DMA(())]),
        out_shape=[jax.ShapeDtypeStruct((t, NP), F32), jax.ShapeDtypeStruct((t, D), BF16),
                   jax.ShapeDtypeStruct((NP, D), BF16)],
        compiler_params=_cparams(("arbitrary",)),
    )(chip, x, g1, gi, own_i)


def _band_mask():
    r_io = lax.broadcasted_iota(jnp.int32, (BLK, 2 * BLK), 0)
    c_io = lax.broadcasted_iota(jnp.int32, (BLK, 2 * BLK), 1)
    return (c_io > r_io) & (c_io <= r_io + BLK), c_io


def _conv_taps(uf, n):
    u1 = pltpu.roll(uf, 1, 0)[8:8 + n]
    u2 = pltpu.roll(uf, 2, 0)[8:8 + n]
    return u1, u2


def _attn_probs(qs, kband, sink, valid):
    s = jnp.where(valid, _nt(qs, kband), NEG)
    m = jnp.maximum(jnp.max(s, axis=-1, keepdims=True), sink)
    p = jnp.exp(s - m)
    es = jnp.exp(sink - m)
    inv = 1.0 / (jnp.sum(p, axis=-1, keepdims=True) + es)
    return p * inv, es * inv


def _norm_keys(kraw, gk):
    out = []
    for h in range(NKV):
        kh = kraw[:, h * HP:(h + 1) * HP]
        rk, khat = _rms_fwd(kh, 1.0 / HD)
        out.append((khat, rk, (khat * gk).astype(BF16)))
    return out


def _mixer_fwd(proj, x, cw, gq, gk, sinks, gco, gao, wo, tq):
    t = proj.shape[0]
    nb = tq // BLK
    r8 = tq // 8

    def body(p_ref, cgp_ref, hcp_ref, kvp_ref, x_ref, cw_ref, gq_ref, gk_ref, sk_ref, gco_ref, gao_ref,
             wo_ref, xm_ref, mix_ref, ao_ref, aop_ref):
        i = pl.program_id(0)
        cg = p_ref[:, O_CG:O_CG + CC]
        hc = p_ref[:, O_HC:O_HC + CC]
        u = cg * hc
        up = jnp.where(i > 0, cgp_ref[...] * hcp_ref[...], 0.0)
        u1, u2 = _conv_taps(jnp.concatenate([up, u], axis=0), tq)
        y = cw_ref[0:1, :] * u2 + cw_ref[1:2, :] * u1 + cw_ref[2:3, :] * u
        co = p_ref[:, O_BG:O_BG + CC] * y
        _, coh = _rms_fwd(co, 1.0 / CC)
        cn = coh * gco_ref[...]
        kraw = jnp.concatenate([kvp_ref[:, 0:NKV * HP], p_ref[:, O_K:O_K + NKV * HP]], axis=0)
        vraw = jnp.concatenate([kvp_ref[:, NKV * HP:], p_ref[:, O_V:O_V + NKV * HP]], axis=0)
        keys = _norm_keys(kraw, gk_ref[...])
        vb = [vraw[:, h * HP:(h + 1) * HP].astype(BF16) for h in range(NKV)]
        base_valid, c_io = _band_mask()
        gqs = gq_ref[...] * SCALE
        for b in range(nb):
            lo = jnp.where(i * nb + b == 0, BLK, 0)
            valid = base_valid & (c_io >= lo)
            for g in range(NQ):
                h = g // GRP
                qg = p_ref[b * BLK:(b + 1) * BLK, O_Q + g * HP:O_Q + (g + 1) * HP]
                _, qh = _rms_fwd(qg, 1.0 / HD)
                qs = (qh * gqs).astype(BF16)
                pr, _ = _attn_probs(qs, keys[h][2][b * BLK:b * BLK + 2 * BLK], sk_ref[0, g], valid)
                aop_ref[b * BLK:(b + 1) * BLK, g * HP:(g + 1) * HP] = jnp.dot(
                    pr.astype(BF16), vb[h][b * BLK:b * BLK + 2 * BLK], preferred_element_type=F32)
        for j in range(NQ // 2):
            ao_ref[:, j * HP:(j + 1) * HP] = (aop_ref[:, 2 * j * HP:(2 * j + 1) * HP]
                                              + pltpu.roll(aop_ref[:, (2 * j + 1) * HP:(2 * j + 2) * HP], HD, 1))
        _, aoh = _rms_fwd(ao_ref[...], 1.0 / (NQ * HD))
        an = aoh * gao_ref[...]
        mix = jnp.concatenate([cn, an], axis=1).astype(BF16)
        mix_ref[...] = mix
        xm_ref[...] = x_ref[...] + jnp.dot(mix, wo_ref[...], preferred_element_type=F32)

    prev8 = lambda col: pl.BlockSpec((8, CC), lambda i: (jnp.maximum(i * r8 - 1, 0), col))
    return pl.pallas_call(
        body, name="mixer_fwd", grid=(t // tq,),
        in_specs=[
            pl.BlockSpec((tq, NP), lambda i: (i, 0)),
            prev8(O_CG // CC), prev8(O_HC // CC),
            pl.BlockSpec((BLK, 2 * NKV * HP), lambda i: (jnp.maximum(i * nb - 1, 0), O_K // (2 * NKV * HP))),
            pl.BlockSpec((tq, D), lambda i: (i, 0)),
            _const_spec((8, CC)), _const_spec((1, HP)), _const_spec((1, HP)),
            pl.BlockSpec(memory_space=pltpu.SMEM),
            _const_spec((1, CC)), _const_spec((1, NQ * HD)), _const_spec((MIXW, D)),
        ],
        out_specs=[pl.BlockSpec((tq, D), lambda i: (i, 0)), pl.BlockSpec((tq, MIXW), lambda i: (i, 0)),
                   pl.BlockSpec((tq, NQ * HD), lambda i: (i, 0))],
        out_shape=[jax.ShapeDtypeStruct((t, D), F32), jax.ShapeDtypeStruct((t, MIXW), BF16),
                   jax.ShapeDtypeStruct((t, NQ * HD), F32)],
        scratch_shapes=[pltpu.VMEM((tq, NQ * HP), F32)],
        compiler_params=_cparams(("parallel",)),
    )(proj, proj, proj, proj, x, cw, gq, gk, sinks, gco, gao, wo)


def _ffn_weight_specs():
    return [pl.BlockSpec((N_CHIPS, FFB, D), lambda i, j=j: (0, j, 0), pipeline_mode=pl.Buffered(1))
            for j in range(3)]


def _ffn_fwd(xm, g2, gf, tm, tgt=None):
    t = xm.shape[0]
    last = tgt is not None

    def body(x_ref, g_ref, wg_ref, wu_ref, wd_ref, *rest):
        t_ref, rest = (rest[0], rest[1:]) if last else (None, rest)
        l_ref, rest = (rest[0], rest[1:]) if last else (None, rest)
        xo_ref, a_ref, b_ref, h2_ref = rest
        xv = x_ref[...]
        _, xh = _rms_fwd(xv, 1.0 / D)
        h2 = (xh * g_ref[...]).astype(BF16)
        h2_ref[...] = h2
        acc = xv
        for k in range(N_CHIPS):
            a = _nt(h2, wg_ref[k])
            b = _nt(h2, wu_ref[k])
            a_ref[k] = a.astype(BF16)
            b_ref[k] = b.astype(BF16)
            hm = (a * jax.nn.sigmoid(a) * b).astype(BF16)
            acc = acc + jnp.dot(hm, wd_ref[k], preferred_element_type=F32)
        if last:
            @pl.when(pl.program_id(0) == 0)
            def _():
                l_ref[...] = jnp.zeros_like(l_ref)

            e = acc - t_ref[...]
            xo_ref[...] = e * (1.0 / D)
            l_ref[...] += jnp.sum(jnp.sum(e * e, axis=-1, keepdims=True), axis=0, keepdims=True) * (0.5 / D)
        else:
            xo_ref[...] = acc

    row = lambda w: pl.BlockSpec((tm, w), lambda i: (i, 0))
    blk = pl.BlockSpec((N_CHIPS, tm, FFB), lambda i: (0, i, 0))
    bsd = jax.ShapeDtypeStruct((N_CHIPS, t, FFB), BF16)
    return pl.pallas_call(
        body, name="ffn_fwd_loss" if last else "ffn_fwd", grid=(t // tm,),
        in_specs=[row(D), _const_spec((1, D))] + _ffn_weight_specs() + ([row(D)] if last else []),
        out_specs=([pl.BlockSpec((8, 128), lambda i: (0, 0))] if last else []) + [row(D), blk, blk, row(D)],
        out_shape=([jax.ShapeDtypeStruct((8, 128), F32)] if last else [])
        + [jax.ShapeDtypeStruct((t, D), F32), bsd, bsd, jax.ShapeDtypeStruct((t, D), BF16)],
        compiler_params=_cparams(("arbitrary" if last else "parallel",)),
    )(*((xm, g2, gf, gf, gf) + ((tgt,) if last else ())))


def _ffn_bwd(dy, xm, g2, a, b, gf, tm):
    t = dy.shape[0]

    def body(dy_ref, x_ref, g_ref, a_ref, b_ref, wg_ref, wu_ref, wd_ref, dx_ref, da_ref, db_ref, hm_ref, dg_ref):
        @pl.when(pl.program_id(0) == 0)
        def _():
            dg_ref[...] = jnp.zeros_like(dg_ref)

        dyv = dy_ref[...]
        dyb = dyv.astype(BF16)
        dh2 = jnp.zeros_like(dyv)
        for k in range(N_CHIPS):
            dhm = _nt(dyb, wd_ref[k])
            av = a_ref[k].astype(F32)
            bv = b_ref[k].astype(F32)
            sig = jax.nn.sigmoid(av)
            sil = av * sig
            hm_ref[k] = (sil * bv).astype(BF16)
            da = (dhm * bv * (sig * (1.0 + av * (1.0 - sig)))).astype(BF16)
            db = (dhm * sil).astype(BF16)
            da_ref[k] = da
            db_ref[k] = db
            dh2 = (dh2 + jnp.dot(da, wg_ref[k], preferred_element_type=F32)
                   + jnp.dot(db, wu_ref[k], preferred_element_type=F32))
        r, xh = _rms_fwd(x_ref[...], 1.0 / D)
        dg_ref[...] += jnp.sum(dh2 * xh, axis=0, keepdims=True)
        dx_ref[...] = dyv + _rms_bwd(dh2, g_ref[...], xh, r, 1.0 / D)

    row = lambda w: pl.BlockSpec((tm, w), lambda i: (i, 0))
    blk = pl.BlockSpec((N_CHIPS, tm, FFB), lambda i: (0, i, 0))
    bsd = jax.ShapeDtypeStruct((N_CHIPS, t, FFB), BF16)
    return pl.pallas_call(
        body, name="ffn_bwd", grid=(t // tm,),
        in_specs=[row(D), row(D), _const_spec((1, D)), blk, blk] + _ffn_weight_specs(),
        out_specs=[row(D), blk, blk, blk, pl.BlockSpec((1, D), lambda i: (0, 0))],
        out_shape=[jax.ShapeDtypeStruct((t, D), F32), bsd, bsd, bsd, jax.ShapeDtypeStruct((1, D), F32)],
        compiler_params=_cparams(("arbitrary",)),
    )(dy, xm, g2, a, b, gf, gf, gf)


def _wgrad_blocks(a, b, tt, name):
    _, t, rows = a.shape
    cols = b.shape[1]
    nsteps = t // tt

    def body(a_ref, b_ref, o_ref, acc_ref):
        s = pl.program_id(0)

        @pl.when(s == 0)
        def _():
            acc_ref[...] = jnp.zeros_like(acc_ref)

        bv = b_ref[...].astype(BF16)
        for k in range(N_CHIPS):
            acc_ref[k] += _tn(a_ref[k], bv)

        @pl.when(s == nsteps - 1)
        def _():
            o_ref[...] = acc_ref[...].astype(BF16)

    return pl.pallas_call(
        body, name=name, grid=(nsteps,),
        in_specs=[pl.BlockSpec((N_CHIPS, tt, rows), lambda s: (0, s, 0)), pl.BlockSpec((tt, cols), lambda s: (s, 0))],
        out_specs=pl.BlockSpec((N_CHIPS, rows, cols), lambda s: (0, 0, 0)),
        out_shape=jax.ShapeDtypeStruct((N_CHIPS, rows, cols), BF16),
        scratch_shapes=[pltpu.VMEM((N_CHIPS, rows, cols), F32)],
        compiler_params=_cparams(("arbitrary",)),
    )(a, b)


def _head_rows(first, n_heads):
    return [(first + g * HD, first + g * HP, HD) for g in range(n_heads)]


def _wgrad(a, b, tt, name):
    t, k = a.shape
    n = b.shape[1]
    nsteps = t // tt

    def body(a_ref, b_ref, o_ref, acc_ref):
        s = pl.program_id(0)

        @pl.when(s == 0)
        def _():
            acc_ref[...] = jnp.zeros_like(acc_ref)

        acc_ref[...] += _tn(a_ref[...].astype(BF16), b_ref[...].astype(BF16))

        @pl.when(s == nsteps - 1)
        def _():
            o_ref[...] = acc_ref[...].astype(BF16)

    return pl.pallas_call(
        body, name=name, grid=(nsteps,),
        in_specs=[pl.BlockSpec((tt, k), lambda s: (s, 0)), pl.BlockSpec((tt, n), lambda s: (s, 0))],
        out_specs=pl.BlockSpec((k, n), lambda s: (0, 0)),
        out_shape=jax.ShapeDtypeStruct((k, n), BF16),
        scratch_shapes=[pltpu.VMEM((k, n), F32)],
        compiler_params=_cparams(("arbitrary",)),
    )(a, b)


def _wgrad_in(dpm, dkv, h, tt):
    t = h.shape[0]
    nsteps = t // tt
    kvw = dkv.shape[1]
    pieces = [(0, 0, O_Q)] + _head_rows(O_Q, NQ + 2 * NKV)

    def body(m_ref, kv_ref, h_ref, o_ref, acc_ref):
        s = pl.program_id(0)

        @pl.when(s == 0)
        def _():
            acc_ref[...] = jnp.zeros_like(acc_ref)

        hv = h_ref[...]
        acc_ref[:NMAIN, :] += _tn(m_ref[...], hv)
        acc_ref[NMAIN:, :] += _tn(kv_ref[...], hv)

        @pl.when(s == nsteps - 1)
        def _():
            for dst, src, size in pieces:
                o_ref[dst:dst + size, :] = acc_ref[src:src + size, :].astype(BF16)

    return pl.pallas_call(
        body, name="wgrad_in", grid=(nsteps,),
        in_specs=[pl.BlockSpec((tt, NMAIN), lambda s: (s, 0)), pl.BlockSpec((tt, kvw), lambda s: (s, 0)),
                  pl.BlockSpec((tt, D), lambda s: (s, 0))],
        out_specs=pl.BlockSpec((W_IN_ROWS, D), lambda s: (0, 0)),
        out_shape=jax.ShapeDtypeStruct((W_IN_ROWS, D), BF16),
        scratch_shapes=[pltpu.VMEM((NMAIN + kvw, D), F32)],
        compiler_params=_cparams(("arbitrary",)),
    )(dpm, dkv, h)


def _mixer_bwd(dxm, proj, ao, cw, gq, gk, sinks, gco, gao, wo, tq):
    t = proj.shape[0]
    nb = tq // BLK
    r8 = tq // 8
    nt = t // tq
    te = tq + 8
    kvw = 2 * NKV * HP

    def body(dx_ref, dxn_ref, p_ref, cgp_ref, hcp_ref, bgn_ref, cgn_ref, hcn_ref, kvp_ref, ao_ref, cw_ref, gq_ref,
             gk_ref, sk_ref, gco_ref, gao_ref, wo_ref,
             dpm_ref, dkvm_ref, dkvh_ref, dcw_ref, dgq_ref, dgk_ref, dsk_ref, dgco_ref, dgao_ref, acc_ref):
        i = pl.program_id(0)

        @pl.when(i == 0)
        def _():
            for r in (dcw_ref, dgq_ref, dgk_ref, dsk_ref, dgco_ref, dgao_ref):
                r[...] = jnp.zeros_like(r)

        acc_ref[...] = jnp.zeros_like(acc_ref)
        live_rows = jnp.where(i < nt - 1, te, tq)
        dxb = dx_ref[...].astype(BF16)
        dxe = jnp.concatenate([dxb, dxn_ref[...].astype(BF16)], axis=0)
        dcn = _nt(dxe, wo_ref[0:CC, :])
        bg = jnp.concatenate([p_ref[:, O_BG:O_BG + CC], bgn_ref[...]], axis=0)
        cg = jnp.concatenate([p_ref[:, O_CG:O_CG + CC], cgn_ref[...]], axis=0)
        hc = jnp.concatenate([p_ref[:, O_HC:O_HC + CC], hcn_ref[...]], axis=0)
        u = cg * hc
        up = jnp.where(i > 0, cgp_ref[...] * hcp_ref[...], 0.0)
        u1, u2 = _conv_taps(jnp.concatenate([up, u], axis=0), te)
        w0, w1, w2 = cw_ref[0:1, :], cw_ref[1:2, :], cw_ref[2:3, :]
        y = w0 * u2 + w1 * u1 + w2 * u
        co = bg * y
        rc, coh = _rms_fwd(co, 1.0 / CC)
        dco = _rms_bwd(dcn, gco_ref[...], coh, rc, 1.0 / CC)
        row_io = lax.broadcasted_iota(jnp.int32, (te, 1), 0)
        own = row_io < tq
        dgco_ref[...] += jnp.sum(jnp.where(own, dcn * coh, 0.0), axis=0, keepdims=True)
        dyc = jnp.where(row_io < live_rows, dco * bg, 0.0)
        dyo = jnp.where(own, dyc, 0.0)
        dcw_ref[0:1, :] += jnp.sum(dyo * u2, axis=0, keepdims=True)
        dcw_ref[1:2, :] += jnp.sum(dyo * u1, axis=0, keepdims=True)
        dcw_ref[2:3, :] += jnp.sum(dyo * u, axis=0, keepdims=True)
        dy1 = pltpu.roll(dyc, te - 1, 0)[0:tq]
        dy2 = pltpu.roll(dyc, te - 2, 0)[0:tq]
        du = w2 * dyc[0:tq] + w1 * dy1 + w0 * dy2
        dpm_ref[:, O_BG:O_BG + CC] = (dco[0:tq] * y[0:tq]).astype(BF16)
        dpm_ref[:, O_CG:O_CG + CC] = (du * hc[0:tq]).astype(BF16)
        dpm_ref[:, O_HC:O_HC + CC] = (du * cg[0:tq]).astype(BF16)
        kraw = jnp.concatenate([kvp_ref[:, 0:NKV * HP], p_ref[:, O_K:O_K + NKV * HP]], axis=0)
        vraw = jnp.concatenate([kvp_ref[:, NKV * HP:], p_ref[:, O_V:O_V + NKV * HP]], axis=0)
        gqv, gkv = gq_ref[...], gk_ref[...]
        keys = _norm_keys(kraw, gkv)
        vb = [vraw[:, h * HP:(h + 1) * HP].astype(BF16) for h in range(NKV)]
        base_valid, c_io = _band_mask()
        lane = lax.broadcasted_iota(jnp.int32, (1, HP), 1)
        dgq, dgk, dsk = (jnp.zeros((1, HP), F32) for _ in range(3))
        dgao = jnp.zeros((1, NQ * HD), F32)
        for b in range(nb):
            lo = jnp.where(i * nb + b == 0, BLK, 0)
            valid = base_valid & (c_io >= lo)
            band = slice(b * BLK, b * BLK + 2 * BLK)
            blk = slice(b * BLK, (b + 1) * BLK)
            ra, aoh = _rms_fwd(ao_ref[blk, :], 1.0 / (NQ * HD))
            danb = _nt(dxb[blk], wo_ref[CC:MIXW, :])
            dgao = dgao + jnp.sum(danb * aoh, axis=0, keepdims=True)
            dao = _rms_bwd(danb, gao_ref[...], aoh, ra, 1.0 / (NQ * HD))
            dos = [dao[:, g // 2 * HP:(g // 2 + 1) * HP] for g in range(NQ)]
            dos = [(d if g % 2 == 0 else pltpu.roll(d, HD, 1)).astype(BF16) for g, d in enumerate(dos)]
            fwd = []
            for g in range(NQ):
                rq, qh = _rms_fwd(p_ref[blk, O_Q + g * HP:O_Q + (g + 1) * HP], 1.0 / HD)
                qs = (qh * (gqv * SCALE)).astype(BF16)
                fwd.append((rq, qh, qs) + _attn_probs(qs, keys[g // GRP][2][band], sk_ref[0, g], valid))
            dqs = []
            for h in range(NKV):
                khat, rk, kn = [a[band] for a in keys[h]]
                dss, prbs, qns, dobs = [], [], [], []
                for g in range(h * GRP, (h + 1) * GRP):
                    rq, qh, qs, pr, ps = fwd[g]
                    dob = dos[g]
                    dp = _nt(dob, vb[h][band])
                    delta = jnp.sum(pr * dp, axis=-1, keepdims=True)
                    dsb = (pr * (dp - delta)).astype(BF16)
                    dsk = dsk + jnp.where(lane == g, -jnp.sum(ps * delta, axis=0, keepdims=True), 0.0)
                    dqn = jnp.dot(dsb, kn, preferred_element_type=F32) * SCALE
                    dgq = dgq + jnp.sum(dqn * qh, axis=0, keepdims=True)
                    dqs.append(_rms_bwd(dqn, gqv, qh, rq, 1.0 / HD).astype(BF16))
                    dss.append(dsb)
                    prbs.append(pr.astype(BF16))
                    qns.append(qs)
                    dobs.append(dob)
                dkn = _tn(jnp.concatenate(dss, axis=0), jnp.concatenate(qns, axis=0))
                dv = _tn(jnp.concatenate(prbs, axis=0), jnp.concatenate(dobs, axis=0))
                dgk = dgk + jnp.sum(dkn * khat, axis=0, keepdims=True)
                acc_ref[band, h * HP:(h + 1) * HP] += _rms_bwd(dkn, gkv, khat, rk, 1.0 / HD)
                acc_ref[band, (NKV + h) * HP:(NKV + h + 1) * HP] += dv
            dpm_ref[blk, O_Q:O_K] = jnp.concatenate(dqs, axis=1)
        dgq_ref[...] += dgq
        dgk_ref[...] += dgk
        dsk_ref[...] += dsk
        dgao_ref[...] += dgao
        dkvh_ref[...] = acc_ref[0:BLK, :]
        dkvm_ref[...] = acc_ref[BLK:, :]

    prev8 = lambda col: pl.BlockSpec((8, CC), lambda i: (jnp.maximum(i * r8 - 1, 0), col))
    next8 = lambda col: pl.BlockSpec((8, CC), lambda i: (jnp.minimum((i + 1) * r8, t // 8 - 1), col))
    small = lambda n: pl.BlockSpec((1, n), lambda i: (0, 0))
    return pl.pallas_call(
        body, name="mixer_bwd", grid=(nt,),
        in_specs=[
            pl.BlockSpec((tq, D), lambda i: (i, 0)),
            pl.BlockSpec((8, D), lambda i: (jnp.minimum((i + 1) * r8, t // 8 - 1), 0)),
            pl.BlockSpec((tq, NP), lambda i: (i, 0)),
            prev8(O_CG // CC), prev8(O_HC // CC),
            next8(O_BG // CC), next8(O_CG // CC), next8(O_HC // CC),
            pl.BlockSpec((BLK, kvw), lambda i: (jnp.maximum(i * nb - 1, 0), O_K // kvw)),
            pl.BlockSpec((tq, NQ * HD), lambda i: (i, 0)),
            _const_spec((8, CC)), _const_spec((1, HP)), _const_spec((1, HP)),
            pl.BlockSpec(memory_space=pltpu.SMEM),
            _const_spec((1, CC)), _const_spec((1, NQ * HD)), _const_spec((MIXW, D)),
        ],
        out_specs=[
            pl.BlockSpec((tq, NMAIN), lambda i: (i, 0)),
            pl.BlockSpec((tq, kvw), lambda i: (i, 0)),
            pl.BlockSpec((BLK, kvw), lambda i: (i, 0)),
            pl.BlockSpec((8, CC), lambda i: (0, 0)), small(HP), small(HP), small(HP), small(CC), small(NQ * HD),
        ],
        out_shape=[
            jax.ShapeDtypeStruct((t, NMAIN), BF16), jax.ShapeDtypeStruct((t, kvw), F32),
            jax.ShapeDtypeStruct((nt * BLK, kvw), F32),
            jax.ShapeDtypeStruct((8, CC), F32), jax.ShapeDtypeStruct((1, HP), F32), jax.ShapeDtypeStruct((1, HP), F32),
            jax.ShapeDtypeStruct((1, HP), F32), jax.ShapeDtypeStruct((1, CC), F32),
            jax.ShapeDtypeStruct((1, NQ * HD), F32),
        ],
        scratch_shapes=[pltpu.VMEM((tq + BLK, kvw), F32)],
        compiler_params=_cparams(("arbitrary",)),
    )(dxm, dxm, proj, proj, proj, proj, proj, proj, proj, ao, cw, gq, gk, sinks, gco, gao, wo)


def _inproj_bwd(dpm, dkvm, dkvh, wpt, x, g1, dxm, tm):
    t = x.shape[0]
    kvw = 2 * NKV * HP
    nt = t // tm

    def body(dp_ref, dk_ref, dh_ref, w_ref, x_ref, g_ref, dxm_ref, dx_ref, dg_ref, dkv_ref):
        i = pl.program_id(0)

        @pl.when(i == 0)
        def _():
            dg_ref[...] = jnp.zeros_like(dg_ref)

        halo = jnp.where(i < nt - 1, dh_ref[...], 0.0)
        dkv_ref[0:tm - BLK, :] = dk_ref[0:tm - BLK, :].astype(BF16)
        dkv_ref[tm - BLK:tm, :] = (dk_ref[tm - BLK:tm, :] + halo).astype(BF16)
        dh = (jnp.dot(dp_ref[...], w_ref[0:NMAIN, :], preferred_element_type=F32)
              + jnp.dot(dkv_ref[...], w_ref[NMAIN:NP, :], preferred_element_type=F32))
        r, xh = _rms_fwd(x_ref[...], 1.0 / D)
        dg_ref[...] += jnp.sum(dh * xh, axis=0, keepdims=True)
        dx_ref[...] = dxm_ref[...] + _rms_bwd(dh, g_ref[...], xh, r, 1.0 / D)

    row = lambda w: pl.BlockSpec((tm, w), lambda i: (i, 0))
    return pl.pallas_call(
        body, name="inproj_bwd", grid=(nt,),
        in_specs=[row(NMAIN), row(kvw), pl.BlockSpec((BLK, kvw), lambda i: (jnp.minimum(i + 1, nt - 1), 0)),
                  _const_spec((NP, D)), row(D), _const_spec((1, D)), row(D)],
        out_specs=[row(D), pl.BlockSpec((1, D), lambda i: (0, 0)), row(kvw)],
        out_shape=[jax.ShapeDtypeStruct((t, D), F32), jax.ShapeDtypeStruct((1, D), F32),
                   jax.ShapeDtypeStruct((t, kvw), BF16)],
        compiler_params=_cparams(("arbitrary",)),
    )(dpm, dkvm, dkvh, wpt, x, g1, dxm)


def _rows_tile(rows, cap=512):
    for cand in range(min(rows, cap) // 16 * 16, 0, -16):
        if rows % cand == 0:
            return cand
    return rows


def _presum_halves(gs, theirs, core):
    n = len(gs)

    def body(c_ref, *refs):
        for g_ref, t_ref, o_ref in zip(refs[:n], refs[n:2 * n], refs[2 * n:]):
            o_ref[...] = (g_ref[...].astype(F32) + t_ref[...].astype(F32)).astype(BF16)

    half = lambda ta: pl.BlockSpec((None,) + ta.shape[1:], lambda k, c_ref: (k, 0, 0))
    own = lambda ta: pl.BlockSpec((None,) + ta.shape[1:], lambda k, c_ref: (k, c_ref[0], 0))
    return pl.pallas_call(
        body, name="presum",
        grid_spec=pltpu.PrefetchScalarGridSpec(
            num_scalar_prefetch=1, grid=(N_CHIPS,),
            in_specs=[own(ta) for ta in theirs] + [half(ta) for ta in theirs],
            out_specs=[half(ta) for ta in theirs]),
        out_shape=[jax.ShapeDtypeStruct(ta.shape, BF16) for ta in theirs],
        compiler_params=_cparams(("parallel",)),
    )(core, *gs, *theirs)


def _sum_chips(got, ps, chip):
    n = len(got)
    steps = 2

    def body(chip_ref, *refs):
        for c_ref, own_ref, o_ref in zip(refs[:n], refs[n:2 * n], refs[2 * n:]):
            acc = None
            for j in range(N_CHIPS):
                term = jnp.where(chip_ref[0] == j, own_ref[...], c_ref[j]).astype(F32)
                acc = term if acc is None else acc + term
            o_ref[...] = acc

    tile = lambda c: (c.shape[1] // steps, c.shape[2])
    return pl.pallas_call(
        body, name="chipsum",
        grid_spec=pltpu.PrefetchScalarGridSpec(
            num_scalar_prefetch=1, grid=(steps,),
            in_specs=[pl.BlockSpec((N_CHIPS,) + tile(c), lambda i, chip_ref: (0, i, 0)) for c in got]
            + [pl.BlockSpec((None,) + tile(c), lambda i, chip_ref: (chip_ref[0], i, 0)) for c in got],
            out_specs=[pl.BlockSpec(tile(c), lambda i, chip_ref: (i, 0)) for c in got]),
        out_shape=[jax.ShapeDtypeStruct(c.shape[1:], F32) for c in got],
        compiler_params=_cparams(("parallel",)),
    )(chip, *got, *ps)


def _adamw(w, g, m, v, name):
    rows, cols = w.shape
    tr = _rows_tile(rows)
    c1 = 1.0 - ADAM_B1 ** ADAM_STEP
    c2 = 1.0 - ADAM_B2 ** ADAM_STEP

    def body(w_ref, g_ref, m_ref, v_ref, d_ref, mo_ref, vo_ref):
        gv = g_ref[...]
        mn = ADAM_B1 * m_ref[...] + (1.0 - ADAM_B1) * gv
        vn = ADAM_B2 * v_ref[...] + (1.0 - ADAM_B2) * (gv * gv)
        mo_ref[...] = mn
        vo_ref[...] = vn
        d_ref[...] = -ADAM_LR * ((mn / c1) / (jnp.sqrt(vn / c2) + ADAM_EPS) + ADAM_WD * w_ref[...])

    spec = pl.BlockSpec((tr, cols), lambda i: (i, 0))
    sds = jax.ShapeDtypeStruct((rows, cols), F32)
    return pl.pallas_call(
        body, name=name, grid=(rows // tr,), in_specs=[spec] * 4, out_specs=[spec] * 3, out_shape=[sds] * 3,
        compiler_params=_cparams(("parallel",)),
    )(w, g, m, v)


def _place():
    x, y, c = lax.axis_index("x"), lax.axis_index("y"), lax.axis_index("c")
    chips = [(1 - x, y), (x, 1 - y), (1 - x, 1 - y)]
    return x, y, c, chips


ANY = pl.BlockSpec(memory_space=pl.ANY)
DMA_ROWS = 64


def _pieces(shape):
    rows = shape[-2]
    step = DMA_ROWS if rows % DMA_ROWS == 0 else rows
    lead = [()]
    for n in shape[:-2]:
        lead = [i + (k,) for i in lead for k in range(n)]
    return [i + (pl.ds(r0, step),) for i in lead for r0 in range(0, rows, step)]


def _start_pieces(make, src, dst):
    for idx in _pieces(src.shape):
        make(src.at[idx], dst.at[idx]).start()


def _gather_body(srcs, outs, sems, layer, start):
    nw = len(srcs)
    ssem, rsem, fssem, frsem = sems
    x, y, c, chips = _place()
    kme = 2 * x + y

    def plane(j, w, to):
        return lambda s, d: pltpu.make_async_remote_copy(
            src_ref=s, dst_ref=d, send_sem=ssem.at[j, w], recv_sem=rsem.at[j, w], device_id=to,
            device_id_type=MESH)

    def passed(j, w):
        return lambda s, d: pltpu.make_async_remote_copy(
            src_ref=s, dst_ref=d, send_sem=fssem.at[j, w], recv_sem=frsem.at[j, w],
            device_id=(x, y, 1 - c), device_id_type=MESH)

    @pl.when(c == layer)
    def _():
        for j, (px, py) in enumerate(chips):
            for w in range(nw):
                start(plane(j, w, (px, py, c)), srcs[w], outs[w].at[kme])
        for j, (px, py) in enumerate(chips):
            for w in range(nw):
                got = outs[w].at[2 * px + py]
                plane(j, w, (px, py, c))(got, got).wait_recv()
                start(passed(j, w), got, got)
        for j, (px, py) in enumerate(chips):
            for w in range(nw):
                got = outs[w].at[2 * px + py]
                plane(j, w, (px, py, c))(got, got).wait_send()
                passed(j, w)(got, got).wait_send()

    @pl.when(c != layer)
    def _():
        for j, (px, py) in enumerate(chips):
            for w in range(nw):
                got = outs[w].at[2 * px + py]
                passed(j, w)(got, got).wait_recv()


def _handshake(peers):
    barrier = pltpu.get_barrier_semaphore()
    for peer in peers:
        pl.semaphore_signal(barrier, inc=1, device_id=peer, device_id_type=MESH)
    pl.semaphore_wait(barrier, len(peers))


def _handshake_all():
    x, y, c, _ = _place()
    _handshake([(x ^ (r >> 2), y ^ ((r >> 1) & 1), c ^ (r & 1)) for r in range(1, 8)])


def _gather_layer_async(blocks, layer, name, collective_id):
    hbm = pltpu.MemorySpace.HBM
    srcs = [jax.new_ref(b, memory_space=hbm) for b in blocks]
    outs = [jax.empty_ref(jax.ShapeDtypeStruct((N_CHIPS,) + b.shape, b.dtype), memory_space=hbm) for b in blocks]

    @pl.kernel(mesh=plsc.ScalarSubcoreMesh(axis_name="seq", num_cores=1), name=name,
               scratch_types=[pltpu.SemaphoreType.DMA((3, len(blocks)))] * 4,
               compiler_params=pltpu.CompilerParams(collective_id=collective_id))
    def launch(*sems):
        _handshake_all()
        _gather_body(srcs, outs, sems, layer, lambda make, s, d: make(s, d).start())

    launch()
    return [o[...] for o in outs]


def _swap_siblings(arrs, halves, name, collective_id=None):
    nw = len(arrs)
    out_sds = [jax.ShapeDtypeStruct((a.shape[0], a.shape[1] // 2, a.shape[2]) if halves else a.shape, a.dtype)
               for a in arrs]

    def exchange(srcs, outs, ssem, rsem, start):
        x, y, c, _ = _place()

        def give(w):
            return lambda s, d: pltpu.make_async_remote_copy(
                src_ref=s, dst_ref=d, send_sem=ssem.at[w], recv_sem=rsem.at[w], device_id=(x, y, 1 - c),
                device_id_type=MESH)

        for w in range(nw):
            hr = outs[w].shape[1]
            start(give(w), srcs[w].at[:, pl.ds((1 - c) * hr, hr)] if halves else srcs[w], outs[w])
        for w in range(nw):
            give(w)(outs[w], outs[w]).wait()

    if collective_id is None:
        def body(*refs):
            exchange(refs[:nw], refs[nw:2 * nw], *refs[2 * nw:], _start_pieces)

        return pl.pallas_call(
            body, name=name, in_specs=[ANY] * nw, out_specs=[ANY] * nw, out_shape=out_sds,
            scratch_shapes=[pltpu.SemaphoreType.DMA((nw,))] * 2,
            compiler_params=_cparams(has_side_effects=True),
        )(*arrs)

    hbm = pltpu.MemorySpace.HBM
    srcs = [jax.new_ref(a, memory_space=hbm) for a in arrs]
    outs = [jax.empty_ref(sds, memory_space=hbm) for sds in out_sds]

    @pl.kernel(mesh=plsc.ScalarSubcoreMesh(axis_name="seq", num_cores=1), name=name,
               scratch_types=[pltpu.SemaphoreType.DMA((nw,))] * 2,
               compiler_params=pltpu.CompilerParams(collective_id=collective_id))
    def launch(ssem, rsem):
        x, y, c, _ = _place()
        _handshake([(x, y, 1 - c)])
        exchange(srcs, outs, ssem, rsem, lambda make, s, d: make(s, d).start())

    launch()
    return [o[...] for o in outs]


def _scatter_body(srcs, outs, sems, start):
    nw = len(srcs)
    ssem, rsem = sems
    x, y, c, chips = _place()
    kme = 2 * x + y

    def give(j, w, to):
        return lambda s, d: pltpu.make_async_remote_copy(
            src_ref=s, dst_ref=d, send_sem=ssem.at[j, w], recv_sem=rsem.at[j, w], device_id=to,
            device_id_type=MESH)

    for j, (px, py) in enumerate(chips):
        for w in range(nw):
            start(give(j, w, (px, py, c)), srcs[w].at[2 * px + py], outs[w].at[kme])
    for j, (px, py) in enumerate(chips):
        for w in range(nw):
            got = outs[w].at[2 * px + py]
            give(j, w, (px, py, c))(got, got).wait_recv()
    for j, (px, py) in enumerate(chips):
        for w in range(nw):
            sent = srcs[w].at[2 * px + py]
            give(j, w, (px, py, c))(sent, sent).wait_send()


def _scatter_chips_async(ps, name, collective_id):
    hbm = pltpu.MemorySpace.HBM
    srcs = [jax.new_ref(p, memory_space=hbm) for p in ps]
    outs = [jax.empty_ref(jax.ShapeDtypeStruct(p.shape, p.dtype), memory_space=hbm) for p in ps]

    @pl.kernel(mesh=plsc.ScalarSubcoreMesh(axis_name="seq", num_cores=1), name=name,
               scratch_types=[pltpu.SemaphoreType.DMA((3, len(ps)))] * 2,
               compiler_params=pltpu.CompilerParams(collective_id=collective_id))
    def launch(*sems):
        _handshake_all()
        _scatter_body(srcs, outs, sems, lambda make, s, d: make(s, d).start())

    launch()
    return [o[...] for o in outs]


def _allreduce_small(v):
    rows = v.shape[0]

    def body(v_ref, o_ref, buf, ssem, rsem):
        x, y, c, _ = _place()
        me = 4 * x + 2 * y + c
        buf[me] = v_ref[...]
        sends = []
        for r in range(1, 8):
            peer = (x ^ (r >> 2), y ^ ((r >> 1) & 1), c ^ (r & 1))
            cp = pltpu.make_async_remote_copy(
                src_ref=v_ref, dst_ref=buf.at[me], send_sem=ssem.at[r - 1], recv_sem=rsem.at[r - 1],
                device_id=peer, device_id_type=MESH)
            cp.start()
            sends.append(cp)
        for r in range(1, 8):
            src = me ^ r
            pltpu.make_async_remote_copy(
                src_ref=v_ref, dst_ref=buf.at[src], send_sem=ssem.at[r - 1], recv_sem=rsem.at[r - 1],
                device_id=(x, y, c), device_id_type=MESH).wait_recv()
        for cp in sends:
            cp.wait_send()
        acc = buf[0]
        for d in range(1, 8):
            acc = acc + buf[d]
        o_ref[...] = acc

    vm = pl.BlockSpec(memory_space=pltpu.VMEM)
    return pl.pallas_call(
        body, name="allreduce_small", in_specs=[vm], out_specs=vm,
        out_shape=jax.ShapeDtypeStruct(v.shape, F32),
        scratch_shapes=[pltpu.VMEM((8, rows, 128), F32), pltpu.SemaphoreType.DMA((7,)),
                        pltpu.SemaphoreType.DMA((7,))],
        compiler_params=_cparams(has_side_effects=True),
    )(v)


def _t(w):
    return jnp.swapaxes(w, -1, -2)


def _count(shape):
    n = 1
    for s in shape:
        n *= s
    return n


def _pack_rows(arrs):
    flat = [jnp.pad(a.reshape(-1), (0, (-_count(a.shape)) % 128)) for a in arrs]
    v = jnp.concatenate(flat)
    rows = -(-v.shape[0] // (8 * 128)) * 8
    return jnp.pad(v, (0, rows * 128 - v.shape[0])).reshape(rows, 128)


def kernel(x, norm1_g, w_in, conv_w, q_norm_g, k_norm_g, sinks, conv_out_g, attn_out_g, w_o, norm2_g, w_gate, w_up, w_down, loss_target, m_norm1_g, m_w_in, m_conv_w, m_q_norm_g, m_k_norm_g, m_sinks, m_conv_out_g, m_attn_out_g, m_w_o, m_norm2_g, m_w_gate, m_w_up, m_w_down, v_norm1_g, v_w_in, v_conv_w, v_q_norm_g, v_k_norm_g, v_sinks, v_conv_out_g, v_attn_out_g, v_w_o, v_norm2_g, v_w_gate, v_w_up, v_w_down):
    depth = w_in.shape[0]
    t = x.shape[1]
    xs = x.reshape(t, D)
    tgt = loss_target.reshape(t, D)
    xi, yi = lax.axis_index("x"), lax.axis_index("y")
    kme = 2 * xi + yi
    tm = min(512, t)
    tq = min(512, t)
    tf = min(256, t)
    tw = min(1024, t)

    cwp = jnp.pad(conv_w.reshape(depth * 3, CC // N_CHIPS), ((0, 8 - depth * 3), (0, 0)))
    own_f = [jnp.concatenate([_t(w_gate[l]), _t(w_up[l]), w_down[l]], axis=0).astype(BF16) for l in range(depth)]
    own_o = [w_o[l].astype(BF16) for l in range(depth)]
    own_i = [_t(w_in[l]).astype(BF16) for l in range(depth)]
    mine = lambda got, own: lax.dynamic_update_index_in_dim(got, own, kme, 0)
    (got_i0,) = _gather_layer_async([own_i[0]], 0, "gather_in0_seq", collective_id=14)
    got_ocw = _gather_layer_async([own_o[0], cwp], 0, "gather_o0_seq", collective_id=15)
    got_i0, own_f, own_o, own_i = lax.optimization_barrier((got_i0, own_f, own_o, own_i))
    gf0_in = lax.optimization_barrier((own_f[0], got_i0))[0]
    (got_f0,) = _gather_layer_async([gf0_in], 0, "gather_ffn0_seq", collective_id=6)

    chip = kme.reshape(1).astype(jnp.int32)

    def layer_params(l, got_o, cw_full):
        return dict(
            wo=mine(got_o, own_o[l]).reshape(MIXW, D),
            cw=jnp.pad(cw_full[l], ((0, 5), (0, 0))),
            g1=norm1_g[l].reshape(1, D), g2=norm2_g[l].reshape(1, D),
            gq=jnp.pad(q_norm_g[l], (0, HP - HD)).reshape(1, HP), gk=jnp.pad(k_norm_g[l], (0, HP - HD)).reshape(1, HP),
            sk=sinks[l].reshape(1, NQ), gco=conv_out_g[l].reshape(1, CC),
            gao=attn_out_g[l].reshape(1, NQ * HD))

    saved, layers = [], []
    cur = xs
    for l in range(depth):
        x_in = cur
        if l == 0:
            got_i = got_i0
        else:
            got_f1, got_o, got_i = lax.optimization_barrier((got_l1, cur))[0]
        proj, h, wpt = _inproj_fwd(cur, norm1_g[l].reshape(1, D), got_i, own_i[l], chip, tm)
        if l == 0:
            got_o, got_cw = lax.optimization_barrier((got_ocw, proj))[0]
            cw_full = mine(got_cw, cwp).transpose(1, 0, 2).reshape(8, CC)[:depth * 3].reshape(depth, 3, CC)
        p = layer_params(l, got_o, cw_full)
        p["wpt"] = wpt
        xm, mix, ao = _mixer_fwd(proj, cur, p["cw"], p["gq"], p["gk"], p["sk"], p["gco"], p["gao"], p["wo"], tq)
        if l == 0:
            got_f0 = lax.optimization_barrier((got_f0, xm))[0]
            l1_in = lax.optimization_barrier(([own_f[1], own_o[1], own_i[1]], got_f0))[0]
            got_l1 = _gather_layer_async(l1_in, 1, "gather_layer1_seq", collective_id=1)
        p["gf"] = mine(got_f0 if l == 0 else got_f1, own_f[l])
        layers.append(p)
        if l < depth - 1:
            cur, a, b, h2 = _ffn_fwd(xm, p["g2"], p["gf"], tm)
        else:
            lpart, dy, a, b, h2 = _ffn_fwd(xm, p["g2"], p["gf"], tm, tgt)
        saved.append(dict(x=x_in, proj=proj, h=h, xm=xm, mix=mix, ao=ao, a=a, b=b, h2=h2))

    ci = lax.axis_index("c")
    core = ci.reshape(1).astype(jnp.int32)
    rbig = [dict() for _ in range(depth)]
    gsmall = [None] * depth

    def after_(vals, after):
        return vals if after is None else lax.optimization_barrier((vals, after))[0]

    def reduce_1(gs, tag, ids):
        return gs, _swap_siblings(gs, True, f"swap_halves_{tag}_seq", ids[0]), tag, ids

    def reduce_2(state, after):
        gs, theirs, tag, ids = state
        ps = _presum_halves(gs, after_(theirs, after), core)
        return ps, _scatter_chips_async(ps, f"scatter_{tag}_seq", ids[1]), tag, ids

    def reduce_3(state, after):
        ps, got, tag, ids = state
        r_mine = _sum_chips(after_(got, after), ps, chip)
        return r_mine, _swap_siblings(r_mine, False, f"swap_reduced_{tag}" + ("_seq" if ids[2] else ""), ids[2])

    def reduce_4(state, after):
        r_mine, r_theirs = state
        return [jnp.where(ci == 0, jnp.concatenate([a, b], axis=0), jnp.concatenate([b, a], axis=0))
                for a, b in zip(r_mine, after_(r_theirs, after))]

    ids = {"ffn1": (7, 4, 8), "in1": (9, 5, 10), "ffn0": (11, 2, 12), "in0": (13, 3, None)}
    in_2 = scattering = None
    handed = {}
    for l in reversed(range(depth)):
        p, s = layers[l], saved[l]
        dxm, da, db, hm, dg2 = _ffn_bwd(dy, s["xm"], p["g2"], s["a"], s["b"], p["gf"], tf)
        if in_2 is not None:
            in_2 = reduce_2(in_2, dxm)
        g_wg = _wgrad_blocks(da, s["h2"], tw, "wgrad_gate")
        g_wu = _wgrad_blocks(db, s["h2"], tw, "wgrad_up")
        g_wd = _wgrad_blocks(hm, dy, tw, "wgrad_down")
        if in_2 is not None:
            handed[f"in{l + 1}"] = reduce_3(in_2, g_wd)
        ffn_1 = reduce_1([g_wg, g_wu, g_wd], f"ffn{l}", ids[f"ffn{l}"])
        dpm, dkvm, dkvh, dcw, dgq, dgk, dsk, dgco, dgao = _mixer_bwd(
            after_(dxm, scattering), s["proj"], s["ao"], p["cw"], p["gq"], p["gk"], p["sk"], p["gco"], p["gao"],
            p["wo"], tq)
        ffn_2 = reduce_2(ffn_1, dpm)
        scattering = ffn_2[0]
        g_o = _wgrad(s["mix"], dxm, tw, "wgrad_o")
        dx, dg1, dkv = _inproj_bwd(dpm, dkvm, dkvh, p["wpt"], s["x"], p["g1"], dxm, tq)
        g_in = _wgrad_in(dpm, dkv, s["h"], tw)
        dy = dx
        gsmall[l] = dict(g1=dg1, cw=dcw[:3], gq=dgq[0, :HD], gk=dgk[0, :HD], sk=dsk[0, :NQ], gco=dgco,
                         gao=dgao, g2=dg2)
        above = handed.get(f"ffn{l + 1}")
        handed[f"ffn{l}"] = reduce_3(ffn_2, g_in if above is None else (g_in, above[0]))
        in_2 = reduce_1([g_in.reshape(N_CHIPS, -1, D), g_o.reshape(N_CHIPS, -1, D)], f"in{l}", ids[f"in{l}"])
    grad_x = dy.reshape(x.shape)

    small_shapes = dict(g1=(D,), cw=(3, CC), gq=(HD,), gk=(HD,), sk=(NQ,), gco=(CC,), gao=(NQ * HD,), g2=(D,))
    red = _allreduce_small(_pack_rows([gsmall[l][n] for l in range(depth) for n in small_shapes]
                                      + [lpart[0:1, 0:1]])).reshape(-1)
    red_small, offs = {n: [] for n in small_shapes}, 0
    for l in range(depth):
        for n, shp in small_shapes.items():
            cnt = _count(shp)
            red_small[n].append(red[offs:offs + cnt].reshape(shp))
            offs += -(-cnt // 128) * 128
    loss = red[offs]
    g_small = {n: jnp.stack(v) for n, v in red_small.items()}
    g_cw = lax.dynamic_slice_in_dim(g_small["cw"], kme * (CC // N_CHIPS), CC // N_CHIPS, axis=2)

    weights = [norm1_g, w_in, conv_w, q_norm_g, k_norm_g, sinks, conv_out_g, attn_out_g, w_o, norm2_g, w_gate,
               w_up, w_down]
    moms = [m_norm1_g, m_w_in, m_conv_w, m_q_norm_g, m_k_norm_g, m_sinks, m_conv_out_g, m_attn_out_g, m_w_o,
            m_norm2_g, m_w_gate, m_w_up, m_w_down]
    vars_ = [v_norm1_g, v_w_in, v_conv_w, v_q_norm_g, v_k_norm_g, v_sinks, v_conv_out_g, v_attn_out_g, v_w_o,
             v_norm2_g, v_w_gate, v_w_up, v_w_down]
    n_w = len(weights)
    big_idx = dict(zip(("in", "o", "g", "u", "d"), (1, 8, 10, 11, 12)))
    small_idx = [n for n in range(n_w) if n not in big_idx.values()]
    grads, deltas, new_m, new_v = [None] * n_w, [None] * n_w, [None] * n_w, [None] * n_w
    for n, g in zip(small_idx, (g_small["g1"], g_cw, g_small["gq"], g_small["gk"], g_small["sk"], g_small["gco"],
                                g_small["gao"], g_small["g2"])):
        grads[n] = g

    def update_big(name):
        n = big_idx[name]
        g = jnp.stack([rbig[l][name] for l in range(depth)])
        flip = g.shape != weights[n].shape
        rows2d = lambda a3: (_t(a3) if flip else a3).reshape(-1, D)
        res = _adamw(rows2d(weights[n]), g.reshape(-1, D), rows2d(moms[n]), rows2d(vars_[n]), f"adamw_{n}")
        res = [g] + [r.reshape(g.shape) for r in res]
        grads[n], deltas[n], new_m[n], new_v[n] = [_t(r) for r in res] if flip else res

    for l in range(depth):
        rbig[l]["g"], rbig[l]["u"], rbig[l]["d"] = reduce_4(handed[f"ffn{l}"], red)
    rbig[1]["in"], rbig[1]["o"] = reduce_4(handed["in1"], red)
    update_big("g")
    in_2 = reduce_2(in_2, new_v[big_idx["g"]])
    update_big("u")
    update_big("d")
    rbig[0]["in"], rbig[0]["o"] = reduce_4(reduce_3(in_2, new_v[big_idx["d"]]), None)
    for name in ("in", "o"):
        update_big(name)
    res = _adamw(*[_pack_rows([arrs[n] for n in small_idx]) for arrs in (weights, grads, moms, vars_)],
                 "adamw_small")
    offs = 0
    for n in small_idx:
        shp = weights[n].shape
        cnt = _count(shp)
        deltas[n], new_m[n], new_v[n] = [r.reshape(-1)[offs:offs + cnt].reshape(shp) for r in res]
        offs += -(-cnt // 128) * 128
    return (loss, grad_x, *grads, *deltas, *new_m, *new_v)
```

```python
import jax
import jax.numpy as jnp
from jax import lax
from jax.experimental import pallas as pl
from jax.experimental.pallas import tpu as pltpu
from jax.experimental.pallas import tpu_sc as plsc

F32 = jnp.float32
BF16 = jnp.bfloat16

D = 1024
CC = 512
NQ = 8
NKV = 2
HD = 64
HP = 128
GRP = NQ // NKV
FF = 2816
FFB = FF // 4
BLK = 128
EPS = 1e-6
NEG = -1e30
SCALE = HD ** -0.5
O_BG, O_CG, O_HC, O_Q = 0, CC, 2 * CC, 3 * CC
O_K = O_Q + NQ * HP
O_V = O_K + NKV * HP
NP = O_V + NKV * HP
NMAIN = O_K
MIXW = CC + NQ * HD
N_CHIPS = 4
VMEM_LIMIT = 56 * 1024 * 1024
MESH = pl.DeviceIdType.MESH

ADAM_LR, ADAM_B1, ADAM_B2, ADAM_EPS, ADAM_WD, ADAM_STEP = 0.001, 0.9, 0.999, 1e-08, 0.01, 10


def _cparams(sem=None, **kw):
    if sem is not None:
        kw["dimension_semantics"] = sem
    return pltpu.CompilerParams(vmem_limit_bytes=VMEM_LIMIT, **kw)


def _const_spec(shape):
    nd = len(shape)
    return pl.BlockSpec(shape, lambda *_: (0,) * nd, pipeline_mode=pl.Buffered(1))


def _nt(a, b):
    return lax.dot_general(a, b, (((1,), (1,)), ((), ())), preferred_element_type=F32)


def _tn(a, b):
    return lax.dot_general(a, b, (((0,), (0,)), ((), ())), preferred_element_type=F32)


def _rms_fwd(x, inv_n):
    r = lax.rsqrt(jnp.sum(x * x, axis=-1, keepdims=True) * inv_n + EPS)
    return r, x * r


def _rms_bwd(dy, g, xh, r, inv_n):
    dxh = dy * g
    return r * (dxh - xh * (jnp.sum(dxh * xh, axis=-1, keepdims=True) * inv_n))


W_IN_ROWS = 3 * CC + (NQ + 2 * NKV) * HD
W_IN_BLOCK = W_IN_ROWS // N_CHIPS


def _padded_row(row):
    return row + max(row - O_Q, 0) // HD * (HP - HD)


def _w_in_pieces(k):
    first = k * W_IN_BLOCK
    plain = min(max(O_Q - first, 0), W_IN_BLOCK)
    pieces = [(0, first, plain)] if plain else []
    return pieces + [(r, _padded_row(first + r), HD) for r in range(plain, W_IN_BLOCK, HD)]


def _inproj_fwd(x, g1, gi, own_i, chip, tm):
    t = x.shape[0]

    def body(chip_ref, x_ref, g_ref, gi_ref, own_ref, p_ref, h_ref, w_ref, sem):
        @pl.when(pl.program_id(0) == 0)
        def _():
            for k in range(N_CHIPS):
                for src, dst, rows in _w_in_pieces(k):
                    @pl.when(chip_ref[0] == k)
                    def _():
                        pltpu.make_async_copy(own_ref.at[pl.ds(src, rows)], w_ref.at[pl.ds(dst, rows)], sem).start()

                    @pl.when(chip_ref[0] != k)
                    def _():
                        pltpu.make_async_copy(gi_ref.at[k, pl.ds(src, rows)], w_ref.at[pl.ds(dst, rows)], sem).start()
            for slot in range(NQ + 2 * NKV):
                w_ref[O_Q + slot * HP + HD:O_Q + (slot + 1) * HP, :] = jnp.zeros((HP - HD, D), BF16)
            landed = w_ref.at[pl.ds(0, W_IN_ROWS)]
            pltpu.make_async_copy(landed, landed, sem).wait()

        _, xh = _rms_fwd(x_ref[...], 1.0 / D)
        h = (xh * g_ref[...]).astype(BF16)
        h_ref[...] = h
        p_ref[...] = _nt(h, w_ref[...])

    const = lambda shape: pl.BlockSpec(shape, lambda i, c: (0,) * len(shape))
    return pl.pallas_call(
        body, name="inproj_fwd",
        grid_spec=pltpu.PrefetchScalarGridSpec(
            num_scalar_prefetch=1, grid=(t // tm,),
            in_specs=[pl.BlockSpec((tm, D), lambda i, c: (i, 0)), const((1, D)), ANY, ANY],
            out_specs=[pl.BlockSpec((tm, NP), lambda i, c: (i, 0)), pl.BlockSpec((tm, D), lambda i, c: (i, 0)),
                       const((NP, D))],
            scratch_shapes=[pltpu.SemaphoreType.DMA(())]),
        out_shape=[jax.ShapeDtypeStruct((t, NP), F32), jax.ShapeDtypeStruct((t, D), BF16),
                   jax.ShapeDtypeStruct((NP, D), BF16)],
        compiler_params=_cparams(("arbitrary",)),
    )(chip, x, g1, gi, own_i)


def _band_mask():
    r_io = lax.broadcasted_iota(jnp.int32, (BLK, 2 * BLK), 0)
    c_io = lax.broadcasted_iota(jnp.int32, (BLK, 2 * BLK), 1)
    return (c_io > r_io) & (c_io <= r_io + BLK), c_io


def _conv_taps(uf, n):
    u1 = pltpu.roll(uf, 1, 0)[8:8 + n]
    u2 = pltpu.roll(uf, 2, 0)[8:8 + n]
    return u1, u2


def _attn_probs(qs, kband, sink, valid):
    s = jnp.where(valid, _nt(qs, kband), NEG)
    m = jnp.maximum(jnp.max(s, axis=-1, keepdims=True), sink)
    p = jnp.exp(s - m)
    es = jnp.exp(sink - m)
    inv = 1.0 / (jnp.sum(p, axis=-1, keepdims=True) + es)
    return p * inv, es * inv


def _norm_keys(kraw, gk):
    out = []
    for h in range(NKV):
        kh = kraw[:, h * HP:(h + 1) * HP]
        rk, khat = _rms_fwd(kh, 1.0 / HD)
        out.append((khat, rk, (khat * gk).astype(BF16)))
    return out


def _mixer_fwd(proj, x, cw, gq, gk, sinks, gco, gao, wo, tq):
    t = proj.shape[0]
    nb = tq // BLK
    r8 = tq // 8

    def body(p_ref, cgp_ref, hcp_ref, kvp_ref, x_ref, cw_ref, gq_ref, gk_ref, sk_ref, gco_ref, gao_ref,
             wo_ref, xm_ref, mix_ref, ao_ref, aop_ref):
        i = pl.program_id(0)
        cg = p_ref[:, O_CG:O_CG + CC]
        hc = p_ref[:, O_HC:O_HC + CC]
        u = cg * hc
        up = jnp.where(i > 0, cgp_ref[...] * hcp_ref[...], 0.0)
        u1, u2 = _conv_taps(jnp.concatenate([up, u], axis=0), tq)
        y = cw_ref[0:1, :] * u2 + cw_ref[1:2, :] * u1 + cw_ref[2:3, :] * u
        co = p_ref[:, O_BG:O_BG + CC] * y
        _, coh = _rms_fwd(co, 1.0 / CC)
        cn = coh * gco_ref[...]
        kraw = jnp.concatenate([kvp_ref[:, 0:NKV * HP], p_ref[:, O_K:O_K + NKV * HP]], axis=0)
        vraw = jnp.concatenate([kvp_ref[:, NKV * HP:], p_ref[:, O_V:O_V + NKV * HP]], axis=0)
        keys = _norm_keys(kraw, gk_ref[...])
        vb = [vraw[:, h * HP:(h + 1) * HP].astype(BF16) for h in range(NKV)]
        base_valid, c_io = _band_mask()
        gqs = gq_ref[...] * SCALE
        for b in range(nb):
            lo = jnp.where(i * nb + b == 0, BLK, 0)
            valid = base_valid & (c_io >= lo)
            for g in range(NQ):
                h = g // GRP
                qg = p_ref[b * BLK:(b + 1) * BLK, O_Q + g * HP:O_Q + (g + 1) * HP]
                _, qh = _rms_fwd(qg, 1.0 / HD)
                qs = (qh * gqs).astype(BF16)
                pr, _ = _attn_probs(qs, keys[h][2][b * BLK:b * BLK + 2 * BLK], sk_ref[0, g], valid)
                aop_ref[b * BLK:(b + 1) * BLK, g * HP:(g + 1) * HP] = jnp.dot(
                    pr.astype(BF16), vb[h][b * BLK:b * BLK + 2 * BLK], preferred_element_type=F32)
        for j in range(NQ // 2):
            ao_ref[:, j * HP:(j + 1) * HP] = (aop_ref[:, 2 * j * HP:(2 * j + 1) * HP]
                                              + pltpu.roll(aop_ref[:, (2 * j + 1) * HP:(2 * j + 2) * HP], HD, 1))
        _, aoh = _rms_fwd(ao_ref[...], 1.0 / (NQ * HD))
        an = aoh * gao_ref[...]
        mix = jnp.concatenate([cn, an], axis=1).astype(BF16)
        mix_ref[...] = mix
        xm_ref[...] = x_ref[...] + jnp.dot(mix, wo_ref[...], preferred_element_type=F32)

    prev8 = lambda col: pl.BlockSpec((8, CC), lambda i: (jnp.maximum(i * r8 - 1, 0), col))
    return pl.pallas_call(
        body, name="mixer_fwd", grid=(t // tq,),
        in_specs=[
            pl.BlockSpec((tq, NP), lambda i: (i, 0)),
            prev8(O_CG // CC), prev8(O_HC // CC),
            pl.BlockSpec((BLK, 2 * NKV * HP), lambda i: (jnp.maximum(i * nb - 1, 0), O_K // (2 * NKV * HP))),
            pl.BlockSpec((tq, D), lambda i: (i, 0)),
            _const_spec((8, CC)), _const_spec((1, HP)), _const_spec((1, HP)),
            pl.BlockSpec(memory_space=pltpu.SMEM),
            _const_spec((1, CC)), _const_spec((1, NQ * HD)), _const_spec((MIXW, D)),
        ],
        out_specs=[pl.BlockSpec((tq, D), lambda i: (i, 0)), pl.BlockSpec((tq, MIXW), lambda i: (i, 0)),
                   pl.BlockSpec((tq, NQ * HD), lambda i: (i, 0))],
        out_shape=[jax.ShapeDtypeStruct((t, D), F32), jax.ShapeDtypeStruct((t, MIXW), BF16),
                   jax.ShapeDtypeStruct((t, NQ * HD), F32)],
        scratch_shapes=[pltpu.VMEM((tq, NQ * HP), F32)],
        compiler_params=_cparams(("parallel",)),
    )(proj, proj, proj, proj, x, cw, gq, gk, sinks, gco, gao, wo)


def _ffn_weight_specs():
    return [pl.BlockSpec((N_CHIPS, FFB, D), lambda i, j=j: (0, j, 0), pipeline_mode=pl.Buffered(1))
            for j in range(3)]


def _ffn_fwd(xm, g2, gf, tm, tgt=None):
    t = xm.shape[0]
    last = tgt is not None

    def body(x_ref, g_ref, wg_ref, wu_ref, wd_ref, *rest):
        t_ref, rest = (rest[0], rest[1:]) if last else (None, rest)
        l_ref, rest = (rest[0], rest[1:]) if last else (None, rest)
        xo_ref, a_ref, b_ref, h2_ref = rest
        xv = x_ref[...]
        _, xh = _rms_fwd(xv, 1.0 / D)
        h2 = (xh * g_ref[...]).astype(BF16)
        h2_ref[...] = h2
        acc = xv
        for k in range(N_CHIPS):
            a = _nt(h2, wg_ref[k])
            b = _nt(h2, wu_ref[k])
            a_ref[k] = a.astype(BF16)
            b_ref[k] = b.astype(BF16)
            hm = (a * jax.nn.sigmoid(a) * b).astype(BF16)
            acc = acc + jnp.dot(hm, wd_ref[k], preferred_element_type=F32)
        if last:
            @pl.when(pl.program_id(0) == 0)
            def _():
                l_ref[...] = jnp.zeros_like(l_ref)

            e = acc - t_ref[...]
            xo_ref[...] = e * (1.0 / D)
            l_ref[...] += jnp.sum(jnp.sum(e * e, axis=-1, keepdims=True), axis=0, keepdims=True) * (0.5 / D)
        else:
            xo_ref[...] = acc

    row = lambda w: pl.BlockSpec((tm, w), lambda i: (i, 0))
    blk = pl.BlockSpec((N_CHIPS, tm, FFB), lambda i: (0, i, 0))
    bsd = jax.ShapeDtypeStruct((N_CHIPS, t, FFB), BF16)
    return pl.pallas_call(
        body, name="ffn_fwd_loss" if last else "ffn_fwd", grid=(t // tm,),
        in_specs=[row(D), _const_spec((1, D))] + _ffn_weight_specs() + ([row(D)] if last else []),
        out_specs=([pl.BlockSpec((8, 128), lambda i: (0, 0))] if last else []) + [row(D), blk, blk, row(D)],
        out_shape=([jax.ShapeDtypeStruct((8, 128), F32)] if last else [])
        + [jax.ShapeDtypeStruct((t, D), F32), bsd, bsd, jax.ShapeDtypeStruct((t, D), BF16)],
        compiler_params=_cparams(("arbitrary" if last else "parallel",)),
    )(*((xm, g2, gf, gf, gf) + ((tgt,) if last else ())))


def _ffn_bwd(dy, xm, g2, a, b, gf, tm):
    t = dy.shape[0]

    def body(dy_ref, x_ref, g_ref, a_ref, b_ref, wg_ref, wu_ref, wd_ref, dx_ref, da_ref, db_ref, hm_ref, dg_ref):
        @pl.when(pl.program_id(0) == 0)
        def _():
            dg_ref[...] = jnp.zeros_like(dg_ref)

        dyv = dy_ref[...]
        dyb = dyv.astype(BF16)
        dh2 = jnp.zeros_like(dyv)
        for k in range(N_CHIPS):
            dhm = _nt(dyb, wd_ref[k])
            av = a_ref[k].astype(F32)
            bv = b_ref[k].astype(F32)
            sig = jax.nn.sigmoid(av)
            sil = av * sig
            hm_ref[k] = (sil * bv).astype(BF16)
            da = (dhm * bv * (sig * (1.0 + av * (1.0 - sig)))).astype(BF16)
            db = (dhm * sil).astype(BF16)
            da_ref[k] = da
            db_ref[k] = db
            dh2 = (dh2 + jnp.dot(da, wg_ref[k], preferred_element_type=F32)
                   + jnp.dot(db, wu_ref[k], preferred_element_type=F32))
        r, xh = _rms_fwd(x_ref[...], 1.0 / D)
        dg_ref[...] += jnp.sum(dh2 * xh, axis=0, keepdims=True)
        dx_ref[...] = dyv + _rms_bwd(dh2, g_ref[...], xh, r, 1.0 / D)

    row = lambda w: pl.BlockSpec((tm, w), lambda i: (i, 0))
    blk = pl.BlockSpec((N_CHIPS, tm, FFB), lambda i: (0, i, 0))
    bsd = jax.ShapeDtypeStruct((N_CHIPS, t, FFB), BF16)
    return pl.pallas_call(
        body, name="ffn_bwd", grid=(t // tm,),
        in_specs=[row(D), row(D), _const_spec((1, D)), blk, blk] + _ffn_weight_specs(),
        out_specs=[row(D), blk, blk, blk, pl.BlockSpec((1, D), lambda i: (0, 0))],
        out_shape=[jax.ShapeDtypeStruct((t, D), F32), bsd, bsd, bsd, jax.ShapeDtypeStruct((1, D), F32)],
        compiler_params=_cparams(("arbitrary",)),
    )(dy, xm, g2, a, b, gf, gf, gf)


def _wgrad_blocks(a, b, tt, name):
    _, t, rows = a.shape
    cols = b.shape[1]
    nsteps = t // tt

    def body(a_ref, b_ref, o_ref, acc_ref):
        s = pl.program_id(0)

        @pl.when(s == 0)
        def _():
            acc_ref[...] = jnp.zeros_like(acc_ref)

        bv = b_ref[...].astype(BF16)
        for k in range(N_CHIPS):
            acc_ref[k] += _tn(a_ref[k], bv)

        @pl.when(s == nsteps - 1)
        def _():
            o_ref[...] = acc_ref[...].astype(BF16)

    return pl.pallas_call(
        body, name=name, grid=(nsteps,),
        in_specs=[pl.BlockSpec((N_CHIPS, tt, rows), lambda s: (0, s, 0)), pl.BlockSpec((tt, cols), lambda s: (s, 0))],
        out_specs=pl.BlockSpec((N_CHIPS, rows, cols), lambda s: (0, 0, 0)),
        out_shape=jax.ShapeDtypeStruct((N_CHIPS, rows, cols), BF16),
        scratch_shapes=[pltpu.VMEM((N_CHIPS, rows, cols), F32)],
        compiler_params=_cparams(("arbitrary",)),
    )(a, b)


def _head_rows(first, n_heads):
    return [(first + g * HD, first + g * HP, HD) for g in range(n_heads)]


def _wgrad(a, b, tt, name):
    t, k = a.shape
    n = b.shape[1]
    nsteps = t // tt

    def body(a_ref, b_ref, o_ref, acc_ref):
        s = pl.program_id(0)

        @pl.when(s == 0)
        def _():
            acc_ref[...] = jnp.zeros_like(acc_ref)

        acc_ref[...] += _tn(a_ref[...].astype(BF16), b_ref[...].astype(BF16))

        @pl.when(s == nsteps - 1)
        def _():
            o_ref[...] = acc_ref[...].astype(BF16)

    return pl.pallas_call(
        body, name=name, grid=(nsteps,),
        in_specs=[pl.BlockSpec((tt, k), lambda s: (s, 0)), pl.BlockSpec((tt, n), lambda s: (s, 0))],
        out_specs=pl.BlockSpec((k, n), lambda s: (0, 0)),
        out_shape=jax.ShapeDtypeStruct((k, n), BF16),
        scratch_shapes=[pltpu.VMEM((k, n), F32)],
        compiler_params=_cparams(("arbitrary",)),
    )(a, b)


def _wgrad_in(dpm, dkv, h, tt):
    t = h.shape[0]
    nsteps = t // tt
    kvw = dkv.shape[1]
    pieces = [(0, 0, O_Q)] + _head_rows(O_Q, NQ + 2 * NKV)

    def body(m_ref, kv_ref, h_ref, o_ref, acc_ref):
        s = pl.program_id(0)

        @pl.when(s == 0)
        def _():
            acc_ref[...] = jnp.zeros_like(acc_ref)

        hv = h_ref[...]
        acc_ref[:NMAIN, :] += _tn(m_ref[...], hv)
        acc_ref[NMAIN:, :] += _tn(kv_ref[...], hv)

        @pl.when(s == nsteps - 1)
        def _():
            for dst, src, size in pieces:
                o_ref[dst:dst + size, :] = acc_ref[src:src + size, :].astype(BF16)

    return pl.pallas_call(
        body, name="wgrad_in", grid=(nsteps,),
        in_specs=[pl.BlockSpec((tt, NMAIN), lambda s: (s, 0)), pl.BlockSpec((tt, kvw), lambda s: (s, 0)),
                  pl.BlockSpec((tt, D), lambda s: (s, 0))],
        out_specs=pl.BlockSpec((W_IN_ROWS, D), lambda s: (0, 0)),
        out_shape=jax.ShapeDtypeStruct((W_IN_ROWS, D), BF16),
        scratch_shapes=[pltpu.VMEM((NMAIN + kvw, D), F32)],
        compiler_params=_cparams(("arbitrary",)),
    )(dpm, dkv, h)


def _mixer_bwd(dxm, proj, ao, cw, gq, gk, sinks, gco, gao, wo, tq):
    t = proj.shape[0]
    nb = tq // BLK
    r8 = tq // 8
    nt = t // tq
    te = tq + 8
    kvw = 2 * NKV * HP

    def body(dx_ref, dxn_ref, p_ref, cgp_ref, hcp_ref, bgn_ref, cgn_ref, hcn_ref, kvp_ref, ao_ref, cw_ref, gq_ref,
             gk_ref, sk_ref, gco_ref, gao_ref, wo_ref,
             dpm_ref, dkvm_ref, dkvh_ref, dcw_ref, dgq_ref, dgk_ref, dsk_ref, dgco_ref, dgao_ref, acc_ref):
        i = pl.program_id(0)

        @pl.when(i == 0)
        def _():
            for r in (dcw_ref, dgq_ref, dgk_ref, dsk_ref, dgco_ref, dgao_ref):
                r[...] = jnp.zeros_like(r)

        acc_ref[...] = jnp.zeros_like(acc_ref)
        live_rows = jnp.where(i < nt - 1, te, tq)
        dxb = dx_ref[...].astype(BF16)
        dxe = jnp.concatenate([dxb, dxn_ref[...].astype(BF16)], axis=0)
        dcn = _nt(dxe, wo_ref[0:CC, :])
        bg = jnp.concatenate([p_ref[:, O_BG:O_BG + CC], bgn_ref[...]], axis=0)
        cg = jnp.concatenate([p_ref[:, O_CG:O_CG + CC], cgn_ref[...]], axis=0)
        hc = jnp.concatenate([p_ref[:, O_HC:O_HC + CC], hcn_ref[...]], axis=0)
        u = cg * hc
        up = jnp.where(i > 0, cgp_ref[...] * hcp_ref[...], 0.0)
        u1, u2 = _conv_taps(jnp.concatenate([up, u], axis=0), te)
        w0, w1, w2 = cw_ref[0:1, :], cw_ref[1:2, :], cw_ref[2:3, :]
        y = w0 * u2 + w1 * u1 + w2 * u
        co = bg * y
        rc, coh = _rms_fwd(co, 1.0 / CC)
        dco = _rms_bwd(dcn, gco_ref[...], coh, rc, 1.0 / CC)
        row_io = lax.broadcasted_iota(jnp.int32, (te, 1), 0)
        own = row_io < tq
        dgco_ref[...] += jnp.sum(jnp.where(own, dcn * coh, 0.0), axis=0, keepdims=True)
        dyc = jnp.where(row_io < live_rows, dco * bg, 0.0)
        dyo = jnp.where(own, dyc, 0.0)
        dcw_ref[0:1, :] += jnp.sum(dyo * u2, axis=0, keepdims=True)
        dcw_ref[1:2, :] += jnp.sum(dyo * u1, axis=0, keepdims=True)
        dcw_ref[2:3, :] += jnp.sum(dyo * u, axis=0, keepdims=True)
        dy1 = pltpu.roll(dyc, te - 1, 0)[0:tq]
        dy2 = pltpu.roll(dyc, te - 2, 0)[0:tq]
        du = w2 * dyc[0:tq] + w1 * dy1 + w0 * dy2
        dpm_ref[:, O_BG:O_BG + CC] = (dco[0:tq] * y[0:tq]).astype(BF16)
        dpm_ref[:, O_CG:O_CG + CC] = (du * hc[0:tq]).astype(BF16)
        dpm_ref[:, O_HC:O_HC + CC] = (du * cg[0:tq]).astype(BF16)
        kraw = jnp.concatenate([kvp_ref[:, 0:NKV * HP], p_ref[:, O_K:O_K + NKV * HP]], axis=0)
        vraw = jnp.concatenate([kvp_ref[:, NKV * HP:], p_ref[:, O_V:O_V + NKV * HP]], axis=0)
        gqv, gkv = gq_ref[...], gk_ref[...]
        keys = _norm_keys(kraw, gkv)
        vb = [vraw[:, h * HP:(h + 1) * HP].astype(BF16) for h in range(NKV)]
        base_valid, c_io = _band_mask()
        lane = lax.broadcasted_iota(jnp.int32, (1, HP), 1)
        dgq, dgk, dsk = (jnp.zeros((1, HP), F32) for _ in range(3))
        dgao = jnp.zeros((1, NQ * HD), F32)
        for b in range(nb):
            lo = jnp.where(i * nb + b == 0, BLK, 0)
            valid = base_valid & (c_io >= lo)
            band = slice(b * BLK, b * BLK + 2 * BLK)
            blk = slice(b * BLK, (b + 1) * BLK)
            ra, aoh = _rms_fwd(ao_ref[blk, :], 1.0 / (NQ * HD))
            danb = _nt(dxb[blk], wo_ref[CC:MIXW, :])
            dgao = dgao + jnp.sum(danb * aoh, axis=0, keepdims=True)
            dao = _rms_bwd(danb, gao_ref[...], aoh, ra, 1.0 / (NQ * HD))
            dos = [dao[:, g // 2 * HP:(g // 2 + 1) * HP] for g in range(NQ)]
            dos = [(d if g % 2 == 0 else pltpu.roll(d, HD, 1)).astype(BF16) for g, d in enumerate(dos)]
            fwd = []
            for g in range(NQ):
                rq, qh = _rms_fwd(p_ref[blk, O_Q + g * HP:O_Q + (g + 1) * HP], 1.0 / HD)
                qs = (qh * (gqv * SCALE)).astype(BF16)
                fwd.append((rq, qh, qs) + _attn_probs(qs, keys[g // GRP][2][band], sk_ref[0, g], valid))
            dqs = []
            for h in range(NKV):
                khat, rk, kn = [a[band] for a in keys[h]]
                dss, prbs, qns, dobs = [], [], [], []
                for g in range(h * GRP, (h + 1) * GRP):
                    rq, qh, qs, pr, ps = fwd[g]
                    dob = dos[g]
                    dp = _nt(dob, vb[h][band])
                    delta = jnp.sum(pr * dp, axis=-1, keepdims=True)
                    dsb = (pr * (dp - delta)).astype(BF16)
                    dsk = dsk + jnp.where(lane == g, -jnp.sum(ps * delta, axis=0, keepdims=True), 0.0)
                    dqn = jnp.dot(dsb, kn, preferred_element_type=F32) * SCALE
                    dgq = dgq + jnp.sum(dqn * qh, axis=0, keepdims=True)
                    dqs.append(_rms_bwd(dqn, gqv, qh, rq, 1.0 / HD).astype(BF16))
                    dss.append(dsb)
                    prbs.append(pr.astype(BF16))
                    qns.append(qs)
                    dobs.append(dob)
                dkn = _tn(jnp.concatenate(dss, axis=0), jnp.concatenate(qns, axis=0))
                dv = _tn(jnp.concatenate(prbs, axis=0), jnp.concatenate(dobs, axis=0))
                dgk = dgk + jnp.sum(dkn * khat, axis=0, keepdims=True)
                acc_ref[band, h * HP:(h + 1) * HP] += _rms_bwd(dkn, gkv, khat, rk, 1.0 / HD)
                acc_ref[band, (NKV + h) * HP:(NKV + h + 1) * HP] += dv
            dpm_ref[blk, O_Q:O_K] = jnp.concatenate(dqs, axis=1)
        dgq_ref[...] += dgq
        dgk_ref[...] += dgk
        dsk_ref[...] += dsk
        dgao_ref[...] += dgao
        dkvh_ref[...] = acc_ref[0:BLK, :]
        dkvm_ref[...] = acc_ref[BLK:, :]

    prev8 = lambda col: pl.BlockSpec((8, CC), lambda i: (jnp.maximum(i * r8 - 1, 0), col))
    next8 = lambda col: pl.BlockSpec((8, CC), lambda i: (jnp.minimum((i + 1) * r8, t // 8 - 1), col))
    small = lambda n: pl.BlockSpec((1, n), lambda i: (0, 0))
    return pl.pallas_call(
        body, name="mixer_bwd", grid=(nt,),
        in_specs=[
            pl.BlockSpec((tq, D), lambda i: (i, 0)),
            pl.BlockSpec((8, D), lambda i: (jnp.minimum((i + 1) * r8, t // 8 - 1), 0)),
            pl.BlockSpec((tq, NP), lambda i: (i, 0)),
            prev8(O_CG // CC), prev8(O_HC // CC),
            next8(O_BG // CC), next8(O_CG // CC), next8(O_HC // CC),
            pl.BlockSpec((BLK, kvw), lambda i: (jnp.maximum(i * nb - 1, 0), O_K // kvw)),
            pl.BlockSpec((tq, NQ * HD), lambda i: (i, 0)),
            _const_spec((8, CC)), _const_spec((1, HP)), _const_spec((1, HP)),
            pl.BlockSpec(memory_space=pltpu.SMEM),
            _const_spec((1, CC)), _const_spec((1, NQ * HD)), _const_spec((MIXW, D)),
        ],
        out_specs=[
            pl.BlockSpec((tq, NMAIN), lambda i: (i, 0)),
            pl.BlockSpec((tq, kvw), lambda i: (i, 0)),
            pl.BlockSpec((BLK, kvw), lambda i: (i, 0)),
            pl.BlockSpec((8, CC), lambda i: (0, 0)), small(HP), small(HP), small(HP), small(CC), small(NQ * HD),
        ],
        out_shape=[
            jax.ShapeDtypeStruct((t, NMAIN), BF16), jax.ShapeDtypeStruct((t, kvw), F32),
            jax.ShapeDtypeStruct((nt * BLK, kvw), F32),
            jax.ShapeDtypeStruct((8, CC), F32), jax.ShapeDtypeStruct((1, HP), F32), jax.ShapeDtypeStruct((1, HP), F32),
            jax.ShapeDtypeStruct((1, HP), F32), jax.ShapeDtypeStruct((1, CC), F32),
            jax.ShapeDtypeStruct((1, NQ * HD), F32),
        ],
        scratch_shapes=[pltpu.VMEM((tq + BLK, kvw), F32)],
        compiler_params=_cparams(("arbitrary",)),
    )(dxm, dxm, proj, proj, proj, proj, proj, proj, proj, ao, cw, gq, gk, sinks, gco, gao, wo)


def _inproj_bwd(dpm, dkvm, dkvh, wpt, x, g1, dxm, tm):
    t = x.shape[0]
    kvw = 2 * NKV * HP
    nt = t // tm

    def body(dp_ref, dk_ref, dh_ref, w_ref, x_ref, g_ref, dxm_ref, dx_ref, dg_ref, dkv_ref):
        i = pl.program_id(0)

        @pl.when(i == 0)
        def _():
            dg_ref[...] = jnp.zeros_like(dg_ref)

        halo = jnp.where(i < nt - 1, dh_ref[...], 0.0)
        dkv_ref[0:tm - BLK, :] = dk_ref[0:tm - BLK, :].astype(BF16)
        dkv_ref[tm - BLK:tm, :] = (dk_ref[tm - BLK:tm, :] + halo).astype(BF16)
        dh = (jnp.dot(dp_ref[...], w_ref[0:NMAIN, :], preferred_element_type=F32)
              + jnp.dot(dkv_ref[...], w_ref[NMAIN:NP, :], preferred_element_type=F32))
        r, xh = _rms_fwd(x_ref[...], 1.0 / D)
        dg_ref[...] += jnp.sum(dh * xh, axis=0, keepdims=True)
        dx_ref[...] = dxm_ref[...] + _rms_bwd(dh, g_ref[...], xh, r, 1.0 / D)

    row = lambda w: pl.BlockSpec((tm, w), lambda i: (i, 0))
    return pl.pallas_call(
        body, name="inproj_bwd", grid=(nt,),
        in_specs=[row(NMAIN), row(kvw), pl.BlockSpec((BLK, kvw), lambda i: (jnp.minimum(i + 1, nt - 1), 0)),
                  _const_spec((NP, D)), row(D), _const_spec((1, D)), row(D)],
        out_specs=[row(D), pl.BlockSpec((1, D), lambda i: (0, 0)), row(kvw)],
        out_shape=[jax.ShapeDtypeStruct((t, D), F32), jax.ShapeDtypeStruct((1, D), F32),
                   jax.ShapeDtypeStruct((t, kvw), BF16)],
        compiler_params=_cparams(("arbitrary",)),
    )(dpm, dkvm, dkvh, wpt, x, g1, dxm)


def _rows_tile(rows, cap=512):
    for cand in range(min(rows, cap) // 16 * 16, 0, -16):
        if rows % cand == 0:
            return cand
    return rows


def _presum_halves(gs, theirs, core):
    n = len(gs)

    def body(c_ref, *refs):
        for g_ref, t_ref, o_ref in zip(refs[:n], refs[n:2 * n], refs[2 * n:]):
            o_ref[...] = (g_ref[...].astype(F32) + t_ref[...].astype(F32)).astype(BF16)

    half = lambda ta: pl.BlockSpec((None,) + ta.shape[1:], lambda k, c_ref: (k, 0, 0))
    own = lambda ta: pl.BlockSpec((None,) + ta.shape[1:], lambda k, c_ref: (k, c_ref[0], 0))
    return pl.pallas_call(
        body, name="presum",
        grid_spec=pltpu.PrefetchScalarGridSpec(
            num_scalar_prefetch=1, grid=(N_CHIPS,),
            in_specs=[own(ta) for ta in theirs] + [half(ta) for ta in theirs],
            out_specs=[half(ta) for ta in theirs]),
        out_shape=[jax.ShapeDtypeStruct(ta.shape, BF16) for ta in theirs],
        compiler_params=_cparams(("parallel",)),
    )(core, *gs, *theirs)


def _sum_chips(got, ps, chip):
    n = len(got)
    steps = 2

    def body(chip_ref, *refs):
        for c_ref, own_ref, o_ref in zip(refs[:n], refs[n:2 * n], refs[2 * n:]):
            acc = None
            for j in range(N_CHIPS):
                term = jnp.where(chip_ref[0] == j, own_ref[...], c_ref[j]).astype(F32)
                acc = term if acc is None else acc + term
            o_ref[...] = acc

    tile = lambda c: (c.shape[1] // steps, c.shape[2])
    return pl.pallas_call(
        body, name="chipsum",
        grid_spec=pltpu.PrefetchScalarGridSpec(
            num_scalar_prefetch=1, grid=(steps,),
            in_specs=[pl.BlockSpec((N_CHIPS,) + tile(c), lambda i, chip_ref: (0, i, 0)) for c in got]
            + [pl.BlockSpec((None,) + tile(c), lambda i, chip_ref: (chip_ref[0], i, 0)) for c in got],
            out_specs=[pl.BlockSpec(tile(c), lambda i, chip_ref: (i, 0)) for c in got]),
        out_shape=[jax.ShapeDtypeStruct(c.shape[1:], F32) for c in got],
        compiler_params=_cparams(("parallel",)),
    )(chip, *got, *ps)


def _adamw_refs(w_ref, g_ref, m_ref, v_ref, d_ref, mo_ref, vo_ref):
    c1 = 1.0 - ADAM_B1 ** ADAM_STEP
    c2 = 1.0 - ADAM_B2 ** ADAM_STEP
    gv = g_ref[...]
    mn = ADAM_B1 * m_ref[...] + (1.0 - ADAM_B1) * gv
    vn = ADAM_B2 * v_ref[...] + (1.0 - ADAM_B2) * (gv * gv)
    mo_ref[...] = mn
    vo_ref[...] = vn
    d_ref[...] = -ADAM_LR * ((mn / c1) / (jnp.sqrt(vn / c2) + ADAM_EPS) + ADAM_WD * w_ref[...])


def _adamw_small(ws, gs, ms, vs):
    n = len(ws)

    def body(*refs):
        for k in range(n):
            _adamw_refs(*refs[k::n])

    res = pl.pallas_call(
        body, name="adamw_small", out_shape=[jax.ShapeDtypeStruct(w.shape, F32) for w in ws] * 3,
        compiler_params=_cparams(),
    )(*ws, *gs, *ms, *vs)
    return res[:n], res[n:2 * n], res[2 * n:]


def _adamw(w, g, m, v, name):
    rows, cols = w.shape
    tr = _rows_tile(rows)

    def body(*refs):
        _adamw_refs(*refs)

    spec = pl.BlockSpec((tr, cols), lambda i: (i, 0))
    sds = jax.ShapeDtypeStruct((rows, cols), F32)
    return pl.pallas_call(
        body, name=name, grid=(rows // tr,), in_specs=[spec] * 4, out_specs=[spec] * 3, out_shape=[sds] * 3,
        compiler_params=_cparams(("parallel",)),
    )(w, g, m, v)


def _place():
    x, y, c = lax.axis_index("x"), lax.axis_index("y"), lax.axis_index("c")
    chips = [(1 - x, y), (x, 1 - y), (1 - x, 1 - y)]
    return x, y, c, chips


ANY = pl.BlockSpec(memory_space=pl.ANY)
DMA_ROWS = 64


def _pieces(shape):
    rows = shape[-2]
    step = DMA_ROWS if rows % DMA_ROWS == 0 else rows
    lead = [()]
    for n in shape[:-2]:
        lead = [i + (k,) for i in lead for k in range(n)]
    return [i + (pl.ds(r0, step),) for i in lead for r0 in range(0, rows, step)]


def _start_pieces(make, src, dst):
    for idx in _pieces(src.shape):
        make(src.at[idx], dst.at[idx]).start()


def _gather_body(srcs, outs, sems, layer, start):
    nw = len(srcs)
    ssem, rsem, fssem, frsem = sems
    x, y, c, chips = _place()
    kme = 2 * x + y

    def plane(j, w, to):
        return lambda s, d: pltpu.make_async_remote_copy(
            src_ref=s, dst_ref=d, send_sem=ssem.at[j, w], recv_sem=rsem.at[j, w], device_id=to,
            device_id_type=MESH)

    def passed(j, w):
        return lambda s, d: pltpu.make_async_remote_copy(
            src_ref=s, dst_ref=d, send_sem=fssem.at[j, w], recv_sem=frsem.at[j, w],
            device_id=(x, y, 1 - c), device_id_type=MESH)

    @pl.when(c == layer)
    def _():
        for j, (px, py) in enumerate(chips):
            for w in range(nw):
                start(plane(j, w, (px, py, c)), srcs[w], outs[w].at[kme])
        for j, (px, py) in enumerate(chips):
            for w in range(nw):
                got = outs[w].at[2 * px + py]
                plane(j, w, (px, py, c))(got, got).wait_recv()
                start(passed(j, w), got, got)
        for j, (px, py) in enumerate(chips):
            for w in range(nw):
                got = outs[w].at[2 * px + py]
                plane(j, w, (px, py, c))(got, got).wait_send()
                passed(j, w)(got, got).wait_send()

    @pl.when(c != layer)
    def _():
        for j, (px, py) in enumerate(chips):
            for w in range(nw):
                got = outs[w].at[2 * px + py]
                passed(j, w)(got, got).wait_recv()


def _handshake(peers):
    barrier = pltpu.get_barrier_semaphore()
    for peer in peers:
        pl.semaphore_signal(barrier, inc=1, device_id=peer, device_id_type=MESH)
    pl.semaphore_wait(barrier, len(peers))


def _handshake_all():
    x, y, c, _ = _place()
    _handshake([(x ^ (r >> 2), y ^ ((r >> 1) & 1), c ^ (r & 1)) for r in range(1, 8)])


def _gather_layer_async(blocks, layer, name, collective_id):
    hbm = pltpu.MemorySpace.HBM
    srcs = [jax.new_ref(b, memory_space=hbm) for b in blocks]
    outs = [jax.empty_ref(jax.ShapeDtypeStruct((N_CHIPS,) + b.shape, b.dtype), memory_space=hbm) for b in blocks]

    @pl.kernel(mesh=plsc.ScalarSubcoreMesh(axis_name="seq", num_cores=1), name=name,
               scratch_types=[pltpu.SemaphoreType.DMA((3, len(blocks)))] * 4,
               compiler_params=pltpu.CompilerParams(collective_id=collective_id))
    def launch(*sems):
        _handshake_all()
        _gather_body(srcs, outs, sems, layer, lambda make, s, d: make(s, d).start())

    launch()
    return [o[...] for o in outs]


def _swap_siblings(arrs, halves, name, collective_id=None):
    nw = len(arrs)
    out_sds = [jax.ShapeDtypeStruct((a.shape[0], a.shape[1] // 2, a.shape[2]) if halves else a.shape, a.dtype)
               for a in arrs]

    def exchange(srcs, outs, ssem, rsem, start):
        x, y, c, _ = _place()

        def give(w):
            return lambda s, d: pltpu.make_async_remote_copy(
                src_ref=s, dst_ref=d, send_sem=ssem.at[w], recv_sem=rsem.at[w], device_id=(x, y, 1 - c),
                device_id_type=MESH)

        for w in range(nw):
            hr = outs[w].shape[1]
            start(give(w), srcs[w].at[:, pl.ds((1 - c) * hr, hr)] if halves else srcs[w], outs[w])
        for w in range(nw):
            give(w)(outs[w], outs[w]).wait()

    if collective_id is None:
        def body(*refs):
            exchange(refs[:nw], refs[nw:2 * nw], *refs[2 * nw:], _start_pieces)

        return pl.pallas_call(
            body, name=name, in_specs=[ANY] * nw, out_specs=[ANY] * nw, out_shape=out_sds,
            scratch_shapes=[pltpu.SemaphoreType.DMA((nw,))] * 2,
            compiler_params=_cparams(has_side_effects=True),
        )(*arrs)

    hbm = pltpu.MemorySpace.HBM
    srcs = [jax.new_ref(a, memory_space=hbm) for a in arrs]
    outs = [jax.empty_ref(sds, memory_space=hbm) for sds in out_sds]

    @pl.kernel(mesh=plsc.ScalarSubcoreMesh(axis_name="seq", num_cores=1), name=name,
               scratch_types=[pltpu.SemaphoreType.DMA((nw,))] * 2,
               compiler_params=pltpu.CompilerParams(collective_id=collective_id))
    def launch(ssem, rsem):
        x, y, c, _ = _place()
        _handshake([(x, y, 1 - c)])
        exchange(srcs, outs, ssem, rsem, lambda make, s, d: make(s, d).start())

    launch()
    return [o[...] for o in outs]


def _scatter_body(srcs, outs, sems, start):
    nw = len(srcs)
    ssem, rsem = sems
    x, y, c, chips = _place()
    kme = 2 * x + y

    def give(j, w, to):
        return lambda s, d: pltpu.make_async_remote_copy(
            src_ref=s, dst_ref=d, send_sem=ssem.at[j, w], recv_sem=rsem.at[j, w], device_id=to,
            device_id_type=MESH)

    for j, (px, py) in enumerate(chips):
        for w in range(nw):
            start(give(j, w, (px, py, c)), srcs[w].at[2 * px + py], outs[w].at[kme])
    for j, (px, py) in enumerate(chips):
        for w in range(nw):
            got = outs[w].at[2 * px + py]
            give(j, w, (px, py, c))(got, got).wait_recv()
    for j, (px, py) in enumerate(chips):
        for w in range(nw):
            sent = srcs[w].at[2 * px + py]
            give(j, w, (px, py, c))(sent, sent).wait_send()


def _scatter_chips_async(ps, name, collective_id):
    hbm = pltpu.MemorySpace.HBM
    srcs = [jax.new_ref(p, memory_space=hbm) for p in ps]
    outs = [jax.empty_ref(jax.ShapeDtypeStruct(p.shape, p.dtype), memory_space=hbm) for p in ps]

    @pl.kernel(mesh=plsc.ScalarSubcoreMesh(axis_name="seq", num_cores=1), name=name,
               scratch_types=[pltpu.SemaphoreType.DMA((3, len(ps)))] * 2,
               compiler_params=pltpu.CompilerParams(collective_id=collective_id))
    def launch(*sems):
        _handshake_all()
        _scatter_body(srcs, outs, sems, lambda make, s, d: make(s, d).start())

    launch()
    return [o[...] for o in outs]


def _allreduce_small(v):
    rows = v.shape[0]

    def body(v_ref, o_ref, buf, ssem, rsem):
        x, y, c, _ = _place()
        me = 4 * x + 2 * y + c
        buf[me] = v_ref[...]
        sends = []
        for r in range(1, 8):
            peer = (x ^ (r >> 2), y ^ ((r >> 1) & 1), c ^ (r & 1))
            cp = pltpu.make_async_remote_copy(
                src_ref=v_ref, dst_ref=buf.at[me], send_sem=ssem.at[r - 1], recv_sem=rsem.at[r - 1],
                device_id=peer, device_id_type=MESH)
            cp.start()
            sends.append(cp)
        for r in range(1, 8):
            src = me ^ r
            pltpu.make_async_remote_copy(
                src_ref=v_ref, dst_ref=buf.at[src], send_sem=ssem.at[r - 1], recv_sem=rsem.at[r - 1],
                device_id=(x, y, c), device_id_type=MESH).wait_recv()
        for cp in sends:
            cp.wait_send()
        acc = buf[0]
        for d in range(1, 8):
            acc = acc + buf[d]
        o_ref[...] = acc

    vm = pl.BlockSpec(memory_space=pltpu.VMEM)
    return pl.pallas_call(
        body, name="allreduce_small", in_specs=[vm], out_specs=vm,
        out_shape=jax.ShapeDtypeStruct(v.shape, F32),
        scratch_shapes=[pltpu.VMEM((8, rows, 128), F32), pltpu.SemaphoreType.DMA((7,)),
                        pltpu.SemaphoreType.DMA((7,))],
        compiler_params=_cparams(has_side_effects=True),
    )(v)


def _t(w):
    return jnp.swapaxes(w, -1, -2)


def _count(shape):
    n = 1
    for s in shape:
        n *= s
    return n


def _pack_rows(arrs):
    flat = [jnp.pad(a.reshape(-1), (0, (-_count(a.shape)) % 128)) for a in arrs]
    v = jnp.concatenate(flat)
    rows = -(-v.shape[0] // (8 * 128)) * 8
    return jnp.pad(v, (0, rows * 128 - v.shape[0])).reshape(rows, 128)


def kernel(x, norm1_g, w_in, conv_w, q_norm_g, k_norm_g, sinks, conv_out_g, attn_out_g, w_o, norm2_g, w_gate, w_up, w_down, loss_target, m_norm1_g, m_w_in, m_conv_w, m_q_norm_g, m_k_norm_g, m_sinks, m_conv_out_g, m_attn_out_g, m_w_o, m_norm2_g, m_w_gate, m_w_up, m_w_down, v_norm1_g, v_w_in, v_conv_w, v_q_norm_g, v_k_norm_g, v_sinks, v_conv_out_g, v_attn_out_g, v_w_o, v_norm2_g, v_w_gate, v_w_up, v_w_down):
    depth = w_in.shape[0]
    t = x.shape[1]
    xs = x.reshape(t, D)
    tgt = loss_target.reshape(t, D)
    xi, yi = lax.axis_index("x"), lax.axis_index("y")
    kme = 2 * xi + yi
    tm = min(512, t)
    tq = min(512, t)
    tf = min(256, t)
    tw = min(1024, t)

    cwp = jnp.pad(conv_w.reshape(depth * 3, CC // N_CHIPS), ((0, 8 - depth * 3), (0, 0)))
    own_f = [jnp.concatenate([_t(w_gate[l]), _t(w_up[l]), w_down[l]], axis=0).astype(BF16) for l in range(depth)]
    own_o = [w_o[l].astype(BF16) for l in range(depth)]
    own_i = [_t(w_in[l]).astype(BF16) for l in range(depth)]
    mine = lambda got, own: lax.dynamic_update_index_in_dim(got, own, kme, 0)
    (got_i0,) = _gather_layer_async([own_i[0]], 0, "gather_in0_seq", collective_id=14)
    got_ocw = _gather_layer_async([own_o[0], cwp], 0, "gather_o0_seq", collective_id=15)
    got_i0, own_f, own_o, own_i = lax.optimization_barrier((got_i0, own_f, own_o, own_i))
    gf0_in = lax.optimization_barrier((own_f[0], got_i0))[0]
    (got_f0,) = _gather_layer_async([gf0_in], 0, "gather_ffn0_seq", collective_id=6)

    chip = kme.reshape(1).astype(jnp.int32)

    def layer_params(l, got_o, cw_full):
        return dict(
            wo=mine(got_o, own_o[l]).reshape(MIXW, D),
            cw=jnp.pad(cw_full[l], ((0, 5), (0, 0))),
            g1=norm1_g[l].reshape(1, D), g2=norm2_g[l].reshape(1, D),
            gq=jnp.pad(q_norm_g[l], (0, HP - HD)).reshape(1, HP), gk=jnp.pad(k_norm_g[l], (0, HP - HD)).reshape(1, HP),
            sk=sinks[l].reshape(1, NQ), gco=conv_out_g[l].reshape(1, CC),
            gao=attn_out_g[l].reshape(1, NQ * HD))

    saved, layers = [], []
    cur = xs
    for l in range(depth):
        x_in = cur
        if l == 0:
            got_i = got_i0
        else:
            got_f1, got_o, got_i = lax.optimization_barrier((got_l1, cur))[0]
        proj, h, wpt = _inproj_fwd(cur, norm1_g[l].reshape(1, D), got_i, own_i[l], chip, tm)
        if l == 0:
            got_o, got_cw = lax.optimization_barrier((got_ocw, proj))[0]
            cw_full = mine(got_cw, cwp).transpose(1, 0, 2).reshape(8, CC)[:depth * 3].reshape(depth, 3, CC)
        p = layer_params(l, got_o, cw_full)
        p["wpt"] = wpt
        xm, mix, ao = _mixer_fwd(proj, cur, p["cw"], p["gq"], p["gk"], p["sk"], p["gco"], p["gao"], p["wo"], tq)
        if l == 0:
            got_f0 = lax.optimization_barrier((got_f0, xm))[0]
            l1_in = lax.optimization_barrier(([own_f[1], own_o[1], own_i[1]], got_f0))[0]
            got_l1 = _gather_layer_async(l1_in, 1, "gather_layer1_seq", collective_id=1)
        p["gf"] = mine(got_f0 if l == 0 else got_f1, own_f[l])
        layers.append(p)
        if l < depth - 1:
            cur, a, b, h2 = _ffn_fwd(xm, p["g2"], p["gf"], tm)
        else:
            lpart, dy, a, b, h2 = _ffn_fwd(xm, p["g2"], p["gf"], tm, tgt)
        saved.append(dict(x=x_in, proj=proj, h=h, xm=xm, mix=mix, ao=ao, a=a, b=b, h2=h2))

    ci = lax.axis_index("c")
    core = ci.reshape(1).astype(jnp.int32)
    rbig = [dict() for _ in range(depth)]
    gsmall = [None] * depth

    def after_(vals, after):
        return vals if after is None else lax.optimization_barrier((vals, after))[0]

    def reduce_1(gs, tag, ids):
        return gs, _swap_siblings(gs, True, f"swap_halves_{tag}_seq", ids[0]), tag, ids

    def reduce_2(state, after):
        gs, theirs, tag, ids = state
        ps = _presum_halves(gs, after_(theirs, after), core)
        return ps, _scatter_chips_async(ps, f"scatter_{tag}_seq", ids[1]), tag, ids

    def reduce_3(state, after):
        ps, got, tag, ids = state
        r_mine = _sum_chips(after_(got, after), ps, chip)
        return r_mine, _swap_siblings(r_mine, False, f"swap_reduced_{tag}" + ("_seq" if ids[2] else ""), ids[2])

    def reduce_4(state, after):
        r_mine, r_theirs = state
        return [jnp.where(ci == 0, jnp.concatenate([a, b], axis=0), jnp.concatenate([b, a], axis=0))
                for a, b in zip(r_mine, after_(r_theirs, after))]

    ids = {"ffn1": (7, 4, 8), "in1": (9, 5, 10), "ffn0": (11, 2, 12), "in0": (13, 3, None)}
    in_2 = scattering = None
    handed = {}
    for l in reversed(range(depth)):
        p, s = layers[l], saved[l]
        dxm, da, db, hm, dg2 = _ffn_bwd(dy, s["xm"], p["g2"], s["a"], s["b"], p["gf"], tf)
        if in_2 is not None:
            in_2 = reduce_2(in_2, dxm)
        g_wg = _wgrad_blocks(da, s["h2"], tw, "wgrad_gate")
        g_wu = _wgrad_blocks(db, s["h2"], tw, "wgrad_up")
        g_wd = _wgrad_blocks(hm, dy, tw, "wgrad_down")
        if in_2 is not None:
            handed[f"in{l + 1}"] = reduce_3(in_2, g_wd)
        ffn_1 = reduce_1([g_wg, g_wu, g_wd], f"ffn{l}", ids[f"ffn{l}"])
        dpm, dkvm, dkvh, dcw, dgq, dgk, dsk, dgco, dgao = _mixer_bwd(
            after_(dxm, scattering), s["proj"], s["ao"], p["cw"], p["gq"], p["gk"], p["sk"], p["gco"], p["gao"],
            p["wo"], tq)
        ffn_2 = reduce_2(ffn_1, dpm)
        scattering = ffn_2[0]
        g_o = _wgrad(s["mix"], dxm, tw, "wgrad_o")
        dx, dg1, dkv = _inproj_bwd(dpm, dkvm, dkvh, p["wpt"], s["x"], p["g1"], dxm, tq)
        g_in = _wgrad_in(dpm, dkv, s["h"], tw)
        dy = dx
        gsmall[l] = dict(g1=dg1, cw=dcw[:3], gq=dgq[0, :HD], gk=dgk[0, :HD], sk=dsk[0, :NQ], gco=dgco,
                         gao=dgao, g2=dg2)
        above = handed.get(f"ffn{l + 1}")
        handed[f"ffn{l}"] = reduce_3(ffn_2, g_in if above is None else (g_in, above[0]))
        in_2 = reduce_1([g_in.reshape(N_CHIPS, -1, D), g_o.reshape(N_CHIPS, -1, D)], f"in{l}", ids[f"in{l}"])
    grad_x = dy.reshape(x.shape)

    small_shapes = dict(g1=(D,), cw=(3, CC), gq=(HD,), gk=(HD,), sk=(NQ,), gco=(CC,), gao=(NQ * HD,), g2=(D,))
    red = _allreduce_small(_pack_rows([gsmall[l][n] for l in range(depth) for n in small_shapes]
                                      + [lpart[0:1, 0:1]])).reshape(-1)
    red_small, offs = {n: [] for n in small_shapes}, 0
    for l in range(depth):
        for n, shp in small_shapes.items():
            cnt = _count(shp)
            red_small[n].append(red[offs:offs + cnt].reshape(shp))
            offs += -(-cnt // 128) * 128
    loss = red[offs]
    g_small = {n: jnp.stack(v) for n, v in red_small.items()}
    g_cw = lax.dynamic_slice_in_dim(g_small["cw"], kme * (CC // N_CHIPS), CC // N_CHIPS, axis=2)

    weights = [norm1_g, w_in, conv_w, q_norm_g, k_norm_g, sinks, conv_out_g, attn_out_g, w_o, norm2_g, w_gate,
               w_up, w_down]
    moms = [m_norm1_g, m_w_in, m_conv_w, m_q_norm_g, m_k_norm_g, m_sinks, m_conv_out_g, m_attn_out_g, m_w_o,
            m_norm2_g, m_w_gate, m_w_up, m_w_down]
    vars_ = [v_norm1_g, v_w_in, v_conv_w, v_q_norm_g, v_k_norm_g, v_sinks, v_conv_out_g, v_attn_out_g, v_w_o,
             v_norm2_g, v_w_gate, v_w_up, v_w_down]
    n_w = len(weights)
    big_idx = dict(zip(("in", "o", "g", "u", "d"), (1, 8, 10, 11, 12)))
    small_idx = [n for n in range(n_w) if n not in big_idx.values()]
    grads, deltas, new_m, new_v = [None] * n_w, [None] * n_w, [None] * n_w, [None] * n_w
    for n, g in zip(small_idx, (g_small["g1"], g_cw, g_small["gq"], g_small["gk"], g_small["sk"], g_small["gco"],
                                g_small["gao"], g_small["g2"])):
        grads[n] = g

    def update_big(name):
        n = big_idx[name]
        g = jnp.stack([rbig[l][name] for l in range(depth)])
        flip = g.shape != weights[n].shape
        rows2d = lambda a3: (_t(a3) if flip else a3).reshape(-1, D)
        res = _adamw(rows2d(weights[n]), g.reshape(-1, D), rows2d(moms[n]), rows2d(vars_[n]), f"adamw_{n}")
        res = [g] + [r.reshape(g.shape) for r in res]
        grads[n], deltas[n], new_m[n], new_v[n] = [_t(r) for r in res] if flip else res

    for l in range(depth):
        rbig[l]["g"], rbig[l]["u"], rbig[l]["d"] = reduce_4(handed[f"ffn{l}"], red)
    rbig[1]["in"], rbig[1]["o"] = reduce_4(handed["in1"], red)
    update_big("g")
    in_2 = reduce_2(in_2, new_v[big_idx["g"]])
    update_big("u")
    update_big("d")
    rbig[0]["in"], rbig[0]["o"] = reduce_4(reduce_3(in_2, new_v[big_idx["d"]]), None)
    for name in ("in", "o"):
        update_big(name)
    res = _adamw_small(*[[arrs[n] for n in small_idx] for arrs in (weights, grads, moms, vars_)])
    for k, n in enumerate(small_idx):
        deltas[n], new_m[n], new_v[n] = res[0][k], res[1][k], res[2][k]
    return (loss, grad_x, *grads, *deltas, *new_m, *new_v)
```

```python
import jax
import jax.numpy as jnp
from jax import lax
from jax.experimental import pallas as pl
from jax.experimental.pallas import tpu as pltpu
from jax.experimental.pallas import tpu_sc as plsc

F32 = jnp.float32
BF16 = jnp.bfloat16

D = 1024
CC = 512
NQ = 8
NKV = 2
HD = 64
HP = 128
GRP = NQ // NKV
FF = 2816
FFB = FF // 4
BLK = 128
EPS = 1e-6
NEG = -1e30
SCALE = HD ** -0.5
O_BG, O_CG, O_HC, O_Q = 0, CC, 2 * CC, 3 * CC
O_K = O_Q + NQ * HP
O_V = O_K + NKV * HP
NP = O_V + NKV * HP
NMAIN = O_K
MIXW = CC + NQ * HD
N_CHIPS = 4
VMEM_LIMIT = 56 * 1024 * 1024
MESH = pl.DeviceIdType.MESH

ADAM_LR, ADAM_B1, ADAM_B2, ADAM_EPS, ADAM_WD, ADAM_STEP = 0.001, 0.9, 0.999, 1e-08, 0.01, 10


def _cparams(sem=None, **kw):
    if sem is not None:
        kw["dimension_semantics"] = sem
    return pltpu.CompilerParams(vmem_limit_bytes=VMEM_LIMIT, **kw)


def _const_spec(shape):
    nd = len(shape)
    return pl.BlockSpec(shape, lambda *_: (0,) * nd, pipeline_mode=pl.Buffered(1))


def _nt(a, b):
    return lax.dot_general(a, b, (((1,), (1,)), ((), ())), preferred_element_type=F32)


def _tn(a, b):
    return lax.dot_general(a, b, (((0,), (0,)), ((), ())), preferred_element_type=F32)


def _rms_fwd(x, inv_n):
    r = lax.rsqrt(jnp.sum(x * x, axis=-1, keepdims=True) * inv_n + EPS)
    return r, x * r


def _rms_bwd(dy, g, xh, r, inv_n):
    dxh = dy * g
    return r * (dxh - xh * (jnp.sum(dxh * xh, axis=-1, keepdims=True) * inv_n))


W_IN_ROWS = 3 * CC + (NQ + 2 * NKV) * HD
W_IN_BLOCK = W_IN_ROWS // N_CHIPS


def _padded_row(row):
    return row + max(row - O_Q, 0) // HD * (HP - HD)


def _w_in_pieces(k):
    first = k * W_IN_BLOCK
    plain = min(max(O_Q - first, 0), W_IN_BLOCK)
    pieces = [(0, first, plain)] if plain else []
    return pieces + [(r, _padded_row(first + r), HD) for r in range(plain, W_IN_BLOCK, HD)]


def _inproj_fwd(x, g1, gi, own_i, chip, tm):
    t = x.shape[0]

    def body(chip_ref, x_ref, g_ref, gi_ref, own_ref, p_ref, h_ref, w_ref, sem):
        @pl.when(pl.program_id(0) == 0)
        def _():
            for k in range(N_CHIPS):
                for src, dst, rows in _w_in_pieces(k):
                    @pl.when(chip_ref[0] == k)
                    def _():
                        pltpu.make_async_copy(own_ref.at[pl.ds(src, rows)], w_ref.at[pl.ds(dst, rows)], sem).start()

                    @pl.when(chip_ref[0] != k)
                    def _():
                        pltpu.make_async_copy(gi_ref.at[k, pl.ds(src, rows)], w_ref.at[pl.ds(dst, rows)], sem).start()
            for slot in range(NQ + 2 * NKV):
                w_ref[O_Q + slot * HP + HD:O_Q + (slot + 1) * HP, :] = jnp.zeros((HP - HD, D), BF16)
            landed = w_ref.at[pl.ds(0, W_IN_ROWS)]
            pltpu.make_async_copy(landed, landed, sem).wait()

        _, xh = _rms_fwd(x_ref[...], 1.0 / D)
        h = (xh * g_ref[...]).astype(BF16)
        h_ref[...] = h
        p_ref[...] = _nt(h, w_ref[...])

    const = lambda shape: pl.BlockSpec(shape, lambda i, c: (0,) * len(shape))
    return pl.pallas_call(
        body, name="inproj_fwd",
        grid_spec=pltpu.PrefetchScalarGridSpec(
            num_scalar_prefetch=1, grid=(t // tm,),
            in_specs=[pl.BlockSpec((tm, D), lambda i, c: (i, 0)), const((1, D)), ANY, ANY],
            out_specs=[pl.BlockSpec((tm, NP), lambda i, c: (i, 0)), pl.BlockSpec((tm, D), lambda i, c: (i, 0)),
                       const((NP, D))],
            scratch_shapes=[pltpu.SemaphoreType.DMA(())]),
        out_shape=[jax.ShapeDtypeStruct((t, NP), F32), jax.ShapeDtypeStruct((t, D), BF16),
                   jax.ShapeDtypeStruct((NP, D), BF16)],
        compiler_params=_cparams(("arbitrary",)),
    )(chip, x, g1, gi, own_i)


def _band_mask():
    r_io = lax.broadcasted_iota(jnp.int32, (BLK, 2 * BLK), 0)
    c_io = lax.broadcasted_iota(jnp.int32, (BLK, 2 * BLK), 1)
    return (c_io > r_io) & (c_io <= r_io + BLK), c_io


def _conv_taps(uf, n):
    u1 = pltpu.roll(uf, 1, 0)[8:8 + n]
    u2 = pltpu.roll(uf, 2, 0)[8:8 + n]
    return u1, u2


def _attn_probs(qs, kband, sink, valid):
    s = jnp.where(valid, _nt(qs, kband), NEG)
    m = jnp.maximum(jnp.max(s, axis=-1, keepdims=True), sink)
    p = jnp.exp(s - m)
    es = jnp.exp(sink - m)
    inv = 1.0 / (jnp.sum(p, axis=-1, keepdims=True) + es)
    return p * inv, es * inv


def _norm_keys(kraw, gk):
    out = []
    for h in range(NKV):
        kh = kraw[:, h * HP:(h + 1) * HP]
        rk, khat = _rms_fwd(kh, 1.0 / HD)
        out.append((khat, rk, (khat * gk).astype(BF16)))
    return out


def _mixer_fwd(proj, x, cw, gq, gk, sinks, gco, gao, wo, tq):
    t = proj.shape[0]
    nb = tq // BLK
    r8 = tq // 8

    def body(p_ref, cgp_ref, hcp_ref, kvp_ref, x_ref, cw_ref, gq_ref, gk_ref, sk_ref, gco_ref, gao_ref,
             wo_ref, xm_ref, mix_ref, ao_ref, aop_ref):
        i = pl.program_id(0)
        cg = p_ref[:, O_CG:O_CG + CC]
        hc = p_ref[:, O_HC:O_HC + CC]
        u = cg * hc
        up = jnp.where(i > 0, cgp_ref[...] * hcp_ref[...], 0.0)
        u1, u2 = _conv_taps(jnp.concatenate([up, u], axis=0), tq)
        y = cw_ref[0:1, :] * u2 + cw_ref[1:2, :] * u1 + cw_ref[2:3, :] * u
        co = p_ref[:, O_BG:O_BG + CC] * y
        _, coh = _rms_fwd(co, 1.0 / CC)
        cn = coh * gco_ref[...]
        kraw = jnp.concatenate([kvp_ref[:, 0:NKV * HP], p_ref[:, O_K:O_K + NKV * HP]], axis=0)
        vraw = jnp.concatenate([kvp_ref[:, NKV * HP:], p_ref[:, O_V:O_V + NKV * HP]], axis=0)
        keys = _norm_keys(kraw, gk_ref[...])
        vb = [vraw[:, h * HP:(h + 1) * HP].astype(BF16) for h in range(NKV)]
        base_valid, c_io = _band_mask()
        gqs = gq_ref[...] * SCALE
        for b in range(nb):
            lo = jnp.where(i * nb + b == 0, BLK, 0)
            valid = base_valid & (c_io >= lo)
            for g in range(NQ):
                h = g // GRP
                qg = p_ref[b * BLK:(b + 1) * BLK, O_Q + g * HP:O_Q + (g + 1) * HP]
                _, qh = _rms_fwd(qg, 1.0 / HD)
                qs = (qh * gqs).astype(BF16)
                pr, _ = _attn_probs(qs, keys[h][2][b * BLK:b * BLK + 2 * BLK], sk_ref[0, g], valid)
                aop_ref[b * BLK:(b + 1) * BLK, g * HP:(g + 1) * HP] = jnp.dot(
                    pr.astype(BF16), vb[h][b * BLK:b * BLK + 2 * BLK], preferred_element_type=F32)
        for j in range(NQ // 2):
            ao_ref[:, j * HP:(j + 1) * HP] = (aop_ref[:, 2 * j * HP:(2 * j + 1) * HP]
                                              + pltpu.roll(aop_ref[:, (2 * j + 1) * HP:(2 * j + 2) * HP], HD, 1))
        _, aoh = _rms_fwd(ao_ref[...], 1.0 / (NQ * HD))
        an = aoh * gao_ref[...]
        mix = jnp.concatenate([cn, an], axis=1).astype(BF16)
        mix_ref[...] = mix
        xm_ref[...] = x_ref[...] + jnp.dot(mix, wo_ref[...], preferred_element_type=F32)

    prev8 = lambda col: pl.BlockSpec((8, CC), lambda i: (jnp.maximum(i * r8 - 1, 0), col))
    return pl.pallas_call(
        body, name="mixer_fwd", grid=(t // tq,),
        in_specs=[
            pl.BlockSpec((tq, NP), lambda i: (i, 0)),
            prev8(O_CG // CC), prev8(O_HC // CC),
            pl.BlockSpec((BLK, 2 * NKV * HP), lambda i: (jnp.maximum(i * nb - 1, 0), O_K // (2 * NKV * HP))),
            pl.BlockSpec((tq, D), lambda i: (i, 0)),
            _const_spec((8, CC)), _const_spec((1, HP)), _const_spec((1, HP)),
            pl.BlockSpec(memory_space=pltpu.SMEM),
            _const_spec((1, CC)), _const_spec((1, NQ * HD)), _const_spec((MIXW, D)),
        ],
        out_specs=[pl.BlockSpec((tq, D), lambda i: (i, 0)), pl.BlockSpec((tq, MIXW), lambda i: (i, 0)),
                   pl.BlockSpec((tq, NQ * HD), lambda i: (i, 0))],
        out_shape=[jax.ShapeDtypeStruct((t, D), F32), jax.ShapeDtypeStruct((t, MIXW), BF16),
                   jax.ShapeDtypeStruct((t, NQ * HD), F32)],
        scratch_shapes=[pltpu.VMEM((tq, NQ * HP), F32)],
        compiler_params=_cparams(("parallel",)),
    )(proj, proj, proj, proj, x, cw, gq, gk, sinks, gco, gao, wo)


def _ffn_weight_specs():
    return [pl.BlockSpec((N_CHIPS, FFB, D), lambda i, j=j: (0, j, 0), pipeline_mode=pl.Buffered(1))
            for j in range(3)]


def _ffn_fwd(xm, g2, gf, tm, tgt=None):
    t = xm.shape[0]
    last = tgt is not None

    def body(x_ref, g_ref, wg_ref, wu_ref, wd_ref, *rest):
        t_ref, rest = (rest[0], rest[1:]) if last else (None, rest)
        l_ref, rest = (rest[0], rest[1:]) if last else (None, rest)
        xo_ref, a_ref, b_ref, h2_ref = rest
        xv = x_ref[...]
        _, xh = _rms_fwd(xv, 1.0 / D)
        h2 = (xh * g_ref[...]).astype(BF16)
        h2_ref[...] = h2
        acc = xv
        for k in range(N_CHIPS):
            a = _nt(h2, wg_ref[k])
            b = _nt(h2, wu_ref[k])
            a_ref[k] = a.astype(BF16)
            b_ref[k] = b.astype(BF16)
            hm = (a * jax.nn.sigmoid(a) * b).astype(BF16)
            acc = acc + jnp.dot(hm, wd_ref[k], preferred_element_type=F32)
        if last:
            @pl.when(pl.program_id(0) == 0)
            def _():
                l_ref[...] = jnp.zeros_like(l_ref)

            e = acc - t_ref[...]
            xo_ref[...] = e * (1.0 / D)
            l_ref[...] += jnp.sum(jnp.sum(e * e, axis=-1, keepdims=True), axis=0, keepdims=True) * (0.5 / D)
        else:
            xo_ref[...] = acc

    row = lambda w: pl.BlockSpec((tm, w), lambda i: (i, 0))
    blk = pl.BlockSpec((N_CHIPS, tm, FFB), lambda i: (0, i, 0))
    bsd = jax.ShapeDtypeStruct((N_CHIPS, t, FFB), BF16)
    return pl.pallas_call(
        body, name="ffn_fwd_loss" if last else "ffn_fwd", grid=(t // tm,),
        in_specs=[row(D), _const_spec((1, D))] + _ffn_weight_specs() + ([row(D)] if last else []),
        out_specs=([pl.BlockSpec((8, 128), lambda i: (0, 0))] if last else []) + [row(D), blk, blk, row(D)],
        out_shape=([jax.ShapeDtypeStruct((8, 128), F32)] if last else [])
        + [jax.ShapeDtypeStruct((t, D), F32), bsd, bsd, jax.ShapeDtypeStruct((t, D), BF16)],
        compiler_params=_cparams(("arbitrary" if last else "parallel",)),
    )(*((xm, g2, gf, gf, gf) + ((tgt,) if last else ())))


def _ffn_bwd(dy, xm, g2, a, b, gf, tm):
    t = dy.shape[0]

    def body(dy_ref, x_ref, g_ref, a_ref, b_ref, wg_ref, wu_ref, wd_ref, dx_ref, da_ref, db_ref, hm_ref, dg_ref):
        @pl.when(pl.program_id(0) == 0)
        def _():
            dg_ref[...] = jnp.zeros_like(dg_ref)

        dyv = dy_ref[...]
        dyb = dyv.astype(BF16)
        dh2 = jnp.zeros_like(dyv)
        for k in range(N_CHIPS):
            dhm = _nt(dyb, wd_ref[k])
            av = a_ref[k].astype(F32)
            bv = b_ref[k].astype(F32)
            sig = jax.nn.sigmoid(av)
            sil = av * sig
            hm_ref[k] = (sil * bv).astype(BF16)
            da = (dhm * bv * (sig * (1.0 + av * (1.0 - sig)))).astype(BF16)
            db = (dhm * sil).astype(BF16)
            da_ref[k] = da
            db_ref[k] = db
            dh2 = (dh2 + jnp.dot(da, wg_ref[k], preferred_element_type=F32)
                   + jnp.dot(db, wu_ref[k], preferred_element_type=F32))
        r, xh = _rms_fwd(x_ref[...], 1.0 / D)
        dg_ref[...] += jnp.sum(dh2 * xh, axis=0, keepdims=True)
        dx_ref[...] = dyv + _rms_bwd(dh2, g_ref[...], xh, r, 1.0 / D)

    row = lambda w: pl.BlockSpec((tm, w), lambda i: (i, 0))
    blk = pl.BlockSpec((N_CHIPS, tm, FFB), lambda i: (0, i, 0))
    bsd = jax.ShapeDtypeStruct((N_CHIPS, t, FFB), BF16)
    return pl.pallas_call(
        body, name="ffn_bwd", grid=(t // tm,),
        in_specs=[row(D), row(D), _const_spec((1, D)), blk, blk] + _ffn_weight_specs(),
        out_specs=[row(D), blk, blk, blk, pl.BlockSpec((1, D), lambda i: (0, 0))],
        out_shape=[jax.ShapeDtypeStruct((t, D), F32), bsd, bsd, bsd, jax.ShapeDtypeStruct((1, D), F32)],
        compiler_params=_cparams(("arbitrary",)),
    )(dy, xm, g2, a, b, gf, gf, gf)


def _wgrad_blocks(a, b, tt, name):
    _, t, rows = a.shape
    cols = b.shape[1]
    nsteps = t // tt

    def body(a_ref, b_ref, o_ref, acc_ref):
        s = pl.program_id(0)

        @pl.when(s == 0)
        def _():
            acc_ref[...] = jnp.zeros_like(acc_ref)

        bv = b_ref[...].astype(BF16)
        for k in range(N_CHIPS):
            acc_ref[k] += _tn(a_ref[k], bv)

        @pl.when(s == nsteps - 1)
        def _():
            o_ref[...] = acc_ref[...].astype(BF16)

    return pl.pallas_call(
        body, name=name, grid=(nsteps,),
        in_specs=[pl.BlockSpec((N_CHIPS, tt, rows), lambda s: (0, s, 0)), pl.BlockSpec((tt, cols), lambda s: (s, 0))],
        out_specs=pl.BlockSpec((N_CHIPS, rows, cols), lambda s: (0, 0, 0)),
        out_shape=jax.ShapeDtypeStruct((N_CHIPS, rows, cols), BF16),
        scratch_shapes=[pltpu.VMEM((N_CHIPS, rows, cols), F32)],
        compiler_params=_cparams(("arbitrary",)),
    )(a, b)


def _head_rows(first, n_heads):
    return [(first + g * HD, first + g * HP, HD) for g in range(n_heads)]


def _wgrad(a, b, tt, name):
    t, k = a.shape
    n = b.shape[1]
    nsteps = t // tt

    def body(a_ref, b_ref, o_ref, acc_ref):
        s = pl.program_id(0)

        @pl.when(s == 0)
        def _():
            acc_ref[...] = jnp.zeros_like(acc_ref)

        acc_ref[...] += _tn(a_ref[...].astype(BF16), b_ref[...].astype(BF16))

        @pl.when(s == nsteps - 1)
        def _():
            o_ref[...] = acc_ref[...].astype(BF16)

    return pl.pallas_call(
        body, name=name, grid=(nsteps,),
        in_specs=[pl.BlockSpec((tt, k), lambda s: (s, 0)), pl.BlockSpec((tt, n), lambda s: (s, 0))],
        out_specs=pl.BlockSpec((k, n), lambda s: (0, 0)),
        out_shape=jax.ShapeDtypeStruct((k, n), BF16),
        scratch_shapes=[pltpu.VMEM((k, n), F32)],
        compiler_params=_cparams(("arbitrary",)),
    )(a, b)


def _wgrad_in(dpm, dkv, h, tt):
    t = h.shape[0]
    nsteps = t // tt
    kvw = dkv.shape[1]
    pieces = [(0, 0, O_Q)] + _head_rows(O_Q, NQ + 2 * NKV)

    def body(m_ref, kv_ref, h_ref, o_ref, acc_ref):
        s = pl.program_id(0)

        @pl.when(s == 0)
        def _():
            acc_ref[...] = jnp.zeros_like(acc_ref)

        hv = h_ref[...]
        acc_ref[:NMAIN, :] += _tn(m_ref[...], hv)
        acc_ref[NMAIN:, :] += _tn(kv_ref[...], hv)

        @pl.when(s == nsteps - 1)
        def _():
            for dst, src, size in pieces:
                o_ref[dst:dst + size, :] = acc_ref[src:src + size, :].astype(BF16)

    return pl.pallas_call(
        body, name="wgrad_in", grid=(nsteps,),
        in_specs=[pl.BlockSpec((tt, NMAIN), lambda s: (s, 0)), pl.BlockSpec((tt, kvw), lambda s: (s, 0)),
                  pl.BlockSpec((tt, D), lambda s: (s, 0))],
        out_specs=pl.BlockSpec((W_IN_ROWS, D), lambda s: (0, 0)),
        out_shape=jax.ShapeDtypeStruct((W_IN_ROWS, D), BF16),
        scratch_shapes=[pltpu.VMEM((NMAIN + kvw, D), F32)],
        compiler_params=_cparams(("arbitrary",)),
    )(dpm, dkv, h)


def _mixer_bwd(dxm, proj, ao, cw, gq, gk, sinks, gco, gao, wo, tq):
    t = proj.shape[0]
    nb = tq // BLK
    r8 = tq // 8
    nt = t // tq
    te = tq + 8
    kvw = 2 * NKV * HP

    def body(dx_ref, dxn_ref, p_ref, cgp_ref, hcp_ref, bgn_ref, cgn_ref, hcn_ref, kvp_ref, ao_ref, cw_ref, gq_ref,
             gk_ref, sk_ref, gco_ref, gao_ref, wo_ref,
             dpm_ref, dkvm_ref, dkvh_ref, dcw_ref, dgq_ref, dgk_ref, dsk_ref, dgco_ref, dgao_ref, acc_ref):
        i = pl.program_id(0)

        @pl.when(i == 0)
        def _():
            for r in (dcw_ref, dgq_ref, dgk_ref, dsk_ref, dgco_ref, dgao_ref):
                r[...] = jnp.zeros_like(r)

        acc_ref[...] = jnp.zeros_like(acc_ref)
        live_rows = jnp.where(i < nt - 1, te, tq)
        dxb = dx_ref[...].astype(BF16)
        dxe = jnp.concatenate([dxb, dxn_ref[...].astype(BF16)], axis=0)
        dcn = _nt(dxe, wo_ref[0:CC, :])
        bg = jnp.concatenate([p_ref[:, O_BG:O_BG + CC], bgn_ref[...]], axis=0)
        cg = jnp.concatenate([p_ref[:, O_CG:O_CG + CC], cgn_ref[...]], axis=0)
        hc = jnp.concatenate([p_ref[:, O_HC:O_HC + CC], hcn_ref[...]], axis=0)
        u = cg * hc
        up = jnp.where(i > 0, cgp_ref[...] * hcp_ref[...], 0.0)
        u1, u2 = _conv_taps(jnp.concatenate([up, u], axis=0), te)
        w0, w1, w2 = cw_ref[0:1, :], cw_ref[1:2, :], cw_ref[2:3, :]
        y = w0 * u2 + w1 * u1 + w2 * u
        co = bg * y
        rc, coh = _rms_fwd(co, 1.0 / CC)
        dco = _rms_bwd(dcn, gco_ref[...], coh, rc, 1.0 / CC)
        row_io = lax.broadcasted_iota(jnp.int32, (te, 1), 0)
        own = row_io < tq
        dgco_ref[...] += jnp.sum(jnp.where(own, dcn * coh, 0.0), axis=0, keepdims=True)
        dyc = jnp.where(row_io < live_rows, dco * bg, 0.0)
        dyo = jnp.where(own, dyc, 0.0)
        dcw_ref[0:1, :] += jnp.sum(dyo * u2, axis=0, keepdims=True)
        dcw_ref[1:2, :] += jnp.sum(dyo * u1, axis=0, keepdims=True)
        dcw_ref[2:3, :] += jnp.sum(dyo * u, axis=0, keepdims=True)
        dy1 = pltpu.roll(dyc, te - 1, 0)[0:tq]
        dy2 = pltpu.roll(dyc, te - 2, 0)[0:tq]
        du = w2 * dyc[0:tq] + w1 * dy1 + w0 * dy2
        dpm_ref[:, O_BG:O_BG + CC] = (dco[0:tq] * y[0:tq]).astype(BF16)
        dpm_ref[:, O_CG:O_CG + CC] = (du * hc[0:tq]).astype(BF16)
        dpm_ref[:, O_HC:O_HC + CC] = (du * cg[0:tq]).astype(BF16)
        kraw = jnp.concatenate([kvp_ref[:, 0:NKV * HP], p_ref[:, O_K:O_K + NKV * HP]], axis=0)
        vraw = jnp.concatenate([kvp_ref[:, NKV * HP:], p_ref[:, O_V:O_V + NKV * HP]], axis=0)
        gqv, gkv = gq_ref[...], gk_ref[...]
        keys = _norm_keys(kraw, gkv)
        vb = [vraw[:, h * HP:(h + 1) * HP].astype(BF16) for h in range(NKV)]
        base_valid, c_io = _band_mask()
        lane = lax.broadcasted_iota(jnp.int32, (1, HP), 1)
        dgq, dgk, dsk = (jnp.zeros((1, HP), F32) for _ in range(3))
        dgao = jnp.zeros((1, NQ * HD), F32)
        for b in range(nb):
            lo = jnp.where(i * nb + b == 0, BLK, 0)
            valid = base_valid & (c_io >= lo)
            band = slice(b * BLK, b * BLK + 2 * BLK)
            blk = slice(b * BLK, (b + 1) * BLK)
            ra, aoh = _rms_fwd(ao_ref[blk, :], 1.0 / (NQ * HD))
            danb = _nt(dxb[blk], wo_ref[CC:MIXW, :])
            dgao = dgao + jnp.sum(danb * aoh, axis=0, keepdims=True)
            dao = _rms_bwd(danb, gao_ref[...], aoh, ra, 1.0 / (NQ * HD))
            dos = [dao[:, g // 2 * HP:(g // 2 + 1) * HP] for g in range(NQ)]
            dos = [(d if g % 2 == 0 else pltpu.roll(d, HD, 1)).astype(BF16) for g, d in enumerate(dos)]
            fwd = []
            for g in range(NQ):
                rq, qh = _rms_fwd(p_ref[blk, O_Q + g * HP:O_Q + (g + 1) * HP], 1.0 / HD)
                qs = (qh * (gqv * SCALE)).astype(BF16)
                fwd.append((rq, qh, qs) + _attn_probs(qs, keys[g // GRP][2][band], sk_ref[0, g], valid))
            dqs = []
            for h in range(NKV):
                khat, rk, kn = [a[band] for a in keys[h]]
                dss, prbs, qns, dobs = [], [], [], []
                for g in range(h * GRP, (h + 1) * GRP):
                    rq, qh, qs, pr, ps = fwd[g]
                    dob = dos[g]
                    dp = _nt(dob, vb[h][band])
                    delta = jnp.sum(pr * dp, axis=-1, keepdims=True)
                    dsb = (pr * (dp - delta)).astype(BF16)
                    dsk = dsk + jnp.where(lane == g, -jnp.sum(ps * delta, axis=0, keepdims=True), 0.0)
                    dqn = jnp.dot(dsb, kn, preferred_element_type=F32) * SCALE
                    dgq = dgq + jnp.sum(dqn * qh, axis=0, keepdims=True)
                    dqs.append(_rms_bwd(dqn, gqv, qh, rq, 1.0 / HD).astype(BF16))
                    dss.append(dsb)
                    prbs.append(pr.astype(BF16))
                    qns.append(qs)
                    dobs.append(dob)
                dkn = _tn(jnp.concatenate(dss, axis=0), jnp.concatenate(qns, axis=0))
                dv = _tn(jnp.concatenate(prbs, axis=0), jnp.concatenate(dobs, axis=0))
                dgk = dgk + jnp.sum(dkn * khat, axis=0, keepdims=True)
                acc_ref[band, h * HP:(h + 1) * HP] += _rms_bwd(dkn, gkv, khat, rk, 1.0 / HD)
                acc_ref[band, (NKV + h) * HP:(NKV + h + 1) * HP] += dv
            dpm_ref[blk, O_Q:O_K] = jnp.concatenate(dqs, axis=1)
        dgq_ref[...] += dgq
        dgk_ref[...] += dgk
        dsk_ref[...] += dsk
        dgao_ref[...] += dgao
        dkvh_ref[...] = acc_ref[0:BLK, :]
        dkvm_ref[...] = acc_ref[BLK:, :]

    prev8 = lambda col: pl.BlockSpec((8, CC), lambda i: (jnp.maximum(i * r8 - 1, 0), col))
    next8 = lambda col: pl.BlockSpec((8, CC), lambda i: (jnp.minimum((i + 1) * r8, t // 8 - 1), col))
    small = lambda n: pl.BlockSpec((1, n), lambda i: (0, 0))
    return pl.pallas_call(
        body, name="mixer_bwd", grid=(nt,),
        in_specs=[
            pl.BlockSpec((tq, D), lambda i: (i, 0)),
            pl.BlockSpec((8, D), lambda i: (jnp.minimum((i + 1) * r8, t // 8 - 1), 0)),
            pl.BlockSpec((tq, NP), lambda i: (i, 0)),
            prev8(O_CG // CC), prev8(O_HC // CC),
            next8(O_BG // CC), next8(O_CG // CC), next8(O_HC // CC),
            pl.BlockSpec((BLK, kvw), lambda i: (jnp.maximum(i * nb - 1, 0), O_K // kvw)),
            pl.BlockSpec((tq, NQ * HD), lambda i: (i, 0)),
            _const_spec((8, CC)), _const_spec((1, HP)), _const_spec((1, HP)),
            pl.BlockSpec(memory_space=pltpu.SMEM),
            _const_spec((1, CC)), _const_spec((1, NQ * HD)), _const_spec((MIXW, D)),
        ],
        out_specs=[
            pl.BlockSpec((tq, NMAIN), lambda i: (i, 0)),
            pl.BlockSpec((tq, kvw), lambda i: (i, 0)),
            pl.BlockSpec((BLK, kvw), lambda i: (i, 0)),
            pl.BlockSpec((8, CC), lambda i: (0, 0)), small(HP), small(HP), small(HP), small(CC), small(NQ * HD),
        ],
        out_shape=[
            jax.ShapeDtypeStruct((t, NMAIN), BF16), jax.ShapeDtypeStruct((t, kvw), F32),
            jax.ShapeDtypeStruct((nt * BLK, kvw), F32),
            jax.ShapeDtypeStruct((8, CC), F32), jax.ShapeDtypeStruct((1, HP), F32), jax.ShapeDtypeStruct((1, HP), F32),
            jax.ShapeDtypeStruct((1, HP), F32), jax.ShapeDtypeStruct((1, CC), F32),
            jax.ShapeDtypeStruct((1, NQ * HD), F32),
        ],
        scratch_shapes=[pltpu.VMEM((tq + BLK, kvw), F32)],
        compiler_params=_cparams(("arbitrary",)),
    )(dxm, dxm, proj, proj, proj, proj, proj, proj, proj, ao, cw, gq, gk, sinks, gco, gao, wo)


def _inproj_bwd(dpm, dkvm, dkvh, wpt, x, g1, dxm, tm):
    t = x.shape[0]
    kvw = 2 * NKV * HP
    nt = t // tm

    def body(dp_ref, dk_ref, dh_ref, w_ref, x_ref, g_ref, dxm_ref, dx_ref, dg_ref, dkv_ref):
        i = pl.program_id(0)

        @pl.when(i == 0)
        def _():
            dg_ref[...] = jnp.zeros_like(dg_ref)

        halo = jnp.where(i < nt - 1, dh_ref[...], 0.0)
        dkv_ref[0:tm - BLK, :] = dk_ref[0:tm - BLK, :].astype(BF16)
        dkv_ref[tm - BLK:tm, :] = (dk_ref[tm - BLK:tm, :] + halo).astype(BF16)
        dh = (jnp.dot(dp_ref[...], w_ref[0:NMAIN, :], preferred_element_type=F32)
              + jnp.dot(dkv_ref[...], w_ref[NMAIN:NP, :], preferred_element_type=F32))
        r, xh = _rms_fwd(x_ref[...], 1.0 / D)
        dg_ref[...] += jnp.sum(dh * xh, axis=0, keepdims=True)
        dx_ref[...] = dxm_ref[...] + _rms_bwd(dh, g_ref[...], xh, r, 1.0 / D)

    row = lambda w: pl.BlockSpec((tm, w), lambda i: (i, 0))
    return pl.pallas_call(
        body, name="inproj_bwd", grid=(nt,),
        in_specs=[row(NMAIN), row(kvw), pl.BlockSpec((BLK, kvw), lambda i: (jnp.minimum(i + 1, nt - 1), 0)),
                  _const_spec((NP, D)), row(D), _const_spec((1, D)), row(D)],
        out_specs=[row(D), pl.BlockSpec((1, D), lambda i: (0, 0)), row(kvw)],
        out_shape=[jax.ShapeDtypeStruct((t, D), F32), jax.ShapeDtypeStruct((1, D), F32),
                   jax.ShapeDtypeStruct((t, kvw), BF16)],
        compiler_params=_cparams(("arbitrary",)),
    )(dpm, dkvm, dkvh, wpt, x, g1, dxm)


def _rows_tile(rows, cap=512):
    for cand in range(min(rows, cap) // 16 * 16, 0, -16):
        if rows % cand == 0:
            return cand
    return rows


def _presum_halves(gs, theirs, core, chip):
    n = len(gs)

    def body(c_ref, chip_ref, *refs):
        for g_ref, t_ref, o_ref, keep_ref in zip(refs[:n], refs[n:2 * n], refs[2 * n:3 * n], refs[3 * n:]):
            val = (g_ref[...].astype(F32) + t_ref[...].astype(F32)).astype(BF16)
            o_ref[...] = val

            @pl.when(pl.program_id(0) == chip_ref[0])
            def _():
                keep_ref[...] = val

    half = lambda ta: pl.BlockSpec((None,) + ta.shape[1:], lambda k, c_ref, chip_ref: (k, 0, 0))
    own = lambda ta: pl.BlockSpec((None,) + ta.shape[1:], lambda k, c_ref, chip_ref: (k, c_ref[0], 0))
    kept = lambda ta: pl.BlockSpec(ta.shape[1:], lambda k, c_ref, chip_ref: (0, 0))
    res = pl.pallas_call(
        body, name="presum",
        grid_spec=pltpu.PrefetchScalarGridSpec(
            num_scalar_prefetch=2, grid=(N_CHIPS,),
            in_specs=[own(ta) for ta in theirs] + [half(ta) for ta in theirs],
            out_specs=[half(ta) for ta in theirs] + [kept(ta) for ta in theirs]),
        out_shape=[jax.ShapeDtypeStruct(ta.shape, BF16) for ta in theirs]
        + [jax.ShapeDtypeStruct(ta.shape[1:], BF16) for ta in theirs],
        compiler_params=_cparams(("arbitrary",)),
    )(core, chip, *gs, *theirs)
    return res[:n], res[n:]


def _sum_chips(got, kept, chip):
    n = len(got)
    steps = 2

    def body(chip_ref, *refs):
        for c_ref, own_ref, o_ref in zip(refs[:n], refs[n:2 * n], refs[2 * n:]):
            acc = None
            for j in range(N_CHIPS):
                term = jnp.where(chip_ref[0] == j, own_ref[...], c_ref[j]).astype(F32)
                acc = term if acc is None else acc + term
            o_ref[...] = acc

    tile = lambda c: (c.shape[1] // steps, c.shape[2])
    return pl.pallas_call(
        body, name="chipsum",
        grid_spec=pltpu.PrefetchScalarGridSpec(
            num_scalar_prefetch=1, grid=(steps,),
            in_specs=[pl.BlockSpec((N_CHIPS,) + tile(c), lambda i, chip_ref: (0, i, 0)) for c in got]
            + [pl.BlockSpec(tile(c), lambda i, chip_ref: (i, 0)) for c in got],
            out_specs=[pl.BlockSpec(tile(c), lambda i, chip_ref: (i, 0)) for c in got]),
        out_shape=[jax.ShapeDtypeStruct(c.shape[1:], F32) for c in got],
        compiler_params=_cparams(("parallel",)),
    )(chip, *got, *kept)


def _adamw_refs(w_ref, g_ref, m_ref, v_ref, d_ref, mo_ref, vo_ref):
    c1 = 1.0 - ADAM_B1 ** ADAM_STEP
    c2 = 1.0 - ADAM_B2 ** ADAM_STEP
    gv = g_ref[...]
    mn = ADAM_B1 * m_ref[...] + (1.0 - ADAM_B1) * gv
    vn = ADAM_B2 * v_ref[...] + (1.0 - ADAM_B2) * (gv * gv)
    mo_ref[...] = mn
    vo_ref[...] = vn
    d_ref[...] = -ADAM_LR * ((mn / c1) / (jnp.sqrt(vn / c2) + ADAM_EPS) + ADAM_WD * w_ref[...])


def _adamw_small(ws, gs, ms, vs):
    n = len(ws)

    def body(*refs):
        for k in range(n):
            _adamw_refs(*refs[k::n])

    res = pl.pallas_call(
        body, name="adamw_small", out_shape=[jax.ShapeDtypeStruct(w.shape, F32) for w in ws] * 3,
        compiler_params=_cparams(),
    )(*ws, *gs, *ms, *vs)
    return res[:n], res[n:2 * n], res[2 * n:]


def _adamw(w, g, m, v, name):
    rows, cols = w.shape
    tr = _rows_tile(rows)

    def body(*refs):
        _adamw_refs(*refs)

    spec = pl.BlockSpec((tr, cols), lambda i: (i, 0))
    sds = jax.ShapeDtypeStruct((rows, cols), F32)
    return pl.pallas_call(
        body, name=name, grid=(rows // tr,), in_specs=[spec] * 4, out_specs=[spec] * 3, out_shape=[sds] * 3,
        compiler_params=_cparams(("parallel",)),
    )(w, g, m, v)


def _place():
    x, y, c = lax.axis_index("x"), lax.axis_index("y"), lax.axis_index("c")
    chips = [(1 - x, y), (x, 1 - y), (1 - x, 1 - y)]
    return x, y, c, chips


ANY = pl.BlockSpec(memory_space=pl.ANY)
DMA_ROWS = 64


def _pieces(shape):
    rows = shape[-2]
    step = DMA_ROWS if rows % DMA_ROWS == 0 else rows
    lead = [()]
    for n in shape[:-2]:
        lead = [i + (k,) for i in lead for k in range(n)]
    return [i + (pl.ds(r0, step),) for i in lead for r0 in range(0, rows, step)]


def _start_pieces(make, src, dst):
    for idx in _pieces(src.shape):
        make(src.at[idx], dst.at[idx]).start()


def _gather_body(srcs, outs, sems, layer, start):
    nw = len(srcs)
    ssem, rsem, fssem, frsem = sems
    x, y, c, chips = _place()
    kme = 2 * x + y

    def plane(j, w, to):
        return lambda s, d: pltpu.make_async_remote_copy(
            src_ref=s, dst_ref=d, send_sem=ssem.at[j, w], recv_sem=rsem.at[j, w], device_id=to,
            device_id_type=MESH)

    def passed(j, w):
        return lambda s, d: pltpu.make_async_remote_copy(
            src_ref=s, dst_ref=d, send_sem=fssem.at[j, w], recv_sem=frsem.at[j, w],
            device_id=(x, y, 1 - c), device_id_type=MESH)

    @pl.when(c == layer)
    def _():
        for j, (px, py) in enumerate(chips):
            for w in range(nw):
                start(plane(j, w, (px, py, c)), srcs[w], outs[w].at[kme])
        for j, (px, py) in enumerate(chips):
            for w in range(nw):
                got = outs[w].at[2 * px + py]
                plane(j, w, (px, py, c))(got, got).wait_recv()
                start(passed(j, w), got, got)
        for j, (px, py) in enumerate(chips):
            for w in range(nw):
                got = outs[w].at[2 * px + py]
                plane(j, w, (px, py, c))(got, got).wait_send()
                passed(j, w)(got, got).wait_send()

    @pl.when(c != layer)
    def _():
        for j, (px, py) in enumerate(chips):
            for w in range(nw):
                got = outs[w].at[2 * px + py]
                passed(j, w)(got, got).wait_recv()


def _handshake(peers):
    barrier = pltpu.get_barrier_semaphore()
    for peer in peers:
        pl.semaphore_signal(barrier, inc=1, device_id=peer, device_id_type=MESH)
    pl.semaphore_wait(barrier, len(peers))


def _handshake_all():
    x, y, c, _ = _place()
    _handshake([(x ^ (r >> 2), y ^ ((r >> 1) & 1), c ^ (r & 1)) for r in range(1, 8)])


def _gather_layer_async(blocks, layer, name, collective_id):
    hbm = pltpu.MemorySpace.HBM
    srcs = [jax.new_ref(b, memory_space=hbm) for b in blocks]
    outs = [jax.empty_ref(jax.ShapeDtypeStruct((N_CHIPS,) + b.shape, b.dtype), memory_space=hbm) for b in blocks]

    @pl.kernel(mesh=plsc.ScalarSubcoreMesh(axis_name="seq", num_cores=1), name=name,
               scratch_types=[pltpu.SemaphoreType.DMA((3, len(blocks)))] * 4,
               compiler_params=pltpu.CompilerParams(collective_id=collective_id))
    def launch(*sems):
        _handshake_all()
        _gather_body(srcs, outs, sems, layer, lambda make, s, d: make(s, d).start())

    launch()
    return [o[...] for o in outs]


def _swap_siblings(arrs, halves, name, collective_id=None):
    nw = len(arrs)
    out_sds = [jax.ShapeDtypeStruct((a.shape[0], a.shape[1] // 2, a.shape[2]) if halves else a.shape, a.dtype)
               for a in arrs]

    def exchange(srcs, outs, ssem, rsem, start):
        x, y, c, _ = _place()

        def give(w):
            return lambda s, d: pltpu.make_async_remote_copy(
                src_ref=s, dst_ref=d, send_sem=ssem.at[w], recv_sem=rsem.at[w], device_id=(x, y, 1 - c),
                device_id_type=MESH)

        for w in range(nw):
            hr = outs[w].shape[1]
            start(give(w), srcs[w].at[:, pl.ds((1 - c) * hr, hr)] if halves else srcs[w], outs[w])
        for w in range(nw):
            give(w)(outs[w], outs[w]).wait()

    if collective_id is None:
        def body(*refs):
            exchange(refs[:nw], refs[nw:2 * nw], *refs[2 * nw:], _start_pieces)

        return pl.pallas_call(
            body, name=name, in_specs=[ANY] * nw, out_specs=[ANY] * nw, out_shape=out_sds,
            scratch_shapes=[pltpu.SemaphoreType.DMA((nw,))] * 2,
            compiler_params=_cparams(has_side_effects=True),
        )(*arrs)

    hbm = pltpu.MemorySpace.HBM
    srcs = [jax.new_ref(a, memory_space=hbm) for a in arrs]
    outs = [jax.empty_ref(sds, memory_space=hbm) for sds in out_sds]

    @pl.kernel(mesh=plsc.ScalarSubcoreMesh(axis_name="seq", num_cores=1), name=name,
               scratch_types=[pltpu.SemaphoreType.DMA((nw,))] * 2,
               compiler_params=pltpu.CompilerParams(collective_id=collective_id))
    def launch(ssem, rsem):
        x, y, c, _ = _place()
        _handshake([(x, y, 1 - c)])
        exchange(srcs, outs, ssem, rsem, lambda make, s, d: make(s, d).start())

    launch()
    return [o[...] for o in outs]


def _scatter_body(srcs, outs, sems, start):
    nw = len(srcs)
    ssem, rsem = sems
    x, y, c, chips = _place()
    kme = 2 * x + y

    def give(j, w, to):
        return lambda s, d: pltpu.make_async_remote_copy(
            src_ref=s, dst_ref=d, send_sem=ssem.at[j, w], recv_sem=rsem.at[j, w], device_id=to,
            device_id_type=MESH)

    for j, (px, py) in enumerate(chips):
        for w in range(nw):
            start(give(j, w, (px, py, c)), srcs[w].at[2 * px + py], outs[w].at[kme])
    for j, (px, py) in enumerate(chips):
        for w in range(nw):
            got = outs[w].at[2 * px + py]
            give(j, w, (px, py, c))(got, got).wait_recv()
    for j, (px, py) in enumerate(chips):
        for w in range(nw):
            sent = srcs[w].at[2 * px + py]
            give(j, w, (px, py, c))(sent, sent).wait_send()


def _scatter_chips_async(ps, name, collective_id):
    hbm = pltpu.MemorySpace.HBM
    srcs = [jax.new_ref(p, memory_space=hbm) for p in ps]
    outs = [jax.empty_ref(jax.ShapeDtypeStruct(p.shape, p.dtype), memory_space=hbm) for p in ps]

    @pl.kernel(mesh=plsc.ScalarSubcoreMesh(axis_name="seq", num_cores=1), name=name,
               scratch_types=[pltpu.SemaphoreType.DMA((3, len(ps)))] * 2,
               compiler_params=pltpu.CompilerParams(collective_id=collective_id))
    def launch(*sems):
        _handshake_all()
        _scatter_body(srcs, outs, sems, lambda make, s, d: make(s, d).start())

    launch()
    return [o[...] for o in outs]


def _allreduce_small(v):
    rows = v.shape[0]

    def body(v_ref, o_ref, buf, ssem, rsem):
        x, y, c, _ = _place()
        me = 4 * x + 2 * y + c
        buf[me] = v_ref[...]
        sends = []
        for r in range(1, 8):
            peer = (x ^ (r >> 2), y ^ ((r >> 1) & 1), c ^ (r & 1))
            cp = pltpu.make_async_remote_copy(
                src_ref=v_ref, dst_ref=buf.at[me], send_sem=ssem.at[r - 1], recv_sem=rsem.at[r - 1],
                device_id=peer, device_id_type=MESH)
            cp.start()
            sends.append(cp)
        for r in range(1, 8):
            src = me ^ r
            pltpu.make_async_remote_copy(
                src_ref=v_ref, dst_ref=buf.at[src], send_sem=ssem.at[r - 1], recv_sem=rsem.at[r - 1],
                device_id=(x, y, c), device_id_type=MESH).wait_recv()
        for cp in sends:
            cp.wait_send()
        acc = buf[0]
        for d in range(1, 8):
            acc = acc + buf[d]
        o_ref[...] = acc

    vm = pl.BlockSpec(memory_space=pltpu.VMEM)
    return pl.pallas_call(
        body, name="allreduce_small", in_specs=[vm], out_specs=vm,
        out_shape=jax.ShapeDtypeStruct(v.shape, F32),
        scratch_shapes=[pltpu.VMEM((8, rows, 128), F32), pltpu.SemaphoreType.DMA((7,)),
                        pltpu.SemaphoreType.DMA((7,))],
        compiler_params=_cparams(has_side_effects=True),
    )(v)


def _t(w):
    return jnp.swapaxes(w, -1, -2)


def _count(shape):
    n = 1
    for s in shape:
        n *= s
    return n


def _pack_rows(arrs):
    flat = [jnp.pad(a.reshape(-1), (0, (-_count(a.shape)) % 128)) for a in arrs]
    v = jnp.concatenate(flat)
    rows = -(-v.shape[0] // (8 * 128)) * 8
    return jnp.pad(v, (0, rows * 128 - v.shape[0])).reshape(rows, 128)


def kernel(x, norm1_g, w_in, conv_w, q_norm_g, k_norm_g, sinks, conv_out_g, attn_out_g, w_o, norm2_g, w_gate, w_up, w_down, loss_target, m_norm1_g, m_w_in, m_conv_w, m_q_norm_g, m_k_norm_g, m_sinks, m_conv_out_g, m_attn_out_g, m_w_o, m_norm2_g, m_w_gate, m_w_up, m_w_down, v_norm1_g, v_w_in, v_conv_w, v_q_norm_g, v_k_norm_g, v_sinks, v_conv_out_g, v_attn_out_g, v_w_o, v_norm2_g, v_w_gate, v_w_up, v_w_down):
    depth = w_in.shape[0]
    t = x.shape[1]
    xs = x.reshape(t, D)
    tgt = loss_target.reshape(t, D)
    xi, yi = lax.axis_index("x"), lax.axis_index("y")
    kme = 2 * xi + yi
    tm = min(512, t)
    tq = min(512, t)
    tf = min(256, t)
    tw = min(1024, t)

    cwp = jnp.pad(conv_w.reshape(depth * 3, CC // N_CHIPS), ((0, 8 - depth * 3), (0, 0)))
    own_f = [jnp.concatenate([_t(w_gate[l]), _t(w_up[l]), w_down[l]], axis=0).astype(BF16) for l in range(depth)]
    own_o = [w_o[l].astype(BF16) for l in range(depth)]
    own_i = [_t(w_in[l]).astype(BF16) for l in range(depth)]
    mine = lambda got, own: lax.dynamic_update_index_in_dim(got, own, kme, 0)
    (got_i0,) = _gather_layer_async([own_i[0]], 0, "gather_in0_seq", collective_id=14)
    got_ocw = _gather_layer_async([own_o[0], cwp], 0, "gather_o0_seq", collective_id=15)
    got_i0, own_f, own_o, own_i = lax.optimization_barrier((got_i0, own_f, own_o, own_i))
    gf0_in = lax.optimization_barrier((own_f[0], got_i0))[0]
    (got_f0,) = _gather_layer_async([gf0_in], 0, "gather_ffn0_seq", collective_id=6)

    chip = kme.reshape(1).astype(jnp.int32)

    def layer_params(l, got_o, cw_full):
        return dict(
            wo=mine(got_o, own_o[l]).reshape(MIXW, D),
            cw=jnp.pad(cw_full[l], ((0, 5), (0, 0))),
            g1=norm1_g[l].reshape(1, D), g2=norm2_g[l].reshape(1, D),
            gq=jnp.pad(q_norm_g[l], (0, HP - HD)).reshape(1, HP), gk=jnp.pad(k_norm_g[l], (0, HP - HD)).reshape(1, HP),
            sk=sinks[l].reshape(1, NQ), gco=conv_out_g[l].reshape(1, CC),
            gao=attn_out_g[l].reshape(1, NQ * HD))

    saved, layers = [], []
    cur = xs
    for l in range(depth):
        x_in = cur
        if l == 0:
            got_i = got_i0
        else:
            got_f1, got_o, got_i = lax.optimization_barrier((got_l1, cur))[0]
        proj, h, wpt = _inproj_fwd(cur, norm1_g[l].reshape(1, D), got_i, own_i[l], chip, tm)
        if l == 0:
            got_o, got_cw = lax.optimization_barrier((got_ocw, proj))[0]
            cw_full = mine(got_cw, cwp).transpose(1, 0, 2).reshape(8, CC)[:depth * 3].reshape(depth, 3, CC)
        p = layer_params(l, got_o, cw_full)
        p["wpt"] = wpt
        xm, mix, ao = _mixer_fwd(proj, cur, p["cw"], p["gq"], p["gk"], p["sk"], p["gco"], p["gao"], p["wo"], tq)
        if l == 0:
            got_f0 = lax.optimization_barrier((got_f0, xm))[0]
            l1_in = lax.optimization_barrier(([own_f[1], own_o[1], own_i[1]], got_f0))[0]
            got_l1 = _gather_layer_async(l1_in, 1, "gather_layer1_seq", collective_id=1)
        p["gf"] = mine(got_f0 if l == 0 else got_f1, own_f[l])
        layers.append(p)
        if l < depth - 1:
            cur, a, b, h2 = _ffn_fwd(xm, p["g2"], p["gf"], tm)
        else:
            lpart, dy, a, b, h2 = _ffn_fwd(xm, p["g2"], p["gf"], tm, tgt)
        saved.append(dict(x=x_in, proj=proj, h=h, xm=xm, mix=mix, ao=ao, a=a, b=b, h2=h2))

    ci = lax.axis_index("c")
    core = ci.reshape(1).astype(jnp.int32)
    rbig = [dict() for _ in range(depth)]
    gsmall = [None] * depth

    def after_(vals, after):
        return vals if after is None else lax.optimization_barrier((vals, after))[0]

    def reduce_1(gs, tag, ids):
        return gs, _swap_siblings(gs, True, f"swap_halves_{tag}_seq", ids[0]), tag, ids

    def reduce_2(state, after):
        gs, theirs, tag, ids = state
        ps, kept = _presum_halves(gs, after_(theirs, after), core, chip)
        return kept, _scatter_chips_async(ps, f"scatter_{tag}_seq", ids[1]), tag, ids

    def reduce_3(state, after):
        kept, got, tag, ids = state
        r_mine = _sum_chips(after_(got, after), kept, chip)
        return r_mine, _swap_siblings(r_mine, False, f"swap_reduced_{tag}" + ("_seq" if ids[2] else ""), ids[2])

    def reduce_4(state, after):
        r_mine, r_theirs = state
        return [jnp.where(ci == 0, jnp.concatenate([a, b], axis=0), jnp.concatenate([b, a], axis=0))
                for a, b in zip(r_mine, after_(r_theirs, after))]

    ids = {"ffn1": (7, 4, 8), "in1": (9, 5, 10), "ffn0": (11, 2, 12), "in0": (13, 3, None)}
    in_2 = scattering = None
    handed = {}
    for l in reversed(range(depth)):
        p, s = layers[l], saved[l]
        dxm, da, db, hm, dg2 = _ffn_bwd(dy, s["xm"], p["g2"], s["a"], s["b"], p["gf"], tf)
        if in_2 is not None:
            in_2 = reduce_2(in_2, dxm)
        g_wg = _wgrad_blocks(da, s["h2"], tw, "wgrad_gate")
        g_wu = _wgrad_blocks(db, s["h2"], tw, "wgrad_up")
        g_wd = _wgrad_blocks(hm, dy, tw, "wgrad_down")
        if in_2 is not None:
            handed[f"in{l + 1}"] = reduce_3(in_2, g_wd)
        ffn_1 = reduce_1([g_wg, g_wu, g_wd], f"ffn{l}", ids[f"ffn{l}"])
        dpm, dkvm, dkvh, dcw, dgq, dgk, dsk, dgco, dgao = _mixer_bwd(
            after_(dxm, scattering), s["proj"], s["ao"], p["cw"], p["gq"], p["gk"], p["sk"], p["gco"], p["gao"],
            p["wo"], tq)
        ffn_2 = reduce_2(ffn_1, dpm)
        scattering = ffn_2[0]
        g_o = _wgrad(s["mix"], dxm, tw, "wgrad_o")
        dx, dg1, dkv = _inproj_bwd(dpm, dkvm, dkvh, p["wpt"], s["x"], p["g1"], dxm, tq)
        g_in = _wgrad_in(dpm, dkv, s["h"], tw)
        dy = dx
        gsmall[l] = dict(g1=dg1, cw=dcw[:3], gq=dgq[0, :HD], gk=dgk[0, :HD], sk=dsk[0, :NQ], gco=dgco,
                         gao=dgao, g2=dg2)
        above = handed.get(f"ffn{l + 1}")
        handed[f"ffn{l}"] = reduce_3(ffn_2, g_in if above is None else (g_in, above[0]))
        in_2 = reduce_1([g_in.reshape(N_CHIPS, -1, D), g_o.reshape(N_CHIPS, -1, D)], f"in{l}", ids[f"in{l}"])
    grad_x = dy.reshape(x.shape)

    small_shapes = dict(g1=(D,), cw=(3, CC), gq=(HD,), gk=(HD,), sk=(NQ,), gco=(CC,), gao=(NQ * HD,), g2=(D,))
    red = _allreduce_small(_pack_rows([gsmall[l][n] for l in range(depth) for n in small_shapes]
                                      + [lpart[0:1, 0:1]])).reshape(-1)
    red_small, offs = {n: [] for n in small_shapes}, 0
    for l in range(depth):
        for n, shp in small_shapes.items():
            cnt = _count(shp)
            red_small[n].append(red[offs:offs + cnt].reshape(shp))
            offs += -(-cnt // 128) * 128
    loss = red[offs]
    g_small = {n: jnp.stack(v) for n, v in red_small.items()}
    g_cw = lax.dynamic_slice_in_dim(g_small["cw"], kme * (CC // N_CHIPS), CC // N_CHIPS, axis=2)

    weights = [norm1_g, w_in, conv_w, q_norm_g, k_norm_g, sinks, conv_out_g, attn_out_g, w_o, norm2_g, w_gate,
               w_up, w_down]
    moms = [m_norm1_g, m_w_in, m_conv_w, m_q_norm_g, m_k_norm_g, m_sinks, m_conv_out_g, m_attn_out_g, m_w_o,
            m_norm2_g, m_w_gate, m_w_up, m_w_down]
    vars_ = [v_norm1_g, v_w_in, v_conv_w, v_q_norm_g, v_k_norm_g, v_sinks, v_conv_out_g, v_attn_out_g, v_w_o,
             v_norm2_g, v_w_gate, v_w_up, v_w_down]
    n_w = len(weights)
    big_idx = dict(zip(("in", "o", "g", "u", "d"), (1, 8, 10, 11, 12)))
    small_idx = [n for n in range(n_w) if n not in big_idx.values()]
    grads, deltas, new_m, new_v = [None] * n_w, [None] * n_w, [None] * n_w, [None] * n_w
    for n, g in zip(small_idx, (g_small["g1"], g_cw, g_small["gq"], g_small["gk"], g_small["sk"], g_small["gco"],
                                g_small["gao"], g_small["g2"])):
        grads[n] = g

    def update_big(name):
        n = big_idx[name]
        g = jnp.stack([rbig[l][name] for l in range(depth)])
        flip = g.shape != weights[n].shape
        rows2d = lambda a3: (_t(a3) if flip else a3).reshape(-1, D)
        res = _adamw(rows2d(weights[n]), g.reshape(-1, D), rows2d(moms[n]), rows2d(vars_[n]), f"adamw_{n}")
        res = [g] + [r.reshape(g.shape) for r in res]
        grads[n], deltas[n], new_m[n], new_v[n] = [_t(r) for r in res] if flip else res

    for l in range(depth):
        rbig[l]["g"], rbig[l]["u"], rbig[l]["d"] = reduce_4(handed[f"ffn{l}"], red)
    rbig[1]["in"], rbig[1]["o"] = reduce_4(handed["in1"], red)
    update_big("g")
    in_2 = reduce_2(in_2, new_v[big_idx["g"]])
    update_big("u")
    update_big("d")
    rbig[0]["in"], rbig[0]["o"] = reduce_4(reduce_3(in_2, new_v[big_idx["d"]]), None)
    for name in ("in", "o"):
        update_big(name)
    res = _adamw_small(*[[arrs[n] for n in small_idx] for arrs in (weights, grads, moms, vars_)])
    for k, n in enumerate(small_idx):
        deltas[n], new_m[n], new_v[n] = res[0][k], res[1][k], res[2][k]
    return (loss, grad_x, *grads, *deltas, *new_m, *new_v)
```

```python
import jax
import jax.numpy as jnp
from jax import lax
from jax.experimental import pallas as pl
from jax.experimental.pallas import tpu as pltpu
from jax.experimental.pallas import tpu_sc as plsc

F32 = jnp.float32
BF16 = jnp.bfloat16

D = 1024
CC = 512
NQ = 8
NKV = 2
HD = 64
HP = 128
GRP = NQ // NKV
FF = 2816
FFB = FF // 4
BLK = 128
EPS = 1e-6
NEG = -1e30
SCALE = HD ** -0.5
O_BG, O_CG, O_HC, O_Q = 0, CC, 2 * CC, 3 * CC
O_K = O_Q + NQ * HP
O_V = O_K + NKV * HP
NP = O_V + NKV * HP
NMAIN = O_K
MIXW = CC + NQ * HD
N_CHIPS = 4
VMEM_LIMIT = 56 * 1024 * 1024
MESH = pl.DeviceIdType.MESH

ADAM_LR, ADAM_B1, ADAM_B2, ADAM_EPS, ADAM_WD, ADAM_STEP = 0.001, 0.9, 0.999, 1e-08, 0.01, 10


def _cparams(sem=None, **kw):
    if sem is not None:
        kw["dimension_semantics"] = sem
    return pltpu.CompilerParams(vmem_limit_bytes=VMEM_LIMIT, **kw)


def _const_spec(shape):
    nd = len(shape)
    return pl.BlockSpec(shape, lambda *_: (0,) * nd, pipeline_mode=pl.Buffered(1))


def _nt(a, b):
    return lax.dot_general(a, b, (((1,), (1,)), ((), ())), preferred_element_type=F32)


def _tn(a, b):
    return lax.dot_general(a, b, (((0,), (0,)), ((), ())), preferred_element_type=F32)


def _rms_fwd(x, inv_n):
    r = lax.rsqrt(jnp.sum(x * x, axis=-1, keepdims=True) * inv_n + EPS)
    return r, x * r


def _rms_bwd(dy, g, xh, r, inv_n):
    dxh = dy * g
    return r * (dxh - xh * (jnp.sum(dxh * xh, axis=-1, keepdims=True) * inv_n))


W_IN_ROWS = 3 * CC + (NQ + 2 * NKV) * HD
W_IN_BLOCK = W_IN_ROWS // N_CHIPS


def _padded_row(row):
    return row + max(row - O_Q, 0) // HD * (HP - HD)


def _w_in_pieces(k):
    first = k * W_IN_BLOCK
    plain = min(max(O_Q - first, 0), W_IN_BLOCK)
    pieces = [(0, first, plain)] if plain else []
    return pieces + [(r, _padded_row(first + r), HD) for r in range(plain, W_IN_BLOCK, HD)]


def _inproj_fwd(x, g1, gi, own_i, chip, tm):
    t = x.shape[0]

    def body(chip_ref, x_ref, g_ref, gi_ref, own_ref, p_ref, h_ref, w_ref, sem):
        @pl.when(pl.program_id(0) == 0)
        def _():
            for k in range(N_CHIPS):
                for src, dst, rows in _w_in_pieces(k):
                    @pl.when(chip_ref[0] == k)
                    def _():
                        pltpu.make_async_copy(own_ref.at[pl.ds(src, rows)], w_ref.at[pl.ds(dst, rows)], sem).start()

                    @pl.when(chip_ref[0] != k)
                    def _():
                        pltpu.make_async_copy(gi_ref.at[k, pl.ds(src, rows)], w_ref.at[pl.ds(dst, rows)], sem).start()
            for slot in range(NQ + 2 * NKV):
                w_ref[O_Q + slot * HP + HD:O_Q + (slot + 1) * HP, :] = jnp.zeros((HP - HD, D), BF16)
            landed = w_ref.at[pl.ds(0, W_IN_ROWS)]
            pltpu.make_async_copy(landed, landed, sem).wait()

        _, xh = _rms_fwd(x_ref[...], 1.0 / D)
        h = (xh * g_ref[...]).astype(BF16)
        h_ref[...] = h
        p_ref[...] = _nt(h, w_ref[...])

    const = lambda shape: pl.BlockSpec(shape, lambda i, c: (0,) * len(shape))
    return pl.pallas_call(
        body, name="inproj_fwd",
        grid_spec=pltpu.PrefetchScalarGridSpec(
            num_scalar_prefetch=1, grid=(t // tm,),
            in_specs=[pl.BlockSpec((tm, D), lambda i, c: (i, 0)), const((1, D)), ANY, ANY],
            out_specs=[pl.BlockSpec((tm, NP), lambda i, c: (i, 0)), pl.BlockSpec((tm, D), lambda i, c: (i, 0)),
                       const((NP, D))],
            scratch_shapes=[pltpu.SemaphoreType.DMA(())]),
        out_shape=[jax.ShapeDtypeStruct((t, NP), F32), jax.ShapeDtypeStruct((t, D), BF16),
                   jax.ShapeDtypeStruct((NP, D), BF16)],
        compiler_params=_cparams(("arbitrary",)),
    )(chip, x, g1, gi, own_i)


def _band_mask():
    r_io = lax.broadcasted_iota(jnp.int32, (BLK, 2 * BLK), 0)
    c_io = lax.broadcasted_iota(jnp.int32, (BLK, 2 * BLK), 1)
    return (c_io > r_io) & (c_io <= r_io + BLK), c_io


def _conv_taps(uf, n):
    u1 = pltpu.roll(uf, 1, 0)[8:8 + n]
    u2 = pltpu.roll(uf, 2, 0)[8:8 + n]
    return u1, u2


def _attn_probs(qs, kband, sink, valid):
    s = jnp.where(valid, _nt(qs, kband), NEG)
    m = jnp.maximum(jnp.max(s, axis=-1, keepdims=True), sink)
    p = jnp.exp(s - m)
    es = jnp.exp(sink - m)
    inv = 1.0 / (jnp.sum(p, axis=-1, keepdims=True) + es)
    return p * inv, es * inv


def _norm_keys(kraw, gk):
    out = []
    for h in range(NKV):
        kh = kraw[:, h * HP:(h + 1) * HP]
        rk, khat = _rms_fwd(kh, 1.0 / HD)
        out.append((khat, rk, (khat * gk).astype(BF16)))
    return out


def _mixer_fwd(proj, x, cw, gq, gk, sinks, gco, gao, wo, tq):
    t = proj.shape[0]
    nb = tq // BLK
    r8 = tq // 8

    def body(p_ref, cgp_ref, hcp_ref, kvp_ref, x_ref, cw_ref, gq_ref, gk_ref, sk_ref, gco_ref, gao_ref,
             wo_ref, xm_ref, mix_ref, ao_ref, aop_ref):
        i = pl.program_id(0)
        cg = p_ref[:, O_CG:O_CG + CC]
        hc = p_ref[:, O_HC:O_HC + CC]
        u = cg * hc
        up = jnp.where(i > 0, cgp_ref[...] * hcp_ref[...], 0.0)
        u1, u2 = _conv_taps(jnp.concatenate([up, u], axis=0), tq)
        y = cw_ref[0:1, :] * u2 + cw_ref[1:2, :] * u1 + cw_ref[2:3, :] * u
        co = p_ref[:, O_BG:O_BG + CC] * y
        _, coh = _rms_fwd(co, 1.0 / CC)
        cn = coh * gco_ref[...]
        kraw = jnp.concatenate([kvp_ref[:, 0:NKV * HP], p_ref[:, O_K:O_K + NKV * HP]], axis=0)
        vraw = jnp.concatenate([kvp_ref[:, NKV * HP:], p_ref[:, O_V:O_V + NKV * HP]], axis=0)
        keys = _norm_keys(kraw, gk_ref[...])
        vb = [vraw[:, h * HP:(h + 1) * HP].astype(BF16) for h in range(NKV)]
        base_valid, c_io = _band_mask()
        gqs = gq_ref[...] * SCALE
        for b in range(nb):
            lo = jnp.where(i * nb + b == 0, BLK, 0)
            valid = base_valid & (c_io >= lo)
            for g in range(NQ):
                h = g // GRP
                qg = p_ref[b * BLK:(b + 1) * BLK, O_Q + g * HP:O_Q + (g + 1) * HP]
                _, qh = _rms_fwd(qg, 1.0 / HD)
                qs = (qh * gqs).astype(BF16)
                pr, _ = _attn_probs(qs, keys[h][2][b * BLK:b * BLK + 2 * BLK], sk_ref[0, g], valid)
                aop_ref[b * BLK:(b + 1) * BLK, g * HP:(g + 1) * HP] = jnp.dot(
                    pr.astype(BF16), vb[h][b * BLK:b * BLK + 2 * BLK], preferred_element_type=F32)
        for j in range(NQ // 2):
            ao_ref[:, j * HP:(j + 1) * HP] = (aop_ref[:, 2 * j * HP:(2 * j + 1) * HP]
                                              + pltpu.roll(aop_ref[:, (2 * j + 1) * HP:(2 * j + 2) * HP], HD, 1))
        _, aoh = _rms_fwd(ao_ref[...], 1.0 / (NQ * HD))
        an = aoh * gao_ref[...]
        mix = jnp.concatenate([cn, an], axis=1).astype(BF16)
        mix_ref[...] = mix
        xm_ref[...] = x_ref[...] + jnp.dot(mix, wo_ref[...], preferred_element_type=F32)

    prev8 = lambda col: pl.BlockSpec((8, CC), lambda i: (jnp.maximum(i * r8 - 1, 0), col))
    return pl.pallas_call(
        body, name="mixer_fwd", grid=(t // tq,),
        in_specs=[
            pl.BlockSpec((tq, NP), lambda i: (i, 0)),
            prev8(O_CG // CC), prev8(O_HC // CC),
            pl.BlockSpec((BLK, 2 * NKV * HP), lambda i: (jnp.maximum(i * nb - 1, 0), O_K // (2 * NKV * HP))),
            pl.BlockSpec((tq, D), lambda i: (i, 0)),
            _const_spec((8, CC)), _const_spec((1, HP)), _const_spec((1, HP)),
            pl.BlockSpec(memory_space=pltpu.SMEM),
            _const_spec((1, CC)), _const_spec((1, NQ * HD)), _const_spec((MIXW, D)),
        ],
        out_specs=[pl.BlockSpec((tq, D), lambda i: (i, 0)), pl.BlockSpec((tq, MIXW), lambda i: (i, 0)),
                   pl.BlockSpec((tq, NQ * HD), lambda i: (i, 0))],
        out_shape=[jax.ShapeDtypeStruct((t, D), F32), jax.ShapeDtypeStruct((t, MIXW), BF16),
                   jax.ShapeDtypeStruct((t, NQ * HD), F32)],
        scratch_shapes=[pltpu.VMEM((tq, NQ * HP), F32)],
        compiler_params=_cparams(("parallel",)),
    )(proj, proj, proj, proj, x, cw, gq, gk, sinks, gco, gao, wo)


def _ffn_weight_specs():
    return [pl.BlockSpec((N_CHIPS, FFB, D), lambda i, j=j: (0, j, 0), pipeline_mode=pl.Buffered(1))
            for j in range(3)]


def _ffn_fwd(xm, g2, gf, tm, tgt=None):
    t = xm.shape[0]
    last = tgt is not None

    def body(x_ref, g_ref, wg_ref, wu_ref, wd_ref, *rest):
        t_ref, rest = (rest[0], rest[1:]) if last else (None, rest)
        l_ref, rest = (rest[0], rest[1:]) if last else (None, rest)
        xo_ref, a_ref, b_ref, h2_ref = rest
        xv = x_ref[...]
        _, xh = _rms_fwd(xv, 1.0 / D)
        h2 = (xh * g_ref[...]).astype(BF16)
        h2_ref[...] = h2
        acc = xv
        for k in range(N_CHIPS):
            a = _nt(h2, wg_ref[k])
            b = _nt(h2, wu_ref[k])
            a_ref[k] = a.astype(BF16)
            b_ref[k] = b.astype(BF16)
            hm = (a * jax.nn.sigmoid(a) * b).astype(BF16)
            acc = acc + jnp.dot(hm, wd_ref[k], preferred_element_type=F32)
        if last:
            @pl.when(pl.program_id(0) == 0)
            def _():
                l_ref[...] = jnp.zeros_like(l_ref)

            e = acc - t_ref[...]
            xo_ref[...] = e * (1.0 / D)
            l_ref[...] += jnp.sum(jnp.sum(e * e, axis=-1, keepdims=True), axis=0, keepdims=True) * (0.5 / D)
        else:
            xo_ref[...] = acc

    row = lambda w: pl.BlockSpec((tm, w), lambda i: (i, 0))
    blk = pl.BlockSpec((N_CHIPS, tm, FFB), lambda i: (0, i, 0))
    bsd = jax.ShapeDtypeStruct((N_CHIPS, t, FFB), BF16)
    return pl.pallas_call(
        body, name="ffn_fwd_loss" if last else "ffn_fwd", grid=(t // tm,),
        in_specs=[row(D), _const_spec((1, D))] + _ffn_weight_specs() + ([row(D)] if last else []),
        out_specs=([pl.BlockSpec((8, 128), lambda i: (0, 0))] if last else []) + [row(D), blk, blk, row(D)],
        out_shape=([jax.ShapeDtypeStruct((8, 128), F32)] if last else [])
        + [jax.ShapeDtypeStruct((t, D), F32), bsd, bsd, jax.ShapeDtypeStruct((t, D), BF16)],
        compiler_params=_cparams(("arbitrary" if last else "parallel",)),
    )(*((xm, g2, gf, gf, gf) + ((tgt,) if last else ())))


def _ffn_bwd(dy, xm, g2, a, b, gf, tm):
    t = dy.shape[0]

    def body(dy_ref, x_ref, g_ref, a_ref, b_ref, wg_ref, wu_ref, wd_ref, dx_ref, da_ref, db_ref, hm_ref, dg_ref):
        @pl.when(pl.program_id(0) == 0)
        def _():
            dg_ref[...] = jnp.zeros_like(dg_ref)

        dyv = dy_ref[...]
        dyb = dyv.astype(BF16)
        dh2 = jnp.zeros_like(dyv)
        for k in range(N_CHIPS):
            dhm = _nt(dyb, wd_ref[k])
            av = a_ref[k].astype(F32)
            bv = b_ref[k].astype(F32)
            sig = jax.nn.sigmoid(av)
            sil = av * sig
            hm_ref[k] = (sil * bv).astype(BF16)
            da = (dhm * bv * (sig * (1.0 + av * (1.0 - sig)))).astype(BF16)
            db = (dhm * sil).astype(BF16)
            da_ref[k] = da
            db_ref[k] = db
            dh2 = (dh2 + jnp.dot(da, wg_ref[k], preferred_element_type=F32)
                   + jnp.dot(db, wu_ref[k], preferred_element_type=F32))
        r, xh = _rms_fwd(x_ref[...], 1.0 / D)
        dg_ref[...] += jnp.sum(dh2 * xh, axis=0, keepdims=True)
        dx_ref[...] = dyv + _rms_bwd(dh2, g_ref[...], xh, r, 1.0 / D)

    row = lambda w: pl.BlockSpec((tm, w), lambda i: (i, 0))
    blk = pl.BlockSpec((N_CHIPS, tm, FFB), lambda i: (0, i, 0))
    bsd = jax.ShapeDtypeStruct((N_CHIPS, t, FFB), BF16)
    return pl.pallas_call(
        body, name="ffn_bwd", grid=(t // tm,),
        in_specs=[row(D), row(D), _const_spec((1, D)), blk, blk] + _ffn_weight_specs(),
        out_specs=[row(D), blk, blk, blk, pl.BlockSpec((1, D), lambda i: (0, 0))],
        out_shape=[jax.ShapeDtypeStruct((t, D), F32), bsd, bsd, bsd, jax.ShapeDtypeStruct((1, D), F32)],
        compiler_params=_cparams(("arbitrary",)),
    )(dy, xm, g2, a, b, gf, gf, gf)


def _wgrad_blocks(a, b, tt, name):
    _, t, rows = a.shape
    cols = b.shape[1]
    nsteps = t // tt

    def body(a_ref, b_ref, o_ref, acc_ref):
        s = pl.program_id(0)

        @pl.when(s == 0)
        def _():
            acc_ref[...] = jnp.zeros_like(acc_ref)

        bv = b_ref[...].astype(BF16)
        for k in range(N_CHIPS):
            acc_ref[k] += _tn(a_ref[k], bv)

        @pl.when(s == nsteps - 1)
        def _():
            o_ref[...] = acc_ref[...].astype(BF16)

    return pl.pallas_call(
        body, name=name, grid=(nsteps,),
        in_specs=[pl.BlockSpec((N_CHIPS, tt, rows), lambda s: (0, s, 0)), pl.BlockSpec((tt, cols), lambda s: (s, 0))],
        out_specs=pl.BlockSpec((N_CHIPS, rows, cols), lambda s: (0, 0, 0)),
        out_shape=jax.ShapeDtypeStruct((N_CHIPS, rows, cols), BF16),
        scratch_shapes=[pltpu.VMEM((N_CHIPS, rows, cols), F32)],
        compiler_params=_cparams(("arbitrary",)),
    )(a, b)


def _head_rows(first, n_heads):
    return [(first + g * HD, first + g * HP, HD) for g in range(n_heads)]


def _wgrad(a, b, tt, name):
    t, k = a.shape
    n = b.shape[1]
    nsteps = t // tt

    def body(a_ref, b_ref, o_ref, acc_ref):
        s = pl.program_id(0)

        @pl.when(s == 0)
        def _():
            acc_ref[...] = jnp.zeros_like(acc_ref)

        acc_ref[...] += _tn(a_ref[...].astype(BF16), b_ref[...].astype(BF16))

        @pl.when(s == nsteps - 1)
        def _():
            o_ref[...] = acc_ref[...].astype(BF16)

    return pl.pallas_call(
        body, name=name, grid=(nsteps,),
        in_specs=[pl.BlockSpec((tt, k), lambda s: (s, 0)), pl.BlockSpec((tt, n), lambda s: (s, 0))],
        out_specs=pl.BlockSpec((k, n), lambda s: (0, 0)),
        out_shape=jax.ShapeDtypeStruct((k, n), BF16),
        scratch_shapes=[pltpu.VMEM((k, n), F32)],
        compiler_params=_cparams(("arbitrary",)),
    )(a, b)


def _wgrad_in(dpm, dkv, h, tt):
    t = h.shape[0]
    nsteps = t // tt
    kvw = dkv.shape[1]
    pieces = [(0, 0, O_Q)] + _head_rows(O_Q, NQ + 2 * NKV)

    def body(m_ref, kv_ref, h_ref, o_ref, acc_ref):
        s = pl.program_id(0)

        @pl.when(s == 0)
        def _():
            acc_ref[...] = jnp.zeros_like(acc_ref)

        hv = h_ref[...]
        acc_ref[:NMAIN, :] += _tn(m_ref[...], hv)
        acc_ref[NMAIN:, :] += _tn(kv_ref[...], hv)

        @pl.when(s == nsteps - 1)
        def _():
            for dst, src, size in pieces:
                o_ref[dst:dst + size, :] = acc_ref[src:src + size, :].astype(BF16)

    return pl.pallas_call(
        body, name="wgrad_in", grid=(nsteps,),
        in_specs=[pl.BlockSpec((tt, NMAIN), lambda s: (s, 0)), pl.BlockSpec((tt, kvw), lambda s: (s, 0)),
                  pl.BlockSpec((tt, D), lambda s: (s, 0))],
        out_specs=pl.BlockSpec((W_IN_ROWS, D), lambda s: (0, 0)),
        out_shape=jax.ShapeDtypeStruct((W_IN_ROWS, D), BF16),
        scratch_shapes=[pltpu.VMEM((NMAIN + kvw, D), F32)],
        compiler_params=_cparams(("arbitrary",)),
    )(dpm, dkv, h)


def _mixer_bwd(dxm, proj, ao, cw, gq, gk, sinks, gco, gao, wo, tq):
    t = proj.shape[0]
    nb = tq // BLK
    r8 = tq // 8
    nt = t // tq
    te = tq + 8
    kvw = 2 * NKV * HP

    def body(dx_ref, dxn_ref, p_ref, cgp_ref, hcp_ref, bgn_ref, cgn_ref, hcn_ref, kvp_ref, ao_ref, cw_ref, gq_ref,
             gk_ref, sk_ref, gco_ref, gao_ref, wo_ref,
             dpm_ref, dkvm_ref, dkvh_ref, dcw_ref, dgq_ref, dgk_ref, dsk_ref, dgco_ref, dgao_ref, acc_ref):
        i = pl.program_id(0)

        @pl.when(i == 0)
        def _():
            for r in (dcw_ref, dgq_ref, dgk_ref, dsk_ref, dgco_ref, dgao_ref):
                r[...] = jnp.zeros_like(r)

        acc_ref[...] = jnp.zeros_like(acc_ref)
        live_rows = jnp.where(i < nt - 1, te, tq)
        dxb = dx_ref[...].astype(BF16)
        dxe = jnp.concatenate([dxb, dxn_ref[...].astype(BF16)], axis=0)
        dcn = _nt(dxe, wo_ref[0:CC, :])
        bg = jnp.concatenate([p_ref[:, O_BG:O_BG + CC], bgn_ref[...]], axis=0)
        cg = jnp.concatenate([p_ref[:, O_CG:O_CG + CC], cgn_ref[...]], axis=0)
        hc = jnp.concatenate([p_ref[:, O_HC:O_HC + CC], hcn_ref[...]], axis=0)
        u = cg * hc
        up = jnp.where(i > 0, cgp_ref[...] * hcp_ref[...], 0.0)
        u1, u2 = _conv_taps(jnp.concatenate([up, u], axis=0), te)
        w0, w1, w2 = cw_ref[0:1, :], cw_ref[1:2, :], cw_ref[2:3, :]
        y = w0 * u2 + w1 * u1 + w2 * u
        co = bg * y
        rc, coh = _rms_fwd(co, 1.0 / CC)
        dco = _rms_bwd(dcn, gco_ref[...], coh, rc, 1.0 / CC)
        row_io = lax.broadcasted_iota(jnp.int32, (te, 1), 0)
        own = row_io < tq
        dgco_ref[...] += jnp.sum(jnp.where(own, dcn * coh, 0.0), axis=0, keepdims=True)
        dyc = jnp.where(row_io < live_rows, dco * bg, 0.0)
        dyo = jnp.where(own, dyc, 0.0)
        dcw_ref[0:1, :] += jnp.sum(dyo * u2, axis=0, keepdims=True)
        dcw_ref[1:2, :] += jnp.sum(dyo * u1, axis=0, keepdims=True)
        dcw_ref[2:3, :] += jnp.sum(dyo * u, axis=0, keepdims=True)
        dy1 = pltpu.roll(dyc, te - 1, 0)[0:tq]
        dy2 = pltpu.roll(dyc, te - 2, 0)[0:tq]
        du = w2 * dyc[0:tq] + w1 * dy1 + w0 * dy2
        dpm_ref[:, O_BG:O_BG + CC] = (dco[0:tq] * y[0:tq]).astype(BF16)
        dpm_ref[:, O_CG:O_CG + CC] = (du * hc[0:tq]).astype(BF16)
        dpm_ref[:, O_HC:O_HC + CC] = (du * cg[0:tq]).astype(BF16)
        kraw = jnp.concatenate([kvp_ref[:, 0:NKV * HP], p_ref[:, O_K:O_K + NKV * HP]], axis=0)
        vraw = jnp.concatenate([kvp_ref[:, NKV * HP:], p_ref[:, O_V:O_V + NKV * HP]], axis=0)
        gqv, gkv = gq_ref[...], gk_ref[...]
        keys = _norm_keys(kraw, gkv)
        vb = [vraw[:, h * HP:(h + 1) * HP].astype(BF16) for h in range(NKV)]
        base_valid, c_io = _band_mask()
        lane = lax.broadcasted_iota(jnp.int32, (1, HP), 1)
        dgq, dgk, dsk = (jnp.zeros((1, HP), F32) for _ in range(3))
        dgao = jnp.zeros((1, NQ * HD), F32)
        for b in range(nb):
            lo = jnp.where(i * nb + b == 0, BLK, 0)
            valid = base_valid & (c_io >= lo)
            band = slice(b * BLK, b * BLK + 2 * BLK)
            blk = slice(b * BLK, (b + 1) * BLK)
            ra, aoh = _rms_fwd(ao_ref[blk, :], 1.0 / (NQ * HD))
            danb = _nt(dxb[blk], wo_ref[CC:MIXW, :])
            dgao = dgao + jnp.sum(danb * aoh, axis=0, keepdims=True)
            dao = _rms_bwd(danb, gao_ref[...], aoh, ra, 1.0 / (NQ * HD))
            dos = [dao[:, g // 2 * HP:(g // 2 + 1) * HP] for g in range(NQ)]
            dos = [(d if g % 2 == 0 else pltpu.roll(d, HD, 1)).astype(BF16) for g, d in enumerate(dos)]
            fwd = []
            for g in range(NQ):
                rq, qh = _rms_fwd(p_ref[blk, O_Q + g * HP:O_Q + (g + 1) * HP], 1.0 / HD)
                qs = (qh * (gqv * SCALE)).astype(BF16)
                fwd.append((rq, qh, qs) + _attn_probs(qs, keys[g // GRP][2][band], sk_ref[0, g], valid))
            dqs = []
            for h in range(NKV):
                khat, rk, kn = [a[band] for a in keys[h]]
                dss, prbs, qns, dobs = [], [], [], []
                for g in range(h * GRP, (h + 1) * GRP):
                    rq, qh, qs, pr, ps = fwd[g]
                    dob = dos[g]
                    dp = _nt(dob, vb[h][band])
                    delta = jnp.sum(pr * dp, axis=-1, keepdims=True)
                    dsb = (pr * (dp - delta)).astype(BF16)
                    dsk = dsk + jnp.where(lane == g, -jnp.sum(ps * delta, axis=0, keepdims=True), 0.0)
                    dqn = jnp.dot(dsb, kn, preferred_element_type=F32) * SCALE
                    dgq = dgq + jnp.sum(dqn * qh, axis=0, keepdims=True)
                    dqs.append(_rms_bwd(dqn, gqv, qh, rq, 1.0 / HD).astype(BF16))
                    dss.append(dsb)
                    prbs.append(pr.astype(BF16))
                    qns.append(qs)
                    dobs.append(dob)
                dkn = _tn(jnp.concatenate(dss, axis=0), jnp.concatenate(qns, axis=0))
                dv = _tn(jnp.concatenate(prbs, axis=0), jnp.concatenate(dobs, axis=0))
                dgk = dgk + jnp.sum(dkn * khat, axis=0, keepdims=True)
                acc_ref[band, h * HP:(h + 1) * HP] += _rms_bwd(dkn, gkv, khat, rk, 1.0 / HD)
                acc_ref[band, (NKV + h) * HP:(NKV + h + 1) * HP] += dv
            dpm_ref[blk, O_Q:O_K] = jnp.concatenate(dqs, axis=1)
        dgq_ref[...] += dgq
        dgk_ref[...] += dgk
        dsk_ref[...] += dsk
        dgao_ref[...] += dgao
        dkvh_ref[...] = acc_ref[0:BLK, :]
        dkvm_ref[...] = acc_ref[BLK:, :]

    prev8 = lambda col: pl.BlockSpec((8, CC), lambda i: (jnp.maximum(i * r8 - 1, 0), col))
    next8 = lambda col: pl.BlockSpec((8, CC), lambda i: (jnp.minimum((i + 1) * r8, t // 8 - 1), col))
    small = lambda n: pl.BlockSpec((1, n), lambda i: (0, 0))
    return pl.pallas_call(
        body, name="mixer_bwd", grid=(nt,),
        in_specs=[
            pl.BlockSpec((tq, D), lambda i: (i, 0)),
            pl.BlockSpec((8, D), lambda i: (jnp.minimum((i + 1) * r8, t // 8 - 1), 0)),
            pl.BlockSpec((tq, NP), lambda i: (i, 0)),
            prev8(O_CG // CC), prev8(O_HC // CC),
            next8(O_BG // CC), next8(O_CG // CC), next8(O_HC // CC),
            pl.BlockSpec((BLK, kvw), lambda i: (jnp.maximum(i * nb - 1, 0), O_K // kvw)),
            pl.BlockSpec((tq, NQ * HD), lambda i: (i, 0)),
            _const_spec((8, CC)), _const_spec((1, HP)), _const_spec((1, HP)),
            pl.BlockSpec(memory_space=pltpu.SMEM),
            _const_spec((1, CC)), _const_spec((1, NQ * HD)), _const_spec((MIXW, D)),
        ],
        out_specs=[
            pl.BlockSpec((tq, NMAIN), lambda i: (i, 0)),
            pl.BlockSpec((tq, kvw), lambda i: (i, 0)),
            pl.BlockSpec((BLK, kvw), lambda i: (i, 0)),
            pl.BlockSpec((8, CC), lambda i: (0, 0)), small(HP), small(HP), small(HP), small(CC), small(NQ * HD),
        ],
        out_shape=[
            jax.ShapeDtypeStruct((t, NMAIN), BF16), jax.ShapeDtypeStruct((t, kvw), F32),
            jax.ShapeDtypeStruct((nt * BLK, kvw), F32),
            jax.ShapeDtypeStruct((8, CC), F32), jax.ShapeDtypeStruct((1, HP), F32), jax.ShapeDtypeStruct((1, HP), F32),
            jax.ShapeDtypeStruct((1, HP), F32), jax.ShapeDtypeStruct((1, CC), F32),
            jax.ShapeDtypeStruct((1, NQ * HD), F32),
        ],
        scratch_shapes=[pltpu.VMEM((tq + BLK, kvw), F32)],
        compiler_params=_cparams(("arbitrary",)),
    )(dxm, dxm, proj, proj, proj, proj, proj, proj, proj, ao, cw, gq, gk, sinks, gco, gao, wo)


def _inproj_bwd(dpm, dkvm, dkvh, wpt, x, g1, dxm, tm):
    t = x.shape[0]
    kvw = 2 * NKV * HP
    nt = t // tm

    def body(dp_ref, dk_ref, dh_ref, w_ref, x_ref, g_ref, dxm_ref, dx_ref, dg_ref, dkv_ref):
        i = pl.program_id(0)

        @pl.when(i == 0)
        def _():
            dg_ref[...] = jnp.zeros_like(dg_ref)

        halo = jnp.where(i < nt - 1, dh_ref[...], 0.0)
        dkv_ref[0:tm - BLK, :] = dk_ref[0:tm - BLK, :].astype(BF16)
        dkv_ref[tm - BLK:tm, :] = (dk_ref[tm - BLK:tm, :] + halo).astype(BF16)
        dh = (jnp.dot(dp_ref[...], w_ref[0:NMAIN, :], preferred_element_type=F32)
              + jnp.dot(dkv_ref[...], w_ref[NMAIN:NP, :], preferred_element_type=F32))
        r, xh = _rms_fwd(x_ref[...], 1.0 / D)
        dg_ref[...] += jnp.sum(dh * xh, axis=0, keepdims=True)
        dx_ref[...] = dxm_ref[...] + _rms_bwd(dh, g_ref[...], xh, r, 1.0 / D)

    row = lambda w: pl.BlockSpec((tm, w), lambda i: (i, 0))
    return pl.pallas_call(
        body, name="inproj_bwd", grid=(nt,),
        in_specs=[row(NMAIN), row(kvw), pl.BlockSpec((BLK, kvw), lambda i: (jnp.minimum(i + 1, nt - 1), 0)),
                  _const_spec((NP, D)), row(D), _const_spec((1, D)), row(D)],
        out_specs=[row(D), pl.BlockSpec((1, D), lambda i: (0, 0)), row(kvw)],
        out_shape=[jax.ShapeDtypeStruct((t, D), F32), jax.ShapeDtypeStruct((1, D), F32),
                   jax.ShapeDtypeStruct((t, kvw), BF16)],
        compiler_params=_cparams(("arbitrary",)),
    )(dpm, dkvm, dkvh, wpt, x, g1, dxm)


def _presum_halves(gs, theirs, core, chip):
    n = len(gs)

    def body(c_ref, chip_ref, *refs):
        for g_ref, t_ref, o_ref, keep_ref in zip(refs[:n], refs[n:2 * n], refs[2 * n:3 * n], refs[3 * n:]):
            val = (g_ref[...].astype(F32) + t_ref[...].astype(F32)).astype(BF16)
            o_ref[...] = val

            @pl.when(pl.program_id(0) == chip_ref[0])
            def _():
                keep_ref[...] = val

    half = lambda ta: pl.BlockSpec((None,) + ta.shape[1:], lambda k, c_ref, chip_ref: (k, 0, 0))
    own = lambda ta: pl.BlockSpec((None,) + ta.shape[1:], lambda k, c_ref, chip_ref: (k, c_ref[0], 0))
    kept = lambda ta: pl.BlockSpec(ta.shape[1:], lambda k, c_ref, chip_ref: (0, 0))
    res = pl.pallas_call(
        body, name="presum",
        grid_spec=pltpu.PrefetchScalarGridSpec(
            num_scalar_prefetch=2, grid=(N_CHIPS,),
            in_specs=[own(ta) for ta in theirs] + [half(ta) for ta in theirs],
            out_specs=[half(ta) for ta in theirs] + [kept(ta) for ta in theirs]),
        out_shape=[jax.ShapeDtypeStruct(ta.shape, BF16) for ta in theirs]
        + [jax.ShapeDtypeStruct(ta.shape[1:], BF16) for ta in theirs],
        compiler_params=_cparams(("arbitrary",)),
    )(core, chip, *gs, *theirs)
    return res[:n], res[n:]


def _sum_chips(got, kept, chip):
    n = len(got)
    steps = 2

    def body(chip_ref, *refs):
        for c_ref, own_ref, o_ref in zip(refs[:n], refs[n:2 * n], refs[2 * n:]):
            acc = None
            for j in range(N_CHIPS):
                term = jnp.where(chip_ref[0] == j, own_ref[...], c_ref[j]).astype(F32)
                acc = term if acc is None else acc + term
            o_ref[...] = acc

    tile = lambda c: (c.shape[1] // steps, c.shape[2])
    return pl.pallas_call(
        body, name="chipsum",
        grid_spec=pltpu.PrefetchScalarGridSpec(
            num_scalar_prefetch=1, grid=(steps,),
            in_specs=[pl.BlockSpec((N_CHIPS,) + tile(c), lambda i, chip_ref: (0, i, 0)) for c in got]
            + [pl.BlockSpec(tile(c), lambda i, chip_ref: (i, 0)) for c in got],
            out_specs=[pl.BlockSpec(tile(c), lambda i, chip_ref: (i, 0)) for c in got]),
        out_shape=[jax.ShapeDtypeStruct(c.shape[1:], F32) for c in got],
        compiler_params=_cparams(("parallel",)),
    )(chip, *got, *kept)


def _adamw_refs(w_ref, g_ref, m_ref, v_ref, d_ref, mo_ref, vo_ref):
    c1 = 1.0 - ADAM_B1 ** ADAM_STEP
    c2 = 1.0 - ADAM_B2 ** ADAM_STEP
    gv = g_ref[...]
    mn = ADAM_B1 * m_ref[...] + (1.0 - ADAM_B1) * gv
    vn = ADAM_B2 * v_ref[...] + (1.0 - ADAM_B2) * (gv * gv)
    mo_ref[...] = mn
    vo_ref[...] = vn
    d_ref[...] = -ADAM_LR * ((mn / c1) / (jnp.sqrt(vn / c2) + ADAM_EPS) + ADAM_WD * w_ref[...])


def _adamw_small(ws, gs, ms, vs):
    n = len(ws)

    def body(*refs):
        for k in range(n):
            _adamw_refs(*refs[k::n])

    res = pl.pallas_call(
        body, name="adamw_small", out_shape=[jax.ShapeDtypeStruct(w.shape, F32) for w in ws] * 3,
        compiler_params=_cparams(),
    )(*ws, *gs, *ms, *vs)
    return res[:n], res[n:2 * n], res[2 * n:]


def _adamw_halves(w, m, v, mine, theirs, core, name):
    depth = len(mine)
    half, cols = mine[0].shape
    assert w.shape == (2 * depth * half, cols)

    def body(core_ref, *refs):
        halves, (w_ref, m_ref, v_ref, g_ref, d_ref, mo_ref, vo_ref) = refs[:2 * depth], refs[2 * depth:]
        for l in range(depth):
            for h in range(2):
                @pl.when(pl.program_id(0) == 2 * l + h)
                def _(l=l, h=h):
                    g_ref[...] = jnp.where(core_ref[0] == h, halves[l][...], halves[depth + l][...])
        _adamw_refs(w_ref, g_ref, m_ref, v_ref, d_ref, mo_ref, vo_ref)

    spec = pl.BlockSpec((half, cols), lambda i, core_ref: (i, 0))
    sds = jax.ShapeDtypeStruct(w.shape, F32)
    return pl.pallas_call(
        body, name=name,
        grid_spec=pltpu.PrefetchScalarGridSpec(
            num_scalar_prefetch=1, grid=(2 * depth,),
            in_specs=[_const_spec((half, cols))] * (2 * depth) + [spec] * 3, out_specs=[spec] * 4),
        out_shape=[sds] * 4,
        compiler_params=_cparams(("arbitrary",)),
    )(core, *mine, *theirs, w, m, v)


def _place():
    x, y, c = lax.axis_index("x"), lax.axis_index("y"), lax.axis_index("c")
    chips = [(1 - x, y), (x, 1 - y), (1 - x, 1 - y)]
    return x, y, c, chips


ANY = pl.BlockSpec(memory_space=pl.ANY)
DMA_ROWS = 64


def _pieces(shape):
    rows = shape[-2]
    step = DMA_ROWS if rows % DMA_ROWS == 0 else rows
    lead = [()]
    for n in shape[:-2]:
        lead = [i + (k,) for i in lead for k in range(n)]
    return [i + (pl.ds(r0, step),) for i in lead for r0 in range(0, rows, step)]


def _start_pieces(make, src, dst):
    for idx in _pieces(src.shape):
        make(src.at[idx], dst.at[idx]).start()


def _gather_body(srcs, outs, sems, layer, start):
    nw = len(srcs)
    ssem, rsem, fssem, frsem = sems
    x, y, c, chips = _place()
    kme = 2 * x + y

    def plane(j, w, to):
        return lambda s, d: pltpu.make_async_remote_copy(
            src_ref=s, dst_ref=d, send_sem=ssem.at[j, w], recv_sem=rsem.at[j, w], device_id=to,
            device_id_type=MESH)

    def passed(j, w):
        return lambda s, d: pltpu.make_async_remote_copy(
            src_ref=s, dst_ref=d, send_sem=fssem.at[j, w], recv_sem=frsem.at[j, w],
            device_id=(x, y, 1 - c), device_id_type=MESH)

    @pl.when(c == layer)
    def _():
        for j, (px, py) in enumerate(chips):
            for w in range(nw):
                start(plane(j, w, (px, py, c)), srcs[w], outs[w].at[kme])
        for j, (px, py) in enumerate(chips):
            for w in range(nw):
                got = outs[w].at[2 * px + py]
                plane(j, w, (px, py, c))(got, got).wait_recv()
                start(passed(j, w), got, got)
        for j, (px, py) in enumerate(chips):
            for w in range(nw):
                got = outs[w].at[2 * px + py]
                plane(j, w, (px, py, c))(got, got).wait_send()
                passed(j, w)(got, got).wait_send()

    @pl.when(c != layer)
    def _():
        for j, (px, py) in enumerate(chips):
            for w in range(nw):
                got = outs[w].at[2 * px + py]
                passed(j, w)(got, got).wait_recv()


def _handshake(peers):
    barrier = pltpu.get_barrier_semaphore()
    for peer in peers:
        pl.semaphore_signal(barrier, inc=1, device_id=peer, device_id_type=MESH)
    pl.semaphore_wait(barrier, len(peers))


def _handshake_all():
    x, y, c, _ = _place()
    _handshake([(x ^ (r >> 2), y ^ ((r >> 1) & 1), c ^ (r & 1)) for r in range(1, 8)])


def _gather_layer_async(blocks, layer, name, collective_id):
    hbm = pltpu.MemorySpace.HBM
    srcs = [jax.new_ref(b, memory_space=hbm) for b in blocks]
    outs = [jax.empty_ref(jax.ShapeDtypeStruct((N_CHIPS,) + b.shape, b.dtype), memory_space=hbm) for b in blocks]

    @pl.kernel(mesh=plsc.ScalarSubcoreMesh(axis_name="seq", num_cores=1), name=name,
               scratch_types=[pltpu.SemaphoreType.DMA((3, len(blocks)))] * 4,
               compiler_params=pltpu.CompilerParams(collective_id=collective_id))
    def launch(*sems):
        _handshake_all()
        _gather_body(srcs, outs, sems, layer, lambda make, s, d: make(s, d).start())

    launch()
    return [o[...] for o in outs]


def _swap_siblings(arrs, halves, name, collective_id=None):
    nw = len(arrs)
    out_sds = [jax.ShapeDtypeStruct((a.shape[0], a.shape[1] // 2, a.shape[2]) if halves else a.shape, a.dtype)
               for a in arrs]

    def exchange(srcs, outs, ssem, rsem, start):
        x, y, c, _ = _place()

        def give(w):
            return lambda s, d: pltpu.make_async_remote_copy(
                src_ref=s, dst_ref=d, send_sem=ssem.at[w], recv_sem=rsem.at[w], device_id=(x, y, 1 - c),
                device_id_type=MESH)

        for w in range(nw):
            hr = outs[w].shape[1]
            start(give(w), srcs[w].at[:, pl.ds((1 - c) * hr, hr)] if halves else srcs[w], outs[w])
        for w in range(nw):
            give(w)(outs[w], outs[w]).wait()

    if collective_id is None:
        def body(*refs):
            exchange(refs[:nw], refs[nw:2 * nw], *refs[2 * nw:], _start_pieces)

        return pl.pallas_call(
            body, name=name, in_specs=[ANY] * nw, out_specs=[ANY] * nw, out_shape=out_sds,
            scratch_shapes=[pltpu.SemaphoreType.DMA((nw,))] * 2,
            compiler_params=_cparams(has_side_effects=True),
        )(*arrs)

    hbm = pltpu.MemorySpace.HBM
    srcs = [jax.new_ref(a, memory_space=hbm) for a in arrs]
    outs = [jax.empty_ref(sds, memory_space=hbm) for sds in out_sds]

    @pl.kernel(mesh=plsc.ScalarSubcoreMesh(axis_name="seq", num_cores=1), name=name,
               scratch_types=[pltpu.SemaphoreType.DMA((nw,))] * 2,
               compiler_params=pltpu.CompilerParams(collective_id=collective_id))
    def launch(ssem, rsem):
        x, y, c, _ = _place()
        _handshake([(x, y, 1 - c)])
        exchange(srcs, outs, ssem, rsem, lambda make, s, d: make(s, d).start())

    launch()
    return [o[...] for o in outs]


def _scatter_body(srcs, outs, sems, start):
    nw = len(srcs)
    ssem, rsem = sems
    x, y, c, chips = _place()
    kme = 2 * x + y

    def give(j, w, to):
        return lambda s, d: pltpu.make_async_remote_copy(
            src_ref=s, dst_ref=d, send_sem=ssem.at[j, w], recv_sem=rsem.at[j, w], device_id=to,
            device_id_type=MESH)

    for j, (px, py) in enumerate(chips):
        for w in range(nw):
            start(give(j, w, (px, py, c)), srcs[w].at[2 * px + py], outs[w].at[kme])
    for j, (px, py) in enumerate(chips):
        for w in range(nw):
            got = outs[w].at[2 * px + py]
            give(j, w, (px, py, c))(got, got).wait_recv()
    for j, (px, py) in enumerate(chips):
        for w in range(nw):
            sent = srcs[w].at[2 * px + py]
            give(j, w, (px, py, c))(sent, sent).wait_send()


def _scatter_chips_async(ps, name, collective_id):
    hbm = pltpu.MemorySpace.HBM
    srcs = [jax.new_ref(p, memory_space=hbm) for p in ps]
    outs = [jax.empty_ref(jax.ShapeDtypeStruct(p.shape, p.dtype), memory_space=hbm) for p in ps]

    @pl.kernel(mesh=plsc.ScalarSubcoreMesh(axis_name="seq", num_cores=1), name=name,
               scratch_types=[pltpu.SemaphoreType.DMA((3, len(ps)))] * 2,
               compiler_params=pltpu.CompilerParams(collective_id=collective_id))
    def launch(*sems):
        _handshake_all()
        _scatter_body(srcs, outs, sems, lambda make, s, d: make(s, d).start())

    launch()
    return [o[...] for o in outs]


def _allreduce_small(v):
    rows = v.shape[0]

    def body(v_ref, o_ref, buf, ssem, rsem):
        x, y, c, _ = _place()
        me = 4 * x + 2 * y + c
        buf[me] = v_ref[...]
        sends = []
        for r in range(1, 8):
            peer = (x ^ (r >> 2), y ^ ((r >> 1) & 1), c ^ (r & 1))
            cp = pltpu.make_async_remote_copy(
                src_ref=v_ref, dst_ref=buf.at[me], send_sem=ssem.at[r - 1], recv_sem=rsem.at[r - 1],
                device_id=peer, device_id_type=MESH)
            cp.start()
            sends.append(cp)
        for r in range(1, 8):
            src = me ^ r
            pltpu.make_async_remote_copy(
                src_ref=v_ref, dst_ref=buf.at[src], send_sem=ssem.at[r - 1], recv_sem=rsem.at[r - 1],
                device_id=(x, y, c), device_id_type=MESH).wait_recv()
        for cp in sends:
            cp.wait_send()
        acc = buf[0]
        for d in range(1, 8):
            acc = acc + buf[d]
        o_ref[...] = acc

    vm = pl.BlockSpec(memory_space=pltpu.VMEM)
    return pl.pallas_call(
        body, name="allreduce_small", in_specs=[vm], out_specs=vm,
        out_shape=jax.ShapeDtypeStruct(v.shape, F32),
        scratch_shapes=[pltpu.VMEM((8, rows, 128), F32), pltpu.SemaphoreType.DMA((7,)),
                        pltpu.SemaphoreType.DMA((7,))],
        compiler_params=_cparams(has_side_effects=True),
    )(v)


def _t(w):
    return jnp.swapaxes(w, -1, -2)


def _count(shape):
    n = 1
    for s in shape:
        n *= s
    return n


def _pack_rows(arrs):
    flat = [jnp.pad(a.reshape(-1), (0, (-_count(a.shape)) % 128)) for a in arrs]
    v = jnp.concatenate(flat)
    rows = -(-v.shape[0] // (8 * 128)) * 8
    return jnp.pad(v, (0, rows * 128 - v.shape[0])).reshape(rows, 128)


def kernel(x, norm1_g, w_in, conv_w, q_norm_g, k_norm_g, sinks, conv_out_g, attn_out_g, w_o, norm2_g, w_gate, w_up, w_down, loss_target, m_norm1_g, m_w_in, m_conv_w, m_q_norm_g, m_k_norm_g, m_sinks, m_conv_out_g, m_attn_out_g, m_w_o, m_norm2_g, m_w_gate, m_w_up, m_w_down, v_norm1_g, v_w_in, v_conv_w, v_q_norm_g, v_k_norm_g, v_sinks, v_conv_out_g, v_attn_out_g, v_w_o, v_norm2_g, v_w_gate, v_w_up, v_w_down):
    depth = w_in.shape[0]
    t = x.shape[1]
    xs = x.reshape(t, D)
    tgt = loss_target.reshape(t, D)
    xi, yi = lax.axis_index("x"), lax.axis_index("y")
    kme = 2 * xi + yi
    tm = min(512, t)
    tq = min(512, t)
    tf = min(256, t)
    tw = min(1024, t)

    cwp = jnp.pad(conv_w.reshape(depth * 3, CC // N_CHIPS), ((0, 8 - depth * 3), (0, 0)))
    own_f = [jnp.concatenate([_t(w_gate[l]), _t(w_up[l]), w_down[l]], axis=0).astype(BF16) for l in range(depth)]
    own_o = [w_o[l].astype(BF16) for l in range(depth)]
    own_i = [_t(w_in[l]).astype(BF16) for l in range(depth)]
    mine = lambda got, own: lax.dynamic_update_index_in_dim(got, own, kme, 0)
    (got_i0,) = _gather_layer_async([own_i[0]], 0, "gather_in0_seq", collective_id=14)
    got_ocw = _gather_layer_async([own_o[0], cwp], 0, "gather_o0_seq", collective_id=15)
    got_i0, own_f, own_o, own_i = lax.optimization_barrier((got_i0, own_f, own_o, own_i))
    gf0_in = lax.optimization_barrier((own_f[0], got_i0))[0]
    (got_f0,) = _gather_layer_async([gf0_in], 0, "gather_ffn0_seq", collective_id=6)

    chip = kme.reshape(1).astype(jnp.int32)

    def layer_params(l, got_o, cw_full):
        return dict(
            wo=mine(got_o, own_o[l]).reshape(MIXW, D),
            cw=jnp.pad(cw_full[l], ((0, 5), (0, 0))),
            g1=norm1_g[l].reshape(1, D), g2=norm2_g[l].reshape(1, D),
            gq=jnp.pad(q_norm_g[l], (0, HP - HD)).reshape(1, HP), gk=jnp.pad(k_norm_g[l], (0, HP - HD)).reshape(1, HP),
            sk=sinks[l].reshape(1, NQ), gco=conv_out_g[l].reshape(1, CC),
            gao=attn_out_g[l].reshape(1, NQ * HD))

    saved, layers = [], []
    cur = xs
    for l in range(depth):
        x_in = cur
        if l == 0:
            got_i = got_i0
        else:
            got_f1, got_o, got_i = lax.optimization_barrier((got_l1, cur))[0]
        proj, h, wpt = _inproj_fwd(cur, norm1_g[l].reshape(1, D), got_i, own_i[l], chip, tm)
        if l == 0:
            got_o, got_cw = lax.optimization_barrier((got_ocw, proj))[0]
            cw_full = mine(got_cw, cwp).transpose(1, 0, 2).reshape(8, CC)[:depth * 3].reshape(depth, 3, CC)
        p = layer_params(l, got_o, cw_full)
        p["wpt"] = wpt
        xm, mix, ao = _mixer_fwd(proj, cur, p["cw"], p["gq"], p["gk"], p["sk"], p["gco"], p["gao"], p["wo"], tq)
        if l == 0:
            got_f0 = lax.optimization_barrier((got_f0, xm))[0]
            l1_in = lax.optimization_barrier(([own_f[1], own_o[1], own_i[1]], got_f0))[0]
            got_l1 = _gather_layer_async(l1_in, 1, "gather_layer1_seq", collective_id=1)
        p["gf"] = mine(got_f0 if l == 0 else got_f1, own_f[l])
        layers.append(p)
        if l < depth - 1:
            cur, a, b, h2 = _ffn_fwd(xm, p["g2"], p["gf"], tm)
        else:
            lpart, dy, a, b, h2 = _ffn_fwd(xm, p["g2"], p["gf"], tm, tgt)
        saved.append(dict(x=x_in, proj=proj, h=h, xm=xm, mix=mix, ao=ao, a=a, b=b, h2=h2))

    ci = lax.axis_index("c")
    core = ci.reshape(1).astype(jnp.int32)
    rbig = [dict() for _ in range(depth)]
    gsmall = [None] * depth

    def after_(vals, after):
        return vals if after is None else lax.optimization_barrier((vals, after))[0]

    def reduce_1(gs, tag, ids):
        return gs, _swap_siblings(gs, True, f"swap_halves_{tag}_seq", ids[0]), tag, ids

    def reduce_2(state, after):
        gs, theirs, tag, ids = state
        ps, kept = _presum_halves(gs, after_(theirs, after), core, chip)
        return kept, _scatter_chips_async(ps, f"scatter_{tag}_seq", ids[1]), tag, ids

    def reduce_3(state, after):
        kept, got, tag, ids = state
        r_mine = _sum_chips(after_(got, after), kept, chip)
        return r_mine, _swap_siblings(r_mine, False, f"swap_reduced_{tag}" + ("_seq" if ids[2] else ""), ids[2])

    def reduce_4(state, after):
        r_mine, r_theirs = state
        return list(zip(r_mine, after_(r_theirs, after)))

    ids = {"ffn1": (7, 4, 8), "in1": (9, 5, 10), "ffn0": (11, 2, 12), "in0": (13, 3, None)}
    in_2 = scattering = None
    handed = {}
    for l in reversed(range(depth)):
        p, s = layers[l], saved[l]
        dxm, da, db, hm, dg2 = _ffn_bwd(dy, s["xm"], p["g2"], s["a"], s["b"], p["gf"], tf)
        if in_2 is not None:
            in_2 = reduce_2(in_2, dxm)
        g_wg = _wgrad_blocks(da, s["h2"], tw, "wgrad_gate")
        g_wu = _wgrad_blocks(db, s["h2"], tw, "wgrad_up")
        g_wd = _wgrad_blocks(hm, dy, tw, "wgrad_down")
        if in_2 is not None:
            handed[f"in{l + 1}"] = reduce_3(in_2, g_wd)
        ffn_1 = reduce_1([g_wg, g_wu, g_wd], f"ffn{l}", ids[f"ffn{l}"])
        dpm, dkvm, dkvh, dcw, dgq, dgk, dsk, dgco, dgao = _mixer_bwd(
            after_(dxm, scattering), s["proj"], s["ao"], p["cw"], p["gq"], p["gk"], p["sk"], p["gco"], p["gao"],
            p["wo"], tq)
        ffn_2 = reduce_2(ffn_1, dpm)
        scattering = ffn_2[0]
        g_o = _wgrad(s["mix"], dxm, tw, "wgrad_o")
        dx, dg1, dkv = _inproj_bwd(dpm, dkvm, dkvh, p["wpt"], s["x"], p["g1"], dxm, tq)
        g_in = _wgrad_in(dpm, dkv, s["h"], tw)
        dy = dx
        gsmall[l] = dict(g1=dg1, cw=dcw[:3], gq=dgq[0, :HD], gk=dgk[0, :HD], sk=dsk[0, :NQ], gco=dgco,
                         gao=dgao, g2=dg2)
        above = handed.get(f"ffn{l + 1}")
        handed[f"ffn{l}"] = reduce_3(ffn_2, g_in if above is None else (g_in, above[0]))
        in_2 = reduce_1([g_in.reshape(N_CHIPS, -1, D), g_o.reshape(N_CHIPS, -1, D)], f"in{l}", ids[f"in{l}"])
    grad_x = dy.reshape(x.shape)

    small_shapes = dict(g1=(D,), cw=(3, CC), gq=(HD,), gk=(HD,), sk=(NQ,), gco=(CC,), gao=(NQ * HD,), g2=(D,))
    red = _allreduce_small(_pack_rows([gsmall[l][n] for l in range(depth) for n in small_shapes]
                                      + [lpart[0:1, 0:1]])).reshape(-1)
    red_small, offs = {n: [] for n in small_shapes}, 0
    for l in range(depth):
        for n, shp in small_shapes.items():
            cnt = _count(shp)
            red_small[n].append(red[offs:offs + cnt].reshape(shp))
            offs += -(-cnt // 128) * 128
    loss = red[offs]
    g_small = {n: jnp.stack(v) for n, v in red_small.items()}
    g_cw = lax.dynamic_slice_in_dim(g_small["cw"], kme * (CC // N_CHIPS), CC // N_CHIPS, axis=2)

    weights = [norm1_g, w_in, conv_w, q_norm_g, k_norm_g, sinks, conv_out_g, attn_out_g, w_o, norm2_g, w_gate,
               w_up, w_down]
    moms = [m_norm1_g, m_w_in, m_conv_w, m_q_norm_g, m_k_norm_g, m_sinks, m_conv_out_g, m_attn_out_g, m_w_o,
            m_norm2_g, m_w_gate, m_w_up, m_w_down]
    vars_ = [v_norm1_g, v_w_in, v_conv_w, v_q_norm_g, v_k_norm_g, v_sinks, v_conv_out_g, v_attn_out_g, v_w_o,
             v_norm2_g, v_w_gate, v_w_up, v_w_down]
    n_w = len(weights)
    big_idx = dict(zip(("in", "o", "g", "u", "d"), (1, 8, 10, 11, 12)))
    small_idx = [n for n in range(n_w) if n not in big_idx.values()]
    grads, deltas, new_m, new_v = [None] * n_w, [None] * n_w, [None] * n_w, [None] * n_w
    for n, g in zip(small_idx, (g_small["g1"], g_cw, g_small["gq"], g_small["gk"], g_small["sk"], g_small["gco"],
                                g_small["gao"], g_small["g2"])):
        grads[n] = g

    def update_big(name):
        n = big_idx[name]
        mine, theirs = zip(*[rbig[l][name] for l in range(depth)])
        shape = (depth, 2 * mine[0].shape[0], D)
        flip = shape != weights[n].shape
        rows2d = lambda a3: (_t(a3) if flip else a3).reshape(-1, D)
        res = _adamw_halves(rows2d(weights[n]), rows2d(moms[n]), rows2d(vars_[n]), mine, theirs, core,
                            f"adamw_{n}")
        res = [r.reshape(shape) for r in res]
        grads[n], deltas[n], new_m[n], new_v[n] = [_t(r) for r in res] if flip else res

    for l in range(depth):
        rbig[l]["g"], rbig[l]["u"], rbig[l]["d"] = reduce_4(handed[f"ffn{l}"], red)
    rbig[1]["in"], rbig[1]["o"] = reduce_4(handed["in1"], red)
    update_big("g")
    in_2 = reduce_2(in_2, new_v[big_idx["g"]])
    update_big("u")
    update_big("d")
    rbig[0]["in"], rbig[0]["o"] = reduce_4(reduce_3(in_2, new_v[big_idx["d"]]), None)
    for name in ("in", "o"):
        update_big(name)
    res = _adamw_small(*[[arrs[n] for n in small_idx] for arrs in (weights, grads, moms, vars_)])
    for k, n in enumerate(small_idx):
        deltas[n], new_m[n], new_v[n] = res[0][k], res[1][k], res[2][k]
    return (loss, grad_x, *grads, *deltas, *new_m, *new_v)
```

```python
import jax
import jax.numpy as jnp
from jax import lax
from jax.experimental import pallas as pl
from jax.experimental.pallas import tpu as pltpu
from jax.experimental.pallas import tpu_sc as plsc

F32 = jnp.float32
BF16 = jnp.bfloat16

D = 1024
CC = 512
NQ = 8
NKV = 2
HD = 64
HP = 128
GRP = NQ // NKV
FF = 2816
FFB = FF // 4
BLK = 128
EPS = 1e-6
NEG = -1e30
SCALE = HD ** -0.5
O_BG, O_CG, O_HC, O_Q = 0, CC, 2 * CC, 3 * CC
O_K = O_Q + NQ * HP
O_V = O_K + NKV * HP
NP = O_V + NKV * HP
NMAIN = O_K
MIXW = CC + NQ * HD
N_CHIPS = 4
VMEM_LIMIT = 56 * 1024 * 1024
MESH = pl.DeviceIdType.MESH

ADAM_LR, ADAM_B1, ADAM_B2, ADAM_EPS, ADAM_WD, ADAM_STEP = 0.001, 0.9, 0.999, 1e-08, 0.01, 10


def _cparams(sem=None, **kw):
    if sem is not None:
        kw["dimension_semantics"] = sem
    return pltpu.CompilerParams(vmem_limit_bytes=VMEM_LIMIT, **kw)


def _const_spec(shape):
    nd = len(shape)
    return pl.BlockSpec(shape, lambda *_: (0,) * nd, pipeline_mode=pl.Buffered(1))


def _nt(a, b):
    return lax.dot_general(a, b, (((1,), (1,)), ((), ())), preferred_element_type=F32)


def _tn(a, b):
    return lax.dot_general(a, b, (((0,), (0,)), ((), ())), preferred_element_type=F32)


def _rms_fwd(x, inv_n):
    r = lax.rsqrt(jnp.sum(x * x, axis=-1, keepdims=True) * inv_n + EPS)
    return r, x * r


def _rms_bwd(dy, g, xh, r, inv_n):
    dxh = dy * g
    return r * (dxh - xh * (jnp.sum(dxh * xh, axis=-1, keepdims=True) * inv_n))


W_IN_ROWS = 3 * CC + (NQ + 2 * NKV) * HD
W_IN_BLOCK = W_IN_ROWS // N_CHIPS


def _padded_row(row):
    return row + max(row - O_Q, 0) // HD * (HP - HD)


def _w_in_pieces(k):
    first = k * W_IN_BLOCK
    plain = min(max(O_Q - first, 0), W_IN_BLOCK)
    pieces = [(0, first, plain)] if plain else []
    return pieces + [(r, _padded_row(first + r), HD) for r in range(plain, W_IN_BLOCK, HD)]


def _inproj_fwd(x, g1, gi, own_i, chip, tm):
    t = x.shape[0]

    def body(chip_ref, x_ref, g_ref, gi_ref, own_ref, p_ref, h_ref, w_ref, sem):
        @pl.when(pl.program_id(0) == 0)
        def _():
            for k in range(N_CHIPS):
                for src, dst, rows in _w_in_pieces(k):
                    @pl.when(chip_ref[0] == k)
                    def _():
                        pltpu.make_async_copy(own_ref.at[pl.ds(src, rows)], w_ref.at[pl.ds(dst, rows)], sem).start()

                    @pl.when(chip_ref[0] != k)
                    def _():
                        pltpu.make_async_copy(gi_ref.at[k, pl.ds(src, rows)], w_ref.at[pl.ds(dst, rows)], sem).start()
            for slot in range(NQ + 2 * NKV):
                w_ref[O_Q + slot * HP + HD:O_Q + (slot + 1) * HP, :] = jnp.zeros((HP - HD, D), BF16)
            landed = w_ref.at[pl.ds(0, W_IN_ROWS)]
            pltpu.make_async_copy(landed, landed, sem).wait()

        _, xh = _rms_fwd(x_ref[...], 1.0 / D)
        h = (xh * g_ref[...]).astype(BF16)
        h_ref[...] = h
        p_ref[...] = _nt(h, w_ref[...])

    const = lambda shape: pl.BlockSpec(shape, lambda i, c: (0,) * len(shape))
    return pl.pallas_call(
        body, name="inproj_fwd",
        grid_spec=pltpu.PrefetchScalarGridSpec(
            num_scalar_prefetch=1, grid=(t // tm,),
            in_specs=[pl.BlockSpec((tm, D), lambda i, c: (i, 0)), const((1, D)), ANY, ANY],
            out_specs=[pl.BlockSpec((tm, NP), lambda i, c: (i, 0)), pl.BlockSpec((tm, D), lambda i, c: (i, 0)),
                       const((NP, D))],
            scratch_shapes=[pltpu.SemaphoreType.DMA(())]),
        out_shape=[jax.ShapeDtypeStruct((t, NP), F32), jax.ShapeDtypeStruct((t, D), BF16),
                   jax.ShapeDtypeStruct((NP, D), BF16)],
        compiler_params=_cparams(("arbitrary",)),
    )(chip, x, g1, gi, own_i)


def _band_mask():
    r_io = lax.broadcasted_iota(jnp.int32, (BLK, 2 * BLK), 0)
    c_io = lax.broadcasted_iota(jnp.int32, (BLK, 2 * BLK), 1)
    return (c_io > r_io) & (c_io <= r_io + BLK), c_io


def _conv_taps(uf, n):
    u1 = pltpu.roll(uf, 1, 0)[8:8 + n]
    u2 = pltpu.roll(uf, 2, 0)[8:8 + n]
    return u1, u2


def _attn_probs(qs, kband, sink, valid):
    s = jnp.where(valid, _nt(qs, kband), NEG)
    m = jnp.maximum(jnp.max(s, axis=-1, keepdims=True), sink)
    p = jnp.exp(s - m)
    es = jnp.exp(sink - m)
    inv = 1.0 / (jnp.sum(p, axis=-1, keepdims=True) + es)
    return p * inv, es * inv


def _norm_keys(kraw, gk):
    out = []
    for h in range(NKV):
        kh = kraw[:, h * HP:(h + 1) * HP]
        rk, khat = _rms_fwd(kh, 1.0 / HD)
        out.append((khat, rk, (khat * gk).astype(BF16)))
    return out


def _mixer_fwd(proj, x, cw, gq, gk, sinks, gco, gao, wo, tq):
    t = proj.shape[0]
    nb = tq // BLK
    r8 = tq // 8

    def body(p_ref, cgp_ref, hcp_ref, kvp_ref, x_ref, cw_ref, gq_ref, gk_ref, sk_ref, gco_ref, gao_ref,
             wo_ref, xm_ref, mix_ref, ao_ref, aop_ref):
        i = pl.program_id(0)
        cg = p_ref[:, O_CG:O_CG + CC]
        hc = p_ref[:, O_HC:O_HC + CC]
        u = cg * hc
        up = jnp.where(i > 0, cgp_ref[...] * hcp_ref[...], 0.0)
        u1, u2 = _conv_taps(jnp.concatenate([up, u], axis=0), tq)
        y = cw_ref[0:1, :] * u2 + cw_ref[1:2, :] * u1 + cw_ref[2:3, :] * u
        co = p_ref[:, O_BG:O_BG + CC] * y
        _, coh = _rms_fwd(co, 1.0 / CC)
        cn = coh * gco_ref[...]
        kraw = jnp.concatenate([kvp_ref[:, 0:NKV * HP], p_ref[:, O_K:O_K + NKV * HP]], axis=0)
        vraw = jnp.concatenate([kvp_ref[:, NKV * HP:], p_ref[:, O_V:O_V + NKV * HP]], axis=0)
        keys = _norm_keys(kraw, gk_ref[...])
        vb = [vraw[:, h * HP:(h + 1) * HP].astype(BF16) for h in range(NKV)]
        base_valid, c_io = _band_mask()
        gqs = gq_ref[...] * SCALE
        for b in range(nb):
            lo = jnp.where(i * nb + b == 0, BLK, 0)
            valid = base_valid & (c_io >= lo)
            for g in range(NQ):
                h = g // GRP
                qg = p_ref[b * BLK:(b + 1) * BLK, O_Q + g * HP:O_Q + (g + 1) * HP]
                _, qh = _rms_fwd(qg, 1.0 / HD)
                qs = (qh * gqs).astype(BF16)
                pr, _ = _attn_probs(qs, keys[h][2][b * BLK:b * BLK + 2 * BLK], sk_ref[0, g], valid)
                aop_ref[b * BLK:(b + 1) * BLK, g * HP:(g + 1) * HP] = jnp.dot(
                    pr.astype(BF16), vb[h][b * BLK:b * BLK + 2 * BLK], preferred_element_type=F32)
        for j in range(NQ // 2):
            ao_ref[:, j * HP:(j + 1) * HP] = (aop_ref[:, 2 * j * HP:(2 * j + 1) * HP]
                                              + pltpu.roll(aop_ref[:, (2 * j + 1) * HP:(2 * j + 2) * HP], HD, 1))
        _, aoh = _rms_fwd(ao_ref[...], 1.0 / (NQ * HD))
        an = aoh * gao_ref[...]
        mix = jnp.concatenate([cn, an], axis=1).astype(BF16)
        mix_ref[...] = mix
        xm_ref[...] = x_ref[...] + jnp.dot(mix, wo_ref[...], preferred_element_type=F32)

    prev8 = lambda col: pl.BlockSpec((8, CC), lambda i: (jnp.maximum(i * r8 - 1, 0), col))
    return pl.pallas_call(
        body, name="mixer_fwd", grid=(t // tq,),
        in_specs=[
            pl.BlockSpec((tq, NP), lambda i: (i, 0)),
            prev8(O_CG // CC), prev8(O_HC // CC),
            pl.BlockSpec((BLK, 2 * NKV * HP), lambda i: (jnp.maximum(i * nb - 1, 0), O_K // (2 * NKV * HP))),
            pl.BlockSpec((tq, D), lambda i: (i, 0)),
            _const_spec((8, CC)), _const_spec((1, HP)), _const_spec((1, HP)),
            pl.BlockSpec(memory_space=pltpu.SMEM),
            _const_spec((1, CC)), _const_spec((1, NQ * HD)), _const_spec((MIXW, D)),
        ],
        out_specs=[pl.BlockSpec((tq, D), lambda i: (i, 0)), pl.BlockSpec((tq, MIXW), lambda i: (i, 0)),
                   pl.BlockSpec((tq, NQ * HD), lambda i: (i, 0))],
        out_shape=[jax.ShapeDtypeStruct((t, D), F32), jax.ShapeDtypeStruct((t, MIXW), BF16),
                   jax.ShapeDtypeStruct((t, NQ * HD), F32)],
        scratch_shapes=[pltpu.VMEM((tq, NQ * HP), F32)],
        compiler_params=_cparams(("parallel",)),
    )(proj, proj, proj, proj, x, cw, gq, gk, sinks, gco, gao, wo)


def _ffn_weight_specs():
    return [pl.BlockSpec((N_CHIPS, FFB, D), lambda i, j=j: (0, j, 0), pipeline_mode=pl.Buffered(1))
            for j in range(3)]


def _ffn_fwd(xm, g2, gf, tm, tgt=None):
    t = xm.shape[0]
    last = tgt is not None

    def body(x_ref, g_ref, wg_ref, wu_ref, wd_ref, *rest):
        t_ref, rest = (rest[0], rest[1:]) if last else (None, rest)
        l_ref, rest = (rest[0], rest[1:]) if last else (None, rest)
        xo_ref, a_ref, b_ref, h2_ref = rest
        xv = x_ref[...]
        _, xh = _rms_fwd(xv, 1.0 / D)
        h2 = (xh * g_ref[...]).astype(BF16)
        h2_ref[...] = h2
        acc = xv
        for k in range(N_CHIPS):
            a = _nt(h2, wg_ref[k])
            b = _nt(h2, wu_ref[k])
            a_ref[k] = a.astype(BF16)
            b_ref[k] = b.astype(BF16)
            hm = (a * jax.nn.sigmoid(a) * b).astype(BF16)
            acc = acc + jnp.dot(hm, wd_ref[k], preferred_element_type=F32)
        if last:
            @pl.when(pl.program_id(0) == 0)
            def _():
                l_ref[...] = jnp.zeros_like(l_ref)

            e = acc - t_ref[...]
            xo_ref[...] = e * (1.0 / D)
            l_ref[...] += jnp.sum(jnp.sum(e * e, axis=-1, keepdims=True), axis=0, keepdims=True) * (0.5 / D)
        else:
            xo_ref[...] = acc

    row = lambda w: pl.BlockSpec((tm, w), lambda i: (i, 0))
    blk = pl.BlockSpec((N_CHIPS, tm, FFB), lambda i: (0, i, 0))
    bsd = jax.ShapeDtypeStruct((N_CHIPS, t, FFB), BF16)
    return pl.pallas_call(
        body, name="ffn_fwd_loss" if last else "ffn_fwd", grid=(t // tm,),
        in_specs=[row(D), _const_spec((1, D))] + _ffn_weight_specs() + ([row(D)] if last else []),
        out_specs=([pl.BlockSpec((8, 128), lambda i: (0, 0))] if last else []) + [row(D), blk, blk, row(D)],
        out_shape=([jax.ShapeDtypeStruct((8, 128), F32)] if last else [])
        + [jax.ShapeDtypeStruct((t, D), F32), bsd, bsd, jax.ShapeDtypeStruct((t, D), BF16)],
        compiler_params=_cparams(("arbitrary" if last else "parallel",)),
    )(*((xm, g2, gf, gf, gf) + ((tgt,) if last else ())))


def _ffn_bwd(dy, xm, g2, a, b, gf, tm):
    t = dy.shape[0]

    def body(dy_ref, x_ref, g_ref, a_ref, b_ref, wg_ref, wu_ref, wd_ref, dx_ref, da_ref, db_ref, hm_ref, dg_ref):
        @pl.when(pl.program_id(0) == 0)
        def _():
            dg_ref[...] = jnp.zeros_like(dg_ref)

        dyv = dy_ref[...]
        dyb = dyv.astype(BF16)
        dh2 = jnp.zeros_like(dyv)
        for k in range(N_CHIPS):
            dhm = _nt(dyb, wd_ref[k])
            av = a_ref[k].astype(F32)
            bv = b_ref[k].astype(F32)
            sig = jax.nn.sigmoid(av)
            sil = av * sig
            hm_ref[k] = (sil * bv).astype(BF16)
            da = (dhm * bv * (sig * (1.0 + av * (1.0 - sig)))).astype(BF16)
            db = (dhm * sil).astype(BF16)
            da_ref[k] = da
            db_ref[k] = db
            dh2 = (dh2 + jnp.dot(da, wg_ref[k], preferred_element_type=F32)
                   + jnp.dot(db, wu_ref[k], preferred_element_type=F32))
        r, xh = _rms_fwd(x_ref[...], 1.0 / D)
        dg_ref[...] += jnp.sum(dh2 * xh, axis=0, keepdims=True)
        dx_ref[...] = dyv + _rms_bwd(dh2, g_ref[...], xh, r, 1.0 / D)

    row = lambda w: pl.BlockSpec((tm, w), lambda i: (i, 0))
    blk = pl.BlockSpec((N_CHIPS, tm, FFB), lambda i: (0, i, 0))
    bsd = jax.ShapeDtypeStruct((N_CHIPS, t, FFB), BF16)
    return pl.pallas_call(
        body, name="ffn_bwd", grid=(t // tm,),
        in_specs=[row(D), row(D), _const_spec((1, D)), blk, blk] + _ffn_weight_specs(),
        out_specs=[row(D), blk, blk, blk, pl.BlockSpec((1, D), lambda i: (0, 0))],
        out_shape=[jax.ShapeDtypeStruct((t, D), F32), bsd, bsd, bsd, jax.ShapeDtypeStruct((1, D), F32)],
        compiler_params=_cparams(("arbitrary",)),
    )(dy, xm, g2, a, b, gf, gf, gf)


def _wgrad_blocks(a, b, tt, name):
    _, t, rows = a.shape
    cols = b.shape[1]
    nsteps = t // tt

    def body(a_ref, b_ref, o_ref, acc_ref):
        s = pl.program_id(0)

        @pl.when(s == 0)
        def _():
            acc_ref[...] = jnp.zeros_like(acc_ref)

        bv = b_ref[...].astype(BF16)
        for k in range(N_CHIPS):
            acc_ref[k] += _tn(a_ref[k], bv)

        @pl.when(s == nsteps - 1)
        def _():
            o_ref[...] = acc_ref[...].astype(BF16)

    return pl.pallas_call(
        body, name=name, grid=(nsteps,),
        in_specs=[pl.BlockSpec((N_CHIPS, tt, rows), lambda s: (0, s, 0)), pl.BlockSpec((tt, cols), lambda s: (s, 0))],
        out_specs=pl.BlockSpec((N_CHIPS, rows, cols), lambda s: (0, 0, 0)),
        out_shape=jax.ShapeDtypeStruct((N_CHIPS, rows, cols), BF16),
        scratch_shapes=[pltpu.VMEM((N_CHIPS, rows, cols), F32)],
        compiler_params=_cparams(("arbitrary",)),
    )(a, b)


def _head_rows(first, n_heads):
    return [(first + g * HD, first + g * HP, HD) for g in range(n_heads)]


def _wgrad(a, b, tt, name):
    t, k = a.shape
    n = b.shape[1]
    nsteps = t // tt

    def body(a_ref, b_ref, o_ref, acc_ref):
        s = pl.program_id(0)

        @pl.when(s == 0)
        def _():
            acc_ref[...] = jnp.zeros_like(acc_ref)

        acc_ref[...] += _tn(a_ref[...].astype(BF16), b_ref[...].astype(BF16))

        @pl.when(s == nsteps - 1)
        def _():
            o_ref[...] = acc_ref[...].astype(BF16)

    return pl.pallas_call(
        body, name=name, grid=(nsteps,),
        in_specs=[pl.BlockSpec((tt, k), lambda s: (s, 0)), pl.BlockSpec((tt, n), lambda s: (s, 0))],
        out_specs=pl.BlockSpec((k, n), lambda s: (0, 0)),
        out_shape=jax.ShapeDtypeStruct((k, n), BF16),
        scratch_shapes=[pltpu.VMEM((k, n), F32)],
        compiler_params=_cparams(("arbitrary",)),
    )(a, b)


def _wgrad_in(dpm, dkv, h, tt):
    t = h.shape[0]
    nsteps = t // tt
    kvw = dkv.shape[1]
    pieces = [(0, 0, O_Q)] + _head_rows(O_Q, NQ + 2 * NKV)

    def body(m_ref, kv_ref, h_ref, o_ref, acc_ref):
        s = pl.program_id(0)

        @pl.when(s == 0)
        def _():
            acc_ref[...] = jnp.zeros_like(acc_ref)

        hv = h_ref[...]
        acc_ref[:NMAIN, :] += _tn(m_ref[...], hv)
        acc_ref[NMAIN:, :] += _tn(kv_ref[...], hv)

        @pl.when(s == nsteps - 1)
        def _():
            for dst, src, size in pieces:
                o_ref[dst:dst + size, :] = acc_ref[src:src + size, :].astype(BF16)

    return pl.pallas_call(
        body, name="wgrad_in", grid=(nsteps,),
        in_specs=[pl.BlockSpec((tt, NMAIN), lambda s: (s, 0)), pl.BlockSpec((tt, kvw), lambda s: (s, 0)),
                  pl.BlockSpec((tt, D), lambda s: (s, 0))],
        out_specs=pl.BlockSpec((W_IN_ROWS, D), lambda s: (0, 0)),
        out_shape=jax.ShapeDtypeStruct((W_IN_ROWS, D), BF16),
        scratch_shapes=[pltpu.VMEM((NMAIN + kvw, D), F32)],
        compiler_params=_cparams(("arbitrary",)),
    )(dpm, dkv, h)


def _mixer_bwd(dxm, proj, ao, cw, gq, gk, sinks, gco, gao, wo, tq):
    t = proj.shape[0]
    nb = tq // BLK
    r8 = tq // 8
    nt = t // tq
    te = tq + 8
    kvw = 2 * NKV * HP

    def body(dx_ref, dxn_ref, p_ref, cgp_ref, hcp_ref, bgn_ref, cgn_ref, hcn_ref, kvp_ref, ao_ref, cw_ref, gq_ref,
             gk_ref, sk_ref, gco_ref, gao_ref, wo_ref,
             dpm_ref, dkvm_ref, dkvh_ref, dcw_ref, dgq_ref, dgk_ref, dsk_ref, dgco_ref, dgao_ref, acc_ref):
        i = pl.program_id(0)

        @pl.when(i == 0)
        def _():
            for r in (dcw_ref, dgq_ref, dgk_ref, dsk_ref, dgco_ref, dgao_ref):
                r[...] = jnp.zeros_like(r)

        acc_ref[...] = jnp.zeros_like(acc_ref)
        live_rows = jnp.where(i < nt - 1, te, tq)
        dxb = dx_ref[...].astype(BF16)
        dxe = jnp.concatenate([dxb, dxn_ref[...].astype(BF16)], axis=0)
        dcn = _nt(dxe, wo_ref[0:CC, :])
        bg = jnp.concatenate([p_ref[:, O_BG:O_BG + CC], bgn_ref[...]], axis=0)
        cg = jnp.concatenate([p_ref[:, O_CG:O_CG + CC], cgn_ref[...]], axis=0)
        hc = jnp.concatenate([p_ref[:, O_HC:O_HC + CC], hcn_ref[...]], axis=0)
        u = cg * hc
        up = jnp.where(i > 0, cgp_ref[...] * hcp_ref[...], 0.0)
        u1, u2 = _conv_taps(jnp.concatenate([up, u], axis=0), te)
        w0, w1, w2 = cw_ref[0:1, :], cw_ref[1:2, :], cw_ref[2:3, :]
        y = w0 * u2 + w1 * u1 + w2 * u
        co = bg * y
        rc, coh = _rms_fwd(co, 1.0 / CC)
        dco = _rms_bwd(dcn, gco_ref[...], coh, rc, 1.0 / CC)
        row_io = lax.broadcasted_iota(jnp.int32, (te, 1), 0)
        own = row_io < tq
        dgco_ref[...] += jnp.sum(jnp.where(own, dcn * coh, 0.0), axis=0, keepdims=True)
        dyc = jnp.where(row_io < live_rows, dco * bg, 0.0)
        dyo = jnp.where(own, dyc, 0.0)
        dcw_ref[0:1, :] += jnp.sum(dyo * u2, axis=0, keepdims=True)
        dcw_ref[1:2, :] += jnp.sum(dyo * u1, axis=0, keepdims=True)
        dcw_ref[2:3, :] += jnp.sum(dyo * u, axis=0, keepdims=True)
        dy1 = pltpu.roll(dyc, te - 1, 0)[0:tq]
        dy2 = pltpu.roll(dyc, te - 2, 0)[0:tq]
        du = w2 * dyc[0:tq] + w1 * dy1 + w0 * dy2
        dpm_ref[:, O_BG:O_BG + CC] = (dco[0:tq] * y[0:tq]).astype(BF16)
        dpm_ref[:, O_CG:O_CG + CC] = (du * hc[0:tq]).astype(BF16)
        dpm_ref[:, O_HC:O_HC + CC] = (du * cg[0:tq]).astype(BF16)
        kraw = jnp.concatenate([kvp_ref[:, 0:NKV * HP], p_ref[:, O_K:O_K + NKV * HP]], axis=0)
        vraw = jnp.concatenate([kvp_ref[:, NKV * HP:], p_ref[:, O_V:O_V + NKV * HP]], axis=0)
        gqv, gkv = gq_ref[...], gk_ref[...]
        keys = _norm_keys(kraw, gkv)
        vb = [vraw[:, h * HP:(h + 1) * HP].astype(BF16) for h in range(NKV)]
        base_valid, c_io = _band_mask()
        lane = lax.broadcasted_iota(jnp.int32, (1, HP), 1)
        dgq, dgk, dsk = (jnp.zeros((1, HP), F32) for _ in range(3))
        dgao = jnp.zeros((1, NQ * HD), F32)
        for b in range(nb):
            lo = jnp.where(i * nb + b == 0, BLK, 0)
            valid = base_valid & (c_io >= lo)
            band = slice(b * BLK, b * BLK + 2 * BLK)
            blk = slice(b * BLK, (b + 1) * BLK)
            ra, aoh = _rms_fwd(ao_ref[blk, :], 1.0 / (NQ * HD))
            danb = _nt(dxb[blk], wo_ref[CC:MIXW, :])
            dgao = dgao + jnp.sum(danb * aoh, axis=0, keepdims=True)
            dao = _rms_bwd(danb, gao_ref[...], aoh, ra, 1.0 / (NQ * HD))
            dos = [dao[:, g // 2 * HP:(g // 2 + 1) * HP] for g in range(NQ)]
            dos = [(d if g % 2 == 0 else pltpu.roll(d, HD, 1)).astype(BF16) for g, d in enumerate(dos)]
            fwd = []
            for g in range(NQ):
                rq, qh = _rms_fwd(p_ref[blk, O_Q + g * HP:O_Q + (g + 1) * HP], 1.0 / HD)
                qs = (qh * (gqv * SCALE)).astype(BF16)
                fwd.append((rq, qh, qs) + _attn_probs(qs, keys[g // GRP][2][band], sk_ref[0, g], valid))
            dqs = []
            for h in range(NKV):
                khat, rk, kn = [a[band] for a in keys[h]]
                dss, prbs, qns, dobs = [], [], [], []
                for g in range(h * GRP, (h + 1) * GRP):
                    rq, qh, qs, pr, ps = fwd[g]
                    dob = dos[g]
                    dp = _nt(dob, vb[h][band])
                    delta = jnp.sum(pr * dp, axis=-1, keepdims=True)
                    dsb = (pr * (dp - delta)).astype(BF16)
                    dsk = dsk + jnp.where(lane == g, -jnp.sum(ps * delta, axis=0, keepdims=True), 0.0)
                    dqn = jnp.dot(dsb, kn, preferred_element_type=F32) * SCALE
                    dgq = dgq + jnp.sum(dqn * qh, axis=0, keepdims=True)
                    dqs.append(_rms_bwd(dqn, gqv, qh, rq, 1.0 / HD).astype(BF16))
                    dss.append(dsb)
                    prbs.append(pr.astype(BF16))
                    qns.append(qs)
                    dobs.append(dob)
                dkn = _tn(jnp.concatenate(dss, axis=0), jnp.concatenate(qns, axis=0))
                dv = _tn(jnp.concatenate(prbs, axis=0), jnp.concatenate(dobs, axis=0))
                dgk = dgk + jnp.sum(dkn * khat, axis=0, keepdims=True)
                acc_ref[band, h * HP:(h + 1) * HP] += _rms_bwd(dkn, gkv, khat, rk, 1.0 / HD)
                acc_ref[band, (NKV + h) * HP:(NKV + h + 1) * HP] += dv
            dpm_ref[blk, O_Q:O_K] = jnp.concatenate(dqs, axis=1)
        dgq_ref[...] += dgq
        dgk_ref[...] += dgk
        dsk_ref[...] += dsk
        dgao_ref[...] += dgao
        dkvh_ref[...] = acc_ref[0:BLK, :]
        dkvm_ref[...] = acc_ref[BLK:, :]

    prev8 = lambda col: pl.BlockSpec((8, CC), lambda i: (jnp.maximum(i * r8 - 1, 0), col))
    next8 = lambda col: pl.BlockSpec((8, CC), lambda i: (jnp.minimum((i + 1) * r8, t // 8 - 1), col))
    small = lambda n: pl.BlockSpec((1, n), lambda i: (0, 0))
    return pl.pallas_call(
        body, name="mixer_bwd", grid=(nt,),
        in_specs=[
            pl.BlockSpec((tq, D), lambda i: (i, 0)),
            pl.BlockSpec((8, D), lambda i: (jnp.minimum((i + 1) * r8, t // 8 - 1), 0)),
            pl.BlockSpec((tq, NP), lambda i: (i, 0)),
            prev8(O_CG // CC), prev8(O_HC // CC),
            next8(O_BG // CC), next8(O_CG // CC), next8(O_HC // CC),
            pl.BlockSpec((BLK, kvw), lambda i: (jnp.maximum(i * nb - 1, 0), O_K // kvw)),
            pl.BlockSpec((tq, NQ * HD), lambda i: (i, 0)),
            _const_spec((8, CC)), _const_spec((1, HP)), _const_spec((1, HP)),
            pl.BlockSpec(memory_space=pltpu.SMEM),
            _const_spec((1, CC)), _const_spec((1, NQ * HD)), _const_spec((MIXW, D)),
        ],
        out_specs=[
            pl.BlockSpec((tq, NMAIN), lambda i: (i, 0)),
            pl.BlockSpec((tq, kvw), lambda i: (i, 0)),
            pl.BlockSpec((BLK, kvw), lambda i: (i, 0)),
            pl.BlockSpec((8, CC), lambda i: (0, 0)), small(HP), small(HP), small(HP), small(CC), small(NQ * HD),
        ],
        out_shape=[
            jax.ShapeDtypeStruct((t, NMAIN), BF16), jax.ShapeDtypeStruct((t, kvw), F32),
            jax.ShapeDtypeStruct((nt * BLK, kvw), F32),
            jax.ShapeDtypeStruct((8, CC), F32), jax.ShapeDtypeStruct((1, HP), F32), jax.ShapeDtypeStruct((1, HP), F32),
            jax.ShapeDtypeStruct((1, HP), F32), jax.ShapeDtypeStruct((1, CC), F32),
            jax.ShapeDtypeStruct((1, NQ * HD), F32),
        ],
        scratch_shapes=[pltpu.VMEM((tq + BLK, kvw), F32)],
        compiler_params=_cparams(("arbitrary",)),
    )(dxm, dxm, proj, proj, proj, proj, proj, proj, proj, ao, cw, gq, gk, sinks, gco, gao, wo)


def _inproj_bwd(dpm, dkvm, dkvh, wpt, x, g1, dxm, tm):
    t = x.shape[0]
    kvw = 2 * NKV * HP
    nt = t // tm

    def body(dp_ref, dk_ref, dh_ref, w_ref, x_ref, g_ref, dxm_ref, dx_ref, dg_ref, dkv_ref):
        i = pl.program_id(0)

        @pl.when(i == 0)
        def _():
            dg_ref[...] = jnp.zeros_like(dg_ref)

        halo = jnp.where(i < nt - 1, dh_ref[...], 0.0)
        dkv_ref[0:tm - BLK, :] = dk_ref[0:tm - BLK, :].astype(BF16)
        dkv_ref[tm - BLK:tm, :] = (dk_ref[tm - BLK:tm, :] + halo).astype(BF16)
        dh = (jnp.dot(dp_ref[...], w_ref[0:NMAIN, :], preferred_element_type=F32)
              + jnp.dot(dkv_ref[...], w_ref[NMAIN:NP, :], preferred_element_type=F32))
        r, xh = _rms_fwd(x_ref[...], 1.0 / D)
        dg_ref[...] += jnp.sum(dh * xh, axis=0, keepdims=True)
        dx_ref[...] = dxm_ref[...] + _rms_bwd(dh, g_ref[...], xh, r, 1.0 / D)

    row = lambda w: pl.BlockSpec((tm, w), lambda i: (i, 0))
    return pl.pallas_call(
        body, name="inproj_bwd", grid=(nt,),
        in_specs=[row(NMAIN), row(kvw), pl.BlockSpec((BLK, kvw), lambda i: (jnp.minimum(i + 1, nt - 1), 0)),
                  _const_spec((NP, D)), row(D), _const_spec((1, D)), row(D)],
        out_specs=[row(D), pl.BlockSpec((1, D), lambda i: (0, 0)), row(kvw)],
        out_shape=[jax.ShapeDtypeStruct((t, D), F32), jax.ShapeDtypeStruct((1, D), F32),
                   jax.ShapeDtypeStruct((t, kvw), BF16)],
        compiler_params=_cparams(("arbitrary",)),
    )(dpm, dkvm, dkvh, wpt, x, g1, dxm)


def _presum_halves(gs, theirs, core, chip):
    n = len(gs)

    def body(c_ref, chip_ref, *refs):
        for g_ref, t_ref, o_ref, keep_ref in zip(refs[:n], refs[n:2 * n], refs[2 * n:3 * n], refs[3 * n:]):
            val = (g_ref[...].astype(F32) + t_ref[...].astype(F32)).astype(BF16)
            o_ref[...] = val

            @pl.when(pl.program_id(1) == chip_ref[0])
            def _():
                keep_ref[...] = val

    tile = lambda ta: (ta.shape[1] // 2, ta.shape[2])
    half = lambda ta: pl.BlockSpec((None,) + tile(ta), lambda j, k, c_ref, chip_ref: (k, j, 0))
    own = lambda ta: pl.BlockSpec((None,) + tile(ta), lambda j, k, c_ref, chip_ref: (k, 2 * c_ref[0] + j, 0))
    kept = lambda ta: pl.BlockSpec(tile(ta), lambda j, k, c_ref, chip_ref: (j, 0))
    res = pl.pallas_call(
        body, name="presum",
        grid_spec=pltpu.PrefetchScalarGridSpec(
            num_scalar_prefetch=2, grid=(2, N_CHIPS),
            in_specs=[own(ta) for ta in theirs] + [half(ta) for ta in theirs],
            out_specs=[half(ta) for ta in theirs] + [kept(ta) for ta in theirs]),
        out_shape=[jax.ShapeDtypeStruct(ta.shape, BF16) for ta in theirs]
        + [jax.ShapeDtypeStruct(ta.shape[1:], BF16) for ta in theirs],
        compiler_params=_cparams(("arbitrary", "arbitrary")),
    )(core, chip, *gs, *theirs)
    return res[:n], res[n:]


def _sum_chips(got, kept, chip):
    n = len(got)
    steps = 2

    def body(chip_ref, *refs):
        for c_ref, own_ref, o_ref in zip(refs[:n], refs[n:2 * n], refs[2 * n:]):
            acc = None
            for j in range(N_CHIPS):
                term = jnp.where(chip_ref[0] == j, own_ref[...], c_ref[j]).astype(F32)
                acc = term if acc is None else acc + term
            o_ref[...] = acc

    tile = lambda c: (c.shape[1] // steps, c.shape[2])
    return pl.pallas_call(
        body, name="chipsum",
        grid_spec=pltpu.PrefetchScalarGridSpec(
            num_scalar_prefetch=1, grid=(steps,),
            in_specs=[pl.BlockSpec((N_CHIPS,) + tile(c), lambda i, chip_ref: (0, i, 0)) for c in got]
            + [pl.BlockSpec(tile(c), lambda i, chip_ref: (i, 0)) for c in got],
            out_specs=[pl.BlockSpec(tile(c), lambda i, chip_ref: (i, 0)) for c in got]),
        out_shape=[jax.ShapeDtypeStruct(c.shape[1:], F32) for c in got],
        compiler_params=_cparams(("parallel",)),
    )(chip, *got, *kept)


def _adamw_refs(w_ref, g_ref, m_ref, v_ref, d_ref, mo_ref, vo_ref):
    c1 = 1.0 - ADAM_B1 ** ADAM_STEP
    c2 = 1.0 - ADAM_B2 ** ADAM_STEP
    gv = g_ref[...]
    mn = ADAM_B1 * m_ref[...] + (1.0 - ADAM_B1) * gv
    vn = ADAM_B2 * v_ref[...] + (1.0 - ADAM_B2) * (gv * gv)
    mo_ref[...] = mn
    vo_ref[...] = vn
    d_ref[...] = -ADAM_LR * ((mn / c1) / (jnp.sqrt(vn / c2) + ADAM_EPS) + ADAM_WD * w_ref[...])


def _adamw_small(ws, gs, ms, vs):
    n = len(ws)

    def body(*refs):
        for k in range(n):
            _adamw_refs(*refs[k::n])

    res = pl.pallas_call(
        body, name="adamw_small", out_shape=[jax.ShapeDtypeStruct(w.shape, F32) for w in ws] * 3,
        compiler_params=_cparams(),
    )(*ws, *gs, *ms, *vs)
    return res[:n], res[n:2 * n], res[2 * n:]


def _adamw_halves(w, m, v, mine, theirs, core, name):
    depth = len(mine)
    half, cols = mine[0].shape
    assert w.shape == (2 * depth * half, cols)

    def body(core_ref, *refs):
        halves, (w_ref, m_ref, v_ref, g_ref, d_ref, mo_ref, vo_ref) = refs[:2 * depth], refs[2 * depth:]
        for l in range(depth):
            for h in range(2):
                @pl.when(pl.program_id(0) == 2 * l + h)
                def _(l=l, h=h):
                    g_ref[...] = jnp.where(core_ref[0] == h, halves[l][...], halves[depth + l][...])
        _adamw_refs(w_ref, g_ref, m_ref, v_ref, d_ref, mo_ref, vo_ref)

    spec = pl.BlockSpec((half, cols), lambda i, core_ref: (i, 0))
    sds = jax.ShapeDtypeStruct(w.shape, F32)
    return pl.pallas_call(
        body, name=name,
        grid_spec=pltpu.PrefetchScalarGridSpec(
            num_scalar_prefetch=1, grid=(2 * depth,),
            in_specs=[_const_spec((half, cols))] * (2 * depth) + [spec] * 3, out_specs=[spec] * 4),
        out_shape=[sds] * 4,
        compiler_params=_cparams(("arbitrary",)),
    )(core, *mine, *theirs, w, m, v)


def _place():
    x, y, c = lax.axis_index("x"), lax.axis_index("y"), lax.axis_index("c")
    chips = [(1 - x, y), (x, 1 - y), (1 - x, 1 - y)]
    return x, y, c, chips


ANY = pl.BlockSpec(memory_space=pl.ANY)
DMA_ROWS = 64


def _pieces(shape):
    rows = shape[-2]
    step = DMA_ROWS if rows % DMA_ROWS == 0 else rows
    lead = [()]
    for n in shape[:-2]:
        lead = [i + (k,) for i in lead for k in range(n)]
    return [i + (pl.ds(r0, step),) for i in lead for r0 in range(0, rows, step)]


def _start_pieces(make, src, dst):
    for idx in _pieces(src.shape):
        make(src.at[idx], dst.at[idx]).start()


def _gather_body(srcs, outs, sems, layer, start):
    nw = len(srcs)
    ssem, rsem, fssem, frsem = sems
    x, y, c, chips = _place()
    kme = 2 * x + y

    def plane(j, w, to):
        return lambda s, d: pltpu.make_async_remote_copy(
            src_ref=s, dst_ref=d, send_sem=ssem.at[j, w], recv_sem=rsem.at[j, w], device_id=to,
            device_id_type=MESH)

    def passed(j, w):
        return lambda s, d: pltpu.make_async_remote_copy(
            src_ref=s, dst_ref=d, send_sem=fssem.at[j, w], recv_sem=frsem.at[j, w],
            device_id=(x, y, 1 - c), device_id_type=MESH)

    @pl.when(c == layer)
    def _():
        for j, (px, py) in enumerate(chips):
            for w in range(nw):
                start(plane(j, w, (px, py, c)), srcs[w], outs[w].at[kme])
        for j, (px, py) in enumerate(chips):
            for w in range(nw):
                got = outs[w].at[2 * px + py]
                plane(j, w, (px, py, c))(got, got).wait_recv()
                start(passed(j, w), got, got)
        for j, (px, py) in enumerate(chips):
            for w in range(nw):
                got = outs[w].at[2 * px + py]
                plane(j, w, (px, py, c))(got, got).wait_send()
                passed(j, w)(got, got).wait_send()

    @pl.when(c != layer)
    def _():
        for j, (px, py) in enumerate(chips):
            for w in range(nw):
                got = outs[w].at[2 * px + py]
                passed(j, w)(got, got).wait_recv()


def _handshake(peers):
    barrier = pltpu.get_barrier_semaphore()
    for peer in peers:
        pl.semaphore_signal(barrier, inc=1, device_id=peer, device_id_type=MESH)
    pl.semaphore_wait(barrier, len(peers))


def _handshake_all():
    x, y, c, _ = _place()
    _handshake([(x ^ (r >> 2), y ^ ((r >> 1) & 1), c ^ (r & 1)) for r in range(1, 8)])


def _gather_layer_async(blocks, layer, name, collective_id):
    hbm = pltpu.MemorySpace.HBM
    srcs = [jax.new_ref(b, memory_space=hbm) for b in blocks]
    outs = [jax.empty_ref(jax.ShapeDtypeStruct((N_CHIPS,) + b.shape, b.dtype), memory_space=hbm) for b in blocks]

    @pl.kernel(mesh=plsc.ScalarSubcoreMesh(axis_name="seq", num_cores=1), name=name,
               scratch_types=[pltpu.SemaphoreType.DMA((3, len(blocks)))] * 4,
               compiler_params=pltpu.CompilerParams(collective_id=collective_id))
    def launch(*sems):
        _handshake_all()
        _gather_body(srcs, outs, sems, layer, lambda make, s, d: make(s, d).start())

    launch()
    return [o[...] for o in outs]


def _swap_siblings(arrs, halves, name, collective_id=None):
    nw = len(arrs)
    out_sds = [jax.ShapeDtypeStruct((a.shape[0], a.shape[1] // 2, a.shape[2]) if halves else a.shape, a.dtype)
               for a in arrs]

    def exchange(srcs, outs, ssem, rsem, start):
        x, y, c, _ = _place()

        def give(w):
            return lambda s, d: pltpu.make_async_remote_copy(
                src_ref=s, dst_ref=d, send_sem=ssem.at[w], recv_sem=rsem.at[w], device_id=(x, y, 1 - c),
                device_id_type=MESH)

        for w in range(nw):
            hr = outs[w].shape[1]
            start(give(w), srcs[w].at[:, pl.ds((1 - c) * hr, hr)] if halves else srcs[w], outs[w])
        for w in range(nw):
            give(w)(outs[w], outs[w]).wait()

    if collective_id is None:
        def body(*refs):
            exchange(refs[:nw], refs[nw:2 * nw], *refs[2 * nw:], _start_pieces)

        return pl.pallas_call(
            body, name=name, in_specs=[ANY] * nw, out_specs=[ANY] * nw, out_shape=out_sds,
            scratch_shapes=[pltpu.SemaphoreType.DMA((nw,))] * 2,
            compiler_params=_cparams(has_side_effects=True),
        )(*arrs)

    hbm = pltpu.MemorySpace.HBM
    srcs = [jax.new_ref(a, memory_space=hbm) for a in arrs]
    outs = [jax.empty_ref(sds, memory_space=hbm) for sds in out_sds]

    @pl.kernel(mesh=plsc.ScalarSubcoreMesh(axis_name="seq", num_cores=1), name=name,
               scratch_types=[pltpu.SemaphoreType.DMA((nw,))] * 2,
               compiler_params=pltpu.CompilerParams(collective_id=collective_id))
    def launch(ssem, rsem):
        x, y, c, _ = _place()
        _handshake([(x, y, 1 - c)])
        exchange(srcs, outs, ssem, rsem, lambda make, s, d: make(s, d).start())

    launch()
    return [o[...] for o in outs]


def _scatter_body(srcs, outs, sems, start):
    nw = len(srcs)
    ssem, rsem = sems
    x, y, c, chips = _place()
    kme = 2 * x + y

    def give(j, w, to):
        return lambda s, d: pltpu.make_async_remote_copy(
            src_ref=s, dst_ref=d, send_sem=ssem.at[j, w], recv_sem=rsem.at[j, w], device_id=to,
            device_id_type=MESH)

    for j, (px, py) in enumerate(chips):
        for w in range(nw):
            start(give(j, w, (px, py, c)), srcs[w].at[2 * px + py], outs[w].at[kme])
    for j, (px, py) in enumerate(chips):
        for w in range(nw):
            got = outs[w].at[2 * px + py]
            give(j, w, (px, py, c))(got, got).wait_recv()
    for j, (px, py) in enumerate(chips):
        for w in range(nw):
            sent = srcs[w].at[2 * px + py]
            give(j, w, (px, py, c))(sent, sent).wait_send()


def _scatter_chips_async(ps, name, collective_id):
    hbm = pltpu.MemorySpace.HBM
    srcs = [jax.new_ref(p, memory_space=hbm) for p in ps]
    outs = [jax.empty_ref(jax.ShapeDtypeStruct(p.shape, p.dtype), memory_space=hbm) for p in ps]

    @pl.kernel(mesh=plsc.ScalarSubcoreMesh(axis_name="seq", num_cores=1), name=name,
               scratch_types=[pltpu.SemaphoreType.DMA((3, len(ps)))] * 2,
               compiler_params=pltpu.CompilerParams(collective_id=collective_id))
    def launch(*sems):
        _handshake_all()
        _scatter_body(srcs, outs, sems, lambda make, s, d: make(s, d).start())

    launch()
    return [o[...] for o in outs]


def _allreduce_small(v):
    rows = v.shape[0]

    def body(v_ref, o_ref, buf, ssem, rsem):
        x, y, c, _ = _place()
        me = 4 * x + 2 * y + c
        buf[me] = v_ref[...]
        sends = []
        for r in range(1, 8):
            peer = (x ^ (r >> 2), y ^ ((r >> 1) & 1), c ^ (r & 1))
            cp = pltpu.make_async_remote_copy(
                src_ref=v_ref, dst_ref=buf.at[me], send_sem=ssem.at[r - 1], recv_sem=rsem.at[r - 1],
                device_id=peer, device_id_type=MESH)
            cp.start()
            sends.append(cp)
        for r in range(1, 8):
            src = me ^ r
            pltpu.make_async_remote_copy(
                src_ref=v_ref, dst_ref=buf.at[src], send_sem=ssem.at[r - 1], recv_sem=rsem.at[r - 1],
                device_id=(x, y, c), device_id_type=MESH).wait_recv()
        for cp in sends:
            cp.wait_send()
        acc = buf[0]
        for d in range(1, 8):
            acc = acc + buf[d]
        o_ref[...] = acc

    vm = pl.BlockSpec(memory_space=pltpu.VMEM)
    return pl.pallas_call(
        body, name="allreduce_small", in_specs=[vm], out_specs=vm,
        out_shape=jax.ShapeDtypeStruct(v.shape, F32),
        scratch_shapes=[pltpu.VMEM((8, rows, 128), F32), pltpu.SemaphoreType.DMA((7,)),
                        pltpu.SemaphoreType.DMA((7,))],
        compiler_params=_cparams(has_side_effects=True),
    )(v)


def _t(w):
    return jnp.swapaxes(w, -1, -2)


def _count(shape):
    n = 1
    for s in shape:
        n *= s
    return n


def _pack_rows(arrs):
    flat = [jnp.pad(a.reshape(-1), (0, (-_count(a.shape)) % 128)) for a in arrs]
    v = jnp.concatenate(flat)
    rows = -(-v.shape[0] // (8 * 128)) * 8
    return jnp.pad(v, (0, rows * 128 - v.shape[0])).reshape(rows, 128)


def kernel(x, norm1_g, w_in, conv_w, q_norm_g, k_norm_g, sinks, conv_out_g, attn_out_g, w_o, norm2_g, w_gate, w_up, w_down, loss_target, m_norm1_g, m_w_in, m_conv_w, m_q_norm_g, m_k_norm_g, m_sinks, m_conv_out_g, m_attn_out_g, m_w_o, m_norm2_g, m_w_gate, m_w_up, m_w_down, v_norm1_g, v_w_in, v_conv_w, v_q_norm_g, v_k_norm_g, v_sinks, v_conv_out_g, v_attn_out_g, v_w_o, v_norm2_g, v_w_gate, v_w_up, v_w_down):
    depth = w_in.shape[0]
    t = x.shape[1]
    xs = x.reshape(t, D)
    tgt = loss_target.reshape(t, D)
    xi, yi = lax.axis_index("x"), lax.axis_index("y")
    kme = 2 * xi + yi
    tm = min(512, t)
    tq = min(512, t)
    tf = min(256, t)
    tw = min(1024, t)

    cwp = jnp.pad(conv_w.reshape(depth * 3, CC // N_CHIPS), ((0, 8 - depth * 3), (0, 0)))
    own_f = [jnp.concatenate([_t(w_gate[l]), _t(w_up[l]), w_down[l]], axis=0).astype(BF16) for l in range(depth)]
    own_o = [w_o[l].astype(BF16) for l in range(depth)]
    own_i = [_t(w_in[l]).astype(BF16) for l in range(depth)]
    mine = lambda got, own: lax.dynamic_update_index_in_dim(got, own, kme, 0)
    (got_i0,) = _gather_layer_async([own_i[0]], 0, "gather_in0_seq", collective_id=14)
    got_ocw = _gather_layer_async([own_o[0], cwp], 0, "gather_o0_seq", collective_id=15)
    got_i0, own_f, own_o, own_i = lax.optimization_barrier((got_i0, own_f, own_o, own_i))
    gf0_in = lax.optimization_barrier((own_f[0], got_i0))[0]
    (got_f0,) = _gather_layer_async([gf0_in], 0, "gather_ffn0_seq", collective_id=6)

    chip = kme.reshape(1).astype(jnp.int32)

    def layer_params(l, got_o, cw_full):
        return dict(
            wo=mine(got_o, own_o[l]).reshape(MIXW, D),
            cw=jnp.pad(cw_full[l], ((0, 5), (0, 0))),
            g1=norm1_g[l].reshape(1, D), g2=norm2_g[l].reshape(1, D),
            gq=jnp.pad(q_norm_g[l], (0, HP - HD)).reshape(1, HP), gk=jnp.pad(k_norm_g[l], (0, HP - HD)).reshape(1, HP),
            sk=sinks[l].reshape(1, NQ), gco=conv_out_g[l].reshape(1, CC),
            gao=attn_out_g[l].reshape(1, NQ * HD))

    saved, layers = [], []
    cur = xs
    for l in range(depth):
        x_in = cur
        if l == 0:
            got_i = got_i0
        else:
            got_f1, got_o, got_i = lax.optimization_barrier((got_l1, cur))[0]
        proj, h, wpt = _inproj_fwd(cur, norm1_g[l].reshape(1, D), got_i, own_i[l], chip, tm)
        if l == 0:
            got_o, got_cw = lax.optimization_barrier((got_ocw, proj))[0]
            cw_full = mine(got_cw, cwp).transpose(1, 0, 2).reshape(8, CC)[:depth * 3].reshape(depth, 3, CC)
        p = layer_params(l, got_o, cw_full)
        p["wpt"] = wpt
        xm, mix, ao = _mixer_fwd(proj, cur, p["cw"], p["gq"], p["gk"], p["sk"], p["gco"], p["gao"], p["wo"], tq)
        if l == 0:
            got_f0 = lax.optimization_barrier((got_f0, xm))[0]
            l1_in = lax.optimization_barrier(([own_f[1], own_o[1], own_i[1]], got_f0))[0]
            got_l1 = _gather_layer_async(l1_in, 1, "gather_layer1_seq", collective_id=1)
        p["gf"] = mine(got_f0 if l == 0 else got_f1, own_f[l])
        layers.append(p)
        if l < depth - 1:
            cur, a, b, h2 = _ffn_fwd(xm, p["g2"], p["gf"], tm)
        else:
            lpart, dy, a, b, h2 = _ffn_fwd(xm, p["g2"], p["gf"], tm, tgt)
        saved.append(dict(x=x_in, proj=proj, h=h, xm=xm, mix=mix, ao=ao, a=a, b=b, h2=h2))

    ci = lax.axis_index("c")
    core = ci.reshape(1).astype(jnp.int32)
    rbig = [dict() for _ in range(depth)]
    gsmall = [None] * depth

    def after_(vals, after):
        return vals if after is None else lax.optimization_barrier((vals, after))[0]

    def reduce_1(gs, tag, ids):
        return gs, _swap_siblings(gs, True, f"swap_halves_{tag}_seq", ids[0]), tag, ids

    def reduce_2(state, after):
        gs, theirs, tag, ids = state
        ps, kept = _presum_halves(gs, after_(theirs, after), core, chip)
        return kept, _scatter_chips_async(ps, f"scatter_{tag}_seq", ids[1]), tag, ids

    def reduce_3(state, after):
        kept, got, tag, ids = state
        r_mine = _sum_chips(after_(got, after), kept, chip)
        return r_mine, _swap_siblings(r_mine, False, f"swap_reduced_{tag}" + ("_seq" if ids[2] else ""), ids[2])

    def reduce_4(state, after):
        r_mine, r_theirs = state
        return list(zip(r_mine, after_(r_theirs, after)))

    ids = {"ffn1": (7, 4, 8), "in1": (9, 5, 10), "ffn0": (11, 2, 12), "in0": (13, 3, None)}
    in_2 = scattering = None
    handed = {}
    for l in reversed(range(depth)):
        p, s = layers[l], saved[l]
        dxm, da, db, hm, dg2 = _ffn_bwd(dy, s["xm"], p["g2"], s["a"], s["b"], p["gf"], tf)
        if in_2 is not None:
            in_2 = reduce_2(in_2, dxm)
        g_wg = _wgrad_blocks(da, s["h2"], tw, "wgrad_gate")
        g_wu = _wgrad_blocks(db, s["h2"], tw, "wgrad_up")
        g_wd = _wgrad_blocks(hm, dy, tw, "wgrad_down")
        if in_2 is not None:
            handed[f"in{l + 1}"] = reduce_3(in_2, g_wd)
        ffn_1 = reduce_1([g_wg, g_wu, g_wd], f"ffn{l}", ids[f"ffn{l}"])
        dpm, dkvm, dkvh, dcw, dgq, dgk, dsk, dgco, dgao = _mixer_bwd(
            after_(dxm, scattering), s["proj"], s["ao"], p["cw"], p["gq"], p["gk"], p["sk"], p["gco"], p["gao"],
            p["wo"], tq)
        ffn_2 = reduce_2(ffn_1, dpm)
        scattering = ffn_2[0]
        g_o = _wgrad(s["mix"], dxm, tw, "wgrad_o")
        dx, dg1, dkv = _inproj_bwd(dpm, dkvm, dkvh, p["wpt"], s["x"], p["g1"], dxm, tq)
        g_in = _wgrad_in(dpm, dkv, s["h"], tw)
        dy = dx
        gsmall[l] = dict(g1=dg1, cw=dcw[:3], gq=dgq[0, :HD], gk=dgk[0, :HD], sk=dsk[0, :NQ], gco=dgco,
                         gao=dgao, g2=dg2)
        above = handed.get(f"ffn{l + 1}")
        handed[f"ffn{l}"] = reduce_3(ffn_2, g_in if above is None else (g_in, above[0]))
        in_2 = reduce_1([g_in.reshape(N_CHIPS, -1, D), g_o.reshape(N_CHIPS, -1, D)], f"in{l}", ids[f"in{l}"])
    grad_x = dy.reshape(x.shape)

    small_shapes = dict(g1=(D,), cw=(3, CC), gq=(HD,), gk=(HD,), sk=(NQ,), gco=(CC,), gao=(NQ * HD,), g2=(D,))
    red = _allreduce_small(_pack_rows([gsmall[l][n] for l in range(depth) for n in small_shapes]
                                      + [lpart[0:1, 0:1]])).reshape(-1)
    red_small, offs = {n: [] for n in small_shapes}, 0
    for l in range(depth):
        for n, shp in small_shapes.items():
            cnt = _count(shp)
            red_small[n].append(red[offs:offs + cnt].reshape(shp))
            offs += -(-cnt // 128) * 128
    loss = red[offs]
    g_small = {n: jnp.stack(v) for n, v in red_small.items()}
    g_cw = lax.dynamic_slice_in_dim(g_small["cw"], kme * (CC // N_CHIPS), CC // N_CHIPS, axis=2)

    weights = [norm1_g, w_in, conv_w, q_norm_g, k_norm_g, sinks, conv_out_g, attn_out_g, w_o, norm2_g, w_gate,
               w_up, w_down]
    moms = [m_norm1_g, m_w_in, m_conv_w, m_q_norm_g, m_k_norm_g, m_sinks, m_conv_out_g, m_attn_out_g, m_w_o,
            m_norm2_g, m_w_gate, m_w_up, m_w_down]
    vars_ = [v_norm1_g, v_w_in, v_conv_w, v_q_norm_g, v_k_norm_g, v_sinks, v_conv_out_g, v_attn_out_g, v_w_o,
             v_norm2_g, v_w_gate, v_w_up, v_w_down]
    n_w = len(weights)
    big_idx = dict(zip(("in", "o", "g", "u", "d"), (1, 8, 10, 11, 12)))
    small_idx = [n for n in range(n_w) if n not in big_idx.values()]
    grads, deltas, new_m, new_v = [None] * n_w, [None] * n_w, [None] * n_w, [None] * n_w
    for n, g in zip(small_idx, (g_small["g1"], g_cw, g_small["gq"], g_small["gk"], g_small["sk"], g_small["gco"],
                                g_small["gao"], g_small["g2"])):
        grads[n] = g

    def update_big(name):
        n = big_idx[name]
        mine, theirs = zip(*[rbig[l][name] for l in range(depth)])
        shape = (depth, 2 * mine[0].shape[0], D)
        flip = shape != weights[n].shape
        rows2d = lambda a3: (_t(a3) if flip else a3).reshape(-1, D)
        res = _adamw_halves(rows2d(weights[n]), rows2d(moms[n]), rows2d(vars_[n]), mine, theirs, core,
                            f"adamw_{n}")
        res = [r.reshape(shape) for r in res]
        grads[n], deltas[n], new_m[n], new_v[n] = [_t(r) for r in res] if flip else res

    for l in range(depth):
        rbig[l]["g"], rbig[l]["u"], rbig[l]["d"] = reduce_4(handed[f"ffn{l}"], red)
    rbig[1]["in"], rbig[1]["o"] = reduce_4(handed["in1"], red)
    update_big("g")
    in_2 = reduce_2(in_2, new_v[big_idx["g"]])
    update_big("u")
    update_big("d")
    rbig[0]["in"], rbig[0]["o"] = reduce_4(reduce_3(in_2, new_v[big_idx["d"]]), None)
    for name in ("in", "o"):
        update_big(name)
    res = _adamw_small(*[[arrs[n] for n in small_idx] for arrs in (weights, grads, moms, vars_)])
    for k, n in enumerate(small_idx):
        deltas[n], new_m[n], new_v[n] = res[0][k], res[1][k], res[2][k]
    return (loss, grad_x, *grads, *deltas, *new_m, *new_v)
```

```python
import jax
import jax.numpy as jnp
from jax import lax
from jax.experimental import pallas as pl
from jax.experimental.pallas import tpu as pltpu
from jax.experimental.pallas import tpu_sc as plsc

F32 = jnp.float32
BF16 = jnp.bfloat16

D = 1024
CC = 512
NQ = 8
NKV = 2
HD = 64
HP = 128
GRP = NQ // NKV
FF = 2816
FFB = FF // 4
BLK = 128
EPS = 1e-6
NEG = -1e30
SCALE = HD ** -0.5
O_BG, O_CG, O_HC, O_Q = 0, CC, 2 * CC, 3 * CC
O_K = O_Q + NQ * HP
O_V = O_K + NKV * HP
NP = O_V + NKV * HP
NMAIN = O_K
MIXW = CC + NQ * HD
N_CHIPS = 4
VMEM_LIMIT = 56 * 1024 * 1024
MESH = pl.DeviceIdType.MESH

ADAM_LR, ADAM_B1, ADAM_B2, ADAM_EPS, ADAM_WD, ADAM_STEP = 0.001, 0.9, 0.999, 1e-08, 0.01, 10


def _cparams(sem=None, **kw):
    if sem is not None:
        kw["dimension_semantics"] = sem
    return pltpu.CompilerParams(vmem_limit_bytes=VMEM_LIMIT, **kw)


def _const_spec(shape):
    nd = len(shape)
    return pl.BlockSpec(shape, lambda *_: (0,) * nd, pipeline_mode=pl.Buffered(1))


def _nt(a, b):
    return lax.dot_general(a, b, (((1,), (1,)), ((), ())), preferred_element_type=F32)


def _tn(a, b):
    return lax.dot_general(a, b, (((0,), (0,)), ((), ())), preferred_element_type=F32)


def _rms_fwd(x, inv_n):
    r = lax.rsqrt(jnp.sum(x * x, axis=-1, keepdims=True) * inv_n + EPS)
    return r, x * r


def _rms_bwd(dy, g, xh, r, inv_n):
    dxh = dy * g
    return r * (dxh - xh * (jnp.sum(dxh * xh, axis=-1, keepdims=True) * inv_n))


W_IN_ROWS = 3 * CC + (NQ + 2 * NKV) * HD
W_IN_BLOCK = W_IN_ROWS // N_CHIPS


def _padded_row(row):
    return row + max(row - O_Q, 0) // HD * (HP - HD)


def _w_in_pieces(k):
    first = k * W_IN_BLOCK
    plain = min(max(O_Q - first, 0), W_IN_BLOCK)
    pieces = [(0, first, plain)] if plain else []
    return pieces + [(r, _padded_row(first + r), HD) for r in range(plain, W_IN_BLOCK, HD)]


def _inproj_fwd(x, g1, gi, own_i, chip, tm):
    t = x.shape[0]

    def body(chip_ref, x_ref, g_ref, gi_ref, own_ref, p_ref, h_ref, w_ref, sem):
        @pl.when(pl.program_id(0) == 0)
        def _():
            for k in range(N_CHIPS):
                for src, dst, rows in _w_in_pieces(k):
                    @pl.when(chip_ref[0] == k)
                    def _():
                        pltpu.make_async_copy(own_ref.at[pl.ds(src, rows)], w_ref.at[pl.ds(dst, rows)], sem).start()

                    @pl.when(chip_ref[0] != k)
                    def _():
                        pltpu.make_async_copy(gi_ref.at[k, pl.ds(src, rows)], w_ref.at[pl.ds(dst, rows)], sem).start()
            for slot in range(NQ + 2 * NKV):
                w_ref[O_Q + slot * HP + HD:O_Q + (slot + 1) * HP, :] = jnp.zeros((HP - HD, D), BF16)
            landed = w_ref.at[pl.ds(0, W_IN_ROWS)]
            pltpu.make_async_copy(landed, landed, sem).wait()

        _, xh = _rms_fwd(x_ref[...], 1.0 / D)
        h = (xh * g_ref[...]).astype(BF16)
        h_ref[...] = h
        p_ref[...] = _nt(h, w_ref[...])

    const = lambda shape: pl.BlockSpec(shape, lambda i, c: (0,) * len(shape))
    return pl.pallas_call(
        body, name="inproj_fwd",
        grid_spec=pltpu.PrefetchScalarGridSpec(
            num_scalar_prefetch=1, grid=(t // tm,),
            in_specs=[pl.BlockSpec((tm, D), lambda i, c: (i, 0)), const((1, D)), ANY, ANY],
            out_specs=[pl.BlockSpec((tm, NP), lambda i, c: (i, 0)), pl.BlockSpec((tm, D), lambda i, c: (i, 0)),
                       const((NP, D))],
            scratch_shapes=[pltpu.SemaphoreType.DMA(())]),
        out_shape=[jax.ShapeDtypeStruct((t, NP), F32), jax.ShapeDtypeStruct((t, D), BF16),
                   jax.ShapeDtypeStruct((NP, D), BF16)],
        compiler_params=_cparams(("arbitrary",)),
    )(chip, x, g1, gi, own_i)


def _band_mask():
    r_io = lax.broadcasted_iota(jnp.int32, (BLK, 2 * BLK), 0)
    c_io = lax.broadcasted_iota(jnp.int32, (BLK, 2 * BLK), 1)
    return (c_io > r_io) & (c_io <= r_io + BLK), c_io


def _conv_taps(uf, n):
    u1 = pltpu.roll(uf, 1, 0)[8:8 + n]
    u2 = pltpu.roll(uf, 2, 0)[8:8 + n]
    return u1, u2


def _attn_probs(qs, kband, sink, valid):
    s = jnp.where(valid, _nt(qs, kband), NEG)
    m = jnp.maximum(jnp.max(s, axis=-1, keepdims=True), sink)
    p = jnp.exp(s - m)
    es = jnp.exp(sink - m)
    inv = 1.0 / (jnp.sum(p, axis=-1, keepdims=True) + es)
    return p * inv, es * inv


def _norm_keys(kraw, gk):
    out = []
    for h in range(NKV):
        kh = kraw[:, h * HP:(h + 1) * HP]
        rk, khat = _rms_fwd(kh, 1.0 / HD)
        out.append((khat, rk, (khat * gk).astype(BF16)))
    return out


def _mixer_fwd(proj, x, cw, gq, gk, sinks, gco, gao, wo, tq):
    t = proj.shape[0]
    nb = tq // BLK
    r8 = tq // 8

    def body(p_ref, cgp_ref, hcp_ref, kvp_ref, x_ref, cw_ref, gq_ref, gk_ref, sk_ref, gco_ref, gao_ref,
             wo_ref, xm_ref, mix_ref, ao_ref, aop_ref):
        i = pl.program_id(0)
        cg = p_ref[:, O_CG:O_CG + CC]
        hc = p_ref[:, O_HC:O_HC + CC]
        u = cg * hc
        up = jnp.where(i > 0, cgp_ref[...] * hcp_ref[...], 0.0)
        u1, u2 = _conv_taps(jnp.concatenate([up, u], axis=0), tq)
        y = cw_ref[0:1, :] * u2 + cw_ref[1:2, :] * u1 + cw_ref[2:3, :] * u
        co = p_ref[:, O_BG:O_BG + CC] * y
        _, coh = _rms_fwd(co, 1.0 / CC)
        cn = coh * gco_ref[...]
        kraw = jnp.concatenate([kvp_ref[:, 0:NKV * HP], p_ref[:, O_K:O_K + NKV * HP]], axis=0)
        vraw = jnp.concatenate([kvp_ref[:, NKV * HP:], p_ref[:, O_V:O_V + NKV * HP]], axis=0)
        keys = _norm_keys(kraw, gk_ref[...])
        vb = [vraw[:, h * HP:(h + 1) * HP].astype(BF16) for h in range(NKV)]
        base_valid, c_io = _band_mask()
        gqs = gq_ref[...] * SCALE
        for b in range(nb):
            lo = jnp.where(i * nb + b == 0, BLK, 0)
            valid = base_valid & (c_io >= lo)
            for g in range(NQ):
                h = g // GRP
                qg = p_ref[b * BLK:(b + 1) * BLK, O_Q + g * HP:O_Q + (g + 1) * HP]
                _, qh = _rms_fwd(qg, 1.0 / HD)
                qs = (qh * gqs).astype(BF16)
                pr, _ = _attn_probs(qs, keys[h][2][b * BLK:b * BLK + 2 * BLK], sk_ref[0, g], valid)
                aop_ref[b * BLK:(b + 1) * BLK, g * HP:(g + 1) * HP] = jnp.dot(
                    pr.astype(BF16), vb[h][b * BLK:b * BLK + 2 * BLK], preferred_element_type=F32)
        for j in range(NQ // 2):
            ao_ref[:, j * HP:(j + 1) * HP] = (aop_ref[:, 2 * j * HP:(2 * j + 1) * HP]
                                              + pltpu.roll(aop_ref[:, (2 * j + 1) * HP:(2 * j + 2) * HP], HD, 1))
        _, aoh = _rms_fwd(ao_ref[...], 1.0 / (NQ * HD))
        an = aoh * gao_ref[...]
        mix = jnp.concatenate([cn, an], axis=1).astype(BF16)
        mix_ref[...] = mix
        xm_ref[...] = x_ref[...] + jnp.dot(mix, wo_ref[...], preferred_element_type=F32)

    prev8 = lambda col: pl.BlockSpec((8, CC), lambda i: (jnp.maximum(i * r8 - 1, 0), col))
    return pl.pallas_call(
        body, name="mixer_fwd", grid=(t // tq,),
        in_specs=[
            pl.BlockSpec((tq, NP), lambda i: (i, 0)),
            prev8(O_CG // CC), prev8(O_HC // CC),
            pl.BlockSpec((BLK, 2 * NKV * HP), lambda i: (jnp.maximum(i * nb - 1, 0), O_K // (2 * NKV * HP))),
            pl.BlockSpec((tq, D), lambda i: (i, 0)),
            _const_spec((8, CC)), _const_spec((1, HP)), _const_spec((1, HP)),
            pl.BlockSpec(memory_space=pltpu.SMEM),
            _const_spec((1, CC)), _const_spec((1, NQ * HD)), _const_spec((MIXW, D)),
        ],
        out_specs=[pl.BlockSpec((tq, D), lambda i: (i, 0)), pl.BlockSpec((tq, MIXW), lambda i: (i, 0)),
                   pl.BlockSpec((tq, NQ * HD), lambda i: (i, 0))],
        out_shape=[jax.ShapeDtypeStruct((t, D), F32), jax.ShapeDtypeStruct((t, MIXW), BF16),
                   jax.ShapeDtypeStruct((t, NQ * HD), F32)],
        scratch_shapes=[pltpu.VMEM((tq, NQ * HP), F32)],
        compiler_params=_cparams(("parallel",)),
    )(proj, proj, proj, proj, x, cw, gq, gk, sinks, gco, gao, wo)


def _ffn_weight_specs():
    return [pl.BlockSpec((N_CHIPS, FFB, D), lambda i, j=j: (0, j, 0), pipeline_mode=pl.Buffered(1))
            for j in range(3)]


def _ffn_fwd(xm, g2, gf, tm, tgt=None):
    t = xm.shape[0]
    last = tgt is not None

    def body(x_ref, g_ref, wg_ref, wu_ref, wd_ref, *rest):
        t_ref, rest = (rest[0], rest[1:]) if last else (None, rest)
        l_ref, rest = (rest[0], rest[1:]) if last else (None, rest)
        xo_ref, a_ref, b_ref, h2_ref = rest
        xv = x_ref[...]
        _, xh = _rms_fwd(xv, 1.0 / D)
        h2 = (xh * g_ref[...]).astype(BF16)
        h2_ref[...] = h2
        acc = xv
        for k in range(N_CHIPS):
            a = _nt(h2, wg_ref[k])
            b = _nt(h2, wu_ref[k])
            a_ref[k] = a.astype(BF16)
            b_ref[k] = b.astype(BF16)
            hm = (a * jax.nn.sigmoid(a) * b).astype(BF16)
            acc = acc + jnp.dot(hm, wd_ref[k], preferred_element_type=F32)
        if last:
            @pl.when(pl.program_id(0) == 0)
            def _():
                l_ref[...] = jnp.zeros_like(l_ref)

            e = acc - t_ref[...]
            xo_ref[...] = e * (1.0 / D)
            l_ref[...] += jnp.sum(jnp.sum(e * e, axis=-1, keepdims=True), axis=0, keepdims=True) * (0.5 / D)
        else:
            xo_ref[...] = acc

    row = lambda w: pl.BlockSpec((tm, w), lambda i: (i, 0))
    blk = pl.BlockSpec((N_CHIPS, tm, FFB), lambda i: (0, i, 0))
    bsd = jax.ShapeDtypeStruct((N_CHIPS, t, FFB), BF16)
    return pl.pallas_call(
        body, name="ffn_fwd_loss" if last else "ffn_fwd", grid=(t // tm,),
        in_specs=[row(D), _const_spec((1, D))] + _ffn_weight_specs() + ([row(D)] if last else []),
        out_specs=([pl.BlockSpec((8, 128), lambda i: (0, 0))] if last else []) + [row(D), blk, blk, row(D)],
        out_shape=([jax.ShapeDtypeStruct((8, 128), F32)] if last else [])
        + [jax.ShapeDtypeStruct((t, D), F32), bsd, bsd, jax.ShapeDtypeStruct((t, D), BF16)],
        compiler_params=_cparams(("arbitrary" if last else "parallel",)),
    )(*((xm, g2, gf, gf, gf) + ((tgt,) if last else ())))


def _ffn_bwd(dy, xm, g2, a, b, gf, tm):
    t = dy.shape[0]

    def body(dy_ref, x_ref, g_ref, a_ref, b_ref, wg_ref, wu_ref, wd_ref, dx_ref, da_ref, db_ref, hm_ref, dg_ref):
        @pl.when(pl.program_id(0) == 0)
        def _():
            dg_ref[...] = jnp.zeros_like(dg_ref)

        dyv = dy_ref[...]
        dyb = dyv.astype(BF16)
        dh2 = jnp.zeros_like(dyv)
        for k in range(N_CHIPS):
            dhm = _nt(dyb, wd_ref[k])
            av = a_ref[k].astype(F32)
            bv = b_ref[k].astype(F32)
            sig = jax.nn.sigmoid(av)
            sil = av * sig
            hm_ref[k] = (sil * bv).astype(BF16)
            da = (dhm * bv * (sig * (1.0 + av * (1.0 - sig)))).astype(BF16)
            db = (dhm * sil).astype(BF16)
            da_ref[k] = da
            db_ref[k] = db
            dh2 = (dh2 + jnp.dot(da, wg_ref[k], preferred_element_type=F32)
                   + jnp.dot(db, wu_ref[k], preferred_element_type=F32))
        r, xh = _rms_fwd(x_ref[...], 1.0 / D)
        dg_ref[...] += jnp.sum(dh2 * xh, axis=0, keepdims=True)
        dx_ref[...] = dyv + _rms_bwd(dh2, g_ref[...], xh, r, 1.0 / D)

    row = lambda w: pl.BlockSpec((tm, w), lambda i: (i, 0))
    blk = pl.BlockSpec((N_CHIPS, tm, FFB), lambda i: (0, i, 0))
    bsd = jax.ShapeDtypeStruct((N_CHIPS, t, FFB), BF16)
    return pl.pallas_call(
        body, name="ffn_bwd", grid=(t // tm,),
        in_specs=[row(D), row(D), _const_spec((1, D)), blk, blk] + _ffn_weight_specs(),
        out_specs=[row(D), blk, blk, blk, pl.BlockSpec((1, D), lambda i: (0, 0))],
        out_shape=[jax.ShapeDtypeStruct((t, D), F32), bsd, bsd, bsd, jax.ShapeDtypeStruct((1, D), F32)],
        compiler_params=_cparams(("arbitrary",)),
    )(dy, xm, g2, a, b, gf, gf, gf)


def _wgrad_blocks(a, b, tt, name):
    _, t, rows = a.shape
    cols = b.shape[1]
    nsteps = t // tt

    def body(a_ref, b_ref, o_ref, acc_ref):
        s = pl.program_id(0)

        @pl.when(s == 0)
        def _():
            acc_ref[...] = jnp.zeros_like(acc_ref)

        bv = b_ref[...].astype(BF16)
        for k in range(N_CHIPS):
            acc_ref[k] += _tn(a_ref[k], bv)

        @pl.when(s == nsteps - 1)
        def _():
            o_ref[...] = acc_ref[...].astype(BF16)

    return pl.pallas_call(
        body, name=name, grid=(nsteps,),
        in_specs=[pl.BlockSpec((N_CHIPS, tt, rows), lambda s: (0, s, 0)), pl.BlockSpec((tt, cols), lambda s: (s, 0))],
        out_specs=pl.BlockSpec((N_CHIPS, rows, cols), lambda s: (0, 0, 0)),
        out_shape=jax.ShapeDtypeStruct((N_CHIPS, rows, cols), BF16),
        scratch_shapes=[pltpu.VMEM((N_CHIPS, rows, cols), F32)],
        compiler_params=_cparams(("arbitrary",)),
    )(a, b)


def _head_rows(first, n_heads):
    return [(first + g * HD, first + g * HP, HD) for g in range(n_heads)]


def _wgrad_in(dpm, dkv, h, tt):
    t = h.shape[0]
    nsteps = t // tt
    kvw = dkv.shape[1]
    pieces = [(0, 0, O_Q)] + _head_rows(O_Q, NQ + 2 * NKV)

    def body(m_ref, kv_ref, h_ref, o_ref, acc_ref):
        s = pl.program_id(0)

        @pl.when(s == 0)
        def _():
            acc_ref[...] = jnp.zeros_like(acc_ref)

        hv = h_ref[...]
        acc_ref[:NMAIN, :] += _tn(m_ref[...], hv)
        acc_ref[NMAIN:, :] += _tn(kv_ref[...], hv)

        @pl.when(s == nsteps - 1)
        def _():
            for dst, src, size in pieces:
                o_ref[dst:dst + size, :] = acc_ref[src:src + size, :].astype(BF16)

    return pl.pallas_call(
        body, name="wgrad_in", grid=(nsteps,),
        in_specs=[pl.BlockSpec((tt, NMAIN), lambda s: (s, 0)), pl.BlockSpec((tt, kvw), lambda s: (s, 0)),
                  pl.BlockSpec((tt, D), lambda s: (s, 0))],
        out_specs=pl.BlockSpec((W_IN_ROWS, D), lambda s: (0, 0)),
        out_shape=jax.ShapeDtypeStruct((W_IN_ROWS, D), BF16),
        scratch_shapes=[pltpu.VMEM((NMAIN + kvw, D), F32)],
        compiler_params=_cparams(("arbitrary",)),
    )(dpm, dkv, h)


def _mixer_bwd(dxm, proj, ao, mix, cw, gq, gk, sinks, gco, gao, wo, tq):
    t = proj.shape[0]
    nb = tq // BLK
    r8 = tq // 8
    nt = t // tq
    te = tq + 8
    kvw = 2 * NKV * HP

    def body(dx_ref, dxn_ref, p_ref, cgp_ref, hcp_ref, bgn_ref, cgn_ref, hcn_ref, kvp_ref, ao_ref, mix_ref, cw_ref,
             gq_ref, gk_ref, sk_ref, gco_ref, gao_ref, wo_ref,
             dpm_ref, dkvm_ref, dkvh_ref, dcw_ref, dgq_ref, dgk_ref, dsk_ref, dgco_ref, dgao_ref, dwo_ref,
             acc_ref, dwo_acc):
        i = pl.program_id(0)

        @pl.when(i == 0)
        def _():
            for r in (dcw_ref, dgq_ref, dgk_ref, dsk_ref, dgco_ref, dgao_ref, dwo_acc):
                r[...] = jnp.zeros_like(r)

        acc_ref[...] = jnp.zeros_like(acc_ref)
        live_rows = jnp.where(i < nt - 1, te, tq)
        dxb = dx_ref[...].astype(BF16)
        dwo_acc[...] += _tn(mix_ref[...], dxb)
        dxe = jnp.concatenate([dxb, dxn_ref[...].astype(BF16)], axis=0)
        dcn = _nt(dxe, wo_ref[0:CC, :])
        bg = jnp.concatenate([p_ref[:, O_BG:O_BG + CC], bgn_ref[...]], axis=0)
        cg = jnp.concatenate([p_ref[:, O_CG:O_CG + CC], cgn_ref[...]], axis=0)
        hc = jnp.concatenate([p_ref[:, O_HC:O_HC + CC], hcn_ref[...]], axis=0)
        u = cg * hc
        up = jnp.where(i > 0, cgp_ref[...] * hcp_ref[...], 0.0)
        u1, u2 = _conv_taps(jnp.concatenate([up, u], axis=0), te)
        w0, w1, w2 = cw_ref[0:1, :], cw_ref[1:2, :], cw_ref[2:3, :]
        y = w0 * u2 + w1 * u1 + w2 * u
        co = bg * y
        rc, coh = _rms_fwd(co, 1.0 / CC)
        dco = _rms_bwd(dcn, gco_ref[...], coh, rc, 1.0 / CC)
        row_io = lax.broadcasted_iota(jnp.int32, (te, 1), 0)
        own = row_io < tq
        dgco_ref[...] += jnp.sum(jnp.where(own, dcn * coh, 0.0), axis=0, keepdims=True)
        dyc = jnp.where(row_io < live_rows, dco * bg, 0.0)
        dyo = jnp.where(own, dyc, 0.0)
        dcw_ref[0:1, :] += jnp.sum(dyo * u2, axis=0, keepdims=True)
        dcw_ref[1:2, :] += jnp.sum(dyo * u1, axis=0, keepdims=True)
        dcw_ref[2:3, :] += jnp.sum(dyo * u, axis=0, keepdims=True)
        dy1 = pltpu.roll(dyc, te - 1, 0)[0:tq]
        dy2 = pltpu.roll(dyc, te - 2, 0)[0:tq]
        du = w2 * dyc[0:tq] + w1 * dy1 + w0 * dy2
        dpm_ref[:, O_BG:O_BG + CC] = (dco[0:tq] * y[0:tq]).astype(BF16)
        dpm_ref[:, O_CG:O_CG + CC] = (du * hc[0:tq]).astype(BF16)
        dpm_ref[:, O_HC:O_HC + CC] = (du * cg[0:tq]).astype(BF16)
        kraw = jnp.concatenate([kvp_ref[:, 0:NKV * HP], p_ref[:, O_K:O_K + NKV * HP]], axis=0)
        vraw = jnp.concatenate([kvp_ref[:, NKV * HP:], p_ref[:, O_V:O_V + NKV * HP]], axis=0)
        gqv, gkv = gq_ref[...], gk_ref[...]
        keys = _norm_keys(kraw, gkv)
        vb = [vraw[:, h * HP:(h + 1) * HP].astype(BF16) for h in range(NKV)]
        base_valid, c_io = _band_mask()
        lane = lax.broadcasted_iota(jnp.int32, (1, HP), 1)
        dgq, dgk, dsk = (jnp.zeros((1, HP), F32) for _ in range(3))
        dgao = jnp.zeros((1, NQ * HD), F32)
        for b in range(nb):
            lo = jnp.where(i * nb + b == 0, BLK, 0)
            valid = base_valid & (c_io >= lo)
            band = slice(b * BLK, b * BLK + 2 * BLK)
            blk = slice(b * BLK, (b + 1) * BLK)
            ra, aoh = _rms_fwd(ao_ref[blk, :], 1.0 / (NQ * HD))
            danb = _nt(dxb[blk], wo_ref[CC:MIXW, :])
            dgao = dgao + jnp.sum(danb * aoh, axis=0, keepdims=True)
            dao = _rms_bwd(danb, gao_ref[...], aoh, ra, 1.0 / (NQ * HD))
            dos = [dao[:, g // 2 * HP:(g // 2 + 1) * HP] for g in range(NQ)]
            dos = [(d if g % 2 == 0 else pltpu.roll(d, HD, 1)).astype(BF16) for g, d in enumerate(dos)]
            fwd = []
            for g in range(NQ):
                rq, qh = _rms_fwd(p_ref[blk, O_Q + g * HP:O_Q + (g + 1) * HP], 1.0 / HD)
                qs = (qh * (gqv * SCALE)).astype(BF16)
                fwd.append((rq, qh, qs) + _attn_probs(qs, keys[g // GRP][2][band], sk_ref[0, g], valid))
            dqs = []
            for h in range(NKV):
                khat, rk, kn = [a[band] for a in keys[h]]
                dss, prbs, qns, dobs = [], [], [], []
                for g in range(h * GRP, (h + 1) * GRP):
                    rq, qh, qs, pr, ps = fwd[g]
                    dob = dos[g]
                    dp = _nt(dob, vb[h][band])
                    delta = jnp.sum(pr * dp, axis=-1, keepdims=True)
                    dsb = (pr * (dp - delta)).astype(BF16)
                    dsk = dsk + jnp.where(lane == g, -jnp.sum(ps * delta, axis=0, keepdims=True), 0.0)
                    dqn = jnp.dot(dsb, kn, preferred_element_type=F32) * SCALE
                    dgq = dgq + jnp.sum(dqn * qh, axis=0, keepdims=True)
                    dqs.append(_rms_bwd(dqn, gqv, qh, rq, 1.0 / HD).astype(BF16))
                    dss.append(dsb)
                    prbs.append(pr.astype(BF16))
                    qns.append(qs)
                    dobs.append(dob)
                dkn = _tn(jnp.concatenate(dss, axis=0), jnp.concatenate(qns, axis=0))
                dv = _tn(jnp.concatenate(prbs, axis=0), jnp.concatenate(dobs, axis=0))
                dgk = dgk + jnp.sum(dkn * khat, axis=0, keepdims=True)
                acc_ref[band, h * HP:(h + 1) * HP] += _rms_bwd(dkn, gkv, khat, rk, 1.0 / HD)
                acc_ref[band, (NKV + h) * HP:(NKV + h + 1) * HP] += dv
            dpm_ref[blk, O_Q:O_K] = jnp.concatenate(dqs, axis=1)
        dgq_ref[...] += dgq
        dgk_ref[...] += dgk
        dsk_ref[...] += dsk
        dgao_ref[...] += dgao
        dkvh_ref[...] = acc_ref[0:BLK, :]
        dkvm_ref[...] = acc_ref[BLK:, :]

        @pl.when(i == nt - 1)
        def _():
            dwo_ref[...] = dwo_acc[...].astype(BF16)

    prev8 = lambda col: pl.BlockSpec((8, CC), lambda i: (jnp.maximum(i * r8 - 1, 0), col))
    next8 = lambda col: pl.BlockSpec((8, CC), lambda i: (jnp.minimum((i + 1) * r8, t // 8 - 1), col))
    small = lambda n: pl.BlockSpec((1, n), lambda i: (0, 0))
    return pl.pallas_call(
        body, name="mixer_bwd", grid=(nt,),
        in_specs=[
            pl.BlockSpec((tq, D), lambda i: (i, 0)),
            pl.BlockSpec((8, D), lambda i: (jnp.minimum((i + 1) * r8, t // 8 - 1), 0)),
            pl.BlockSpec((tq, NP), lambda i: (i, 0)),
            prev8(O_CG // CC), prev8(O_HC // CC),
            next8(O_BG // CC), next8(O_CG // CC), next8(O_HC // CC),
            pl.BlockSpec((BLK, kvw), lambda i: (jnp.maximum(i * nb - 1, 0), O_K // kvw)),
            pl.BlockSpec((tq, NQ * HD), lambda i: (i, 0)),
            pl.BlockSpec((tq, MIXW), lambda i: (i, 0)),
            _const_spec((8, CC)), _const_spec((1, HP)), _const_spec((1, HP)),
            pl.BlockSpec(memory_space=pltpu.SMEM),
            _const_spec((1, CC)), _const_spec((1, NQ * HD)), _const_spec((MIXW, D)),
        ],
        out_specs=[
            pl.BlockSpec((tq, NMAIN), lambda i: (i, 0)),
            pl.BlockSpec((tq, kvw), lambda i: (i, 0)),
            pl.BlockSpec((BLK, kvw), lambda i: (i, 0)),
            pl.BlockSpec((8, CC), lambda i: (0, 0)), small(HP), small(HP), small(HP), small(CC), small(NQ * HD),
            pl.BlockSpec((MIXW, D), lambda i: (0, 0)),
        ],
        out_shape=[
            jax.ShapeDtypeStruct((t, NMAIN), BF16), jax.ShapeDtypeStruct((t, kvw), F32),
            jax.ShapeDtypeStruct((nt * BLK, kvw), F32),
            jax.ShapeDtypeStruct((8, CC), F32), jax.ShapeDtypeStruct((1, HP), F32), jax.ShapeDtypeStruct((1, HP), F32),
            jax.ShapeDtypeStruct((1, HP), F32), jax.ShapeDtypeStruct((1, CC), F32),
            jax.ShapeDtypeStruct((1, NQ * HD), F32), jax.ShapeDtypeStruct((MIXW, D), BF16),
        ],
        scratch_shapes=[pltpu.VMEM((tq + BLK, kvw), F32), pltpu.VMEM((MIXW, D), F32)],
        compiler_params=_cparams(("arbitrary",)),
    )(dxm, dxm, proj, proj, proj, proj, proj, proj, proj, ao, mix, cw, gq, gk, sinks, gco, gao, wo)


def _inproj_bwd(dpm, dkvm, dkvh, wpt, x, g1, dxm, tm):
    t = x.shape[0]
    kvw = 2 * NKV * HP
    nt = t // tm

    def body(dp_ref, dk_ref, dh_ref, w_ref, x_ref, g_ref, dxm_ref, dx_ref, dg_ref, dkv_ref):
        i = pl.program_id(0)

        @pl.when(i == 0)
        def _():
            dg_ref[...] = jnp.zeros_like(dg_ref)

        halo = jnp.where(i < nt - 1, dh_ref[...], 0.0)
        dkv_ref[0:tm - BLK, :] = dk_ref[0:tm - BLK, :].astype(BF16)
        dkv_ref[tm - BLK:tm, :] = (dk_ref[tm - BLK:tm, :] + halo).astype(BF16)
        dh = (jnp.dot(dp_ref[...], w_ref[0:NMAIN, :], preferred_element_type=F32)
              + jnp.dot(dkv_ref[...], w_ref[NMAIN:NP, :], preferred_element_type=F32))
        r, xh = _rms_fwd(x_ref[...], 1.0 / D)
        dg_ref[...] += jnp.sum(dh * xh, axis=0, keepdims=True)
        dx_ref[...] = dxm_ref[...] + _rms_bwd(dh, g_ref[...], xh, r, 1.0 / D)

    row = lambda w: pl.BlockSpec((tm, w), lambda i: (i, 0))
    return pl.pallas_call(
        body, name="inproj_bwd", grid=(nt,),
        in_specs=[row(NMAIN), row(kvw), pl.BlockSpec((BLK, kvw), lambda i: (jnp.minimum(i + 1, nt - 1), 0)),
                  _const_spec((NP, D)), row(D), _const_spec((1, D)), row(D)],
        out_specs=[row(D), pl.BlockSpec((1, D), lambda i: (0, 0)), row(kvw)],
        out_shape=[jax.ShapeDtypeStruct((t, D), F32), jax.ShapeDtypeStruct((1, D), F32),
                   jax.ShapeDtypeStruct((t, kvw), BF16)],
        compiler_params=_cparams(("arbitrary",)),
    )(dpm, dkvm, dkvh, wpt, x, g1, dxm)


def _presum_halves(gs, theirs, core, chip):
    n = len(gs)

    def body(c_ref, chip_ref, *refs):
        for g_ref, t_ref, o_ref, keep_ref in zip(refs[:n], refs[n:2 * n], refs[2 * n:3 * n], refs[3 * n:]):
            val = (g_ref[...].astype(F32) + t_ref[...].astype(F32)).astype(BF16)
            o_ref[...] = val

            @pl.when(pl.program_id(0) == chip_ref[0])
            def _():
                keep_ref[...] = val

    half = lambda ta: pl.BlockSpec((None,) + ta.shape[1:], lambda k, c_ref, chip_ref: (k, 0, 0))
    own = lambda ta: pl.BlockSpec((None,) + ta.shape[1:], lambda k, c_ref, chip_ref: (k, c_ref[0], 0))
    kept = lambda ta: pl.BlockSpec(ta.shape[1:], lambda k, c_ref, chip_ref: (0, 0))
    res = pl.pallas_call(
        body, name="presum",
        grid_spec=pltpu.PrefetchScalarGridSpec(
            num_scalar_prefetch=2, grid=(N_CHIPS,),
            in_specs=[own(ta) for ta in theirs] + [half(ta) for ta in theirs],
            out_specs=[half(ta) for ta in theirs] + [kept(ta) for ta in theirs]),
        out_shape=[jax.ShapeDtypeStruct(ta.shape, BF16) for ta in theirs]
        + [jax.ShapeDtypeStruct(ta.shape[1:], BF16) for ta in theirs],
        compiler_params=_cparams(("arbitrary",)),
    )(core, chip, *gs, *theirs)
    return res[:n], res[n:]


def _sum_chips(got, kept, chip):
    n = len(got)
    steps = 2

    def body(chip_ref, *refs):
        for c_ref, own_ref, o_ref in zip(refs[:n], refs[n:2 * n], refs[2 * n:]):
            acc = None
            for j in range(N_CHIPS):
                term = jnp.where(chip_ref[0] == j, own_ref[...], c_ref[j]).astype(F32)
                acc = term if acc is None else acc + term
            o_ref[...] = acc

    tile = lambda c: (c.shape[1] // steps, c.shape[2])
    return pl.pallas_call(
        body, name="chipsum",
        grid_spec=pltpu.PrefetchScalarGridSpec(
            num_scalar_prefetch=1, grid=(steps,),
            in_specs=[pl.BlockSpec((N_CHIPS,) + tile(c), lambda i, chip_ref: (0, i, 0)) for c in got]
            + [pl.BlockSpec(tile(c), lambda i, chip_ref: (i, 0)) for c in got],
            out_specs=[pl.BlockSpec(tile(c), lambda i, chip_ref: (i, 0)) for c in got]),
        out_shape=[jax.ShapeDtypeStruct(c.shape[1:], F32) for c in got],
        compiler_params=_cparams(("parallel",)),
    )(chip, *got, *kept)


def _adamw_refs(w_ref, g_ref, m_ref, v_ref, d_ref, mo_ref, vo_ref):
    c1 = 1.0 - ADAM_B1 ** ADAM_STEP
    c2 = 1.0 - ADAM_B2 ** ADAM_STEP
    gv = g_ref[...]
    mn = ADAM_B1 * m_ref[...] + (1.0 - ADAM_B1) * gv
    vn = ADAM_B2 * v_ref[...] + (1.0 - ADAM_B2) * (gv * gv)
    mo_ref[...] = mn
    vo_ref[...] = vn
    d_ref[...] = -ADAM_LR * ((mn / c1) / (jnp.sqrt(vn / c2) + ADAM_EPS) + ADAM_WD * w_ref[...])


def _adamw_small(ws, gs, ms, vs):
    n = len(ws)

    def body(*refs):
        for k in range(n):
            _adamw_refs(*refs[k::n])

    res = pl.pallas_call(
        body, name="adamw_small", out_shape=[jax.ShapeDtypeStruct(w.shape, F32) for w in ws] * 3,
        compiler_params=_cparams(),
    )(*ws, *gs, *ms, *vs)
    return res[:n], res[n:2 * n], res[2 * n:]


def _adamw_halves(w, m, v, mine, theirs, core, name):
    depth = len(mine)
    half, cols = mine[0].shape
    assert w.shape == (2 * depth * half, cols)

    def body(core_ref, *refs):
        halves, (w_ref, m_ref, v_ref, g_ref, d_ref, mo_ref, vo_ref) = refs[:2 * depth], refs[2 * depth:]
        for l in range(depth):
            for h in range(2):
                @pl.when(pl.program_id(0) == 2 * l + h)
                def _(l=l, h=h):
                    g_ref[...] = jnp.where(core_ref[0] == h, halves[l][...], halves[depth + l][...])
        _adamw_refs(w_ref, g_ref, m_ref, v_ref, d_ref, mo_ref, vo_ref)

    spec = pl.BlockSpec((half, cols), lambda i, core_ref: (i, 0))
    sds = jax.ShapeDtypeStruct(w.shape, F32)
    return pl.pallas_call(
        body, name=name,
        grid_spec=pltpu.PrefetchScalarGridSpec(
            num_scalar_prefetch=1, grid=(2 * depth,),
            in_specs=[_const_spec((half, cols))] * (2 * depth) + [spec] * 3, out_specs=[spec] * 4),
        out_shape=[sds] * 4,
        compiler_params=_cparams(("arbitrary",)),
    )(core, *mine, *theirs, w, m, v)


def _place():
    x, y, c = lax.axis_index("x"), lax.axis_index("y"), lax.axis_index("c")
    chips = [(1 - x, y), (x, 1 - y), (1 - x, 1 - y)]
    return x, y, c, chips


ANY = pl.BlockSpec(memory_space=pl.ANY)
DMA_ROWS = 64


def _pieces(shape):
    rows = shape[-2]
    step = DMA_ROWS if rows % DMA_ROWS == 0 else rows
    lead = [()]
    for n in shape[:-2]:
        lead = [i + (k,) for i in lead for k in range(n)]
    return [i + (pl.ds(r0, step),) for i in lead for r0 in range(0, rows, step)]


def _start_pieces(make, src, dst):
    for idx in _pieces(src.shape):
        make(src.at[idx], dst.at[idx]).start()


def _gather_body(srcs, outs, sems, layer, start):
    nw = len(srcs)
    ssem, rsem, fssem, frsem = sems
    x, y, c, chips = _place()
    kme = 2 * x + y

    def plane(j, w, to):
        return lambda s, d: pltpu.make_async_remote_copy(
            src_ref=s, dst_ref=d, send_sem=ssem.at[j, w], recv_sem=rsem.at[j, w], device_id=to,
            device_id_type=MESH)

    def passed(j, w):
        return lambda s, d: pltpu.make_async_remote_copy(
            src_ref=s, dst_ref=d, send_sem=fssem.at[j, w], recv_sem=frsem.at[j, w],
            device_id=(x, y, 1 - c), device_id_type=MESH)

    @pl.when(c == layer)
    def _():
        for j, (px, py) in enumerate(chips):
            for w in range(nw):
                start(plane(j, w, (px, py, c)), srcs[w], outs[w].at[kme])
        for j, (px, py) in enumerate(chips):
            for w in range(nw):
                got = outs[w].at[2 * px + py]
                plane(j, w, (px, py, c))(got, got).wait_recv()
                start(passed(j, w), got, got)
        for j, (px, py) in enumerate(chips):
            for w in range(nw):
                got = outs[w].at[2 * px + py]
                plane(j, w, (px, py, c))(got, got).wait_send()
                passed(j, w)(got, got).wait_send()

    @pl.when(c != layer)
    def _():
        for j, (px, py) in enumerate(chips):
            for w in range(nw):
                got = outs[w].at[2 * px + py]
                passed(j, w)(got, got).wait_recv()


def _handshake(peers):
    barrier = pltpu.get_barrier_semaphore()
    for peer in peers:
        pl.semaphore_signal(barrier, inc=1, device_id=peer, device_id_type=MESH)
    pl.semaphore_wait(barrier, len(peers))


def _handshake_all():
    x, y, c, _ = _place()
    _handshake([(x ^ (r >> 2), y ^ ((r >> 1) & 1), c ^ (r & 1)) for r in range(1, 8)])


def _gather_layer_async(blocks, layer, name, collective_id):
    hbm = pltpu.MemorySpace.HBM
    srcs = [jax.new_ref(b, memory_space=hbm) for b in blocks]
    outs = [jax.empty_ref(jax.ShapeDtypeStruct((N_CHIPS,) + b.shape, b.dtype), memory_space=hbm) for b in blocks]

    @pl.kernel(mesh=plsc.ScalarSubcoreMesh(axis_name="seq", num_cores=1), name=name,
               scratch_types=[pltpu.SemaphoreType.DMA((3, len(blocks)))] * 4,
               compiler_params=pltpu.CompilerParams(collective_id=collective_id))
    def launch(*sems):
        _handshake_all()
        _gather_body(srcs, outs, sems, layer, lambda make, s, d: make(s, d).start())

    launch()
    return [o[...] for o in outs]


def _swap_siblings(arrs, halves, name, collective_id=None):
    nw = len(arrs)
    out_sds = [jax.ShapeDtypeStruct((a.shape[0], a.shape[1] // 2, a.shape[2]) if halves else a.shape, a.dtype)
               for a in arrs]

    def exchange(srcs, outs, ssem, rsem, start):
        x, y, c, _ = _place()

        def give(w):
            return lambda s, d: pltpu.make_async_remote_copy(
                src_ref=s, dst_ref=d, send_sem=ssem.at[w], recv_sem=rsem.at[w], device_id=(x, y, 1 - c),
                device_id_type=MESH)

        for w in range(nw):
            hr = outs[w].shape[1]
            start(give(w), srcs[w].at[:, pl.ds((1 - c) * hr, hr)] if halves else srcs[w], outs[w])
        for w in range(nw):
            give(w)(outs[w], outs[w]).wait()

    if collective_id is None:
        def body(*refs):
            exchange(refs[:nw], refs[nw:2 * nw], *refs[2 * nw:], _start_pieces)

        return pl.pallas_call(
            body, name=name, in_specs=[ANY] * nw, out_specs=[ANY] * nw, out_shape=out_sds,
            scratch_shapes=[pltpu.SemaphoreType.DMA((nw,))] * 2,
            compiler_params=_cparams(has_side_effects=True),
        )(*arrs)

    hbm = pltpu.MemorySpace.HBM
    srcs = [jax.new_ref(a, memory_space=hbm) for a in arrs]
    outs = [jax.empty_ref(sds, memory_space=hbm) for sds in out_sds]

    @pl.kernel(mesh=plsc.ScalarSubcoreMesh(axis_name="seq", num_cores=1), name=name,
               scratch_types=[pltpu.SemaphoreType.DMA((nw,))] * 2,
               compiler_params=pltpu.CompilerParams(collective_id=collective_id))
    def launch(ssem, rsem):
        x, y, c, _ = _place()
        _handshake([(x, y, 1 - c)])
        exchange(srcs, outs, ssem, rsem, lambda make, s, d: make(s, d).start())

    launch()
    return [o[...] for o in outs]


def _scatter_body(srcs, outs, sems, start):
    nw = len(srcs)
    ssem, rsem = sems
    x, y, c, chips = _place()
    kme = 2 * x + y

    def give(j, w, to):
        return lambda s, d: pltpu.make_async_remote_copy(
            src_ref=s, dst_ref=d, send_sem=ssem.at[j, w], recv_sem=rsem.at[j, w], device_id=to,
            device_id_type=MESH)

    for j, (px, py) in enumerate(chips):
        for w in range(nw):
            start(give(j, w, (px, py, c)), srcs[w].at[2 * px + py], outs[w].at[kme])
    for j, (px, py) in enumerate(chips):
        for w in range(nw):
            got = outs[w].at[2 * px + py]
            give(j, w, (px, py, c))(got, got).wait_recv()
    for j, (px, py) in enumerate(chips):
        for w in range(nw):
            sent = srcs[w].at[2 * px + py]
            give(j, w, (px, py, c))(sent, sent).wait_send()


def _scatter_chips_async(ps, name, collective_id):
    hbm = pltpu.MemorySpace.HBM
    srcs = [jax.new_ref(p, memory_space=hbm) for p in ps]
    outs = [jax.empty_ref(jax.ShapeDtypeStruct(p.shape, p.dtype), memory_space=hbm) for p in ps]

    @pl.kernel(mesh=plsc.ScalarSubcoreMesh(axis_name="seq", num_cores=1), name=name,
               scratch_types=[pltpu.SemaphoreType.DMA((3, len(ps)))] * 2,
               compiler_params=pltpu.CompilerParams(collective_id=collective_id))
    def launch(*sems):
        _handshake_all()
        _scatter_body(srcs, outs, sems, lambda make, s, d: make(s, d).start())

    launch()
    return [o[...] for o in outs]


def _allreduce_small(v):
    rows = v.shape[0]

    def body(v_ref, o_ref, buf, ssem, rsem):
        x, y, c, _ = _place()
        me = 4 * x + 2 * y + c
        buf[me] = v_ref[...]
        sends = []
        for r in range(1, 8):
            peer = (x ^ (r >> 2), y ^ ((r >> 1) & 1), c ^ (r & 1))
            cp = pltpu.make_async_remote_copy(
                src_ref=v_ref, dst_ref=buf.at[me], send_sem=ssem.at[r - 1], recv_sem=rsem.at[r - 1],
                device_id=peer, device_id_type=MESH)
            cp.start()
            sends.append(cp)
        for r in range(1, 8):
            src = me ^ r
            pltpu.make_async_remote_copy(
                src_ref=v_ref, dst_ref=buf.at[src], send_sem=ssem.at[r - 1], recv_sem=rsem.at[r - 1],
                device_id=(x, y, c), device_id_type=MESH).wait_recv()
        for cp in sends:
            cp.wait_send()
        acc = buf[0]
        for d in range(1, 8):
            acc = acc + buf[d]
        o_ref[...] = acc

    vm = pl.BlockSpec(memory_space=pltpu.VMEM)
    return pl.pallas_call(
        body, name="allreduce_small", in_specs=[vm], out_specs=vm,
        out_shape=jax.ShapeDtypeStruct(v.shape, F32),
        scratch_shapes=[pltpu.VMEM((8, rows, 128), F32), pltpu.SemaphoreType.DMA((7,)),
                        pltpu.SemaphoreType.DMA((7,))],
        compiler_params=_cparams(has_side_effects=True),
    )(v)


def _t(w):
    return jnp.swapaxes(w, -1, -2)


def _count(shape):
    n = 1
    for s in shape:
        n *= s
    return n


def _pack_rows(arrs):
    flat = [jnp.pad(a.reshape(-1), (0, (-_count(a.shape)) % 128)) for a in arrs]
    v = jnp.concatenate(flat)
    rows = -(-v.shape[0] // (8 * 128)) * 8
    return jnp.pad(v, (0, rows * 128 - v.shape[0])).reshape(rows, 128)


def kernel(x, norm1_g, w_in, conv_w, q_norm_g, k_norm_g, sinks, conv_out_g, attn_out_g, w_o, norm2_g, w_gate, w_up, w_down, loss_target, m_norm1_g, m_w_in, m_conv_w, m_q_norm_g, m_k_norm_g, m_sinks, m_conv_out_g, m_attn_out_g, m_w_o, m_norm2_g, m_w_gate, m_w_up, m_w_down, v_norm1_g, v_w_in, v_conv_w, v_q_norm_g, v_k_norm_g, v_sinks, v_conv_out_g, v_attn_out_g, v_w_o, v_norm2_g, v_w_gate, v_w_up, v_w_down):
    depth = w_in.shape[0]
    t = x.shape[1]
    xs = x.reshape(t, D)
    tgt = loss_target.reshape(t, D)
    xi, yi = lax.axis_index("x"), lax.axis_index("y")
    kme = 2 * xi + yi
    tm = min(512, t)
    tq = min(512, t)
    tf = min(256, t)
    tw = min(1024, t)

    cwp = jnp.pad(conv_w.reshape(depth * 3, CC // N_CHIPS), ((0, 8 - depth * 3), (0, 0)))
    own_f = [jnp.concatenate([_t(w_gate[l]), _t(w_up[l]), w_down[l]], axis=0).astype(BF16) for l in range(depth)]
    own_o = [w_o[l].astype(BF16) for l in range(depth)]
    own_i = [_t(w_in[l]).astype(BF16) for l in range(depth)]
    mine = lambda got, own: lax.dynamic_update_index_in_dim(got, own, kme, 0)
    (got_i0,) = _gather_layer_async([own_i[0]], 0, "gather_in0_seq", collective_id=14)
    got_ocw = _gather_layer_async([own_o[0], cwp], 0, "gather_o0_seq", collective_id=15)
    got_i0, own_f, own_o, own_i = lax.optimization_barrier((got_i0, own_f, own_o, own_i))
    gf0_in = lax.optimization_barrier((own_f[0], got_i0))[0]
    (got_f0,) = _gather_layer_async([gf0_in], 0, "gather_ffn0_seq", collective_id=6)

    chip = kme.reshape(1).astype(jnp.int32)

    def layer_params(l, got_o, cw_full):
        return dict(
            wo=mine(got_o, own_o[l]).reshape(MIXW, D),
            cw=jnp.pad(cw_full[l], ((0, 5), (0, 0))),
            g1=norm1_g[l].reshape(1, D), g2=norm2_g[l].reshape(1, D),
            gq=jnp.pad(q_norm_g[l], (0, HP - HD)).reshape(1, HP), gk=jnp.pad(k_norm_g[l], (0, HP - HD)).reshape(1, HP),
            sk=sinks[l].reshape(1, NQ), gco=conv_out_g[l].reshape(1, CC),
            gao=attn_out_g[l].reshape(1, NQ * HD))

    saved, layers = [], []
    cur = xs
    for l in range(depth):
        x_in = cur
        if l == 0:
            got_i = got_i0
        else:
            got_f1, got_o, got_i = lax.optimization_barrier((got_l1, cur))[0]
        proj, h, wpt = _inproj_fwd(cur, norm1_g[l].reshape(1, D), got_i, own_i[l], chip, tm)
        if l == 0:
            got_o, got_cw = lax.optimization_barrier((got_ocw, proj))[0]
            cw_full = mine(got_cw, cwp).transpose(1, 0, 2).reshape(8, CC)[:depth * 3].reshape(depth, 3, CC)
        p = layer_params(l, got_o, cw_full)
        p["wpt"] = wpt
        xm, mix, ao = _mixer_fwd(proj, cur, p["cw"], p["gq"], p["gk"], p["sk"], p["gco"], p["gao"], p["wo"], tq)
        if l == 0:
            got_f0 = lax.optimization_barrier((got_f0, xm))[0]
            l1_in = lax.optimization_barrier(([own_f[1], own_o[1], own_i[1]], got_f0))[0]
            got_l1 = _gather_layer_async(l1_in, 1, "gather_layer1_seq", collective_id=1)
        p["gf"] = mine(got_f0 if l == 0 else got_f1, own_f[l])
        layers.append(p)
        if l < depth - 1:
            cur, a, b, h2 = _ffn_fwd(xm, p["g2"], p["gf"], tm)
        else:
            lpart, dy, a, b, h2 = _ffn_fwd(xm, p["g2"], p["gf"], tm, tgt)
        saved.append(dict(x=x_in, proj=proj, h=h, xm=xm, mix=mix, ao=ao, a=a, b=b, h2=h2))

    ci = lax.axis_index("c")
    core = ci.reshape(1).astype(jnp.int32)
    rbig = [dict() for _ in range(depth)]
    gsmall = [None] * depth

    def after_(vals, after):
        return vals if after is None else lax.optimization_barrier((vals, after))[0]

    def reduce_1(gs, tag, ids):
        return gs, _swap_siblings(gs, True, f"swap_halves_{tag}_seq", ids[0]), tag, ids

    def reduce_2(state, after):
        gs, theirs, tag, ids = state
        ps, kept = _presum_halves(gs, after_(theirs, after), core, chip)
        return kept, _scatter_chips_async(ps, f"scatter_{tag}_seq", ids[1]), tag, ids

    def reduce_3(state, after):
        kept, got, tag, ids = state
        r_mine = _sum_chips(after_(got, after), kept, chip)
        return r_mine, _swap_siblings(r_mine, False, f"swap_reduced_{tag}" + ("_seq" if ids[2] else ""), ids[2])

    def reduce_4(state, after):
        r_mine, r_theirs = state
        return list(zip(r_mine, after_(r_theirs, after)))

    ids = {"ffn1": (7, 4, 8), "in1": (9, 5, 10), "ffn0": (11, 2, 12), "in0": (13, 3, None)}
    in_2 = scattering = None
    handed = {}
    for l in reversed(range(depth)):
        p, s = layers[l], saved[l]
        dxm, da, db, hm, dg2 = _ffn_bwd(dy, s["xm"], p["g2"], s["a"], s["b"], p["gf"], tf)
        if in_2 is not None:
            in_2 = reduce_2(in_2, dxm)
        g_wg = _wgrad_blocks(da, s["h2"], tw, "wgrad_gate")
        g_wu = _wgrad_blocks(db, s["h2"], tw, "wgrad_up")
        g_wd = _wgrad_blocks(hm, dy, tw, "wgrad_down")
        if in_2 is not None:
            handed[f"in{l + 1}"] = reduce_3(in_2, g_wd)
        ffn_1 = reduce_1([g_wg, g_wu, g_wd], f"ffn{l}", ids[f"ffn{l}"])
        dpm, dkvm, dkvh, dcw, dgq, dgk, dsk, dgco, dgao, g_o = _mixer_bwd(
            after_(dxm, scattering), s["proj"], s["ao"], s["mix"], p["cw"], p["gq"], p["gk"], p["sk"], p["gco"],
            p["gao"], p["wo"], tq)
        ffn_2 = reduce_2(ffn_1, dpm)
        scattering = ffn_2[0]
        dx, dg1, dkv = _inproj_bwd(dpm, dkvm, dkvh, p["wpt"], s["x"], p["g1"], dxm, tq)
        g_in = _wgrad_in(dpm, dkv, s["h"], tw)
        dy = dx
        gsmall[l] = dict(g1=dg1, cw=dcw[:3], gq=dgq[0, :HD], gk=dgk[0, :HD], sk=dsk[0, :NQ], gco=dgco,
                         gao=dgao, g2=dg2)
        above = handed.get(f"ffn{l + 1}")
        handed[f"ffn{l}"] = reduce_3(ffn_2, g_in if above is None else (g_in, above[0]))
        in_2 = reduce_1([g_in.reshape(N_CHIPS, -1, D), g_o.reshape(N_CHIPS, -1, D)], f"in{l}", ids[f"in{l}"])
    grad_x = dy.reshape(x.shape)

    small_shapes = dict(g1=(D,), cw=(3, CC), gq=(HD,), gk=(HD,), sk=(NQ,), gco=(CC,), gao=(NQ * HD,), g2=(D,))
    red = _allreduce_small(_pack_rows([gsmall[l][n] for l in range(depth) for n in small_shapes]
                                      + [lpart[0:1, 0:1]])).reshape(-1)
    red_small, offs = {n: [] for n in small_shapes}, 0
    for l in range(depth):
        for n, shp in small_shapes.items():
            cnt = _count(shp)
            red_small[n].append(red[offs:offs + cnt].reshape(shp))
            offs += -(-cnt // 128) * 128
    loss = red[offs]
    g_small = {n: jnp.stack(v) for n, v in red_small.items()}
    g_cw = lax.dynamic_slice_in_dim(g_small["cw"], kme * (CC // N_CHIPS), CC // N_CHIPS, axis=2)

    weights = [norm1_g, w_in, conv_w, q_norm_g, k_norm_g, sinks, conv_out_g, attn_out_g, w_o, norm2_g, w_gate,
               w_up, w_down]
    moms = [m_norm1_g, m_w_in, m_conv_w, m_q_norm_g, m_k_norm_g, m_sinks, m_conv_out_g, m_attn_out_g, m_w_o,
            m_norm2_g, m_w_gate, m_w_up, m_w_down]
    vars_ = [v_norm1_g, v_w_in, v_conv_w, v_q_norm_g, v_k_norm_g, v_sinks, v_conv_out_g, v_attn_out_g, v_w_o,
             v_norm2_g, v_w_gate, v_w_up, v_w_down]
    n_w = len(weights)
    big_idx = dict(zip(("in", "o", "g", "u", "d"), (1, 8, 10, 11, 12)))
    small_idx = [n for n in range(n_w) if n not in big_idx.values()]
    grads, deltas, new_m, new_v = [None] * n_w, [None] * n_w, [None] * n_w, [None] * n_w
    for n, g in zip(small_idx, (g_small["g1"], g_cw, g_small["gq"], g_small["gk"], g_small["sk"], g_small["gco"],
                                g_small["gao"], g_small["g2"])):
        grads[n] = g

    def update_big(name):
        n = big_idx[name]
        mine, theirs = zip(*[rbig[l][name] for l in range(depth)])
        shape = (depth, 2 * mine[0].shape[0], D)
        flip = shape != weights[n].shape
        rows2d = lambda a3: (_t(a3) if flip else a3).reshape(-1, D)
        res = _adamw_halves(rows2d(weights[n]), rows2d(moms[n]), rows2d(vars_[n]), mine, theirs, core,
                            f"adamw_{n}")
        res = [r.reshape(shape) for r in res]
        grads[n], deltas[n], new_m[n], new_v[n] = [_t(r) for r in res] if flip else res

    for l in range(depth):
        rbig[l]["g"], rbig[l]["u"], rbig[l]["d"] = reduce_4(handed[f"ffn{l}"], red)
    rbig[1]["in"], rbig[1]["o"] = reduce_4(handed["in1"], red)
    update_big("g")
    in_2 = reduce_2(in_2, new_v[big_idx["g"]])
    update_big("u")
    update_big("d")
    rbig[0]["in"], rbig[0]["o"] = reduce_4(reduce_3(in_2, new_v[big_idx["d"]]), None)
    for name in ("in", "o"):
        update_big(name)
    res = _adamw_small(*[[arrs[n] for n in small_idx] for arrs in (weights, grads, moms, vars_)])
    for k, n in enumerate(small_idx):
        deltas[n], new_m[n], new_v[n] = res[0][k], res[1][k], res[2][k]
    return (loss, grad_x, *grads, *deltas, *new_m, *new_v)
```

```python
import jax
import jax.numpy as jnp
from jax import lax
from jax.experimental import pallas as pl
from jax.experimental.pallas import tpu as pltpu
from jax.experimental.pallas import tpu_sc as plsc

F32 = jnp.float32
BF16 = jnp.bfloat16

D = 1024
CC = 512
NQ = 8
NKV = 2
HD = 64
HP = 128
GRP = NQ // NKV
FF = 2816
FFB = FF // 4
BLK = 128
EPS = 1e-6
NEG = -1e30
SCALE = HD ** -0.5
O_BG, O_CG, O_HC, O_Q = 0, CC, 2 * CC, 3 * CC
O_K = O_Q + NQ * HP
O_V = O_K + NKV * HP
NP = O_V + NKV * HP
NMAIN = O_K
MIXW = CC + NQ * HD
N_CHIPS = 4
VMEM_LIMIT = 56 * 1024 * 1024
MESH = pl.DeviceIdType.MESH

ADAM_LR, ADAM_B1, ADAM_B2, ADAM_EPS, ADAM_WD, ADAM_STEP = 0.001, 0.9, 0.999, 1e-08, 0.01, 10


def _cparams(sem=None, **kw):
    if sem is not None:
        kw["dimension_semantics"] = sem
    return pltpu.CompilerParams(vmem_limit_bytes=VMEM_LIMIT, **kw)


def _const_spec(shape):
    nd = len(shape)
    return pl.BlockSpec(shape, lambda *_: (0,) * nd, pipeline_mode=pl.Buffered(1))


def _nt(a, b):
    return lax.dot_general(a, b, (((1,), (1,)), ((), ())), preferred_element_type=F32)


def _tn(a, b):
    return lax.dot_general(a, b, (((0,), (0,)), ((), ())), preferred_element_type=F32)


def _rms_fwd(x, inv_n):
    r = lax.rsqrt(jnp.sum(x * x, axis=-1, keepdims=True) * inv_n + EPS)
    return r, x * r


def _rms_bwd(dy, g, xh, r, inv_n):
    dxh = dy * g
    return r * (dxh - xh * (jnp.sum(dxh * xh, axis=-1, keepdims=True) * inv_n))


W_IN_ROWS = 3 * CC + (NQ + 2 * NKV) * HD
W_IN_BLOCK = W_IN_ROWS // N_CHIPS


def _padded_row(row):
    return row + max(row - O_Q, 0) // HD * (HP - HD)


def _w_in_pieces(k):
    first = k * W_IN_BLOCK
    plain = min(max(O_Q - first, 0), W_IN_BLOCK)
    pieces = [(0, first, plain)] if plain else []
    return pieces + [(r, _padded_row(first + r), HD) for r in range(plain, W_IN_BLOCK, HD)]


def _inproj_fwd(x, g1, gi, own_i, chip, tm):
    t = x.shape[0]

    def body(chip_ref, x_ref, g_ref, gi_ref, own_ref, p_ref, h_ref, w_ref, sem):
        @pl.when(pl.program_id(0) == 0)
        def _():
            for k in range(N_CHIPS):
                for src, dst, rows in _w_in_pieces(k):
                    @pl.when(chip_ref[0] == k)
                    def _():
                        pltpu.make_async_copy(own_ref.at[pl.ds(src, rows)], w_ref.at[pl.ds(dst, rows)], sem).start()

                    @pl.when(chip_ref[0] != k)
                    def _():
                        pltpu.make_async_copy(gi_ref.at[k, pl.ds(src, rows)], w_ref.at[pl.ds(dst, rows)], sem).start()
            for slot in range(NQ + 2 * NKV):
                w_ref[O_Q + slot * HP + HD:O_Q + (slot + 1) * HP, :] = jnp.zeros((HP - HD, D), BF16)
            landed = w_ref.at[pl.ds(0, W_IN_ROWS)]
            pltpu.make_async_copy(landed, landed, sem).wait()

        _, xh = _rms_fwd(x_ref[...], 1.0 / D)
        h = (xh * g_ref[...]).astype(BF16)
        h_ref[...] = h
        p_ref[...] = _nt(h, w_ref[...])

    const = lambda shape: pl.BlockSpec(shape, lambda i, c: (0,) * len(shape))
    return pl.pallas_call(
        body, name="inproj_fwd",
        grid_spec=pltpu.PrefetchScalarGridSpec(
            num_scalar_prefetch=1, grid=(t // tm,),
            in_specs=[pl.BlockSpec((tm, D), lambda i, c: (i, 0)), const((1, D)), ANY, ANY],
            out_specs=[pl.BlockSpec((tm, NP), lambda i, c: (i, 0)), pl.BlockSpec((tm, D), lambda i, c: (i, 0)),
                       const((NP, D))],
            scratch_shapes=[pltpu.SemaphoreType.DMA(())]),
        out_shape=[jax.ShapeDtypeStruct((t, NP), F32), jax.ShapeDtypeStruct((t, D), BF16),
                   jax.ShapeDtypeStruct((NP, D), BF16)],
        compiler_params=_cparams(("arbitrary",)),
    )(chip, x, g1, gi, own_i)


def _band_mask():
    r_io = lax.broadcasted_iota(jnp.int32, (BLK, 2 * BLK), 0)
    c_io = lax.broadcasted_iota(jnp.int32, (BLK, 2 * BLK), 1)
    return (c_io > r_io) & (c_io <= r_io + BLK), c_io


def _conv_taps(uf, n):
    u1 = pltpu.roll(uf, 1, 0)[8:8 + n]
    u2 = pltpu.roll(uf, 2, 0)[8:8 + n]
    return u1, u2


def _attn_probs(qs, kband, sink, valid):
    s = jnp.where(valid, _nt(qs, kband), NEG)
    m = jnp.maximum(jnp.max(s, axis=-1, keepdims=True), sink)
    p = jnp.exp(s - m)
    es = jnp.exp(sink - m)
    inv = 1.0 / (jnp.sum(p, axis=-1, keepdims=True) + es)
    return p * inv, es * inv


def _norm_keys(kraw, gk):
    out = []
    for h in range(NKV):
        kh = kraw[:, h * HP:(h + 1) * HP]
        rk, khat = _rms_fwd(kh, 1.0 / HD)
        out.append((khat, rk, (khat * gk).astype(BF16)))
    return out


def _mixer_fwd(proj, x, cw, gq, gk, sinks, gco, gao, wo, tq):
    t = proj.shape[0]
    nb = tq // BLK
    r8 = tq // 8

    def body(p_ref, cgp_ref, hcp_ref, kvp_ref, x_ref, cw_ref, gq_ref, gk_ref, sk_ref, gco_ref, gao_ref,
             wo_ref, xm_ref, mix_ref, ao_ref, aop_ref):
        i = pl.program_id(0)
        cg = p_ref[:, O_CG:O_CG + CC]
        hc = p_ref[:, O_HC:O_HC + CC]
        u = cg * hc
        up = jnp.where(i > 0, cgp_ref[...] * hcp_ref[...], 0.0)
        u1, u2 = _conv_taps(jnp.concatenate([up, u], axis=0), tq)
        y = cw_ref[0:1, :] * u2 + cw_ref[1:2, :] * u1 + cw_ref[2:3, :] * u
        co = p_ref[:, O_BG:O_BG + CC] * y
        _, coh = _rms_fwd(co, 1.0 / CC)
        cn = coh * gco_ref[...]
        kraw = jnp.concatenate([kvp_ref[:, 0:NKV * HP], p_ref[:, O_K:O_K + NKV * HP]], axis=0)
        vraw = jnp.concatenate([kvp_ref[:, NKV * HP:], p_ref[:, O_V:O_V + NKV * HP]], axis=0)
        keys = _norm_keys(kraw, gk_ref[...])
        vb = [vraw[:, h * HP:(h + 1) * HP].astype(BF16) for h in range(NKV)]
        base_valid, c_io = _band_mask()
        gqs = gq_ref[...] * SCALE
        for b in range(nb):
            lo = jnp.where(i * nb + b == 0, BLK, 0)
            valid = base_valid & (c_io >= lo)
            for g in range(NQ):
                h = g // GRP
                qg = p_ref[b * BLK:(b + 1) * BLK, O_Q + g * HP:O_Q + (g + 1) * HP]
                _, qh = _rms_fwd(qg, 1.0 / HD)
                qs = (qh * gqs).astype(BF16)
                pr, _ = _attn_probs(qs, keys[h][2][b * BLK:b * BLK + 2 * BLK], sk_ref[0, g], valid)
                aop_ref[b * BLK:(b + 1) * BLK, g * HP:(g + 1) * HP] = jnp.dot(
                    pr.astype(BF16), vb[h][b * BLK:b * BLK + 2 * BLK], preferred_element_type=F32)
        for j in range(NQ // 2):
            ao_ref[:, j * HP:(j + 1) * HP] = (aop_ref[:, 2 * j * HP:(2 * j + 1) * HP]
                                              + pltpu.roll(aop_ref[:, (2 * j + 1) * HP:(2 * j + 2) * HP], HD, 1))
        _, aoh = _rms_fwd(ao_ref[...], 1.0 / (NQ * HD))
        an = aoh * gao_ref[...]
        mix = jnp.concatenate([cn, an], axis=1).astype(BF16)
        mix_ref[...] = mix
        xm_ref[...] = x_ref[...] + jnp.dot(mix, wo_ref[...], preferred_element_type=F32)

    prev8 = lambda col: pl.BlockSpec((8, CC), lambda i: (jnp.maximum(i * r8 - 1, 0), col))
    return pl.pallas_call(
        body, name="mixer_fwd", grid=(t // tq,),
        in_specs=[
            pl.BlockSpec((tq, NP), lambda i: (i, 0)),
            prev8(O_CG // CC), prev8(O_HC // CC),
            pl.BlockSpec((BLK, 2 * NKV * HP), lambda i: (jnp.maximum(i * nb - 1, 0), O_K // (2 * NKV * HP))),
            pl.BlockSpec((tq, D), lambda i: (i, 0)),
            _const_spec((8, CC)), _const_spec((1, HP)), _const_spec((1, HP)),
            pl.BlockSpec(memory_space=pltpu.SMEM),
            _const_spec((1, CC)), _const_spec((1, NQ * HD)), _const_spec((MIXW, D)),
        ],
        out_specs=[pl.BlockSpec((tq, D), lambda i: (i, 0)), pl.BlockSpec((tq, MIXW), lambda i: (i, 0)),
                   pl.BlockSpec((tq, NQ * HD), lambda i: (i, 0))],
        out_shape=[jax.ShapeDtypeStruct((t, D), F32), jax.ShapeDtypeStruct((t, MIXW), BF16),
                   jax.ShapeDtypeStruct((t, NQ * HD), F32)],
        scratch_shapes=[pltpu.VMEM((tq, NQ * HP), F32)],
        compiler_params=_cparams(("parallel",)),
    )(proj, proj, proj, proj, x, cw, gq, gk, sinks, gco, gao, wo)


def _ffn_weight_specs():
    return [pl.BlockSpec((N_CHIPS, FFB, D), lambda i, j=j: (0, j, 0), pipeline_mode=pl.Buffered(1))
            for j in range(3)]


def _ffn_fwd(xm, g2, gf, tm, tgt=None):
    t = xm.shape[0]
    last = tgt is not None

    def body(x_ref, g_ref, wg_ref, wu_ref, wd_ref, *rest):
        t_ref, rest = (rest[0], rest[1:]) if last else (None, rest)
        l_ref, rest = (rest[0], rest[1:]) if last else (None, rest)
        xo_ref, a_ref, b_ref, h2_ref = rest
        xv = x_ref[...]
        _, xh = _rms_fwd(xv, 1.0 / D)
        h2 = (xh * g_ref[...]).astype(BF16)
        h2_ref[...] = h2
        acc = xv
        for k in range(N_CHIPS):
            a = _nt(h2, wg_ref[k])
            b = _nt(h2, wu_ref[k])
            a_ref[k] = a.astype(BF16)
            b_ref[k] = b.astype(BF16)
            hm = (a * jax.nn.sigmoid(a) * b).astype(BF16)
            acc = acc + jnp.dot(hm, wd_ref[k], preferred_element_type=F32)
        if last:
            @pl.when(pl.program_id(0) == 0)
            def _():
                l_ref[...] = jnp.zeros_like(l_ref)

            e = acc - t_ref[...]
            xo_ref[...] = e * (1.0 / D)
            l_ref[...] += jnp.sum(jnp.sum(e * e, axis=-1, keepdims=True), axis=0, keepdims=True) * (0.5 / D)
        else:
            xo_ref[...] = acc

    row = lambda w: pl.BlockSpec((tm, w), lambda i: (i, 0))
    blk = pl.BlockSpec((N_CHIPS, tm, FFB), lambda i: (0, i, 0))
    bsd = jax.ShapeDtypeStruct((N_CHIPS, t, FFB), BF16)
    return pl.pallas_call(
        body, name="ffn_fwd_loss" if last else "ffn_fwd", grid=(t // tm,),
        in_specs=[row(D), _const_spec((1, D))] + _ffn_weight_specs() + ([row(D)] if last else []),
        out_specs=([pl.BlockSpec((8, 128), lambda i: (0, 0))] if last else []) + [row(D), blk, blk, row(D)],
        out_shape=([jax.ShapeDtypeStruct((8, 128), F32)] if last else [])
        + [jax.ShapeDtypeStruct((t, D), F32), bsd, bsd, jax.ShapeDtypeStruct((t, D), BF16)],
        compiler_params=_cparams(("arbitrary" if last else "parallel",)),
    )(*((xm, g2, gf, gf, gf) + ((tgt,) if last else ())))


def _ffn_bwd(dy, xm, g2, a, b, gf, tm):
    t = dy.shape[0]

    def body(dy_ref, x_ref, g_ref, a_ref, b_ref, wg_ref, wu_ref, wd_ref, dx_ref, da_ref, db_ref, hm_ref, dg_ref):
        @pl.when(pl.program_id(0) == 0)
        def _():
            dg_ref[...] = jnp.zeros_like(dg_ref)

        dyv = dy_ref[...]
        dyb = dyv.astype(BF16)
        dh2 = jnp.zeros_like(dyv)
        for k in range(N_CHIPS):
            dhm = _nt(dyb, wd_ref[k])
            av = a_ref[k].astype(F32)
            bv = b_ref[k].astype(F32)
            sig = jax.nn.sigmoid(av)
            sil = av * sig
            hm_ref[k] = (sil * bv).astype(BF16)
            da = (dhm * bv * (sig * (1.0 + av * (1.0 - sig)))).astype(BF16)
            db = (dhm * sil).astype(BF16)
            da_ref[k] = da
            db_ref[k] = db
            dh2 = (dh2 + jnp.dot(da, wg_ref[k], preferred_element_type=F32)
                   + jnp.dot(db, wu_ref[k], preferred_element_type=F32))
        r, xh = _rms_fwd(x_ref[...], 1.0 / D)
        dg_ref[...] += jnp.sum(dh2 * xh, axis=0, keepdims=True)
        dx_ref[...] = dyv + _rms_bwd(dh2, g_ref[...], xh, r, 1.0 / D)

    row = lambda w: pl.BlockSpec((tm, w), lambda i: (i, 0))
    blk = pl.BlockSpec((N_CHIPS, tm, FFB), lambda i: (0, i, 0))
    bsd = jax.ShapeDtypeStruct((N_CHIPS, t, FFB), BF16)
    return pl.pallas_call(
        body, name="ffn_bwd", grid=(t // tm,),
        in_specs=[row(D), row(D), _const_spec((1, D)), blk, blk] + _ffn_weight_specs(),
        out_specs=[row(D), blk, blk, blk, pl.BlockSpec((1, D), lambda i: (0, 0))],
        out_shape=[jax.ShapeDtypeStruct((t, D), F32), bsd, bsd, bsd, jax.ShapeDtypeStruct((1, D), F32)],
        compiler_params=_cparams(("arbitrary",)),
    )(dy, xm, g2, a, b, gf, gf, gf)


def _wgrad_blocks(a, b, tt, name):
    _, t, rows = a.shape
    cols = b.shape[1]
    nsteps = t // tt

    def body(a_ref, b_ref, o_ref, acc_ref):
        s = pl.program_id(0)

        @pl.when(s == 0)
        def _():
            acc_ref[...] = jnp.zeros_like(acc_ref)

        bv = b_ref[...].astype(BF16)
        for k in range(N_CHIPS):
            acc_ref[k] += _tn(a_ref[k], bv)

        @pl.when(s == nsteps - 1)
        def _():
            o_ref[...] = acc_ref[...].astype(BF16)

    return pl.pallas_call(
        body, name=name, grid=(nsteps,),
        in_specs=[pl.BlockSpec((N_CHIPS, tt, rows), lambda s: (0, s, 0)), pl.BlockSpec((tt, cols), lambda s: (s, 0))],
        out_specs=pl.BlockSpec((N_CHIPS, rows, cols), lambda s: (0, 0, 0)),
        out_shape=jax.ShapeDtypeStruct((N_CHIPS, rows, cols), BF16),
        scratch_shapes=[pltpu.VMEM((N_CHIPS, rows, cols), F32)],
        compiler_params=_cparams(("arbitrary",)),
    )(a, b)


def _wgrad_in(dpm, dkv, dwq, h, tt):
    t = h.shape[0]
    nsteps = t // tt
    kvw = dkv.shape[1]
    nq = dwq.shape[0]
    kv_pieces = [(O_Q + nq + g * HD, O_Q + g * HP, HD) for g in range(2 * NKV)]

    def body(m_ref, kv_ref, q_ref, h_ref, o_ref, acc_ref):
        s = pl.program_id(0)

        @pl.when(s == 0)
        def _():
            acc_ref[...] = jnp.zeros_like(acc_ref)

        hv = h_ref[...]
        acc_ref[:O_Q, :] += _tn(m_ref[...], hv)
        acc_ref[O_Q:, :] += _tn(kv_ref[...], hv)

        @pl.when(s == nsteps - 1)
        def _():
            o_ref[:O_Q, :] = acc_ref[:O_Q, :].astype(BF16)
            o_ref[O_Q:O_Q + nq, :] = q_ref[...]
            for dst, src, size in kv_pieces:
                o_ref[dst:dst + size, :] = acc_ref[src:src + size, :].astype(BF16)

    return pl.pallas_call(
        body, name="wgrad_in", grid=(nsteps,),
        in_specs=[pl.BlockSpec((tt, O_Q), lambda s: (s, 0)), pl.BlockSpec((tt, kvw), lambda s: (s, 0)),
                  _const_spec((nq, D)), pl.BlockSpec((tt, D), lambda s: (s, 0))],
        out_specs=pl.BlockSpec((W_IN_ROWS, D), lambda s: (0, 0)),
        out_shape=jax.ShapeDtypeStruct((W_IN_ROWS, D), BF16),
        scratch_shapes=[pltpu.VMEM((O_Q + kvw, D), F32)],
        compiler_params=_cparams(("arbitrary",)),
    )(dpm, dkv, dwq, h)


def _mixer_bwd(dxm, proj, ao, mix, h, cw, gq, gk, sinks, gco, gao, wo, tq):
    t = proj.shape[0]
    nb = tq // BLK
    assert nb % 2 == 0
    r8 = tq // 8
    nt = t // tq
    te = tq + 8
    kvw = 2 * NKV * HP

    def body(dx_ref, dxn_ref, p_ref, cgp_ref, hcp_ref, bgn_ref, cgn_ref, hcn_ref, kvp_ref, ao_ref, mix_ref, h_ref,
             cw_ref, gq_ref, gk_ref, sk_ref, gco_ref, gao_ref, wo_ref,
             dpm_ref, dkvm_ref, dkvh_ref, dcw_ref, dgq_ref, dgk_ref, dsk_ref, dgco_ref, dgao_ref, dwo_ref, dwq_ref,
             acc_ref, dwo_acc, dwq_acc):
        i = pl.program_id(0)

        @pl.when(i == 0)
        def _():
            for r in (dcw_ref, dgq_ref, dgk_ref, dsk_ref, dgco_ref, dgao_ref, dwo_acc, dwq_acc):
                r[...] = jnp.zeros_like(r)

        acc_ref[...] = jnp.zeros_like(acc_ref)
        live_rows = jnp.where(i < nt - 1, te, tq)
        dxb = dx_ref[...].astype(BF16)
        dwo_acc[...] += _tn(mix_ref[...], dxb)
        dxe = jnp.concatenate([dxb, dxn_ref[...].astype(BF16)], axis=0)
        dcn = _nt(dxe, wo_ref[0:CC, :])
        bg = jnp.concatenate([p_ref[:, O_BG:O_BG + CC], bgn_ref[...]], axis=0)
        cg = jnp.concatenate([p_ref[:, O_CG:O_CG + CC], cgn_ref[...]], axis=0)
        hc = jnp.concatenate([p_ref[:, O_HC:O_HC + CC], hcn_ref[...]], axis=0)
        u = cg * hc
        up = jnp.where(i > 0, cgp_ref[...] * hcp_ref[...], 0.0)
        u1, u2 = _conv_taps(jnp.concatenate([up, u], axis=0), te)
        w0, w1, w2 = cw_ref[0:1, :], cw_ref[1:2, :], cw_ref[2:3, :]
        y = w0 * u2 + w1 * u1 + w2 * u
        co = bg * y
        rc, coh = _rms_fwd(co, 1.0 / CC)
        dco = _rms_bwd(dcn, gco_ref[...], coh, rc, 1.0 / CC)
        row_io = lax.broadcasted_iota(jnp.int32, (te, 1), 0)
        own = row_io < tq
        dgco_ref[...] += jnp.sum(jnp.where(own, dcn * coh, 0.0), axis=0, keepdims=True)
        dyc = jnp.where(row_io < live_rows, dco * bg, 0.0)
        dyo = jnp.where(own, dyc, 0.0)
        dcw_ref[0:1, :] += jnp.sum(dyo * u2, axis=0, keepdims=True)
        dcw_ref[1:2, :] += jnp.sum(dyo * u1, axis=0, keepdims=True)
        dcw_ref[2:3, :] += jnp.sum(dyo * u, axis=0, keepdims=True)
        dy1 = pltpu.roll(dyc, te - 1, 0)[0:tq]
        dy2 = pltpu.roll(dyc, te - 2, 0)[0:tq]
        du = w2 * dyc[0:tq] + w1 * dy1 + w0 * dy2
        dpm_ref[:, O_BG:O_BG + CC] = (dco[0:tq] * y[0:tq]).astype(BF16)
        dpm_ref[:, O_CG:O_CG + CC] = (du * hc[0:tq]).astype(BF16)
        dpm_ref[:, O_HC:O_HC + CC] = (du * cg[0:tq]).astype(BF16)
        kraw = jnp.concatenate([kvp_ref[:, 0:NKV * HP], p_ref[:, O_K:O_K + NKV * HP]], axis=0)
        vraw = jnp.concatenate([kvp_ref[:, NKV * HP:], p_ref[:, O_V:O_V + NKV * HP]], axis=0)
        gqv, gkv = gq_ref[...], gk_ref[...]
        keys = _norm_keys(kraw, gkv)
        vb = [vraw[:, h * HP:(h + 1) * HP].astype(BF16) for h in range(NKV)]
        base_valid, c_io = _band_mask()
        lane = lax.broadcasted_iota(jnp.int32, (1, HP), 1)
        dgq, dgk, dsk = (jnp.zeros((1, HP), F32) for _ in range(3))
        dgao = jnp.zeros((1, NQ * HD), F32)
        for b in range(nb):
            lo = jnp.where(i * nb + b == 0, BLK, 0)
            valid = base_valid & (c_io >= lo)
            band = slice(b * BLK, b * BLK + 2 * BLK)
            blk = slice(b * BLK, (b + 1) * BLK)
            ra, aoh = _rms_fwd(ao_ref[blk, :], 1.0 / (NQ * HD))
            danb = _nt(dxb[blk], wo_ref[CC:MIXW, :])
            dgao = dgao + jnp.sum(danb * aoh, axis=0, keepdims=True)
            dao = _rms_bwd(danb, gao_ref[...], aoh, ra, 1.0 / (NQ * HD))
            dos = [dao[:, g // 2 * HP:(g // 2 + 1) * HP] for g in range(NQ)]
            dos = [(d if g % 2 == 0 else pltpu.roll(d, HD, 1)).astype(BF16) for g, d in enumerate(dos)]
            fwd = []
            for g in range(NQ):
                rq, qh = _rms_fwd(p_ref[blk, O_Q + g * HP:O_Q + (g + 1) * HP], 1.0 / HD)
                qs = (qh * (gqv * SCALE)).astype(BF16)
                fwd.append((rq, qh, qs) + _attn_probs(qs, keys[g // GRP][2][band], sk_ref[0, g], valid))
            dqs = []
            for h in range(NKV):
                khat, rk, kn = [a[band] for a in keys[h]]
                dss, prbs, qns, dobs = [], [], [], []
                for g in range(h * GRP, (h + 1) * GRP):
                    rq, qh, qs, pr, ps = fwd[g]
                    dob = dos[g]
                    dp = _nt(dob, vb[h][band])
                    delta = jnp.sum(pr * dp, axis=-1, keepdims=True)
                    dsb = (pr * (dp - delta)).astype(BF16)
                    dsk = dsk + jnp.where(lane == g, -jnp.sum(ps * delta, axis=0, keepdims=True), 0.0)
                    dqn = jnp.dot(dsb, kn, preferred_element_type=F32) * SCALE
                    dgq = dgq + jnp.sum(dqn * qh, axis=0, keepdims=True)
                    dqs.append(_rms_bwd(dqn, gqv, qh, rq, 1.0 / HD).astype(BF16))
                    dss.append(dsb)
                    prbs.append(pr.astype(BF16))
                    qns.append(qs)
                    dobs.append(dob)
                dkn = _tn(jnp.concatenate(dss, axis=0), jnp.concatenate(qns, axis=0))
                dv = _tn(jnp.concatenate(prbs, axis=0), jnp.concatenate(dobs, axis=0))
                dgk = dgk + jnp.sum(dkn * khat, axis=0, keepdims=True)
                acc_ref[band, h * HP:(h + 1) * HP] += _rms_bwd(dkn, gkv, khat, rk, 1.0 / HD)
                acc_ref[band, (NKV + h) * HP:(NKV + h + 1) * HP] += dv
            dpm_ref[blk, O_Q:O_K] = jnp.concatenate(dqs, axis=1)
            if b % 2 == 1:
                two = slice((b - 1) * BLK, (b + 1) * BLK)
                dwq_acc[...] += _tn(dpm_ref[two, O_Q:O_K], h_ref[two, :])
        dgq_ref[...] += dgq
        dgk_ref[...] += dgk
        dsk_ref[...] += dsk
        dgao_ref[...] += dgao
        dkvh_ref[...] = acc_ref[0:BLK, :]
        dkvm_ref[...] = acc_ref[BLK:, :]

        @pl.when(i == nt - 1)
        def _():
            dwo_ref[...] = dwo_acc[...].astype(BF16)
            for g in range(NQ):
                dwq_ref[g * HD:(g + 1) * HD, :] = dwq_acc[g * HP:g * HP + HD, :].astype(BF16)

    prev8 = lambda col: pl.BlockSpec((8, CC), lambda i: (jnp.maximum(i * r8 - 1, 0), col))
    next8 = lambda col: pl.BlockSpec((8, CC), lambda i: (jnp.minimum((i + 1) * r8, t // 8 - 1), col))
    small = lambda n: pl.BlockSpec((1, n), lambda i: (0, 0))
    return pl.pallas_call(
        body, name="mixer_bwd", grid=(nt,),
        in_specs=[
            pl.BlockSpec((tq, D), lambda i: (i, 0)),
            pl.BlockSpec((8, D), lambda i: (jnp.minimum((i + 1) * r8, t // 8 - 1), 0)),
            pl.BlockSpec((tq, NP), lambda i: (i, 0)),
            prev8(O_CG // CC), prev8(O_HC // CC),
            next8(O_BG // CC), next8(O_CG // CC), next8(O_HC // CC),
            pl.BlockSpec((BLK, kvw), lambda i: (jnp.maximum(i * nb - 1, 0), O_K // kvw)),
            pl.BlockSpec((tq, NQ * HD), lambda i: (i, 0)),
            pl.BlockSpec((tq, MIXW), lambda i: (i, 0)),
            pl.BlockSpec((tq, D), lambda i: (i, 0)),
            _const_spec((8, CC)), _const_spec((1, HP)), _const_spec((1, HP)),
            pl.BlockSpec(memory_space=pltpu.SMEM),
            _const_spec((1, CC)), _const_spec((1, NQ * HD)), _const_spec((MIXW, D)),
        ],
        out_specs=[
            pl.BlockSpec((tq, NMAIN), lambda i: (i, 0)),
            pl.BlockSpec((tq, kvw), lambda i: (i, 0)),
            pl.BlockSpec((BLK, kvw), lambda i: (i, 0)),
            pl.BlockSpec((8, CC), lambda i: (0, 0)), small(HP), small(HP), small(HP), small(CC), small(NQ * HD),
            pl.BlockSpec((MIXW, D), lambda i: (0, 0)), pl.BlockSpec((NQ * HD, D), lambda i: (0, 0)),
        ],
        out_shape=[
            jax.ShapeDtypeStruct((t, NMAIN), BF16), jax.ShapeDtypeStruct((t, kvw), F32),
            jax.ShapeDtypeStruct((nt * BLK, kvw), F32),
            jax.ShapeDtypeStruct((8, CC), F32), jax.ShapeDtypeStruct((1, HP), F32), jax.ShapeDtypeStruct((1, HP), F32),
            jax.ShapeDtypeStruct((1, HP), F32), jax.ShapeDtypeStruct((1, CC), F32),
            jax.ShapeDtypeStruct((1, NQ * HD), F32), jax.ShapeDtypeStruct((MIXW, D), BF16),
            jax.ShapeDtypeStruct((NQ * HD, D), BF16),
        ],
        scratch_shapes=[pltpu.VMEM((tq + BLK, kvw), F32), pltpu.VMEM((MIXW, D), F32),
                        pltpu.VMEM((NQ * HP, D), F32)],
        compiler_params=_cparams(("arbitrary",)),
    )(dxm, dxm, proj, proj, proj, proj, proj, proj, proj, ao, mix, h, cw, gq, gk, sinks, gco, gao, wo)


def _inproj_bwd(dpm, dkvm, dkvh, wpt, x, g1, dxm, tm):
    t = x.shape[0]
    kvw = 2 * NKV * HP
    nt = t // tm

    def body(dp_ref, dk_ref, dh_ref, w_ref, x_ref, g_ref, dxm_ref, dx_ref, dg_ref, dkv_ref):
        i = pl.program_id(0)

        @pl.when(i == 0)
        def _():
            dg_ref[...] = jnp.zeros_like(dg_ref)

        halo = jnp.where(i < nt - 1, dh_ref[...], 0.0)
        dkv_ref[0:tm - BLK, :] = dk_ref[0:tm - BLK, :].astype(BF16)
        dkv_ref[tm - BLK:tm, :] = (dk_ref[tm - BLK:tm, :] + halo).astype(BF16)
        dh = (jnp.dot(dp_ref[...], w_ref[0:NMAIN, :], preferred_element_type=F32)
              + jnp.dot(dkv_ref[...], w_ref[NMAIN:NP, :], preferred_element_type=F32))
        r, xh = _rms_fwd(x_ref[...], 1.0 / D)
        dg_ref[...] += jnp.sum(dh * xh, axis=0, keepdims=True)
        dx_ref[...] = dxm_ref[...] + _rms_bwd(dh, g_ref[...], xh, r, 1.0 / D)

    row = lambda w: pl.BlockSpec((tm, w), lambda i: (i, 0))
    return pl.pallas_call(
        body, name="inproj_bwd", grid=(nt,),
        in_specs=[row(NMAIN), row(kvw), pl.BlockSpec((BLK, kvw), lambda i: (jnp.minimum(i + 1, nt - 1), 0)),
                  _const_spec((NP, D)), row(D), _const_spec((1, D)), row(D)],
        out_specs=[row(D), pl.BlockSpec((1, D), lambda i: (0, 0)), row(kvw)],
        out_shape=[jax.ShapeDtypeStruct((t, D), F32), jax.ShapeDtypeStruct((1, D), F32),
                   jax.ShapeDtypeStruct((t, kvw), BF16)],
        compiler_params=_cparams(("arbitrary",)),
    )(dpm, dkvm, dkvh, wpt, x, g1, dxm)


def _presum_halves(gs, theirs, core, chip):
    n = len(gs)

    def body(c_ref, chip_ref, *refs):
        for g_ref, t_ref, o_ref, keep_ref in zip(refs[:n], refs[n:2 * n], refs[2 * n:3 * n], refs[3 * n:]):
            val = (g_ref[...].astype(F32) + t_ref[...].astype(F32)).astype(BF16)
            o_ref[...] = val

            @pl.when(pl.program_id(0) == chip_ref[0])
            def _():
                keep_ref[...] = val

    half = lambda ta: pl.BlockSpec((None,) + ta.shape[1:], lambda k, c_ref, chip_ref: (k, 0, 0))
    own = lambda ta: pl.BlockSpec((None,) + ta.shape[1:], lambda k, c_ref, chip_ref: (k, c_ref[0], 0))
    kept = lambda ta: pl.BlockSpec(ta.shape[1:], lambda k, c_ref, chip_ref: (0, 0))
    res = pl.pallas_call(
        body, name="presum",
        grid_spec=pltpu.PrefetchScalarGridSpec(
            num_scalar_prefetch=2, grid=(N_CHIPS,),
            in_specs=[own(ta) for ta in theirs] + [half(ta) for ta in theirs],
            out_specs=[half(ta) for ta in theirs] + [kept(ta) for ta in theirs]),
        out_shape=[jax.ShapeDtypeStruct(ta.shape, BF16) for ta in theirs]
        + [jax.ShapeDtypeStruct(ta.shape[1:], BF16) for ta in theirs],
        compiler_params=_cparams(("arbitrary",)),
    )(core, chip, *gs, *theirs)
    return res[:n], res[n:]


def _sum_chips(got, kept, chip):
    n = len(got)
    steps = 2

    def body(chip_ref, *refs):
        for c_ref, own_ref, o_ref in zip(refs[:n], refs[n:2 * n], refs[2 * n:]):
            acc = None
            for j in range(N_CHIPS):
                term = jnp.where(chip_ref[0] == j, own_ref[...], c_ref[j]).astype(F32)
                acc = term if acc is None else acc + term
            o_ref[...] = acc

    tile = lambda c: (c.shape[1] // steps, c.shape[2])
    return pl.pallas_call(
        body, name="chipsum",
        grid_spec=pltpu.PrefetchScalarGridSpec(
            num_scalar_prefetch=1, grid=(steps,),
            in_specs=[pl.BlockSpec((N_CHIPS,) + tile(c), lambda i, chip_ref: (0, i, 0)) for c in got]
            + [pl.BlockSpec(tile(c), lambda i, chip_ref: (i, 0)) for c in got],
            out_specs=[pl.BlockSpec(tile(c), lambda i, chip_ref: (i, 0)) for c in got]),
        out_shape=[jax.ShapeDtypeStruct(c.shape[1:], F32) for c in got],
        compiler_params=_cparams(("parallel",)),
    )(chip, *got, *kept)


def _adamw_refs(w_ref, g_ref, m_ref, v_ref, d_ref, mo_ref, vo_ref):
    c1 = 1.0 - ADAM_B1 ** ADAM_STEP
    c2 = 1.0 - ADAM_B2 ** ADAM_STEP
    gv = g_ref[...]
    mn = ADAM_B1 * m_ref[...] + (1.0 - ADAM_B1) * gv
    vn = ADAM_B2 * v_ref[...] + (1.0 - ADAM_B2) * (gv * gv)
    mo_ref[...] = mn
    vo_ref[...] = vn
    d_ref[...] = -ADAM_LR * ((mn / c1) / (jnp.sqrt(vn / c2) + ADAM_EPS) + ADAM_WD * w_ref[...])


def _adamw_small(ws, gs, ms, vs):
    n = len(ws)

    def body(*refs):
        for k in range(n):
            _adamw_refs(*refs[k::n])

    res = pl.pallas_call(
        body, name="adamw_small", out_shape=[jax.ShapeDtypeStruct(w.shape, F32) for w in ws] * 3,
        compiler_params=_cparams(),
    )(*ws, *gs, *ms, *vs)
    return res[:n], res[n:2 * n], res[2 * n:]


def _adamw_halves(w, m, v, mine, theirs, core, name):
    depth = len(mine)
    half, cols = mine[0].shape
    assert w.shape == (2 * depth * half, cols)

    def body(core_ref, *refs):
        halves, (w_ref, m_ref, v_ref, g_ref, d_ref, mo_ref, vo_ref) = refs[:2 * depth], refs[2 * depth:]
        for l in range(depth):
            for h in range(2):
                @pl.when(pl.program_id(0) == 2 * l + h)
                def _(l=l, h=h):
                    g_ref[...] = jnp.where(core_ref[0] == h, halves[l][...], halves[depth + l][...])
        _adamw_refs(w_ref, g_ref, m_ref, v_ref, d_ref, mo_ref, vo_ref)

    spec = pl.BlockSpec((half, cols), lambda i, core_ref: (i, 0))
    sds = jax.ShapeDtypeStruct(w.shape, F32)
    return pl.pallas_call(
        body, name=name,
        grid_spec=pltpu.PrefetchScalarGridSpec(
            num_scalar_prefetch=1, grid=(2 * depth,),
            in_specs=[_const_spec((half, cols))] * (2 * depth) + [spec] * 3, out_specs=[spec] * 4),
        out_shape=[sds] * 4,
        compiler_params=_cparams(("arbitrary",)),
    )(core, *mine, *theirs, w, m, v)


def _place():
    x, y, c = lax.axis_index("x"), lax.axis_index("y"), lax.axis_index("c")
    chips = [(1 - x, y), (x, 1 - y), (1 - x, 1 - y)]
    return x, y, c, chips


ANY = pl.BlockSpec(memory_space=pl.ANY)
DMA_ROWS = 64


def _pieces(shape):
    rows = shape[-2]
    step = DMA_ROWS if rows % DMA_ROWS == 0 else rows
    lead = [()]
    for n in shape[:-2]:
        lead = [i + (k,) for i in lead for k in range(n)]
    return [i + (pl.ds(r0, step),) for i in lead for r0 in range(0, rows, step)]


def _start_pieces(make, src, dst):
    for idx in _pieces(src.shape):
        make(src.at[idx], dst.at[idx]).start()


def _gather_body(srcs, outs, sems, layer, start):
    nw = len(srcs)
    ssem, rsem, fssem, frsem = sems
    x, y, c, chips = _place()
    kme = 2 * x + y

    def plane(j, w, to):
        return lambda s, d: pltpu.make_async_remote_copy(
            src_ref=s, dst_ref=d, send_sem=ssem.at[j, w], recv_sem=rsem.at[j, w], device_id=to,
            device_id_type=MESH)

    def passed(j, w):
        return lambda s, d: pltpu.make_async_remote_copy(
            src_ref=s, dst_ref=d, send_sem=fssem.at[j, w], recv_sem=frsem.at[j, w],
            device_id=(x, y, 1 - c), device_id_type=MESH)

    @pl.when(c == layer)
    def _():
        for j, (px, py) in enumerate(chips):
            for w in range(nw):
                start(plane(j, w, (px, py, c)), srcs[w], outs[w].at[kme])
        for j, (px, py) in enumerate(chips):
            for w in range(nw):
                got = outs[w].at[2 * px + py]
                plane(j, w, (px, py, c))(got, got).wait_recv()
                start(passed(j, w), got, got)
        for j, (px, py) in enumerate(chips):
            for w in range(nw):
                got = outs[w].at[2 * px + py]
                plane(j, w, (px, py, c))(got, got).wait_send()
                passed(j, w)(got, got).wait_send()

    @pl.when(c != layer)
    def _():
        for j, (px, py) in enumerate(chips):
            for w in range(nw):
                got = outs[w].at[2 * px + py]
                passed(j, w)(got, got).wait_recv()


def _handshake(peers):
    barrier = pltpu.get_barrier_semaphore()
    for peer in peers:
        pl.semaphore_signal(barrier, inc=1, device_id=peer, device_id_type=MESH)
    pl.semaphore_wait(barrier, len(peers))


def _handshake_all():
    x, y, c, _ = _place()
    _handshake([(x ^ (r >> 2), y ^ ((r >> 1) & 1), c ^ (r & 1)) for r in range(1, 8)])


def _gather_layer_async(blocks, layer, name, collective_id):
    hbm = pltpu.MemorySpace.HBM
    srcs = [jax.new_ref(b, memory_space=hbm) for b in blocks]
    outs = [jax.empty_ref(jax.ShapeDtypeStruct((N_CHIPS,) + b.shape, b.dtype), memory_space=hbm) for b in blocks]

    @pl.kernel(mesh=plsc.ScalarSubcoreMesh(axis_name="seq", num_cores=1), name=name,
               scratch_types=[pltpu.SemaphoreType.DMA((3, len(blocks)))] * 4,
               compiler_params=pltpu.CompilerParams(collective_id=collective_id))
    def launch(*sems):
        _handshake_all()
        _gather_body(srcs, outs, sems, layer, lambda make, s, d: make(s, d).start())

    launch()
    return [o[...] for o in outs]


def _swap_siblings(arrs, halves, name, collective_id=None):
    nw = len(arrs)
    out_sds = [jax.ShapeDtypeStruct((a.shape[0], a.shape[1] // 2, a.shape[2]) if halves else a.shape, a.dtype)
               for a in arrs]

    def exchange(srcs, outs, ssem, rsem, start):
        x, y, c, _ = _place()

        def give(w):
            return lambda s, d: pltpu.make_async_remote_copy(
                src_ref=s, dst_ref=d, send_sem=ssem.at[w], recv_sem=rsem.at[w], device_id=(x, y, 1 - c),
                device_id_type=MESH)

        for w in range(nw):
            hr = outs[w].shape[1]
            start(give(w), srcs[w].at[:, pl.ds((1 - c) * hr, hr)] if halves else srcs[w], outs[w])
        for w in range(nw):
            give(w)(outs[w], outs[w]).wait()

    if collective_id is None:
        def body(*refs):
            exchange(refs[:nw], refs[nw:2 * nw], *refs[2 * nw:], _start_pieces)

        return pl.pallas_call(
            body, name=name, in_specs=[ANY] * nw, out_specs=[ANY] * nw, out_shape=out_sds,
            scratch_shapes=[pltpu.SemaphoreType.DMA((nw,))] * 2,
            compiler_params=_cparams(has_side_effects=True),
        )(*arrs)

    hbm = pltpu.MemorySpace.HBM
    srcs = [jax.new_ref(a, memory_space=hbm) for a in arrs]
    outs = [jax.empty_ref(sds, memory_space=hbm) for sds in out_sds]

    @pl.kernel(mesh=plsc.ScalarSubcoreMesh(axis_name="seq", num_cores=1), name=name,
               scratch_types=[pltpu.SemaphoreType.DMA((nw,))] * 2,
               compiler_params=pltpu.CompilerParams(collective_id=collective_id))
    def launch(ssem, rsem):
        x, y, c, _ = _place()
        _handshake([(x, y, 1 - c)])
        exchange(srcs, outs, ssem, rsem, lambda make, s, d: make(s, d).start())

    launch()
    return [o[...] for o in outs]


def _scatter_body(srcs, outs, sems, start):
    nw = len(srcs)
    ssem, rsem = sems
    x, y, c, chips = _place()
    kme = 2 * x + y

    def give(j, w, to):
        return lambda s, d: pltpu.make_async_remote_copy(
            src_ref=s, dst_ref=d, send_sem=ssem.at[j, w], recv_sem=rsem.at[j, w], device_id=to,
            device_id_type=MESH)

    for j, (px, py) in enumerate(chips):
        for w in range(nw):
            start(give(j, w, (px, py, c)), srcs[w].at[2 * px + py], outs[w].at[kme])
    for j, (px, py) in enumerate(chips):
        for w in range(nw):
            got = outs[w].at[2 * px + py]
            give(j, w, (px, py, c))(got, got).wait_recv()
    for j, (px, py) in enumerate(chips):
        for w in range(nw):
            sent = srcs[w].at[2 * px + py]
            give(j, w, (px, py, c))(sent, sent).wait_send()


def _scatter_chips_async(ps, name, collective_id):
    hbm = pltpu.MemorySpace.HBM
    srcs = [jax.new_ref(p, memory_space=hbm) for p in ps]
    outs = [jax.empty_ref(jax.ShapeDtypeStruct(p.shape, p.dtype), memory_space=hbm) for p in ps]

    @pl.kernel(mesh=plsc.ScalarSubcoreMesh(axis_name="seq", num_cores=1), name=name,
               scratch_types=[pltpu.SemaphoreType.DMA((3, len(ps)))] * 2,
               compiler_params=pltpu.CompilerParams(collective_id=collective_id))
    def launch(*sems):
        _handshake_all()
        _scatter_body(srcs, outs, sems, lambda make, s, d: make(s, d).start())

    launch()
    return [o[...] for o in outs]


def _allreduce_small(v):
    rows = v.shape[0]

    def body(v_ref, o_ref, buf, ssem, rsem):
        x, y, c, _ = _place()
        me = 4 * x + 2 * y + c
        buf[me] = v_ref[...]
        sends = []
        for r in range(1, 8):
            peer = (x ^ (r >> 2), y ^ ((r >> 1) & 1), c ^ (r & 1))
            cp = pltpu.make_async_remote_copy(
                src_ref=v_ref, dst_ref=buf.at[me], send_sem=ssem.at[r - 1], recv_sem=rsem.at[r - 1],
                device_id=peer, device_id_type=MESH)
            cp.start()
            sends.append(cp)
        for r in range(1, 8):
            src = me ^ r
            pltpu.make_async_remote_copy(
                src_ref=v_ref, dst_ref=buf.at[src], send_sem=ssem.at[r - 1], recv_sem=rsem.at[r - 1],
                device_id=(x, y, c), device_id_type=MESH).wait_recv()
        for cp in sends:
            cp.wait_send()
        acc = buf[0]
        for d in range(1, 8):
            acc = acc + buf[d]
        o_ref[...] = acc

    vm = pl.BlockSpec(memory_space=pltpu.VMEM)
    return pl.pallas_call(
        body, name="allreduce_small", in_specs=[vm], out_specs=vm,
        out_shape=jax.ShapeDtypeStruct(v.shape, F32),
        scratch_shapes=[pltpu.VMEM((8, rows, 128), F32), pltpu.SemaphoreType.DMA((7,)),
                        pltpu.SemaphoreType.DMA((7,))],
        compiler_params=_cparams(has_side_effects=True),
    )(v)


def _t(w):
    return jnp.swapaxes(w, -1, -2)


def _count(shape):
    n = 1
    for s in shape:
        n *= s
    return n


def _pack_rows(arrs):
    flat = [jnp.pad(a.reshape(-1), (0, (-_count(a.shape)) % 128)) for a in arrs]
    v = jnp.concatenate(flat)
    rows = -(-v.shape[0] // (8 * 128)) * 8
    return jnp.pad(v, (0, rows * 128 - v.shape[0])).reshape(rows, 128)


def kernel(x, norm1_g, w_in, conv_w, q_norm_g, k_norm_g, sinks, conv_out_g, attn_out_g, w_o, norm2_g, w_gate, w_up, w_down, loss_target, m_norm1_g, m_w_in, m_conv_w, m_q_norm_g, m_k_norm_g, m_sinks, m_conv_out_g, m_attn_out_g, m_w_o, m_norm2_g, m_w_gate, m_w_up, m_w_down, v_norm1_g, v_w_in, v_conv_w, v_q_norm_g, v_k_norm_g, v_sinks, v_conv_out_g, v_attn_out_g, v_w_o, v_norm2_g, v_w_gate, v_w_up, v_w_down):
    depth = w_in.shape[0]
    t = x.shape[1]
    xs = x.reshape(t, D)
    tgt = loss_target.reshape(t, D)
    xi, yi = lax.axis_index("x"), lax.axis_index("y")
    kme = 2 * xi + yi
    tm = min(512, t)
    tq = min(512, t)
    tf = min(256, t)
    tw = min(1024, t)

    cwp = jnp.pad(conv_w.reshape(depth * 3, CC // N_CHIPS), ((0, 8 - depth * 3), (0, 0)))
    own_f = [jnp.concatenate([_t(w_gate[l]), _t(w_up[l]), w_down[l]], axis=0).astype(BF16) for l in range(depth)]
    own_o = [w_o[l].astype(BF16) for l in range(depth)]
    own_i = [_t(w_in[l]).astype(BF16) for l in range(depth)]
    mine = lambda got, own: lax.dynamic_update_index_in_dim(got, own, kme, 0)
    (got_i0,) = _gather_layer_async([own_i[0]], 0, "gather_in0_seq", collective_id=14)
    got_ocw = _gather_layer_async([own_o[0], cwp], 0, "gather_o0_seq", collective_id=15)
    got_i0, own_f, own_o, own_i = lax.optimization_barrier((got_i0, own_f, own_o, own_i))
    gf0_in = lax.optimization_barrier((own_f[0], got_i0))[0]
    (got_f0,) = _gather_layer_async([gf0_in], 0, "gather_ffn0_seq", collective_id=6)

    chip = kme.reshape(1).astype(jnp.int32)

    def layer_params(l, got_o, cw_full):
        return dict(
            wo=mine(got_o, own_o[l]).reshape(MIXW, D),
            cw=jnp.pad(cw_full[l], ((0, 5), (0, 0))),
            g1=norm1_g[l].reshape(1, D), g2=norm2_g[l].reshape(1, D),
            gq=jnp.pad(q_norm_g[l], (0, HP - HD)).reshape(1, HP), gk=jnp.pad(k_norm_g[l], (0, HP - HD)).reshape(1, HP),
            sk=sinks[l].reshape(1, NQ), gco=conv_out_g[l].reshape(1, CC),
            gao=attn_out_g[l].reshape(1, NQ * HD))

    saved, layers = [], []
    cur = xs
    for l in range(depth):
        x_in = cur
        if l == 0:
            got_i = got_i0
        else:
            got_f1, got_o, got_i = lax.optimization_barrier((got_l1, cur))[0]
        proj, h, wpt = _inproj_fwd(cur, norm1_g[l].reshape(1, D), got_i, own_i[l], chip, tm)
        if l == 0:
            got_o, got_cw = lax.optimization_barrier((got_ocw, proj))[0]
            cw_full = mine(got_cw, cwp).transpose(1, 0, 2).reshape(8, CC)[:depth * 3].reshape(depth, 3, CC)
        p = layer_params(l, got_o, cw_full)
        p["wpt"] = wpt
        xm, mix, ao = _mixer_fwd(proj, cur, p["cw"], p["gq"], p["gk"], p["sk"], p["gco"], p["gao"], p["wo"], tq)
        if l == 0:
            got_f0 = lax.optimization_barrier((got_f0, xm))[0]
            l1_in = lax.optimization_barrier(([own_f[1], own_o[1], own_i[1]], got_f0))[0]
            got_l1 = _gather_layer_async(l1_in, 1, "gather_layer1_seq", collective_id=1)
        p["gf"] = mine(got_f0 if l == 0 else got_f1, own_f[l])
        layers.append(p)
        if l < depth - 1:
            cur, a, b, h2 = _ffn_fwd(xm, p["g2"], p["gf"], tm)
        else:
            lpart, dy, a, b, h2 = _ffn_fwd(xm, p["g2"], p["gf"], tm, tgt)
        saved.append(dict(x=x_in, proj=proj, h=h, xm=xm, mix=mix, ao=ao, a=a, b=b, h2=h2))

    ci = lax.axis_index("c")
    core = ci.reshape(1).astype(jnp.int32)
    rbig = [dict() for _ in range(depth)]
    gsmall = [None] * depth

    def after_(vals, after):
        return vals if after is None else lax.optimization_barrier((vals, after))[0]

    def reduce_1(gs, tag, ids):
        return gs, _swap_siblings(gs, True, f"swap_halves_{tag}_seq", ids[0]), tag, ids

    def reduce_2(state, after):
        gs, theirs, tag, ids = state
        ps, kept = _presum_halves(gs, after_(theirs, after), core, chip)
        return kept, _scatter_chips_async(ps, f"scatter_{tag}_seq", ids[1]), tag, ids

    def reduce_3(state, after):
        kept, got, tag, ids = state
        r_mine = _sum_chips(after_(got, after), kept, chip)
        return r_mine, _swap_siblings(r_mine, False, f"swap_reduced_{tag}" + ("_seq" if ids[2] else ""), ids[2])

    def reduce_4(state, after):
        r_mine, r_theirs = state
        return list(zip(r_mine, after_(r_theirs, after)))

    ids = {"ffn1": (7, 4, 8), "in1": (9, 5, 10), "ffn0": (11, 2, 12), "in0": (13, 3, None)}
    in_2 = scattering = None
    handed = {}
    for l in reversed(range(depth)):
        p, s = layers[l], saved[l]
        dxm, da, db, hm, dg2 = _ffn_bwd(dy, s["xm"], p["g2"], s["a"], s["b"], p["gf"], tf)
        if in_2 is not None:
            in_2 = reduce_2(in_2, dxm)
        g_wg = _wgrad_blocks(da, s["h2"], tw, "wgrad_gate")
        g_wu = _wgrad_blocks(db, s["h2"], tw, "wgrad_up")
        g_wd = _wgrad_blocks(hm, dy, tw, "wgrad_down")
        if in_2 is not None:
            handed[f"in{l + 1}"] = reduce_3(in_2, g_wd)
        ffn_1 = reduce_1([g_wg, g_wu, g_wd], f"ffn{l}", ids[f"ffn{l}"])
        dpm, dkvm, dkvh, dcw, dgq, dgk, dsk, dgco, dgao, g_o, g_q = _mixer_bwd(
            after_(dxm, scattering), s["proj"], s["ao"], s["mix"], s["h"], p["cw"], p["gq"], p["gk"], p["sk"],
            p["gco"], p["gao"], p["wo"], tq)
        ffn_2 = reduce_2(ffn_1, dpm)
        scattering = ffn_2[0]
        dx, dg1, dkv = _inproj_bwd(dpm, dkvm, dkvh, p["wpt"], s["x"], p["g1"], dxm, tq)
        g_in = _wgrad_in(dpm, dkv, g_q, s["h"], tw)
        dy = dx
        gsmall[l] = dict(g1=dg1, cw=dcw[:3], gq=dgq[0, :HD], gk=dgk[0, :HD], sk=dsk[0, :NQ], gco=dgco,
                         gao=dgao, g2=dg2)
        above = handed.get(f"ffn{l + 1}")
        handed[f"ffn{l}"] = reduce_3(ffn_2, g_in if above is None else (g_in, above[0]))
        in_2 = reduce_1([g_in.reshape(N_CHIPS, -1, D), g_o.reshape(N_CHIPS, -1, D)], f"in{l}", ids[f"in{l}"])
    grad_x = dy.reshape(x.shape)

    small_shapes = dict(g1=(D,), cw=(3, CC), gq=(HD,), gk=(HD,), sk=(NQ,), gco=(CC,), gao=(NQ * HD,), g2=(D,))
    red = _allreduce_small(_pack_rows([gsmall[l][n] for l in range(depth) for n in small_shapes]
                                      + [lpart[0:1, 0:1]])).reshape(-1)
    red_small, offs = {n: [] for n in small_shapes}, 0
    for l in range(depth):
        for n, shp in small_shapes.items():
            cnt = _count(shp)
            red_small[n].append(red[offs:offs + cnt].reshape(shp))
            offs += -(-cnt // 128) * 128
    loss = red[offs]
    g_small = {n: jnp.stack(v) for n, v in red_small.items()}
    g_cw = lax.dynamic_slice_in_dim(g_small["cw"], kme * (CC // N_CHIPS), CC // N_CHIPS, axis=2)

    weights = [norm1_g, w_in, conv_w, q_norm_g, k_norm_g, sinks, conv_out_g, attn_out_g, w_o, norm2_g, w_gate,
               w_up, w_down]
    moms = [m_norm1_g, m_w_in, m_conv_w, m_q_norm_g, m_k_norm_g, m_sinks, m_conv_out_g, m_attn_out_g, m_w_o,
            m_norm2_g, m_w_gate, m_w_up, m_w_down]
    vars_ = [v_norm1_g, v_w_in, v_conv_w, v_q_norm_g, v_k_norm_g, v_sinks, v_conv_out_g, v_attn_out_g, v_w_o,
             v_norm2_g, v_w_gate, v_w_up, v_w_down]
    n_w = len(weights)
    big_idx = dict(zip(("in", "o", "g", "u", "d"), (1, 8, 10, 11, 12)))
    small_idx = [n for n in range(n_w) if n not in big_idx.values()]
    grads, deltas, new_m, new_v = [None] * n_w, [None] * n_w, [None] * n_w, [None] * n_w
    for n, g in zip(small_idx, (g_small["g1"], g_cw, g_small["gq"], g_small["gk"], g_small["sk"], g_small["gco"],
                                g_small["gao"], g_small["g2"])):
        grads[n] = g

    def update_big(name):
        n = big_idx[name]
        mine, theirs = zip(*[rbig[l][name] for l in range(depth)])
        shape = (depth, 2 * mine[0].shape[0], D)
        flip = shape != weights[n].shape
        rows2d = lambda a3: (_t(a3) if flip else a3).reshape(-1, D)
        res = _adamw_halves(rows2d(weights[n]), rows2d(moms[n]), rows2d(vars_[n]), mine, theirs, core,
                            f"adamw_{n}")
        res = [r.reshape(shape) for r in res]
        grads[n], deltas[n], new_m[n], new_v[n] = [_t(r) for r in res] if flip else res

    for l in range(depth):
        rbig[l]["g"], rbig[l]["u"], rbig[l]["d"] = reduce_4(handed[f"ffn{l}"], red)
    rbig[1]["in"], rbig[1]["o"] = reduce_4(handed["in1"], red)
    update_big("g")
    in_2 = reduce_2(in_2, new_v[big_idx["g"]])
    update_big("u")
    update_big("d")
    rbig[0]["in"], rbig[0]["o"] = reduce_4(reduce_3(in_2, new_v[big_idx["d"]]), None)
    for name in ("in", "o"):
        update_big(name)
    res = _adamw_small(*[[arrs[n] for n in small_idx] for arrs in (weights, grads, moms, vars_)])
    for k, n in enumerate(small_idx):
        deltas[n], new_m[n], new_v[n] = res[0][k], res[1][k], res[2][k]
    return (loss, grad_x, *grads, *deltas, *new_m, *new_v)
```
